```python
import math
import jax, jax.numpy as jnp
from jax import lax
import numpy as np

D_MODEL = 1024
BATCH = 8
SEQ = 8192
DEPTH = 1

MIX_WIDTH = D_MODEL
GLA_WIDTH = MIX_WIDTH // 2
GMLP_WIDTH = MIX_WIDTH - GLA_WIDTH
GLA_HEADS = 4
GLA_DV = GLA_WIDTH // GLA_HEADS
GLA_DK = GLA_DV // 2
GLA_KEY_WIDTH = GLA_HEADS * GLA_DK
GLA_LOWRANK = 16
GLA_TAU = 16.0
GLA_CHUNK = 64
GMLP_GROUPS = 4
GMLP_GROUP_DIM = GMLP_WIDTH // GMLP_GROUPS
GMLP_CHUNK = 128
D_FF = int(math.ceil(8 * D_MODEL / 3 / 256) * 256)
EPS = 1e-6

PROJ_SIZES = [GLA_KEY_WIDTH, GLA_KEY_WIDTH, GLA_WIDTH, GLA_WIDTH,
              GLA_LOWRANK, GLA_LOWRANK, 2 * GMLP_WIDTH]
PROJ_WIDTH = sum(PROJ_SIZES)
PROJ_SPLITS = [int(v) for v in np.cumsum(PROJ_SIZES)[:-1]]

kernel_name = "hybrid_gla_gmlp_encoder_block"


def rmsnorm(x, g):
    xf = x.astype(jnp.float32)
    y = xf * lax.rsqrt(jnp.mean(xf * xf, axis=-1, keepdims=True) + EPS)
    return (y * g.astype(jnp.float32)).astype(x.dtype)


def layernorm(x, g, b):
    xf = x.astype(jnp.float32)
    mu = jnp.mean(xf, axis=-1, keepdims=True)
    xc = xf - mu
    y = xc * lax.rsqrt(jnp.mean(xc * xc, axis=-1, keepdims=True) + EPS)
    return (y * g.astype(jnp.float32) + b.astype(jnp.float32)).astype(x.dtype)


def gla_one_direction(q, k, v, log_a):
    B, S, H, DK = q.shape
    DV = v.shape[-1]
    C = GLA_CHUNK
    N = S // C
    f32 = jnp.float32
    q = q.astype(f32).reshape(B, N, C, H, DK)
    k = k.astype(f32).reshape(B, N, C, H, DK)
    v = v.astype(f32).reshape(B, N, C, H, DV)
    b = jnp.cumsum(log_a.astype(f32).reshape(B, N, C, H, DK), axis=2)
    b_last = b[:, :, -1]
    q_dec = q * jnp.exp(b)
    k_dec = k * jnp.exp(-b)
    k_to_end = k * jnp.exp(b_last[:, :, None] - b)
    scores = jnp.einsum('bnthd,bnshd->bnhts', q_dec, k_dec)
    tril = jnp.tril(jnp.ones((C, C), dtype=bool))
    scores = jnp.where(tril, scores, 0.0)
    o_intra = jnp.einsum('bnhts,bnshv->bnthv', scores, v)
    d_state = jnp.einsum('bnshd,bnshv->bnhdv', k_to_end, v)
    chunk_decay = jnp.exp(b_last)

    def step(state, inp):
        ds, dec = inp
        return dec[..., None] * state + ds, state

    state0 = jnp.zeros((B, H, DK, DV), f32)
    _, states_before = lax.scan(step, state0,
                                (jnp.moveaxis(d_state, 1, 0), jnp.moveaxis(chunk_decay, 1, 0)))
    states_before = jnp.moveaxis(states_before, 0, 1)
    o_inter = jnp.einsum('bnthd,bnhdv->bnthv', q_dec, states_before)
    return (o_intra + o_inter).reshape(B, S, H, DV)


def gla_mixer(h_q, h_k, h_v, h_g, lr_f, lr_b, w_decay_f, b_decay_f, w_decay_b, b_decay_b, gla_norm_g):
    B, S, _ = h_q.shape
    f32 = jnp.float32
    q = h_q.reshape(B, S, GLA_HEADS, GLA_DK) * (GLA_DK ** -0.5)
    k = h_k.reshape(B, S, GLA_HEADS, GLA_DK)
    v = h_v.reshape(B, S, GLA_HEADS, GLA_DV)
    la_f = (jax.nn.log_sigmoid((lr_f @ w_decay_f + b_decay_f).astype(f32)) / GLA_TAU
            ).reshape(B, S, GLA_HEADS, GLA_DK)
    la_b = (jax.nn.log_sigmoid((lr_b @ w_decay_b + b_decay_b).astype(f32)) / GLA_TAU
            ).reshape(B, S, GLA_HEADS, GLA_DK)
    o_fwd = gla_one_direction(q, k, v, la_f)
    o_bwd = jnp.flip(gla_one_direction(jnp.flip(q, 1), jnp.flip(k, 1), jnp.flip(v, 1),
                                       jnp.flip(la_b, 1)), 1)
    o = o_fwd + o_bwd
    o = o * lax.rsqrt(jnp.mean(o * o, axis=-1, keepdims=True) + EPS)
    o = o.reshape(B, S, GLA_WIDTH) * gla_norm_g.astype(f32)
    return (o * jax.nn.silu(h_g.astype(f32))).astype(h_q.dtype)


def gmlp_mixer(h_uv, ln_g, ln_b, w_spatial, b_spatial):
    B, S, _ = h_uv.shape
    z = jax.nn.gelu(h_uv, approximate=False)
    u, v = jnp.split(z, 2, axis=-1)
    v = layernorm(v, ln_g, ln_b)
    v = v.reshape(B, S // GMLP_CHUNK, GMLP_CHUNK, GMLP_GROUPS, GMLP_GROUP_DIM)
    s = jnp.einsum('gij,bnjgc->bnigc', w_spatial, v) + b_spatial.T[None, None, :, :, None]
    return u * s.reshape(B, S, GMLP_WIDTH)


def _fwd_setup_inputs(seed: int = 0) -> dict:
    key = jax.random.key(seed)
    ks = jax.random.split(key, 20)
    L = DEPTH
    nrm = lambda k, shape, fan_in: jax.random.normal(k, shape, jnp.float32) * (fan_in ** -0.5)
    gain = lambda k, shape: 1.0 + 0.02 * jax.random.normal(k, shape, jnp.float32)
    small = lambda k, shape: 0.01 * jax.random.normal(k, shape, jnp.float32)
    return {
        "x": jax.random.normal(ks[0], (BATCH, SEQ, D_MODEL), jnp.float32),
        "norm1_g": gain(ks[1], (L, D_MODEL)),
        "w_in": nrm(ks[2], (L, D_MODEL, PROJ_WIDTH), D_MODEL),
        "w_decay_f": nrm(ks[3], (L, GLA_LOWRANK, GLA_KEY_WIDTH), GLA_LOWRANK),
        "b_decay_f": small(ks[4], (L, GLA_KEY_WIDTH)),
        "w_decay_b": nrm(ks[5], (L, GLA_LOWRANK, GLA_KEY_WIDTH), GLA_LOWRANK),
        "b_decay_b": small(ks[6], (L, GLA_KEY_WIDTH)),
        "gla_norm_g": gain(ks[7], (L, GLA_WIDTH)),
        "gmlp_ln_g": gain(ks[8], (L, GMLP_WIDTH)),
        "gmlp_ln_b": small(ks[9], (L, GMLP_WIDTH)),
        "w_spatial": nrm(ks[10], (L, GMLP_GROUPS, GMLP_CHUNK, GMLP_CHUNK), GMLP_CHUNK),
        "b_spatial": gain(ks[11], (L, GMLP_GROUPS, GMLP_CHUNK)),
        "w_out": nrm(ks[12], (L, MIX_WIDTH, D_MODEL), MIX_WIDTH),
        "norm2_g": gain(ks[13], (L, D_MODEL)),
        "w_gate": nrm(ks[14], (L, D_MODEL, D_FF), D_MODEL),
        "w_up": nrm(ks[15], (L, D_MODEL, D_FF), D_MODEL),
        "w_down": nrm(ks[16], (L, D_FF, D_MODEL), D_FF),
        "final_norm_g": gain(ks[17], (D_MODEL,)),
    }


def _fwd_reference(x, norm1_g, w_in, w_decay_f, b_decay_f, w_decay_b, b_decay_b, gla_norm_g,
              gmlp_ln_g, gmlp_ln_b, w_spatial, b_spatial, w_out, norm2_g, w_gate, w_up,
              w_down, final_norm_g):
    for l in range(DEPTH):
        h = rmsnorm(x, norm1_g[l])
        p = h @ w_in[l]
        h_q, h_k, h_v, h_g, lr_f, lr_b, h_uv = jnp.split(p, PROJ_SPLITS, axis=-1)
        y_a = gla_mixer(h_q, h_k, h_v, h_g, lr_f, lr_b, w_decay_f[l], b_decay_f[l],
                        w_decay_b[l], b_decay_b[l], gla_norm_g[l])
        y_b = gmlp_mixer(h_uv, gmlp_ln_g[l], gmlp_ln_b[l], w_spatial[l], b_spatial[l])
        x = x + jnp.concatenate([y_a, y_b.astype(y_a.dtype)], axis=-1) @ w_out[l]
        h2 = rmsnorm(x, norm2_g[l])
        x = x + (jax.nn.silu(h2 @ w_gate[l]) * (h2 @ w_up[l])) @ w_down[l]
    return rmsnorm(x, final_norm_g)


import jax as _jax
import jax.numpy as _jnp

TWIN_FORMAT = 'train_step'
FWD_PARAMS = ['x', 'norm1_g', 'w_in', 'w_decay_f', 'b_decay_f', 'w_decay_b', 'b_decay_b', 'gla_norm_g', 'gmlp_ln_g', 'gmlp_ln_b', 'w_spatial', 'b_spatial', 'w_out', 'norm2_g', 'w_gate', 'w_up', 'w_down', 'final_norm_g']
TWIN_WEIGHTS = ['norm1_g', 'w_in', 'w_decay_f', 'b_decay_f', 'w_decay_b', 'b_decay_b', 'gla_norm_g', 'gmlp_ln_g', 'gmlp_ln_b', 'w_spatial', 'b_spatial', 'w_out', 'norm2_g', 'w_gate', 'w_up', 'w_down', 'final_norm_g']
TWIN_DIFF_INPUT = 'x'
TWIN_INPUTS = ['x', 'norm1_g', 'w_in', 'w_decay_f', 'b_decay_f', 'w_decay_b', 'b_decay_b', 'gla_norm_g', 'gmlp_ln_g', 'gmlp_ln_b', 'w_spatial', 'b_spatial', 'w_out', 'norm2_g', 'w_gate', 'w_up', 'w_down', 'final_norm_g', 'loss_target', 'm_norm1_g', 'm_w_in', 'm_w_decay_f', 'm_b_decay_f', 'm_w_decay_b', 'm_b_decay_b', 'm_gla_norm_g', 'm_gmlp_ln_g', 'm_gmlp_ln_b', 'm_w_spatial', 'm_b_spatial', 'm_w_out', 'm_norm2_g', 'm_w_gate', 'm_w_up', 'm_w_down', 'm_final_norm_g', 'v_norm1_g', 'v_w_in', 'v_w_decay_f', 'v_b_decay_f', 'v_w_decay_b', 'v_b_decay_b', 'v_gla_norm_g', 'v_gmlp_ln_g', 'v_gmlp_ln_b', 'v_w_spatial', 'v_b_spatial', 'v_w_out', 'v_norm2_g', 'v_w_gate', 'v_w_up', 'v_w_down', 'v_final_norm_g']
TWIN_OUTPUTS = ['loss', 'grad_x', 'grad_norm1_g', 'grad_w_in', 'grad_w_decay_f', 'grad_b_decay_f', 'grad_w_decay_b', 'grad_b_decay_b', 'grad_gla_norm_g', 'grad_gmlp_ln_g', 'grad_gmlp_ln_b', 'grad_w_spatial', 'grad_b_spatial', 'grad_w_out', 'grad_norm2_g', 'grad_w_gate', 'grad_w_up', 'grad_w_down', 'grad_final_norm_g', 'delta_norm1_g', 'delta_w_in', 'delta_w_decay_f', 'delta_b_decay_f', 'delta_w_decay_b', 'delta_b_decay_b', 'delta_gla_norm_g', 'delta_gmlp_ln_g', 'delta_gmlp_ln_b', 'delta_w_spatial', 'delta_b_spatial', 'delta_w_out', 'delta_norm2_g', 'delta_w_gate', 'delta_w_up', 'delta_w_down', 'delta_final_norm_g', 'new_m_norm1_g', 'new_m_w_in', 'new_m_w_decay_f', 'new_m_b_decay_f', 'new_m_w_decay_b', 'new_m_b_decay_b', 'new_m_gla_norm_g', 'new_m_gmlp_ln_g', 'new_m_gmlp_ln_b', 'new_m_w_spatial', 'new_m_b_spatial', 'new_m_w_out', 'new_m_norm2_g', 'new_m_w_gate', 'new_m_w_up', 'new_m_w_down', 'new_m_final_norm_g', 'new_v_norm1_g', 'new_v_w_in', 'new_v_w_decay_f', 'new_v_b_decay_f', 'new_v_w_decay_b', 'new_v_b_decay_b', 'new_v_gla_norm_g', 'new_v_gmlp_ln_g', 'new_v_gmlp_ln_b', 'new_v_w_spatial', 'new_v_b_spatial', 'new_v_w_out', 'new_v_norm2_g', 'new_v_w_gate', 'new_v_w_up', 'new_v_w_down', 'new_v_final_norm_g']
TWIN_LEAF_KINDS = {'loss': 'loss', 'grad_x': 'grad_x', 'grad_norm1_g': 'grad_w', 'grad_w_in': 'grad_w', 'grad_w_decay_f': 'grad_w', 'grad_b_decay_f': 'grad_w', 'grad_w_decay_b': 'grad_w', 'grad_b_decay_b': 'grad_w', 'grad_gla_norm_g': 'grad_w', 'grad_gmlp_ln_g': 'grad_w', 'grad_gmlp_ln_b': 'grad_w', 'grad_w_spatial': 'grad_w', 'grad_b_spatial': 'grad_w', 'grad_w_out': 'grad_w', 'grad_norm2_g': 'grad_w', 'grad_w_gate': 'grad_w', 'grad_w_up': 'grad_w', 'grad_w_down': 'grad_w', 'grad_final_norm_g': 'grad_w', 'delta_norm1_g': 'delta_w', 'delta_w_in': 'delta_w', 'delta_w_decay_f': 'delta_w', 'delta_b_decay_f': 'delta_w', 'delta_w_decay_b': 'delta_w', 'delta_b_decay_b': 'delta_w', 'delta_gla_norm_g': 'delta_w', 'delta_gmlp_ln_g': 'delta_w', 'delta_gmlp_ln_b': 'delta_w', 'delta_w_spatial': 'delta_w', 'delta_b_spatial': 'delta_w', 'delta_w_out': 'delta_w', 'delta_norm2_g': 'delta_w', 'delta_w_gate': 'delta_w', 'delta_w_up': 'delta_w', 'delta_w_down': 'delta_w', 'delta_final_norm_g': 'delta_w', 'new_m_norm1_g': 'new_m', 'new_m_w_in': 'new_m', 'new_m_w_decay_f': 'new_m', 'new_m_b_decay_f': 'new_m', 'new_m_w_decay_b': 'new_m', 'new_m_b_decay_b': 'new_m', 'new_m_gla_norm_g': 'new_m', 'new_m_gmlp_ln_g': 'new_m', 'new_m_gmlp_ln_b': 'new_m', 'new_m_w_spatial': 'new_m', 'new_m_b_spatial': 'new_m', 'new_m_w_out': 'new_m', 'new_m_norm2_g': 'new_m', 'new_m_w_gate': 'new_m', 'new_m_w_up': 'new_m', 'new_m_w_down': 'new_m', 'new_m_final_norm_g': 'new_m', 'new_v_norm1_g': 'new_v', 'new_v_w_in': 'new_v', 'new_v_w_decay_f': 'new_v', 'new_v_b_decay_f': 'new_v', 'new_v_w_decay_b': 'new_v', 'new_v_b_decay_b': 'new_v', 'new_v_gla_norm_g': 'new_v', 'new_v_gmlp_ln_g': 'new_v', 'new_v_gmlp_ln_b': 'new_v', 'new_v_w_spatial': 'new_v', 'new_v_b_spatial': 'new_v', 'new_v_w_out': 'new_v', 'new_v_norm2_g': 'new_v', 'new_v_w_gate': 'new_v', 'new_v_w_up': 'new_v', 'new_v_w_down': 'new_v', 'new_v_final_norm_g': 'new_v'}


def _forward(args):
    return _fwd_reference(*[args[k] for k in FWD_PARAMS])


def _output_shape():
    def fwd():
        inp = _fwd_setup_inputs(0)
        return _fwd_reference(*[inp[k] for k in FWD_PARAMS])
    out = _jax.eval_shape(fwd)
    return out.shape, out.dtype

N_MICROBATCH = 1
ADAM_LR = 0.001
ADAM_B1 = 0.9
ADAM_B2 = 0.999
ADAM_EPS = 1e-08
ADAM_WD = 0.01
ADAM_STEP = 10
PER_EXAMPLE_BATCH_AXIS = {'x': 0, 'loss_target': 0}
SHARED_INPUTS = []
_WEIGHT_DTYPES = {'norm1_g': _jnp.float32, 'w_in': _jnp.float32, 'w_decay_f': _jnp.float32, 'b_decay_f': _jnp.float32, 'w_decay_b': _jnp.float32, 'b_decay_b': _jnp.float32, 'gla_norm_g': _jnp.float32, 'gmlp_ln_g': _jnp.float32, 'gmlp_ln_b': _jnp.float32, 'w_spatial': _jnp.float32, 'b_spatial': _jnp.float32, 'w_out': _jnp.float32, 'norm2_g': _jnp.float32, 'w_gate': _jnp.float32, 'w_up': _jnp.float32, 'w_down': _jnp.float32, 'final_norm_g': _jnp.float32}
MOMENT_SCALE = {'norm1_g': 2.654003e-01, 'w_in': 1.653209e-01, 'w_decay_f': 1.458459e-02, 'b_decay_f': 6.578748e-02, 'w_decay_b': 1.437488e-02, 'b_decay_b': 5.833012e-02, 'gla_norm_g': 1.293359e-01, 'gmlp_ln_g': 1.418123e-01, 'gmlp_ln_b': 1.471043e-01, 'w_spatial': 1.408868e-01, 'b_spatial': 1.456169e-01, 'w_out': 1.702538e-01, 'norm2_g': 1.563555e-01, 'w_gate': 6.660979e-02, 'w_up': 6.441512e-02, 'w_down': 1.072114e-01, 'final_norm_g': 6.411304e+01}


def _to_microbatches(a, axis):
    t = _jnp.moveaxis(a, axis, 0)
    t = t.reshape((N_MICROBATCH, t.shape[0] // N_MICROBATCH) + t.shape[1:])
    return _jnp.moveaxis(t, 1, axis + 1)


def setup_inputs(seed: int = 0) -> dict:
    inp = _fwd_setup_inputs(seed)
    key = _jax.random.fold_in(_jax.random.key(seed), 7919)
    shape, _ = _output_shape()
    out = dict(inp)
    out["loss_target"] = _jax.random.normal(_jax.random.fold_in(key, 0), shape, _jnp.float32)
    for i, name in enumerate(TWIN_WEIGHTS):
        w = inp[name].astype(_jnp.float32)
        if MOMENT_SCALE is None:
            s = _jnp.sqrt(_jnp.mean(_jnp.square(w)) + 1e-30)
        else:
            s = MOMENT_SCALE[name]
        km, kv = _jax.random.split(_jax.random.fold_in(key, i + 1))
        out[name] = w
        out["m_" + name] = s * _jax.random.normal(km, w.shape, _jnp.float32)
        out["v_" + name] = (s * s) * _jax.random.uniform(kv, w.shape, _jnp.float32, 0.5, 1.5)
    if N_MICROBATCH > 1:
        for name, axis in PER_EXAMPLE_BATCH_AXIS.items():
            out[name] = _to_microbatches(out[name], axis)
    return {'x': out['x'], 'norm1_g': out['norm1_g'], 'w_in': out['w_in'], 'w_decay_f': out['w_decay_f'], 'b_decay_f': out['b_decay_f'], 'w_decay_b': out['w_decay_b'], 'b_decay_b': out['b_decay_b'], 'gla_norm_g': out['gla_norm_g'], 'gmlp_ln_g': out['gmlp_ln_g'], 'gmlp_ln_b': out['gmlp_ln_b'], 'w_spatial': out['w_spatial'], 'b_spatial': out['b_spatial'], 'w_out': out['w_out'], 'norm2_g': out['norm2_g'], 'w_gate': out['w_gate'], 'w_up': out['w_up'], 'w_down': out['w_down'], 'final_norm_g': out['final_norm_g'], 'loss_target': out['loss_target'], 'm_norm1_g': out['m_norm1_g'], 'm_w_in': out['m_w_in'], 'm_w_decay_f': out['m_w_decay_f'], 'm_b_decay_f': out['m_b_decay_f'], 'm_w_decay_b': out['m_w_decay_b'], 'm_b_decay_b': out['m_b_decay_b'], 'm_gla_norm_g': out['m_gla_norm_g'], 'm_gmlp_ln_g': out['m_gmlp_ln_g'], 'm_gmlp_ln_b': out['m_gmlp_ln_b'], 'm_w_spatial': out['m_w_spatial'], 'm_b_spatial': out['m_b_spatial'], 'm_w_out': out['m_w_out'], 'm_norm2_g': out['m_norm2_g'], 'm_w_gate': out['m_w_gate'], 'm_w_up': out['m_w_up'], 'm_w_down': out['m_w_down'], 'm_final_norm_g': out['m_final_norm_g'], 'v_norm1_g': out['v_norm1_g'], 'v_w_in': out['v_w_in'], 'v_w_decay_f': out['v_w_decay_f'], 'v_b_decay_f': out['v_b_decay_f'], 'v_w_decay_b': out['v_w_decay_b'], 'v_b_decay_b': out['v_b_decay_b'], 'v_gla_norm_g': out['v_gla_norm_g'], 'v_gmlp_ln_g': out['v_gmlp_ln_g'], 'v_gmlp_ln_b': out['v_gmlp_ln_b'], 'v_w_spatial': out['v_w_spatial'], 'v_b_spatial': out['v_b_spatial'], 'v_w_out': out['v_w_out'], 'v_norm2_g': out['v_norm2_g'], 'v_w_gate': out['v_w_gate'], 'v_w_up': out['v_w_up'], 'v_w_down': out['v_w_down'], 'v_final_norm_g': out['v_final_norm_g']}


def _loss(weights, diff, rest, loss_target):
    with _jax.named_scope("forward"):
        args = {**rest, TWIN_DIFF_INPUT: diff, **{k: w.astype(_WEIGHT_DTYPES[k]) for k, w in weights.items()}}
        y = _forward(args)
    with _jax.named_scope("loss_head"):
        err = _jnp.square(y.astype(_jnp.float32) - loss_target)
        return 0.5 * _jnp.sum(_jnp.mean(err, axis=-1)) if err.ndim else 0.5 * err


def _adamw(w, g, m, v):
    m = ADAM_B1 * m + (1.0 - ADAM_B1) * g
    v = ADAM_B2 * v + (1.0 - ADAM_B2) * _jnp.square(g)
    m_hat = m / (1.0 - ADAM_B1 ** ADAM_STEP)
    v_hat = v / (1.0 - ADAM_B2 ** ADAM_STEP)
    delta = -ADAM_LR * (m_hat / (_jnp.sqrt(v_hat) + ADAM_EPS) + ADAM_WD * w)
    return delta, m, v


def reference(x, norm1_g, w_in, w_decay_f, b_decay_f, w_decay_b, b_decay_b, gla_norm_g, gmlp_ln_g, gmlp_ln_b, w_spatial, b_spatial, w_out, norm2_g, w_gate, w_up, w_down, final_norm_g, loss_target, m_norm1_g, m_w_in, m_w_decay_f, m_b_decay_f, m_w_decay_b, m_b_decay_b, m_gla_norm_g, m_gmlp_ln_g, m_gmlp_ln_b, m_w_spatial, m_b_spatial, m_w_out, m_norm2_g, m_w_gate, m_w_up, m_w_down, m_final_norm_g, v_norm1_g, v_w_in, v_w_decay_f, v_b_decay_f, v_w_decay_b, v_b_decay_b, v_gla_norm_g, v_gmlp_ln_g, v_gmlp_ln_b, v_w_spatial, v_b_spatial, v_w_out, v_norm2_g, v_w_gate, v_w_up, v_w_down, v_final_norm_g):
    given = dict(x=x, norm1_g=norm1_g, w_in=w_in, w_decay_f=w_decay_f, b_decay_f=b_decay_f, w_decay_b=w_decay_b, b_decay_b=b_decay_b, gla_norm_g=gla_norm_g, gmlp_ln_g=gmlp_ln_g, gmlp_ln_b=gmlp_ln_b, w_spatial=w_spatial, b_spatial=b_spatial, w_out=w_out, norm2_g=norm2_g, w_gate=w_gate, w_up=w_up, w_down=w_down, final_norm_g=final_norm_g, loss_target=loss_target, m_norm1_g=m_norm1_g, m_w_in=m_w_in, m_w_decay_f=m_w_decay_f, m_b_decay_f=m_b_decay_f, m_w_decay_b=m_w_decay_b, m_b_decay_b=m_b_decay_b, m_gla_norm_g=m_gla_norm_g, m_gmlp_ln_g=m_gmlp_ln_g, m_gmlp_ln_b=m_gmlp_ln_b, m_w_spatial=m_w_spatial, m_b_spatial=m_b_spatial, m_w_out=m_w_out, m_norm2_g=m_norm2_g, m_w_gate=m_w_gate, m_w_up=m_w_up, m_w_down=m_w_down, m_final_norm_g=m_final_norm_g, v_norm1_g=v_norm1_g, v_w_in=v_w_in, v_w_decay_f=v_w_decay_f, v_b_decay_f=v_b_decay_f, v_w_decay_b=v_w_decay_b, v_b_decay_b=v_b_decay_b, v_gla_norm_g=v_gla_norm_g, v_gmlp_ln_g=v_gmlp_ln_g, v_gmlp_ln_b=v_gmlp_ln_b, v_w_spatial=v_w_spatial, v_b_spatial=v_b_spatial, v_w_out=v_w_out, v_norm2_g=v_norm2_g, v_w_gate=v_w_gate, v_w_up=v_w_up, v_w_down=v_w_down, v_final_norm_g=v_final_norm_g)
    weights = {n: given[n] for n in TWIN_WEIGHTS}
    shared = {n: given[n] for n in SHARED_INPUTS}
    per_example = {n: given[n] for n in ['x']}
    grad_fn = _jax.value_and_grad(_loss, argnums=(0, 1))

    def one_microbatch(ex, loss_target):
        ex = dict(ex)
        diff = ex.pop(TWIN_DIFF_INPUT)
        return grad_fn(weights, diff, {**shared, **ex}, loss_target)

    if N_MICROBATCH == 1:
        loss, (grad_w, grad_x) = one_microbatch(per_example, given["loss_target"])
    else:
        def body(carry, xs):
            loss_sum, grad_sum = carry
            l_k, (gw_k, gx_k) = one_microbatch(xs[0], xs[1])
            with _jax.named_scope("update"):
                return (loss_sum + l_k, _jax.tree.map(_jnp.add, grad_sum, gw_k)), gx_k

        init = (_jnp.zeros((), _jnp.float32), _jax.tree.map(_jnp.zeros_like, weights))
        (loss, grad_w), grad_x = _jax.lax.scan(body, init, (per_example, given["loss_target"]))
    with _jax.named_scope("update"):
        delta_w, new_m, new_v = {}, {}, {}
        for n in TWIN_WEIGHTS:
            delta_w[n], new_m[n], new_v[n] = _adamw(weights[n], grad_w[n], given["m_" + n], given["v_" + n])
    return (loss, grad_x, *[grad_w[n] for n in TWIN_WEIGHTS], *[delta_w[n] for n in TWIN_WEIGHTS],
            *[new_m[n] for n in TWIN_WEIGHTS], *[new_v[n] for n in TWIN_WEIGHTS])
```

```python
import functools
import math

import jax
import jax.numpy as jnp
from jax import lax
from jax.experimental import pallas as pl
from jax.experimental.pallas import tpu as pltpu

F32, BF16 = jnp.float32, jnp.bfloat16

D_MODEL = 1024
GLA_HEADS = 4
GLA_DK = 64
GLA_DV = 128
KEY_W = GLA_HEADS * GLA_DK
GLA_W = GLA_HEADS * GLA_DV
GMLP_W = 512
GMLP_GROUPS = 4
GMLP_CHUNK = 128
LOWRANK = 16
GLA_CHUNK = 64
GLA_TAU = 16.0
PROJ_W = 2592
PROJ_WP = 2688
D_FF = 2816
N_SHARDS = 4
FF_SHARD = D_FF // N_SHARDS
EPS = 1e-6
LANES = 128
MIB = 1024 * 1024

ADAM_LR = 0.001
ADAM_B1 = 0.9
ADAM_B2 = 0.999
ADAM_EPS = 1e-08
ADAM_WD = 0.01
ADAM_STEP = 10

COL_Q, COL_K = 0, 256
COL_V, COL_G, COL_U, COL_VV = 512, 1024, 1536, 2048
COL_LR = 2560

MESH = pl.DeviceIdType.MESH


def _nn(a, b):
    return jnp.dot(a, b, preferred_element_type=F32)


def _nt(a, b):
    return lax.dot_general(a, b, (((1,), (1,)), ((), ())), preferred_element_type=F32)


def _tn(a, b):
    return lax.dot_general(a, b, (((0,), (0,)), ((), ())), preferred_element_type=F32)


def _bnn(a, b):
    return jnp.einsum("nik,nkj->nij", a, b, preferred_element_type=F32)


def _bnt(a, b):
    return jnp.einsum("nik,njk->nij", a, b, preferred_element_type=F32)


def _btn(a, b):
    return jnp.einsum("nki,nkj->nij", a, b, preferred_element_type=F32)


def _resident(shape):
    zeros = (0,) * len(shape)
    return pl.BlockSpec(shape, lambda *_: zeros, pipeline_mode=pl.Buffered(1))


def _params(vmem_mib, semantics=("arbitrary",)):
    return pltpu.CompilerParams(vmem_limit_bytes=vmem_mib * MIB, dimension_semantics=semantics)


def _sigmoid(x):
    return 1.0 / (1.0 + jnp.exp(-x))


def _gelu(x):
    return 0.5 * x * (1.0 + lax.erf(x * (1.0 / math.sqrt(2.0))))


def _gelu_grad(x):
    return 0.5 * (1.0 + lax.erf(x * (1.0 / math.sqrt(2.0)))) + x * jnp.exp(-0.5 * x * x) * (1.0 / math.sqrt(2.0 * math.pi))


def _log_sigmoid(x):
    return jnp.minimum(x, 0.0) - jnp.log(1.0 + jnp.exp(-jnp.abs(x)))


def _rms_bwd(dxh, xh, r):
    return r * (dxh - xh * jnp.mean(dxh * xh, axis=-1, keepdims=True))


def _chunk_cumsum(v, row_in_chunk, reverse):
    rows = v.shape[0]
    for sh in (1, 2, 4, 8, 16, 32):
        if reverse:
            v = v + jnp.where(row_in_chunk + sh < GLA_CHUNK, pltpu.roll(v, rows - sh, axis=0), 0.0)
        else:
            v = v + jnp.where(row_in_chunk >= sh, pltpu.roll(v, sh, axis=0), 0.0)
    return v


def _inproj(x, g1, w_in_p):
    seq = x.shape[0]
    tm = min(seq, 512)

    def body(x_ref, g_ref, w_ref, p_ref):
        xv = x_ref[...]
        r = lax.rsqrt(jnp.mean(xv * xv, axis=-1, keepdims=True) + EPS)
        h = (xv * r * g_ref[...]).astype(BF16)
        p_ref[...] = _nn(h, w_ref[...])

    return pl.pallas_call(
        body,
        name="inproj",
        grid=(seq // tm,),
        in_specs=[pl.BlockSpec((tm, D_MODEL), lambda i: (i, 0)), _resident((1, D_MODEL)), _resident((D_MODEL, PROJ_WP))],
        out_specs=pl.BlockSpec((tm, PROJ_WP), lambda i: (i, 0)),
        out_shape=jax.ShapeDtypeStruct((seq, PROJ_WP), F32),
        compiler_params=_params(48, ("parallel",)),
    )(x, g1, w_in_p)


def _gla_tile(seq):
    return min(seq, 512)


def _gla_decay_terms(lr_bf, wd_ref, bd_ref, pair, row_in_chunk, reverse, n):
    cols = pl.ds(pair * LANES, LANES)
    pre = _nn(lr_bf, wd_ref[:, cols]) + bd_ref[:, cols]
    la = _log_sigmoid(pre) * (1.0 / GLA_TAU)
    b = _chunk_cumsum(la, row_in_chunk, reverse)
    b3 = b.reshape(n, GLA_CHUNK, LANES)
    blast = b3[:, 0:1, :] if reverse else b3[:, GLA_CHUNK - 1 : GLA_CHUNK, :]
    return pre, b3, blast


def _gla_fwd(p, wd_pad, bd, reverse):
    seq = p.shape[0]
    tg = _gla_tile(seq)
    nt = seq // tg
    n = tg // GLA_CHUNK
    scale = GLA_DK**-0.5

    def tile(i):
        return nt - 1 - i if reverse else i

    def body(q_ref, k_ref, v_ref, lr_ref, wd_ref, bd_ref, o_ref, st_ref, carry):
        @pl.when(pl.program_id(0) == 0)
        def _():
            carry[...] = jnp.zeros_like(carry)

        lr_bf = lr_ref[...].astype(BF16)
        row_in_chunk = lax.broadcasted_iota(jnp.int32, (tg, LANES), 0) % GLA_CHUNK
        lane_head = lax.broadcasted_iota(jnp.int32, (1, LANES), 1) // GLA_DK
        tt = lax.broadcasted_iota(jnp.int32, (GLA_CHUNK, GLA_CHUNK), 0)
        ss = lax.broadcasted_iota(jnp.int32, (GLA_CHUNK, GLA_CHUNK), 1)
        causal = (tt <= ss) if reverse else (tt >= ss)
        order = range(n - 1, -1, -1) if reverse else range(n)
        for pair in range(2):
            cols = pl.ds(pair * LANES, LANES)
            _, b3, blast = _gla_decay_terms(lr_bf, wd_ref, bd_ref, pair, row_in_chunk, reverse, n)
            q3 = q_ref[:, cols].reshape(n, GLA_CHUNK, LANES) * scale
            k3 = k_ref[:, cols].reshape(n, GLA_CHUNK, LANES)
            qd = q3 * jnp.exp(b3)
            kd = (k3 * jnp.exp(-b3)).astype(BF16)
            kte = k3 * jnp.exp(blast - b3)
            dec = jnp.exp(blast)
            for hh in range(2):
                h = 2 * pair + hh
                m = (lane_head == hh).astype(F32)
                qdh = (qd * m).astype(BF16)
                kteh = (kte * m).astype(BF16)
                vh = v_ref[:, pl.ds(h * GLA_DV, GLA_DV)].reshape(n, GLA_CHUNK, GLA_DV).astype(BF16)
                sc = jnp.where(causal, _bnt(qdh, kd), 0.0)
                o_intra = _bnn(sc.astype(BF16), vh)
                dst = _btn(vh, kteh)
                st = carry[h]
                for j in order:
                    st_ref[j, h] = st
                    st = st * dec[j] + dst[j]
                carry[h] = st
                o_inter = _bnt(qdh, st_ref[:, h].astype(BF16))
                o_ref[:, pl.ds(h * GLA_DV, GLA_DV)] = (o_intra + o_inter).reshape(tg, GLA_DV)

    nchunks = seq // GLA_CHUNK
    return pl.pallas_call(
        body,
        name="gla_fwd_rev" if reverse else "gla_fwd",
        grid=(nt,),
        in_specs=[
            pl.BlockSpec((tg, KEY_W), lambda i: (tile(i), COL_Q // KEY_W)),
            pl.BlockSpec((tg, KEY_W), lambda i: (tile(i), COL_K // KEY_W)),
            pl.BlockSpec((tg, GLA_W), lambda i: (tile(i), COL_V // GLA_W)),
            pl.BlockSpec((tg, LANES), lambda i: (tile(i), COL_LR // LANES)),
            _resident((LANES, KEY_W)),
            _resident((1, KEY_W)),
        ],
        out_specs=[
            pl.BlockSpec((tg, GLA_W), lambda i: (tile(i), 0)),
            pl.BlockSpec((n, GLA_HEADS, GLA_DV, LANES), lambda i: (tile(i), 0, 0, 0)),
        ],
        out_shape=[
            jax.ShapeDtypeStruct((seq, GLA_W), F32),
            jax.ShapeDtypeStruct((nchunks, GLA_HEADS, GLA_DV, LANES), F32),
        ],
        scratch_shapes=[pltpu.VMEM((GLA_HEADS, GLA_DV, LANES), F32)],
        compiler_params=_params(48),
    )(p, p, p, p, wd_pad, bd)


def _mixer_out(x, o_f, o_b, p, gn, lng, lnb, ws_bf, bs_col, w_out):
    seq = x.shape[0]
    tm = min(seq, 512)

    def body(x_ref, of_ref, ob_ref, g_ref, u_ref, vv_ref, gn_ref, lng_ref, lnb_ref, ws_ref, bs_ref, wo_ref, x1_ref, yc_ref, vn_sc):
        for h in range(GLA_HEADS):
            cols = pl.ds(h * GLA_DV, GLA_DV)
            oh = of_ref[:, cols] + ob_ref[:, cols]
            on = oh * lax.rsqrt(jnp.mean(oh * oh, axis=-1, keepdims=True) + EPS)
            gh = g_ref[:, cols]
            yc_ref[:, cols] = (on * gn_ref[:, cols] * (gh * _sigmoid(gh))).astype(BF16)
        zv = _gelu(vv_ref[...])
        xc = zv - jnp.mean(zv, axis=-1, keepdims=True)
        vhat = xc * lax.rsqrt(jnp.mean(xc * xc, axis=-1, keepdims=True) + EPS)
        vn_sc[...] = (vhat * lng_ref[...] + lnb_ref[...]).astype(BF16)
        for c in range(tm // GMLP_CHUNK):
            rows = pl.ds(c * GMLP_CHUNK, GMLP_CHUNK)
            for g in range(GMLP_GROUPS):
                cols = pl.ds(g * LANES, LANES)
                s = _nn(ws_ref[g], vn_sc[rows, cols]) + bs_ref[g]
                yc_ref[rows, pl.ds(GLA_W + g * LANES, LANES)] = (_gelu(u_ref[rows, cols]) * s).astype(BF16)
        x1_ref[...] = x_ref[...] + _nn(yc_ref[...], wo_ref[...])

    row = lambda w: pl.BlockSpec((tm, w), lambda i: (i, 0))
    pcol = lambda col: pl.BlockSpec((tm, GLA_W), lambda i: (i, col // GLA_W))
    return pl.pallas_call(
        body,
        name="mixer_out",
        grid=(seq // tm,),
        in_specs=[
            row(D_MODEL), row(GLA_W), row(GLA_W), pcol(COL_G), pcol(COL_U), pcol(COL_VV),
            _resident((1, GLA_W)), _resident((1, GMLP_W)), _resident((1, GMLP_W)),
            _resident((GMLP_GROUPS, GMLP_CHUNK, GMLP_CHUNK)), _resident((GMLP_GROUPS, GMLP_CHUNK, 1)),
            _resident((D_MODEL, D_MODEL)),
        ],
        out_specs=[row(D_MODEL), row(D_MODEL)],
        out_shape=[jax.ShapeDtypeStruct((seq, D_MODEL), F32), jax.ShapeDtypeStruct((seq, D_MODEL), BF16)],
        scratch_shapes=[pltpu.VMEM((tm, GMLP_W), BF16)],
        compiler_params=_params(48, ("parallel",)),
    )(x, o_f, o_b, p, p, p, gn, lng, lnb, ws_bf, bs_col, w_out)


def _ffn_fwd(x1, target, g2, gf, wg4, wu4, wd4):
    seq = x1.shape[0]
    tm = min(seq, 256)

    def body(x1_ref, t_ref, g2_ref, gf_ref, wg_ref, wu_ref, wd_ref, h2_ref, gate_ref, up_ref, act_ref, dx2_ref, loss_ref, dgf_ref):
        @pl.when(pl.program_id(0) == 0)
        def _():
            loss_ref[...] = jnp.zeros_like(loss_ref)
            dgf_ref[...] = jnp.zeros_like(dgf_ref)

        x1v = x1_ref[...]
        h2 = (x1v * lax.rsqrt(jnp.mean(x1v * x1v, axis=-1, keepdims=True) + EPS) * g2_ref[...]).astype(BF16)
        h2_ref[...] = h2
        acc = jnp.zeros((tm, D_MODEL), F32)
        for s in range(N_SHARDS):
            gate = _nn(h2, wg_ref[s])
            up = _nn(h2, wu_ref[s])
            act = (gate * _sigmoid(gate) * up).astype(BF16)
            gate_ref[s] = gate
            up_ref[s] = up
            act_ref[s] = act
            acc = acc + _nn(act, wd_ref[s])
        x2 = x1v + acc
        rf = lax.rsqrt(jnp.mean(x2 * x2, axis=-1, keepdims=True) + EPS)
        xh = x2 * rf
        err = xh * gf_ref[...] - t_ref[...]
        loss_ref[...] += 0.5 * jnp.sum(jnp.mean(err * err, axis=-1, keepdims=True))
        dy = err * (1.0 / D_MODEL)
        dgf_ref[...] += jnp.sum(dy * xh, axis=0, keepdims=True)
        dx2_ref[...] = _rms_bwd(dy * gf_ref[...], xh, rf)

    row = lambda w: pl.BlockSpec((tm, w), lambda i: (i, 0))
    ff = pl.BlockSpec((N_SHARDS, tm, FF_SHARD), lambda i: (0, i, 0))
    return pl.pallas_call(
        body,
        name="ffn_fwd",
        grid=(seq // tm,),
        in_specs=[
            row(D_MODEL), row(D_MODEL), _resident((1, D_MODEL)), _resident((1, D_MODEL)),
            _resident((N_SHARDS, D_MODEL, FF_SHARD)), _resident((N_SHARDS, D_MODEL, FF_SHARD)), _resident((N_SHARDS, FF_SHARD, D_MODEL)),
        ],
        out_specs=[row(D_MODEL), ff, ff, ff, row(D_MODEL), pl.BlockSpec((1, LANES), lambda i: (0, 0)), pl.BlockSpec((1, D_MODEL), lambda i: (0, 0))],
        out_shape=[
            jax.ShapeDtypeStruct((seq, D_MODEL), BF16),
            jax.ShapeDtypeStruct((N_SHARDS, seq, FF_SHARD), F32),
            jax.ShapeDtypeStruct((N_SHARDS, seq, FF_SHARD), F32),
            jax.ShapeDtypeStruct((N_SHARDS, seq, FF_SHARD), BF16),
            jax.ShapeDtypeStruct((seq, D_MODEL), F32),
            jax.ShapeDtypeStruct((1, LANES), F32),
            jax.ShapeDtypeStruct((1, D_MODEL), F32),
        ],
        compiler_params=_params(56),
    )(x1, target, g2, gf, wg4, wu4, wd4)


def _ffn_bwd(dx2, gate4, up4, x1, g2, wg4, wu4, wd4):
    seq = x1.shape[0]
    tm = min(seq, 256)

    def body(dx2_ref, gate_ref, up_ref, x1_ref, g2_ref, wg_ref, wu_ref, wd_ref, dgate_ref, dup_ref, dx1_ref, dg2_ref):
        @pl.when(pl.program_id(0) == 0)
        def _():
            dg2_ref[...] = jnp.zeros_like(dg2_ref)

        dx2v = dx2_ref[...]
        dx2b = dx2v.astype(BF16)
        dh2 = jnp.zeros((tm, D_MODEL), F32)
        for s in range(N_SHARDS):
            dact = _nt(dx2b, wd_ref[s])
            gate = gate_ref[s]
            sg = _sigmoid(gate)
            dgate = (dact * up_ref[s] * (sg * (1.0 + gate * (1.0 - sg)))).astype(BF16)
            dup = (dact * (gate * sg)).astype(BF16)
            dgate_ref[s] = dgate
            dup_ref[s] = dup
            dh2 = dh2 + _nt(dgate, wg_ref[s]) + _nt(dup, wu_ref[s])
        x1v = x1_ref[...]
        r2 = lax.rsqrt(jnp.mean(x1v * x1v, axis=-1, keepdims=True) + EPS)
        xh = x1v * r2
        dg2_ref[...] += jnp.sum(dh2 * xh, axis=0, keepdims=True)
        dx1_ref[...] = dx2v + _rms_bwd(dh2 * g2_ref[...], xh, r2)

    row = lambda w: pl.BlockSpec((tm, w), lambda i: (i, 0))
    ff = pl.BlockSpec((N_SHARDS, tm, FF_SHARD), lambda i: (0, i, 0))
    return pl.pallas_call(
        body,
        name="ffn_bwd",
        grid=(seq // tm,),
        in_specs=[
            row(D_MODEL), ff, ff, row(D_MODEL), _resident((1, D_MODEL)),
            _resident((N_SHARDS, D_MODEL, FF_SHARD)), _resident((N_SHARDS, D_MODEL, FF_SHARD)), _resident((N_SHARDS, FF_SHARD, D_MODEL)),
        ],
        out_specs=[ff, ff, row(D_MODEL), pl.BlockSpec((1, D_MODEL), lambda i: (0, 0))],
        out_shape=[
            jax.ShapeDtypeStruct((N_SHARDS, seq, FF_SHARD), BF16),
            jax.ShapeDtypeStruct((N_SHARDS, seq, FF_SHARD), BF16),
            jax.ShapeDtypeStruct((seq, D_MODEL), F32),
            jax.ShapeDtypeStruct((1, D_MODEL), F32),
        ],
        compiler_params=_params(56),
    )(dx2, gate4, up4, x1, g2, wg4, wu4, wd4)


def _ffn_wgrad(h2, dgate4, dup4, act4, dx2):
    seq = h2.shape[0]
    tm = min(seq, 512)

    def body(h2_ref, dgate_ref, dup_ref, act_ref, dx2_ref, dwg_ref, dwu_ref, dwd_ref):
        @pl.when(pl.program_id(1) == 0)
        def _():
            dwg_ref[...] = jnp.zeros_like(dwg_ref)
            dwu_ref[...] = jnp.zeros_like(dwu_ref)
            dwd_ref[...] = jnp.zeros_like(dwd_ref)

        h2v = h2_ref[...]
        dwg_ref[...] += _tn(h2v, dgate_ref[...])
        dwu_ref[...] += _tn(h2v, dup_ref[...])
        dwd_ref[...] += _tn(act_ref[...], dx2_ref[...].astype(BF16))

    ff = pl.BlockSpec((None, tm, FF_SHARD), lambda s, i: (s, i, 0))
    row = pl.BlockSpec((tm, D_MODEL), lambda s, i: (i, 0))
    return pl.pallas_call(
        body,
        name="ffn_wgrad",
        grid=(N_SHARDS, seq // tm),
        in_specs=[row, ff, ff, ff, row],
        out_specs=[
            pl.BlockSpec((None, D_MODEL, FF_SHARD), lambda s, i: (s, 0, 0)),
            pl.BlockSpec((None, D_MODEL, FF_SHARD), lambda s, i: (s, 0, 0)),
            pl.BlockSpec((None, FF_SHARD, D_MODEL), lambda s, i: (s, 0, 0)),
        ],
        out_shape=[
            jax.ShapeDtypeStruct((N_SHARDS, D_MODEL, FF_SHARD), F32),
            jax.ShapeDtypeStruct((N_SHARDS, D_MODEL, FF_SHARD), F32),
            jax.ShapeDtypeStruct((N_SHARDS, FF_SHARD, D_MODEL), F32),
        ],
        compiler_params=_params(48, ("parallel", "arbitrary")),
    )(h2, dgate4, dup4, act4, dx2)


def _mixer_bwd(dx1, ycat, o_f, o_b, p, gn, lng, lnb, ws_bf, wst_bf, bs_col, w_out):
    seq = dx1.shape[0]
    tm = min(seq, 512)
    nsteps = seq // tm

    def body(dx1_ref, yc_ref, of_ref, ob_ref, g_ref, u_ref, vv_ref, gn_ref, lng_ref, lnb_ref, ws_ref, wst_ref, bs_ref, wo_ref,
             do_ref, dg_ref, du_ref, dvv_ref, dwo_ref, dgn_ref, dlng_ref, dlnb_ref, dws_ref, dbs_ref, vn_sc, dvn_sc, dbs_acc):
        step = pl.program_id(0)

        @pl.when(step == 0)
        def _():
            for r in (dwo_ref, dgn_ref, dlng_ref, dlnb_ref, dws_ref, dbs_acc):
                r[...] = jnp.zeros_like(r)

        dx1b = dx1_ref[...].astype(BF16)
        dyc = _nt(dx1b, wo_ref[...])
        dwo_ref[...] += _tn(yc_ref[...], dx1b)
        for h in range(GLA_HEADS):
            cols = pl.ds(h * GLA_DV, GLA_DV)
            dya = dyc[:, h * GLA_DV : (h + 1) * GLA_DV]
            oh = of_ref[:, cols] + ob_ref[:, cols]
            rn = lax.rsqrt(jnp.mean(oh * oh, axis=-1, keepdims=True) + EPS)
            on = oh * rn
            gh = g_ref[:, cols]
            sg = _sigmoid(gh)
            sil = gh * sg
            gnh = gn_ref[:, cols]
            dgn_ref[:, cols] += jnp.sum(dya * on * sil, axis=0, keepdims=True)
            dg_ref[:, cols] = dya * on * gnh * (sg * (1.0 + gh * (1.0 - sg)))
            do_ref[:, cols] = _rms_bwd(dya * gnh * sil, on, rn)
        vv = vv_ref[...]
        zv = _gelu(vv)
        xc = zv - jnp.mean(zv, axis=-1, keepdims=True)
        rstd = lax.rsqrt(jnp.mean(xc * xc, axis=-1, keepdims=True) + EPS)
        vhat = xc * rstd
        vn_sc[...] = (vhat * lng_ref[...] + lnb_ref[...]).astype(BF16)
        for c in range(tm // GMLP_CHUNK):
            rows = pl.ds(c * GMLP_CHUNK, GMLP_CHUNK)
            for g in range(GMLP_GROUPS):
                cols = pl.ds(g * LANES, LANES)
                vn = vn_sc[rows, cols]
                s = _nn(ws_ref[g], vn) + bs_ref[g]
                dyb = dyc[c * GMLP_CHUNK : (c + 1) * GMLP_CHUNK, GLA_W + g * LANES : GLA_W + (g + 1) * LANES]
                u = u_ref[rows, cols]
                du_ref[rows, cols] = dyb * s * _gelu_grad(u)
                ds = dyb * _gelu(u)
                dbs_acc[g] += ds
                dsb = ds.astype(BF16)
                dws_ref[g] += _nt(dsb, vn)
                dvn_sc[rows, cols] = _nn(wst_ref[g], dsb)
        dvn = dvn_sc[...]
        dlng_ref[...] += jnp.sum(dvn * vhat, axis=0, keepdims=True)
        dlnb_ref[...] += jnp.sum(dvn, axis=0, keepdims=True)
        dvh = dvn * lng_ref[...]
        dzv = rstd * (dvh - jnp.mean(dvh, axis=-1, keepdims=True) - vhat * jnp.mean(dvh * vhat, axis=-1, keepdims=True))
        dvv_ref[...] = dzv * _gelu_grad(vv)

        @pl.when(step == nsteps - 1)
        def _():
            dbs_ref[...] = jnp.sum(dbs_acc[...], axis=-1, keepdims=True)

    row = lambda w: pl.BlockSpec((tm, w), lambda i: (i, 0))
    pcol = lambda col: pl.BlockSpec((tm, GLA_W), lambda i: (i, col // GLA_W))
    const = lambda shape: pl.BlockSpec(shape, lambda i: (0,) * len(shape))
    return pl.pallas_call(
        body,
        name="mixer_bwd",
        grid=(nsteps,),
        in_specs=[
            row(D_MODEL), row(D_MODEL), row(GLA_W), row(GLA_W), pcol(COL_G), pcol(COL_U), pcol(COL_VV),
            _resident((1, GLA_W)), _resident((1, GMLP_W)), _resident((1, GMLP_W)),
            _resident((GMLP_GROUPS, GMLP_CHUNK, GMLP_CHUNK)), _resident((GMLP_GROUPS, GMLP_CHUNK, GMLP_CHUNK)),
            _resident((GMLP_GROUPS, GMLP_CHUNK, 1)), _resident((D_MODEL, D_MODEL)),
        ],
        out_specs=[
            row(GLA_W), row(GLA_W), row(GMLP_W), row(GMLP_W), const((D_MODEL, D_MODEL)),
            const((1, GLA_W)), const((1, GMLP_W)), const((1, GMLP_W)),
            const((GMLP_GROUPS, GMLP_CHUNK, GMLP_CHUNK)), const((GMLP_GROUPS, GMLP_CHUNK, 1)),
        ],
        out_shape=[
            jax.ShapeDtypeStruct((seq, GLA_W), F32), jax.ShapeDtypeStruct((seq, GLA_W), F32),
            jax.ShapeDtypeStruct((seq, GMLP_W), F32), jax.ShapeDtypeStruct((seq, GMLP_W), F32),
            jax.ShapeDtypeStruct((D_MODEL, D_MODEL), F32),
            jax.ShapeDtypeStruct((1, GLA_W), F32), jax.ShapeDtypeStruct((1, GMLP_W), F32), jax.ShapeDtypeStruct((1, GMLP_W), F32),
            jax.ShapeDtypeStruct((GMLP_GROUPS, GMLP_CHUNK, GMLP_CHUNK), F32), jax.ShapeDtypeStruct((GMLP_GROUPS, GMLP_CHUNK, 1), F32),
        ],
        scratch_shapes=[pltpu.VMEM((tm, GMLP_W), BF16), pltpu.VMEM((tm, GMLP_W), F32), pltpu.VMEM((GMLP_GROUPS, GMLP_CHUNK, GMLP_CHUNK), F32)],
        compiler_params=_params(56),
    )(dx1, ycat, o_f, o_b, p, p, p, gn, lng, lnb, ws_bf, wst_bf, bs_col, w_out)


def _gla_bwd(p, do, st, wd_pad, bd, reverse):
    seq = p.shape[0]
    tg = _gla_tile(seq)
    nt = seq // tg
    n = tg // GLA_CHUNK
    scale = GLA_DK**-0.5

    def tile(i):
        return i if reverse else nt - 1 - i

    def body(q_ref, k_ref, v_ref, lr_ref, do_ref, st_ref, wd_ref, bd_ref, dq_ref, dk_ref, dv_ref, dlr_ref, dwd_ref, dbd_ref, carry, dsa):
        @pl.when(pl.program_id(0) == 0)
        def _():
            carry[...] = jnp.zeros_like(carry)
            dwd_ref[...] = jnp.zeros_like(dwd_ref)
            dbd_ref[...] = jnp.zeros_like(dbd_ref)

        lr_bf = lr_ref[...].astype(BF16)
        row_in_chunk = lax.broadcasted_iota(jnp.int32, (tg, LANES), 0) % GLA_CHUNK
        lane_head = lax.broadcasted_iota(jnp.int32, (1, LANES), 1) // GLA_DK
        tt = lax.broadcasted_iota(jnp.int32, (GLA_CHUNK, GLA_CHUNK), 0)
        ss = lax.broadcasted_iota(jnp.int32, (GLA_CHUNK, GLA_CHUNK), 1)
        causal = (tt <= ss) if reverse else (tt >= ss)
        causal_t = (tt >= ss) if reverse else (tt <= ss)
        order = range(n) if reverse else range(n - 1, -1, -1)
        dlr = jnp.zeros((tg, LANES), F32)
        for pair in range(2):
            cols = pl.ds(pair * LANES, LANES)
            pre, b3, blast = _gla_decay_terms(lr_bf, wd_ref, bd_ref, pair, row_in_chunk, reverse, n)
            q3 = q_ref[:, cols].reshape(n, GLA_CHUNK, LANES) * scale
            k3 = k_ref[:, cols].reshape(n, GLA_CHUNK, LANES)
            eb = jnp.exp(b3)
            emb = jnp.exp(-b3)
            ekte = jnp.exp(blast - b3)
            qd = q3 * eb
            kdf = k3 * emb
            kd = kdf.astype(BF16)
            kte = k3 * ekte
            dec = jnp.exp(blast)
            dqd = jnp.zeros((n, GLA_CHUNK, LANES), F32)
            dkd = jnp.zeros((n, GLA_CHUNK, LANES), F32)
            dkte = jnp.zeros((n, GLA_CHUNK, LANES), F32)
            ddec = jnp.zeros((n, 1, LANES), F32)
            for hh in range(2):
                h = 2 * pair + hh
                vcols = pl.ds(h * GLA_DV, GLA_DV)
                m = (lane_head == hh).astype(F32)
                qdh = (qd * m).astype(BF16)
                kteh = (kte * m).astype(BF16)
                vh = v_ref[:, vcols].reshape(n, GLA_CHUNK, GLA_DV).astype(BF16)
                doh = do_ref[:, vcols].reshape(n, GLA_CHUNK, GLA_DV).astype(BF16)
                stb = st_ref[:, h]
                stb_bf = stb.astype(BF16)
                sc_t = jnp.where(causal_t, _bnt(kd, qdh), 0.0).astype(BF16)
                dp = jnp.where(causal, _bnt(doh, vh), 0.0).astype(BF16)
                dp_t = jnp.where(causal_t, _bnt(vh, doh), 0.0).astype(BF16)
                a = _btn(doh, qdh)
                c = carry[h]
                for j in order:
                    dsa[j] = c
                    c = a[j] + dec[j] * c
                carry[h] = c
                dsa_f = dsa[...]
                dsa_bf = dsa_f.astype(BF16)
                dqd = dqd + (_bnn(dp, kd) * m + _bnn(doh, stb_bf))
                dkd = dkd + _bnn(dp_t, qdh)
                dkte = dkte + _bnn(vh, dsa_bf)
                ddec = ddec + jnp.sum(dsa_f * stb, axis=1, keepdims=True)
                dv_ref[:, vcols] = (_bnn(sc_t, doh) + _bnt(kteh, dsa_bf)).reshape(tg, GLA_DV)
            dq_ref[:, cols] = (dqd * (scale * eb)).reshape(tg, LANES)
            dk_ref[:, cols] = (dkd * emb + dkte * ekte).reshape(tg, LANES)
            db = dqd * qd - dkd * kdf - dkte * kte
            dblast = jnp.sum(dkte * kte, axis=1, keepdims=True) + ddec * dec
            dla = _chunk_cumsum(db.reshape(tg, LANES), row_in_chunk, not reverse) + jnp.broadcast_to(dblast, (n, GLA_CHUNK, LANES)).reshape(tg, LANES)
            dpre = (dla * (1.0 / GLA_TAU) * _sigmoid(-pre))
            dpre_bf = dpre.astype(BF16)
            dlr = dlr + _nt(dpre_bf, wd_ref[:, cols])
            dwd_ref[:, cols] += _tn(lr_bf, dpre_bf)
            dbd_ref[:, cols] += jnp.sum(dpre, axis=0, keepdims=True)
        dlr_ref[...] = dlr

    return pl.pallas_call(
        body,
        name="gla_bwd_rev" if reverse else "gla_bwd",
        grid=(nt,),
        in_specs=[
            pl.BlockSpec((tg, KEY_W), lambda i: (tile(i), COL_Q // KEY_W)),
            pl.BlockSpec((tg, KEY_W), lambda i: (tile(i), COL_K // KEY_W)),
            pl.BlockSpec((tg, GLA_W), lambda i: (tile(i), COL_V // GLA_W)),
            pl.BlockSpec((tg, LANES), lambda i: (tile(i), COL_LR // LANES)),
            pl.BlockSpec((tg, GLA_W), lambda i: (tile(i), 0)),
            pl.BlockSpec((n, GLA_HEADS, GLA_DV, LANES), lambda i: (tile(i), 0, 0, 0)),
            _resident((LANES, KEY_W)),
            _resident((1, KEY_W)),
        ],
        out_specs=[
            pl.BlockSpec((tg, KEY_W), lambda i: (tile(i), 0)),
            pl.BlockSpec((tg, KEY_W), lambda i: (tile(i), 0)),
            pl.BlockSpec((tg, GLA_W), lambda i: (tile(i), 0)),
            pl.BlockSpec((tg, LANES), lambda i: (tile(i), 0)),
            pl.BlockSpec((LANES, KEY_W), lambda i: (0, 0)),
            pl.BlockSpec((1, KEY_W), lambda i: (0, 0)),
        ],
        out_shape=[
            jax.ShapeDtypeStruct((seq, KEY_W), F32), jax.ShapeDtypeStruct((seq, KEY_W), F32),
            jax.ShapeDtypeStruct((seq, GLA_W), F32), jax.ShapeDtypeStruct((seq, LANES), F32),
            jax.ShapeDtypeStruct((LANES, KEY_W), F32), jax.ShapeDtypeStruct((1, KEY_W), F32),
        ],
        scratch_shapes=[pltpu.VMEM((GLA_HEADS, GLA_DV, LANES), F32), pltpu.VMEM((n, GLA_DV, LANES), F32)],
        compiler_params=_params(48),
    )(p, p, p, p, do, st, wd_pad, bd)


def _inproj_bwd(x, dx1, g1, w_in_p, dq_f, dq_b, dk_f, dk_b, dv_f, dv_b, dg, du, dvv, dlr_f, dlr_b):
    seq = x.shape[0]
    tm = min(seq, 256)

    def body(x_ref, dx1_ref, g1_ref, w_ref, dqf, dqb, dkf, dkb, dvf, dvb, dg_ref, du_ref, dvv_ref, dlrf, dlrb, dx_ref, dw_ref, dg1_ref, dp_sc):
        @pl.when(pl.program_id(0) == 0)
        def _():
            dw_ref[...] = jnp.zeros_like(dw_ref)
            dg1_ref[...] = jnp.zeros_like(dg1_ref)

        dp_sc[:, COL_Q : COL_Q + KEY_W] = (dqf[...] + dqb[...]).astype(BF16)
        dp_sc[:, COL_K : COL_K + KEY_W] = (dkf[...] + dkb[...]).astype(BF16)
        dp_sc[:, COL_V : COL_V + GLA_W] = (dvf[...] + dvb[...]).astype(BF16)
        dp_sc[:, COL_G : COL_G + GLA_W] = dg_ref[...].astype(BF16)
        dp_sc[:, COL_U : COL_U + GMLP_W] = du_ref[...].astype(BF16)
        dp_sc[:, COL_VV : COL_VV + GMLP_W] = dvv_ref[...].astype(BF16)
        dp_sc[:, COL_LR : COL_LR + LANES] = (dlrf[...] + dlrb[...]).astype(BF16)
        xv = x_ref[...]
        r1 = lax.rsqrt(jnp.mean(xv * xv, axis=-1, keepdims=True) + EPS)
        xh = xv * r1
        h = (xh * g1_ref[...]).astype(BF16)
        dp = dp_sc[...]
        dw_ref[...] += _tn(h, dp)
        dh = _nt(dp, w_ref[...])
        dg1_ref[...] += jnp.sum(dh * xh, axis=0, keepdims=True)
        dx_ref[...] = dx1_ref[...] + _rms_bwd(dh * g1_ref[...], xh, r1)

    row = lambda w: pl.BlockSpec((tm, w), lambda i: (i, 0))
    return pl.pallas_call(
        body,
        name="inproj_bwd",
        grid=(seq // tm,),
        in_specs=[
            row(D_MODEL), row(D_MODEL), _resident((1, D_MODEL)), _resident((D_MODEL, PROJ_WP)),
            row(KEY_W), row(KEY_W), row(KEY_W), row(KEY_W), row(GLA_W), row(GLA_W),
            row(GLA_W), row(GMLP_W), row(GMLP_W), row(LANES), row(LANES),
        ],
        out_specs=[row(D_MODEL), pl.BlockSpec((D_MODEL, PROJ_WP), lambda i: (0, 0)), pl.BlockSpec((1, D_MODEL), lambda i: (0, 0))],
        out_shape=[
            jax.ShapeDtypeStruct((seq, D_MODEL), F32),
            jax.ShapeDtypeStruct((D_MODEL, PROJ_WP), F32),
            jax.ShapeDtypeStruct((1, D_MODEL), F32),
        ],
        scratch_shapes=[pltpu.VMEM((tm, PROJ_WP), BF16)],
        compiler_params=_params(56),
    )(x, dx1, g1, w_in_p, dq_f, dq_b, dk_f, dk_b, dv_f, dv_b, dg, du, dvv, dlr_f, dlr_b)


def _row_tile(rows):
    for t in (256, 176, 128, 64, 32, 16, 8):
        if rows % t == 0:
            return t
    return rows


def _to_bf16(w):
    rows, cols = w.shape
    tr = _row_tile(rows)

    def body(w_ref, o_ref):
        o_ref[...] = w_ref[...].astype(BF16)

    spec = pl.BlockSpec((tr, cols), lambda i: (i, 0))
    return pl.pallas_call(
        body, name="to_bf16", grid=(rows // tr,), in_specs=[spec], out_specs=spec,
        out_shape=jax.ShapeDtypeStruct(w.shape, BF16), compiler_params=_params(32, ("parallel",)),
    )(w)


def _add_halves(g4, recv, c):
    _, rows, cols = g4.shape
    hr = rows // 2
    tr = _row_tile(hr)
    steps = hr // tr

    def body(c_ref, g_ref, r_ref, o_ref):
        o_ref[...] = g_ref[...] + r_ref[...]

    return pl.pallas_call(
        body,
        name="add_halves",
        grid_spec=pltpu.PrefetchScalarGridSpec(
            num_scalar_prefetch=1,
            grid=(N_SHARDS, steps),
            in_specs=[
                pl.BlockSpec((None, tr, cols), lambda s, i, c_ref: (s, c_ref[0] * steps + i, 0)),
                pl.BlockSpec((None, tr, cols), lambda s, i, c_ref: (s, i, 0)),
            ],
            out_specs=pl.BlockSpec((None, tr, cols), lambda s, i, c_ref: (s, i, 0)),
        ),
        out_shape=jax.ShapeDtypeStruct((N_SHARDS, hr, cols), F32),
        compiler_params=_params(32, ("parallel", "parallel")),
    )(c, g4, recv)


def _add_partials(part4, recv3, shard):
    _, hr, cols = part4.shape
    tr = _row_tile(hr)

    def body(s_ref, p_ref, r_ref, o_ref):
        o_ref[...] = ((p_ref[...] + r_ref[0]) + r_ref[1]) + r_ref[2]

    return pl.pallas_call(
        body,
        name="add_partials",
        grid_spec=pltpu.PrefetchScalarGridSpec(
            num_scalar_prefetch=1,
            grid=(hr // tr,),
            in_specs=[
                pl.BlockSpec((None, tr, cols), lambda i, s_ref: (s_ref[0], i, 0)),
                pl.BlockSpec((3, tr, cols), lambda i, s_ref: (0, i, 0)),
            ],
            out_specs=pl.BlockSpec((tr, cols), lambda i, s_ref: (i, 0)),
        ),
        out_shape=jax.ShapeDtypeStruct((hr, cols), F32),
        compiler_params=_params(32, ("parallel",)),
    )(shard, part4, recv3)


def _adam_math(w, g, m, v):
    m = ADAM_B1 * m + (1.0 - ADAM_B1) * g
    v = ADAM_B2 * v + (1.0 - ADAM_B2) * (g * g)
    m_hat = m / (1.0 - ADAM_B1**ADAM_STEP)
    v_hat = v / (1.0 - ADAM_B2**ADAM_STEP)
    delta = -ADAM_LR * (m_hat / (jnp.sqrt(v_hat) + ADAM_EPS) + ADAM_WD * w)
    return delta, m, v


def _adamw(w, g, m, v):
    rows, cols = w.shape
    tr = _row_tile(rows)

    def body(w_ref, g_ref, m_ref, v_ref, d_ref, mo_ref, vo_ref):
        d_ref[...], mo_ref[...], vo_ref[...] = _adam_math(w_ref[...], g_ref[...], m_ref[...], v_ref[...])

    spec = pl.BlockSpec((tr, cols), lambda i: (i, 0))
    shape = jax.ShapeDtypeStruct(w.shape, F32)
    return pl.pallas_call(
        body, name="adamw", grid=(rows // tr,), in_specs=[spec] * 4, out_specs=[spec] * 3, out_shape=[shape] * 3,
        compiler_params=_params(32, ("parallel",)),
    )(w, g, m, v)


SMALL_ROWS = 560
DECAY_ROWS = 8
SMALL_TOTAL = SMALL_ROWS + 2 * N_SHARDS * DECAY_ROWS


def _adamw_small(gathered, wp, mp, vp):
    out_rows = SMALL_ROWS + 2 * DECAY_ROWS

    def body(ga_ref, w_ref, m_ref, v_ref, g_ref, d_ref, mo_ref, vo_ref):
        shard = 2 * lax.axis_index("x") + lax.axis_index("y")
        g_ref[pl.ds(0, SMALL_ROWS), :] = functools.reduce(lambda a, b: a + b, [ga_ref[d, pl.ds(0, SMALL_ROWS), :] for d in range(8)])
        for k in range(2):
            start = pl.multiple_of(SMALL_ROWS + k * N_SHARDS * DECAY_ROWS + shard * DECAY_ROWS, DECAY_ROWS)
            g_ref[pl.ds(SMALL_ROWS + k * DECAY_ROWS, DECAY_ROWS), :] = functools.reduce(
                lambda a, b: a + b, [ga_ref[d, pl.ds(start, DECAY_ROWS), :] for d in range(8)])
        d_ref[...], mo_ref[...], vo_ref[...] = _adam_math(w_ref[...], g_ref[...], m_ref[...], v_ref[...])

    shape = jax.ShapeDtypeStruct((out_rows, LANES), F32)
    return pl.pallas_call(body, name="adamw_small", out_shape=[shape] * 4, compiler_params=_params(32, None))(gathered, wp, mp, vp)


ANY = pl.BlockSpec(memory_space=pl.ANY)


def _position():
    return lax.axis_index("x"), lax.axis_index("y"), lax.axis_index("c")


def _other_chips(x, y):
    return [(1 - x, y), (x, 1 - y), (1 - x, 1 - y)]


def _allgather_weights(shards):
    n = len(shards)

    def body(*refs):
        ins, outs = refs[:n], refs[n : 2 * n]
        send_sems, recv_sems, local_sems = refs[2 * n :]
        x, y, c = _position()
        me, sibling = (x, y, c), (x, y, 1 - c)
        chips = _other_chips(x, y)

        def block(k, px, py, pc):
            hr = shards[k].shape[0] // 2
            return outs[k].at[2 * px + py, pl.ds(pc * hr, hr), :]

        def copy(k, j, blk, to, src=None):
            return pltpu.make_async_remote_copy(
                src_ref=block(k, *blk) if src is None else src, dst_ref=block(k, *blk),
                send_sem=send_sems.at[7 * k + j], recv_sem=recv_sems.at[7 * k + j], device_id=to, device_id_type=MESH)

        started = []
        local = []
        for k in range(n):
            hr = shards[k].shape[0] // 2
            mine = ins[k].at[pl.ds(c * hr, hr), :]
            lc = pltpu.make_async_copy(mine, block(k, *me), local_sems.at[k])
            lc.start()
            local.append(lc)
            first = [copy(k, 0, me, sibling, src=mine)] + [copy(k, 1 + j, me, (*chip, c), src=mine) for j, chip in enumerate(chips)]
            for cp in first:
                cp.start()
            started += first
        for k in range(n):
            for j, chip in enumerate(chips):
                copy(k, 1 + j, (*chip, c), me).wait_recv()
                passed = copy(k, 4 + j, (*chip, c), sibling)
                passed.start()
                started.append(passed)
        for k in range(n):
            copy(k, 0, sibling, me).wait_recv()
            for j, chip in enumerate(chips):
                copy(k, 4 + j, (*chip, 1 - c), me).wait_recv()
        for cp in started:
            cp.wait_send()
        for lc in local:
            lc.wait()

    return pl.pallas_call(
        body,
        name="allgather_weights",
        in_specs=[ANY] * n,
        out_specs=[ANY] * n,
        out_shape=[jax.ShapeDtypeStruct((N_SHARDS,) + s.shape, s.dtype) for s in shards],
        scratch_shapes=[pltpu.SemaphoreType.DMA((7 * n,)), pltpu.SemaphoreType.DMA((7 * n,)), pltpu.SemaphoreType.DMA((n,))],
        compiler_params=pltpu.CompilerParams(has_side_effects=True),
    )(*shards)


def _exchange_halves(grads4):
    n = len(grads4)

    def body(*refs):
        ins, outs = refs[:n], refs[n : 2 * n]
        send_sems, recv_sems = refs[2 * n :]
        x, y, c = _position()
        copies = []
        for k in range(n):
            hr = grads4[k].shape[1] // 2
            cp = pltpu.make_async_remote_copy(
                src_ref=ins[k].at[:, pl.ds((1 - c) * hr, hr), :], dst_ref=outs[k],
                send_sem=send_sems.at[k], recv_sem=recv_sems.at[k], device_id=(x, y, 1 - c), device_id_type=MESH)
            cp.start()
            copies.append(cp)
        for cp in copies:
            cp.wait()

    return pl.pallas_call(
        body,
        name="exchange_halves",
        in_specs=[ANY] * n,
        out_specs=[ANY] * n,
        out_shape=[jax.ShapeDtypeStruct((N_SHARDS, g.shape[1] // 2, g.shape[2]), g.dtype) for g in grads4],
        scratch_shapes=[pltpu.SemaphoreType.DMA((n,)), pltpu.SemaphoreType.DMA((n,))],
        compiler_params=pltpu.CompilerParams(has_side_effects=True),
    )(*grads4)


def _scatter_partials(parts4):
    n = len(parts4)

    def body(*refs):
        ins, outs = refs[:n], refs[n : 2 * n]
        send_sems, recv_sems = refs[2 * n :]
        x, y, c = _position()
        copies = []
        for k in range(n):
            for j, (px, py) in enumerate(_other_chips(x, y)):
                cp = pltpu.make_async_remote_copy(
                    src_ref=ins[k].at[2 * px + py], dst_ref=outs[k].at[j],
                    send_sem=send_sems.at[3 * k + j], recv_sem=recv_sems.at[3 * k + j], device_id=(px, py, c), device_id_type=MESH)
                cp.start()
                copies.append(cp)
        for cp in copies:
            cp.wait()

    return pl.pallas_call(
        body,
        name="scatter_partials",
        in_specs=[ANY] * n,
        out_specs=[ANY] * n,
        out_shape=[jax.ShapeDtypeStruct((3,) + g.shape[1:], g.dtype) for g in parts4],
        scratch_shapes=[pltpu.SemaphoreType.DMA((3 * n,)), pltpu.SemaphoreType.DMA((3 * n,))],
        compiler_params=pltpu.CompilerParams(has_side_effects=True),
    )(*parts4)


def _join_halves(halves):
    n = len(halves)

    def body(*refs):
        ins, outs = refs[:n], refs[n : 2 * n]
        send_sems, recv_sems, local_sems = refs[2 * n :]
        x, y, c = _position()
        remote, local = [], []
        for k in range(n):
            hr = halves[k].shape[0]
            lc = pltpu.make_async_copy(ins[k], outs[k].at[pl.ds(c * hr, hr), :], local_sems.at[k])
            lc.start()
            local.append(lc)
            cp = pltpu.make_async_remote_copy(
                src_ref=ins[k], dst_ref=outs[k].at[pl.ds(c * hr, hr), :],
                send_sem=send_sems.at[k], recv_sem=recv_sems.at[k], device_id=(x, y, 1 - c), device_id_type=MESH)
            cp.start()
            remote.append(cp)
        for k in range(n):
            hr = halves[k].shape[0]
            remote[k].wait_send()
            pltpu.make_async_remote_copy(
                src_ref=ins[k], dst_ref=outs[k].at[pl.ds((1 - c) * hr, hr), :],
                send_sem=send_sems.at[k], recv_sem=recv_sems.at[k], device_id=(x, y, 1 - c), device_id_type=MESH).wait_recv()
            local[k].wait()

    return pl.pallas_call(
        body,
        name="join_halves",
        in_specs=[ANY] * n,
        out_specs=[ANY] * n,
        out_shape=[jax.ShapeDtypeStruct((2 * h.shape[0], h.shape[1]), h.dtype) for h in halves],
        scratch_shapes=[pltpu.SemaphoreType.DMA((n,)), pltpu.SemaphoreType.DMA((n,)), pltpu.SemaphoreType.DMA((n,))],
        compiler_params=pltpu.CompilerParams(has_side_effects=True),
    )(*halves)


def _allgather_small(block):
    m_per, ncol = block.shape

    def body(x_ref, out_ref, send_sems, recv_sems, local_sem):
        x, y, c = _position()
        me, sibling = (x, y, c), (x, y, 1 - c)
        chips = _other_chips(x, y)

        def rows(px, py, pc):
            return out_ref.at[4 * px + 2 * py + pc]

        def copy(k, blk, to, src=None):
            return pltpu.make_async_remote_copy(
                src_ref=rows(*blk) if src is None else src, dst_ref=rows(*blk),
                send_sem=send_sems.at[k], recv_sem=recv_sems.at[k], device_id=to, device_id_type=MESH)

        mine = pltpu.make_async_copy(x_ref, rows(*me), local_sem)
        mine.start()
        first = [copy(0, me, sibling, src=x_ref)] + [copy(1 + j, me, (*chip, c), src=x_ref) for j, chip in enumerate(chips)]
        for cp in first:
            cp.start()
        passed = [copy(4 + j, (*chip, c), sibling) for j, chip in enumerate(chips)]
        for j, chip in enumerate(chips):
            copy(1 + j, (*chip, c), me).wait_recv()
            passed[j].start()
        copy(0, sibling, me).wait_recv()
        for j, chip in enumerate(chips):
            copy(4 + j, (*chip, 1 - c), me).wait_recv()
        for cp in first + passed:
            cp.wait_send()
        mine.wait()

    return pl.pallas_call(
        body,
        name="allgather_small",
        in_specs=[pl.BlockSpec(memory_space=pltpu.VMEM)],
        out_specs=pl.BlockSpec(memory_space=pltpu.VMEM),
        out_shape=jax.ShapeDtypeStruct((8, m_per, ncol), block.dtype),
        scratch_shapes=[pltpu.SemaphoreType.DMA((7,)), pltpu.SemaphoreType.DMA((7,)), pltpu.SemaphoreType.DMA],
        compiler_params=pltpu.CompilerParams(has_side_effects=True, vmem_limit_bytes=32 * MIB),
    )(block)


SMALL_NAMES = ["norm1_g", "b_decay_f", "b_decay_b", "gla_norm_g", "gmlp_ln_g", "gmlp_ln_b", "w_spatial", "b_spatial", "norm2_g", "final_norm_g"]


def _pack_small(parts, decay_parts):
    flat = jnp.concatenate([a.reshape(-1) for a in parts])
    flat = jnp.pad(flat, (0, SMALL_ROWS * LANES - flat.shape[0])).reshape(SMALL_ROWS, LANES)
    return jnp.concatenate([flat] + [d.reshape(-1, LANES) for d in decay_parts], axis=0)


def _unpack_small(packed, like):
    out, off = [], 0
    flat = packed[:SMALL_ROWS].reshape(-1)
    for a in like:
        out.append(flat[off : off + a.size].reshape(a.shape))
        off += a.size
    return out


def kernel(x, norm1_g, w_in, w_decay_f, b_decay_f, w_decay_b, b_decay_b, gla_norm_g, gmlp_ln_g, gmlp_ln_b, w_spatial, b_spatial, w_out, norm2_g, w_gate, w_up, w_down, final_norm_g, loss_target, m_norm1_g, m_w_in, m_w_decay_f, m_b_decay_f, m_w_decay_b, m_b_decay_b, m_gla_norm_g, m_gmlp_ln_g, m_gmlp_ln_b, m_w_spatial, m_b_spatial, m_w_out, m_norm2_g, m_w_gate, m_w_up, m_w_down, m_final_norm_g, v_norm1_g, v_w_in, v_w_decay_f, v_b_decay_f, v_w_decay_b, v_b_decay_b, v_gla_norm_g, v_gmlp_ln_g, v_gmlp_ln_b, v_w_spatial, v_b_spatial, v_w_out, v_norm2_g, v_w_gate, v_w_up, v_w_down, v_final_norm_g):
    args = dict(locals())
    cx, cy, cc = lax.axis_index("x"), lax.axis_index("y"), lax.axis_index("c")
    shard = 2 * cx + cy
    xs = x[0]
    target = loss_target[0]

    big_names = ["w_in", "w_out", "w_gate", "w_up", "w_down"]
    big_shards = {k: args[k][0] for k in big_names}
    gathered = _allgather_weights([_to_bf16(big_shards[k]) for k in big_names])
    w_in4, w_out4, w_gate4, w_up4, w_down4 = gathered
    w_in_full = jnp.transpose(w_in4, (1, 0, 2)).reshape(D_MODEL, PROJ_W)
    w_in_p = jnp.concatenate(
        [w_in_full[:, :1536], w_in_full[:, 1568:], w_in_full[:, 1536:1568], jnp.zeros((D_MODEL, PROJ_WP - PROJ_W), BF16)], axis=1)
    w_out_full = w_out4.reshape(D_MODEL, D_MODEL)

    dec_block = jnp.concatenate([w_decay_f[0].reshape(-1, LANES), w_decay_b[0].reshape(-1, LANES)], axis=0)
    dec_all = _allgather_small(dec_block)
    dec_all = dec_all[::2].reshape(N_SHARDS, 2, LOWRANK, KEY_W // N_SHARDS)
    wdf_full = jnp.transpose(dec_all[:, 0], (1, 0, 2)).reshape(LOWRANK, KEY_W)
    wdb_full = jnp.transpose(dec_all[:, 1], (1, 0, 2)).reshape(LOWRANK, KEY_W)
    wd_pad_f = jnp.zeros((LANES, KEY_W), F32).at[0:LOWRANK].set(wdf_full).astype(BF16)
    wd_pad_b = jnp.zeros((LANES, KEY_W), F32).at[LOWRANK : 2 * LOWRANK].set(wdb_full).astype(BF16)

    ws_bf = w_spatial[0].astype(BF16)
    wst_bf = jnp.transpose(w_spatial[0], (0, 2, 1)).astype(BF16)
    bs_col = b_spatial[0].reshape(GMLP_GROUPS, GMLP_CHUNK, 1)

    p = _inproj(xs, norm1_g, w_in_p)
    o_f, st_f = _gla_fwd(p, wd_pad_f, b_decay_f, reverse=False)
    o_b, st_b = _gla_fwd(p, wd_pad_b, b_decay_b, reverse=True)
    x1, ycat = _mixer_out(xs, o_f, o_b, p, gla_norm_g, gmlp_ln_g, gmlp_ln_b, ws_bf, bs_col, w_out_full)
    gf = final_norm_g.reshape(1, D_MODEL)
    h2, gate4, up4, act4, dx2, loss_acc, dgf = _ffn_fwd(x1, target, norm2_g, gf, w_gate4, w_up4, w_down4)

    dgate4, dup4, dx1, dg2 = _ffn_bwd(dx2, gate4, up4, x1, norm2_g, w_gate4, w_up4, w_down4)
    dwg4, dwu4, dwd4 = _ffn_wgrad(h2, dgate4, dup4, act4, dx2)
    do, dg, du, dvv, dwo, dgn, dlng, dlnb, dws, dbs = _mixer_bwd(
        dx1, ycat, o_f, o_b, p, gla_norm_g, gmlp_ln_g, gmlp_ln_b, ws_bf, wst_bf, bs_col, w_out_full)
    dq_f, dk_f, dv_f, dlr_f, dwdec_f, dbdec_f = _gla_bwd(p, do, st_f, wd_pad_f, b_decay_f, reverse=False)
    dq_b, dk_b, dv_b, dlr_b, dwdec_b, dbdec_b = _gla_bwd(p, do, st_b, wd_pad_b, b_decay_b, reverse=True)
    dx, dwin_p, dg1 = _inproj_bwd(xs, dx1, norm1_g, w_in_p, dq_f, dq_b, dk_f, dk_b, dv_f, dv_b, dg, du, dvv, dlr_f, dlr_b)

    dwin = jnp.concatenate([dwin_p[:, :1536], dwin_p[:, 2560:2592], dwin_p[:, 1536:2560]], axis=1)
    dwin4 = jnp.transpose(dwin.reshape(D_MODEL, N_SHARDS, PROJ_W // N_SHARDS), (1, 0, 2))
    dwo4 = dwo.reshape(N_SHARDS, D_MODEL // N_SHARDS, D_MODEL)
    grads4 = [dwin4, dwo4, dwg4, dwu4, dwd4]
    c_arr = cc.reshape(1).astype(jnp.int32)
    s_arr = shard.reshape(1).astype(jnp.int32)
    recv_a = _exchange_halves(grads4)
    parts4 = [_add_halves(g, r, c_arr) for g, r in zip(grads4, recv_a)]
    recv_b = _scatter_partials(parts4)
    halves = [_add_partials(pp, r, s_arr) for pp, r in zip(parts4, recv_b)]
    big_grads = dict(zip(big_names, _join_halves(halves)))

    dwdec_f16 = dwdec_f[0:LOWRANK]
    dwdec_b16 = dwdec_b[LOWRANK : 2 * LOWRANK]
    shard_major = lambda a: jnp.transpose(a.reshape(LOWRANK, N_SHARDS, KEY_W // N_SHARDS), (1, 0, 2))
    small_grads = {
        "norm1_g": dg1, "b_decay_f": dbdec_f, "b_decay_b": dbdec_b, "gla_norm_g": dgn, "gmlp_ln_g": dlng, "gmlp_ln_b": dlnb,
        "w_spatial": dws, "b_spatial": dbs, "norm2_g": dg2, "final_norm_g": dgf,
    }
    g_pack = _pack_small([small_grads[k] for k in SMALL_NAMES], [shard_major(dwdec_f16), shard_major(dwdec_b16)])
    g_all = _allgather_small(g_pack)
    pack_own = lambda pre: _pack_small([args[pre + k] for k in SMALL_NAMES], [args[pre + "w_decay_f"], args[pre + "w_decay_b"]])
    sg, sd, sm, sv = _adamw_small(g_all, pack_own(""), pack_own("m_"), pack_own("v_"))

    names = ["norm1_g", "w_in", "w_decay_f", "b_decay_f", "w_decay_b", "b_decay_b", "gla_norm_g", "gmlp_ln_g", "gmlp_ln_b",
             "w_spatial", "b_spatial", "w_out", "norm2_g", "w_gate", "w_up", "w_down", "final_norm_g"]
    like = [args[k] for k in SMALL_NAMES]
    results = {"g": {}, "d": {}, "m": {}, "v": {}}
    for tag, packed in (("g", sg), ("d", sd), ("m", sm), ("v", sv)):
        for k, a in zip(SMALL_NAMES, _unpack_small(packed, like)):
            results[tag][k] = a
        results[tag]["w_decay_f"] = packed[SMALL_ROWS : SMALL_ROWS + DECAY_ROWS].reshape(w_decay_f.shape)
        results[tag]["w_decay_b"] = packed[SMALL_ROWS + DECAY_ROWS :].reshape(w_decay_b.shape)
    for k in big_names:
        g = big_grads[k]
        d, mo, vo = _adamw(big_shards[k], g, args["m_" + k][0], args["v_" + k][0])
        for tag, a in (("g", g), ("d", d), ("m", mo), ("v", vo)):
            results[tag][k] = a.reshape(args[k].shape)

    loss = lax.psum(loss_acc[0, 0], ("x", "y", "c"))
    grad_x = dx.reshape(x.shape)
    return (loss, grad_x, *[results["g"][k] for k in names], *[results["d"][k] for k in names],
            *[results["m"][k] for k in names], *[results["v"][k] for k in names])
```

```python
import functools
import math

import jax
import jax.numpy as jnp
from jax import lax
from jax.experimental import pallas as pl
from jax.experimental.pallas import tpu as pltpu

F32, BF16 = jnp.float32, jnp.bfloat16

D_MODEL = 1024
GLA_HEADS = 4
GLA_DK = 64
GLA_DV = 128
KEY_W = GLA_HEADS * GLA_DK
GLA_W = GLA_HEADS * GLA_DV
GMLP_W = 512
GMLP_GROUPS = 4
GMLP_CHUNK = 128
LOWRANK = 16
GLA_CHUNK = 64
GLA_TAU = 16.0
PROJ_W = 2592
PROJ_WP = 2688
D_FF = 2816
N_SHARDS = 4
FF_SHARD = D_FF // N_SHARDS
EPS = 1e-6
LANES = 128
MIB = 1024 * 1024

ADAM_LR = 0.001
ADAM_B1 = 0.9
ADAM_B2 = 0.999
ADAM_EPS = 1e-08
ADAM_WD = 0.01
ADAM_STEP = 10

COL_Q, COL_K = 0, 256
COL_V, COL_G, COL_U, COL_VV = 512, 1024, 1536, 2048
COL_LR = 2560

MESH = pl.DeviceIdType.MESH


def _nn(a, b):
    return jnp.dot(a, b, preferred_element_type=F32)


def _nt(a, b):
    return lax.dot_general(a, b, (((1,), (1,)), ((), ())), preferred_element_type=F32)


def _tn(a, b):
    return lax.dot_general(a, b, (((0,), (0,)), ((), ())), preferred_element_type=F32)


def _bnn(a, b):
    return jnp.einsum("nik,nkj->nij", a, b, preferred_element_type=F32)


def _bnt(a, b):
    return jnp.einsum("nik,njk->nij", a, b, preferred_element_type=F32)


def _btn(a, b):
    return jnp.einsum("nki,nkj->nij", a, b, preferred_element_type=F32)


def _resident(shape):
    zeros = (0,) * len(shape)
    return pl.BlockSpec(shape, lambda *_: zeros, pipeline_mode=pl.Buffered(1))


def _params(vmem_mib, semantics=("arbitrary",)):
    return pltpu.CompilerParams(vmem_limit_bytes=vmem_mib * MIB, dimension_semantics=semantics)


def _sigmoid(x):
    return 1.0 / (1.0 + jnp.exp(-x))


def _gelu(x):
    return 0.5 * x * (1.0 + lax.erf(x * (1.0 / math.sqrt(2.0))))


def _gelu_grad(x):
    return 0.5 * (1.0 + lax.erf(x * (1.0 / math.sqrt(2.0)))) + x * jnp.exp(-0.5 * x * x) * (1.0 / math.sqrt(2.0 * math.pi))


def _log_sigmoid(x):
    return jnp.minimum(x, 0.0) - jnp.log(1.0 + jnp.exp(-jnp.abs(x)))


def _rms_bwd(dxh, xh, r):
    return r * (dxh - xh * jnp.mean(dxh * xh, axis=-1, keepdims=True))


def _chunk_cumsum(v, row_in_chunk, reverse):
    rows = v.shape[0]
    for sh in (1, 2, 4, 8, 16, 32):
        if reverse:
            v = v + jnp.where(row_in_chunk + sh < GLA_CHUNK, pltpu.roll(v, rows - sh, axis=0), 0.0)
        else:
            v = v + jnp.where(row_in_chunk >= sh, pltpu.roll(v, sh, axis=0), 0.0)
    return v


def _inproj(x, g1, w_in_p):
    seq = x.shape[0]
    tm = min(seq, 512)

    def body(x_ref, g_ref, w_ref, p_ref):
        xv = x_ref[...]
        r = lax.rsqrt(jnp.mean(xv * xv, axis=-1, keepdims=True) + EPS)
        h = (xv * r * g_ref[...]).astype(BF16)
        p_ref[...] = _nn(h, w_ref[...])

    return pl.pallas_call(
        body,
        name="inproj",
        grid=(seq // tm,),
        in_specs=[pl.BlockSpec((tm, D_MODEL), lambda i: (i, 0)), _resident((1, D_MODEL)), _resident((D_MODEL, PROJ_WP))],
        out_specs=pl.BlockSpec((tm, PROJ_WP), lambda i: (i, 0)),
        out_shape=jax.ShapeDtypeStruct((seq, PROJ_WP), F32),
        compiler_params=_params(48, ("parallel",)),
    )(x, g1, w_in_p)


def _gla_tile(seq):
    return min(seq, 512)


def _gla_decay_terms(lr_bf, wd_ref, bd_ref, pair, row_in_chunk, reverse, n):
    cols = pl.ds(pair * LANES, LANES)
    pre = _nn(lr_bf, wd_ref[:, cols]) + bd_ref[:, cols]
    la = _log_sigmoid(pre) * (1.0 / GLA_TAU)
    b = _chunk_cumsum(la, row_in_chunk, reverse)
    b3 = b.reshape(n, GLA_CHUNK, LANES)
    blast = b3[:, 0:1, :] if reverse else b3[:, GLA_CHUNK - 1 : GLA_CHUNK, :]
    return pre, b3, blast


def _gla_fwd(p, wd_pad, bd, reverse):
    seq = p.shape[0]
    tg = _gla_tile(seq)
    nt = seq // tg
    n = tg // GLA_CHUNK
    scale = GLA_DK**-0.5

    def tile(i):
        return nt - 1 - i if reverse else i

    def body(q_ref, k_ref, v_ref, lr_ref, wd_ref, bd_ref, o_ref, st_ref, carry):
        @pl.when(pl.program_id(0) == 0)
        def _():
            carry[...] = jnp.zeros_like(carry)

        lr_bf = lr_ref[...].astype(BF16)
        row_in_chunk = lax.broadcasted_iota(jnp.int32, (tg, LANES), 0) % GLA_CHUNK
        lane_head = lax.broadcasted_iota(jnp.int32, (1, LANES), 1) // GLA_DK
        tt = lax.broadcasted_iota(jnp.int32, (GLA_CHUNK, GLA_CHUNK), 0)
        ss = lax.broadcasted_iota(jnp.int32, (GLA_CHUNK, GLA_CHUNK), 1)
        causal = (tt <= ss) if reverse else (tt >= ss)
        order = range(n - 1, -1, -1) if reverse else range(n)
        for pair in range(2):
            cols = pl.ds(pair * LANES, LANES)
            _, b3, blast = _gla_decay_terms(lr_bf, wd_ref, bd_ref, pair, row_in_chunk, reverse, n)
            q3 = q_ref[:, cols].reshape(n, GLA_CHUNK, LANES) * scale
            k3 = k_ref[:, cols].reshape(n, GLA_CHUNK, LANES)
            qd = q3 * jnp.exp(b3)
            kd = (k3 * jnp.exp(-b3)).astype(BF16)
            kte = k3 * jnp.exp(blast - b3)
            dec = jnp.exp(blast)
            for hh in range(2):
                h = 2 * pair + hh
                m = (lane_head == hh).astype(F32)
                qdh = (qd * m).astype(BF16)
                kteh = (kte * m).astype(BF16)
                vh = v_ref[:, pl.ds(h * GLA_DV, GLA_DV)].reshape(n, GLA_CHUNK, GLA_DV).astype(BF16)
                sc = jnp.where(causal, _bnt(qdh, kd), 0.0)
                o_intra = _bnn(sc.astype(BF16), vh)
                dst = _btn(vh, kteh)
                st = carry[h]
                for j in order:
                    st_ref[j, h] = st
                    st = st * dec[j] + dst[j]
                carry[h] = st
                o_inter = _bnt(qdh, st_ref[:, h].astype(BF16))
                o_ref[:, pl.ds(h * GLA_DV, GLA_DV)] = (o_intra + o_inter).reshape(tg, GLA_DV)

    nchunks = seq // GLA_CHUNK
    return pl.pallas_call(
        body,
        name="gla_fwd_rev" if reverse else "gla_fwd",
        grid=(nt,),
        in_specs=[
            pl.BlockSpec((tg, KEY_W), lambda i: (tile(i), COL_Q // KEY_W)),
            pl.BlockSpec((tg, KEY_W), lambda i: (tile(i), COL_K // KEY_W)),
            pl.BlockSpec((tg, GLA_W), lambda i: (tile(i), COL_V // GLA_W)),
            pl.BlockSpec((tg, LANES), lambda i: (tile(i), COL_LR // LANES)),
            _resident((LANES, KEY_W)),
            _resident((1, KEY_W)),
        ],
        out_specs=[
            pl.BlockSpec((tg, GLA_W), lambda i: (tile(i), 0)),
            pl.BlockSpec((n, GLA_HEADS, GLA_DV, LANES), lambda i: (tile(i), 0, 0, 0)),
        ],
        out_shape=[
            jax.ShapeDtypeStruct((seq, GLA_W), F32),
            jax.ShapeDtypeStruct((nchunks, GLA_HEADS, GLA_DV, LANES), F32),
        ],
        scratch_shapes=[pltpu.VMEM((GLA_HEADS, GLA_DV, LANES), F32)],
        compiler_params=_params(48),
    )(p, p, p, p, wd_pad, bd)


def _mixer_out(x, o_f, o_b, p, gn, lng, lnb, ws_bf, bs_col, w_out):
    seq = x.shape[0]
    tm = min(seq, 512)

    def body(x_ref, of_ref, ob_ref, g_ref, u_ref, vv_ref, gn_ref, lng_ref, lnb_ref, ws_ref, bs_ref, wo_ref, x1_ref, yc_ref, vn_sc):
        for h in range(GLA_HEADS):
            cols = pl.ds(h * GLA_DV, GLA_DV)
            oh = of_ref[:, cols] + ob_ref[:, cols]
            on = oh * lax.rsqrt(jnp.mean(oh * oh, axis=-1, keepdims=True) + EPS)
            gh = g_ref[:, cols]
            yc_ref[:, cols] = (on * gn_ref[:, cols] * (gh * _sigmoid(gh))).astype(BF16)
        zv = _gelu(vv_ref[...])
        xc = zv - jnp.mean(zv, axis=-1, keepdims=True)
        vhat = xc * lax.rsqrt(jnp.mean(xc * xc, axis=-1, keepdims=True) + EPS)
        vn_sc[...] = (vhat * lng_ref[...] + lnb_ref[...]).astype(BF16)
        for c in range(tm // GMLP_CHUNK):
            rows = pl.ds(c * GMLP_CHUNK, GMLP_CHUNK)
            for g in range(GMLP_GROUPS):
                cols = pl.ds(g * LANES, LANES)
                s = _nn(ws_ref[g], vn_sc[rows, cols]) + bs_ref[g]
                yc_ref[rows, pl.ds(GLA_W + g * LANES, LANES)] = (_gelu(u_ref[rows, cols]) * s).astype(BF16)
        x1_ref[...] = x_ref[...] + _nn(yc_ref[...], wo_ref[...])

    row = lambda w: pl.BlockSpec((tm, w), lambda i: (i, 0))
    pcol = lambda col: pl.BlockSpec((tm, GLA_W), lambda i: (i, col // GLA_W))
    return pl.pallas_call(
        body,
        name="mixer_out",
        grid=(seq // tm,),
        in_specs=[
            row(D_MODEL), row(GLA_W), row(GLA_W), pcol(COL_G), pcol(COL_U), pcol(COL_VV),
            _resident((1, GLA_W)), _resident((1, GMLP_W)), _resident((1, GMLP_W)),
            _resident((GMLP_GROUPS, GMLP_CHUNK, GMLP_CHUNK)), _resident((GMLP_GROUPS, GMLP_CHUNK, 1)),
            _resident((D_MODEL, D_MODEL)),
        ],
        out_specs=[row(D_MODEL), row(D_MODEL)],
        out_shape=[jax.ShapeDtypeStruct((seq, D_MODEL), F32), jax.ShapeDtypeStruct((seq, D_MODEL), BF16)],
        scratch_shapes=[pltpu.VMEM((tm, GMLP_W), BF16)],
        compiler_params=_params(48, ("parallel",)),
    )(x, o_f, o_b, p, p, p, gn, lng, lnb, ws_bf, bs_col, w_out)


def _ffn_fwd(x1, target, g2, gf, wg4, wu4, wd4):
    seq = x1.shape[0]
    tm = min(seq, 256)

    def body(x1_ref, t_ref, g2_ref, gf_ref, wg_ref, wu_ref, wd_ref, h2_ref, gate_ref, up_ref, act_ref, dx2_ref, loss_ref, dgf_ref):
        @pl.when(pl.program_id(0) == 0)
        def _():
            loss_ref[...] = jnp.zeros_like(loss_ref)
            dgf_ref[...] = jnp.zeros_like(dgf_ref)

        x1v = x1_ref[...]
        h2 = (x1v * lax.rsqrt(jnp.mean(x1v * x1v, axis=-1, keepdims=True) + EPS) * g2_ref[...]).astype(BF16)
        h2_ref[...] = h2
        acc = jnp.zeros((tm, D_MODEL), F32)
        for s in range(N_SHARDS):
            gate = _nn(h2, wg_ref[s])
            up = _nn(h2, wu_ref[s])
            act = (gate * _sigmoid(gate) * up).astype(BF16)
            gate_ref[s] = gate
            up_ref[s] = up
            act_ref[s] = act
            acc = acc + _nn(act, wd_ref[s])
        x2 = x1v + acc
        rf = lax.rsqrt(jnp.mean(x2 * x2, axis=-1, keepdims=True) + EPS)
        xh = x2 * rf
        err = xh * gf_ref[...] - t_ref[...]
        loss_ref[...] += 0.5 * jnp.sum(jnp.mean(err * err, axis=-1, keepdims=True))
        dy = err * (1.0 / D_MODEL)
        dgf_ref[...] += jnp.sum(dy * xh, axis=0, keepdims=True)
        dx2_ref[...] = _rms_bwd(dy * gf_ref[...], xh, rf)

    row = lambda w: pl.BlockSpec((tm, w), lambda i: (i, 0))
    ff = pl.BlockSpec((N_SHARDS, tm, FF_SHARD), lambda i: (0, i, 0))
    return pl.pallas_call(
        body,
        name="ffn_fwd",
        grid=(seq // tm,),
        in_specs=[
            row(D_MODEL), row(D_MODEL), _resident((1, D_MODEL)), _resident((1, D_MODEL)),
            _resident((N_SHARDS, D_MODEL, FF_SHARD)), _resident((N_SHARDS, D_MODEL, FF_SHARD)), _resident((N_SHARDS, FF_SHARD, D_MODEL)),
        ],
        out_specs=[row(D_MODEL), ff, ff, ff, row(D_MODEL), pl.BlockSpec((1, LANES), lambda i: (0, 0)), pl.BlockSpec((1, D_MODEL), lambda i: (0, 0))],
        out_shape=[
            jax.ShapeDtypeStruct((seq, D_MODEL), BF16),
            jax.ShapeDtypeStruct((N_SHARDS, seq, FF_SHARD), F32),
            jax.ShapeDtypeStruct((N_SHARDS, seq, FF_SHARD), F32),
            jax.ShapeDtypeStruct((N_SHARDS, seq, FF_SHARD), BF16),
            jax.ShapeDtypeStruct((seq, D_MODEL), F32),
            jax.ShapeDtypeStruct((1, LANES), F32),
            jax.ShapeDtypeStruct((1, D_MODEL), F32),
        ],
        compiler_params=_params(56),
    )(x1, target, g2, gf, wg4, wu4, wd4)


def _ffn_bwd(dx2, gate4, up4, x1, g2, wg4, wu4, wd4):
    seq = x1.shape[0]
    tm = min(seq, 256)

    def body(dx2_ref, gate_ref, up_ref, x1_ref, g2_ref, wg_ref, wu_ref, wd_ref, dgate_ref, dup_ref, dx1_ref, dg2_ref):
        @pl.when(pl.program_id(0) == 0)
        def _():
            dg2_ref[...] = jnp.zeros_like(dg2_ref)

        dx2v = dx2_ref[...]
        dx2b = dx2v.astype(BF16)
        dh2 = jnp.zeros((tm, D_MODEL), F32)
        for s in range(N_SHARDS):
            dact = _nt(dx2b, wd_ref[s])
            gate = gate_ref[s]
            sg = _sigmoid(gate)
            dgate = (dact * up_ref[s] * (sg * (1.0 + gate * (1.0 - sg)))).astype(BF16)
            dup = (dact * (gate * sg)).astype(BF16)
            dgate_ref[s] = dgate
            dup_ref[s] = dup
            dh2 = dh2 + _nt(dgate, wg_ref[s]) + _nt(dup, wu_ref[s])
        x1v = x1_ref[...]
        r2 = lax.rsqrt(jnp.mean(x1v * x1v, axis=-1, keepdims=True) + EPS)
        xh = x1v * r2
        dg2_ref[...] += jnp.sum(dh2 * xh, axis=0, keepdims=True)
        dx1_ref[...] = dx2v + _rms_bwd(dh2 * g2_ref[...], xh, r2)

    row = lambda w: pl.BlockSpec((tm, w), lambda i: (i, 0))
    ff = pl.BlockSpec((N_SHARDS, tm, FF_SHARD), lambda i: (0, i, 0))
    return pl.pallas_call(
        body,
        name="ffn_bwd",
        grid=(seq // tm,),
        in_specs=[
            row(D_MODEL), ff, ff, row(D_MODEL), _resident((1, D_MODEL)),
            _resident((N_SHARDS, D_MODEL, FF_SHARD)), _resident((N_SHARDS, D_MODEL, FF_SHARD)), _resident((N_SHARDS, FF_SHARD, D_MODEL)),
        ],
        out_specs=[ff, ff, row(D_MODEL), pl.BlockSpec((1, D_MODEL), lambda i: (0, 0))],
        out_shape=[
            jax.ShapeDtypeStruct((N_SHARDS, seq, FF_SHARD), BF16),
            jax.ShapeDtypeStruct((N_SHARDS, seq, FF_SHARD), BF16),
            jax.ShapeDtypeStruct((seq, D_MODEL), F32),
            jax.ShapeDtypeStruct((1, D_MODEL), F32),
        ],
        compiler_params=_params(56),
    )(dx2, gate4, up4, x1, g2, wg4, wu4, wd4)


def _ffn_wgrad(h2, dgate4, dup4, act4, dx2):
    seq = h2.shape[0]
    tm = min(seq, 512)

    def body(h2_ref, dgate_ref, dup_ref, act_ref, dx2_ref, dwg_ref, dwu_ref, dwd_ref):
        @pl.when(pl.program_id(1) == 0)
        def _():
            dwg_ref[...] = jnp.zeros_like(dwg_ref)
            dwu_ref[...] = jnp.zeros_like(dwu_ref)
            dwd_ref[...] = jnp.zeros_like(dwd_ref)

        h2v = h2_ref[...]
        dwg_ref[...] += _tn(h2v, dgate_ref[...])
        dwu_ref[...] += _tn(h2v, dup_ref[...])
        dwd_ref[...] += _tn(act_ref[...], dx2_ref[...].astype(BF16))

    ff = pl.BlockSpec((None, tm, FF_SHARD), lambda s, i: (s, i, 0))
    row = pl.BlockSpec((tm, D_MODEL), lambda s, i: (i, 0))
    return pl.pallas_call(
        body,
        name="ffn_wgrad",
        grid=(N_SHARDS, seq // tm),
        in_specs=[row, ff, ff, ff, row],
        out_specs=[
            pl.BlockSpec((None, D_MODEL, FF_SHARD), lambda s, i: (s, 0, 0)),
            pl.BlockSpec((None, D_MODEL, FF_SHARD), lambda s, i: (s, 0, 0)),
            pl.BlockSpec((None, FF_SHARD, D_MODEL), lambda s, i: (s, 0, 0)),
        ],
        out_shape=[
            jax.ShapeDtypeStruct((N_SHARDS, D_MODEL, FF_SHARD), F32),
            jax.ShapeDtypeStruct((N_SHARDS, D_MODEL, FF_SHARD), F32),
            jax.ShapeDtypeStruct((N_SHARDS, FF_SHARD, D_MODEL), F32),
        ],
        compiler_params=_params(48, ("parallel", "arbitrary")),
    )(h2, dgate4, dup4, act4, dx2)


def _mixer_bwd(dx1, ycat, o_f, o_b, p, gn, lng, lnb, ws_bf, wst_bf, bs_col, w_out):
    seq = dx1.shape[0]
    tm = min(seq, 512)
    nsteps = seq // tm

    def body(dx1_ref, yc_ref, of_ref, ob_ref, g_ref, u_ref, vv_ref, gn_ref, lng_ref, lnb_ref, ws_ref, wst_ref, bs_ref, wo_ref,
             do_ref, dg_ref, du_ref, dvv_ref, dwo_ref, dgn_ref, dlng_ref, dlnb_ref, dws_ref, dbs_ref, vn_sc, dvn_sc, dbs_acc):
        step = pl.program_id(0)

        @pl.when(step == 0)
        def _():
            for r in (dwo_ref, dgn_ref, dlng_ref, dlnb_ref, dws_ref, dbs_acc):
                r[...] = jnp.zeros_like(r)

        dx1b = dx1_ref[...].astype(BF16)
        dyc = _nt(dx1b, wo_ref[...])
        dwo_ref[...] += _tn(yc_ref[...], dx1b)
        for h in range(GLA_HEADS):
            cols = pl.ds(h * GLA_DV, GLA_DV)
            dya = dyc[:, h * GLA_DV : (h + 1) * GLA_DV]
            oh = of_ref[:, cols] + ob_ref[:, cols]
            rn = lax.rsqrt(jnp.mean(oh * oh, axis=-1, keepdims=True) + EPS)
            on = oh * rn
            gh = g_ref[:, cols]
            sg = _sigmoid(gh)
            sil = gh * sg
            gnh = gn_ref[:, cols]
            dgn_ref[:, cols] += jnp.sum(dya * on * sil, axis=0, keepdims=True)
            dg_ref[:, cols] = dya * on * gnh * (sg * (1.0 + gh * (1.0 - sg)))
            do_ref[:, cols] = _rms_bwd(dya * gnh * sil, on, rn)
        vv = vv_ref[...]
        zv = _gelu(vv)
        xc = zv - jnp.mean(zv, axis=-1, keepdims=True)
        rstd = lax.rsqrt(jnp.mean(xc * xc, axis=-1, keepdims=True) + EPS)
        vhat = xc * rstd
        vn_sc[...] = (vhat * lng_ref[...] + lnb_ref[...]).astype(BF16)
        for c in range(tm // GMLP_CHUNK):
            rows = pl.ds(c * GMLP_CHUNK, GMLP_CHUNK)
            for g in range(GMLP_GROUPS):
                cols = pl.ds(g * LANES, LANES)
                vn = vn_sc[rows, cols]
                s = _nn(ws_ref[g], vn) + bs_ref[g]
                dyb = dyc[c * GMLP_CHUNK : (c + 1) * GMLP_CHUNK, GLA_W + g * LANES : GLA_W + (g + 1) * LANES]
                u = u_ref[rows, cols]
                du_ref[rows, cols] = dyb * s * _gelu_grad(u)
                ds = dyb * _gelu(u)
                dbs_acc[g] += ds
                dsb = ds.astype(BF16)
                dws_ref[g] += _nt(dsb, vn)
                dvn_sc[rows, cols] = _nn(wst_ref[g], dsb)
        dvn = dvn_sc[...]
        dlng_ref[...] += jnp.sum(dvn * vhat, axis=0, keepdims=True)
        dlnb_ref[...] += jnp.sum(dvn, axis=0, keepdims=True)
        dvh = dvn * lng_ref[...]
        dzv = rstd * (dvh - jnp.mean(dvh, axis=-1, keepdims=True) - vhat * jnp.mean(dvh * vhat, axis=-1, keepdims=True))
        dvv_ref[...] = dzv * _gelu_grad(vv)

        @pl.when(step == nsteps - 1)
        def _():
            dbs_ref[...] = jnp.sum(dbs_acc[...], axis=-1, keepdims=True)

    row = lambda w: pl.BlockSpec((tm, w), lambda i: (i, 0))
    pcol = lambda col: pl.BlockSpec((tm, GLA_W), lambda i: (i, col // GLA_W))
    const = lambda shape: pl.BlockSpec(shape, lambda i: (0,) * len(shape))
    return pl.pallas_call(
        body,
        name="mixer_bwd",
        grid=(nsteps,),
        in_specs=[
            row(D_MODEL), row(D_MODEL), row(GLA_W), row(GLA_W), pcol(COL_G), pcol(COL_U), pcol(COL_VV),
            _resident((1, GLA_W)), _resident((1, GMLP_W)), _resident((1, GMLP_W)),
            _resident((GMLP_GROUPS, GMLP_CHUNK, GMLP_CHUNK)), _resident((GMLP_GROUPS, GMLP_CHUNK, GMLP_CHUNK)),
            _resident((GMLP_GROUPS, GMLP_CHUNK, 1)), _resident((D_MODEL, D_MODEL)),
        ],
        out_specs=[
            row(GLA_W), row(GLA_W), row(GMLP_W), row(GMLP_W), const((D_MODEL, D_MODEL)),
            const((1, GLA_W)), const((1, GMLP_W)), const((1, GMLP_W)),
            const((GMLP_GROUPS, GMLP_CHUNK, GMLP_CHUNK)), const((GMLP_GROUPS, GMLP_CHUNK, 1)),
        ],
        out_shape=[
            jax.ShapeDtypeStruct((seq, GLA_W), F32), jax.ShapeDtypeStruct((seq, GLA_W), F32),
            jax.ShapeDtypeStruct((seq, GMLP_W), F32), jax.ShapeDtypeStruct((seq, GMLP_W), F32),
            jax.ShapeDtypeStruct((D_MODEL, D_MODEL), F32),
            jax.ShapeDtypeStruct((1, GLA_W), F32), jax.ShapeDtypeStruct((1, GMLP_W), F32), jax.ShapeDtypeStruct((1, GMLP_W), F32),
            jax.ShapeDtypeStruct((GMLP_GROUPS, GMLP_CHUNK, GMLP_CHUNK), F32), jax.ShapeDtypeStruct((GMLP_GROUPS, GMLP_CHUNK, 1), F32),
        ],
        scratch_shapes=[pltpu.VMEM((tm, GMLP_W), BF16), pltpu.VMEM((tm, GMLP_W), F32), pltpu.VMEM((GMLP_GROUPS, GMLP_CHUNK, GMLP_CHUNK), F32)],
        compiler_params=_params(56),
    )(dx1, ycat, o_f, o_b, p, p, p, gn, lng, lnb, ws_bf, wst_bf, bs_col, w_out)


def _gla_bwd(p, do, st, wd_pad, bd, reverse):
    seq = p.shape[0]
    tg = _gla_tile(seq)
    nt = seq // tg
    n = tg // GLA_CHUNK
    scale = GLA_DK**-0.5

    def tile(i):
        return i if reverse else nt - 1 - i

    def body(q_ref, k_ref, v_ref, lr_ref, do_ref, st_ref, wd_ref, bd_ref, dq_ref, dk_ref, dv_ref, dlr_ref, dwd_ref, dbd_ref, carry, dsa):
        @pl.when(pl.program_id(0) == 0)
        def _():
            carry[...] = jnp.zeros_like(carry)
            dwd_ref[...] = jnp.zeros_like(dwd_ref)
            dbd_ref[...] = jnp.zeros_like(dbd_ref)

        lr_bf = lr_ref[...].astype(BF16)
        row_in_chunk = lax.broadcasted_iota(jnp.int32, (tg, LANES), 0) % GLA_CHUNK
        lane_head = lax.broadcasted_iota(jnp.int32, (1, LANES), 1) // GLA_DK
        tt = lax.broadcasted_iota(jnp.int32, (GLA_CHUNK, GLA_CHUNK), 0)
        ss = lax.broadcasted_iota(jnp.int32, (GLA_CHUNK, GLA_CHUNK), 1)
        causal = (tt <= ss) if reverse else (tt >= ss)
        causal_t = (tt >= ss) if reverse else (tt <= ss)
        order = range(n) if reverse else range(n - 1, -1, -1)
        dlr = jnp.zeros((tg, LANES), F32)
        for pair in range(2):
            cols = pl.ds(pair * LANES, LANES)
            pre, b3, blast = _gla_decay_terms(lr_bf, wd_ref, bd_ref, pair, row_in_chunk, reverse, n)
            q3 = q_ref[:, cols].reshape(n, GLA_CHUNK, LANES) * scale
            k3 = k_ref[:, cols].reshape(n, GLA_CHUNK, LANES)
            eb = jnp.exp(b3)
            emb = jnp.exp(-b3)
            ekte = jnp.exp(blast - b3)
            qd = q3 * eb
            kdf = k3 * emb
            kd = kdf.astype(BF16)
            kte = k3 * ekte
            dec = jnp.exp(blast)
            dqd = jnp.zeros((n, GLA_CHUNK, LANES), F32)
            dkd = jnp.zeros((n, GLA_CHUNK, LANES), F32)
            dkte = jnp.zeros((n, GLA_CHUNK, LANES), F32)
            ddec = jnp.zeros((n, 1, LANES), F32)
            for hh in range(2):
                h = 2 * pair + hh
                vcols = pl.ds(h * GLA_DV, GLA_DV)
                m = (lane_head == hh).astype(F32)
                qdh = (qd * m).astype(BF16)
                kteh = (kte * m).astype(BF16)
                vh = v_ref[:, vcols].reshape(n, GLA_CHUNK, GLA_DV).astype(BF16)
                doh = do_ref[:, vcols].reshape(n, GLA_CHUNK, GLA_DV).astype(BF16)
                stb = st_ref[:, h]
                stb_bf = stb.astype(BF16)
                sc_t = jnp.where(causal_t, _bnt(kd, qdh), 0.0).astype(BF16)
                dp = jnp.where(causal, _bnt(doh, vh), 0.0).astype(BF16)
                dp_t = jnp.where(causal_t, _bnt(vh, doh), 0.0).astype(BF16)
                a = _btn(doh, qdh)
                c = carry[h]
                for j in order:
                    dsa[j] = c
                    c = a[j] + dec[j] * c
                carry[h] = c
                dsa_f = dsa[...]
                dsa_bf = dsa_f.astype(BF16)
                dqd = dqd + (_bnn(dp, kd) * m + _bnn(doh, stb_bf))
                dkd = dkd + _bnn(dp_t, qdh)
                dkte = dkte + _bnn(vh, dsa_bf)
                ddec = ddec + jnp.sum(dsa_f * stb, axis=1, keepdims=True)
                dv_ref[:, vcols] = (_bnn(sc_t, doh) + _bnt(kteh, dsa_bf)).reshape(tg, GLA_DV)
            dq_ref[:, cols] = (dqd * (scale * eb)).reshape(tg, LANES)
            dk_ref[:, cols] = (dkd * emb + dkte * ekte).reshape(tg, LANES)
            db = dqd * qd - dkd * kdf - dkte * kte
            dblast = jnp.sum(dkte * kte, axis=1, keepdims=True) + ddec * dec
            dla = _chunk_cumsum(db.reshape(tg, LANES), row_in_chunk, not reverse) + jnp.broadcast_to(dblast, (n, GLA_CHUNK, LANES)).reshape(tg, LANES)
            dpre = (dla * (1.0 / GLA_TAU) * _sigmoid(-pre))
            dpre_bf = dpre.astype(BF16)
            dlr = dlr + _nt(dpre_bf, wd_ref[:, cols])
            dwd_ref[:, cols] += _tn(lr_bf, dpre_bf)
            dbd_ref[:, cols] += jnp.sum(dpre, axis=0, keepdims=True)
        dlr_ref[...] = dlr

    return pl.pallas_call(
        body,
        name="gla_bwd_rev" if reverse else "gla_bwd",
        grid=(nt,),
        in_specs=[
            pl.BlockSpec((tg, KEY_W), lambda i: (tile(i), COL_Q // KEY_W)),
            pl.BlockSpec((tg, KEY_W), lambda i: (tile(i), COL_K // KEY_W)),
            pl.BlockSpec((tg, GLA_W), lambda i: (tile(i), COL_V // GLA_W)),
            pl.BlockSpec((tg, LANES), lambda i: (tile(i), COL_LR // LANES)),
            pl.BlockSpec((tg, GLA_W), lambda i: (tile(i), 0)),
            pl.BlockSpec((n, GLA_HEADS, GLA_DV, LANES), lambda i: (tile(i), 0, 0, 0)),
            _resident((LANES, KEY_W)),
            _resident((1, KEY_W)),
        ],
        out_specs=[
            pl.BlockSpec((tg, KEY_W), lambda i: (tile(i), 0)),
            pl.BlockSpec((tg, KEY_W), lambda i: (tile(i), 0)),
            pl.BlockSpec((tg, GLA_W), lambda i: (tile(i), 0)),
            pl.BlockSpec((tg, LANES), lambda i: (tile(i), 0)),
            pl.BlockSpec((LANES, KEY_W), lambda i: (0, 0)),
            pl.BlockSpec((1, KEY_W), lambda i: (0, 0)),
        ],
        out_shape=[
            jax.ShapeDtypeStruct((seq, KEY_W), F32), jax.ShapeDtypeStruct((seq, KEY_W), F32),
            jax.ShapeDtypeStruct((seq, GLA_W), F32), jax.ShapeDtypeStruct((seq, LANES), F32),
            jax.ShapeDtypeStruct((LANES, KEY_W), F32), jax.ShapeDtypeStruct((1, KEY_W), F32),
        ],
        scratch_shapes=[pltpu.VMEM((GLA_HEADS, GLA_DV, LANES), F32), pltpu.VMEM((n, GLA_DV, LANES), F32)],
        compiler_params=_params(48),
    )(p, p, p, p, do, st, wd_pad, bd)


def _inproj_bwd(x, dx1, g1, w_in_p, dq_f, dq_b, dk_f, dk_b, dv_f, dv_b, dg, du, dvv, dlr_f, dlr_b):
    seq = x.shape[0]
    tm = min(seq, 256)

    def body(x_ref, dx1_ref, g1_ref, w_ref, dqf, dqb, dkf, dkb, dvf, dvb, dg_ref, du_ref, dvv_ref, dlrf, dlrb, dx_ref, dw_ref, dg1_ref, dp_sc):
        @pl.when(pl.program_id(0) == 0)
        def _():
            dw_ref[...] = jnp.zeros_like(dw_ref)
            dg1_ref[...] = jnp.zeros_like(dg1_ref)

        dp_sc[:, COL_Q : COL_Q + KEY_W] = (dqf[...] + dqb[...]).astype(BF16)
        dp_sc[:, COL_K : COL_K + KEY_W] = (dkf[...] + dkb[...]).astype(BF16)
        dp_sc[:, COL_V : COL_V + GLA_W] = (dvf[...] + dvb[...]).astype(BF16)
        dp_sc[:, COL_G : COL_G + GLA_W] = dg_ref[...].astype(BF16)
        dp_sc[:, COL_U : COL_U + GMLP_W] = du_ref[...].astype(BF16)
        dp_sc[:, COL_VV : COL_VV + GMLP_W] = dvv_ref[...].astype(BF16)
        dp_sc[:, COL_LR : COL_LR + LANES] = (dlrf[...] + dlrb[...]).astype(BF16)
        xv = x_ref[...]
        r1 = lax.rsqrt(jnp.mean(xv * xv, axis=-1, keepdims=True) + EPS)
        xh = xv * r1
        h = (xh * g1_ref[...]).astype(BF16)
        dp = dp_sc[...]
        dw_ref[...] += _tn(h, dp)
        dh = _nt(dp, w_ref[...])
        dg1_ref[...] += jnp.sum(dh * xh, axis=0, keepdims=True)
        dx_ref[...] = dx1_ref[...] + _rms_bwd(dh * g1_ref[...], xh, r1)

    row = lambda w: pl.BlockSpec((tm, w), lambda i: (i, 0))
    return pl.pallas_call(
        body,
        name="inproj_bwd",
        grid=(seq // tm,),
        in_specs=[
            row(D_MODEL), row(D_MODEL), _resident((1, D_MODEL)), _resident((D_MODEL, PROJ_WP)),
            row(KEY_W), row(KEY_W), row(KEY_W), row(KEY_W), row(GLA_W), row(GLA_W),
            row(GLA_W), row(GMLP_W), row(GMLP_W), row(LANES), row(LANES),
        ],
        out_specs=[row(D_MODEL), pl.BlockSpec((D_MODEL, PROJ_WP), lambda i: (0, 0)), pl.BlockSpec((1, D_MODEL), lambda i: (0, 0))],
        out_shape=[
            jax.ShapeDtypeStruct((seq, D_MODEL), F32),
            jax.ShapeDtypeStruct((D_MODEL, PROJ_WP), F32),
            jax.ShapeDtypeStruct((1, D_MODEL), F32),
        ],
        scratch_shapes=[pltpu.VMEM((tm, PROJ_WP), BF16)],
        compiler_params=_params(56),
    )(x, dx1, g1, w_in_p, dq_f, dq_b, dk_f, dk_b, dv_f, dv_b, dg, du, dvv, dlr_f, dlr_b)


def _row_tile(rows):
    for t in (256, 176, 128, 64, 32, 16, 8):
        if rows % t == 0:
            return t
    return rows


def _to_bf16(w):
    rows, cols = w.shape
    tr = _row_tile(rows)

    def body(w_ref, o_ref):
        o_ref[...] = w_ref[...].astype(BF16)

    spec = pl.BlockSpec((tr, cols), lambda i: (i, 0))
    return pl.pallas_call(
        body, name="to_bf16", grid=(rows // tr,), in_specs=[spec], out_specs=spec,
        out_shape=jax.ShapeDtypeStruct(w.shape, BF16), compiler_params=_params(32, ("parallel",)),
    )(w)


def _add_halves(g4, recv, c):
    _, rows, cols = g4.shape
    hr = rows // 2
    tr = _row_tile(hr)
    steps = hr // tr

    def body(c_ref, g_ref, r_ref, o_ref, ob_ref):
        total = g_ref[...] + r_ref[...]
        o_ref[...] = total
        ob_ref[...] = total.astype(BF16)

    out = pl.BlockSpec((None, tr, cols), lambda s, i, c_ref: (s, i, 0))
    return pl.pallas_call(
        body,
        name="add_halves",
        grid_spec=pltpu.PrefetchScalarGridSpec(
            num_scalar_prefetch=1,
            grid=(N_SHARDS, steps),
            in_specs=[pl.BlockSpec((None, tr, cols), lambda s, i, c_ref: (s, c_ref[0] * steps + i, 0)), out],
            out_specs=[out, out],
        ),
        out_shape=[jax.ShapeDtypeStruct((N_SHARDS, hr, cols), F32), jax.ShapeDtypeStruct((N_SHARDS, hr, cols), BF16)],
        compiler_params=_params(32, ("parallel", "parallel")),
    )(c, g4, recv)


def _add_partials(part4, recv3, shard_core):
    _, hr, cols = part4.shape
    tr = _row_tile(hr)
    steps = hr // tr

    def body(sc_ref, p_ref, r_ref, o_ref):
        o_ref[...] = ((p_ref[...] + r_ref[0].astype(F32)) + r_ref[1].astype(F32)) + r_ref[2].astype(F32)

    return pl.pallas_call(
        body,
        name="add_partials",
        grid_spec=pltpu.PrefetchScalarGridSpec(
            num_scalar_prefetch=1,
            grid=(steps,),
            in_specs=[
                pl.BlockSpec((None, tr, cols), lambda i, sc_ref: (sc_ref[0], i, 0)),
                pl.BlockSpec((3, tr, cols), lambda i, sc_ref: (0, i, 0)),
            ],
            out_specs=pl.BlockSpec((tr, cols), lambda i, sc_ref: (sc_ref[1] * steps + i, 0)),
        ),
        out_shape=jax.ShapeDtypeStruct((2 * hr, cols), F32),
        compiler_params=_params(32, ("parallel",)),
    )(shard_core, part4, recv3)


def _adam_math(w, g, m, v):
    m = ADAM_B1 * m + (1.0 - ADAM_B1) * g
    v = ADAM_B2 * v + (1.0 - ADAM_B2) * (g * g)
    m_hat = m / (1.0 - ADAM_B1**ADAM_STEP)
    v_hat = v / (1.0 - ADAM_B2**ADAM_STEP)
    delta = -ADAM_LR * (m_hat / (jnp.sqrt(v_hat) + ADAM_EPS) + ADAM_WD * w)
    return delta, m, v


def _adamw(w, g, m, v):
    rows, cols = w.shape
    tr = _row_tile(rows)

    def body(w_ref, g_ref, m_ref, v_ref, d_ref, mo_ref, vo_ref):
        d_ref[...], mo_ref[...], vo_ref[...] = _adam_math(w_ref[...], g_ref[...], m_ref[...], v_ref[...])

    spec = pl.BlockSpec((tr, cols), lambda i: (i, 0))
    shape = jax.ShapeDtypeStruct(w.shape, F32)
    return pl.pallas_call(
        body, name="adamw", grid=(rows // tr,), in_specs=[spec] * 4, out_specs=[spec] * 3, out_shape=[shape] * 3,
        compiler_params=_params(32, ("parallel",)),
    )(w, g, m, v)


SMALL_ROWS = 560
DECAY_ROWS = 8
SMALL_TOTAL = SMALL_ROWS + 2 * N_SHARDS * DECAY_ROWS


def _adamw_small(gathered, wp, mp, vp):
    out_rows = SMALL_ROWS + 2 * DECAY_ROWS

    def body(ga_ref, w_ref, m_ref, v_ref, g_ref, d_ref, mo_ref, vo_ref):
        shard = 2 * lax.axis_index("x") + lax.axis_index("y")
        g_ref[pl.ds(0, SMALL_ROWS), :] = functools.reduce(lambda a, b: a + b, [ga_ref[d, pl.ds(0, SMALL_ROWS), :] for d in range(8)])
        for k in range(2):
            start = pl.multiple_of(SMALL_ROWS + k * N_SHARDS * DECAY_ROWS + shard * DECAY_ROWS, DECAY_ROWS)
            g_ref[pl.ds(SMALL_ROWS + k * DECAY_ROWS, DECAY_ROWS), :] = functools.reduce(
                lambda a, b: a + b, [ga_ref[d, pl.ds(start, DECAY_ROWS), :] for d in range(8)])
        d_ref[...], mo_ref[...], vo_ref[...] = _adam_math(w_ref[...], g_ref[...], m_ref[...], v_ref[...])

    shape = jax.ShapeDtypeStruct((out_rows, LANES), F32)
    return pl.pallas_call(body, name="adamw_small", out_shape=[shape] * 4, compiler_params=_params(32, None))(gathered, wp, mp, vp)


ANY = pl.BlockSpec(memory_space=pl.ANY)


def _position():
    return lax.axis_index("x"), lax.axis_index("y"), lax.axis_index("c")


def _other_chips(x, y):
    return [(1 - x, y), (x, 1 - y), (1 - x, 1 - y)]


def _allgather_weights(shards):
    n = len(shards)

    def body(*refs):
        ins, outs = refs[:n], refs[n : 2 * n]
        send_sems, recv_sems, local_sems = refs[2 * n :]
        x, y, c = _position()
        me, sibling = (x, y, c), (x, y, 1 - c)
        chips = _other_chips(x, y)

        def block(k, px, py, pc):
            hr = shards[k].shape[0] // 2
            return outs[k].at[2 * px + py, pl.ds(pc * hr, hr), :]

        def copy(k, j, blk, to, src=None):
            return pltpu.make_async_remote_copy(
                src_ref=block(k, *blk) if src is None else src, dst_ref=block(k, *blk),
                send_sem=send_sems.at[7 * k + j], recv_sem=recv_sems.at[7 * k + j], device_id=to, device_id_type=MESH)

        started = []
        local = []
        for k in range(n):
            hr = shards[k].shape[0] // 2
            mine = ins[k].at[pl.ds(c * hr, hr), :]
            lc = pltpu.make_async_copy(mine, block(k, *me), local_sems.at[k])
            lc.start()
            local.append(lc)
            first = [copy(k, 0, me, sibling, src=mine)] + [copy(k, 1 + j, me, (*chip, c), src=mine) for j, chip in enumerate(chips)]
            for cp in first:
                cp.start()
            started += first
        for k in range(n):
            for j, chip in enumerate(chips):
                copy(k, 1 + j, (*chip, c), me).wait_recv()
                passed = copy(k, 4 + j, (*chip, c), sibling)
                passed.start()
                started.append(passed)
        for k in range(n):
            copy(k, 0, sibling, me).wait_recv()
            for j, chip in enumerate(chips):
                copy(k, 4 + j, (*chip, 1 - c), me).wait_recv()
        for cp in started:
            cp.wait_send()
        for lc in local:
            lc.wait()

    return pl.pallas_call(
        body,
        name="allgather_weights",
        in_specs=[ANY] * n,
        out_specs=[ANY] * n,
        out_shape=[jax.ShapeDtypeStruct((N_SHARDS,) + s.shape, s.dtype) for s in shards],
        scratch_shapes=[pltpu.SemaphoreType.DMA((7 * n,)), pltpu.SemaphoreType.DMA((7 * n,)), pltpu.SemaphoreType.DMA((n,))],
        compiler_params=pltpu.CompilerParams(has_side_effects=True),
    )(*shards)


def _exchange_halves(grads4):
    n = len(grads4)

    def body(*refs):
        ins, outs = refs[:n], refs[n : 2 * n]
        send_sems, recv_sems = refs[2 * n :]
        x, y, c = _position()
        copies = []
        for k in range(n):
            hr = grads4[k].shape[1] // 2
            cp = pltpu.make_async_remote_copy(
                src_ref=ins[k].at[:, pl.ds((1 - c) * hr, hr), :], dst_ref=outs[k],
                send_sem=send_sems.at[k], recv_sem=recv_sems.at[k], device_id=(x, y, 1 - c), device_id_type=MESH)
            cp.start()
            copies.append(cp)
        for cp in copies:
            cp.wait()

    return pl.pallas_call(
        body,
        name="exchange_halves",
        in_specs=[ANY] * n,
        out_specs=[ANY] * n,
        out_shape=[jax.ShapeDtypeStruct((N_SHARDS, g.shape[1] // 2, g.shape[2]), g.dtype) for g in grads4],
        scratch_shapes=[pltpu.SemaphoreType.DMA((n,)), pltpu.SemaphoreType.DMA((n,))],
        compiler_params=pltpu.CompilerParams(has_side_effects=True),
    )(*grads4)


def _scatter_partials(parts4):
    n = len(parts4)

    def body(*refs):
        ins, outs = refs[:n], refs[n : 2 * n]
        send_sems, recv_sems = refs[2 * n :]
        x, y, c = _position()
        copies = []
        for k in range(n):
            for j, (px, py) in enumerate(_other_chips(x, y)):
                cp = pltpu.make_async_remote_copy(
                    src_ref=ins[k].at[2 * px + py], dst_ref=outs[k].at[j],
                    send_sem=send_sems.at[3 * k + j], recv_sem=recv_sems.at[3 * k + j], device_id=(px, py, c), device_id_type=MESH)
                cp.start()
                copies.append(cp)
        for cp in copies:
            cp.wait()

    return pl.pallas_call(
        body,
        name="scatter_partials",
        in_specs=[ANY] * n,
        out_specs=[ANY] * n,
        out_shape=[jax.ShapeDtypeStruct((3,) + g.shape[1:], g.dtype) for g in parts4],
        scratch_shapes=[pltpu.SemaphoreType.DMA((3 * n,)), pltpu.SemaphoreType.DMA((3 * n,))],
        compiler_params=pltpu.CompilerParams(has_side_effects=True),
    )(*parts4)


def _join_halves(bufs):
    n = len(bufs)

    def body(*refs):
        outs = refs[n : 2 * n]
        send_sems, recv_sems = refs[2 * n :]
        x, y, c = _position()
        for k in range(n):
            hr = bufs[k].shape[0] // 2
            mine = outs[k].at[pl.ds(c * hr, hr), :]
            pltpu.make_async_remote_copy(
                src_ref=mine, dst_ref=mine, send_sem=send_sems.at[k], recv_sem=recv_sems.at[k],
                device_id=(x, y, 1 - c), device_id_type=MESH).start()
        for k in range(n):
            hr = bufs[k].shape[0] // 2
            mine = outs[k].at[pl.ds(c * hr, hr), :]
            theirs = outs[k].at[pl.ds((1 - c) * hr, hr), :]
            wait = pltpu.make_async_remote_copy(
                src_ref=mine, dst_ref=theirs, send_sem=send_sems.at[k], recv_sem=recv_sems.at[k],
                device_id=(x, y, 1 - c), device_id_type=MESH)
            wait.wait_send()
            wait.wait_recv()

    return pl.pallas_call(
        body,
        name="join_halves",
        in_specs=[ANY] * n,
        out_specs=[ANY] * n,
        out_shape=[jax.ShapeDtypeStruct(b.shape, b.dtype) for b in bufs],
        input_output_aliases={k: k for k in range(n)},
        scratch_shapes=[pltpu.SemaphoreType.DMA((n,)), pltpu.SemaphoreType.DMA((n,))],
        compiler_params=pltpu.CompilerParams(has_side_effects=True),
    )(*bufs)


def _allgather_small(block):
    m_per, ncol = block.shape

    def body(x_ref, out_ref, send_sems, recv_sems, local_sem):
        x, y, c = _position()
        me, sibling = (x, y, c), (x, y, 1 - c)
        chips = _other_chips(x, y)

        def rows(px, py, pc):
            return out_ref.at[4 * px + 2 * py + pc]

        def copy(k, blk, to, src=None):
            return pltpu.make_async_remote_copy(
                src_ref=rows(*blk) if src is None else src, dst_ref=rows(*blk),
                send_sem=send_sems.at[k], recv_sem=recv_sems.at[k], device_id=to, device_id_type=MESH)

        mine = pltpu.make_async_copy(x_ref, rows(*me), local_sem)
        mine.start()
        first = [copy(0, me, sibling, src=x_ref)] + [copy(1 + j, me, (*chip, c), src=x_ref) for j, chip in enumerate(chips)]
        for cp in first:
            cp.start()
        passed = [copy(4 + j, (*chip, c), sibling) for j, chip in enumerate(chips)]
        for j, chip in enumerate(chips):
            copy(1 + j, (*chip, c), me).wait_recv()
            passed[j].start()
        copy(0, sibling, me).wait_recv()
        for j, chip in enumerate(chips):
            copy(4 + j, (*chip, 1 - c), me).wait_recv()
        for cp in first + passed:
            cp.wait_send()
        mine.wait()

    return pl.pallas_call(
        body,
        name="allgather_small",
        in_specs=[pl.BlockSpec(memory_space=pltpu.VMEM)],
        out_specs=pl.BlockSpec(memory_space=pltpu.VMEM),
        out_shape=jax.ShapeDtypeStruct((8, m_per, ncol), block.dtype),
        scratch_shapes=[pltpu.SemaphoreType.DMA((7,)), pltpu.SemaphoreType.DMA((7,)), pltpu.SemaphoreType.DMA],
        compiler_params=pltpu.CompilerParams(has_side_effects=True, vmem_limit_bytes=32 * MIB),
    )(block)


SMALL_NAMES = ["norm1_g", "b_decay_f", "b_decay_b", "gla_norm_g", "gmlp_ln_g", "gmlp_ln_b", "w_spatial", "b_spatial", "norm2_g", "final_norm_g"]


def _pack_small(parts, decay_parts):
    flat = jnp.concatenate([a.reshape(-1) for a in parts])
    flat = jnp.pad(flat, (0, SMALL_ROWS * LANES - flat.shape[0])).reshape(SMALL_ROWS, LANES)
    return jnp.concatenate([flat] + [d.reshape(-1, LANES) for d in decay_parts], axis=0)


def _unpack_small(packed, like):
    out, off = [], 0
    flat = packed[:SMALL_ROWS].reshape(-1)
    for a in like:
        out.append(flat[off : off + a.size].reshape(a.shape))
        off += a.size
    return out


def kernel(x, norm1_g, w_in, w_decay_f, b_decay_f, w_decay_b, b_decay_b, gla_norm_g, gmlp_ln_g, gmlp_ln_b, w_spatial, b_spatial, w_out, norm2_g, w_gate, w_up, w_down, final_norm_g, loss_target, m_norm1_g, m_w_in, m_w_decay_f, m_b_decay_f, m_w_decay_b, m_b_decay_b, m_gla_norm_g, m_gmlp_ln_g, m_gmlp_ln_b, m_w_spatial, m_b_spatial, m_w_out, m_norm2_g, m_w_gate, m_w_up, m_w_down, m_final_norm_g, v_norm1_g, v_w_in, v_w_decay_f, v_b_decay_f, v_w_decay_b, v_b_decay_b, v_gla_norm_g, v_gmlp_ln_g, v_gmlp_ln_b, v_w_spatial, v_b_spatial, v_w_out, v_norm2_g, v_w_gate, v_w_up, v_w_down, v_final_norm_g):
    args = dict(locals())
    cx, cy, cc = lax.axis_index("x"), lax.axis_index("y"), lax.axis_index("c")
    shard = 2 * cx + cy
    xs = x[0]
    target = loss_target[0]

    big_names = ["w_in", "w_out", "w_gate", "w_up", "w_down"]
    big_shards = {k: args[k][0] for k in big_names}
    gathered = _allgather_weights([_to_bf16(big_shards[k]) for k in big_names])
    w_in4, w_out4, w_gate4, w_up4, w_down4 = gathered
    w_in_full = jnp.transpose(w_in4, (1, 0, 2)).reshape(D_MODEL, PROJ_W)
    w_in_p = jnp.concatenate(
        [w_in_full[:, :1536], w_in_full[:, 1568:], w_in_full[:, 1536:1568], jnp.zeros((D_MODEL, PROJ_WP - PROJ_W), BF16)], axis=1)
    w_out_full = w_out4.reshape(D_MODEL, D_MODEL)

    dec_block = jnp.concatenate([w_decay_f[0].reshape(-1, LANES), w_decay_b[0].reshape(-1, LANES)], axis=0)
    dec_all = _allgather_small(dec_block)
    dec_all = dec_all[::2].reshape(N_SHARDS, 2, LOWRANK, KEY_W // N_SHARDS)
    wdf_full = jnp.transpose(dec_all[:, 0], (1, 0, 2)).reshape(LOWRANK, KEY_W)
    wdb_full = jnp.transpose(dec_all[:, 1], (1, 0, 2)).reshape(LOWRANK, KEY_W)
    wd_pad_f = jnp.zeros((LANES, KEY_W), F32).at[0:LOWRANK].set(wdf_full).astype(BF16)
    wd_pad_b = jnp.zeros((LANES, KEY_W), F32).at[LOWRANK : 2 * LOWRANK].set(wdb_full).astype(BF16)

    ws_bf = w_spatial[0].astype(BF16)
    wst_bf = jnp.transpose(w_spatial[0], (0, 2, 1)).astype(BF16)
    bs_col = b_spatial[0].reshape(GMLP_GROUPS, GMLP_CHUNK, 1)

    p = _inproj(xs, norm1_g, w_in_p)
    o_f, st_f = _gla_fwd(p, wd_pad_f, b_decay_f, reverse=False)
    o_b, st_b = _gla_fwd(p, wd_pad_b, b_decay_b, reverse=True)
    x1, ycat = _mixer_out(xs, o_f, o_b, p, gla_norm_g, gmlp_ln_g, gmlp_ln_b, ws_bf, bs_col, w_out_full)
    gf = final_norm_g.reshape(1, D_MODEL)
    h2, gate4, up4, act4, dx2, loss_acc, dgf = _ffn_fwd(x1, target, norm2_g, gf, w_gate4, w_up4, w_down4)

    dgate4, dup4, dx1, dg2 = _ffn_bwd(dx2, gate4, up4, x1, norm2_g, w_gate4, w_up4, w_down4)
    dwg4, dwu4, dwd4 = _ffn_wgrad(h2, dgate4, dup4, act4, dx2)
    do, dg, du, dvv, dwo, dgn, dlng, dlnb, dws, dbs = _mixer_bwd(
        dx1, ycat, o_f, o_b, p, gla_norm_g, gmlp_ln_g, gmlp_ln_b, ws_bf, wst_bf, bs_col, w_out_full)
    dq_f, dk_f, dv_f, dlr_f, dwdec_f, dbdec_f = _gla_bwd(p, do, st_f, wd_pad_f, b_decay_f, reverse=False)
    dq_b, dk_b, dv_b, dlr_b, dwdec_b, dbdec_b = _gla_bwd(p, do, st_b, wd_pad_b, b_decay_b, reverse=True)
    dx, dwin_p, dg1 = _inproj_bwd(xs, dx1, norm1_g, w_in_p, dq_f, dq_b, dk_f, dk_b, dv_f, dv_b, dg, du, dvv, dlr_f, dlr_b)

    dwin = jnp.concatenate([dwin_p[:, :1536], dwin_p[:, 2560:2592], dwin_p[:, 1536:2560]], axis=1)
    dwin4 = jnp.transpose(dwin.reshape(D_MODEL, N_SHARDS, PROJ_W // N_SHARDS), (1, 0, 2))
    dwo4 = dwo.reshape(N_SHARDS, D_MODEL // N_SHARDS, D_MODEL)
    grads4 = [dwin4, dwo4, dwg4, dwu4, dwd4]
    c_arr = cc.reshape(1).astype(jnp.int32)
    sc_arr = jnp.stack([shard, cc]).astype(jnp.int32)
    recv_a = _exchange_halves(grads4)
    parts = [_add_halves(g, r, c_arr) for g, r in zip(grads4, recv_a)]
    recv_b = _scatter_partials([pb for _, pb in parts])
    bufs = [_add_partials(pf, r, sc_arr) for (pf, _), r in zip(parts, recv_b)]
    big_grads = dict(zip(big_names, _join_halves(bufs)))

    dwdec_f16 = dwdec_f[0:LOWRANK]
    dwdec_b16 = dwdec_b[LOWRANK : 2 * LOWRANK]
    shard_major = lambda a: jnp.transpose(a.reshape(LOWRANK, N_SHARDS, KEY_W // N_SHARDS), (1, 0, 2))
    small_grads = {
        "norm1_g": dg1, "b_decay_f": dbdec_f, "b_decay_b": dbdec_b, "gla_norm_g": dgn, "gmlp_ln_g": dlng, "gmlp_ln_b": dlnb,
        "w_spatial": dws, "b_spatial": dbs, "norm2_g": dg2, "final_norm_g": dgf,
    }
    g_pack = _pack_small([small_grads[k] for k in SMALL_NAMES], [shard_major(dwdec_f16), shard_major(dwdec_b16)])
    g_all = _allgather_small(g_pack)
    pack_own = lambda pre: _pack_small([args[pre + k] for k in SMALL_NAMES], [args[pre + "w_decay_f"], args[pre + "w_decay_b"]])
    sg, sd, sm, sv = _adamw_small(g_all, pack_own(""), pack_own("m_"), pack_own("v_"))

    names = ["norm1_g", "w_in", "w_decay_f", "b_decay_f", "w_decay_b", "b_decay_b", "gla_norm_g", "gmlp_ln_g", "gmlp_ln_b",
             "w_spatial", "b_spatial", "w_out", "norm2_g", "w_gate", "w_up", "w_down", "final_norm_g"]
    like = [args[k] for k in SMALL_NAMES]
    results = {"g": {}, "d": {}, "m": {}, "v": {}}
    for tag, packed in (("g", sg), ("d", sd), ("m", sm), ("v", sv)):
        for k, a in zip(SMALL_NAMES, _unpack_small(packed, like)):
            results[tag][k] = a
        results[tag]["w_decay_f"] = packed[SMALL_ROWS : SMALL_ROWS + DECAY_ROWS].reshape(w_decay_f.shape)
        results[tag]["w_decay_b"] = packed[SMALL_ROWS + DECAY_ROWS :].reshape(w_decay_b.shape)
    for k in big_names:
        g = big_grads[k]
        d, mo, vo = _adamw(big_shards[k], g, args["m_" + k][0], args["v_" + k][0])
        for tag, a in (("g", g), ("d", d), ("m", mo), ("v", vo)):
            results[tag][k] = a.reshape(args[k].shape)

    loss = lax.psum(loss_acc[0, 0], ("x", "y", "c"))
    grad_x = dx.reshape(x.shape)
    return (loss, grad_x, *[results["g"][k] for k in names], *[results["d"][k] for k in names],
            *[results["m"][k] for k in names], *[results["v"][k] for k in names])
```

```python
import functools
import math

import jax
import jax.numpy as jnp
from jax import lax
from jax.experimental import pallas as pl
from jax.experimental.pallas import tpu as pltpu

F32, BF16 = jnp.float32, jnp.bfloat16

D_MODEL = 1024
GLA_HEADS = 4
GLA_DK = 64
GLA_DV = 128
KEY_W = GLA_HEADS * GLA_DK
GLA_W = GLA_HEADS * GLA_DV
GMLP_W = 512
GMLP_GROUPS = 4
GMLP_CHUNK = 128
LOWRANK = 16
GLA_CHUNK = 64
GLA_TAU = 16.0
PROJ_W = 2592
PROJ_WP = 2688
D_FF = 2816
N_SHARDS = 4
FF_SHARD = D_FF // N_SHARDS
EPS = 1e-6
LANES = 128
TOKEN_SHAPE = (8, LANES)
MIB = 1024 * 1024

ADAM_LR = 0.001
ADAM_B1 = 0.9
ADAM_B2 = 0.999
ADAM_EPS = 1e-08
ADAM_WD = 0.01
ADAM_STEP = 10

COL_Q, COL_K = 0, 256
COL_V, COL_G, COL_U, COL_VV = 512, 1024, 1536, 2048
COL_LR = 2560

MESH = pl.DeviceIdType.MESH


def _nn(a, b):
    return jnp.dot(a, b, preferred_element_type=F32)


def _nt(a, b):
    return lax.dot_general(a, b, (((1,), (1,)), ((), ())), preferred_element_type=F32)


def _tn(a, b):
    return lax.dot_general(a, b, (((0,), (0,)), ((), ())), preferred_element_type=F32)


def _bnn(a, b):
    return jnp.einsum("nik,nkj->nij", a, b, preferred_element_type=F32)


def _bnt(a, b):
    return jnp.einsum("nik,njk->nij", a, b, preferred_element_type=F32)


def _btn(a, b):
    return jnp.einsum("nki,nkj->nij", a, b, preferred_element_type=F32)


def _resident(shape):
    zeros = (0,) * len(shape)
    return pl.BlockSpec(shape, lambda *_: zeros, pipeline_mode=pl.Buffered(1))


def _params(vmem_mib, semantics=("arbitrary",)):
    return pltpu.CompilerParams(vmem_limit_bytes=vmem_mib * MIB, dimension_semantics=semantics)


def _sigmoid(x):
    return 1.0 / (1.0 + jnp.exp(-x))


def _gelu(x):
    return 0.5 * x * (1.0 + lax.erf(x * (1.0 / math.sqrt(2.0))))


def _gelu_grad(x):
    return 0.5 * (1.0 + lax.erf(x * (1.0 / math.sqrt(2.0)))) + x * jnp.exp(-0.5 * x * x) * (1.0 / math.sqrt(2.0 * math.pi))


def _log_sigmoid(x):
    return jnp.minimum(x, 0.0) - jnp.log(1.0 + jnp.exp(-jnp.abs(x)))


def _rms_bwd(dxh, xh, r):
    return r * (dxh - xh * jnp.mean(dxh * xh, axis=-1, keepdims=True))


def _chunk_cumsum(v, row_in_chunk, reverse):
    rows = v.shape[0]
    for sh in (1, 2, 4, 8, 16, 32):
        if reverse:
            v = v + jnp.where(row_in_chunk + sh < GLA_CHUNK, pltpu.roll(v, rows - sh, axis=0), 0.0)
        else:
            v = v + jnp.where(row_in_chunk >= sh, pltpu.roll(v, sh, axis=0), 0.0)
    return v


def _inproj(x, g1, w_in_p, token):
    seq = x.shape[0]
    tm = min(seq, 512)

    def body(x_ref, g_ref, w_ref, token_ref, p_ref):
        xv = x_ref[...]
        r = lax.rsqrt(jnp.mean(xv * xv, axis=-1, keepdims=True) + EPS)
        h = (xv * r * g_ref[...]).astype(BF16)
        p_ref[...] = _nn(h, w_ref[...])

    return pl.pallas_call(
        body,
        name="inproj",
        grid=(seq // tm,),
        in_specs=[pl.BlockSpec((tm, D_MODEL), lambda i: (i, 0)), _resident((1, D_MODEL)), _resident((D_MODEL, PROJ_WP)), _resident(TOKEN_SHAPE)],
        out_specs=pl.BlockSpec((tm, PROJ_WP), lambda i: (i, 0)),
        out_shape=jax.ShapeDtypeStruct((seq, PROJ_WP), F32),
        compiler_params=_params(48, ("parallel",)),
    )(x, g1, w_in_p, token)


def _gla_tile(seq):
    return min(seq, 512)


def _gla_decay_terms(lr_bf, wd_ref, bd_ref, pair, row_in_chunk, reverse, n):
    cols = pl.ds(pair * LANES, LANES)
    pre = _nn(lr_bf, wd_ref[:, cols]) + bd_ref[:, cols]
    la = _log_sigmoid(pre) * (1.0 / GLA_TAU)
    b = _chunk_cumsum(la, row_in_chunk, reverse)
    b3 = b.reshape(n, GLA_CHUNK, LANES)
    blast = b3[:, 0:1, :] if reverse else b3[:, GLA_CHUNK - 1 : GLA_CHUNK, :]
    return pre, b3, blast


def _gla_fwd(p, wd_pad, bd, reverse):
    seq = p.shape[0]
    tg = _gla_tile(seq)
    nt = seq // tg
    n = tg // GLA_CHUNK
    scale = GLA_DK**-0.5

    def tile(i):
        return nt - 1 - i if reverse else i

    def body(q_ref, k_ref, v_ref, lr_ref, wd_ref, bd_ref, o_ref, st_ref, carry):
        @pl.when(pl.program_id(0) == 0)
        def _():
            carry[...] = jnp.zeros_like(carry)

        lr_bf = lr_ref[...].astype(BF16)
        row_in_chunk = lax.broadcasted_iota(jnp.int32, (tg, LANES), 0) % GLA_CHUNK
        lane_head = lax.broadcasted_iota(jnp.int32, (1, LANES), 1) // GLA_DK
        tt = lax.broadcasted_iota(jnp.int32, (GLA_CHUNK, GLA_CHUNK), 0)
        ss = lax.broadcasted_iota(jnp.int32, (GLA_CHUNK, GLA_CHUNK), 1)
        causal = (tt <= ss) if reverse else (tt >= ss)
        order = range(n - 1, -1, -1) if reverse else range(n)
        for pair in range(2):
            cols = pl.ds(pair * LANES, LANES)
            _, b3, blast = _gla_decay_terms(lr_bf, wd_ref, bd_ref, pair, row_in_chunk, reverse, n)
            q3 = q_ref[:, cols].reshape(n, GLA_CHUNK, LANES) * scale
            k3 = k_ref[:, cols].reshape(n, GLA_CHUNK, LANES)
            qd = q3 * jnp.exp(b3)
            kd = (k3 * jnp.exp(-b3)).astype(BF16)
            kte = k3 * jnp.exp(blast - b3)
            dec = jnp.exp(blast)
            for hh in range(2):
                h = 2 * pair + hh
                m = (lane_head == hh).astype(F32)
                qdh = (qd * m).astype(BF16)
                kteh = (kte * m).astype(BF16)
                vh = v_ref[:, pl.ds(h * GLA_DV, GLA_DV)].reshape(n, GLA_CHUNK, GLA_DV).astype(BF16)
                sc = jnp.where(causal, _bnt(qdh, kd), 0.0)
                o_intra = _bnn(sc.astype(BF16), vh)
                dst = _btn(vh, kteh)
                st = carry[h]
                for j in order:
                    st_ref[j, h] = st
                    st = st * dec[j] + dst[j]
                carry[h] = st
                o_inter = _bnt(qdh, st_ref[:, h].astype(BF16))
                o_ref[:, pl.ds(h * GLA_DV, GLA_DV)] = (o_intra + o_inter).reshape(tg, GLA_DV)

    nchunks = seq // GLA_CHUNK
    return pl.pallas_call(
        body,
        name="gla_fwd_rev" if reverse else "gla_fwd",
        grid=(nt,),
        in_specs=[
            pl.BlockSpec((tg, KEY_W), lambda i: (tile(i), COL_Q // KEY_W)),
            pl.BlockSpec((tg, KEY_W), lambda i: (tile(i), COL_K // KEY_W)),
            pl.BlockSpec((tg, GLA_W), lambda i: (tile(i), COL_V // GLA_W)),
            pl.BlockSpec((tg, LANES), lambda i: (tile(i), COL_LR // LANES)),
            _resident((LANES, KEY_W)),
            _resident((1, KEY_W)),
        ],
        out_specs=[
            pl.BlockSpec((tg, GLA_W), lambda i: (tile(i), 0)),
            pl.BlockSpec((n, GLA_HEADS, GLA_DV, LANES), lambda i: (tile(i), 0, 0, 0)),
        ],
        out_shape=[
            jax.ShapeDtypeStruct((seq, GLA_W), F32),
            jax.ShapeDtypeStruct((nchunks, GLA_HEADS, GLA_DV, LANES), F32),
        ],
        scratch_shapes=[pltpu.VMEM((GLA_HEADS, GLA_DV, LANES), F32)],
        compiler_params=_params(48),
    )(p, p, p, p, wd_pad, bd)


def _mixer_out(x, o_f, o_b, p, gn, lng, lnb, ws_bf, bs_col, w_out):
    seq = x.shape[0]
    tm = min(seq, 512)

    def body(x_ref, of_ref, ob_ref, g_ref, u_ref, vv_ref, gn_ref, lng_ref, lnb_ref, ws_ref, bs_ref, wo_ref, x1_ref, yc_ref, vn_sc):
        for h in range(GLA_HEADS):
            cols = pl.ds(h * GLA_DV, GLA_DV)
            oh = of_ref[:, cols] + ob_ref[:, cols]
            on = oh * lax.rsqrt(jnp.mean(oh * oh, axis=-1, keepdims=True) + EPS)
            gh = g_ref[:, cols]
            yc_ref[:, cols] = (on * gn_ref[:, cols] * (gh * _sigmoid(gh))).astype(BF16)
        zv = _gelu(vv_ref[...])
        xc = zv - jnp.mean(zv, axis=-1, keepdims=True)
        vhat = xc * lax.rsqrt(jnp.mean(xc * xc, axis=-1, keepdims=True) + EPS)
        vn_sc[...] = (vhat * lng_ref[...] + lnb_ref[...]).astype(BF16)
        for c in range(tm // GMLP_CHUNK):
            rows = pl.ds(c * GMLP_CHUNK, GMLP_CHUNK)
            for g in range(GMLP_GROUPS):
                cols = pl.ds(g * LANES, LANES)
                s = _nn(ws_ref[g], vn_sc[rows, cols]) + bs_ref[g]
                yc_ref[rows, pl.ds(GLA_W + g * LANES, LANES)] = (_gelu(u_ref[rows, cols]) * s).astype(BF16)
        x1_ref[...] = x_ref[...] + _nn(yc_ref[...], wo_ref[...])

    row = lambda w: pl.BlockSpec((tm, w), lambda i: (i, 0))
    pcol = lambda col: pl.BlockSpec((tm, GLA_W), lambda i: (i, col // GLA_W))
    return pl.pallas_call(
        body,
        name="mixer_out",
        grid=(seq // tm,),
        in_specs=[
            row(D_MODEL), row(GLA_W), row(GLA_W), pcol(COL_G), pcol(COL_U), pcol(COL_VV),
            _resident((1, GLA_W)), _resident((1, GMLP_W)), _resident((1, GMLP_W)),
            _resident((GMLP_GROUPS, GMLP_CHUNK, GMLP_CHUNK)), _resident((GMLP_GROUPS, GMLP_CHUNK, 1)),
            _resident((D_MODEL, D_MODEL)),
        ],
        out_specs=[row(D_MODEL), row(D_MODEL)],
        out_shape=[jax.ShapeDtypeStruct((seq, D_MODEL), F32), jax.ShapeDtypeStruct((seq, D_MODEL), BF16)],
        scratch_shapes=[pltpu.VMEM((tm, GMLP_W), BF16)],
        compiler_params=_params(48, ("parallel",)),
    )(x, o_f, o_b, p, p, p, gn, lng, lnb, ws_bf, bs_col, w_out)


def _ffn_fwd(x1, target, g2, gf, wg4, wu4, wd4):
    seq = x1.shape[0]
    tm = min(seq, 256)

    def body(x1_ref, t_ref, g2_ref, gf_ref, wg_ref, wu_ref, wd_ref, h2_ref, gate_ref, up_ref, act_ref, dx2_ref, loss_ref, dgf_ref):
        @pl.when(pl.program_id(0) == 0)
        def _():
            loss_ref[...] = jnp.zeros_like(loss_ref)
            dgf_ref[...] = jnp.zeros_like(dgf_ref)

        x1v = x1_ref[...]
        h2 = (x1v * lax.rsqrt(jnp.mean(x1v * x1v, axis=-1, keepdims=True) + EPS) * g2_ref[...]).astype(BF16)
        h2_ref[...] = h2
        acc = jnp.zeros((tm, D_MODEL), F32)
        for s in range(N_SHARDS):
            gate = _nn(h2, wg_ref[s])
            up = _nn(h2, wu_ref[s])
            act = (gate * _sigmoid(gate) * up).astype(BF16)
            gate_ref[s] = gate
            up_ref[s] = up
            act_ref[s] = act
            acc = acc + _nn(act, wd_ref[s])
        x2 = x1v + acc
        rf = lax.rsqrt(jnp.mean(x2 * x2, axis=-1, keepdims=True) + EPS)
        xh = x2 * rf
        err = xh * gf_ref[...] - t_ref[...]
        loss_ref[...] += 0.5 * jnp.sum(jnp.mean(err * err, axis=-1, keepdims=True))
        dy = err * (1.0 / D_MODEL)
        dgf_ref[...] += jnp.sum(dy * xh, axis=0, keepdims=True)
        dx2_ref[...] = _rms_bwd(dy * gf_ref[...], xh, rf)

    row = lambda w: pl.BlockSpec((tm, w), lambda i: (i, 0))
    ff = pl.BlockSpec((N_SHARDS, tm, FF_SHARD), lambda i: (0, i, 0))
    return pl.pallas_call(
        body,
        name="ffn_fwd",
        grid=(seq // tm,),
        in_specs=[
            row(D_MODEL), row(D_MODEL), _resident((1, D_MODEL)), _resident((1, D_MODEL)),
            _resident((N_SHARDS, D_MODEL, FF_SHARD)), _resident((N_SHARDS, D_MODEL, FF_SHARD)), _resident((N_SHARDS, FF_SHARD, D_MODEL)),
        ],
        out_specs=[row(D_MODEL), ff, ff, ff, row(D_MODEL), pl.BlockSpec((1, LANES), lambda i: (0, 0)), pl.BlockSpec((1, D_MODEL), lambda i: (0, 0))],
        out_shape=[
            jax.ShapeDtypeStruct((seq, D_MODEL), BF16),
            jax.ShapeDtypeStruct((N_SHARDS, seq, FF_SHARD), F32),
            jax.ShapeDtypeStruct((N_SHARDS, seq, FF_SHARD), F32),
            jax.ShapeDtypeStruct((N_SHARDS, seq, FF_SHARD), BF16),
            jax.ShapeDtypeStruct((seq, D_MODEL), F32),
            jax.ShapeDtypeStruct((1, LANES), F32),
            jax.ShapeDtypeStruct((1, D_MODEL), F32),
        ],
        compiler_params=_params(56),
    )(x1, target, g2, gf, wg4, wu4, wd4)


def _ffn_bwd(dx2, gate4, up4, x1, g2, wg4, wu4, wd4):
    seq = x1.shape[0]
    tm = min(seq, 256)

    def body(dx2_ref, gate_ref, up_ref, x1_ref, g2_ref, wg_ref, wu_ref, wd_ref, dgate_ref, dup_ref, dx1_ref, dg2_ref):
        @pl.when(pl.program_id(0) == 0)
        def _():
            dg2_ref[...] = jnp.zeros_like(dg2_ref)

        dx2v = dx2_ref[...]
        dx2b = dx2v.astype(BF16)
        dh2 = jnp.zeros((tm, D_MODEL), F32)
        for s in range(N_SHARDS):
            dact = _nt(dx2b, wd_ref[s])
            gate = gate_ref[s]
            sg = _sigmoid(gate)
            dgate = (dact * up_ref[s] * (sg * (1.0 + gate * (1.0 - sg)))).astype(BF16)
            dup = (dact * (gate * sg)).astype(BF16)
            dgate_ref[s] = dgate
            dup_ref[s] = dup
            dh2 = dh2 + _nt(dgate, wg_ref[s]) + _nt(dup, wu_ref[s])
        x1v = x1_ref[...]
        r2 = lax.rsqrt(jnp.mean(x1v * x1v, axis=-1, keepdims=True) + EPS)
        xh = x1v * r2
        dg2_ref[...] += jnp.sum(dh2 * xh, axis=0, keepdims=True)
        dx1_ref[...] = dx2v + _rms_bwd(dh2 * g2_ref[...], xh, r2)

    row = lambda w: pl.BlockSpec((tm, w), lambda i: (i, 0))
    ff = pl.BlockSpec((N_SHARDS, tm, FF_SHARD), lambda i: (0, i, 0))
    return pl.pallas_call(
        body,
        name="ffn_bwd",
        grid=(seq // tm,),
        in_specs=[
            row(D_MODEL), ff, ff, row(D_MODEL), _resident((1, D_MODEL)),
            _resident((N_SHARDS, D_MODEL, FF_SHARD)), _resident((N_SHARDS, D_MODEL, FF_SHARD)), _resident((N_SHARDS, FF_SHARD, D_MODEL)),
        ],
        out_specs=[ff, ff, row(D_MODEL), pl.BlockSpec((1, D_MODEL), lambda i: (0, 0))],
        out_shape=[
            jax.ShapeDtypeStruct((N_SHARDS, seq, FF_SHARD), BF16),
            jax.ShapeDtypeStruct((N_SHARDS, seq, FF_SHARD), BF16),
            jax.ShapeDtypeStruct((seq, D_MODEL), F32),
            jax.ShapeDtypeStruct((1, D_MODEL), F32),
        ],
        compiler_params=_params(56),
    )(dx2, gate4, up4, x1, g2, wg4, wu4, wd4)


def _ffn_wgrad(h2, dgate4, dup4, act4, dx2):
    seq = h2.shape[0]
    tm = min(seq, 512)

    def body(h2_ref, dgate_ref, dup_ref, act_ref, dx2_ref, dwg_ref, dwu_ref, dwd_ref):
        @pl.when(pl.program_id(1) == 0)
        def _():
            dwg_ref[...] = jnp.zeros_like(dwg_ref)
            dwu_ref[...] = jnp.zeros_like(dwu_ref)
            dwd_ref[...] = jnp.zeros_like(dwd_ref)

        h2v = h2_ref[...]
        dwg_ref[...] += _tn(h2v, dgate_ref[...])
        dwu_ref[...] += _tn(h2v, dup_ref[...])
        dwd_ref[...] += _tn(act_ref[...], dx2_ref[...].astype(BF16))

    ff = pl.BlockSpec((None, tm, FF_SHARD), lambda s, i: (s, i, 0))
    row = pl.BlockSpec((tm, D_MODEL), lambda s, i: (i, 0))
    return pl.pallas_call(
        body,
        name="ffn_wgrad",
        grid=(N_SHARDS, seq // tm),
        in_specs=[row, ff, ff, ff, row],
        out_specs=[
            pl.BlockSpec((None, D_MODEL, FF_SHARD), lambda s, i: (s, 0, 0)),
            pl.BlockSpec((None, D_MODEL, FF_SHARD), lambda s, i: (s, 0, 0)),
            pl.BlockSpec((None, FF_SHARD, D_MODEL), lambda s, i: (s, 0, 0)),
        ],
        out_shape=[
            jax.ShapeDtypeStruct((N_SHARDS, D_MODEL, FF_SHARD), F32),
            jax.ShapeDtypeStruct((N_SHARDS, D_MODEL, FF_SHARD), F32),
            jax.ShapeDtypeStruct((N_SHARDS, FF_SHARD, D_MODEL), F32),
        ],
        compiler_params=_params(48, ("parallel", "arbitrary")),
    )(h2, dgate4, dup4, act4, dx2)


def _mixer_bwd(dx1, ycat, o_f, o_b, p, gn, lng, lnb, ws_bf, wst_bf, bs_col, w_out, token):
    seq = dx1.shape[0]
    tm = min(seq, 512)
    nsteps = seq // tm

    def body(dx1_ref, yc_ref, of_ref, ob_ref, g_ref, u_ref, vv_ref, gn_ref, lng_ref, lnb_ref, ws_ref, wst_ref, bs_ref, wo_ref, token_ref,
             do_ref, dg_ref, du_ref, dvv_ref, dwo_ref, dgn_ref, dlng_ref, dlnb_ref, dws_ref, dbs_ref, vn_sc, dvn_sc, dbs_acc):
        step = pl.program_id(0)

        @pl.when(step == 0)
        def _():
            for r in (dwo_ref, dgn_ref, dlng_ref, dlnb_ref, dws_ref, dbs_acc):
                r[...] = jnp.zeros_like(r)

        dx1b = dx1_ref[...].astype(BF16)
        dyc = _nt(dx1b, wo_ref[...])
        dwo_ref[...] += _tn(yc_ref[...], dx1b)
        for h in range(GLA_HEADS):
            cols = pl.ds(h * GLA_DV, GLA_DV)
            dya = dyc[:, h * GLA_DV : (h + 1) * GLA_DV]
            oh = of_ref[:, cols] + ob_ref[:, cols]
            rn = lax.rsqrt(jnp.mean(oh * oh, axis=-1, keepdims=True) + EPS)
            on = oh * rn
            gh = g_ref[:, cols]
            sg = _sigmoid(gh)
            sil = gh * sg
            gnh = gn_ref[:, cols]
            dgn_ref[:, cols] += jnp.sum(dya * on * sil, axis=0, keepdims=True)
            dg_ref[:, cols] = dya * on * gnh * (sg * (1.0 + gh * (1.0 - sg)))
            do_ref[:, cols] = _rms_bwd(dya * gnh * sil, on, rn)
        vv = vv_ref[...]
        zv = _gelu(vv)
        xc = zv - jnp.mean(zv, axis=-1, keepdims=True)
        rstd = lax.rsqrt(jnp.mean(xc * xc, axis=-1, keepdims=True) + EPS)
        vhat = xc * rstd
        vn_sc[...] = (vhat * lng_ref[...] + lnb_ref[...]).astype(BF16)
        for c in range(tm // GMLP_CHUNK):
            rows = pl.ds(c * GMLP_CHUNK, GMLP_CHUNK)
            for g in range(GMLP_GROUPS):
                cols = pl.ds(g * LANES, LANES)
                vn = vn_sc[rows, cols]
                s = _nn(ws_ref[g], vn) + bs_ref[g]
                dyb = dyc[c * GMLP_CHUNK : (c + 1) * GMLP_CHUNK, GLA_W + g * LANES : GLA_W + (g + 1) * LANES]
                u = u_ref[rows, cols]
                du_ref[rows, cols] = dyb * s * _gelu_grad(u)
                ds = dyb * _gelu(u)
                dbs_acc[g] += ds
                dsb = ds.astype(BF16)
                dws_ref[g] += _nt(dsb, vn)
                dvn_sc[rows, cols] = _nn(wst_ref[g], dsb)
        dvn = dvn_sc[...]
        dlng_ref[...] += jnp.sum(dvn * vhat, axis=0, keepdims=True)
        dlnb_ref[...] += jnp.sum(dvn, axis=0, keepdims=True)
        dvh = dvn * lng_ref[...]
        dzv = rstd * (dvh - jnp.mean(dvh, axis=-1, keepdims=True) - vhat * jnp.mean(dvh * vhat, axis=-1, keepdims=True))
        dvv_ref[...] = dzv * _gelu_grad(vv)

        @pl.when(step == nsteps - 1)
        def _():
            dbs_ref[...] = jnp.sum(dbs_acc[...], axis=-1, keepdims=True)

    row = lambda w: pl.BlockSpec((tm, w), lambda i: (i, 0))
    pcol = lambda col: pl.BlockSpec((tm, GLA_W), lambda i: (i, col // GLA_W))
    const = lambda shape: pl.BlockSpec(shape, lambda i: (0,) * len(shape))
    return pl.pallas_call(
        body,
        name="mixer_bwd",
        grid=(nsteps,),
        in_specs=[
            row(D_MODEL), row(D_MODEL), row(GLA_W), row(GLA_W), pcol(COL_G), pcol(COL_U), pcol(COL_VV),
            _resident((1, GLA_W)), _resident((1, GMLP_W)), _resident((1, GMLP_W)),
            _resident((GMLP_GROUPS, GMLP_CHUNK, GMLP_CHUNK)), _resident((GMLP_GROUPS, GMLP_CHUNK, GMLP_CHUNK)),
            _resident((GMLP_GROUPS, GMLP_CHUNK, 1)), _resident((D_MODEL, D_MODEL)), _resident(TOKEN_SHAPE),
        ],
        out_specs=[
            row(GLA_W), row(GLA_W), row(GMLP_W), row(GMLP_W), const((D_MODEL, D_MODEL)),
            const((1, GLA_W)), const((1, GMLP_W)), const((1, GMLP_W)),
            const((GMLP_GROUPS, GMLP_CHUNK, GMLP_CHUNK)), const((GMLP_GROUPS, GMLP_CHUNK, 1)),
        ],
        out_shape=[
            jax.ShapeDtypeStruct((seq, GLA_W), F32), jax.ShapeDtypeStruct((seq, GLA_W), F32),
            jax.ShapeDtypeStruct((seq, GMLP_W), F32), jax.ShapeDtypeStruct((seq, GMLP_W), F32),
            jax.ShapeDtypeStruct((D_MODEL, D_MODEL), F32),
            jax.ShapeDtypeStruct((1, GLA_W), F32), jax.ShapeDtypeStruct((1, GMLP_W), F32), jax.ShapeDtypeStruct((1, GMLP_W), F32),
            jax.ShapeDtypeStruct((GMLP_GROUPS, GMLP_CHUNK, GMLP_CHUNK), F32), jax.ShapeDtypeStruct((GMLP_GROUPS, GMLP_CHUNK, 1), F32),
        ],
        scratch_shapes=[pltpu.VMEM((tm, GMLP_W), BF16), pltpu.VMEM((tm, GMLP_W), F32), pltpu.VMEM((GMLP_GROUPS, GMLP_CHUNK, GMLP_CHUNK), F32)],
        compiler_params=_params(56),
    )(dx1, ycat, o_f, o_b, p, p, p, gn, lng, lnb, ws_bf, wst_bf, bs_col, w_out, token)


def _gla_bwd(p, do, st, wd_pad, bd, reverse):
    seq = p.shape[0]
    tg = _gla_tile(seq)
    nt = seq // tg
    n = tg // GLA_CHUNK
    scale = GLA_DK**-0.5

    def tile(i):
        return i if reverse else nt - 1 - i

    def body(q_ref, k_ref, v_ref, lr_ref, do_ref, st_ref, wd_ref, bd_ref, dq_ref, dk_ref, dv_ref, dlr_ref, dwd_ref, dbd_ref, carry, dsa):
        @pl.when(pl.program_id(0) == 0)
        def _():
            carry[...] = jnp.zeros_like(carry)
            dwd_ref[...] = jnp.zeros_like(dwd_ref)
            dbd_ref[...] = jnp.zeros_like(dbd_ref)

        lr_bf = lr_ref[...].astype(BF16)
        row_in_chunk = lax.broadcasted_iota(jnp.int32, (tg, LANES), 0) % GLA_CHUNK
        lane_head = lax.broadcasted_iota(jnp.int32, (1, LANES), 1) // GLA_DK
        tt = lax.broadcasted_iota(jnp.int32, (GLA_CHUNK, GLA_CHUNK), 0)
        ss = lax.broadcasted_iota(jnp.int32, (GLA_CHUNK, GLA_CHUNK), 1)
        causal = (tt <= ss) if reverse else (tt >= ss)
        causal_t = (tt >= ss) if reverse else (tt <= ss)
        order = range(n) if reverse else range(n - 1, -1, -1)
        dlr = jnp.zeros((tg, LANES), F32)
        for pair in range(2):
            cols = pl.ds(pair * LANES, LANES)
            pre, b3, blast = _gla_decay_terms(lr_bf, wd_ref, bd_ref, pair, row_in_chunk, reverse, n)
            q3 = q_ref[:, cols].reshape(n, GLA_CHUNK, LANES) * scale
            k3 = k_ref[:, cols].reshape(n, GLA_CHUNK, LANES)
            eb = jnp.exp(b3)
            emb = jnp.exp(-b3)
            ekte = jnp.exp(blast - b3)
            qd = q3 * eb
            kdf = k3 * emb
            kd = kdf.astype(BF16)
            kte = k3 * ekte
            dec = jnp.exp(blast)
            dqd = jnp.zeros((n, GLA_CHUNK, LANES), F32)
            dkd = jnp.zeros((n, GLA_CHUNK, LANES), F32)
            dkte = jnp.zeros((n, GLA_CHUNK, LANES), F32)
            ddec = jnp.zeros((n, 1, LANES), F32)
            for hh in range(2):
                h = 2 * pair + hh
                vcols = pl.ds(h * GLA_DV, GLA_DV)
                m = (lane_head == hh).astype(F32)
                qdh = (qd * m).astype(BF16)
                kteh = (kte * m).astype(BF16)
                vh = v_ref[:, vcols].reshape(n, GLA_CHUNK, GLA_DV).astype(BF16)
                doh = do_ref[:, vcols].reshape(n, GLA_CHUNK, GLA_DV).astype(BF16)
                stb = st_ref[:, h]
                stb_bf = stb.astype(BF16)
                sc_t = jnp.where(causal_t, _bnt(kd, qdh), 0.0).astype(BF16)
                dp = jnp.where(causal, _bnt(doh, vh), 0.0).astype(BF16)
                dp_t = jnp.where(causal_t, _bnt(vh, doh), 0.0).astype(BF16)
                a = _btn(doh, qdh)
                c = carry[h]
                for j in order:
                    dsa[j] = c
                    c = a[j] + dec[j] * c
                carry[h] = c
                dsa_f = dsa[...]
                dsa_bf = dsa_f.astype(BF16)
                dqd = dqd + (_bnn(dp, kd) * m + _bnn(doh, stb_bf))
                dkd = dkd + _bnn(dp_t, qdh)
                dkte = dkte + _bnn(vh, dsa_bf)
                ddec = ddec + jnp.sum(dsa_f * stb, axis=1, keepdims=True)
                dv_ref[:, vcols] = (_bnn(sc_t, doh) + _bnt(kteh, dsa_bf)).reshape(tg, GLA_DV)
            dq_ref[:, cols] = (dqd * (scale * eb)).reshape(tg, LANES)
            dk_ref[:, cols] = (dkd * emb + dkte * ekte).reshape(tg, LANES)
            db = dqd * qd - dkd * kdf - dkte * kte
            dblast = jnp.sum(dkte * kte, axis=1, keepdims=True) + ddec * dec
            dla = _chunk_cumsum(db.reshape(tg, LANES), row_in_chunk, not reverse) + jnp.broadcast_to(dblast, (n, GLA_CHUNK, LANES)).reshape(tg, LANES)
            dpre = (dla * (1.0 / GLA_TAU) * _sigmoid(-pre))
            dpre_bf = dpre.astype(BF16)
            dlr = dlr + _nt(dpre_bf, wd_ref[:, cols])
            dwd_ref[:, cols] += _tn(lr_bf, dpre_bf)
            dbd_ref[:, cols] += jnp.sum(dpre, axis=0, keepdims=True)
        dlr_ref[...] = dlr

    return pl.pallas_call(
        body,
        name="gla_bwd_rev" if reverse else "gla_bwd",
        grid=(nt,),
        in_specs=[
            pl.BlockSpec((tg, KEY_W), lambda i: (tile(i), COL_Q // KEY_W)),
            pl.BlockSpec((tg, KEY_W), lambda i: (tile(i), COL_K // KEY_W)),
            pl.BlockSpec((tg, GLA_W), lambda i: (tile(i), COL_V // GLA_W)),
            pl.BlockSpec((tg, LANES), lambda i: (tile(i), COL_LR // LANES)),
            pl.BlockSpec((tg, GLA_W), lambda i: (tile(i), 0)),
            pl.BlockSpec((n, GLA_HEADS, GLA_DV, LANES), lambda i: (tile(i), 0, 0, 0)),
            _resident((LANES, KEY_W)),
            _resident((1, KEY_W)),
        ],
        out_specs=[
            pl.BlockSpec((tg, KEY_W), lambda i: (tile(i), 0)),
            pl.BlockSpec((tg, KEY_W), lambda i: (tile(i), 0)),
            pl.BlockSpec((tg, GLA_W), lambda i: (tile(i), 0)),
            pl.BlockSpec((tg, LANES), lambda i: (tile(i), 0)),
            pl.BlockSpec((LANES, KEY_W), lambda i: (0, 0)),
            pl.BlockSpec((1, KEY_W), lambda i: (0, 0)),
        ],
        out_shape=[
            jax.ShapeDtypeStruct((seq, KEY_W), F32), jax.ShapeDtypeStruct((seq, KEY_W), F32),
            jax.ShapeDtypeStruct((seq, GLA_W), F32), jax.ShapeDtypeStruct((seq, LANES), F32),
            jax.ShapeDtypeStruct((LANES, KEY_W), F32), jax.ShapeDtypeStruct((1, KEY_W), F32),
        ],
        scratch_shapes=[pltpu.VMEM((GLA_HEADS, GLA_DV, LANES), F32), pltpu.VMEM((n, GLA_DV, LANES), F32)],
        compiler_params=_params(48),
    )(p, p, p, p, do, st, wd_pad, bd)


def _inproj_bwd(x, dx1, g1, w_in_p, dq_f, dq_b, dk_f, dk_b, dv_f, dv_b, dg, du, dvv, dlr_f, dlr_b):
    seq = x.shape[0]
    tm = min(seq, 256)

    def body(x_ref, dx1_ref, g1_ref, w_ref, dqf, dqb, dkf, dkb, dvf, dvb, dg_ref, du_ref, dvv_ref, dlrf, dlrb, dx_ref, dw_ref, dg1_ref, dp_sc):
        @pl.when(pl.program_id(0) == 0)
        def _():
            dw_ref[...] = jnp.zeros_like(dw_ref)
            dg1_ref[...] = jnp.zeros_like(dg1_ref)

        dp_sc[:, COL_Q : COL_Q + KEY_W] = (dqf[...] + dqb[...]).astype(BF16)
        dp_sc[:, COL_K : COL_K + KEY_W] = (dkf[...] + dkb[...]).astype(BF16)
        dp_sc[:, COL_V : COL_V + GLA_W] = (dvf[...] + dvb[...]).astype(BF16)
        dp_sc[:, COL_G : COL_G + GLA_W] = dg_ref[...].astype(BF16)
        dp_sc[:, COL_U : COL_U + GMLP_W] = du_ref[...].astype(BF16)
        dp_sc[:, COL_VV : COL_VV + GMLP_W] = dvv_ref[...].astype(BF16)
        dp_sc[:, COL_LR : COL_LR + LANES] = (dlrf[...] + dlrb[...]).astype(BF16)
        xv = x_ref[...]
        r1 = lax.rsqrt(jnp.mean(xv * xv, axis=-1, keepdims=True) + EPS)
        xh = xv * r1
        h = (xh * g1_ref[...]).astype(BF16)
        dp = dp_sc[...]
        dw_ref[...] += _tn(h, dp)
        dh = _nt(dp, w_ref[...])
        dg1_ref[...] += jnp.sum(dh * xh, axis=0, keepdims=True)
        dx_ref[...] = dx1_ref[...] + _rms_bwd(dh * g1_ref[...], xh, r1)

    row = lambda w: pl.BlockSpec((tm, w), lambda i: (i, 0))
    return pl.pallas_call(
        body,
        name="inproj_bwd",
        grid=(seq // tm,),
        in_specs=[
            row(D_MODEL), row(D_MODEL), _resident((1, D_MODEL)), _resident((D_MODEL, PROJ_WP)),
            row(KEY_W), row(KEY_W), row(KEY_W), row(KEY_W), row(GLA_W), row(GLA_W),
            row(GLA_W), row(GMLP_W), row(GMLP_W), row(LANES), row(LANES),
        ],
        out_specs=[row(D_MODEL), pl.BlockSpec((D_MODEL, PROJ_WP), lambda i: (0, 0)), pl.BlockSpec((1, D_MODEL), lambda i: (0, 0))],
        out_shape=[
            jax.ShapeDtypeStruct((seq, D_MODEL), F32),
            jax.ShapeDtypeStruct((D_MODEL, PROJ_WP), F32),
            jax.ShapeDtypeStruct((1, D_MODEL), F32),
        ],
        scratch_shapes=[pltpu.VMEM((tm, PROJ_WP), BF16)],
        compiler_params=_params(56),
    )(x, dx1, g1, w_in_p, dq_f, dq_b, dk_f, dk_b, dv_f, dv_b, dg, du, dvv, dlr_f, dlr_b)


def _row_tile(rows):
    for t in (256, 176, 128, 64, 32, 16, 8):
        if rows % t == 0:
            return t
    return rows


def _cast_into_slot(w, shard):
    rows, cols = w.shape
    tr = _row_tile(rows)

    def body(s_ref, w_ref, o_ref):
        o_ref[...] = w_ref[...].astype(BF16)

    return pl.pallas_call(
        body,
        name="cast_into_slot",
        grid_spec=pltpu.PrefetchScalarGridSpec(
            num_scalar_prefetch=1,
            grid=(rows // tr,),
            in_specs=[pl.BlockSpec((tr, cols), lambda i, s_ref: (i, 0))],
            out_specs=pl.BlockSpec((None, tr, cols), lambda i, s_ref: (s_ref[0], i, 0)),
        ),
        out_shape=jax.ShapeDtypeStruct((N_SHARDS, rows, cols), BF16),
        compiler_params=_params(32, ("parallel",)),
    )(shard, w)


def _add_halves(g4, recv, c):
    _, rows, cols = g4.shape
    hr = rows // 2
    tr = _row_tile(hr)
    steps = hr // tr

    def body(c_ref, g_ref, r_ref, o_ref, ob_ref):
        total = g_ref[...] + r_ref[...]
        o_ref[...] = total
        ob_ref[...] = total.astype(BF16)

    out = pl.BlockSpec((None, tr, cols), lambda s, i, c_ref: (s, i, 0))
    return pl.pallas_call(
        body,
        name="add_halves",
        grid_spec=pltpu.PrefetchScalarGridSpec(
            num_scalar_prefetch=1,
            grid=(N_SHARDS, steps),
            in_specs=[pl.BlockSpec((None, tr, cols), lambda s, i, c_ref: (s, c_ref[0] * steps + i, 0)), out],
            out_specs=[out, out],
        ),
        out_shape=[jax.ShapeDtypeStruct((N_SHARDS, hr, cols), F32), jax.ShapeDtypeStruct((N_SHARDS, hr, cols), BF16)],
        compiler_params=_params(32, ("parallel", "parallel")),
    )(c, g4, recv)


def _add_partials(part4, recv3, shard_core):
    _, hr, cols = part4.shape
    tr = _row_tile(hr)
    steps = hr // tr

    def body(sc_ref, p_ref, r_ref, o_ref):
        o_ref[...] = ((p_ref[...] + r_ref[0].astype(F32)) + r_ref[1].astype(F32)) + r_ref[2].astype(F32)

    return pl.pallas_call(
        body,
        name="add_partials",
        grid_spec=pltpu.PrefetchScalarGridSpec(
            num_scalar_prefetch=1,
            grid=(steps,),
            in_specs=[
                pl.BlockSpec((None, tr, cols), lambda i, sc_ref: (sc_ref[0], i, 0)),
                pl.BlockSpec((3, tr, cols), lambda i, sc_ref: (0, i, 0)),
            ],
            out_specs=pl.BlockSpec((tr, cols), lambda i, sc_ref: (sc_ref[1] * steps + i, 0)),
        ),
        out_shape=jax.ShapeDtypeStruct((2 * hr, cols), F32),
        compiler_params=_params(32, ("parallel",)),
    )(shard_core, part4, recv3)


def _adam_math(w, g, m, v):
    m = ADAM_B1 * m + (1.0 - ADAM_B1) * g
    v = ADAM_B2 * v + (1.0 - ADAM_B2) * (g * g)
    m_hat = m / (1.0 - ADAM_B1**ADAM_STEP)
    v_hat = v / (1.0 - ADAM_B2**ADAM_STEP)
    delta = -ADAM_LR * (m_hat / (jnp.sqrt(v_hat) + ADAM_EPS) + ADAM_WD * w)
    return delta, m, v


def _adamw(w, g, m, v):
    rows, cols = w.shape
    tr = _row_tile(rows)

    def body(w_ref, g_ref, m_ref, v_ref, d_ref, mo_ref, vo_ref):
        d_ref[...], mo_ref[...], vo_ref[...] = _adam_math(w_ref[...], g_ref[...], m_ref[...], v_ref[...])

    spec = pl.BlockSpec((tr, cols), lambda i: (i, 0))
    shape = jax.ShapeDtypeStruct(w.shape, F32)
    return pl.pallas_call(
        body, name="adamw", grid=(rows // tr,), in_specs=[spec] * 4, out_specs=[spec] * 3, out_shape=[shape] * 3,
        compiler_params=_params(32, ("parallel",)),
    )(w, g, m, v)


SMALL_ROWS = 560
DECAY_ROWS = 8
SMALL_TOTAL = SMALL_ROWS + 2 * N_SHARDS * DECAY_ROWS


def _adamw_small(gathered, wp, mp, vp):
    out_rows = SMALL_ROWS + 2 * DECAY_ROWS

    def body(ga_ref, w_ref, m_ref, v_ref, g_ref, d_ref, mo_ref, vo_ref):
        shard = 2 * lax.axis_index("x") + lax.axis_index("y")
        g_ref[pl.ds(0, SMALL_ROWS), :] = functools.reduce(lambda a, b: a + b, [ga_ref[d, pl.ds(0, SMALL_ROWS), :] for d in range(8)])
        for k in range(2):
            start = pl.multiple_of(SMALL_ROWS + k * N_SHARDS * DECAY_ROWS + shard * DECAY_ROWS, DECAY_ROWS)
            g_ref[pl.ds(SMALL_ROWS + k * DECAY_ROWS, DECAY_ROWS), :] = functools.reduce(
                lambda a, b: a + b, [ga_ref[d, pl.ds(start, DECAY_ROWS), :] for d in range(8)])
        d_ref[...], mo_ref[...], vo_ref[...] = _adam_math(w_ref[...], g_ref[...], m_ref[...], v_ref[...])

    shape = jax.ShapeDtypeStruct((out_rows, LANES), F32)
    return pl.pallas_call(body, name="adamw_small", out_shape=[shape] * 4, compiler_params=_params(32, None))(gathered, wp, mp, vp)


ANY = pl.BlockSpec(memory_space=pl.ANY)


def _position():
    return lax.axis_index("x"), lax.axis_index("y"), lax.axis_index("c")


def _other_chips(x, y):
    return [(1 - x, y), (x, 1 - y), (1 - x, 1 - y)]


HBM = pl.BlockSpec(memory_space=pltpu.HBM)
SEM = pl.BlockSpec(memory_space=pltpu.SEMAPHORE)
TOKEN = jax.ShapeDtypeStruct(TOKEN_SHAPE, F32)
DATAFLOW = pltpu.SideEffectType.DATAFLOW_SIDE_EFFECTING


def _half_block(ref4, slot, core):
    hr = ref4.shape[1] // 2
    return ref4.at[slot, pl.ds(core * hr, hr), :]


def _gather_ici_copies(refs4, send_sems, recv_sems, stride):
    x, y, c = _position()
    pairs = []
    for k, ref4 in enumerate(refs4):
        mine = _half_block(ref4, 2 * x + y, c)
        for j, (px, py) in enumerate(_other_chips(x, y)):
            sems = dict(send_sem=send_sems.at[stride * k + j], recv_sem=recv_sems.at[stride * k + j], device_id=(px, py, c), device_id_type=MESH)
            pairs.append((functools.partial(pltpu.make_async_remote_copy, src_ref=mine, dst_ref=mine, **sems),
                          functools.partial(pltpu.make_async_remote_copy, src_ref=mine, dst_ref=_half_block(ref4, 2 * px + py, c), **sems)))
    return pairs


def _gather_d2d_copies(refs4, send_sems, recv_sems, stride, offset):
    x, y, c = _position()
    pairs = []
    for k, ref4 in enumerate(refs4):
        for j, (px, py) in enumerate(_other_chips(x, y)):
            have = _half_block(ref4, 2 * px + py, c)
            sems = dict(send_sem=send_sems.at[stride * k + offset + j], recv_sem=recv_sems.at[stride * k + offset + j],
                        device_id=(x, y, 1 - c), device_id_type=MESH)
            pairs.append((functools.partial(pltpu.make_async_remote_copy, src_ref=have, dst_ref=have, **sems),
                          functools.partial(pltpu.make_async_remote_copy, src_ref=have, dst_ref=_half_block(ref4, 2 * px + py, 1 - c), **sems)))
    return pairs


def _gather_sync(bufs):
    n = len(bufs)

    def body(*refs):
        outs = refs[n : 2 * n]
        send_sems, recv_sems = refs[2 * n :]
        ici = _gather_ici_copies(outs, send_sems, recv_sems, 6)
        d2d = _gather_d2d_copies(outs, send_sems, recv_sems, 6, 3)
        for send, _ in ici:
            send().start()
        for (_, arrival), (forward, _) in zip(ici, d2d):
            arrival().wait_recv()
            forward().start()
        for _, arrival in d2d:
            arrival().wait_recv()
        for send, _ in ici + d2d:
            send().wait_send()

    return pl.pallas_call(
        body,
        name="gather_sync",
        in_specs=[ANY] * n,
        out_specs=[ANY] * n,
        out_shape=[jax.ShapeDtypeStruct(b.shape, b.dtype) for b in bufs],
        input_output_aliases={k: k for k in range(n)},
        scratch_shapes=[pltpu.SemaphoreType.DMA((6 * n,)), pltpu.SemaphoreType.DMA((6 * n,))],
        compiler_params=pltpu.CompilerParams(has_side_effects=True),
    )(*bufs)


def _gather_start(bufs, after):
    n = len(bufs)

    def body(*refs):
        ins = refs[:n]
        send_sems, recv_sems = refs[n + 1], refs[n + 2]
        token = refs[2 * n + 3]
        for send, _ in _gather_ici_copies(ins, send_sems, recv_sems, 3):
            send().start()
        token[...] = jnp.zeros_like(token)

    out = pl.pallas_call(
        body,
        name="gather_start",
        in_specs=[HBM] * n + [ANY],
        out_specs=(SEM, SEM, *[HBM] * n, pl.BlockSpec(memory_space=pltpu.VMEM)),
        out_shape=(pltpu.SemaphoreType.DMA((3 * n,)), pltpu.SemaphoreType.DMA((3 * n,)), *[pltpu.HBM(b.shape, b.dtype) for b in bufs], TOKEN),
        input_output_aliases={k: 2 + k for k in range(n)},
        compiler_params=pltpu.CompilerParams(has_side_effects=DATAFLOW),
    )(*[pltpu.with_memory_space_constraint(b, pltpu.HBM) for b in bufs], after)
    return out[0], out[1], list(out[2 : 2 + n]), out[2 + n]


def _gather_wait(send_sems, recv_sems, bufs, after):
    n = len(bufs)

    def body(*refs):
        ins = refs[:n]
        for send, arrival in _gather_ici_copies(ins, refs[n], refs[n + 1], 3):
            send().wait_send()
            arrival().wait_recv()

    return pl.pallas_call(
        body,
        name="gather_wait",
        in_specs=[HBM] * n + [SEM, SEM] + [ANY] * len(after),
        out_specs=tuple([HBM] * n),
        out_shape=tuple(pltpu.HBM(b.shape, b.dtype) for b in bufs),
        input_output_aliases={k: k for k in range(n)},
        compiler_params=pltpu.CompilerParams(has_side_effects=DATAFLOW),
    )(*bufs, send_sems, recv_sems, *after)


def _gather_forward(bufs):
    n = len(bufs)

    def body(*refs):
        outs = refs[n : 2 * n]
        send_sems, recv_sems = refs[2 * n :]
        d2d = _gather_d2d_copies(outs, send_sems, recv_sems, 3, 0)
        for forward, _ in d2d:
            forward().start()
        for forward, arrival in d2d:
            arrival().wait_recv()
            forward().wait_send()

    return pl.pallas_call(
        body,
        name="gather_forward",
        in_specs=[ANY] * n,
        out_specs=[ANY] * n,
        out_shape=[jax.ShapeDtypeStruct(b.shape, b.dtype) for b in bufs],
        input_output_aliases={k: k for k in range(n)},
        scratch_shapes=[pltpu.SemaphoreType.DMA((3 * n,)), pltpu.SemaphoreType.DMA((3 * n,))],
        compiler_params=pltpu.CompilerParams(has_side_effects=True),
    )(*bufs)


def _exchange_halves(grads4):
    n = len(grads4)

    def body(*refs):
        ins, outs = refs[:n], refs[n : 2 * n]
        send_sems, recv_sems = refs[2 * n :]
        x, y, c = _position()
        copies = []
        for k in range(n):
            hr = grads4[k].shape[1] // 2
            cp = pltpu.make_async_remote_copy(
                src_ref=ins[k].at[:, pl.ds((1 - c) * hr, hr), :], dst_ref=outs[k],
                send_sem=send_sems.at[k], recv_sem=recv_sems.at[k], device_id=(x, y, 1 - c), device_id_type=MESH)
            cp.start()
            copies.append(cp)
        for cp in copies:
            cp.wait()

    return pl.pallas_call(
        body,
        name="exchange_halves",
        in_specs=[ANY] * n,
        out_specs=[ANY] * n,
        out_shape=[jax.ShapeDtypeStruct((N_SHARDS, g.shape[1] // 2, g.shape[2]), g.dtype) for g in grads4],
        scratch_shapes=[pltpu.SemaphoreType.DMA((n,)), pltpu.SemaphoreType.DMA((n,))],
        compiler_params=pltpu.CompilerParams(has_side_effects=True),
    )(*grads4)


def _scatter_partials(parts4):
    n = len(parts4)

    def body(*refs):
        ins, outs = refs[:n], refs[n : 2 * n]
        send_sems, recv_sems = refs[2 * n :]
        x, y, c = _position()
        copies = []
        for k in range(n):
            for j, (px, py) in enumerate(_other_chips(x, y)):
                cp = pltpu.make_async_remote_copy(
                    src_ref=ins[k].at[2 * px + py], dst_ref=outs[k].at[j],
                    send_sem=send_sems.at[3 * k + j], recv_sem=recv_sems.at[3 * k + j], device_id=(px, py, c), device_id_type=MESH)
                cp.start()
                copies.append(cp)
        for cp in copies:
            cp.wait()

    return pl.pallas_call(
        body,
        name="scatter_partials",
        in_specs=[ANY] * n,
        out_specs=[ANY] * n,
        out_shape=[jax.ShapeDtypeStruct((3,) + g.shape[1:], g.dtype) for g in parts4],
        scratch_shapes=[pltpu.SemaphoreType.DMA((3 * n,)), pltpu.SemaphoreType.DMA((3 * n,))],
        compiler_params=pltpu.CompilerParams(has_side_effects=True),
    )(*parts4)


def _scatter_copies(parts, lands, send_sems, recv_sems):
    x, y, c = _position()
    copies = []
    for k in range(len(parts)):
        for j, (px, py) in enumerate(_other_chips(x, y)):
            copies.append(pltpu.make_async_remote_copy(
                src_ref=parts[k].at[2 * px + py], dst_ref=lands[k].at[j],
                send_sem=send_sems.at[3 * k + j], recv_sem=recv_sems.at[3 * k + j], device_id=(px, py, c), device_id_type=MESH))
    return copies


def _scatter_start(parts4):
    n = len(parts4)
    lands = [lax.empty((3,) + g.shape[1:], g.dtype) for g in parts4]

    def body(*refs):
        ins, land_in = refs[:n], refs[n : 2 * n]
        send_sems, recv_sems = refs[2 * n], refs[2 * n + 1]
        token = refs[4 * n + 2]
        for cp in _scatter_copies(ins, land_in, send_sems, recv_sems):
            cp.start()
        token[...] = jnp.zeros_like(token)

    hbm = lambda a: pltpu.HBM(a.shape, a.dtype)
    out = pl.pallas_call(
        body,
        name="scatter_start",
        in_specs=[HBM] * (2 * n),
        out_specs=(SEM, SEM, *[HBM] * (2 * n), pl.BlockSpec(memory_space=pltpu.VMEM)),
        out_shape=(pltpu.SemaphoreType.DMA((3 * n,)), pltpu.SemaphoreType.DMA((3 * n,)), *[hbm(a) for a in parts4], *[hbm(a) for a in lands], TOKEN),
        input_output_aliases={k: 2 + k for k in range(2 * n)},
        compiler_params=pltpu.CompilerParams(has_side_effects=DATAFLOW),
    )(*[pltpu.with_memory_space_constraint(a, pltpu.HBM) for a in parts4 + lands])
    return out[0], out[1], list(out[2 : 2 + n]), list(out[2 + n : 2 + 2 * n]), out[2 + 2 * n]


def _scatter_wait(send_sems, recv_sems, parts4, lands, after):
    n = len(parts4)

    def body(*refs):
        for cp in _scatter_copies(refs[:n], refs[n : 2 * n], refs[2 * n], refs[2 * n + 1]):
            cp.wait_send()
            cp.wait_recv()

    hbm = lambda a: pltpu.HBM(a.shape, a.dtype)
    out = pl.pallas_call(
        body,
        name="scatter_wait",
        in_specs=[HBM] * (2 * n) + [SEM, SEM] + [ANY] * len(after),
        out_specs=tuple([HBM] * (2 * n)),
        out_shape=tuple(hbm(a) for a in parts4 + lands),
        input_output_aliases={k: k for k in range(2 * n)},
        compiler_params=pltpu.CompilerParams(has_side_effects=DATAFLOW),
    )(*parts4, *lands, send_sems, recv_sems, *after)
    return list(out[n:])


def _join_halves(bufs):
    n = len(bufs)

    def body(*refs):
        outs = refs[n : 2 * n]
        send_sems, recv_sems = refs[2 * n :]
        x, y, c = _position()
        for k in range(n):
            hr = bufs[k].shape[0] // 2
            mine = outs[k].at[pl.ds(c * hr, hr), :]
            pltpu.make_async_remote_copy(
                src_ref=mine, dst_ref=mine, send_sem=send_sems.at[k], recv_sem=recv_sems.at[k],
                device_id=(x, y, 1 - c), device_id_type=MESH).start()
        for k in range(n):
            hr = bufs[k].shape[0] // 2
            mine = outs[k].at[pl.ds(c * hr, hr), :]
            theirs = outs[k].at[pl.ds((1 - c) * hr, hr), :]
            wait = pltpu.make_async_remote_copy(
                src_ref=mine, dst_ref=theirs, send_sem=send_sems.at[k], recv_sem=recv_sems.at[k],
                device_id=(x, y, 1 - c), device_id_type=MESH)
            wait.wait_send()
            wait.wait_recv()

    return pl.pallas_call(
        body,
        name="join_halves",
        in_specs=[ANY] * n,
        out_specs=[ANY] * n,
        out_shape=[jax.ShapeDtypeStruct(b.shape, b.dtype) for b in bufs],
        input_output_aliases={k: k for k in range(n)},
        scratch_shapes=[pltpu.SemaphoreType.DMA((n,)), pltpu.SemaphoreType.DMA((n,))],
        compiler_params=pltpu.CompilerParams(has_side_effects=True),
    )(*bufs)


def _allgather_small(block):
    m_per, ncol = block.shape

    def body(x_ref, out_ref, send_sems, recv_sems, local_sem):
        x, y, c = _position()
        me, sibling = (x, y, c), (x, y, 1 - c)
        chips = _other_chips(x, y)

        def rows(px, py, pc):
            return out_ref.at[4 * px + 2 * py + pc]

        def copy(k, blk, to, src=None):
            return pltpu.make_async_remote_copy(
                src_ref=rows(*blk) if src is None else src, dst_ref=rows(*blk),
                send_sem=send_sems.at[k], recv_sem=recv_sems.at[k], device_id=to, device_id_type=MESH)

        mine = pltpu.make_async_copy(x_ref, rows(*me), local_sem)
        mine.start()
        first = [copy(0, me, sibling, src=x_ref)] + [copy(1 + j, me, (*chip, c), src=x_ref) for j, chip in enumerate(chips)]
        for cp in first:
            cp.start()
        passed = [copy(4 + j, (*chip, c), sibling) for j, chip in enumerate(chips)]
        for j, chip in enumerate(chips):
            copy(1 + j, (*chip, c), me).wait_recv()
            passed[j].start()
        copy(0, sibling, me).wait_recv()
        for j, chip in enumerate(chips):
            copy(4 + j, (*chip, 1 - c), me).wait_recv()
        for cp in first + passed:
            cp.wait_send()
        mine.wait()

    return pl.pallas_call(
        body,
        name="allgather_small",
        in_specs=[pl.BlockSpec(memory_space=pltpu.VMEM)],
        out_specs=pl.BlockSpec(memory_space=pltpu.VMEM),
        out_shape=jax.ShapeDtypeStruct((8, m_per, ncol), block.dtype),
        scratch_shapes=[pltpu.SemaphoreType.DMA((7,)), pltpu.SemaphoreType.DMA((7,)), pltpu.SemaphoreType.DMA],
        compiler_params=pltpu.CompilerParams(has_side_effects=True, vmem_limit_bytes=32 * MIB),
    )(block)


SMALL_NAMES = ["norm1_g", "b_decay_f", "b_decay_b", "gla_norm_g", "gmlp_ln_g", "gmlp_ln_b", "w_spatial", "b_spatial", "norm2_g", "final_norm_g"]


def _pack_small(parts, decay_parts):
    flat = jnp.concatenate([a.reshape(-1) for a in parts])
    flat = jnp.pad(flat, (0, SMALL_ROWS * LANES - flat.shape[0])).reshape(SMALL_ROWS, LANES)
    return jnp.concatenate([flat] + [d.reshape(-1, LANES) for d in decay_parts], axis=0)


def _unpack_small(packed, like):
    out, off = [], 0
    flat = packed[:SMALL_ROWS].reshape(-1)
    for a in like:
        out.append(flat[off : off + a.size].reshape(a.shape))
        off += a.size
    return out


def kernel(x, norm1_g, w_in, w_decay_f, b_decay_f, w_decay_b, b_decay_b, gla_norm_g, gmlp_ln_g, gmlp_ln_b, w_spatial, b_spatial, w_out, norm2_g, w_gate, w_up, w_down, final_norm_g, loss_target, m_norm1_g, m_w_in, m_w_decay_f, m_b_decay_f, m_w_decay_b, m_b_decay_b, m_gla_norm_g, m_gmlp_ln_g, m_gmlp_ln_b, m_w_spatial, m_b_spatial, m_w_out, m_norm2_g, m_w_gate, m_w_up, m_w_down, m_final_norm_g, v_norm1_g, v_w_in, v_w_decay_f, v_b_decay_f, v_w_decay_b, v_b_decay_b, v_gla_norm_g, v_gmlp_ln_g, v_gmlp_ln_b, v_w_spatial, v_b_spatial, v_w_out, v_norm2_g, v_w_gate, v_w_up, v_w_down, v_final_norm_g):
    args = dict(locals())
    cx, cy, cc = lax.axis_index("x"), lax.axis_index("y"), lax.axis_index("c")
    shard = 2 * cx + cy
    xs = x[0]
    target = loss_target[0]

    big_names = ["w_in", "w_out", "w_gate", "w_up", "w_down"]
    big_shards = {k: args[k][0] for k in big_names}
    c_arr = cc.reshape(1).astype(jnp.int32)
    s_arr = shard.reshape(1).astype(jnp.int32)
    sc_arr = jnp.stack([shard, cc]).astype(jnp.int32)
    slots = {k: _cast_into_slot(big_shards[k], s_arr) for k in big_names}
    (w_in4,) = _gather_sync([slots["w_in"]])
    late = ["w_out", "w_gate", "w_up", "w_down"]
    g_send, g_recv, late_bufs, token_gather = _gather_start([slots[k] for k in late], w_in4)
    w_in_full = jnp.transpose(w_in4, (1, 0, 2)).reshape(D_MODEL, PROJ_W)
    w_in_p = jnp.concatenate(
        [w_in_full[:, :1536], w_in_full[:, 1568:], w_in_full[:, 1536:1568], jnp.zeros((D_MODEL, PROJ_WP - PROJ_W), BF16)], axis=1)

    dec_block = jnp.concatenate([w_decay_f[0].reshape(-1, LANES), w_decay_b[0].reshape(-1, LANES)], axis=0)
    dec_all = _allgather_small(dec_block)
    dec_all = dec_all[::2].reshape(N_SHARDS, 2, LOWRANK, KEY_W // N_SHARDS)
    wdf_full = jnp.transpose(dec_all[:, 0], (1, 0, 2)).reshape(LOWRANK, KEY_W)
    wdb_full = jnp.transpose(dec_all[:, 1], (1, 0, 2)).reshape(LOWRANK, KEY_W)
    wd_pad_f = jnp.zeros((LANES, KEY_W), F32).at[0:LOWRANK].set(wdf_full).astype(BF16)
    wd_pad_b = jnp.zeros((LANES, KEY_W), F32).at[LOWRANK : 2 * LOWRANK].set(wdb_full).astype(BF16)

    ws_bf = w_spatial[0].astype(BF16)
    wst_bf = jnp.transpose(w_spatial[0], (0, 2, 1)).astype(BF16)
    bs_col = b_spatial[0].reshape(GMLP_GROUPS, GMLP_CHUNK, 1)

    p = _inproj(xs, norm1_g, w_in_p, token_gather)
    o_f, st_f = _gla_fwd(p, wd_pad_f, b_decay_f, reverse=False)
    o_b, st_b = _gla_fwd(p, wd_pad_b, b_decay_b, reverse=True)
    late_bufs = _gather_forward(_gather_wait(g_send, g_recv, late_bufs, (o_f, o_b)))
    w_out4, w_gate4, w_up4, w_down4 = late_bufs
    w_out_full = w_out4.reshape(D_MODEL, D_MODEL)
    x1, ycat = _mixer_out(xs, o_f, o_b, p, gla_norm_g, gmlp_ln_g, gmlp_ln_b, ws_bf, bs_col, w_out_full)
    gf = final_norm_g.reshape(1, D_MODEL)
    h2, gate4, up4, act4, dx2, loss_acc, dgf = _ffn_fwd(x1, target, norm2_g, gf, w_gate4, w_up4, w_down4)

    dgate4, dup4, dx1, dg2 = _ffn_bwd(dx2, gate4, up4, x1, norm2_g, w_gate4, w_up4, w_down4)
    dwg4, dwu4, dwd4 = _ffn_wgrad(h2, dgate4, dup4, act4, dx2)
    ffn_grads4 = [dwg4, dwu4, dwd4]
    ffn_parts = [_add_halves(g, r, c_arr) for g, r in zip(ffn_grads4, _exchange_halves(ffn_grads4))]
    s_send, s_recv, s_parts, s_lands, token_scatter = _scatter_start([pb for _, pb in ffn_parts])
    do, dg, du, dvv, dwo, dgn, dlng, dlnb, dws, dbs = _mixer_bwd(
        dx1, ycat, o_f, o_b, p, gla_norm_g, gmlp_ln_g, gmlp_ln_b, ws_bf, wst_bf, bs_col, w_out_full, token_scatter)
    dq_f, dk_f, dv_f, dlr_f, dwdec_f, dbdec_f = _gla_bwd(p, do, st_f, wd_pad_f, b_decay_f, reverse=False)
    dq_b, dk_b, dv_b, dlr_b, dwdec_b, dbdec_b = _gla_bwd(p, do, st_b, wd_pad_b, b_decay_b, reverse=True)
    dx, dwin_p, dg1 = _inproj_bwd(xs, dx1, norm1_g, w_in_p, dq_f, dq_b, dk_f, dk_b, dv_f, dv_b, dg, du, dvv, dlr_f, dlr_b)
    ffn_recv = _scatter_wait(s_send, s_recv, s_parts, s_lands, (dwin_p,))

    dwin = jnp.concatenate([dwin_p[:, :1536], dwin_p[:, 2560:2592], dwin_p[:, 1536:2560]], axis=1)
    dwin4 = jnp.transpose(dwin.reshape(D_MODEL, N_SHARDS, PROJ_W // N_SHARDS), (1, 0, 2))
    dwo4 = dwo.reshape(N_SHARDS, D_MODEL // N_SHARDS, D_MODEL)
    proj_grads4 = [dwin4, dwo4]
    proj_parts = [_add_halves(g, r, c_arr) for g, r in zip(proj_grads4, _exchange_halves(proj_grads4))]
    proj_recv = _scatter_partials([pb for _, pb in proj_parts])
    parts_f32 = [pf for pf, _ in proj_parts + ffn_parts]
    bufs = [_add_partials(pf, r, sc_arr) for pf, r in zip(parts_f32, list(proj_recv) + ffn_recv)]
    big_grads = dict(zip(big_names, _join_halves(bufs)))

    dwdec_f16 = dwdec_f[0:LOWRANK]
    dwdec_b16 = dwdec_b[LOWRANK : 2 * LOWRANK]
    shard_major = lambda a: jnp.transpose(a.reshape(LOWRANK, N_SHARDS, KEY_W // N_SHARDS), (1, 0, 2))
    small_grads = {
        "norm1_g": dg1, "b_decay_f": dbdec_f, "b_decay_b": dbdec_b, "gla_norm_g": dgn, "gmlp_ln_g": dlng, "gmlp_ln_b": dlnb,
        "w_spatial": dws, "b_spatial": dbs, "norm2_g": dg2, "final_norm_g": dgf,
    }
    g_pack = _pack_small([small_grads[k] for k in SMALL_NAMES], [shard_major(dwdec_f16), shard_major(dwdec_b16)])
    g_all = _allgather_small(g_pack)
    pack_own = lambda pre: _pack_small([args[pre + k] for k in SMALL_NAMES], [args[pre + "w_decay_f"], args[pre + "w_decay_b"]])
    sg, sd, sm, sv = _adamw_small(g_all, pack_own(""), pack_own("m_"), pack_own("v_"))

    names = ["norm1_g", "w_in", "w_decay_f", "b_decay_f", "w_decay_b", "b_decay_b", "gla_norm_g", "gmlp_ln_g", "gmlp_ln_b",
             "w_spatial", "b_spatial", "w_out", "norm2_g", "w_gate", "w_up", "w_down", "final_norm_g"]
    like = [args[k] for k in SMALL_NAMES]
    results = {"g": {}, "d": {}, "m": {}, "v": {}}
    for tag, packed in (("g", sg), ("d", sd), ("m", sm), ("v", sv)):
        for k, a in zip(SMALL_NAMES, _unpack_small(packed, like)):
            results[tag][k] = a
        results[tag]["w_decay_f"] = packed[SMALL_ROWS : SMALL_ROWS + DECAY_ROWS].reshape(w_decay_f.shape)
        results[tag]["w_decay_b"] = packed[SMALL_ROWS + DECAY_ROWS :].reshape(w_decay_b.shape)
    for k in big_names:
        g = big_grads[k]
        d, mo, vo = _adamw(big_shards[k], g, args["m_" + k][0], args["v_" + k][0])
        for tag, a in (("g", g), ("d", d), ("m", mo), ("v", vo)):
            results[tag][k] = a.reshape(args[k].shape)

    loss = lax.psum(loss_acc[0, 0], ("x", "y", "c"))
    grad_x = dx.reshape(x.shape)
    return (loss, grad_x, *[results["g"][k] for k in names], *[results["d"][k] for k in names],
            *[results["m"][k] for k in names], *[results["v"][k] for k in names])
```

```python
import functools
import math

import jax
import jax.numpy as jnp
from jax import lax
from jax.experimental import pallas as pl
from jax.experimental.pallas import tpu as pltpu

F32, BF16 = jnp.float32, jnp.bfloat16

D_MODEL = 1024
GLA_HEADS = 4
GLA_DK = 64
GLA_DV = 128
KEY_W = GLA_HEADS * GLA_DK
GLA_W = GLA_HEADS * GLA_DV
GMLP_W = 512
GMLP_GROUPS = 4
GMLP_CHUNK = 128
LOWRANK = 16
GLA_CHUNK = 64
GLA_TAU = 16.0
PROJ_W = 2592
PROJ_WP = 2688
D_FF = 2816
N_SHARDS = 4
FF_SHARD = D_FF // N_SHARDS
EPS = 1e-6
LANES = 128
TOKEN_SHAPE = (8, LANES)
MIB = 1024 * 1024

ADAM_LR = 0.001
ADAM_B1 = 0.9
ADAM_B2 = 0.999
ADAM_EPS = 1e-08
ADAM_WD = 0.01
ADAM_STEP = 10

COL_Q, COL_K = 0, 256
COL_V, COL_G, COL_U, COL_VV = 512, 1024, 1536, 2048
COL_LR = 2560
ROW_LR, ROW_UV = 1536, 1568
HALF = D_MODEL // 2

MESH = pl.DeviceIdType.MESH


def _nn(a, b):
    return jnp.dot(a, b, preferred_element_type=F32)


def _nt(a, b):
    return lax.dot_general(a, b, (((1,), (1,)), ((), ())), preferred_element_type=F32)


def _tn(a, b):
    return lax.dot_general(a, b, (((0,), (0,)), ((), ())), preferred_element_type=F32)


def _bnn(a, b):
    return jnp.einsum("nik,nkj->nij", a, b, preferred_element_type=F32)


def _bnt(a, b):
    return jnp.einsum("nik,njk->nij", a, b, preferred_element_type=F32)


def _btn(a, b):
    return jnp.einsum("nki,nkj->nij", a, b, preferred_element_type=F32)


def _resident(shape):
    zeros = (0,) * len(shape)
    return pl.BlockSpec(shape, lambda *_: zeros, pipeline_mode=pl.Buffered(1))


def _params(vmem_mib, semantics=("arbitrary",)):
    return pltpu.CompilerParams(vmem_limit_bytes=vmem_mib * MIB, dimension_semantics=semantics)


def _sigmoid(x):
    return 1.0 / (1.0 + jnp.exp(-x))


def _gelu(x):
    return 0.5 * x * (1.0 + lax.erf(x * (1.0 / math.sqrt(2.0))))


def _gelu_grad(x):
    return 0.5 * (1.0 + lax.erf(x * (1.0 / math.sqrt(2.0)))) + x * jnp.exp(-0.5 * x * x) * (1.0 / math.sqrt(2.0 * math.pi))


def _log_sigmoid(x):
    return jnp.minimum(x, 0.0) - jnp.log(1.0 + jnp.exp(-jnp.abs(x)))


def _rms_bwd(dxh, xh, r):
    return r * (dxh - xh * jnp.mean(dxh * xh, axis=-1, keepdims=True))


def _chunk_cumsum(v, row_in_chunk, reverse):
    rows = v.shape[0]
    for sh in (1, 2, 4, 8, 16, 32):
        if reverse:
            v = v + jnp.where(row_in_chunk + sh < GLA_CHUNK, pltpu.roll(v, rows - sh, axis=0), 0.0)
        else:
            v = v + jnp.where(row_in_chunk >= sh, pltpu.roll(v, sh, axis=0), 0.0)
    return v


def _inproj(x, g1, w_in_t, token):
    seq = x.shape[0]
    tm = min(seq, 512)

    def body(x_ref, g_ref, w_ref, token_ref, p_ref):
        xv = x_ref[...]
        r = lax.rsqrt(jnp.mean(xv * xv, axis=-1, keepdims=True) + EPS)
        h = (xv * r * g_ref[...]).astype(BF16)
        p_ref[:, 0:COL_U] = _nt(h, w_ref[0:ROW_LR, :])
        p_ref[:, COL_U:COL_LR] = _nt(h, w_ref[ROW_UV:PROJ_W, :])
        p_ref[:, COL_LR:PROJ_WP] = _nt(h, w_ref[ROW_LR : ROW_LR + LANES, :])

    return pl.pallas_call(
        body,
        name="inproj",
        grid=(seq // tm,),
        in_specs=[pl.BlockSpec((tm, D_MODEL), lambda i: (i, 0)), _resident((1, D_MODEL)), _resident((PROJ_W, D_MODEL)), _resident(TOKEN_SHAPE)],
        out_specs=pl.BlockSpec((tm, PROJ_WP), lambda i: (i, 0)),
        out_shape=jax.ShapeDtypeStruct((seq, PROJ_WP), F32),
        compiler_params=_params(48, ("parallel",)),
    )(x, g1, w_in_t, token)


def _gla_tile(seq):
    return min(seq, 512)


def _gla_decay_terms(lr_bf, wd_ref, bd_ref, pair, row_in_chunk, reverse, n):
    cols = pl.ds(pair * LANES, LANES)
    pre = _nn(lr_bf, wd_ref[:, cols]) + bd_ref[:, cols]
    la = _log_sigmoid(pre) * (1.0 / GLA_TAU)
    b = _chunk_cumsum(la, row_in_chunk, reverse)
    b3 = b.reshape(n, GLA_CHUNK, LANES)
    blast = b3[:, 0:1, :] if reverse else b3[:, GLA_CHUNK - 1 : GLA_CHUNK, :]
    return pre, b3, blast


def _gla_fwd(p, wd_pad, bd, reverse):
    seq = p.shape[0]
    tg = _gla_tile(seq)
    nt = seq // tg
    n = tg // GLA_CHUNK
    scale = GLA_DK**-0.5

    def tile(i):
        return nt - 1 - i if reverse else i

    def body(q_ref, k_ref, v_ref, lr_ref, wd_ref, bd_ref, o_ref, st_ref, carry):
        @pl.when(pl.program_id(0) == 0)
        def _():
            carry[...] = jnp.zeros_like(carry)

        lr_bf = lr_ref[...].astype(BF16)
        row_in_chunk = lax.broadcasted_iota(jnp.int32, (tg, LANES), 0) % GLA_CHUNK
        lane_head = lax.broadcasted_iota(jnp.int32, (1, LANES), 1) // GLA_DK
        tt = lax.broadcasted_iota(jnp.int32, (GLA_CHUNK, GLA_CHUNK), 0)
        ss = lax.broadcasted_iota(jnp.int32, (GLA_CHUNK, GLA_CHUNK), 1)
        causal = (tt <= ss) if reverse else (tt >= ss)
        order = range(n - 1, -1, -1) if reverse else range(n)
        for pair in range(2):
            cols = pl.ds(pair * LANES, LANES)
            _, b3, blast = _gla_decay_terms(lr_bf, wd_ref, bd_ref, pair, row_in_chunk, reverse, n)
            q3 = q_ref[:, cols].reshape(n, GLA_CHUNK, LANES) * scale
            k3 = k_ref[:, cols].reshape(n, GLA_CHUNK, LANES)
            qd = q3 * jnp.exp(b3)
            kd = (k3 * jnp.exp(-b3)).astype(BF16)
            kte = k3 * jnp.exp(blast - b3)
            dec = jnp.exp(blast)
            for hh in range(2):
                h = 2 * pair + hh
                m = (lane_head == hh).astype(F32)
                qdh = (qd * m).astype(BF16)
                kteh = (kte * m).astype(BF16)
                vh = v_ref[:, pl.ds(h * GLA_DV, GLA_DV)].reshape(n, GLA_CHUNK, GLA_DV).astype(BF16)
                sc = jnp.where(causal, _bnt(qdh, kd), 0.0)
                o_intra = _bnn(sc.astype(BF16), vh)
                dst = _btn(vh, kteh)
                st = carry[h]
                for j in order:
                    st_ref[j, h] = st
                    st = st * dec[j] + dst[j]
                carry[h] = st
                o_inter = _bnt(qdh, st_ref[:, h].astype(BF16))
                o_ref[:, pl.ds(h * GLA_DV, GLA_DV)] = (o_intra + o_inter).reshape(tg, GLA_DV)

    nchunks = seq // GLA_CHUNK
    return pl.pallas_call(
        body,
        name="gla_fwd_rev" if reverse else "gla_fwd",
        grid=(nt,),
        in_specs=[
            pl.BlockSpec((tg, KEY_W), lambda i: (tile(i), COL_Q // KEY_W)),
            pl.BlockSpec((tg, KEY_W), lambda i: (tile(i), COL_K // KEY_W)),
            pl.BlockSpec((tg, GLA_W), lambda i: (tile(i), COL_V // GLA_W)),
            pl.BlockSpec((tg, LANES), lambda i: (tile(i), COL_LR // LANES)),
            _resident((LANES, KEY_W)),
            _resident((1, KEY_W)),
        ],
        out_specs=[
            pl.BlockSpec((tg, GLA_W), lambda i: (tile(i), 0)),
            pl.BlockSpec((n, GLA_HEADS, GLA_DV, LANES), lambda i: (tile(i), 0, 0, 0)),
        ],
        out_shape=[
            jax.ShapeDtypeStruct((seq, GLA_W), F32),
            jax.ShapeDtypeStruct((nchunks, GLA_HEADS, GLA_DV, LANES), F32),
        ],
        scratch_shapes=[pltpu.VMEM((GLA_HEADS, GLA_DV, LANES), F32)],
        compiler_params=_params(48),
    )(p, p, p, p, wd_pad, bd)


def _mixer_out(x, o_f, o_b, p, gn, lng, lnb, ws_bf, bs_col, w_out):
    seq = x.shape[0]
    tm = min(seq, 512)

    def body(x_ref, of_ref, ob_ref, g_ref, u_ref, vv_ref, gn_ref, lng_ref, lnb_ref, ws_ref, bs_ref, wo_ref, x1_ref, yc_ref, vn_sc):
        for h in range(GLA_HEADS):
            cols = pl.ds(h * GLA_DV, GLA_DV)
            oh = of_ref[:, cols] + ob_ref[:, cols]
            on = oh * lax.rsqrt(jnp.mean(oh * oh, axis=-1, keepdims=True) + EPS)
            gh = g_ref[:, cols]
            yc_ref[:, cols] = (on * gn_ref[:, cols] * (gh * _sigmoid(gh))).astype(BF16)
        zv = _gelu(vv_ref[...])
        xc = zv - jnp.mean(zv, axis=-1, keepdims=True)
        vhat = xc * lax.rsqrt(jnp.mean(xc * xc, axis=-1, keepdims=True) + EPS)
        vn_sc[...] = (vhat * lng_ref[...] + lnb_ref[...]).astype(BF16)
        for c in range(tm // GMLP_CHUNK):
            rows = pl.ds(c * GMLP_CHUNK, GMLP_CHUNK)
            for g in range(GMLP_GROUPS):
                cols = pl.ds(g * LANES, LANES)
                s = _nn(ws_ref[g], vn_sc[rows, cols]) + bs_ref[g]
                yc_ref[rows, pl.ds(GLA_W + g * LANES, LANES)] = (_gelu(u_ref[rows, cols]) * s).astype(BF16)
        x1_ref[...] = x_ref[...] + _nn(yc_ref[...], wo_ref[...])

    row = lambda w: pl.BlockSpec((tm, w), lambda i: (i, 0))
    pcol = lambda col: pl.BlockSpec((tm, GLA_W), lambda i: (i, col // GLA_W))
    return pl.pallas_call(
        body,
        name="mixer_out",
        grid=(seq // tm,),
        in_specs=[
            row(D_MODEL), row(GLA_W), row(GLA_W), pcol(COL_G), pcol(COL_U), pcol(COL_VV),
            _resident((1, GLA_W)), _resident((1, GMLP_W)), _resident((1, GMLP_W)),
            _resident((GMLP_GROUPS, GMLP_CHUNK, GMLP_CHUNK)), _resident((GMLP_GROUPS, GMLP_CHUNK, 1)),
            _resident((D_MODEL, D_MODEL)),
        ],
        out_specs=[row(D_MODEL), row(D_MODEL)],
        out_shape=[jax.ShapeDtypeStruct((seq, D_MODEL), F32), jax.ShapeDtypeStruct((seq, D_MODEL), BF16)],
        scratch_shapes=[pltpu.VMEM((tm, GMLP_W), BF16)],
        compiler_params=_params(48, ("parallel",)),
    )(x, o_f, o_b, p, p, p, gn, lng, lnb, ws_bf, bs_col, w_out)


def _ffn_fwd(x1, target, g2, gf, wg4, wu4, wd4):
    seq = x1.shape[0]
    tm = min(seq, 256)

    def body(x1_ref, t_ref, g2_ref, gf_ref, wg_ref, wu_ref, wd_ref, h2_ref, gate_ref, up_ref, act_ref, dx2_ref, loss_ref, dgf_ref):
        @pl.when(pl.program_id(0) == 0)
        def _():
            loss_ref[...] = jnp.zeros_like(loss_ref)
            dgf_ref[...] = jnp.zeros_like(dgf_ref)

        x1v = x1_ref[...]
        h2 = (x1v * lax.rsqrt(jnp.mean(x1v * x1v, axis=-1, keepdims=True) + EPS) * g2_ref[...]).astype(BF16)
        h2_ref[...] = h2
        acc = jnp.zeros((tm, D_MODEL), F32)
        for s in range(N_SHARDS):
            gate = _nt(h2, wg_ref[s])
            up = _nt(h2, wu_ref[s])
            act = (gate * _sigmoid(gate) * up).astype(BF16)
            gate_ref[s] = gate
            up_ref[s] = up
            act_ref[s] = act
            acc = acc + _nn(act, wd_ref[s])
        x2 = x1v + acc
        rf = lax.rsqrt(jnp.mean(x2 * x2, axis=-1, keepdims=True) + EPS)
        xh = x2 * rf
        err = xh * gf_ref[...] - t_ref[...]
        loss_ref[...] += 0.5 * jnp.sum(jnp.mean(err * err, axis=-1, keepdims=True))
        dy = err * (1.0 / D_MODEL)
        dgf_ref[...] += jnp.sum(dy * xh, axis=0, keepdims=True)
        dx2_ref[...] = _rms_bwd(dy * gf_ref[...], xh, rf)

    row = lambda w: pl.BlockSpec((tm, w), lambda i: (i, 0))
    ff = pl.BlockSpec((N_SHARDS, tm, FF_SHARD), lambda i: (0, i, 0))
    return pl.pallas_call(
        body,
        name="ffn_fwd",
        grid=(seq // tm,),
        in_specs=[
            row(D_MODEL), row(D_MODEL), _resident((1, D_MODEL)), _resident((1, D_MODEL)),
            _resident((N_SHARDS, FF_SHARD, D_MODEL)), _resident((N_SHARDS, FF_SHARD, D_MODEL)), _resident((N_SHARDS, FF_SHARD, D_MODEL)),
        ],
        out_specs=[row(D_MODEL), ff, ff, ff, row(D_MODEL), pl.BlockSpec((1, LANES), lambda i: (0, 0)), pl.BlockSpec((1, D_MODEL), lambda i: (0, 0))],
        out_shape=[
            jax.ShapeDtypeStruct((seq, D_MODEL), BF16),
            jax.ShapeDtypeStruct((N_SHARDS, seq, FF_SHARD), F32),
            jax.ShapeDtypeStruct((N_SHARDS, seq, FF_SHARD), F32),
            jax.ShapeDtypeStruct((N_SHARDS, seq, FF_SHARD), BF16),
            jax.ShapeDtypeStruct((seq, D_MODEL), F32),
            jax.ShapeDtypeStruct((1, LANES), F32),
            jax.ShapeDtypeStruct((1, D_MODEL), F32),
        ],
        compiler_params=_params(56),
    )(x1, target, g2, gf, wg4, wu4, wd4)


def _ffn_bwd(dx2, gate4, up4, x1, g2, wg4, wu4, wd4):
    seq = x1.shape[0]
    tm = min(seq, 256)

    def body(dx2_ref, gate_ref, up_ref, x1_ref, g2_ref, wg_ref, wu_ref, wd_ref, dgate_ref, dup_ref, dx1_ref, dg2_ref):
        @pl.when(pl.program_id(0) == 0)
        def _():
            dg2_ref[...] = jnp.zeros_like(dg2_ref)

        dx2v = dx2_ref[...]
        dx2b = dx2v.astype(BF16)
        dh2 = jnp.zeros((tm, D_MODEL), F32)
        for s in range(N_SHARDS):
            dact = _nt(dx2b, wd_ref[s])
            gate = gate_ref[s]
            sg = _sigmoid(gate)
            dgate = (dact * up_ref[s] * (sg * (1.0 + gate * (1.0 - sg)))).astype(BF16)
            dup = (dact * (gate * sg)).astype(BF16)
            dgate_ref[s] = dgate
            dup_ref[s] = dup
            dh2 = dh2 + _nn(dgate, wg_ref[s]) + _nn(dup, wu_ref[s])
        x1v = x1_ref[...]
        r2 = lax.rsqrt(jnp.mean(x1v * x1v, axis=-1, keepdims=True) + EPS)
        xh = x1v * r2
        dg2_ref[...] += jnp.sum(dh2 * xh, axis=0, keepdims=True)
        dx1_ref[...] = dx2v + _rms_bwd(dh2 * g2_ref[...], xh, r2)

    row = lambda w: pl.BlockSpec((tm, w), lambda i: (i, 0))
    ff = pl.BlockSpec((N_SHARDS, tm, FF_SHARD), lambda i: (0, i, 0))
    return pl.pallas_call(
        body,
        name="ffn_bwd",
        grid=(seq // tm,),
        in_specs=[
            row(D_MODEL), ff, ff, row(D_MODEL), _resident((1, D_MODEL)),
            _resident((N_SHARDS, FF_SHARD, D_MODEL)), _resident((N_SHARDS, FF_SHARD, D_MODEL)), _resident((N_SHARDS, FF_SHARD, D_MODEL)),
        ],
        out_specs=[ff, ff, row(D_MODEL), pl.BlockSpec((1, D_MODEL), lambda i: (0, 0))],
        out_shape=[
            jax.ShapeDtypeStruct((N_SHARDS, seq, FF_SHARD), BF16),
            jax.ShapeDtypeStruct((N_SHARDS, seq, FF_SHARD), BF16),
            jax.ShapeDtypeStruct((seq, D_MODEL), F32),
            jax.ShapeDtypeStruct((1, D_MODEL), F32),
        ],
        compiler_params=_params(56),
    )(dx2, gate4, up4, x1, g2, wg4, wu4, wd4)


def _ffn_wgrad(h2, dgate4, dup4, act4, dx2):
    seq = h2.shape[0]
    tm = min(seq, 512)

    def body(h2_ref, dgate_ref, dup_ref, act_ref, dx2_ref, dwg_ref, dwu_ref, dwd_ref):
        @pl.when(pl.program_id(1) == 0)
        def _():
            dwg_ref[...] = jnp.zeros_like(dwg_ref)
            dwu_ref[...] = jnp.zeros_like(dwu_ref)
            dwd_ref[...] = jnp.zeros_like(dwd_ref)

        h2v = h2_ref[...]
        dwg_ref[...] += _tn(dgate_ref[...], h2v)
        dwu_ref[...] += _tn(dup_ref[...], h2v)
        dwd_ref[...] += _tn(act_ref[...], dx2_ref[...].astype(BF16))

    ff = pl.BlockSpec((None, tm, FF_SHARD), lambda s, i: (s, i, 0))
    row = pl.BlockSpec((tm, D_MODEL), lambda s, i: (i, 0))
    return pl.pallas_call(
        body,
        name="ffn_wgrad",
        grid=(N_SHARDS, seq // tm),
        in_specs=[row, ff, ff, ff, row],
        out_specs=[
            pl.BlockSpec((None, FF_SHARD, D_MODEL), lambda s, i: (s, 0, 0)),
            pl.BlockSpec((None, FF_SHARD, D_MODEL), lambda s, i: (s, 0, 0)),
            pl.BlockSpec((None, FF_SHARD, D_MODEL), lambda s, i: (s, 0, 0)),
        ],
        out_shape=[
            jax.ShapeDtypeStruct((N_SHARDS, FF_SHARD, D_MODEL), F32),
            jax.ShapeDtypeStruct((N_SHARDS, FF_SHARD, D_MODEL), F32),
            jax.ShapeDtypeStruct((N_SHARDS, FF_SHARD, D_MODEL), F32),
        ],
        compiler_params=_params(48, ("parallel", "arbitrary")),
    )(h2, dgate4, dup4, act4, dx2)


def _mixer_bwd(dx1, ycat, o_f, o_b, p, gn, lng, lnb, ws_bf, wst_bf, bs_col, w_out, token):
    seq = dx1.shape[0]
    tm = min(seq, 512)
    nsteps = seq // tm

    def body(dx1_ref, yc_ref, of_ref, ob_ref, g_ref, u_ref, vv_ref, gn_ref, lng_ref, lnb_ref, ws_ref, wst_ref, bs_ref, wo_ref, token_ref,
             do_ref, dg_ref, du_ref, dvv_ref, dwo_ref, dgn_ref, dlng_ref, dlnb_ref, dws_ref, dbs_ref, vn_sc, dvn_sc, dbs_acc):
        step = pl.program_id(0)

        @pl.when(step == 0)
        def _():
            for r in (dwo_ref, dgn_ref, dlng_ref, dlnb_ref, dws_ref, dbs_acc):
                r[...] = jnp.zeros_like(r)

        dx1b = dx1_ref[...].astype(BF16)
        dyc = _nt(dx1b, wo_ref[...])
        dwo_ref[...] += _tn(yc_ref[...], dx1b)
        for h in range(GLA_HEADS):
            cols = pl.ds(h * GLA_DV, GLA_DV)
            dya = dyc[:, h * GLA_DV : (h + 1) * GLA_DV]
            oh = of_ref[:, cols] + ob_ref[:, cols]
            rn = lax.rsqrt(jnp.mean(oh * oh, axis=-1, keepdims=True) + EPS)
            on = oh * rn
            gh = g_ref[:, cols]
            sg = _sigmoid(gh)
            sil = gh * sg
            gnh = gn_ref[:, cols]
            dgn_ref[:, cols] += jnp.sum(dya * on * sil, axis=0, keepdims=True)
            dg_ref[:, cols] = dya * on * gnh * (sg * (1.0 + gh * (1.0 - sg)))
            do_ref[:, cols] = _rms_bwd(dya * gnh * sil, on, rn)
        vv = vv_ref[...]
        zv = _gelu(vv)
        xc = zv - jnp.mean(zv, axis=-1, keepdims=True)
        rstd = lax.rsqrt(jnp.mean(xc * xc, axis=-1, keepdims=True) + EPS)
        vhat = xc * rstd
        vn_sc[...] = (vhat * lng_ref[...] + lnb_ref[...]).astype(BF16)
        for c in range(tm // GMLP_CHUNK):
            rows = pl.ds(c * GMLP_CHUNK, GMLP_CHUNK)
            for g in range(GMLP_GROUPS):
                cols = pl.ds(g * LANES, LANES)
                vn = vn_sc[rows, cols]
                s = _nn(ws_ref[g], vn) + bs_ref[g]
                dyb = dyc[c * GMLP_CHUNK : (c + 1) * GMLP_CHUNK, GLA_W + g * LANES : GLA_W + (g + 1) * LANES]
                u = u_ref[rows, cols]
                du_ref[rows, cols] = dyb * s * _gelu_grad(u)
                ds = dyb * _gelu(u)
                dbs_acc[g] += ds
                dsb = ds.astype(BF16)
                dws_ref[g] += _nt(dsb, vn)
                dvn_sc[rows, cols] = _nn(wst_ref[g], dsb)
        dvn = dvn_sc[...]
        dlng_ref[...] += jnp.sum(dvn * vhat, axis=0, keepdims=True)
        dlnb_ref[...] += jnp.sum(dvn, axis=0, keepdims=True)
        dvh = dvn * lng_ref[...]
        dzv = rstd * (dvh - jnp.mean(dvh, axis=-1, keepdims=True) - vhat * jnp.mean(dvh * vhat, axis=-1, keepdims=True))
        dvv_ref[...] = dzv * _gelu_grad(vv)

        @pl.when(step == nsteps - 1)
        def _():
            dbs_ref[...] = jnp.sum(dbs_acc[...], axis=-1, keepdims=True)

    row = lambda w: pl.BlockSpec((tm, w), lambda i: (i, 0))
    pcol = lambda col: pl.BlockSpec((tm, GLA_W), lambda i: (i, col // GLA_W))
    const = lambda shape: pl.BlockSpec(shape, lambda i: (0,) * len(shape))
    return pl.pallas_call(
        body,
        name="mixer_bwd",
        grid=(nsteps,),
        in_specs=[
            row(D_MODEL), row(D_MODEL), row(GLA_W), row(GLA_W), pcol(COL_G), pcol(COL_U), pcol(COL_VV),
            _resident((1, GLA_W)), _resident((1, GMLP_W)), _resident((1, GMLP_W)),
            _resident((GMLP_GROUPS, GMLP_CHUNK, GMLP_CHUNK)), _resident((GMLP_GROUPS, GMLP_CHUNK, GMLP_CHUNK)),
            _resident((GMLP_GROUPS, GMLP_CHUNK, 1)), _resident((D_MODEL, D_MODEL)), _resident(TOKEN_SHAPE),
        ],
        out_specs=[
            row(GLA_W), row(GLA_W), row(GMLP_W), row(GMLP_W), const((D_MODEL, D_MODEL)),
            const((1, GLA_W)), const((1, GMLP_W)), const((1, GMLP_W)),
            const((GMLP_GROUPS, GMLP_CHUNK, GMLP_CHUNK)), const((GMLP_GROUPS, GMLP_CHUNK, 1)),
        ],
        out_shape=[
            jax.ShapeDtypeStruct((seq, GLA_W), F32), jax.ShapeDtypeStruct((seq, GLA_W), F32),
            jax.ShapeDtypeStruct((seq, GMLP_W), F32), jax.ShapeDtypeStruct((seq, GMLP_W), F32),
            jax.ShapeDtypeStruct((D_MODEL, D_MODEL), F32),
            jax.ShapeDtypeStruct((1, GLA_W), F32), jax.ShapeDtypeStruct((1, GMLP_W), F32), jax.ShapeDtypeStruct((1, GMLP_W), F32),
            jax.ShapeDtypeStruct((GMLP_GROUPS, GMLP_CHUNK, GMLP_CHUNK), F32), jax.ShapeDtypeStruct((GMLP_GROUPS, GMLP_CHUNK, 1), F32),
        ],
        scratch_shapes=[pltpu.VMEM((tm, GMLP_W), BF16), pltpu.VMEM((tm, GMLP_W), F32), pltpu.VMEM((GMLP_GROUPS, GMLP_CHUNK, GMLP_CHUNK), F32)],
        compiler_params=_params(56),
    )(dx1, ycat, o_f, o_b, p, p, p, gn, lng, lnb, ws_bf, wst_bf, bs_col, w_out, token)


def _gla_bwd(p, do, st, wd_pad, bd, reverse):
    seq = p.shape[0]
    tg = _gla_tile(seq)
    nt = seq // tg
    n = tg // GLA_CHUNK
    scale = GLA_DK**-0.5

    def tile(i):
        return i if reverse else nt - 1 - i

    def body(q_ref, k_ref, v_ref, lr_ref, do_ref, st_ref, wd_ref, bd_ref, dq_ref, dk_ref, dv_ref, dlr_ref, dwd_ref, dbd_ref, carry, dsa):
        @pl.when(pl.program_id(0) == 0)
        def _():
            carry[...] = jnp.zeros_like(carry)
            dwd_ref[...] = jnp.zeros_like(dwd_ref)
            dbd_ref[...] = jnp.zeros_like(dbd_ref)

        lr_bf = lr_ref[...].astype(BF16)
        row_in_chunk = lax.broadcasted_iota(jnp.int32, (tg, LANES), 0) % GLA_CHUNK
        lane_head = lax.broadcasted_iota(jnp.int32, (1, LANES), 1) // GLA_DK
        tt = lax.broadcasted_iota(jnp.int32, (GLA_CHUNK, GLA_CHUNK), 0)
        ss = lax.broadcasted_iota(jnp.int32, (GLA_CHUNK, GLA_CHUNK), 1)
        causal = (tt <= ss) if reverse else (tt >= ss)
        causal_t = (tt >= ss) if reverse else (tt <= ss)
        order = range(n) if reverse else range(n - 1, -1, -1)
        dlr = jnp.zeros((tg, LANES), F32)
        for pair in range(2):
            cols = pl.ds(pair * LANES, LANES)
            pre, b3, blast = _gla_decay_terms(lr_bf, wd_ref, bd_ref, pair, row_in_chunk, reverse, n)
            q3 = q_ref[:, cols].reshape(n, GLA_CHUNK, LANES) * scale
            k3 = k_ref[:, cols].reshape(n, GLA_CHUNK, LANES)
            eb = jnp.exp(b3)
            emb = jnp.exp(-b3)
            ekte = jnp.exp(blast - b3)
            qd = q3 * eb
            kdf = k3 * emb
            kd = kdf.astype(BF16)
            kte = k3 * ekte
            dec = jnp.exp(blast)
            dqd = jnp.zeros((n, GLA_CHUNK, LANES), F32)
            dkd = jnp.zeros((n, GLA_CHUNK, LANES), F32)
            dkte = jnp.zeros((n, GLA_CHUNK, LANES), F32)
            ddec = jnp.zeros((n, 1, LANES), F32)
            for hh in range(2):
                h = 2 * pair + hh
                vcols = pl.ds(h * GLA_DV, GLA_DV)
                m = (lane_head == hh).astype(F32)
                qdh = (qd * m).astype(BF16)
                kteh = (kte * m).astype(BF16)
                vh = v_ref[:, vcols].reshape(n, GLA_CHUNK, GLA_DV).astype(BF16)
                doh = do_ref[:, vcols].reshape(n, GLA_CHUNK, GLA_DV).astype(BF16)
                stb = st_ref[:, h]
                stb_bf = stb.astype(BF16)
                sc_t = jnp.where(causal_t, _bnt(kd, qdh), 0.0).astype(BF16)
                dp = jnp.where(causal, _bnt(doh, vh), 0.0).astype(BF16)
                dp_t = jnp.where(causal_t, _bnt(vh, doh), 0.0).astype(BF16)
                a = _btn(doh, qdh)
                c = carry[h]
                for j in order:
                    dsa[j] = c
                    c = a[j] + dec[j] * c
                carry[h] = c
                dsa_f = dsa[...]
                dsa_bf = dsa_f.astype(BF16)
                dqd = dqd + (_bnn(dp, kd) * m + _bnn(doh, stb_bf))
                dkd = dkd + _bnn(dp_t, qdh)
                dkte = dkte + _bnn(vh, dsa_bf)
                ddec = ddec + jnp.sum(dsa_f * stb, axis=1, keepdims=True)
                dv_ref[:, vcols] = (_bnn(sc_t, doh) + _bnt(kteh, dsa_bf)).reshape(tg, GLA_DV)
            dq_ref[:, cols] = (dqd * (scale * eb)).reshape(tg, LANES)
            dk_ref[:, cols] = (dkd * emb + dkte * ekte).reshape(tg, LANES)
            db = dqd * qd - dkd * kdf - dkte * kte
            dblast = jnp.sum(dkte * kte, axis=1, keepdims=True) + ddec * dec
            dla = _chunk_cumsum(db.reshape(tg, LANES), row_in_chunk, not reverse) + jnp.broadcast_to(dblast, (n, GLA_CHUNK, LANES)).reshape(tg, LANES)
            dpre = (dla * (1.0 / GLA_TAU) * _sigmoid(-pre))
            dpre_bf = dpre.astype(BF16)
            dlr = dlr + _nt(dpre_bf, wd_ref[:, cols])
            dwd_ref[:, cols] += _tn(lr_bf, dpre_bf)
            dbd_ref[:, cols] += jnp.sum(dpre, axis=0, keepdims=True)
        dlr_ref[...] = dlr

    return pl.pallas_call(
        body,
        name="gla_bwd_rev" if reverse else "gla_bwd",
        grid=(nt,),
        in_specs=[
            pl.BlockSpec((tg, KEY_W), lambda i: (tile(i), COL_Q // KEY_W)),
            pl.BlockSpec((tg, KEY_W), lambda i: (tile(i), COL_K // KEY_W)),
            pl.BlockSpec((tg, GLA_W), lambda i: (tile(i), COL_V // GLA_W)),
            pl.BlockSpec((tg, LANES), lambda i: (tile(i), COL_LR // LANES)),
            pl.BlockSpec((tg, GLA_W), lambda i: (tile(i), 0)),
            pl.BlockSpec((n, GLA_HEADS, GLA_DV, LANES), lambda i: (tile(i), 0, 0, 0)),
            _resident((LANES, KEY_W)),
            _resident((1, KEY_W)),
        ],
        out_specs=[
            pl.BlockSpec((tg, KEY_W), lambda i: (tile(i), 0)),
            pl.BlockSpec((tg, KEY_W), lambda i: (tile(i), 0)),
            pl.BlockSpec((tg, GLA_W), lambda i: (tile(i), 0)),
            pl.BlockSpec((tg, LANES), lambda i: (tile(i), 0)),
            pl.BlockSpec((LANES, KEY_W), lambda i: (0, 0)),
            pl.BlockSpec((1, KEY_W), lambda i: (0, 0)),
        ],
        out_shape=[
            jax.ShapeDtypeStruct((seq, KEY_W), F32), jax.ShapeDtypeStruct((seq, KEY_W), F32),
            jax.ShapeDtypeStruct((seq, GLA_W), F32), jax.ShapeDtypeStruct((seq, LANES), F32),
            jax.ShapeDtypeStruct((LANES, KEY_W), F32), jax.ShapeDtypeStruct((1, KEY_W), F32),
        ],
        scratch_shapes=[pltpu.VMEM((GLA_HEADS, GLA_DV, LANES), F32), pltpu.VMEM((n, GLA_DV, LANES), F32)],
        compiler_params=_params(48),
    )(p, p, p, p, do, st, wd_pad, bd)


def _inproj_bwd(x, dx1, g1, w_in_t, dq_f, dq_b, dk_f, dk_b, dv_f, dv_b, dg, du, dvv, dlr_f, dlr_b):
    seq = x.shape[0]
    tm = min(seq, 256)

    def body(x_ref, dx1_ref, g1_ref, w_ref, dqf, dqb, dkf, dkb, dvf, dvb, dg_ref, du_ref, dvv_ref, dlrf, dlrb, dx_ref, dw_ref, dg1_ref, dp_sc):
        @pl.when(pl.program_id(0) == 0)
        def _():
            dw_ref[...] = jnp.zeros_like(dw_ref)
            dg1_ref[...] = jnp.zeros_like(dg1_ref)

        dp_sc[:, COL_Q : COL_Q + KEY_W] = (dqf[...] + dqb[...]).astype(BF16)
        dp_sc[:, COL_K : COL_K + KEY_W] = (dkf[...] + dkb[...]).astype(BF16)
        dp_sc[:, COL_V : COL_V + GLA_W] = (dvf[...] + dvb[...]).astype(BF16)
        dp_sc[:, COL_G : COL_G + GLA_W] = dg_ref[...].astype(BF16)
        dp_sc[:, COL_U : COL_U + GMLP_W] = du_ref[...].astype(BF16)
        dp_sc[:, COL_VV : COL_VV + GMLP_W] = dvv_ref[...].astype(BF16)
        dp_sc[:, COL_LR : COL_LR + LANES] = (dlrf[...] + dlrb[...]).astype(BF16)
        xv = x_ref[...]
        r1 = lax.rsqrt(jnp.mean(xv * xv, axis=-1, keepdims=True) + EPS)
        xh = xv * r1
        h = (xh * g1_ref[...]).astype(BF16)
        main, uv, lr = dp_sc[:, 0:COL_U], dp_sc[:, COL_U:COL_LR], dp_sc[:, COL_LR:PROJ_WP]
        dw_ref[0:ROW_LR, :] += _tn(main, h)
        dw_ref[ROW_UV:PROJ_W, :] += _tn(uv, h)
        dw_ref[ROW_LR:ROW_UV, :] += _tn(lr, h)[0 : ROW_UV - ROW_LR]
        dh = _nn(main, w_ref[0:ROW_LR, :]) + _nn(uv, w_ref[ROW_UV:PROJ_W, :]) + _nn(lr, w_ref[ROW_LR : ROW_LR + LANES, :])
        dg1_ref[...] += jnp.sum(dh * xh, axis=0, keepdims=True)
        dx_ref[...] = dx1_ref[...] + _rms_bwd(dh * g1_ref[...], xh, r1)

    row = lambda w: pl.BlockSpec((tm, w), lambda i: (i, 0))
    return pl.pallas_call(
        body,
        name="inproj_bwd",
        grid=(seq // tm,),
        in_specs=[
            row(D_MODEL), row(D_MODEL), _resident((1, D_MODEL)), _resident((PROJ_W, D_MODEL)),
            row(KEY_W), row(KEY_W), row(KEY_W), row(KEY_W), row(GLA_W), row(GLA_W),
            row(GLA_W), row(GMLP_W), row(GMLP_W), row(LANES), row(LANES),
        ],
        out_specs=[row(D_MODEL), pl.BlockSpec((PROJ_W, D_MODEL), lambda i: (0, 0)), pl.BlockSpec((1, D_MODEL), lambda i: (0, 0))],
        out_shape=[
            jax.ShapeDtypeStruct((seq, D_MODEL), F32),
            jax.ShapeDtypeStruct((PROJ_W, D_MODEL), F32),
            jax.ShapeDtypeStruct((1, D_MODEL), F32),
        ],
        scratch_shapes=[pltpu.VMEM((tm, PROJ_WP), BF16)],
        compiler_params=_params(56),
    )(x, dx1, g1, w_in_t, dq_f, dq_b, dk_f, dk_b, dv_f, dv_b, dg, du, dvv, dlr_f, dlr_b)


def _row_tile(rows, multiple=8):
    for t in range(min(rows, 512), 0, -1):
        if rows % t == 0 and t % multiple == 0:
            return t
    return rows


def _cast_into_slot(w, shard):
    rows, cols = w.shape
    tr = _row_tile(rows, 16)

    def body(s_ref, w_ref, o_ref):
        o_ref[...] = w_ref[...].astype(BF16)

    return pl.pallas_call(
        body,
        name="cast_into_slot",
        grid_spec=pltpu.PrefetchScalarGridSpec(
            num_scalar_prefetch=1,
            grid=(rows // tr,),
            in_specs=[pl.BlockSpec((tr, cols), lambda i, s_ref: (i, 0))],
            out_specs=pl.BlockSpec((None, tr, cols), lambda i, s_ref: (s_ref[0], i, 0)),
        ),
        out_shape=jax.ShapeDtypeStruct((N_SHARDS, rows, cols), BF16),
        compiler_params=_params(32, ("parallel",)),
    )(shard, w)


def _add_halves(g4, recv, c):
    _, rows, _ = g4.shape
    tr = _row_tile(rows, 16)

    def body(c_ref, g_ref, r_ref, o_ref, ob_ref):
        total = g_ref[...] + r_ref[...]
        o_ref[...] = total
        ob_ref[...] = total.astype(BF16)

    out = pl.BlockSpec((None, tr, HALF), lambda s, i, c_ref: (s, i, 0))
    return pl.pallas_call(
        body,
        name="add_halves",
        grid_spec=pltpu.PrefetchScalarGridSpec(
            num_scalar_prefetch=1,
            grid=(N_SHARDS, rows // tr),
            in_specs=[pl.BlockSpec((None, tr, HALF), lambda s, i, c_ref: (s, i, c_ref[0])), out],
            out_specs=[out, out],
        ),
        out_shape=[jax.ShapeDtypeStruct((N_SHARDS, rows, HALF), F32), jax.ShapeDtypeStruct((N_SHARDS, rows, HALF), BF16)],
        compiler_params=_params(32, ("parallel", "parallel")),
    )(c, g4, recv)


def _add_partials(part4, recv3, shard_core):
    _, rows, _ = part4.shape
    tr = _row_tile(rows, 16)

    def body(sc_ref, p_ref, r_ref, o_ref):
        o_ref[...] = ((p_ref[...] + r_ref[0].astype(F32)) + r_ref[1].astype(F32)) + r_ref[2].astype(F32)

    return pl.pallas_call(
        body,
        name="add_partials",
        grid_spec=pltpu.PrefetchScalarGridSpec(
            num_scalar_prefetch=1,
            grid=(rows // tr,),
            in_specs=[
                pl.BlockSpec((None, tr, HALF), lambda i, sc_ref: (sc_ref[0], i, 0)),
                pl.BlockSpec((3, tr, HALF), lambda i, sc_ref: (0, i, 0)),
            ],
            out_specs=pl.BlockSpec((tr, HALF), lambda i, sc_ref: (i, sc_ref[1])),
        ),
        out_shape=jax.ShapeDtypeStruct((rows, 2 * HALF), F32),
        compiler_params=_params(32, ("parallel",)),
    )(shard_core, part4, recv3)


def _adam_math(w, g, m, v):
    m = ADAM_B1 * m + (1.0 - ADAM_B1) * g
    v = ADAM_B2 * v + (1.0 - ADAM_B2) * (g * g)
    m_hat = m / (1.0 - ADAM_B1**ADAM_STEP)
    v_hat = v / (1.0 - ADAM_B2**ADAM_STEP)
    delta = -ADAM_LR * (m_hat / (jnp.sqrt(v_hat) + ADAM_EPS) + ADAM_WD * w)
    return delta, m, v


def _adamw(w, g, m, v):
    rows, cols = w.shape
    tr = _row_tile(rows)

    def body(w_ref, g_ref, m_ref, v_ref, go_ref, d_ref, mo_ref, vo_ref):
        gv = g_ref[...]
        go_ref[...] = gv
        d_ref[...], mo_ref[...], vo_ref[...] = _adam_math(w_ref[...], gv, m_ref[...], v_ref[...])

    spec = pl.BlockSpec((tr, cols), lambda i: (i, 0))
    shape = jax.ShapeDtypeStruct(w.shape, F32)
    return pl.pallas_call(
        body, name="adamw", grid=(rows // tr,), in_specs=[spec] * 4, out_specs=[spec] * 4, out_shape=[shape] * 4,
        compiler_params=_params(32, ("parallel",)),
    )(w, g, m, v)


SMALL_ROWS = 560
DECAY_ROWS = 8
SMALL_TOTAL = SMALL_ROWS + 2 * N_SHARDS * DECAY_ROWS


def _adamw_small(gathered, wp, mp, vp):
    out_rows = SMALL_ROWS + 2 * DECAY_ROWS

    def body(ga_ref, w_ref, m_ref, v_ref, g_ref, d_ref, mo_ref, vo_ref):
        shard = 2 * lax.axis_index("x") + lax.axis_index("y")
        g_ref[pl.ds(0, SMALL_ROWS), :] = functools.reduce(lambda a, b: a + b, [ga_ref[d, pl.ds(0, SMALL_ROWS), :] for d in range(8)])
        for k in range(2):
            start = pl.multiple_of(SMALL_ROWS + k * N_SHARDS * DECAY_ROWS + shard * DECAY_ROWS, DECAY_ROWS)
            g_ref[pl.ds(SMALL_ROWS + k * DECAY_ROWS, DECAY_ROWS), :] = functools.reduce(
                lambda a, b: a + b, [ga_ref[d, pl.ds(start, DECAY_ROWS), :] for d in range(8)])
        d_ref[...], mo_ref[...], vo_ref[...] = _adam_math(w_ref[...], g_ref[...], m_ref[...], v_ref[...])

    shape = jax.ShapeDtypeStruct((out_rows, LANES), F32)
    return pl.pallas_call(body, name="adamw_small", out_shape=[shape] * 4, compiler_params=_params(32, None))(gathered, wp, mp, vp)


ANY = pl.BlockSpec(memory_space=pl.ANY)


def _position():
    return lax.axis_index("x"), lax.axis_index("y"), lax.axis_index("c")


def _other_chips(x, y):
    return [(1 - x, y), (x, 1 - y), (1 - x, 1 - y)]


HBM = pl.BlockSpec(memory_space=pltpu.HBM)
SEM = pl.BlockSpec(memory_space=pltpu.SEMAPHORE)
TOKEN = jax.ShapeDtypeStruct(TOKEN_SHAPE, F32)
DATAFLOW = pltpu.SideEffectType.DATAFLOW_SIDE_EFFECTING


def _half_block(ref4, slot, core):
    return ref4.at[slot, :, pl.ds(pl.multiple_of(core * HALF, HALF), HALF)]


def _gather_ici_copies(refs4, send_sems, recv_sems, stride):
    x, y, c = _position()
    pairs = []
    for k, ref4 in enumerate(refs4):
        mine = _half_block(ref4, 2 * x + y, c)
        for j, (px, py) in enumerate(_other_chips(x, y)):
            sems = dict(send_sem=send_sems.at[stride * k + j], recv_sem=recv_sems.at[stride * k + j], device_id=(px, py, c), device_id_type=MESH)
            pairs.append((functools.partial(pltpu.make_async_remote_copy, src_ref=mine, dst_ref=mine, **sems),
                          functools.partial(pltpu.make_async_remote_copy, src_ref=mine, dst_ref=_half_block(ref4, 2 * px + py, c), **sems)))
    return pairs


def _gather_d2d_copies(refs4, send_sems, recv_sems, stride, offset):
    x, y, c = _position()
    pairs = []
    for k, ref4 in enumerate(refs4):
        for j, (px, py) in enumerate(_other_chips(x, y)):
            have = _half_block(ref4, 2 * px + py, c)
            sems = dict(send_sem=send_sems.at[stride * k + offset + j], recv_sem=recv_sems.at[stride * k + offset + j],
                        device_id=(x, y, 1 - c), device_id_type=MESH)
            pairs.append((functools.partial(pltpu.make_async_remote_copy, src_ref=have, dst_ref=have, **sems),
                          functools.partial(pltpu.make_async_remote_copy, src_ref=have, dst_ref=_half_block(ref4, 2 * px + py, 1 - c), **sems)))
    return pairs


def _gather_sync(bufs):
    n = len(bufs)

    def body(*refs):
        outs = refs[n : 2 * n]
        send_sems, recv_sems = refs[2 * n :]
        ici = _gather_ici_copies(outs, send_sems, recv_sems, 6)
        d2d = _gather_d2d_copies(outs, send_sems, recv_sems, 6, 3)
        for send, _ in ici:
            send().start()
        for (_, arrival), (forward, _) in zip(ici, d2d):
            arrival().wait_recv()
            forward().start()
        for _, arrival in d2d:
            arrival().wait_recv()
        for send, _ in ici + d2d:
            send().wait_send()

    return pl.pallas_call(
        body,
        name="gather_sync",
        in_specs=[ANY] * n,
        out_specs=[ANY] * n,
        out_shape=[jax.ShapeDtypeStruct(b.shape, b.dtype) for b in bufs],
        input_output_aliases={k: k for k in range(n)},
        scratch_shapes=[pltpu.SemaphoreType.DMA((6 * n,)), pltpu.SemaphoreType.DMA((6 * n,))],
        compiler_params=pltpu.CompilerParams(has_side_effects=True),
    )(*bufs)


def _gather_start(bufs, after):
    n = len(bufs)

    def body(*refs):
        ins = refs[:n]
        send_sems, recv_sems = refs[n + 1], refs[n + 2]
        token = refs[2 * n + 3]
        for send, _ in _gather_ici_copies(ins, send_sems, recv_sems, 3):
            send().start()
        token[...] = jnp.zeros_like(token)

    out = pl.pallas_call(
        body,
        name="gather_start",
        in_specs=[HBM] * n + [ANY],
        out_specs=(SEM, SEM, *[HBM] * n, pl.BlockSpec(memory_space=pltpu.VMEM)),
        out_shape=(pltpu.SemaphoreType.DMA((3 * n,)), pltpu.SemaphoreType.DMA((3 * n,)), *[pltpu.HBM(b.shape, b.dtype) for b in bufs], TOKEN),
        input_output_aliases={k: 2 + k for k in range(n)},
        compiler_params=pltpu.CompilerParams(has_side_effects=DATAFLOW),
    )(*[pltpu.with_memory_space_constraint(b, pltpu.HBM) for b in bufs], after)
    return out[0], out[1], list(out[2 : 2 + n]), out[2 + n]


def _gather_wait(send_sems, recv_sems, bufs, after):
    n = len(bufs)

    def body(*refs):
        ins = refs[:n]
        for send, arrival in _gather_ici_copies(ins, refs[n], refs[n + 1], 3):
            send().wait_send()
            arrival().wait_recv()

    return pl.pallas_call(
        body,
        name="gather_wait",
        in_specs=[HBM] * n + [SEM, SEM] + [ANY] * len(after),
        out_specs=tuple([HBM] * n),
        out_shape=tuple(pltpu.HBM(b.shape, b.dtype) for b in bufs),
        input_output_aliases={k: k for k in range(n)},
        compiler_params=pltpu.CompilerParams(has_side_effects=DATAFLOW),
    )(*bufs, send_sems, recv_sems, *after)


def _gather_forward(bufs):
    n = len(bufs)

    def body(*refs):
        outs = refs[n : 2 * n]
        send_sems, recv_sems = refs[2 * n :]
        d2d = _gather_d2d_copies(outs, send_sems, recv_sems, 3, 0)
        for forward, _ in d2d:
            forward().start()
        for forward, arrival in d2d:
            arrival().wait_recv()
            forward().wait_send()

    return pl.pallas_call(
        body,
        name="gather_forward",
        in_specs=[ANY] * n,
        out_specs=[ANY] * n,
        out_shape=[jax.ShapeDtypeStruct(b.shape, b.dtype) for b in bufs],
        input_output_aliases={k: k for k in range(n)},
        scratch_shapes=[pltpu.SemaphoreType.DMA((3 * n,)), pltpu.SemaphoreType.DMA((3 * n,))],
        compiler_params=pltpu.CompilerParams(has_side_effects=True),
    )(*bufs)


def _exchange_halves(grads4):
    n = len(grads4)

    def body(*refs):
        ins, outs = refs[:n], refs[n : 2 * n]
        send_sems, recv_sems = refs[2 * n :]
        x, y, c = _position()
        copies = []
        for k in range(n):
            cp = pltpu.make_async_remote_copy(
                src_ref=ins[k].at[:, :, pl.ds(pl.multiple_of((1 - c) * HALF, HALF), HALF)], dst_ref=outs[k],
                send_sem=send_sems.at[k], recv_sem=recv_sems.at[k], device_id=(x, y, 1 - c), device_id_type=MESH)
            cp.start()
            copies.append(cp)
        for cp in copies:
            cp.wait()

    return pl.pallas_call(
        body,
        name="exchange_halves",
        in_specs=[ANY] * n,
        out_specs=[ANY] * n,
        out_shape=[jax.ShapeDtypeStruct((N_SHARDS, g.shape[1], HALF), g.dtype) for g in grads4],
        scratch_shapes=[pltpu.SemaphoreType.DMA((n,)), pltpu.SemaphoreType.DMA((n,))],
        compiler_params=pltpu.CompilerParams(has_side_effects=True),
    )(*grads4)


def _scatter_partials(parts4):
    n = len(parts4)

    def body(*refs):
        ins, outs = refs[:n], refs[n : 2 * n]
        send_sems, recv_sems = refs[2 * n :]
        x, y, c = _position()
        copies = []
        for k in range(n):
            for j, (px, py) in enumerate(_other_chips(x, y)):
                cp = pltpu.make_async_remote_copy(
                    src_ref=ins[k].at[2 * px + py], dst_ref=outs[k].at[j],
                    send_sem=send_sems.at[3 * k + j], recv_sem=recv_sems.at[3 * k + j], device_id=(px, py, c), device_id_type=MESH)
                cp.start()
                copies.append(cp)
        for cp in copies:
            cp.wait()

    return pl.pallas_call(
        body,
        name="scatter_partials",
        in_specs=[ANY] * n,
        out_specs=[ANY] * n,
        out_shape=[jax.ShapeDtypeStruct((3,) + g.shape[1:], g.dtype) for g in parts4],
        scratch_shapes=[pltpu.SemaphoreType.DMA((3 * n,)), pltpu.SemaphoreType.DMA((3 * n,))],
        compiler_params=pltpu.CompilerParams(has_side_effects=True),
    )(*parts4)


def _scatter_copies(parts, lands, send_sems, recv_sems):
    x, y, c = _position()
    copies = []
    for k in range(len(parts)):
        for j, (px, py) in enumerate(_other_chips(x, y)):
            copies.append(pltpu.make_async_remote_copy(
                src_ref=parts[k].at[2 * px + py], dst_ref=lands[k].at[j],
                send_sem=send_sems.at[3 * k + j], recv_sem=recv_sems.at[3 * k + j], device_id=(px, py, c), device_id_type=MESH))
    return copies


def _scatter_start(parts4):
    n = len(parts4)
    lands = [lax.empty((3,) + g.shape[1:], g.dtype) for g in parts4]

    def body(*refs):
        ins, land_in = refs[:n], refs[n : 2 * n]
        send_sems, recv_sems = refs[2 * n], refs[2 * n + 1]
        token = refs[4 * n + 2]
        for cp in _scatter_copies(ins, land_in, send_sems, recv_sems):
            cp.start()
        token[...] = jnp.zeros_like(token)

    hbm = lambda a: pltpu.HBM(a.shape, a.dtype)
    out = pl.pallas_call(
        body,
        name="scatter_start",
        in_specs=[HBM] * (2 * n),
        out_specs=(SEM, SEM, *[HBM] * (2 * n), pl.BlockSpec(memory_space=pltpu.VMEM)),
        out_shape=(pltpu.SemaphoreType.DMA((3 * n,)), pltpu.SemaphoreType.DMA((3 * n,)), *[hbm(a) for a in parts4], *[hbm(a) for a in lands], TOKEN),
        input_output_aliases={k: 2 + k for k in range(2 * n)},
        compiler_params=pltpu.CompilerParams(has_side_effects=DATAFLOW),
    )(*[pltpu.with_memory_space_constraint(a, pltpu.HBM) for a in parts4 + lands])
    return out[0], out[1], list(out[2 : 2 + n]), list(out[2 + n : 2 + 2 * n]), out[2 + 2 * n]


def _scatter_wait(send_sems, recv_sems, parts4, lands, after):
    n = len(parts4)

    def body(*refs):
        for cp in _scatter_copies(refs[:n], refs[n : 2 * n], refs[2 * n], refs[2 * n + 1]):
            cp.wait_send()
            cp.wait_recv()

    hbm = lambda a: pltpu.HBM(a.shape, a.dtype)
    out = pl.pallas_call(
        body,
        name="scatter_wait",
        in_specs=[HBM] * (2 * n) + [SEM, SEM] + [ANY] * len(after),
        out_specs=tuple([HBM] * (2 * n)),
        out_shape=tuple(hbm(a) for a in parts4 + lands),
        input_output_aliases={k: k for k in range(2 * n)},
        compiler_params=pltpu.CompilerParams(has_side_effects=DATAFLOW),
    )(*parts4, *lands, send_sems, recv_sems, *after)
    return list(out[n:])


def _join_halves(bufs):
    n = len(bufs)

    def body(*refs):
        outs = refs[n : 2 * n]
        send_sems, recv_sems = refs[2 * n :]
        x, y, c = _position()
        half = lambda ref, core: ref.at[:, pl.ds(pl.multiple_of(core * HALF, HALF), HALF)]
        for k in range(n):
            mine = half(outs[k], c)
            pltpu.make_async_remote_copy(
                src_ref=mine, dst_ref=mine, send_sem=send_sems.at[k], recv_sem=recv_sems.at[k],
                device_id=(x, y, 1 - c), device_id_type=MESH).start()
        for k in range(n):
            wait = pltpu.make_async_remote_copy(
                src_ref=half(outs[k], c), dst_ref=half(outs[k], 1 - c), send_sem=send_sems.at[k], recv_sem=recv_sems.at[k],
                device_id=(x, y, 1 - c), device_id_type=MESH)
            wait.wait_send()
            wait.wait_recv()

    return pl.pallas_call(
        body,
        name="join_halves",
        in_specs=[ANY] * n,
        out_specs=[ANY] * n,
        out_shape=[jax.ShapeDtypeStruct(b.shape, b.dtype) for b in bufs],
        input_output_aliases={k: k for k in range(n)},
        scratch_shapes=[pltpu.SemaphoreType.DMA((n,)), pltpu.SemaphoreType.DMA((n,))],
        compiler_params=pltpu.CompilerParams(has_side_effects=True),
    )(*bufs)


def _allgather_small(block):
    m_per, ncol = block.shape

    def body(x_ref, out_ref, send_sems, recv_sems, local_sem):
        x, y, c = _position()
        me, sibling = (x, y, c), (x, y, 1 - c)
        chips = _other_chips(x, y)

        def rows(px, py, pc):
            return out_ref.at[4 * px + 2 * py + pc]

        def copy(k, blk, to, src=None):
            return pltpu.make_async_remote_copy(
                src_ref=rows(*blk) if src is None else src, dst_ref=rows(*blk),
                send_sem=send_sems.at[k], recv_sem=recv_sems.at[k], device_id=to, device_id_type=MESH)

        mine = pltpu.make_async_copy(x_ref, rows(*me), local_sem)
        mine.start()
        first = [copy(0, me, sibling, src=x_ref)] + [copy(1 + j, me, (*chip, c), src=x_ref) for j, chip in enumerate(chips)]
        for cp in first:
            cp.start()
        passed = [copy(4 + j, (*chip, c), sibling) for j, chip in enumerate(chips)]
        for j, chip in enumerate(chips):
            copy(1 + j, (*chip, c), me).wait_recv()
            passed[j].start()
        copy(0, sibling, me).wait_recv()
        for j, chip in enumerate(chips):
            copy(4 + j, (*chip, 1 - c), me).wait_recv()
        for cp in first + passed:
            cp.wait_send()
        mine.wait()

    return pl.pallas_call(
        body,
        name="allgather_small",
        in_specs=[pl.BlockSpec(memory_space=pltpu.VMEM)],
        out_specs=pl.BlockSpec(memory_space=pltpu.VMEM),
        out_shape=jax.ShapeDtypeStruct((8, m_per, ncol), block.dtype),
        scratch_shapes=[pltpu.SemaphoreType.DMA((7,)), pltpu.SemaphoreType.DMA((7,)), pltpu.SemaphoreType.DMA],
        compiler_params=pltpu.CompilerParams(has_side_effects=True, vmem_limit_bytes=32 * MIB),
    )(block)


SMALL_NAMES = ["norm1_g", "b_decay_f", "b_decay_b", "gla_norm_g", "gmlp_ln_g", "gmlp_ln_b", "w_spatial", "b_spatial", "norm2_g", "final_norm_g"]


def _pack_small(parts, decay_parts):
    flat = jnp.concatenate([a.reshape(-1) for a in parts])
    flat = jnp.pad(flat, (0, SMALL_ROWS * LANES - flat.shape[0])).reshape(SMALL_ROWS, LANES)
    return jnp.concatenate([flat] + [d.reshape(-1, LANES) for d in decay_parts], axis=0)


def _unpack_small(packed, like):
    out, off = [], 0
    flat = packed[:SMALL_ROWS].reshape(-1)
    for a in like:
        out.append(flat[off : off + a.size].reshape(a.shape))
        off += a.size
    return out


def kernel(x, norm1_g, w_in, w_decay_f, b_decay_f, w_decay_b, b_decay_b, gla_norm_g, gmlp_ln_g, gmlp_ln_b, w_spatial, b_spatial, w_out, norm2_g, w_gate, w_up, w_down, final_norm_g, loss_target, m_norm1_g, m_w_in, m_w_decay_f, m_b_decay_f, m_w_decay_b, m_b_decay_b, m_gla_norm_g, m_gmlp_ln_g, m_gmlp_ln_b, m_w_spatial, m_b_spatial, m_w_out, m_norm2_g, m_w_gate, m_w_up, m_w_down, m_final_norm_g, v_norm1_g, v_w_in, v_w_decay_f, v_b_decay_f, v_w_decay_b, v_b_decay_b, v_gla_norm_g, v_gmlp_ln_g, v_gmlp_ln_b, v_w_spatial, v_b_spatial, v_w_out, v_norm2_g, v_w_gate, v_w_up, v_w_down, v_final_norm_g):
    args = dict(locals())
    cx, cy, cc = lax.axis_index("x"), lax.axis_index("y"), lax.axis_index("c")
    shard = 2 * cx + cy
    xs = x[0]
    target = loss_target[0]

    big_names = ["w_in", "w_out", "w_gate", "w_up", "w_down"]
    transposed = ("w_in", "w_gate", "w_up")
    rows_of = lambda pre, k: jnp.transpose(args[pre + k][0]) if k in transposed else args[pre + k][0]
    big_shards = {k: rows_of("", k) for k in big_names}
    c_arr = cc.reshape(1).astype(jnp.int32)
    s_arr = shard.reshape(1).astype(jnp.int32)
    sc_arr = jnp.stack([shard, cc]).astype(jnp.int32)
    slots = {k: _cast_into_slot(big_shards[k], s_arr) for k in big_names}
    (w_in4,) = _gather_sync([slots["w_in"]])
    late = ["w_out", "w_gate", "w_up", "w_down"]
    g_send, g_recv, late_bufs, token_gather = _gather_start([slots[k] for k in late], w_in4)
    w_in_t = w_in4.reshape(PROJ_W, D_MODEL)

    dec_block = jnp.concatenate([w_decay_f[0].reshape(-1, LANES), w_decay_b[0].reshape(-1, LANES)], axis=0)
    dec_all = _allgather_small(dec_block)
    dec_all = dec_all[::2].reshape(N_SHARDS, 2, LOWRANK, KEY_W // N_SHARDS)
    wdf_full = jnp.transpose(dec_all[:, 0], (1, 0, 2)).reshape(LOWRANK, KEY_W)
    wdb_full = jnp.transpose(dec_all[:, 1], (1, 0, 2)).reshape(LOWRANK, KEY_W)
    wd_pad_f = jnp.zeros((LANES, KEY_W), F32).at[0:LOWRANK].set(wdf_full).astype(BF16)
    wd_pad_b = jnp.zeros((LANES, KEY_W), F32).at[LOWRANK : 2 * LOWRANK].set(wdb_full).astype(BF16)

    ws_bf = w_spatial[0].astype(BF16)
    wst_bf = jnp.transpose(w_spatial[0], (0, 2, 1)).astype(BF16)
    bs_col = b_spatial[0].reshape(GMLP_GROUPS, GMLP_CHUNK, 1)

    p = _inproj(xs, norm1_g, w_in_t, token_gather)
    o_f, st_f = _gla_fwd(p, wd_pad_f, b_decay_f, reverse=False)
    o_b, st_b = _gla_fwd(p, wd_pad_b, b_decay_b, reverse=True)
    late_bufs = _gather_forward(_gather_wait(g_send, g_recv, late_bufs, (o_f, o_b)))
    w_out4, w_gate4, w_up4, w_down4 = late_bufs
    w_out_full = w_out4.reshape(D_MODEL, D_MODEL)
    x1, ycat = _mixer_out(xs, o_f, o_b, p, gla_norm_g, gmlp_ln_g, gmlp_ln_b, ws_bf, bs_col, w_out_full)
    gf = final_norm_g.reshape(1, D_MODEL)
    h2, gate4, up4, act4, dx2, loss_acc, dgf = _ffn_fwd(x1, target, norm2_g, gf, w_gate4, w_up4, w_down4)

    dgate4, dup4, dx1, dg2 = _ffn_bwd(dx2, gate4, up4, x1, norm2_g, w_gate4, w_up4, w_down4)
    dwg4, dwu4, dwd4 = _ffn_wgrad(h2, dgate4, dup4, act4, dx2)
    ffn_grads4 = [dwg4, dwu4, dwd4]
    ffn_parts = [_add_halves(g, r, c_arr) for g, r in zip(ffn_grads4, _exchange_halves(ffn_grads4))]
    s_send, s_recv, s_parts, s_lands, token_scatter = _scatter_start([pb for _, pb in ffn_parts])
    do, dg, du, dvv, dwo, dgn, dlng, dlnb, dws, dbs = _mixer_bwd(
        dx1, ycat, o_f, o_b, p, gla_norm_g, gmlp_ln_g, gmlp_ln_b, ws_bf, wst_bf, bs_col, w_out_full, token_scatter)
    dq_f, dk_f, dv_f, dlr_f, dwdec_f, dbdec_f = _gla_bwd(p, do, st_f, wd_pad_f, b_decay_f, reverse=False)
    dq_b, dk_b, dv_b, dlr_b, dwdec_b, dbdec_b = _gla_bwd(p, do, st_b, wd_pad_b, b_decay_b, reverse=True)
    dx, dwin_t, dg1 = _inproj_bwd(xs, dx1, norm1_g, w_in_t, dq_f, dq_b, dk_f, dk_b, dv_f, dv_b, dg, du, dvv, dlr_f, dlr_b)
    ffn_recv = _scatter_wait(s_send, s_recv, s_parts, s_lands, (dwin_t,))

    dwin4 = dwin_t.reshape(N_SHARDS, PROJ_W // N_SHARDS, D_MODEL)
    dwo4 = dwo.reshape(N_SHARDS, D_MODEL // N_SHARDS, D_MODEL)
    proj_grads4 = [dwin4, dwo4]
    proj_parts = [_add_halves(g, r, c_arr) for g, r in zip(proj_grads4, _exchange_halves(proj_grads4))]
    proj_recv = _scatter_partials([pb for _, pb in proj_parts])
    parts_f32 = [pf for pf, _ in proj_parts + ffn_parts]
    bufs = [_add_partials(pf, r, sc_arr) for pf, r in zip(parts_f32, list(proj_recv) + ffn_recv)]
    big_grads = dict(zip(big_names, _join_halves(bufs)))

    dwdec_f16 = dwdec_f[0:LOWRANK]
    dwdec_b16 = dwdec_b[LOWRANK : 2 * LOWRANK]
    shard_major = lambda a: jnp.transpose(a.reshape(LOWRANK, N_SHARDS, KEY_W // N_SHARDS), (1, 0, 2))
    small_grads = {
        "norm1_g": dg1, "b_decay_f": dbdec_f, "b_decay_b": dbdec_b, "gla_norm_g": dgn, "gmlp_ln_g": dlng, "gmlp_ln_b": dlnb,
        "w_spatial": dws, "b_spatial": dbs, "norm2_g": dg2, "final_norm_g": dgf,
    }
    g_pack = _pack_small([small_grads[k] for k in SMALL_NAMES], [shard_major(dwdec_f16), shard_major(dwdec_b16)])
    g_all = _allgather_small(g_pack)
    pack_own = lambda pre: _pack_small([args[pre + k] for k in SMALL_NAMES], [args[pre + "w_decay_f"], args[pre + "w_decay_b"]])
    sg, sd, sm, sv = _adamw_small(g_all, pack_own(""), pack_own("m_"), pack_own("v_"))

    names = ["norm1_g", "w_in", "w_decay_f", "b_decay_f", "w_decay_b", "b_decay_b", "gla_norm_g", "gmlp_ln_g", "gmlp_ln_b",
             "w_spatial", "b_spatial", "w_out", "norm2_g", "w_gate", "w_up", "w_down", "final_norm_g"]
    like = [args[k] for k in SMALL_NAMES]
    results = {"g": {}, "d": {}, "m": {}, "v": {}}
    for tag, packed in (("g", sg), ("d", sd), ("m", sm), ("v", sv)):
        for k, a in zip(SMALL_NAMES, _unpack_small(packed, like)):
            results[tag][k] = a
        results[tag]["w_decay_f"] = packed[SMALL_ROWS : SMALL_ROWS + DECAY_ROWS].reshape(w_decay_f.shape)
        results[tag]["w_decay_b"] = packed[SMALL_ROWS + DECAY_ROWS :].reshape(w_decay_b.shape)
    for k in big_names:
        g, d, mo, vo = _adamw(big_shards[k], big_grads[k], rows_of("m_", k), rows_of("v_", k))
        for tag, a in (("g", g), ("d", d), ("m", mo), ("v", vo)):
            results[tag][k] = (jnp.transpose(a) if k in transposed else a).reshape(args[k].shape)

    loss = lax.psum(loss_acc[0, 0], ("x", "y", "c"))
    grad_x = dx.reshape(x.shape)
    return (loss, grad_x, *[results["g"][k] for k in names], *[results["d"][k] for k in names],
            *[results["m"][k] for k in names], *[results["v"][k] for k in names])
```

```python
import functools
import math

import jax
import jax.numpy as jnp
from jax import lax
from jax.experimental import pallas as pl
from jax.experimental.pallas import tpu as pltpu

F32, BF16 = jnp.float32, jnp.bfloat16

D_MODEL = 1024
GLA_HEADS = 4
GLA_DK = 64
GLA_DV = 128
KEY_W = GLA_HEADS * GLA_DK
GLA_W = GLA_HEADS * GLA_DV
GMLP_W = 512
GMLP_GROUPS = 4
GMLP_CHUNK = 128
LOWRANK = 16
GLA_CHUNK = 64
GLA_TAU = 16.0
PROJ_W = 2592
PROJ_WP = 2688
D_FF = 2816
N_SHARDS = 4
FF_SHARD = D_FF // N_SHARDS
EPS = 1e-6
LANES = 128
TOKEN_SHAPE = (8, LANES)
MIB = 1024 * 1024

ADAM_LR = 0.001
ADAM_B1 = 0.9
ADAM_B2 = 0.999
ADAM_EPS = 1e-08
ADAM_WD = 0.01
ADAM_STEP = 10

COL_Q, COL_K = 0, 256
COL_V, COL_G, COL_U, COL_VV = 512, 1024, 1536, 2048
COL_LR = 2560
ROW_LR, ROW_UV = 1536, 1568
HALF = D_MODEL // 2

MESH = pl.DeviceIdType.MESH


def _nn(a, b):
    return jnp.dot(a, b, preferred_element_type=F32)


def _nt(a, b):
    return lax.dot_general(a, b, (((1,), (1,)), ((), ())), preferred_element_type=F32)


def _tn(a, b):
    return lax.dot_general(a, b, (((0,), (0,)), ((), ())), preferred_element_type=F32)


def _bnn(a, b):
    return jnp.einsum("nik,nkj->nij", a, b, preferred_element_type=F32)


def _bnt(a, b):
    return jnp.einsum("nik,njk->nij", a, b, preferred_element_type=F32)


def _btn(a, b):
    return jnp.einsum("nki,nkj->nij", a, b, preferred_element_type=F32)


def _resident(shape):
    zeros = (0,) * len(shape)
    return pl.BlockSpec(shape, lambda *_: zeros, pipeline_mode=pl.Buffered(1))


def _params(vmem_mib, semantics=("arbitrary",)):
    return pltpu.CompilerParams(vmem_limit_bytes=vmem_mib * MIB, dimension_semantics=semantics)


def _sigmoid(x):
    return 1.0 / (1.0 + jnp.exp(-x))


def _gelu(x):
    return 0.5 * x * (1.0 + lax.erf(x * (1.0 / math.sqrt(2.0))))


def _gelu_grad(x):
    return 0.5 * (1.0 + lax.erf(x * (1.0 / math.sqrt(2.0)))) + x * jnp.exp(-0.5 * x * x) * (1.0 / math.sqrt(2.0 * math.pi))


def _log_sigmoid(x):
    return jnp.minimum(x, 0.0) - jnp.log(1.0 + jnp.exp(-jnp.abs(x)))


def _rms_bwd(dxh, xh, r):
    return r * (dxh - xh * jnp.mean(dxh * xh, axis=-1, keepdims=True))


def _chunk_cumsum(v, row_in_chunk, reverse):
    rows = v.shape[0]
    for sh in (1, 2, 4, 8, 16, 32):
        if reverse:
            v = v + jnp.where(row_in_chunk + sh < GLA_CHUNK, pltpu.roll(v, rows - sh, axis=0), 0.0)
        else:
            v = v + jnp.where(row_in_chunk >= sh, pltpu.roll(v, sh, axis=0), 0.0)
    return v


def _inproj(x, g1, w_in_t, token):
    seq = x.shape[0]
    tm = min(seq, 512)

    def body(x_ref, g_ref, w_ref, token_ref, p_ref):
        xv = x_ref[...]
        r = lax.rsqrt(jnp.mean(xv * xv, axis=-1, keepdims=True) + EPS)
        h = (xv * r * g_ref[...]).astype(BF16)
        p_ref[:, 0:COL_U] = _nt(h, w_ref[0:ROW_LR, :])
        p_ref[:, COL_U:COL_LR] = _nt(h, w_ref[ROW_UV:PROJ_W, :])
        p_ref[:, COL_LR:PROJ_WP] = _nt(h, w_ref[ROW_LR : ROW_LR + LANES, :])

    return pl.pallas_call(
        body,
        name="inproj",
        grid=(seq // tm,),
        in_specs=[pl.BlockSpec((tm, D_MODEL), lambda i: (i, 0)), _resident((1, D_MODEL)), _resident((PROJ_W, D_MODEL)), _resident(TOKEN_SHAPE)],
        out_specs=pl.BlockSpec((tm, PROJ_WP), lambda i: (i, 0)),
        out_shape=jax.ShapeDtypeStruct((seq, PROJ_WP), F32),
        compiler_params=_params(48, ("parallel",)),
    )(x, g1, w_in_t, token)


def _gla_tile(seq):
    return min(seq, 512)


def _gla_decay_terms(lr_bf, wd_ref, bd_ref, pair, row_in_chunk, reverse, n):
    cols = pl.ds(pair * LANES, LANES)
    pre = _nn(lr_bf, wd_ref[:, cols]) + bd_ref[:, cols]
    la = _log_sigmoid(pre) * (1.0 / GLA_TAU)
    b = _chunk_cumsum(la, row_in_chunk, reverse)
    b3 = b.reshape(n, GLA_CHUNK, LANES)
    blast = b3[:, 0:1, :] if reverse else b3[:, GLA_CHUNK - 1 : GLA_CHUNK, :]
    return pre, b3, blast


def _gla_fwd(p, wd_pad, bd, reverse):
    seq = p.shape[0]
    tg = _gla_tile(seq)
    nt = seq // tg
    n = tg // GLA_CHUNK
    scale = GLA_DK**-0.5

    def tile(i):
        return nt - 1 - i if reverse else i

    def body(q_ref, k_ref, v_ref, lr_ref, wd_ref, bd_ref, o_ref, st_ref, carry):
        @pl.when(pl.program_id(0) == 0)
        def _():
            carry[...] = jnp.zeros_like(carry)

        lr_bf = lr_ref[...].astype(BF16)
        row_in_chunk = lax.broadcasted_iota(jnp.int32, (tg, LANES), 0) % GLA_CHUNK
        lane_head = lax.broadcasted_iota(jnp.int32, (1, LANES), 1) // GLA_DK
        tt = lax.broadcasted_iota(jnp.int32, (GLA_CHUNK, GLA_CHUNK), 0)
        ss = lax.broadcasted_iota(jnp.int32, (GLA_CHUNK, GLA_CHUNK), 1)
        causal = (tt <= ss) if reverse else (tt >= ss)
        order = range(n - 1, -1, -1) if reverse else range(n)
        for pair in range(2):
            cols = pl.ds(pair * LANES, LANES)
            _, b3, blast = _gla_decay_terms(lr_bf, wd_ref, bd_ref, pair, row_in_chunk, reverse, n)
            q3 = q_ref[:, cols].reshape(n, GLA_CHUNK, LANES) * scale
            k3 = k_ref[:, cols].reshape(n, GLA_CHUNK, LANES)
            qd = q3 * jnp.exp(b3)
            kd = (k3 * jnp.exp(-b3)).astype(BF16)
            kte = k3 * jnp.exp(blast - b3)
            dec = jnp.exp(blast)
            for hh in range(2):
                h = 2 * pair + hh
                m = (lane_head == hh).astype(F32)
                qdh = (qd * m).astype(BF16)
                kteh = (kte * m).astype(BF16)
                vh = v_ref[:, pl.ds(h * GLA_DV, GLA_DV)].reshape(n, GLA_CHUNK, GLA_DV).astype(BF16)
                sc = jnp.where(causal, _bnt(qdh, kd), 0.0)
                o_intra = _bnn(sc.astype(BF16), vh)
                dst = _btn(vh, kteh)
                st = carry[h]
                for j in order:
                    st_ref[j, h] = st
                    st = st * dec[j] + dst[j]
                carry[h] = st
                o_inter = _bnt(qdh, st_ref[:, h].astype(BF16))
                o_ref[:, pl.ds(h * GLA_DV, GLA_DV)] = (o_intra + o_inter).reshape(tg, GLA_DV)

    nchunks = seq // GLA_CHUNK
    return pl.pallas_call(
        body,
        name="gla_fwd_rev" if reverse else "gla_fwd",
        grid=(nt,),
        in_specs=[
            pl.BlockSpec((tg, KEY_W), lambda i: (tile(i), COL_Q // KEY_W)),
            pl.BlockSpec((tg, KEY_W), lambda i: (tile(i), COL_K // KEY_W)),
            pl.BlockSpec((tg, GLA_W), lambda i: (tile(i), COL_V // GLA_W)),
            pl.BlockSpec((tg, LANES), lambda i: (tile(i), COL_LR // LANES)),
            _resident((LANES, KEY_W)),
            _resident((1, KEY_W)),
        ],
        out_specs=[
            pl.BlockSpec((tg, GLA_W), lambda i: (tile(i), 0)),
            pl.BlockSpec((n, GLA_HEADS, GLA_DV, LANES), lambda i: (tile(i), 0, 0, 0)),
        ],
        out_shape=[
            jax.ShapeDtypeStruct((seq, GLA_W), F32),
            jax.ShapeDtypeStruct((nchunks, GLA_HEADS, GLA_DV, LANES), F32),
        ],
        scratch_shapes=[pltpu.VMEM((GLA_HEADS, GLA_DV, LANES), F32)],
        compiler_params=_params(48),
    )(p, p, p, p, wd_pad, bd)


def _mixer_out(x, o_f, o_b, p, gn, lng, lnb, ws_bf, bs_col, w_out):
    seq = x.shape[0]
    tm = min(seq, 512)

    def body(x_ref, of_ref, ob_ref, g_ref, u_ref, vv_ref, gn_ref, lng_ref, lnb_ref, ws_ref, bs_ref, wo_ref, x1_ref, yc_ref, vn_sc):
        for h in range(GLA_HEADS):
            cols = pl.ds(h * GLA_DV, GLA_DV)
            oh = of_ref[:, cols] + ob_ref[:, cols]
            on = oh * lax.rsqrt(jnp.mean(oh * oh, axis=-1, keepdims=True) + EPS)
            gh = g_ref[:, cols]
            yc_ref[:, cols] = (on * gn_ref[:, cols] * (gh * _sigmoid(gh))).astype(BF16)
        zv = _gelu(vv_ref[...])
        xc = zv - jnp.mean(zv, axis=-1, keepdims=True)
        vhat = xc * lax.rsqrt(jnp.mean(xc * xc, axis=-1, keepdims=True) + EPS)
        vn_sc[...] = (vhat * lng_ref[...] + lnb_ref[...]).astype(BF16)
        for c in range(tm // GMLP_CHUNK):
            rows = pl.ds(c * GMLP_CHUNK, GMLP_CHUNK)
            for g in range(GMLP_GROUPS):
                cols = pl.ds(g * LANES, LANES)
                s = _nn(ws_ref[g], vn_sc[rows, cols]) + bs_ref[g]
                yc_ref[rows, pl.ds(GLA_W + g * LANES, LANES)] = (_gelu(u_ref[rows, cols]) * s).astype(BF16)
        x1_ref[...] = x_ref[...] + _nn(yc_ref[...], wo_ref[...])

    row = lambda w: pl.BlockSpec((tm, w), lambda i: (i, 0))
    pcol = lambda col: pl.BlockSpec((tm, GLA_W), lambda i: (i, col // GLA_W))
    return pl.pallas_call(
        body,
        name="mixer_out",
        grid=(seq // tm,),
        in_specs=[
            row(D_MODEL), row(GLA_W), row(GLA_W), pcol(COL_G), pcol(COL_U), pcol(COL_VV),
            _resident((1, GLA_W)), _resident((1, GMLP_W)), _resident((1, GMLP_W)),
            _resident((GMLP_GROUPS, GMLP_CHUNK, GMLP_CHUNK)), _resident((GMLP_GROUPS, GMLP_CHUNK, 1)),
            _resident((D_MODEL, D_MODEL)),
        ],
        out_specs=[row(D_MODEL), row(D_MODEL)],
        out_shape=[jax.ShapeDtypeStruct((seq, D_MODEL), F32), jax.ShapeDtypeStruct((seq, D_MODEL), BF16)],
        scratch_shapes=[pltpu.VMEM((tm, GMLP_W), BF16)],
        compiler_params=_params(48, ("parallel",)),
    )(x, o_f, o_b, p, p, p, gn, lng, lnb, ws_bf, bs_col, w_out)


def _ffn_fwd(x1, target, g2, gf, wg_t, wu_t, wd):
    seq = x1.shape[0]
    tm = min(seq, 256)

    def body(x1_ref, t_ref, g2_ref, gf_ref, wg_ref, wu_ref, wd_ref, h2_ref, gate_ref, up_ref, act_ref, dx2_ref, loss_ref, dgf_ref):
        @pl.when(pl.program_id(0) == 0)
        def _():
            loss_ref[...] = jnp.zeros_like(loss_ref)
            dgf_ref[...] = jnp.zeros_like(dgf_ref)

        x1v = x1_ref[...]
        h2 = (x1v * lax.rsqrt(jnp.mean(x1v * x1v, axis=-1, keepdims=True) + EPS) * g2_ref[...]).astype(BF16)
        h2_ref[...] = h2
        gate = _nt(h2, wg_ref[...])
        up = _nt(h2, wu_ref[...])
        act = (gate * _sigmoid(gate) * up).astype(BF16)
        gate_ref[...] = gate
        up_ref[...] = up
        act_ref[...] = act
        x2 = x1v + _nn(act, wd_ref[...])
        rf = lax.rsqrt(jnp.mean(x2 * x2, axis=-1, keepdims=True) + EPS)
        xh = x2 * rf
        err = xh * gf_ref[...] - t_ref[...]
        loss_ref[...] += 0.5 * jnp.sum(jnp.mean(err * err, axis=-1, keepdims=True))
        dy = err * (1.0 / D_MODEL)
        dgf_ref[...] += jnp.sum(dy * xh, axis=0, keepdims=True)
        dx2_ref[...] = _rms_bwd(dy * gf_ref[...], xh, rf)

    row = lambda w: pl.BlockSpec((tm, w), lambda i: (i, 0))
    weight = _resident((D_FF, D_MODEL))
    return pl.pallas_call(
        body,
        name="ffn_fwd",
        grid=(seq // tm,),
        in_specs=[row(D_MODEL), row(D_MODEL), _resident((1, D_MODEL)), _resident((1, D_MODEL)), weight, weight, weight],
        out_specs=[row(D_MODEL), row(D_FF), row(D_FF), row(D_FF), row(D_MODEL),
                   pl.BlockSpec((1, LANES), lambda i: (0, 0)), pl.BlockSpec((1, D_MODEL), lambda i: (0, 0))],
        out_shape=[
            jax.ShapeDtypeStruct((seq, D_MODEL), BF16),
            jax.ShapeDtypeStruct((seq, D_FF), F32),
            jax.ShapeDtypeStruct((seq, D_FF), F32),
            jax.ShapeDtypeStruct((seq, D_FF), BF16),
            jax.ShapeDtypeStruct((seq, D_MODEL), F32),
            jax.ShapeDtypeStruct((1, LANES), F32),
            jax.ShapeDtypeStruct((1, D_MODEL), F32),
        ],
        compiler_params=_params(56),
    )(x1, target, g2, gf, wg_t, wu_t, wd)


def _ffn_bwd(dx2, gate, up, x1, g2, wg_t, wu_t, wd):
    seq = x1.shape[0]
    tm = min(seq, 256)

    def body(dx2_ref, gate_ref, up_ref, x1_ref, g2_ref, wg_ref, wu_ref, wd_ref, dgate_ref, dup_ref, dx1_ref, dg2_ref):
        @pl.when(pl.program_id(0) == 0)
        def _():
            dg2_ref[...] = jnp.zeros_like(dg2_ref)

        dx2v = dx2_ref[...]
        dact = _nt(dx2v.astype(BF16), wd_ref[...])
        gate = gate_ref[...]
        sg = _sigmoid(gate)
        dgate = (dact * up_ref[...] * (sg * (1.0 + gate * (1.0 - sg)))).astype(BF16)
        dup = (dact * (gate * sg)).astype(BF16)
        dgate_ref[...] = dgate
        dup_ref[...] = dup
        dh2 = _nn(dgate, wg_ref[...]) + _nn(dup, wu_ref[...])
        x1v = x1_ref[...]
        r2 = lax.rsqrt(jnp.mean(x1v * x1v, axis=-1, keepdims=True) + EPS)
        xh = x1v * r2
        dg2_ref[...] += jnp.sum(dh2 * xh, axis=0, keepdims=True)
        dx1_ref[...] = dx2v + _rms_bwd(dh2 * g2_ref[...], xh, r2)

    row = lambda w: pl.BlockSpec((tm, w), lambda i: (i, 0))
    weight = _resident((D_FF, D_MODEL))
    return pl.pallas_call(
        body,
        name="ffn_bwd",
        grid=(seq // tm,),
        in_specs=[row(D_MODEL), row(D_FF), row(D_FF), row(D_MODEL), _resident((1, D_MODEL)), weight, weight, weight],
        out_specs=[row(D_FF), row(D_FF), row(D_MODEL), pl.BlockSpec((1, D_MODEL), lambda i: (0, 0))],
        out_shape=[
            jax.ShapeDtypeStruct((seq, D_FF), BF16),
            jax.ShapeDtypeStruct((seq, D_FF), BF16),
            jax.ShapeDtypeStruct((seq, D_MODEL), F32),
            jax.ShapeDtypeStruct((1, D_MODEL), F32),
        ],
        compiler_params=_params(56),
    )(dx2, gate, up, x1, g2, wg_t, wu_t, wd)


WGRAD_ROWS = 256


def _ffn_wgrad(h2, dgate, dup, act, dx2):
    seq = h2.shape[0]
    tm = min(seq, 1024)

    def body(h2_ref, dgate_ref, dup_ref, act_ref, dx2_ref, dwg_ref, dwu_ref, dwd_ref):
        @pl.when(pl.program_id(1) == 0)
        def _():
            dwg_ref[...] = jnp.zeros_like(dwg_ref)
            dwu_ref[...] = jnp.zeros_like(dwu_ref)
            dwd_ref[...] = jnp.zeros_like(dwd_ref)

        h2v = h2_ref[...]
        dwg_ref[...] += _tn(dgate_ref[...], h2v)
        dwu_ref[...] += _tn(dup_ref[...], h2v)
        dwd_ref[...] += _tn(act_ref[...], dx2_ref[...].astype(BF16))

    ff = pl.BlockSpec((tm, WGRAD_ROWS), lambda j, i: (i, j))
    row = pl.BlockSpec((tm, D_MODEL), lambda j, i: (i, 0))
    out = pl.BlockSpec((WGRAD_ROWS, D_MODEL), lambda j, i: (j, 0))
    return pl.pallas_call(
        body,
        name="ffn_wgrad",
        grid=(D_FF // WGRAD_ROWS, seq // tm),
        in_specs=[row, ff, ff, ff, row],
        out_specs=[out, out, out],
        out_shape=[jax.ShapeDtypeStruct((D_FF, D_MODEL), F32)] * 3,
        compiler_params=_params(48, ("parallel", "arbitrary")),
    )(h2, dgate, dup, act, dx2)


def _mixer_bwd(dx1, ycat, o_f, o_b, p, gn, lng, lnb, ws_bf, wst_bf, bs_col, w_out, token):
    seq = dx1.shape[0]
    tm = min(seq, 512)
    nsteps = seq // tm

    def body(dx1_ref, yc_ref, of_ref, ob_ref, g_ref, u_ref, vv_ref, gn_ref, lng_ref, lnb_ref, ws_ref, wst_ref, bs_ref, wo_ref, token_ref,
             do_ref, dg_ref, du_ref, dvv_ref, dwo_ref, dgn_ref, dlng_ref, dlnb_ref, dws_ref, dbs_ref, vn_sc, dvn_sc, dbs_acc):
        step = pl.program_id(0)

        @pl.when(step == 0)
        def _():
            for r in (dwo_ref, dgn_ref, dlng_ref, dlnb_ref, dws_ref, dbs_acc):
                r[...] = jnp.zeros_like(r)

        dx1b = dx1_ref[...].astype(BF16)
        dyc = _nt(dx1b, wo_ref[...])
        dwo_ref[...] += _tn(yc_ref[...], dx1b)
        for h in range(GLA_HEADS):
            cols = pl.ds(h * GLA_DV, GLA_DV)
            dya = dyc[:, h * GLA_DV : (h + 1) * GLA_DV]
            oh = of_ref[:, cols] + ob_ref[:, cols]
            rn = lax.rsqrt(jnp.mean(oh * oh, axis=-1, keepdims=True) + EPS)
            on = oh * rn
            gh = g_ref[:, cols]
            sg = _sigmoid(gh)
            sil = gh * sg
            gnh = gn_ref[:, cols]
            dgn_ref[:, cols] += jnp.sum(dya * on * sil, axis=0, keepdims=True)
            dg_ref[:, cols] = dya * on * gnh * (sg * (1.0 + gh * (1.0 - sg)))
            do_ref[:, cols] = _rms_bwd(dya * gnh * sil, on, rn)
        vv = vv_ref[...]
        zv = _gelu(vv)
        xc = zv - jnp.mean(zv, axis=-1, keepdims=True)
        rstd = lax.rsqrt(jnp.mean(xc * xc, axis=-1, keepdims=True) + EPS)
        vhat = xc * rstd
        vn_sc[...] = (vhat * lng_ref[...] + lnb_ref[...]).astype(BF16)
        for c in range(tm // GMLP_CHUNK):
            rows = pl.ds(c * GMLP_CHUNK, GMLP_CHUNK)
            for g in range(GMLP_GROUPS):
                cols = pl.ds(g * LANES, LANES)
                vn = vn_sc[rows, cols]
                s = _nn(ws_ref[g], vn) + bs_ref[g]
                dyb = dyc[c * GMLP_CHUNK : (c + 1) * GMLP_CHUNK, GLA_W + g * LANES : GLA_W + (g + 1) * LANES]
                u = u_ref[rows, cols]
                du_ref[rows, cols] = dyb * s * _gelu_grad(u)
                ds = dyb * _gelu(u)
                dbs_acc[g] += ds
                dsb = ds.astype(BF16)
                dws_ref[g] += _nt(dsb, vn)
                dvn_sc[rows, cols] = _nn(wst_ref[g], dsb)
        dvn = dvn_sc[...]
        dlng_ref[...] += jnp.sum(dvn * vhat, axis=0, keepdims=True)
        dlnb_ref[...] += jnp.sum(dvn, axis=0, keepdims=True)
        dvh = dvn * lng_ref[...]
        dzv = rstd * (dvh - jnp.mean(dvh, axis=-1, keepdims=True) - vhat * jnp.mean(dvh * vhat, axis=-1, keepdims=True))
        dvv_ref[...] = dzv * _gelu_grad(vv)

        @pl.when(step == nsteps - 1)
        def _():
            dbs_ref[...] = jnp.sum(dbs_acc[...], axis=-1, keepdims=True)

    row = lambda w: pl.BlockSpec((tm, w), lambda i: (i, 0))
    pcol = lambda col: pl.BlockSpec((tm, GLA_W), lambda i: (i, col // GLA_W))
    const = lambda shape: pl.BlockSpec(shape, lambda i: (0,) * len(shape))
    return pl.pallas_call(
        body,
        name="mixer_bwd",
        grid=(nsteps,),
        in_specs=[
            row(D_MODEL), row(D_MODEL), row(GLA_W), row(GLA_W), pcol(COL_G), pcol(COL_U), pcol(COL_VV),
            _resident((1, GLA_W)), _resident((1, GMLP_W)), _resident((1, GMLP_W)),
            _resident((GMLP_GROUPS, GMLP_CHUNK, GMLP_CHUNK)), _resident((GMLP_GROUPS, GMLP_CHUNK, GMLP_CHUNK)),
            _resident((GMLP_GROUPS, GMLP_CHUNK, 1)), _resident((D_MODEL, D_MODEL)), _resident(TOKEN_SHAPE),
        ],
        out_specs=[
            row(GLA_W), row(GLA_W), row(GMLP_W), row(GMLP_W), const((D_MODEL, D_MODEL)),
            const((1, GLA_W)), const((1, GMLP_W)), const((1, GMLP_W)),
            const((GMLP_GROUPS, GMLP_CHUNK, GMLP_CHUNK)), const((GMLP_GROUPS, GMLP_CHUNK, 1)),
        ],
        out_shape=[
            jax.ShapeDtypeStruct((seq, GLA_W), F32), jax.ShapeDtypeStruct((seq, GLA_W), F32),
            jax.ShapeDtypeStruct((seq, GMLP_W), F32), jax.ShapeDtypeStruct((seq, GMLP_W), F32),
            jax.ShapeDtypeStruct((D_MODEL, D_MODEL), F32),
            jax.ShapeDtypeStruct((1, GLA_W), F32), jax.ShapeDtypeStruct((1, GMLP_W), F32), jax.ShapeDtypeStruct((1, GMLP_W), F32),
            jax.ShapeDtypeStruct((GMLP_GROUPS, GMLP_CHUNK, GMLP_CHUNK), F32), jax.ShapeDtypeStruct((GMLP_GROUPS, GMLP_CHUNK, 1), F32),
        ],
        scratch_shapes=[pltpu.VMEM((tm, GMLP_W), BF16), pltpu.VMEM((tm, GMLP_W), F32), pltpu.VMEM((GMLP_GROUPS, GMLP_CHUNK, GMLP_CHUNK), F32)],
        compiler_params=_params(56),
    )(dx1, ycat, o_f, o_b, p, p, p, gn, lng, lnb, ws_bf, wst_bf, bs_col, w_out, token)


def _gla_bwd(p, do, st, wd_pad, bd, reverse):
    seq = p.shape[0]
    tg = _gla_tile(seq)
    nt = seq // tg
    n = tg // GLA_CHUNK
    scale = GLA_DK**-0.5

    def tile(i):
        return i if reverse else nt - 1 - i

    def body(q_ref, k_ref, v_ref, lr_ref, do_ref, st_ref, wd_ref, bd_ref, dq_ref, dk_ref, dv_ref, dlr_ref, dwd_ref, dbd_ref, carry, dsa):
        @pl.when(pl.program_id(0) == 0)
        def _():
            carry[...] = jnp.zeros_like(carry)
            dwd_ref[...] = jnp.zeros_like(dwd_ref)
            dbd_ref[...] = jnp.zeros_like(dbd_ref)

        lr_bf = lr_ref[...].astype(BF16)
        row_in_chunk = lax.broadcasted_iota(jnp.int32, (tg, LANES), 0) % GLA_CHUNK
        lane_head = lax.broadcasted_iota(jnp.int32, (1, LANES), 1) // GLA_DK
        tt = lax.broadcasted_iota(jnp.int32, (GLA_CHUNK, GLA_CHUNK), 0)
        ss = lax.broadcasted_iota(jnp.int32, (GLA_CHUNK, GLA_CHUNK), 1)
        causal = (tt <= ss) if reverse else (tt >= ss)
        causal_t = (tt >= ss) if reverse else (tt <= ss)
        order = range(n) if reverse else range(n - 1, -1, -1)
        dlr = jnp.zeros((tg, LANES), F32)
        for pair in range(2):
            cols = pl.ds(pair * LANES, LANES)
            pre, b3, blast = _gla_decay_terms(lr_bf, wd_ref, bd_ref, pair, row_in_chunk, reverse, n)
            q3 = q_ref[:, cols].reshape(n, GLA_CHUNK, LANES) * scale
            k3 = k_ref[:, cols].reshape(n, GLA_CHUNK, LANES)
            eb = jnp.exp(b3)
            emb = jnp.exp(-b3)
            ekte = jnp.exp(blast - b3)
            qd = q3 * eb
            kdf = k3 * emb
            kd = kdf.astype(BF16)
            kte = k3 * ekte
            dec = jnp.exp(blast)
            dqd = jnp.zeros((n, GLA_CHUNK, LANES), F32)
            dkd = jnp.zeros((n, GLA_CHUNK, LANES), F32)
            dkte = jnp.zeros((n, GLA_CHUNK, LANES), F32)
            ddec = jnp.zeros((n, 1, LANES), F32)
            for hh in range(2):
                h = 2 * pair + hh
                vcols = pl.ds(h * GLA_DV, GLA_DV)
                m = (lane_head == hh).astype(F32)
                qdh = (qd * m).astype(BF16)
                kteh = (kte * m).astype(BF16)
                vh = v_ref[:, vcols].reshape(n, GLA_CHUNK, GLA_DV).astype(BF16)
                doh = do_ref[:, vcols].reshape(n, GLA_CHUNK, GLA_DV).astype(BF16)
                stb = st_ref[:, h]
                stb_bf = stb.astype(BF16)
                sc_t = jnp.where(causal_t, _bnt(kd, qdh), 0.0).astype(BF16)
                dp = jnp.where(causal, _bnt(doh, vh), 0.0).astype(BF16)
                dp_t = jnp.where(causal_t, _bnt(vh, doh), 0.0).astype(BF16)
                a = _btn(doh, qdh)
                c = carry[h]
                for j in order:
                    dsa[j] = c
                    c = a[j] + dec[j] * c
                carry[h] = c
                dsa_f = dsa[...]
                dsa_bf = dsa_f.astype(BF16)
                dqd = dqd + (_bnn(dp, kd) * m + _bnn(doh, stb_bf))
                dkd = dkd + _bnn(dp_t, qdh)
                dkte = dkte + _bnn(vh, dsa_bf)
                ddec = ddec + jnp.sum(dsa_f * stb, axis=1, keepdims=True)
                dv_ref[:, vcols] = (_bnn(sc_t, doh) + _bnt(kteh, dsa_bf)).reshape(tg, GLA_DV)
            dq_ref[:, cols] = (dqd * (scale * eb)).reshape(tg, LANES)
            dk_ref[:, cols] = (dkd * emb + dkte * ekte).reshape(tg, LANES)
            db = dqd * qd - dkd * kdf - dkte * kte
            dblast = jnp.sum(dkte * kte, axis=1, keepdims=True) + ddec * dec
            dla = _chunk_cumsum(db.reshape(tg, LANES), row_in_chunk, not reverse) + jnp.broadcast_to(dblast, (n, GLA_CHUNK, LANES)).reshape(tg, LANES)
            dpre = (dla * (1.0 / GLA_TAU) * _sigmoid(-pre))
            dpre_bf = dpre.astype(BF16)
            dlr = dlr + _nt(dpre_bf, wd_ref[:, cols])
            dwd_ref[:, cols] += _tn(lr_bf, dpre_bf)
            dbd_ref[:, cols] += jnp.sum(dpre, axis=0, keepdims=True)
        dlr_ref[...] = dlr

    return pl.pallas_call(
        body,
        name="gla_bwd_rev" if reverse else "gla_bwd",
        grid=(nt,),
        in_specs=[
            pl.BlockSpec((tg, KEY_W), lambda i: (tile(i), COL_Q // KEY_W)),
            pl.BlockSpec((tg, KEY_W), lambda i: (tile(i), COL_K // KEY_W)),
            pl.BlockSpec((tg, GLA_W), lambda i: (tile(i), COL_V // GLA_W)),
            pl.BlockSpec((tg, LANES), lambda i: (tile(i), COL_LR // LANES)),
            pl.BlockSpec((tg, GLA_W), lambda i: (tile(i), 0)),
            pl.BlockSpec((n, GLA_HEADS, GLA_DV, LANES), lambda i: (tile(i), 0, 0, 0)),
            _resident((LANES, KEY_W)),
            _resident((1, KEY_W)),
        ],
        out_specs=[
            pl.BlockSpec((tg, KEY_W), lambda i: (tile(i), 0)),
            pl.BlockSpec((tg, KEY_W), lambda i: (tile(i), 0)),
            pl.BlockSpec((tg, GLA_W), lambda i: (tile(i), 0)),
            pl.BlockSpec((tg, LANES), lambda i: (tile(i), 0)),
            pl.BlockSpec((LANES, KEY_W), lambda i: (0, 0)),
            pl.BlockSpec((1, KEY_W), lambda i: (0, 0)),
        ],
        out_shape=[
            jax.ShapeDtypeStruct((seq, KEY_W), F32), jax.ShapeDtypeStruct((seq, KEY_W), F32),
            jax.ShapeDtypeStruct((seq, GLA_W), F32), jax.ShapeDtypeStruct((seq, LANES), F32),
            jax.ShapeDtypeStruct((LANES, KEY_W), F32), jax.ShapeDtypeStruct((1, KEY_W), F32),
        ],
        scratch_shapes=[pltpu.VMEM((GLA_HEADS, GLA_DV, LANES), F32), pltpu.VMEM((n, GLA_DV, LANES), F32)],
        compiler_params=_params(48),
    )(p, p, p, p, do, st, wd_pad, bd)


def _inproj_bwd(x, dx1, g1, w_in_t, dq_f, dq_b, dk_f, dk_b, dv_f, dv_b, dg, du, dvv, dlr_f, dlr_b):
    seq = x.shape[0]
    tm = min(seq, 256)

    def body(x_ref, dx1_ref, g1_ref, w_ref, dqf, dqb, dkf, dkb, dvf, dvb, dg_ref, du_ref, dvv_ref, dlrf, dlrb, dx_ref, dw_ref, dg1_ref, dp_sc):
        @pl.when(pl.program_id(0) == 0)
        def _():
            dw_ref[...] = jnp.zeros_like(dw_ref)
            dg1_ref[...] = jnp.zeros_like(dg1_ref)

        dp_sc[:, COL_Q : COL_Q + KEY_W] = (dqf[...] + dqb[...]).astype(BF16)
        dp_sc[:, COL_K : COL_K + KEY_W] = (dkf[...] + dkb[...]).astype(BF16)
        dp_sc[:, COL_V : COL_V + GLA_W] = (dvf[...] + dvb[...]).astype(BF16)
        dp_sc[:, COL_G : COL_G + GLA_W] = dg_ref[...].astype(BF16)
        dp_sc[:, COL_U : COL_U + GMLP_W] = du_ref[...].astype(BF16)
        dp_sc[:, COL_VV : COL_VV + GMLP_W] = dvv_ref[...].astype(BF16)
        dp_sc[:, COL_LR : COL_LR + LANES] = (dlrf[...] + dlrb[...]).astype(BF16)
        xv = x_ref[...]
        r1 = lax.rsqrt(jnp.mean(xv * xv, axis=-1, keepdims=True) + EPS)
        xh = xv * r1
        h = (xh * g1_ref[...]).astype(BF16)
        main, uv, lr = dp_sc[:, 0:COL_U], dp_sc[:, COL_U:COL_LR], dp_sc[:, COL_LR:PROJ_WP]
        dw_ref[0:ROW_LR, :] += _tn(main, h)
        dw_ref[ROW_UV:PROJ_W, :] += _tn(uv, h)
        dw_ref[ROW_LR:ROW_UV, :] += _tn(lr, h)[0 : ROW_UV - ROW_LR]
        dh = _nn(main, w_ref[0:ROW_LR, :]) + _nn(uv, w_ref[ROW_UV:PROJ_W, :]) + _nn(lr, w_ref[ROW_LR : ROW_LR + LANES, :])
        dg1_ref[...] += jnp.sum(dh * xh, axis=0, keepdims=True)
        dx_ref[...] = dx1_ref[...] + _rms_bwd(dh * g1_ref[...], xh, r1)

    row = lambda w: pl.BlockSpec((tm, w), lambda i: (i, 0))
    return pl.pallas_call(
        body,
        name="inproj_bwd",
        grid=(seq // tm,),
        in_specs=[
            row(D_MODEL), row(D_MODEL), _resident((1, D_MODEL)), _resident((PROJ_W, D_MODEL)),
            row(KEY_W), row(KEY_W), row(KEY_W), row(KEY_W), row(GLA_W), row(GLA_W),
            row(GLA_W), row(GMLP_W), row(GMLP_W), row(LANES), row(LANES),
        ],
        out_specs=[row(D_MODEL), pl.BlockSpec((PROJ_W, D_MODEL), lambda i: (0, 0)), pl.BlockSpec((1, D_MODEL), lambda i: (0, 0))],
        out_shape=[
            jax.ShapeDtypeStruct((seq, D_MODEL), F32),
            jax.ShapeDtypeStruct((PROJ_W, D_MODEL), F32),
            jax.ShapeDtypeStruct((1, D_MODEL), F32),
        ],
        scratch_shapes=[pltpu.VMEM((tm, PROJ_WP), BF16)],
        compiler_params=_params(56),
    )(x, dx1, g1, w_in_t, dq_f, dq_b, dk_f, dk_b, dv_f, dv_b, dg, du, dvv, dlr_f, dlr_b)


def _row_tile(rows, multiple=8):
    for t in range(min(rows, 512), 0, -1):
        if rows % t == 0 and t % multiple == 0:
            return t
    return rows


def _cast_into_slot(w, shard):
    rows, cols = w.shape
    tr = _row_tile(rows, 16)

    def body(s_ref, w_ref, o_ref):
        o_ref[...] = w_ref[...].astype(BF16)

    return pl.pallas_call(
        body,
        name="cast_into_slot",
        grid_spec=pltpu.PrefetchScalarGridSpec(
            num_scalar_prefetch=1,
            grid=(rows // tr,),
            in_specs=[pl.BlockSpec((tr, cols), lambda i, s_ref: (i, 0))],
            out_specs=pl.BlockSpec((None, tr, cols), lambda i, s_ref: (s_ref[0], i, 0)),
        ),
        out_shape=jax.ShapeDtypeStruct((N_SHARDS, rows, cols), BF16),
        compiler_params=_params(32, ("parallel",)),
    )(shard, w)


def _add_halves(g4, recv, c):
    _, rows, _ = g4.shape
    tr = _row_tile(rows, 16)

    def body(c_ref, g_ref, r_ref, o_ref, ob_ref):
        total = g_ref[...] + r_ref[...]
        o_ref[...] = total
        ob_ref[...] = total.astype(BF16)

    out = pl.BlockSpec((None, tr, HALF), lambda s, i, c_ref: (s, i, 0))
    return pl.pallas_call(
        body,
        name="add_halves",
        grid_spec=pltpu.PrefetchScalarGridSpec(
            num_scalar_prefetch=1,
            grid=(N_SHARDS, rows // tr),
            in_specs=[pl.BlockSpec((None, tr, HALF), lambda s, i, c_ref: (s, i, c_ref[0])), out],
            out_specs=[out, out],
        ),
        out_shape=[jax.ShapeDtypeStruct((N_SHARDS, rows, HALF), F32), jax.ShapeDtypeStruct((N_SHARDS, rows, HALF), BF16)],
        compiler_params=_params(32, ("parallel", "parallel")),
    )(c, g4, recv)


def _add_partials(part4, recv3, shard_core):
    _, rows, _ = part4.shape
    tr = _row_tile(rows, 16)

    def body(sc_ref, p_ref, r_ref, o_ref):
        o_ref[...] = ((p_ref[...] + r_ref[0].astype(F32)) + r_ref[1].astype(F32)) + r_ref[2].astype(F32)

    return pl.pallas_call(
        body,
        name="add_partials",
        grid_spec=pltpu.PrefetchScalarGridSpec(
            num_scalar_prefetch=1,
            grid=(rows // tr,),
            in_specs=[
                pl.BlockSpec((None, tr, HALF), lambda i, sc_ref: (sc_ref[0], i, 0)),
                pl.BlockSpec((3, tr, HALF), lambda i, sc_ref: (0, i, 0)),
            ],
            out_specs=pl.BlockSpec((tr, HALF), lambda i, sc_ref: (i, sc_ref[1])),
        ),
        out_shape=jax.ShapeDtypeStruct((rows, 2 * HALF), F32),
        compiler_params=_params(32, ("parallel",)),
    )(shard_core, part4, recv3)


def _adam_math(w, g, m, v):
    m = ADAM_B1 * m + (1.0 - ADAM_B1) * g
    v = ADAM_B2 * v + (1.0 - ADAM_B2) * (g * g)
    m_hat = m / (1.0 - ADAM_B1**ADAM_STEP)
    v_hat = v / (1.0 - ADAM_B2**ADAM_STEP)
    delta = -ADAM_LR * (m_hat / (jnp.sqrt(v_hat) + ADAM_EPS) + ADAM_WD * w)
    return delta, m, v


def _adamw(w, g, m, v):
    rows, cols = w.shape
    tr = _row_tile(rows)

    def body(w_ref, g_ref, m_ref, v_ref, go_ref, d_ref, mo_ref, vo_ref):
        gv = g_ref[...]
        go_ref[...] = gv
        d_ref[...], mo_ref[...], vo_ref[...] = _adam_math(w_ref[...], gv, m_ref[...], v_ref[...])

    spec = pl.BlockSpec((tr, cols), lambda i: (i, 0))
    shape = jax.ShapeDtypeStruct(w.shape, F32)
    return pl.pallas_call(
        body, name="adamw", grid=(rows // tr,), in_specs=[spec] * 4, out_specs=[spec] * 4, out_shape=[shape] * 4,
        compiler_params=_params(32, ("parallel",)),
    )(w, g, m, v)


SMALL_ROWS = 560
DECAY_ROWS = 8
SMALL_TOTAL = SMALL_ROWS + 2 * N_SHARDS * DECAY_ROWS


def _adamw_small(gathered, wp, mp, vp):
    out_rows = SMALL_ROWS + 2 * DECAY_ROWS

    def body(ga_ref, w_ref, m_ref, v_ref, g_ref, d_ref, mo_ref, vo_ref):
        shard = 2 * lax.axis_index("x") + lax.axis_index("y")
        g_ref[pl.ds(0, SMALL_ROWS), :] = functools.reduce(lambda a, b: a + b, [ga_ref[d, pl.ds(0, SMALL_ROWS), :] for d in range(8)])
        for k in range(2):
            start = pl.multiple_of(SMALL_ROWS + k * N_SHARDS * DECAY_ROWS + shard * DECAY_ROWS, DECAY_ROWS)
            g_ref[pl.ds(SMALL_ROWS + k * DECAY_ROWS, DECAY_ROWS), :] = functools.reduce(
                lambda a, b: a + b, [ga_ref[d, pl.ds(start, DECAY_ROWS), :] for d in range(8)])
        d_ref[...], mo_ref[...], vo_ref[...] = _adam_math(w_ref[...], g_ref[...], m_ref[...], v_ref[...])

    shape = jax.ShapeDtypeStruct((out_rows, LANES), F32)
    return pl.pallas_call(body, name="adamw_small", out_shape=[shape] * 4, compiler_params=_params(32, None))(gathered, wp, mp, vp)


ANY = pl.BlockSpec(memory_space=pl.ANY)


def _position():
    return lax.axis_index("x"), lax.axis_index("y"), lax.axis_index("c")


def _other_chips(x, y):
    return [(1 - x, y), (x, 1 - y), (1 - x, 1 - y)]


HBM = pl.BlockSpec(memory_space=pltpu.HBM)
SEM = pl.BlockSpec(memory_space=pltpu.SEMAPHORE)
TOKEN = jax.ShapeDtypeStruct(TOKEN_SHAPE, F32)
DATAFLOW = pltpu.SideEffectType.DATAFLOW_SIDE_EFFECTING


def _half_block(ref4, slot, core):
    return ref4.at[slot, :, pl.ds(pl.multiple_of(core * HALF, HALF), HALF)]


def _gather_ici_copies(refs4, send_sems, recv_sems, stride):
    x, y, c = _position()
    pairs = []
    for k, ref4 in enumerate(refs4):
        mine = _half_block(ref4, 2 * x + y, c)
        for j, (px, py) in enumerate(_other_chips(x, y)):
            sems = dict(send_sem=send_sems.at[stride * k + j], recv_sem=recv_sems.at[stride * k + j], device_id=(px, py, c), device_id_type=MESH)
            pairs.append((functools.partial(pltpu.make_async_remote_copy, src_ref=mine, dst_ref=mine, **sems),
                          functools.partial(pltpu.make_async_remote_copy, src_ref=mine, dst_ref=_half_block(ref4, 2 * px + py, c), **sems)))
    return pairs


def _gather_d2d_copies(refs4, send_sems, recv_sems, stride, offset):
    x, y, c = _position()
    pairs = []
    for k, ref4 in enumerate(refs4):
        for j, (px, py) in enumerate(_other_chips(x, y)):
            have = _half_block(ref4, 2 * px + py, c)
            sems = dict(send_sem=send_sems.at[stride * k + offset + j], recv_sem=recv_sems.at[stride * k + offset + j],
                        device_id=(x, y, 1 - c), device_id_type=MESH)
            pairs.append((functools.partial(pltpu.make_async_remote_copy, src_ref=have, dst_ref=have, **sems),
                          functools.partial(pltpu.make_async_remote_copy, src_ref=have, dst_ref=_half_block(ref4, 2 * px + py, 1 - c), **sems)))
    return pairs


def _gather_sync(bufs):
    n = len(bufs)

    def body(*refs):
        outs = refs[n : 2 * n]
        send_sems, recv_sems = refs[2 * n :]
        ici = _gather_ici_copies(outs, send_sems, recv_sems, 6)
        d2d = _gather_d2d_copies(outs, send_sems, recv_sems, 6, 3)
        for send, _ in ici:
            send().start()
        for (_, arrival), (forward, _) in zip(ici, d2d):
            arrival().wait_recv()
            forward().start()
        for _, arrival in d2d:
            arrival().wait_recv()
        for send, _ in ici + d2d:
            send().wait_send()

    return pl.pallas_call(
        body,
        name="gather_sync",
        in_specs=[ANY] * n,
        out_specs=[ANY] * n,
        out_shape=[jax.ShapeDtypeStruct(b.shape, b.dtype) for b in bufs],
        input_output_aliases={k: k for k in range(n)},
        scratch_shapes=[pltpu.SemaphoreType.DMA((6 * n,)), pltpu.SemaphoreType.DMA((6 * n,))],
        compiler_params=pltpu.CompilerParams(has_side_effects=True),
    )(*bufs)


def _gather_start(bufs, after):
    n, na = len(bufs), len(after)

    def body(*refs):
        ins = refs[:n]
        send_sems, recv_sems = refs[n + na], refs[n + na + 1]
        token = refs[2 * n + na + 2]
        for send, _ in _gather_ici_copies(ins, send_sems, recv_sems, 3):
            send().start()
        token[...] = jnp.zeros_like(token)

    out = pl.pallas_call(
        body,
        name="gather_start",
        in_specs=[HBM] * n + [ANY] * na,
        out_specs=(SEM, SEM, *[HBM] * n, pl.BlockSpec(memory_space=pltpu.VMEM)),
        out_shape=(pltpu.SemaphoreType.DMA((3 * n,)), pltpu.SemaphoreType.DMA((3 * n,)), *[pltpu.HBM(b.shape, b.dtype) for b in bufs], TOKEN),
        input_output_aliases={k: 2 + k for k in range(n)},
        compiler_params=pltpu.CompilerParams(has_side_effects=DATAFLOW),
    )(*[pltpu.with_memory_space_constraint(b, pltpu.HBM) for b in bufs], *after)
    return out[0], out[1], list(out[2 : 2 + n]), out[2 + n]


def _gather_wait(send_sems, recv_sems, bufs, after):
    n = len(bufs)

    def body(*refs):
        ins = refs[:n]
        for send, arrival in _gather_ici_copies(ins, refs[n], refs[n + 1], 3):
            send().wait_send()
            arrival().wait_recv()

    return pl.pallas_call(
        body,
        name="gather_wait",
        in_specs=[HBM] * n + [SEM, SEM] + [ANY] * len(after),
        out_specs=tuple([HBM] * n),
        out_shape=tuple(pltpu.HBM(b.shape, b.dtype) for b in bufs),
        input_output_aliases={k: k for k in range(n)},
        compiler_params=pltpu.CompilerParams(has_side_effects=DATAFLOW),
    )(*bufs, send_sems, recv_sems, *after)


def _gather_forward(bufs):
    n = len(bufs)

    def body(*refs):
        outs = refs[n : 2 * n]
        send_sems, recv_sems = refs[2 * n :]
        d2d = _gather_d2d_copies(outs, send_sems, recv_sems, 3, 0)
        for forward, _ in d2d:
            forward().start()
        for forward, arrival in d2d:
            arrival().wait_recv()
            forward().wait_send()

    return pl.pallas_call(
        body,
        name="gather_forward",
        in_specs=[ANY] * n,
        out_specs=[ANY] * n,
        out_shape=[jax.ShapeDtypeStruct(b.shape, b.dtype) for b in bufs],
        input_output_aliases={k: k for k in range(n)},
        scratch_shapes=[pltpu.SemaphoreType.DMA((3 * n,)), pltpu.SemaphoreType.DMA((3 * n,))],
        compiler_params=pltpu.CompilerParams(has_side_effects=True),
    )(*bufs)


def _exchange_halves(grads4):
    n = len(grads4)

    def body(*refs):
        ins, outs = refs[:n], refs[n : 2 * n]
        send_sems, recv_sems = refs[2 * n :]
        x, y, c = _position()
        copies = []
        for k in range(n):
            cp = pltpu.make_async_remote_copy(
                src_ref=ins[k].at[:, :, pl.ds(pl.multiple_of((1 - c) * HALF, HALF), HALF)], dst_ref=outs[k],
                send_sem=send_sems.at[k], recv_sem=recv_sems.at[k], device_id=(x, y, 1 - c), device_id_type=MESH)
            cp.start()
            copies.append(cp)
        for cp in copies:
            cp.wait()

    return pl.pallas_call(
        body,
        name="exchange_halves",
        in_specs=[ANY] * n,
        out_specs=[ANY] * n,
        out_shape=[jax.ShapeDtypeStruct((N_SHARDS, g.shape[1], HALF), g.dtype) for g in grads4],
        scratch_shapes=[pltpu.SemaphoreType.DMA((n,)), pltpu.SemaphoreType.DMA((n,))],
        compiler_params=pltpu.CompilerParams(has_side_effects=True),
    )(*grads4)


def _scatter_partials(parts4):
    n = len(parts4)

    def body(*refs):
        ins, outs = refs[:n], refs[n : 2 * n]
        send_sems, recv_sems = refs[2 * n :]
        x, y, c = _position()
        copies = []
        for k in range(n):
            for j, (px, py) in enumerate(_other_chips(x, y)):
                cp = pltpu.make_async_remote_copy(
                    src_ref=ins[k].at[2 * px + py], dst_ref=outs[k].at[j],
                    send_sem=send_sems.at[3 * k + j], recv_sem=recv_sems.at[3 * k + j], device_id=(px, py, c), device_id_type=MESH)
                cp.start()
                copies.append(cp)
        for cp in copies:
            cp.wait()

    return pl.pallas_call(
        body,
        name="scatter_partials",
        in_specs=[ANY] * n,
        out_specs=[ANY] * n,
        out_shape=[jax.ShapeDtypeStruct((3,) + g.shape[1:], g.dtype) for g in parts4],
        scratch_shapes=[pltpu.SemaphoreType.DMA((3 * n,)), pltpu.SemaphoreType.DMA((3 * n,))],
        compiler_params=pltpu.CompilerParams(has_side_effects=True),
    )(*parts4)


def _scatter_copies(parts, lands, send_sems, recv_sems):
    x, y, c = _position()
    copies = []
    for k in range(len(parts)):
        for j, (px, py) in enumerate(_other_chips(x, y)):
            copies.append(pltpu.make_async_remote_copy(
                src_ref=parts[k].at[2 * px + py], dst_ref=lands[k].at[j],
                send_sem=send_sems.at[3 * k + j], recv_sem=recv_sems.at[3 * k + j], device_id=(px, py, c), device_id_type=MESH))
    return copies


def _scatter_start(parts4):
    n = len(parts4)
    lands = [lax.empty((3,) + g.shape[1:], g.dtype) for g in parts4]

    def body(*refs):
        ins, land_in = refs[:n], refs[n : 2 * n]
        send_sems, recv_sems = refs[2 * n], refs[2 * n + 1]
        token = refs[4 * n + 2]
        for cp in _scatter_copies(ins, land_in, send_sems, recv_sems):
            cp.start()
        token[...] = jnp.zeros_like(token)

    hbm = lambda a: pltpu.HBM(a.shape, a.dtype)
    out = pl.pallas_call(
        body,
        name="scatter_start",
        in_specs=[HBM] * (2 * n),
        out_specs=(SEM, SEM, *[HBM] * (2 * n), pl.BlockSpec(memory_space=pltpu.VMEM)),
        out_shape=(pltpu.SemaphoreType.DMA((3 * n,)), pltpu.SemaphoreType.DMA((3 * n,)), *[hbm(a) for a in parts4], *[hbm(a) for a in lands], TOKEN),
        input_output_aliases={k: 2 + k for k in range(2 * n)},
        compiler_params=pltpu.CompilerParams(has_side_effects=DATAFLOW),
    )(*[pltpu.with_memory_space_constraint(a, pltpu.HBM) for a in parts4 + lands])
    return out[0], out[1], list(out[2 : 2 + n]), list(out[2 + n : 2 + 2 * n]), out[2 + 2 * n]


def _scatter_wait(send_sems, recv_sems, parts4, lands, after):
    n = len(parts4)

    def body(*refs):
        for cp in _scatter_copies(refs[:n], refs[n : 2 * n], refs[2 * n], refs[2 * n + 1]):
            cp.wait_send()
            cp.wait_recv()

    hbm = lambda a: pltpu.HBM(a.shape, a.dtype)
    out = pl.pallas_call(
        body,
        name="scatter_wait",
        in_specs=[HBM] * (2 * n) + [SEM, SEM] + [ANY] * len(after),
        out_specs=tuple([HBM] * (2 * n)),
        out_shape=tuple(hbm(a) for a in parts4 + lands),
        input_output_aliases={k: k for k in range(2 * n)},
        compiler_params=pltpu.CompilerParams(has_side_effects=DATAFLOW),
    )(*parts4, *lands, send_sems, recv_sems, *after)
    return list(out[n:])


def _join_halves(bufs):
    n = len(bufs)

    def body(*refs):
        outs = refs[n : 2 * n]
        send_sems, recv_sems = refs[2 * n :]
        x, y, c = _position()
        half = lambda ref, core: ref.at[:, pl.ds(pl.multiple_of(core * HALF, HALF), HALF)]
        for k in range(n):
            mine = half(outs[k], c)
            pltpu.make_async_remote_copy(
                src_ref=mine, dst_ref=mine, send_sem=send_sems.at[k], recv_sem=recv_sems.at[k],
                device_id=(x, y, 1 - c), device_id_type=MESH).start()
        for k in range(n):
            wait = pltpu.make_async_remote_copy(
                src_ref=half(outs[k], c), dst_ref=half(outs[k], 1 - c), send_sem=send_sems.at[k], recv_sem=recv_sems.at[k],
                device_id=(x, y, 1 - c), device_id_type=MESH)
            wait.wait_send()
            wait.wait_recv()

    return pl.pallas_call(
        body,
        name="join_halves",
        in_specs=[ANY] * n,
        out_specs=[ANY] * n,
        out_shape=[jax.ShapeDtypeStruct(b.shape, b.dtype) for b in bufs],
        input_output_aliases={k: k for k in range(n)},
        scratch_shapes=[pltpu.SemaphoreType.DMA((n,)), pltpu.SemaphoreType.DMA((n,))],
        compiler_params=pltpu.CompilerParams(has_side_effects=True),
    )(*bufs)


def _allgather_small(block):
    m_per, ncol = block.shape

    def body(x_ref, out_ref, send_sems, recv_sems, local_sem):
        x, y, c = _position()
        me, sibling = (x, y, c), (x, y, 1 - c)
        chips = _other_chips(x, y)

        def rows(px, py, pc):
            return out_ref.at[4 * px + 2 * py + pc]

        def copy(k, blk, to, src=None):
            return pltpu.make_async_remote_copy(
                src_ref=rows(*blk) if src is None else src, dst_ref=rows(*blk),
                send_sem=send_sems.at[k], recv_sem=recv_sems.at[k], device_id=to, device_id_type=MESH)

        mine = pltpu.make_async_copy(x_ref, rows(*me), local_sem)
        mine.start()
        first = [copy(0, me, sibling, src=x_ref)] + [copy(1 + j, me, (*chip, c), src=x_ref) for j, chip in enumerate(chips)]
        for cp in first:
            cp.start()
        passed = [copy(4 + j, (*chip, c), sibling) for j, chip in enumerate(chips)]
        for j, chip in enumerate(chips):
            copy(1 + j, (*chip, c), me).wait_recv()
            passed[j].start()
        copy(0, sibling, me).wait_recv()
        for j, chip in enumerate(chips):
            copy(4 + j, (*chip, 1 - c), me).wait_recv()
        for cp in first + passed:
            cp.wait_send()
        mine.wait()

    return pl.pallas_call(
        body,
        name="allgather_small",
        in_specs=[pl.BlockSpec(memory_space=pltpu.VMEM)],
        out_specs=pl.BlockSpec(memory_space=pltpu.VMEM),
        out_shape=jax.ShapeDtypeStruct((8, m_per, ncol), block.dtype),
        scratch_shapes=[pltpu.SemaphoreType.DMA((7,)), pltpu.SemaphoreType.DMA((7,)), pltpu.SemaphoreType.DMA],
        compiler_params=pltpu.CompilerParams(has_side_effects=True, vmem_limit_bytes=32 * MIB),
    )(block)


SMALL_NAMES = ["norm1_g", "b_decay_f", "b_decay_b", "gla_norm_g", "gmlp_ln_g", "gmlp_ln_b", "w_spatial", "b_spatial", "norm2_g", "final_norm_g"]


def _pack_small(parts, decay_parts):
    flat = jnp.concatenate([a.reshape(-1) for a in parts])
    flat = jnp.pad(flat, (0, SMALL_ROWS * LANES - flat.shape[0])).reshape(SMALL_ROWS, LANES)
    return jnp.concatenate([flat] + [d.reshape(-1, LANES) for d in decay_parts], axis=0)


def _unpack_small(packed, like):
    out, off = [], 0
    flat = packed[:SMALL_ROWS].reshape(-1)
    for a in like:
        out.append(flat[off : off + a.size].reshape(a.shape))
        off += a.size
    return out


def kernel(x, norm1_g, w_in, w_decay_f, b_decay_f, w_decay_b, b_decay_b, gla_norm_g, gmlp_ln_g, gmlp_ln_b, w_spatial, b_spatial, w_out, norm2_g, w_gate, w_up, w_down, final_norm_g, loss_target, m_norm1_g, m_w_in, m_w_decay_f, m_b_decay_f, m_w_decay_b, m_b_decay_b, m_gla_norm_g, m_gmlp_ln_g, m_gmlp_ln_b, m_w_spatial, m_b_spatial, m_w_out, m_norm2_g, m_w_gate, m_w_up, m_w_down, m_final_norm_g, v_norm1_g, v_w_in, v_w_decay_f, v_b_decay_f, v_w_decay_b, v_b_decay_b, v_gla_norm_g, v_gmlp_ln_g, v_gmlp_ln_b, v_w_spatial, v_b_spatial, v_w_out, v_norm2_g, v_w_gate, v_w_up, v_w_down, v_final_norm_g):
    args = dict(locals())
    cx, cy, cc = lax.axis_index("x"), lax.axis_index("y"), lax.axis_index("c")
    shard = 2 * cx + cy
    xs = x[0]
    target = loss_target[0]

    big_names = ["w_in", "w_out", "w_gate", "w_up", "w_down"]
    transposed = ("w_in", "w_gate", "w_up")
    rows_of = lambda pre, k: jnp.transpose(args[pre + k][0]) if k in transposed else args[pre + k][0]
    big_shards = {k: rows_of("", k) for k in big_names}
    c_arr = cc.reshape(1).astype(jnp.int32)
    s_arr = shard.reshape(1).astype(jnp.int32)
    sc_arr = jnp.stack([shard, cc]).astype(jnp.int32)
    slots = {k: _cast_into_slot(big_shards[k], s_arr) for k in big_names}
    (w_in4,) = _gather_sync([slots["w_in"]])
    w_in_t = w_in4.reshape(PROJ_W, D_MODEL)

    dec_block = jnp.concatenate([w_decay_f[0].reshape(-1, LANES), w_decay_b[0].reshape(-1, LANES)], axis=0)
    dec_all = _allgather_small(dec_block)
    late = ["w_out", "w_gate", "w_up", "w_down"]
    g_send, g_recv, late_bufs, token_gather = _gather_start([slots[k] for k in late], (w_in4, dec_all))
    dec_all = dec_all[::2].reshape(N_SHARDS, 2, LOWRANK, KEY_W // N_SHARDS)
    wdf_full = jnp.transpose(dec_all[:, 0], (1, 0, 2)).reshape(LOWRANK, KEY_W)
    wdb_full = jnp.transpose(dec_all[:, 1], (1, 0, 2)).reshape(LOWRANK, KEY_W)
    wd_pad_f = jnp.zeros((LANES, KEY_W), F32).at[0:LOWRANK].set(wdf_full).astype(BF16)
    wd_pad_b = jnp.zeros((LANES, KEY_W), F32).at[LOWRANK : 2 * LOWRANK].set(wdb_full).astype(BF16)

    ws_bf = w_spatial[0].astype(BF16)
    wst_bf = jnp.transpose(w_spatial[0], (0, 2, 1)).astype(BF16)
    bs_col = b_spatial[0].reshape(GMLP_GROUPS, GMLP_CHUNK, 1)

    p = _inproj(xs, norm1_g, w_in_t, token_gather)
    o_f, st_f = _gla_fwd(p, wd_pad_f, b_decay_f, reverse=False)
    o_b, st_b = _gla_fwd(p, wd_pad_b, b_decay_b, reverse=True)
    late_bufs = _gather_forward(_gather_wait(g_send, g_recv, late_bufs, (o_f, o_b)))
    w_out_full, wg_t, wu_t, wd = [b.reshape(-1, D_MODEL) for b in late_bufs]
    x1, ycat = _mixer_out(xs, o_f, o_b, p, gla_norm_g, gmlp_ln_g, gmlp_ln_b, ws_bf, bs_col, w_out_full)
    gf = final_norm_g.reshape(1, D_MODEL)
    h2, gate, up, act, dx2, loss_acc, dgf = _ffn_fwd(x1, target, norm2_g, gf, wg_t, wu_t, wd)

    dgate, dup, dx1, dg2 = _ffn_bwd(dx2, gate, up, x1, norm2_g, wg_t, wu_t, wd)
    ffn_grads4 = [g.reshape(N_SHARDS, FF_SHARD, D_MODEL) for g in _ffn_wgrad(h2, dgate, dup, act, dx2)]
    ffn_parts = [_add_halves(g, r, c_arr) for g, r in zip(ffn_grads4, _exchange_halves(ffn_grads4))]
    s_send, s_recv, s_parts, s_lands, token_scatter = _scatter_start([pb for _, pb in ffn_parts])
    do, dg, du, dvv, dwo, dgn, dlng, dlnb, dws, dbs = _mixer_bwd(
        dx1, ycat, o_f, o_b, p, gla_norm_g, gmlp_ln_g, gmlp_ln_b, ws_bf, wst_bf, bs_col, w_out_full, token_scatter)
    dq_f, dk_f, dv_f, dlr_f, dwdec_f, dbdec_f = _gla_bwd(p, do, st_f, wd_pad_f, b_decay_f, reverse=False)
    dq_b, dk_b, dv_b, dlr_b, dwdec_b, dbdec_b = _gla_bwd(p, do, st_b, wd_pad_b, b_decay_b, reverse=True)
    dx, dwin_t, dg1 = _inproj_bwd(xs, dx1, norm1_g, w_in_t, dq_f, dq_b, dk_f, dk_b, dv_f, dv_b, dg, du, dvv, dlr_f, dlr_b)
    ffn_recv = _scatter_wait(s_send, s_recv, s_parts, s_lands, (dwin_t,))

    dwin4 = dwin_t.reshape(N_SHARDS, PROJ_W // N_SHARDS, D_MODEL)
    dwo4 = dwo.reshape(N_SHARDS, D_MODEL // N_SHARDS, D_MODEL)
    proj_grads4 = [dwin4, dwo4]
    proj_parts = [_add_halves(g, r, c_arr) for g, r in zip(proj_grads4, _exchange_halves(proj_grads4))]
    proj_recv = _scatter_partials([pb for _, pb in proj_parts])
    parts_f32 = [pf for pf, _ in proj_parts + ffn_parts]
    bufs = [_add_partials(pf, r, sc_arr) for pf, r in zip(parts_f32, list(proj_recv) + ffn_recv)]
    big_grads = dict(zip(big_names, _join_halves(bufs)))

    dwdec_f16 = dwdec_f[0:LOWRANK]
    dwdec_b16 = dwdec_b[LOWRANK : 2 * LOWRANK]
    shard_major = lambda a: jnp.transpose(a.reshape(LOWRANK, N_SHARDS, KEY_W // N_SHARDS), (1, 0, 2))
    small_grads = {
        "norm1_g": dg1, "b_decay_f": dbdec_f, "b_decay_b": dbdec_b, "gla_norm_g": dgn, "gmlp_ln_g": dlng, "gmlp_ln_b": dlnb,
        "w_spatial": dws, "b_spatial": dbs, "norm2_g": dg2, "final_norm_g": dgf,
    }
    g_pack = _pack_small([small_grads[k] for k in SMALL_NAMES] + [loss_acc], [shard_major(dwdec_f16), shard_major(dwdec_b16)])
    g_all = _allgather_small(g_pack)
    pack_own = lambda pre: _pack_small([args[pre + k] for k in SMALL_NAMES], [args[pre + "w_decay_f"], args[pre + "w_decay_b"]])
    sg, sd, sm, sv = _adamw_small(g_all, pack_own(""), pack_own("m_"), pack_own("v_"))

    names = ["norm1_g", "w_in", "w_decay_f", "b_decay_f", "w_decay_b", "b_decay_b", "gla_norm_g", "gmlp_ln_g", "gmlp_ln_b",
             "w_spatial", "b_spatial", "w_out", "norm2_g", "w_gate", "w_up", "w_down", "final_norm_g"]
    like = [args[k] for k in SMALL_NAMES]
    results = {"g": {}, "d": {}, "m": {}, "v": {}}
    for tag, packed in (("g", sg), ("d", sd), ("m", sm), ("v", sv)):
        for k, a in zip(SMALL_NAMES, _unpack_small(packed, like)):
            results[tag][k] = a
        results[tag]["w_decay_f"] = packed[SMALL_ROWS : SMALL_ROWS + DECAY_ROWS].reshape(w_decay_f.shape)
        results[tag]["w_decay_b"] = packed[SMALL_ROWS + DECAY_ROWS :].reshape(w_decay_b.shape)
    for k in big_names:
        g, d, mo, vo = _adamw(big_shards[k], big_grads[k], rows_of("m_", k), rows_of("v_", k))
        for tag, a in (("g", g), ("d", d), ("m", mo), ("v", vo)):
            results[tag][k] = (jnp.transpose(a) if k in transposed else a).reshape(args[k].shape)

    loss = sg[:SMALL_ROWS].reshape(-1)[sum(a.size for a in like)]
    grad_x = dx.reshape(x.shape)
    return (loss, grad_x, *[results["g"][k] for k in names], *[results["d"][k] for k in names],
            *[results["m"][k] for k in names], *[results["v"][k] for k in names])
```

```python
import functools
import math

import jax
import jax.numpy as jnp
from jax import lax
from jax.experimental import pallas as pl
from jax.experimental.pallas import tpu as pltpu

F32, BF16 = jnp.float32, jnp.bfloat16

D_MODEL = 1024
GLA_HEADS = 4
GLA_DK = 64
GLA_DV = 128
KEY_W = GLA_HEADS * GLA_DK
GLA_W = GLA_HEADS * GLA_DV
GMLP_W = 512
GMLP_GROUPS = 4
GMLP_CHUNK = 128
LOWRANK = 16
GLA_CHUNK = 64
GLA_TAU = 16.0
PROJ_W = 2592
PROJ_WP = 2688
D_FF = 2816
N_SHARDS = 4
FF_SHARD = D_FF // N_SHARDS
EPS = 1e-6
LANES = 128
TOKEN_SHAPE = (8, LANES)
MIB = 1024 * 1024

ADAM_LR = 0.001
ADAM_B1 = 0.9
ADAM_B2 = 0.999
ADAM_EPS = 1e-08
ADAM_WD = 0.01
ADAM_STEP = 10

COL_Q, COL_K = 0, 256
COL_V, COL_G, COL_U, COL_VV = 512, 1024, 1536, 2048
COL_LR = 2560
ROW_LR, ROW_UV = 1536, 1568
HALF = D_MODEL // 2

MESH = pl.DeviceIdType.MESH


def _nn(a, b):
    return jnp.dot(a, b, preferred_element_type=F32)


def _nt(a, b):
    return lax.dot_general(a, b, (((1,), (1,)), ((), ())), preferred_element_type=F32)


def _tn(a, b):
    return lax.dot_general(a, b, (((0,), (0,)), ((), ())), preferred_element_type=F32)


def _bnn(a, b):
    return jnp.einsum("nik,nkj->nij", a, b, preferred_element_type=F32)


def _bnt(a, b):
    return jnp.einsum("nik,njk->nij", a, b, preferred_element_type=F32)


def _btn(a, b):
    return jnp.einsum("nki,nkj->nij", a, b, preferred_element_type=F32)


def _resident(shape):
    zeros = (0,) * len(shape)
    return pl.BlockSpec(shape, lambda *_: zeros, pipeline_mode=pl.Buffered(1))


def _params(vmem_mib, semantics=("arbitrary",)):
    return pltpu.CompilerParams(vmem_limit_bytes=vmem_mib * MIB, dimension_semantics=semantics)


def _sigmoid(x):
    return 1.0 / (1.0 + jnp.exp(-x))


def _gelu(x):
    return 0.5 * x * (1.0 + lax.erf(x * (1.0 / math.sqrt(2.0))))


def _gelu_grad(x):
    return 0.5 * (1.0 + lax.erf(x * (1.0 / math.sqrt(2.0)))) + x * jnp.exp(-0.5 * x * x) * (1.0 / math.sqrt(2.0 * math.pi))


def _log_sigmoid(x):
    return jnp.minimum(x, 0.0) - jnp.log(1.0 + jnp.exp(-jnp.abs(x)))


def _rms_bwd(dxh, xh, r):
    return r * (dxh - xh * jnp.mean(dxh * xh, axis=-1, keepdims=True))


def _chunk_cumsum(v, row_in_chunk, reverse):
    rows = v.shape[0]
    for sh in (1, 2, 4, 8, 16, 32):
        if reverse:
            v = v + jnp.where(row_in_chunk + sh < GLA_CHUNK, pltpu.roll(v, rows - sh, axis=0), 0.0)
        else:
            v = v + jnp.where(row_in_chunk >= sh, pltpu.roll(v, sh, axis=0), 0.0)
    return v


def _inproj(x, g1, w_in_t, token):
    seq = x.shape[0]
    tm = min(seq, 512)

    def body(x_ref, g_ref, w_ref, token_ref, p_ref):
        xv = x_ref[...]
        r = lax.rsqrt(jnp.mean(xv * xv, axis=-1, keepdims=True) + EPS)
        h = (xv * r * g_ref[...]).astype(BF16)
        p_ref[:, 0:COL_U] = _nt(h, w_ref[0:ROW_LR, :])
        p_ref[:, COL_U:COL_LR] = _nt(h, w_ref[ROW_UV:PROJ_W, :])
        p_ref[:, COL_LR:PROJ_WP] = _nt(h, w_ref[ROW_LR : ROW_LR + LANES, :])

    return pl.pallas_call(
        body,
        name="inproj",
        grid=(seq // tm,),
        in_specs=[pl.BlockSpec((tm, D_MODEL), lambda i: (i, 0)), _resident((1, D_MODEL)), _resident((PROJ_W, D_MODEL)), _resident(TOKEN_SHAPE)],
        out_specs=pl.BlockSpec((tm, PROJ_WP), lambda i: (i, 0)),
        out_shape=jax.ShapeDtypeStruct((seq, PROJ_WP), F32),
        compiler_params=_params(48, ("parallel",)),
    )(x, g1, w_in_t, token)


def _gla_tile(seq):
    return min(seq, 512)


def _gla_decay_terms(lr_bf, wd_ref, bd_ref, pair, row_in_chunk, reverse, n):
    cols = pl.ds(pair * LANES, LANES)
    pre = _nn(lr_bf, wd_ref[:, cols]) + bd_ref[:, cols]
    la = _log_sigmoid(pre) * (1.0 / GLA_TAU)
    b = _chunk_cumsum(la, row_in_chunk, reverse)
    b3 = b.reshape(n, GLA_CHUNK, LANES)
    blast = b3[:, 0:1, :] if reverse else b3[:, GLA_CHUNK - 1 : GLA_CHUNK, :]
    return pre, b3, blast


def _gla_fwd(p, wd_pad, bd, reverse):
    seq = p.shape[0]
    tg = _gla_tile(seq)
    nt = seq // tg
    n = tg // GLA_CHUNK
    scale = GLA_DK**-0.5

    def tile(i):
        return nt - 1 - i if reverse else i

    def body(q_ref, k_ref, v_ref, lr_ref, wd_ref, bd_ref, o_ref, st_ref, carry):
        @pl.when(pl.program_id(0) == 0)
        def _():
            carry[...] = jnp.zeros_like(carry)

        lr_bf = lr_ref[...].astype(BF16)
        states = [carry[h] for h in range(GLA_HEADS)]
        row_in_chunk = lax.broadcasted_iota(jnp.int32, (tg, LANES), 0) % GLA_CHUNK
        lane_head = lax.broadcasted_iota(jnp.int32, (1, LANES), 1) // GLA_DK
        tt = lax.broadcasted_iota(jnp.int32, (GLA_CHUNK, GLA_CHUNK), 0)
        ss = lax.broadcasted_iota(jnp.int32, (GLA_CHUNK, GLA_CHUNK), 1)
        causal = (tt <= ss) if reverse else (tt >= ss)
        order = range(n - 1, -1, -1) if reverse else range(n)
        heads = range(GLA_HEADS)
        qdh, kds, vhs, decs, sc_raw, dst = {}, {}, {}, {}, {}, {}
        for pair in range(2):
            cols = pl.ds(pair * LANES, LANES)
            _, b3, blast = _gla_decay_terms(lr_bf, wd_ref, bd_ref, pair, row_in_chunk, reverse, n)
            q3 = q_ref[:, cols].reshape(n, GLA_CHUNK, LANES) * scale
            k3 = k_ref[:, cols].reshape(n, GLA_CHUNK, LANES)
            qd = q3 * jnp.exp(b3)
            kd = (k3 * jnp.exp(-b3)).astype(BF16)
            kte = k3 * jnp.exp(blast - b3)
            dec = jnp.exp(blast)
            for hh in range(2):
                h = 2 * pair + hh
                m = (lane_head == hh).astype(F32)
                qdh[h], kds[h], decs[h] = (qd * m).astype(BF16), kd, dec
                vhs[h] = v_ref[:, pl.ds(h * GLA_DV, GLA_DV)].reshape(n, GLA_CHUNK, GLA_DV).astype(BF16)
                sc_raw[h] = _bnt(qdh[h], kd)
                dst[h] = _btn(vhs[h], (kte * m).astype(BF16))
        o_intra, befores = {}, {}
        for h in heads:
            o_intra[h] = _bnn(jnp.where(causal, sc_raw[h], 0.0).astype(BF16), vhs[h])
            st, before = states[h], [None] * n
            for j in order:
                before[j] = st
                st = st * decs[h][j] + dst[h][j]
            states[h] = st
            befores[h] = jnp.stack(before)
        outs = {h: (o_intra[h] + _bnt(qdh[h], befores[h].astype(BF16))).reshape(tg, GLA_DV) for h in heads}
        for h in range(GLA_HEADS):
            o_ref[:, pl.ds(h * GLA_DV, GLA_DV)] = outs[h]
            st_ref[:, h] = befores[h]
            carry[h] = states[h]

    nchunks = seq // GLA_CHUNK
    return pl.pallas_call(
        body,
        name="gla_fwd_rev" if reverse else "gla_fwd",
        grid=(nt,),
        in_specs=[
            pl.BlockSpec((tg, KEY_W), lambda i: (tile(i), COL_Q // KEY_W)),
            pl.BlockSpec((tg, KEY_W), lambda i: (tile(i), COL_K // KEY_W)),
            pl.BlockSpec((tg, GLA_W), lambda i: (tile(i), COL_V // GLA_W)),
            pl.BlockSpec((tg, LANES), lambda i: (tile(i), COL_LR // LANES)),
            _resident((LANES, KEY_W)),
            _resident((1, KEY_W)),
        ],
        out_specs=[
            pl.BlockSpec((tg, GLA_W), lambda i: (tile(i), 0)),
            pl.BlockSpec((n, GLA_HEADS, GLA_DV, LANES), lambda i: (tile(i), 0, 0, 0)),
        ],
        out_shape=[
            jax.ShapeDtypeStruct((seq, GLA_W), F32),
            jax.ShapeDtypeStruct((nchunks, GLA_HEADS, GLA_DV, LANES), F32),
        ],
        scratch_shapes=[pltpu.VMEM((GLA_HEADS, GLA_DV, LANES), F32)],
        compiler_params=_params(48),
    )(p, p, p, p, wd_pad, bd)


def _mixer_out(x, o_f, o_b, p, gn, lng, lnb, ws_bf, bs_col, w_out):
    seq = x.shape[0]
    tm = min(seq, 512)

    def body(x_ref, of_ref, ob_ref, g_ref, u_ref, vv_ref, gn_ref, lng_ref, lnb_ref, ws_ref, bs_ref, wo_ref, x1_ref, yc_ref, vn_sc):
        for h in range(GLA_HEADS):
            cols = pl.ds(h * GLA_DV, GLA_DV)
            oh = of_ref[:, cols] + ob_ref[:, cols]
            on = oh * lax.rsqrt(jnp.mean(oh * oh, axis=-1, keepdims=True) + EPS)
            gh = g_ref[:, cols]
            yc_ref[:, cols] = (on * gn_ref[:, cols] * (gh * _sigmoid(gh))).astype(BF16)
        zv = _gelu(vv_ref[...])
        xc = zv - jnp.mean(zv, axis=-1, keepdims=True)
        vhat = xc * lax.rsqrt(jnp.mean(xc * xc, axis=-1, keepdims=True) + EPS)
        vn_sc[...] = (vhat * lng_ref[...] + lnb_ref[...]).astype(BF16)
        for c in range(tm // GMLP_CHUNK):
            rows = pl.ds(c * GMLP_CHUNK, GMLP_CHUNK)
            for g in range(GMLP_GROUPS):
                cols = pl.ds(g * LANES, LANES)
                s = _nn(ws_ref[g], vn_sc[rows, cols]) + bs_ref[g]
                yc_ref[rows, pl.ds(GLA_W + g * LANES, LANES)] = (_gelu(u_ref[rows, cols]) * s).astype(BF16)
        x1_ref[...] = x_ref[...] + _nn(yc_ref[...], wo_ref[...])

    row = lambda w: pl.BlockSpec((tm, w), lambda i: (i, 0))
    pcol = lambda col: pl.BlockSpec((tm, GLA_W), lambda i: (i, col // GLA_W))
    return pl.pallas_call(
        body,
        name="mixer_out",
        grid=(seq // tm,),
        in_specs=[
            row(D_MODEL), row(GLA_W), row(GLA_W), pcol(COL_G), pcol(COL_U), pcol(COL_VV),
            _resident((1, GLA_W)), _resident((1, GMLP_W)), _resident((1, GMLP_W)),
            _resident((GMLP_GROUPS, GMLP_CHUNK, GMLP_CHUNK)), _resident((GMLP_GROUPS, GMLP_CHUNK, 1)),
            _resident((D_MODEL, D_MODEL)),
        ],
        out_specs=[row(D_MODEL), row(D_MODEL)],
        out_shape=[jax.ShapeDtypeStruct((seq, D_MODEL), F32), jax.ShapeDtypeStruct((seq, D_MODEL), BF16)],
        scratch_shapes=[pltpu.VMEM((tm, GMLP_W), BF16)],
        compiler_params=_params(48, ("parallel",)),
    )(x, o_f, o_b, p, p, p, gn, lng, lnb, ws_bf, bs_col, w_out)


def _ffn_fwd(x1, target, g2, gf, wg_t, wu_t, wd):
    seq = x1.shape[0]
    tm = min(seq, 256)

    def body(x1_ref, t_ref, g2_ref, gf_ref, wg_ref, wu_ref, wd_ref, h2_ref, gate_ref, up_ref, act_ref, dx2_ref, loss_ref, dgf_ref):
        @pl.when(pl.program_id(0) == 0)
        def _():
            loss_ref[...] = jnp.zeros_like(loss_ref)
            dgf_ref[...] = jnp.zeros_like(dgf_ref)

        x1v = x1_ref[...]
        h2 = (x1v * lax.rsqrt(jnp.mean(x1v * x1v, axis=-1, keepdims=True) + EPS) * g2_ref[...]).astype(BF16)
        h2_ref[...] = h2
        gate = _nt(h2, wg_ref[...])
        up = _nt(h2, wu_ref[...])
        act = (gate * _sigmoid(gate) * up).astype(BF16)
        gate_ref[...] = gate
        up_ref[...] = up
        act_ref[...] = act
        x2 = x1v + _nn(act, wd_ref[...])
        rf = lax.rsqrt(jnp.mean(x2 * x2, axis=-1, keepdims=True) + EPS)
        xh = x2 * rf
        err = xh * gf_ref[...] - t_ref[...]
        loss_ref[...] += 0.5 * jnp.sum(jnp.mean(err * err, axis=-1, keepdims=True))
        dy = err * (1.0 / D_MODEL)
        dgf_ref[...] += jnp.sum(dy * xh, axis=0, keepdims=True)
        dx2_ref[...] = _rms_bwd(dy * gf_ref[...], xh, rf)

    row = lambda w: pl.BlockSpec((tm, w), lambda i: (i, 0))
    weight = _resident((D_FF, D_MODEL))
    return pl.pallas_call(
        body,
        name="ffn_fwd",
        grid=(seq // tm,),
        in_specs=[row(D_MODEL), row(D_MODEL), _resident((1, D_MODEL)), _resident((1, D_MODEL)), weight, weight, weight],
        out_specs=[row(D_MODEL), row(D_FF), row(D_FF), row(D_FF), row(D_MODEL),
                   pl.BlockSpec((1, LANES), lambda i: (0, 0)), pl.BlockSpec((1, D_MODEL), lambda i: (0, 0))],
        out_shape=[
            jax.ShapeDtypeStruct((seq, D_MODEL), BF16),
            jax.ShapeDtypeStruct((seq, D_FF), F32),
            jax.ShapeDtypeStruct((seq, D_FF), F32),
            jax.ShapeDtypeStruct((seq, D_FF), BF16),
            jax.ShapeDtypeStruct((seq, D_MODEL), F32),
            jax.ShapeDtypeStruct((1, LANES), F32),
            jax.ShapeDtypeStruct((1, D_MODEL), F32),
        ],
        compiler_params=_params(56),
    )(x1, target, g2, gf, wg_t, wu_t, wd)


def _ffn_bwd(dx2, gate, up, x1, g2, wg_t, wu_t, wd):
    seq = x1.shape[0]
    tm = min(seq, 256)

    def body(dx2_ref, gate_ref, up_ref, x1_ref, g2_ref, wg_ref, wu_ref, wd_ref, dgate_ref, dup_ref, dx1_ref, dg2_ref):
        @pl.when(pl.program_id(0) == 0)
        def _():
            dg2_ref[...] = jnp.zeros_like(dg2_ref)

        dx2v = dx2_ref[...]
        dact = _nt(dx2v.astype(BF16), wd_ref[...])
        gate = gate_ref[...]
        sg = _sigmoid(gate)
        dgate = (dact * up_ref[...] * (sg * (1.0 + gate * (1.0 - sg)))).astype(BF16)
        dup = (dact * (gate * sg)).astype(BF16)
        dgate_ref[...] = dgate
        dup_ref[...] = dup
        dh2 = _nn(dgate, wg_ref[...]) + _nn(dup, wu_ref[...])
        x1v = x1_ref[...]
        r2 = lax.rsqrt(jnp.mean(x1v * x1v, axis=-1, keepdims=True) + EPS)
        xh = x1v * r2
        dg2_ref[...] += jnp.sum(dh2 * xh, axis=0, keepdims=True)
        dx1_ref[...] = dx2v + _rms_bwd(dh2 * g2_ref[...], xh, r2)

    row = lambda w: pl.BlockSpec((tm, w), lambda i: (i, 0))
    weight = _resident((D_FF, D_MODEL))
    return pl.pallas_call(
        body,
        name="ffn_bwd",
        grid=(seq // tm,),
        in_specs=[row(D_MODEL), row(D_FF), row(D_FF), row(D_MODEL), _resident((1, D_MODEL)), weight, weight, weight],
        out_specs=[row(D_FF), row(D_FF), row(D_MODEL), pl.BlockSpec((1, D_MODEL), lambda i: (0, 0))],
        out_shape=[
            jax.ShapeDtypeStruct((seq, D_FF), BF16),
            jax.ShapeDtypeStruct((seq, D_FF), BF16),
            jax.ShapeDtypeStruct((seq, D_MODEL), F32),
            jax.ShapeDtypeStruct((1, D_MODEL), F32),
        ],
        compiler_params=_params(56),
    )(dx2, gate, up, x1, g2, wg_t, wu_t, wd)


WGRAD_ROWS = D_FF // 2


def _ffn_wgrad(h2, dgate, dup, act, dx2):
    seq = h2.shape[0]
    tm = min(seq, 512)

    def body(h2_ref, dgate_ref, dup_ref, act_ref, dx2_ref, dwg_ref, dwu_ref, dwd_ref):
        @pl.when(pl.program_id(1) == 0)
        def _():
            dwg_ref[...] = jnp.zeros_like(dwg_ref)
            dwu_ref[...] = jnp.zeros_like(dwu_ref)
            dwd_ref[...] = jnp.zeros_like(dwd_ref)

        h2v = h2_ref[...]
        dwg_ref[...] += _tn(dgate_ref[...], h2v)
        dwu_ref[...] += _tn(dup_ref[...], h2v)
        dwd_ref[...] += _tn(act_ref[...], dx2_ref[...].astype(BF16))

    ff = pl.BlockSpec((tm, WGRAD_ROWS), lambda j, i: (i, j))
    row = pl.BlockSpec((tm, D_MODEL), lambda j, i: (i, 0))
    out = pl.BlockSpec((WGRAD_ROWS, D_MODEL), lambda j, i: (j, 0))
    return pl.pallas_call(
        body,
        name="ffn_wgrad",
        grid=(D_FF // WGRAD_ROWS, seq // tm),
        in_specs=[row, ff, ff, ff, row],
        out_specs=[out, out, out],
        out_shape=[jax.ShapeDtypeStruct((D_FF, D_MODEL), F32)] * 3,
        compiler_params=_params(56, ("parallel", "arbitrary")),
    )(h2, dgate, dup, act, dx2)


def _mixer_bwd(dx1, ycat, o_f, o_b, p, gn, lng, lnb, ws_bf, wst_bf, bs_col, w_out, token):
    seq = dx1.shape[0]
    tm = min(seq, 512)
    nsteps = seq // tm

    def body(dx1_ref, yc_ref, of_ref, ob_ref, g_ref, u_ref, vv_ref, gn_ref, lng_ref, lnb_ref, ws_ref, wst_ref, bs_ref, wo_ref, token_ref,
             do_ref, dg_ref, du_ref, dvv_ref, dwo_ref, dgn_ref, dlng_ref, dlnb_ref, dws_ref, dbs_ref, vn_sc, dvn_sc, dbs_acc):
        step = pl.program_id(0)

        @pl.when(step == 0)
        def _():
            for r in (dwo_ref, dgn_ref, dlng_ref, dlnb_ref, dws_ref, dbs_acc):
                r[...] = jnp.zeros_like(r)

        dx1b = dx1_ref[...].astype(BF16)
        dyc = _nt(dx1b, wo_ref[...])
        dwo_ref[...] += _tn(yc_ref[...], dx1b)
        for h in range(GLA_HEADS):
            cols = pl.ds(h * GLA_DV, GLA_DV)
            dya = dyc[:, h * GLA_DV : (h + 1) * GLA_DV]
            oh = of_ref[:, cols] + ob_ref[:, cols]
            rn = lax.rsqrt(jnp.mean(oh * oh, axis=-1, keepdims=True) + EPS)
            on = oh * rn
            gh = g_ref[:, cols]
            sg = _sigmoid(gh)
            sil = gh * sg
            gnh = gn_ref[:, cols]
            dgn_ref[:, cols] += jnp.sum(dya * on * sil, axis=0, keepdims=True)
            dg_ref[:, cols] = dya * on * gnh * (sg * (1.0 + gh * (1.0 - sg)))
            do_ref[:, cols] = _rms_bwd(dya * gnh * sil, on, rn)
        vv = vv_ref[...]
        zv = _gelu(vv)
        xc = zv - jnp.mean(zv, axis=-1, keepdims=True)
        rstd = lax.rsqrt(jnp.mean(xc * xc, axis=-1, keepdims=True) + EPS)
        vhat = xc * rstd
        vn_sc[...] = (vhat * lng_ref[...] + lnb_ref[...]).astype(BF16)
        for c in range(tm // GMLP_CHUNK):
            rows = pl.ds(c * GMLP_CHUNK, GMLP_CHUNK)
            for g in range(GMLP_GROUPS):
                cols = pl.ds(g * LANES, LANES)
                vn = vn_sc[rows, cols]
                s = _nn(ws_ref[g], vn) + bs_ref[g]
                dyb = dyc[c * GMLP_CHUNK : (c + 1) * GMLP_CHUNK, GLA_W + g * LANES : GLA_W + (g + 1) * LANES]
                u = u_ref[rows, cols]
                du_ref[rows, cols] = dyb * s * _gelu_grad(u)
                ds = dyb * _gelu(u)
                dbs_acc[g] += ds
                dsb = ds.astype(BF16)
                dws_ref[g] += _nt(dsb, vn)
                dvn_sc[rows, cols] = _nn(wst_ref[g], dsb)
        dvn = dvn_sc[...]
        dlng_ref[...] += jnp.sum(dvn * vhat, axis=0, keepdims=True)
        dlnb_ref[...] += jnp.sum(dvn, axis=0, keepdims=True)
        dvh = dvn * lng_ref[...]
        dzv = rstd * (dvh - jnp.mean(dvh, axis=-1, keepdims=True) - vhat * jnp.mean(dvh * vhat, axis=-1, keepdims=True))
        dvv_ref[...] = dzv * _gelu_grad(vv)

        @pl.when(step == nsteps - 1)
        def _():
            dbs_ref[...] = jnp.sum(dbs_acc[...], axis=-1, keepdims=True)

    row = lambda w: pl.BlockSpec((tm, w), lambda i: (i, 0))
    pcol = lambda col: pl.BlockSpec((tm, GLA_W), lambda i: (i, col // GLA_W))
    const = lambda shape: pl.BlockSpec(shape, lambda i: (0,) * len(shape))
    return pl.pallas_call(
        body,
        name="mixer_bwd",
        grid=(nsteps,),
        in_specs=[
            row(D_MODEL), row(D_MODEL), row(GLA_W), row(GLA_W), pcol(COL_G), pcol(COL_U), pcol(COL_VV),
            _resident((1, GLA_W)), _resident((1, GMLP_W)), _resident((1, GMLP_W)),
            _resident((GMLP_GROUPS, GMLP_CHUNK, GMLP_CHUNK)), _resident((GMLP_GROUPS, GMLP_CHUNK, GMLP_CHUNK)),
            _resident((GMLP_GROUPS, GMLP_CHUNK, 1)), _resident((D_MODEL, D_MODEL)), _resident(TOKEN_SHAPE),
        ],
        out_specs=[
            row(GLA_W), row(GLA_W), row(GMLP_W), row(GMLP_W), const((D_MODEL, D_MODEL)),
            const((1, GLA_W)), const((1, GMLP_W)), const((1, GMLP_W)),
            const((GMLP_GROUPS, GMLP_CHUNK, GMLP_CHUNK)), const((GMLP_GROUPS, GMLP_CHUNK, 1)),
        ],
        out_shape=[
            jax.ShapeDtypeStruct((seq, GLA_W), F32), jax.ShapeDtypeStruct((seq, GLA_W), F32),
            jax.ShapeDtypeStruct((seq, GMLP_W), F32), jax.ShapeDtypeStruct((seq, GMLP_W), F32),
            jax.ShapeDtypeStruct((D_MODEL, D_MODEL), F32),
            jax.ShapeDtypeStruct((1, GLA_W), F32), jax.ShapeDtypeStruct((1, GMLP_W), F32), jax.ShapeDtypeStruct((1, GMLP_W), F32),
            jax.ShapeDtypeStruct((GMLP_GROUPS, GMLP_CHUNK, GMLP_CHUNK), F32), jax.ShapeDtypeStruct((GMLP_GROUPS, GMLP_CHUNK, 1), F32),
        ],
        scratch_shapes=[pltpu.VMEM((tm, GMLP_W), BF16), pltpu.VMEM((tm, GMLP_W), F32), pltpu.VMEM((GMLP_GROUPS, GMLP_CHUNK, GMLP_CHUNK), F32)],
        compiler_params=_params(56),
    )(dx1, ycat, o_f, o_b, p, p, p, gn, lng, lnb, ws_bf, wst_bf, bs_col, w_out, token)


def _gla_bwd(p, do, st, wd_pad, bd, token, reverse):
    seq = p.shape[0]
    tg = _gla_tile(seq)
    nt = seq // tg
    n = tg // GLA_CHUNK
    scale = GLA_DK**-0.5

    def tile(i):
        return i if reverse else nt - 1 - i

    def body(q_ref, k_ref, v_ref, lr_ref, do_ref, st_ref, wd_ref, bd_ref, token_ref, dq_ref, dk_ref, dv_ref, dlr_ref, dwd_ref, dbd_ref, carry):
        @pl.when(pl.program_id(0) == 0)
        def _():
            carry[...] = jnp.zeros_like(carry)
            dwd_ref[...] = jnp.zeros_like(dwd_ref)
            dbd_ref[...] = jnp.zeros_like(dbd_ref)

        lr_bf = lr_ref[...].astype(BF16)
        carries = [carry[h] for h in range(GLA_HEADS)]
        row_in_chunk = lax.broadcasted_iota(jnp.int32, (tg, LANES), 0) % GLA_CHUNK
        lane_head = lax.broadcasted_iota(jnp.int32, (1, LANES), 1) // GLA_DK
        tt = lax.broadcasted_iota(jnp.int32, (GLA_CHUNK, GLA_CHUNK), 0)
        ss = lax.broadcasted_iota(jnp.int32, (GLA_CHUNK, GLA_CHUNK), 1)
        causal = (tt <= ss) if reverse else (tt >= ss)
        causal_t = (tt >= ss) if reverse else (tt <= ss)
        order = range(n) if reverse else range(n - 1, -1, -1)
        dlr = jnp.zeros((tg, LANES), F32)
        heads = range(GLA_HEADS)
        pv, masks, qdh, kteh, vhs, dohs, stbs = {}, {}, {}, {}, {}, {}, {}
        sc_t, dp, dp_t, acc = {}, {}, {}, {}
        for pair in range(2):
            cols = pl.ds(pair * LANES, LANES)
            pre, b3, blast = _gla_decay_terms(lr_bf, wd_ref, bd_ref, pair, row_in_chunk, reverse, n)
            q3 = q_ref[:, cols].reshape(n, GLA_CHUNK, LANES) * scale
            k3 = k_ref[:, cols].reshape(n, GLA_CHUNK, LANES)
            eb = jnp.exp(b3)
            emb = jnp.exp(-b3)
            ekte = jnp.exp(blast - b3)
            kdf = k3 * emb
            pv[pair] = dict(pre=pre, eb=eb, emb=emb, ekte=ekte, qd=q3 * eb, kdf=kdf, kd=kdf.astype(BF16), kte=k3 * ekte, dec=jnp.exp(blast))
            for hh in range(2):
                h = 2 * pair + hh
                vcols = pl.ds(h * GLA_DV, GLA_DV)
                masks[h] = (lane_head == hh).astype(F32)
                qdh[h] = (pv[pair]["qd"] * masks[h]).astype(BF16)
                kteh[h] = (pv[pair]["kte"] * masks[h]).astype(BF16)
                vhs[h] = v_ref[:, vcols].reshape(n, GLA_CHUNK, GLA_DV).astype(BF16)
                dohs[h] = do_ref[:, vcols].reshape(n, GLA_CHUNK, GLA_DV).astype(BF16)
                stbs[h] = st_ref[:, h]
                sc_t[h] = _bnt(pv[pair]["kd"], qdh[h])
                dp[h] = _bnt(dohs[h], vhs[h])
                dp_t[h] = _bnt(vhs[h], dohs[h])
                acc[h] = _btn(dohs[h], qdh[h])
        dsa = {}
        for h in heads:
            sc_t[h] = jnp.where(causal_t, sc_t[h], 0.0).astype(BF16)
            dp[h] = jnp.where(causal, dp[h], 0.0).astype(BF16)
            dp_t[h] = jnp.where(causal_t, dp_t[h], 0.0).astype(BF16)
            dec = pv[h // 2]["dec"]
            c, after = carries[h], [None] * n
            for j in order:
                after[j] = c
                c = acc[h][j] + dec[j] * c
            carries[h] = c
            dsa[h] = jnp.stack(after)
        dvs, dqs, dks, dwds, dbds = [], [], [], [], []
        for pair in range(2):
            cols = pl.ds(pair * LANES, LANES)
            v = pv[pair]
            dqd = jnp.zeros((n, GLA_CHUNK, LANES), F32)
            dkd = jnp.zeros((n, GLA_CHUNK, LANES), F32)
            dkte = jnp.zeros((n, GLA_CHUNK, LANES), F32)
            ddec = jnp.zeros((n, 1, LANES), F32)
            for h in (2 * pair, 2 * pair + 1):
                dsa_bf = dsa[h].astype(BF16)
                dqd = dqd + (_bnn(dp[h], v["kd"]) * masks[h] + _bnn(dohs[h], stbs[h].astype(BF16)))
                dkd = dkd + _bnn(dp_t[h], qdh[h])
                dkte = dkte + _bnn(vhs[h], dsa_bf)
                ddec = ddec + jnp.sum(dsa[h] * stbs[h], axis=1, keepdims=True)
                dvs.append((_bnn(sc_t[h], dohs[h]) + _bnt(kteh[h], dsa_bf)).reshape(tg, GLA_DV))
            dqs.append((dqd * (scale * v["eb"])).reshape(tg, LANES))
            dks.append((dkd * v["emb"] + dkte * v["ekte"]).reshape(tg, LANES))
            db = dqd * v["qd"] - dkd * v["kdf"] - dkte * v["kte"]
            dblast = jnp.sum(dkte * v["kte"], axis=1, keepdims=True) + ddec * v["dec"]
            dla = _chunk_cumsum(db.reshape(tg, LANES), row_in_chunk, not reverse) + jnp.broadcast_to(dblast, (n, GLA_CHUNK, LANES)).reshape(tg, LANES)
            dpre = (dla * (1.0 / GLA_TAU) * _sigmoid(-v["pre"]))
            dpre_bf = dpre.astype(BF16)
            dlr = dlr + _nt(dpre_bf, wd_ref[:, cols])
            dwds.append(_tn(lr_bf, dpre_bf))
            dbds.append(jnp.sum(dpre, axis=0, keepdims=True))
        dlr_ref[...] = dlr
        for pair in range(2):
            cols = pl.ds(pair * LANES, LANES)
            dq_ref[:, cols] = dqs[pair]
            dk_ref[:, cols] = dks[pair]
            dwd_ref[:, cols] += dwds[pair]
            dbd_ref[:, cols] += dbds[pair]
        for h in range(GLA_HEADS):
            dv_ref[:, pl.ds(h * GLA_DV, GLA_DV)] = dvs[h]
            carry[h] = carries[h]

    return pl.pallas_call(
        body,
        name="gla_bwd_rev" if reverse else "gla_bwd",
        grid=(nt,),
        in_specs=[
            pl.BlockSpec((tg, KEY_W), lambda i: (tile(i), COL_Q // KEY_W)),
            pl.BlockSpec((tg, KEY_W), lambda i: (tile(i), COL_K // KEY_W)),
            pl.BlockSpec((tg, GLA_W), lambda i: (tile(i), COL_V // GLA_W)),
            pl.BlockSpec((tg, LANES), lambda i: (tile(i), COL_LR // LANES)),
            pl.BlockSpec((tg, GLA_W), lambda i: (tile(i), 0)),
            pl.BlockSpec((n, GLA_HEADS, GLA_DV, LANES), lambda i: (tile(i), 0, 0, 0)),
            _resident((LANES, KEY_W)),
            _resident((1, KEY_W)),
            _resident(TOKEN_SHAPE),
        ],
        out_specs=[
            pl.BlockSpec((tg, KEY_W), lambda i: (tile(i), 0)),
            pl.BlockSpec((tg, KEY_W), lambda i: (tile(i), 0)),
            pl.BlockSpec((tg, GLA_W), lambda i: (tile(i), 0)),
            pl.BlockSpec((tg, LANES), lambda i: (tile(i), 0)),
            pl.BlockSpec((LANES, KEY_W), lambda i: (0, 0)),
            pl.BlockSpec((1, KEY_W), lambda i: (0, 0)),
        ],
        out_shape=[
            jax.ShapeDtypeStruct((seq, KEY_W), F32), jax.ShapeDtypeStruct((seq, KEY_W), F32),
            jax.ShapeDtypeStruct((seq, GLA_W), F32), jax.ShapeDtypeStruct((seq, LANES), F32),
            jax.ShapeDtypeStruct((LANES, KEY_W), F32), jax.ShapeDtypeStruct((1, KEY_W), F32),
        ],
        scratch_shapes=[pltpu.VMEM((GLA_HEADS, GLA_DV, LANES), F32)],
        compiler_params=_params(48),
    )(p, p, p, p, do, st, wd_pad, bd, token)


def _inproj_bwd(x, dx1, g1, w_in_t, dq_f, dq_b, dk_f, dk_b, dv_f, dv_b, dg, du, dvv, dlr_f, dlr_b):
    seq = x.shape[0]
    tm = min(seq, 256)

    def body(x_ref, dx1_ref, g1_ref, w_ref, dqf, dqb, dkf, dkb, dvf, dvb, dg_ref, du_ref, dvv_ref, dlrf, dlrb, dx_ref, dw_ref, dg1_ref, dp_sc):
        @pl.when(pl.program_id(0) == 0)
        def _():
            dw_ref[...] = jnp.zeros_like(dw_ref)
            dg1_ref[...] = jnp.zeros_like(dg1_ref)

        dp_sc[:, COL_Q : COL_Q + KEY_W] = (dqf[...] + dqb[...]).astype(BF16)
        dp_sc[:, COL_K : COL_K + KEY_W] = (dkf[...] + dkb[...]).astype(BF16)
        dp_sc[:, COL_V : COL_V + GLA_W] = (dvf[...] + dvb[...]).astype(BF16)
        dp_sc[:, COL_G : COL_G + GLA_W] = dg_ref[...].astype(BF16)
        dp_sc[:, COL_U : COL_U + GMLP_W] = du_ref[...].astype(BF16)
        dp_sc[:, COL_VV : COL_VV + GMLP_W] = dvv_ref[...].astype(BF16)
        dp_sc[:, COL_LR : COL_LR + LANES] = (dlrf[...] + dlrb[...]).astype(BF16)
        xv = x_ref[...]
        r1 = lax.rsqrt(jnp.mean(xv * xv, axis=-1, keepdims=True) + EPS)
        xh = xv * r1
        h = (xh * g1_ref[...]).astype(BF16)
        main, uv, lr = dp_sc[:, 0:COL_U], dp_sc[:, COL_U:COL_LR], dp_sc[:, COL_LR:PROJ_WP]
        dw_ref[0:ROW_LR, :] += _tn(main, h)
        dw_ref[ROW_UV:PROJ_W, :] += _tn(uv, h)
        dw_ref[ROW_LR:ROW_UV, :] += _tn(lr, h)[0 : ROW_UV - ROW_LR]
        dh = _nn(main, w_ref[0:ROW_LR, :]) + _nn(uv, w_ref[ROW_UV:PROJ_W, :]) + _nn(lr, w_ref[ROW_LR : ROW_LR + LANES, :])
        dg1_ref[...] += jnp.sum(dh * xh, axis=0, keepdims=True)
        dx_ref[...] = dx1_ref[...] + _rms_bwd(dh * g1_ref[...], xh, r1)

    row = lambda w: pl.BlockSpec((tm, w), lambda i: (i, 0))
    return pl.pallas_call(
        body,
        name="inproj_bwd",
        grid=(seq // tm,),
        in_specs=[
            row(D_MODEL), row(D_MODEL), _resident((1, D_MODEL)), _resident((PROJ_W, D_MODEL)),
            row(KEY_W), row(KEY_W), row(KEY_W), row(KEY_W), row(GLA_W), row(GLA_W),
            row(GLA_W), row(GMLP_W), row(GMLP_W), row(LANES), row(LANES),
        ],
        out_specs=[row(D_MODEL), pl.BlockSpec((PROJ_W, D_MODEL), lambda i: (0, 0)), pl.BlockSpec((1, D_MODEL), lambda i: (0, 0))],
        out_shape=[
            jax.ShapeDtypeStruct((seq, D_MODEL), F32),
            jax.ShapeDtypeStruct((PROJ_W, D_MODEL), F32),
            jax.ShapeDtypeStruct((1, D_MODEL), F32),
        ],
        scratch_shapes=[pltpu.VMEM((tm, PROJ_WP), BF16)],
        compiler_params=_params(56),
    )(x, dx1, g1, w_in_t, dq_f, dq_b, dk_f, dk_b, dv_f, dv_b, dg, du, dvv, dlr_f, dlr_b)


def _row_tile(rows, multiple=8):
    for t in range(min(rows, 512), 0, -1):
        if rows % t == 0 and t % multiple == 0:
            return t
    return rows


def _cast_into_slot(w, shard):
    rows, cols = w.shape
    tr = _row_tile(rows, 16)

    def body(s_ref, w_ref, o_ref):
        o_ref[...] = w_ref[...].astype(BF16)

    return pl.pallas_call(
        body,
        name="cast_into_slot",
        grid_spec=pltpu.PrefetchScalarGridSpec(
            num_scalar_prefetch=1,
            grid=(rows // tr,),
            in_specs=[pl.BlockSpec((tr, cols), lambda i, s_ref: (i, 0))],
            out_specs=pl.BlockSpec((None, tr, cols), lambda i, s_ref: (s_ref[0], i, 0)),
        ),
        out_shape=jax.ShapeDtypeStruct((N_SHARDS, rows, cols), BF16),
        compiler_params=_params(32, ("parallel",)),
    )(shard, w)


def _add_halves(g4, recv, c):
    _, rows, _ = g4.shape
    tr = _row_tile(rows, 16)

    def body(c_ref, g_ref, r_ref, o_ref, ob_ref):
        total = g_ref[...] + r_ref[...]
        o_ref[...] = total
        ob_ref[...] = total.astype(BF16)

    out = pl.BlockSpec((None, tr, HALF), lambda s, i, c_ref: (s, i, 0))
    return pl.pallas_call(
        body,
        name="add_halves",
        grid_spec=pltpu.PrefetchScalarGridSpec(
            num_scalar_prefetch=1,
            grid=(N_SHARDS, rows // tr),
            in_specs=[pl.BlockSpec((None, tr, HALF), lambda s, i, c_ref: (s, i, c_ref[0])), out],
            out_specs=[out, out],
        ),
        out_shape=[jax.ShapeDtypeStruct((N_SHARDS, rows, HALF), F32), jax.ShapeDtypeStruct((N_SHARDS, rows, HALF), BF16)],
        compiler_params=_params(32, ("parallel", "parallel")),
    )(c, g4, recv)


def _add_partials(part4, recv3, shard_core):
    _, rows, _ = part4.shape
    tr = _row_tile(rows, 16)

    def body(sc_ref, p_ref, r_ref, o_ref):
        o_ref[...] = ((p_ref[...] + r_ref[0].astype(F32)) + r_ref[1].astype(F32)) + r_ref[2].astype(F32)

    return pl.pallas_call(
        body,
        name="add_partials",
        grid_spec=pltpu.PrefetchScalarGridSpec(
            num_scalar_prefetch=1,
            grid=(rows // tr,),
            in_specs=[
                pl.BlockSpec((None, tr, HALF), lambda i, sc_ref: (sc_ref[0], i, 0)),
                pl.BlockSpec((3, tr, HALF), lambda i, sc_ref: (0, i, 0)),
            ],
            out_specs=pl.BlockSpec((tr, HALF), lambda i, sc_ref: (i, sc_ref[1])),
        ),
        out_shape=jax.ShapeDtypeStruct((rows, 2 * HALF), F32),
        compiler_params=_params(32, ("parallel",)),
    )(shard_core, part4, recv3)


def _adam_math(w, g, m, v):
    m = ADAM_B1 * m + (1.0 - ADAM_B1) * g
    v = ADAM_B2 * v + (1.0 - ADAM_B2) * (g * g)
    m_hat = m / (1.0 - ADAM_B1**ADAM_STEP)
    v_hat = v / (1.0 - ADAM_B2**ADAM_STEP)
    delta = -ADAM_LR * (m_hat / (jnp.sqrt(v_hat) + ADAM_EPS) + ADAM_WD * w)
    return delta, m, v


def _adamw(w, g, m, v):
    rows, cols = w.shape
    tr = _row_tile(rows)

    def body(w_ref, g_ref, m_ref, v_ref, go_ref, d_ref, mo_ref, vo_ref):
        gv = g_ref[...]
        go_ref[...] = gv
        d_ref[...], mo_ref[...], vo_ref[...] = _adam_math(w_ref[...], gv, m_ref[...], v_ref[...])

    spec = pl.BlockSpec((tr, cols), lambda i: (i, 0))
    shape = jax.ShapeDtypeStruct(w.shape, F32)
    return pl.pallas_call(
        body, name="adamw", grid=(rows // tr,), in_specs=[spec] * 4, out_specs=[spec] * 4, out_shape=[shape] * 4,
        compiler_params=_params(32, ("parallel",)),
    )(w, g, m, v)


SMALL_ROWS = 560
DECAY_ROWS = 8
SMALL_TOTAL = SMALL_ROWS + 2 * N_SHARDS * DECAY_ROWS


def _adamw_small(gathered, wp, mp, vp):
    out_rows = SMALL_ROWS + 2 * DECAY_ROWS

    def body(ga_ref, w_ref, m_ref, v_ref, g_ref, d_ref, mo_ref, vo_ref):
        shard = 2 * lax.axis_index("x") + lax.axis_index("y")
        g_ref[pl.ds(0, SMALL_ROWS), :] = functools.reduce(lambda a, b: a + b, [ga_ref[d, pl.ds(0, SMALL_ROWS), :] for d in range(8)])
        for k in range(2):
            start = pl.multiple_of(SMALL_ROWS + k * N_SHARDS * DECAY_ROWS + shard * DECAY_ROWS, DECAY_ROWS)
            g_ref[pl.ds(SMALL_ROWS + k * DECAY_ROWS, DECAY_ROWS), :] = functools.reduce(
                lambda a, b: a + b, [ga_ref[d, pl.ds(start, DECAY_ROWS), :] for d in range(8)])
        d_ref[...], mo_ref[...], vo_ref[...] = _adam_math(w_ref[...], g_ref[...], m_ref[...], v_ref[...])

    shape = jax.ShapeDtypeStruct((out_rows, LANES), F32)
    return pl.pallas_call(body, name="adamw_small", out_shape=[shape] * 4, compiler_params=_params(32, None))(gathered, wp, mp, vp)


ANY = pl.BlockSpec(memory_space=pl.ANY)


def _position():
    return lax.axis_index("x"), lax.axis_index("y"), lax.axis_index("c")


def _other_chips(x, y):
    return [(1 - x, y), (x, 1 - y), (1 - x, 1 - y)]


HBM = pl.BlockSpec(memory_space=pltpu.HBM)
SEM = pl.BlockSpec(memory_space=pltpu.SEMAPHORE)
TOKEN = jax.ShapeDtypeStruct(TOKEN_SHAPE, F32)
DATAFLOW = pltpu.SideEffectType.DATAFLOW_SIDE_EFFECTING


def _half_block(ref4, slot, core):
    return ref4.at[slot, :, pl.ds(pl.multiple_of(core * HALF, HALF), HALF)]


def _gather_ici_copies(refs4, send_sems, recv_sems, stride):
    x, y, c = _position()
    pairs = []
    for k, ref4 in enumerate(refs4):
        mine = _half_block(ref4, 2 * x + y, c)
        for j, (px, py) in enumerate(_other_chips(x, y)):
            sems = dict(send_sem=send_sems.at[stride * k + j], recv_sem=recv_sems.at[stride * k + j], device_id=(px, py, c), device_id_type=MESH)
            pairs.append((functools.partial(pltpu.make_async_remote_copy, src_ref=mine, dst_ref=mine, **sems),
                          functools.partial(pltpu.make_async_remote_copy, src_ref=mine, dst_ref=_half_block(ref4, 2 * px + py, c), **sems)))
    return pairs


def _gather_d2d_copies(refs4, send_sems, recv_sems, stride, offset):
    x, y, c = _position()
    pairs = []
    for k, ref4 in enumerate(refs4):
        for j, (px, py) in enumerate(_other_chips(x, y)):
            have = _half_block(ref4, 2 * px + py, c)
            sems = dict(send_sem=send_sems.at[stride * k + offset + j], recv_sem=recv_sems.at[stride * k + offset + j],
                        device_id=(x, y, 1 - c), device_id_type=MESH)
            pairs.append((functools.partial(pltpu.make_async_remote_copy, src_ref=have, dst_ref=have, **sems),
                          functools.partial(pltpu.make_async_remote_copy, src_ref=have, dst_ref=_half_block(ref4, 2 * px + py, 1 - c), **sems)))
    return pairs


def _gather_sync(bufs):
    n = len(bufs)

    def body(*refs):
        outs = refs[n : 2 * n]
        send_sems, recv_sems = refs[2 * n :]
        ici = _gather_ici_copies(outs, send_sems, recv_sems, 6)
        d2d = _gather_d2d_copies(outs, send_sems, recv_sems, 6, 3)
        for send, _ in ici:
            send().start()
        for (_, arrival), (forward, _) in zip(ici, d2d):
            arrival().wait_recv()
            forward().start()
        for _, arrival in d2d:
            arrival().wait_recv()
        for send, _ in ici + d2d:
            send().wait_send()

    return pl.pallas_call(
        body,
        name="gather_sync",
        in_specs=[ANY] * n,
        out_specs=[ANY] * n,
        out_shape=[jax.ShapeDtypeStruct(b.shape, b.dtype) for b in bufs],
        input_output_aliases={k: k for k in range(n)},
        scratch_shapes=[pltpu.SemaphoreType.DMA((6 * n,)), pltpu.SemaphoreType.DMA((6 * n,))],
        compiler_params=pltpu.CompilerParams(has_side_effects=True),
    )(*bufs)


def _gather_start(bufs, after):
    n, na = len(bufs), len(after)

    def body(*refs):
        ins = refs[:n]
        send_sems, recv_sems = refs[n + na], refs[n + na + 1]
        token = refs[2 * n + na + 2]
        for send, _ in _gather_ici_copies(ins, send_sems, recv_sems, 3):
            send().start()
        token[...] = jnp.zeros_like(token)

    out = pl.pallas_call(
        body,
        name="gather_start",
        in_specs=[HBM] * n + [ANY] * na,
        out_specs=(SEM, SEM, *[HBM] * n, pl.BlockSpec(memory_space=pltpu.VMEM)),
        out_shape=(pltpu.SemaphoreType.DMA((3 * n,)), pltpu.SemaphoreType.DMA((3 * n,)), *[pltpu.HBM(b.shape, b.dtype) for b in bufs], TOKEN),
        input_output_aliases={k: 2 + k for k in range(n)},
        compiler_params=pltpu.CompilerParams(has_side_effects=DATAFLOW),
    )(*[pltpu.with_memory_space_constraint(b, pltpu.HBM) for b in bufs], *after)
    return out[0], out[1], list(out[2 : 2 + n]), out[2 + n]


def _gather_wait(send_sems, recv_sems, bufs, after):
    n = len(bufs)

    def body(*refs):
        ins = refs[:n]
        for send, arrival in _gather_ici_copies(ins, refs[n], refs[n + 1], 3):
            send().wait_send()
            arrival().wait_recv()

    return pl.pallas_call(
        body,
        name="gather_wait",
        in_specs=[HBM] * n + [SEM, SEM] + [ANY] * len(after),
        out_specs=tuple([HBM] * n),
        out_shape=tuple(pltpu.HBM(b.shape, b.dtype) for b in bufs),
        input_output_aliases={k: k for k in range(n)},
        compiler_params=pltpu.CompilerParams(has_side_effects=DATAFLOW),
    )(*bufs, send_sems, recv_sems, *after)


def _gather_forward(bufs):
    n = len(bufs)

    def body(*refs):
        outs = refs[n : 2 * n]
        send_sems, recv_sems = refs[2 * n :]
        d2d = _gather_d2d_copies(outs, send_sems, recv_sems, 3, 0)
        for forward, _ in d2d:
            forward().start()
        for forward, arrival in d2d:
            arrival().wait_recv()
            forward().wait_send()

    return pl.pallas_call(
        body,
        name="gather_forward",
        in_specs=[ANY] * n,
        out_specs=[ANY] * n,
        out_shape=[jax.ShapeDtypeStruct(b.shape, b.dtype) for b in bufs],
        input_output_aliases={k: k for k in range(n)},
        scratch_shapes=[pltpu.SemaphoreType.DMA((3 * n,)), pltpu.SemaphoreType.DMA((3 * n,))],
        compiler_params=pltpu.CompilerParams(has_side_effects=True),
    )(*bufs)


def _exchange_halves(grads4):
    n = len(grads4)

    def body(*refs):
        ins, outs = refs[:n], refs[n : 2 * n]
        send_sems, recv_sems = refs[2 * n :]
        x, y, c = _position()
        copies = []
        for k in range(n):
            cp = pltpu.make_async_remote_copy(
                src_ref=ins[k].at[:, :, pl.ds(pl.multiple_of((1 - c) * HALF, HALF), HALF)], dst_ref=outs[k],
                send_sem=send_sems.at[k], recv_sem=recv_sems.at[k], device_id=(x, y, 1 - c), device_id_type=MESH)
            cp.start()
            copies.append(cp)
        for cp in copies:
            cp.wait()

    return pl.pallas_call(
        body,
        name="exchange_halves",
        in_specs=[ANY] * n,
        out_specs=[ANY] * n,
        out_shape=[jax.ShapeDtypeStruct((N_SHARDS, g.shape[1], HALF), g.dtype) for g in grads4],
        scratch_shapes=[pltpu.SemaphoreType.DMA((n,)), pltpu.SemaphoreType.DMA((n,))],
        compiler_params=pltpu.CompilerParams(has_side_effects=True),
    )(*grads4)


def _scatter_partials(parts4):
    n = len(parts4)

    def body(*refs):
        ins, outs = refs[:n], refs[n : 2 * n]
        send_sems, recv_sems = refs[2 * n :]
        x, y, c = _position()
        copies = []
        for k in range(n):
            for j, (px, py) in enumerate(_other_chips(x, y)):
                cp = pltpu.make_async_remote_copy(
                    src_ref=ins[k].at[2 * px + py], dst_ref=outs[k].at[j],
                    send_sem=send_sems.at[3 * k + j], recv_sem=recv_sems.at[3 * k + j], device_id=(px, py, c), device_id_type=MESH)
                cp.start()
                copies.append(cp)
        for cp in copies:
            cp.wait()

    return pl.pallas_call(
        body,
        name="scatter_partials",
        in_specs=[ANY] * n,
        out_specs=[ANY] * n,
        out_shape=[jax.ShapeDtypeStruct((3,) + g.shape[1:], g.dtype) for g in parts4],
        scratch_shapes=[pltpu.SemaphoreType.DMA((3 * n,)), pltpu.SemaphoreType.DMA((3 * n,))],
        compiler_params=pltpu.CompilerParams(has_side_effects=True),
    )(*parts4)


def _scatter_copies(parts, lands, send_sems, recv_sems):
    x, y, c = _position()
    copies = []
    for k in range(len(parts)):
        for j, (px, py) in enumerate(_other_chips(x, y)):
            copies.append(pltpu.make_async_remote_copy(
                src_ref=parts[k].at[2 * px + py], dst_ref=lands[k].at[j],
                send_sem=send_sems.at[3 * k + j], recv_sem=recv_sems.at[3 * k + j], device_id=(px, py, c), device_id_type=MESH))
    return copies


def _exchange_copies(grads, lands, send_sems, recv_sems):
    x, y, c = _position()
    return [pltpu.make_async_remote_copy(
        src_ref=grads[k].at[:, :, pl.ds(pl.multiple_of((1 - c) * HALF, HALF), HALF)], dst_ref=lands[k],
        send_sem=send_sems.at[k], recv_sem=recv_sems.at[k], device_id=(x, y, 1 - c), device_id_type=MESH) for k in range(len(grads))]


def _exchange_lands(grads4):
    return [jax.ShapeDtypeStruct((N_SHARDS, g.shape[1], HALF), g.dtype) for g in grads4]


def _scatter_lands(parts4):
    return [jax.ShapeDtypeStruct((3,) + g.shape[1:], g.dtype) for g in parts4]


def _split_start(name, srcs, land_shapes, make_copies, nsem):
    n, nl = len(srcs), len(land_shapes)
    lands = [lax.empty(a.shape, a.dtype) for a in land_shapes]

    def body(*refs):
        send_sems, recv_sems = refs[n + nl], refs[n + nl + 1]
        token = refs[2 * (n + nl) + 2]
        for cp in make_copies(refs[:n], refs[n : n + nl], send_sems, recv_sems):
            cp.start()
        token[...] = jnp.zeros_like(token)

    hbm = lambda a: pltpu.HBM(a.shape, a.dtype)
    out = pl.pallas_call(
        body,
        name=name,
        in_specs=[HBM] * (n + nl),
        out_specs=(SEM, SEM, *[HBM] * (n + nl), pl.BlockSpec(memory_space=pltpu.VMEM)),
        out_shape=(pltpu.SemaphoreType.DMA((nsem,)), pltpu.SemaphoreType.DMA((nsem,)), *[hbm(a) for a in srcs + lands], TOKEN),
        input_output_aliases={k: 2 + k for k in range(n + nl)},
        compiler_params=pltpu.CompilerParams(has_side_effects=DATAFLOW),
    )(*[pltpu.with_memory_space_constraint(a, pltpu.HBM) for a in srcs + lands])
    return out[0], out[1], list(out[2 : 2 + n]), list(out[2 + n : 2 + n + nl]), out[2 + n + nl]


def _split_wait(name, send_sems, recv_sems, srcs, lands, make_copies, after):
    n, nl = len(srcs), len(lands)

    def body(*refs):
        for cp in make_copies(refs[:n], refs[n : n + nl], refs[n + nl], refs[n + nl + 1]):
            cp.wait_send()
            cp.wait_recv()

    hbm = lambda a: pltpu.HBM(a.shape, a.dtype)
    out = pl.pallas_call(
        body,
        name=name,
        in_specs=[HBM] * (n + nl) + [SEM, SEM] + [ANY] * len(after),
        out_specs=tuple([HBM] * (n + nl)),
        out_shape=tuple(hbm(a) for a in srcs + lands),
        input_output_aliases={k: k for k in range(n + nl)},
        compiler_params=pltpu.CompilerParams(has_side_effects=DATAFLOW),
    )(*srcs, *lands, send_sems, recv_sems, *after)
    return list(out[n:])


def _join_halves(bufs):
    n = len(bufs)

    def body(*refs):
        outs = refs[n : 2 * n]
        send_sems, recv_sems = refs[2 * n :]
        x, y, c = _position()
        half = lambda ref, core: ref.at[:, pl.ds(pl.multiple_of(core * HALF, HALF), HALF)]
        for k in range(n):
            mine = half(outs[k], c)
            pltpu.make_async_remote_copy(
                src_ref=mine, dst_ref=mine, send_sem=send_sems.at[k], recv_sem=recv_sems.at[k],
                device_id=(x, y, 1 - c), device_id_type=MESH).start()
        for k in range(n):
            wait = pltpu.make_async_remote_copy(
                src_ref=half(outs[k], c), dst_ref=half(outs[k], 1 - c), send_sem=send_sems.at[k], recv_sem=recv_sems.at[k],
                device_id=(x, y, 1 - c), device_id_type=MESH)
            wait.wait_send()
            wait.wait_recv()

    return pl.pallas_call(
        body,
        name="join_halves",
        in_specs=[ANY] * n,
        out_specs=[ANY] * n,
        out_shape=[jax.ShapeDtypeStruct(b.shape, b.dtype) for b in bufs],
        input_output_aliases={k: k for k in range(n)},
        scratch_shapes=[pltpu.SemaphoreType.DMA((n,)), pltpu.SemaphoreType.DMA((n,))],
        compiler_params=pltpu.CompilerParams(has_side_effects=True),
    )(*bufs)


def _allgather_small(block):
    m_per, ncol = block.shape

    def body(x_ref, out_ref, send_sems, recv_sems, local_sem):
        x, y, c = _position()
        me, sibling = (x, y, c), (x, y, 1 - c)
        chips = _other_chips(x, y)

        def rows(px, py, pc):
            return out_ref.at[4 * px + 2 * py + pc]

        def copy(k, blk, to, src=None):
            return pltpu.make_async_remote_copy(
                src_ref=rows(*blk) if src is None else src, dst_ref=rows(*blk),
                send_sem=send_sems.at[k], recv_sem=recv_sems.at[k], device_id=to, device_id_type=MESH)

        mine = pltpu.make_async_copy(x_ref, rows(*me), local_sem)
        mine.start()
        first = [copy(0, me, sibling, src=x_ref)] + [copy(1 + j, me, (*chip, c), src=x_ref) for j, chip in enumerate(chips)]
        for cp in first:
            cp.start()
        passed = [copy(4 + j, (*chip, c), sibling) for j, chip in enumerate(chips)]
        for j, chip in enumerate(chips):
            copy(1 + j, (*chip, c), me).wait_recv()
            passed[j].start()
        copy(0, sibling, me).wait_recv()
        for j, chip in enumerate(chips):
            copy(4 + j, (*chip, 1 - c), me).wait_recv()
        for cp in first + passed:
            cp.wait_send()
        mine.wait()

    return pl.pallas_call(
        body,
        name="allgather_small",
        in_specs=[pl.BlockSpec(memory_space=pltpu.VMEM)],
        out_specs=pl.BlockSpec(memory_space=pltpu.VMEM),
        out_shape=jax.ShapeDtypeStruct((8, m_per, ncol), block.dtype),
        scratch_shapes=[pltpu.SemaphoreType.DMA((7,)), pltpu.SemaphoreType.DMA((7,)), pltpu.SemaphoreType.DMA],
        compiler_params=pltpu.CompilerParams(has_side_effects=True, vmem_limit_bytes=32 * MIB),
    )(block)


SMALL_NAMES = ["norm1_g", "b_decay_f", "b_decay_b", "gla_norm_g", "gmlp_ln_g", "gmlp_ln_b", "w_spatial", "b_spatial", "norm2_g", "final_norm_g"]


def _pack_small(parts, decay_parts):
    flat = jnp.concatenate([a.reshape(-1) for a in parts])
    flat = jnp.pad(flat, (0, SMALL_ROWS * LANES - flat.shape[0])).reshape(SMALL_ROWS, LANES)
    return jnp.concatenate([flat] + [d.reshape(-1, LANES) for d in decay_parts], axis=0)


def _unpack_small(packed, like):
    out, off = [], 0
    flat = packed[:SMALL_ROWS].reshape(-1)
    for a in like:
        out.append(flat[off : off + a.size].reshape(a.shape))
        off += a.size
    return out


def kernel(x, norm1_g, w_in, w_decay_f, b_decay_f, w_decay_b, b_decay_b, gla_norm_g, gmlp_ln_g, gmlp_ln_b, w_spatial, b_spatial, w_out, norm2_g, w_gate, w_up, w_down, final_norm_g, loss_target, m_norm1_g, m_w_in, m_w_decay_f, m_b_decay_f, m_w_decay_b, m_b_decay_b, m_gla_norm_g, m_gmlp_ln_g, m_gmlp_ln_b, m_w_spatial, m_b_spatial, m_w_out, m_norm2_g, m_w_gate, m_w_up, m_w_down, m_final_norm_g, v_norm1_g, v_w_in, v_w_decay_f, v_b_decay_f, v_w_decay_b, v_b_decay_b, v_gla_norm_g, v_gmlp_ln_g, v_gmlp_ln_b, v_w_spatial, v_b_spatial, v_w_out, v_norm2_g, v_w_gate, v_w_up, v_w_down, v_final_norm_g):
    args = dict(locals())
    cx, cy, cc = lax.axis_index("x"), lax.axis_index("y"), lax.axis_index("c")
    shard = 2 * cx + cy
    xs = x[0]
    target = loss_target[0]

    big_names = ["w_in", "w_out", "w_gate", "w_up", "w_down"]
    transposed = ("w_in", "w_gate", "w_up")
    rows_of = lambda pre, k: jnp.transpose(args[pre + k][0]) if k in transposed else args[pre + k][0]
    big_shards = {k: rows_of("", k) for k in big_names}
    c_arr = cc.reshape(1).astype(jnp.int32)
    s_arr = shard.reshape(1).astype(jnp.int32)
    sc_arr = jnp.stack([shard, cc]).astype(jnp.int32)
    slots = {k: _cast_into_slot(big_shards[k], s_arr) for k in big_names}
    (w_in4,) = _gather_sync([slots["w_in"]])
    w_in_t = w_in4.reshape(PROJ_W, D_MODEL)

    dec_block = jnp.concatenate([w_decay_f[0].reshape(-1, LANES), w_decay_b[0].reshape(-1, LANES)], axis=0)
    dec_all = _allgather_small(dec_block)
    late = ["w_out", "w_gate", "w_up", "w_down"]
    g_send, g_recv, late_bufs, token_gather = _gather_start([slots[k] for k in late], (w_in4, dec_all))
    dec_all = dec_all[::2].reshape(N_SHARDS, 2, LOWRANK, KEY_W // N_SHARDS)
    wdf_full = jnp.transpose(dec_all[:, 0], (1, 0, 2)).reshape(LOWRANK, KEY_W)
    wdb_full = jnp.transpose(dec_all[:, 1], (1, 0, 2)).reshape(LOWRANK, KEY_W)
    wd_pad_f = jnp.zeros((LANES, KEY_W), F32).at[0:LOWRANK].set(wdf_full).astype(BF16)
    wd_pad_b = jnp.zeros((LANES, KEY_W), F32).at[LOWRANK : 2 * LOWRANK].set(wdb_full).astype(BF16)

    ws_bf = w_spatial[0].astype(BF16)
    wst_bf = jnp.transpose(w_spatial[0], (0, 2, 1)).astype(BF16)
    bs_col = b_spatial[0].reshape(GMLP_GROUPS, GMLP_CHUNK, 1)

    p = _inproj(xs, norm1_g, w_in_t, token_gather)
    o_f, st_f = _gla_fwd(p, wd_pad_f, b_decay_f, reverse=False)
    o_b, st_b = _gla_fwd(p, wd_pad_b, b_decay_b, reverse=True)
    late_bufs = _gather_forward(_gather_wait(g_send, g_recv, late_bufs, (o_f, o_b)))
    w_out_full, wg_t, wu_t, wd = [b.reshape(-1, D_MODEL) for b in late_bufs]
    x1, ycat = _mixer_out(xs, o_f, o_b, p, gla_norm_g, gmlp_ln_g, gmlp_ln_b, ws_bf, bs_col, w_out_full)
    gf = final_norm_g.reshape(1, D_MODEL)
    h2, gate, up, act, dx2, loss_acc, dgf = _ffn_fwd(x1, target, norm2_g, gf, wg_t, wu_t, wd)

    dgate, dup, dx1, dg2 = _ffn_bwd(dx2, gate, up, x1, norm2_g, wg_t, wu_t, wd)
    ffn_grads4 = [g.reshape(N_SHARDS, FF_SHARD, D_MODEL) for g in _ffn_wgrad(h2, dgate, dup, act, dx2)]
    e_send, e_recv, e_srcs, e_lands, token_exchange = _split_start(
        "exchange_start", ffn_grads4, _exchange_lands(ffn_grads4), _exchange_copies, len(ffn_grads4))
    do, dg, du, dvv, dwo, dgn, dlng, dlnb, dws, dbs = _mixer_bwd(
        dx1, ycat, o_f, o_b, p, gla_norm_g, gmlp_ln_g, gmlp_ln_b, ws_bf, wst_bf, bs_col, w_out_full, token_exchange)
    ffn_other = _split_wait("exchange_wait", e_send, e_recv, e_srcs, e_lands, _exchange_copies, (do,))
    ffn_parts = [_add_halves(g, r, c_arr) for g, r in zip(e_srcs, ffn_other)]
    ffn_payload = [pb for _, pb in ffn_parts]
    s_send, s_recv, s_parts, s_lands, token_scatter = _split_start(
        "scatter_start", ffn_payload, _scatter_lands(ffn_payload), _scatter_copies, 3 * len(ffn_payload))
    dq_f, dk_f, dv_f, dlr_f, dwdec_f, dbdec_f = _gla_bwd(p, do, st_f, wd_pad_f, b_decay_f, token_scatter, reverse=False)
    dq_b, dk_b, dv_b, dlr_b, dwdec_b, dbdec_b = _gla_bwd(p, do, st_b, wd_pad_b, b_decay_b, token_scatter, reverse=True)
    dx, dwin_t, dg1 = _inproj_bwd(xs, dx1, norm1_g, w_in_t, dq_f, dq_b, dk_f, dk_b, dv_f, dv_b, dg, du, dvv, dlr_f, dlr_b)
    ffn_recv = _split_wait("scatter_wait", s_send, s_recv, s_parts, s_lands, _scatter_copies, (dwin_t,))

    dwin4 = dwin_t.reshape(N_SHARDS, PROJ_W // N_SHARDS, D_MODEL)
    dwo4 = dwo.reshape(N_SHARDS, D_MODEL // N_SHARDS, D_MODEL)
    proj_grads4 = [dwin4, dwo4]
    proj_parts = [_add_halves(g, r, c_arr) for g, r in zip(proj_grads4, _exchange_halves(proj_grads4))]
    proj_recv = _scatter_partials([pb for _, pb in proj_parts])
    parts_f32 = [pf for pf, _ in proj_parts + ffn_parts]
    bufs = [_add_partials(pf, r, sc_arr) for pf, r in zip(parts_f32, list(proj_recv) + ffn_recv)]
    big_grads = dict(zip(big_names, _join_halves(bufs)))

    dwdec_f16 = dwdec_f[0:LOWRANK]
    dwdec_b16 = dwdec_b[LOWRANK : 2 * LOWRANK]
    shard_major = lambda a: jnp.transpose(a.reshape(LOWRANK, N_SHARDS, KEY_W // N_SHARDS), (1, 0, 2))
    small_grads = {
        "norm1_g": dg1, "b_decay_f": dbdec_f, "b_decay_b": dbdec_b, "gla_norm_g": dgn, "gmlp_ln_g": dlng, "gmlp_ln_b": dlnb,
        "w_spatial": dws, "b_spatial": dbs, "norm2_g": dg2, "final_norm_g": dgf,
    }
    g_pack = _pack_small([small_grads[k] for k in SMALL_NAMES] + [loss_acc], [shard_major(dwdec_f16), shard_major(dwdec_b16)])
    g_all = _allgather_small(g_pack)
    pack_own = lambda pre: _pack_small([args[pre + k] for k in SMALL_NAMES], [args[pre + "w_decay_f"], args[pre + "w_decay_b"]])
    sg, sd, sm, sv = _adamw_small(g_all, pack_own(""), pack_own("m_"), pack_own("v_"))

    names = ["norm1_g", "w_in", "w_decay_f", "b_decay_f", "w_decay_b", "b_decay_b", "gla_norm_g", "gmlp_ln_g", "gmlp_ln_b",
             "w_spatial", "b_spatial", "w_out", "norm2_g", "w_gate", "w_up", "w_down", "final_norm_g"]
    like = [args[k] for k in SMALL_NAMES]
    results = {"g": {}, "d": {}, "m": {}, "v": {}}
    for tag, packed in (("g", sg), ("d", sd), ("m", sm), ("v", sv)):
        for k, a in zip(SMALL_NAMES, _unpack_small(packed, like)):
            results[tag][k] = a
        results[tag]["w_decay_f"] = packed[SMALL_ROWS : SMALL_ROWS + DECAY_ROWS].reshape(w_decay_f.shape)
        results[tag]["w_decay_b"] = packed[SMALL_ROWS + DECAY_ROWS :].reshape(w_decay_b.shape)
    for k in big_names:
        g, d, mo, vo = _adamw(big_shards[k], big_grads[k], rows_of("m_", k), rows_of("v_", k))
        for tag, a in (("g", g), ("d", d), ("m", mo), ("v", vo)):
            results[tag][k] = (jnp.transpose(a) if k in transposed else a).reshape(args[k].shape)

    loss = sg[:SMALL_ROWS].reshape(-1)[sum(a.size for a in like)]
    grad_x = dx.reshape(x.shape)
    return (loss, grad_x, *[results["g"][k] for k in names], *[results["d"][k] for k in names],
            *[results["m"][k] for k in names], *[results["v"][k] for k in names])
```

```python
import functools
import math

import jax
import jax.numpy as jnp
from jax import lax
from jax.experimental import pallas as pl
from jax.experimental.pallas import tpu as pltpu

F32, BF16 = jnp.float32, jnp.bfloat16

D_MODEL = 1024
GLA_HEADS = 4
GLA_DK = 64
GLA_DV = 128
KEY_W = GLA_HEADS * GLA_DK
GLA_W = GLA_HEADS * GLA_DV
GMLP_W = 512
GMLP_GROUPS = 4
GMLP_CHUNK = 128
LOWRANK = 16
GLA_CHUNK = 64
GLA_TAU = 16.0
PROJ_W = 2592
PROJ_WP = 2688
D_FF = 2816
N_SHARDS = 4
FF_SHARD = D_FF // N_SHARDS
EPS = 1e-6
LANES = 128
TOKEN_SHAPE = (8, LANES)
MIB = 1024 * 1024

ADAM_LR = 0.001
ADAM_B1 = 0.9
ADAM_B2 = 0.999
ADAM_EPS = 1e-08
ADAM_WD = 0.01
ADAM_STEP = 10

COL_Q, COL_K = 0, 256
COL_V, COL_G, COL_U, COL_VV = 512, 1024, 1536, 2048
COL_LR = 2560
ROW_LR, ROW_UV = 1536, 1568
HALF = D_MODEL // 2

MESH = pl.DeviceIdType.MESH


def _nn(a, b):
    return jnp.dot(a, b, preferred_element_type=F32)


def _nt(a, b):
    return lax.dot_general(a, b, (((1,), (1,)), ((), ())), preferred_element_type=F32)


def _tn(a, b):
    return lax.dot_general(a, b, (((0,), (0,)), ((), ())), preferred_element_type=F32)


def _bnn(a, b):
    return jnp.einsum("nik,nkj->nij", a, b, preferred_element_type=F32)


def _bnt(a, b):
    return jnp.einsum("nik,njk->nij", a, b, preferred_element_type=F32)


def _btn(a, b):
    return jnp.einsum("nki,nkj->nij", a, b, preferred_element_type=F32)


def _resident(shape):
    zeros = (0,) * len(shape)
    return pl.BlockSpec(shape, lambda *_: zeros, pipeline_mode=pl.Buffered(1))


def _params(vmem_mib, semantics=("arbitrary",)):
    return pltpu.CompilerParams(vmem_limit_bytes=vmem_mib * MIB, dimension_semantics=semantics)


def _sigmoid(x):
    return 1.0 / (1.0 + jnp.exp(-x))


def _gelu(x):
    return 0.5 * x * (1.0 + lax.erf(x * (1.0 / math.sqrt(2.0))))


def _gelu_grad(x):
    return 0.5 * (1.0 + lax.erf(x * (1.0 / math.sqrt(2.0)))) + x * jnp.exp(-0.5 * x * x) * (1.0 / math.sqrt(2.0 * math.pi))


def _log_sigmoid(x):
    return jnp.minimum(x, 0.0) - jnp.log(1.0 + jnp.exp(-jnp.abs(x)))


def _rms_bwd(dxh, xh, r):
    return r * (dxh - xh * jnp.mean(dxh * xh, axis=-1, keepdims=True))


def _chunk_cumsum(v, row_in_chunk, reverse):
    rows = v.shape[0]
    for sh in (1, 2, 4, 8, 16, 32):
        if reverse:
            v = v + jnp.where(row_in_chunk + sh < GLA_CHUNK, pltpu.roll(v, rows - sh, axis=0), 0.0)
        else:
            v = v + jnp.where(row_in_chunk >= sh, pltpu.roll(v, sh, axis=0), 0.0)
    return v


def _inproj(x, g1, w_in_t, token):
    seq = x.shape[0]
    tm = min(seq, 512)

    def body(x_ref, g_ref, w_ref, token_ref, p_ref):
        xv = x_ref[...]
        r = lax.rsqrt(jnp.mean(xv * xv, axis=-1, keepdims=True) + EPS)
        h = (xv * r * g_ref[...]).astype(BF16)
        p_ref[:, 0:COL_U] = _nt(h, w_ref[0:ROW_LR, :])
        p_ref[:, COL_U:COL_LR] = _nt(h, w_ref[ROW_UV:PROJ_W, :])
        p_ref[:, COL_LR:PROJ_WP] = _nt(h, w_ref[ROW_LR : ROW_LR + LANES, :])

    return pl.pallas_call(
        body,
        name="inproj",
        grid=(seq // tm,),
        in_specs=[pl.BlockSpec((tm, D_MODEL), lambda i: (i, 0)), _resident((1, D_MODEL)), _resident((PROJ_W, D_MODEL)), _resident(TOKEN_SHAPE)],
        out_specs=pl.BlockSpec((tm, PROJ_WP), lambda i: (i, 0)),
        out_shape=jax.ShapeDtypeStruct((seq, PROJ_WP), F32),
        compiler_params=_params(48, ("parallel",)),
    )(x, g1, w_in_t, token)


def _gla_tile(seq):
    return min(seq, 512)


def _gla_decay_terms(lr_bf, wd_ref, bd_ref, pair, row_in_chunk, reverse, n):
    cols = pl.ds(pair * LANES, LANES)
    pre = _nn(lr_bf, wd_ref[:, cols]) + bd_ref[:, cols]
    la = _log_sigmoid(pre) * (1.0 / GLA_TAU)
    b = _chunk_cumsum(la, row_in_chunk, reverse)
    b3 = b.reshape(n, GLA_CHUNK, LANES)
    blast = b3[:, 0:1, :] if reverse else b3[:, GLA_CHUNK - 1 : GLA_CHUNK, :]
    return pre, b3, blast


def _gla_fwd(p, wd_pad, bd, reverse):
    seq = p.shape[0]
    tg = _gla_tile(seq)
    nt = seq // tg
    n = tg // GLA_CHUNK
    scale = GLA_DK**-0.5

    def tile(i):
        return nt - 1 - i if reverse else i

    def body(q_ref, k_ref, v_ref, lr_ref, wd_ref, bd_ref, o_ref, st_ref, carry):
        @pl.when(pl.program_id(0) == 0)
        def _():
            carry[...] = jnp.zeros_like(carry)

        lr_bf = lr_ref[...].astype(BF16)
        states = [carry[h] for h in range(GLA_HEADS)]
        row_in_chunk = lax.broadcasted_iota(jnp.int32, (tg, LANES), 0) % GLA_CHUNK
        lane_head = lax.broadcasted_iota(jnp.int32, (1, LANES), 1) // GLA_DK
        tt = lax.broadcasted_iota(jnp.int32, (GLA_CHUNK, GLA_CHUNK), 0)
        ss = lax.broadcasted_iota(jnp.int32, (GLA_CHUNK, GLA_CHUNK), 1)
        causal = (tt <= ss) if reverse else (tt >= ss)
        order = range(n - 1, -1, -1) if reverse else range(n)
        heads = range(GLA_HEADS)
        qdh, kds, vhs, decs, sc_raw, dst = {}, {}, {}, {}, {}, {}
        for pair in range(2):
            cols = pl.ds(pair * LANES, LANES)
            _, b3, blast = _gla_decay_terms(lr_bf, wd_ref, bd_ref, pair, row_in_chunk, reverse, n)
            q3 = q_ref[:, cols].reshape(n, GLA_CHUNK, LANES) * scale
            k3 = k_ref[:, cols].reshape(n, GLA_CHUNK, LANES)
            qd = q3 * jnp.exp(b3)
            kd = (k3 * jnp.exp(-b3)).astype(BF16)
            kte = k3 * jnp.exp(blast - b3)
            dec = jnp.exp(blast)
            for hh in range(2):
                h = 2 * pair + hh
                m = (lane_head == hh).astype(F32)
                qdh[h], kds[h], decs[h] = (qd * m).astype(BF16), kd, dec
                vhs[h] = v_ref[:, pl.ds(h * GLA_DV, GLA_DV)].reshape(n, GLA_CHUNK, GLA_DV).astype(BF16)
                sc_raw[h] = _bnt(qdh[h], kd)
                dst[h] = _btn(vhs[h], (kte * m).astype(BF16))
        o_intra, befores = {}, {}
        for h in heads:
            o_intra[h] = _bnn(jnp.where(causal, sc_raw[h], 0.0).astype(BF16), vhs[h])
            st, before = states[h], [None] * n
            for j in order:
                before[j] = st
                st = st * decs[h][j] + dst[h][j]
            states[h] = st
            befores[h] = jnp.stack(before)
        outs = {h: (o_intra[h] + _bnt(qdh[h], befores[h].astype(BF16))).reshape(tg, GLA_DV) for h in heads}
        for h in range(GLA_HEADS):
            o_ref[:, pl.ds(h * GLA_DV, GLA_DV)] = outs[h]
            st_ref[:, h] = befores[h]
            carry[h] = states[h]

    nchunks = seq // GLA_CHUNK
    return pl.pallas_call(
        body,
        name="gla_fwd_rev" if reverse else "gla_fwd",
        grid=(nt,),
        in_specs=[
            pl.BlockSpec((tg, KEY_W), lambda i: (tile(i), COL_Q // KEY_W)),
            pl.BlockSpec((tg, KEY_W), lambda i: (tile(i), COL_K // KEY_W)),
            pl.BlockSpec((tg, GLA_W), lambda i: (tile(i), COL_V // GLA_W)),
            pl.BlockSpec((tg, LANES), lambda i: (tile(i), COL_LR // LANES)),
            _resident((LANES, KEY_W)),
            _resident((1, KEY_W)),
        ],
        out_specs=[
            pl.BlockSpec((tg, GLA_W), lambda i: (tile(i), 0)),
            pl.BlockSpec((n, GLA_HEADS, GLA_DV, LANES), lambda i: (tile(i), 0, 0, 0)),
        ],
        out_shape=[
            jax.ShapeDtypeStruct((seq, GLA_W), F32),
            jax.ShapeDtypeStruct((nchunks, GLA_HEADS, GLA_DV, LANES), F32),
        ],
        scratch_shapes=[pltpu.VMEM((GLA_HEADS, GLA_DV, LANES), F32)],
        compiler_params=_params(48),
    )(p, p, p, p, wd_pad, bd)


def _mixer_out(x, o_f, o_b, p, gn, lng, lnb, ws_bf, bs_col, w_out):
    seq = x.shape[0]
    tm = min(seq, 512)

    def body(x_ref, of_ref, ob_ref, g_ref, u_ref, vv_ref, gn_ref, lng_ref, lnb_ref, ws_ref, bs_ref, wo_ref, x1_ref, yc_ref, vn_sc):
        for h in range(GLA_HEADS):
            cols = pl.ds(h * GLA_DV, GLA_DV)
            oh = of_ref[:, cols] + ob_ref[:, cols]
            on = oh * lax.rsqrt(jnp.mean(oh * oh, axis=-1, keepdims=True) + EPS)
            gh = g_ref[:, cols]
            yc_ref[:, cols] = (on * gn_ref[:, cols] * (gh * _sigmoid(gh))).astype(BF16)
        zv = _gelu(vv_ref[...])
        xc = zv - jnp.mean(zv, axis=-1, keepdims=True)
        vhat = xc * lax.rsqrt(jnp.mean(xc * xc, axis=-1, keepdims=True) + EPS)
        vn_sc[...] = (vhat * lng_ref[...] + lnb_ref[...]).astype(BF16)
        for c in range(tm // GMLP_CHUNK):
            rows = pl.ds(c * GMLP_CHUNK, GMLP_CHUNK)
            for g in range(GMLP_GROUPS):
                cols = pl.ds(g * LANES, LANES)
                s = _nn(ws_ref[g], vn_sc[rows, cols]) + bs_ref[g]
                yc_ref[rows, pl.ds(GLA_W + g * LANES, LANES)] = (_gelu(u_ref[rows, cols]) * s).astype(BF16)
        x1_ref[...] = x_ref[...] + _nn(yc_ref[...], wo_ref[...])

    row = lambda w: pl.BlockSpec((tm, w), lambda i: (i, 0))
    pcol = lambda col: pl.BlockSpec((tm, GLA_W), lambda i: (i, col // GLA_W))
    return pl.pallas_call(
        body,
        name="mixer_out",
        grid=(seq // tm,),
        in_specs=[
            row(D_MODEL), row(GLA_W), row(GLA_W), pcol(COL_G), pcol(COL_U), pcol(COL_VV),
            _resident((1, GLA_W)), _resident((1, GMLP_W)), _resident((1, GMLP_W)),
            _resident((GMLP_GROUPS, GMLP_CHUNK, GMLP_CHUNK)), _resident((GMLP_GROUPS, GMLP_CHUNK, 1)),
            _resident((D_MODEL, D_MODEL)),
        ],
        out_specs=[row(D_MODEL), row(D_MODEL)],
        out_shape=[jax.ShapeDtypeStruct((seq, D_MODEL), F32), jax.ShapeDtypeStruct((seq, D_MODEL), BF16)],
        scratch_shapes=[pltpu.VMEM((tm, GMLP_W), BF16)],
        compiler_params=_params(48, ("parallel",)),
    )(x, o_f, o_b, p, p, p, gn, lng, lnb, ws_bf, bs_col, w_out)


def _ffn_fwd(x1, target, g2, gf, wg_t, wu_t, wd):
    seq = x1.shape[0]
    tm = min(seq, 256)

    def body(x1_ref, t_ref, g2_ref, gf_ref, wg_ref, wu_ref, wd_ref, h2_ref, gate_ref, up_ref, act_ref, dx2_ref, loss_ref, dgf_ref):
        @pl.when(pl.program_id(0) == 0)
        def _():
            loss_ref[...] = jnp.zeros_like(loss_ref)
            dgf_ref[...] = jnp.zeros_like(dgf_ref)

        x1v = x1_ref[...]
        h2 = (x1v * lax.rsqrt(jnp.mean(x1v * x1v, axis=-1, keepdims=True) + EPS) * g2_ref[...]).astype(BF16)
        h2_ref[...] = h2
        gate = _nt(h2, wg_ref[...])
        up = _nt(h2, wu_ref[...])
        act = (gate * _sigmoid(gate) * up).astype(BF16)
        gate_ref[...] = gate
        up_ref[...] = up
        act_ref[...] = act
        x2 = x1v + _nn(act, wd_ref[...])
        rf = lax.rsqrt(jnp.mean(x2 * x2, axis=-1, keepdims=True) + EPS)
        xh = x2 * rf
        err = xh * gf_ref[...] - t_ref[...]
        loss_ref[...] += 0.5 * jnp.sum(jnp.mean(err * err, axis=-1, keepdims=True))
        dy = err * (1.0 / D_MODEL)
        dgf_ref[...] += jnp.sum(dy * xh, axis=0, keepdims=True)
        dx2_ref[...] = _rms_bwd(dy * gf_ref[...], xh, rf)

    row = lambda w: pl.BlockSpec((tm, w), lambda i: (i, 0))
    weight = _resident((D_FF, D_MODEL))
    return pl.pallas_call(
        body,
        name="ffn_fwd",
        grid=(seq // tm,),
        in_specs=[row(D_MODEL), row(D_MODEL), _resident((1, D_MODEL)), _resident((1, D_MODEL)), weight, weight, weight],
        out_specs=[row(D_MODEL), row(D_FF), row(D_FF), row(D_FF), row(D_MODEL),
                   pl.BlockSpec((1, LANES), lambda i: (0, 0)), pl.BlockSpec((1, D_MODEL), lambda i: (0, 0))],
        out_shape=[
            jax.ShapeDtypeStruct((seq, D_MODEL), BF16),
            jax.ShapeDtypeStruct((seq, D_FF), F32),
            jax.ShapeDtypeStruct((seq, D_FF), F32),
            jax.ShapeDtypeStruct((seq, D_FF), BF16),
            jax.ShapeDtypeStruct((seq, D_MODEL), F32),
            jax.ShapeDtypeStruct((1, LANES), F32),
            jax.ShapeDtypeStruct((1, D_MODEL), F32),
        ],
        compiler_params=_params(56),
    )(x1, target, g2, gf, wg_t, wu_t, wd)


def _ffn_bwd(dx2, gate, up, x1, g2, wg_t, wu_t, wd):
    seq = x1.shape[0]
    tm = min(seq, 256)

    def body(dx2_ref, gate_ref, up_ref, x1_ref, g2_ref, wg_ref, wu_ref, wd_ref, dgate_ref, dup_ref, dx1_ref, dg2_ref):
        @pl.when(pl.program_id(0) == 0)
        def _():
            dg2_ref[...] = jnp.zeros_like(dg2_ref)

        dx2v = dx2_ref[...]
        dact = _nt(dx2v.astype(BF16), wd_ref[...])
        gate = gate_ref[...]
        sg = _sigmoid(gate)
        dgate = (dact * up_ref[...] * (sg * (1.0 + gate * (1.0 - sg)))).astype(BF16)
        dup = (dact * (gate * sg)).astype(BF16)
        dgate_ref[...] = dgate
        dup_ref[...] = dup
        dh2 = _nn(dgate, wg_ref[...]) + _nn(dup, wu_ref[...])
        x1v = x1_ref[...]
        r2 = lax.rsqrt(jnp.mean(x1v * x1v, axis=-1, keepdims=True) + EPS)
        xh = x1v * r2
        dg2_ref[...] += jnp.sum(dh2 * xh, axis=0, keepdims=True)
        dx1_ref[...] = dx2v + _rms_bwd(dh2 * g2_ref[...], xh, r2)

    row = lambda w: pl.BlockSpec((tm, w), lambda i: (i, 0))
    weight = _resident((D_FF, D_MODEL))
    return pl.pallas_call(
        body,
        name="ffn_bwd",
        grid=(seq // tm,),
        in_specs=[row(D_MODEL), row(D_FF), row(D_FF), row(D_MODEL), _resident((1, D_MODEL)), weight, weight, weight],
        out_specs=[row(D_FF), row(D_FF), row(D_MODEL), pl.BlockSpec((1, D_MODEL), lambda i: (0, 0))],
        out_shape=[
            jax.ShapeDtypeStruct((seq, D_FF), BF16),
            jax.ShapeDtypeStruct((seq, D_FF), BF16),
            jax.ShapeDtypeStruct((seq, D_MODEL), F32),
            jax.ShapeDtypeStruct((1, D_MODEL), F32),
        ],
        compiler_params=_params(56),
    )(dx2, gate, up, x1, g2, wg_t, wu_t, wd)


WGRAD_ROWS = D_FF // 2


def _ffn_wgrad(h2, dgate, dup, act, dx2):
    seq = h2.shape[0]
    tm = min(seq, 512)

    def body(h2_ref, dgate_ref, dup_ref, act_ref, dx2_ref, dwg_ref, dwu_ref, dwd_ref):
        @pl.when(pl.program_id(1) == 0)
        def _():
            dwg_ref[...] = jnp.zeros_like(dwg_ref)
            dwu_ref[...] = jnp.zeros_like(dwu_ref)
            dwd_ref[...] = jnp.zeros_like(dwd_ref)

        h2v = h2_ref[...]
        dwg_ref[...] += _tn(dgate_ref[...], h2v)
        dwu_ref[...] += _tn(dup_ref[...], h2v)
        dwd_ref[...] += _tn(act_ref[...], dx2_ref[...].astype(BF16))

    ff = pl.BlockSpec((tm, WGRAD_ROWS), lambda j, i: (i, j))
    row = pl.BlockSpec((tm, D_MODEL), lambda j, i: (i, 0))
    out = pl.BlockSpec((WGRAD_ROWS, D_MODEL), lambda j, i: (j, 0))
    return pl.pallas_call(
        body,
        name="ffn_wgrad",
        grid=(D_FF // WGRAD_ROWS, seq // tm),
        in_specs=[row, ff, ff, ff, row],
        out_specs=[out, out, out],
        out_shape=[jax.ShapeDtypeStruct((D_FF, D_MODEL), F32)] * 3,
        compiler_params=_params(56, ("parallel", "arbitrary")),
    )(h2, dgate, dup, act, dx2)


def _mixer_bwd(dx1, ycat, o_f, o_b, p, gn, lng, lnb, ws_bf, wst_bf, bs_col, w_out, token):
    seq = dx1.shape[0]
    tm = min(seq, 512)
    nsteps = seq // tm

    def body(dx1_ref, yc_ref, of_ref, ob_ref, g_ref, u_ref, vv_ref, gn_ref, lng_ref, lnb_ref, ws_ref, wst_ref, bs_ref, wo_ref, token_ref,
             do_ref, dg_ref, du_ref, dvv_ref, dwo_ref, dgn_ref, dlng_ref, dlnb_ref, dws_ref, dbs_ref, vn_sc, dvn_sc, dbs_acc):
        step = pl.program_id(0)

        @pl.when(step == 0)
        def _():
            for r in (dwo_ref, dgn_ref, dlng_ref, dlnb_ref, dws_ref, dbs_acc):
                r[...] = jnp.zeros_like(r)

        dx1b = dx1_ref[...].astype(BF16)
        dyc = _nt(dx1b, wo_ref[...])
        dwo_ref[...] += _tn(yc_ref[...], dx1b)
        for h in range(GLA_HEADS):
            cols = pl.ds(h * GLA_DV, GLA_DV)
            dya = dyc[:, h * GLA_DV : (h + 1) * GLA_DV]
            oh = of_ref[:, cols] + ob_ref[:, cols]
            rn = lax.rsqrt(jnp.mean(oh * oh, axis=-1, keepdims=True) + EPS)
            on = oh * rn
            gh = g_ref[:, cols]
            sg = _sigmoid(gh)
            sil = gh * sg
            gnh = gn_ref[:, cols]
            dgn_ref[:, cols] += jnp.sum(dya * on * sil, axis=0, keepdims=True)
            dg_ref[:, cols] = dya * on * gnh * (sg * (1.0 + gh * (1.0 - sg)))
            do_ref[:, cols] = _rms_bwd(dya * gnh * sil, on, rn)
        vv = vv_ref[...]
        zv = _gelu(vv)
        xc = zv - jnp.mean(zv, axis=-1, keepdims=True)
        rstd = lax.rsqrt(jnp.mean(xc * xc, axis=-1, keepdims=True) + EPS)
        vhat = xc * rstd
        vn_sc[...] = (vhat * lng_ref[...] + lnb_ref[...]).astype(BF16)
        for c in range(tm // GMLP_CHUNK):
            rows = pl.ds(c * GMLP_CHUNK, GMLP_CHUNK)
            for g in range(GMLP_GROUPS):
                cols = pl.ds(g * LANES, LANES)
                vn = vn_sc[rows, cols]
                s = _nn(ws_ref[g], vn) + bs_ref[g]
                dyb = dyc[c * GMLP_CHUNK : (c + 1) * GMLP_CHUNK, GLA_W + g * LANES : GLA_W + (g + 1) * LANES]
                u = u_ref[rows, cols]
                du_ref[rows, cols] = dyb * s * _gelu_grad(u)
                ds = dyb * _gelu(u)
                dbs_acc[g] += ds
                dsb = ds.astype(BF16)
                dws_ref[g] += _nt(dsb, vn)
                dvn_sc[rows, cols] = _nn(wst_ref[g], dsb)
        dvn = dvn_sc[...]
        dlng_ref[...] += jnp.sum(dvn * vhat, axis=0, keepdims=True)
        dlnb_ref[...] += jnp.sum(dvn, axis=0, keepdims=True)
        dvh = dvn * lng_ref[...]
        dzv = rstd * (dvh - jnp.mean(dvh, axis=-1, keepdims=True) - vhat * jnp.mean(dvh * vhat, axis=-1, keepdims=True))
        dvv_ref[...] = dzv * _gelu_grad(vv)

        @pl.when(step == nsteps - 1)
        def _():
            dbs_ref[...] = jnp.sum(dbs_acc[...], axis=-1, keepdims=True)

    row = lambda w: pl.BlockSpec((tm, w), lambda i: (i, 0))
    pcol = lambda col: pl.BlockSpec((tm, GLA_W), lambda i: (i, col // GLA_W))
    const = lambda shape: pl.BlockSpec(shape, lambda i: (0,) * len(shape))
    return pl.pallas_call(
        body,
        name="mixer_bwd",
        grid=(nsteps,),
        in_specs=[
            row(D_MODEL), row(D_MODEL), row(GLA_W), row(GLA_W), pcol(COL_G), pcol(COL_U), pcol(COL_VV),
            _resident((1, GLA_W)), _resident((1, GMLP_W)), _resident((1, GMLP_W)),
            _resident((GMLP_GROUPS, GMLP_CHUNK, GMLP_CHUNK)), _resident((GMLP_GROUPS, GMLP_CHUNK, GMLP_CHUNK)),
            _resident((GMLP_GROUPS, GMLP_CHUNK, 1)), _resident((D_MODEL, D_MODEL)), _resident(TOKEN_SHAPE),
        ],
        out_specs=[
            row(GLA_W), row(GLA_W), row(GMLP_W), row(GMLP_W), const((D_MODEL, D_MODEL)),
            const((1, GLA_W)), const((1, GMLP_W)), const((1, GMLP_W)),
            const((GMLP_GROUPS, GMLP_CHUNK, GMLP_CHUNK)), const((GMLP_GROUPS, GMLP_CHUNK, 1)),
        ],
        out_shape=[
            jax.ShapeDtypeStruct((seq, GLA_W), F32), jax.ShapeDtypeStruct((seq, GLA_W), F32),
            jax.ShapeDtypeStruct((seq, GMLP_W), F32), jax.ShapeDtypeStruct((seq, GMLP_W), F32),
            jax.ShapeDtypeStruct((D_MODEL, D_MODEL), F32),
            jax.ShapeDtypeStruct((1, GLA_W), F32), jax.ShapeDtypeStruct((1, GMLP_W), F32), jax.ShapeDtypeStruct((1, GMLP_W), F32),
            jax.ShapeDtypeStruct((GMLP_GROUPS, GMLP_CHUNK, GMLP_CHUNK), F32), jax.ShapeDtypeStruct((GMLP_GROUPS, GMLP_CHUNK, 1), F32),
        ],
        scratch_shapes=[pltpu.VMEM((tm, GMLP_W), BF16), pltpu.VMEM((tm, GMLP_W), F32), pltpu.VMEM((GMLP_GROUPS, GMLP_CHUNK, GMLP_CHUNK), F32)],
        compiler_params=_params(56),
    )(dx1, ycat, o_f, o_b, p, p, p, gn, lng, lnb, ws_bf, wst_bf, bs_col, w_out, token)


def _gla_bwd(p, do, st, wd_pad, bd, token, reverse):
    seq = p.shape[0]
    tg = _gla_tile(seq)
    nt = seq // tg
    n = tg // GLA_CHUNK
    scale = GLA_DK**-0.5

    def tile(i):
        return i if reverse else nt - 1 - i

    def body(q_ref, k_ref, v_ref, lr_ref, do_ref, st_ref, wd_ref, bd_ref, token_ref, dq_ref, dk_ref, dv_ref, dlr_ref, dwd_ref, dbd_ref, carry):
        @pl.when(pl.program_id(0) == 0)
        def _():
            carry[...] = jnp.zeros_like(carry)
            dwd_ref[...] = jnp.zeros_like(dwd_ref)
            dbd_ref[...] = jnp.zeros_like(dbd_ref)

        lr_bf = lr_ref[...].astype(BF16)
        carries = [carry[h] for h in range(GLA_HEADS)]
        row_in_chunk = lax.broadcasted_iota(jnp.int32, (tg, LANES), 0) % GLA_CHUNK
        lane_head = lax.broadcasted_iota(jnp.int32, (1, LANES), 1) // GLA_DK
        tt = lax.broadcasted_iota(jnp.int32, (GLA_CHUNK, GLA_CHUNK), 0)
        ss = lax.broadcasted_iota(jnp.int32, (GLA_CHUNK, GLA_CHUNK), 1)
        causal = (tt <= ss) if reverse else (tt >= ss)
        causal_t = (tt >= ss) if reverse else (tt <= ss)
        order = range(n) if reverse else range(n - 1, -1, -1)
        dlr = jnp.zeros((tg, LANES), F32)
        heads = range(GLA_HEADS)
        pv, masks, qdh, kteh, vhs, dohs, stbs = {}, {}, {}, {}, {}, {}, {}
        sc_t, dp, dp_t, acc = {}, {}, {}, {}
        for pair in range(2):
            cols = pl.ds(pair * LANES, LANES)
            pre, b3, blast = _gla_decay_terms(lr_bf, wd_ref, bd_ref, pair, row_in_chunk, reverse, n)
            q3 = q_ref[:, cols].reshape(n, GLA_CHUNK, LANES) * scale
            k3 = k_ref[:, cols].reshape(n, GLA_CHUNK, LANES)
            eb = jnp.exp(b3)
            emb = jnp.exp(-b3)
            ekte = jnp.exp(blast - b3)
            kdf = k3 * emb
            pv[pair] = dict(pre=pre, eb=eb, emb=emb, ekte=ekte, qd=q3 * eb, kdf=kdf, kd=kdf.astype(BF16), kte=k3 * ekte, dec=jnp.exp(blast))
            for hh in range(2):
                h = 2 * pair + hh
                vcols = pl.ds(h * GLA_DV, GLA_DV)
                masks[h] = (lane_head == hh).astype(F32)
                qdh[h] = (pv[pair]["qd"] * masks[h]).astype(BF16)
                kteh[h] = (pv[pair]["kte"] * masks[h]).astype(BF16)
                vhs[h] = v_ref[:, vcols].reshape(n, GLA_CHUNK, GLA_DV).astype(BF16)
                dohs[h] = do_ref[:, vcols].reshape(n, GLA_CHUNK, GLA_DV).astype(BF16)
                stbs[h] = st_ref[:, h]
                sc_t[h] = _bnt(pv[pair]["kd"], qdh[h])
                dp[h] = _bnt(dohs[h], vhs[h])
                dp_t[h] = _bnt(vhs[h], dohs[h])
                acc[h] = _btn(dohs[h], qdh[h])
        dsa = {}
        for h in heads:
            sc_t[h] = jnp.where(causal_t, sc_t[h], 0.0).astype(BF16)
            dp[h] = jnp.where(causal, dp[h], 0.0).astype(BF16)
            dp_t[h] = jnp.where(causal_t, dp_t[h], 0.0).astype(BF16)
            dec = pv[h // 2]["dec"]
            c, after = carries[h], [None] * n
            for j in order:
                after[j] = c
                c = acc[h][j] + dec[j] * c
            carries[h] = c
            dsa[h] = jnp.stack(after)
        dvs, dqs, dks, dwds, dbds = [], [], [], [], []
        for pair in range(2):
            cols = pl.ds(pair * LANES, LANES)
            v = pv[pair]
            dqd = jnp.zeros((n, GLA_CHUNK, LANES), F32)
            dkd = jnp.zeros((n, GLA_CHUNK, LANES), F32)
            dkte = jnp.zeros((n, GLA_CHUNK, LANES), F32)
            ddec = jnp.zeros((n, 1, LANES), F32)
            for h in (2 * pair, 2 * pair + 1):
                dsa_bf = dsa[h].astype(BF16)
                dqd = dqd + (_bnn(dp[h], v["kd"]) * masks[h] + _bnn(dohs[h], stbs[h].astype(BF16)))
                dkd = dkd + _bnn(dp_t[h], qdh[h])
                dkte = dkte + _bnn(vhs[h], dsa_bf)
                ddec = ddec + jnp.sum(dsa[h] * stbs[h], axis=1, keepdims=True)
                dvs.append((_bnn(sc_t[h], dohs[h]) + _bnt(kteh[h], dsa_bf)).reshape(tg, GLA_DV))
            dqs.append((dqd * (scale * v["eb"])).reshape(tg, LANES))
            dks.append((dkd * v["emb"] + dkte * v["ekte"]).reshape(tg, LANES))
            db = dqd * v["qd"] - dkd * v["kdf"] - dkte * v["kte"]
            dblast = jnp.sum(dkte * v["kte"], axis=1, keepdims=True) + ddec * v["dec"]
            dla = _chunk_cumsum(db.reshape(tg, LANES), row_in_chunk, not reverse) + jnp.broadcast_to(dblast, (n, GLA_CHUNK, LANES)).reshape(tg, LANES)
            dpre = (dla * (1.0 / GLA_TAU) * _sigmoid(-v["pre"]))
            dpre_bf = dpre.astype(BF16)
            dlr = dlr + _nt(dpre_bf, wd_ref[:, cols])
            dwds.append(_tn(lr_bf, dpre_bf))
            dbds.append(jnp.sum(dpre, axis=0, keepdims=True))
        dlr_ref[...] = dlr
        for pair in range(2):
            cols = pl.ds(pair * LANES, LANES)
            dq_ref[:, cols] = dqs[pair]
            dk_ref[:, cols] = dks[pair]
            dwd_ref[:, cols] += dwds[pair]
            dbd_ref[:, cols] += dbds[pair]
        for h in range(GLA_HEADS):
            dv_ref[:, pl.ds(h * GLA_DV, GLA_DV)] = dvs[h]
            carry[h] = carries[h]

    return pl.pallas_call(
        body,
        name="gla_bwd_rev" if reverse else "gla_bwd",
        grid=(nt,),
        in_specs=[
            pl.BlockSpec((tg, KEY_W), lambda i: (tile(i), COL_Q // KEY_W)),
            pl.BlockSpec((tg, KEY_W), lambda i: (tile(i), COL_K // KEY_W)),
            pl.BlockSpec((tg, GLA_W), lambda i: (tile(i), COL_V // GLA_W)),
            pl.BlockSpec((tg, LANES), lambda i: (tile(i), COL_LR // LANES)),
            pl.BlockSpec((tg, GLA_W), lambda i: (tile(i), 0)),
            pl.BlockSpec((n, GLA_HEADS, GLA_DV, LANES), lambda i: (tile(i), 0, 0, 0)),
            _resident((LANES, KEY_W)),
            _resident((1, KEY_W)),
            _resident(TOKEN_SHAPE),
        ],
        out_specs=[
            pl.BlockSpec((tg, KEY_W), lambda i: (tile(i), 0)),
            pl.BlockSpec((tg, KEY_W), lambda i: (tile(i), 0)),
            pl.BlockSpec((tg, GLA_W), lambda i: (tile(i), 0)),
            pl.BlockSpec((tg, LANES), lambda i: (tile(i), 0)),
            pl.BlockSpec((LANES, KEY_W), lambda i: (0, 0)),
            pl.BlockSpec((1, KEY_W), lambda i: (0, 0)),
        ],
        out_shape=[
            jax.ShapeDtypeStruct((seq, KEY_W), F32), jax.ShapeDtypeStruct((seq, KEY_W), F32),
            jax.ShapeDtypeStruct((seq, GLA_W), F32), jax.ShapeDtypeStruct((seq, LANES), F32),
            jax.ShapeDtypeStruct((LANES, KEY_W), F32), jax.ShapeDtypeStruct((1, KEY_W), F32),
        ],
        scratch_shapes=[pltpu.VMEM((GLA_HEADS, GLA_DV, LANES), F32)],
        compiler_params=_params(48),
    )(p, p, p, p, do, st, wd_pad, bd, token)


def _inproj_wgrad(x, g1, dq_f, dq_b, dk_f, dk_b, dv_f, dv_b, dg, du, dvv, dlr_f, dlr_b):
    seq = x.shape[0]
    tm = min(seq, 512)

    def body(x_ref, g1_ref, dqf, dqb, dkf, dkb, dvf, dvb, dg_ref, du_ref, dvv_ref, dlrf, dlrb, dw_ref, dp_ref):
        @pl.when(pl.program_id(0) == 0)
        def _():
            dw_ref[...] = jnp.zeros_like(dw_ref)

        dp_ref[:, COL_Q : COL_Q + KEY_W] = (dqf[...] + dqb[...]).astype(BF16)
        dp_ref[:, COL_K : COL_K + KEY_W] = (dkf[...] + dkb[...]).astype(BF16)
        dp_ref[:, COL_V : COL_V + GLA_W] = (dvf[...] + dvb[...]).astype(BF16)
        dp_ref[:, COL_G : COL_G + GLA_W] = dg_ref[...].astype(BF16)
        dp_ref[:, COL_U : COL_U + GMLP_W] = du_ref[...].astype(BF16)
        dp_ref[:, COL_VV : COL_VV + GMLP_W] = dvv_ref[...].astype(BF16)
        dp_ref[:, COL_LR : COL_LR + LANES] = (dlrf[...] + dlrb[...]).astype(BF16)
        xv = x_ref[...]
        h = (xv * lax.rsqrt(jnp.mean(xv * xv, axis=-1, keepdims=True) + EPS) * g1_ref[...]).astype(BF16)
        dw_ref[0:ROW_LR, :] += _tn(dp_ref[:, 0:COL_U], h)
        dw_ref[ROW_UV:PROJ_W, :] += _tn(dp_ref[:, COL_U:COL_LR], h)
        dw_ref[ROW_LR:ROW_UV, :] += _tn(dp_ref[:, COL_LR:PROJ_WP], h)[0 : ROW_UV - ROW_LR]

    row = lambda w: pl.BlockSpec((tm, w), lambda i: (i, 0))
    return pl.pallas_call(
        body,
        name="inproj_wgrad",
        grid=(seq // tm,),
        in_specs=[
            row(D_MODEL), _resident((1, D_MODEL)),
            row(KEY_W), row(KEY_W), row(KEY_W), row(KEY_W), row(GLA_W), row(GLA_W),
            row(GLA_W), row(GMLP_W), row(GMLP_W), row(LANES), row(LANES),
        ],
        out_specs=[pl.BlockSpec((PROJ_W, D_MODEL), lambda i: (0, 0)), row(PROJ_WP)],
        out_shape=[jax.ShapeDtypeStruct((PROJ_W, D_MODEL), F32), jax.ShapeDtypeStruct((seq, PROJ_WP), BF16)],
        compiler_params=_params(56),
    )(x, g1, dq_f, dq_b, dk_f, dk_b, dv_f, dv_b, dg, du, dvv, dlr_f, dlr_b)


def _inproj_dx(x, dx1, g1, w_in_t, dp, token):
    seq = x.shape[0]
    tm = min(seq, 512)

    def body(x_ref, dx1_ref, g1_ref, w_ref, dp_ref, token_ref, dx_ref, dg1_ref):
        @pl.when(pl.program_id(0) == 0)
        def _():
            dg1_ref[...] = jnp.zeros_like(dg1_ref)

        xv = x_ref[...]
        r1 = lax.rsqrt(jnp.mean(xv * xv, axis=-1, keepdims=True) + EPS)
        xh = xv * r1
        dh = (_nn(dp_ref[:, 0:COL_U], w_ref[0:ROW_LR, :]) + _nn(dp_ref[:, COL_U:COL_LR], w_ref[ROW_UV:PROJ_W, :])
              + _nn(dp_ref[:, COL_LR:PROJ_WP], w_ref[ROW_LR : ROW_LR + LANES, :]))
        dg1_ref[...] += jnp.sum(dh * xh, axis=0, keepdims=True)
        dx_ref[...] = dx1_ref[...] + _rms_bwd(dh * g1_ref[...], xh, r1)

    row = lambda w: pl.BlockSpec((tm, w), lambda i: (i, 0))
    return pl.pallas_call(
        body,
        name="inproj_dx",
        grid=(seq // tm,),
        in_specs=[row(D_MODEL), row(D_MODEL), _resident((1, D_MODEL)), _resident((PROJ_W, D_MODEL)), row(PROJ_WP), _resident(TOKEN_SHAPE)],
        out_specs=[row(D_MODEL), pl.BlockSpec((1, D_MODEL), lambda i: (0, 0))],
        out_shape=[jax.ShapeDtypeStruct((seq, D_MODEL), F32), jax.ShapeDtypeStruct((1, D_MODEL), F32)],
        compiler_params=_params(48),
    )(x, dx1, g1, w_in_t, dp, token)


def _row_tile(rows, multiple=8):
    for t in range(min(rows, 512), 0, -1):
        if rows % t == 0 and t % multiple == 0:
            return t
    return rows


def _cast_into_slot(w, shard):
    rows, cols = w.shape
    tr = _row_tile(rows, 16)

    def body(s_ref, w_ref, o_ref):
        o_ref[...] = w_ref[...].astype(BF16)

    return pl.pallas_call(
        body,
        name="cast_into_slot",
        grid_spec=pltpu.PrefetchScalarGridSpec(
            num_scalar_prefetch=1,
            grid=(rows // tr,),
            in_specs=[pl.BlockSpec((tr, cols), lambda i, s_ref: (i, 0))],
            out_specs=pl.BlockSpec((None, tr, cols), lambda i, s_ref: (s_ref[0], i, 0)),
        ),
        out_shape=jax.ShapeDtypeStruct((N_SHARDS, rows, cols), BF16),
        compiler_params=_params(32, ("parallel",)),
    )(shard, w)


def _add_halves(g4, recv, c):
    _, rows, _ = g4.shape
    tr = _row_tile(rows, 16)

    def body(c_ref, g_ref, r_ref, o_ref, ob_ref):
        total = g_ref[...] + r_ref[...]
        o_ref[...] = total
        ob_ref[...] = total.astype(BF16)

    out = pl.BlockSpec((None, tr, HALF), lambda s, i, c_ref: (s, i, 0))
    return pl.pallas_call(
        body,
        name="add_halves",
        grid_spec=pltpu.PrefetchScalarGridSpec(
            num_scalar_prefetch=1,
            grid=(N_SHARDS, rows // tr),
            in_specs=[pl.BlockSpec((None, tr, HALF), lambda s, i, c_ref: (s, i, c_ref[0])), out],
            out_specs=[out, out],
        ),
        out_shape=[jax.ShapeDtypeStruct((N_SHARDS, rows, HALF), F32), jax.ShapeDtypeStruct((N_SHARDS, rows, HALF), BF16)],
        compiler_params=_params(32, ("parallel", "parallel")),
    )(c, g4, recv)


def _add_partials(part4, recv3, shard_core):
    _, rows, _ = part4.shape
    tr = _row_tile(rows, 16)

    def body(sc_ref, p_ref, r_ref, o_ref):
        o_ref[...] = ((p_ref[...] + r_ref[0].astype(F32)) + r_ref[1].astype(F32)) + r_ref[2].astype(F32)

    return pl.pallas_call(
        body,
        name="add_partials",
        grid_spec=pltpu.PrefetchScalarGridSpec(
            num_scalar_prefetch=1,
            grid=(rows // tr,),
            in_specs=[
                pl.BlockSpec((None, tr, HALF), lambda i, sc_ref: (sc_ref[0], i, 0)),
                pl.BlockSpec((3, tr, HALF), lambda i, sc_ref: (0, i, 0)),
            ],
            out_specs=pl.BlockSpec((tr, HALF), lambda i, sc_ref: (i, sc_ref[1])),
        ),
        out_shape=jax.ShapeDtypeStruct((rows, 2 * HALF), F32),
        compiler_params=_params(32, ("parallel",)),
    )(shard_core, part4, recv3)


def _adam_math(w, g, m, v):
    m = ADAM_B1 * m + (1.0 - ADAM_B1) * g
    v = ADAM_B2 * v + (1.0 - ADAM_B2) * (g * g)
    m_hat = m / (1.0 - ADAM_B1**ADAM_STEP)
    v_hat = v / (1.0 - ADAM_B2**ADAM_STEP)
    delta = -ADAM_LR * (m_hat / (jnp.sqrt(v_hat) + ADAM_EPS) + ADAM_WD * w)
    return delta, m, v


def _adamw(w, g, m, v):
    rows, cols = w.shape
    tr = _row_tile(rows)

    def body(w_ref, g_ref, m_ref, v_ref, go_ref, d_ref, mo_ref, vo_ref):
        gv = g_ref[...]
        go_ref[...] = gv
        d_ref[...], mo_ref[...], vo_ref[...] = _adam_math(w_ref[...], gv, m_ref[...], v_ref[...])

    spec = pl.BlockSpec((tr, cols), lambda i: (i, 0))
    shape = jax.ShapeDtypeStruct(w.shape, F32)
    return pl.pallas_call(
        body, name="adamw", grid=(rows // tr,), in_specs=[spec] * 4, out_specs=[spec] * 4, out_shape=[shape] * 4,
        compiler_params=_params(32, ("parallel",)),
    )(w, g, m, v)


SMALL_ROWS = 560
DECAY_ROWS = 8
SMALL_TOTAL = SMALL_ROWS + 2 * N_SHARDS * DECAY_ROWS


def _adamw_small(gathered, wp, mp, vp):
    out_rows = SMALL_ROWS + 2 * DECAY_ROWS

    def body(ga_ref, w_ref, m_ref, v_ref, g_ref, d_ref, mo_ref, vo_ref):
        shard = 2 * lax.axis_index("x") + lax.axis_index("y")
        g_ref[pl.ds(0, SMALL_ROWS), :] = functools.reduce(lambda a, b: a + b, [ga_ref[d, pl.ds(0, SMALL_ROWS), :] for d in range(8)])
        for k in range(2):
            start = pl.multiple_of(SMALL_ROWS + k * N_SHARDS * DECAY_ROWS + shard * DECAY_ROWS, DECAY_ROWS)
            g_ref[pl.ds(SMALL_ROWS + k * DECAY_ROWS, DECAY_ROWS), :] = functools.reduce(
                lambda a, b: a + b, [ga_ref[d, pl.ds(start, DECAY_ROWS), :] for d in range(8)])
        d_ref[...], mo_ref[...], vo_ref[...] = _adam_math(w_ref[...], g_ref[...], m_ref[...], v_ref[...])

    shape = jax.ShapeDtypeStruct((out_rows, LANES), F32)
    return pl.pallas_call(body, name="adamw_small", out_shape=[shape] * 4, compiler_params=_params(32, None))(gathered, wp, mp, vp)


ANY = pl.BlockSpec(memory_space=pl.ANY)


def _position():
    return lax.axis_index("x"), lax.axis_index("y"), lax.axis_index("c")


def _other_chips(x, y):
    return [(1 - x, y), (x, 1 - y), (1 - x, 1 - y)]


HBM = pl.BlockSpec(memory_space=pltpu.HBM)
SEM = pl.BlockSpec(memory_space=pltpu.SEMAPHORE)
TOKEN = jax.ShapeDtypeStruct(TOKEN_SHAPE, F32)
DATAFLOW = pltpu.SideEffectType.DATAFLOW_SIDE_EFFECTING


def _half_block(ref4, slot, core):
    return ref4.at[slot, :, pl.ds(pl.multiple_of(core * HALF, HALF), HALF)]


def _gather_ici_copies(refs4, send_sems, recv_sems, stride):
    x, y, c = _position()
    pairs = []
    for k, ref4 in enumerate(refs4):
        mine = _half_block(ref4, 2 * x + y, c)
        for j, (px, py) in enumerate(_other_chips(x, y)):
            sems = dict(send_sem=send_sems.at[stride * k + j], recv_sem=recv_sems.at[stride * k + j], device_id=(px, py, c), device_id_type=MESH)
            pairs.append((functools.partial(pltpu.make_async_remote_copy, src_ref=mine, dst_ref=mine, **sems),
                          functools.partial(pltpu.make_async_remote_copy, src_ref=mine, dst_ref=_half_block(ref4, 2 * px + py, c), **sems)))
    return pairs


def _gather_d2d_copies(refs4, send_sems, recv_sems, stride, offset):
    x, y, c = _position()
    pairs = []
    for k, ref4 in enumerate(refs4):
        for j, (px, py) in enumerate(_other_chips(x, y)):
            have = _half_block(ref4, 2 * px + py, c)
            sems = dict(send_sem=send_sems.at[stride * k + offset + j], recv_sem=recv_sems.at[stride * k + offset + j],
                        device_id=(x, y, 1 - c), device_id_type=MESH)
            pairs.append((functools.partial(pltpu.make_async_remote_copy, src_ref=have, dst_ref=have, **sems),
                          functools.partial(pltpu.make_async_remote_copy, src_ref=have, dst_ref=_half_block(ref4, 2 * px + py, 1 - c), **sems)))
    return pairs


def _gather_sync(bufs):
    n = len(bufs)

    def body(*refs):
        outs = refs[n : 2 * n]
        send_sems, recv_sems = refs[2 * n :]
        ici = _gather_ici_copies(outs, send_sems, recv_sems, 6)
        d2d = _gather_d2d_copies(outs, send_sems, recv_sems, 6, 3)
        for send, _ in ici:
            send().start()
        for (_, arrival), (forward, _) in zip(ici, d2d):
            arrival().wait_recv()
            forward().start()
        for _, arrival in d2d:
            arrival().wait_recv()
        for send, _ in ici + d2d:
            send().wait_send()

    return pl.pallas_call(
        body,
        name="gather_sync",
        in_specs=[ANY] * n,
        out_specs=[ANY] * n,
        out_shape=[jax.ShapeDtypeStruct(b.shape, b.dtype) for b in bufs],
        input_output_aliases={k: k for k in range(n)},
        scratch_shapes=[pltpu.SemaphoreType.DMA((6 * n,)), pltpu.SemaphoreType.DMA((6 * n,))],
        compiler_params=pltpu.CompilerParams(has_side_effects=True),
    )(*bufs)


def _gather_start(bufs, after):
    n, na = len(bufs), len(after)

    def body(*refs):
        ins = refs[:n]
        send_sems, recv_sems = refs[n + na], refs[n + na + 1]
        token = refs[2 * n + na + 2]
        for send, _ in _gather_ici_copies(ins, send_sems, recv_sems, 3):
            send().start()
        token[...] = jnp.zeros_like(token)

    out = pl.pallas_call(
        body,
        name="gather_start",
        in_specs=[HBM] * n + [ANY] * na,
        out_specs=(SEM, SEM, *[HBM] * n, pl.BlockSpec(memory_space=pltpu.VMEM)),
        out_shape=(pltpu.SemaphoreType.DMA((3 * n,)), pltpu.SemaphoreType.DMA((3 * n,)), *[pltpu.HBM(b.shape, b.dtype) for b in bufs], TOKEN),
        input_output_aliases={k: 2 + k for k in range(n)},
        compiler_params=pltpu.CompilerParams(has_side_effects=DATAFLOW),
    )(*[pltpu.with_memory_space_constraint(b, pltpu.HBM) for b in bufs], *after)
    return out[0], out[1], list(out[2 : 2 + n]), out[2 + n]


def _gather_wait(send_sems, recv_sems, bufs, after):
    n = len(bufs)

    def body(*refs):
        ins = refs[:n]
        for send, arrival in _gather_ici_copies(ins, refs[n], refs[n + 1], 3):
            send().wait_send()
            arrival().wait_recv()

    return pl.pallas_call(
        body,
        name="gather_wait",
        in_specs=[HBM] * n + [SEM, SEM] + [ANY] * len(after),
        out_specs=tuple([HBM] * n),
        out_shape=tuple(pltpu.HBM(b.shape, b.dtype) for b in bufs),
        input_output_aliases={k: k for k in range(n)},
        compiler_params=pltpu.CompilerParams(has_side_effects=DATAFLOW),
    )(*bufs, send_sems, recv_sems, *after)


def _gather_forward(bufs):
    n = len(bufs)

    def body(*refs):
        outs = refs[n : 2 * n]
        send_sems, recv_sems = refs[2 * n :]
        d2d = _gather_d2d_copies(outs, send_sems, recv_sems, 3, 0)
        for forward, _ in d2d:
            forward().start()
        for forward, arrival in d2d:
            arrival().wait_recv()
            forward().wait_send()

    return pl.pallas_call(
        body,
        name="gather_forward",
        in_specs=[ANY] * n,
        out_specs=[ANY] * n,
        out_shape=[jax.ShapeDtypeStruct(b.shape, b.dtype) for b in bufs],
        input_output_aliases={k: k for k in range(n)},
        scratch_shapes=[pltpu.SemaphoreType.DMA((3 * n,)), pltpu.SemaphoreType.DMA((3 * n,))],
        compiler_params=pltpu.CompilerParams(has_side_effects=True),
    )(*bufs)


def _exchange_halves(grads4):
    n = len(grads4)

    def body(*refs):
        ins, outs = refs[:n], refs[n : 2 * n]
        send_sems, recv_sems = refs[2 * n :]
        x, y, c = _position()
        copies = []
        for k in range(n):
            cp = pltpu.make_async_remote_copy(
                src_ref=ins[k].at[:, :, pl.ds(pl.multiple_of((1 - c) * HALF, HALF), HALF)], dst_ref=outs[k],
                send_sem=send_sems.at[k], recv_sem=recv_sems.at[k], device_id=(x, y, 1 - c), device_id_type=MESH)
            cp.start()
            copies.append(cp)
        for cp in copies:
            cp.wait()

    return pl.pallas_call(
        body,
        name="exchange_halves",
        in_specs=[ANY] * n,
        out_specs=[ANY] * n,
        out_shape=[jax.ShapeDtypeStruct((N_SHARDS, g.shape[1], HALF), g.dtype) for g in grads4],
        scratch_shapes=[pltpu.SemaphoreType.DMA((n,)), pltpu.SemaphoreType.DMA((n,))],
        compiler_params=pltpu.CompilerParams(has_side_effects=True),
    )(*grads4)


def _scatter_partials(parts4):
    n = len(parts4)

    def body(*refs):
        ins, outs = refs[:n], refs[n : 2 * n]
        send_sems, recv_sems = refs[2 * n :]
        x, y, c = _position()
        copies = []
        for k in range(n):
            for j, (px, py) in enumerate(_other_chips(x, y)):
                cp = pltpu.make_async_remote_copy(
                    src_ref=ins[k].at[2 * px + py], dst_ref=outs[k].at[j],
                    send_sem=send_sems.at[3 * k + j], recv_sem=recv_sems.at[3 * k + j], device_id=(px, py, c), device_id_type=MESH)
                cp.start()
                copies.append(cp)
        for cp in copies:
            cp.wait()

    return pl.pallas_call(
        body,
        name="scatter_partials",
        in_specs=[ANY] * n,
        out_specs=[ANY] * n,
        out_shape=[jax.ShapeDtypeStruct((3,) + g.shape[1:], g.dtype) for g in parts4],
        scratch_shapes=[pltpu.SemaphoreType.DMA((3 * n,)), pltpu.SemaphoreType.DMA((3 * n,))],
        compiler_params=pltpu.CompilerParams(has_side_effects=True),
    )(*parts4)


def _scatter_copies(parts, lands, send_sems, recv_sems):
    x, y, c = _position()
    copies = []
    for k in range(len(parts)):
        for j, (px, py) in enumerate(_other_chips(x, y)):
            copies.append(pltpu.make_async_remote_copy(
                src_ref=parts[k].at[2 * px + py], dst_ref=lands[k].at[j],
                send_sem=send_sems.at[3 * k + j], recv_sem=recv_sems.at[3 * k + j], device_id=(px, py, c), device_id_type=MESH))
    return copies


def _exchange_copies(grads, lands, send_sems, recv_sems):
    x, y, c = _position()
    return [pltpu.make_async_remote_copy(
        src_ref=grads[k].at[:, :, pl.ds(pl.multiple_of((1 - c) * HALF, HALF), HALF)], dst_ref=lands[k],
        send_sem=send_sems.at[k], recv_sem=recv_sems.at[k], device_id=(x, y, 1 - c), device_id_type=MESH) for k in range(len(grads))]


def _exchange_lands(grads4):
    return [jax.ShapeDtypeStruct((N_SHARDS, g.shape[1], HALF), g.dtype) for g in grads4]


def _scatter_lands(parts4):
    return [jax.ShapeDtypeStruct((3,) + g.shape[1:], g.dtype) for g in parts4]


def _split_start(name, srcs, land_shapes, make_copies, nsem):
    n, nl = len(srcs), len(land_shapes)
    lands = [lax.empty(a.shape, a.dtype) for a in land_shapes]

    def body(*refs):
        send_sems, recv_sems = refs[n + nl], refs[n + nl + 1]
        token = refs[2 * (n + nl) + 2]
        for cp in make_copies(refs[:n], refs[n : n + nl], send_sems, recv_sems):
            cp.start()
        token[...] = jnp.zeros_like(token)

    hbm = lambda a: pltpu.HBM(a.shape, a.dtype)
    out = pl.pallas_call(
        body,
        name=name,
        in_specs=[HBM] * (n + nl),
        out_specs=(SEM, SEM, *[HBM] * (n + nl), pl.BlockSpec(memory_space=pltpu.VMEM)),
        out_shape=(pltpu.SemaphoreType.DMA((nsem,)), pltpu.SemaphoreType.DMA((nsem,)), *[hbm(a) for a in srcs + lands], TOKEN),
        input_output_aliases={k: 2 + k for k in range(n + nl)},
        compiler_params=pltpu.CompilerParams(has_side_effects=DATAFLOW),
    )(*[pltpu.with_memory_space_constraint(a, pltpu.HBM) for a in srcs + lands])
    return out[0], out[1], list(out[2 : 2 + n]), list(out[2 + n : 2 + n + nl]), out[2 + n + nl]


def _split_wait(name, send_sems, recv_sems, srcs, lands, make_copies, after):
    n, nl = len(srcs), len(lands)

    def body(*refs):
        for cp in make_copies(refs[:n], refs[n : n + nl], refs[n + nl], refs[n + nl + 1]):
            cp.wait_send()
            cp.wait_recv()

    hbm = lambda a: pltpu.HBM(a.shape, a.dtype)
    out = pl.pallas_call(
        body,
        name=name,
        in_specs=[HBM] * (n + nl) + [SEM, SEM] + [ANY] * len(after),
        out_specs=tuple([HBM] * (n + nl)),
        out_shape=tuple(hbm(a) for a in srcs + lands),
        input_output_aliases={k: k for k in range(n + nl)},
        compiler_params=pltpu.CompilerParams(has_side_effects=DATAFLOW),
    )(*srcs, *lands, send_sems, recv_sems, *after)
    return list(out[:n]), list(out[n:])


def _join_halves(bufs):
    n = len(bufs)

    def body(*refs):
        outs = refs[n : 2 * n]
        send_sems, recv_sems = refs[2 * n :]
        x, y, c = _position()
        half = lambda ref, core: ref.at[:, pl.ds(pl.multiple_of(core * HALF, HALF), HALF)]
        for k in range(n):
            mine = half(outs[k], c)
            pltpu.make_async_remote_copy(
                src_ref=mine, dst_ref=mine, send_sem=send_sems.at[k], recv_sem=recv_sems.at[k],
                device_id=(x, y, 1 - c), device_id_type=MESH).start()
        for k in range(n):
            wait = pltpu.make_async_remote_copy(
                src_ref=half(outs[k], c), dst_ref=half(outs[k], 1 - c), send_sem=send_sems.at[k], recv_sem=recv_sems.at[k],
                device_id=(x, y, 1 - c), device_id_type=MESH)
            wait.wait_send()
            wait.wait_recv()

    return pl.pallas_call(
        body,
        name="join_halves",
        in_specs=[ANY] * n,
        out_specs=[ANY] * n,
        out_shape=[jax.ShapeDtypeStruct(b.shape, b.dtype) for b in bufs],
        input_output_aliases={k: k for k in range(n)},
        scratch_shapes=[pltpu.SemaphoreType.DMA((n,)), pltpu.SemaphoreType.DMA((n,))],
        compiler_params=pltpu.CompilerParams(has_side_effects=True),
    )(*bufs)


def _allgather_small(block):
    m_per, ncol = block.shape

    def body(x_ref, out_ref, send_sems, recv_sems, local_sem):
        x, y, c = _position()
        me, sibling = (x, y, c), (x, y, 1 - c)
        chips = _other_chips(x, y)

        def rows(px, py, pc):
            return out_ref.at[4 * px + 2 * py + pc]

        def copy(k, blk, to, src=None):
            return pltpu.make_async_remote_copy(
                src_ref=rows(*blk) if src is None else src, dst_ref=rows(*blk),
                send_sem=send_sems.at[k], recv_sem=recv_sems.at[k], device_id=to, device_id_type=MESH)

        mine = pltpu.make_async_copy(x_ref, rows(*me), local_sem)
        mine.start()
        first = [copy(0, me, sibling, src=x_ref)] + [copy(1 + j, me, (*chip, c), src=x_ref) for j, chip in enumerate(chips)]
        for cp in first:
            cp.start()
        passed = [copy(4 + j, (*chip, c), sibling) for j, chip in enumerate(chips)]
        for j, chip in enumerate(chips):
            copy(1 + j, (*chip, c), me).wait_recv()
            passed[j].start()
        copy(0, sibling, me).wait_recv()
        for j, chip in enumerate(chips):
            copy(4 + j, (*chip, 1 - c), me).wait_recv()
        for cp in first + passed:
            cp.wait_send()
        mine.wait()

    return pl.pallas_call(
        body,
        name="allgather_small",
        in_specs=[pl.BlockSpec(memory_space=pltpu.VMEM)],
        out_specs=pl.BlockSpec(memory_space=pltpu.VMEM),
        out_shape=jax.ShapeDtypeStruct((8, m_per, ncol), block.dtype),
        scratch_shapes=[pltpu.SemaphoreType.DMA((7,)), pltpu.SemaphoreType.DMA((7,)), pltpu.SemaphoreType.DMA],
        compiler_params=pltpu.CompilerParams(has_side_effects=True, vmem_limit_bytes=32 * MIB),
    )(block)


SMALL_NAMES = ["norm1_g", "b_decay_f", "b_decay_b", "gla_norm_g", "gmlp_ln_g", "gmlp_ln_b", "w_spatial", "b_spatial", "norm2_g", "final_norm_g"]


def _pack_small(parts, decay_parts):
    flat = jnp.concatenate([a.reshape(-1) for a in parts])
    flat = jnp.pad(flat, (0, SMALL_ROWS * LANES - flat.shape[0])).reshape(SMALL_ROWS, LANES)
    return jnp.concatenate([flat] + [d.reshape(-1, LANES) for d in decay_parts], axis=0)


def _unpack_small(packed, like):
    out, off = [], 0
    flat = packed[:SMALL_ROWS].reshape(-1)
    for a in like:
        out.append(flat[off : off + a.size].reshape(a.shape))
        off += a.size
    return out


def kernel(x, norm1_g, w_in, w_decay_f, b_decay_f, w_decay_b, b_decay_b, gla_norm_g, gmlp_ln_g, gmlp_ln_b, w_spatial, b_spatial, w_out, norm2_g, w_gate, w_up, w_down, final_norm_g, loss_target, m_norm1_g, m_w_in, m_w_decay_f, m_b_decay_f, m_w_decay_b, m_b_decay_b, m_gla_norm_g, m_gmlp_ln_g, m_gmlp_ln_b, m_w_spatial, m_b_spatial, m_w_out, m_norm2_g, m_w_gate, m_w_up, m_w_down, m_final_norm_g, v_norm1_g, v_w_in, v_w_decay_f, v_b_decay_f, v_w_decay_b, v_b_decay_b, v_gla_norm_g, v_gmlp_ln_g, v_gmlp_ln_b, v_w_spatial, v_b_spatial, v_w_out, v_norm2_g, v_w_gate, v_w_up, v_w_down, v_final_norm_g):
    args = dict(locals())
    cx, cy, cc = lax.axis_index("x"), lax.axis_index("y"), lax.axis_index("c")
    shard = 2 * cx + cy
    xs = x[0]
    target = loss_target[0]

    big_names = ["w_in", "w_out", "w_gate", "w_up", "w_down"]
    transposed = ("w_in", "w_gate", "w_up")
    rows_of = lambda pre, k: jnp.transpose(args[pre + k][0]) if k in transposed else args[pre + k][0]
    big_shards = {k: rows_of("", k) for k in big_names}
    c_arr = cc.reshape(1).astype(jnp.int32)
    s_arr = shard.reshape(1).astype(jnp.int32)
    sc_arr = jnp.stack([shard, cc]).astype(jnp.int32)
    slots = {k: _cast_into_slot(big_shards[k], s_arr) for k in big_names}
    (w_in4,) = _gather_sync([slots["w_in"]])
    w_in_t = w_in4.reshape(PROJ_W, D_MODEL)

    dec_block = jnp.concatenate([w_decay_f[0].reshape(-1, LANES), w_decay_b[0].reshape(-1, LANES)], axis=0)
    dec_all = _allgather_small(dec_block)
    late = ["w_out", "w_gate", "w_up", "w_down"]
    g_send, g_recv, late_bufs, token_gather = _gather_start([slots[k] for k in late], (w_in4, dec_all))
    dec_all = dec_all[::2].reshape(N_SHARDS, 2, LOWRANK, KEY_W // N_SHARDS)
    wdf_full = jnp.transpose(dec_all[:, 0], (1, 0, 2)).reshape(LOWRANK, KEY_W)
    wdb_full = jnp.transpose(dec_all[:, 1], (1, 0, 2)).reshape(LOWRANK, KEY_W)
    wd_pad_f = jnp.zeros((LANES, KEY_W), F32).at[0:LOWRANK].set(wdf_full).astype(BF16)
    wd_pad_b = jnp.zeros((LANES, KEY_W), F32).at[LOWRANK : 2 * LOWRANK].set(wdb_full).astype(BF16)

    ws_bf = w_spatial[0].astype(BF16)
    wst_bf = jnp.transpose(w_spatial[0], (0, 2, 1)).astype(BF16)
    bs_col = b_spatial[0].reshape(GMLP_GROUPS, GMLP_CHUNK, 1)

    p = _inproj(xs, norm1_g, w_in_t, token_gather)
    o_f, st_f = _gla_fwd(p, wd_pad_f, b_decay_f, reverse=False)
    o_b, st_b = _gla_fwd(p, wd_pad_b, b_decay_b, reverse=True)
    late_bufs = _gather_forward(_gather_wait(g_send, g_recv, late_bufs, (o_f, o_b)))
    w_out_full, wg_t, wu_t, wd = [b.reshape(-1, D_MODEL) for b in late_bufs]
    x1, ycat = _mixer_out(xs, o_f, o_b, p, gla_norm_g, gmlp_ln_g, gmlp_ln_b, ws_bf, bs_col, w_out_full)
    gf = final_norm_g.reshape(1, D_MODEL)
    h2, gate, up, act, dx2, loss_acc, dgf = _ffn_fwd(x1, target, norm2_g, gf, wg_t, wu_t, wd)

    dgate, dup, dx1, dg2 = _ffn_bwd(dx2, gate, up, x1, norm2_g, wg_t, wu_t, wd)
    ffn_grads4 = [g.reshape(N_SHARDS, FF_SHARD, D_MODEL) for g in _ffn_wgrad(h2, dgate, dup, act, dx2)]
    e_send, e_recv, e_srcs, e_lands, token_exchange = _split_start(
        "exchange_start", ffn_grads4, _exchange_lands(ffn_grads4), _exchange_copies, len(ffn_grads4))
    do, dg, du, dvv, dwo, dgn, dlng, dlnb, dws, dbs = _mixer_bwd(
        dx1, ycat, o_f, o_b, p, gla_norm_g, gmlp_ln_g, gmlp_ln_b, ws_bf, wst_bf, bs_col, w_out_full, token_exchange)
    ffn_mine, ffn_other = _split_wait("exchange_wait", e_send, e_recv, e_srcs, e_lands, _exchange_copies, (do,))
    ffn_parts = [_add_halves(g, r, c_arr) for g, r in zip(ffn_mine, ffn_other)]
    ffn_payload = [pb for _, pb in ffn_parts]
    s_send, s_recv, s_parts, s_lands, token_scatter = _split_start(
        "scatter_start", ffn_payload, _scatter_lands(ffn_payload), _scatter_copies, 3 * len(ffn_payload))
    dq_f, dk_f, dv_f, dlr_f, dwdec_f, dbdec_f = _gla_bwd(p, do, st_f, wd_pad_f, b_decay_f, token_scatter, reverse=False)
    dq_b, dk_b, dv_b, dlr_b, dwdec_b, dbdec_b = _gla_bwd(p, do, st_b, wd_pad_b, b_decay_b, token_scatter, reverse=True)
    dwin_t, dp = _inproj_wgrad(xs, norm1_g, dq_f, dq_b, dk_f, dk_b, dv_f, dv_b, dg, du, dvv, dlr_f, dlr_b)
    _, ffn_recv = _split_wait("scatter_wait", s_send, s_recv, s_parts, s_lands, _scatter_copies, (dwin_t,))

    dwin4 = dwin_t.reshape(N_SHARDS, PROJ_W // N_SHARDS, D_MODEL)
    dwo4 = dwo.reshape(N_SHARDS, D_MODEL // N_SHARDS, D_MODEL)
    proj_grads4 = [dwin4, dwo4]
    proj_parts = [_add_halves(g, r, c_arr) for g, r in zip(proj_grads4, _exchange_halves(proj_grads4))]
    proj_payload = [pb for _, pb in proj_parts]
    p_send, p_recv, p_parts, p_lands, token_proj = _split_start(
        "proj_scatter_start", proj_payload, _scatter_lands(proj_payload), _scatter_copies, 3 * len(proj_payload))
    dx, dg1 = _inproj_dx(xs, dx1, norm1_g, w_in_t, dp, token_proj)
    _, proj_recv = _split_wait("proj_scatter_wait", p_send, p_recv, p_parts, p_lands, _scatter_copies, (dx,))
    parts_f32 = [pf for pf, _ in proj_parts + ffn_parts]
    bufs = [_add_partials(pf, r, sc_arr) for pf, r in zip(parts_f32, proj_recv + ffn_recv)]
    big_grads = dict(zip(big_names, _join_halves(bufs)))

    dwdec_f16 = dwdec_f[0:LOWRANK]
    dwdec_b16 = dwdec_b[LOWRANK : 2 * LOWRANK]
    shard_major = lambda a: jnp.transpose(a.reshape(LOWRANK, N_SHARDS, KEY_W // N_SHARDS), (1, 0, 2))
    small_grads = {
        "norm1_g": dg1, "b_decay_f": dbdec_f, "b_decay_b": dbdec_b, "gla_norm_g": dgn, "gmlp_ln_g": dlng, "gmlp_ln_b": dlnb,
        "w_spatial": dws, "b_spatial": dbs, "norm2_g": dg2, "final_norm_g": dgf,
    }
    g_pack = _pack_small([small_grads[k] for k in SMALL_NAMES] + [loss_acc], [shard_major(dwdec_f16), shard_major(dwdec_b16)])
    g_all = _allgather_small(g_pack)
    pack_own = lambda pre: _pack_small([args[pre + k] for k in SMALL_NAMES], [args[pre + "w_decay_f"], args[pre + "w_decay_b"]])
    sg, sd, sm, sv = _adamw_small(g_all, pack_own(""), pack_own("m_"), pack_own("v_"))

    names = ["norm1_g", "w_in", "w_decay_f", "b_decay_f", "w_decay_b", "b_decay_b", "gla_norm_g", "gmlp_ln_g", "gmlp_ln_b",
             "w_spatial", "b_spatial", "w_out", "norm2_g", "w_gate", "w_up", "w_down", "final_norm_g"]
    like = [args[k] for k in SMALL_NAMES]
    results = {"g": {}, "d": {}, "m": {}, "v": {}}
    for tag, packed in (("g", sg), ("d", sd), ("m", sm), ("v", sv)):
        for k, a in zip(SMALL_NAMES, _unpack_small(packed, like)):
            results[tag][k] = a
        results[tag]["w_decay_f"] = packed[SMALL_ROWS : SMALL_ROWS + DECAY_ROWS].reshape(w_decay_f.shape)
        results[tag]["w_decay_b"] = packed[SMALL_ROWS + DECAY_ROWS :].reshape(w_decay_b.shape)
    for k in big_names:
        g, d, mo, vo = _adamw(big_shards[k], big_grads[k], rows_of("m_", k), rows_of("v_", k))
        for tag, a in (("g", g), ("d", d), ("m", mo), ("v", vo)):
            results[tag][k] = (jnp.transpose(a) if k in transposed else a).reshape(args[k].shape)

    loss = sg[:SMALL_ROWS].reshape(-1)[sum(a.size for a in like)]
    grad_x = dx.reshape(x.shape)
    return (loss, grad_x, *[results["g"][k] for k in names], *[results["d"][k] for k in names],
            *[results["m"][k] for k in names], *[results["v"][k] for k in names])
```

```python
import functools
import math

import jax
import jax.numpy as jnp
from jax import lax
from jax.experimental import pallas as pl
from jax.experimental.pallas import tpu as pltpu

F32, BF16 = jnp.float32, jnp.bfloat16

D_MODEL = 1024
GLA_HEADS = 4
GLA_DK = 64
GLA_DV = 128
KEY_W = GLA_HEADS * GLA_DK
GLA_W = GLA_HEADS * GLA_DV
GMLP_W = 512
GMLP_GROUPS = 4
GMLP_CHUNK = 128
LOWRANK = 16
GLA_CHUNK = 64
GLA_TAU = 16.0
PROJ_W = 2592
PROJ_WP = 2688
D_FF = 2816
N_SHARDS = 4
FF_SHARD = D_FF // N_SHARDS
EPS = 1e-6
LANES = 128
TOKEN_SHAPE = (8, LANES)
MIB = 1024 * 1024

ADAM_LR = 0.001
ADAM_B1 = 0.9
ADAM_B2 = 0.999
ADAM_EPS = 1e-08
ADAM_WD = 0.01
ADAM_STEP = 10

COL_Q, COL_K = 0, 256
COL_V, COL_G, COL_U, COL_VV = 512, 1024, 1536, 2048
COL_LR = 2560
ROW_LR, ROW_UV = 1536, 1568
HALF = D_MODEL // 2

MESH = pl.DeviceIdType.MESH


def _nn(a, b):
    return jnp.dot(a, b, preferred_element_type=F32)


def _nt(a, b):
    return lax.dot_general(a, b, (((1,), (1,)), ((), ())), preferred_element_type=F32)


def _tn(a, b):
    return lax.dot_general(a, b, (((0,), (0,)), ((), ())), preferred_element_type=F32)


def _bnn(a, b):
    return jnp.einsum("nik,nkj->nij", a, b, preferred_element_type=F32)


def _bnt(a, b):
    return jnp.einsum("nik,njk->nij", a, b, preferred_element_type=F32)


def _btn(a, b):
    return jnp.einsum("nki,nkj->nij", a, b, preferred_element_type=F32)


def _resident(shape):
    zeros = (0,) * len(shape)
    return pl.BlockSpec(shape, lambda *_: zeros, pipeline_mode=pl.Buffered(1))


def _params(vmem_mib, semantics=("arbitrary",)):
    return pltpu.CompilerParams(vmem_limit_bytes=vmem_mib * MIB, dimension_semantics=semantics)


def _sigmoid(x):
    return 1.0 / (1.0 + jnp.exp(-x))


def _gelu(x):
    return 0.5 * x * (1.0 + lax.erf(x * (1.0 / math.sqrt(2.0))))


def _gelu_and_grad(x):
    cdf = 0.5 * (1.0 + lax.erf(x * (1.0 / math.sqrt(2.0))))
    return x * cdf, cdf + x * jnp.exp(-0.5 * x * x) * (1.0 / math.sqrt(2.0 * math.pi))


def _log_sigmoid(x):
    return jnp.minimum(x, 0.0) - jnp.log(1.0 + jnp.exp(-jnp.abs(x)))


def _rms_bwd(dxh, xh, r):
    return r * (dxh - xh * jnp.mean(dxh * xh, axis=-1, keepdims=True))


def _chunk_cumsum(v, row_in_chunk, reverse):
    rows = v.shape[0]
    for sh in (1, 2, 4, 8, 16, 32):
        if reverse:
            v = v + jnp.where(row_in_chunk + sh < GLA_CHUNK, pltpu.roll(v, rows - sh, axis=0), 0.0)
        else:
            v = v + jnp.where(row_in_chunk >= sh, pltpu.roll(v, sh, axis=0), 0.0)
    return v


def _inproj(x, g1, w_in_t, token):
    seq = x.shape[0]
    tm = min(seq, 512)

    def body(x_ref, g_ref, w_ref, token_ref, p_ref):
        xv = x_ref[...]
        r = lax.rsqrt(jnp.mean(xv * xv, axis=-1, keepdims=True) + EPS)
        h = (xv * r * g_ref[...]).astype(BF16)
        p_ref[:, 0:COL_U] = _nt(h, w_ref[0:ROW_LR, :])
        p_ref[:, COL_U:COL_LR] = _nt(h, w_ref[ROW_UV:PROJ_W, :])
        p_ref[:, COL_LR:PROJ_WP] = _nt(h, w_ref[ROW_LR : ROW_LR + LANES, :])

    return pl.pallas_call(
        body,
        name="inproj",
        grid=(seq // tm,),
        in_specs=[pl.BlockSpec((tm, D_MODEL), lambda i: (i, 0)), _resident((1, D_MODEL)), _resident((PROJ_W, D_MODEL)), _resident(TOKEN_SHAPE)],
        out_specs=pl.BlockSpec((tm, PROJ_WP), lambda i: (i, 0)),
        out_shape=jax.ShapeDtypeStruct((seq, PROJ_WP), F32),
        compiler_params=_params(48, ("parallel",)),
    )(x, g1, w_in_t, token)


def _gla_tile(seq):
    return min(seq, 1024)


def _gla_decay_terms(lr_bf, wd_ref, bd_ref, pair, row_in_chunk, reverse, n):
    cols = pl.ds(pair * LANES, LANES)
    pre = _nn(lr_bf, wd_ref[:, cols]) + bd_ref[:, cols]
    la = _log_sigmoid(pre) * (1.0 / GLA_TAU)
    b = _chunk_cumsum(la, row_in_chunk, reverse)
    b3 = b.reshape(n, GLA_CHUNK, LANES)
    blast = b3[:, 0:1, :] if reverse else b3[:, GLA_CHUNK - 1 : GLA_CHUNK, :]
    return pre, b3, blast


def _gla_fwd(p, wd_pad, bd, reverse):
    seq = p.shape[0]
    tg = _gla_tile(seq)
    nt = seq // tg
    n = tg // GLA_CHUNK
    scale = GLA_DK**-0.5

    def tile(i):
        return nt - 1 - i if reverse else i

    def body(q_ref, k_ref, v_ref, lr_ref, wd_ref, bd_ref, o_ref, st_ref, carry):
        @pl.when(pl.program_id(0) == 0)
        def _():
            carry[...] = jnp.zeros_like(carry)

        lr_bf = lr_ref[...].astype(BF16)
        states = [carry[h] for h in range(GLA_HEADS)]
        row_in_chunk = lax.broadcasted_iota(jnp.int32, (tg, LANES), 0) % GLA_CHUNK
        lane_head = lax.broadcasted_iota(jnp.int32, (1, LANES), 1) // GLA_DK
        tt = lax.broadcasted_iota(jnp.int32, (GLA_CHUNK, GLA_CHUNK), 0)
        ss = lax.broadcasted_iota(jnp.int32, (GLA_CHUNK, GLA_CHUNK), 1)
        causal = (tt <= ss) if reverse else (tt >= ss)
        order = range(n - 1, -1, -1) if reverse else range(n)
        heads = range(GLA_HEADS)
        qdh, kds, vhs, decs, sc_raw, dst = {}, {}, {}, {}, {}, {}
        for pair in range(2):
            cols = pl.ds(pair * LANES, LANES)
            _, b3, blast = _gla_decay_terms(lr_bf, wd_ref, bd_ref, pair, row_in_chunk, reverse, n)
            q3 = q_ref[:, cols].reshape(n, GLA_CHUNK, LANES) * scale
            k3 = k_ref[:, cols].reshape(n, GLA_CHUNK, LANES)
            qd = q3 * jnp.exp(b3)
            kd = (k3 * jnp.exp(-b3)).astype(BF16)
            kte = k3 * jnp.exp(blast - b3)
            dec = jnp.exp(blast)
            for hh in range(2):
                h = 2 * pair + hh
                m = (lane_head == hh).astype(F32)
                qdh[h], kds[h], decs[h] = (qd * m).astype(BF16), kd, dec
                vhs[h] = v_ref[:, pl.ds(h * GLA_DV, GLA_DV)].reshape(n, GLA_CHUNK, GLA_DV).astype(BF16)
                sc_raw[h] = _bnt(qdh[h], kd)
                dst[h] = _btn(vhs[h], (kte * m).astype(BF16))
        o_intra, befores = {}, {}
        for h in heads:
            o_intra[h] = _bnn(jnp.where(causal, sc_raw[h], 0.0).astype(BF16), vhs[h])
            st, before = states[h], [None] * n
            for j in order:
                before[j] = st
                st = st * decs[h][j] + dst[h][j]
            states[h] = st
            befores[h] = jnp.stack(before)
        outs = {h: (o_intra[h] + _bnt(qdh[h], befores[h].astype(BF16))).reshape(tg, GLA_DV) for h in heads}
        for h in range(GLA_HEADS):
            o_ref[:, pl.ds(h * GLA_DV, GLA_DV)] = outs[h]
            st_ref[:, h] = befores[h]
            carry[h] = states[h]

    nchunks = seq // GLA_CHUNK
    return pl.pallas_call(
        body,
        name="gla_fwd_rev" if reverse else "gla_fwd",
        grid=(nt,),
        in_specs=[
            pl.BlockSpec((tg, KEY_W), lambda i: (tile(i), COL_Q // KEY_W)),
            pl.BlockSpec((tg, KEY_W), lambda i: (tile(i), COL_K // KEY_W)),
            pl.BlockSpec((tg, GLA_W), lambda i: (tile(i), COL_V // GLA_W)),
            pl.BlockSpec((tg, LANES), lambda i: (tile(i), COL_LR // LANES)),
            _resident((LANES, KEY_W)),
            _resident((1, KEY_W)),
        ],
        out_specs=[
            pl.BlockSpec((tg, GLA_W), lambda i: (tile(i), 0)),
            pl.BlockSpec((n, GLA_HEADS, GLA_DV, LANES), lambda i: (tile(i), 0, 0, 0)),
        ],
        out_shape=[
            jax.ShapeDtypeStruct((seq, GLA_W), F32),
            jax.ShapeDtypeStruct((nchunks, GLA_HEADS, GLA_DV, LANES), F32),
        ],
        scratch_shapes=[pltpu.VMEM((GLA_HEADS, GLA_DV, LANES), F32)],
        compiler_params=_params(48),
    )(p, p, p, p, wd_pad, bd)


def _mixer_out(x, o_f, o_b, p, gn, lng, lnb, ws_bf, bs_col, w_out):
    seq = x.shape[0]
    tm = min(seq, 512)

    def body(x_ref, of_ref, ob_ref, g_ref, u_ref, vv_ref, gn_ref, lng_ref, lnb_ref, ws_ref, bs_ref, wo_ref, x1_ref, yc_ref, vn_sc):
        for h in range(GLA_HEADS):
            cols = pl.ds(h * GLA_DV, GLA_DV)
            oh = of_ref[:, cols] + ob_ref[:, cols]
            on = oh * lax.rsqrt(jnp.mean(oh * oh, axis=-1, keepdims=True) + EPS)
            gh = g_ref[:, cols]
            yc_ref[:, cols] = (on * gn_ref[:, cols] * (gh * _sigmoid(gh))).astype(BF16)
        zv = _gelu(vv_ref[...])
        xc = zv - jnp.mean(zv, axis=-1, keepdims=True)
        vhat = xc * lax.rsqrt(jnp.mean(xc * xc, axis=-1, keepdims=True) + EPS)
        vn_sc[...] = (vhat * lng_ref[...] + lnb_ref[...]).astype(BF16)
        for c in range(tm // GMLP_CHUNK):
            rows = pl.ds(c * GMLP_CHUNK, GMLP_CHUNK)
            for g in range(GMLP_GROUPS):
                cols = pl.ds(g * LANES, LANES)
                s = _nn(ws_ref[g], vn_sc[rows, cols]) + bs_ref[g]
                yc_ref[rows, pl.ds(GLA_W + g * LANES, LANES)] = (_gelu(u_ref[rows, cols]) * s).astype(BF16)
        x1_ref[...] = x_ref[...] + _nn(yc_ref[...], wo_ref[...])

    row = lambda w: pl.BlockSpec((tm, w), lambda i: (i, 0))
    pcol = lambda col: pl.BlockSpec((tm, GLA_W), lambda i: (i, col // GLA_W))
    return pl.pallas_call(
        body,
        name="mixer_out",
        grid=(seq // tm,),
        in_specs=[
            row(D_MODEL), row(GLA_W), row(GLA_W), pcol(COL_G), pcol(COL_U), pcol(COL_VV),
            _resident((1, GLA_W)), _resident((1, GMLP_W)), _resident((1, GMLP_W)),
            _resident((GMLP_GROUPS, GMLP_CHUNK, GMLP_CHUNK)), _resident((GMLP_GROUPS, GMLP_CHUNK, 1)),
            _resident((D_MODEL, D_MODEL)),
        ],
        out_specs=[row(D_MODEL), row(D_MODEL)],
        out_shape=[jax.ShapeDtypeStruct((seq, D_MODEL), F32), jax.ShapeDtypeStruct((seq, D_MODEL), BF16)],
        scratch_shapes=[pltpu.VMEM((tm, GMLP_W), BF16)],
        compiler_params=_params(48, ("parallel",)),
    )(x, o_f, o_b, p, p, p, gn, lng, lnb, ws_bf, bs_col, w_out)


def _ffn_fwd(x1, target, g2, gf, wg_t, wu_t, wd):
    seq = x1.shape[0]
    tm = min(seq, 256)

    def body(x1_ref, t_ref, g2_ref, gf_ref, wg_ref, wu_ref, wd_ref, h2_ref, gate_ref, up_ref, act_ref, dx2_ref, loss_ref, dgf_ref):
        @pl.when(pl.program_id(0) == 0)
        def _():
            loss_ref[...] = jnp.zeros_like(loss_ref)
            dgf_ref[...] = jnp.zeros_like(dgf_ref)

        x1v = x1_ref[...]
        h2 = (x1v * lax.rsqrt(jnp.mean(x1v * x1v, axis=-1, keepdims=True) + EPS) * g2_ref[...]).astype(BF16)
        h2_ref[...] = h2
        gate = _nt(h2, wg_ref[...])
        up = _nt(h2, wu_ref[...])
        act = (gate * _sigmoid(gate) * up).astype(BF16)
        gate_ref[...] = gate
        up_ref[...] = up
        act_ref[...] = act
        x2 = x1v + _nn(act, wd_ref[...])
        rf = lax.rsqrt(jnp.mean(x2 * x2, axis=-1, keepdims=True) + EPS)
        xh = x2 * rf
        err = xh * gf_ref[...] - t_ref[...]
        loss_ref[...] += 0.5 * jnp.sum(jnp.mean(err * err, axis=-1, keepdims=True))
        dy = err * (1.0 / D_MODEL)
        dgf_ref[...] += jnp.sum(dy * xh, axis=0, keepdims=True)
        dx2_ref[...] = _rms_bwd(dy * gf_ref[...], xh, rf)

    row = lambda w: pl.BlockSpec((tm, w), lambda i: (i, 0))
    weight = _resident((D_FF, D_MODEL))
    return pl.pallas_call(
        body,
        name="ffn_fwd",
        grid=(seq // tm,),
        in_specs=[row(D_MODEL), row(D_MODEL), _resident((1, D_MODEL)), _resident((1, D_MODEL)), weight, weight, weight],
        out_specs=[row(D_MODEL), row(D_FF), row(D_FF), row(D_FF), row(D_MODEL),
                   pl.BlockSpec((1, LANES), lambda i: (0, 0)), pl.BlockSpec((1, D_MODEL), lambda i: (0, 0))],
        out_shape=[
            jax.ShapeDtypeStruct((seq, D_MODEL), BF16),
            jax.ShapeDtypeStruct((seq, D_FF), F32),
            jax.ShapeDtypeStruct((seq, D_FF), F32),
            jax.ShapeDtypeStruct((seq, D_FF), BF16),
            jax.ShapeDtypeStruct((seq, D_MODEL), F32),
            jax.ShapeDtypeStruct((1, LANES), F32),
            jax.ShapeDtypeStruct((1, D_MODEL), F32),
        ],
        compiler_params=_params(56),
    )(x1, target, g2, gf, wg_t, wu_t, wd)


def _ffn_bwd(dx2, gate, up, x1, g2, wg_t, wu_t, wd):
    seq = x1.shape[0]
    tm = min(seq, 256)

    def body(dx2_ref, gate_ref, up_ref, x1_ref, g2_ref, wg_ref, wu_ref, wd_ref, dgate_ref, dup_ref, dx1_ref, dg2_ref):
        @pl.when(pl.program_id(0) == 0)
        def _():
            dg2_ref[...] = jnp.zeros_like(dg2_ref)

        dx2v = dx2_ref[...]
        dact = _nt(dx2v.astype(BF16), wd_ref[...])
        gate = gate_ref[...]
        sg = _sigmoid(gate)
        dgate = (dact * up_ref[...] * (sg * (1.0 + gate * (1.0 - sg)))).astype(BF16)
        dup = (dact * (gate * sg)).astype(BF16)
        dgate_ref[...] = dgate
        dup_ref[...] = dup
        dh2 = _nn(dgate, wg_ref[...]) + _nn(dup, wu_ref[...])
        x1v = x1_ref[...]
        r2 = lax.rsqrt(jnp.mean(x1v * x1v, axis=-1, keepdims=True) + EPS)
        xh = x1v * r2
        dg2_ref[...] += jnp.sum(dh2 * xh, axis=0, keepdims=True)
        dx1_ref[...] = dx2v + _rms_bwd(dh2 * g2_ref[...], xh, r2)

    row = lambda w: pl.BlockSpec((tm, w), lambda i: (i, 0))
    weight = _resident((D_FF, D_MODEL))
    return pl.pallas_call(
        body,
        name="ffn_bwd",
        grid=(seq // tm,),
        in_specs=[row(D_MODEL), row(D_FF), row(D_FF), row(D_MODEL), _resident((1, D_MODEL)), weight, weight, weight],
        out_specs=[row(D_FF), row(D_FF), row(D_MODEL), pl.BlockSpec((1, D_MODEL), lambda i: (0, 0))],
        out_shape=[
            jax.ShapeDtypeStruct((seq, D_FF), BF16),
            jax.ShapeDtypeStruct((seq, D_FF), BF16),
            jax.ShapeDtypeStruct((seq, D_MODEL), F32),
            jax.ShapeDtypeStruct((1, D_MODEL), F32),
        ],
        compiler_params=_params(56),
    )(dx2, gate, up, x1, g2, wg_t, wu_t, wd)


WGRAD_ROWS = D_FF // 2


def _ffn_wgrad(h2, dgate, dup, act, dx2):
    seq = h2.shape[0]
    tm = min(seq, 512)

    def body(h2_ref, dgate_ref, dup_ref, act_ref, dx2_ref, dwg_ref, dwu_ref, dwd_ref):
        @pl.when(pl.program_id(1) == 0)
        def _():
            dwg_ref[...] = jnp.zeros_like(dwg_ref)
            dwu_ref[...] = jnp.zeros_like(dwu_ref)
            dwd_ref[...] = jnp.zeros_like(dwd_ref)

        h2v = h2_ref[...]
        dwg_ref[...] += _tn(dgate_ref[...], h2v)
        dwu_ref[...] += _tn(dup_ref[...], h2v)
        dwd_ref[...] += _tn(act_ref[...], dx2_ref[...].astype(BF16))

    ff = pl.BlockSpec((tm, WGRAD_ROWS), lambda j, i: (i, j))
    row = pl.BlockSpec((tm, D_MODEL), lambda j, i: (i, 0))
    out = pl.BlockSpec((WGRAD_ROWS, D_MODEL), lambda j, i: (j, 0))
    return pl.pallas_call(
        body,
        name="ffn_wgrad",
        grid=(D_FF // WGRAD_ROWS, seq // tm),
        in_specs=[row, ff, ff, ff, row],
        out_specs=[out, out, out],
        out_shape=[jax.ShapeDtypeStruct((D_FF, D_MODEL), F32)] * 3,
        compiler_params=_params(56, ("parallel", "arbitrary")),
    )(h2, dgate, dup, act, dx2)


def _mixer_bwd(dx1, ycat, o_f, o_b, p, gn, lng, lnb, ws_bf, wst_bf, bs_col, w_out, token):
    seq = dx1.shape[0]
    tm = min(seq, 512)
    nsteps = seq // tm

    def body(dx1_ref, yc_ref, of_ref, ob_ref, g_ref, u_ref, vv_ref, gn_ref, lng_ref, lnb_ref, ws_ref, wst_ref, bs_ref, wo_ref, token_ref,
             do_ref, dg_ref, du_ref, dvv_ref, dwo_ref, dgn_ref, dlng_ref, dlnb_ref, dws_ref, dbs_ref, vn_sc, dvn_sc, dbs_acc):
        step = pl.program_id(0)

        @pl.when(step == 0)
        def _():
            for r in (dwo_ref, dgn_ref, dlng_ref, dlnb_ref, dws_ref, dbs_acc):
                r[...] = jnp.zeros_like(r)

        dx1b = dx1_ref[...].astype(BF16)
        dyc = _nt(dx1b, wo_ref[...])
        dwo_ref[...] += _tn(yc_ref[...], dx1b)
        for h in range(GLA_HEADS):
            cols = pl.ds(h * GLA_DV, GLA_DV)
            dya = dyc[:, h * GLA_DV : (h + 1) * GLA_DV]
            oh = of_ref[:, cols] + ob_ref[:, cols]
            rn = lax.rsqrt(jnp.mean(oh * oh, axis=-1, keepdims=True) + EPS)
            on = oh * rn
            gh = g_ref[:, cols]
            sg = _sigmoid(gh)
            sil = gh * sg
            gnh = gn_ref[:, cols]
            dgn_ref[:, cols] += jnp.sum(dya * on * sil, axis=0, keepdims=True)
            dg_ref[:, cols] = dya * on * gnh * (sg * (1.0 + gh * (1.0 - sg)))
            do_ref[:, cols] = _rms_bwd(dya * gnh * sil, on, rn)
        vv = vv_ref[...]
        zv, zv_grad = _gelu_and_grad(vv)
        xc = zv - jnp.mean(zv, axis=-1, keepdims=True)
        rstd = lax.rsqrt(jnp.mean(xc * xc, axis=-1, keepdims=True) + EPS)
        vhat = xc * rstd
        vn_sc[...] = (vhat * lng_ref[...] + lnb_ref[...]).astype(BF16)
        for c in range(tm // GMLP_CHUNK):
            rows = pl.ds(c * GMLP_CHUNK, GMLP_CHUNK)
            for g in range(GMLP_GROUPS):
                cols = pl.ds(g * LANES, LANES)
                vn = vn_sc[rows, cols]
                s = _nn(ws_ref[g], vn) + bs_ref[g]
                dyb = dyc[c * GMLP_CHUNK : (c + 1) * GMLP_CHUNK, GLA_W + g * LANES : GLA_W + (g + 1) * LANES]
                zu, zu_grad = _gelu_and_grad(u_ref[rows, cols])
                du_ref[rows, cols] = dyb * s * zu_grad
                ds = dyb * zu
                dbs_acc[g] += ds
                dsb = ds.astype(BF16)
                dws_ref[g] += _nt(dsb, vn)
                dvn_sc[rows, cols] = _nn(wst_ref[g], dsb)
        dvn = dvn_sc[...]
        dlng_ref[...] += jnp.sum(dvn * vhat, axis=0, keepdims=True)
        dlnb_ref[...] += jnp.sum(dvn, axis=0, keepdims=True)
        dvh = dvn * lng_ref[...]
        dzv = rstd * (dvh - jnp.mean(dvh, axis=-1, keepdims=True) - vhat * jnp.mean(dvh * vhat, axis=-1, keepdims=True))
        dvv_ref[...] = dzv * zv_grad

        @pl.when(step == nsteps - 1)
        def _():
            dbs_ref[...] = jnp.sum(dbs_acc[...], axis=-1, keepdims=True)

    row = lambda w: pl.BlockSpec((tm, w), lambda i: (i, 0))
    pcol = lambda col: pl.BlockSpec((tm, GLA_W), lambda i: (i, col // GLA_W))
    const = lambda shape: pl.BlockSpec(shape, lambda i: (0,) * len(shape))
    return pl.pallas_call(
        body,
        name="mixer_bwd",
        grid=(nsteps,),
        in_specs=[
            row(D_MODEL), row(D_MODEL), row(GLA_W), row(GLA_W), pcol(COL_G), pcol(COL_U), pcol(COL_VV),
            _resident((1, GLA_W)), _resident((1, GMLP_W)), _resident((1, GMLP_W)),
            _resident((GMLP_GROUPS, GMLP_CHUNK, GMLP_CHUNK)), _resident((GMLP_GROUPS, GMLP_CHUNK, GMLP_CHUNK)),
            _resident((GMLP_GROUPS, GMLP_CHUNK, 1)), _resident((D_MODEL, D_MODEL)), _resident(TOKEN_SHAPE),
        ],
        out_specs=[
            row(GLA_W), row(GLA_W), row(GMLP_W), row(GMLP_W), const((D_MODEL, D_MODEL)),
            const((1, GLA_W)), const((1, GMLP_W)), const((1, GMLP_W)),
            const((GMLP_GROUPS, GMLP_CHUNK, GMLP_CHUNK)), const((GMLP_GROUPS, GMLP_CHUNK, 1)),
        ],
        out_shape=[
            jax.ShapeDtypeStruct((seq, GLA_W), F32), jax.ShapeDtypeStruct((seq, GLA_W), F32),
            jax.ShapeDtypeStruct((seq, GMLP_W), F32), jax.ShapeDtypeStruct((seq, GMLP_W), F32),
            jax.ShapeDtypeStruct((D_MODEL, D_MODEL), F32),
            jax.ShapeDtypeStruct((1, GLA_W), F32), jax.ShapeDtypeStruct((1, GMLP_W), F32), jax.ShapeDtypeStruct((1, GMLP_W), F32),
            jax.ShapeDtypeStruct((GMLP_GROUPS, GMLP_CHUNK, GMLP_CHUNK), F32), jax.ShapeDtypeStruct((GMLP_GROUPS, GMLP_CHUNK, 1), F32),
        ],
        scratch_shapes=[pltpu.VMEM((tm, GMLP_W), BF16), pltpu.VMEM((tm, GMLP_W), F32), pltpu.VMEM((GMLP_GROUPS, GMLP_CHUNK, GMLP_CHUNK), F32)],
        compiler_params=_params(56),
    )(dx1, ycat, o_f, o_b, p, p, p, gn, lng, lnb, ws_bf, wst_bf, bs_col, w_out, token)


def _gla_bwd(p, do, st, wd_pad, bd, token, reverse):
    seq = p.shape[0]
    tg = _gla_tile(seq)
    nt = seq // tg
    n = tg // GLA_CHUNK
    scale = GLA_DK**-0.5

    def tile(i):
        return i if reverse else nt - 1 - i

    def body(q_ref, k_ref, v_ref, lr_ref, do_ref, st_ref, wd_ref, bd_ref, token_ref, dq_ref, dk_ref, dv_ref, dlr_ref, dwd_ref, dbd_ref, carry):
        @pl.when(pl.program_id(0) == 0)
        def _():
            carry[...] = jnp.zeros_like(carry)
            dwd_ref[...] = jnp.zeros_like(dwd_ref)
            dbd_ref[...] = jnp.zeros_like(dbd_ref)

        lr_bf = lr_ref[...].astype(BF16)
        carries = [carry[h] for h in range(GLA_HEADS)]
        row_in_chunk = lax.broadcasted_iota(jnp.int32, (tg, LANES), 0) % GLA_CHUNK
        lane_head = lax.broadcasted_iota(jnp.int32, (1, LANES), 1) // GLA_DK
        tt = lax.broadcasted_iota(jnp.int32, (GLA_CHUNK, GLA_CHUNK), 0)
        ss = lax.broadcasted_iota(jnp.int32, (GLA_CHUNK, GLA_CHUNK), 1)
        causal = (tt <= ss) if reverse else (tt >= ss)
        causal_t = (tt >= ss) if reverse else (tt <= ss)
        order = range(n) if reverse else range(n - 1, -1, -1)
        dlr = jnp.zeros((tg, LANES), F32)
        heads = range(GLA_HEADS)
        pv, masks, qdh, kteh, vhs, dohs, stbs = {}, {}, {}, {}, {}, {}, {}
        sc_t, dp, dp_t, acc = {}, {}, {}, {}
        for pair in range(2):
            cols = pl.ds(pair * LANES, LANES)
            pre, b3, blast = _gla_decay_terms(lr_bf, wd_ref, bd_ref, pair, row_in_chunk, reverse, n)
            q3 = q_ref[:, cols].reshape(n, GLA_CHUNK, LANES) * scale
            k3 = k_ref[:, cols].reshape(n, GLA_CHUNK, LANES)
            eb = jnp.exp(b3)
            emb = jnp.exp(-b3)
            ekte = jnp.exp(blast - b3)
            kdf = k3 * emb
            pv[pair] = dict(pre=pre, eb=eb, emb=emb, ekte=ekte, qd=q3 * eb, kdf=kdf, kd=kdf.astype(BF16), kte=k3 * ekte, dec=jnp.exp(blast))
            for hh in range(2):
                h = 2 * pair + hh
                vcols = pl.ds(h * GLA_DV, GLA_DV)
                masks[h] = (lane_head == hh).astype(F32)
                qdh[h] = (pv[pair]["qd"] * masks[h]).astype(BF16)
                kteh[h] = (pv[pair]["kte"] * masks[h]).astype(BF16)
                vhs[h] = v_ref[:, vcols].reshape(n, GLA_CHUNK, GLA_DV).astype(BF16)
                dohs[h] = do_ref[:, vcols].reshape(n, GLA_CHUNK, GLA_DV).astype(BF16)
                stbs[h] = st_ref[:, h]
                sc_t[h] = _bnt(pv[pair]["kd"], qdh[h])
                dp[h] = _bnt(dohs[h], vhs[h])
                dp_t[h] = _bnt(vhs[h], dohs[h])
                acc[h] = _btn(dohs[h], qdh[h])
        dsa = {}
        for h in heads:
            sc_t[h] = jnp.where(causal_t, sc_t[h], 0.0).astype(BF16)
            dp[h] = jnp.where(causal, dp[h], 0.0).astype(BF16)
            dp_t[h] = jnp.where(causal_t, dp_t[h], 0.0).astype(BF16)
            dec = pv[h // 2]["dec"]
            c, after = carries[h], [None] * n
            for j in order:
                after[j] = c
                c = acc[h][j] + dec[j] * c
            carries[h] = c
            dsa[h] = jnp.stack(after)
        dvs, dqs, dks, dwds, dbds = [], [], [], [], []
        for pair in range(2):
            cols = pl.ds(pair * LANES, LANES)
            v = pv[pair]
            dqd = jnp.zeros((n, GLA_CHUNK, LANES), F32)
            dkd = jnp.zeros((n, GLA_CHUNK, LANES), F32)
            dkte = jnp.zeros((n, GLA_CHUNK, LANES), F32)
            ddec = jnp.zeros((n, 1, LANES), F32)
            for h in (2 * pair, 2 * pair + 1):
                dsa_bf = dsa[h].astype(BF16)
                dqd = dqd + (_bnn(dp[h], v["kd"]) * masks[h] + _bnn(dohs[h], stbs[h].astype(BF16)))
                dkd = dkd + _bnn(dp_t[h], qdh[h])
                dkte = dkte + _bnn(vhs[h], dsa_bf)
                ddec = ddec + jnp.sum(dsa[h] * stbs[h], axis=1, keepdims=True)
                dvs.append((_bnn(sc_t[h], dohs[h]) + _bnt(kteh[h], dsa_bf)).reshape(tg, GLA_DV))
            dqs.append((dqd * (scale * v["eb"])).reshape(tg, LANES))
            dks.append((dkd * v["emb"] + dkte * v["ekte"]).reshape(tg, LANES))
            db = dqd * v["qd"] - dkd * v["kdf"] - dkte * v["kte"]
            dblast = jnp.sum(dkte * v["kte"], axis=1, keepdims=True) + ddec * v["dec"]
            dla = _chunk_cumsum(db.reshape(tg, LANES), row_in_chunk, not reverse) + jnp.broadcast_to(dblast, (n, GLA_CHUNK, LANES)).reshape(tg, LANES)
            dpre = (dla * (1.0 / GLA_TAU) * _sigmoid(-v["pre"]))
            dpre_bf = dpre.astype(BF16)
            dlr = dlr + _nt(dpre_bf, wd_ref[:, cols])
            dwds.append(_tn(lr_bf, dpre_bf))
            dbds.append(jnp.sum(dpre, axis=0, keepdims=True))
        dlr_ref[...] = dlr
        for pair in range(2):
            cols = pl.ds(pair * LANES, LANES)
            dq_ref[:, cols] = dqs[pair]
            dk_ref[:, cols] = dks[pair]
            dwd_ref[:, cols] += dwds[pair]
            dbd_ref[:, cols] += dbds[pair]
        for h in range(GLA_HEADS):
            dv_ref[:, pl.ds(h * GLA_DV, GLA_DV)] = dvs[h]
            carry[h] = carries[h]

    return pl.pallas_call(
        body,
        name="gla_bwd_rev" if reverse else "gla_bwd",
        grid=(nt,),
        in_specs=[
            pl.BlockSpec((tg, KEY_W), lambda i: (tile(i), COL_Q // KEY_W)),
            pl.BlockSpec((tg, KEY_W), lambda i: (tile(i), COL_K // KEY_W)),
            pl.BlockSpec((tg, GLA_W), lambda i: (tile(i), COL_V // GLA_W)),
            pl.BlockSpec((tg, LANES), lambda i: (tile(i), COL_LR // LANES)),
            pl.BlockSpec((tg, GLA_W), lambda i: (tile(i), 0)),
            pl.BlockSpec((n, GLA_HEADS, GLA_DV, LANES), lambda i: (tile(i), 0, 0, 0)),
            _resident((LANES, KEY_W)),
            _resident((1, KEY_W)),
            _resident(TOKEN_SHAPE),
        ],
        out_specs=[
            pl.BlockSpec((tg, KEY_W), lambda i: (tile(i), 0)),
            pl.BlockSpec((tg, KEY_W), lambda i: (tile(i), 0)),
            pl.BlockSpec((tg, GLA_W), lambda i: (tile(i), 0)),
            pl.BlockSpec((tg, LANES), lambda i: (tile(i), 0)),
            pl.BlockSpec((LANES, KEY_W), lambda i: (0, 0)),
            pl.BlockSpec((1, KEY_W), lambda i: (0, 0)),
        ],
        out_shape=[
            jax.ShapeDtypeStruct((seq, KEY_W), F32), jax.ShapeDtypeStruct((seq, KEY_W), F32),
            jax.ShapeDtypeStruct((seq, GLA_W), F32), jax.ShapeDtypeStruct((seq, LANES), F32),
            jax.ShapeDtypeStruct((LANES, KEY_W), F32), jax.ShapeDtypeStruct((1, KEY_W), F32),
        ],
        scratch_shapes=[pltpu.VMEM((GLA_HEADS, GLA_DV, LANES), F32)],
        compiler_params=_params(48),
    )(p, p, p, p, do, st, wd_pad, bd, token)


def _inproj_wgrad(x, g1, dq_f, dq_b, dk_f, dk_b, dv_f, dv_b, dg, du, dvv, dlr_f, dlr_b):
    seq = x.shape[0]
    tm = min(seq, 512)

    def body(x_ref, g1_ref, dqf, dqb, dkf, dkb, dvf, dvb, dg_ref, du_ref, dvv_ref, dlrf, dlrb, dw_ref, dp_ref):
        @pl.when(pl.program_id(0) == 0)
        def _():
            dw_ref[...] = jnp.zeros_like(dw_ref)

        dp_ref[:, COL_Q : COL_Q + KEY_W] = (dqf[...] + dqb[...]).astype(BF16)
        dp_ref[:, COL_K : COL_K + KEY_W] = (dkf[...] + dkb[...]).astype(BF16)
        dp_ref[:, COL_V : COL_V + GLA_W] = (dvf[...] + dvb[...]).astype(BF16)
        dp_ref[:, COL_G : COL_G + GLA_W] = dg_ref[...].astype(BF16)
        dp_ref[:, COL_U : COL_U + GMLP_W] = du_ref[...].astype(BF16)
        dp_ref[:, COL_VV : COL_VV + GMLP_W] = dvv_ref[...].astype(BF16)
        dp_ref[:, COL_LR : COL_LR + LANES] = (dlrf[...] + dlrb[...]).astype(BF16)
        xv = x_ref[...]
        h = (xv * lax.rsqrt(jnp.mean(xv * xv, axis=-1, keepdims=True) + EPS) * g1_ref[...]).astype(BF16)
        dw_ref[0:ROW_LR, :] += _tn(dp_ref[:, 0:COL_U], h)
        dw_ref[ROW_UV:PROJ_W, :] += _tn(dp_ref[:, COL_U:COL_LR], h)
        dw_ref[ROW_LR:ROW_UV, :] += _tn(dp_ref[:, COL_LR:PROJ_WP], h)[0 : ROW_UV - ROW_LR]

    row = lambda w: pl.BlockSpec((tm, w), lambda i: (i, 0))
    return pl.pallas_call(
        body,
        name="inproj_wgrad",
        grid=(seq // tm,),
        in_specs=[
            row(D_MODEL), _resident((1, D_MODEL)),
            row(KEY_W), row(KEY_W), row(KEY_W), row(KEY_W), row(GLA_W), row(GLA_W),
            row(GLA_W), row(GMLP_W), row(GMLP_W), row(LANES), row(LANES),
        ],
        out_specs=[pl.BlockSpec((PROJ_W, D_MODEL), lambda i: (0, 0)), row(PROJ_WP)],
        out_shape=[jax.ShapeDtypeStruct((PROJ_W, D_MODEL), F32), jax.ShapeDtypeStruct((seq, PROJ_WP), BF16)],
        compiler_params=_params(56),
    )(x, g1, dq_f, dq_b, dk_f, dk_b, dv_f, dv_b, dg, du, dvv, dlr_f, dlr_b)


def _inproj_dx(x, dx1, g1, w_in_t, dp, token):
    seq = x.shape[0]
    tm = min(seq, 512)

    def body(x_ref, dx1_ref, g1_ref, w_ref, dp_ref, token_ref, dx_ref, dg1_ref):
        @pl.when(pl.program_id(0) == 0)
        def _():
            dg1_ref[...] = jnp.zeros_like(dg1_ref)

        xv = x_ref[...]
        r1 = lax.rsqrt(jnp.mean(xv * xv, axis=-1, keepdims=True) + EPS)
        xh = xv * r1
        dh = (_nn(dp_ref[:, 0:COL_U], w_ref[0:ROW_LR, :]) + _nn(dp_ref[:, COL_U:COL_LR], w_ref[ROW_UV:PROJ_W, :])
              + _nn(dp_ref[:, COL_LR:PROJ_WP], w_ref[ROW_LR : ROW_LR + LANES, :]))
        dg1_ref[...] += jnp.sum(dh * xh, axis=0, keepdims=True)
        dx_ref[...] = dx1_ref[...] + _rms_bwd(dh * g1_ref[...], xh, r1)

    row = lambda w: pl.BlockSpec((tm, w), lambda i: (i, 0))
    return pl.pallas_call(
        body,
        name="inproj_dx",
        grid=(seq // tm,),
        in_specs=[row(D_MODEL), row(D_MODEL), _resident((1, D_MODEL)), _resident((PROJ_W, D_MODEL)), row(PROJ_WP), _resident(TOKEN_SHAPE)],
        out_specs=[row(D_MODEL), pl.BlockSpec((1, D_MODEL), lambda i: (0, 0))],
        out_shape=[jax.ShapeDtypeStruct((seq, D_MODEL), F32), jax.ShapeDtypeStruct((1, D_MODEL), F32)],
        compiler_params=_params(48),
    )(x, dx1, g1, w_in_t, dp, token)


def _row_tile(rows, multiple=8):
    for t in range(min(rows, 512), 0, -1):
        if rows % t == 0 and t % multiple == 0:
            return t
    return rows


def _cast_into_slot(w, shard):
    rows, cols = w.shape
    tr = _row_tile(rows, 16)

    def body(s_ref, w_ref, o_ref):
        o_ref[...] = w_ref[...].astype(BF16)

    return pl.pallas_call(
        body,
        name="cast_into_slot",
        grid_spec=pltpu.PrefetchScalarGridSpec(
            num_scalar_prefetch=1,
            grid=(rows // tr,),
            in_specs=[pl.BlockSpec((tr, cols), lambda i, s_ref: (i, 0))],
            out_specs=pl.BlockSpec((None, tr, cols), lambda i, s_ref: (s_ref[0], i, 0)),
        ),
        out_shape=jax.ShapeDtypeStruct((N_SHARDS, rows, cols), BF16),
        compiler_params=_params(32, ("parallel",)),
    )(shard, w)


def _add_halves(grads4, recvs, c):
    n = len(grads4)
    _, rows, _ = grads4[0].shape
    tr = _row_tile(rows, 16)

    def body(c_ref, *refs):
        for k in range(n):
            total = refs[k][...] + refs[n + k][...]
            refs[2 * n + k][...] = total
            refs[3 * n + k][...] = total.astype(BF16)

    out = pl.BlockSpec((None, tr, HALF), lambda s, i, c_ref: (s, i, 0))
    mine = pl.BlockSpec((None, tr, HALF), lambda s, i, c_ref: (s, i, c_ref[0]))
    outs = pl.pallas_call(
        body,
        name="add_halves",
        grid_spec=pltpu.PrefetchScalarGridSpec(
            num_scalar_prefetch=1,
            grid=(N_SHARDS, rows // tr),
            in_specs=[mine] * n + [out] * n,
            out_specs=[out] * (2 * n),
        ),
        out_shape=[jax.ShapeDtypeStruct((N_SHARDS, rows, HALF), F32)] * n + [jax.ShapeDtypeStruct((N_SHARDS, rows, HALF), BF16)] * n,
        compiler_params=_params(48, ("parallel", "parallel")),
    )(c, *grads4, *recvs)
    return list(zip(outs[:n], outs[n:]))


def _add_partials(part4, recv3, shard_core):
    _, rows, _ = part4.shape
    tr = _row_tile(rows, 16)

    def body(sc_ref, p_ref, r_ref, o_ref):
        o_ref[...] = ((p_ref[...] + r_ref[0].astype(F32)) + r_ref[1].astype(F32)) + r_ref[2].astype(F32)

    return pl.pallas_call(
        body,
        name="add_partials",
        grid_spec=pltpu.PrefetchScalarGridSpec(
            num_scalar_prefetch=1,
            grid=(rows // tr,),
            in_specs=[
                pl.BlockSpec((None, tr, HALF), lambda i, sc_ref: (sc_ref[0], i, 0)),
                pl.BlockSpec((3, tr, HALF), lambda i, sc_ref: (0, i, 0)),
            ],
            out_specs=pl.BlockSpec((tr, HALF), lambda i, sc_ref: (i, sc_ref[1])),
        ),
        out_shape=jax.ShapeDtypeStruct((rows, 2 * HALF), F32),
        compiler_params=_params(32, ("parallel",)),
    )(shard_core, part4, recv3)


def _adam_math(w, g, m, v):
    m = ADAM_B1 * m + (1.0 - ADAM_B1) * g
    v = ADAM_B2 * v + (1.0 - ADAM_B2) * (g * g)
    m_hat = m / (1.0 - ADAM_B1**ADAM_STEP)
    v_hat = v / (1.0 - ADAM_B2**ADAM_STEP)
    delta = -ADAM_LR * (m_hat / (jnp.sqrt(v_hat) + ADAM_EPS) + ADAM_WD * w)
    return delta, m, v


def _adamw(w, g, m, v):
    rows, cols = w.shape
    tr = _row_tile(rows)

    def body(w_ref, g_ref, m_ref, v_ref, go_ref, d_ref, mo_ref, vo_ref):
        gv = g_ref[...]
        go_ref[...] = gv
        d_ref[...], mo_ref[...], vo_ref[...] = _adam_math(w_ref[...], gv, m_ref[...], v_ref[...])

    spec = pl.BlockSpec((tr, cols), lambda i: (i, 0))
    shape = jax.ShapeDtypeStruct(w.shape, F32)
    return pl.pallas_call(
        body, name="adamw", grid=(rows // tr,), in_specs=[spec] * 4, out_specs=[spec] * 4, out_shape=[shape] * 4,
        compiler_params=_params(32, ("parallel",)),
    )(w, g, m, v)


SMALL_ROWS = 560
DECAY_ROWS = 8
SMALL_TOTAL = SMALL_ROWS + 2 * N_SHARDS * DECAY_ROWS


def _adamw_small(gathered, wp, mp, vp):
    out_rows = SMALL_ROWS + 2 * DECAY_ROWS

    def body(ga_ref, w_ref, m_ref, v_ref, g_ref, d_ref, mo_ref, vo_ref):
        shard = 2 * lax.axis_index("x") + lax.axis_index("y")
        g_ref[pl.ds(0, SMALL_ROWS), :] = functools.reduce(lambda a, b: a + b, [ga_ref[d, pl.ds(0, SMALL_ROWS), :] for d in range(8)])
        for k in range(2):
            start = pl.multiple_of(SMALL_ROWS + k * N_SHARDS * DECAY_ROWS + shard * DECAY_ROWS, DECAY_ROWS)
            g_ref[pl.ds(SMALL_ROWS + k * DECAY_ROWS, DECAY_ROWS), :] = functools.reduce(
                lambda a, b: a + b, [ga_ref[d, pl.ds(start, DECAY_ROWS), :] for d in range(8)])
        d_ref[...], mo_ref[...], vo_ref[...] = _adam_math(w_ref[...], g_ref[...], m_ref[...], v_ref[...])

    shape = jax.ShapeDtypeStruct((out_rows, LANES), F32)
    return pl.pallas_call(body, name="adamw_small", out_shape=[shape] * 4, compiler_params=_params(32, None))(gathered, wp, mp, vp)


ANY = pl.BlockSpec(memory_space=pl.ANY)


def _position():
    return lax.axis_index("x"), lax.axis_index("y"), lax.axis_index("c")


def _other_chips(x, y):
    return [(1 - x, y), (x, 1 - y), (1 - x, 1 - y)]


HBM = pl.BlockSpec(memory_space=pltpu.HBM)
SEM = pl.BlockSpec(memory_space=pltpu.SEMAPHORE)
TOKEN = jax.ShapeDtypeStruct(TOKEN_SHAPE, F32)
DATAFLOW = pltpu.SideEffectType.DATAFLOW_SIDE_EFFECTING


def _half_block(ref4, slot, core):
    return ref4.at[slot, :, pl.ds(pl.multiple_of(core * HALF, HALF), HALF)]


def _gather_ici_copies(refs4, send_sems, recv_sems, stride):
    x, y, c = _position()
    pairs = []
    for k, ref4 in enumerate(refs4):
        mine = _half_block(ref4, 2 * x + y, c)
        for j, (px, py) in enumerate(_other_chips(x, y)):
            sems = dict(send_sem=send_sems.at[stride * k + j], recv_sem=recv_sems.at[stride * k + j], device_id=(px, py, c), device_id_type=MESH)
            pairs.append((functools.partial(pltpu.make_async_remote_copy, src_ref=mine, dst_ref=mine, **sems),
                          functools.partial(pltpu.make_async_remote_copy, src_ref=mine, dst_ref=_half_block(ref4, 2 * px + py, c), **sems)))
    return pairs


def _gather_d2d_copies(refs4, send_sems, recv_sems, stride, offset):
    x, y, c = _position()
    pairs = []
    for k, ref4 in enumerate(refs4):
        for j, (px, py) in enumerate(_other_chips(x, y)):
            have = _half_block(ref4, 2 * px + py, c)
            sems = dict(send_sem=send_sems.at[stride * k + offset + j], recv_sem=recv_sems.at[stride * k + offset + j],
                        device_id=(x, y, 1 - c), device_id_type=MESH)
            pairs.append((functools.partial(pltpu.make_async_remote_copy, src_ref=have, dst_ref=have, **sems),
                          functools.partial(pltpu.make_async_remote_copy, src_ref=have, dst_ref=_half_block(ref4, 2 * px + py, 1 - c), **sems)))
    return pairs


def _gather_sync(bufs):
    n = len(bufs)

    def body(*refs):
        outs = refs[n : 2 * n]
        send_sems, recv_sems = refs[2 * n :]
        ici = _gather_ici_copies(outs, send_sems, recv_sems, 6)
        d2d = _gather_d2d_copies(outs, send_sems, recv_sems, 6, 3)
        for send, _ in ici:
            send().start()
        for (_, arrival), (forward, _) in zip(ici, d2d):
            arrival().wait_recv()
            forward().start()
        for _, arrival in d2d:
            arrival().wait_recv()
        for send, _ in ici + d2d:
            send().wait_send()

    return pl.pallas_call(
        body,
        name="gather_sync",
        in_specs=[ANY] * n,
        out_specs=[ANY] * n,
        out_shape=[jax.ShapeDtypeStruct(b.shape, b.dtype) for b in bufs],
        input_output_aliases={k: k for k in range(n)},
        scratch_shapes=[pltpu.SemaphoreType.DMA((6 * n,)), pltpu.SemaphoreType.DMA((6 * n,))],
        compiler_params=pltpu.CompilerParams(has_side_effects=True),
    )(*bufs)


def _gather_start(bufs, after):
    n, na = len(bufs), len(after)

    def body(*refs):
        ins = refs[:n]
        send_sems, recv_sems = refs[n + na], refs[n + na + 1]
        token = refs[2 * n + na + 2]
        for send, _ in _gather_ici_copies(ins, send_sems, recv_sems, 3):
            send().start()
        token[...] = jnp.zeros_like(token)

    out = pl.pallas_call(
        body,
        name="gather_start",
        in_specs=[HBM] * n + [ANY] * na,
        out_specs=(SEM, SEM, *[HBM] * n, pl.BlockSpec(memory_space=pltpu.VMEM)),
        out_shape=(pltpu.SemaphoreType.DMA((3 * n,)), pltpu.SemaphoreType.DMA((3 * n,)), *[pltpu.HBM(b.shape, b.dtype) for b in bufs], TOKEN),
        input_output_aliases={k: 2 + k for k in range(n)},
        compiler_params=pltpu.CompilerParams(has_side_effects=DATAFLOW),
    )(*[pltpu.with_memory_space_constraint(b, pltpu.HBM) for b in bufs], *after)
    return out[0], out[1], list(out[2 : 2 + n]), out[2 + n]


def _gather_wait(send_sems, recv_sems, bufs, after):
    n = len(bufs)

    def body(*refs):
        ins = refs[:n]
        for send, arrival in _gather_ici_copies(ins, refs[n], refs[n + 1], 3):
            send().wait_send()
            arrival().wait_recv()

    return pl.pallas_call(
        body,
        name="gather_wait",
        in_specs=[HBM] * n + [SEM, SEM] + [ANY] * len(after),
        out_specs=tuple([HBM] * n),
        out_shape=tuple(pltpu.HBM(b.shape, b.dtype) for b in bufs),
        input_output_aliases={k: k for k in range(n)},
        compiler_params=pltpu.CompilerParams(has_side_effects=DATAFLOW),
    )(*bufs, send_sems, recv_sems, *after)


def _gather_forward(bufs):
    n = len(bufs)

    def body(*refs):
        outs = refs[n : 2 * n]
        send_sems, recv_sems = refs[2 * n :]
        d2d = _gather_d2d_copies(outs, send_sems, recv_sems, 3, 0)
        for forward, _ in d2d:
            forward().start()
        for forward, arrival in d2d:
            arrival().wait_recv()
            forward().wait_send()

    return pl.pallas_call(
        body,
        name="gather_forward",
        in_specs=[ANY] * n,
        out_specs=[ANY] * n,
        out_shape=[jax.ShapeDtypeStruct(b.shape, b.dtype) for b in bufs],
        input_output_aliases={k: k for k in range(n)},
        scratch_shapes=[pltpu.SemaphoreType.DMA((3 * n,)), pltpu.SemaphoreType.DMA((3 * n,))],
        compiler_params=pltpu.CompilerParams(has_side_effects=True),
    )(*bufs)


def _exchange_halves(grads4):
    n = len(grads4)

    def body(*refs):
        ins, outs = refs[:n], refs[n : 2 * n]
        send_sems, recv_sems = refs[2 * n :]
        x, y, c = _position()
        copies = []
        for k in range(n):
            cp = pltpu.make_async_remote_copy(
                src_ref=ins[k].at[:, :, pl.ds(pl.multiple_of((1 - c) * HALF, HALF), HALF)], dst_ref=outs[k],
                send_sem=send_sems.at[k], recv_sem=recv_sems.at[k], device_id=(x, y, 1 - c), device_id_type=MESH)
            cp.start()
            copies.append(cp)
        for cp in copies:
            cp.wait()

    return pl.pallas_call(
        body,
        name="exchange_halves",
        in_specs=[ANY] * n,
        out_specs=[ANY] * n,
        out_shape=[jax.ShapeDtypeStruct((N_SHARDS, g.shape[1], HALF), g.dtype) for g in grads4],
        scratch_shapes=[pltpu.SemaphoreType.DMA((n,)), pltpu.SemaphoreType.DMA((n,))],
        compiler_params=pltpu.CompilerParams(has_side_effects=True),
    )(*grads4)


def _scatter_partials(parts4):
    n = len(parts4)

    def body(*refs):
        ins, outs = refs[:n], refs[n : 2 * n]
        send_sems, recv_sems = refs[2 * n :]
        x, y, c = _position()
        copies = []
        for k in range(n):
            for j, (px, py) in enumerate(_other_chips(x, y)):
                cp = pltpu.make_async_remote_copy(
                    src_ref=ins[k].at[2 * px + py], dst_ref=outs[k].at[j],
                    send_sem=send_sems.at[3 * k + j], recv_sem=recv_sems.at[3 * k + j], device_id=(px, py, c), device_id_type=MESH)
                cp.start()
                copies.append(cp)
        for cp in copies:
            cp.wait()

    return pl.pallas_call(
        body,
        name="scatter_partials",
        in_specs=[ANY] * n,
        out_specs=[ANY] * n,
        out_shape=[jax.ShapeDtypeStruct((3,) + g.shape[1:], g.dtype) for g in parts4],
        scratch_shapes=[pltpu.SemaphoreType.DMA((3 * n,)), pltpu.SemaphoreType.DMA((3 * n,))],
        compiler_params=pltpu.CompilerParams(has_side_effects=True),
    )(*parts4)


def _scatter_copies(parts, lands, send_sems, recv_sems):
    x, y, c = _position()
    copies = []
    for k in range(len(parts)):
        for j, (px, py) in enumerate(_other_chips(x, y)):
            copies.append(pltpu.make_async_remote_copy(
                src_ref=parts[k].at[2 * px + py], dst_ref=lands[k].at[j],
                send_sem=send_sems.at[3 * k + j], recv_sem=recv_sems.at[3 * k + j], device_id=(px, py, c), device_id_type=MESH))
    return copies


def _exchange_copies(grads, lands, send_sems, recv_sems):
    x, y, c = _position()
    return [pltpu.make_async_remote_copy(
        src_ref=grads[k].at[:, :, pl.ds(pl.multiple_of((1 - c) * HALF, HALF), HALF)], dst_ref=lands[k],
        send_sem=send_sems.at[k], recv_sem=recv_sems.at[k], device_id=(x, y, 1 - c), device_id_type=MESH) for k in range(len(grads))]


def _exchange_lands(grads4):
    return [jax.ShapeDtypeStruct((N_SHARDS, g.shape[1], HALF), g.dtype) for g in grads4]


def _scatter_lands(parts4):
    return [jax.ShapeDtypeStruct((3,) + g.shape[1:], g.dtype) for g in parts4]


def _split_start(name, srcs, land_shapes, make_copies, nsem):
    n, nl = len(srcs), len(land_shapes)
    lands = [lax.empty(a.shape, a.dtype) for a in land_shapes]

    def body(*refs):
        send_sems, recv_sems = refs[n + nl], refs[n + nl + 1]
        token = refs[2 * (n + nl) + 2]
        for cp in make_copies(refs[:n], refs[n : n + nl], send_sems, recv_sems):
            cp.start()
        token[...] = jnp.zeros_like(token)

    hbm = lambda a: pltpu.HBM(a.shape, a.dtype)
    out = pl.pallas_call(
        body,
        name=name,
        in_specs=[HBM] * (n + nl),
        out_specs=(SEM, SEM, *[HBM] * (n + nl), pl.BlockSpec(memory_space=pltpu.VMEM)),
        out_shape=(pltpu.SemaphoreType.DMA((nsem,)), pltpu.SemaphoreType.DMA((nsem,)), *[hbm(a) for a in srcs + lands], TOKEN),
        input_output_aliases={k: 2 + k for k in range(n + nl)},
        compiler_params=pltpu.CompilerParams(has_side_effects=DATAFLOW),
    )(*[pltpu.with_memory_space_constraint(a, pltpu.HBM) for a in srcs + lands])
    return out[0], out[1], list(out[2 : 2 + n]), list(out[2 + n : 2 + n + nl]), out[2 + n + nl]


def _split_wait(name, send_sems, recv_sems, srcs, lands, make_copies, after):
    n, nl = len(srcs), len(lands)

    def body(*refs):
        for cp in make_copies(refs[:n], refs[n : n + nl], refs[n + nl], refs[n + nl + 1]):
            cp.wait_send()
            cp.wait_recv()

    hbm = lambda a: pltpu.HBM(a.shape, a.dtype)
    out = pl.pallas_call(
        body,
        name=name,
        in_specs=[HBM] * (n + nl) + [SEM, SEM] + [ANY] * len(after),
        out_specs=tuple([HBM] * (n + nl)),
        out_shape=tuple(hbm(a) for a in srcs + lands),
        input_output_aliases={k: k for k in range(n + nl)},
        compiler_params=pltpu.CompilerParams(has_side_effects=DATAFLOW),
    )(*srcs, *lands, send_sems, recv_sems, *after)
    return list(out[:n]), list(out[n:])


def _join_halves(bufs):
    n = len(bufs)

    def body(*refs):
        outs = refs[n : 2 * n]
        send_sems, recv_sems = refs[2 * n :]
        x, y, c = _position()
        half = lambda ref, core: ref.at[:, pl.ds(pl.multiple_of(core * HALF, HALF), HALF)]
        for k in range(n):
            mine = half(outs[k], c)
            pltpu.make_async_remote_copy(
                src_ref=mine, dst_ref=mine, send_sem=send_sems.at[k], recv_sem=recv_sems.at[k],
                device_id=(x, y, 1 - c), device_id_type=MESH).start()
        for k in range(n):
            wait = pltpu.make_async_remote_copy(
                src_ref=half(outs[k], c), dst_ref=half(outs[k], 1 - c), send_sem=send_sems.at[k], recv_sem=recv_sems.at[k],
                device_id=(x, y, 1 - c), device_id_type=MESH)
            wait.wait_send()
            wait.wait_recv()

    return pl.pallas_call(
        body,
        name="join_halves",
        in_specs=[ANY] * n,
        out_specs=[ANY] * n,
        out_shape=[jax.ShapeDtypeStruct(b.shape, b.dtype) for b in bufs],
        input_output_aliases={k: k for k in range(n)},
        scratch_shapes=[pltpu.SemaphoreType.DMA((n,)), pltpu.SemaphoreType.DMA((n,))],
        compiler_params=pltpu.CompilerParams(has_side_effects=True),
    )(*bufs)


def _allgather_small(block):
    m_per, ncol = block.shape

    def body(x_ref, out_ref, send_sems, recv_sems, local_sem):
        x, y, c = _position()
        me, sibling = (x, y, c), (x, y, 1 - c)
        chips = _other_chips(x, y)

        def rows(px, py, pc):
            return out_ref.at[4 * px + 2 * py + pc]

        def copy(k, blk, to, src=None):
            return pltpu.make_async_remote_copy(
                src_ref=rows(*blk) if src is None else src, dst_ref=rows(*blk),
                send_sem=send_sems.at[k], recv_sem=recv_sems.at[k], device_id=to, device_id_type=MESH)

        mine = pltpu.make_async_copy(x_ref, rows(*me), local_sem)
        mine.start()
        first = [copy(0, me, sibling, src=x_ref)] + [copy(1 + j, me, (*chip, c), src=x_ref) for j, chip in enumerate(chips)]
        for cp in first:
            cp.start()
        passed = [copy(4 + j, (*chip, c), sibling) for j, chip in enumerate(chips)]
        for j, chip in enumerate(chips):
            copy(1 + j, (*chip, c), me).wait_recv()
            passed[j].start()
        copy(0, sibling, me).wait_recv()
        for j, chip in enumerate(chips):
            copy(4 + j, (*chip, 1 - c), me).wait_recv()
        for cp in first + passed:
            cp.wait_send()
        mine.wait()

    return pl.pallas_call(
        body,
        name="allgather_small",
        in_specs=[pl.BlockSpec(memory_space=pltpu.VMEM)],
        out_specs=pl.BlockSpec(memory_space=pltpu.VMEM),
        out_shape=jax.ShapeDtypeStruct((8, m_per, ncol), block.dtype),
        scratch_shapes=[pltpu.SemaphoreType.DMA((7,)), pltpu.SemaphoreType.DMA((7,)), pltpu.SemaphoreType.DMA],
        compiler_params=pltpu.CompilerParams(has_side_effects=True, vmem_limit_bytes=32 * MIB),
    )(block)


SMALL_NAMES = ["norm1_g", "b_decay_f", "b_decay_b", "gla_norm_g", "gmlp_ln_g", "gmlp_ln_b", "w_spatial", "b_spatial", "norm2_g", "final_norm_g"]


def _pack_small(parts, decay_parts):
    flat = jnp.concatenate([a.reshape(-1) for a in parts])
    flat = jnp.pad(flat, (0, SMALL_ROWS * LANES - flat.shape[0])).reshape(SMALL_ROWS, LANES)
    return jnp.concatenate([flat] + [d.reshape(-1, LANES) for d in decay_parts], axis=0)


def _unpack_small(packed, like):
    out, off = [], 0
    flat = packed[:SMALL_ROWS].reshape(-1)
    for a in like:
        out.append(flat[off : off + a.size].reshape(a.shape))
        off += a.size
    return out


def kernel(x, norm1_g, w_in, w_decay_f, b_decay_f, w_decay_b, b_decay_b, gla_norm_g, gmlp_ln_g, gmlp_ln_b, w_spatial, b_spatial, w_out, norm2_g, w_gate, w_up, w_down, final_norm_g, loss_target, m_norm1_g, m_w_in, m_w_decay_f, m_b_decay_f, m_w_decay_b, m_b_decay_b, m_gla_norm_g, m_gmlp_ln_g, m_gmlp_ln_b, m_w_spatial, m_b_spatial, m_w_out, m_norm2_g, m_w_gate, m_w_up, m_w_down, m_final_norm_g, v_norm1_g, v_w_in, v_w_decay_f, v_b_decay_f, v_w_decay_b, v_b_decay_b, v_gla_norm_g, v_gmlp_ln_g, v_gmlp_ln_b, v_w_spatial, v_b_spatial, v_w_out, v_norm2_g, v_w_gate, v_w_up, v_w_down, v_final_norm_g):
    args = dict(locals())
    cx, cy, cc = lax.axis_index("x"), lax.axis_index("y"), lax.axis_index("c")
    shard = 2 * cx + cy
    xs = x[0]
    target = loss_target[0]

    big_names = ["w_in", "w_out", "w_gate", "w_up", "w_down"]
    transposed = ("w_in", "w_gate", "w_up")
    rows_of = lambda pre, k: jnp.transpose(args[pre + k][0]) if k in transposed else args[pre + k][0]
    big_shards = {k: rows_of("", k) for k in big_names}
    c_arr = cc.reshape(1).astype(jnp.int32)
    s_arr = shard.reshape(1).astype(jnp.int32)
    sc_arr = jnp.stack([shard, cc]).astype(jnp.int32)
    slots = {k: _cast_into_slot(big_shards[k], s_arr) for k in big_names}
    (w_in4,) = _gather_sync([slots["w_in"]])
    w_in_t = w_in4.reshape(PROJ_W, D_MODEL)

    dec_block = jnp.concatenate([w_decay_f[0].reshape(-1, LANES), w_decay_b[0].reshape(-1, LANES)], axis=0)
    dec_all = _allgather_small(dec_block)
    late = ["w_out", "w_gate", "w_up", "w_down"]
    g_send, g_recv, late_bufs, token_gather = _gather_start([slots[k] for k in late], (w_in4, dec_all))
    dec_all = dec_all[::2].reshape(N_SHARDS, 2, LOWRANK, KEY_W // N_SHARDS)
    wdf_full = jnp.transpose(dec_all[:, 0], (1, 0, 2)).reshape(LOWRANK, KEY_W)
    wdb_full = jnp.transpose(dec_all[:, 1], (1, 0, 2)).reshape(LOWRANK, KEY_W)
    wd_pad_f = jnp.zeros((LANES, KEY_W), F32).at[0:LOWRANK].set(wdf_full).astype(BF16)
    wd_pad_b = jnp.zeros((LANES, KEY_W), F32).at[LOWRANK : 2 * LOWRANK].set(wdb_full).astype(BF16)

    ws_bf = w_spatial[0].astype(BF16)
    wst_bf = jnp.transpose(w_spatial[0], (0, 2, 1)).astype(BF16)
    bs_col = b_spatial[0].reshape(GMLP_GROUPS, GMLP_CHUNK, 1)

    p = _inproj(xs, norm1_g, w_in_t, token_gather)
    o_f, st_f = _gla_fwd(p, wd_pad_f, b_decay_f, reverse=False)
    o_b, st_b = _gla_fwd(p, wd_pad_b, b_decay_b, reverse=True)
    late_bufs = _gather_forward(_gather_wait(g_send, g_recv, late_bufs, (o_f, o_b)))
    w_out_full, wg_t, wu_t, wd = [b.reshape(-1, D_MODEL) for b in late_bufs]
    x1, ycat = _mixer_out(xs, o_f, o_b, p, gla_norm_g, gmlp_ln_g, gmlp_ln_b, ws_bf, bs_col, w_out_full)
    gf = final_norm_g.reshape(1, D_MODEL)
    h2, gate, up, act, dx2, loss_acc, dgf = _ffn_fwd(x1, target, norm2_g, gf, wg_t, wu_t, wd)

    dgate, dup, dx1, dg2 = _ffn_bwd(dx2, gate, up, x1, norm2_g, wg_t, wu_t, wd)
    ffn_grads4 = [g.reshape(N_SHARDS, FF_SHARD, D_MODEL) for g in _ffn_wgrad(h2, dgate, dup, act, dx2)]
    e_send, e_recv, e_srcs, e_lands, token_exchange = _split_start(
        "exchange_start", ffn_grads4, _exchange_lands(ffn_grads4), _exchange_copies, len(ffn_grads4))
    do, dg, du, dvv, dwo, dgn, dlng, dlnb, dws, dbs = _mixer_bwd(
        dx1, ycat, o_f, o_b, p, gla_norm_g, gmlp_ln_g, gmlp_ln_b, ws_bf, wst_bf, bs_col, w_out_full, token_exchange)
    ffn_mine, ffn_other = _split_wait("exchange_wait", e_send, e_recv, e_srcs, e_lands, _exchange_copies, (do,))
    ffn_parts = _add_halves(ffn_mine, ffn_other, c_arr)
    ffn_payload = [pb for _, pb in ffn_parts]
    s_send, s_recv, s_parts, s_lands, token_scatter = _split_start(
        "scatter_start", ffn_payload, _scatter_lands(ffn_payload), _scatter_copies, 3 * len(ffn_payload))
    dq_f, dk_f, dv_f, dlr_f, dwdec_f, dbdec_f = _gla_bwd(p, do, st_f, wd_pad_f, b_decay_f, token_scatter, reverse=False)
    dq_b, dk_b, dv_b, dlr_b, dwdec_b, dbdec_b = _gla_bwd(p, do, st_b, wd_pad_b, b_decay_b, token_scatter, reverse=True)
    dwin_t, dp = _inproj_wgrad(xs, norm1_g, dq_f, dq_b, dk_f, dk_b, dv_f, dv_b, dg, du, dvv, dlr_f, dlr_b)
    _, ffn_recv = _split_wait("scatter_wait", s_send, s_recv, s_parts, s_lands, _scatter_copies, (dwin_t,))

    dwin4 = dwin_t.reshape(N_SHARDS, PROJ_W // N_SHARDS, D_MODEL)
    dwo4 = dwo.reshape(N_SHARDS, D_MODEL // N_SHARDS, D_MODEL)
    proj_grads4 = [dwin4, dwo4]
    proj_parts = [_add_halves([g], [r], c_arr)[0] for g, r in zip(proj_grads4, _exchange_halves(proj_grads4))]
    proj_payload = [pb for _, pb in proj_parts]
    p_send, p_recv, p_parts, p_lands, token_proj = _split_start(
        "proj_scatter_start", proj_payload, _scatter_lands(proj_payload), _scatter_copies, 3 * len(proj_payload))
    dx, dg1 = _inproj_dx(xs, dx1, norm1_g, w_in_t, dp, token_proj)
    _, proj_recv = _split_wait("proj_scatter_wait", p_send, p_recv, p_parts, p_lands, _scatter_copies, (dx,))
    parts_f32 = [pf for pf, _ in proj_parts + ffn_parts]
    bufs = [_add_partials(pf, r, sc_arr) for pf, r in zip(parts_f32, proj_recv + ffn_recv)]
    big_grads = dict(zip(big_names, _join_halves(bufs)))

    dwdec_f16 = dwdec_f[0:LOWRANK]
    dwdec_b16 = dwdec_b[LOWRANK : 2 * LOWRANK]
    shard_major = lambda a: jnp.transpose(a.reshape(LOWRANK, N_SHARDS, KEY_W // N_SHARDS), (1, 0, 2))
    small_grads = {
        "norm1_g": dg1, "b_decay_f": dbdec_f, "b_decay_b": dbdec_b, "gla_norm_g": dgn, "gmlp_ln_g": dlng, "gmlp_ln_b": dlnb,
        "w_spatial": dws, "b_spatial": dbs, "norm2_g": dg2, "final_norm_g": dgf,
    }
    g_pack = _pack_small([small_grads[k] for k in SMALL_NAMES] + [loss_acc], [shard_major(dwdec_f16), shard_major(dwdec_b16)])
    g_all = _allgather_small(g_pack)
    pack_own = lambda pre: _pack_small([args[pre + k] for k in SMALL_NAMES], [args[pre + "w_decay_f"], args[pre + "w_decay_b"]])
    sg, sd, sm, sv = _adamw_small(g_all, pack_own(""), pack_own("m_"), pack_own("v_"))

    names = ["norm1_g", "w_in", "w_decay_f", "b_decay_f", "w_decay_b", "b_decay_b", "gla_norm_g", "gmlp_ln_g", "gmlp_ln_b",
             "w_spatial", "b_spatial", "w_out", "norm2_g", "w_gate", "w_up", "w_down", "final_norm_g"]
    like = [args[k] for k in SMALL_NAMES]
    results = {"g": {}, "d": {}, "m": {}, "v": {}}
    for tag, packed in (("g", sg), ("d", sd), ("m", sm), ("v", sv)):
        for k, a in zip(SMALL_NAMES, _unpack_small(packed, like)):
            results[tag][k] = a
        results[tag]["w_decay_f"] = packed[SMALL_ROWS : SMALL_ROWS + DECAY_ROWS].reshape(w_decay_f.shape)
        results[tag]["w_decay_b"] = packed[SMALL_ROWS + DECAY_ROWS :].reshape(w_decay_b.shape)
    for k in big_names:
        g, d, mo, vo = _adamw(big_shards[k], big_grads[k], rows_of("m_", k), rows_of("v_", k))
        for tag, a in (("g", g), ("d", d), ("m", mo), ("v", vo)):
            results[tag][k] = (jnp.transpose(a) if k in transposed else a).reshape(args[k].shape)

    loss = sg[:SMALL_ROWS].reshape(-1)[sum(a.size for a in like)]
    grad_x = dx.reshape(x.shape)
    return (loss, grad_x, *[results["g"][k] for k in names], *[results["d"][k] for k in names],
            *[results["m"][k] for k in names], *[results["v"][k] for k in names])
```

```python
import functools
import math

import jax
import jax.numpy as jnp
from jax import lax
from jax.experimental import pallas as pl
from jax.experimental.pallas import tpu as pltpu

F32, BF16 = jnp.float32, jnp.bfloat16

D_MODEL = 1024
GLA_HEADS = 4
GLA_DK = 64
GLA_DV = 128
KEY_W = GLA_HEADS * GLA_DK
GLA_W = GLA_HEADS * GLA_DV
GMLP_W = 512
GMLP_GROUPS = 4
GMLP_CHUNK = 128
LOWRANK = 16
GLA_CHUNK = 64
GLA_TAU = 16.0
PROJ_W = 2592
PROJ_WP = 2688
D_FF = 2816
N_SHARDS = 4
FF_SHARD = D_FF // N_SHARDS
EPS = 1e-6
LANES = 128
TOKEN_SHAPE = (8, LANES)
MIB = 1024 * 1024

ADAM_LR = 0.001
ADAM_B1 = 0.9
ADAM_B2 = 0.999
ADAM_EPS = 1e-08
ADAM_WD = 0.01
ADAM_STEP = 10

COL_Q, COL_K = 0, 256
COL_V, COL_G, COL_U, COL_VV = 512, 1024, 1536, 2048
COL_LR = 2560
ROW_LR, ROW_UV = 1536, 1568
HALF = D_MODEL // 2

MESH = pl.DeviceIdType.MESH


def _nn(a, b):
    return jnp.dot(a, b, preferred_element_type=F32)


def _nt(a, b):
    return lax.dot_general(a, b, (((1,), (1,)), ((), ())), preferred_element_type=F32)


def _tn(a, b):
    return lax.dot_general(a, b, (((0,), (0,)), ((), ())), preferred_element_type=F32)


def _bnn(a, b):
    return jnp.einsum("nik,nkj->nij", a, b, preferred_element_type=F32)


def _bnt(a, b):
    return jnp.einsum("nik,njk->nij", a, b, preferred_element_type=F32)


def _btn(a, b):
    return jnp.einsum("nki,nkj->nij", a, b, preferred_element_type=F32)


def _resident(shape):
    zeros = (0,) * len(shape)
    return pl.BlockSpec(shape, lambda *_: zeros, pipeline_mode=pl.Buffered(1))


def _params(vmem_mib, semantics=("arbitrary",)):
    return pltpu.CompilerParams(vmem_limit_bytes=vmem_mib * MIB, dimension_semantics=semantics)


def _sigmoid(x):
    return 1.0 / (1.0 + jnp.exp(-x))


def _gelu(x):
    return 0.5 * x * (1.0 + lax.erf(x * (1.0 / math.sqrt(2.0))))


def _gelu_and_grad(x):
    cdf = 0.5 * (1.0 + lax.erf(x * (1.0 / math.sqrt(2.0))))
    return x * cdf, cdf + x * jnp.exp(-0.5 * x * x) * (1.0 / math.sqrt(2.0 * math.pi))


def _log_sigmoid(x):
    return jnp.minimum(x, 0.0) - jnp.log(1.0 + jnp.exp(-jnp.abs(x)))


def _rms_bwd(dxh, xh, r):
    return r * (dxh - xh * jnp.mean(dxh * xh, axis=-1, keepdims=True))


def _chunk_cumsum(v, row_in_chunk, reverse):
    rows = v.shape[0]
    for sh in (1, 2, 4, 8, 16, 32):
        if reverse:
            v = v + jnp.where(row_in_chunk + sh < GLA_CHUNK, pltpu.roll(v, rows - sh, axis=0), 0.0)
        else:
            v = v + jnp.where(row_in_chunk >= sh, pltpu.roll(v, sh, axis=0), 0.0)
    return v


def _inproj(x, g1, w_in_t, token):
    seq = x.shape[0]
    tm = min(seq, 512)

    def body(x_ref, g_ref, w_ref, token_ref, p_ref):
        xv = x_ref[...]
        r = lax.rsqrt(jnp.mean(xv * xv, axis=-1, keepdims=True) + EPS)
        h = (xv * r * g_ref[...]).astype(BF16)
        p_ref[:, 0:COL_U] = _nt(h, w_ref[0:ROW_LR, :])
        p_ref[:, COL_U:COL_LR] = _nt(h, w_ref[ROW_UV:PROJ_W, :])
        p_ref[:, COL_LR:PROJ_WP] = _nt(h, w_ref[ROW_LR : ROW_LR + LANES, :])

    return pl.pallas_call(
        body,
        name="inproj",
        grid=(seq // tm,),
        in_specs=[pl.BlockSpec((tm, D_MODEL), lambda i: (i, 0)), _resident((1, D_MODEL)), _resident((PROJ_W, D_MODEL)), _resident(TOKEN_SHAPE)],
        out_specs=pl.BlockSpec((tm, PROJ_WP), lambda i: (i, 0)),
        out_shape=jax.ShapeDtypeStruct((seq, PROJ_WP), F32),
        compiler_params=_params(48, ("parallel",)),
    )(x, g1, w_in_t, token)


def _gla_tile(seq):
    return min(seq, 1024)


def _gla_decay_terms(lr_bf, wd_ref, bd_ref, pair, row_in_chunk, reverse, n):
    cols = pl.ds(pair * LANES, LANES)
    pre = _nn(lr_bf, wd_ref[:, cols]) + bd_ref[:, cols]
    la = _log_sigmoid(pre) * (1.0 / GLA_TAU)
    b = _chunk_cumsum(la, row_in_chunk, reverse)
    b3 = b.reshape(n, GLA_CHUNK, LANES)
    blast = b3[:, 0:1, :] if reverse else b3[:, GLA_CHUNK - 1 : GLA_CHUNK, :]
    return pre, b3, blast


def _gla_fwd(p, wd_pad, bd, reverse):
    seq = p.shape[0]
    tg = _gla_tile(seq)
    nt = seq // tg
    n = tg // GLA_CHUNK
    scale = GLA_DK**-0.5

    def tile(i):
        return nt - 1 - i if reverse else i

    def body(q_ref, k_ref, v_ref, lr_ref, wd_ref, bd_ref, o_ref, st_ref, carry):
        @pl.when(pl.program_id(0) == 0)
        def _():
            carry[...] = jnp.zeros_like(carry)

        lr_bf = lr_ref[...].astype(BF16)
        states = [carry[h] for h in range(GLA_HEADS)]
        row_in_chunk = lax.broadcasted_iota(jnp.int32, (tg, LANES), 0) % GLA_CHUNK
        lane_head = lax.broadcasted_iota(jnp.int32, (1, LANES), 1) // GLA_DK
        tt = lax.broadcasted_iota(jnp.int32, (GLA_CHUNK, GLA_CHUNK), 0)
        ss = lax.broadcasted_iota(jnp.int32, (GLA_CHUNK, GLA_CHUNK), 1)
        causal = (tt <= ss) if reverse else (tt >= ss)
        order = range(n - 1, -1, -1) if reverse else range(n)
        heads = range(GLA_HEADS)
        qdh, kds, vhs, decs, sc_raw, dst = {}, {}, {}, {}, {}, {}
        for pair in range(2):
            cols = pl.ds(pair * LANES, LANES)
            _, b3, blast = _gla_decay_terms(lr_bf, wd_ref, bd_ref, pair, row_in_chunk, reverse, n)
            q3 = q_ref[:, cols].reshape(n, GLA_CHUNK, LANES) * scale
            k3 = k_ref[:, cols].reshape(n, GLA_CHUNK, LANES)
            qd = q3 * jnp.exp(b3)
            kd = (k3 * jnp.exp(-b3)).astype(BF16)
            kte = k3 * jnp.exp(blast - b3)
            dec = jnp.exp(blast)
            for hh in range(2):
                h = 2 * pair + hh
                m = (lane_head == hh).astype(F32)
                qdh[h], kds[h], decs[h] = (qd * m).astype(BF16), kd, dec
                vhs[h] = v_ref[:, pl.ds(h * GLA_DV, GLA_DV)].reshape(n, GLA_CHUNK, GLA_DV).astype(BF16)
                sc_raw[h] = _bnt(qdh[h], kd)
                dst[h] = _btn(vhs[h], (kte * m).astype(BF16))
        o_intra, befores = {}, {}
        for h in heads:
            o_intra[h] = _bnn(jnp.where(causal, sc_raw[h], 0.0).astype(BF16), vhs[h])
            st, before = states[h], [None] * n
            for j in order:
                before[j] = st
                st = st * decs[h][j] + dst[h][j]
            states[h] = st
            befores[h] = jnp.stack(before)
        outs = {h: (o_intra[h] + _bnt(qdh[h], befores[h].astype(BF16))).reshape(tg, GLA_DV) for h in heads}
        for h in range(GLA_HEADS):
            o_ref[:, pl.ds(h * GLA_DV, GLA_DV)] = outs[h]
            st_ref[:, h] = befores[h]
            carry[h] = states[h]

    nchunks = seq // GLA_CHUNK
    return pl.pallas_call(
        body,
        name="gla_fwd_rev" if reverse else "gla_fwd",
        grid=(nt,),
        in_specs=[
            pl.BlockSpec((tg, KEY_W), lambda i: (tile(i), COL_Q // KEY_W)),
            pl.BlockSpec((tg, KEY_W), lambda i: (tile(i), COL_K // KEY_W)),
            pl.BlockSpec((tg, GLA_W), lambda i: (tile(i), COL_V // GLA_W)),
            pl.BlockSpec((tg, LANES), lambda i: (tile(i), COL_LR // LANES)),
            _resident((LANES, KEY_W)),
            _resident((1, KEY_W)),
        ],
        out_specs=[
            pl.BlockSpec((tg, GLA_W), lambda i: (tile(i), 0)),
            pl.BlockSpec((n, GLA_HEADS, GLA_DV, LANES), lambda i: (tile(i), 0, 0, 0)),
        ],
        out_shape=[
            jax.ShapeDtypeStruct((seq, GLA_W), F32),
            jax.ShapeDtypeStruct((nchunks, GLA_HEADS, GLA_DV, LANES), F32),
        ],
        scratch_shapes=[pltpu.VMEM((GLA_HEADS, GLA_DV, LANES), F32)],
        compiler_params=_params(48),
    )(p, p, p, p, wd_pad, bd)


def _mixer_out(x, o_f, o_b, p, gn, lng, lnb, ws_bf, bs_col, w_out):
    seq = x.shape[0]
    tm = min(seq, 512)

    def body(x_ref, of_ref, ob_ref, g_ref, u_ref, vv_ref, gn_ref, lng_ref, lnb_ref, ws_ref, bs_ref, wo_ref, x1_ref, yc_ref, vn_sc):
        for h in range(GLA_HEADS):
            cols = pl.ds(h * GLA_DV, GLA_DV)
            oh = of_ref[:, cols] + ob_ref[:, cols]
            on = oh * lax.rsqrt(jnp.mean(oh * oh, axis=-1, keepdims=True) + EPS)
            gh = g_ref[:, cols]
            yc_ref[:, cols] = (on * gn_ref[:, cols] * (gh * _sigmoid(gh))).astype(BF16)
        zv = _gelu(vv_ref[...])
        xc = zv - jnp.mean(zv, axis=-1, keepdims=True)
        vhat = xc * lax.rsqrt(jnp.mean(xc * xc, axis=-1, keepdims=True) + EPS)
        vn_sc[...] = (vhat * lng_ref[...] + lnb_ref[...]).astype(BF16)
        for c in range(tm // GMLP_CHUNK):
            rows = pl.ds(c * GMLP_CHUNK, GMLP_CHUNK)
            for g in range(GMLP_GROUPS):
                cols = pl.ds(g * LANES, LANES)
                s = _nn(ws_ref[g], vn_sc[rows, cols]) + bs_ref[g]
                yc_ref[rows, pl.ds(GLA_W + g * LANES, LANES)] = (_gelu(u_ref[rows, cols]) * s).astype(BF16)
        x1_ref[...] = x_ref[...] + _nn(yc_ref[...], wo_ref[...])

    row = lambda w: pl.BlockSpec((tm, w), lambda i: (i, 0))
    pcol = lambda col: pl.BlockSpec((tm, GLA_W), lambda i: (i, col // GLA_W))
    return pl.pallas_call(
        body,
        name="mixer_out",
        grid=(seq // tm,),
        in_specs=[
            row(D_MODEL), row(GLA_W), row(GLA_W), pcol(COL_G), pcol(COL_U), pcol(COL_VV),
            _resident((1, GLA_W)), _resident((1, GMLP_W)), _resident((1, GMLP_W)),
            _resident((GMLP_GROUPS, GMLP_CHUNK, GMLP_CHUNK)), _resident((GMLP_GROUPS, GMLP_CHUNK, 1)),
            _resident((D_MODEL, D_MODEL)),
        ],
        out_specs=[row(D_MODEL), row(D_MODEL)],
        out_shape=[jax.ShapeDtypeStruct((seq, D_MODEL), F32), jax.ShapeDtypeStruct((seq, D_MODEL), BF16)],
        scratch_shapes=[pltpu.VMEM((tm, GMLP_W), BF16)],
        compiler_params=_params(48, ("parallel",)),
    )(x, o_f, o_b, p, p, p, gn, lng, lnb, ws_bf, bs_col, w_out)


def _ffn_fwd(x1, target, g2, gf, wg_t, wu_t, wd):
    seq = x1.shape[0]
    tm = min(seq, 256)

    def body(x1_ref, t_ref, g2_ref, gf_ref, wg_ref, wu_ref, wd_ref, h2_ref, gate_ref, up_ref, act_ref, dx2_ref, loss_ref, dgf_ref):
        @pl.when(pl.program_id(0) == 0)
        def _():
            loss_ref[...] = jnp.zeros_like(loss_ref)
            dgf_ref[...] = jnp.zeros_like(dgf_ref)

        x1v = x1_ref[...]
        h2 = (x1v * lax.rsqrt(jnp.mean(x1v * x1v, axis=-1, keepdims=True) + EPS) * g2_ref[...]).astype(BF16)
        h2_ref[...] = h2
        gate = _nt(h2, wg_ref[...])
        up = _nt(h2, wu_ref[...])
        act = (gate * _sigmoid(gate) * up).astype(BF16)
        gate_ref[...] = gate
        up_ref[...] = up
        act_ref[...] = act
        x2 = x1v + _nn(act, wd_ref[...])
        rf = lax.rsqrt(jnp.mean(x2 * x2, axis=-1, keepdims=True) + EPS)
        xh = x2 * rf
        err = xh * gf_ref[...] - t_ref[...]
        loss_ref[...] += 0.5 * jnp.sum(jnp.mean(err * err, axis=-1, keepdims=True))
        dy = err * (1.0 / D_MODEL)
        dgf_ref[...] += jnp.sum(dy * xh, axis=0, keepdims=True)
        dx2_ref[...] = _rms_bwd(dy * gf_ref[...], xh, rf)

    row = lambda w: pl.BlockSpec((tm, w), lambda i: (i, 0))
    weight = _resident((D_FF, D_MODEL))
    return pl.pallas_call(
        body,
        name="ffn_fwd",
        grid=(seq // tm,),
        in_specs=[row(D_MODEL), row(D_MODEL), _resident((1, D_MODEL)), _resident((1, D_MODEL)), weight, weight, weight],
        out_specs=[row(D_MODEL), row(D_FF), row(D_FF), row(D_FF), row(D_MODEL),
                   pl.BlockSpec((1, LANES), lambda i: (0, 0)), pl.BlockSpec((1, D_MODEL), lambda i: (0, 0))],
        out_shape=[
            jax.ShapeDtypeStruct((seq, D_MODEL), BF16),
            jax.ShapeDtypeStruct((seq, D_FF), F32),
            jax.ShapeDtypeStruct((seq, D_FF), F32),
            jax.ShapeDtypeStruct((seq, D_FF), BF16),
            jax.ShapeDtypeStruct((seq, D_MODEL), F32),
            jax.ShapeDtypeStruct((1, LANES), F32),
            jax.ShapeDtypeStruct((1, D_MODEL), F32),
        ],
        compiler_params=_params(56),
    )(x1, target, g2, gf, wg_t, wu_t, wd)


def _ffn_bwd(dx2, gate, up, x1, g2, wg_t, wu_t, wd):
    seq = x1.shape[0]
    tm = min(seq, 256)

    def body(dx2_ref, gate_ref, up_ref, x1_ref, g2_ref, wg_ref, wu_ref, wd_ref, dgate_ref, dup_ref, dx1_ref, dg2_ref):
        @pl.when(pl.program_id(0) == 0)
        def _():
            dg2_ref[...] = jnp.zeros_like(dg2_ref)

        dx2v = dx2_ref[...]
        dact = _nt(dx2v.astype(BF16), wd_ref[...])
        gate = gate_ref[...]
        sg = _sigmoid(gate)
        dgate = (dact * up_ref[...] * (sg * (1.0 + gate * (1.0 - sg)))).astype(BF16)
        dup = (dact * (gate * sg)).astype(BF16)
        dgate_ref[...] = dgate
        dup_ref[...] = dup
        dh2 = _nn(dgate, wg_ref[...]) + _nn(dup, wu_ref[...])
        x1v = x1_ref[...]
        r2 = lax.rsqrt(jnp.mean(x1v * x1v, axis=-1, keepdims=True) + EPS)
        xh = x1v * r2
        dg2_ref[...] += jnp.sum(dh2 * xh, axis=0, keepdims=True)
        dx1_ref[...] = dx2v + _rms_bwd(dh2 * g2_ref[...], xh, r2)

    row = lambda w: pl.BlockSpec((tm, w), lambda i: (i, 0))
    weight = _resident((D_FF, D_MODEL))
    return pl.pallas_call(
        body,
        name="ffn_bwd",
        grid=(seq // tm,),
        in_specs=[row(D_MODEL), row(D_FF), row(D_FF), row(D_MODEL), _resident((1, D_MODEL)), weight, weight, weight],
        out_specs=[row(D_FF), row(D_FF), row(D_MODEL), pl.BlockSpec((1, D_MODEL), lambda i: (0, 0))],
        out_shape=[
            jax.ShapeDtypeStruct((seq, D_FF), BF16),
            jax.ShapeDtypeStruct((seq, D_FF), BF16),
            jax.ShapeDtypeStruct((seq, D_MODEL), F32),
            jax.ShapeDtypeStruct((1, D_MODEL), F32),
        ],
        compiler_params=_params(56),
    )(dx2, gate, up, x1, g2, wg_t, wu_t, wd)


WGRAD_ROWS = D_FF // 2


def _ffn_wgrad(h2, dgate, dup, act, dx2):
    seq = h2.shape[0]
    tm = min(seq, 512)

    def body(h2_ref, dgate_ref, dup_ref, act_ref, dx2_ref, dwg_ref, dwu_ref, dwd_ref):
        @pl.when(pl.program_id(1) == 0)
        def _():
            dwg_ref[...] = jnp.zeros_like(dwg_ref)
            dwu_ref[...] = jnp.zeros_like(dwu_ref)
            dwd_ref[...] = jnp.zeros_like(dwd_ref)

        h2v = h2_ref[...]
        dwg_ref[...] += _tn(dgate_ref[...], h2v)
        dwu_ref[...] += _tn(dup_ref[...], h2v)
        dwd_ref[...] += _tn(act_ref[...], dx2_ref[...].astype(BF16))

    ff = pl.BlockSpec((tm, WGRAD_ROWS), lambda j, i: (i, j))
    row = pl.BlockSpec((tm, D_MODEL), lambda j, i: (i, 0))
    out = pl.BlockSpec((WGRAD_ROWS, D_MODEL), lambda j, i: (j, 0))
    return pl.pallas_call(
        body,
        name="ffn_wgrad",
        grid=(D_FF // WGRAD_ROWS, seq // tm),
        in_specs=[row, ff, ff, ff, row],
        out_specs=[out, out, out],
        out_shape=[jax.ShapeDtypeStruct((D_FF, D_MODEL), F32)] * 3,
        compiler_params=_params(56, ("parallel", "arbitrary")),
    )(h2, dgate, dup, act, dx2)


def _mixer_bwd(dx1, ycat, o_f, o_b, p, gn, lng, lnb, ws_bf, wst_bf, bs_col, w_out, token):
    seq = dx1.shape[0]
    tm = min(seq, 512)
    nsteps = seq // tm

    def body(dx1_ref, yc_ref, of_ref, ob_ref, g_ref, u_ref, vv_ref, gn_ref, lng_ref, lnb_ref, ws_ref, wst_ref, bs_ref, wo_ref, token_ref,
             do_ref, dg_ref, du_ref, dvv_ref, dwo_ref, dgn_ref, dlng_ref, dlnb_ref, dws_ref, dbs_ref, vn_sc, dvn_sc, dbs_acc):
        step = pl.program_id(0)

        @pl.when(step == 0)
        def _():
            for r in (dwo_ref, dgn_ref, dlng_ref, dlnb_ref, dws_ref, dbs_acc):
                r[...] = jnp.zeros_like(r)

        dx1b = dx1_ref[...].astype(BF16)
        dyc = _nt(dx1b, wo_ref[...])
        dwo_ref[...] += _tn(yc_ref[...], dx1b)
        for h in range(GLA_HEADS):
            cols = pl.ds(h * GLA_DV, GLA_DV)
            dya = dyc[:, h * GLA_DV : (h + 1) * GLA_DV]
            oh = of_ref[:, cols] + ob_ref[:, cols]
            rn = lax.rsqrt(jnp.mean(oh * oh, axis=-1, keepdims=True) + EPS)
            on = oh * rn
            gh = g_ref[:, cols]
            sg = _sigmoid(gh)
            sil = gh * sg
            gnh = gn_ref[:, cols]
            dgn_ref[:, cols] += jnp.sum(dya * on * sil, axis=0, keepdims=True)
            dg_ref[:, cols] = dya * on * gnh * (sg * (1.0 + gh * (1.0 - sg)))
            do_ref[:, cols] = _rms_bwd(dya * gnh * sil, on, rn)
        vv = vv_ref[...]
        zv, zv_grad = _gelu_and_grad(vv)
        xc = zv - jnp.mean(zv, axis=-1, keepdims=True)
        rstd = lax.rsqrt(jnp.mean(xc * xc, axis=-1, keepdims=True) + EPS)
        vhat = xc * rstd
        vn_sc[...] = (vhat * lng_ref[...] + lnb_ref[...]).astype(BF16)
        for c in range(tm // GMLP_CHUNK):
            rows = pl.ds(c * GMLP_CHUNK, GMLP_CHUNK)
            for g in range(GMLP_GROUPS):
                cols = pl.ds(g * LANES, LANES)
                vn = vn_sc[rows, cols]
                s = _nn(ws_ref[g], vn) + bs_ref[g]
                dyb = dyc[c * GMLP_CHUNK : (c + 1) * GMLP_CHUNK, GLA_W + g * LANES : GLA_W + (g + 1) * LANES]
                zu, zu_grad = _gelu_and_grad(u_ref[rows, cols])
                du_ref[rows, cols] = dyb * s * zu_grad
                ds = dyb * zu
                dbs_acc[g] += ds
                dsb = ds.astype(BF16)
                dws_ref[g] += _nt(dsb, vn)
                dvn_sc[rows, cols] = _nn(wst_ref[g], dsb)
        dvn = dvn_sc[...]
        dlng_ref[...] += jnp.sum(dvn * vhat, axis=0, keepdims=True)
        dlnb_ref[...] += jnp.sum(dvn, axis=0, keepdims=True)
        dvh = dvn * lng_ref[...]
        dzv = rstd * (dvh - jnp.mean(dvh, axis=-1, keepdims=True) - vhat * jnp.mean(dvh * vhat, axis=-1, keepdims=True))
        dvv_ref[...] = dzv * zv_grad

        @pl.when(step == nsteps - 1)
        def _():
            dbs_ref[...] = jnp.sum(dbs_acc[...], axis=-1, keepdims=True)

    row = lambda w: pl.BlockSpec((tm, w), lambda i: (i, 0))
    pcol = lambda col: pl.BlockSpec((tm, GLA_W), lambda i: (i, col // GLA_W))
    const = lambda shape: pl.BlockSpec(shape, lambda i: (0,) * len(shape))
    return pl.pallas_call(
        body,
        name="mixer_bwd",
        grid=(nsteps,),
        in_specs=[
            row(D_MODEL), row(D_MODEL), row(GLA_W), row(GLA_W), pcol(COL_G), pcol(COL_U), pcol(COL_VV),
            _resident((1, GLA_W)), _resident((1, GMLP_W)), _resident((1, GMLP_W)),
            _resident((GMLP_GROUPS, GMLP_CHUNK, GMLP_CHUNK)), _resident((GMLP_GROUPS, GMLP_CHUNK, GMLP_CHUNK)),
            _resident((GMLP_GROUPS, GMLP_CHUNK, 1)), _resident((D_MODEL, D_MODEL)), _resident(TOKEN_SHAPE),
        ],
        out_specs=[
            row(GLA_W), row(GLA_W), row(GMLP_W), row(GMLP_W), const((D_MODEL, D_MODEL)),
            const((1, GLA_W)), const((1, GMLP_W)), const((1, GMLP_W)),
            const((GMLP_GROUPS, GMLP_CHUNK, GMLP_CHUNK)), const((GMLP_GROUPS, GMLP_CHUNK, 1)),
        ],
        out_shape=[
            jax.ShapeDtypeStruct((seq, GLA_W), F32), jax.ShapeDtypeStruct((seq, GLA_W), F32),
            jax.ShapeDtypeStruct((seq, GMLP_W), F32), jax.ShapeDtypeStruct((seq, GMLP_W), F32),
            jax.ShapeDtypeStruct((D_MODEL, D_MODEL), F32),
            jax.ShapeDtypeStruct((1, GLA_W), F32), jax.ShapeDtypeStruct((1, GMLP_W), F32), jax.ShapeDtypeStruct((1, GMLP_W), F32),
            jax.ShapeDtypeStruct((GMLP_GROUPS, GMLP_CHUNK, GMLP_CHUNK), F32), jax.ShapeDtypeStruct((GMLP_GROUPS, GMLP_CHUNK, 1), F32),
        ],
        scratch_shapes=[pltpu.VMEM((tm, GMLP_W), BF16), pltpu.VMEM((tm, GMLP_W), F32), pltpu.VMEM((GMLP_GROUPS, GMLP_CHUNK, GMLP_CHUNK), F32)],
        compiler_params=_params(56),
    )(dx1, ycat, o_f, o_b, p, p, p, gn, lng, lnb, ws_bf, wst_bf, bs_col, w_out, token)


def _gla_bwd(p, do, st, wd_pad, bd, token, reverse):
    seq = p.shape[0]
    tg = _gla_tile(seq)
    nt = seq // tg
    n = tg // GLA_CHUNK
    scale = GLA_DK**-0.5

    def tile(i):
        return i if reverse else nt - 1 - i

    def body(q_ref, k_ref, v_ref, lr_ref, do_ref, st_ref, wd_ref, bd_ref, token_ref, dq_ref, dk_ref, dv_ref, dlr_ref, dwd_ref, dbd_ref, carry):
        @pl.when(pl.program_id(0) == 0)
        def _():
            carry[...] = jnp.zeros_like(carry)
            dwd_ref[...] = jnp.zeros_like(dwd_ref)
            dbd_ref[...] = jnp.zeros_like(dbd_ref)

        lr_bf = lr_ref[...].astype(BF16)
        carries = [carry[h] for h in range(GLA_HEADS)]
        row_in_chunk = lax.broadcasted_iota(jnp.int32, (tg, LANES), 0) % GLA_CHUNK
        lane_head = lax.broadcasted_iota(jnp.int32, (1, LANES), 1) // GLA_DK
        tt = lax.broadcasted_iota(jnp.int32, (GLA_CHUNK, GLA_CHUNK), 0)
        ss = lax.broadcasted_iota(jnp.int32, (GLA_CHUNK, GLA_CHUNK), 1)
        causal = (tt <= ss) if reverse else (tt >= ss)
        causal_t = (tt >= ss) if reverse else (tt <= ss)
        order = range(n) if reverse else range(n - 1, -1, -1)
        dlr = jnp.zeros((tg, LANES), F32)
        heads = range(GLA_HEADS)
        pv, masks, qdh, kteh, vhs, dohs, stbs = {}, {}, {}, {}, {}, {}, {}
        sc_t, dp, dp_t, acc = {}, {}, {}, {}
        for pair in range(2):
            cols = pl.ds(pair * LANES, LANES)
            pre, b3, blast = _gla_decay_terms(lr_bf, wd_ref, bd_ref, pair, row_in_chunk, reverse, n)
            q3 = q_ref[:, cols].reshape(n, GLA_CHUNK, LANES) * scale
            k3 = k_ref[:, cols].reshape(n, GLA_CHUNK, LANES)
            eb = jnp.exp(b3)
            emb = jnp.exp(-b3)
            ekte = jnp.exp(blast - b3)
            kdf = k3 * emb
            pv[pair] = dict(pre=pre, eb=eb, emb=emb, ekte=ekte, qd=q3 * eb, kdf=kdf, kd=kdf.astype(BF16), kte=k3 * ekte, dec=jnp.exp(blast))
            for hh in range(2):
                h = 2 * pair + hh
                vcols = pl.ds(h * GLA_DV, GLA_DV)
                masks[h] = (lane_head == hh).astype(F32)
                qdh[h] = (pv[pair]["qd"] * masks[h]).astype(BF16)
                kteh[h] = (pv[pair]["kte"] * masks[h]).astype(BF16)
                vhs[h] = v_ref[:, vcols].reshape(n, GLA_CHUNK, GLA_DV).astype(BF16)
                dohs[h] = do_ref[:, vcols].reshape(n, GLA_CHUNK, GLA_DV).astype(BF16)
                stbs[h] = st_ref[:, h]
                sc_t[h] = _bnt(pv[pair]["kd"], qdh[h])
                dp[h] = _bnt(dohs[h], vhs[h])
                dp_t[h] = _bnt(vhs[h], dohs[h])
                acc[h] = _btn(dohs[h], qdh[h])
        dsa = {}
        for h in heads:
            sc_t[h] = jnp.where(causal_t, sc_t[h], 0.0).astype(BF16)
            dp[h] = jnp.where(causal, dp[h], 0.0).astype(BF16)
            dp_t[h] = jnp.where(causal_t, dp_t[h], 0.0).astype(BF16)
            dec = pv[h // 2]["dec"]
            c, after = carries[h], [None] * n
            for j in order:
                after[j] = c
                c = acc[h][j] + dec[j] * c
            carries[h] = c
            dsa[h] = jnp.stack(after)
        dvs, dqs, dks, dwds, dbds = [], [], [], [], []
        for pair in range(2):
            cols = pl.ds(pair * LANES, LANES)
            v = pv[pair]
            dqd = jnp.zeros((n, GLA_CHUNK, LANES), F32)
            dkd = jnp.zeros((n, GLA_CHUNK, LANES), F32)
            dkte = jnp.zeros((n, GLA_CHUNK, LANES), F32)
            ddec = jnp.zeros((n, 1, LANES), F32)
            for h in (2 * pair, 2 * pair + 1):
                dsa_bf = dsa[h].astype(BF16)
                dqd = dqd + (_bnn(dp[h], v["kd"]) * masks[h] + _bnn(dohs[h], stbs[h].astype(BF16)))
                dkd = dkd + _bnn(dp_t[h], qdh[h])
                dkte = dkte + _bnn(vhs[h], dsa_bf)
                ddec = ddec + jnp.sum(dsa[h] * stbs[h], axis=1, keepdims=True)
                dvs.append((_bnn(sc_t[h], dohs[h]) + _bnt(kteh[h], dsa_bf)).reshape(tg, GLA_DV))
            dqs.append((dqd * (scale * v["eb"])).reshape(tg, LANES))
            dks.append((dkd * v["emb"] + dkte * v["ekte"]).reshape(tg, LANES))
            db = dqd * v["qd"] - dkd * v["kdf"] - dkte * v["kte"]
            dblast = jnp.sum(dkte * v["kte"], axis=1, keepdims=True) + ddec * v["dec"]
            dla = _chunk_cumsum(db.reshape(tg, LANES), row_in_chunk, not reverse) + jnp.broadcast_to(dblast, (n, GLA_CHUNK, LANES)).reshape(tg, LANES)
            dpre = (dla * (1.0 / GLA_TAU) * _sigmoid(-v["pre"]))
            dpre_bf = dpre.astype(BF16)
            dlr = dlr + _nt(dpre_bf, wd_ref[:, cols])
            dwds.append(_tn(lr_bf, dpre_bf))
            dbds.append(jnp.sum(dpre, axis=0, keepdims=True))
        dlr_ref[...] = dlr
        for pair in range(2):
            cols = pl.ds(pair * LANES, LANES)
            dq_ref[:, cols] = dqs[pair]
            dk_ref[:, cols] = dks[pair]
            dwd_ref[:, cols] += dwds[pair]
            dbd_ref[:, cols] += dbds[pair]
        for h in range(GLA_HEADS):
            dv_ref[:, pl.ds(h * GLA_DV, GLA_DV)] = dvs[h]
            carry[h] = carries[h]

    return pl.pallas_call(
        body,
        name="gla_bwd_rev" if reverse else "gla_bwd",
        grid=(nt,),
        in_specs=[
            pl.BlockSpec((tg, KEY_W), lambda i: (tile(i), COL_Q // KEY_W)),
            pl.BlockSpec((tg, KEY_W), lambda i: (tile(i), COL_K // KEY_W)),
            pl.BlockSpec((tg, GLA_W), lambda i: (tile(i), COL_V // GLA_W)),
            pl.BlockSpec((tg, LANES), lambda i: (tile(i), COL_LR // LANES)),
            pl.BlockSpec((tg, GLA_W), lambda i: (tile(i), 0)),
            pl.BlockSpec((n, GLA_HEADS, GLA_DV, LANES), lambda i: (tile(i), 0, 0, 0)),
            _resident((LANES, KEY_W)),
            _resident((1, KEY_W)),
            _resident(TOKEN_SHAPE),
        ],
        out_specs=[
            pl.BlockSpec((tg, KEY_W), lambda i: (tile(i), 0)),
            pl.BlockSpec((tg, KEY_W), lambda i: (tile(i), 0)),
            pl.BlockSpec((tg, GLA_W), lambda i: (tile(i), 0)),
            pl.BlockSpec((tg, LANES), lambda i: (tile(i), 0)),
            pl.BlockSpec((LANES, KEY_W), lambda i: (0, 0)),
            pl.BlockSpec((1, KEY_W), lambda i: (0, 0)),
        ],
        out_shape=[
            jax.ShapeDtypeStruct((seq, KEY_W), F32), jax.ShapeDtypeStruct((seq, KEY_W), F32),
            jax.ShapeDtypeStruct((seq, GLA_W), F32), jax.ShapeDtypeStruct((seq, LANES), F32),
            jax.ShapeDtypeStruct((LANES, KEY_W), F32), jax.ShapeDtypeStruct((1, KEY_W), F32),
        ],
        scratch_shapes=[pltpu.VMEM((GLA_HEADS, GLA_DV, LANES), F32)],
        compiler_params=_params(48),
    )(p, p, p, p, do, st, wd_pad, bd, token)


def _inproj_wgrad(x, g1, dq_f, dq_b, dk_f, dk_b, dv_f, dv_b, dg, du, dvv, dlr_f, dlr_b):
    seq = x.shape[0]
    tm = min(seq, 512)

    def body(x_ref, g1_ref, dqf, dqb, dkf, dkb, dvf, dvb, dg_ref, du_ref, dvv_ref, dlrf, dlrb, dw_ref, dp_ref):
        @pl.when(pl.program_id(0) == 0)
        def _():
            dw_ref[...] = jnp.zeros_like(dw_ref)

        dp_ref[:, COL_Q : COL_Q + KEY_W] = (dqf[...] + dqb[...]).astype(BF16)
        dp_ref[:, COL_K : COL_K + KEY_W] = (dkf[...] + dkb[...]).astype(BF16)
        dp_ref[:, COL_V : COL_V + GLA_W] = (dvf[...] + dvb[...]).astype(BF16)
        dp_ref[:, COL_G : COL_G + GLA_W] = dg_ref[...].astype(BF16)
        dp_ref[:, COL_U : COL_U + GMLP_W] = du_ref[...].astype(BF16)
        dp_ref[:, COL_VV : COL_VV + GMLP_W] = dvv_ref[...].astype(BF16)
        dp_ref[:, COL_LR : COL_LR + LANES] = (dlrf[...] + dlrb[...]).astype(BF16)
        xv = x_ref[...]
        h = (xv * lax.rsqrt(jnp.mean(xv * xv, axis=-1, keepdims=True) + EPS) * g1_ref[...]).astype(BF16)
        dw_ref[0:ROW_LR, :] += _tn(dp_ref[:, 0:COL_U], h)
        dw_ref[ROW_UV:PROJ_W, :] += _tn(dp_ref[:, COL_U:COL_LR], h)
        dw_ref[ROW_LR:ROW_UV, :] += _tn(dp_ref[:, COL_LR:PROJ_WP], h)[0 : ROW_UV - ROW_LR]

    row = lambda w: pl.BlockSpec((tm, w), lambda i: (i, 0))
    return pl.pallas_call(
        body,
        name="inproj_wgrad",
        grid=(seq // tm,),
        in_specs=[
            row(D_MODEL), _resident((1, D_MODEL)),
            row(KEY_W), row(KEY_W), row(KEY_W), row(KEY_W), row(GLA_W), row(GLA_W),
            row(GLA_W), row(GMLP_W), row(GMLP_W), row(LANES), row(LANES),
        ],
        out_specs=[pl.BlockSpec((PROJ_W, D_MODEL), lambda i: (0, 0)), row(PROJ_WP)],
        out_shape=[jax.ShapeDtypeStruct((PROJ_W, D_MODEL), F32), jax.ShapeDtypeStruct((seq, PROJ_WP), BF16)],
        compiler_params=_params(56),
    )(x, g1, dq_f, dq_b, dk_f, dk_b, dv_f, dv_b, dg, du, dvv, dlr_f, dlr_b)


def _inproj_dx(x, dx1, g1, w_in_t, dp, token):
    seq = x.shape[0]
    tm = min(seq, 512)

    def body(x_ref, dx1_ref, g1_ref, w_ref, dp_ref, token_ref, dx_ref, dg1_ref):
        @pl.when(pl.program_id(0) == 0)
        def _():
            dg1_ref[...] = jnp.zeros_like(dg1_ref)

        xv = x_ref[...]
        r1 = lax.rsqrt(jnp.mean(xv * xv, axis=-1, keepdims=True) + EPS)
        xh = xv * r1
        dh = (_nn(dp_ref[:, 0:COL_U], w_ref[0:ROW_LR, :]) + _nn(dp_ref[:, COL_U:COL_LR], w_ref[ROW_UV:PROJ_W, :])
              + _nn(dp_ref[:, COL_LR:PROJ_WP], w_ref[ROW_LR : ROW_LR + LANES, :]))
        dg1_ref[...] += jnp.sum(dh * xh, axis=0, keepdims=True)
        dx_ref[...] = dx1_ref[...] + _rms_bwd(dh * g1_ref[...], xh, r1)

    row = lambda w: pl.BlockSpec((tm, w), lambda i: (i, 0))
    return pl.pallas_call(
        body,
        name="inproj_dx",
        grid=(seq // tm,),
        in_specs=[row(D_MODEL), row(D_MODEL), _resident((1, D_MODEL)), _resident((PROJ_W, D_MODEL)), row(PROJ_WP), _resident(TOKEN_SHAPE)],
        out_specs=[row(D_MODEL), pl.BlockSpec((1, D_MODEL), lambda i: (0, 0))],
        out_shape=[jax.ShapeDtypeStruct((seq, D_MODEL), F32), jax.ShapeDtypeStruct((1, D_MODEL), F32)],
        compiler_params=_params(48),
    )(x, dx1, g1, w_in_t, dp, token)


def _in_hbm(a):
    return pltpu.with_memory_space_constraint(a, pltpu.HBM)


def _row_tile(rows, multiple=8):
    for t in range(min(rows, 512), 0, -1):
        if rows % t == 0 and t % multiple == 0:
            return t
    return rows


def _cast_into_slot(w, shard):
    rows, cols = w.shape
    tr = _row_tile(rows, 16)

    def body(s_ref, w_ref, o_ref):
        o_ref[...] = w_ref[...].astype(BF16)

    return pl.pallas_call(
        body,
        name="cast_into_slot",
        grid_spec=pltpu.PrefetchScalarGridSpec(
            num_scalar_prefetch=1,
            grid=(rows // tr,),
            in_specs=[pl.BlockSpec((tr, cols), lambda i, s_ref: (i, 0))],
            out_specs=pl.BlockSpec((None, tr, cols), lambda i, s_ref: (s_ref[0], i, 0)),
        ),
        out_shape=pltpu.HBM((N_SHARDS, rows, cols), BF16),
        compiler_params=_params(32, ("parallel",)),
    )(shard, _in_hbm(w))


def _add_halves(grads4, recvs, c):
    n = len(grads4)
    _, rows, _ = grads4[0].shape
    tr = _row_tile(rows, 16)

    def body(c_ref, *refs):
        for k in range(n):
            total = refs[k][...] + refs[n + k][...]
            refs[2 * n + k][...] = total
            refs[3 * n + k][...] = total.astype(BF16)

    out = pl.BlockSpec((None, tr, HALF), lambda s, i, c_ref: (s, i, 0))
    mine = pl.BlockSpec((None, tr, HALF), lambda s, i, c_ref: (s, i, c_ref[0]))
    outs = pl.pallas_call(
        body,
        name="add_halves",
        grid_spec=pltpu.PrefetchScalarGridSpec(
            num_scalar_prefetch=1,
            grid=(N_SHARDS, rows // tr),
            in_specs=[mine] * n + [out] * n,
            out_specs=[out] * (2 * n),
        ),
        out_shape=[pltpu.HBM((N_SHARDS, rows, HALF), F32)] * n + [pltpu.HBM((N_SHARDS, rows, HALF), BF16)] * n,
        compiler_params=_params(48, ("parallel", "parallel")),
    )(c, *[_in_hbm(a) for a in list(grads4) + list(recvs)])
    return list(zip(outs[:n], outs[n:]))


def _add_partials(part4, recv3, shard_core):
    _, rows, _ = part4.shape
    tr = _row_tile(rows, 16)

    def body(sc_ref, p_ref, r_ref, o_ref):
        o_ref[...] = ((p_ref[...] + r_ref[0].astype(F32)) + r_ref[1].astype(F32)) + r_ref[2].astype(F32)

    return pl.pallas_call(
        body,
        name="add_partials",
        grid_spec=pltpu.PrefetchScalarGridSpec(
            num_scalar_prefetch=1,
            grid=(rows // tr,),
            in_specs=[
                pl.BlockSpec((None, tr, HALF), lambda i, sc_ref: (sc_ref[0], i, 0)),
                pl.BlockSpec((3, tr, HALF), lambda i, sc_ref: (0, i, 0)),
            ],
            out_specs=pl.BlockSpec((tr, HALF), lambda i, sc_ref: (i, sc_ref[1])),
        ),
        out_shape=pltpu.HBM((rows, 2 * HALF), F32),
        compiler_params=_params(32, ("parallel",)),
    )(shard_core, _in_hbm(part4), _in_hbm(recv3))


def _adam_math(w, g, m, v):
    m = ADAM_B1 * m + (1.0 - ADAM_B1) * g
    v = ADAM_B2 * v + (1.0 - ADAM_B2) * (g * g)
    m_hat = m / (1.0 - ADAM_B1**ADAM_STEP)
    v_hat = v / (1.0 - ADAM_B2**ADAM_STEP)
    delta = -ADAM_LR * (m_hat / (jnp.sqrt(v_hat) + ADAM_EPS) + ADAM_WD * w)
    return delta, m, v


def _adamw(w, g, m, v):
    rows, cols = w.shape
    tr = _row_tile(rows)

    def body(w_ref, g_ref, m_ref, v_ref, go_ref, d_ref, mo_ref, vo_ref):
        gv = g_ref[...]
        go_ref[...] = gv
        d_ref[...], mo_ref[...], vo_ref[...] = _adam_math(w_ref[...], gv, m_ref[...], v_ref[...])

    spec = pl.BlockSpec((tr, cols), lambda i: (i, 0))
    return pl.pallas_call(
        body, name="adamw", grid=(rows // tr,), in_specs=[spec] * 4, out_specs=[spec] * 4, out_shape=[pltpu.HBM(w.shape, F32)] * 4,
        compiler_params=_params(32, ("parallel",)),
    )(_in_hbm(w), _in_hbm(g), _in_hbm(m), _in_hbm(v))


SMALL_ROWS = 560
DECAY_ROWS = 8
SMALL_TOTAL = SMALL_ROWS + 2 * N_SHARDS * DECAY_ROWS


def _adamw_small(gathered, wp, mp, vp):
    out_rows = SMALL_ROWS + 2 * DECAY_ROWS

    def body(ga_ref, w_ref, m_ref, v_ref, g_ref, d_ref, mo_ref, vo_ref):
        shard = 2 * lax.axis_index("x") + lax.axis_index("y")
        g_ref[pl.ds(0, SMALL_ROWS), :] = functools.reduce(lambda a, b: a + b, [ga_ref[d, pl.ds(0, SMALL_ROWS), :] for d in range(8)])
        for k in range(2):
            start = pl.multiple_of(SMALL_ROWS + k * N_SHARDS * DECAY_ROWS + shard * DECAY_ROWS, DECAY_ROWS)
            g_ref[pl.ds(SMALL_ROWS + k * DECAY_ROWS, DECAY_ROWS), :] = functools.reduce(
                lambda a, b: a + b, [ga_ref[d, pl.ds(start, DECAY_ROWS), :] for d in range(8)])
        d_ref[...], mo_ref[...], vo_ref[...] = _adam_math(w_ref[...], g_ref[...], m_ref[...], v_ref[...])

    shape = jax.ShapeDtypeStruct((out_rows, LANES), F32)
    return pl.pallas_call(body, name="adamw_small", out_shape=[shape] * 4, compiler_params=_params(32, None))(gathered, wp, mp, vp)


ANY = pl.BlockSpec(memory_space=pl.ANY)


def _position():
    return lax.axis_index("x"), lax.axis_index("y"), lax.axis_index("c")


def _other_chips(x, y):
    return [(1 - x, y), (x, 1 - y), (1 - x, 1 - y)]


HBM = pl.BlockSpec(memory_space=pltpu.HBM)
SEM = pl.BlockSpec(memory_space=pltpu.SEMAPHORE)
TOKEN = jax.ShapeDtypeStruct(TOKEN_SHAPE, F32)
DATAFLOW = pltpu.SideEffectType.DATAFLOW_SIDE_EFFECTING


def _half_block(ref4, slot, core):
    return ref4.at[slot, :, pl.ds(pl.multiple_of(core * HALF, HALF), HALF)]


def _gather_ici_copies(refs4, send_sems, recv_sems, stride):
    x, y, c = _position()
    pairs = []
    for k, ref4 in enumerate(refs4):
        mine = _half_block(ref4, 2 * x + y, c)
        for j, (px, py) in enumerate(_other_chips(x, y)):
            sems = dict(send_sem=send_sems.at[stride * k + j], recv_sem=recv_sems.at[stride * k + j], device_id=(px, py, c), device_id_type=MESH)
            pairs.append((functools.partial(pltpu.make_async_remote_copy, src_ref=mine, dst_ref=mine, **sems),
                          functools.partial(pltpu.make_async_remote_copy, src_ref=mine, dst_ref=_half_block(ref4, 2 * px + py, c), **sems)))
    return pairs


def _gather_d2d_copies(refs4, send_sems, recv_sems, stride, offset):
    x, y, c = _position()
    pairs = []
    for k, ref4 in enumerate(refs4):
        for j, (px, py) in enumerate(_other_chips(x, y)):
            have = _half_block(ref4, 2 * px + py, c)
            sems = dict(send_sem=send_sems.at[stride * k + offset + j], recv_sem=recv_sems.at[stride * k + offset + j],
                        device_id=(x, y, 1 - c), device_id_type=MESH)
            pairs.append((functools.partial(pltpu.make_async_remote_copy, src_ref=have, dst_ref=have, **sems),
                          functools.partial(pltpu.make_async_remote_copy, src_ref=have, dst_ref=_half_block(ref4, 2 * px + py, 1 - c), **sems)))
    return pairs


def _gather_sync(bufs):
    n = len(bufs)

    def body(*refs):
        outs = refs[n : 2 * n]
        send_sems, recv_sems = refs[2 * n :]
        ici = _gather_ici_copies(outs, send_sems, recv_sems, 6)
        d2d = _gather_d2d_copies(outs, send_sems, recv_sems, 6, 3)
        for send, _ in ici:
            send().start()
        for (_, arrival), (forward, _) in zip(ici, d2d):
            arrival().wait_recv()
            forward().start()
        for _, arrival in d2d:
            arrival().wait_recv()
        for send, _ in ici + d2d:
            send().wait_send()

    return pl.pallas_call(
        body,
        name="gather_sync",
        in_specs=[ANY] * n,
        out_specs=[ANY] * n,
        out_shape=[jax.ShapeDtypeStruct(b.shape, b.dtype) for b in bufs],
        input_output_aliases={k: k for k in range(n)},
        scratch_shapes=[pltpu.SemaphoreType.DMA((6 * n,)), pltpu.SemaphoreType.DMA((6 * n,))],
        compiler_params=pltpu.CompilerParams(has_side_effects=True),
    )(*bufs)


def _gather_start(bufs, after):
    n, na = len(bufs), len(after)

    def body(*refs):
        ins = refs[:n]
        send_sems, recv_sems = refs[n + na], refs[n + na + 1]
        token = refs[2 * n + na + 2]
        for send, _ in _gather_ici_copies(ins, send_sems, recv_sems, 3):
            send().start()
        token[...] = jnp.zeros_like(token)

    out = pl.pallas_call(
        body,
        name="gather_start",
        in_specs=[HBM] * n + [ANY] * na,
        out_specs=(SEM, SEM, *[HBM] * n, pl.BlockSpec(memory_space=pltpu.VMEM)),
        out_shape=(pltpu.SemaphoreType.DMA((3 * n,)), pltpu.SemaphoreType.DMA((3 * n,)), *[pltpu.HBM(b.shape, b.dtype) for b in bufs], TOKEN),
        input_output_aliases={k: 2 + k for k in range(n)},
        compiler_params=pltpu.CompilerParams(has_side_effects=DATAFLOW),
    )(*[pltpu.with_memory_space_constraint(b, pltpu.HBM) for b in bufs], *after)
    return out[0], out[1], list(out[2 : 2 + n]), out[2 + n]


def _gather_wait(send_sems, recv_sems, bufs, after):
    n = len(bufs)

    def body(*refs):
        ins = refs[:n]
        for send, arrival in _gather_ici_copies(ins, refs[n], refs[n + 1], 3):
            send().wait_send()
            arrival().wait_recv()

    return pl.pallas_call(
        body,
        name="gather_wait",
        in_specs=[HBM] * n + [SEM, SEM] + [ANY] * len(after),
        out_specs=tuple([HBM] * n),
        out_shape=tuple(pltpu.HBM(b.shape, b.dtype) for b in bufs),
        input_output_aliases={k: k for k in range(n)},
        compiler_params=pltpu.CompilerParams(has_side_effects=DATAFLOW),
    )(*bufs, send_sems, recv_sems, *after)


def _gather_forward(bufs):
    n = len(bufs)

    def body(*refs):
        outs = refs[n : 2 * n]
        send_sems, recv_sems = refs[2 * n :]
        d2d = _gather_d2d_copies(outs, send_sems, recv_sems, 3, 0)
        for forward, _ in d2d:
            forward().start()
        for forward, arrival in d2d:
            arrival().wait_recv()
            forward().wait_send()

    return pl.pallas_call(
        body,
        name="gather_forward",
        in_specs=[ANY] * n,
        out_specs=[ANY] * n,
        out_shape=[jax.ShapeDtypeStruct(b.shape, b.dtype) for b in bufs],
        input_output_aliases={k: k for k in range(n)},
        scratch_shapes=[pltpu.SemaphoreType.DMA((3 * n,)), pltpu.SemaphoreType.DMA((3 * n,))],
        compiler_params=pltpu.CompilerParams(has_side_effects=True),
    )(*bufs)


def _exchange_halves(grads4):
    n = len(grads4)

    def body(*refs):
        ins, outs = refs[:n], refs[n : 2 * n]
        send_sems, recv_sems = refs[2 * n :]
        x, y, c = _position()
        copies = []
        for k in range(n):
            cp = pltpu.make_async_remote_copy(
                src_ref=ins[k].at[:, :, pl.ds(pl.multiple_of((1 - c) * HALF, HALF), HALF)], dst_ref=outs[k],
                send_sem=send_sems.at[k], recv_sem=recv_sems.at[k], device_id=(x, y, 1 - c), device_id_type=MESH)
            cp.start()
            copies.append(cp)
        for cp in copies:
            cp.wait()

    return pl.pallas_call(
        body,
        name="exchange_halves",
        in_specs=[ANY] * n,
        out_specs=[ANY] * n,
        out_shape=[jax.ShapeDtypeStruct((N_SHARDS, g.shape[1], HALF), g.dtype) for g in grads4],
        scratch_shapes=[pltpu.SemaphoreType.DMA((n,)), pltpu.SemaphoreType.DMA((n,))],
        compiler_params=pltpu.CompilerParams(has_side_effects=True),
    )(*grads4)


def _scatter_partials(parts4):
    n = len(parts4)

    def body(*refs):
        ins, outs = refs[:n], refs[n : 2 * n]
        send_sems, recv_sems = refs[2 * n :]
        x, y, c = _position()
        copies = []
        for k in range(n):
            for j, (px, py) in enumerate(_other_chips(x, y)):
                cp = pltpu.make_async_remote_copy(
                    src_ref=ins[k].at[2 * px + py], dst_ref=outs[k].at[j],
                    send_sem=send_sems.at[3 * k + j], recv_sem=recv_sems.at[3 * k + j], device_id=(px, py, c), device_id_type=MESH)
                cp.start()
                copies.append(cp)
        for cp in copies:
            cp.wait()

    return pl.pallas_call(
        body,
        name="scatter_partials",
        in_specs=[ANY] * n,
        out_specs=[ANY] * n,
        out_shape=[jax.ShapeDtypeStruct((3,) + g.shape[1:], g.dtype) for g in parts4],
        scratch_shapes=[pltpu.SemaphoreType.DMA((3 * n,)), pltpu.SemaphoreType.DMA((3 * n,))],
        compiler_params=pltpu.CompilerParams(has_side_effects=True),
    )(*parts4)


def _scatter_copies(parts, lands, send_sems, recv_sems):
    x, y, c = _position()
    copies = []
    for k in range(len(parts)):
        for j, (px, py) in enumerate(_other_chips(x, y)):
            copies.append(pltpu.make_async_remote_copy(
                src_ref=parts[k].at[2 * px + py], dst_ref=lands[k].at[j],
                send_sem=send_sems.at[3 * k + j], recv_sem=recv_sems.at[3 * k + j], device_id=(px, py, c), device_id_type=MESH))
    return copies


def _exchange_copies(grads, lands, send_sems, recv_sems):
    x, y, c = _position()
    return [pltpu.make_async_remote_copy(
        src_ref=grads[k].at[:, :, pl.ds(pl.multiple_of((1 - c) * HALF, HALF), HALF)], dst_ref=lands[k],
        send_sem=send_sems.at[k], recv_sem=recv_sems.at[k], device_id=(x, y, 1 - c), device_id_type=MESH) for k in range(len(grads))]


def _exchange_lands(grads4):
    return [jax.ShapeDtypeStruct((N_SHARDS, g.shape[1], HALF), g.dtype) for g in grads4]


def _scatter_lands(parts4):
    return [jax.ShapeDtypeStruct((3,) + g.shape[1:], g.dtype) for g in parts4]


def _split_start(name, srcs, land_shapes, make_copies, nsem):
    n, nl = len(srcs), len(land_shapes)
    lands = [lax.empty(a.shape, a.dtype) for a in land_shapes]

    def body(*refs):
        send_sems, recv_sems = refs[n + nl], refs[n + nl + 1]
        token = refs[2 * (n + nl) + 2]
        for cp in make_copies(refs[:n], refs[n : n + nl], send_sems, recv_sems):
            cp.start()
        token[...] = jnp.zeros_like(token)

    hbm = lambda a: pltpu.HBM(a.shape, a.dtype)
    out = pl.pallas_call(
        body,
        name=name,
        in_specs=[HBM] * (n + nl),
        out_specs=(SEM, SEM, *[HBM] * (n + nl), pl.BlockSpec(memory_space=pltpu.VMEM)),
        out_shape=(pltpu.SemaphoreType.DMA((nsem,)), pltpu.SemaphoreType.DMA((nsem,)), *[hbm(a) for a in srcs + lands], TOKEN),
        input_output_aliases={k: 2 + k for k in range(n + nl)},
        compiler_params=pltpu.CompilerParams(has_side_effects=DATAFLOW),
    )(*[pltpu.with_memory_space_constraint(a, pltpu.HBM) for a in srcs + lands])
    return out[0], out[1], list(out[2 : 2 + n]), list(out[2 + n : 2 + n + nl]), out[2 + n + nl]


def _split_wait(name, send_sems, recv_sems, srcs, lands, make_copies, after):
    n, nl = len(srcs), len(lands)

    def body(*refs):
        for cp in make_copies(refs[:n], refs[n : n + nl], refs[n + nl], refs[n + nl + 1]):
            cp.wait_send()
            cp.wait_recv()

    hbm = lambda a: pltpu.HBM(a.shape, a.dtype)
    out = pl.pallas_call(
        body,
        name=name,
        in_specs=[HBM] * (n + nl) + [SEM, SEM] + [ANY] * len(after),
        out_specs=tuple([HBM] * (n + nl)),
        out_shape=tuple(hbm(a) for a in srcs + lands),
        input_output_aliases={k: k for k in range(n + nl)},
        compiler_params=pltpu.CompilerParams(has_side_effects=DATAFLOW),
    )(*srcs, *lands, send_sems, recv_sems, *after)
    return list(out[:n]), list(out[n:])


def _join_halves(bufs):
    n = len(bufs)

    def body(*refs):
        outs = refs[n : 2 * n]
        send_sems, recv_sems = refs[2 * n :]
        x, y, c = _position()
        half = lambda ref, core: ref.at[:, pl.ds(pl.multiple_of(core * HALF, HALF), HALF)]
        for k in range(n):
            mine = half(outs[k], c)
            pltpu.make_async_remote_copy(
                src_ref=mine, dst_ref=mine, send_sem=send_sems.at[k], recv_sem=recv_sems.at[k],
                device_id=(x, y, 1 - c), device_id_type=MESH).start()
        for k in range(n):
            wait = pltpu.make_async_remote_copy(
                src_ref=half(outs[k], c), dst_ref=half(outs[k], 1 - c), send_sem=send_sems.at[k], recv_sem=recv_sems.at[k],
                device_id=(x, y, 1 - c), device_id_type=MESH)
            wait.wait_send()
            wait.wait_recv()

    return pl.pallas_call(
        body,
        name="join_halves",
        in_specs=[ANY] * n,
        out_specs=[ANY] * n,
        out_shape=[jax.ShapeDtypeStruct(b.shape, b.dtype) for b in bufs],
        input_output_aliases={k: k for k in range(n)},
        scratch_shapes=[pltpu.SemaphoreType.DMA((n,)), pltpu.SemaphoreType.DMA((n,))],
        compiler_params=pltpu.CompilerParams(has_side_effects=True),
    )(*bufs)


def _allgather_small(block):
    m_per, ncol = block.shape

    def body(x_ref, out_ref, send_sems, recv_sems, local_sem):
        x, y, c = _position()
        me, sibling = (x, y, c), (x, y, 1 - c)
        chips = _other_chips(x, y)

        def rows(px, py, pc):
            return out_ref.at[4 * px + 2 * py + pc]

        def copy(k, blk, to, src=None):
            return pltpu.make_async_remote_copy(
                src_ref=rows(*blk) if src is None else src, dst_ref=rows(*blk),
                send_sem=send_sems.at[k], recv_sem=recv_sems.at[k], device_id=to, device_id_type=MESH)

        mine = pltpu.make_async_copy(x_ref, rows(*me), local_sem)
        mine.start()
        first = [copy(0, me, sibling, src=x_ref)] + [copy(1 + j, me, (*chip, c), src=x_ref) for j, chip in enumerate(chips)]
        for cp in first:
            cp.start()
        passed = [copy(4 + j, (*chip, c), sibling) for j, chip in enumerate(chips)]
        for j, chip in enumerate(chips):
            copy(1 + j, (*chip, c), me).wait_recv()
            passed[j].start()
        copy(0, sibling, me).wait_recv()
        for j, chip in enumerate(chips):
            copy(4 + j, (*chip, 1 - c), me).wait_recv()
        for cp in first + passed:
            cp.wait_send()
        mine.wait()

    return pl.pallas_call(
        body,
        name="allgather_small",
        in_specs=[pl.BlockSpec(memory_space=pltpu.VMEM)],
        out_specs=pl.BlockSpec(memory_space=pltpu.VMEM),
        out_shape=jax.ShapeDtypeStruct((8, m_per, ncol), block.dtype),
        scratch_shapes=[pltpu.SemaphoreType.DMA((7,)), pltpu.SemaphoreType.DMA((7,)), pltpu.SemaphoreType.DMA],
        compiler_params=pltpu.CompilerParams(has_side_effects=True, vmem_limit_bytes=32 * MIB),
    )(block)


SMALL_NAMES = ["norm1_g", "b_decay_f", "b_decay_b", "gla_norm_g", "gmlp_ln_g", "gmlp_ln_b", "w_spatial", "b_spatial", "norm2_g", "final_norm_g"]


def _pack_small(parts, decay_parts):
    flat = jnp.concatenate([a.reshape(-1) for a in parts])
    flat = jnp.pad(flat, (0, SMALL_ROWS * LANES - flat.shape[0])).reshape(SMALL_ROWS, LANES)
    return jnp.concatenate([flat] + [d.reshape(-1, LANES) for d in decay_parts], axis=0)


def _unpack_small(packed, like):
    out, off = [], 0
    flat = packed[:SMALL_ROWS].reshape(-1)
    for a in like:
        out.append(flat[off : off + a.size].reshape(a.shape))
        off += a.size
    return out


def kernel(x, norm1_g, w_in, w_decay_f, b_decay_f, w_decay_b, b_decay_b, gla_norm_g, gmlp_ln_g, gmlp_ln_b, w_spatial, b_spatial, w_out, norm2_g, w_gate, w_up, w_down, final_norm_g, loss_target, m_norm1_g, m_w_in, m_w_decay_f, m_b_decay_f, m_w_decay_b, m_b_decay_b, m_gla_norm_g, m_gmlp_ln_g, m_gmlp_ln_b, m_w_spatial, m_b_spatial, m_w_out, m_norm2_g, m_w_gate, m_w_up, m_w_down, m_final_norm_g, v_norm1_g, v_w_in, v_w_decay_f, v_b_decay_f, v_w_decay_b, v_b_decay_b, v_gla_norm_g, v_gmlp_ln_g, v_gmlp_ln_b, v_w_spatial, v_b_spatial, v_w_out, v_norm2_g, v_w_gate, v_w_up, v_w_down, v_final_norm_g):
    args = dict(locals())
    cx, cy, cc = lax.axis_index("x"), lax.axis_index("y"), lax.axis_index("c")
    shard = 2 * cx + cy
    xs = x[0]
    target = loss_target[0]

    big_names = ["w_in", "w_out", "w_gate", "w_up", "w_down"]
    transposed = ("w_in", "w_gate", "w_up")
    rows_of = lambda pre, k: jnp.transpose(args[pre + k][0]) if k in transposed else args[pre + k][0]
    big_shards = {k: rows_of("", k) for k in big_names}
    c_arr = cc.reshape(1).astype(jnp.int32)
    s_arr = shard.reshape(1).astype(jnp.int32)
    sc_arr = jnp.stack([shard, cc]).astype(jnp.int32)
    slots = {k: _cast_into_slot(big_shards[k], s_arr) for k in big_names}
    (w_in4,) = _gather_sync([slots["w_in"]])
    w_in_t = w_in4.reshape(PROJ_W, D_MODEL)

    dec_block = jnp.concatenate([w_decay_f[0].reshape(-1, LANES), w_decay_b[0].reshape(-1, LANES)], axis=0)
    dec_all = _allgather_small(dec_block)
    late = ["w_out", "w_gate", "w_up", "w_down"]
    g_send, g_recv, late_bufs, token_gather = _gather_start([slots[k] for k in late], (w_in4, dec_all))
    dec_all = dec_all[::2].reshape(N_SHARDS, 2, LOWRANK, KEY_W // N_SHARDS)
    wdf_full = jnp.transpose(dec_all[:, 0], (1, 0, 2)).reshape(LOWRANK, KEY_W)
    wdb_full = jnp.transpose(dec_all[:, 1], (1, 0, 2)).reshape(LOWRANK, KEY_W)
    wd_pad_f = jnp.zeros((LANES, KEY_W), F32).at[0:LOWRANK].set(wdf_full).astype(BF16)
    wd_pad_b = jnp.zeros((LANES, KEY_W), F32).at[LOWRANK : 2 * LOWRANK].set(wdb_full).astype(BF16)

    ws_bf = w_spatial[0].astype(BF16)
    wst_bf = jnp.transpose(w_spatial[0], (0, 2, 1)).astype(BF16)
    bs_col = b_spatial[0].reshape(GMLP_GROUPS, GMLP_CHUNK, 1)

    p = _inproj(xs, norm1_g, w_in_t, token_gather)
    o_f, st_f = _gla_fwd(p, wd_pad_f, b_decay_f, reverse=False)
    o_b, st_b = _gla_fwd(p, wd_pad_b, b_decay_b, reverse=True)
    late_bufs = _gather_forward(_gather_wait(g_send, g_recv, late_bufs, (o_f, o_b)))
    w_out_full, wg_t, wu_t, wd = [b.reshape(-1, D_MODEL) for b in late_bufs]
    x1, ycat = _mixer_out(xs, o_f, o_b, p, gla_norm_g, gmlp_ln_g, gmlp_ln_b, ws_bf, bs_col, w_out_full)
    gf = final_norm_g.reshape(1, D_MODEL)
    h2, gate, up, act, dx2, loss_acc, dgf = _ffn_fwd(x1, target, norm2_g, gf, wg_t, wu_t, wd)

    dgate, dup, dx1, dg2 = _ffn_bwd(dx2, gate, up, x1, norm2_g, wg_t, wu_t, wd)
    ffn_grads4 = [g.reshape(N_SHARDS, FF_SHARD, D_MODEL) for g in _ffn_wgrad(h2, dgate, dup, act, dx2)]
    e_send, e_recv, e_srcs, e_lands, token_exchange = _split_start(
        "exchange_start", ffn_grads4, _exchange_lands(ffn_grads4), _exchange_copies, len(ffn_grads4))
    do, dg, du, dvv, dwo, dgn, dlng, dlnb, dws, dbs = _mixer_bwd(
        dx1, ycat, o_f, o_b, p, gla_norm_g, gmlp_ln_g, gmlp_ln_b, ws_bf, wst_bf, bs_col, w_out_full, token_exchange)
    ffn_mine, ffn_other = _split_wait("exchange_wait", e_send, e_recv, e_srcs, e_lands, _exchange_copies, (do,))
    ffn_parts = _add_halves(ffn_mine, ffn_other, c_arr)
    ffn_payload = [pb for _, pb in ffn_parts]
    s_send, s_recv, s_parts, s_lands, token_scatter = _split_start(
        "scatter_start", ffn_payload, _scatter_lands(ffn_payload), _scatter_copies, 3 * len(ffn_payload))
    dq_f, dk_f, dv_f, dlr_f, dwdec_f, dbdec_f = _gla_bwd(p, do, st_f, wd_pad_f, b_decay_f, token_scatter, reverse=False)
    dq_b, dk_b, dv_b, dlr_b, dwdec_b, dbdec_b = _gla_bwd(p, do, st_b, wd_pad_b, b_decay_b, token_scatter, reverse=True)
    dwin_t, dp = _inproj_wgrad(xs, norm1_g, dq_f, dq_b, dk_f, dk_b, dv_f, dv_b, dg, du, dvv, dlr_f, dlr_b)
    _, ffn_recv = _split_wait("scatter_wait", s_send, s_recv, s_parts, s_lands, _scatter_copies, (dwin_t,))

    dwin4 = dwin_t.reshape(N_SHARDS, PROJ_W // N_SHARDS, D_MODEL)
    dwo4 = dwo.reshape(N_SHARDS, D_MODEL // N_SHARDS, D_MODEL)
    proj_grads4 = [dwin4, dwo4]
    proj_parts = [_add_halves([g], [r], c_arr)[0] for g, r in zip(proj_grads4, _exchange_halves(proj_grads4))]
    proj_payload = [pb for _, pb in proj_parts]
    p_send, p_recv, p_parts, p_lands, token_proj = _split_start(
        "proj_scatter_start", proj_payload, _scatter_lands(proj_payload), _scatter_copies, 3 * len(proj_payload))
    dx, dg1 = _inproj_dx(xs, dx1, norm1_g, w_in_t, dp, token_proj)
    _, proj_recv = _split_wait("proj_scatter_wait", p_send, p_recv, p_parts, p_lands, _scatter_copies, (dx,))
    parts_f32 = [pf for pf, _ in proj_parts + ffn_parts]
    bufs = [_add_partials(pf, r, sc_arr) for pf, r in zip(parts_f32, proj_recv + ffn_recv)]
    big_grads = dict(zip(big_names, _join_halves(bufs)))

    dwdec_f16 = dwdec_f[0:LOWRANK]
    dwdec_b16 = dwdec_b[LOWRANK : 2 * LOWRANK]
    shard_major = lambda a: jnp.transpose(a.reshape(LOWRANK, N_SHARDS, KEY_W // N_SHARDS), (1, 0, 2))
    small_grads = {
        "norm1_g": dg1, "b_decay_f": dbdec_f, "b_decay_b": dbdec_b, "gla_norm_g": dgn, "gmlp_ln_g": dlng, "gmlp_ln_b": dlnb,
        "w_spatial": dws, "b_spatial": dbs, "norm2_g": dg2, "final_norm_g": dgf,
    }
    g_pack = _pack_small([small_grads[k] for k in SMALL_NAMES] + [loss_acc], [shard_major(dwdec_f16), shard_major(dwdec_b16)])
    g_all = _allgather_small(g_pack)
    pack_own = lambda pre: _pack_small([args[pre + k] for k in SMALL_NAMES], [args[pre + "w_decay_f"], args[pre + "w_decay_b"]])
    sg, sd, sm, sv = _adamw_small(g_all, pack_own(""), pack_own("m_"), pack_own("v_"))

    names = ["norm1_g", "w_in", "w_decay_f", "b_decay_f", "w_decay_b", "b_decay_b", "gla_norm_g", "gmlp_ln_g", "gmlp_ln_b",
             "w_spatial", "b_spatial", "w_out", "norm2_g", "w_gate", "w_up", "w_down", "final_norm_g"]
    like = [args[k] for k in SMALL_NAMES]
    results = {"g": {}, "d": {}, "m": {}, "v": {}}
    for tag, packed in (("g", sg), ("d", sd), ("m", sm), ("v", sv)):
        for k, a in zip(SMALL_NAMES, _unpack_small(packed, like)):
            results[tag][k] = a
        results[tag]["w_decay_f"] = packed[SMALL_ROWS : SMALL_ROWS + DECAY_ROWS].reshape(w_decay_f.shape)
        results[tag]["w_decay_b"] = packed[SMALL_ROWS + DECAY_ROWS :].reshape(w_decay_b.shape)
    for k in big_names:
        g, d, mo, vo = _adamw(big_shards[k], big_grads[k], rows_of("m_", k), rows_of("v_", k))
        for tag, a in (("g", g), ("d", d), ("m", mo), ("v", vo)):
            results[tag][k] = (jnp.transpose(a) if k in transposed else a).reshape(args[k].shape)

    loss = sg[:SMALL_ROWS].reshape(-1)[sum(a.size for a in like)]
    grad_x = dx.reshape(x.shape)
    return (loss, grad_x, *[results["g"][k] for k in names], *[results["d"][k] for k in names],
            *[results["m"][k] for k in names], *[results["v"][k] for k in names])
```

```python
import functools
import math

import jax
import jax.numpy as jnp
from jax import lax
from jax.experimental import pallas as pl
from jax.experimental.pallas import tpu as pltpu

F32, BF16 = jnp.float32, jnp.bfloat16

D_MODEL = 1024
GLA_HEADS = 4
GLA_DK = 64
GLA_DV = 128
KEY_W = GLA_HEADS * GLA_DK
GLA_W = GLA_HEADS * GLA_DV
GMLP_W = 512
GMLP_GROUPS = 4
GMLP_CHUNK = 128
LOWRANK = 16
GLA_CHUNK = 64
GLA_TAU = 16.0
PROJ_W = 2592
PROJ_WP = 2688
D_FF = 2816
N_SHARDS = 4
FF_SHARD = D_FF // N_SHARDS
EPS = 1e-6
LANES = 128
TOKEN_SHAPE = (8, LANES)
MIB = 1024 * 1024

ADAM_LR = 0.001
ADAM_B1 = 0.9
ADAM_B2 = 0.999
ADAM_EPS = 1e-08
ADAM_WD = 0.01
ADAM_STEP = 10

COL_Q, COL_K = 0, 256
COL_V, COL_G, COL_U, COL_VV = 512, 1024, 1536, 2048
COL_LR = 2560
ROW_LR, ROW_UV = 1536, 1568
HALF = D_MODEL // 2

MESH = pl.DeviceIdType.MESH


def _nn(a, b):
    return jnp.dot(a, b, preferred_element_type=F32)


def _nt(a, b):
    return lax.dot_general(a, b, (((1,), (1,)), ((), ())), preferred_element_type=F32)


def _tn(a, b):
    return lax.dot_general(a, b, (((0,), (0,)), ((), ())), preferred_element_type=F32)


def _bnn(a, b):
    return jnp.einsum("nik,nkj->nij", a, b, preferred_element_type=F32)


def _bnt(a, b):
    return jnp.einsum("nik,njk->nij", a, b, preferred_element_type=F32)


def _btn(a, b):
    return jnp.einsum("nki,nkj->nij", a, b, preferred_element_type=F32)


def _resident(shape):
    zeros = (0,) * len(shape)
    return pl.BlockSpec(shape, lambda *_: zeros, pipeline_mode=pl.Buffered(1))


def _params(vmem_mib, semantics=("arbitrary",)):
    return pltpu.CompilerParams(vmem_limit_bytes=vmem_mib * MIB, dimension_semantics=semantics)


def _sigmoid(x):
    return 1.0 / (1.0 + jnp.exp(-x))


def _gelu(x):
    return 0.5 * x * (1.0 + lax.erf(x * (1.0 / math.sqrt(2.0))))


def _gelu_and_grad(x):
    cdf = 0.5 * (1.0 + lax.erf(x * (1.0 / math.sqrt(2.0))))
    return x * cdf, cdf + x * jnp.exp(-0.5 * x * x) * (1.0 / math.sqrt(2.0 * math.pi))


def _log_sigmoid(x):
    return jnp.minimum(x, 0.0) - jnp.log(1.0 + jnp.exp(-jnp.abs(x)))


def _rms_bwd(dxh, xh, r):
    return r * (dxh - xh * jnp.mean(dxh * xh, axis=-1, keepdims=True))


def _chunk_cumsum(v, row_in_chunk, reverse):
    rows = v.shape[0]
    for sh in (1, 2, 4, 8, 16, 32):
        if reverse:
            v = v + jnp.where(row_in_chunk + sh < GLA_CHUNK, pltpu.roll(v, rows - sh, axis=0), 0.0)
        else:
            v = v + jnp.where(row_in_chunk >= sh, pltpu.roll(v, sh, axis=0), 0.0)
    return v


def _inproj(x, g1, w_in_t, token):
    seq = x.shape[0]
    tm = min(seq, 512)

    def body(x_ref, g_ref, w_ref, token_ref, p_ref):
        xv = x_ref[...]
        r = lax.rsqrt(jnp.mean(xv * xv, axis=-1, keepdims=True) + EPS)
        h = (xv * r * g_ref[...]).astype(BF16)
        p_ref[:, 0:COL_U] = _nt(h, w_ref[0:ROW_LR, :])
        p_ref[:, COL_U:COL_LR] = _nt(h, w_ref[ROW_UV:PROJ_W, :])
        p_ref[:, COL_LR:PROJ_WP] = _nt(h, w_ref[ROW_LR : ROW_LR + LANES, :])

    return pl.pallas_call(
        body,
        name="inproj",
        grid=(seq // tm,),
        in_specs=[pl.BlockSpec((tm, D_MODEL), lambda i: (i, 0)), _resident((1, D_MODEL)), _resident((PROJ_W, D_MODEL)), _resident(TOKEN_SHAPE)],
        out_specs=pl.BlockSpec((tm, PROJ_WP), lambda i: (i, 0)),
        out_shape=jax.ShapeDtypeStruct((seq, PROJ_WP), F32),
        compiler_params=_params(48, ("parallel",)),
    )(x, g1, w_in_t, token)


def _gla_tile(seq):
    return min(seq, 1024)


def _gla_decay_terms(lr_bf, wd_ref, bd_ref, pair, row_in_chunk, reverse, n):
    cols = pl.ds(pair * LANES, LANES)
    pre = _nn(lr_bf, wd_ref[:, cols]) + bd_ref[:, cols]
    la = _log_sigmoid(pre) * (1.0 / GLA_TAU)
    b = _chunk_cumsum(la, row_in_chunk, reverse)
    b3 = b.reshape(n, GLA_CHUNK, LANES)
    blast = b3[:, 0:1, :] if reverse else b3[:, GLA_CHUNK - 1 : GLA_CHUNK, :]
    return pre, b3, blast


def _gla_fwd(p, wd_pad, bd, reverse):
    seq = p.shape[0]
    tg = _gla_tile(seq)
    nt = seq // tg
    n = tg // GLA_CHUNK
    scale = GLA_DK**-0.5

    def tile(i):
        return nt - 1 - i if reverse else i

    def body(q_ref, k_ref, v_ref, lr_ref, wd_ref, bd_ref, o_ref, st_ref, carry):
        @pl.when(pl.program_id(0) == 0)
        def _():
            carry[...] = jnp.zeros_like(carry)

        lr_bf = lr_ref[...].astype(BF16)
        states = [carry[h] for h in range(GLA_HEADS)]
        row_in_chunk = lax.broadcasted_iota(jnp.int32, (tg, LANES), 0) % GLA_CHUNK
        lane_head = lax.broadcasted_iota(jnp.int32, (1, LANES), 1) // GLA_DK
        tt = lax.broadcasted_iota(jnp.int32, (GLA_CHUNK, GLA_CHUNK), 0)
        ss = lax.broadcasted_iota(jnp.int32, (GLA_CHUNK, GLA_CHUNK), 1)
        causal = (tt <= ss) if reverse else (tt >= ss)
        order = range(n - 1, -1, -1) if reverse else range(n)
        heads = range(GLA_HEADS)
        qdh, kds, vhs, decs, sc_raw, dst = {}, {}, {}, {}, {}, {}
        for pair in range(2):
            cols = pl.ds(pair * LANES, LANES)
            _, b3, blast = _gla_decay_terms(lr_bf, wd_ref, bd_ref, pair, row_in_chunk, reverse, n)
            q3 = q_ref[:, cols].reshape(n, GLA_CHUNK, LANES) * scale
            k3 = k_ref[:, cols].reshape(n, GLA_CHUNK, LANES)
            qd = q3 * jnp.exp(b3)
            kd = (k3 * jnp.exp(-b3)).astype(BF16)
            kte = k3 * jnp.exp(blast - b3)
            dec = jnp.exp(blast)
            for hh in range(2):
                h = 2 * pair + hh
                m = (lane_head == hh).astype(F32)
                qdh[h], kds[h], decs[h] = (qd * m).astype(BF16), kd, dec
                vhs[h] = v_ref[:, pl.ds(h * GLA_DV, GLA_DV)].reshape(n, GLA_CHUNK, GLA_DV).astype(BF16)
                sc_raw[h] = _bnt(qdh[h], kd)
                dst[h] = _btn(vhs[h], (kte * m).astype(BF16))
        o_intra, befores = {}, {}
        for h in heads:
            o_intra[h] = _bnn(jnp.where(causal, sc_raw[h], 0.0).astype(BF16), vhs[h])
            st, before = states[h], [None] * n
            for j in order:
                before[j] = st
                st = st * decs[h][j] + dst[h][j]
            states[h] = st
            befores[h] = jnp.stack(before)
        outs = {h: (o_intra[h] + _bnt(qdh[h], befores[h].astype(BF16))).reshape(tg, GLA_DV) for h in heads}
        for h in range(GLA_HEADS):
            o_ref[:, pl.ds(h * GLA_DV, GLA_DV)] = outs[h]
            st_ref[:, h] = befores[h]
            carry[h] = states[h]

    nchunks = seq // GLA_CHUNK
    return pl.pallas_call(
        body,
        name="gla_fwd_rev" if reverse else "gla_fwd",
        grid=(nt,),
        in_specs=[
            pl.BlockSpec((tg, KEY_W), lambda i: (tile(i), COL_Q // KEY_W)),
            pl.BlockSpec((tg, KEY_W), lambda i: (tile(i), COL_K // KEY_W)),
            pl.BlockSpec((tg, GLA_W), lambda i: (tile(i), COL_V // GLA_W)),
            pl.BlockSpec((tg, LANES), lambda i: (tile(i), COL_LR // LANES)),
            _resident((LANES, KEY_W)),
            _resident((1, KEY_W)),
        ],
        out_specs=[
            pl.BlockSpec((tg, GLA_W), lambda i: (tile(i), 0)),
            pl.BlockSpec((n, GLA_HEADS, GLA_DV, LANES), lambda i: (tile(i), 0, 0, 0)),
        ],
        out_shape=[
            jax.ShapeDtypeStruct((seq, GLA_W), F32),
            jax.ShapeDtypeStruct((nchunks, GLA_HEADS, GLA_DV, LANES), F32),
        ],
        scratch_shapes=[pltpu.VMEM((GLA_HEADS, GLA_DV, LANES), F32)],
        compiler_params=_params(48),
    )(p, p, p, p, wd_pad, bd)


def _mixer_out(x, o_f, o_b, p, gn, lng, lnb, ws_bf, bs_col, w_out):
    seq = x.shape[0]
    tm = min(seq, 512)

    def body(x_ref, of_ref, ob_ref, g_ref, u_ref, vv_ref, gn_ref, lng_ref, lnb_ref, ws_ref, bs_ref, wo_ref, x1_ref, yc_ref, vn_sc):
        for h in range(GLA_HEADS):
            cols = pl.ds(h * GLA_DV, GLA_DV)
            oh = of_ref[:, cols] + ob_ref[:, cols]
            on = oh * lax.rsqrt(jnp.mean(oh * oh, axis=-1, keepdims=True) + EPS)
            gh = g_ref[:, cols]
            yc_ref[:, cols] = (on * gn_ref[:, cols] * (gh * _sigmoid(gh))).astype(BF16)
        zv = _gelu(vv_ref[...])
        xc = zv - jnp.mean(zv, axis=-1, keepdims=True)
        vhat = xc * lax.rsqrt(jnp.mean(xc * xc, axis=-1, keepdims=True) + EPS)
        vn_sc[...] = (vhat * lng_ref[...] + lnb_ref[...]).astype(BF16)
        for c in range(tm // GMLP_CHUNK):
            rows = pl.ds(c * GMLP_CHUNK, GMLP_CHUNK)
            for g in range(GMLP_GROUPS):
                cols = pl.ds(g * LANES, LANES)
                s = _nn(ws_ref[g], vn_sc[rows, cols]) + bs_ref[g]
                yc_ref[rows, pl.ds(GLA_W + g * LANES, LANES)] = (_gelu(u_ref[rows, cols]) * s).astype(BF16)
        x1_ref[...] = x_ref[...] + _nn(yc_ref[...], wo_ref[...])

    row = lambda w: pl.BlockSpec((tm, w), lambda i: (i, 0))
    pcol = lambda col: pl.BlockSpec((tm, GLA_W), lambda i: (i, col // GLA_W))
    return pl.pallas_call(
        body,
        name="mixer_out",
        grid=(seq // tm,),
        in_specs=[
            row(D_MODEL), row(GLA_W), row(GLA_W), pcol(COL_G), pcol(COL_U), pcol(COL_VV),
            _resident((1, GLA_W)), _resident((1, GMLP_W)), _resident((1, GMLP_W)),
            _resident((GMLP_GROUPS, GMLP_CHUNK, GMLP_CHUNK)), _resident((GMLP_GROUPS, GMLP_CHUNK, 1)),
            _resident((D_MODEL, D_MODEL)),
        ],
        out_specs=[row(D_MODEL), row(D_MODEL)],
        out_shape=[jax.ShapeDtypeStruct((seq, D_MODEL), F32), jax.ShapeDtypeStruct((seq, D_MODEL), BF16)],
        scratch_shapes=[pltpu.VMEM((tm, GMLP_W), BF16)],
        compiler_params=_params(48, ("parallel",)),
    )(x, o_f, o_b, p, p, p, gn, lng, lnb, ws_bf, bs_col, w_out)


def _ffn_fwd(x1, target, g2, gf, wg_t, wu_t, wd):
    seq = x1.shape[0]
    tm = min(seq, 256)

    def body(x1_ref, t_ref, g2_ref, gf_ref, wg_ref, wu_ref, wd_ref, h2_ref, gate_ref, up_ref, act_ref, dx2_ref, loss_ref, dgf_ref):
        @pl.when(pl.program_id(0) == 0)
        def _():
            loss_ref[...] = jnp.zeros_like(loss_ref)
            dgf_ref[...] = jnp.zeros_like(dgf_ref)

        x1v = x1_ref[...]
        h2 = (x1v * lax.rsqrt(jnp.mean(x1v * x1v, axis=-1, keepdims=True) + EPS) * g2_ref[...]).astype(BF16)
        h2_ref[...] = h2
        gate = _nt(h2, wg_ref[...])
        up = _nt(h2, wu_ref[...])
        act = (gate * _sigmoid(gate) * up).astype(BF16)
        gate_ref[...] = gate
        up_ref[...] = up
        act_ref[...] = act
        x2 = x1v + _nn(act, wd_ref[...])
        rf = lax.rsqrt(jnp.mean(x2 * x2, axis=-1, keepdims=True) + EPS)
        xh = x2 * rf
        err = xh * gf_ref[...] - t_ref[...]
        loss_ref[...] += 0.5 * jnp.sum(jnp.mean(err * err, axis=-1, keepdims=True))
        dy = err * (1.0 / D_MODEL)
        dgf_ref[...] += jnp.sum(dy * xh, axis=0, keepdims=True)
        dx2_ref[...] = _rms_bwd(dy * gf_ref[...], xh, rf)

    row = lambda w: pl.BlockSpec((tm, w), lambda i: (i, 0))
    weight = _resident((D_FF, D_MODEL))
    return pl.pallas_call(
        body,
        name="ffn_fwd",
        grid=(seq // tm,),
        in_specs=[row(D_MODEL), row(D_MODEL), _resident((1, D_MODEL)), _resident((1, D_MODEL)), weight, weight, weight],
        out_specs=[row(D_MODEL), row(D_FF), row(D_FF), row(D_FF), row(D_MODEL),
                   pl.BlockSpec((1, LANES), lambda i: (0, 0)), pl.BlockSpec((1, D_MODEL), lambda i: (0, 0))],
        out_shape=[
            jax.ShapeDtypeStruct((seq, D_MODEL), BF16),
            jax.ShapeDtypeStruct((seq, D_FF), F32),
            jax.ShapeDtypeStruct((seq, D_FF), F32),
            jax.ShapeDtypeStruct((seq, D_FF), BF16),
            jax.ShapeDtypeStruct((seq, D_MODEL), F32),
            jax.ShapeDtypeStruct((1, LANES), F32),
            jax.ShapeDtypeStruct((1, D_MODEL), F32),
        ],
        compiler_params=_params(56),
    )(x1, target, g2, gf, wg_t, wu_t, wd)


def _ffn_bwd(dx2, gate, up, x1, g2, wg_t, wu_t, wd):
    seq = x1.shape[0]
    tm = min(seq, 256)

    def body(dx2_ref, gate_ref, up_ref, x1_ref, g2_ref, wg_ref, wu_ref, wd_ref, dgate_ref, dup_ref, dx1_ref, dg2_ref):
        @pl.when(pl.program_id(0) == 0)
        def _():
            dg2_ref[...] = jnp.zeros_like(dg2_ref)

        dx2v = dx2_ref[...]
        dact = _nt(dx2v.astype(BF16), wd_ref[...])
        gate = gate_ref[...]
        sg = _sigmoid(gate)
        dgate = (dact * up_ref[...] * (sg * (1.0 + gate * (1.0 - sg)))).astype(BF16)
        dup = (dact * (gate * sg)).astype(BF16)
        dgate_ref[...] = dgate
        dup_ref[...] = dup
        dh2 = _nn(dgate, wg_ref[...]) + _nn(dup, wu_ref[...])
        x1v = x1_ref[...]
        r2 = lax.rsqrt(jnp.mean(x1v * x1v, axis=-1, keepdims=True) + EPS)
        xh = x1v * r2
        dg2_ref[...] += jnp.sum(dh2 * xh, axis=0, keepdims=True)
        dx1_ref[...] = dx2v + _rms_bwd(dh2 * g2_ref[...], xh, r2)

    row = lambda w: pl.BlockSpec((tm, w), lambda i: (i, 0))
    weight = _resident((D_FF, D_MODEL))
    return pl.pallas_call(
        body,
        name="ffn_bwd",
        grid=(seq // tm,),
        in_specs=[row(D_MODEL), row(D_FF), row(D_FF), row(D_MODEL), _resident((1, D_MODEL)), weight, weight, weight],
        out_specs=[row(D_FF), row(D_FF), row(D_MODEL), pl.BlockSpec((1, D_MODEL), lambda i: (0, 0))],
        out_shape=[
            jax.ShapeDtypeStruct((seq, D_FF), BF16),
            jax.ShapeDtypeStruct((seq, D_FF), BF16),
            jax.ShapeDtypeStruct((seq, D_MODEL), F32),
            jax.ShapeDtypeStruct((1, D_MODEL), F32),
        ],
        compiler_params=_params(56),
    )(dx2, gate, up, x1, g2, wg_t, wu_t, wd)


WGRAD_ROWS = D_FF // 2


def _ffn_wgrad(a, b):
    seq = a.shape[0]
    tm = min(seq, 1024)

    def body(a_ref, b_ref, dw_ref):
        @pl.when(pl.program_id(1) == 0)
        def _():
            dw_ref[...] = jnp.zeros_like(dw_ref)

        b = b_ref[...]
        dw_ref[...] += _tn(a_ref[...], b if b.dtype == BF16 else b.astype(BF16))

    return pl.pallas_call(
        body,
        name="ffn_wgrad",
        grid=(D_FF // WGRAD_ROWS, seq // tm),
        in_specs=[pl.BlockSpec((tm, WGRAD_ROWS), lambda j, i: (i, j)), pl.BlockSpec((tm, D_MODEL), lambda j, i: (i, 0))],
        out_specs=pl.BlockSpec((WGRAD_ROWS, D_MODEL), lambda j, i: (j, 0)),
        out_shape=jax.ShapeDtypeStruct((D_FF, D_MODEL), F32),
        compiler_params=_params(48, ("parallel", "arbitrary")),
    )(a, b)


def _mixer_bwd(dx1, ycat, o_f, o_b, p, gn, lng, lnb, ws_bf, wst_bf, bs_col, w_out, token):
    seq = dx1.shape[0]
    tm = min(seq, 512)
    nsteps = seq // tm

    def body(dx1_ref, yc_ref, of_ref, ob_ref, g_ref, u_ref, vv_ref, gn_ref, lng_ref, lnb_ref, ws_ref, wst_ref, bs_ref, wo_ref, token_ref,
             do_ref, dg_ref, du_ref, dvv_ref, dwo_ref, dgn_ref, dlng_ref, dlnb_ref, dws_ref, dbs_ref, vn_sc, dvn_sc, dbs_acc):
        step = pl.program_id(0)

        @pl.when(step == 0)
        def _():
            for r in (dwo_ref, dgn_ref, dlng_ref, dlnb_ref, dws_ref, dbs_acc):
                r[...] = jnp.zeros_like(r)

        dx1b = dx1_ref[...].astype(BF16)
        dyc = _nt(dx1b, wo_ref[...])
        dwo_ref[...] += _tn(yc_ref[...], dx1b)
        for h in range(GLA_HEADS):
            cols = pl.ds(h * GLA_DV, GLA_DV)
            dya = dyc[:, h * GLA_DV : (h + 1) * GLA_DV]
            oh = of_ref[:, cols] + ob_ref[:, cols]
            rn = lax.rsqrt(jnp.mean(oh * oh, axis=-1, keepdims=True) + EPS)
            on = oh * rn
            gh = g_ref[:, cols]
            sg = _sigmoid(gh)
            sil = gh * sg
            gnh = gn_ref[:, cols]
            dgn_ref[:, cols] += jnp.sum(dya * on * sil, axis=0, keepdims=True)
            dg_ref[:, cols] = dya * on * gnh * (sg * (1.0 + gh * (1.0 - sg)))
            do_ref[:, cols] = _rms_bwd(dya * gnh * sil, on, rn)
        vv = vv_ref[...]
        zv, zv_grad = _gelu_and_grad(vv)
        xc = zv - jnp.mean(zv, axis=-1, keepdims=True)
        rstd = lax.rsqrt(jnp.mean(xc * xc, axis=-1, keepdims=True) + EPS)
        vhat = xc * rstd
        vn_sc[...] = (vhat * lng_ref[...] + lnb_ref[...]).astype(BF16)
        for c in range(tm // GMLP_CHUNK):
            rows = pl.ds(c * GMLP_CHUNK, GMLP_CHUNK)
            for g in range(GMLP_GROUPS):
                cols = pl.ds(g * LANES, LANES)
                vn = vn_sc[rows, cols]
                s = _nn(ws_ref[g], vn) + bs_ref[g]
                dyb = dyc[c * GMLP_CHUNK : (c + 1) * GMLP_CHUNK, GLA_W + g * LANES : GLA_W + (g + 1) * LANES]
                zu, zu_grad = _gelu_and_grad(u_ref[rows, cols])
                du_ref[rows, cols] = dyb * s * zu_grad
                ds = dyb * zu
                dbs_acc[g] += ds
                dsb = ds.astype(BF16)
                dws_ref[g] += _nt(dsb, vn)
                dvn_sc[rows, cols] = _nn(wst_ref[g], dsb)
        dvn = dvn_sc[...]
        dlng_ref[...] += jnp.sum(dvn * vhat, axis=0, keepdims=True)
        dlnb_ref[...] += jnp.sum(dvn, axis=0, keepdims=True)
        dvh = dvn * lng_ref[...]
        dzv = rstd * (dvh - jnp.mean(dvh, axis=-1, keepdims=True) - vhat * jnp.mean(dvh * vhat, axis=-1, keepdims=True))
        dvv_ref[...] = dzv * zv_grad

        @pl.when(step == nsteps - 1)
        def _():
            dbs_ref[...] = jnp.sum(dbs_acc[...], axis=-1, keepdims=True)

    row = lambda w: pl.BlockSpec((tm, w), lambda i: (i, 0))
    pcol = lambda col: pl.BlockSpec((tm, GLA_W), lambda i: (i, col // GLA_W))
    const = lambda shape: pl.BlockSpec(shape, lambda i: (0,) * len(shape))
    return pl.pallas_call(
        body,
        name="mixer_bwd",
        grid=(nsteps,),
        in_specs=[
            row(D_MODEL), row(D_MODEL), row(GLA_W), row(GLA_W), pcol(COL_G), pcol(COL_U), pcol(COL_VV),
            _resident((1, GLA_W)), _resident((1, GMLP_W)), _resident((1, GMLP_W)),
            _resident((GMLP_GROUPS, GMLP_CHUNK, GMLP_CHUNK)), _resident((GMLP_GROUPS, GMLP_CHUNK, GMLP_CHUNK)),
            _resident((GMLP_GROUPS, GMLP_CHUNK, 1)), _resident((D_MODEL, D_MODEL)), _resident(TOKEN_SHAPE),
        ],
        out_specs=[
            row(GLA_W), row(GLA_W), row(GMLP_W), row(GMLP_W), const((D_MODEL, D_MODEL)),
            const((1, GLA_W)), const((1, GMLP_W)), const((1, GMLP_W)),
            const((GMLP_GROUPS, GMLP_CHUNK, GMLP_CHUNK)), const((GMLP_GROUPS, GMLP_CHUNK, 1)),
        ],
        out_shape=[
            jax.ShapeDtypeStruct((seq, GLA_W), F32), jax.ShapeDtypeStruct((seq, GLA_W), F32),
            jax.ShapeDtypeStruct((seq, GMLP_W), F32), jax.ShapeDtypeStruct((seq, GMLP_W), F32),
            jax.ShapeDtypeStruct((D_MODEL, D_MODEL), F32),
            jax.ShapeDtypeStruct((1, GLA_W), F32), jax.ShapeDtypeStruct((1, GMLP_W), F32), jax.ShapeDtypeStruct((1, GMLP_W), F32),
            jax.ShapeDtypeStruct((GMLP_GROUPS, GMLP_CHUNK, GMLP_CHUNK), F32), jax.ShapeDtypeStruct((GMLP_GROUPS, GMLP_CHUNK, 1), F32),
        ],
        scratch_shapes=[pltpu.VMEM((tm, GMLP_W), BF16), pltpu.VMEM((tm, GMLP_W), F32), pltpu.VMEM((GMLP_GROUPS, GMLP_CHUNK, GMLP_CHUNK), F32)],
        compiler_params=_params(56),
    )(dx1, ycat, o_f, o_b, p, p, p, gn, lng, lnb, ws_bf, wst_bf, bs_col, w_out, token)


def _gla_bwd(p, do, st, wd_pad, bd, token, reverse):
    seq = p.shape[0]
    tg = _gla_tile(seq)
    nt = seq // tg
    n = tg // GLA_CHUNK
    scale = GLA_DK**-0.5

    def tile(i):
        return i if reverse else nt - 1 - i

    def body(q_ref, k_ref, v_ref, lr_ref, do_ref, st_ref, wd_ref, bd_ref, token_ref, dq_ref, dk_ref, dv_ref, dlr_ref, dwd_ref, dbd_ref, carry):
        @pl.when(pl.program_id(0) == 0)
        def _():
            carry[...] = jnp.zeros_like(carry)
            dwd_ref[...] = jnp.zeros_like(dwd_ref)
            dbd_ref[...] = jnp.zeros_like(dbd_ref)

        lr_bf = lr_ref[...].astype(BF16)
        carries = [carry[h] for h in range(GLA_HEADS)]
        row_in_chunk = lax.broadcasted_iota(jnp.int32, (tg, LANES), 0) % GLA_CHUNK
        lane_head = lax.broadcasted_iota(jnp.int32, (1, LANES), 1) // GLA_DK
        tt = lax.broadcasted_iota(jnp.int32, (GLA_CHUNK, GLA_CHUNK), 0)
        ss = lax.broadcasted_iota(jnp.int32, (GLA_CHUNK, GLA_CHUNK), 1)
        causal = (tt <= ss) if reverse else (tt >= ss)
        causal_t = (tt >= ss) if reverse else (tt <= ss)
        order = range(n) if reverse else range(n - 1, -1, -1)
        dlr = jnp.zeros((tg, LANES), F32)
        heads = range(GLA_HEADS)
        pv, masks, qdh, kteh, vhs, dohs, stbs = {}, {}, {}, {}, {}, {}, {}
        sc_t, dp, dp_t, acc = {}, {}, {}, {}
        for pair in range(2):
            cols = pl.ds(pair * LANES, LANES)
            pre, b3, blast = _gla_decay_terms(lr_bf, wd_ref, bd_ref, pair, row_in_chunk, reverse, n)
            q3 = q_ref[:, cols].reshape(n, GLA_CHUNK, LANES) * scale
            k3 = k_ref[:, cols].reshape(n, GLA_CHUNK, LANES)
            eb = jnp.exp(b3)
            emb = jnp.exp(-b3)
            ekte = jnp.exp(blast - b3)
            kdf = k3 * emb
            pv[pair] = dict(pre=pre, eb=eb, emb=emb, ekte=ekte, qd=q3 * eb, kdf=kdf, kd=kdf.astype(BF16), kte=k3 * ekte, dec=jnp.exp(blast))
            for hh in range(2):
                h = 2 * pair + hh
                vcols = pl.ds(h * GLA_DV, GLA_DV)
                masks[h] = (lane_head == hh).astype(F32)
                qdh[h] = (pv[pair]["qd"] * masks[h]).astype(BF16)
                kteh[h] = (pv[pair]["kte"] * masks[h]).astype(BF16)
                vhs[h] = v_ref[:, vcols].reshape(n, GLA_CHUNK, GLA_DV).astype(BF16)
                dohs[h] = do_ref[:, vcols].reshape(n, GLA_CHUNK, GLA_DV).astype(BF16)
                stbs[h] = st_ref[:, h]
                sc_t[h] = _bnt(pv[pair]["kd"], qdh[h])
                dp[h] = _bnt(dohs[h], vhs[h])
                dp_t[h] = _bnt(vhs[h], dohs[h])
                acc[h] = _btn(dohs[h], qdh[h])
        dsa = {}
        for h in heads:
            sc_t[h] = jnp.where(causal_t, sc_t[h], 0.0).astype(BF16)
            dp[h] = jnp.where(causal, dp[h], 0.0).astype(BF16)
            dp_t[h] = jnp.where(causal_t, dp_t[h], 0.0).astype(BF16)
            dec = pv[h // 2]["dec"]
            c, after = carries[h], [None] * n
            for j in order:
                after[j] = c
                c = acc[h][j] + dec[j] * c
            carries[h] = c
            dsa[h] = jnp.stack(after)
        dvs, dqs, dks, dwds, dbds = [], [], [], [], []
        for pair in range(2):
            cols = pl.ds(pair * LANES, LANES)
            v = pv[pair]
            dqd = jnp.zeros((n, GLA_CHUNK, LANES), F32)
            dkd = jnp.zeros((n, GLA_CHUNK, LANES), F32)
            dkte = jnp.zeros((n, GLA_CHUNK, LANES), F32)
            ddec = jnp.zeros((n, 1, LANES), F32)
            for h in (2 * pair, 2 * pair + 1):
                dsa_bf = dsa[h].astype(BF16)
                dqd = dqd + (_bnn(dp[h], v["kd"]) * masks[h] + _bnn(dohs[h], stbs[h].astype(BF16)))
                dkd = dkd + _bnn(dp_t[h], qdh[h])
                dkte = dkte + _bnn(vhs[h], dsa_bf)
                ddec = ddec + jnp.sum(dsa[h] * stbs[h], axis=1, keepdims=True)
                dvs.append((_bnn(sc_t[h], dohs[h]) + _bnt(kteh[h], dsa_bf)).reshape(tg, GLA_DV))
            dqs.append((dqd * (scale * v["eb"])).reshape(tg, LANES))
            dks.append((dkd * v["emb"] + dkte * v["ekte"]).reshape(tg, LANES))
            db = dqd * v["qd"] - dkd * v["kdf"] - dkte * v["kte"]
            dblast = jnp.sum(dkte * v["kte"], axis=1, keepdims=True) + ddec * v["dec"]
            dla = _chunk_cumsum(db.reshape(tg, LANES), row_in_chunk, not reverse) + jnp.broadcast_to(dblast, (n, GLA_CHUNK, LANES)).reshape(tg, LANES)
            dpre = (dla * (1.0 / GLA_TAU) * _sigmoid(-v["pre"]))
            dpre_bf = dpre.astype(BF16)
            dlr = dlr + _nt(dpre_bf, wd_ref[:, cols])
            dwds.append(_tn(lr_bf, dpre_bf))
            dbds.append(jnp.sum(dpre, axis=0, keepdims=True))
        dlr_ref[...] = dlr
        for pair in range(2):
            cols = pl.ds(pair * LANES, LANES)
            dq_ref[:, cols] = dqs[pair]
            dk_ref[:, cols] = dks[pair]
            dwd_ref[:, cols] += dwds[pair]
            dbd_ref[:, cols] += dbds[pair]
        for h in range(GLA_HEADS):
            dv_ref[:, pl.ds(h * GLA_DV, GLA_DV)] = dvs[h]
            carry[h] = carries[h]

    return pl.pallas_call(
        body,
        name="gla_bwd_rev" if reverse else "gla_bwd",
        grid=(nt,),
        in_specs=[
            pl.BlockSpec((tg, KEY_W), lambda i: (tile(i), COL_Q // KEY_W)),
            pl.BlockSpec((tg, KEY_W), lambda i: (tile(i), COL_K // KEY_W)),
            pl.BlockSpec((tg, GLA_W), lambda i: (tile(i), COL_V // GLA_W)),
            pl.BlockSpec((tg, LANES), lambda i: (tile(i), COL_LR // LANES)),
            pl.BlockSpec((tg, GLA_W), lambda i: (tile(i), 0)),
            pl.BlockSpec((n, GLA_HEADS, GLA_DV, LANES), lambda i: (tile(i), 0, 0, 0)),
            _resident((LANES, KEY_W)),
            _resident((1, KEY_W)),
            _resident(TOKEN_SHAPE),
        ],
        out_specs=[
            pl.BlockSpec((tg, KEY_W), lambda i: (tile(i), 0)),
            pl.BlockSpec((tg, KEY_W), lambda i: (tile(i), 0)),
            pl.BlockSpec((tg, GLA_W), lambda i: (tile(i), 0)),
            pl.BlockSpec((tg, LANES), lambda i: (tile(i), 0)),
            pl.BlockSpec((LANES, KEY_W), lambda i: (0, 0)),
            pl.BlockSpec((1, KEY_W), lambda i: (0, 0)),
        ],
        out_shape=[
            jax.ShapeDtypeStruct((seq, KEY_W), F32), jax.ShapeDtypeStruct((seq, KEY_W), F32),
            jax.ShapeDtypeStruct((seq, GLA_W), F32), jax.ShapeDtypeStruct((seq, LANES), F32),
            jax.ShapeDtypeStruct((LANES, KEY_W), F32), jax.ShapeDtypeStruct((1, KEY_W), F32),
        ],
        scratch_shapes=[pltpu.VMEM((GLA_HEADS, GLA_DV, LANES), F32)],
        compiler_params=_params(48),
    )(p, p, p, p, do, st, wd_pad, bd, token)


def _inproj_wgrad(x, g1, dq_f, dq_b, dk_f, dk_b, dv_f, dv_b, dg, du, dvv, dlr_f, dlr_b):
    seq = x.shape[0]
    tm = min(seq, 512)

    def body(x_ref, g1_ref, dqf, dqb, dkf, dkb, dvf, dvb, dg_ref, du_ref, dvv_ref, dlrf, dlrb, dw_ref, dp_ref):
        @pl.when(pl.program_id(0) == 0)
        def _():
            dw_ref[...] = jnp.zeros_like(dw_ref)

        dp_ref[:, COL_Q : COL_Q + KEY_W] = (dqf[...] + dqb[...]).astype(BF16)
        dp_ref[:, COL_K : COL_K + KEY_W] = (dkf[...] + dkb[...]).astype(BF16)
        dp_ref[:, COL_V : COL_V + GLA_W] = (dvf[...] + dvb[...]).astype(BF16)
        dp_ref[:, COL_G : COL_G + GLA_W] = dg_ref[...].astype(BF16)
        dp_ref[:, COL_U : COL_U + GMLP_W] = du_ref[...].astype(BF16)
        dp_ref[:, COL_VV : COL_VV + GMLP_W] = dvv_ref[...].astype(BF16)
        dp_ref[:, COL_LR : COL_LR + LANES] = (dlrf[...] + dlrb[...]).astype(BF16)
        xv = x_ref[...]
        h = (xv * lax.rsqrt(jnp.mean(xv * xv, axis=-1, keepdims=True) + EPS) * g1_ref[...]).astype(BF16)
        dw_ref[0:ROW_LR, :] += _tn(dp_ref[:, 0:COL_U], h)
        dw_ref[ROW_UV:PROJ_W, :] += _tn(dp_ref[:, COL_U:COL_LR], h)
        dw_ref[ROW_LR:ROW_UV, :] += _tn(dp_ref[:, COL_LR:PROJ_WP], h)[0 : ROW_UV - ROW_LR]

    row = lambda w: pl.BlockSpec((tm, w), lambda i: (i, 0))
    return pl.pallas_call(
        body,
        name="inproj_wgrad",
        grid=(seq // tm,),
        in_specs=[
            row(D_MODEL), _resident((1, D_MODEL)),
            row(KEY_W), row(KEY_W), row(KEY_W), row(KEY_W), row(GLA_W), row(GLA_W),
            row(GLA_W), row(GMLP_W), row(GMLP_W), row(LANES), row(LANES),
        ],
        out_specs=[pl.BlockSpec((PROJ_W, D_MODEL), lambda i: (0, 0)), row(PROJ_WP)],
        out_shape=[jax.ShapeDtypeStruct((PROJ_W, D_MODEL), F32), jax.ShapeDtypeStruct((seq, PROJ_WP), BF16)],
        compiler_params=_params(56),
    )(x, g1, dq_f, dq_b, dk_f, dk_b, dv_f, dv_b, dg, du, dvv, dlr_f, dlr_b)


def _inproj_dx(x, dx1, g1, w_in_t, dp, token):
    seq = x.shape[0]
    tm = min(seq, 512)

    def body(x_ref, dx1_ref, g1_ref, w_ref, dp_ref, token_ref, dx_ref, dg1_ref):
        @pl.when(pl.program_id(0) == 0)
        def _():
            dg1_ref[...] = jnp.zeros_like(dg1_ref)

        xv = x_ref[...]
        r1 = lax.rsqrt(jnp.mean(xv * xv, axis=-1, keepdims=True) + EPS)
        xh = xv * r1
        dh = (_nn(dp_ref[:, 0:COL_U], w_ref[0:ROW_LR, :]) + _nn(dp_ref[:, COL_U:COL_LR], w_ref[ROW_UV:PROJ_W, :])
              + _nn(dp_ref[:, COL_LR:PROJ_WP], w_ref[ROW_LR : ROW_LR + LANES, :]))
        dg1_ref[...] += jnp.sum(dh * xh, axis=0, keepdims=True)
        dx_ref[...] = dx1_ref[...] + _rms_bwd(dh * g1_ref[...], xh, r1)

    row = lambda w: pl.BlockSpec((tm, w), lambda i: (i, 0))
    return pl.pallas_call(
        body,
        name="inproj_dx",
        grid=(seq // tm,),
        in_specs=[row(D_MODEL), row(D_MODEL), _resident((1, D_MODEL)), _resident((PROJ_W, D_MODEL)), row(PROJ_WP), _resident(TOKEN_SHAPE)],
        out_specs=[row(D_MODEL), pl.BlockSpec((1, D_MODEL), lambda i: (0, 0))],
        out_shape=[jax.ShapeDtypeStruct((seq, D_MODEL), F32), jax.ShapeDtypeStruct((1, D_MODEL), F32)],
        compiler_params=_params(48),
    )(x, dx1, g1, w_in_t, dp, token)


def _in_hbm(a):
    return pltpu.with_memory_space_constraint(a, pltpu.HBM)


def _row_tile(rows, multiple=8):
    for t in range(min(rows, 512), 0, -1):
        if rows % t == 0 and t % multiple == 0:
            return t
    return rows


def _cast_into_slot(w, shard):
    rows, cols = w.shape
    tr = _row_tile(rows, 16)

    def body(s_ref, w_ref, o_ref):
        o_ref[...] = w_ref[...].astype(BF16)

    return pl.pallas_call(
        body,
        name="cast_into_slot",
        grid_spec=pltpu.PrefetchScalarGridSpec(
            num_scalar_prefetch=1,
            grid=(rows // tr,),
            in_specs=[pl.BlockSpec((tr, cols), lambda i, s_ref: (i, 0))],
            out_specs=pl.BlockSpec((None, tr, cols), lambda i, s_ref: (s_ref[0], i, 0)),
        ),
        out_shape=pltpu.HBM((N_SHARDS, rows, cols), BF16),
        compiler_params=_params(32, ("parallel",)),
    )(shard, _in_hbm(w))


def _add_halves(grads4, recvs, c):
    n = len(grads4)
    _, rows, _ = grads4[0].shape
    tr = _row_tile(rows, 16)

    def body(c_ref, *refs):
        for k in range(n):
            total = refs[k][...] + refs[n + k][...]
            refs[2 * n + k][...] = total
            refs[3 * n + k][...] = total.astype(BF16)

    out = pl.BlockSpec((None, tr, HALF), lambda s, i, c_ref: (s, i, 0))
    mine = pl.BlockSpec((None, tr, HALF), lambda s, i, c_ref: (s, i, c_ref[0]))
    outs = pl.pallas_call(
        body,
        name="add_halves",
        grid_spec=pltpu.PrefetchScalarGridSpec(
            num_scalar_prefetch=1,
            grid=(N_SHARDS, rows // tr),
            in_specs=[mine] * n + [out] * n,
            out_specs=[out] * (2 * n),
        ),
        out_shape=[pltpu.HBM((N_SHARDS, rows, HALF), F32)] * n + [pltpu.HBM((N_SHARDS, rows, HALF), BF16)] * n,
        compiler_params=_params(48, ("parallel", "parallel")),
    )(c, *[_in_hbm(a) for a in list(grads4) + list(recvs)])
    return list(zip(outs[:n], outs[n:]))


def _add_partials(part4, recv3, shard_core):
    _, rows, _ = part4.shape
    tr = _row_tile(rows, 16)

    def body(sc_ref, p_ref, r_ref, o_ref):
        o_ref[...] = ((p_ref[...] + r_ref[0].astype(F32)) + r_ref[1].astype(F32)) + r_ref[2].astype(F32)

    return pl.pallas_call(
        body,
        name="add_partials",
        grid_spec=pltpu.PrefetchScalarGridSpec(
            num_scalar_prefetch=1,
            grid=(rows // tr,),
            in_specs=[
                pl.BlockSpec((None, tr, HALF), lambda i, sc_ref: (sc_ref[0], i, 0)),
                pl.BlockSpec((3, tr, HALF), lambda i, sc_ref: (0, i, 0)),
            ],
            out_specs=pl.BlockSpec((tr, HALF), lambda i, sc_ref: (i, sc_ref[1])),
        ),
        out_shape=pltpu.HBM((rows, 2 * HALF), F32),
        compiler_params=_params(32, ("parallel",)),
    )(shard_core, _in_hbm(part4), _in_hbm(recv3))


def _adam_math(w, g, m, v):
    m = ADAM_B1 * m + (1.0 - ADAM_B1) * g
    v = ADAM_B2 * v + (1.0 - ADAM_B2) * (g * g)
    m_hat = m / (1.0 - ADAM_B1**ADAM_STEP)
    v_hat = v / (1.0 - ADAM_B2**ADAM_STEP)
    delta = -ADAM_LR * (m_hat / (jnp.sqrt(v_hat) + ADAM_EPS) + ADAM_WD * w)
    return delta, m, v


def _adamw(ws, gs, ms, vs):
    n = len(ws)

    def body(*refs):
        for k in range(n):
            w_ref, g_ref, m_ref, v_ref = (refs[j * n + k] for j in range(4))
            go_ref, d_ref, mo_ref, vo_ref = (refs[(4 + j) * n + k] for j in range(4))
            gv = g_ref[...]
            go_ref[...] = gv
            d_ref[...], mo_ref[...], vo_ref[...] = _adam_math(w_ref[...], gv, m_ref[...], v_ref[...])

    specs = [pl.BlockSpec((w.shape[0], LANES), lambda i: (0, i)) for w in ws]
    shapes = [pltpu.HBM(w.shape, F32) for w in ws]
    outs = pl.pallas_call(
        body, name="adamw", grid=(D_MODEL // LANES,), in_specs=specs * 4, out_specs=specs * 4, out_shape=shapes * 4,
        compiler_params=_params(48, ("parallel",)),
    )(*[_in_hbm(a) for a in list(ws) + list(gs) + list(ms) + list(vs)])
    return [tuple(outs[j * n + k] for j in range(4)) for k in range(n)]


SMALL_ROWS = 560
DECAY_ROWS = 8
SMALL_TOTAL = SMALL_ROWS + 2 * N_SHARDS * DECAY_ROWS


def _adamw_small(gathered, wp, mp, vp):
    out_rows = SMALL_ROWS + 2 * DECAY_ROWS

    def body(ga_ref, w_ref, m_ref, v_ref, g_ref, d_ref, mo_ref, vo_ref):
        shard = 2 * lax.axis_index("x") + lax.axis_index("y")
        g_ref[pl.ds(0, SMALL_ROWS), :] = functools.reduce(lambda a, b: a + b, [ga_ref[d, pl.ds(0, SMALL_ROWS), :] for d in range(8)])
        for k in range(2):
            start = pl.multiple_of(SMALL_ROWS + k * N_SHARDS * DECAY_ROWS + shard * DECAY_ROWS, DECAY_ROWS)
            g_ref[pl.ds(SMALL_ROWS + k * DECAY_ROWS, DECAY_ROWS), :] = functools.reduce(
                lambda a, b: a + b, [ga_ref[d, pl.ds(start, DECAY_ROWS), :] for d in range(8)])
        d_ref[...], mo_ref[...], vo_ref[...] = _adam_math(w_ref[...], g_ref[...], m_ref[...], v_ref[...])

    shape = jax.ShapeDtypeStruct((out_rows, LANES), F32)
    return pl.pallas_call(body, name="adamw_small", out_shape=[shape] * 4, compiler_params=_params(32, None))(gathered, wp, mp, vp)


ANY = pl.BlockSpec(memory_space=pl.ANY)


def _position():
    return lax.axis_index("x"), lax.axis_index("y"), lax.axis_index("c")


def _other_chips(x, y):
    return [(1 - x, y), (x, 1 - y), (1 - x, 1 - y)]


HBM = pl.BlockSpec(memory_space=pltpu.HBM)
SEM = pl.BlockSpec(memory_space=pltpu.SEMAPHORE)
TOKEN = jax.ShapeDtypeStruct(TOKEN_SHAPE, F32)
DATAFLOW = pltpu.SideEffectType.DATAFLOW_SIDE_EFFECTING


def _half_block(ref4, slot, core):
    return ref4.at[slot, :, pl.ds(pl.multiple_of(core * HALF, HALF), HALF)]


def _gather_ici_copies(refs4, send_sems, recv_sems, stride):
    x, y, c = _position()
    pairs = []
    for k, ref4 in enumerate(refs4):
        mine = _half_block(ref4, 2 * x + y, c)
        for j, (px, py) in enumerate(_other_chips(x, y)):
            sems = dict(send_sem=send_sems.at[stride * k + j], recv_sem=recv_sems.at[stride * k + j], device_id=(px, py, c), device_id_type=MESH)
            pairs.append((functools.partial(pltpu.make_async_remote_copy, src_ref=mine, dst_ref=mine, **sems),
                          functools.partial(pltpu.make_async_remote_copy, src_ref=mine, dst_ref=_half_block(ref4, 2 * px + py, c), **sems)))
    return pairs


def _gather_d2d_copies(refs4, send_sems, recv_sems, stride, offset):
    x, y, c = _position()
    pairs = []
    for k, ref4 in enumerate(refs4):
        for j, (px, py) in enumerate(_other_chips(x, y)):
            have = _half_block(ref4, 2 * px + py, c)
            sems = dict(send_sem=send_sems.at[stride * k + offset + j], recv_sem=recv_sems.at[stride * k + offset + j],
                        device_id=(x, y, 1 - c), device_id_type=MESH)
            pairs.append((functools.partial(pltpu.make_async_remote_copy, src_ref=have, dst_ref=have, **sems),
                          functools.partial(pltpu.make_async_remote_copy, src_ref=have, dst_ref=_half_block(ref4, 2 * px + py, 1 - c), **sems)))
    return pairs


def _gather_sync(bufs):
    n = len(bufs)

    def body(*refs):
        outs = refs[n : 2 * n]
        send_sems, recv_sems = refs[2 * n :]
        ici = _gather_ici_copies(outs, send_sems, recv_sems, 6)
        d2d = _gather_d2d_copies(outs, send_sems, recv_sems, 6, 3)
        for send, _ in ici:
            send().start()
        for (_, arrival), (forward, _) in zip(ici, d2d):
            arrival().wait_recv()
            forward().start()
        for _, arrival in d2d:
            arrival().wait_recv()
        for send, _ in ici + d2d:
            send().wait_send()

    return pl.pallas_call(
        body,
        name="gather_sync",
        in_specs=[ANY] * n,
        out_specs=[ANY] * n,
        out_shape=[jax.ShapeDtypeStruct(b.shape, b.dtype) for b in bufs],
        input_output_aliases={k: k for k in range(n)},
        scratch_shapes=[pltpu.SemaphoreType.DMA((6 * n,)), pltpu.SemaphoreType.DMA((6 * n,))],
        compiler_params=pltpu.CompilerParams(has_side_effects=True),
    )(*bufs)


def _gather_start(bufs, after):
    n, na = len(bufs), len(after)

    def body(*refs):
        ins = refs[:n]
        send_sems, recv_sems = refs[n + na], refs[n + na + 1]
        token = refs[2 * n + na + 2]
        for send, _ in _gather_ici_copies(ins, send_sems, recv_sems, 3):
            send().start()
        token[...] = jnp.zeros_like(token)

    out = pl.pallas_call(
        body,
        name="gather_start",
        in_specs=[HBM] * n + [ANY] * na,
        out_specs=(SEM, SEM, *[HBM] * n, pl.BlockSpec(memory_space=pltpu.VMEM)),
        out_shape=(pltpu.SemaphoreType.DMA((3 * n,)), pltpu.SemaphoreType.DMA((3 * n,)), *[pltpu.HBM(b.shape, b.dtype) for b in bufs], TOKEN),
        input_output_aliases={k: 2 + k for k in range(n)},
        compiler_params=pltpu.CompilerParams(has_side_effects=DATAFLOW),
    )(*[pltpu.with_memory_space_constraint(b, pltpu.HBM) for b in bufs], *after)
    return out[0], out[1], list(out[2 : 2 + n]), out[2 + n]


def _gather_wait(send_sems, recv_sems, bufs, after):
    n = len(bufs)

    def body(*refs):
        ins = refs[:n]
        for send, arrival in _gather_ici_copies(ins, refs[n], refs[n + 1], 3):
            send().wait_send()
            arrival().wait_recv()

    return pl.pallas_call(
        body,
        name="gather_wait",
        in_specs=[HBM] * n + [SEM, SEM] + [ANY] * len(after),
        out_specs=tuple([HBM] * n),
        out_shape=tuple(pltpu.HBM(b.shape, b.dtype) for b in bufs),
        input_output_aliases={k: k for k in range(n)},
        compiler_params=pltpu.CompilerParams(has_side_effects=DATAFLOW),
    )(*bufs, send_sems, recv_sems, *after)


def _gather_forward(bufs):
    n = len(bufs)

    def body(*refs):
        outs = refs[n : 2 * n]
        send_sems, recv_sems = refs[2 * n :]
        d2d = _gather_d2d_copies(outs, send_sems, recv_sems, 3, 0)
        for forward, _ in d2d:
            forward().start()
        for forward, arrival in d2d:
            arrival().wait_recv()
            forward().wait_send()

    return pl.pallas_call(
        body,
        name="gather_forward",
        in_specs=[ANY] * n,
        out_specs=[ANY] * n,
        out_shape=[jax.ShapeDtypeStruct(b.shape, b.dtype) for b in bufs],
        input_output_aliases={k: k for k in range(n)},
        scratch_shapes=[pltpu.SemaphoreType.DMA((3 * n,)), pltpu.SemaphoreType.DMA((3 * n,))],
        compiler_params=pltpu.CompilerParams(has_side_effects=True),
    )(*bufs)


def _exchange_halves(grads4):
    n = len(grads4)

    def body(*refs):
        ins, outs = refs[:n], refs[n : 2 * n]
        send_sems, recv_sems = refs[2 * n :]
        x, y, c = _position()
        copies = []
        for k in range(n):
            cp = pltpu.make_async_remote_copy(
                src_ref=ins[k].at[:, :, pl.ds(pl.multiple_of((1 - c) * HALF, HALF), HALF)], dst_ref=outs[k],
                send_sem=send_sems.at[k], recv_sem=recv_sems.at[k], device_id=(x, y, 1 - c), device_id_type=MESH)
            cp.start()
            copies.append(cp)
        for cp in copies:
            cp.wait()

    return pl.pallas_call(
        body,
        name="exchange_halves",
        in_specs=[ANY] * n,
        out_specs=[ANY] * n,
        out_shape=[jax.ShapeDtypeStruct((N_SHARDS, g.shape[1], HALF), g.dtype) for g in grads4],
        scratch_shapes=[pltpu.SemaphoreType.DMA((n,)), pltpu.SemaphoreType.DMA((n,))],
        compiler_params=pltpu.CompilerParams(has_side_effects=True),
    )(*grads4)


def _scatter_copies(parts, lands, send_sems, recv_sems):
    x, y, c = _position()
    copies = []
    for k in range(len(parts)):
        for j, (px, py) in enumerate(_other_chips(x, y)):
            copies.append(pltpu.make_async_remote_copy(
                src_ref=parts[k].at[2 * px + py], dst_ref=lands[k].at[j],
                send_sem=send_sems.at[3 * k + j], recv_sem=recv_sems.at[3 * k + j], device_id=(px, py, c), device_id_type=MESH))
    return copies


def _exchange_copies(grads, lands, send_sems, recv_sems):
    x, y, c = _position()
    return [pltpu.make_async_remote_copy(
        src_ref=grads[k].at[:, :, pl.ds(pl.multiple_of((1 - c) * HALF, HALF), HALF)], dst_ref=lands[k],
        send_sem=send_sems.at[k], recv_sem=recv_sems.at[k], device_id=(x, y, 1 - c), device_id_type=MESH) for k in range(len(grads))]


def _exchange_lands(grads4):
    return [jax.ShapeDtypeStruct((N_SHARDS, g.shape[1], HALF), g.dtype) for g in grads4]


def _scatter_lands(parts4):
    return [jax.ShapeDtypeStruct((3,) + g.shape[1:], g.dtype) for g in parts4]


def _split_start(name, srcs, land_shapes, make_copies, nsem):
    n, nl = len(srcs), len(land_shapes)
    lands = [lax.empty(a.shape, a.dtype) for a in land_shapes]

    def body(*refs):
        send_sems, recv_sems = refs[n + nl], refs[n + nl + 1]
        token = refs[2 * (n + nl) + 2]
        for cp in make_copies(refs[:n], refs[n : n + nl], send_sems, recv_sems):
            cp.start()
        token[...] = jnp.zeros_like(token)

    hbm = lambda a: pltpu.HBM(a.shape, a.dtype)
    out = pl.pallas_call(
        body,
        name=name,
        in_specs=[HBM] * (n + nl),
        out_specs=(SEM, SEM, *[HBM] * (n + nl), pl.BlockSpec(memory_space=pltpu.VMEM)),
        out_shape=(pltpu.SemaphoreType.DMA((nsem,)), pltpu.SemaphoreType.DMA((nsem,)), *[hbm(a) for a in srcs + lands], TOKEN),
        input_output_aliases={k: 2 + k for k in range(n + nl)},
        compiler_params=pltpu.CompilerParams(has_side_effects=DATAFLOW),
    )(*[pltpu.with_memory_space_constraint(a, pltpu.HBM) for a in srcs + lands])
    return out[0], out[1], list(out[2 : 2 + n]), list(out[2 + n : 2 + n + nl]), out[2 + n + nl]


def _split_wait(name, send_sems, recv_sems, srcs, lands, make_copies, after):
    n, nl = len(srcs), len(lands)

    def body(*refs):
        for cp in make_copies(refs[:n], refs[n : n + nl], refs[n + nl], refs[n + nl + 1]):
            cp.wait_send()
            cp.wait_recv()

    hbm = lambda a: pltpu.HBM(a.shape, a.dtype)
    out = pl.pallas_call(
        body,
        name=name,
        in_specs=[HBM] * (n + nl) + [SEM, SEM] + [ANY] * len(after),
        out_specs=tuple([HBM] * (n + nl)),
        out_shape=tuple(hbm(a) for a in srcs + lands),
        input_output_aliases={k: k for k in range(n + nl)},
        compiler_params=pltpu.CompilerParams(has_side_effects=DATAFLOW),
    )(*srcs, *lands, send_sems, recv_sems, *after)
    return list(out[:n]), list(out[n:])


def _join_halves(bufs):
    n = len(bufs)

    def body(*refs):
        outs = refs[n : 2 * n]
        send_sems, recv_sems = refs[2 * n :]
        x, y, c = _position()
        half = lambda ref, core: ref.at[:, pl.ds(pl.multiple_of(core * HALF, HALF), HALF)]
        for k in range(n):
            mine = half(outs[k], c)
            pltpu.make_async_remote_copy(
                src_ref=mine, dst_ref=mine, send_sem=send_sems.at[k], recv_sem=recv_sems.at[k],
                device_id=(x, y, 1 - c), device_id_type=MESH).start()
        for k in range(n):
            wait = pltpu.make_async_remote_copy(
                src_ref=half(outs[k], c), dst_ref=half(outs[k], 1 - c), send_sem=send_sems.at[k], recv_sem=recv_sems.at[k],
                device_id=(x, y, 1 - c), device_id_type=MESH)
            wait.wait_send()
            wait.wait_recv()

    return pl.pallas_call(
        body,
        name="join_halves",
        in_specs=[ANY] * n,
        out_specs=[ANY] * n,
        out_shape=[jax.ShapeDtypeStruct(b.shape, b.dtype) for b in bufs],
        input_output_aliases={k: k for k in range(n)},
        scratch_shapes=[pltpu.SemaphoreType.DMA((n,)), pltpu.SemaphoreType.DMA((n,))],
        compiler_params=pltpu.CompilerParams(has_side_effects=True),
    )(*bufs)


def _allgather_small(block):
    m_per, ncol = block.shape

    def body(x_ref, out_ref, send_sems, recv_sems, local_sem):
        x, y, c = _position()
        me, sibling = (x, y, c), (x, y, 1 - c)
        chips = _other_chips(x, y)

        def rows(px, py, pc):
            return out_ref.at[4 * px + 2 * py + pc]

        def copy(k, blk, to, src=None):
            return pltpu.make_async_remote_copy(
                src_ref=rows(*blk) if src is None else src, dst_ref=rows(*blk),
                send_sem=send_sems.at[k], recv_sem=recv_sems.at[k], device_id=to, device_id_type=MESH)

        mine = pltpu.make_async_copy(x_ref, rows(*me), local_sem)
        mine.start()
        first = [copy(0, me, sibling, src=x_ref)] + [copy(1 + j, me, (*chip, c), src=x_ref) for j, chip in enumerate(chips)]
        for cp in first:
            cp.start()
        passed = [copy(4 + j, (*chip, c), sibling) for j, chip in enumerate(chips)]
        for j, chip in enumerate(chips):
            copy(1 + j, (*chip, c), me).wait_recv()
            passed[j].start()
        copy(0, sibling, me).wait_recv()
        for j, chip in enumerate(chips):
            copy(4 + j, (*chip, 1 - c), me).wait_recv()
        for cp in first + passed:
            cp.wait_send()
        mine.wait()

    return pl.pallas_call(
        body,
        name="allgather_small",
        in_specs=[pl.BlockSpec(memory_space=pltpu.VMEM)],
        out_specs=pl.BlockSpec(memory_space=pltpu.VMEM),
        out_shape=jax.ShapeDtypeStruct((8, m_per, ncol), block.dtype),
        scratch_shapes=[pltpu.SemaphoreType.DMA((7,)), pltpu.SemaphoreType.DMA((7,)), pltpu.SemaphoreType.DMA],
        compiler_params=pltpu.CompilerParams(has_side_effects=True, vmem_limit_bytes=32 * MIB),
    )(block)


SMALL_NAMES = ["norm1_g", "b_decay_f", "b_decay_b", "gla_norm_g", "gmlp_ln_g", "gmlp_ln_b", "w_spatial", "b_spatial", "norm2_g", "final_norm_g"]


def _pack_small(parts, decay_parts):
    flat = jnp.concatenate([a.reshape(-1) for a in parts])
    flat = jnp.pad(flat, (0, SMALL_ROWS * LANES - flat.shape[0])).reshape(SMALL_ROWS, LANES)
    return jnp.concatenate([flat] + [d.reshape(-1, LANES) for d in decay_parts], axis=0)


def _unpack_small(packed, like):
    out, off = [], 0
    flat = packed[:SMALL_ROWS].reshape(-1)
    for a in like:
        out.append(flat[off : off + a.size].reshape(a.shape))
        off += a.size
    return out


def kernel(x, norm1_g, w_in, w_decay_f, b_decay_f, w_decay_b, b_decay_b, gla_norm_g, gmlp_ln_g, gmlp_ln_b, w_spatial, b_spatial, w_out, norm2_g, w_gate, w_up, w_down, final_norm_g, loss_target, m_norm1_g, m_w_in, m_w_decay_f, m_b_decay_f, m_w_decay_b, m_b_decay_b, m_gla_norm_g, m_gmlp_ln_g, m_gmlp_ln_b, m_w_spatial, m_b_spatial, m_w_out, m_norm2_g, m_w_gate, m_w_up, m_w_down, m_final_norm_g, v_norm1_g, v_w_in, v_w_decay_f, v_b_decay_f, v_w_decay_b, v_b_decay_b, v_gla_norm_g, v_gmlp_ln_g, v_gmlp_ln_b, v_w_spatial, v_b_spatial, v_w_out, v_norm2_g, v_w_gate, v_w_up, v_w_down, v_final_norm_g):
    args = dict(locals())
    cx, cy, cc = lax.axis_index("x"), lax.axis_index("y"), lax.axis_index("c")
    shard = 2 * cx + cy
    xs = x[0]
    target = loss_target[0]

    big_names = ["w_in", "w_out", "w_gate", "w_up", "w_down"]
    transposed = ("w_in", "w_gate", "w_up")
    rows_of = lambda pre, k: jnp.transpose(args[pre + k][0]) if k in transposed else args[pre + k][0]
    big_shards = {k: rows_of("", k) for k in big_names}
    c_arr = cc.reshape(1).astype(jnp.int32)
    s_arr = shard.reshape(1).astype(jnp.int32)
    sc_arr = jnp.stack([shard, cc]).astype(jnp.int32)
    slots = {k: _cast_into_slot(big_shards[k], s_arr) for k in big_names}
    (w_in4,) = _gather_sync([slots["w_in"]])
    w_in_t = w_in4.reshape(PROJ_W, D_MODEL)

    dec_block = jnp.concatenate([w_decay_f[0].reshape(-1, LANES), w_decay_b[0].reshape(-1, LANES)], axis=0)
    dec_all = _allgather_small(dec_block)
    late = ["w_out", "w_gate", "w_up", "w_down"]
    g_send, g_recv, late_bufs, token_gather = _gather_start([slots[k] for k in late], (w_in4, dec_all))
    dec_all = dec_all[::2].reshape(N_SHARDS, 2, LOWRANK, KEY_W // N_SHARDS)
    wdf_full = jnp.transpose(dec_all[:, 0], (1, 0, 2)).reshape(LOWRANK, KEY_W)
    wdb_full = jnp.transpose(dec_all[:, 1], (1, 0, 2)).reshape(LOWRANK, KEY_W)
    wd_pad_f = jnp.zeros((LANES, KEY_W), F32).at[0:LOWRANK].set(wdf_full).astype(BF16)
    wd_pad_b = jnp.zeros((LANES, KEY_W), F32).at[LOWRANK : 2 * LOWRANK].set(wdb_full).astype(BF16)

    ws_bf = w_spatial[0].astype(BF16)
    wst_bf = jnp.transpose(w_spatial[0], (0, 2, 1)).astype(BF16)
    bs_col = b_spatial[0].reshape(GMLP_GROUPS, GMLP_CHUNK, 1)

    p = _inproj(xs, norm1_g, w_in_t, token_gather)
    o_f, st_f = _gla_fwd(p, wd_pad_f, b_decay_f, reverse=False)
    o_b, st_b = _gla_fwd(p, wd_pad_b, b_decay_b, reverse=True)
    late_bufs = _gather_forward(_gather_wait(g_send, g_recv, late_bufs, (o_f, o_b)))
    w_out_full, wg_t, wu_t, wd = [b.reshape(-1, D_MODEL) for b in late_bufs]
    x1, ycat = _mixer_out(xs, o_f, o_b, p, gla_norm_g, gmlp_ln_g, gmlp_ln_b, ws_bf, bs_col, w_out_full)
    gf = final_norm_g.reshape(1, D_MODEL)
    h2, gate, up, act, dx2, loss_acc, dgf = _ffn_fwd(x1, target, norm2_g, gf, wg_t, wu_t, wd)

    dgate, dup, dx1, dg2 = _ffn_bwd(dx2, gate, up, x1, norm2_g, wg_t, wu_t, wd)
    ffn_grads = [_ffn_wgrad(dgate, h2), _ffn_wgrad(dup, h2), _ffn_wgrad(act, dx2)]
    ffn_grads4 = [g.reshape(N_SHARDS, FF_SHARD, D_MODEL) for g in ffn_grads]
    e_send, e_recv, e_srcs, e_lands, token_exchange = _split_start(
        "exchange_start", ffn_grads4, _exchange_lands(ffn_grads4), _exchange_copies, len(ffn_grads4))
    do, dg, du, dvv, dwo, dgn, dlng, dlnb, dws, dbs = _mixer_bwd(
        dx1, ycat, o_f, o_b, p, gla_norm_g, gmlp_ln_g, gmlp_ln_b, ws_bf, wst_bf, bs_col, w_out_full, token_exchange)
    ffn_mine, ffn_other = _split_wait("exchange_wait", e_send, e_recv, e_srcs, e_lands, _exchange_copies, (do,))
    ffn_parts = _add_halves(ffn_mine, ffn_other, c_arr)
    ffn_payload = [pb for _, pb in ffn_parts]
    s_send, s_recv, s_parts, s_lands, token_scatter = _split_start(
        "scatter_start", ffn_payload, _scatter_lands(ffn_payload), _scatter_copies, 3 * len(ffn_payload))
    dq_f, dk_f, dv_f, dlr_f, dwdec_f, dbdec_f = _gla_bwd(p, do, st_f, wd_pad_f, b_decay_f, token_scatter, reverse=False)
    dq_b, dk_b, dv_b, dlr_b, dwdec_b, dbdec_b = _gla_bwd(p, do, st_b, wd_pad_b, b_decay_b, token_scatter, reverse=True)
    dwin_t, dp = _inproj_wgrad(xs, norm1_g, dq_f, dq_b, dk_f, dk_b, dv_f, dv_b, dg, du, dvv, dlr_f, dlr_b)
    _, ffn_recv = _split_wait("scatter_wait", s_send, s_recv, s_parts, s_lands, _scatter_copies, (dwin_t,))

    dwin4 = dwin_t.reshape(N_SHARDS, PROJ_W // N_SHARDS, D_MODEL)
    dwo4 = dwo.reshape(N_SHARDS, D_MODEL // N_SHARDS, D_MODEL)
    proj_grads4 = [dwin4, dwo4]
    proj_parts = [_add_halves([g], [r], c_arr)[0] for g, r in zip(proj_grads4, _exchange_halves(proj_grads4))]
    proj_payload = [pb for _, pb in proj_parts]
    p_send, p_recv, p_parts, p_lands, token_proj = _split_start(
        "proj_scatter_start", proj_payload, _scatter_lands(proj_payload), _scatter_copies, 3 * len(proj_payload))
    dx, dg1 = _inproj_dx(xs, dx1, norm1_g, w_in_t, dp, token_proj)
    _, proj_recv = _split_wait("proj_scatter_wait", p_send, p_recv, p_parts, p_lands, _scatter_copies, (dx,))
    parts_f32 = [pf for pf, _ in proj_parts + ffn_parts]
    bufs = [_add_partials(pf, r, sc_arr) for pf, r in zip(parts_f32, proj_recv + ffn_recv)]
    big_grads = dict(zip(big_names, _join_halves(bufs)))

    dwdec_f16 = dwdec_f[0:LOWRANK]
    dwdec_b16 = dwdec_b[LOWRANK : 2 * LOWRANK]
    shard_major = lambda a: jnp.transpose(a.reshape(LOWRANK, N_SHARDS, KEY_W // N_SHARDS), (1, 0, 2))
    small_grads = {
        "norm1_g": dg1, "b_decay_f": dbdec_f, "b_decay_b": dbdec_b, "gla_norm_g": dgn, "gmlp_ln_g": dlng, "gmlp_ln_b": dlnb,
        "w_spatial": dws, "b_spatial": dbs, "norm2_g": dg2, "final_norm_g": dgf,
    }
    g_pack = _pack_small([small_grads[k] for k in SMALL_NAMES] + [loss_acc], [shard_major(dwdec_f16), shard_major(dwdec_b16)])
    g_all = _allgather_small(g_pack)
    pack_own = lambda pre: _pack_small([args[pre + k] for k in SMALL_NAMES], [args[pre + "w_decay_f"], args[pre + "w_decay_b"]])
    sg, sd, sm, sv = _adamw_small(g_all, pack_own(""), pack_own("m_"), pack_own("v_"))

    names = ["norm1_g", "w_in", "w_decay_f", "b_decay_f", "w_decay_b", "b_decay_b", "gla_norm_g", "gmlp_ln_g", "gmlp_ln_b",
             "w_spatial", "b_spatial", "w_out", "norm2_g", "w_gate", "w_up", "w_down", "final_norm_g"]
    like = [args[k] for k in SMALL_NAMES]
    results = {"g": {}, "d": {}, "m": {}, "v": {}}
    for tag, packed in (("g", sg), ("d", sd), ("m", sm), ("v", sv)):
        for k, a in zip(SMALL_NAMES, _unpack_small(packed, like)):
            results[tag][k] = a
        results[tag]["w_decay_f"] = packed[SMALL_ROWS : SMALL_ROWS + DECAY_ROWS].reshape(w_decay_f.shape)
        results[tag]["w_decay_b"] = packed[SMALL_ROWS + DECAY_ROWS :].reshape(w_decay_b.shape)
    updates = _adamw([big_shards[k] for k in big_names], [big_grads[k] for k in big_names],
                     [rows_of("m_", k) for k in big_names], [rows_of("v_", k) for k in big_names])
    for k, (g, d, mo, vo) in zip(big_names, updates):
        for tag, a in (("g", g), ("d", d), ("m", mo), ("v", vo)):
            results[tag][k] = (jnp.transpose(a) if k in transposed else a).reshape(args[k].shape)

    loss = sg[:SMALL_ROWS].reshape(-1)[sum(a.size for a in like)]
    grad_x = dx.reshape(x.shape)
    return (loss, grad_x, *[results["g"][k] for k in names], *[results["d"][k] for k in names],
            *[results["m"][k] for k in names], *[results["v"][k] for k in names])
```

```python
import functools
import math

import jax
import jax.numpy as jnp
from jax import lax
from jax.experimental import pallas as pl
from jax.experimental.pallas import tpu as pltpu

F32, BF16 = jnp.float32, jnp.bfloat16

D_MODEL = 1024
GLA_HEADS = 4
GLA_DK = 64
GLA_DV = 128
KEY_W = GLA_HEADS * GLA_DK
GLA_W = GLA_HEADS * GLA_DV
GMLP_W = 512
GMLP_GROUPS = 4
GMLP_CHUNK = 128
LOWRANK = 16
GLA_CHUNK = 64
GLA_TAU = 16.0
PROJ_W = 2592
PROJ_WP = 2688
D_FF = 2816
N_SHARDS = 4
FF_SHARD = D_FF // N_SHARDS
EPS = 1e-6
LANES = 128
TOKEN_SHAPE = (8, LANES)
MIB = 1024 * 1024

ADAM_LR = 0.001
ADAM_B1 = 0.9
ADAM_B2 = 0.999
ADAM_EPS = 1e-08
ADAM_WD = 0.01
ADAM_STEP = 10

COL_Q, COL_K = 0, 256
COL_V, COL_G, COL_U, COL_VV = 512, 1024, 1536, 2048
COL_LR = 2560
ROW_LR, ROW_UV = 1536, 1568
HALF = D_MODEL // 2

MESH = pl.DeviceIdType.MESH


def _nn(a, b):
    return jnp.dot(a, b, preferred_element_type=F32)


def _nt(a, b):
    return lax.dot_general(a, b, (((1,), (1,)), ((), ())), preferred_element_type=F32)


def _tn(a, b):
    return lax.dot_general(a, b, (((0,), (0,)), ((), ())), preferred_element_type=F32)


def _bnn(a, b):
    return jnp.einsum("nik,nkj->nij", a, b, preferred_element_type=F32)


def _bnt(a, b):
    return jnp.einsum("nik,njk->nij", a, b, preferred_element_type=F32)


def _btn(a, b):
    return jnp.einsum("nki,nkj->nij", a, b, preferred_element_type=F32)


def _resident(shape):
    zeros = (0,) * len(shape)
    return pl.BlockSpec(shape, lambda *_: zeros, pipeline_mode=pl.Buffered(1))


def _params(vmem_mib, semantics=("arbitrary",)):
    return pltpu.CompilerParams(vmem_limit_bytes=vmem_mib * MIB, dimension_semantics=semantics)


def _sigmoid(x):
    return 1.0 / (1.0 + jnp.exp(-x))


def _gelu(x):
    return 0.5 * x * (1.0 + lax.erf(x * (1.0 / math.sqrt(2.0))))


def _gelu_and_grad(x):
    cdf = 0.5 * (1.0 + lax.erf(x * (1.0 / math.sqrt(2.0))))
    return x * cdf, cdf + x * jnp.exp(-0.5 * x * x) * (1.0 / math.sqrt(2.0 * math.pi))


def _log_sigmoid(x):
    return jnp.minimum(x, 0.0) - jnp.log(1.0 + jnp.exp(-jnp.abs(x)))


def _rms_bwd(dxh, xh, r):
    return r * (dxh - xh * jnp.mean(dxh * xh, axis=-1, keepdims=True))


def _chunk_cumsum(v, row_in_chunk, reverse):
    rows = v.shape[0]
    for sh in (1, 2, 4, 8, 16, 32):
        if reverse:
            v = v + jnp.where(row_in_chunk + sh < GLA_CHUNK, pltpu.roll(v, rows - sh, axis=0), 0.0)
        else:
            v = v + jnp.where(row_in_chunk >= sh, pltpu.roll(v, sh, axis=0), 0.0)
    return v


def _inproj(x, g1, w_in_t, token):
    seq = x.shape[0]
    tm = min(seq, 512)

    def body(x_ref, g_ref, w_ref, token_ref, p_ref):
        xv = x_ref[...]
        r = lax.rsqrt(jnp.mean(xv * xv, axis=-1, keepdims=True) + EPS)
        h = (xv * r * g_ref[...]).astype(BF16)
        p_ref[:, 0:COL_U] = _nt(h, w_ref[0:ROW_LR, :])
        p_ref[:, COL_U:COL_LR] = _nt(h, w_ref[ROW_UV:PROJ_W, :])
        p_ref[:, COL_LR:PROJ_WP] = _nt(h, w_ref[ROW_LR : ROW_LR + LANES, :])

    return pl.pallas_call(
        body,
        name="inproj",
        grid=(seq // tm,),
        in_specs=[pl.BlockSpec((tm, D_MODEL), lambda i: (i, 0)), _resident((1, D_MODEL)), _resident((PROJ_W, D_MODEL)), _resident(TOKEN_SHAPE)],
        out_specs=pl.BlockSpec((tm, PROJ_WP), lambda i: (i, 0)),
        out_shape=jax.ShapeDtypeStruct((seq, PROJ_WP), F32),
        compiler_params=_params(48, ("parallel",)),
    )(x, g1, w_in_t, token)


def _gla_tile(seq):
    return min(seq, 1024)


def _gla_decay_terms(lr_bf, wd_ref, bd_ref, pair, row_in_chunk, reverse, n):
    cols = pl.ds(pair * LANES, LANES)
    pre = _nn(lr_bf, wd_ref[:, cols]) + bd_ref[:, cols]
    la = _log_sigmoid(pre) * (1.0 / GLA_TAU)
    b = _chunk_cumsum(la, row_in_chunk, reverse)
    b3 = b.reshape(n, GLA_CHUNK, LANES)
    blast = b3[:, 0:1, :] if reverse else b3[:, GLA_CHUNK - 1 : GLA_CHUNK, :]
    return pre, b3, blast


def _gla_fwd(p, wd_pad, bd, reverse):
    seq = p.shape[0]
    tg = _gla_tile(seq)
    nt = seq // tg
    n = tg // GLA_CHUNK
    scale = GLA_DK**-0.5

    def tile(i):
        return nt - 1 - i if reverse else i

    def body(q_ref, k_ref, v_ref, lr_ref, wd_ref, bd_ref, o_ref, st_ref, carry):
        @pl.when(pl.program_id(0) == 0)
        def _():
            carry[...] = jnp.zeros_like(carry)

        lr_bf = lr_ref[...].astype(BF16)
        states = [carry[h] for h in range(GLA_HEADS)]
        row_in_chunk = lax.broadcasted_iota(jnp.int32, (tg, LANES), 0) % GLA_CHUNK
        lane_head = lax.broadcasted_iota(jnp.int32, (1, LANES), 1) // GLA_DK
        tt = lax.broadcasted_iota(jnp.int32, (GLA_CHUNK, GLA_CHUNK), 0)
        ss = lax.broadcasted_iota(jnp.int32, (GLA_CHUNK, GLA_CHUNK), 1)
        causal = (tt <= ss) if reverse else (tt >= ss)
        order = range(n - 1, -1, -1) if reverse else range(n)
        heads = range(GLA_HEADS)
        qdh, kds, vhs, decs, sc_raw, dst = {}, {}, {}, {}, {}, {}
        for pair in range(2):
            cols = pl.ds(pair * LANES, LANES)
            _, b3, blast = _gla_decay_terms(lr_bf, wd_ref, bd_ref, pair, row_in_chunk, reverse, n)
            q3 = q_ref[:, cols].reshape(n, GLA_CHUNK, LANES) * scale
            k3 = k_ref[:, cols].reshape(n, GLA_CHUNK, LANES)
            qd = q3 * jnp.exp(b3)
            kd = (k3 * jnp.exp(-b3)).astype(BF16)
            kte = k3 * jnp.exp(blast - b3)
            dec = jnp.exp(blast)
            for hh in range(2):
                h = 2 * pair + hh
                m = (lane_head == hh).astype(F32)
                qdh[h], kds[h], decs[h] = (qd * m).astype(BF16), kd, dec
                vhs[h] = v_ref[:, pl.ds(h * GLA_DV, GLA_DV)].reshape(n, GLA_CHUNK, GLA_DV).astype(BF16)
                sc_raw[h] = _bnt(qdh[h], kd)
                dst[h] = _btn(vhs[h], (kte * m).astype(BF16))
        o_intra, befores = {}, {}
        for h in heads:
            o_intra[h] = _bnn(jnp.where(causal, sc_raw[h], 0.0).astype(BF16), vhs[h])
            st, before = states[h], [None] * n
            for j in order:
                before[j] = st
                st = st * decs[h][j] + dst[h][j]
            states[h] = st
            befores[h] = jnp.stack(before)
        outs = {h: (o_intra[h] + _bnt(qdh[h], befores[h].astype(BF16))).reshape(tg, GLA_DV) for h in heads}
        for h in range(GLA_HEADS):
            o_ref[:, pl.ds(h * GLA_DV, GLA_DV)] = outs[h]
            st_ref[:, h] = befores[h]
            carry[h] = states[h]

    nchunks = seq // GLA_CHUNK
    return pl.pallas_call(
        body,
        name="gla_fwd_rev" if reverse else "gla_fwd",
        grid=(nt,),
        in_specs=[
            pl.BlockSpec((tg, KEY_W), lambda i: (tile(i), COL_Q // KEY_W)),
            pl.BlockSpec((tg, KEY_W), lambda i: (tile(i), COL_K // KEY_W)),
            pl.BlockSpec((tg, GLA_W), lambda i: (tile(i), COL_V // GLA_W)),
            pl.BlockSpec((tg, LANES), lambda i: (tile(i), COL_LR // LANES)),
            _resident((LANES, KEY_W)),
            _resident((1, KEY_W)),
        ],
        out_specs=[
            pl.BlockSpec((tg, GLA_W), lambda i: (tile(i), 0)),
            pl.BlockSpec((n, GLA_HEADS, GLA_DV, LANES), lambda i: (tile(i), 0, 0, 0)),
        ],
        out_shape=[
            jax.ShapeDtypeStruct((seq, GLA_W), F32),
            jax.ShapeDtypeStruct((nchunks, GLA_HEADS, GLA_DV, LANES), F32),
        ],
        scratch_shapes=[pltpu.VMEM((GLA_HEADS, GLA_DV, LANES), F32)],
        compiler_params=_params(48),
    )(p, p, p, p, wd_pad, bd)


def _mixer_out(x, o_f, o_b, p, gn, lng, lnb, ws_bf, bs_col, w_out):
    seq = x.shape[0]
    tm = min(seq, 512)

    def body(x_ref, of_ref, ob_ref, g_ref, u_ref, vv_ref, gn_ref, lng_ref, lnb_ref, ws_ref, bs_ref, wo_ref, x1_ref, yc_ref, vn_sc):
        for h in range(GLA_HEADS):
            cols = pl.ds(h * GLA_DV, GLA_DV)
            oh = of_ref[:, cols] + ob_ref[:, cols]
            on = oh * lax.rsqrt(jnp.mean(oh * oh, axis=-1, keepdims=True) + EPS)
            gh = g_ref[:, cols]
            yc_ref[:, cols] = (on * gn_ref[:, cols] * (gh * _sigmoid(gh))).astype(BF16)
        zv = _gelu(vv_ref[...])
        xc = zv - jnp.mean(zv, axis=-1, keepdims=True)
        vhat = xc * lax.rsqrt(jnp.mean(xc * xc, axis=-1, keepdims=True) + EPS)
        vn_sc[...] = (vhat * lng_ref[...] + lnb_ref[...]).astype(BF16)
        for c in range(tm // GMLP_CHUNK):
            rows = pl.ds(c * GMLP_CHUNK, GMLP_CHUNK)
            for g in range(GMLP_GROUPS):
                cols = pl.ds(g * LANES, LANES)
                s = _nn(ws_ref[g], vn_sc[rows, cols]) + bs_ref[g]
                yc_ref[rows, pl.ds(GLA_W + g * LANES, LANES)] = (_gelu(u_ref[rows, cols]) * s).astype(BF16)
        x1_ref[...] = x_ref[...] + _nn(yc_ref[...], wo_ref[...])

    row = lambda w: pl.BlockSpec((tm, w), lambda i: (i, 0))
    pcol = lambda col: pl.BlockSpec((tm, GLA_W), lambda i: (i, col // GLA_W))
    return pl.pallas_call(
        body,
        name="mixer_out",
        grid=(seq // tm,),
        in_specs=[
            row(D_MODEL), row(GLA_W), row(GLA_W), pcol(COL_G), pcol(COL_U), pcol(COL_VV),
            _resident((1, GLA_W)), _resident((1, GMLP_W)), _resident((1, GMLP_W)),
            _resident((GMLP_GROUPS, GMLP_CHUNK, GMLP_CHUNK)), _resident((GMLP_GROUPS, GMLP_CHUNK, 1)),
            _resident((D_MODEL, D_MODEL)),
        ],
        out_specs=[row(D_MODEL), row(D_MODEL)],
        out_shape=[jax.ShapeDtypeStruct((seq, D_MODEL), F32), jax.ShapeDtypeStruct((seq, D_MODEL), BF16)],
        scratch_shapes=[pltpu.VMEM((tm, GMLP_W), BF16)],
        compiler_params=_params(48, ("parallel",)),
    )(x, o_f, o_b, p, p, p, gn, lng, lnb, ws_bf, bs_col, w_out)


def _ffn_fwd(x1, target, g2, gf, wg_t, wu_t, wd):
    seq = x1.shape[0]
    tm = min(seq, 256)

    def body(x1_ref, t_ref, g2_ref, gf_ref, wg_ref, wu_ref, wd_ref, h2_ref, gate_ref, up_ref, act_ref, dx2_ref, loss_ref, dgf_ref):
        @pl.when(pl.program_id(0) == 0)
        def _():
            loss_ref[...] = jnp.zeros_like(loss_ref)
            dgf_ref[...] = jnp.zeros_like(dgf_ref)

        x1v = x1_ref[...]
        h2 = (x1v * lax.rsqrt(jnp.mean(x1v * x1v, axis=-1, keepdims=True) + EPS) * g2_ref[...]).astype(BF16)
        h2_ref[...] = h2
        gate = _nt(h2, wg_ref[...])
        up = _nt(h2, wu_ref[...])
        act = (gate * _sigmoid(gate) * up).astype(BF16)
        gate_ref[...] = gate
        up_ref[...] = up
        act_ref[...] = act
        x2 = x1v + _nn(act, wd_ref[...])
        rf = lax.rsqrt(jnp.mean(x2 * x2, axis=-1, keepdims=True) + EPS)
        xh = x2 * rf
        err = xh * gf_ref[...] - t_ref[...]
        loss_ref[...] += 0.5 * jnp.sum(jnp.mean(err * err, axis=-1, keepdims=True))
        dy = err * (1.0 / D_MODEL)
        dgf_ref[...] += jnp.sum(dy * xh, axis=0, keepdims=True)
        dx2_ref[...] = _rms_bwd(dy * gf_ref[...], xh, rf)

    row = lambda w: pl.BlockSpec((tm, w), lambda i: (i, 0))
    weight = _resident((D_FF, D_MODEL))
    return pl.pallas_call(
        body,
        name="ffn_fwd",
        grid=(seq // tm,),
        in_specs=[row(D_MODEL), row(D_MODEL), _resident((1, D_MODEL)), _resident((1, D_MODEL)), weight, weight, weight],
        out_specs=[row(D_MODEL), row(D_FF), row(D_FF), row(D_FF), row(D_MODEL),
                   pl.BlockSpec((1, LANES), lambda i: (0, 0)), pl.BlockSpec((1, D_MODEL), lambda i: (0, 0))],
        out_shape=[
            jax.ShapeDtypeStruct((seq, D_MODEL), BF16),
            jax.ShapeDtypeStruct((seq, D_FF), F32),
            jax.ShapeDtypeStruct((seq, D_FF), F32),
            jax.ShapeDtypeStruct((seq, D_FF), BF16),
            jax.ShapeDtypeStruct((seq, D_MODEL), F32),
            jax.ShapeDtypeStruct((1, LANES), F32),
            jax.ShapeDtypeStruct((1, D_MODEL), F32),
        ],
        compiler_params=_params(56),
    )(x1, target, g2, gf, wg_t, wu_t, wd)


def _ffn_bwd(dx2, gate, up, x1, g2, wg_t, wu_t, wd):
    seq = x1.shape[0]
    tm = min(seq, 256)

    def body(dx2_ref, gate_ref, up_ref, x1_ref, g2_ref, wg_ref, wu_ref, wd_ref, dgate_ref, dup_ref, dx1_ref, dg2_ref):
        @pl.when(pl.program_id(0) == 0)
        def _():
            dg2_ref[...] = jnp.zeros_like(dg2_ref)

        dx2v = dx2_ref[...]
        dact = _nt(dx2v.astype(BF16), wd_ref[...])
        gate = gate_ref[...]
        sg = _sigmoid(gate)
        dgate = (dact * up_ref[...] * (sg * (1.0 + gate * (1.0 - sg)))).astype(BF16)
        dup = (dact * (gate * sg)).astype(BF16)
        dgate_ref[...] = dgate
        dup_ref[...] = dup
        dh2 = _nn(dgate, wg_ref[...]) + _nn(dup, wu_ref[...])
        x1v = x1_ref[...]
        r2 = lax.rsqrt(jnp.mean(x1v * x1v, axis=-1, keepdims=True) + EPS)
        xh = x1v * r2
        dg2_ref[...] += jnp.sum(dh2 * xh, axis=0, keepdims=True)
        dx1_ref[...] = dx2v + _rms_bwd(dh2 * g2_ref[...], xh, r2)

    row = lambda w: pl.BlockSpec((tm, w), lambda i: (i, 0))
    weight = _resident((D_FF, D_MODEL))
    return pl.pallas_call(
        body,
        name="ffn_bwd",
        grid=(seq // tm,),
        in_specs=[row(D_MODEL), row(D_FF), row(D_FF), row(D_MODEL), _resident((1, D_MODEL)), weight, weight, weight],
        out_specs=[row(D_FF), row(D_FF), row(D_MODEL), pl.BlockSpec((1, D_MODEL), lambda i: (0, 0))],
        out_shape=[
            jax.ShapeDtypeStruct((seq, D_FF), BF16),
            jax.ShapeDtypeStruct((seq, D_FF), BF16),
            jax.ShapeDtypeStruct((seq, D_MODEL), F32),
            jax.ShapeDtypeStruct((1, D_MODEL), F32),
        ],
        compiler_params=_params(56),
    )(dx2, gate, up, x1, g2, wg_t, wu_t, wd)


WGRAD_ROWS = D_FF // 2


def _ffn_wgrad(h2, dgate, dup, act, dx2):
    seq = h2.shape[0]
    tm = min(seq, 512)

    def body(h2_ref, dgate_ref, dup_ref, act_ref, dx2_ref, dwg_ref, dwu_ref, dwd_ref):
        @pl.when(pl.program_id(1) == 0)
        def _():
            dwg_ref[...] = jnp.zeros_like(dwg_ref)
            dwu_ref[...] = jnp.zeros_like(dwu_ref)
            dwd_ref[...] = jnp.zeros_like(dwd_ref)

        h2v = h2_ref[...]
        dwg_ref[...] += _tn(dgate_ref[...], h2v)
        dwu_ref[...] += _tn(dup_ref[...], h2v)
        dwd_ref[...] += _tn(act_ref[...], dx2_ref[...].astype(BF16))

    ff = pl.BlockSpec((tm, WGRAD_ROWS), lambda j, i: (i, j))
    row = pl.BlockSpec((tm, D_MODEL), lambda j, i: (i, 0))
    out = pl.BlockSpec((WGRAD_ROWS, D_MODEL), lambda j, i: (j, 0))
    return pl.pallas_call(
        body,
        name="ffn_wgrad",
        grid=(D_FF // WGRAD_ROWS, seq // tm),
        in_specs=[row, ff, ff, ff, row],
        out_specs=[out, out, out],
        out_shape=[jax.ShapeDtypeStruct((D_FF, D_MODEL), F32)] * 3,
        compiler_params=_params(56, ("parallel", "arbitrary")),
    )(h2, dgate, dup, act, dx2)


def _mixer_bwd(dx1, ycat, o_f, o_b, p, gn, lng, lnb, ws_bf, wst_bf, bs_col, w_out, token):
    seq = dx1.shape[0]
    tm = min(seq, 512)
    nsteps = seq // tm

    def body(dx1_ref, yc_ref, of_ref, ob_ref, g_ref, u_ref, vv_ref, gn_ref, lng_ref, lnb_ref, ws_ref, wst_ref, bs_ref, wo_ref, token_ref,
             do_ref, dg_ref, du_ref, dvv_ref, dwo_ref, dgn_ref, dlng_ref, dlnb_ref, dws_ref, dbs_ref, vn_sc, dvn_sc, dbs_acc):
        step = pl.program_id(0)

        @pl.when(step == 0)
        def _():
            for r in (dwo_ref, dgn_ref, dlng_ref, dlnb_ref, dws_ref, dbs_acc):
                r[...] = jnp.zeros_like(r)

        dx1b = dx1_ref[...].astype(BF16)
        dyc = _nt(dx1b, wo_ref[...])
        dwo_ref[...] += _tn(yc_ref[...], dx1b)
        for h in range(GLA_HEADS):
            cols = pl.ds(h * GLA_DV, GLA_DV)
            dya = dyc[:, h * GLA_DV : (h + 1) * GLA_DV]
            oh = of_ref[:, cols] + ob_ref[:, cols]
            rn = lax.rsqrt(jnp.mean(oh * oh, axis=-1, keepdims=True) + EPS)
            on = oh * rn
            gh = g_ref[:, cols]
            sg = _sigmoid(gh)
            sil = gh * sg
            gnh = gn_ref[:, cols]
            dgn_ref[:, cols] += jnp.sum(dya * on * sil, axis=0, keepdims=True)
            dg_ref[:, cols] = (dya * on * gnh * (sg * (1.0 + gh * (1.0 - sg)))).astype(BF16)
            do_ref[:, cols] = _rms_bwd(dya * gnh * sil, on, rn)
        vv = vv_ref[...]
        zv, zv_grad = _gelu_and_grad(vv)
        xc = zv - jnp.mean(zv, axis=-1, keepdims=True)
        rstd = lax.rsqrt(jnp.mean(xc * xc, axis=-1, keepdims=True) + EPS)
        vhat = xc * rstd
        vn_sc[...] = (vhat * lng_ref[...] + lnb_ref[...]).astype(BF16)
        for c in range(tm // GMLP_CHUNK):
            rows = pl.ds(c * GMLP_CHUNK, GMLP_CHUNK)
            for g in range(GMLP_GROUPS):
                cols = pl.ds(g * LANES, LANES)
                vn = vn_sc[rows, cols]
                s = _nn(ws_ref[g], vn) + bs_ref[g]
                dyb = dyc[c * GMLP_CHUNK : (c + 1) * GMLP_CHUNK, GLA_W + g * LANES : GLA_W + (g + 1) * LANES]
                zu, zu_grad = _gelu_and_grad(u_ref[rows, cols])
                du_ref[rows, cols] = (dyb * s * zu_grad).astype(BF16)
                ds = dyb * zu
                dbs_acc[g] += ds
                dsb = ds.astype(BF16)
                dws_ref[g] += _nt(dsb, vn)
                dvn_sc[rows, cols] = _nn(wst_ref[g], dsb)
        dvn = dvn_sc[...]
        dlng_ref[...] += jnp.sum(dvn * vhat, axis=0, keepdims=True)
        dlnb_ref[...] += jnp.sum(dvn, axis=0, keepdims=True)
        dvh = dvn * lng_ref[...]
        dzv = rstd * (dvh - jnp.mean(dvh, axis=-1, keepdims=True) - vhat * jnp.mean(dvh * vhat, axis=-1, keepdims=True))
        dvv_ref[...] = (dzv * zv_grad).astype(BF16)

        @pl.when(step == nsteps - 1)
        def _():
            dbs_ref[...] = jnp.sum(dbs_acc[...], axis=-1, keepdims=True)

    row = lambda w: pl.BlockSpec((tm, w), lambda i: (i, 0))
    pcol = lambda col: pl.BlockSpec((tm, GLA_W), lambda i: (i, col // GLA_W))
    const = lambda shape: pl.BlockSpec(shape, lambda i: (0,) * len(shape))
    return pl.pallas_call(
        body,
        name="mixer_bwd",
        grid=(nsteps,),
        in_specs=[
            row(D_MODEL), row(D_MODEL), row(GLA_W), row(GLA_W), pcol(COL_G), pcol(COL_U), pcol(COL_VV),
            _resident((1, GLA_W)), _resident((1, GMLP_W)), _resident((1, GMLP_W)),
            _resident((GMLP_GROUPS, GMLP_CHUNK, GMLP_CHUNK)), _resident((GMLP_GROUPS, GMLP_CHUNK, GMLP_CHUNK)),
            _resident((GMLP_GROUPS, GMLP_CHUNK, 1)), _resident((D_MODEL, D_MODEL)), _resident(TOKEN_SHAPE),
        ],
        out_specs=[
            row(GLA_W), row(GLA_W), row(GMLP_W), row(GMLP_W), const((D_MODEL, D_MODEL)),
            const((1, GLA_W)), const((1, GMLP_W)), const((1, GMLP_W)),
            const((GMLP_GROUPS, GMLP_CHUNK, GMLP_CHUNK)), const((GMLP_GROUPS, GMLP_CHUNK, 1)),
        ],
        out_shape=[
            jax.ShapeDtypeStruct((seq, GLA_W), F32), jax.ShapeDtypeStruct((seq, GLA_W), BF16),
            jax.ShapeDtypeStruct((seq, GMLP_W), BF16), jax.ShapeDtypeStruct((seq, GMLP_W), BF16),
            jax.ShapeDtypeStruct((D_MODEL, D_MODEL), F32),
            jax.ShapeDtypeStruct((1, GLA_W), F32), jax.ShapeDtypeStruct((1, GMLP_W), F32), jax.ShapeDtypeStruct((1, GMLP_W), F32),
            jax.ShapeDtypeStruct((GMLP_GROUPS, GMLP_CHUNK, GMLP_CHUNK), F32), jax.ShapeDtypeStruct((GMLP_GROUPS, GMLP_CHUNK, 1), F32),
        ],
        scratch_shapes=[pltpu.VMEM((tm, GMLP_W), BF16), pltpu.VMEM((tm, GMLP_W), F32), pltpu.VMEM((GMLP_GROUPS, GMLP_CHUNK, GMLP_CHUNK), F32)],
        compiler_params=_params(56),
    )(dx1, ycat, o_f, o_b, p, p, p, gn, lng, lnb, ws_bf, wst_bf, bs_col, w_out, token)


def _gla_bwd(p, do, st, wd_pad, bd, token, reverse, other=None):
    seq = p.shape[0]
    tg = _gla_tile(seq)
    nt = seq // tg
    n = tg // GLA_CHUNK
    scale = GLA_DK**-0.5

    def tile(i):
        return i if reverse else nt - 1 - i

    def body(q_ref, k_ref, v_ref, lr_ref, do_ref, st_ref, wd_ref, bd_ref, token_ref, *rest):
        others, (dq_ref, dk_ref, dv_ref, dlr_ref, dwd_ref, dbd_ref, carry) = rest[:-7], rest[-7:]
        if others:
            odq_ref, odk_ref, odv_ref, odlr_ref = others

            def put(ref, idx, val, oref):
                ref[idx] = (val + oref[idx]).astype(BF16)
        else:
            odq_ref = odk_ref = odv_ref = odlr_ref = None

            def put(ref, idx, val, oref):
                ref[idx] = val

        @pl.when(pl.program_id(0) == 0)
        def _():
            carry[...] = jnp.zeros_like(carry)
            dwd_ref[...] = jnp.zeros_like(dwd_ref)
            dbd_ref[...] = jnp.zeros_like(dbd_ref)

        lr_bf = lr_ref[...].astype(BF16)
        carries = [carry[h] for h in range(GLA_HEADS)]
        row_in_chunk = lax.broadcasted_iota(jnp.int32, (tg, LANES), 0) % GLA_CHUNK
        lane_head = lax.broadcasted_iota(jnp.int32, (1, LANES), 1) // GLA_DK
        tt = lax.broadcasted_iota(jnp.int32, (GLA_CHUNK, GLA_CHUNK), 0)
        ss = lax.broadcasted_iota(jnp.int32, (GLA_CHUNK, GLA_CHUNK), 1)
        causal = (tt <= ss) if reverse else (tt >= ss)
        causal_t = (tt >= ss) if reverse else (tt <= ss)
        order = range(n) if reverse else range(n - 1, -1, -1)
        dlr = jnp.zeros((tg, LANES), F32)
        heads = range(GLA_HEADS)
        pv, masks, qdh, kteh, vhs, dohs, stbs = {}, {}, {}, {}, {}, {}, {}
        sc_t, dp, dp_t, acc = {}, {}, {}, {}
        for pair in range(2):
            cols = pl.ds(pair * LANES, LANES)
            pre, b3, blast = _gla_decay_terms(lr_bf, wd_ref, bd_ref, pair, row_in_chunk, reverse, n)
            q3 = q_ref[:, cols].reshape(n, GLA_CHUNK, LANES) * scale
            k3 = k_ref[:, cols].reshape(n, GLA_CHUNK, LANES)
            eb = jnp.exp(b3)
            emb = jnp.exp(-b3)
            ekte = jnp.exp(blast - b3)
            kdf = k3 * emb
            pv[pair] = dict(pre=pre, eb=eb, emb=emb, ekte=ekte, qd=q3 * eb, kdf=kdf, kd=kdf.astype(BF16), kte=k3 * ekte, dec=jnp.exp(blast))
            for hh in range(2):
                h = 2 * pair + hh
                vcols = pl.ds(h * GLA_DV, GLA_DV)
                masks[h] = (lane_head == hh).astype(F32)
                qdh[h] = (pv[pair]["qd"] * masks[h]).astype(BF16)
                kteh[h] = (pv[pair]["kte"] * masks[h]).astype(BF16)
                vhs[h] = v_ref[:, vcols].reshape(n, GLA_CHUNK, GLA_DV).astype(BF16)
                dohs[h] = do_ref[:, vcols].reshape(n, GLA_CHUNK, GLA_DV).astype(BF16)
                stbs[h] = st_ref[:, h]
                sc_t[h] = _bnt(pv[pair]["kd"], qdh[h])
                dp[h] = _bnt(dohs[h], vhs[h])
                dp_t[h] = _bnt(vhs[h], dohs[h])
                acc[h] = _btn(dohs[h], qdh[h])
        dsa = {}
        for h in heads:
            sc_t[h] = jnp.where(causal_t, sc_t[h], 0.0).astype(BF16)
            dp[h] = jnp.where(causal, dp[h], 0.0).astype(BF16)
            dp_t[h] = jnp.where(causal_t, dp_t[h], 0.0).astype(BF16)
            dec = pv[h // 2]["dec"]
            c, after = carries[h], [None] * n
            for j in order:
                after[j] = c
                c = acc[h][j] + dec[j] * c
            carries[h] = c
            dsa[h] = jnp.stack(after)
        dvs, dqs, dks, dwds, dbds = [], [], [], [], []
        for pair in range(2):
            cols = pl.ds(pair * LANES, LANES)
            v = pv[pair]
            dqd = jnp.zeros((n, GLA_CHUNK, LANES), F32)
            dkd = jnp.zeros((n, GLA_CHUNK, LANES), F32)
            dkte = jnp.zeros((n, GLA_CHUNK, LANES), F32)
            ddec = jnp.zeros((n, 1, LANES), F32)
            for h in (2 * pair, 2 * pair + 1):
                dsa_bf = dsa[h].astype(BF16)
                dqd = dqd + (_bnn(dp[h], v["kd"]) * masks[h] + _bnn(dohs[h], stbs[h].astype(BF16)))
                dkd = dkd + _bnn(dp_t[h], qdh[h])
                dkte = dkte + _bnn(vhs[h], dsa_bf)
                ddec = ddec + jnp.sum(dsa[h] * stbs[h], axis=1, keepdims=True)
                dvs.append((_bnn(sc_t[h], dohs[h]) + _bnt(kteh[h], dsa_bf)).reshape(tg, GLA_DV))
            dqs.append((dqd * (scale * v["eb"])).reshape(tg, LANES))
            dks.append((dkd * v["emb"] + dkte * v["ekte"]).reshape(tg, LANES))
            db = dqd * v["qd"] - dkd * v["kdf"] - dkte * v["kte"]
            dblast = jnp.sum(dkte * v["kte"], axis=1, keepdims=True) + ddec * v["dec"]
            dla = _chunk_cumsum(db.reshape(tg, LANES), row_in_chunk, not reverse) + jnp.broadcast_to(dblast, (n, GLA_CHUNK, LANES)).reshape(tg, LANES)
            dpre = (dla * (1.0 / GLA_TAU) * _sigmoid(-v["pre"]))
            dpre_bf = dpre.astype(BF16)
            dlr = dlr + _nt(dpre_bf, wd_ref[:, cols])
            dwds.append(_tn(lr_bf, dpre_bf))
            dbds.append(jnp.sum(dpre, axis=0, keepdims=True))
        put(dlr_ref, (slice(None), slice(None)), dlr, odlr_ref)
        for pair in range(2):
            cols = pl.ds(pair * LANES, LANES)
            put(dq_ref, (slice(None), cols), dqs[pair], odq_ref)
            put(dk_ref, (slice(None), cols), dks[pair], odk_ref)
            dwd_ref[:, cols] += dwds[pair]
            dbd_ref[:, cols] += dbds[pair]
        for h in range(GLA_HEADS):
            put(dv_ref, (slice(None), pl.ds(h * GLA_DV, GLA_DV)), dvs[h], odv_ref)
            carry[h] = carries[h]

    pieces = [
        pl.BlockSpec((tg, KEY_W), lambda i: (tile(i), 0)),
        pl.BlockSpec((tg, KEY_W), lambda i: (tile(i), 0)),
        pl.BlockSpec((tg, GLA_W), lambda i: (tile(i), 0)),
        pl.BlockSpec((tg, LANES), lambda i: (tile(i), 0)),
    ]
    piece_dtype = BF16 if other else F32
    return pl.pallas_call(
        body,
        name="gla_bwd_rev" if reverse else "gla_bwd",
        grid=(nt,),
        in_specs=[
            pl.BlockSpec((tg, KEY_W), lambda i: (tile(i), COL_Q // KEY_W)),
            pl.BlockSpec((tg, KEY_W), lambda i: (tile(i), COL_K // KEY_W)),
            pl.BlockSpec((tg, GLA_W), lambda i: (tile(i), COL_V // GLA_W)),
            pl.BlockSpec((tg, LANES), lambda i: (tile(i), COL_LR // LANES)),
            pl.BlockSpec((tg, GLA_W), lambda i: (tile(i), 0)),
            pl.BlockSpec((n, GLA_HEADS, GLA_DV, LANES), lambda i: (tile(i), 0, 0, 0)),
            _resident((LANES, KEY_W)),
            _resident((1, KEY_W)),
            _resident(TOKEN_SHAPE),
        ] + (pieces if other else []),
        out_specs=pieces + [pl.BlockSpec((LANES, KEY_W), lambda i: (0, 0)), pl.BlockSpec((1, KEY_W), lambda i: (0, 0))],
        out_shape=[
            jax.ShapeDtypeStruct((seq, KEY_W), piece_dtype), jax.ShapeDtypeStruct((seq, KEY_W), piece_dtype),
            jax.ShapeDtypeStruct((seq, GLA_W), piece_dtype), jax.ShapeDtypeStruct((seq, LANES), piece_dtype),
            jax.ShapeDtypeStruct((LANES, KEY_W), F32), jax.ShapeDtypeStruct((1, KEY_W), F32),
        ],
        scratch_shapes=[pltpu.VMEM((GLA_HEADS, GLA_DV, LANES), F32)],
        compiler_params=_params(56),
    )(p, p, p, p, do, st, wd_pad, bd, token, *(other or ()))


def _inproj_wgrad(x, g1, dq, dk, dv, dg, du, dvv, dlr):
    seq = x.shape[0]
    tm = min(seq, 512)

    def body(x_ref, g1_ref, dq_ref, dk_ref, dv_ref, dg_ref, du_ref, dvv_ref, dlr_ref, dw_ref, dp_ref):
        @pl.when(pl.program_id(0) == 0)
        def _():
            dw_ref[...] = jnp.zeros_like(dw_ref)

        for col, ref in ((COL_Q, dq_ref), (COL_K, dk_ref), (COL_V, dv_ref), (COL_G, dg_ref), (COL_U, du_ref), (COL_VV, dvv_ref), (COL_LR, dlr_ref)):
            dp_ref[:, col : col + ref.shape[1]] = ref[...]
        xv = x_ref[...]
        h = (xv * lax.rsqrt(jnp.mean(xv * xv, axis=-1, keepdims=True) + EPS) * g1_ref[...]).astype(BF16)
        dw_ref[0:ROW_LR, :] += _tn(dp_ref[:, 0:COL_U], h)
        dw_ref[ROW_UV:PROJ_W, :] += _tn(dp_ref[:, COL_U:COL_LR], h)
        dw_ref[ROW_LR:ROW_UV, :] += _tn(dp_ref[:, COL_LR:PROJ_WP], h)[0 : ROW_UV - ROW_LR]

    row = lambda w: pl.BlockSpec((tm, w), lambda i: (i, 0))
    return pl.pallas_call(
        body,
        name="inproj_wgrad",
        grid=(seq // tm,),
        in_specs=[row(D_MODEL), _resident((1, D_MODEL)), row(KEY_W), row(KEY_W), row(GLA_W), row(GLA_W), row(GMLP_W), row(GMLP_W), row(LANES)],
        out_specs=[pl.BlockSpec((PROJ_W, D_MODEL), lambda i: (0, 0)), row(PROJ_WP)],
        out_shape=[jax.ShapeDtypeStruct((PROJ_W, D_MODEL), F32), jax.ShapeDtypeStruct((seq, PROJ_WP), BF16)],
        compiler_params=_params(56),
    )(x, g1, dq, dk, dv, dg, du, dvv, dlr)


def _inproj_dx(x, dx1, g1, w_in_t, dp, token):
    seq = x.shape[0]
    tm = min(seq, 512)

    def body(x_ref, dx1_ref, g1_ref, w_ref, dp_ref, token_ref, dx_ref, dg1_ref):
        @pl.when(pl.program_id(0) == 0)
        def _():
            dg1_ref[...] = jnp.zeros_like(dg1_ref)

        xv = x_ref[...]
        r1 = lax.rsqrt(jnp.mean(xv * xv, axis=-1, keepdims=True) + EPS)
        xh = xv * r1
        dh = (_nn(dp_ref[:, 0:COL_U], w_ref[0:ROW_LR, :]) + _nn(dp_ref[:, COL_U:COL_LR], w_ref[ROW_UV:PROJ_W, :])
              + _nn(dp_ref[:, COL_LR:PROJ_WP], w_ref[ROW_LR : ROW_LR + LANES, :]))
        dg1_ref[...] += jnp.sum(dh * xh, axis=0, keepdims=True)
        dx_ref[...] = dx1_ref[...] + _rms_bwd(dh * g1_ref[...], xh, r1)

    row = lambda w: pl.BlockSpec((tm, w), lambda i: (i, 0))
    return pl.pallas_call(
        body,
        name="inproj_dx",
        grid=(seq // tm,),
        in_specs=[row(D_MODEL), row(D_MODEL), _resident((1, D_MODEL)), _resident((PROJ_W, D_MODEL)), row(PROJ_WP), _resident(TOKEN_SHAPE)],
        out_specs=[row(D_MODEL), pl.BlockSpec((1, D_MODEL), lambda i: (0, 0))],
        out_shape=[jax.ShapeDtypeStruct((seq, D_MODEL), F32), jax.ShapeDtypeStruct((1, D_MODEL), F32)],
        compiler_params=_params(48),
    )(x, dx1, g1, w_in_t, dp, token)


def _in_hbm(a):
    return pltpu.with_memory_space_constraint(a, pltpu.HBM)


def _row_tile(rows, multiple=8):
    for t in range(min(rows, 512), 0, -1):
        if rows % t == 0 and t % multiple == 0:
            return t
    return rows


def _cast_into_slot(w, shard):
    rows, cols = w.shape
    tr = _row_tile(rows, 16)

    def body(s_ref, w_ref, o_ref):
        o_ref[...] = w_ref[...].astype(BF16)

    return pl.pallas_call(
        body,
        name="cast_into_slot",
        grid_spec=pltpu.PrefetchScalarGridSpec(
            num_scalar_prefetch=1,
            grid=(rows // tr,),
            in_specs=[pl.BlockSpec((tr, cols), lambda i, s_ref: (i, 0))],
            out_specs=pl.BlockSpec((None, tr, cols), lambda i, s_ref: (s_ref[0], i, 0)),
        ),
        out_shape=pltpu.HBM((N_SHARDS, rows, cols), BF16),
        compiler_params=_params(32, ("parallel",)),
    )(shard, _in_hbm(w))


def _add_halves(grads4, recvs, c):
    n = len(grads4)
    _, rows, _ = grads4[0].shape
    tr = _row_tile(rows, 16)

    def body(c_ref, *refs):
        for k in range(n):
            total = refs[k][...] + refs[n + k][...]
            refs[2 * n + k][...] = total
            refs[3 * n + k][...] = total.astype(BF16)

    out = pl.BlockSpec((None, tr, HALF), lambda s, i, c_ref: (s, i, 0))
    mine = pl.BlockSpec((None, tr, HALF), lambda s, i, c_ref: (s, i, c_ref[0]))
    outs = pl.pallas_call(
        body,
        name="add_halves",
        grid_spec=pltpu.PrefetchScalarGridSpec(
            num_scalar_prefetch=1,
            grid=(N_SHARDS, rows // tr),
            in_specs=[mine] * n + [out] * n,
            out_specs=[out] * (2 * n),
        ),
        out_shape=[pltpu.HBM((N_SHARDS, rows, HALF), F32)] * n + [pltpu.HBM((N_SHARDS, rows, HALF), BF16)] * n,
        compiler_params=_params(48, ("parallel", "parallel")),
    )(c, *[_in_hbm(a) for a in list(grads4) + list(recvs)])
    return list(zip(outs[:n], outs[n:]))


def _add_partials(part4, recv3, shard_core):
    _, rows, _ = part4.shape
    tr = _row_tile(rows, 16)

    def body(sc_ref, p_ref, r_ref, o_ref):
        o_ref[...] = ((p_ref[...] + r_ref[0].astype(F32)) + r_ref[1].astype(F32)) + r_ref[2].astype(F32)

    return pl.pallas_call(
        body,
        name="add_partials",
        grid_spec=pltpu.PrefetchScalarGridSpec(
            num_scalar_prefetch=1,
            grid=(rows // tr,),
            in_specs=[
                pl.BlockSpec((None, tr, HALF), lambda i, sc_ref: (sc_ref[0], i, 0)),
                pl.BlockSpec((3, tr, HALF), lambda i, sc_ref: (0, i, 0)),
            ],
            out_specs=pl.BlockSpec((tr, HALF), lambda i, sc_ref: (i, sc_ref[1])),
        ),
        out_shape=pltpu.HBM((rows, 2 * HALF), F32),
        compiler_params=_params(32, ("parallel",)),
    )(shard_core, _in_hbm(part4), _in_hbm(recv3))


def _adam_math(w, g, m, v):
    m = ADAM_B1 * m + (1.0 - ADAM_B1) * g
    v = ADAM_B2 * v + (1.0 - ADAM_B2) * (g * g)
    m_hat = m / (1.0 - ADAM_B1**ADAM_STEP)
    v_hat = v / (1.0 - ADAM_B2**ADAM_STEP)
    delta = -ADAM_LR * (m_hat / (jnp.sqrt(v_hat) + ADAM_EPS) + ADAM_WD * w)
    return delta, m, v


def _adamw(w, g, m, v):
    rows, cols = w.shape
    tr = _row_tile(rows)

    def body(w_ref, g_ref, m_ref, v_ref, go_ref, d_ref, mo_ref, vo_ref):
        gv = g_ref[...]
        go_ref[...] = gv
        d_ref[...], mo_ref[...], vo_ref[...] = _adam_math(w_ref[...], gv, m_ref[...], v_ref[...])

    spec = pl.BlockSpec((tr, cols), lambda i: (i, 0))
    return pl.pallas_call(
        body, name="adamw", grid=(rows // tr,), in_specs=[spec] * 4, out_specs=[spec] * 4, out_shape=[pltpu.HBM(w.shape, F32)] * 4,
        compiler_params=_params(32, ("parallel",)),
    )(_in_hbm(w), _in_hbm(g), _in_hbm(m), _in_hbm(v))


SMALL_ROWS = 560
DECAY_ROWS = 8
SMALL_TOTAL = SMALL_ROWS + 2 * N_SHARDS * DECAY_ROWS


def _adamw_small(gathered, wp, mp, vp):
    out_rows = SMALL_ROWS + 2 * DECAY_ROWS

    def body(ga_ref, w_ref, m_ref, v_ref, g_ref, d_ref, mo_ref, vo_ref):
        shard = 2 * lax.axis_index("x") + lax.axis_index("y")
        g_ref[pl.ds(0, SMALL_ROWS), :] = functools.reduce(lambda a, b: a + b, [ga_ref[d, pl.ds(0, SMALL_ROWS), :] for d in range(8)])
        for k in range(2):
            start = pl.multiple_of(SMALL_ROWS + k * N_SHARDS * DECAY_ROWS + shard * DECAY_ROWS, DECAY_ROWS)
            g_ref[pl.ds(SMALL_ROWS + k * DECAY_ROWS, DECAY_ROWS), :] = functools.reduce(
                lambda a, b: a + b, [ga_ref[d, pl.ds(start, DECAY_ROWS), :] for d in range(8)])
        d_ref[...], mo_ref[...], vo_ref[...] = _adam_math(w_ref[...], g_ref[...], m_ref[...], v_ref[...])

    shape = jax.ShapeDtypeStruct((out_rows, LANES), F32)
    return pl.pallas_call(body, name="adamw_small", out_shape=[shape] * 4, compiler_params=_params(32, None))(gathered, wp, mp, vp)


ANY = pl.BlockSpec(memory_space=pl.ANY)


def _position():
    return lax.axis_index("x"), lax.axis_index("y"), lax.axis_index("c")


def _other_chips(x, y):
    return [(1 - x, y), (x, 1 - y), (1 - x, 1 - y)]


HBM = pl.BlockSpec(memory_space=pltpu.HBM)
SEM = pl.BlockSpec(memory_space=pltpu.SEMAPHORE)
TOKEN = jax.ShapeDtypeStruct(TOKEN_SHAPE, F32)
DATAFLOW = pltpu.SideEffectType.DATAFLOW_SIDE_EFFECTING


def _half_block(ref4, slot, core):
    return ref4.at[slot, :, pl.ds(pl.multiple_of(core * HALF, HALF), HALF)]


def _gather_ici_copies(refs4, send_sems, recv_sems, stride):
    x, y, c = _position()
    pairs = []
    for k, ref4 in enumerate(refs4):
        mine = _half_block(ref4, 2 * x + y, c)
        for j, (px, py) in enumerate(_other_chips(x, y)):
            sems = dict(send_sem=send_sems.at[stride * k + j], recv_sem=recv_sems.at[stride * k + j], device_id=(px, py, c), device_id_type=MESH)
            pairs.append((functools.partial(pltpu.make_async_remote_copy, src_ref=mine, dst_ref=mine, **sems),
                          functools.partial(pltpu.make_async_remote_copy, src_ref=mine, dst_ref=_half_block(ref4, 2 * px + py, c), **sems)))
    return pairs


def _gather_d2d_copies(refs4, send_sems, recv_sems, stride, offset):
    x, y, c = _position()
    pairs = []
    for k, ref4 in enumerate(refs4):
        for j, (px, py) in enumerate(_other_chips(x, y)):
            have = _half_block(ref4, 2 * px + py, c)
            sems = dict(send_sem=send_sems.at[stride * k + offset + j], recv_sem=recv_sems.at[stride * k + offset + j],
                        device_id=(x, y, 1 - c), device_id_type=MESH)
            pairs.append((functools.partial(pltpu.make_async_remote_copy, src_ref=have, dst_ref=have, **sems),
                          functools.partial(pltpu.make_async_remote_copy, src_ref=have, dst_ref=_half_block(ref4, 2 * px + py, 1 - c), **sems)))
    return pairs


def _gather_sync(bufs):
    n = len(bufs)

    def body(*refs):
        outs = refs[n : 2 * n]
        send_sems, recv_sems = refs[2 * n :]
        ici = _gather_ici_copies(outs, send_sems, recv_sems, 6)
        d2d = _gather_d2d_copies(outs, send_sems, recv_sems, 6, 3)
        for send, _ in ici:
            send().start()
        for (_, arrival), (forward, _) in zip(ici, d2d):
            arrival().wait_recv()
            forward().start()
        for _, arrival in d2d:
            arrival().wait_recv()
        for send, _ in ici + d2d:
            send().wait_send()

    return pl.pallas_call(
        body,
        name="gather_sync",
        in_specs=[ANY] * n,
        out_specs=[ANY] * n,
        out_shape=[jax.ShapeDtypeStruct(b.shape, b.dtype) for b in bufs],
        input_output_aliases={k: k for k in range(n)},
        scratch_shapes=[pltpu.SemaphoreType.DMA((6 * n,)), pltpu.SemaphoreType.DMA((6 * n,))],
        compiler_params=pltpu.CompilerParams(has_side_effects=True),
    )(*bufs)


def _gather_start(bufs, after):
    n, na = len(bufs), len(after)

    def body(*refs):
        ins = refs[:n]
        send_sems, recv_sems = refs[n + na], refs[n + na + 1]
        token = refs[2 * n + na + 2]
        for send, _ in _gather_ici_copies(ins, send_sems, recv_sems, 3):
            send().start()
        token[...] = jnp.zeros_like(token)

    out = pl.pallas_call(
        body,
        name="gather_start",
        in_specs=[HBM] * n + [ANY] * na,
        out_specs=(SEM, SEM, *[HBM] * n, pl.BlockSpec(memory_space=pltpu.VMEM)),
        out_shape=(pltpu.SemaphoreType.DMA((3 * n,)), pltpu.SemaphoreType.DMA((3 * n,)), *[pltpu.HBM(b.shape, b.dtype) for b in bufs], TOKEN),
        input_output_aliases={k: 2 + k for k in range(n)},
        compiler_params=pltpu.CompilerParams(has_side_effects=DATAFLOW),
    )(*[pltpu.with_memory_space_constraint(b, pltpu.HBM) for b in bufs], *after)
    return out[0], out[1], list(out[2 : 2 + n]), out[2 + n]


def _gather_wait(send_sems, recv_sems, bufs, after):
    n = len(bufs)

    def body(*refs):
        ins = refs[:n]
        for send, arrival in _gather_ici_copies(ins, refs[n], refs[n + 1], 3):
            send().wait_send()
            arrival().wait_recv()

    return pl.pallas_call(
        body,
        name="gather_wait",
        in_specs=[HBM] * n + [SEM, SEM] + [ANY] * len(after),
        out_specs=tuple([HBM] * n),
        out_shape=tuple(pltpu.HBM(b.shape, b.dtype) for b in bufs),
        input_output_aliases={k: k for k in range(n)},
        compiler_params=pltpu.CompilerParams(has_side_effects=DATAFLOW),
    )(*bufs, send_sems, recv_sems, *after)


def _gather_forward(bufs):
    n = len(bufs)

    def body(*refs):
        outs = refs[n : 2 * n]
        send_sems, recv_sems = refs[2 * n :]
        d2d = _gather_d2d_copies(outs, send_sems, recv_sems, 3, 0)
        for forward, _ in d2d:
            forward().start()
        for forward, arrival in d2d:
            arrival().wait_recv()
            forward().wait_send()

    return pl.pallas_call(
        body,
        name="gather_forward",
        in_specs=[ANY] * n,
        out_specs=[ANY] * n,
        out_shape=[jax.ShapeDtypeStruct(b.shape, b.dtype) for b in bufs],
        input_output_aliases={k: k for k in range(n)},
        scratch_shapes=[pltpu.SemaphoreType.DMA((3 * n,)), pltpu.SemaphoreType.DMA((3 * n,))],
        compiler_params=pltpu.CompilerParams(has_side_effects=True),
    )(*bufs)


def _exchange_halves(grads4):
    n = len(grads4)

    def body(*refs):
        ins, outs = refs[:n], refs[n : 2 * n]
        send_sems, recv_sems = refs[2 * n :]
        x, y, c = _position()
        copies = []
        for k in range(n):
            cp = pltpu.make_async_remote_copy(
                src_ref=ins[k].at[:, :, pl.ds(pl.multiple_of((1 - c) * HALF, HALF), HALF)], dst_ref=outs[k],
                send_sem=send_sems.at[k], recv_sem=recv_sems.at[k], device_id=(x, y, 1 - c), device_id_type=MESH)
            cp.start()
            copies.append(cp)
        for cp in copies:
            cp.wait()

    return pl.pallas_call(
        body,
        name="exchange_halves",
        in_specs=[ANY] * n,
        out_specs=[ANY] * n,
        out_shape=[jax.ShapeDtypeStruct((N_SHARDS, g.shape[1], HALF), g.dtype) for g in grads4],
        scratch_shapes=[pltpu.SemaphoreType.DMA((n,)), pltpu.SemaphoreType.DMA((n,))],
        compiler_params=pltpu.CompilerParams(has_side_effects=True),
    )(*grads4)


def _scatter_copies(parts, lands, send_sems, recv_sems):
    x, y, c = _position()
    copies = []
    for k in range(len(parts)):
        for j, (px, py) in enumerate(_other_chips(x, y)):
            copies.append(pltpu.make_async_remote_copy(
                src_ref=parts[k].at[2 * px + py], dst_ref=lands[k].at[j],
                send_sem=send_sems.at[3 * k + j], recv_sem=recv_sems.at[3 * k + j], device_id=(px, py, c), device_id_type=MESH))
    return copies


def _exchange_copies(grads, lands, send_sems, recv_sems):
    x, y, c = _position()
    return [pltpu.make_async_remote_copy(
        src_ref=grads[k].at[:, :, pl.ds(pl.multiple_of((1 - c) * HALF, HALF), HALF)], dst_ref=lands[k],
        send_sem=send_sems.at[k], recv_sem=recv_sems.at[k], device_id=(x, y, 1 - c), device_id_type=MESH) for k in range(len(grads))]


def _exchange_lands(grads4):
    return [jax.ShapeDtypeStruct((N_SHARDS, g.shape[1], HALF), g.dtype) for g in grads4]


def _scatter_lands(parts4):
    return [jax.ShapeDtypeStruct((3,) + g.shape[1:], g.dtype) for g in parts4]


def _split_start(name, srcs, land_shapes, make_copies, nsem):
    n, nl = len(srcs), len(land_shapes)
    lands = [lax.empty(a.shape, a.dtype) for a in land_shapes]

    def body(*refs):
        send_sems, recv_sems = refs[n + nl], refs[n + nl + 1]
        token = refs[2 * (n + nl) + 2]
        for cp in make_copies(refs[:n], refs[n : n + nl], send_sems, recv_sems):
            cp.start()
        token[...] = jnp.zeros_like(token)

    hbm = lambda a: pltpu.HBM(a.shape, a.dtype)
    out = pl.pallas_call(
        body,
        name=name,
        in_specs=[HBM] * (n + nl),
        out_specs=(SEM, SEM, *[HBM] * (n + nl), pl.BlockSpec(memory_space=pltpu.VMEM)),
        out_shape=(pltpu.SemaphoreType.DMA((nsem,)), pltpu.SemaphoreType.DMA((nsem,)), *[hbm(a) for a in srcs + lands], TOKEN),
        input_output_aliases={k: 2 + k for k in range(n + nl)},
        compiler_params=pltpu.CompilerParams(has_side_effects=DATAFLOW),
    )(*[pltpu.with_memory_space_constraint(a, pltpu.HBM) for a in srcs + lands])
    return out[0], out[1], list(out[2 : 2 + n]), list(out[2 + n : 2 + n + nl]), out[2 + n + nl]


def _split_wait(name, send_sems, recv_sems, srcs, lands, make_copies, after):
    n, nl = len(srcs), len(lands)

    def body(*refs):
        for cp in make_copies(refs[:n], refs[n : n + nl], refs[n + nl], refs[n + nl + 1]):
            cp.wait_send()
            cp.wait_recv()

    hbm = lambda a: pltpu.HBM(a.shape, a.dtype)
    out = pl.pallas_call(
        body,
        name=name,
        in_specs=[HBM] * (n + nl) + [SEM, SEM] + [ANY] * len(after),
        out_specs=tuple([HBM] * (n + nl)),
        out_shape=tuple(hbm(a) for a in srcs + lands),
        input_output_aliases={k: k for k in range(n + nl)},
        compiler_params=pltpu.CompilerParams(has_side_effects=DATAFLOW),
    )(*srcs, *lands, send_sems, recv_sems, *after)
    return list(out[:n]), list(out[n:])


def _join_halves(bufs):
    n = len(bufs)

    def body(*refs):
        outs = refs[n : 2 * n]
        send_sems, recv_sems = refs[2 * n :]
        x, y, c = _position()
        half = lambda ref, core: ref.at[:, pl.ds(pl.multiple_of(core * HALF, HALF), HALF)]
        for k in range(n):
            mine = half(outs[k], c)
            pltpu.make_async_remote_copy(
                src_ref=mine, dst_ref=mine, send_sem=send_sems.at[k], recv_sem=recv_sems.at[k],
                device_id=(x, y, 1 - c), device_id_type=MESH).start()
        for k in range(n):
            wait = pltpu.make_async_remote_copy(
                src_ref=half(outs[k], c), dst_ref=half(outs[k], 1 - c), send_sem=send_sems.at[k], recv_sem=recv_sems.at[k],
                device_id=(x, y, 1 - c), device_id_type=MESH)
            wait.wait_send()
            wait.wait_recv()

    return pl.pallas_call(
        body,
        name="join_halves",
        in_specs=[ANY] * n,
        out_specs=[ANY] * n,
        out_shape=[jax.ShapeDtypeStruct(b.shape, b.dtype) for b in bufs],
        input_output_aliases={k: k for k in range(n)},
        scratch_shapes=[pltpu.SemaphoreType.DMA((n,)), pltpu.SemaphoreType.DMA((n,))],
        compiler_params=pltpu.CompilerParams(has_side_effects=True),
    )(*bufs)


def _allgather_small(block):
    m_per, ncol = block.shape

    def body(x_ref, out_ref, send_sems, recv_sems, local_sem):
        x, y, c = _position()
        me, sibling = (x, y, c), (x, y, 1 - c)
        chips = _other_chips(x, y)

        def rows(px, py, pc):
            return out_ref.at[4 * px + 2 * py + pc]

        def copy(k, blk, to, src=None):
            return pltpu.make_async_remote_copy(
                src_ref=rows(*blk) if src is None else src, dst_ref=rows(*blk),
                send_sem=send_sems.at[k], recv_sem=recv_sems.at[k], device_id=to, device_id_type=MESH)

        mine = pltpu.make_async_copy(x_ref, rows(*me), local_sem)
        mine.start()
        first = [copy(0, me, sibling, src=x_ref)] + [copy(1 + j, me, (*chip, c), src=x_ref) for j, chip in enumerate(chips)]
        for cp in first:
            cp.start()
        passed = [copy(4 + j, (*chip, c), sibling) for j, chip in enumerate(chips)]
        for j, chip in enumerate(chips):
            copy(1 + j, (*chip, c), me).wait_recv()
            passed[j].start()
        copy(0, sibling, me).wait_recv()
        for j, chip in enumerate(chips):
            copy(4 + j, (*chip, 1 - c), me).wait_recv()
        for cp in first + passed:
            cp.wait_send()
        mine.wait()

    return pl.pallas_call(
        body,
        name="allgather_small",
        in_specs=[pl.BlockSpec(memory_space=pltpu.VMEM)],
        out_specs=pl.BlockSpec(memory_space=pltpu.VMEM),
        out_shape=jax.ShapeDtypeStruct((8, m_per, ncol), block.dtype),
        scratch_shapes=[pltpu.SemaphoreType.DMA((7,)), pltpu.SemaphoreType.DMA((7,)), pltpu.SemaphoreType.DMA],
        compiler_params=pltpu.CompilerParams(has_side_effects=True, vmem_limit_bytes=32 * MIB),
    )(block)


SMALL_NAMES = ["norm1_g", "b_decay_f", "b_decay_b", "gla_norm_g", "gmlp_ln_g", "gmlp_ln_b", "w_spatial", "b_spatial", "norm2_g", "final_norm_g"]


def _pack_small(parts, decay_parts):
    flat = jnp.concatenate([a.reshape(-1) for a in parts])
    flat = jnp.pad(flat, (0, SMALL_ROWS * LANES - flat.shape[0])).reshape(SMALL_ROWS, LANES)
    return jnp.concatenate([flat] + [d.reshape(-1, LANES) for d in decay_parts], axis=0)


def _unpack_small(packed, like):
    out, off = [], 0
    flat = packed[:SMALL_ROWS].reshape(-1)
    for a in like:
        out.append(flat[off : off + a.size].reshape(a.shape))
        off += a.size
    return out


def kernel(x, norm1_g, w_in, w_decay_f, b_decay_f, w_decay_b, b_decay_b, gla_norm_g, gmlp_ln_g, gmlp_ln_b, w_spatial, b_spatial, w_out, norm2_g, w_gate, w_up, w_down, final_norm_g, loss_target, m_norm1_g, m_w_in, m_w_decay_f, m_b_decay_f, m_w_decay_b, m_b_decay_b, m_gla_norm_g, m_gmlp_ln_g, m_gmlp_ln_b, m_w_spatial, m_b_spatial, m_w_out, m_norm2_g, m_w_gate, m_w_up, m_w_down, m_final_norm_g, v_norm1_g, v_w_in, v_w_decay_f, v_b_decay_f, v_w_decay_b, v_b_decay_b, v_gla_norm_g, v_gmlp_ln_g, v_gmlp_ln_b, v_w_spatial, v_b_spatial, v_w_out, v_norm2_g, v_w_gate, v_w_up, v_w_down, v_final_norm_g):
    args = dict(locals())
    cx, cy, cc = lax.axis_index("x"), lax.axis_index("y"), lax.axis_index("c")
    shard = 2 * cx + cy
    xs = x[0]
    target = loss_target[0]

    big_names = ["w_in", "w_out", "w_gate", "w_up", "w_down"]
    transposed = ("w_in", "w_gate", "w_up")
    rows_of = lambda pre, k: jnp.transpose(args[pre + k][0]) if k in transposed else args[pre + k][0]
    big_shards = {k: rows_of("", k) for k in big_names}
    c_arr = cc.reshape(1).astype(jnp.int32)
    s_arr = shard.reshape(1).astype(jnp.int32)
    sc_arr = jnp.stack([shard, cc]).astype(jnp.int32)
    slots = {k: _cast_into_slot(big_shards[k], s_arr) for k in big_names}
    (w_in4,) = _gather_sync([slots["w_in"]])
    w_in_t = w_in4.reshape(PROJ_W, D_MODEL)

    dec_block = jnp.concatenate([w_decay_f[0].reshape(-1, LANES), w_decay_b[0].reshape(-1, LANES)], axis=0)
    dec_all = _allgather_small(dec_block)
    late = ["w_out", "w_gate", "w_up", "w_down"]
    g_send, g_recv, late_bufs, token_gather = _gather_start([slots[k] for k in late], (w_in4, dec_all))
    dec_all = dec_all[::2].reshape(N_SHARDS, 2, LOWRANK, KEY_W // N_SHARDS)
    wdf_full = jnp.transpose(dec_all[:, 0], (1, 0, 2)).reshape(LOWRANK, KEY_W)
    wdb_full = jnp.transpose(dec_all[:, 1], (1, 0, 2)).reshape(LOWRANK, KEY_W)
    wd_pad_f = jnp.zeros((LANES, KEY_W), F32).at[0:LOWRANK].set(wdf_full).astype(BF16)
    wd_pad_b = jnp.zeros((LANES, KEY_W), F32).at[LOWRANK : 2 * LOWRANK].set(wdb_full).astype(BF16)

    ws_bf = w_spatial[0].astype(BF16)
    wst_bf = jnp.transpose(w_spatial[0], (0, 2, 1)).astype(BF16)
    bs_col = b_spatial[0].reshape(GMLP_GROUPS, GMLP_CHUNK, 1)

    p = _inproj(xs, norm1_g, w_in_t, token_gather)
    o_f, st_f = _gla_fwd(p, wd_pad_f, b_decay_f, reverse=False)
    o_b, st_b = _gla_fwd(p, wd_pad_b, b_decay_b, reverse=True)
    late_bufs = _gather_forward(_gather_wait(g_send, g_recv, late_bufs, (o_f, o_b)))
    w_out_full, wg_t, wu_t, wd = [b.reshape(-1, D_MODEL) for b in late_bufs]
    x1, ycat = _mixer_out(xs, o_f, o_b, p, gla_norm_g, gmlp_ln_g, gmlp_ln_b, ws_bf, bs_col, w_out_full)
    gf = final_norm_g.reshape(1, D_MODEL)
    h2, gate, up, act, dx2, loss_acc, dgf = _ffn_fwd(x1, target, norm2_g, gf, wg_t, wu_t, wd)

    dgate, dup, dx1, dg2 = _ffn_bwd(dx2, gate, up, x1, norm2_g, wg_t, wu_t, wd)
    ffn_grads4 = [g.reshape(N_SHARDS, FF_SHARD, D_MODEL) for g in _ffn_wgrad(h2, dgate, dup, act, dx2)]
    e_send, e_recv, e_srcs, e_lands, token_exchange = _split_start(
        "exchange_start", ffn_grads4, _exchange_lands(ffn_grads4), _exchange_copies, len(ffn_grads4))
    do, dg, du, dvv, dwo, dgn, dlng, dlnb, dws, dbs = _mixer_bwd(
        dx1, ycat, o_f, o_b, p, gla_norm_g, gmlp_ln_g, gmlp_ln_b, ws_bf, wst_bf, bs_col, w_out_full, token_exchange)
    ffn_mine, ffn_other = _split_wait("exchange_wait", e_send, e_recv, e_srcs, e_lands, _exchange_copies, (do,))
    ffn_parts = _add_halves(ffn_mine, ffn_other, c_arr)
    ffn_payload = [pb for _, pb in ffn_parts]
    s_send, s_recv, s_parts, s_lands, token_scatter = _split_start(
        "scatter_start", ffn_payload, _scatter_lands(ffn_payload), _scatter_copies, 3 * len(ffn_payload))
    dq_f, dk_f, dv_f, dlr_f, dwdec_f, dbdec_f = _gla_bwd(p, do, st_f, wd_pad_f, b_decay_f, token_scatter, reverse=False)
    dq, dk, dv, dlr, dwdec_b, dbdec_b = _gla_bwd(
        p, do, st_b, wd_pad_b, b_decay_b, token_scatter, reverse=True, other=(dq_f, dk_f, dv_f, dlr_f))
    dwin_t, dp = _inproj_wgrad(xs, norm1_g, dq, dk, dv, dg, du, dvv, dlr)
    _, ffn_recv = _split_wait("scatter_wait", s_send, s_recv, s_parts, s_lands, _scatter_copies, (dwin_t,))

    dwin4 = dwin_t.reshape(N_SHARDS, PROJ_W // N_SHARDS, D_MODEL)
    dwo4 = dwo.reshape(N_SHARDS, D_MODEL // N_SHARDS, D_MODEL)
    proj_grads4 = [dwin4, dwo4]
    proj_parts = [_add_halves([g], [r], c_arr)[0] for g, r in zip(proj_grads4, _exchange_halves(proj_grads4))]
    proj_payload = [pb for _, pb in proj_parts]
    p_send, p_recv, p_parts, p_lands, token_proj = _split_start(
        "proj_scatter_start", proj_payload, _scatter_lands(proj_payload), _scatter_copies, 3 * len(proj_payload))
    dx, dg1 = _inproj_dx(xs, dx1, norm1_g, w_in_t, dp, token_proj)
    _, proj_recv = _split_wait("proj_scatter_wait", p_send, p_recv, p_parts, p_lands, _scatter_copies, (dx,))
    parts_f32 = [pf for pf, _ in proj_parts + ffn_parts]
    bufs = [_add_partials(pf, r, sc_arr) for pf, r in zip(parts_f32, proj_recv + ffn_recv)]
    big_grads = dict(zip(big_names, _join_halves(bufs)))

    dwdec_f16 = dwdec_f[0:LOWRANK]
    dwdec_b16 = dwdec_b[LOWRANK : 2 * LOWRANK]
    shard_major = lambda a: jnp.transpose(a.reshape(LOWRANK, N_SHARDS, KEY_W // N_SHARDS), (1, 0, 2))
    small_grads = {
        "norm1_g": dg1, "b_decay_f": dbdec_f, "b_decay_b": dbdec_b, "gla_norm_g": dgn, "gmlp_ln_g": dlng, "gmlp_ln_b": dlnb,
        "w_spatial": dws, "b_spatial": dbs, "norm2_g": dg2, "final_norm_g": dgf,
    }
    g_pack = _pack_small([small_grads[k] for k in SMALL_NAMES] + [loss_acc], [shard_major(dwdec_f16), shard_major(dwdec_b16)])
    g_all = _allgather_small(g_pack)
    pack_own = lambda pre: _pack_small([args[pre + k] for k in SMALL_NAMES], [args[pre + "w_decay_f"], args[pre + "w_decay_b"]])
    sg, sd, sm, sv = _adamw_small(g_all, pack_own(""), pack_own("m_"), pack_own("v_"))

    names = ["norm1_g", "w_in", "w_decay_f", "b_decay_f", "w_decay_b", "b_decay_b", "gla_norm_g", "gmlp_ln_g", "gmlp_ln_b",
             "w_spatial", "b_spatial", "w_out", "norm2_g", "w_gate", "w_up", "w_down", "final_norm_g"]
    like = [args[k] for k in SMALL_NAMES]
    results = {"g": {}, "d": {}, "m": {}, "v": {}}
    for tag, packed in (("g", sg), ("d", sd), ("m", sm), ("v", sv)):
        for k, a in zip(SMALL_NAMES, _unpack_small(packed, like)):
            results[tag][k] = a
        results[tag]["w_decay_f"] = packed[SMALL_ROWS : SMALL_ROWS + DECAY_ROWS].reshape(w_decay_f.shape)
        results[tag]["w_decay_b"] = packed[SMALL_ROWS + DECAY_ROWS :].reshape(w_decay_b.shape)
    for k in big_names:
        g, d, mo, vo = _adamw(big_shards[k], big_grads[k], rows_of("m_", k), rows_of("v_", k))
        for tag, a in (("g", g), ("d", d), ("m", mo), ("v", vo)):
            results[tag][k] = (jnp.transpose(a) if k in transposed else a).reshape(args[k].shape)

    loss = sg[:SMALL_ROWS].reshape(-1)[sum(a.size for a in like)]
    grad_x = dx.reshape(x.shape)
    return (loss, grad_x, *[results["g"][k] for k in names], *[results["d"][k] for k in names],
            *[results["m"][k] for k in names], *[results["v"][k] for k in names])
```

```python
import functools
import math

import jax
import jax.numpy as jnp
from jax import lax
from jax.experimental import pallas as pl
from jax.experimental.pallas import tpu as pltpu

F32, BF16 = jnp.float32, jnp.bfloat16

D_MODEL = 1024
GLA_HEADS = 4
GLA_DK = 64
GLA_DV = 128
KEY_W = GLA_HEADS * GLA_DK
GLA_W = GLA_HEADS * GLA_DV
GMLP_W = 512
GMLP_GROUPS = 4
GMLP_CHUNK = 128
LOWRANK = 16
GLA_CHUNK = 64
GLA_TAU = 16.0
PROJ_W = 2592
PROJ_WP = 2688
D_FF = 2816
N_SHARDS = 4
FF_SHARD = D_FF // N_SHARDS
EPS = 1e-6
LANES = 128
TOKEN_SHAPE = (8, LANES)
MIB = 1024 * 1024

ADAM_LR = 0.001
ADAM_B1 = 0.9
ADAM_B2 = 0.999
ADAM_EPS = 1e-08
ADAM_WD = 0.01
ADAM_STEP = 10

COL_Q, COL_K = 0, 256
COL_V, COL_G, COL_U, COL_VV = 512, 1024, 1536, 2048
COL_LR = 2560
ROW_LR, ROW_UV = 1536, 1568
HALF = D_MODEL // 2

MESH = pl.DeviceIdType.MESH


def _nn(a, b):
    return jnp.dot(a, b, preferred_element_type=F32)


def _nt(a, b):
    return lax.dot_general(a, b, (((1,), (1,)), ((), ())), preferred_element_type=F32)


def _tn(a, b):
    return lax.dot_general(a, b, (((0,), (0,)), ((), ())), preferred_element_type=F32)


def _bnn(a, b):
    return jnp.einsum("nik,nkj->nij", a, b, preferred_element_type=F32)


def _bnt(a, b):
    return jnp.einsum("nik,njk->nij", a, b, preferred_element_type=F32)


def _btn(a, b):
    return jnp.einsum("nki,nkj->nij", a, b, preferred_element_type=F32)


def _resident(shape):
    zeros = (0,) * len(shape)
    return pl.BlockSpec(shape, lambda *_: zeros, pipeline_mode=pl.Buffered(1))


def _params(vmem_mib, semantics=("arbitrary",)):
    return pltpu.CompilerParams(vmem_limit_bytes=vmem_mib * MIB, dimension_semantics=semantics)


def _sigmoid(x):
    return 1.0 / (1.0 + jnp.exp(-x))


def _gelu(x):
    return 0.5 * x * (1.0 + lax.erf(x * (1.0 / math.sqrt(2.0))))


def _gelu_and_grad(x):
    cdf = 0.5 * (1.0 + lax.erf(x * (1.0 / math.sqrt(2.0))))
    return x * cdf, cdf + x * jnp.exp(-0.5 * x * x) * (1.0 / math.sqrt(2.0 * math.pi))


def _log_sigmoid(x):
    return jnp.minimum(x, 0.0) - jnp.log(1.0 + jnp.exp(-jnp.abs(x)))


def _rms_bwd(dxh, xh, r):
    return r * (dxh - xh * jnp.mean(dxh * xh, axis=-1, keepdims=True))


def _chunk_cumsum(v, row_in_chunk, reverse):
    rows = v.shape[0]
    for sh in (1, 2, 4, 8, 16, 32):
        if reverse:
            v = v + jnp.where(row_in_chunk + sh < GLA_CHUNK, pltpu.roll(v, rows - sh, axis=0), 0.0)
        else:
            v = v + jnp.where(row_in_chunk >= sh, pltpu.roll(v, sh, axis=0), 0.0)
    return v


def _inproj(x, g1, w_in_t, token):
    seq = x.shape[0]
    tm = min(seq, 512)

    def body(x_ref, g_ref, w_ref, token_ref, p_ref):
        xv = x_ref[...]
        r = lax.rsqrt(jnp.mean(xv * xv, axis=-1, keepdims=True) + EPS)
        h = (xv * r * g_ref[...]).astype(BF16)
        p_ref[:, 0:COL_U] = _nt(h, w_ref[0:ROW_LR, :])
        p_ref[:, COL_U:COL_LR] = _nt(h, w_ref[ROW_UV:PROJ_W, :])
        p_ref[:, COL_LR:PROJ_WP] = _nt(h, w_ref[ROW_LR : ROW_LR + LANES, :])

    return pl.pallas_call(
        body,
        name="inproj",
        grid=(seq // tm,),
        in_specs=[pl.BlockSpec((tm, D_MODEL), lambda i: (i, 0)), _resident((1, D_MODEL)), _resident((PROJ_W, D_MODEL)), _resident(TOKEN_SHAPE)],
        out_specs=pl.BlockSpec((tm, PROJ_WP), lambda i: (i, 0)),
        out_shape=jax.ShapeDtypeStruct((seq, PROJ_WP), F32),
        compiler_params=_params(48, ("parallel",)),
    )(x, g1, w_in_t, token)


def _gla_tile(seq):
    return min(seq, 1024)


def _gla_decay_terms(lr_bf, wd_ref, bd_ref, pair, row_in_chunk, reverse, n):
    cols = pl.ds(pair * LANES, LANES)
    pre = _nn(lr_bf, wd_ref[:, cols]) + bd_ref[:, cols]
    la = _log_sigmoid(pre) * (1.0 / GLA_TAU)
    b = _chunk_cumsum(la, row_in_chunk, reverse)
    b3 = b.reshape(n, GLA_CHUNK, LANES)
    blast = b3[:, 0:1, :] if reverse else b3[:, GLA_CHUNK - 1 : GLA_CHUNK, :]
    return pre, b3, blast


def _gla_fwd(p, wd_pad, bd, reverse):
    seq = p.shape[0]
    tg = _gla_tile(seq)
    nt = seq // tg
    n = tg // GLA_CHUNK
    scale = GLA_DK**-0.5

    def tile(i):
        return nt - 1 - i if reverse else i

    def body(q_ref, k_ref, v_ref, lr_ref, wd_ref, bd_ref, o_ref, st_ref, carry):
        @pl.when(pl.program_id(0) == 0)
        def _():
            carry[...] = jnp.zeros_like(carry)

        lr_bf = lr_ref[...].astype(BF16)
        states = [carry[h] for h in range(GLA_HEADS)]
        row_in_chunk = lax.broadcasted_iota(jnp.int32, (tg, LANES), 0) % GLA_CHUNK
        lane_head = lax.broadcasted_iota(jnp.int32, (1, LANES), 1) // GLA_DK
        tt = lax.broadcasted_iota(jnp.int32, (GLA_CHUNK, GLA_CHUNK), 0)
        ss = lax.broadcasted_iota(jnp.int32, (GLA_CHUNK, GLA_CHUNK), 1)
        causal = (tt <= ss) if reverse else (tt >= ss)
        order = range(n - 1, -1, -1) if reverse else range(n)
        heads = range(GLA_HEADS)
        qdh, kds, vhs, decs, sc_raw, dst = {}, {}, {}, {}, {}, {}
        for pair in range(2):
            cols = pl.ds(pair * LANES, LANES)
            _, b3, blast = _gla_decay_terms(lr_bf, wd_ref, bd_ref, pair, row_in_chunk, reverse, n)
            q3 = q_ref[:, cols].reshape(n, GLA_CHUNK, LANES) * scale
            k3 = k_ref[:, cols].reshape(n, GLA_CHUNK, LANES)
            qd = q3 * jnp.exp(b3)
            kd = (k3 * jnp.exp(-b3)).astype(BF16)
            kte = k3 * jnp.exp(blast - b3)
            dec = jnp.exp(blast)
            for hh in range(2):
                h = 2 * pair + hh
                m = (lane_head == hh).astype(F32)
                qdh[h], kds[h], decs[h] = (qd * m).astype(BF16), kd, dec
                vhs[h] = v_ref[:, pl.ds(h * GLA_DV, GLA_DV)].reshape(n, GLA_CHUNK, GLA_DV).astype(BF16)
                sc_raw[h] = _bnt(qdh[h], kd)
                dst[h] = _btn(vhs[h], (kte * m).astype(BF16))
        o_intra, befores = {}, {}
        for h in heads:
            o_intra[h] = _bnn(jnp.where(causal, sc_raw[h], 0.0).astype(BF16), vhs[h])
            st, before = states[h], [None] * n
            for j in order:
                before[j] = st
                st = st * decs[h][j] + dst[h][j]
            states[h] = st
            befores[h] = jnp.stack(before)
        outs = {h: (o_intra[h] + _bnt(qdh[h], befores[h].astype(BF16))).reshape(tg, GLA_DV) for h in heads}
        for h in range(GLA_HEADS):
            o_ref[:, pl.ds(h * GLA_DV, GLA_DV)] = outs[h]
            st_ref[:, h] = befores[h]
            carry[h] = states[h]

    nchunks = seq // GLA_CHUNK
    return pl.pallas_call(
        body,
        name="gla_fwd_rev" if reverse else "gla_fwd",
        grid=(nt,),
        in_specs=[
            pl.BlockSpec((tg, KEY_W), lambda i: (tile(i), COL_Q // KEY_W)),
            pl.BlockSpec((tg, KEY_W), lambda i: (tile(i), COL_K // KEY_W)),
            pl.BlockSpec((tg, GLA_W), lambda i: (tile(i), COL_V // GLA_W)),
            pl.BlockSpec((tg, LANES), lambda i: (tile(i), COL_LR // LANES)),
            _resident((LANES, KEY_W)),
            _resident((1, KEY_W)),
        ],
        out_specs=[
            pl.BlockSpec((tg, GLA_W), lambda i: (tile(i), 0)),
            pl.BlockSpec((n, GLA_HEADS, GLA_DV, LANES), lambda i: (tile(i), 0, 0, 0)),
        ],
        out_shape=[
            jax.ShapeDtypeStruct((seq, GLA_W), F32),
            jax.ShapeDtypeStruct((nchunks, GLA_HEADS, GLA_DV, LANES), F32),
        ],
        scratch_shapes=[pltpu.VMEM((GLA_HEADS, GLA_DV, LANES), F32)],
        compiler_params=_params(48),
    )(p, p, p, p, wd_pad, bd)


def _mixer_out(x, o_f, o_b, p, gn, lng, lnb, ws_bf, bs_col, w_out):
    seq = x.shape[0]
    tm = min(seq, 512)

    def body(x_ref, of_ref, ob_ref, g_ref, u_ref, vv_ref, gn_ref, lng_ref, lnb_ref, ws_ref, bs_ref, wo_ref, x1_ref, yc_ref, vn_sc):
        for h in range(GLA_HEADS):
            cols = pl.ds(h * GLA_DV, GLA_DV)
            oh = of_ref[:, cols] + ob_ref[:, cols]
            on = oh * lax.rsqrt(jnp.mean(oh * oh, axis=-1, keepdims=True) + EPS)
            gh = g_ref[:, cols]
            yc_ref[:, cols] = (on * gn_ref[:, cols] * (gh * _sigmoid(gh))).astype(BF16)
        zv = _gelu(vv_ref[...])
        xc = zv - jnp.mean(zv, axis=-1, keepdims=True)
        vhat = xc * lax.rsqrt(jnp.mean(xc * xc, axis=-1, keepdims=True) + EPS)
        vn_sc[...] = (vhat * lng_ref[...] + lnb_ref[...]).astype(BF16)
        for c in range(tm // GMLP_CHUNK):
            rows = pl.ds(c * GMLP_CHUNK, GMLP_CHUNK)
            for g in range(GMLP_GROUPS):
                cols = pl.ds(g * LANES, LANES)
                s = _nn(ws_ref[g], vn_sc[rows, cols]) + bs_ref[g]
                yc_ref[rows, pl.ds(GLA_W + g * LANES, LANES)] = (_gelu(u_ref[rows, cols]) * s).astype(BF16)
        x1_ref[...] = x_ref[...] + _nn(yc_ref[...], wo_ref[...])

    row = lambda w: pl.BlockSpec((tm, w), lambda i: (i, 0))
    pcol = lambda col: pl.BlockSpec((tm, GLA_W), lambda i: (i, col // GLA_W))
    return pl.pallas_call(
        body,
        name="mixer_out",
        grid=(seq // tm,),
        in_specs=[
            row(D_MODEL), row(GLA_W), row(GLA_W), pcol(COL_G), pcol(COL_U), pcol(COL_VV),
            _resident((1, GLA_W)), _resident((1, GMLP_W)), _resident((1, GMLP_W)),
            _resident((GMLP_GROUPS, GMLP_CHUNK, GMLP_CHUNK)), _resident((GMLP_GROUPS, GMLP_CHUNK, 1)),
            _resident((D_MODEL, D_MODEL)),
        ],
        out_specs=[row(D_MODEL), row(D_MODEL)],
        out_shape=[jax.ShapeDtypeStruct((seq, D_MODEL), F32), jax.ShapeDtypeStruct((seq, D_MODEL), BF16)],
        scratch_shapes=[pltpu.VMEM((tm, GMLP_W), BF16)],
        compiler_params=_params(48, ("parallel",)),
    )(x, o_f, o_b, p, p, p, gn, lng, lnb, ws_bf, bs_col, w_out)


def _ffn_fwd(x1, target, g2, gf, wg_t, wu_t, wd):
    seq = x1.shape[0]
    tm = min(seq, 256)

    def body(x1_ref, t_ref, g2_ref, gf_ref, wg_ref, wu_ref, wd_ref, h2_ref, gate_ref, up_ref, act_ref, dx2_ref, loss_ref, dgf_ref):
        @pl.when(pl.program_id(0) == 0)
        def _():
            loss_ref[...] = jnp.zeros_like(loss_ref)
            dgf_ref[...] = jnp.zeros_like(dgf_ref)

        x1v = x1_ref[...]
        h2 = (x1v * lax.rsqrt(jnp.mean(x1v * x1v, axis=-1, keepdims=True) + EPS) * g2_ref[...]).astype(BF16)
        h2_ref[...] = h2
        gate = _nt(h2, wg_ref[...])
        up = _nt(h2, wu_ref[...])
        act = (gate * _sigmoid(gate) * up).astype(BF16)
        gate_ref[...] = gate
        up_ref[...] = up
        act_ref[...] = act
        x2 = x1v + _nn(act, wd_ref[...])
        rf = lax.rsqrt(jnp.mean(x2 * x2, axis=-1, keepdims=True) + EPS)
        xh = x2 * rf
        err = xh * gf_ref[...] - t_ref[...]
        loss_ref[...] += 0.5 * jnp.sum(jnp.mean(err * err, axis=-1, keepdims=True))
        dy = err * (1.0 / D_MODEL)
        dgf_ref[...] += jnp.sum(dy * xh, axis=0, keepdims=True)
        dx2_ref[...] = _rms_bwd(dy * gf_ref[...], xh, rf)

    row = lambda w: pl.BlockSpec((tm, w), lambda i: (i, 0))
    weight = _resident((D_FF, D_MODEL))
    return pl.pallas_call(
        body,
        name="ffn_fwd",
        grid=(seq // tm,),
        in_specs=[row(D_MODEL), row(D_MODEL), _resident((1, D_MODEL)), _resident((1, D_MODEL)), weight, weight, weight],
        out_specs=[row(D_MODEL), row(D_FF), row(D_FF), row(D_FF), row(D_MODEL),
                   pl.BlockSpec((1, LANES), lambda i: (0, 0)), pl.BlockSpec((1, D_MODEL), lambda i: (0, 0))],
        out_shape=[
            jax.ShapeDtypeStruct((seq, D_MODEL), BF16),
            jax.ShapeDtypeStruct((seq, D_FF), F32),
            jax.ShapeDtypeStruct((seq, D_FF), F32),
            jax.ShapeDtypeStruct((seq, D_FF), BF16),
            jax.ShapeDtypeStruct((seq, D_MODEL), F32),
            jax.ShapeDtypeStruct((1, LANES), F32),
            jax.ShapeDtypeStruct((1, D_MODEL), F32),
        ],
        compiler_params=_params(56),
    )(x1, target, g2, gf, wg_t, wu_t, wd)


def _ffn_bwd(dx2, gate, up, x1, g2, wg_t, wu_t, wd):
    seq = x1.shape[0]
    tm = min(seq, 256)

    def body(dx2_ref, gate_ref, up_ref, x1_ref, g2_ref, wg_ref, wu_ref, wd_ref, dgate_ref, dup_ref, dx1_ref, dg2_ref):
        @pl.when(pl.program_id(0) == 0)
        def _():
            dg2_ref[...] = jnp.zeros_like(dg2_ref)

        dx2v = dx2_ref[...]
        dact = _nt(dx2v.astype(BF16), wd_ref[...])
        gate = gate_ref[...]
        sg = _sigmoid(gate)
        dgate = (dact * up_ref[...] * (sg * (1.0 + gate * (1.0 - sg)))).astype(BF16)
        dup = (dact * (gate * sg)).astype(BF16)
        dgate_ref[...] = dgate
        dup_ref[...] = dup
        dh2 = _nn(dgate, wg_ref[...]) + _nn(dup, wu_ref[...])
        x1v = x1_ref[...]
        r2 = lax.rsqrt(jnp.mean(x1v * x1v, axis=-1, keepdims=True) + EPS)
        xh = x1v * r2
        dg2_ref[...] += jnp.sum(dh2 * xh, axis=0, keepdims=True)
        dx1_ref[...] = dx2v + _rms_bwd(dh2 * g2_ref[...], xh, r2)

    row = lambda w: pl.BlockSpec((tm, w), lambda i: (i, 0))
    weight = _resident((D_FF, D_MODEL))
    return pl.pallas_call(
        body,
        name="ffn_bwd",
        grid=(seq // tm,),
        in_specs=[row(D_MODEL), row(D_FF), row(D_FF), row(D_MODEL), _resident((1, D_MODEL)), weight, weight, weight],
        out_specs=[row(D_FF), row(D_FF), row(D_MODEL), pl.BlockSpec((1, D_MODEL), lambda i: (0, 0))],
        out_shape=[
            jax.ShapeDtypeStruct((seq, D_FF), BF16),
            jax.ShapeDtypeStruct((seq, D_FF), BF16),
            jax.ShapeDtypeStruct((seq, D_MODEL), F32),
            jax.ShapeDtypeStruct((1, D_MODEL), F32),
        ],
        compiler_params=_params(56),
    )(dx2, gate, up, x1, g2, wg_t, wu_t, wd)


WGRAD_ROWS = D_FF // 2


def _ffn_wgrad(h2, dgate, dup, act, dx2):
    seq = h2.shape[0]
    tm = min(seq, 512)

    def body(h2_ref, dgate_ref, dup_ref, act_ref, dx2_ref, dwg_ref, dwu_ref, dwd_ref):
        @pl.when(pl.program_id(1) == 0)
        def _():
            dwg_ref[...] = jnp.zeros_like(dwg_ref)
            dwu_ref[...] = jnp.zeros_like(dwu_ref)
            dwd_ref[...] = jnp.zeros_like(dwd_ref)

        h2v = h2_ref[...]
        dwg_ref[...] += _tn(dgate_ref[...], h2v)
        dwu_ref[...] += _tn(dup_ref[...], h2v)
        dwd_ref[...] += _tn(act_ref[...], dx2_ref[...].astype(BF16))

    ff = pl.BlockSpec((tm, WGRAD_ROWS), lambda j, i: (i, j))
    row = pl.BlockSpec((tm, D_MODEL), lambda j, i: (i, 0))
    out = pl.BlockSpec((WGRAD_ROWS, D_MODEL), lambda j, i: (j, 0))
    return pl.pallas_call(
        body,
        name="ffn_wgrad",
        grid=(D_FF // WGRAD_ROWS, seq // tm),
        in_specs=[row, ff, ff, ff, row],
        out_specs=[out, out, out],
        out_shape=[jax.ShapeDtypeStruct((D_FF, D_MODEL), F32)] * 3,
        compiler_params=_params(56, ("parallel", "arbitrary")),
    )(h2, dgate, dup, act, dx2)


def _mixer_bwd(dx1, ycat, o_f, o_b, p, gn, lng, lnb, ws_bf, wst_bf, bs_col, w_out, token):
    seq = dx1.shape[0]
    tm = min(seq, 512)
    nsteps = seq // tm

    def body(dx1_ref, yc_ref, of_ref, ob_ref, g_ref, u_ref, vv_ref, gn_ref, lng_ref, lnb_ref, ws_ref, wst_ref, bs_ref, wo_ref, token_ref,
             do_ref, dg_ref, du_ref, dvv_ref, dwo_ref, dgn_ref, dlng_ref, dlnb_ref, dws_ref, dbs_ref, vn_sc, dvn_sc, dbs_acc):
        step = pl.program_id(0)

        @pl.when(step == 0)
        def _():
            for r in (dwo_ref, dgn_ref, dlng_ref, dlnb_ref, dws_ref, dbs_acc):
                r[...] = jnp.zeros_like(r)

        dx1b = dx1_ref[...].astype(BF16)
        dyc = _nt(dx1b, wo_ref[...])
        dwo_ref[...] += _tn(yc_ref[...], dx1b)
        for h in range(GLA_HEADS):
            cols = pl.ds(h * GLA_DV, GLA_DV)
            dya = dyc[:, h * GLA_DV : (h + 1) * GLA_DV]
            oh = of_ref[:, cols] + ob_ref[:, cols]
            rn = lax.rsqrt(jnp.mean(oh * oh, axis=-1, keepdims=True) + EPS)
            on = oh * rn
            gh = g_ref[:, cols]
            sg = _sigmoid(gh)
            sil = gh * sg
            gnh = gn_ref[:, cols]
            dgn_ref[:, cols] += jnp.sum(dya * on * sil, axis=0, keepdims=True)
            dg_ref[:, cols] = (dya * on * gnh * (sg * (1.0 + gh * (1.0 - sg)))).astype(BF16)
            do_ref[:, cols] = _rms_bwd(dya * gnh * sil, on, rn)
        vv = vv_ref[...]
        zv, zv_grad = _gelu_and_grad(vv)
        xc = zv - jnp.mean(zv, axis=-1, keepdims=True)
        rstd = lax.rsqrt(jnp.mean(xc * xc, axis=-1, keepdims=True) + EPS)
        vhat = xc * rstd
        vn_sc[...] = (vhat * lng_ref[...] + lnb_ref[...]).astype(BF16)
        for c in range(tm // GMLP_CHUNK):
            rows = pl.ds(c * GMLP_CHUNK, GMLP_CHUNK)
            for g in range(GMLP_GROUPS):
                cols = pl.ds(g * LANES, LANES)
                vn = vn_sc[rows, cols]
                s = _nn(ws_ref[g], vn) + bs_ref[g]
                dyb = dyc[c * GMLP_CHUNK : (c + 1) * GMLP_CHUNK, GLA_W + g * LANES : GLA_W + (g + 1) * LANES]
                zu, zu_grad = _gelu_and_grad(u_ref[rows, cols])
                du_ref[rows, cols] = (dyb * s * zu_grad).astype(BF16)
                ds = dyb * zu
                dbs_acc[g] += ds
                dsb = ds.astype(BF16)
                dws_ref[g] += _nt(dsb, vn)
                dvn_sc[rows, cols] = _nn(wst_ref[g], dsb)
        dvn = dvn_sc[...]
        dlng_ref[...] += jnp.sum(dvn * vhat, axis=0, keepdims=True)
        dlnb_ref[...] += jnp.sum(dvn, axis=0, keepdims=True)
        dvh = dvn * lng_ref[...]
        dzv = rstd * (dvh - jnp.mean(dvh, axis=-1, keepdims=True) - vhat * jnp.mean(dvh * vhat, axis=-1, keepdims=True))
        dvv_ref[...] = (dzv * zv_grad).astype(BF16)

        @pl.when(step == nsteps - 1)
        def _():
            dbs_ref[...] = jnp.sum(dbs_acc[...], axis=-1, keepdims=True)

    row = lambda w: pl.BlockSpec((tm, w), lambda i: (i, 0))
    pcol = lambda col: pl.BlockSpec((tm, GLA_W), lambda i: (i, col // GLA_W))
    const = lambda shape: pl.BlockSpec(shape, lambda i: (0,) * len(shape))
    return pl.pallas_call(
        body,
        name="mixer_bwd",
        grid=(nsteps,),
        in_specs=[
            row(D_MODEL), row(D_MODEL), row(GLA_W), row(GLA_W), pcol(COL_G), pcol(COL_U), pcol(COL_VV),
            _resident((1, GLA_W)), _resident((1, GMLP_W)), _resident((1, GMLP_W)),
            _resident((GMLP_GROUPS, GMLP_CHUNK, GMLP_CHUNK)), _resident((GMLP_GROUPS, GMLP_CHUNK, GMLP_CHUNK)),
            _resident((GMLP_GROUPS, GMLP_CHUNK, 1)), _resident((D_MODEL, D_MODEL)), _resident(TOKEN_SHAPE),
        ],
        out_specs=[
            row(GLA_W), row(GLA_W), row(GMLP_W), row(GMLP_W), const((D_MODEL, D_MODEL)),
            const((1, GLA_W)), const((1, GMLP_W)), const((1, GMLP_W)),
            const((GMLP_GROUPS, GMLP_CHUNK, GMLP_CHUNK)), const((GMLP_GROUPS, GMLP_CHUNK, 1)),
        ],
        out_shape=[
            jax.ShapeDtypeStruct((seq, GLA_W), F32), jax.ShapeDtypeStruct((seq, GLA_W), BF16),
            jax.ShapeDtypeStruct((seq, GMLP_W), BF16), jax.ShapeDtypeStruct((seq, GMLP_W), BF16),
            jax.ShapeDtypeStruct((D_MODEL, D_MODEL), F32),
            jax.ShapeDtypeStruct((1, GLA_W), F32), jax.ShapeDtypeStruct((1, GMLP_W), F32), jax.ShapeDtypeStruct((1, GMLP_W), F32),
            jax.ShapeDtypeStruct((GMLP_GROUPS, GMLP_CHUNK, GMLP_CHUNK), F32), jax.ShapeDtypeStruct((GMLP_GROUPS, GMLP_CHUNK, 1), F32),
        ],
        scratch_shapes=[pltpu.VMEM((tm, GMLP_W), BF16), pltpu.VMEM((tm, GMLP_W), F32), pltpu.VMEM((GMLP_GROUPS, GMLP_CHUNK, GMLP_CHUNK), F32)],
        compiler_params=_params(56),
    )(dx1, ycat, o_f, o_b, p, p, p, gn, lng, lnb, ws_bf, wst_bf, bs_col, w_out, token)


def _gla_bwd(p, do, st, wd_pad, bd, token, reverse, other=None):
    seq = p.shape[0]
    tg = _gla_tile(seq)
    nt = seq // tg
    n = tg // GLA_CHUNK
    scale = GLA_DK**-0.5

    def tile(i):
        return i if reverse else nt - 1 - i

    def body(q_ref, k_ref, v_ref, lr_ref, do_ref, st_ref, wd_ref, bd_ref, token_ref, *rest):
        others, (dq_ref, dk_ref, dv_ref, dlr_ref, dwd_ref, dbd_ref, carry) = rest[:-7], rest[-7:]
        if others:
            odq_ref, odk_ref, odv_ref, odlr_ref = others

            def put(ref, idx, val, oref):
                ref[idx] = (val + oref[idx]).astype(BF16)
        else:
            odq_ref = odk_ref = odv_ref = odlr_ref = None

            def put(ref, idx, val, oref):
                ref[idx] = val

        @pl.when(pl.program_id(0) == 0)
        def _():
            carry[...] = jnp.zeros_like(carry)
            dwd_ref[...] = jnp.zeros_like(dwd_ref)
            dbd_ref[...] = jnp.zeros_like(dbd_ref)

        lr_bf = lr_ref[...].astype(BF16)
        carries = [carry[h] for h in range(GLA_HEADS)]
        row_in_chunk = lax.broadcasted_iota(jnp.int32, (tg, LANES), 0) % GLA_CHUNK
        lane_head = lax.broadcasted_iota(jnp.int32, (1, LANES), 1) // GLA_DK
        tt = lax.broadcasted_iota(jnp.int32, (GLA_CHUNK, GLA_CHUNK), 0)
        ss = lax.broadcasted_iota(jnp.int32, (GLA_CHUNK, GLA_CHUNK), 1)
        causal = (tt <= ss) if reverse else (tt >= ss)
        causal_t = (tt >= ss) if reverse else (tt <= ss)
        order = range(n) if reverse else range(n - 1, -1, -1)
        dlr = jnp.zeros((tg, LANES), F32)
        heads = range(GLA_HEADS)
        pv, masks, qdh, kteh, vhs, dohs, stbs = {}, {}, {}, {}, {}, {}, {}
        sc_t, dp, dp_t, acc = {}, {}, {}, {}
        for pair in range(2):
            cols = pl.ds(pair * LANES, LANES)
            pre, b3, blast = _gla_decay_terms(lr_bf, wd_ref, bd_ref, pair, row_in_chunk, reverse, n)
            q3 = q_ref[:, cols].reshape(n, GLA_CHUNK, LANES) * scale
            k3 = k_ref[:, cols].reshape(n, GLA_CHUNK, LANES)
            eb = jnp.exp(b3)
            emb = jnp.exp(-b3)
            ekte = jnp.exp(blast - b3)
            kdf = k3 * emb
            pv[pair] = dict(pre=pre, eb=eb, emb=emb, ekte=ekte, qd=q3 * eb, kdf=kdf, kd=kdf.astype(BF16), kte=k3 * ekte, dec=jnp.exp(blast))
            for hh in range(2):
                h = 2 * pair + hh
                vcols = pl.ds(h * GLA_DV, GLA_DV)
                masks[h] = (lane_head == hh).astype(F32)
                qdh[h] = (pv[pair]["qd"] * masks[h]).astype(BF16)
                kteh[h] = (pv[pair]["kte"] * masks[h]).astype(BF16)
                vhs[h] = v_ref[:, vcols].reshape(n, GLA_CHUNK, GLA_DV).astype(BF16)
                dohs[h] = do_ref[:, vcols].reshape(n, GLA_CHUNK, GLA_DV).astype(BF16)
                stbs[h] = st_ref[:, h]
                sc_t[h] = _bnt(pv[pair]["kd"], qdh[h])
                dp[h] = _bnt(dohs[h], vhs[h])
                dp_t[h] = _bnt(vhs[h], dohs[h])
                acc[h] = _btn(dohs[h], qdh[h])
        dsa = {}
        for h in heads:
            sc_t[h] = jnp.where(causal_t, sc_t[h], 0.0).astype(BF16)
            dp[h] = jnp.where(causal, dp[h], 0.0).astype(BF16)
            dp_t[h] = jnp.where(causal_t, dp_t[h], 0.0).astype(BF16)
            dec = pv[h // 2]["dec"]
            c, after = carries[h], [None] * n
            for j in order:
                after[j] = c
                c = acc[h][j] + dec[j] * c
            carries[h] = c
            dsa[h] = jnp.stack(after)
        dvs, dqs, dks, dwds, dbds = [], [], [], [], []
        for pair in range(2):
            cols = pl.ds(pair * LANES, LANES)
            v = pv[pair]
            dqd = jnp.zeros((n, GLA_CHUNK, LANES), F32)
            dkd = jnp.zeros((n, GLA_CHUNK, LANES), F32)
            dkte = jnp.zeros((n, GLA_CHUNK, LANES), F32)
            ddec = jnp.zeros((n, 1, LANES), F32)
            for h in (2 * pair, 2 * pair + 1):
                dsa_bf = dsa[h].astype(BF16)
                dqd = dqd + (_bnn(dp[h], v["kd"]) * masks[h] + _bnn(dohs[h], stbs[h].astype(BF16)))
                dkd = dkd + _bnn(dp_t[h], qdh[h])
                dkte = dkte + _bnn(vhs[h], dsa_bf)
                ddec = ddec + jnp.sum(dsa[h] * stbs[h], axis=1, keepdims=True)
                dvs.append((_bnn(sc_t[h], dohs[h]) + _bnt(kteh[h], dsa_bf)).reshape(tg, GLA_DV))
            dqs.append((dqd * (scale * v["eb"])).reshape(tg, LANES))
            dks.append((dkd * v["emb"] + dkte * v["ekte"]).reshape(tg, LANES))
            db = dqd * v["qd"] - dkd * v["kdf"] - dkte * v["kte"]
            dblast = jnp.sum(dkte * v["kte"], axis=1, keepdims=True) + ddec * v["dec"]
            dla = _chunk_cumsum(db.reshape(tg, LANES), row_in_chunk, not reverse) + jnp.broadcast_to(dblast, (n, GLA_CHUNK, LANES)).reshape(tg, LANES)
            dpre = (dla * (1.0 / GLA_TAU) * _sigmoid(-v["pre"]))
            dpre_bf = dpre.astype(BF16)
            dlr = dlr + _nt(dpre_bf, wd_ref[:, cols])
            dwds.append(_tn(lr_bf, dpre_bf))
            dbds.append(jnp.sum(dpre, axis=0, keepdims=True))
        put(dlr_ref, (slice(None), slice(None)), dlr, odlr_ref)
        for pair in range(2):
            cols = pl.ds(pair * LANES, LANES)
            put(dq_ref, (slice(None), cols), dqs[pair], odq_ref)
            put(dk_ref, (slice(None), cols), dks[pair], odk_ref)
            dwd_ref[:, cols] += dwds[pair]
            dbd_ref[:, cols] += dbds[pair]
        for h in range(GLA_HEADS):
            put(dv_ref, (slice(None), pl.ds(h * GLA_DV, GLA_DV)), dvs[h], odv_ref)
            carry[h] = carries[h]

    pieces = [
        pl.BlockSpec((tg, KEY_W), lambda i: (tile(i), 0)),
        pl.BlockSpec((tg, KEY_W), lambda i: (tile(i), 0)),
        pl.BlockSpec((tg, GLA_W), lambda i: (tile(i), 0)),
        pl.BlockSpec((tg, LANES), lambda i: (tile(i), 0)),
    ]
    piece_dtype = BF16 if other else F32
    return pl.pallas_call(
        body,
        name="gla_bwd_rev" if reverse else "gla_bwd",
        grid=(nt,),
        in_specs=[
            pl.BlockSpec((tg, KEY_W), lambda i: (tile(i), COL_Q // KEY_W)),
            pl.BlockSpec((tg, KEY_W), lambda i: (tile(i), COL_K // KEY_W)),
            pl.BlockSpec((tg, GLA_W), lambda i: (tile(i), COL_V // GLA_W)),
            pl.BlockSpec((tg, LANES), lambda i: (tile(i), COL_LR // LANES)),
            pl.BlockSpec((tg, GLA_W), lambda i: (tile(i), 0)),
            pl.BlockSpec((n, GLA_HEADS, GLA_DV, LANES), lambda i: (tile(i), 0, 0, 0)),
            _resident((LANES, KEY_W)),
            _resident((1, KEY_W)),
            _resident(TOKEN_SHAPE),
        ] + (pieces if other else []),
        out_specs=pieces + [pl.BlockSpec((LANES, KEY_W), lambda i: (0, 0)), pl.BlockSpec((1, KEY_W), lambda i: (0, 0))],
        out_shape=[
            jax.ShapeDtypeStruct((seq, KEY_W), piece_dtype), jax.ShapeDtypeStruct((seq, KEY_W), piece_dtype),
            jax.ShapeDtypeStruct((seq, GLA_W), piece_dtype), jax.ShapeDtypeStruct((seq, LANES), piece_dtype),
            jax.ShapeDtypeStruct((LANES, KEY_W), F32), jax.ShapeDtypeStruct((1, KEY_W), F32),
        ],
        scratch_shapes=[pltpu.VMEM((GLA_HEADS, GLA_DV, LANES), F32)],
        compiler_params=_params(56),
    )(p, p, p, p, do, st, wd_pad, bd, token, *(other or ()))


def _inproj_wgrad(x, g1, dq, dk, dv, dg, du, dvv, dlr):
    seq = x.shape[0]
    tm = min(seq, 512)

    def body(x_ref, g1_ref, dq_ref, dk_ref, dv_ref, dg_ref, du_ref, dvv_ref, dlr_ref, dw_ref, dp_ref):
        @pl.when(pl.program_id(0) == 0)
        def _():
            dw_ref[...] = jnp.zeros_like(dw_ref)

        for col, ref in ((COL_Q, dq_ref), (COL_K, dk_ref), (COL_V, dv_ref), (COL_G, dg_ref), (COL_U, du_ref), (COL_VV, dvv_ref), (COL_LR, dlr_ref)):
            dp_ref[:, col : col + ref.shape[1]] = ref[...]
        xv = x_ref[...]
        h = (xv * lax.rsqrt(jnp.mean(xv * xv, axis=-1, keepdims=True) + EPS) * g1_ref[...]).astype(BF16)
        dw_ref[0:ROW_LR, :] += _tn(dp_ref[:, 0:COL_U], h)
        dw_ref[ROW_UV:PROJ_W, :] += _tn(dp_ref[:, COL_U:COL_LR], h)
        dw_ref[ROW_LR:ROW_UV, :] += _tn(dp_ref[:, COL_LR:PROJ_WP], h)[0 : ROW_UV - ROW_LR]

    row = lambda w: pl.BlockSpec((tm, w), lambda i: (i, 0))
    return pl.pallas_call(
        body,
        name="inproj_wgrad",
        grid=(seq // tm,),
        in_specs=[row(D_MODEL), _resident((1, D_MODEL)), row(KEY_W), row(KEY_W), row(GLA_W), row(GLA_W), row(GMLP_W), row(GMLP_W), row(LANES)],
        out_specs=[pl.BlockSpec((PROJ_W, D_MODEL), lambda i: (0, 0)), row(PROJ_WP)],
        out_shape=[jax.ShapeDtypeStruct((PROJ_W, D_MODEL), F32), jax.ShapeDtypeStruct((seq, PROJ_WP), BF16)],
        compiler_params=_params(56),
    )(x, g1, dq, dk, dv, dg, du, dvv, dlr)


def _inproj_dx(x, dx1, g1, w_in_t, dp, token):
    seq = x.shape[0]
    tm = min(seq, 512)

    def body(x_ref, dx1_ref, g1_ref, w_ref, dp_ref, token_ref, dx_ref, dg1_ref):
        @pl.when(pl.program_id(0) == 0)
        def _():
            dg1_ref[...] = jnp.zeros_like(dg1_ref)

        xv = x_ref[...]
        r1 = lax.rsqrt(jnp.mean(xv * xv, axis=-1, keepdims=True) + EPS)
        xh = xv * r1
        dh = (_nn(dp_ref[:, 0:COL_U], w_ref[0:ROW_LR, :]) + _nn(dp_ref[:, COL_U:COL_LR], w_ref[ROW_UV:PROJ_W, :])
              + _nn(dp_ref[:, COL_LR:PROJ_WP], w_ref[ROW_LR : ROW_LR + LANES, :]))
        dg1_ref[...] += jnp.sum(dh * xh, axis=0, keepdims=True)
        dx_ref[...] = dx1_ref[...] + _rms_bwd(dh * g1_ref[...], xh, r1)

    row = lambda w: pl.BlockSpec((tm, w), lambda i: (i, 0))
    return pl.pallas_call(
        body,
        name="inproj_dx",
        grid=(seq // tm,),
        in_specs=[row(D_MODEL), row(D_MODEL), _resident((1, D_MODEL)), _resident((PROJ_W, D_MODEL)), row(PROJ_WP), _resident(TOKEN_SHAPE)],
        out_specs=[row(D_MODEL), pl.BlockSpec((1, D_MODEL), lambda i: (0, 0))],
        out_shape=[jax.ShapeDtypeStruct((seq, D_MODEL), F32), jax.ShapeDtypeStruct((1, D_MODEL), F32)],
        compiler_params=_params(48),
    )(x, dx1, g1, w_in_t, dp, token)


def _in_hbm(a):
    return pltpu.with_memory_space_constraint(a, pltpu.HBM)


STREAM_STEPS = 8


def _row_tile(rows, multiple=8, steps=STREAM_STEPS):
    fits = [t for t in range(multiple, rows + 1, multiple) if rows % t == 0]
    deep = [t for t in fits if rows // t >= steps]
    return max(deep) if deep else (min(fits) if fits else rows)


def _cast_into_slot(w, shard):
    rows, cols = w.shape
    tr = _row_tile(rows, 16)

    def body(s_ref, w_ref, o_ref):
        o_ref[...] = w_ref[...].astype(BF16)

    return pl.pallas_call(
        body,
        name="cast_into_slot",
        grid_spec=pltpu.PrefetchScalarGridSpec(
            num_scalar_prefetch=1,
            grid=(rows // tr,),
            in_specs=[pl.BlockSpec((tr, cols), lambda i, s_ref: (i, 0))],
            out_specs=pl.BlockSpec((None, tr, cols), lambda i, s_ref: (s_ref[0], i, 0)),
        ),
        out_shape=pltpu.HBM((N_SHARDS, rows, cols), BF16),
        compiler_params=_params(32, ("parallel",)),
    )(shard, _in_hbm(w))


def _add_halves(grads4, recvs, c):
    n = len(grads4)
    _, rows, _ = grads4[0].shape
    tr = _row_tile(rows, 16, STREAM_STEPS // N_SHARDS)

    def body(c_ref, *refs):
        for k in range(n):
            total = refs[k][...] + refs[n + k][...]
            refs[2 * n + k][...] = total
            refs[3 * n + k][...] = total.astype(BF16)

    out = pl.BlockSpec((None, tr, HALF), lambda s, i, c_ref: (s, i, 0))
    mine = pl.BlockSpec((None, tr, HALF), lambda s, i, c_ref: (s, i, c_ref[0]))
    outs = pl.pallas_call(
        body,
        name="add_halves",
        grid_spec=pltpu.PrefetchScalarGridSpec(
            num_scalar_prefetch=1,
            grid=(N_SHARDS, rows // tr),
            in_specs=[mine] * n + [out] * n,
            out_specs=[out] * (2 * n),
        ),
        out_shape=[pltpu.HBM((N_SHARDS, rows, HALF), F32)] * n + [pltpu.HBM((N_SHARDS, rows, HALF), BF16)] * n,
        compiler_params=_params(48, ("parallel", "parallel")),
    )(c, *[_in_hbm(a) for a in list(grads4) + list(recvs)])
    return list(zip(outs[:n], outs[n:]))


def _add_partials(part4, recv3, shard_core):
    _, rows, _ = part4.shape
    tr = _row_tile(rows, 16)

    def body(sc_ref, p_ref, r_ref, o_ref):
        o_ref[...] = ((p_ref[...] + r_ref[0].astype(F32)) + r_ref[1].astype(F32)) + r_ref[2].astype(F32)

    return pl.pallas_call(
        body,
        name="add_partials",
        grid_spec=pltpu.PrefetchScalarGridSpec(
            num_scalar_prefetch=1,
            grid=(rows // tr,),
            in_specs=[
                pl.BlockSpec((None, tr, HALF), lambda i, sc_ref: (sc_ref[0], i, 0)),
                pl.BlockSpec((3, tr, HALF), lambda i, sc_ref: (0, i, 0)),
            ],
            out_specs=pl.BlockSpec((tr, HALF), lambda i, sc_ref: (i, sc_ref[1])),
        ),
        out_shape=pltpu.HBM((rows, 2 * HALF), F32),
        compiler_params=_params(32, ("parallel",)),
    )(shard_core, _in_hbm(part4), _in_hbm(recv3))


def _adam_math(w, g, m, v):
    m = ADAM_B1 * m + (1.0 - ADAM_B1) * g
    v = ADAM_B2 * v + (1.0 - ADAM_B2) * (g * g)
    m_hat = m / (1.0 - ADAM_B1**ADAM_STEP)
    v_hat = v / (1.0 - ADAM_B2**ADAM_STEP)
    delta = -ADAM_LR * (m_hat / (jnp.sqrt(v_hat) + ADAM_EPS) + ADAM_WD * w)
    return delta, m, v


def _adamw(w, g, m, v):
    rows, cols = w.shape
    tr = _row_tile(rows)

    def body(w_ref, g_ref, m_ref, v_ref, go_ref, d_ref, mo_ref, vo_ref):
        gv = g_ref[...]
        go_ref[...] = gv
        d_ref[...], mo_ref[...], vo_ref[...] = _adam_math(w_ref[...], gv, m_ref[...], v_ref[...])

    spec = pl.BlockSpec((tr, cols), lambda i: (i, 0))
    return pl.pallas_call(
        body, name="adamw", grid=(rows // tr,), in_specs=[spec] * 4, out_specs=[spec] * 4, out_shape=[pltpu.HBM(w.shape, F32)] * 4,
        compiler_params=_params(32, ("parallel",)),
    )(_in_hbm(w), _in_hbm(g), _in_hbm(m), _in_hbm(v))


SMALL_ROWS = 560
DECAY_ROWS = 8
SMALL_TOTAL = SMALL_ROWS + 2 * N_SHARDS * DECAY_ROWS


def _adamw_small(gathered, wp, mp, vp):
    out_rows = SMALL_ROWS + 2 * DECAY_ROWS

    def body(ga_ref, w_ref, m_ref, v_ref, g_ref, d_ref, mo_ref, vo_ref):
        shard = 2 * lax.axis_index("x") + lax.axis_index("y")
        g_ref[pl.ds(0, SMALL_ROWS), :] = functools.reduce(lambda a, b: a + b, [ga_ref[d, pl.ds(0, SMALL_ROWS), :] for d in range(8)])
        for k in range(2):
            start = pl.multiple_of(SMALL_ROWS + k * N_SHARDS * DECAY_ROWS + shard * DECAY_ROWS, DECAY_ROWS)
            g_ref[pl.ds(SMALL_ROWS + k * DECAY_ROWS, DECAY_ROWS), :] = functools.reduce(
                lambda a, b: a + b, [ga_ref[d, pl.ds(start, DECAY_ROWS), :] for d in range(8)])
        d_ref[...], mo_ref[...], vo_ref[...] = _adam_math(w_ref[...], g_ref[...], m_ref[...], v_ref[...])

    shape = jax.ShapeDtypeStruct((out_rows, LANES), F32)
    return pl.pallas_call(body, name="adamw_small", out_shape=[shape] * 4, compiler_params=_params(32, None))(gathered, wp, mp, vp)


ANY = pl.BlockSpec(memory_space=pl.ANY)


def _position():
    return lax.axis_index("x"), lax.axis_index("y"), lax.axis_index("c")


def _other_chips(x, y):
    return [(1 - x, y), (x, 1 - y), (1 - x, 1 - y)]


HBM = pl.BlockSpec(memory_space=pltpu.HBM)
SEM = pl.BlockSpec(memory_space=pltpu.SEMAPHORE)
TOKEN = jax.ShapeDtypeStruct(TOKEN_SHAPE, F32)
DATAFLOW = pltpu.SideEffectType.DATAFLOW_SIDE_EFFECTING


def _half_block(ref4, slot, core):
    return ref4.at[slot, :, pl.ds(pl.multiple_of(core * HALF, HALF), HALF)]


def _gather_ici_copies(refs4, send_sems, recv_sems, stride):
    x, y, c = _position()
    pairs = []
    for k, ref4 in enumerate(refs4):
        mine = _half_block(ref4, 2 * x + y, c)
        for j, (px, py) in enumerate(_other_chips(x, y)):
            sems = dict(send_sem=send_sems.at[stride * k + j], recv_sem=recv_sems.at[stride * k + j], device_id=(px, py, c), device_id_type=MESH)
            pairs.append((functools.partial(pltpu.make_async_remote_copy, src_ref=mine, dst_ref=mine, **sems),
                          functools.partial(pltpu.make_async_remote_copy, src_ref=mine, dst_ref=_half_block(ref4, 2 * px + py, c), **sems)))
    return pairs


def _gather_d2d_copies(refs4, send_sems, recv_sems, stride, offset):
    x, y, c = _position()
    pairs = []
    for k, ref4 in enumerate(refs4):
        for j, (px, py) in enumerate(_other_chips(x, y)):
            have = _half_block(ref4, 2 * px + py, c)
            sems = dict(send_sem=send_sems.at[stride * k + offset + j], recv_sem=recv_sems.at[stride * k + offset + j],
                        device_id=(x, y, 1 - c), device_id_type=MESH)
            pairs.append((functools.partial(pltpu.make_async_remote_copy, src_ref=have, dst_ref=have, **sems),
                          functools.partial(pltpu.make_async_remote_copy, src_ref=have, dst_ref=_half_block(ref4, 2 * px + py, 1 - c), **sems)))
    return pairs


def _gather_sync(bufs):
    n = len(bufs)

    def body(*refs):
        outs = refs[n : 2 * n]
        send_sems, recv_sems = refs[2 * n :]
        ici = _gather_ici_copies(outs, send_sems, recv_sems, 6)
        d2d = _gather_d2d_copies(outs, send_sems, recv_sems, 6, 3)
        for send, _ in ici:
            send().start()
        for (_, arrival), (forward, _) in zip(ici, d2d):
            arrival().wait_recv()
            forward().start()
        for _, arrival in d2d:
            arrival().wait_recv()
        for send, _ in ici + d2d:
            send().wait_send()

    return pl.pallas_call(
        body,
        name="gather_sync",
        in_specs=[ANY] * n,
        out_specs=[ANY] * n,
        out_shape=[jax.ShapeDtypeStruct(b.shape, b.dtype) for b in bufs],
        input_output_aliases={k: k for k in range(n)},
        scratch_shapes=[pltpu.SemaphoreType.DMA((6 * n,)), pltpu.SemaphoreType.DMA((6 * n,))],
        compiler_params=pltpu.CompilerParams(has_side_effects=True),
    )(*bufs)


def _gather_start(bufs, after):
    n, na = len(bufs), len(after)

    def body(*refs):
        ins = refs[:n]
        send_sems, recv_sems = refs[n + na], refs[n + na + 1]
        token = refs[2 * n + na + 2]
        for send, _ in _gather_ici_copies(ins, send_sems, recv_sems, 3):
            send().start()
        token[...] = jnp.zeros_like(token)

    out = pl.pallas_call(
        body,
        name="gather_start",
        in_specs=[HBM] * n + [ANY] * na,
        out_specs=(SEM, SEM, *[HBM] * n, pl.BlockSpec(memory_space=pltpu.VMEM)),
        out_shape=(pltpu.SemaphoreType.DMA((3 * n,)), pltpu.SemaphoreType.DMA((3 * n,)), *[pltpu.HBM(b.shape, b.dtype) for b in bufs], TOKEN),
        input_output_aliases={k: 2 + k for k in range(n)},
        compiler_params=pltpu.CompilerParams(has_side_effects=DATAFLOW),
    )(*[pltpu.with_memory_space_constraint(b, pltpu.HBM) for b in bufs], *after)
    return out[0], out[1], list(out[2 : 2 + n]), out[2 + n]


def _gather_wait(send_sems, recv_sems, bufs, after):
    n = len(bufs)

    def body(*refs):
        ins = refs[:n]
        for send, arrival in _gather_ici_copies(ins, refs[n], refs[n + 1], 3):
            send().wait_send()
            arrival().wait_recv()

    return pl.pallas_call(
        body,
        name="gather_wait",
        in_specs=[HBM] * n + [SEM, SEM] + [ANY] * len(after),
        out_specs=tuple([HBM] * n),
        out_shape=tuple(pltpu.HBM(b.shape, b.dtype) for b in bufs),
        input_output_aliases={k: k for k in range(n)},
        compiler_params=pltpu.CompilerParams(has_side_effects=DATAFLOW),
    )(*bufs, send_sems, recv_sems, *after)


def _gather_forward(bufs):
    n = len(bufs)

    def body(*refs):
        outs = refs[n : 2 * n]
        send_sems, recv_sems = refs[2 * n :]
        d2d = _gather_d2d_copies(outs, send_sems, recv_sems, 3, 0)
        for forward, _ in d2d:
            forward().start()
        for forward, arrival in d2d:
            arrival().wait_recv()
            forward().wait_send()

    return pl.pallas_call(
        body,
        name="gather_forward",
        in_specs=[ANY] * n,
        out_specs=[ANY] * n,
        out_shape=[jax.ShapeDtypeStruct(b.shape, b.dtype) for b in bufs],
        input_output_aliases={k: k for k in range(n)},
        scratch_shapes=[pltpu.SemaphoreType.DMA((3 * n,)), pltpu.SemaphoreType.DMA((3 * n,))],
        compiler_params=pltpu.CompilerParams(has_side_effects=True),
    )(*bufs)


def _exchange_halves(grads4):
    n = len(grads4)

    def body(*refs):
        ins, outs = refs[:n], refs[n : 2 * n]
        send_sems, recv_sems = refs[2 * n :]
        x, y, c = _position()
        copies = []
        for k in range(n):
            cp = pltpu.make_async_remote_copy(
                src_ref=ins[k].at[:, :, pl.ds(pl.multiple_of((1 - c) * HALF, HALF), HALF)], dst_ref=outs[k],
                send_sem=send_sems.at[k], recv_sem=recv_sems.at[k], device_id=(x, y, 1 - c), device_id_type=MESH)
            cp.start()
            copies.append(cp)
        for cp in copies:
            cp.wait()

    return pl.pallas_call(
        body,
        name="exchange_halves",
        in_specs=[ANY] * n,
        out_specs=[ANY] * n,
        out_shape=[jax.ShapeDtypeStruct((N_SHARDS, g.shape[1], HALF), g.dtype) for g in grads4],
        scratch_shapes=[pltpu.SemaphoreType.DMA((n,)), pltpu.SemaphoreType.DMA((n,))],
        compiler_params=pltpu.CompilerParams(has_side_effects=True),
    )(*grads4)


def _scatter_copies(parts, lands, send_sems, recv_sems):
    x, y, c = _position()
    copies = []
    for k in range(len(parts)):
        for j, (px, py) in enumerate(_other_chips(x, y)):
            copies.append(pltpu.make_async_remote_copy(
                src_ref=parts[k].at[2 * px + py], dst_ref=lands[k].at[j],
                send_sem=send_sems.at[3 * k + j], recv_sem=recv_sems.at[3 * k + j], device_id=(px, py, c), device_id_type=MESH))
    return copies


def _exchange_copies(grads, lands, send_sems, recv_sems):
    x, y, c = _position()
    return [pltpu.make_async_remote_copy(
        src_ref=grads[k].at[:, :, pl.ds(pl.multiple_of((1 - c) * HALF, HALF), HALF)], dst_ref=lands[k],
        send_sem=send_sems.at[k], recv_sem=recv_sems.at[k], device_id=(x, y, 1 - c), device_id_type=MESH) for k in range(len(grads))]


def _exchange_lands(grads4):
    return [jax.ShapeDtypeStruct((N_SHARDS, g.shape[1], HALF), g.dtype) for g in grads4]


def _scatter_lands(parts4):
    return [jax.ShapeDtypeStruct((3,) + g.shape[1:], g.dtype) for g in parts4]


def _split_start(name, srcs, land_shapes, make_copies, nsem):
    n, nl = len(srcs), len(land_shapes)
    lands = [lax.empty(a.shape, a.dtype) for a in land_shapes]

    def body(*refs):
        send_sems, recv_sems = refs[n + nl], refs[n + nl + 1]
        token = refs[2 * (n + nl) + 2]
        for cp in make_copies(refs[:n], refs[n : n + nl], send_sems, recv_sems):
            cp.start()
        token[...] = jnp.zeros_like(token)

    hbm = lambda a: pltpu.HBM(a.shape, a.dtype)
    out = pl.pallas_call(
        body,
        name=name,
        in_specs=[HBM] * (n + nl),
        out_specs=(SEM, SEM, *[HBM] * (n + nl), pl.BlockSpec(memory_space=pltpu.VMEM)),
        out_shape=(pltpu.SemaphoreType.DMA((nsem,)), pltpu.SemaphoreType.DMA((nsem,)), *[hbm(a) for a in srcs + lands], TOKEN),
        input_output_aliases={k: 2 + k for k in range(n + nl)},
        compiler_params=pltpu.CompilerParams(has_side_effects=DATAFLOW),
    )(*[pltpu.with_memory_space_constraint(a, pltpu.HBM) for a in srcs + lands])
    return out[0], out[1], list(out[2 : 2 + n]), list(out[2 + n : 2 + n + nl]), out[2 + n + nl]


def _split_wait(name, send_sems, recv_sems, srcs, lands, make_copies, after):
    n, nl = len(srcs), len(lands)

    def body(*refs):
        for cp in make_copies(refs[:n], refs[n : n + nl], refs[n + nl], refs[n + nl + 1]):
            cp.wait_send()
            cp.wait_recv()

    hbm = lambda a: pltpu.HBM(a.shape, a.dtype)
    out = pl.pallas_call(
        body,
        name=name,
        in_specs=[HBM] * (n + nl) + [SEM, SEM] + [ANY] * len(after),
        out_specs=tuple([HBM] * (n + nl)),
        out_shape=tuple(hbm(a) for a in srcs + lands),
        input_output_aliases={k: k for k in range(n + nl)},
        compiler_params=pltpu.CompilerParams(has_side_effects=DATAFLOW),
    )(*srcs, *lands, send_sems, recv_sems, *after)
    return list(out[:n]), list(out[n:])


def _join_halves(bufs):
    n = len(bufs)

    def body(*refs):
        outs = refs[n : 2 * n]
        send_sems, recv_sems = refs[2 * n :]
        x, y, c = _position()
        half = lambda ref, core: ref.at[:, pl.ds(pl.multiple_of(core * HALF, HALF), HALF)]
        for k in range(n):
            mine = half(outs[k], c)
            pltpu.make_async_remote_copy(
                src_ref=mine, dst_ref=mine, send_sem=send_sems.at[k], recv_sem=recv_sems.at[k],
                device_id=(x, y, 1 - c), device_id_type=MESH).start()
        for k in range(n):
            wait = pltpu.make_async_remote_copy(
                src_ref=half(outs[k], c), dst_ref=half(outs[k], 1 - c), send_sem=send_sems.at[k], recv_sem=recv_sems.at[k],
                device_id=(x, y, 1 - c), device_id_type=MESH)
            wait.wait_send()
            wait.wait_recv()

    return pl.pallas_call(
        body,
        name="join_halves",
        in_specs=[ANY] * n,
        out_specs=[ANY] * n,
        out_shape=[jax.ShapeDtypeStruct(b.shape, b.dtype) for b in bufs],
        input_output_aliases={k: k for k in range(n)},
        scratch_shapes=[pltpu.SemaphoreType.DMA((n,)), pltpu.SemaphoreType.DMA((n,))],
        compiler_params=pltpu.CompilerParams(has_side_effects=True),
    )(*bufs)


def _allgather_small(block):
    m_per, ncol = block.shape

    def body(x_ref, out_ref, send_sems, recv_sems, local_sem):
        x, y, c = _position()
        me, sibling = (x, y, c), (x, y, 1 - c)
        chips = _other_chips(x, y)

        def rows(px, py, pc):
            return out_ref.at[4 * px + 2 * py + pc]

        def copy(k, blk, to, src=None):
            return pltpu.make_async_remote_copy(
                src_ref=rows(*blk) if src is None else src, dst_ref=rows(*blk),
                send_sem=send_sems.at[k], recv_sem=recv_sems.at[k], device_id=to, device_id_type=MESH)

        mine = pltpu.make_async_copy(x_ref, rows(*me), local_sem)
        mine.start()
        first = [copy(0, me, sibling, src=x_ref)] + [copy(1 + j, me, (*chip, c), src=x_ref) for j, chip in enumerate(chips)]
        for cp in first:
            cp.start()
        passed = [copy(4 + j, (*chip, c), sibling) for j, chip in enumerate(chips)]
        for j, chip in enumerate(chips):
            copy(1 + j, (*chip, c), me).wait_recv()
            passed[j].start()
        copy(0, sibling, me).wait_recv()
        for j, chip in enumerate(chips):
            copy(4 + j, (*chip, 1 - c), me).wait_recv()
        for cp in first + passed:
            cp.wait_send()
        mine.wait()

    return pl.pallas_call(
        body,
        name="allgather_small",
        in_specs=[pl.BlockSpec(memory_space=pltpu.VMEM)],
        out_specs=pl.BlockSpec(memory_space=pltpu.VMEM),
        out_shape=jax.ShapeDtypeStruct((8, m_per, ncol), block.dtype),
        scratch_shapes=[pltpu.SemaphoreType.DMA((7,)), pltpu.SemaphoreType.DMA((7,)), pltpu.SemaphoreType.DMA],
        compiler_params=pltpu.CompilerParams(has_side_effects=True, vmem_limit_bytes=32 * MIB),
    )(block)


SMALL_NAMES = ["norm1_g", "b_decay_f", "b_decay_b", "gla_norm_g", "gmlp_ln_g", "gmlp_ln_b", "w_spatial", "b_spatial", "norm2_g", "final_norm_g"]


def _pack_small(parts, decay_parts):
    flat = jnp.concatenate([a.reshape(-1) for a in parts])
    flat = jnp.pad(flat, (0, SMALL_ROWS * LANES - flat.shape[0])).reshape(SMALL_ROWS, LANES)
    return jnp.concatenate([flat] + [d.reshape(-1, LANES) for d in decay_parts], axis=0)


def _unpack_small(packed, like):
    out, off = [], 0
    flat = packed[:SMALL_ROWS].reshape(-1)
    for a in like:
        out.append(flat[off : off + a.size].reshape(a.shape))
        off += a.size
    return out


def kernel(x, norm1_g, w_in, w_decay_f, b_decay_f, w_decay_b, b_decay_b, gla_norm_g, gmlp_ln_g, gmlp_ln_b, w_spatial, b_spatial, w_out, norm2_g, w_gate, w_up, w_down, final_norm_g, loss_target, m_norm1_g, m_w_in, m_w_decay_f, m_b_decay_f, m_w_decay_b, m_b_decay_b, m_gla_norm_g, m_gmlp_ln_g, m_gmlp_ln_b, m_w_spatial, m_b_spatial, m_w_out, m_norm2_g, m_w_gate, m_w_up, m_w_down, m_final_norm_g, v_norm1_g, v_w_in, v_w_decay_f, v_b_decay_f, v_w_decay_b, v_b_decay_b, v_gla_norm_g, v_gmlp_ln_g, v_gmlp_ln_b, v_w_spatial, v_b_spatial, v_w_out, v_norm2_g, v_w_gate, v_w_up, v_w_down, v_final_norm_g):
    args = dict(locals())
    cx, cy, cc = lax.axis_index("x"), lax.axis_index("y"), lax.axis_index("c")
    shard = 2 * cx + cy
    xs = x[0]
    target = loss_target[0]

    big_names = ["w_in", "w_out", "w_gate", "w_up", "w_down"]
    transposed = ("w_in", "w_gate", "w_up")
    rows_of = lambda pre, k: jnp.transpose(args[pre + k][0]) if k in transposed else args[pre + k][0]
    big_shards = {k: rows_of("", k) for k in big_names}
    c_arr = cc.reshape(1).astype(jnp.int32)
    s_arr = shard.reshape(1).astype(jnp.int32)
    sc_arr = jnp.stack([shard, cc]).astype(jnp.int32)
    slots = {k: _cast_into_slot(big_shards[k], s_arr) for k in big_names}
    (w_in4,) = _gather_sync([slots["w_in"]])
    w_in_t = w_in4.reshape(PROJ_W, D_MODEL)

    dec_block = jnp.concatenate([w_decay_f[0].reshape(-1, LANES), w_decay_b[0].reshape(-1, LANES)], axis=0)
    dec_all = _allgather_small(dec_block)
    late = ["w_out", "w_gate", "w_up", "w_down"]
    g_send, g_recv, late_bufs, token_gather = _gather_start([slots[k] for k in late], (w_in4, dec_all))
    dec_all = dec_all[::2].reshape(N_SHARDS, 2, LOWRANK, KEY_W // N_SHARDS)
    wdf_full = jnp.transpose(dec_all[:, 0], (1, 0, 2)).reshape(LOWRANK, KEY_W)
    wdb_full = jnp.transpose(dec_all[:, 1], (1, 0, 2)).reshape(LOWRANK, KEY_W)
    wd_pad_f = jnp.zeros((LANES, KEY_W), F32).at[0:LOWRANK].set(wdf_full).astype(BF16)
    wd_pad_b = jnp.zeros((LANES, KEY_W), F32).at[LOWRANK : 2 * LOWRANK].set(wdb_full).astype(BF16)

    ws_bf = w_spatial[0].astype(BF16)
    wst_bf = jnp.transpose(w_spatial[0], (0, 2, 1)).astype(BF16)
    bs_col = b_spatial[0].reshape(GMLP_GROUPS, GMLP_CHUNK, 1)

    p = _inproj(xs, norm1_g, w_in_t, token_gather)
    o_f, st_f = _gla_fwd(p, wd_pad_f, b_decay_f, reverse=False)
    o_b, st_b = _gla_fwd(p, wd_pad_b, b_decay_b, reverse=True)
    late_bufs = _gather_forward(_gather_wait(g_send, g_recv, late_bufs, (o_f, o_b)))
    w_out_full, wg_t, wu_t, wd = [b.reshape(-1, D_MODEL) for b in late_bufs]
    x1, ycat = _mixer_out(xs, o_f, o_b, p, gla_norm_g, gmlp_ln_g, gmlp_ln_b, ws_bf, bs_col, w_out_full)
    gf = final_norm_g.reshape(1, D_MODEL)
    h2, gate, up, act, dx2, loss_acc, dgf = _ffn_fwd(x1, target, norm2_g, gf, wg_t, wu_t, wd)

    dgate, dup, dx1, dg2 = _ffn_bwd(dx2, gate, up, x1, norm2_g, wg_t, wu_t, wd)
    ffn_grads4 = [g.reshape(N_SHARDS, FF_SHARD, D_MODEL) for g in _ffn_wgrad(h2, dgate, dup, act, dx2)]
    e_send, e_recv, e_srcs, e_lands, token_exchange = _split_start(
        "exchange_start", ffn_grads4, _exchange_lands(ffn_grads4), _exchange_copies, len(ffn_grads4))
    do, dg, du, dvv, dwo, dgn, dlng, dlnb, dws, dbs = _mixer_bwd(
        dx1, ycat, o_f, o_b, p, gla_norm_g, gmlp_ln_g, gmlp_ln_b, ws_bf, wst_bf, bs_col, w_out_full, token_exchange)
    ffn_mine, ffn_other = _split_wait("exchange_wait", e_send, e_recv, e_srcs, e_lands, _exchange_copies, (do,))
    ffn_parts = _add_halves(ffn_mine, ffn_other, c_arr)
    ffn_payload = [pb for _, pb in ffn_parts]
    s_send, s_recv, s_parts, s_lands, token_scatter = _split_start(
        "scatter_start", ffn_payload, _scatter_lands(ffn_payload), _scatter_copies, 3 * len(ffn_payload))
    dq_f, dk_f, dv_f, dlr_f, dwdec_f, dbdec_f = _gla_bwd(p, do, st_f, wd_pad_f, b_decay_f, token_scatter, reverse=False)
    dq, dk, dv, dlr, dwdec_b, dbdec_b = _gla_bwd(
        p, do, st_b, wd_pad_b, b_decay_b, token_scatter, reverse=True, other=(dq_f, dk_f, dv_f, dlr_f))
    dwin_t, dp = _inproj_wgrad(xs, norm1_g, dq, dk, dv, dg, du, dvv, dlr)
    _, ffn_recv = _split_wait("scatter_wait", s_send, s_recv, s_parts, s_lands, _scatter_copies, (dwin_t,))

    dwin4 = dwin_t.reshape(N_SHARDS, PROJ_W // N_SHARDS, D_MODEL)
    dwo4 = dwo.reshape(N_SHARDS, D_MODEL // N_SHARDS, D_MODEL)
    proj_grads4 = [dwin4, dwo4]
    proj_parts = [_add_halves([g], [r], c_arr)[0] for g, r in zip(proj_grads4, _exchange_halves(proj_grads4))]
    proj_payload = [pb for _, pb in proj_parts]
    p_send, p_recv, p_parts, p_lands, token_proj = _split_start(
        "proj_scatter_start", proj_payload, _scatter_lands(proj_payload), _scatter_copies, 3 * len(proj_payload))
    dx, dg1 = _inproj_dx(xs, dx1, norm1_g, w_in_t, dp, token_proj)
    _, proj_recv = _split_wait("proj_scatter_wait", p_send, p_recv, p_parts, p_lands, _scatter_copies, (dx,))
    parts_f32 = [pf for pf, _ in proj_parts + ffn_parts]
    bufs = [_add_partials(pf, r, sc_arr) for pf, r in zip(parts_f32, proj_recv + ffn_recv)]
    big_grads = dict(zip(big_names, _join_halves(bufs)))

    dwdec_f16 = dwdec_f[0:LOWRANK]
    dwdec_b16 = dwdec_b[LOWRANK : 2 * LOWRANK]
    shard_major = lambda a: jnp.transpose(a.reshape(LOWRANK, N_SHARDS, KEY_W // N_SHARDS), (1, 0, 2))
    small_grads = {
        "norm1_g": dg1, "b_decay_f": dbdec_f, "b_decay_b": dbdec_b, "gla_norm_g": dgn, "gmlp_ln_g": dlng, "gmlp_ln_b": dlnb,
        "w_spatial": dws, "b_spatial": dbs, "norm2_g": dg2, "final_norm_g": dgf,
    }
    g_pack = _pack_small([small_grads[k] for k in SMALL_NAMES] + [loss_acc], [shard_major(dwdec_f16), shard_major(dwdec_b16)])
    g_all = _allgather_small(g_pack)
    pack_own = lambda pre: _pack_small([args[pre + k] for k in SMALL_NAMES], [args[pre + "w_decay_f"], args[pre + "w_decay_b"]])
    sg, sd, sm, sv = _adamw_small(g_all, pack_own(""), pack_own("m_"), pack_own("v_"))

    names = ["norm1_g", "w_in", "w_decay_f", "b_decay_f", "w_decay_b", "b_decay_b", "gla_norm_g", "gmlp_ln_g", "gmlp_ln_b",
             "w_spatial", "b_spatial", "w_out", "norm2_g", "w_gate", "w_up", "w_down", "final_norm_g"]
    like = [args[k] for k in SMALL_NAMES]
    results = {"g": {}, "d": {}, "m": {}, "v": {}}
    for tag, packed in (("g", sg), ("d", sd), ("m", sm), ("v", sv)):
        for k, a in zip(SMALL_NAMES, _unpack_small(packed, like)):
            results[tag][k] = a
        results[tag]["w_decay_f"] = packed[SMALL_ROWS : SMALL_ROWS + DECAY_ROWS].reshape(w_decay_f.shape)
        results[tag]["w_decay_b"] = packed[SMALL_ROWS + DECAY_ROWS :].reshape(w_decay_b.shape)
    for k in big_names:
        g, d, mo, vo = _adamw(big_shards[k], big_grads[k], rows_of("m_", k), rows_of("v_", k))
        for tag, a in (("g", g), ("d", d), ("m", mo), ("v", vo)):
            results[tag][k] = (jnp.transpose(a) if k in transposed else a).reshape(args[k].shape)

    loss = sg[:SMALL_ROWS].reshape(-1)[sum(a.size for a in like)]
    grad_x = dx.reshape(x.shape)
    return (loss, grad_x, *[results["g"][k] for k in names], *[results["d"][k] for k in names],
            *[results["m"][k] for k in names], *[results["v"][k] for k in names])
```

```python
import functools
import math

import jax
import jax.numpy as jnp
from jax import lax
from jax.experimental import pallas as pl
from jax.experimental.pallas import tpu as pltpu

F32, BF16 = jnp.float32, jnp.bfloat16

D_MODEL = 1024
GLA_HEADS = 4
GLA_DK = 64
GLA_DV = 128
KEY_W = GLA_HEADS * GLA_DK
GLA_W = GLA_HEADS * GLA_DV
GMLP_W = 512
GMLP_GROUPS = 4
GMLP_CHUNK = 128
LOWRANK = 16
GLA_CHUNK = 64
GLA_TAU = 16.0
PROJ_W = 2592
PROJ_WP = 2688
D_FF = 2816
N_SHARDS = 4
FF_SHARD = D_FF // N_SHARDS
EPS = 1e-6
LANES = 128
TOKEN_SHAPE = (8, LANES)
MIB = 1024 * 1024

ADAM_LR = 0.001
ADAM_B1 = 0.9
ADAM_B2 = 0.999
ADAM_EPS = 1e-08
ADAM_WD = 0.01
ADAM_STEP = 10

COL_Q, COL_K = 0, 256
COL_V, COL_G, COL_U, COL_VV = 512, 1024, 1536, 2048
COL_LR = 2560
ROW_LR, ROW_UV = 1536, 1568
HALF = D_MODEL // 2

MESH = pl.DeviceIdType.MESH


def _nn(a, b):
    return jnp.dot(a, b, preferred_element_type=F32)


def _nt(a, b):
    return lax.dot_general(a, b, (((1,), (1,)), ((), ())), preferred_element_type=F32)


def _tn(a, b):
    return lax.dot_general(a, b, (((0,), (0,)), ((), ())), preferred_element_type=F32)


def _bnn(a, b):
    return jnp.einsum("nik,nkj->nij", a, b, preferred_element_type=F32)


def _bnt(a, b):
    return jnp.einsum("nik,njk->nij", a, b, preferred_element_type=F32)


def _btn(a, b):
    return jnp.einsum("nki,nkj->nij", a, b, preferred_element_type=F32)


def _resident(shape):
    zeros = (0,) * len(shape)
    return pl.BlockSpec(shape, lambda *_: zeros, pipeline_mode=pl.Buffered(1))


def _params(vmem_mib, semantics=("arbitrary",)):
    return pltpu.CompilerParams(vmem_limit_bytes=vmem_mib * MIB, dimension_semantics=semantics)


def _sigmoid(x):
    return 1.0 / (1.0 + jnp.exp(-x))


def _gelu(x):
    return 0.5 * x * (1.0 + lax.erf(x * (1.0 / math.sqrt(2.0))))


def _gelu_and_grad(x):
    cdf = 0.5 * (1.0 + lax.erf(x * (1.0 / math.sqrt(2.0))))
    return x * cdf, cdf + x * jnp.exp(-0.5 * x * x) * (1.0 / math.sqrt(2.0 * math.pi))


def _log_sigmoid(x):
    return jnp.minimum(x, 0.0) - jnp.log(1.0 + jnp.exp(-jnp.abs(x)))


def _rms_bwd(dxh, xh, r):
    return r * (dxh - xh * jnp.mean(dxh * xh, axis=-1, keepdims=True))


def _chunk_cumsum(v, row_in_chunk, reverse):
    rows = v.shape[0]
    for sh in (1, 2, 4, 8, 16, 32):
        if reverse:
            v = v + jnp.where(row_in_chunk + sh < GLA_CHUNK, pltpu.roll(v, rows - sh, axis=0), 0.0)
        else:
            v = v + jnp.where(row_in_chunk >= sh, pltpu.roll(v, sh, axis=0), 0.0)
    return v


def _inproj(x, g1, w_in_t, token):
    seq = x.shape[0]
    tm = min(seq, 512)

    def body(x_ref, g_ref, w_ref, token_ref, p_ref):
        xv = x_ref[...]
        r = lax.rsqrt(jnp.mean(xv * xv, axis=-1, keepdims=True) + EPS)
        h = (xv * r * g_ref[...]).astype(BF16)
        p_ref[:, 0:COL_U] = _nt(h, w_ref[0:ROW_LR, :])
        p_ref[:, COL_U:COL_LR] = _nt(h, w_ref[ROW_UV:PROJ_W, :])
        p_ref[:, COL_LR:PROJ_WP] = _nt(h, w_ref[ROW_LR : ROW_LR + LANES, :])

    return pl.pallas_call(
        body,
        name="inproj",
        grid=(seq // tm,),
        in_specs=[pl.BlockSpec((tm, D_MODEL), lambda i: (i, 0)), _resident((1, D_MODEL)), _resident((PROJ_W, D_MODEL)), _resident(TOKEN_SHAPE)],
        out_specs=pl.BlockSpec((tm, PROJ_WP), lambda i: (i, 0)),
        out_shape=jax.ShapeDtypeStruct((seq, PROJ_WP), F32),
        compiler_params=_params(48, ("parallel",)),
    )(x, g1, w_in_t, token)


def _gla_tile(seq):
    return min(seq, 1024)


def _gla_decay_terms(lr_bf, wd_ref, bd_ref, pair, row_in_chunk, reverse, n):
    cols = pl.ds(pair * LANES, LANES)
    pre = _nn(lr_bf, wd_ref[:, cols]) + bd_ref[:, cols]
    la = _log_sigmoid(pre) * (1.0 / GLA_TAU)
    b = _chunk_cumsum(la, row_in_chunk, reverse)
    b3 = b.reshape(n, GLA_CHUNK, LANES)
    blast = b3[:, 0:1, :] if reverse else b3[:, GLA_CHUNK - 1 : GLA_CHUNK, :]
    return pre, b3, blast


def _gla_fwd(p, wd_pad, bd, reverse):
    seq = p.shape[0]
    tg = _gla_tile(seq)
    nt = seq // tg
    n = tg // GLA_CHUNK
    scale = GLA_DK**-0.5

    def tile(i):
        return nt - 1 - i if reverse else i

    def body(q_ref, k_ref, v_ref, lr_ref, wd_ref, bd_ref, o_ref, st_ref, carry):
        @pl.when(pl.program_id(0) == 0)
        def _():
            carry[...] = jnp.zeros_like(carry)

        lr_bf = lr_ref[...].astype(BF16)
        states = [carry[h] for h in range(GLA_HEADS)]
        row_in_chunk = lax.broadcasted_iota(jnp.int32, (tg, LANES), 0) % GLA_CHUNK
        lane_head = lax.broadcasted_iota(jnp.int32, (1, LANES), 1) // GLA_DK
        tt = lax.broadcasted_iota(jnp.int32, (GLA_CHUNK, GLA_CHUNK), 0)
        ss = lax.broadcasted_iota(jnp.int32, (GLA_CHUNK, GLA_CHUNK), 1)
        causal = (tt <= ss) if reverse else (tt >= ss)
        order = range(n - 1, -1, -1) if reverse else range(n)
        heads = range(GLA_HEADS)
        qds, vhs, decs, sc_raw, dst = {}, {}, {}, {}, {}
        for pair in range(2):
            cols = pl.ds(pair * LANES, LANES)
            _, b3, blast = _gla_decay_terms(lr_bf, wd_ref, bd_ref, pair, row_in_chunk, reverse, n)
            q3 = q_ref[:, cols].reshape(n, GLA_CHUNK, LANES) * scale
            k3 = k_ref[:, cols].reshape(n, GLA_CHUNK, LANES)
            qd = q3 * jnp.exp(b3)
            kd = (k3 * jnp.exp(-b3)).astype(BF16)
            kte = k3 * jnp.exp(blast - b3)
            decs[pair] = jnp.exp(blast)
            qds[pair] = qd.astype(BF16)
            m0 = (lane_head == 0).astype(F32)
            m1 = (lane_head == 1).astype(F32)
            q_both = jnp.concatenate([(qd * m0).astype(BF16), (qd * m1).astype(BF16)], axis=1)
            sc_both = _bnt(q_both, kd)
            for hh, m in ((0, m0), (1, m1)):
                h = 2 * pair + hh
                vhs[h] = v_ref[:, pl.ds(h * GLA_DV, GLA_DV)].reshape(n, GLA_CHUNK, GLA_DV).astype(BF16)
                sc_raw[h] = sc_both[:, hh * GLA_CHUNK : (hh + 1) * GLA_CHUNK, :]
                dst[h] = _btn(vhs[h], (kte * m).astype(BF16))
        o_intra, befores = {}, {}
        for h in heads:
            o_intra[h] = _bnn(jnp.where(causal, sc_raw[h], 0.0).astype(BF16), vhs[h])
            st, before = states[h], [None] * n
            for j in order:
                before[j] = st
                st = st * decs[h // 2][j] + dst[h][j]
            states[h] = st
            befores[h] = jnp.stack(before)
        outs = {}
        for pair in range(2):
            both = jnp.concatenate([befores[2 * pair], befores[2 * pair + 1]], axis=1).astype(BF16)
            o_inter = _bnt(qds[pair], both)
            for hh in range(2):
                h = 2 * pair + hh
                outs[h] = (o_intra[h] + o_inter[:, :, hh * GLA_DV : (hh + 1) * GLA_DV]).reshape(tg, GLA_DV)
        for h in range(GLA_HEADS):
            o_ref[:, pl.ds(h * GLA_DV, GLA_DV)] = outs[h]
            st_ref[:, h] = befores[h]
            carry[h] = states[h]

    nchunks = seq // GLA_CHUNK
    return pl.pallas_call(
        body,
        name="gla_fwd_rev" if reverse else "gla_fwd",
        grid=(nt,),
        in_specs=[
            pl.BlockSpec((tg, KEY_W), lambda i: (tile(i), COL_Q // KEY_W)),
            pl.BlockSpec((tg, KEY_W), lambda i: (tile(i), COL_K // KEY_W)),
            pl.BlockSpec((tg, GLA_W), lambda i: (tile(i), COL_V // GLA_W)),
            pl.BlockSpec((tg, LANES), lambda i: (tile(i), COL_LR // LANES)),
            _resident((LANES, KEY_W)),
            _resident((1, KEY_W)),
        ],
        out_specs=[
            pl.BlockSpec((tg, GLA_W), lambda i: (tile(i), 0)),
            pl.BlockSpec((n, GLA_HEADS, GLA_DV, LANES), lambda i: (tile(i), 0, 0, 0)),
        ],
        out_shape=[
            jax.ShapeDtypeStruct((seq, GLA_W), F32),
            jax.ShapeDtypeStruct((nchunks, GLA_HEADS, GLA_DV, LANES), F32),
        ],
        scratch_shapes=[pltpu.VMEM((GLA_HEADS, GLA_DV, LANES), F32)],
        compiler_params=_params(48),
    )(p, p, p, p, wd_pad, bd)


def _mixer_out(x, o_f, o_b, p, gn, lng, lnb, ws_bf, bs_col, w_out):
    seq = x.shape[0]
    tm = min(seq, 512)

    def body(x_ref, of_ref, ob_ref, g_ref, u_ref, vv_ref, gn_ref, lng_ref, lnb_ref, ws_ref, bs_ref, wo_ref, x1_ref, yc_ref, vn_sc):
        for h in range(GLA_HEADS):
            cols = pl.ds(h * GLA_DV, GLA_DV)
            oh = of_ref[:, cols] + ob_ref[:, cols]
            on = oh * lax.rsqrt(jnp.mean(oh * oh, axis=-1, keepdims=True) + EPS)
            gh = g_ref[:, cols]
            yc_ref[:, cols] = (on * gn_ref[:, cols] * (gh * _sigmoid(gh))).astype(BF16)
        zv = _gelu(vv_ref[...])
        xc = zv - jnp.mean(zv, axis=-1, keepdims=True)
        vhat = xc * lax.rsqrt(jnp.mean(xc * xc, axis=-1, keepdims=True) + EPS)
        vn_sc[...] = (vhat * lng_ref[...] + lnb_ref[...]).astype(BF16)
        for c in range(tm // GMLP_CHUNK):
            rows = pl.ds(c * GMLP_CHUNK, GMLP_CHUNK)
            for g in range(GMLP_GROUPS):
                cols = pl.ds(g * LANES, LANES)
                s = _nn(ws_ref[g], vn_sc[rows, cols]) + bs_ref[g]
                yc_ref[rows, pl.ds(GLA_W + g * LANES, LANES)] = (_gelu(u_ref[rows, cols]) * s).astype(BF16)
        x1_ref[...] = x_ref[...] + _nn(yc_ref[...], wo_ref[...])

    row = lambda w: pl.BlockSpec((tm, w), lambda i: (i, 0))
    pcol = lambda col: pl.BlockSpec((tm, GLA_W), lambda i: (i, col // GLA_W))
    return pl.pallas_call(
        body,
        name="mixer_out",
        grid=(seq // tm,),
        in_specs=[
            row(D_MODEL), row(GLA_W), row(GLA_W), pcol(COL_G), pcol(COL_U), pcol(COL_VV),
            _resident((1, GLA_W)), _resident((1, GMLP_W)), _resident((1, GMLP_W)),
            _resident((GMLP_GROUPS, GMLP_CHUNK, GMLP_CHUNK)), _resident((GMLP_GROUPS, GMLP_CHUNK, 1)),
            _resident((D_MODEL, D_MODEL)),
        ],
        out_specs=[row(D_MODEL), row(D_MODEL)],
        out_shape=[jax.ShapeDtypeStruct((seq, D_MODEL), F32), jax.ShapeDtypeStruct((seq, D_MODEL), BF16)],
        scratch_shapes=[pltpu.VMEM((tm, GMLP_W), BF16)],
        compiler_params=_params(48, ("parallel",)),
    )(x, o_f, o_b, p, p, p, gn, lng, lnb, ws_bf, bs_col, w_out)


def _ffn_fwd(x1, target, g2, gf, wg_t, wu_t, wd):
    seq = x1.shape[0]
    tm = min(seq, 256)

    def body(x1_ref, t_ref, g2_ref, gf_ref, wg_ref, wu_ref, wd_ref, h2_ref, gate_ref, up_ref, act_ref, dx2_ref, loss_ref, dgf_ref):
        @pl.when(pl.program_id(0) == 0)
        def _():
            loss_ref[...] = jnp.zeros_like(loss_ref)
            dgf_ref[...] = jnp.zeros_like(dgf_ref)

        x1v = x1_ref[...]
        h2 = (x1v * lax.rsqrt(jnp.mean(x1v * x1v, axis=-1, keepdims=True) + EPS) * g2_ref[...]).astype(BF16)
        h2_ref[...] = h2
        gate = _nt(h2, wg_ref[...])
        up = _nt(h2, wu_ref[...])
        act = (gate * _sigmoid(gate) * up).astype(BF16)
        gate_ref[...] = gate
        up_ref[...] = up
        act_ref[...] = act
        x2 = x1v + _nn(act, wd_ref[...])
        rf = lax.rsqrt(jnp.mean(x2 * x2, axis=-1, keepdims=True) + EPS)
        xh = x2 * rf
        err = xh * gf_ref[...] - t_ref[...]
        loss_ref[...] += 0.5 * jnp.sum(jnp.mean(err * err, axis=-1, keepdims=True))
        dy = err * (1.0 / D_MODEL)
        dgf_ref[...] += jnp.sum(dy * xh, axis=0, keepdims=True)
        dx2_ref[...] = _rms_bwd(dy * gf_ref[...], xh, rf)

    row = lambda w: pl.BlockSpec((tm, w), lambda i: (i, 0))
    weight = _resident((D_FF, D_MODEL))
    return pl.pallas_call(
        body,
        name="ffn_fwd",
        grid=(seq // tm,),
        in_specs=[row(D_MODEL), row(D_MODEL), _resident((1, D_MODEL)), _resident((1, D_MODEL)), weight, weight, weight],
        out_specs=[row(D_MODEL), row(D_FF), row(D_FF), row(D_FF), row(D_MODEL),
                   pl.BlockSpec((1, LANES), lambda i: (0, 0)), pl.BlockSpec((1, D_MODEL), lambda i: (0, 0))],
        out_shape=[
            jax.ShapeDtypeStruct((seq, D_MODEL), BF16),
            jax.ShapeDtypeStruct((seq, D_FF), F32),
            jax.ShapeDtypeStruct((seq, D_FF), F32),
            jax.ShapeDtypeStruct((seq, D_FF), BF16),
            jax.ShapeDtypeStruct((seq, D_MODEL), F32),
            jax.ShapeDtypeStruct((1, LANES), F32),
            jax.ShapeDtypeStruct((1, D_MODEL), F32),
        ],
        compiler_params=_params(56),
    )(x1, target, g2, gf, wg_t, wu_t, wd)


def _ffn_bwd(dx2, gate, up, x1, g2, wg_t, wu_t, wd):
    seq = x1.shape[0]
    tm = min(seq, 256)

    def body(dx2_ref, gate_ref, up_ref, x1_ref, g2_ref, wg_ref, wu_ref, wd_ref, dgate_ref, dup_ref, dx1_ref, dg2_ref):
        @pl.when(pl.program_id(0) == 0)
        def _():
            dg2_ref[...] = jnp.zeros_like(dg2_ref)

        dx2v = dx2_ref[...]
        dact = _nt(dx2v.astype(BF16), wd_ref[...])
        gate = gate_ref[...]
        sg = _sigmoid(gate)
        dgate = (dact * up_ref[...] * (sg * (1.0 + gate * (1.0 - sg)))).astype(BF16)
        dup = (dact * (gate * sg)).astype(BF16)
        dgate_ref[...] = dgate
        dup_ref[...] = dup
        dh2 = _nn(dgate, wg_ref[...]) + _nn(dup, wu_ref[...])
        x1v = x1_ref[...]
        r2 = lax.rsqrt(jnp.mean(x1v * x1v, axis=-1, keepdims=True) + EPS)
        xh = x1v * r2
        dg2_ref[...] += jnp.sum(dh2 * xh, axis=0, keepdims=True)
        dx1_ref[...] = dx2v + _rms_bwd(dh2 * g2_ref[...], xh, r2)

    row = lambda w: pl.BlockSpec((tm, w), lambda i: (i, 0))
    weight = _resident((D_FF, D_MODEL))
    return pl.pallas_call(
        body,
        name="ffn_bwd",
        grid=(seq // tm,),
        in_specs=[row(D_MODEL), row(D_FF), row(D_FF), row(D_MODEL), _resident((1, D_MODEL)), weight, weight, weight],
        out_specs=[row(D_FF), row(D_FF), row(D_MODEL), pl.BlockSpec((1, D_MODEL), lambda i: (0, 0))],
        out_shape=[
            jax.ShapeDtypeStruct((seq, D_FF), BF16),
            jax.ShapeDtypeStruct((seq, D_FF), BF16),
            jax.ShapeDtypeStruct((seq, D_MODEL), F32),
            jax.ShapeDtypeStruct((1, D_MODEL), F32),
        ],
        compiler_params=_params(56),
    )(dx2, gate, up, x1, g2, wg_t, wu_t, wd)


WGRAD_ROWS = D_FF // 2


def _ffn_wgrad(h2, dgate, dup, act, dx2):
    seq = h2.shape[0]
    tm = min(seq, 512)

    def body(h2_ref, dgate_ref, dup_ref, act_ref, dx2_ref, dwg_ref, dwu_ref, dwd_ref):
        @pl.when(pl.program_id(1) == 0)
        def _():
            dwg_ref[...] = jnp.zeros_like(dwg_ref)
            dwu_ref[...] = jnp.zeros_like(dwu_ref)
            dwd_ref[...] = jnp.zeros_like(dwd_ref)

        h2v = h2_ref[...]
        dwg_ref[...] += _tn(dgate_ref[...], h2v)
        dwu_ref[...] += _tn(dup_ref[...], h2v)
        dwd_ref[...] += _tn(act_ref[...], dx2_ref[...].astype(BF16))

    ff = pl.BlockSpec((tm, WGRAD_ROWS), lambda j, i: (i, j))
    row = pl.BlockSpec((tm, D_MODEL), lambda j, i: (i, 0))
    out = pl.BlockSpec((WGRAD_ROWS, D_MODEL), lambda j, i: (j, 0))
    return pl.pallas_call(
        body,
        name="ffn_wgrad",
        grid=(D_FF // WGRAD_ROWS, seq // tm),
        in_specs=[row, ff, ff, ff, row],
        out_specs=[out, out, out],
        out_shape=[jax.ShapeDtypeStruct((D_FF, D_MODEL), F32)] * 3,
        compiler_params=_params(56, ("parallel", "arbitrary")),
    )(h2, dgate, dup, act, dx2)


def _mixer_bwd(dx1, ycat, o_f, o_b, p, gn, lng, lnb, ws_bf, wst_bf, bs_col, w_out, token):
    seq = dx1.shape[0]
    tm = min(seq, 512)
    nsteps = seq // tm

    def body(dx1_ref, yc_ref, of_ref, ob_ref, g_ref, u_ref, vv_ref, gn_ref, lng_ref, lnb_ref, ws_ref, wst_ref, bs_ref, wo_ref, token_ref,
             do_ref, dg_ref, du_ref, dvv_ref, dwo_ref, dgn_ref, dlng_ref, dlnb_ref, dws_ref, dbs_ref, vn_sc, dvn_sc, dbs_acc):
        step = pl.program_id(0)

        @pl.when(step == 0)
        def _():
            for r in (dwo_ref, dgn_ref, dlng_ref, dlnb_ref, dws_ref, dbs_acc):
                r[...] = jnp.zeros_like(r)

        dx1b = dx1_ref[...].astype(BF16)
        dyc = _nt(dx1b, wo_ref[...])
        dwo_ref[...] += _tn(yc_ref[...], dx1b)
        for h in range(GLA_HEADS):
            cols = pl.ds(h * GLA_DV, GLA_DV)
            dya = dyc[:, h * GLA_DV : (h + 1) * GLA_DV]
            oh = of_ref[:, cols] + ob_ref[:, cols]
            rn = lax.rsqrt(jnp.mean(oh * oh, axis=-1, keepdims=True) + EPS)
            on = oh * rn
            gh = g_ref[:, cols]
            sg = _sigmoid(gh)
            sil = gh * sg
            gnh = gn_ref[:, cols]
            dgn_ref[:, cols] += jnp.sum(dya * on * sil, axis=0, keepdims=True)
            dg_ref[:, cols] = (dya * on * gnh * (sg * (1.0 + gh * (1.0 - sg)))).astype(BF16)
            do_ref[:, cols] = _rms_bwd(dya * gnh * sil, on, rn)
        vv = vv_ref[...]
        zv, zv_grad = _gelu_and_grad(vv)
        xc = zv - jnp.mean(zv, axis=-1, keepdims=True)
        rstd = lax.rsqrt(jnp.mean(xc * xc, axis=-1, keepdims=True) + EPS)
        vhat = xc * rstd
        vn_sc[...] = (vhat * lng_ref[...] + lnb_ref[...]).astype(BF16)
        for c in range(tm // GMLP_CHUNK):
            rows = pl.ds(c * GMLP_CHUNK, GMLP_CHUNK)
            for g in range(GMLP_GROUPS):
                cols = pl.ds(g * LANES, LANES)
                vn = vn_sc[rows, cols]
                s = _nn(ws_ref[g], vn) + bs_ref[g]
                dyb = dyc[c * GMLP_CHUNK : (c + 1) * GMLP_CHUNK, GLA_W + g * LANES : GLA_W + (g + 1) * LANES]
                zu, zu_grad = _gelu_and_grad(u_ref[rows, cols])
                du_ref[rows, cols] = (dyb * s * zu_grad).astype(BF16)
                ds = dyb * zu
                dbs_acc[g] += ds
                dsb = ds.astype(BF16)
                dws_ref[g] += _nt(dsb, vn)
                dvn_sc[rows, cols] = _nn(wst_ref[g], dsb)
        dvn = dvn_sc[...]
        dlng_ref[...] += jnp.sum(dvn * vhat, axis=0, keepdims=True)
        dlnb_ref[...] += jnp.sum(dvn, axis=0, keepdims=True)
        dvh = dvn * lng_ref[...]
        dzv = rstd * (dvh - jnp.mean(dvh, axis=-1, keepdims=True) - vhat * jnp.mean(dvh * vhat, axis=-1, keepdims=True))
        dvv_ref[...] = (dzv * zv_grad).astype(BF16)

        @pl.when(step == nsteps - 1)
        def _():
            dbs_ref[...] = jnp.sum(dbs_acc[...], axis=-1, keepdims=True)

    row = lambda w: pl.BlockSpec((tm, w), lambda i: (i, 0))
    pcol = lambda col: pl.BlockSpec((tm, GLA_W), lambda i: (i, col // GLA_W))
    const = lambda shape: pl.BlockSpec(shape, lambda i: (0,) * len(shape))
    return pl.pallas_call(
        body,
        name="mixer_bwd",
        grid=(nsteps,),
        in_specs=[
            row(D_MODEL), row(D_MODEL), row(GLA_W), row(GLA_W), pcol(COL_G), pcol(COL_U), pcol(COL_VV),
            _resident((1, GLA_W)), _resident((1, GMLP_W)), _resident((1, GMLP_W)),
            _resident((GMLP_GROUPS, GMLP_CHUNK, GMLP_CHUNK)), _resident((GMLP_GROUPS, GMLP_CHUNK, GMLP_CHUNK)),
            _resident((GMLP_GROUPS, GMLP_CHUNK, 1)), _resident((D_MODEL, D_MODEL)), _resident(TOKEN_SHAPE),
        ],
        out_specs=[
            row(GLA_W), row(GLA_W), row(GMLP_W), row(GMLP_W), const((D_MODEL, D_MODEL)),
            const((1, GLA_W)), const((1, GMLP_W)), const((1, GMLP_W)),
            const((GMLP_GROUPS, GMLP_CHUNK, GMLP_CHUNK)), const((GMLP_GROUPS, GMLP_CHUNK, 1)),
        ],
        out_shape=[
            jax.ShapeDtypeStruct((seq, GLA_W), F32), jax.ShapeDtypeStruct((seq, GLA_W), BF16),
            jax.ShapeDtypeStruct((seq, GMLP_W), BF16), jax.ShapeDtypeStruct((seq, GMLP_W), BF16),
            jax.ShapeDtypeStruct((D_MODEL, D_MODEL), F32),
            jax.ShapeDtypeStruct((1, GLA_W), F32), jax.ShapeDtypeStruct((1, GMLP_W), F32), jax.ShapeDtypeStruct((1, GMLP_W), F32),
            jax.ShapeDtypeStruct((GMLP_GROUPS, GMLP_CHUNK, GMLP_CHUNK), F32), jax.ShapeDtypeStruct((GMLP_GROUPS, GMLP_CHUNK, 1), F32),
        ],
        scratch_shapes=[pltpu.VMEM((tm, GMLP_W), BF16), pltpu.VMEM((tm, GMLP_W), F32), pltpu.VMEM((GMLP_GROUPS, GMLP_CHUNK, GMLP_CHUNK), F32)],
        compiler_params=_params(56),
    )(dx1, ycat, o_f, o_b, p, p, p, gn, lng, lnb, ws_bf, wst_bf, bs_col, w_out, token)


def _gla_bwd(p, do, st, wd_pad, bd, token, reverse, other=None):
    seq = p.shape[0]
    tg = _gla_tile(seq)
    nt = seq // tg
    n = tg // GLA_CHUNK
    scale = GLA_DK**-0.5

    def tile(i):
        return i if reverse else nt - 1 - i

    def body(q_ref, k_ref, v_ref, lr_ref, do_ref, st_ref, wd_ref, bd_ref, token_ref, *rest):
        others, (dq_ref, dk_ref, dv_ref, dlr_ref, dwd_ref, dbd_ref, carry) = rest[:-7], rest[-7:]
        if others:
            odq_ref, odk_ref, odv_ref, odlr_ref = others

            def put(ref, idx, val, oref):
                ref[idx] = (val + oref[idx]).astype(BF16)
        else:
            odq_ref = odk_ref = odv_ref = odlr_ref = None

            def put(ref, idx, val, oref):
                ref[idx] = val

        @pl.when(pl.program_id(0) == 0)
        def _():
            carry[...] = jnp.zeros_like(carry)
            dwd_ref[...] = jnp.zeros_like(dwd_ref)
            dbd_ref[...] = jnp.zeros_like(dbd_ref)

        lr_bf = lr_ref[...].astype(BF16)
        carries = [carry[h] for h in range(GLA_HEADS)]
        row_in_chunk = lax.broadcasted_iota(jnp.int32, (tg, LANES), 0) % GLA_CHUNK
        lane_head = lax.broadcasted_iota(jnp.int32, (1, LANES), 1) // GLA_DK
        tt = lax.broadcasted_iota(jnp.int32, (GLA_CHUNK, GLA_CHUNK), 0)
        ss = lax.broadcasted_iota(jnp.int32, (GLA_CHUNK, GLA_CHUNK), 1)
        causal = (tt <= ss) if reverse else (tt >= ss)
        order = range(n) if reverse else range(n - 1, -1, -1)
        dlr = jnp.zeros((tg, LANES), F32)
        heads = range(GLA_HEADS)
        pv, masks, qdh, vhs, dohs, stbs = {}, {}, {}, {}, {}, {}
        sc_raw, dp, acc = {}, {}, {}
        for pair in range(2):
            cols = pl.ds(pair * LANES, LANES)
            pre, b3, blast = _gla_decay_terms(lr_bf, wd_ref, bd_ref, pair, row_in_chunk, reverse, n)
            q3 = q_ref[:, cols].reshape(n, GLA_CHUNK, LANES) * scale
            k3 = k_ref[:, cols].reshape(n, GLA_CHUNK, LANES)
            eb = jnp.exp(b3)
            emb = jnp.exp(-b3)
            ekte = jnp.exp(blast - b3)
            kdf = k3 * emb
            kte = k3 * ekte
            both = pl.ds(2 * pair * GLA_DV, 2 * GLA_DV)
            pv[pair] = dict(pre=pre, eb=eb, emb=emb, ekte=ekte, qd=q3 * eb, kdf=kdf, kd=kdf.astype(BF16), kte=kte, kte_bf=kte.astype(BF16),
                            dec=jnp.exp(blast), v=v_ref[:, both].reshape(n, GLA_CHUNK, 2 * GLA_DV).astype(BF16),
                            do=do_ref[:, both].reshape(n, GLA_CHUNK, 2 * GLA_DV).astype(BF16))
            for hh in range(2):
                h = 2 * pair + hh
                masks[h] = (lane_head == hh).astype(F32)
                qdh[h] = (pv[pair]["qd"] * masks[h]).astype(BF16)
                vhs[h] = pv[pair]["v"][:, :, hh * GLA_DV : (hh + 1) * GLA_DV]
                dohs[h] = pv[pair]["do"][:, :, hh * GLA_DV : (hh + 1) * GLA_DV]
                stbs[h] = st_ref[:, h]
                dp[h] = _bnt(dohs[h], vhs[h])
                acc[h] = _btn(dohs[h], qdh[h])
            sc_both = _bnt(jnp.concatenate([qdh[2 * pair], qdh[2 * pair + 1]], axis=1), pv[pair]["kd"])
            for hh in range(2):
                sc_raw[2 * pair + hh] = sc_both[:, hh * GLA_CHUNK : (hh + 1) * GLA_CHUNK, :]
        dsa, sc = {}, {}
        for h in heads:
            sc[h] = jnp.where(causal, sc_raw[h], 0.0).astype(BF16)
            dp[h] = jnp.where(causal, dp[h], 0.0).astype(BF16)
            dec = pv[h // 2]["dec"]
            c, after = carries[h], [None] * n
            for j in order:
                after[j] = c
                c = acc[h][j] + dec[j] * c
            carries[h] = c
            dsa[h] = jnp.stack(after)
        dvs, dqs, dks, dwds, dbds = [], [], [], [], []
        for pair in range(2):
            cols = pl.ds(pair * LANES, LANES)
            v = pv[pair]
            h0, h1 = 2 * pair, 2 * pair + 1
            dsa_both = jnp.concatenate([dsa[h0], dsa[h1]], axis=1)
            dsa_bf = dsa_both.astype(BF16)
            stb_bf = jnp.concatenate([stbs[h0], stbs[h1]], axis=1).astype(BF16)
            dq_intra = _bnn(jnp.concatenate([dp[h0], dp[h1]], axis=1), v["kd"])
            dqd = (dq_intra[:, :GLA_CHUNK, :] * masks[h0] + dq_intra[:, GLA_CHUNK:, :] * masks[h1]) + _bnn(v["do"], stb_bf)
            dkd = _btn(dp[h0], qdh[h0]) + _btn(dp[h1], qdh[h1])
            dkte = _bnn(v["v"], dsa_bf)
            ddec = jnp.sum(dsa[h0] * stbs[h0] + dsa[h1] * stbs[h1], axis=1, keepdims=True)
            dv_inter = _bnt(v["kte_bf"], dsa_bf)
            for hh, h in ((0, h0), (1, h1)):
                dvs.append((_btn(sc[h], dohs[h]) + dv_inter[:, :, hh * GLA_DV : (hh + 1) * GLA_DV]).reshape(tg, GLA_DV))
            dqs.append((dqd * (scale * v["eb"])).reshape(tg, LANES))
            dks.append((dkd * v["emb"] + dkte * v["ekte"]).reshape(tg, LANES))
            db = dqd * v["qd"] - dkd * v["kdf"] - dkte * v["kte"]
            dblast = jnp.sum(dkte * v["kte"], axis=1, keepdims=True) + ddec * v["dec"]
            dla = _chunk_cumsum(db.reshape(tg, LANES), row_in_chunk, not reverse) + jnp.broadcast_to(dblast, (n, GLA_CHUNK, LANES)).reshape(tg, LANES)
            dpre = (dla * (1.0 / GLA_TAU) * _sigmoid(-v["pre"]))
            dpre_bf = dpre.astype(BF16)
            dlr = dlr + _nt(dpre_bf, wd_ref[:, cols])
            dwds.append(_tn(lr_bf, dpre_bf))
            dbds.append(jnp.sum(dpre, axis=0, keepdims=True))
        put(dlr_ref, (slice(None), slice(None)), dlr, odlr_ref)
        for pair in range(2):
            cols = pl.ds(pair * LANES, LANES)
            put(dq_ref, (slice(None), cols), dqs[pair], odq_ref)
            put(dk_ref, (slice(None), cols), dks[pair], odk_ref)
            dwd_ref[:, cols] += dwds[pair]
            dbd_ref[:, cols] += dbds[pair]
        for h in range(GLA_HEADS):
            put(dv_ref, (slice(None), pl.ds(h * GLA_DV, GLA_DV)), dvs[h], odv_ref)
            carry[h] = carries[h]

    pieces = [
        pl.BlockSpec((tg, KEY_W), lambda i: (tile(i), 0)),
        pl.BlockSpec((tg, KEY_W), lambda i: (tile(i), 0)),
        pl.BlockSpec((tg, GLA_W), lambda i: (tile(i), 0)),
        pl.BlockSpec((tg, LANES), lambda i: (tile(i), 0)),
    ]
    piece_dtype = BF16 if other else F32
    return pl.pallas_call(
        body,
        name="gla_bwd_rev" if reverse else "gla_bwd",
        grid=(nt,),
        in_specs=[
            pl.BlockSpec((tg, KEY_W), lambda i: (tile(i), COL_Q // KEY_W)),
            pl.BlockSpec((tg, KEY_W), lambda i: (tile(i), COL_K // KEY_W)),
            pl.BlockSpec((tg, GLA_W), lambda i: (tile(i), COL_V // GLA_W)),
            pl.BlockSpec((tg, LANES), lambda i: (tile(i), COL_LR // LANES)),
            pl.BlockSpec((tg, GLA_W), lambda i: (tile(i), 0)),
            pl.BlockSpec((n, GLA_HEADS, GLA_DV, LANES), lambda i: (tile(i), 0, 0, 0)),
            _resident((LANES, KEY_W)),
            _resident((1, KEY_W)),
            _resident(TOKEN_SHAPE),
        ] + (pieces if other else []),
        out_specs=pieces + [pl.BlockSpec((LANES, KEY_W), lambda i: (0, 0)), pl.BlockSpec((1, KEY_W), lambda i: (0, 0))],
        out_shape=[
            jax.ShapeDtypeStruct((seq, KEY_W), piece_dtype), jax.ShapeDtypeStruct((seq, KEY_W), piece_dtype),
            jax.ShapeDtypeStruct((seq, GLA_W), piece_dtype), jax.ShapeDtypeStruct((seq, LANES), piece_dtype),
            jax.ShapeDtypeStruct((LANES, KEY_W), F32), jax.ShapeDtypeStruct((1, KEY_W), F32),
        ],
        scratch_shapes=[pltpu.VMEM((GLA_HEADS, GLA_DV, LANES), F32)],
        compiler_params=_params(56),
    )(p, p, p, p, do, st, wd_pad, bd, token, *(other or ()))


def _inproj_wgrad(x, g1, dq, dk, dv, dg, du, dvv, dlr):
    seq = x.shape[0]
    tm = min(seq, 512)

    def body(x_ref, g1_ref, dq_ref, dk_ref, dv_ref, dg_ref, du_ref, dvv_ref, dlr_ref, dw_ref, dp_ref):
        @pl.when(pl.program_id(0) == 0)
        def _():
            dw_ref[...] = jnp.zeros_like(dw_ref)

        for col, ref in ((COL_Q, dq_ref), (COL_K, dk_ref), (COL_V, dv_ref), (COL_G, dg_ref), (COL_U, du_ref), (COL_VV, dvv_ref), (COL_LR, dlr_ref)):
            dp_ref[:, col : col + ref.shape[1]] = ref[...]
        xv = x_ref[...]
        h = (xv * lax.rsqrt(jnp.mean(xv * xv, axis=-1, keepdims=True) + EPS) * g1_ref[...]).astype(BF16)
        dw_ref[0:ROW_LR, :] += _tn(dp_ref[:, 0:COL_U], h)
        dw_ref[ROW_UV:PROJ_W, :] += _tn(dp_ref[:, COL_U:COL_LR], h)
        dw_ref[ROW_LR:ROW_UV, :] += _tn(dp_ref[:, COL_LR:PROJ_WP], h)[0 : ROW_UV - ROW_LR]

    row = lambda w: pl.BlockSpec((tm, w), lambda i: (i, 0))
    return pl.pallas_call(
        body,
        name="inproj_wgrad",
        grid=(seq // tm,),
        in_specs=[row(D_MODEL), _resident((1, D_MODEL)), row(KEY_W), row(KEY_W), row(GLA_W), row(GLA_W), row(GMLP_W), row(GMLP_W), row(LANES)],
        out_specs=[pl.BlockSpec((PROJ_W, D_MODEL), lambda i: (0, 0)), row(PROJ_WP)],
        out_shape=[jax.ShapeDtypeStruct((PROJ_W, D_MODEL), F32), jax.ShapeDtypeStruct((seq, PROJ_WP), BF16)],
        compiler_params=_params(56),
    )(x, g1, dq, dk, dv, dg, du, dvv, dlr)


def _inproj_dx(x, dx1, g1, w_in_t, dp, token):
    seq = x.shape[0]
    tm = min(seq, 512)

    def body(x_ref, dx1_ref, g1_ref, w_ref, dp_ref, token_ref, dx_ref, dg1_ref):
        @pl.when(pl.program_id(0) == 0)
        def _():
            dg1_ref[...] = jnp.zeros_like(dg1_ref)

        xv = x_ref[...]
        r1 = lax.rsqrt(jnp.mean(xv * xv, axis=-1, keepdims=True) + EPS)
        xh = xv * r1
        dh = (_nn(dp_ref[:, 0:COL_U], w_ref[0:ROW_LR, :]) + _nn(dp_ref[:, COL_U:COL_LR], w_ref[ROW_UV:PROJ_W, :])
              + _nn(dp_ref[:, COL_LR:PROJ_WP], w_ref[ROW_LR : ROW_LR + LANES, :]))
        dg1_ref[...] += jnp.sum(dh * xh, axis=0, keepdims=True)
        dx_ref[...] = dx1_ref[...] + _rms_bwd(dh * g1_ref[...], xh, r1)

    row = lambda w: pl.BlockSpec((tm, w), lambda i: (i, 0))
    return pl.pallas_call(
        body,
        name="inproj_dx",
        grid=(seq // tm,),
        in_specs=[row(D_MODEL), row(D_MODEL), _resident((1, D_MODEL)), _resident((PROJ_W, D_MODEL)), row(PROJ_WP), _resident(TOKEN_SHAPE)],
        out_specs=[row(D_MODEL), pl.BlockSpec((1, D_MODEL), lambda i: (0, 0))],
        out_shape=[jax.ShapeDtypeStruct((seq, D_MODEL), F32), jax.ShapeDtypeStruct((1, D_MODEL), F32)],
        compiler_params=_params(48),
    )(x, dx1, g1, w_in_t, dp, token)


def _in_hbm(a):
    return pltpu.with_memory_space_constraint(a, pltpu.HBM)


def _row_tile(rows, multiple=8):
    for t in range(min(rows, 512), 0, -1):
        if rows % t == 0 and t % multiple == 0:
            return t
    return rows


def _cast_into_slot(w, shard):
    rows, cols = w.shape
    tr = _row_tile(rows, 16)

    def body(s_ref, w_ref, o_ref):
        o_ref[...] = w_ref[...].astype(BF16)

    return pl.pallas_call(
        body,
        name="cast_into_slot",
        grid_spec=pltpu.PrefetchScalarGridSpec(
            num_scalar_prefetch=1,
            grid=(rows // tr,),
            in_specs=[pl.BlockSpec((tr, cols), lambda i, s_ref: (i, 0))],
            out_specs=pl.BlockSpec((None, tr, cols), lambda i, s_ref: (s_ref[0], i, 0)),
        ),
        out_shape=pltpu.HBM((N_SHARDS, rows, cols), BF16),
        compiler_params=_params(32, ("parallel",)),
    )(shard, _in_hbm(w))


def _add_halves(grads4, recvs, c):
    n = len(grads4)
    _, rows, _ = grads4[0].shape
    tr = _row_tile(rows, 16)

    def body(c_ref, *refs):
        for k in range(n):
            total = refs[k][...] + refs[n + k][...]
            refs[2 * n + k][...] = total
            refs[3 * n + k][...] = total.astype(BF16)

    out = pl.BlockSpec((None, tr, HALF), lambda s, i, c_ref: (s, i, 0))
    mine = pl.BlockSpec((None, tr, HALF), lambda s, i, c_ref: (s, i, c_ref[0]))
    outs = pl.pallas_call(
        body,
        name="add_halves",
        grid_spec=pltpu.PrefetchScalarGridSpec(
            num_scalar_prefetch=1,
            grid=(N_SHARDS, rows // tr),
            in_specs=[mine] * n + [out] * n,
            out_specs=[out] * (2 * n),
        ),
        out_shape=[pltpu.HBM((N_SHARDS, rows, HALF), F32)] * n + [pltpu.HBM((N_SHARDS, rows, HALF), BF16)] * n,
        compiler_params=_params(48, ("parallel", "parallel")),
    )(c, *[_in_hbm(a) for a in list(grads4) + list(recvs)])
    return list(zip(outs[:n], outs[n:]))


def _add_partials(part4, recv3, shard_core):
    _, rows, _ = part4.shape
    tr = _row_tile(rows, 16)

    def body(sc_ref, p_ref, r_ref, o_ref):
        o_ref[...] = ((p_ref[...] + r_ref[0].astype(F32)) + r_ref[1].astype(F32)) + r_ref[2].astype(F32)

    return pl.pallas_call(
        body,
        name="add_partials",
        grid_spec=pltpu.PrefetchScalarGridSpec(
            num_scalar_prefetch=1,
            grid=(rows // tr,),
            in_specs=[
                pl.BlockSpec((None, tr, HALF), lambda i, sc_ref: (sc_ref[0], i, 0)),
                pl.BlockSpec((3, tr, HALF), lambda i, sc_ref: (0, i, 0)),
            ],
            out_specs=pl.BlockSpec((tr, HALF), lambda i, sc_ref: (i, sc_ref[1])),
        ),
        out_shape=pltpu.HBM((rows, 2 * HALF), F32),
        compiler_params=_params(32, ("parallel",)),
    )(shard_core, _in_hbm(part4), _in_hbm(recv3))


def _adam_math(w, g, m, v):
    m = ADAM_B1 * m + (1.0 - ADAM_B1) * g
    v = ADAM_B2 * v + (1.0 - ADAM_B2) * (g * g)
    m_hat = m / (1.0 - ADAM_B1**ADAM_STEP)
    v_hat = v / (1.0 - ADAM_B2**ADAM_STEP)
    delta = -ADAM_LR * (m_hat / (jnp.sqrt(v_hat) + ADAM_EPS) + ADAM_WD * w)
    return delta, m, v


def _adamw(w, g, m, v):
    rows, cols = w.shape
    tr = _row_tile(rows)

    def body(w_ref, g_ref, m_ref, v_ref, go_ref, d_ref, mo_ref, vo_ref):
        gv = g_ref[...]
        go_ref[...] = gv
        d_ref[...], mo_ref[...], vo_ref[...] = _adam_math(w_ref[...], gv, m_ref[...], v_ref[...])

    spec = pl.BlockSpec((tr, cols), lambda i: (i, 0))
    return pl.pallas_call(
        body, name="adamw", grid=(rows // tr,), in_specs=[spec] * 4, out_specs=[spec] * 4, out_shape=[pltpu.HBM(w.shape, F32)] * 4,
        compiler_params=_params(32, ("parallel",)),
    )(_in_hbm(w), _in_hbm(g), _in_hbm(m), _in_hbm(v))


SMALL_ROWS = 560
DECAY_ROWS = 8
SMALL_TOTAL = SMALL_ROWS + 2 * N_SHARDS * DECAY_ROWS


def _adamw_small(gathered, wp, mp, vp):
    out_rows = SMALL_ROWS + 2 * DECAY_ROWS

    def body(ga_ref, w_ref, m_ref, v_ref, g_ref, d_ref, mo_ref, vo_ref):
        shard = 2 * lax.axis_index("x") + lax.axis_index("y")
        g_ref[pl.ds(0, SMALL_ROWS), :] = functools.reduce(lambda a, b: a + b, [ga_ref[d, pl.ds(0, SMALL_ROWS), :] for d in range(8)])
        for k in range(2):
            start = pl.multiple_of(SMALL_ROWS + k * N_SHARDS * DECAY_ROWS + shard * DECAY_ROWS, DECAY_ROWS)
            g_ref[pl.ds(SMALL_ROWS + k * DECAY_ROWS, DECAY_ROWS), :] = functools.reduce(
                lambda a, b: a + b, [ga_ref[d, pl.ds(start, DECAY_ROWS), :] for d in range(8)])
        d_ref[...], mo_ref[...], vo_ref[...] = _adam_math(w_ref[...], g_ref[...], m_ref[...], v_ref[...])

    shape = jax.ShapeDtypeStruct((out_rows, LANES), F32)
    return pl.pallas_call(body, name="adamw_small", out_shape=[shape] * 4, compiler_params=_params(32, None))(gathered, wp, mp, vp)


ANY = pl.BlockSpec(memory_space=pl.ANY)


def _position():
    return lax.axis_index("x"), lax.axis_index("y"), lax.axis_index("c")


def _other_chips(x, y):
    return [(1 - x, y), (x, 1 - y), (1 - x, 1 - y)]


HBM = pl.BlockSpec(memory_space=pltpu.HBM)
SEM = pl.BlockSpec(memory_space=pltpu.SEMAPHORE)
TOKEN = jax.ShapeDtypeStruct(TOKEN_SHAPE, F32)
DATAFLOW = pltpu.SideEffectType.DATAFLOW_SIDE_EFFECTING


def _half_block(ref4, slot, core):
    return ref4.at[slot, :, pl.ds(pl.multiple_of(core * HALF, HALF), HALF)]


def _gather_ici_copies(refs4, send_sems, recv_sems, stride):
    x, y, c = _position()
    pairs = []
    for k, ref4 in enumerate(refs4):
        mine = _half_block(ref4, 2 * x + y, c)
        for j, (px, py) in enumerate(_other_chips(x, y)):
            sems = dict(send_sem=send_sems.at[stride * k + j], recv_sem=recv_sems.at[stride * k + j], device_id=(px, py, c), device_id_type=MESH)
            pairs.append((functools.partial(pltpu.make_async_remote_copy, src_ref=mine, dst_ref=mine, **sems),
                          functools.partial(pltpu.make_async_remote_copy, src_ref=mine, dst_ref=_half_block(ref4, 2 * px + py, c), **sems)))
    return pairs


def _gather_d2d_copies(refs4, send_sems, recv_sems, stride, offset):
    x, y, c = _position()
    pairs = []
    for k, ref4 in enumerate(refs4):
        for j, (px, py) in enumerate(_other_chips(x, y)):
            have = _half_block(ref4, 2 * px + py, c)
            sems = dict(send_sem=send_sems.at[stride * k + offset + j], recv_sem=recv_sems.at[stride * k + offset + j],
                        device_id=(x, y, 1 - c), device_id_type=MESH)
            pairs.append((functools.partial(pltpu.make_async_remote_copy, src_ref=have, dst_ref=have, **sems),
                          functools.partial(pltpu.make_async_remote_copy, src_ref=have, dst_ref=_half_block(ref4, 2 * px + py, 1 - c), **sems)))
    return pairs


def _gather_sync(bufs):
    n = len(bufs)

    def body(*refs):
        outs = refs[n : 2 * n]
        send_sems, recv_sems = refs[2 * n :]
        ici = _gather_ici_copies(outs, send_sems, recv_sems, 6)
        d2d = _gather_d2d_copies(outs, send_sems, recv_sems, 6, 3)
        for send, _ in ici:
            send().start()
        for (_, arrival), (forward, _) in zip(ici, d2d):
            arrival().wait_recv()
            forward().start()
        for _, arrival in d2d:
            arrival().wait_recv()
        for send, _ in ici + d2d:
            send().wait_send()

    return pl.pallas_call(
        body,
        name="gather_sync",
        in_specs=[ANY] * n,
        out_specs=[ANY] * n,
        out_shape=[jax.ShapeDtypeStruct(b.shape, b.dtype) for b in bufs],
        input_output_aliases={k: k for k in range(n)},
        scratch_shapes=[pltpu.SemaphoreType.DMA((6 * n,)), pltpu.SemaphoreType.DMA((6 * n,))],
        compiler_params=pltpu.CompilerParams(has_side_effects=True),
    )(*bufs)


def _gather_start(bufs, after):
    n, na = len(bufs), len(after)

    def body(*refs):
        ins = refs[:n]
        send_sems, recv_sems = refs[n + na], refs[n + na + 1]
        token = refs[2 * n + na + 2]
        for send, _ in _gather_ici_copies(ins, send_sems, recv_sems, 3):
            send().start()
        token[...] = jnp.zeros_like(token)

    out = pl.pallas_call(
        body,
        name="gather_start",
        in_specs=[HBM] * n + [ANY] * na,
        out_specs=(SEM, SEM, *[HBM] * n, pl.BlockSpec(memory_space=pltpu.VMEM)),
        out_shape=(pltpu.SemaphoreType.DMA((3 * n,)), pltpu.SemaphoreType.DMA((3 * n,)), *[pltpu.HBM(b.shape, b.dtype) for b in bufs], TOKEN),
        input_output_aliases={k: 2 + k for k in range(n)},
        compiler_params=pltpu.CompilerParams(has_side_effects=DATAFLOW),
    )(*[pltpu.with_memory_space_constraint(b, pltpu.HBM) for b in bufs], *after)
    return out[0], out[1], list(out[2 : 2 + n]), out[2 + n]


def _gather_wait(send_sems, recv_sems, bufs, after):
    n = len(bufs)

    def body(*refs):
        ins = refs[:n]
        for send, arrival in _gather_ici_copies(ins, refs[n], refs[n + 1], 3):
            send().wait_send()
            arrival().wait_recv()

    return pl.pallas_call(
        body,
        name="gather_wait",
        in_specs=[HBM] * n + [SEM, SEM] + [ANY] * len(after),
        out_specs=tuple([HBM] * n),
        out_shape=tuple(pltpu.HBM(b.shape, b.dtype) for b in bufs),
        input_output_aliases={k: k for k in range(n)},
        compiler_params=pltpu.CompilerParams(has_side_effects=DATAFLOW),
    )(*bufs, send_sems, recv_sems, *after)


def _gather_forward(bufs):
    n = len(bufs)

    def body(*refs):
        outs = refs[n : 2 * n]
        send_sems, recv_sems = refs[2 * n :]
        d2d = _gather_d2d_copies(outs, send_sems, recv_sems, 3, 0)
        for forward, _ in d2d:
            forward().start()
        for forward, arrival in d2d:
            arrival().wait_recv()
            forward().wait_send()

    return pl.pallas_call(
        body,
        name="gather_forward",
        in_specs=[ANY] * n,
        out_specs=[ANY] * n,
        out_shape=[jax.ShapeDtypeStruct(b.shape, b.dtype) for b in bufs],
        input_output_aliases={k: k for k in range(n)},
        scratch_shapes=[pltpu.SemaphoreType.DMA((3 * n,)), pltpu.SemaphoreType.DMA((3 * n,))],
        compiler_params=pltpu.CompilerParams(has_side_effects=True),
    )(*bufs)


def _exchange_halves(grads4):
    n = len(grads4)

    def body(*refs):
        ins, outs = refs[:n], refs[n : 2 * n]
        send_sems, recv_sems = refs[2 * n :]
        x, y, c = _position()
        copies = []
        for k in range(n):
            cp = pltpu.make_async_remote_copy(
                src_ref=ins[k].at[:, :, pl.ds(pl.multiple_of((1 - c) * HALF, HALF), HALF)], dst_ref=outs[k],
                send_sem=send_sems.at[k], recv_sem=recv_sems.at[k], device_id=(x, y, 1 - c), device_id_type=MESH)
            cp.start()
            copies.append(cp)
        for cp in copies:
            cp.wait()

    return pl.pallas_call(
        body,
        name="exchange_halves",
        in_specs=[ANY] * n,
        out_specs=[ANY] * n,
        out_shape=[jax.ShapeDtypeStruct((N_SHARDS, g.shape[1], HALF), g.dtype) for g in grads4],
        scratch_shapes=[pltpu.SemaphoreType.DMA((n,)), pltpu.SemaphoreType.DMA((n,))],
        compiler_params=pltpu.CompilerParams(has_side_effects=True),
    )(*grads4)


def _scatter_copies(parts, lands, send_sems, recv_sems):
    x, y, c = _position()
    copies = []
    for k in range(len(parts)):
        for j, (px, py) in enumerate(_other_chips(x, y)):
            copies.append(pltpu.make_async_remote_copy(
                src_ref=parts[k].at[2 * px + py], dst_ref=lands[k].at[j],
                send_sem=send_sems.at[3 * k + j], recv_sem=recv_sems.at[3 * k + j], device_id=(px, py, c), device_id_type=MESH))
    return copies


def _exchange_copies(grads, lands, send_sems, recv_sems):
    x, y, c = _position()
    return [pltpu.make_async_remote_copy(
        src_ref=grads[k].at[:, :, pl.ds(pl.multiple_of((1 - c) * HALF, HALF), HALF)], dst_ref=lands[k],
        send_sem=send_sems.at[k], recv_sem=recv_sems.at[k], device_id=(x, y, 1 - c), device_id_type=MESH) for k in range(len(grads))]


def _exchange_lands(grads4):
    return [jax.ShapeDtypeStruct((N_SHARDS, g.shape[1], HALF), g.dtype) for g in grads4]


def _scatter_lands(parts4):
    return [jax.ShapeDtypeStruct((3,) + g.shape[1:], g.dtype) for g in parts4]


def _split_start(name, srcs, land_shapes, make_copies, nsem):
    n, nl = len(srcs), len(land_shapes)
    lands = [lax.empty(a.shape, a.dtype) for a in land_shapes]

    def body(*refs):
        send_sems, recv_sems = refs[n + nl], refs[n + nl + 1]
        token = refs[2 * (n + nl) + 2]
        for cp in make_copies(refs[:n], refs[n : n + nl], send_sems, recv_sems):
            cp.start()
        token[...] = jnp.zeros_like(token)

    hbm = lambda a: pltpu.HBM(a.shape, a.dtype)
    out = pl.pallas_call(
        body,
        name=name,
        in_specs=[HBM] * (n + nl),
        out_specs=(SEM, SEM, *[HBM] * (n + nl), pl.BlockSpec(memory_space=pltpu.VMEM)),
        out_shape=(pltpu.SemaphoreType.DMA((nsem,)), pltpu.SemaphoreType.DMA((nsem,)), *[hbm(a) for a in srcs + lands], TOKEN),
        input_output_aliases={k: 2 + k for k in range(n + nl)},
        compiler_params=pltpu.CompilerParams(has_side_effects=DATAFLOW),
    )(*[pltpu.with_memory_space_constraint(a, pltpu.HBM) for a in srcs + lands])
    return out[0], out[1], list(out[2 : 2 + n]), list(out[2 + n : 2 + n + nl]), out[2 + n + nl]


def _split_wait(name, send_sems, recv_sems, srcs, lands, make_copies, after):
    n, nl = len(srcs), len(lands)

    def body(*refs):
        for cp in make_copies(refs[:n], refs[n : n + nl], refs[n + nl], refs[n + nl + 1]):
            cp.wait_send()
            cp.wait_recv()

    hbm = lambda a: pltpu.HBM(a.shape, a.dtype)
    out = pl.pallas_call(
        body,
        name=name,
        in_specs=[HBM] * (n + nl) + [SEM, SEM] + [ANY] * len(after),
        out_specs=tuple([HBM] * (n + nl)),
        out_shape=tuple(hbm(a) for a in srcs + lands),
        input_output_aliases={k: k for k in range(n + nl)},
        compiler_params=pltpu.CompilerParams(has_side_effects=DATAFLOW),
    )(*srcs, *lands, send_sems, recv_sems, *after)
    return list(out[:n]), list(out[n:])


def _join_halves(bufs):
    n = len(bufs)

    def body(*refs):
        outs = refs[n : 2 * n]
        send_sems, recv_sems = refs[2 * n :]
        x, y, c = _position()
        half = lambda ref, core: ref.at[:, pl.ds(pl.multiple_of(core * HALF, HALF), HALF)]
        for k in range(n):
            mine = half(outs[k], c)
            pltpu.make_async_remote_copy(
                src_ref=mine, dst_ref=mine, send_sem=send_sems.at[k], recv_sem=recv_sems.at[k],
                device_id=(x, y, 1 - c), device_id_type=MESH).start()
        for k in range(n):
            wait = pltpu.make_async_remote_copy(
                src_ref=half(outs[k], c), dst_ref=half(outs[k], 1 - c), send_sem=send_sems.at[k], recv_sem=recv_sems.at[k],
                device_id=(x, y, 1 - c), device_id_type=MESH)
            wait.wait_send()
            wait.wait_recv()

    return pl.pallas_call(
        body,
        name="join_halves",
        in_specs=[ANY] * n,
        out_specs=[ANY] * n,
        out_shape=[jax.ShapeDtypeStruct(b.shape, b.dtype) for b in bufs],
        input_output_aliases={k: k for k in range(n)},
        scratch_shapes=[pltpu.SemaphoreType.DMA((n,)), pltpu.SemaphoreType.DMA((n,))],
        compiler_params=pltpu.CompilerParams(has_side_effects=True),
    )(*bufs)


def _allgather_small(block):
    m_per, ncol = block.shape

    def body(x_ref, out_ref, send_sems, recv_sems, local_sem):
        x, y, c = _position()
        me, sibling = (x, y, c), (x, y, 1 - c)
        chips = _other_chips(x, y)

        def rows(px, py, pc):
            return out_ref.at[4 * px + 2 * py + pc]

        def copy(k, blk, to, src=None):
            return pltpu.make_async_remote_copy(
                src_ref=rows(*blk) if src is None else src, dst_ref=rows(*blk),
                send_sem=send_sems.at[k], recv_sem=recv_sems.at[k], device_id=to, device_id_type=MESH)

        mine = pltpu.make_async_copy(x_ref, rows(*me), local_sem)
        mine.start()
        first = [copy(0, me, sibling, src=x_ref)] + [copy(1 + j, me, (*chip, c), src=x_ref) for j, chip in enumerate(chips)]
        for cp in first:
            cp.start()
        passed = [copy(4 + j, (*chip, c), sibling) for j, chip in enumerate(chips)]
        for j, chip in enumerate(chips):
            copy(1 + j, (*chip, c), me).wait_recv()
            passed[j].start()
        copy(0, sibling, me).wait_recv()
        for j, chip in enumerate(chips):
            copy(4 + j, (*chip, 1 - c), me).wait_recv()
        for cp in first + passed:
            cp.wait_send()
        mine.wait()

    return pl.pallas_call(
        body,
        name="allgather_small",
        in_specs=[pl.BlockSpec(memory_space=pltpu.VMEM)],
        out_specs=pl.BlockSpec(memory_space=pltpu.VMEM),
        out_shape=jax.ShapeDtypeStruct((8, m_per, ncol), block.dtype),
        scratch_shapes=[pltpu.SemaphoreType.DMA((7,)), pltpu.SemaphoreType.DMA((7,)), pltpu.SemaphoreType.DMA],
        compiler_params=pltpu.CompilerParams(has_side_effects=True, vmem_limit_bytes=32 * MIB),
    )(block)


SMALL_NAMES = ["norm1_g", "b_decay_f", "b_decay_b", "gla_norm_g", "gmlp_ln_g", "gmlp_ln_b", "w_spatial", "b_spatial", "norm2_g", "final_norm_g"]


def _pack_small(parts, decay_parts):
    flat = jnp.concatenate([a.reshape(-1) for a in parts])
    flat = jnp.pad(flat, (0, SMALL_ROWS * LANES - flat.shape[0])).reshape(SMALL_ROWS, LANES)
    return jnp.concatenate([flat] + [d.reshape(-1, LANES) for d in decay_parts], axis=0)


def _unpack_small(packed, like):
    out, off = [], 0
    flat = packed[:SMALL_ROWS].reshape(-1)
    for a in like:
        out.append(flat[off : off + a.size].reshape(a.shape))
        off += a.size
    return out


def kernel(x, norm1_g, w_in, w_decay_f, b_decay_f, w_decay_b, b_decay_b, gla_norm_g, gmlp_ln_g, gmlp_ln_b, w_spatial, b_spatial, w_out, norm2_g, w_gate, w_up, w_down, final_norm_g, loss_target, m_norm1_g, m_w_in, m_w_decay_f, m_b_decay_f, m_w_decay_b, m_b_decay_b, m_gla_norm_g, m_gmlp_ln_g, m_gmlp_ln_b, m_w_spatial, m_b_spatial, m_w_out, m_norm2_g, m_w_gate, m_w_up, m_w_down, m_final_norm_g, v_norm1_g, v_w_in, v_w_decay_f, v_b_decay_f, v_w_decay_b, v_b_decay_b, v_gla_norm_g, v_gmlp_ln_g, v_gmlp_ln_b, v_w_spatial, v_b_spatial, v_w_out, v_norm2_g, v_w_gate, v_w_up, v_w_down, v_final_norm_g):
    args = dict(locals())
    cx, cy, cc = lax.axis_index("x"), lax.axis_index("y"), lax.axis_index("c")
    shard = 2 * cx + cy
    xs = x[0]
    target = loss_target[0]

    big_names = ["w_in", "w_out", "w_gate", "w_up", "w_down"]
    transposed = ("w_in", "w_gate", "w_up")
    rows_of = lambda pre, k: jnp.transpose(args[pre + k][0]) if k in transposed else args[pre + k][0]
    big_shards = {k: rows_of("", k) for k in big_names}
    c_arr = cc.reshape(1).astype(jnp.int32)
    s_arr = shard.reshape(1).astype(jnp.int32)
    sc_arr = jnp.stack([shard, cc]).astype(jnp.int32)
    slots = {k: _cast_into_slot(big_shards[k], s_arr) for k in big_names}
    (w_in4,) = _gather_sync([slots["w_in"]])
    w_in_t = w_in4.reshape(PROJ_W, D_MODEL)

    dec_block = jnp.concatenate([w_decay_f[0].reshape(-1, LANES), w_decay_b[0].reshape(-1, LANES)], axis=0)
    dec_all = _allgather_small(dec_block)
    late = ["w_out", "w_gate", "w_up", "w_down"]
    g_send, g_recv, late_bufs, token_gather = _gather_start([slots[k] for k in late], (w_in4, dec_all))
    dec_all = dec_all[::2].reshape(N_SHARDS, 2, LOWRANK, KEY_W // N_SHARDS)
    wdf_full = jnp.transpose(dec_all[:, 0], (1, 0, 2)).reshape(LOWRANK, KEY_W)
    wdb_full = jnp.transpose(dec_all[:, 1], (1, 0, 2)).reshape(LOWRANK, KEY_W)
    wd_pad_f = jnp.zeros((LANES, KEY_W), F32).at[0:LOWRANK].set(wdf_full).astype(BF16)
    wd_pad_b = jnp.zeros((LANES, KEY_W), F32).at[LOWRANK : 2 * LOWRANK].set(wdb_full).astype(BF16)

    ws_bf = w_spatial[0].astype(BF16)
    wst_bf = jnp.transpose(w_spatial[0], (0, 2, 1)).astype(BF16)
    bs_col = b_spatial[0].reshape(GMLP_GROUPS, GMLP_CHUNK, 1)

    p = _inproj(xs, norm1_g, w_in_t, token_gather)
    o_f, st_f = _gla_fwd(p, wd_pad_f, b_decay_f, reverse=False)
    o_b, st_b = _gla_fwd(p, wd_pad_b, b_decay_b, reverse=True)
    late_bufs = _gather_forward(_gather_wait(g_send, g_recv, late_bufs, (o_f, o_b)))
    w_out_full, wg_t, wu_t, wd = [b.reshape(-1, D_MODEL) for b in late_bufs]
    x1, ycat = _mixer_out(xs, o_f, o_b, p, gla_norm_g, gmlp_ln_g, gmlp_ln_b, ws_bf, bs_col, w_out_full)
    gf = final_norm_g.reshape(1, D_MODEL)
    h2, gate, up, act, dx2, loss_acc, dgf = _ffn_fwd(x1, target, norm2_g, gf, wg_t, wu_t, wd)

    dgate, dup, dx1, dg2 = _ffn_bwd(dx2, gate, up, x1, norm2_g, wg_t, wu_t, wd)
    ffn_grads4 = [g.reshape(N_SHARDS, FF_SHARD, D_MODEL) for g in _ffn_wgrad(h2, dgate, dup, act, dx2)]
    e_send, e_recv, e_srcs, e_lands, token_exchange = _split_start(
        "exchange_start", ffn_grads4, _exchange_lands(ffn_grads4), _exchange_copies, len(ffn_grads4))
    do, dg, du, dvv, dwo, dgn, dlng, dlnb, dws, dbs = _mixer_bwd(
        dx1, ycat, o_f, o_b, p, gla_norm_g, gmlp_ln_g, gmlp_ln_b, ws_bf, wst_bf, bs_col, w_out_full, token_exchange)
    ffn_mine, ffn_other = _split_wait("exchange_wait", e_send, e_recv, e_srcs, e_lands, _exchange_copies, (do,))
    ffn_parts = _add_halves(ffn_mine, ffn_other, c_arr)
    ffn_payload = [pb for _, pb in ffn_parts]
    s_send, s_recv, s_parts, s_lands, token_scatter = _split_start(
        "scatter_start", ffn_payload, _scatter_lands(ffn_payload), _scatter_copies, 3 * len(ffn_payload))
    dq_f, dk_f, dv_f, dlr_f, dwdec_f, dbdec_f = _gla_bwd(p, do, st_f, wd_pad_f, b_decay_f, token_scatter, reverse=False)
    dq, dk, dv, dlr, dwdec_b, dbdec_b = _gla_bwd(
        p, do, st_b, wd_pad_b, b_decay_b, token_scatter, reverse=True, other=(dq_f, dk_f, dv_f, dlr_f))
    dwin_t, dp = _inproj_wgrad(xs, norm1_g, dq, dk, dv, dg, du, dvv, dlr)
    _, ffn_recv = _split_wait("scatter_wait", s_send, s_recv, s_parts, s_lands, _scatter_copies, (dwin_t,))

    dwin4 = dwin_t.reshape(N_SHARDS, PROJ_W // N_SHARDS, D_MODEL)
    dwo4 = dwo.reshape(N_SHARDS, D_MODEL // N_SHARDS, D_MODEL)
    proj_grads4 = [dwin4, dwo4]
    proj_parts = [_add_halves([g], [r], c_arr)[0] for g, r in zip(proj_grads4, _exchange_halves(proj_grads4))]
    proj_payload = [pb for _, pb in proj_parts]
    p_send, p_recv, p_parts, p_lands, token_proj = _split_start(
        "proj_scatter_start", proj_payload, _scatter_lands(proj_payload), _scatter_copies, 3 * len(proj_payload))
    dx, dg1 = _inproj_dx(xs, dx1, norm1_g, w_in_t, dp, token_proj)
    _, proj_recv = _split_wait("proj_scatter_wait", p_send, p_recv, p_parts, p_lands, _scatter_copies, (dx,))
    parts_f32 = [pf for pf, _ in proj_parts + ffn_parts]
    bufs = [_add_partials(pf, r, sc_arr) for pf, r in zip(parts_f32, proj_recv + ffn_recv)]
    big_grads = dict(zip(big_names, _join_halves(bufs)))

    dwdec_f16 = dwdec_f[0:LOWRANK]
    dwdec_b16 = dwdec_b[LOWRANK : 2 * LOWRANK]
    shard_major = lambda a: jnp.transpose(a.reshape(LOWRANK, N_SHARDS, KEY_W // N_SHARDS), (1, 0, 2))
    small_grads = {
        "norm1_g": dg1, "b_decay_f": dbdec_f, "b_decay_b": dbdec_b, "gla_norm_g": dgn, "gmlp_ln_g": dlng, "gmlp_ln_b": dlnb,
        "w_spatial": dws, "b_spatial": dbs, "norm2_g": dg2, "final_norm_g": dgf,
    }
    g_pack = _pack_small([small_grads[k] for k in SMALL_NAMES] + [loss_acc], [shard_major(dwdec_f16), shard_major(dwdec_b16)])
    g_all = _allgather_small(g_pack)
    pack_own = lambda pre: _pack_small([args[pre + k] for k in SMALL_NAMES], [args[pre + "w_decay_f"], args[pre + "w_decay_b"]])
    sg, sd, sm, sv = _adamw_small(g_all, pack_own(""), pack_own("m_"), pack_own("v_"))

    names = ["norm1_g", "w_in", "w_decay_f", "b_decay_f", "w_decay_b", "b_decay_b", "gla_norm_g", "gmlp_ln_g", "gmlp_ln_b",
             "w_spatial", "b_spatial", "w_out", "norm2_g", "w_gate", "w_up", "w_down", "final_norm_g"]
    like = [args[k] for k in SMALL_NAMES]
    results = {"g": {}, "d": {}, "m": {}, "v": {}}
    for tag, packed in (("g", sg), ("d", sd), ("m", sm), ("v", sv)):
        for k, a in zip(SMALL_NAMES, _unpack_small(packed, like)):
            results[tag][k] = a
        results[tag]["w_decay_f"] = packed[SMALL_ROWS : SMALL_ROWS + DECAY_ROWS].reshape(w_decay_f.shape)
        results[tag]["w_decay_b"] = packed[SMALL_ROWS + DECAY_ROWS :].reshape(w_decay_b.shape)
    for k in big_names:
        g, d, mo, vo = _adamw(big_shards[k], big_grads[k], rows_of("m_", k), rows_of("v_", k))
        for tag, a in (("g", g), ("d", d), ("m", mo), ("v", vo)):
            results[tag][k] = (jnp.transpose(a) if k in transposed else a).reshape(args[k].shape)

    loss = sg[:SMALL_ROWS].reshape(-1)[sum(a.size for a in like)]
    grad_x = dx.reshape(x.shape)
    return (loss, grad_x, *[results["g"][k] for k in names], *[results["d"][k] for k in names],
            *[results["m"][k] for k in names], *[results["v"][k] for k in names])
```

```python
import functools
import math

import jax
import jax.numpy as jnp
from jax import lax
from jax.experimental import pallas as pl
from jax.experimental.pallas import tpu as pltpu

F32, BF16 = jnp.float32, jnp.bfloat16

D_MODEL = 1024
GLA_HEADS = 4
GLA_DK = 64
GLA_DV = 128
KEY_W = GLA_HEADS * GLA_DK
GLA_W = GLA_HEADS * GLA_DV
GMLP_W = 512
GMLP_GROUPS = 4
GMLP_CHUNK = 128
LOWRANK = 16
GLA_CHUNK = 64
GLA_TAU = 16.0
PROJ_W = 2592
PROJ_WP = 2688
D_FF = 2816
N_SHARDS = 4
FF_SHARD = D_FF // N_SHARDS
EPS = 1e-6
LANES = 128
TOKEN_SHAPE = (8, LANES)
MIB = 1024 * 1024

ADAM_LR = 0.001
ADAM_B1 = 0.9
ADAM_B2 = 0.999
ADAM_EPS = 1e-08
ADAM_WD = 0.01
ADAM_STEP = 10

COL_Q, COL_K = 0, 256
COL_V, COL_G, COL_U, COL_VV = 512, 1024, 1536, 2048
COL_LR = 2560
ROW_LR, ROW_UV = 1536, 1568
HALF = D_MODEL // 2

MESH = pl.DeviceIdType.MESH


def _nn(a, b):
    return jnp.dot(a, b, preferred_element_type=F32)


def _nt(a, b):
    return lax.dot_general(a, b, (((1,), (1,)), ((), ())), preferred_element_type=F32)


def _tn(a, b):
    return lax.dot_general(a, b, (((0,), (0,)), ((), ())), preferred_element_type=F32)


def _bnn(a, b):
    return jnp.einsum("nik,nkj->nij", a, b, preferred_element_type=F32)


def _bnt(a, b):
    return jnp.einsum("nik,njk->nij", a, b, preferred_element_type=F32)


def _btn(a, b):
    return jnp.einsum("nki,nkj->nij", a, b, preferred_element_type=F32)


def _resident(shape):
    zeros = (0,) * len(shape)
    return pl.BlockSpec(shape, lambda *_: zeros, pipeline_mode=pl.Buffered(1))


def _params(vmem_mib, semantics=("arbitrary",)):
    return pltpu.CompilerParams(vmem_limit_bytes=vmem_mib * MIB, dimension_semantics=semantics)


def _sigmoid(x):
    return 1.0 / (1.0 + jnp.exp(-x))


def _gelu(x):
    return 0.5 * x * (1.0 + lax.erf(x * (1.0 / math.sqrt(2.0))))


def _gelu_and_grad(x):
    cdf = 0.5 * (1.0 + lax.erf(x * (1.0 / math.sqrt(2.0))))
    return x * cdf, cdf + x * jnp.exp(-0.5 * x * x) * (1.0 / math.sqrt(2.0 * math.pi))


def _log_sigmoid(x):
    return jnp.minimum(x, 0.0) - jnp.log(1.0 + jnp.exp(-jnp.abs(x)))


def _rms_bwd(dxh, xh, r):
    return r * (dxh - xh * jnp.mean(dxh * xh, axis=-1, keepdims=True))


def _chunk_cumsum(v, row_in_chunk, reverse):
    rows = v.shape[0]
    for sh in (1, 2, 4, 8, 16, 32):
        if reverse:
            v = v + jnp.where(row_in_chunk + sh < GLA_CHUNK, pltpu.roll(v, rows - sh, axis=0), 0.0)
        else:
            v = v + jnp.where(row_in_chunk >= sh, pltpu.roll(v, sh, axis=0), 0.0)
    return v


def _inproj(x, g1, w_in_t, token):
    seq = x.shape[0]
    tm = min(seq, 512)

    def body(x_ref, g_ref, w_ref, token_ref, p_ref):
        xv = x_ref[...]
        r = lax.rsqrt(jnp.mean(xv * xv, axis=-1, keepdims=True) + EPS)
        h = (xv * r * g_ref[...]).astype(BF16)
        p_ref[:, 0:COL_U] = _nt(h, w_ref[0:ROW_LR, :])
        p_ref[:, COL_U:COL_LR] = _nt(h, w_ref[ROW_UV:PROJ_W, :])
        p_ref[:, COL_LR:PROJ_WP] = _nt(h, w_ref[ROW_LR : ROW_LR + LANES, :])

    return pl.pallas_call(
        body,
        name="inproj",
        grid=(seq // tm,),
        in_specs=[pl.BlockSpec((tm, D_MODEL), lambda i: (i, 0)), _resident((1, D_MODEL)), _resident((PROJ_W, D_MODEL)), _resident(TOKEN_SHAPE)],
        out_specs=pl.BlockSpec((tm, PROJ_WP), lambda i: (i, 0)),
        out_shape=jax.ShapeDtypeStruct((seq, PROJ_WP), F32),
        compiler_params=_params(48, ("parallel",)),
    )(x, g1, w_in_t, token)


def _gla_tile(seq):
    return min(seq, 1024)


def _gla_decay_terms(lr_bf, wd_ref, bd_ref, pair, row_in_chunk, reverse, n):
    cols = pl.ds(pair * LANES, LANES)
    pre = _nn(lr_bf, wd_ref[:, cols]) + bd_ref[:, cols]
    la = _log_sigmoid(pre) * (1.0 / GLA_TAU)
    b = _chunk_cumsum(la, row_in_chunk, reverse)
    b3 = b.reshape(n, GLA_CHUNK, LANES)
    blast = b3[:, 0:1, :] if reverse else b3[:, GLA_CHUNK - 1 : GLA_CHUNK, :]
    return pre, b3, blast


def _gla_fwd(p, wd_pad, bd, reverse):
    seq = p.shape[0]
    tg = _gla_tile(seq)
    nt = seq // tg
    n = tg // GLA_CHUNK
    scale = GLA_DK**-0.5

    def tile(i):
        return nt - 1 - i if reverse else i

    def body(q_ref, k_ref, v_ref, lr_ref, wd_ref, bd_ref, o_ref, st_ref, carry):
        @pl.when(pl.program_id(0) == 0)
        def _():
            carry[...] = jnp.zeros_like(carry)

        lr_bf = lr_ref[...].astype(BF16)
        states = [carry[h] for h in range(GLA_HEADS)]
        row_in_chunk = lax.broadcasted_iota(jnp.int32, (tg, LANES), 0) % GLA_CHUNK
        lane_head = lax.broadcasted_iota(jnp.int32, (1, LANES), 1) // GLA_DK
        tt = lax.broadcasted_iota(jnp.int32, (GLA_CHUNK, GLA_CHUNK), 0)
        ss = lax.broadcasted_iota(jnp.int32, (GLA_CHUNK, GLA_CHUNK), 1)
        causal = (tt <= ss) if reverse else (tt >= ss)
        order = range(n - 1, -1, -1) if reverse else range(n)
        heads = range(GLA_HEADS)
        qds, vhs, decs, sc_raw, dst = {}, {}, {}, {}, {}
        for pair in range(2):
            cols = pl.ds(pair * LANES, LANES)
            _, b3, blast = _gla_decay_terms(lr_bf, wd_ref, bd_ref, pair, row_in_chunk, reverse, n)
            q3 = q_ref[:, cols].reshape(n, GLA_CHUNK, LANES) * scale
            k3 = k_ref[:, cols].reshape(n, GLA_CHUNK, LANES)
            qd = q3 * jnp.exp(b3)
            kd = (k3 * jnp.exp(-b3)).astype(BF16)
            kte = k3 * jnp.exp(blast - b3)
            decs[pair] = jnp.exp(blast)
            qds[pair] = qd.astype(BF16)
            m0 = (lane_head == 0).astype(F32)
            m1 = (lane_head == 1).astype(F32)
            q_both = jnp.concatenate([(qd * m0).astype(BF16), (qd * m1).astype(BF16)], axis=1)
            sc_both = _bnt(q_both, kd)
            for hh, m in ((0, m0), (1, m1)):
                h = 2 * pair + hh
                vhs[h] = v_ref[:, pl.ds(h * GLA_DV, GLA_DV)].reshape(n, GLA_CHUNK, GLA_DV).astype(BF16)
                sc_raw[h] = sc_both[:, hh * GLA_CHUNK : (hh + 1) * GLA_CHUNK, :]
                dst[h] = _btn(vhs[h], (kte * m).astype(BF16))
        o_intra, befores = {}, {}
        for h in heads:
            o_intra[h] = _bnn(jnp.where(causal, sc_raw[h], 0.0).astype(BF16), vhs[h])
            st, before = states[h], [None] * n
            for j in order:
                before[j] = st
                st = st * decs[h // 2][j] + dst[h][j]
            states[h] = st
            befores[h] = jnp.stack(before).astype(BF16)
        outs = {}
        for pair in range(2):
            both = jnp.concatenate([befores[2 * pair], befores[2 * pair + 1]], axis=1)
            o_inter = _bnt(qds[pair], both)
            for hh in range(2):
                h = 2 * pair + hh
                outs[h] = (o_intra[h] + o_inter[:, :, hh * GLA_DV : (hh + 1) * GLA_DV]).reshape(tg, GLA_DV)
        for h in range(GLA_HEADS):
            o_ref[:, pl.ds(h * GLA_DV, GLA_DV)] = outs[h]
            st_ref[:, h] = befores[h]
            carry[h] = states[h]

    nchunks = seq // GLA_CHUNK
    return pl.pallas_call(
        body,
        name="gla_fwd_rev" if reverse else "gla_fwd",
        grid=(nt,),
        in_specs=[
            pl.BlockSpec((tg, KEY_W), lambda i: (tile(i), COL_Q // KEY_W)),
            pl.BlockSpec((tg, KEY_W), lambda i: (tile(i), COL_K // KEY_W)),
            pl.BlockSpec((tg, GLA_W), lambda i: (tile(i), COL_V // GLA_W)),
            pl.BlockSpec((tg, LANES), lambda i: (tile(i), COL_LR // LANES)),
            _resident((LANES, KEY_W)),
            _resident((1, KEY_W)),
        ],
        out_specs=[
            pl.BlockSpec((tg, GLA_W), lambda i: (tile(i), 0)),
            pl.BlockSpec((n, GLA_HEADS, GLA_DV, LANES), lambda i: (tile(i), 0, 0, 0)),
        ],
        out_shape=[
            jax.ShapeDtypeStruct((seq, GLA_W), F32),
            jax.ShapeDtypeStruct((nchunks, GLA_HEADS, GLA_DV, LANES), BF16),
        ],
        scratch_shapes=[pltpu.VMEM((GLA_HEADS, GLA_DV, LANES), F32)],
        compiler_params=_params(48),
    )(p, p, p, p, wd_pad, bd)


def _mixer_out(x, o_f, o_b, p, gn, lng, lnb, ws_bf, bs_col, w_out):
    seq = x.shape[0]
    tm = min(seq, 512)

    def body(x_ref, of_ref, ob_ref, g_ref, u_ref, vv_ref, gn_ref, lng_ref, lnb_ref, ws_ref, bs_ref, wo_ref, x1_ref, yc_ref, vn_sc):
        for h in range(GLA_HEADS):
            cols = pl.ds(h * GLA_DV, GLA_DV)
            oh = of_ref[:, cols] + ob_ref[:, cols]
            on = oh * lax.rsqrt(jnp.mean(oh * oh, axis=-1, keepdims=True) + EPS)
            gh = g_ref[:, cols]
            yc_ref[:, cols] = (on * gn_ref[:, cols] * (gh * _sigmoid(gh))).astype(BF16)
        zv = _gelu(vv_ref[...])
        xc = zv - jnp.mean(zv, axis=-1, keepdims=True)
        vhat = xc * lax.rsqrt(jnp.mean(xc * xc, axis=-1, keepdims=True) + EPS)
        vn_sc[...] = (vhat * lng_ref[...] + lnb_ref[...]).astype(BF16)
        for c in range(tm // GMLP_CHUNK):
            rows = pl.ds(c * GMLP_CHUNK, GMLP_CHUNK)
            for g in range(GMLP_GROUPS):
                cols = pl.ds(g * LANES, LANES)
                s = _nn(ws_ref[g], vn_sc[rows, cols]) + bs_ref[g]
                yc_ref[rows, pl.ds(GLA_W + g * LANES, LANES)] = (_gelu(u_ref[rows, cols]) * s).astype(BF16)
        x1_ref[...] = x_ref[...] + _nn(yc_ref[...], wo_ref[...])

    row = lambda w: pl.BlockSpec((tm, w), lambda i: (i, 0))
    pcol = lambda col: pl.BlockSpec((tm, GLA_W), lambda i: (i, col // GLA_W))
    return pl.pallas_call(
        body,
        name="mixer_out",
        grid=(seq // tm,),
        in_specs=[
            row(D_MODEL), row(GLA_W), row(GLA_W), pcol(COL_G), pcol(COL_U), pcol(COL_VV),
            _resident((1, GLA_W)), _resident((1, GMLP_W)), _resident((1, GMLP_W)),
            _resident((GMLP_GROUPS, GMLP_CHUNK, GMLP_CHUNK)), _resident((GMLP_GROUPS, GMLP_CHUNK, 1)),
            _resident((D_MODEL, D_MODEL)),
        ],
        out_specs=[row(D_MODEL), row(D_MODEL)],
        out_shape=[jax.ShapeDtypeStruct((seq, D_MODEL), F32), jax.ShapeDtypeStruct((seq, D_MODEL), BF16)],
        scratch_shapes=[pltpu.VMEM((tm, GMLP_W), BF16)],
        compiler_params=_params(48, ("parallel",)),
    )(x, o_f, o_b, p, p, p, gn, lng, lnb, ws_bf, bs_col, w_out)


def _ffn_fwd(x1, target, g2, gf, wg_t, wu_t, wd):
    seq = x1.shape[0]
    tm = min(seq, 256)

    def body(x1_ref, t_ref, g2_ref, gf_ref, wg_ref, wu_ref, wd_ref, h2_ref, gate_ref, up_ref, act_ref, dx2_ref, loss_ref, dgf_ref):
        @pl.when(pl.program_id(0) == 0)
        def _():
            loss_ref[...] = jnp.zeros_like(loss_ref)
            dgf_ref[...] = jnp.zeros_like(dgf_ref)

        x1v = x1_ref[...]
        h2 = (x1v * lax.rsqrt(jnp.mean(x1v * x1v, axis=-1, keepdims=True) + EPS) * g2_ref[...]).astype(BF16)
        h2_ref[...] = h2
        gate = _nt(h2, wg_ref[...])
        up = _nt(h2, wu_ref[...])
        act = (gate * _sigmoid(gate) * up).astype(BF16)
        gate_ref[...] = gate
        up_ref[...] = up
        act_ref[...] = act
        x2 = x1v + _nn(act, wd_ref[...])
        rf = lax.rsqrt(jnp.mean(x2 * x2, axis=-1, keepdims=True) + EPS)
        xh = x2 * rf
        err = xh * gf_ref[...] - t_ref[...]
        loss_ref[...] += 0.5 * jnp.sum(jnp.mean(err * err, axis=-1, keepdims=True))
        dy = err * (1.0 / D_MODEL)
        dgf_ref[...] += jnp.sum(dy * xh, axis=0, keepdims=True)
        dx2_ref[...] = _rms_bwd(dy * gf_ref[...], xh, rf)

    row = lambda w: pl.BlockSpec((tm, w), lambda i: (i, 0))
    weight = _resident((D_FF, D_MODEL))
    return pl.pallas_call(
        body,
        name="ffn_fwd",
        grid=(seq // tm,),
        in_specs=[row(D_MODEL), row(D_MODEL), _resident((1, D_MODEL)), _resident((1, D_MODEL)), weight, weight, weight],
        out_specs=[row(D_MODEL), row(D_FF), row(D_FF), row(D_FF), row(D_MODEL),
                   pl.BlockSpec((1, LANES), lambda i: (0, 0)), pl.BlockSpec((1, D_MODEL), lambda i: (0, 0))],
        out_shape=[
            jax.ShapeDtypeStruct((seq, D_MODEL), BF16),
            jax.ShapeDtypeStruct((seq, D_FF), F32),
            jax.ShapeDtypeStruct((seq, D_FF), F32),
            jax.ShapeDtypeStruct((seq, D_FF), BF16),
            jax.ShapeDtypeStruct((seq, D_MODEL), F32),
            jax.ShapeDtypeStruct((1, LANES), F32),
            jax.ShapeDtypeStruct((1, D_MODEL), F32),
        ],
        compiler_params=_params(56),
    )(x1, target, g2, gf, wg_t, wu_t, wd)


def _ffn_bwd(dx2, gate, up, x1, g2, wg_t, wu_t, wd):
    seq = x1.shape[0]
    tm = min(seq, 256)

    def body(dx2_ref, gate_ref, up_ref, x1_ref, g2_ref, wg_ref, wu_ref, wd_ref, dgate_ref, dup_ref, dx1_ref, dg2_ref):
        @pl.when(pl.program_id(0) == 0)
        def _():
            dg2_ref[...] = jnp.zeros_like(dg2_ref)

        dx2v = dx2_ref[...]
        dact = _nt(dx2v.astype(BF16), wd_ref[...])
        gate = gate_ref[...]
        sg = _sigmoid(gate)
        dgate = (dact * up_ref[...] * (sg * (1.0 + gate * (1.0 - sg)))).astype(BF16)
        dup = (dact * (gate * sg)).astype(BF16)
        dgate_ref[...] = dgate
        dup_ref[...] = dup
        dh2 = _nn(dgate, wg_ref[...]) + _nn(dup, wu_ref[...])
        x1v = x1_ref[...]
        r2 = lax.rsqrt(jnp.mean(x1v * x1v, axis=-1, keepdims=True) + EPS)
        xh = x1v * r2
        dg2_ref[...] += jnp.sum(dh2 * xh, axis=0, keepdims=True)
        dx1_ref[...] = dx2v + _rms_bwd(dh2 * g2_ref[...], xh, r2)

    row = lambda w: pl.BlockSpec((tm, w), lambda i: (i, 0))
    weight = _resident((D_FF, D_MODEL))
    return pl.pallas_call(
        body,
        name="ffn_bwd",
        grid=(seq // tm,),
        in_specs=[row(D_MODEL), row(D_FF), row(D_FF), row(D_MODEL), _resident((1, D_MODEL)), weight, weight, weight],
        out_specs=[row(D_FF), row(D_FF), row(D_MODEL), pl.BlockSpec((1, D_MODEL), lambda i: (0, 0))],
        out_shape=[
            jax.ShapeDtypeStruct((seq, D_FF), BF16),
            jax.ShapeDtypeStruct((seq, D_FF), BF16),
            jax.ShapeDtypeStruct((seq, D_MODEL), F32),
            jax.ShapeDtypeStruct((1, D_MODEL), F32),
        ],
        compiler_params=_params(56),
    )(dx2, gate, up, x1, g2, wg_t, wu_t, wd)


WGRAD_ROWS = D_FF // 2


def _ffn_wgrad(h2, dgate, dup, act, dx2):
    seq = h2.shape[0]
    tm = min(seq, 512)

    def body(h2_ref, dgate_ref, dup_ref, act_ref, dx2_ref, dwg_ref, dwu_ref, dwd_ref):
        @pl.when(pl.program_id(1) == 0)
        def _():
            dwg_ref[...] = jnp.zeros_like(dwg_ref)
            dwu_ref[...] = jnp.zeros_like(dwu_ref)
            dwd_ref[...] = jnp.zeros_like(dwd_ref)

        h2v = h2_ref[...]
        dwg_ref[...] += _tn(dgate_ref[...], h2v)
        dwu_ref[...] += _tn(dup_ref[...], h2v)
        dwd_ref[...] += _tn(act_ref[...], dx2_ref[...].astype(BF16))

    ff = pl.BlockSpec((tm, WGRAD_ROWS), lambda j, i: (i, j))
    row = pl.BlockSpec((tm, D_MODEL), lambda j, i: (i, 0))
    out = pl.BlockSpec((WGRAD_ROWS, D_MODEL), lambda j, i: (j, 0))
    return pl.pallas_call(
        body,
        name="ffn_wgrad",
        grid=(D_FF // WGRAD_ROWS, seq // tm),
        in_specs=[row, ff, ff, ff, row],
        out_specs=[out, out, out],
        out_shape=[jax.ShapeDtypeStruct((D_FF, D_MODEL), F32)] * 3,
        compiler_params=_params(56, ("parallel", "arbitrary")),
    )(h2, dgate, dup, act, dx2)


def _mixer_bwd(dx1, ycat, o_f, o_b, p, gn, lng, lnb, ws_bf, wst_bf, bs_col, w_out, token):
    seq = dx1.shape[0]
    tm = min(seq, 512)
    nsteps = seq // tm

    def body(dx1_ref, yc_ref, of_ref, ob_ref, g_ref, u_ref, vv_ref, gn_ref, lng_ref, lnb_ref, ws_ref, wst_ref, bs_ref, wo_ref, token_ref,
             do_ref, dg_ref, du_ref, dvv_ref, dwo_ref, dgn_ref, dlng_ref, dlnb_ref, dws_ref, dbs_ref, vn_sc, dvn_sc, dbs_acc):
        step = pl.program_id(0)

        @pl.when(step == 0)
        def _():
            for r in (dwo_ref, dgn_ref, dlng_ref, dlnb_ref, dws_ref, dbs_acc):
                r[...] = jnp.zeros_like(r)

        dx1b = dx1_ref[...].astype(BF16)
        dyc = _nt(dx1b, wo_ref[...])
        dwo_ref[...] += _tn(yc_ref[...], dx1b)
        for h in range(GLA_HEADS):
            cols = pl.ds(h * GLA_DV, GLA_DV)
            dya = dyc[:, h * GLA_DV : (h + 1) * GLA_DV]
            oh = of_ref[:, cols] + ob_ref[:, cols]
            rn = lax.rsqrt(jnp.mean(oh * oh, axis=-1, keepdims=True) + EPS)
            on = oh * rn
            gh = g_ref[:, cols]
            sg = _sigmoid(gh)
            sil = gh * sg
            gnh = gn_ref[:, cols]
            dgn_ref[:, cols] += jnp.sum(dya * on * sil, axis=0, keepdims=True)
            dg_ref[:, cols] = (dya * on * gnh * (sg * (1.0 + gh * (1.0 - sg)))).astype(BF16)
            do_ref[:, cols] = _rms_bwd(dya * gnh * sil, on, rn)
        vv = vv_ref[...]
        zv, zv_grad = _gelu_and_grad(vv)
        xc = zv - jnp.mean(zv, axis=-1, keepdims=True)
        rstd = lax.rsqrt(jnp.mean(xc * xc, axis=-1, keepdims=True) + EPS)
        vhat = xc * rstd
        vn_sc[...] = (vhat * lng_ref[...] + lnb_ref[...]).astype(BF16)
        for c in range(tm // GMLP_CHUNK):
            rows = pl.ds(c * GMLP_CHUNK, GMLP_CHUNK)
            for g in range(GMLP_GROUPS):
                cols = pl.ds(g * LANES, LANES)
                vn = vn_sc[rows, cols]
                s = _nn(ws_ref[g], vn) + bs_ref[g]
                dyb = dyc[c * GMLP_CHUNK : (c + 1) * GMLP_CHUNK, GLA_W + g * LANES : GLA_W + (g + 1) * LANES]
                zu, zu_grad = _gelu_and_grad(u_ref[rows, cols])
                du_ref[rows, cols] = (dyb * s * zu_grad).astype(BF16)
                ds = dyb * zu
                dbs_acc[g] += ds
                dsb = ds.astype(BF16)
                dws_ref[g] += _nt(dsb, vn)
                dvn_sc[rows, cols] = _nn(wst_ref[g], dsb)
        dvn = dvn_sc[...]
        dlng_ref[...] += jnp.sum(dvn * vhat, axis=0, keepdims=True)
        dlnb_ref[...] += jnp.sum(dvn, axis=0, keepdims=True)
        dvh = dvn * lng_ref[...]
        dzv = rstd * (dvh - jnp.mean(dvh, axis=-1, keepdims=True) - vhat * jnp.mean(dvh * vhat, axis=-1, keepdims=True))
        dvv_ref[...] = (dzv * zv_grad).astype(BF16)

        @pl.when(step == nsteps - 1)
        def _():
            dbs_ref[...] = jnp.sum(dbs_acc[...], axis=-1, keepdims=True)

    row = lambda w: pl.BlockSpec((tm, w), lambda i: (i, 0))
    pcol = lambda col: pl.BlockSpec((tm, GLA_W), lambda i: (i, col // GLA_W))
    const = lambda shape: pl.BlockSpec(shape, lambda i: (0,) * len(shape))
    return pl.pallas_call(
        body,
        name="mixer_bwd",
        grid=(nsteps,),
        in_specs=[
            row(D_MODEL), row(D_MODEL), row(GLA_W), row(GLA_W), pcol(COL_G), pcol(COL_U), pcol(COL_VV),
            _resident((1, GLA_W)), _resident((1, GMLP_W)), _resident((1, GMLP_W)),
            _resident((GMLP_GROUPS, GMLP_CHUNK, GMLP_CHUNK)), _resident((GMLP_GROUPS, GMLP_CHUNK, GMLP_CHUNK)),
            _resident((GMLP_GROUPS, GMLP_CHUNK, 1)), _resident((D_MODEL, D_MODEL)), _resident(TOKEN_SHAPE),
        ],
        out_specs=[
            row(GLA_W), row(GLA_W), row(GMLP_W), row(GMLP_W), const((D_MODEL, D_MODEL)),
            const((1, GLA_W)), const((1, GMLP_W)), const((1, GMLP_W)),
            const((GMLP_GROUPS, GMLP_CHUNK, GMLP_CHUNK)), const((GMLP_GROUPS, GMLP_CHUNK, 1)),
        ],
        out_shape=[
            jax.ShapeDtypeStruct((seq, GLA_W), F32), jax.ShapeDtypeStruct((seq, GLA_W), BF16),
            jax.ShapeDtypeStruct((seq, GMLP_W), BF16), jax.ShapeDtypeStruct((seq, GMLP_W), BF16),
            jax.ShapeDtypeStruct((D_MODEL, D_MODEL), F32),
            jax.ShapeDtypeStruct((1, GLA_W), F32), jax.ShapeDtypeStruct((1, GMLP_W), F32), jax.ShapeDtypeStruct((1, GMLP_W), F32),
            jax.ShapeDtypeStruct((GMLP_GROUPS, GMLP_CHUNK, GMLP_CHUNK), F32), jax.ShapeDtypeStruct((GMLP_GROUPS, GMLP_CHUNK, 1), F32),
        ],
        scratch_shapes=[pltpu.VMEM((tm, GMLP_W), BF16), pltpu.VMEM((tm, GMLP_W), F32), pltpu.VMEM((GMLP_GROUPS, GMLP_CHUNK, GMLP_CHUNK), F32)],
        compiler_params=_params(56),
    )(dx1, ycat, o_f, o_b, p, p, p, gn, lng, lnb, ws_bf, wst_bf, bs_col, w_out, token)


def _gla_bwd(p, do, st, wd_pad, bd, token, reverse, other=None):
    seq = p.shape[0]
    tg = _gla_tile(seq)
    nt = seq // tg
    n = tg // GLA_CHUNK
    scale = GLA_DK**-0.5

    def tile(i):
        return i if reverse else nt - 1 - i

    def body(q_ref, k_ref, v_ref, lr_ref, do_ref, st_ref, wd_ref, bd_ref, token_ref, *rest):
        others, (dq_ref, dk_ref, dv_ref, dlr_ref, dwd_ref, dbd_ref, carry) = rest[:-7], rest[-7:]
        if others:
            odq_ref, odk_ref, odv_ref, odlr_ref = others

            def put(ref, idx, val, oref):
                ref[idx] = (val + oref[idx]).astype(BF16)
        else:
            odq_ref = odk_ref = odv_ref = odlr_ref = None

            def put(ref, idx, val, oref):
                ref[idx] = val

        @pl.when(pl.program_id(0) == 0)
        def _():
            carry[...] = jnp.zeros_like(carry)
            dwd_ref[...] = jnp.zeros_like(dwd_ref)
            dbd_ref[...] = jnp.zeros_like(dbd_ref)

        lr_bf = lr_ref[...].astype(BF16)
        carries = [carry[h] for h in range(GLA_HEADS)]
        row_in_chunk = lax.broadcasted_iota(jnp.int32, (tg, LANES), 0) % GLA_CHUNK
        lane_head = lax.broadcasted_iota(jnp.int32, (1, LANES), 1) // GLA_DK
        tt = lax.broadcasted_iota(jnp.int32, (GLA_CHUNK, GLA_CHUNK), 0)
        ss = lax.broadcasted_iota(jnp.int32, (GLA_CHUNK, GLA_CHUNK), 1)
        causal = (tt <= ss) if reverse else (tt >= ss)
        order = range(n) if reverse else range(n - 1, -1, -1)
        dlr = jnp.zeros((tg, LANES), F32)
        heads = range(GLA_HEADS)
        pv, masks, qdh, vhs, dohs, stbs = {}, {}, {}, {}, {}, {}
        sc_raw, dp, acc = {}, {}, {}
        for pair in range(2):
            cols = pl.ds(pair * LANES, LANES)
            pre, b3, blast = _gla_decay_terms(lr_bf, wd_ref, bd_ref, pair, row_in_chunk, reverse, n)
            q3 = q_ref[:, cols].reshape(n, GLA_CHUNK, LANES) * scale
            k3 = k_ref[:, cols].reshape(n, GLA_CHUNK, LANES)
            eb = jnp.exp(b3)
            emb = jnp.exp(-b3)
            ekte = jnp.exp(blast - b3)
            kdf = k3 * emb
            kte = k3 * ekte
            both = pl.ds(2 * pair * GLA_DV, 2 * GLA_DV)
            pv[pair] = dict(pre=pre, eb=eb, emb=emb, ekte=ekte, qd=q3 * eb, kdf=kdf, kd=kdf.astype(BF16), kte=kte, kte_bf=kte.astype(BF16),
                            dec=jnp.exp(blast), v=v_ref[:, both].reshape(n, GLA_CHUNK, 2 * GLA_DV).astype(BF16),
                            do=do_ref[:, both].reshape(n, GLA_CHUNK, 2 * GLA_DV).astype(BF16))
            for hh in range(2):
                h = 2 * pair + hh
                masks[h] = (lane_head == hh).astype(F32)
                qdh[h] = (pv[pair]["qd"] * masks[h]).astype(BF16)
                vhs[h] = pv[pair]["v"][:, :, hh * GLA_DV : (hh + 1) * GLA_DV]
                dohs[h] = pv[pair]["do"][:, :, hh * GLA_DV : (hh + 1) * GLA_DV]
                stbs[h] = st_ref[:, h]
                dp[h] = _bnt(dohs[h], vhs[h])
                acc[h] = _btn(dohs[h], qdh[h])
            sc_both = _bnt(jnp.concatenate([qdh[2 * pair], qdh[2 * pair + 1]], axis=1), pv[pair]["kd"])
            for hh in range(2):
                sc_raw[2 * pair + hh] = sc_both[:, hh * GLA_CHUNK : (hh + 1) * GLA_CHUNK, :]
        dsa, sc = {}, {}
        for h in heads:
            sc[h] = jnp.where(causal, sc_raw[h], 0.0).astype(BF16)
            dp[h] = jnp.where(causal, dp[h], 0.0).astype(BF16)
            dec = pv[h // 2]["dec"]
            c, after = carries[h], [None] * n
            for j in order:
                after[j] = c
                c = acc[h][j] + dec[j] * c
            carries[h] = c
            dsa[h] = jnp.stack(after)
        dvs, dqs, dks, dwds, dbds = [], [], [], [], []
        for pair in range(2):
            cols = pl.ds(pair * LANES, LANES)
            v = pv[pair]
            h0, h1 = 2 * pair, 2 * pair + 1
            dsa_both = jnp.concatenate([dsa[h0], dsa[h1]], axis=1)
            dsa_bf = dsa_both.astype(BF16)
            stb_bf = jnp.concatenate([stbs[h0], stbs[h1]], axis=1)
            dq_intra = _bnn(jnp.concatenate([dp[h0], dp[h1]], axis=1), v["kd"])
            dqd = (dq_intra[:, :GLA_CHUNK, :] * masks[h0] + dq_intra[:, GLA_CHUNK:, :] * masks[h1]) + _bnn(v["do"], stb_bf)
            dkd = _btn(dp[h0], qdh[h0]) + _btn(dp[h1], qdh[h1])
            dkte = _bnn(v["v"], dsa_bf)
            ddec = jnp.sum(dsa[h0] * stbs[h0].astype(F32) + dsa[h1] * stbs[h1].astype(F32), axis=1, keepdims=True)
            dv_inter = _bnt(v["kte_bf"], dsa_bf)
            for hh, h in ((0, h0), (1, h1)):
                dvs.append((_btn(sc[h], dohs[h]) + dv_inter[:, :, hh * GLA_DV : (hh + 1) * GLA_DV]).reshape(tg, GLA_DV))
            dqs.append((dqd * (scale * v["eb"])).reshape(tg, LANES))
            dks.append((dkd * v["emb"] + dkte * v["ekte"]).reshape(tg, LANES))
            db = dqd * v["qd"] - dkd * v["kdf"] - dkte * v["kte"]
            dblast = jnp.sum(dkte * v["kte"], axis=1, keepdims=True) + ddec * v["dec"]
            dla = _chunk_cumsum(db.reshape(tg, LANES), row_in_chunk, not reverse) + jnp.broadcast_to(dblast, (n, GLA_CHUNK, LANES)).reshape(tg, LANES)
            dpre = (dla * (1.0 / GLA_TAU) * _sigmoid(-v["pre"]))
            dpre_bf = dpre.astype(BF16)
            dlr = dlr + _nt(dpre_bf, wd_ref[:, cols])
            dwds.append(_tn(lr_bf, dpre_bf))
            dbds.append(jnp.sum(dpre, axis=0, keepdims=True))
        put(dlr_ref, (slice(None), slice(None)), dlr, odlr_ref)
        for pair in range(2):
            cols = pl.ds(pair * LANES, LANES)
            put(dq_ref, (slice(None), cols), dqs[pair], odq_ref)
            put(dk_ref, (slice(None), cols), dks[pair], odk_ref)
            dwd_ref[:, cols] += dwds[pair]
            dbd_ref[:, cols] += dbds[pair]
        for h in range(GLA_HEADS):
            put(dv_ref, (slice(None), pl.ds(h * GLA_DV, GLA_DV)), dvs[h], odv_ref)
            carry[h] = carries[h]

    pieces = [
        pl.BlockSpec((tg, KEY_W), lambda i: (tile(i), 0)),
        pl.BlockSpec((tg, KEY_W), lambda i: (tile(i), 0)),
        pl.BlockSpec((tg, GLA_W), lambda i: (tile(i), 0)),
        pl.BlockSpec((tg, LANES), lambda i: (tile(i), 0)),
    ]
    piece_dtype = BF16 if other else F32
    return pl.pallas_call(
        body,
        name="gla_bwd_rev" if reverse else "gla_bwd",
        grid=(nt,),
        in_specs=[
            pl.BlockSpec((tg, KEY_W), lambda i: (tile(i), COL_Q // KEY_W)),
            pl.BlockSpec((tg, KEY_W), lambda i: (tile(i), COL_K // KEY_W)),
            pl.BlockSpec((tg, GLA_W), lambda i: (tile(i), COL_V // GLA_W)),
            pl.BlockSpec((tg, LANES), lambda i: (tile(i), COL_LR // LANES)),
            pl.BlockSpec((tg, GLA_W), lambda i: (tile(i), 0)),
            pl.BlockSpec((n, GLA_HEADS, GLA_DV, LANES), lambda i: (tile(i), 0, 0, 0)),
            _resident((LANES, KEY_W)),
            _resident((1, KEY_W)),
            _resident(TOKEN_SHAPE),
        ] + (pieces if other else []),
        out_specs=pieces + [pl.BlockSpec((LANES, KEY_W), lambda i: (0, 0)), pl.BlockSpec((1, KEY_W), lambda i: (0, 0))],
        out_shape=[
            jax.ShapeDtypeStruct((seq, KEY_W), piece_dtype), jax.ShapeDtypeStruct((seq, KEY_W), piece_dtype),
            jax.ShapeDtypeStruct((seq, GLA_W), piece_dtype), jax.ShapeDtypeStruct((seq, LANES), piece_dtype),
            jax.ShapeDtypeStruct((LANES, KEY_W), F32), jax.ShapeDtypeStruct((1, KEY_W), F32),
        ],
        scratch_shapes=[pltpu.VMEM((GLA_HEADS, GLA_DV, LANES), F32)],
        compiler_params=_params(56),
    )(p, p, p, p, do, st, wd_pad, bd, token, *(other or ()))


def _inproj_wgrad(x, g1, dq, dk, dv, dg, du, dvv, dlr):
    seq = x.shape[0]
    tm = min(seq, 512)

    def body(x_ref, g1_ref, dq_ref, dk_ref, dv_ref, dg_ref, du_ref, dvv_ref, dlr_ref, dw_ref, dp_ref):
        @pl.when(pl.program_id(0) == 0)
        def _():
            dw_ref[...] = jnp.zeros_like(dw_ref)

        for col, ref in ((COL_Q, dq_ref), (COL_K, dk_ref), (COL_V, dv_ref), (COL_G, dg_ref), (COL_U, du_ref), (COL_VV, dvv_ref), (COL_LR, dlr_ref)):
            dp_ref[:, col : col + ref.shape[1]] = ref[...]
        xv = x_ref[...]
        h = (xv * lax.rsqrt(jnp.mean(xv * xv, axis=-1, keepdims=True) + EPS) * g1_ref[...]).astype(BF16)
        dw_ref[0:ROW_LR, :] += _tn(dp_ref[:, 0:COL_U], h)
        dw_ref[ROW_UV:PROJ_W, :] += _tn(dp_ref[:, COL_U:COL_LR], h)
        dw_ref[ROW_LR:ROW_UV, :] += _tn(dp_ref[:, COL_LR:PROJ_WP], h)[0 : ROW_UV - ROW_LR]

    row = lambda w: pl.BlockSpec((tm, w), lambda i: (i, 0))
    return pl.pallas_call(
        body,
        name="inproj_wgrad",
        grid=(seq // tm,),
        in_specs=[row(D_MODEL), _resident((1, D_MODEL)), row(KEY_W), row(KEY_W), row(GLA_W), row(GLA_W), row(GMLP_W), row(GMLP_W), row(LANES)],
        out_specs=[pl.BlockSpec((PROJ_W, D_MODEL), lambda i: (0, 0)), row(PROJ_WP)],
        out_shape=[jax.ShapeDtypeStruct((PROJ_W, D_MODEL), F32), jax.ShapeDtypeStruct((seq, PROJ_WP), BF16)],
        compiler_params=_params(56),
    )(x, g1, dq, dk, dv, dg, du, dvv, dlr)


def _inproj_dx(x, dx1, g1, w_in_t, dp, token):
    seq = x.shape[0]
    tm = min(seq, 512)

    def body(x_ref, dx1_ref, g1_ref, w_ref, dp_ref, token_ref, dx_ref, dg1_ref):
        @pl.when(pl.program_id(0) == 0)
        def _():
            dg1_ref[...] = jnp.zeros_like(dg1_ref)

        xv = x_ref[...]
        r1 = lax.rsqrt(jnp.mean(xv * xv, axis=-1, keepdims=True) + EPS)
        xh = xv * r1
        dh = (_nn(dp_ref[:, 0:COL_U], w_ref[0:ROW_LR, :]) + _nn(dp_ref[:, COL_U:COL_LR], w_ref[ROW_UV:PROJ_W, :])
              + _nn(dp_ref[:, COL_LR:PROJ_WP], w_ref[ROW_LR : ROW_LR + LANES, :]))
        dg1_ref[...] += jnp.sum(dh * xh, axis=0, keepdims=True)
        dx_ref[...] = dx1_ref[...] + _rms_bwd(dh * g1_ref[...], xh, r1)

    row = lambda w: pl.BlockSpec((tm, w), lambda i: (i, 0))
    return pl.pallas_call(
        body,
        name="inproj_dx",
        grid=(seq // tm,),
        in_specs=[row(D_MODEL), row(D_MODEL), _resident((1, D_MODEL)), _resident((PROJ_W, D_MODEL)), row(PROJ_WP), _resident(TOKEN_SHAPE)],
        out_specs=[row(D_MODEL), pl.BlockSpec((1, D_MODEL), lambda i: (0, 0))],
        out_shape=[jax.ShapeDtypeStruct((seq, D_MODEL), F32), jax.ShapeDtypeStruct((1, D_MODEL), F32)],
        compiler_params=_params(48),
    )(x, dx1, g1, w_in_t, dp, token)


def _in_hbm(a):
    return pltpu.with_memory_space_constraint(a, pltpu.HBM)


def _row_tile(rows, multiple=8):
    for t in range(min(rows, 512), 0, -1):
        if rows % t == 0 and t % multiple == 0:
            return t
    return rows


def _cast_into_slot(w, shard):
    rows, cols = w.shape
    tr = _row_tile(rows, 16)

    def body(s_ref, w_ref, o_ref):
        o_ref[...] = w_ref[...].astype(BF16)

    return pl.pallas_call(
        body,
        name="cast_into_slot",
        grid_spec=pltpu.PrefetchScalarGridSpec(
            num_scalar_prefetch=1,
            grid=(rows // tr,),
            in_specs=[pl.BlockSpec((tr, cols), lambda i, s_ref: (i, 0))],
            out_specs=pl.BlockSpec((None, tr, cols), lambda i, s_ref: (s_ref[0], i, 0)),
        ),
        out_shape=pltpu.HBM((N_SHARDS, rows, cols), BF16),
        compiler_params=_params(32, ("parallel",)),
    )(shard, _in_hbm(w))


def _add_halves(grads4, recvs, c):
    n = len(grads4)
    _, rows, _ = grads4[0].shape
    tr = _row_tile(rows, 16)

    def body(c_ref, *refs):
        for k in range(n):
            total = refs[k][...] + refs[n + k][...]
            refs[2 * n + k][...] = total
            refs[3 * n + k][...] = total.astype(BF16)

    out = pl.BlockSpec((None, tr, HALF), lambda s, i, c_ref: (s, i, 0))
    mine = pl.BlockSpec((None, tr, HALF), lambda s, i, c_ref: (s, i, c_ref[0]))
    outs = pl.pallas_call(
        body,
        name="add_halves",
        grid_spec=pltpu.PrefetchScalarGridSpec(
            num_scalar_prefetch=1,
            grid=(N_SHARDS, rows // tr),
            in_specs=[mine] * n + [out] * n,
            out_specs=[out] * (2 * n),
        ),
        out_shape=[pltpu.HBM((N_SHARDS, rows, HALF), F32)] * n + [pltpu.HBM((N_SHARDS, rows, HALF), BF16)] * n,
        compiler_params=_params(48, ("parallel", "parallel")),
    )(c, *[_in_hbm(a) for a in list(grads4) + list(recvs)])
    return list(zip(outs[:n], outs[n:]))


def _add_partials(part4, recv3, shard_core):
    _, rows, _ = part4.shape
    tr = _row_tile(rows, 16)

    def body(sc_ref, p_ref, r_ref, o_ref):
        o_ref[...] = ((p_ref[...] + r_ref[0].astype(F32)) + r_ref[1].astype(F32)) + r_ref[2].astype(F32)

    return pl.pallas_call(
        body,
        name="add_partials",
        grid_spec=pltpu.PrefetchScalarGridSpec(
            num_scalar_prefetch=1,
            grid=(rows // tr,),
            in_specs=[
                pl.BlockSpec((None, tr, HALF), lambda i, sc_ref: (sc_ref[0], i, 0)),
                pl.BlockSpec((3, tr, HALF), lambda i, sc_ref: (0, i, 0)),
            ],
            out_specs=pl.BlockSpec((tr, HALF), lambda i, sc_ref: (i, sc_ref[1])),
        ),
        out_shape=pltpu.HBM((rows, 2 * HALF), F32),
        compiler_params=_params(32, ("parallel",)),
    )(shard_core, _in_hbm(part4), _in_hbm(recv3))


def _adam_math(w, g, m, v):
    m = ADAM_B1 * m + (1.0 - ADAM_B1) * g
    v = ADAM_B2 * v + (1.0 - ADAM_B2) * (g * g)
    m_hat = m / (1.0 - ADAM_B1**ADAM_STEP)
    v_hat = v / (1.0 - ADAM_B2**ADAM_STEP)
    delta = -ADAM_LR * (m_hat / (jnp.sqrt(v_hat) + ADAM_EPS) + ADAM_WD * w)
    return delta, m, v


def _adamw(w, g, m, v):
    rows, cols = w.shape
    tr = _row_tile(rows)

    def body(w_ref, g_ref, m_ref, v_ref, go_ref, d_ref, mo_ref, vo_ref):
        gv = g_ref[...]
        go_ref[...] = gv
        d_ref[...], mo_ref[...], vo_ref[...] = _adam_math(w_ref[...], gv, m_ref[...], v_ref[...])

    spec = pl.BlockSpec((tr, cols), lambda i: (i, 0))
    return pl.pallas_call(
        body, name="adamw", grid=(rows // tr,), in_specs=[spec] * 4, out_specs=[spec] * 4, out_shape=[pltpu.HBM(w.shape, F32)] * 4,
        compiler_params=_params(32, ("parallel",)),
    )(_in_hbm(w), _in_hbm(g), _in_hbm(m), _in_hbm(v))


SMALL_ROWS = 560
DECAY_ROWS = 8
SMALL_TOTAL = SMALL_ROWS + 2 * N_SHARDS * DECAY_ROWS


def _adamw_small(gathered, wp, mp, vp):
    out_rows = SMALL_ROWS + 2 * DECAY_ROWS

    def body(ga_ref, w_ref, m_ref, v_ref, g_ref, d_ref, mo_ref, vo_ref):
        shard = 2 * lax.axis_index("x") + lax.axis_index("y")
        g_ref[pl.ds(0, SMALL_ROWS), :] = functools.reduce(lambda a, b: a + b, [ga_ref[d, pl.ds(0, SMALL_ROWS), :] for d in range(8)])
        for k in range(2):
            start = pl.multiple_of(SMALL_ROWS + k * N_SHARDS * DECAY_ROWS + shard * DECAY_ROWS, DECAY_ROWS)
            g_ref[pl.ds(SMALL_ROWS + k * DECAY_ROWS, DECAY_ROWS), :] = functools.reduce(
                lambda a, b: a + b, [ga_ref[d, pl.ds(start, DECAY_ROWS), :] for d in range(8)])
        d_ref[...], mo_ref[...], vo_ref[...] = _adam_math(w_ref[...], g_ref[...], m_ref[...], v_ref[...])

    shape = jax.ShapeDtypeStruct((out_rows, LANES), F32)
    return pl.pallas_call(body, name="adamw_small", out_shape=[shape] * 4, compiler_params=_params(32, None))(gathered, wp, mp, vp)


ANY = pl.BlockSpec(memory_space=pl.ANY)


def _position():
    return lax.axis_index("x"), lax.axis_index("y"), lax.axis_index("c")


def _other_chips(x, y):
    return [(1 - x, y), (x, 1 - y), (1 - x, 1 - y)]


HBM = pl.BlockSpec(memory_space=pltpu.HBM)
SEM = pl.BlockSpec(memory_space=pltpu.SEMAPHORE)
TOKEN = jax.ShapeDtypeStruct(TOKEN_SHAPE, F32)
DATAFLOW = pltpu.SideEffectType.DATAFLOW_SIDE_EFFECTING


def _half_block(ref4, slot, core):
    return ref4.at[slot, :, pl.ds(pl.multiple_of(core * HALF, HALF), HALF)]


def _gather_ici_copies(refs4, send_sems, recv_sems, stride):
    x, y, c = _position()
    pairs = []
    for k, ref4 in enumerate(refs4):
        mine = _half_block(ref4, 2 * x + y, c)
        for j, (px, py) in enumerate(_other_chips(x, y)):
            sems = dict(send_sem=send_sems.at[stride * k + j], recv_sem=recv_sems.at[stride * k + j], device_id=(px, py, c), device_id_type=MESH)
            pairs.append((functools.partial(pltpu.make_async_remote_copy, src_ref=mine, dst_ref=mine, **sems),
                          functools.partial(pltpu.make_async_remote_copy, src_ref=mine, dst_ref=_half_block(ref4, 2 * px + py, c), **sems)))
    return pairs


def _gather_d2d_copies(refs4, send_sems, recv_sems, stride, offset):
    x, y, c = _position()
    pairs = []
    for k, ref4 in enumerate(refs4):
        for j, (px, py) in enumerate(_other_chips(x, y)):
            have = _half_block(ref4, 2 * px + py, c)
            sems = dict(send_sem=send_sems.at[stride * k + offset + j], recv_sem=recv_sems.at[stride * k + offset + j],
                        device_id=(x, y, 1 - c), device_id_type=MESH)
            pairs.append((functools.partial(pltpu.make_async_remote_copy, src_ref=have, dst_ref=have, **sems),
                          functools.partial(pltpu.make_async_remote_copy, src_ref=have, dst_ref=_half_block(ref4, 2 * px + py, 1 - c), **sems)))
    return pairs


def _gather_sync(bufs):
    n = len(bufs)

    def body(*refs):
        outs = refs[n : 2 * n]
        send_sems, recv_sems = refs[2 * n :]
        ici = _gather_ici_copies(outs, send_sems, recv_sems, 6)
        d2d = _gather_d2d_copies(outs, send_sems, recv_sems, 6, 3)
        for send, _ in ici:
            send().start()
        for (_, arrival), (forward, _) in zip(ici, d2d):
            arrival().wait_recv()
            forward().start()
        for _, arrival in d2d:
            arrival().wait_recv()
        for send, _ in ici + d2d:
            send().wait_send()

    return pl.pallas_call(
        body,
        name="gather_sync",
        in_specs=[ANY] * n,
        out_specs=[ANY] * n,
        out_shape=[jax.ShapeDtypeStruct(b.shape, b.dtype) for b in bufs],
        input_output_aliases={k: k for k in range(n)},
        scratch_shapes=[pltpu.SemaphoreType.DMA((6 * n,)), pltpu.SemaphoreType.DMA((6 * n,))],
        compiler_params=pltpu.CompilerParams(has_side_effects=True),
    )(*bufs)


def _gather_start(bufs, after):
    n, na = len(bufs), len(after)

    def body(*refs):
        ins = refs[:n]
        send_sems, recv_sems = refs[n + na], refs[n + na + 1]
        token = refs[2 * n + na + 2]
        for send, _ in _gather_ici_copies(ins, send_sems, recv_sems, 3):
            send().start()
        token[...] = jnp.zeros_like(token)

    out = pl.pallas_call(
        body,
        name="gather_start",
        in_specs=[HBM] * n + [ANY] * na,
        out_specs=(SEM, SEM, *[HBM] * n, pl.BlockSpec(memory_space=pltpu.VMEM)),
        out_shape=(pltpu.SemaphoreType.DMA((3 * n,)), pltpu.SemaphoreType.DMA((3 * n,)), *[pltpu.HBM(b.shape, b.dtype) for b in bufs], TOKEN),
        input_output_aliases={k: 2 + k for k in range(n)},
        compiler_params=pltpu.CompilerParams(has_side_effects=DATAFLOW),
    )(*[pltpu.with_memory_space_constraint(b, pltpu.HBM) for b in bufs], *after)
    return out[0], out[1], list(out[2 : 2 + n]), out[2 + n]


def _gather_wait(send_sems, recv_sems, bufs, after):
    n = len(bufs)

    def body(*refs):
        ins = refs[:n]
        for send, arrival in _gather_ici_copies(ins, refs[n], refs[n + 1], 3):
            send().wait_send()
            arrival().wait_recv()

    return pl.pallas_call(
        body,
        name="gather_wait",
        in_specs=[HBM] * n + [SEM, SEM] + [ANY] * len(after),
        out_specs=tuple([HBM] * n),
        out_shape=tuple(pltpu.HBM(b.shape, b.dtype) for b in bufs),
        input_output_aliases={k: k for k in range(n)},
        compiler_params=pltpu.CompilerParams(has_side_effects=DATAFLOW),
    )(*bufs, send_sems, recv_sems, *after)


def _gather_forward(bufs):
    n = len(bufs)

    def body(*refs):
        outs = refs[n : 2 * n]
        send_sems, recv_sems = refs[2 * n :]
        d2d = _gather_d2d_copies(outs, send_sems, recv_sems, 3, 0)
        for forward, _ in d2d:
            forward().start()
        for forward, arrival in d2d:
            arrival().wait_recv()
            forward().wait_send()

    return pl.pallas_call(
        body,
        name="gather_forward",
        in_specs=[ANY] * n,
        out_specs=[ANY] * n,
        out_shape=[jax.ShapeDtypeStruct(b.shape, b.dtype) for b in bufs],
        input_output_aliases={k: k for k in range(n)},
        scratch_shapes=[pltpu.SemaphoreType.DMA((3 * n,)), pltpu.SemaphoreType.DMA((3 * n,))],
        compiler_params=pltpu.CompilerParams(has_side_effects=True),
    )(*bufs)


def _exchange_halves(grads4):
    n = len(grads4)

    def body(*refs):
        ins, outs = refs[:n], refs[n : 2 * n]
        send_sems, recv_sems = refs[2 * n :]
        x, y, c = _position()
        copies = []
        for k in range(n):
            cp = pltpu.make_async_remote_copy(
                src_ref=ins[k].at[:, :, pl.ds(pl.multiple_of((1 - c) * HALF, HALF), HALF)], dst_ref=outs[k],
                send_sem=send_sems.at[k], recv_sem=recv_sems.at[k], device_id=(x, y, 1 - c), device_id_type=MESH)
            cp.start()
            copies.append(cp)
        for cp in copies:
            cp.wait()

    return pl.pallas_call(
        body,
        name="exchange_halves",
        in_specs=[ANY] * n,
        out_specs=[ANY] * n,
        out_shape=[jax.ShapeDtypeStruct((N_SHARDS, g.shape[1], HALF), g.dtype) for g in grads4],
        scratch_shapes=[pltpu.SemaphoreType.DMA((n,)), pltpu.SemaphoreType.DMA((n,))],
        compiler_params=pltpu.CompilerParams(has_side_effects=True),
    )(*grads4)


def _scatter_copies(parts, lands, send_sems, recv_sems):
    x, y, c = _position()
    copies = []
    for k in range(len(parts)):
        for j, (px, py) in enumerate(_other_chips(x, y)):
            copies.append(pltpu.make_async_remote_copy(
                src_ref=parts[k].at[2 * px + py], dst_ref=lands[k].at[j],
                send_sem=send_sems.at[3 * k + j], recv_sem=recv_sems.at[3 * k + j], device_id=(px, py, c), device_id_type=MESH))
    return copies


def _exchange_copies(grads, lands, send_sems, recv_sems):
    x, y, c = _position()
    return [pltpu.make_async_remote_copy(
        src_ref=grads[k].at[:, :, pl.ds(pl.multiple_of((1 - c) * HALF, HALF), HALF)], dst_ref=lands[k],
        send_sem=send_sems.at[k], recv_sem=recv_sems.at[k], device_id=(x, y, 1 - c), device_id_type=MESH) for k in range(len(grads))]


def _exchange_lands(grads4):
    return [jax.ShapeDtypeStruct((N_SHARDS, g.shape[1], HALF), g.dtype) for g in grads4]


def _scatter_lands(parts4):
    return [jax.ShapeDtypeStruct((3,) + g.shape[1:], g.dtype) for g in parts4]


def _split_start(name, srcs, land_shapes, make_copies, nsem):
    n, nl = len(srcs), len(land_shapes)
    lands = [lax.empty(a.shape, a.dtype) for a in land_shapes]

    def body(*refs):
        send_sems, recv_sems = refs[n + nl], refs[n + nl + 1]
        token = refs[2 * (n + nl) + 2]
        for cp in make_copies(refs[:n], refs[n : n + nl], send_sems, recv_sems):
            cp.start()
        token[...] = jnp.zeros_like(token)

    hbm = lambda a: pltpu.HBM(a.shape, a.dtype)
    out = pl.pallas_call(
        body,
        name=name,
        in_specs=[HBM] * (n + nl),
        out_specs=(SEM, SEM, *[HBM] * (n + nl), pl.BlockSpec(memory_space=pltpu.VMEM)),
        out_shape=(pltpu.SemaphoreType.DMA((nsem,)), pltpu.SemaphoreType.DMA((nsem,)), *[hbm(a) for a in srcs + lands], TOKEN),
        input_output_aliases={k: 2 + k for k in range(n + nl)},
        compiler_params=pltpu.CompilerParams(has_side_effects=DATAFLOW),
    )(*[pltpu.with_memory_space_constraint(a, pltpu.HBM) for a in srcs + lands])
    return out[0], out[1], list(out[2 : 2 + n]), list(out[2 + n : 2 + n + nl]), out[2 + n + nl]


def _split_wait(name, send_sems, recv_sems, srcs, lands, make_copies, after):
    n, nl = len(srcs), len(lands)

    def body(*refs):
        for cp in make_copies(refs[:n], refs[n : n + nl], refs[n + nl], refs[n + nl + 1]):
            cp.wait_send()
            cp.wait_recv()

    hbm = lambda a: pltpu.HBM(a.shape, a.dtype)
    out = pl.pallas_call(
        body,
        name=name,
        in_specs=[HBM] * (n + nl) + [SEM, SEM] + [ANY] * len(after),
        out_specs=tuple([HBM] * (n + nl)),
        out_shape=tuple(hbm(a) for a in srcs + lands),
        input_output_aliases={k: k for k in range(n + nl)},
        compiler_params=pltpu.CompilerParams(has_side_effects=DATAFLOW),
    )(*srcs, *lands, send_sems, recv_sems, *after)
    return list(out[:n]), list(out[n:])


def _join_halves(bufs):
    n = len(bufs)

    def body(*refs):
        outs = refs[n : 2 * n]
        send_sems, recv_sems = refs[2 * n :]
        x, y, c = _position()
        half = lambda ref, core: ref.at[:, pl.ds(pl.multiple_of(core * HALF, HALF), HALF)]
        for k in range(n):
            mine = half(outs[k], c)
            pltpu.make_async_remote_copy(
                src_ref=mine, dst_ref=mine, send_sem=send_sems.at[k], recv_sem=recv_sems.at[k],
                device_id=(x, y, 1 - c), device_id_type=MESH).start()
        for k in range(n):
            wait = pltpu.make_async_remote_copy(
                src_ref=half(outs[k], c), dst_ref=half(outs[k], 1 - c), send_sem=send_sems.at[k], recv_sem=recv_sems.at[k],
                device_id=(x, y, 1 - c), device_id_type=MESH)
            wait.wait_send()
            wait.wait_recv()

    return pl.pallas_call(
        body,
        name="join_halves",
        in_specs=[ANY] * n,
        out_specs=[ANY] * n,
        out_shape=[jax.ShapeDtypeStruct(b.shape, b.dtype) for b in bufs],
        input_output_aliases={k: k for k in range(n)},
        scratch_shapes=[pltpu.SemaphoreType.DMA((n,)), pltpu.SemaphoreType.DMA((n,))],
        compiler_params=pltpu.CompilerParams(has_side_effects=True),
    )(*bufs)


def _allgather_small(block):
    m_per, ncol = block.shape

    def body(x_ref, out_ref, send_sems, recv_sems, local_sem):
        x, y, c = _position()
        me, sibling = (x, y, c), (x, y, 1 - c)
        chips = _other_chips(x, y)

        def rows(px, py, pc):
            return out_ref.at[4 * px + 2 * py + pc]

        def copy(k, blk, to, src=None):
            return pltpu.make_async_remote_copy(
                src_ref=rows(*blk) if src is None else src, dst_ref=rows(*blk),
                send_sem=send_sems.at[k], recv_sem=recv_sems.at[k], device_id=to, device_id_type=MESH)

        mine = pltpu.make_async_copy(x_ref, rows(*me), local_sem)
        mine.start()
        first = [copy(0, me, sibling, src=x_ref)] + [copy(1 + j, me, (*chip, c), src=x_ref) for j, chip in enumerate(chips)]
        for cp in first:
            cp.start()
        passed = [copy(4 + j, (*chip, c), sibling) for j, chip in enumerate(chips)]
        for j, chip in enumerate(chips):
            copy(1 + j, (*chip, c), me).wait_recv()
            passed[j].start()
        copy(0, sibling, me).wait_recv()
        for j, chip in enumerate(chips):
            copy(4 + j, (*chip, 1 - c), me).wait_recv()
        for cp in first + passed:
            cp.wait_send()
        mine.wait()

    return pl.pallas_call(
        body,
        name="allgather_small",
        in_specs=[pl.BlockSpec(memory_space=pltpu.VMEM)],
        out_specs=pl.BlockSpec(memory_space=pltpu.VMEM),
        out_shape=jax.ShapeDtypeStruct((8, m_per, ncol), block.dtype),
        scratch_shapes=[pltpu.SemaphoreType.DMA((7,)), pltpu.SemaphoreType.DMA((7,)), pltpu.SemaphoreType.DMA],
        compiler_params=pltpu.CompilerParams(has_side_effects=True, vmem_limit_bytes=32 * MIB),
    )(block)


SMALL_NAMES = ["norm1_g", "b_decay_f", "b_decay_b", "gla_norm_g", "gmlp_ln_g", "gmlp_ln_b", "w_spatial", "b_spatial", "norm2_g", "final_norm_g"]


def _pack_small(parts, decay_parts):
    flat = jnp.concatenate([a.reshape(-1) for a in parts])
    flat = jnp.pad(flat, (0, SMALL_ROWS * LANES - flat.shape[0])).reshape(SMALL_ROWS, LANES)
    return jnp.concatenate([flat] + [d.reshape(-1, LANES) for d in decay_parts], axis=0)


def _unpack_small(packed, like):
    out, off = [], 0
    flat = packed[:SMALL_ROWS].reshape(-1)
    for a in like:
        out.append(flat[off : off + a.size].reshape(a.shape))
        off += a.size
    return out


def kernel(x, norm1_g, w_in, w_decay_f, b_decay_f, w_decay_b, b_decay_b, gla_norm_g, gmlp_ln_g, gmlp_ln_b, w_spatial, b_spatial, w_out, norm2_g, w_gate, w_up, w_down, final_norm_g, loss_target, m_norm1_g, m_w_in, m_w_decay_f, m_b_decay_f, m_w_decay_b, m_b_decay_b, m_gla_norm_g, m_gmlp_ln_g, m_gmlp_ln_b, m_w_spatial, m_b_spatial, m_w_out, m_norm2_g, m_w_gate, m_w_up, m_w_down, m_final_norm_g, v_norm1_g, v_w_in, v_w_decay_f, v_b_decay_f, v_w_decay_b, v_b_decay_b, v_gla_norm_g, v_gmlp_ln_g, v_gmlp_ln_b, v_w_spatial, v_b_spatial, v_w_out, v_norm2_g, v_w_gate, v_w_up, v_w_down, v_final_norm_g):
    args = dict(locals())
    cx, cy, cc = lax.axis_index("x"), lax.axis_index("y"), lax.axis_index("c")
    shard = 2 * cx + cy
    xs = x[0]
    target = loss_target[0]

    big_names = ["w_in", "w_out", "w_gate", "w_up", "w_down"]
    transposed = ("w_in", "w_gate", "w_up")
    rows_of = lambda pre, k: jnp.transpose(args[pre + k][0]) if k in transposed else args[pre + k][0]
    big_shards = {k: rows_of("", k) for k in big_names}
    c_arr = cc.reshape(1).astype(jnp.int32)
    s_arr = shard.reshape(1).astype(jnp.int32)
    sc_arr = jnp.stack([shard, cc]).astype(jnp.int32)
    slots = {k: _cast_into_slot(big_shards[k], s_arr) for k in big_names}
    (w_in4,) = _gather_sync([slots["w_in"]])
    w_in_t = w_in4.reshape(PROJ_W, D_MODEL)

    dec_block = jnp.concatenate([w_decay_f[0].reshape(-1, LANES), w_decay_b[0].reshape(-1, LANES)], axis=0)
    dec_all = _allgather_small(dec_block)
    late = ["w_out", "w_gate", "w_up", "w_down"]
    g_send, g_recv, late_bufs, token_gather = _gather_start([slots[k] for k in late], (w_in4, dec_all))
    dec_all = dec_all[::2].reshape(N_SHARDS, 2, LOWRANK, KEY_W // N_SHARDS)
    wdf_full = jnp.transpose(dec_all[:, 0], (1, 0, 2)).reshape(LOWRANK, KEY_W)
    wdb_full = jnp.transpose(dec_all[:, 1], (1, 0, 2)).reshape(LOWRANK, KEY_W)
    wd_pad_f = jnp.zeros((LANES, KEY_W), F32).at[0:LOWRANK].set(wdf_full).astype(BF16)
    wd_pad_b = jnp.zeros((LANES, KEY_W), F32).at[LOWRANK : 2 * LOWRANK].set(wdb_full).astype(BF16)

    ws_bf = w_spatial[0].astype(BF16)
    wst_bf = jnp.transpose(w_spatial[0], (0, 2, 1)).astype(BF16)
    bs_col = b_spatial[0].reshape(GMLP_GROUPS, GMLP_CHUNK, 1)

    p = _inproj(xs, norm1_g, w_in_t, token_gather)
    o_f, st_f = _gla_fwd(p, wd_pad_f, b_decay_f, reverse=False)
    o_b, st_b = _gla_fwd(p, wd_pad_b, b_decay_b, reverse=True)
    late_bufs = _gather_forward(_gather_wait(g_send, g_recv, late_bufs, (o_f, o_b)))
    w_out_full, wg_t, wu_t, wd = [b.reshape(-1, D_MODEL) for b in late_bufs]
    x1, ycat = _mixer_out(xs, o_f, o_b, p, gla_norm_g, gmlp_ln_g, gmlp_ln_b, ws_bf, bs_col, w_out_full)
    gf = final_norm_g.reshape(1, D_MODEL)
    h2, gate, up, act, dx2, loss_acc, dgf = _ffn_fwd(x1, target, norm2_g, gf, wg_t, wu_t, wd)

    dgate, dup, dx1, dg2 = _ffn_bwd(dx2, gate, up, x1, norm2_g, wg_t, wu_t, wd)
    ffn_grads4 = [g.reshape(N_SHARDS, FF_SHARD, D_MODEL) for g in _ffn_wgrad(h2, dgate, dup, act, dx2)]
    e_send, e_recv, e_srcs, e_lands, token_exchange = _split_start(
        "exchange_start", ffn_grads4, _exchange_lands(ffn_grads4), _exchange_copies, len(ffn_grads4))
    do, dg, du, dvv, dwo, dgn, dlng, dlnb, dws, dbs = _mixer_bwd(
        dx1, ycat, o_f, o_b, p, gla_norm_g, gmlp_ln_g, gmlp_ln_b, ws_bf, wst_bf, bs_col, w_out_full, token_exchange)
    ffn_mine, ffn_other = _split_wait("exchange_wait", e_send, e_recv, e_srcs, e_lands, _exchange_copies, (do,))
    ffn_parts = _add_halves(ffn_mine, ffn_other, c_arr)
    ffn_payload = [pb for _, pb in ffn_parts]
    s_send, s_recv, s_parts, s_lands, token_scatter = _split_start(
        "scatter_start", ffn_payload, _scatter_lands(ffn_payload), _scatter_copies, 3 * len(ffn_payload))
    dq_f, dk_f, dv_f, dlr_f, dwdec_f, dbdec_f = _gla_bwd(p, do, st_f, wd_pad_f, b_decay_f, token_scatter, reverse=False)
    dq, dk, dv, dlr, dwdec_b, dbdec_b = _gla_bwd(
        p, do, st_b, wd_pad_b, b_decay_b, token_scatter, reverse=True, other=(dq_f, dk_f, dv_f, dlr_f))
    dwin_t, dp = _inproj_wgrad(xs, norm1_g, dq, dk, dv, dg, du, dvv, dlr)
    _, ffn_recv = _split_wait("scatter_wait", s_send, s_recv, s_parts, s_lands, _scatter_copies, (dwin_t,))

    dwin4 = dwin_t.reshape(N_SHARDS, PROJ_W // N_SHARDS, D_MODEL)
    dwo4 = dwo.reshape(N_SHARDS, D_MODEL // N_SHARDS, D_MODEL)
    proj_grads4 = [dwin4, dwo4]
    proj_parts = [_add_halves([g], [r], c_arr)[0] for g, r in zip(proj_grads4, _exchange_halves(proj_grads4))]
    proj_payload = [pb for _, pb in proj_parts]
    p_send, p_recv, p_parts, p_lands, token_proj = _split_start(
        "proj_scatter_start", proj_payload, _scatter_lands(proj_payload), _scatter_copies, 3 * len(proj_payload))
    dx, dg1 = _inproj_dx(xs, dx1, norm1_g, w_in_t, dp, token_proj)
    _, proj_recv = _split_wait("proj_scatter_wait", p_send, p_recv, p_parts, p_lands, _scatter_copies, (dx,))
    parts_f32 = [pf for pf, _ in proj_parts + ffn_parts]
    bufs = [_add_partials(pf, r, sc_arr) for pf, r in zip(parts_f32, proj_recv + ffn_recv)]
    big_grads = dict(zip(big_names, _join_halves(bufs)))

    dwdec_f16 = dwdec_f[0:LOWRANK]
    dwdec_b16 = dwdec_b[LOWRANK : 2 * LOWRANK]
    shard_major = lambda a: jnp.transpose(a.reshape(LOWRANK, N_SHARDS, KEY_W // N_SHARDS), (1, 0, 2))
    small_grads = {
        "norm1_g": dg1, "b_decay_f": dbdec_f, "b_decay_b": dbdec_b, "gla_norm_g": dgn, "gmlp_ln_g": dlng, "gmlp_ln_b": dlnb,
        "w_spatial": dws, "b_spatial": dbs, "norm2_g": dg2, "final_norm_g": dgf,
    }
    g_pack = _pack_small([small_grads[k] for k in SMALL_NAMES] + [loss_acc], [shard_major(dwdec_f16), shard_major(dwdec_b16)])
    g_all = _allgather_small(g_pack)
    pack_own = lambda pre: _pack_small([args[pre + k] for k in SMALL_NAMES], [args[pre + "w_decay_f"], args[pre + "w_decay_b"]])
    sg, sd, sm, sv = _adamw_small(g_all, pack_own(""), pack_own("m_"), pack_own("v_"))

    names = ["norm1_g", "w_in", "w_decay_f", "b_decay_f", "w_decay_b", "b_decay_b", "gla_norm_g", "gmlp_ln_g", "gmlp_ln_b",
             "w_spatial", "b_spatial", "w_out", "norm2_g", "w_gate", "w_up", "w_down", "final_norm_g"]
    like = [args[k] for k in SMALL_NAMES]
    results = {"g": {}, "d": {}, "m": {}, "v": {}}
    for tag, packed in (("g", sg), ("d", sd), ("m", sm), ("v", sv)):
        for k, a in zip(SMALL_NAMES, _unpack_small(packed, like)):
            results[tag][k] = a
        results[tag]["w_decay_f"] = packed[SMALL_ROWS : SMALL_ROWS + DECAY_ROWS].reshape(w_decay_f.shape)
        results[tag]["w_decay_b"] = packed[SMALL_ROWS + DECAY_ROWS :].reshape(w_decay_b.shape)
    for k in big_names:
        g, d, mo, vo = _adamw(big_shards[k], big_grads[k], rows_of("m_", k), rows_of("v_", k))
        for tag, a in (("g", g), ("d", d), ("m", mo), ("v", vo)):
            results[tag][k] = (jnp.transpose(a) if k in transposed else a).reshape(args[k].shape)

    loss = sg[:SMALL_ROWS].reshape(-1)[sum(a.size for a in like)]
    grad_x = dx.reshape(x.shape)
    return (loss, grad_x, *[results["g"][k] for k in names], *[results["d"][k] for k in names],
            *[results["m"][k] for k in names], *[results["v"][k] for k in names])
```

```python
import functools
import math

import jax
import jax.numpy as jnp
from jax import lax
from jax.experimental import pallas as pl
from jax.experimental.pallas import tpu as pltpu

F32, BF16 = jnp.float32, jnp.bfloat16

D_MODEL = 1024
GLA_HEADS = 4
GLA_DK = 64
GLA_DV = 128
KEY_W = GLA_HEADS * GLA_DK
GLA_W = GLA_HEADS * GLA_DV
GMLP_W = 512
GMLP_GROUPS = 4
GMLP_CHUNK = 128
LOWRANK = 16
GLA_CHUNK = 64
GLA_TAU = 16.0
PROJ_W = 2592
PROJ_WP = 2688
D_FF = 2816
N_SHARDS = 4
FF_SHARD = D_FF // N_SHARDS
EPS = 1e-6
LANES = 128
TOKEN_SHAPE = (8, LANES)
MIB = 1024 * 1024

ADAM_LR = 0.001
ADAM_B1 = 0.9
ADAM_B2 = 0.999
ADAM_EPS = 1e-08
ADAM_WD = 0.01
ADAM_STEP = 10

COL_Q, COL_K = 0, 256
COL_V, COL_G, COL_U, COL_VV = 512, 1024, 1536, 2048
COL_LR = 2560
ROW_LR, ROW_UV = 1536, 1568
HALF = D_MODEL // 2

MESH = pl.DeviceIdType.MESH


def _nn(a, b):
    return jnp.dot(a, b, preferred_element_type=F32)


def _nt(a, b):
    return lax.dot_general(a, b, (((1,), (1,)), ((), ())), preferred_element_type=F32)


def _tn(a, b):
    return lax.dot_general(a, b, (((0,), (0,)), ((), ())), preferred_element_type=F32)


def _bnn(a, b):
    return jnp.einsum("nik,nkj->nij", a, b, preferred_element_type=F32)


def _bnt(a, b):
    return jnp.einsum("nik,njk->nij", a, b, preferred_element_type=F32)


def _btn(a, b):
    return jnp.einsum("nki,nkj->nij", a, b, preferred_element_type=F32)


def _resident(shape):
    zeros = (0,) * len(shape)
    return pl.BlockSpec(shape, lambda *_: zeros, pipeline_mode=pl.Buffered(1))


def _params(vmem_mib, semantics=("arbitrary",)):
    return pltpu.CompilerParams(vmem_limit_bytes=vmem_mib * MIB, dimension_semantics=semantics)


def _sigmoid(x):
    return 1.0 / (1.0 + jnp.exp(-x))


def _gelu(x):
    return 0.5 * x * (1.0 + lax.erf(x * (1.0 / math.sqrt(2.0))))


def _gelu_and_grad(x):
    cdf = 0.5 * (1.0 + lax.erf(x * (1.0 / math.sqrt(2.0))))
    return x * cdf, cdf + x * jnp.exp(-0.5 * x * x) * (1.0 / math.sqrt(2.0 * math.pi))


def _log_sigmoid(x):
    return jnp.minimum(x, 0.0) - jnp.log(1.0 + jnp.exp(-jnp.abs(x)))


def _rms_bwd(dxh, xh, r):
    return r * (dxh - xh * jnp.mean(dxh * xh, axis=-1, keepdims=True))


def _chunk_cumsum(v, row_in_chunk, reverse):
    rows = v.shape[0]
    for sh in (1, 2, 4, 8, 16, 32):
        if reverse:
            v = v + jnp.where(row_in_chunk + sh < GLA_CHUNK, pltpu.roll(v, rows - sh, axis=0), 0.0)
        else:
            v = v + jnp.where(row_in_chunk >= sh, pltpu.roll(v, sh, axis=0), 0.0)
    return v


def _inproj(x, g1, w_in_t, token):
    seq = x.shape[0]
    tm = min(seq, 512)

    def body(x_ref, g_ref, w_ref, token_ref, p_ref):
        xv = x_ref[...]
        r = lax.rsqrt(jnp.mean(xv * xv, axis=-1, keepdims=True) + EPS)
        h = (xv * r * g_ref[...]).astype(BF16)
        p_ref[:, 0:COL_U] = _nt(h, w_ref[0:ROW_LR, :])
        p_ref[:, COL_U:COL_LR] = _nt(h, w_ref[ROW_UV:PROJ_W, :])
        p_ref[:, COL_LR:PROJ_WP] = _nt(h, w_ref[ROW_LR : ROW_LR + LANES, :])

    return pl.pallas_call(
        body,
        name="inproj",
        grid=(seq // tm,),
        in_specs=[pl.BlockSpec((tm, D_MODEL), lambda i: (i, 0)), _resident((1, D_MODEL)), _resident((PROJ_W, D_MODEL)), _resident(TOKEN_SHAPE)],
        out_specs=pl.BlockSpec((tm, PROJ_WP), lambda i: (i, 0)),
        out_shape=jax.ShapeDtypeStruct((seq, PROJ_WP), F32),
        compiler_params=_params(48, ("parallel",)),
    )(x, g1, w_in_t, token)


def _gla_tile(seq):
    return min(seq, 1024)


def _gla_decay_terms(lr_bf, wd_ref, bd_ref, pair, row_in_chunk, reverse, n):
    cols = pl.ds(pair * LANES, LANES)
    pre = _nn(lr_bf, wd_ref[:, cols]) + bd_ref[:, cols]
    la = _log_sigmoid(pre) * (1.0 / GLA_TAU)
    b = _chunk_cumsum(la, row_in_chunk, reverse)
    b3 = b.reshape(n, GLA_CHUNK, LANES)
    blast = b3[:, 0:1, :] if reverse else b3[:, GLA_CHUNK - 1 : GLA_CHUNK, :]
    return pre, b3, blast


def _gla_fwd(p, wd_pad, bd, token, reverse):
    seq = p.shape[0]
    tg = _gla_tile(seq)
    nt = seq // tg
    n = tg // GLA_CHUNK
    scale = GLA_DK**-0.5

    def tile(i):
        return nt - 1 - i if reverse else i

    def body(q_ref, k_ref, v_ref, lr_ref, wd_ref, bd_ref, token_ref, o_ref, st_ref, carry):
        @pl.when(pl.program_id(0) == 0)
        def _():
            carry[...] = jnp.zeros_like(carry)

        lr_bf = lr_ref[...].astype(BF16)
        states = [carry[h] for h in range(GLA_HEADS)]
        row_in_chunk = lax.broadcasted_iota(jnp.int32, (tg, LANES), 0) % GLA_CHUNK
        lane_head = lax.broadcasted_iota(jnp.int32, (1, LANES), 1) // GLA_DK
        tt = lax.broadcasted_iota(jnp.int32, (GLA_CHUNK, GLA_CHUNK), 0)
        ss = lax.broadcasted_iota(jnp.int32, (GLA_CHUNK, GLA_CHUNK), 1)
        causal = (tt <= ss) if reverse else (tt >= ss)
        order = range(n - 1, -1, -1) if reverse else range(n)
        heads = range(GLA_HEADS)
        qds, vhs, decs, sc_raw, dst = {}, {}, {}, {}, {}
        for pair in range(2):
            cols = pl.ds(pair * LANES, LANES)
            _, b3, blast = _gla_decay_terms(lr_bf, wd_ref, bd_ref, pair, row_in_chunk, reverse, n)
            q3 = q_ref[:, cols].reshape(n, GLA_CHUNK, LANES) * scale
            k3 = k_ref[:, cols].reshape(n, GLA_CHUNK, LANES)
            qd = q3 * jnp.exp(b3)
            kd = (k3 * jnp.exp(-b3)).astype(BF16)
            kte = k3 * jnp.exp(blast - b3)
            decs[pair] = jnp.exp(blast)
            qds[pair] = qd.astype(BF16)
            m0 = (lane_head == 0).astype(F32)
            m1 = (lane_head == 1).astype(F32)
            q_both = jnp.concatenate([(qd * m0).astype(BF16), (qd * m1).astype(BF16)], axis=1)
            sc_both = _bnt(q_both, kd)
            for hh, m in ((0, m0), (1, m1)):
                h = 2 * pair + hh
                vhs[h] = v_ref[:, pl.ds(h * GLA_DV, GLA_DV)].reshape(n, GLA_CHUNK, GLA_DV).astype(BF16)
                sc_raw[h] = sc_both[:, hh * GLA_CHUNK : (hh + 1) * GLA_CHUNK, :]
                dst[h] = _btn(vhs[h], (kte * m).astype(BF16))
        o_intra, befores = {}, {}
        for h in heads:
            o_intra[h] = _bnn(jnp.where(causal, sc_raw[h], 0.0).astype(BF16), vhs[h])
            st, before = states[h], [None] * n
            for j in order:
                before[j] = st
                st = st * decs[h // 2][j] + dst[h][j]
            states[h] = st
            befores[h] = jnp.stack(before).astype(BF16)
        outs = {}
        for pair in range(2):
            both = jnp.concatenate([befores[2 * pair], befores[2 * pair + 1]], axis=1)
            o_inter = _bnt(qds[pair], both)
            for hh in range(2):
                h = 2 * pair + hh
                outs[h] = (o_intra[h] + o_inter[:, :, hh * GLA_DV : (hh + 1) * GLA_DV]).reshape(tg, GLA_DV)
        for h in range(GLA_HEADS):
            o_ref[:, pl.ds(h * GLA_DV, GLA_DV)] = outs[h]
            st_ref[:, h] = befores[h]
            carry[h] = states[h]

    nchunks = seq // GLA_CHUNK
    return pl.pallas_call(
        body,
        name="gla_fwd_rev" if reverse else "gla_fwd",
        grid=(nt,),
        in_specs=[
            pl.BlockSpec((tg, KEY_W), lambda i: (tile(i), COL_Q // KEY_W)),
            pl.BlockSpec((tg, KEY_W), lambda i: (tile(i), COL_K // KEY_W)),
            pl.BlockSpec((tg, GLA_W), lambda i: (tile(i), COL_V // GLA_W)),
            pl.BlockSpec((tg, LANES), lambda i: (tile(i), COL_LR // LANES)),
            _resident((LANES, KEY_W)),
            _resident((1, KEY_W)),
            _resident(TOKEN_SHAPE),
        ],
        out_specs=[
            pl.BlockSpec((tg, GLA_W), lambda i: (tile(i), 0)),
            pl.BlockSpec((n, GLA_HEADS, GLA_DV, LANES), lambda i: (tile(i), 0, 0, 0)),
        ],
        out_shape=[
            jax.ShapeDtypeStruct((seq, GLA_W), F32),
            jax.ShapeDtypeStruct((nchunks, GLA_HEADS, GLA_DV, LANES), BF16),
        ],
        scratch_shapes=[pltpu.VMEM((GLA_HEADS, GLA_DV, LANES), F32)],
        compiler_params=_params(48),
    )(p, p, p, p, wd_pad, bd, token)


def _mixer_out(x, o_f, o_b, p, gn, lng, lnb, ws_bf, bs_col, w_out):
    seq = x.shape[0]
    tm = min(seq, 512)

    def body(x_ref, of_ref, ob_ref, g_ref, u_ref, vv_ref, gn_ref, lng_ref, lnb_ref, ws_ref, bs_ref, wo_ref, x1_ref, yc_ref, vn_sc):
        for h in range(GLA_HEADS):
            cols = pl.ds(h * GLA_DV, GLA_DV)
            oh = of_ref[:, cols] + ob_ref[:, cols]
            on = oh * lax.rsqrt(jnp.mean(oh * oh, axis=-1, keepdims=True) + EPS)
            gh = g_ref[:, cols]
            yc_ref[:, cols] = (on * gn_ref[:, cols] * (gh * _sigmoid(gh))).astype(BF16)
        zv = _gelu(vv_ref[...])
        xc = zv - jnp.mean(zv, axis=-1, keepdims=True)
        vhat = xc * lax.rsqrt(jnp.mean(xc * xc, axis=-1, keepdims=True) + EPS)
        vn_sc[...] = (vhat * lng_ref[...] + lnb_ref[...]).astype(BF16)
        for c in range(tm // GMLP_CHUNK):
            rows = pl.ds(c * GMLP_CHUNK, GMLP_CHUNK)
            for g in range(GMLP_GROUPS):
                cols = pl.ds(g * LANES, LANES)
                s = _nn(ws_ref[g], vn_sc[rows, cols]) + bs_ref[g]
                yc_ref[rows, pl.ds(GLA_W + g * LANES, LANES)] = (_gelu(u_ref[rows, cols]) * s).astype(BF16)
        x1_ref[...] = x_ref[...] + _nn(yc_ref[...], wo_ref[...])

    row = lambda w: pl.BlockSpec((tm, w), lambda i: (i, 0))
    pcol = lambda col: pl.BlockSpec((tm, GLA_W), lambda i: (i, col // GLA_W))
    return pl.pallas_call(
        body,
        name="mixer_out",
        grid=(seq // tm,),
        in_specs=[
            row(D_MODEL), row(GLA_W), row(GLA_W), pcol(COL_G), pcol(COL_U), pcol(COL_VV),
            _resident((1, GLA_W)), _resident((1, GMLP_W)), _resident((1, GMLP_W)),
            _resident((GMLP_GROUPS, GMLP_CHUNK, GMLP_CHUNK)), _resident((GMLP_GROUPS, GMLP_CHUNK, 1)),
            _resident((D_MODEL, D_MODEL)),
        ],
        out_specs=[row(D_MODEL), row(D_MODEL)],
        out_shape=[jax.ShapeDtypeStruct((seq, D_MODEL), F32), jax.ShapeDtypeStruct((seq, D_MODEL), BF16)],
        scratch_shapes=[pltpu.VMEM((tm, GMLP_W), BF16)],
        compiler_params=_params(48, ("parallel",)),
    )(x, o_f, o_b, p, p, p, gn, lng, lnb, ws_bf, bs_col, w_out)


def _ffn_fwd(x1, target, g2, gf, wg_t, wu_t, wd):
    seq = x1.shape[0]
    tm = min(seq, 256)

    def body(x1_ref, t_ref, g2_ref, gf_ref, wg_ref, wu_ref, wd_ref, h2_ref, gate_ref, up_ref, act_ref, dx2_ref, loss_ref, dgf_ref):
        @pl.when(pl.program_id(0) == 0)
        def _():
            loss_ref[...] = jnp.zeros_like(loss_ref)
            dgf_ref[...] = jnp.zeros_like(dgf_ref)

        x1v = x1_ref[...]
        h2 = (x1v * lax.rsqrt(jnp.mean(x1v * x1v, axis=-1, keepdims=True) + EPS) * g2_ref[...]).astype(BF16)
        h2_ref[...] = h2
        gate = _nt(h2, wg_ref[...])
        up = _nt(h2, wu_ref[...])
        act = (gate * _sigmoid(gate) * up).astype(BF16)
        gate_ref[...] = gate
        up_ref[...] = up
        act_ref[...] = act
        x2 = x1v + _nn(act, wd_ref[...])
        rf = lax.rsqrt(jnp.mean(x2 * x2, axis=-1, keepdims=True) + EPS)
        xh = x2 * rf
        err = xh * gf_ref[...] - t_ref[...]
        loss_ref[...] += 0.5 * jnp.sum(jnp.mean(err * err, axis=-1, keepdims=True))
        dy = err * (1.0 / D_MODEL)
        dgf_ref[...] += jnp.sum(dy * xh, axis=0, keepdims=True)
        dx2_ref[...] = _rms_bwd(dy * gf_ref[...], xh, rf)

    row = lambda w: pl.BlockSpec((tm, w), lambda i: (i, 0))
    weight = _resident((D_FF, D_MODEL))
    return pl.pallas_call(
        body,
        name="ffn_fwd",
        grid=(seq // tm,),
        in_specs=[row(D_MODEL), row(D_MODEL), _resident((1, D_MODEL)), _resident((1, D_MODEL)), weight, weight, weight],
        out_specs=[row(D_MODEL), row(D_FF), row(D_FF), row(D_FF), row(D_MODEL),
                   pl.BlockSpec((1, LANES), lambda i: (0, 0)), pl.BlockSpec((1, D_MODEL), lambda i: (0, 0))],
        out_shape=[
            jax.ShapeDtypeStruct((seq, D_MODEL), BF16),
            jax.ShapeDtypeStruct((seq, D_FF), F32),
            jax.ShapeDtypeStruct((seq, D_FF), F32),
            jax.ShapeDtypeStruct((seq, D_FF), BF16),
            jax.ShapeDtypeStruct((seq, D_MODEL), F32),
            jax.ShapeDtypeStruct((1, LANES), F32),
            jax.ShapeDtypeStruct((1, D_MODEL), F32),
        ],
        compiler_params=_params(56),
    )(x1, target, g2, gf, wg_t, wu_t, wd)


def _ffn_bwd(dx2, gate, up, x1, g2, wg_t, wu_t, wd):
    seq = x1.shape[0]
    tm = min(seq, 256)

    def body(dx2_ref, gate_ref, up_ref, x1_ref, g2_ref, wg_ref, wu_ref, wd_ref, dgate_ref, dup_ref, dx1_ref, dg2_ref):
        @pl.when(pl.program_id(0) == 0)
        def _():
            dg2_ref[...] = jnp.zeros_like(dg2_ref)

        dx2v = dx2_ref[...]
        dact = _nt(dx2v.astype(BF16), wd_ref[...])
        gate = gate_ref[...]
        sg = _sigmoid(gate)
        dgate = (dact * up_ref[...] * (sg * (1.0 + gate * (1.0 - sg)))).astype(BF16)
        dup = (dact * (gate * sg)).astype(BF16)
        dgate_ref[...] = dgate
        dup_ref[...] = dup
        dh2 = _nn(dgate, wg_ref[...]) + _nn(dup, wu_ref[...])
        x1v = x1_ref[...]
        r2 = lax.rsqrt(jnp.mean(x1v * x1v, axis=-1, keepdims=True) + EPS)
        xh = x1v * r2
        dg2_ref[...] += jnp.sum(dh2 * xh, axis=0, keepdims=True)
        dx1_ref[...] = dx2v + _rms_bwd(dh2 * g2_ref[...], xh, r2)

    row = lambda w: pl.BlockSpec((tm, w), lambda i: (i, 0))
    weight = _resident((D_FF, D_MODEL))
    return pl.pallas_call(
        body,
        name="ffn_bwd",
        grid=(seq // tm,),
        in_specs=[row(D_MODEL), row(D_FF), row(D_FF), row(D_MODEL), _resident((1, D_MODEL)), weight, weight, weight],
        out_specs=[row(D_FF), row(D_FF), row(D_MODEL), pl.BlockSpec((1, D_MODEL), lambda i: (0, 0))],
        out_shape=[
            jax.ShapeDtypeStruct((seq, D_FF), BF16),
            jax.ShapeDtypeStruct((seq, D_FF), BF16),
            jax.ShapeDtypeStruct((seq, D_MODEL), F32),
            jax.ShapeDtypeStruct((1, D_MODEL), F32),
        ],
        compiler_params=_params(56),
    )(dx2, gate, up, x1, g2, wg_t, wu_t, wd)


WGRAD_ROWS = D_FF // 2


def _ffn_wgrad(h2, dgate, dup, act, dx2):
    seq = h2.shape[0]
    tm = min(seq, 512)

    def body(h2_ref, dgate_ref, dup_ref, act_ref, dx2_ref, dwg_ref, dwu_ref, dwd_ref):
        @pl.when(pl.program_id(1) == 0)
        def _():
            dwg_ref[...] = jnp.zeros_like(dwg_ref)
            dwu_ref[...] = jnp.zeros_like(dwu_ref)
            dwd_ref[...] = jnp.zeros_like(dwd_ref)

        h2v = h2_ref[...]
        dwg_ref[...] += _tn(dgate_ref[...], h2v)
        dwu_ref[...] += _tn(dup_ref[...], h2v)
        dwd_ref[...] += _tn(act_ref[...], dx2_ref[...].astype(BF16))

    ff = pl.BlockSpec((tm, WGRAD_ROWS), lambda j, i: (i, j))
    row = pl.BlockSpec((tm, D_MODEL), lambda j, i: (i, 0))
    out = pl.BlockSpec((WGRAD_ROWS, D_MODEL), lambda j, i: (j, 0))
    return pl.pallas_call(
        body,
        name="ffn_wgrad",
        grid=(D_FF // WGRAD_ROWS, seq // tm),
        in_specs=[row, ff, ff, ff, row],
        out_specs=[out, out, out],
        out_shape=[jax.ShapeDtypeStruct((D_FF, D_MODEL), F32)] * 3,
        compiler_params=_params(56, ("parallel", "arbitrary")),
    )(h2, dgate, dup, act, dx2)


def _mixer_bwd(dx1, ycat, o_f, o_b, p, gn, lng, lnb, ws_bf, wst_bf, bs_col, w_out, token):
    seq = dx1.shape[0]
    tm = min(seq, 512)
    nsteps = seq // tm

    def body(dx1_ref, yc_ref, of_ref, ob_ref, g_ref, u_ref, vv_ref, gn_ref, lng_ref, lnb_ref, ws_ref, wst_ref, bs_ref, wo_ref, token_ref,
             do_ref, dg_ref, du_ref, dvv_ref, dwo_ref, dgn_ref, dlng_ref, dlnb_ref, dws_ref, dbs_ref, vn_sc, dvn_sc, dbs_acc):
        step = pl.program_id(0)

        @pl.when(step == 0)
        def _():
            for r in (dwo_ref, dgn_ref, dlng_ref, dlnb_ref, dws_ref, dbs_acc):
                r[...] = jnp.zeros_like(r)

        dx1b = dx1_ref[...].astype(BF16)
        dyc = _nt(dx1b, wo_ref[...])
        dwo_ref[...] += _tn(yc_ref[...], dx1b)
        for h in range(GLA_HEADS):
            cols = pl.ds(h * GLA_DV, GLA_DV)
            dya = dyc[:, h * GLA_DV : (h + 1) * GLA_DV]
            oh = of_ref[:, cols] + ob_ref[:, cols]
            rn = lax.rsqrt(jnp.mean(oh * oh, axis=-1, keepdims=True) + EPS)
            on = oh * rn
            gh = g_ref[:, cols]
            sg = _sigmoid(gh)
            sil = gh * sg
            gnh = gn_ref[:, cols]
            dgn_ref[:, cols] += jnp.sum(dya * on * sil, axis=0, keepdims=True)
            dg_ref[:, cols] = (dya * on * gnh * (sg * (1.0 + gh * (1.0 - sg)))).astype(BF16)
            do_ref[:, cols] = _rms_bwd(dya * gnh * sil, on, rn)
        vv = vv_ref[...]
        zv, zv_grad = _gelu_and_grad(vv)
        xc = zv - jnp.mean(zv, axis=-1, keepdims=True)
        rstd = lax.rsqrt(jnp.mean(xc * xc, axis=-1, keepdims=True) + EPS)
        vhat = xc * rstd
        vn_sc[...] = (vhat * lng_ref[...] + lnb_ref[...]).astype(BF16)
        for c in range(tm // GMLP_CHUNK):
            rows = pl.ds(c * GMLP_CHUNK, GMLP_CHUNK)
            for g in range(GMLP_GROUPS):
                cols = pl.ds(g * LANES, LANES)
                vn = vn_sc[rows, cols]
                s = _nn(ws_ref[g], vn) + bs_ref[g]
                dyb = dyc[c * GMLP_CHUNK : (c + 1) * GMLP_CHUNK, GLA_W + g * LANES : GLA_W + (g + 1) * LANES]
                zu, zu_grad = _gelu_and_grad(u_ref[rows, cols])
                du_ref[rows, cols] = (dyb * s * zu_grad).astype(BF16)
                ds = dyb * zu
                dbs_acc[g] += ds
                dsb = ds.astype(BF16)
                dws_ref[g] += _nt(dsb, vn)
                dvn_sc[rows, cols] = _nn(wst_ref[g], dsb)
        dvn = dvn_sc[...]
        dlng_ref[...] += jnp.sum(dvn * vhat, axis=0, keepdims=True)
        dlnb_ref[...] += jnp.sum(dvn, axis=0, keepdims=True)
        dvh = dvn * lng_ref[...]
        dzv = rstd * (dvh - jnp.mean(dvh, axis=-1, keepdims=True) - vhat * jnp.mean(dvh * vhat, axis=-1, keepdims=True))
        dvv_ref[...] = (dzv * zv_grad).astype(BF16)

        @pl.when(step == nsteps - 1)
        def _():
            dbs_ref[...] = jnp.sum(dbs_acc[...], axis=-1, keepdims=True)

    row = lambda w: pl.BlockSpec((tm, w), lambda i: (i, 0))
    pcol = lambda col: pl.BlockSpec((tm, GLA_W), lambda i: (i, col // GLA_W))
    const = lambda shape: pl.BlockSpec(shape, lambda i: (0,) * len(shape))
    return pl.pallas_call(
        body,
        name="mixer_bwd",
        grid=(nsteps,),
        in_specs=[
            row(D_MODEL), row(D_MODEL), row(GLA_W), row(GLA_W), pcol(COL_G), pcol(COL_U), pcol(COL_VV),
            _resident((1, GLA_W)), _resident((1, GMLP_W)), _resident((1, GMLP_W)),
            _resident((GMLP_GROUPS, GMLP_CHUNK, GMLP_CHUNK)), _resident((GMLP_GROUPS, GMLP_CHUNK, GMLP_CHUNK)),
            _resident((GMLP_GROUPS, GMLP_CHUNK, 1)), _resident((D_MODEL, D_MODEL)), _resident(TOKEN_SHAPE),
        ],
        out_specs=[
            row(GLA_W), row(GLA_W), row(GMLP_W), row(GMLP_W), const((D_MODEL, D_MODEL)),
            const((1, GLA_W)), const((1, GMLP_W)), const((1, GMLP_W)),
            const((GMLP_GROUPS, GMLP_CHUNK, GMLP_CHUNK)), const((GMLP_GROUPS, GMLP_CHUNK, 1)),
        ],
        out_shape=[
            jax.ShapeDtypeStruct((seq, GLA_W), F32), jax.ShapeDtypeStruct((seq, GLA_W), BF16),
            jax.ShapeDtypeStruct((seq, GMLP_W), BF16), jax.ShapeDtypeStruct((seq, GMLP_W), BF16),
            jax.ShapeDtypeStruct((D_MODEL, D_MODEL), F32),
            jax.ShapeDtypeStruct((1, GLA_W), F32), jax.ShapeDtypeStruct((1, GMLP_W), F32), jax.ShapeDtypeStruct((1, GMLP_W), F32),
            jax.ShapeDtypeStruct((GMLP_GROUPS, GMLP_CHUNK, GMLP_CHUNK), F32), jax.ShapeDtypeStruct((GMLP_GROUPS, GMLP_CHUNK, 1), F32),
        ],
        scratch_shapes=[pltpu.VMEM((tm, GMLP_W), BF16), pltpu.VMEM((tm, GMLP_W), F32), pltpu.VMEM((GMLP_GROUPS, GMLP_CHUNK, GMLP_CHUNK), F32)],
        compiler_params=_params(56),
    )(dx1, ycat, o_f, o_b, p, p, p, gn, lng, lnb, ws_bf, wst_bf, bs_col, w_out, token)


def _gla_bwd(p, do, st, wd_pad, bd, token, reverse, other=None):
    seq = p.shape[0]
    tg = _gla_tile(seq)
    nt = seq // tg
    n = tg // GLA_CHUNK
    scale = GLA_DK**-0.5

    def tile(i):
        return i if reverse else nt - 1 - i

    def body(q_ref, k_ref, v_ref, lr_ref, do_ref, st_ref, wd_ref, bd_ref, token_ref, *rest):
        others, (dq_ref, dk_ref, dv_ref, dlr_ref, dwd_ref, dbd_ref, carry) = rest[:-7], rest[-7:]
        if others:
            odq_ref, odk_ref, odv_ref, odlr_ref = others

            def put(ref, idx, val, oref):
                ref[idx] = (val + oref[idx]).astype(BF16)
        else:
            odq_ref = odk_ref = odv_ref = odlr_ref = None

            def put(ref, idx, val, oref):
                ref[idx] = val

        @pl.when(pl.program_id(0) == 0)
        def _():
            carry[...] = jnp.zeros_like(carry)
            dwd_ref[...] = jnp.zeros_like(dwd_ref)
            dbd_ref[...] = jnp.zeros_like(dbd_ref)

        lr_bf = lr_ref[...].astype(BF16)
        carries = [carry[h] for h in range(GLA_HEADS)]
        row_in_chunk = lax.broadcasted_iota(jnp.int32, (tg, LANES), 0) % GLA_CHUNK
        lane_head = lax.broadcasted_iota(jnp.int32, (1, LANES), 1) // GLA_DK
        tt = lax.broadcasted_iota(jnp.int32, (GLA_CHUNK, GLA_CHUNK), 0)
        ss = lax.broadcasted_iota(jnp.int32, (GLA_CHUNK, GLA_CHUNK), 1)
        causal = (tt <= ss) if reverse else (tt >= ss)
        order = range(n) if reverse else range(n - 1, -1, -1)
        dlr = jnp.zeros((tg, LANES), F32)
        heads = range(GLA_HEADS)
        pv, masks, qdh, vhs, dohs, stbs = {}, {}, {}, {}, {}, {}
        sc_raw, dp, acc = {}, {}, {}
        for pair in range(2):
            cols = pl.ds(pair * LANES, LANES)
            pre, b3, blast = _gla_decay_terms(lr_bf, wd_ref, bd_ref, pair, row_in_chunk, reverse, n)
            q3 = q_ref[:, cols].reshape(n, GLA_CHUNK, LANES) * scale
            k3 = k_ref[:, cols].reshape(n, GLA_CHUNK, LANES)
            eb = jnp.exp(b3)
            emb = jnp.exp(-b3)
            ekte = jnp.exp(blast - b3)
            kdf = k3 * emb
            kte = k3 * ekte
            both = pl.ds(2 * pair * GLA_DV, 2 * GLA_DV)
            pv[pair] = dict(pre=pre, eb=eb, emb=emb, ekte=ekte, qd=q3 * eb, kdf=kdf, kd=kdf.astype(BF16), kte=kte, kte_bf=kte.astype(BF16),
                            dec=jnp.exp(blast), v=v_ref[:, both].reshape(n, GLA_CHUNK, 2 * GLA_DV).astype(BF16),
                            do=do_ref[:, both].reshape(n, GLA_CHUNK, 2 * GLA_DV).astype(BF16))
            for hh in range(2):
                h = 2 * pair + hh
                masks[h] = (lane_head == hh).astype(F32)
                qdh[h] = (pv[pair]["qd"] * masks[h]).astype(BF16)
                vhs[h] = pv[pair]["v"][:, :, hh * GLA_DV : (hh + 1) * GLA_DV]
                dohs[h] = pv[pair]["do"][:, :, hh * GLA_DV : (hh + 1) * GLA_DV]
                stbs[h] = st_ref[:, h]
                dp[h] = _bnt(dohs[h], vhs[h])
                acc[h] = _btn(dohs[h], qdh[h])
            sc_both = _bnt(jnp.concatenate([qdh[2 * pair], qdh[2 * pair + 1]], axis=1), pv[pair]["kd"])
            for hh in range(2):
                sc_raw[2 * pair + hh] = sc_both[:, hh * GLA_CHUNK : (hh + 1) * GLA_CHUNK, :]
        dsa, sc = {}, {}
        for h in heads:
            sc[h] = jnp.where(causal, sc_raw[h], 0.0).astype(BF16)
            dp[h] = jnp.where(causal, dp[h], 0.0).astype(BF16)
            dec = pv[h // 2]["dec"]
            c, after = carries[h], [None] * n
            for j in order:
                after[j] = c
                c = acc[h][j] + dec[j] * c
            carries[h] = c
            dsa[h] = jnp.stack(after)
        dvs, dqs, dks, dwds, dbds = [], [], [], [], []
        for pair in range(2):
            cols = pl.ds(pair * LANES, LANES)
            v = pv[pair]
            h0, h1 = 2 * pair, 2 * pair + 1
            dsa_both = jnp.concatenate([dsa[h0], dsa[h1]], axis=1)
            dsa_bf = dsa_both.astype(BF16)
            stb_bf = jnp.concatenate([stbs[h0], stbs[h1]], axis=1)
            dq_intra = _bnn(jnp.concatenate([dp[h0], dp[h1]], axis=1), v["kd"])
            dqd = (dq_intra[:, :GLA_CHUNK, :] * masks[h0] + dq_intra[:, GLA_CHUNK:, :] * masks[h1]) + _bnn(v["do"], stb_bf)
            dkd = _btn(dp[h0], qdh[h0]) + _btn(dp[h1], qdh[h1])
            dkte = _bnn(v["v"], dsa_bf)
            ddec = jnp.sum(dsa[h0] * stbs[h0].astype(F32) + dsa[h1] * stbs[h1].astype(F32), axis=1, keepdims=True)
            dv_inter = _bnt(v["kte_bf"], dsa_bf)
            for hh, h in ((0, h0), (1, h1)):
                dvs.append((_btn(sc[h], dohs[h]) + dv_inter[:, :, hh * GLA_DV : (hh + 1) * GLA_DV]).reshape(tg, GLA_DV))
            dqs.append((dqd * (scale * v["eb"])).reshape(tg, LANES))
            dks.append((dkd * v["emb"] + dkte * v["ekte"]).reshape(tg, LANES))
            db = dqd * v["qd"] - dkd * v["kdf"] - dkte * v["kte"]
            dblast = jnp.sum(dkte * v["kte"], axis=1, keepdims=True) + ddec * v["dec"]
            dla = _chunk_cumsum(db.reshape(tg, LANES), row_in_chunk, not reverse) + jnp.broadcast_to(dblast, (n, GLA_CHUNK, LANES)).reshape(tg, LANES)
            dpre = (dla * (1.0 / GLA_TAU) * _sigmoid(-v["pre"]))
            dpre_bf = dpre.astype(BF16)
            dlr = dlr + _nt(dpre_bf, wd_ref[:, cols])
            dwds.append(_tn(lr_bf, dpre_bf))
            dbds.append(jnp.sum(dpre, axis=0, keepdims=True))
        put(dlr_ref, (slice(None), slice(None)), dlr, odlr_ref)
        for pair in range(2):
            cols = pl.ds(pair * LANES, LANES)
            put(dq_ref, (slice(None), cols), dqs[pair], odq_ref)
            put(dk_ref, (slice(None), cols), dks[pair], odk_ref)
            dwd_ref[:, cols] += dwds[pair]
            dbd_ref[:, cols] += dbds[pair]
        for h in range(GLA_HEADS):
            put(dv_ref, (slice(None), pl.ds(h * GLA_DV, GLA_DV)), dvs[h], odv_ref)
            carry[h] = carries[h]

    pieces = [
        pl.BlockSpec((tg, KEY_W), lambda i: (tile(i), 0)),
        pl.BlockSpec((tg, KEY_W), lambda i: (tile(i), 0)),
        pl.BlockSpec((tg, GLA_W), lambda i: (tile(i), 0)),
        pl.BlockSpec((tg, LANES), lambda i: (tile(i), 0)),
    ]
    piece_dtype = BF16 if other else F32
    return pl.pallas_call(
        body,
        name="gla_bwd_rev" if reverse else "gla_bwd",
        grid=(nt,),
        in_specs=[
            pl.BlockSpec((tg, KEY_W), lambda i: (tile(i), COL_Q // KEY_W)),
            pl.BlockSpec((tg, KEY_W), lambda i: (tile(i), COL_K // KEY_W)),
            pl.BlockSpec((tg, GLA_W), lambda i: (tile(i), COL_V // GLA_W)),
            pl.BlockSpec((tg, LANES), lambda i: (tile(i), COL_LR // LANES)),
            pl.BlockSpec((tg, GLA_W), lambda i: (tile(i), 0)),
            pl.BlockSpec((n, GLA_HEADS, GLA_DV, LANES), lambda i: (tile(i), 0, 0, 0)),
            _resident((LANES, KEY_W)),
            _resident((1, KEY_W)),
            _resident(TOKEN_SHAPE),
        ] + (pieces if other else []),
        out_specs=pieces + [pl.BlockSpec((LANES, KEY_W), lambda i: (0, 0)), pl.BlockSpec((1, KEY_W), lambda i: (0, 0))],
        out_shape=[
            jax.ShapeDtypeStruct((seq, KEY_W), piece_dtype), jax.ShapeDtypeStruct((seq, KEY_W), piece_dtype),
            jax.ShapeDtypeStruct((seq, GLA_W), piece_dtype), jax.ShapeDtypeStruct((seq, LANES), piece_dtype),
            jax.ShapeDtypeStruct((LANES, KEY_W), F32), jax.ShapeDtypeStruct((1, KEY_W), F32),
        ],
        scratch_shapes=[pltpu.VMEM((GLA_HEADS, GLA_DV, LANES), F32)],
        compiler_params=_params(56),
    )(p, p, p, p, do, st, wd_pad, bd, token, *(other or ()))


def _inproj_wgrad(x, g1, dq, dk, dv, dg, du, dvv, dlr):
    seq = x.shape[0]
    tm = min(seq, 512)

    def body(x_ref, g1_ref, dq_ref, dk_ref, dv_ref, dg_ref, du_ref, dvv_ref, dlr_ref, dw_ref, dp_ref):
        @pl.when(pl.program_id(0) == 0)
        def _():
            dw_ref[...] = jnp.zeros_like(dw_ref)

        for col, ref in ((COL_Q, dq_ref), (COL_K, dk_ref), (COL_V, dv_ref), (COL_G, dg_ref), (COL_U, du_ref), (COL_VV, dvv_ref), (COL_LR, dlr_ref)):
            dp_ref[:, col : col + ref.shape[1]] = ref[...]
        xv = x_ref[...]
        h = (xv * lax.rsqrt(jnp.mean(xv * xv, axis=-1, keepdims=True) + EPS) * g1_ref[...]).astype(BF16)
        dw_ref[0:ROW_LR, :] += _tn(dp_ref[:, 0:COL_U], h)
        dw_ref[ROW_UV:PROJ_W, :] += _tn(dp_ref[:, COL_U:COL_LR], h)
        dw_ref[ROW_LR:ROW_UV, :] += _tn(dp_ref[:, COL_LR:PROJ_WP], h)[0 : ROW_UV - ROW_LR]

    row = lambda w: pl.BlockSpec((tm, w), lambda i: (i, 0))
    return pl.pallas_call(
        body,
        name="inproj_wgrad",
        grid=(seq // tm,),
        in_specs=[row(D_MODEL), _resident((1, D_MODEL)), row(KEY_W), row(KEY_W), row(GLA_W), row(GLA_W), row(GMLP_W), row(GMLP_W), row(LANES)],
        out_specs=[pl.BlockSpec((PROJ_W, D_MODEL), lambda i: (0, 0)), row(PROJ_WP)],
        out_shape=[jax.ShapeDtypeStruct((PROJ_W, D_MODEL), F32), jax.ShapeDtypeStruct((seq, PROJ_WP), BF16)],
        compiler_params=_params(56),
    )(x, g1, dq, dk, dv, dg, du, dvv, dlr)


def _inproj_dx(x, dx1, g1, w_in_t, dp, token):
    seq = x.shape[0]
    tm = min(seq, 512)

    def body(x_ref, dx1_ref, g1_ref, w_ref, dp_ref, token_ref, dx_ref, dg1_ref):
        @pl.when(pl.program_id(0) == 0)
        def _():
            dg1_ref[...] = jnp.zeros_like(dg1_ref)

        xv = x_ref[...]
        r1 = lax.rsqrt(jnp.mean(xv * xv, axis=-1, keepdims=True) + EPS)
        xh = xv * r1
        dh = (_nn(dp_ref[:, 0:COL_U], w_ref[0:ROW_LR, :]) + _nn(dp_ref[:, COL_U:COL_LR], w_ref[ROW_UV:PROJ_W, :])
              + _nn(dp_ref[:, COL_LR:PROJ_WP], w_ref[ROW_LR : ROW_LR + LANES, :]))
        dg1_ref[...] += jnp.sum(dh * xh, axis=0, keepdims=True)
        dx_ref[...] = dx1_ref[...] + _rms_bwd(dh * g1_ref[...], xh, r1)

    row = lambda w: pl.BlockSpec((tm, w), lambda i: (i, 0))
    return pl.pallas_call(
        body,
        name="inproj_dx",
        grid=(seq // tm,),
        in_specs=[row(D_MODEL), row(D_MODEL), _resident((1, D_MODEL)), _resident((PROJ_W, D_MODEL)), row(PROJ_WP), _resident(TOKEN_SHAPE)],
        out_specs=[row(D_MODEL), pl.BlockSpec((1, D_MODEL), lambda i: (0, 0))],
        out_shape=[jax.ShapeDtypeStruct((seq, D_MODEL), F32), jax.ShapeDtypeStruct((1, D_MODEL), F32)],
        compiler_params=_params(48),
    )(x, dx1, g1, w_in_t, dp, token)


def _in_hbm(a):
    return pltpu.with_memory_space_constraint(a, pltpu.HBM)


def _row_tile(rows, multiple=8):
    for t in range(min(rows, 512), 0, -1):
        if rows % t == 0 and t % multiple == 0:
            return t
    return rows


def _cast_into_slot(w, shard, token):
    rows, cols = w.shape
    tr = _row_tile(rows, 16)

    def body(s_ref, w_ref, token_ref, o_ref):
        o_ref[...] = w_ref[...].astype(BF16)

    return pl.pallas_call(
        body,
        name="cast_into_slot",
        grid_spec=pltpu.PrefetchScalarGridSpec(
            num_scalar_prefetch=1,
            grid=(rows // tr,),
            in_specs=[pl.BlockSpec((tr, cols), lambda i, s_ref: (i, 0)), pl.BlockSpec(TOKEN_SHAPE, lambda i, s_ref: (0, 0))],
            out_specs=pl.BlockSpec((None, tr, cols), lambda i, s_ref: (s_ref[0], i, 0)),
        ),
        out_shape=pltpu.HBM((N_SHARDS, rows, cols), BF16),
        compiler_params=_params(32, ("parallel",)),
    )(shard, _in_hbm(w), token)


def _add_halves(grads4, recvs, c):
    n = len(grads4)
    _, rows, _ = grads4[0].shape
    tr = _row_tile(rows, 16)

    def body(c_ref, *refs):
        for k in range(n):
            total = refs[k][...] + refs[n + k][...]
            refs[2 * n + k][...] = total
            refs[3 * n + k][...] = total.astype(BF16)

    out = pl.BlockSpec((None, tr, HALF), lambda s, i, c_ref: (s, i, 0))
    mine = pl.BlockSpec((None, tr, HALF), lambda s, i, c_ref: (s, i, c_ref[0]))
    outs = pl.pallas_call(
        body,
        name="add_halves",
        grid_spec=pltpu.PrefetchScalarGridSpec(
            num_scalar_prefetch=1,
            grid=(N_SHARDS, rows // tr),
            in_specs=[mine] * n + [out] * n,
            out_specs=[out] * (2 * n),
        ),
        out_shape=[pltpu.HBM((N_SHARDS, rows, HALF), F32)] * n + [pltpu.HBM((N_SHARDS, rows, HALF), BF16)] * n,
        compiler_params=_params(48, ("parallel", "parallel")),
    )(c, *[_in_hbm(a) for a in list(grads4) + list(recvs)])
    return list(zip(outs[:n], outs[n:]))


def _add_partials(part4, recv3, shard_core):
    _, rows, _ = part4.shape
    tr = _row_tile(rows, 16)

    def body(sc_ref, p_ref, r_ref, o_ref):
        o_ref[...] = ((p_ref[...] + r_ref[0].astype(F32)) + r_ref[1].astype(F32)) + r_ref[2].astype(F32)

    return pl.pallas_call(
        body,
        name="add_partials",
        grid_spec=pltpu.PrefetchScalarGridSpec(
            num_scalar_prefetch=1,
            grid=(rows // tr,),
            in_specs=[
                pl.BlockSpec((None, tr, HALF), lambda i, sc_ref: (sc_ref[0], i, 0)),
                pl.BlockSpec((3, tr, HALF), lambda i, sc_ref: (0, i, 0)),
            ],
            out_specs=pl.BlockSpec((tr, HALF), lambda i, sc_ref: (i, sc_ref[1])),
        ),
        out_shape=pltpu.HBM((rows, 2 * HALF), F32),
        compiler_params=_params(32, ("parallel",)),
    )(shard_core, _in_hbm(part4), _in_hbm(recv3))


def _adam_math(w, g, m, v):
    m = ADAM_B1 * m + (1.0 - ADAM_B1) * g
    v = ADAM_B2 * v + (1.0 - ADAM_B2) * (g * g)
    m_hat = m / (1.0 - ADAM_B1**ADAM_STEP)
    v_hat = v / (1.0 - ADAM_B2**ADAM_STEP)
    delta = -ADAM_LR * (m_hat / (jnp.sqrt(v_hat) + ADAM_EPS) + ADAM_WD * w)
    return delta, m, v


def _adamw(w, g, m, v):
    rows, cols = w.shape
    tr = _row_tile(rows)

    def body(w_ref, g_ref, m_ref, v_ref, go_ref, d_ref, mo_ref, vo_ref):
        gv = g_ref[...]
        go_ref[...] = gv
        d_ref[...], mo_ref[...], vo_ref[...] = _adam_math(w_ref[...], gv, m_ref[...], v_ref[...])

    spec = pl.BlockSpec((tr, cols), lambda i: (i, 0))
    return pl.pallas_call(
        body, name="adamw", grid=(rows // tr,), in_specs=[spec] * 4, out_specs=[spec] * 4, out_shape=[pltpu.HBM(w.shape, F32)] * 4,
        compiler_params=_params(32, ("parallel",)),
    )(_in_hbm(w), _in_hbm(g), _in_hbm(m), _in_hbm(v))


SMALL_ROWS = 560
DECAY_ROWS = 8
SMALL_TOTAL = SMALL_ROWS + 2 * N_SHARDS * DECAY_ROWS


def _adamw_small(gathered, wp, mp, vp):
    out_rows = SMALL_ROWS + 2 * DECAY_ROWS

    def body(ga_ref, w_ref, m_ref, v_ref, g_ref, d_ref, mo_ref, vo_ref):
        shard = 2 * lax.axis_index("x") + lax.axis_index("y")
        g_ref[pl.ds(0, SMALL_ROWS), :] = functools.reduce(lambda a, b: a + b, [ga_ref[d, pl.ds(0, SMALL_ROWS), :] for d in range(8)])
        for k in range(2):
            start = pl.multiple_of(SMALL_ROWS + k * N_SHARDS * DECAY_ROWS + shard * DECAY_ROWS, DECAY_ROWS)
            g_ref[pl.ds(SMALL_ROWS + k * DECAY_ROWS, DECAY_ROWS), :] = functools.reduce(
                lambda a, b: a + b, [ga_ref[d, pl.ds(start, DECAY_ROWS), :] for d in range(8)])
        d_ref[...], mo_ref[...], vo_ref[...] = _adam_math(w_ref[...], g_ref[...], m_ref[...], v_ref[...])

    shape = jax.ShapeDtypeStruct((out_rows, LANES), F32)
    return pl.pallas_call(body, name="adamw_small", out_shape=[shape] * 4, compiler_params=_params(32, None))(gathered, wp, mp, vp)


ANY = pl.BlockSpec(memory_space=pl.ANY)


def _position():
    return lax.axis_index("x"), lax.axis_index("y"), lax.axis_index("c")


def _other_chips(x, y):
    return [(1 - x, y), (x, 1 - y), (1 - x, 1 - y)]


HBM = pl.BlockSpec(memory_space=pltpu.HBM)
SEM = pl.BlockSpec(memory_space=pltpu.SEMAPHORE)
TOKEN = jax.ShapeDtypeStruct(TOKEN_SHAPE, F32)
DATAFLOW = pltpu.SideEffectType.DATAFLOW_SIDE_EFFECTING


def _half_block(ref4, slot, core):
    return ref4.at[slot, :, pl.ds(pl.multiple_of(core * HALF, HALF), HALF)]


def _gather_ici_copies(bufs, lands, send_sems, recv_sems):
    x, y, c = _position()
    pairs = []
    for k, ref4 in enumerate(bufs):
        mine = _half_block(ref4, 2 * x + y, c)
        for j, (px, py) in enumerate(_other_chips(x, y)):
            sems = dict(send_sem=send_sems.at[3 * k + j], recv_sem=recv_sems.at[3 * k + j], device_id=(px, py, c), device_id_type=MESH)
            pairs.append((functools.partial(pltpu.make_async_remote_copy, src_ref=mine, dst_ref=mine, **sems),
                          functools.partial(pltpu.make_async_remote_copy, src_ref=mine, dst_ref=_half_block(ref4, 2 * px + py, c), **sems)))
    return pairs


def _gather_d2d_copies(bufs, lands, send_sems, recv_sems):
    x, y, c = _position()
    pairs = []
    for k, ref4 in enumerate(bufs):
        for j, (px, py) in enumerate(_other_chips(x, y)):
            have = _half_block(ref4, 2 * px + py, c)
            sems = dict(send_sem=send_sems.at[3 * k + j], recv_sem=recv_sems.at[3 * k + j], device_id=(x, y, 1 - c), device_id_type=MESH)
            pairs.append((functools.partial(pltpu.make_async_remote_copy, src_ref=have, dst_ref=have, **sems),
                          functools.partial(pltpu.make_async_remote_copy, src_ref=have, dst_ref=_half_block(ref4, 2 * px + py, 1 - c), **sems)))
    return pairs


def _gather_forward(bufs):
    n = len(bufs)

    def body(*refs):
        outs = refs[n : 2 * n]
        send_sems, recv_sems = refs[2 * n :]
        d2d = _gather_d2d_copies(outs, (), send_sems, recv_sems)
        for forward, _ in d2d:
            forward().start()
        for forward, arrival in d2d:
            arrival().wait_recv()
            forward().wait_send()

    return pl.pallas_call(
        body,
        name="gather_forward",
        in_specs=[ANY] * n,
        out_specs=[ANY] * n,
        out_shape=[jax.ShapeDtypeStruct(b.shape, b.dtype) for b in bufs],
        input_output_aliases={k: k for k in range(n)},
        scratch_shapes=[pltpu.SemaphoreType.DMA((3 * n,)), pltpu.SemaphoreType.DMA((3 * n,))],
        compiler_params=pltpu.CompilerParams(has_side_effects=True),
    )(*bufs)


def _exchange_halves(grads4):
    n = len(grads4)

    def body(*refs):
        ins, outs = refs[:n], refs[n : 2 * n]
        send_sems, recv_sems = refs[2 * n :]
        x, y, c = _position()
        copies = []
        for k in range(n):
            cp = pltpu.make_async_remote_copy(
                src_ref=ins[k].at[:, :, pl.ds(pl.multiple_of((1 - c) * HALF, HALF), HALF)], dst_ref=outs[k],
                send_sem=send_sems.at[k], recv_sem=recv_sems.at[k], device_id=(x, y, 1 - c), device_id_type=MESH)
            cp.start()
            copies.append(cp)
        for cp in copies:
            cp.wait()

    return pl.pallas_call(
        body,
        name="exchange_halves",
        in_specs=[ANY] * n,
        out_specs=[ANY] * n,
        out_shape=[jax.ShapeDtypeStruct((N_SHARDS, g.shape[1], HALF), g.dtype) for g in grads4],
        scratch_shapes=[pltpu.SemaphoreType.DMA((n,)), pltpu.SemaphoreType.DMA((n,))],
        compiler_params=pltpu.CompilerParams(has_side_effects=True),
    )(*grads4)


def _both_ends(**copy):
    maker = functools.partial(pltpu.make_async_remote_copy, **copy)
    return maker, maker


def _scatter_copies(parts, lands, send_sems, recv_sems):
    x, y, c = _position()
    return [_both_ends(src_ref=parts[k].at[2 * px + py], dst_ref=lands[k].at[j], send_sem=send_sems.at[3 * k + j],
                       recv_sem=recv_sems.at[3 * k + j], device_id=(px, py, c), device_id_type=MESH)
            for k in range(len(parts)) for j, (px, py) in enumerate(_other_chips(x, y))]


def _exchange_copies(grads, lands, send_sems, recv_sems):
    x, y, c = _position()
    return [_both_ends(src_ref=grads[k].at[:, :, pl.ds(pl.multiple_of((1 - c) * HALF, HALF), HALF)], dst_ref=lands[k],
                       send_sem=send_sems.at[k], recv_sem=recv_sems.at[k], device_id=(x, y, 1 - c), device_id_type=MESH)
            for k in range(len(grads))]


def _exchange_lands(grads4):
    return [jax.ShapeDtypeStruct((N_SHARDS, g.shape[1], HALF), g.dtype) for g in grads4]


def _scatter_lands(parts4):
    return [jax.ShapeDtypeStruct((3,) + g.shape[1:], g.dtype) for g in parts4]


def _split_start(name, srcs, land_shapes, make_copies, nsem, after=()):
    n, nl, na = len(srcs), len(land_shapes), len(after)
    lands = [lax.empty(a.shape, a.dtype) for a in land_shapes]

    def body(*refs):
        send_sems, recv_sems = refs[n + nl + na], refs[n + nl + na + 1]
        token = refs[2 * (n + nl) + na + 2]
        for send, _ in make_copies(refs[:n], refs[n : n + nl], send_sems, recv_sems):
            send().start()
        token[...] = jnp.zeros_like(token)

    hbm = lambda a: pltpu.HBM(a.shape, a.dtype)
    out = pl.pallas_call(
        body,
        name=name,
        in_specs=[HBM] * (n + nl) + [ANY] * na,
        out_specs=(SEM, SEM, *[HBM] * (n + nl), pl.BlockSpec(memory_space=pltpu.VMEM)),
        out_shape=(pltpu.SemaphoreType.DMA((nsem,)), pltpu.SemaphoreType.DMA((nsem,)), *[hbm(a) for a in list(srcs) + lands], TOKEN),
        input_output_aliases={k: 2 + k for k in range(n + nl)},
        compiler_params=pltpu.CompilerParams(has_side_effects=DATAFLOW),
    )(*[pltpu.with_memory_space_constraint(a, pltpu.HBM) for a in list(srcs) + lands], *after)
    return out[0], out[1], list(out[2 : 2 + n]), list(out[2 + n : 2 + n + nl]), out[2 + n + nl]


def _split_wait(name, send_sems, recv_sems, srcs, lands, make_copies, after):
    n, nl = len(srcs), len(lands)

    def body(*refs):
        for send, arrival in make_copies(refs[:n], refs[n : n + nl], refs[n + nl], refs[n + nl + 1]):
            send().wait_send()
            arrival().wait_recv()

    hbm = lambda a: pltpu.HBM(a.shape, a.dtype)
    out = pl.pallas_call(
        body,
        name=name,
        in_specs=[HBM] * (n + nl) + [SEM, SEM] + [ANY] * len(after),
        out_specs=tuple([HBM] * (n + nl)),
        out_shape=tuple(hbm(a) for a in list(srcs) + list(lands)),
        input_output_aliases={k: k for k in range(n + nl)},
        compiler_params=pltpu.CompilerParams(has_side_effects=DATAFLOW),
    )(*srcs, *lands, send_sems, recv_sems, *after)
    return list(out[:n]), list(out[n:])


def _join_halves(bufs):
    n = len(bufs)

    def body(*refs):
        outs = refs[n : 2 * n]
        send_sems, recv_sems = refs[2 * n :]
        x, y, c = _position()
        half = lambda ref, core: ref.at[:, pl.ds(pl.multiple_of(core * HALF, HALF), HALF)]
        for k in range(n):
            mine = half(outs[k], c)
            pltpu.make_async_remote_copy(
                src_ref=mine, dst_ref=mine, send_sem=send_sems.at[k], recv_sem=recv_sems.at[k],
                device_id=(x, y, 1 - c), device_id_type=MESH).start()
        for k in range(n):
            wait = pltpu.make_async_remote_copy(
                src_ref=half(outs[k], c), dst_ref=half(outs[k], 1 - c), send_sem=send_sems.at[k], recv_sem=recv_sems.at[k],
                device_id=(x, y, 1 - c), device_id_type=MESH)
            wait.wait_send()
            wait.wait_recv()

    return pl.pallas_call(
        body,
        name="join_halves",
        in_specs=[ANY] * n,
        out_specs=[ANY] * n,
        out_shape=[jax.ShapeDtypeStruct(b.shape, b.dtype) for b in bufs],
        input_output_aliases={k: k for k in range(n)},
        scratch_shapes=[pltpu.SemaphoreType.DMA((n,)), pltpu.SemaphoreType.DMA((n,))],
        compiler_params=pltpu.CompilerParams(has_side_effects=True),
    )(*bufs)


def _allgather_small(block):
    m_per, ncol = block.shape

    def body(x_ref, out_ref, send_sems, recv_sems, local_sem):
        x, y, c = _position()
        me, sibling = (x, y, c), (x, y, 1 - c)
        chips = _other_chips(x, y)

        def rows(px, py, pc):
            return out_ref.at[4 * px + 2 * py + pc]

        def copy(k, blk, to, src=None):
            return pltpu.make_async_remote_copy(
                src_ref=rows(*blk) if src is None else src, dst_ref=rows(*blk),
                send_sem=send_sems.at[k], recv_sem=recv_sems.at[k], device_id=to, device_id_type=MESH)

        mine = pltpu.make_async_copy(x_ref, rows(*me), local_sem)
        mine.start()
        first = [copy(0, me, sibling, src=x_ref)] + [copy(1 + j, me, (*chip, c), src=x_ref) for j, chip in enumerate(chips)]
        for cp in first:
            cp.start()
        passed = [copy(4 + j, (*chip, c), sibling) for j, chip in enumerate(chips)]
        for j, chip in enumerate(chips):
            copy(1 + j, (*chip, c), me).wait_recv()
            passed[j].start()
        copy(0, sibling, me).wait_recv()
        for j, chip in enumerate(chips):
            copy(4 + j, (*chip, 1 - c), me).wait_recv()
        for cp in first + passed:
            cp.wait_send()
        mine.wait()

    return pl.pallas_call(
        body,
        name="allgather_small",
        in_specs=[pl.BlockSpec(memory_space=pltpu.VMEM)],
        out_specs=pl.BlockSpec(memory_space=pltpu.VMEM),
        out_shape=jax.ShapeDtypeStruct((8, m_per, ncol), block.dtype),
        scratch_shapes=[pltpu.SemaphoreType.DMA((7,)), pltpu.SemaphoreType.DMA((7,)), pltpu.SemaphoreType.DMA],
        compiler_params=pltpu.CompilerParams(has_side_effects=True, vmem_limit_bytes=32 * MIB),
    )(block)


SMALL_NAMES = ["norm1_g", "b_decay_f", "b_decay_b", "gla_norm_g", "gmlp_ln_g", "gmlp_ln_b", "w_spatial", "b_spatial", "norm2_g", "final_norm_g"]


def _pack_small(parts, decay_parts):
    flat = jnp.concatenate([a.reshape(-1) for a in parts])
    flat = jnp.pad(flat, (0, SMALL_ROWS * LANES - flat.shape[0])).reshape(SMALL_ROWS, LANES)
    return jnp.concatenate([flat] + [d.reshape(-1, LANES) for d in decay_parts], axis=0)


def _unpack_small(packed, like):
    out, off = [], 0
    flat = packed[:SMALL_ROWS].reshape(-1)
    for a in like:
        out.append(flat[off : off + a.size].reshape(a.shape))
        off += a.size
    return out


def kernel(x, norm1_g, w_in, w_decay_f, b_decay_f, w_decay_b, b_decay_b, gla_norm_g, gmlp_ln_g, gmlp_ln_b, w_spatial, b_spatial, w_out, norm2_g, w_gate, w_up, w_down, final_norm_g, loss_target, m_norm1_g, m_w_in, m_w_decay_f, m_b_decay_f, m_w_decay_b, m_b_decay_b, m_gla_norm_g, m_gmlp_ln_g, m_gmlp_ln_b, m_w_spatial, m_b_spatial, m_w_out, m_norm2_g, m_w_gate, m_w_up, m_w_down, m_final_norm_g, v_norm1_g, v_w_in, v_w_decay_f, v_b_decay_f, v_w_decay_b, v_b_decay_b, v_gla_norm_g, v_gmlp_ln_g, v_gmlp_ln_b, v_w_spatial, v_b_spatial, v_w_out, v_norm2_g, v_w_gate, v_w_up, v_w_down, v_final_norm_g):
    args = dict(locals())
    cx, cy, cc = lax.axis_index("x"), lax.axis_index("y"), lax.axis_index("c")
    shard = 2 * cx + cy
    xs = x[0]
    target = loss_target[0]

    big_names = ["w_in", "w_out", "w_gate", "w_up", "w_down"]
    transposed = ("w_in", "w_gate", "w_up")
    rows_of = lambda pre, k: jnp.transpose(args[pre + k][0]) if k in transposed else args[pre + k][0]
    big_shards = {k: rows_of("", k) for k in big_names}
    c_arr = cc.reshape(1).astype(jnp.int32)
    s_arr = shard.reshape(1).astype(jnp.int32)
    sc_arr = jnp.stack([shard, cc]).astype(jnp.int32)
    zero_token = jnp.zeros(TOKEN_SHAPE, F32)
    w_send, w_recv, (w_in4,), _, token_w_in = _split_start(
        "w_in_gather_start", [_cast_into_slot(big_shards["w_in"], s_arr, zero_token)], [], _gather_ici_copies, 3)
    late = ["w_out", "w_gate", "w_up", "w_down"]
    late_slots = [_cast_into_slot(big_shards[k], s_arr, token_w_in) for k in late]
    dec_block = jnp.concatenate([w_decay_f[0].reshape(-1, LANES), w_decay_b[0].reshape(-1, LANES)], axis=0)
    dec_all = _allgather_small(dec_block)
    (w_in4,), _ = _split_wait("w_in_gather_wait", w_send, w_recv, [w_in4], [], _gather_ici_copies, (dec_all, *late_slots))
    (w_in4,) = _gather_forward([w_in4])
    w_in_t = w_in4.reshape(PROJ_W, D_MODEL)
    g_send, g_recv, late_bufs, _, token_gather = _split_start(
        "gather_start", late_slots, [], _gather_ici_copies, 3 * len(late), after=(w_in4,))
    dec_all = dec_all[::2].reshape(N_SHARDS, 2, LOWRANK, KEY_W // N_SHARDS)
    wdf_full = jnp.transpose(dec_all[:, 0], (1, 0, 2)).reshape(LOWRANK, KEY_W)
    wdb_full = jnp.transpose(dec_all[:, 1], (1, 0, 2)).reshape(LOWRANK, KEY_W)
    wd_pad_f = jnp.zeros((LANES, KEY_W), F32).at[0:LOWRANK].set(wdf_full).astype(BF16)
    wd_pad_b = jnp.zeros((LANES, KEY_W), F32).at[LOWRANK : 2 * LOWRANK].set(wdb_full).astype(BF16)

    ws_bf = w_spatial[0].astype(BF16)
    wst_bf = jnp.transpose(w_spatial[0], (0, 2, 1)).astype(BF16)
    bs_col = b_spatial[0].reshape(GMLP_GROUPS, GMLP_CHUNK, 1)

    p = _inproj(xs, norm1_g, w_in_t, token_gather)
    o_f, st_f = _gla_fwd(p, wd_pad_f, b_decay_f, token_gather, reverse=False)
    late_bufs, _ = _split_wait("gather_wait", g_send, g_recv, late_bufs, [], _gather_ici_copies, (o_f,))
    f_send, f_recv, late_bufs, _, token_forward = _split_start("forward_start", late_bufs, [], _gather_d2d_copies, 3 * len(late))
    o_b, st_b = _gla_fwd(p, wd_pad_b, b_decay_b, token_forward, reverse=True)
    late_bufs, _ = _split_wait("forward_wait", f_send, f_recv, late_bufs, [], _gather_d2d_copies, (o_b,))
    w_out_full, wg_t, wu_t, wd = [b.reshape(-1, D_MODEL) for b in late_bufs]
    x1, ycat = _mixer_out(xs, o_f, o_b, p, gla_norm_g, gmlp_ln_g, gmlp_ln_b, ws_bf, bs_col, w_out_full)
    gf = final_norm_g.reshape(1, D_MODEL)
    h2, gate, up, act, dx2, loss_acc, dgf = _ffn_fwd(x1, target, norm2_g, gf, wg_t, wu_t, wd)

    dgate, dup, dx1, dg2 = _ffn_bwd(dx2, gate, up, x1, norm2_g, wg_t, wu_t, wd)
    ffn_grads4 = [g.reshape(N_SHARDS, FF_SHARD, D_MODEL) for g in _ffn_wgrad(h2, dgate, dup, act, dx2)]
    e_send, e_recv, e_srcs, e_lands, token_exchange = _split_start(
        "exchange_start", ffn_grads4, _exchange_lands(ffn_grads4), _exchange_copies, len(ffn_grads4))
    do, dg, du, dvv, dwo, dgn, dlng, dlnb, dws, dbs = _mixer_bwd(
        dx1, ycat, o_f, o_b, p, gla_norm_g, gmlp_ln_g, gmlp_ln_b, ws_bf, wst_bf, bs_col, w_out_full, token_exchange)
    ffn_mine, ffn_other = _split_wait("exchange_wait", e_send, e_recv, e_srcs, e_lands, _exchange_copies, (do,))
    ffn_parts = _add_halves(ffn_mine, ffn_other, c_arr)
    ffn_payload = [pb for _, pb in ffn_parts]
    s_send, s_recv, s_parts, s_lands, token_scatter = _split_start(
        "scatter_start", ffn_payload, _scatter_lands(ffn_payload), _scatter_copies, 3 * len(ffn_payload))
    dq_f, dk_f, dv_f, dlr_f, dwdec_f, dbdec_f = _gla_bwd(p, do, st_f, wd_pad_f, b_decay_f, token_scatter, reverse=False)
    dq, dk, dv, dlr, dwdec_b, dbdec_b = _gla_bwd(
        p, do, st_b, wd_pad_b, b_decay_b, token_scatter, reverse=True, other=(dq_f, dk_f, dv_f, dlr_f))
    dwin_t, dp = _inproj_wgrad(xs, norm1_g, dq, dk, dv, dg, du, dvv, dlr)
    _, ffn_recv = _split_wait("scatter_wait", s_send, s_recv, s_parts, s_lands, _scatter_copies, (dwin_t,))

    dwin4 = dwin_t.reshape(N_SHARDS, PROJ_W // N_SHARDS, D_MODEL)
    dwo4 = dwo.reshape(N_SHARDS, D_MODEL // N_SHARDS, D_MODEL)
    proj_grads4 = [dwin4, dwo4]
    proj_parts = [_add_halves([g], [r], c_arr)[0] for g, r in zip(proj_grads4, _exchange_halves(proj_grads4))]
    proj_payload = [pb for _, pb in proj_parts]
    p_send, p_recv, p_parts, p_lands, token_proj = _split_start(
        "proj_scatter_start", proj_payload, _scatter_lands(proj_payload), _scatter_copies, 3 * len(proj_payload))
    dx, dg1 = _inproj_dx(xs, dx1, norm1_g, w_in_t, dp, token_proj)
    _, proj_recv = _split_wait("proj_scatter_wait", p_send, p_recv, p_parts, p_lands, _scatter_copies, (dx,))
    parts_f32 = [pf for pf, _ in proj_parts + ffn_parts]
    bufs = [_add_partials(pf, r, sc_arr) for pf, r in zip(parts_f32, proj_recv + ffn_recv)]
    big_grads = dict(zip(big_names, _join_halves(bufs)))

    dwdec_f16 = dwdec_f[0:LOWRANK]
    dwdec_b16 = dwdec_b[LOWRANK : 2 * LOWRANK]
    shard_major = lambda a: jnp.transpose(a.reshape(LOWRANK, N_SHARDS, KEY_W // N_SHARDS), (1, 0, 2))
    small_grads = {
        "norm1_g": dg1, "b_decay_f": dbdec_f, "b_decay_b": dbdec_b, "gla_norm_g": dgn, "gmlp_ln_g": dlng, "gmlp_ln_b": dlnb,
        "w_spatial": dws, "b_spatial": dbs, "norm2_g": dg2, "final_norm_g": dgf,
    }
    g_pack = _pack_small([small_grads[k] for k in SMALL_NAMES] + [loss_acc], [shard_major(dwdec_f16), shard_major(dwdec_b16)])
    g_all = _allgather_small(g_pack)
    pack_own = lambda pre: _pack_small([args[pre + k] for k in SMALL_NAMES], [args[pre + "w_decay_f"], args[pre + "w_decay_b"]])
    sg, sd, sm, sv = _adamw_small(g_all, pack_own(""), pack_own("m_"), pack_own("v_"))

    names = ["norm1_g", "w_in", "w_decay_f", "b_decay_f", "w_decay_b", "b_decay_b", "gla_norm_g", "gmlp_ln_g", "gmlp_ln_b",
             "w_spatial", "b_spatial", "w_out", "norm2_g", "w_gate", "w_up", "w_down", "final_norm_g"]
    like = [args[k] for k in SMALL_NAMES]
    results = {"g": {}, "d": {}, "m": {}, "v": {}}
    for tag, packed in (("g", sg), ("d", sd), ("m", sm), ("v", sv)):
        for k, a in zip(SMALL_NAMES, _unpack_small(packed, like)):
            results[tag][k] = a
        results[tag]["w_decay_f"] = packed[SMALL_ROWS : SMALL_ROWS + DECAY_ROWS].reshape(w_decay_f.shape)
        results[tag]["w_decay_b"] = packed[SMALL_ROWS + DECAY_ROWS :].reshape(w_decay_b.shape)
    for k in big_names:
        g, d, mo, vo = _adamw(big_shards[k], big_grads[k], rows_of("m_", k), rows_of("v_", k))
        for tag, a in (("g", g), ("d", d), ("m", mo), ("v", vo)):
            results[tag][k] = (jnp.transpose(a) if k in transposed else a).reshape(args[k].shape)

    loss = sg[:SMALL_ROWS].reshape(-1)[sum(a.size for a in like)]
    grad_x = dx.reshape(x.shape)
    return (loss, grad_x, *[results["g"][k] for k in names], *[results["d"][k] for k in names],
            *[results["m"][k] for k in names], *[results["v"][k] for k in names])
```

```python
import functools
import math

import jax
import jax.numpy as jnp
from jax import lax
from jax.experimental import pallas as pl
from jax.experimental.pallas import tpu as pltpu

F32, BF16 = jnp.float32, jnp.bfloat16

D_MODEL = 1024
GLA_HEADS = 4
GLA_DK = 64
GLA_DV = 128
KEY_W = GLA_HEADS * GLA_DK
GLA_W = GLA_HEADS * GLA_DV
GMLP_W = 512
GMLP_GROUPS = 4
GMLP_CHUNK = 128
LOWRANK = 16
GLA_CHUNK = 64
GLA_TAU = 16.0
PROJ_W = 2592
PROJ_WP = 2688
D_FF = 2816
N_SHARDS = 4
FF_SHARD = D_FF // N_SHARDS
EPS = 1e-6
LANES = 128
TOKEN_SHAPE = (8, LANES)
MIB = 1024 * 1024

ADAM_LR = 0.001
ADAM_B1 = 0.9
ADAM_B2 = 0.999
ADAM_EPS = 1e-08
ADAM_WD = 0.01
ADAM_STEP = 10

COL_Q, COL_K = 0, 256
COL_V, COL_G, COL_U, COL_VV = 512, 1024, 1536, 2048
COL_LR = 2560
ROW_LR, ROW_UV = 1536, 1568
HALF = D_MODEL // 2

MESH = pl.DeviceIdType.MESH


def _nn(a, b):
    return jnp.dot(a, b, preferred_element_type=F32)


def _nt(a, b):
    return lax.dot_general(a, b, (((1,), (1,)), ((), ())), preferred_element_type=F32)


def _tn(a, b):
    return lax.dot_general(a, b, (((0,), (0,)), ((), ())), preferred_element_type=F32)


def _bnn(a, b):
    return jnp.einsum("nik,nkj->nij", a, b, preferred_element_type=F32)


def _bnt(a, b):
    return jnp.einsum("nik,njk->nij", a, b, preferred_element_type=F32)


def _btn(a, b):
    return jnp.einsum("nki,nkj->nij", a, b, preferred_element_type=F32)


def _resident(shape):
    zeros = (0,) * len(shape)
    return pl.BlockSpec(shape, lambda *_: zeros, pipeline_mode=pl.Buffered(1))


def _params(vmem_mib, semantics=("arbitrary",)):
    return pltpu.CompilerParams(vmem_limit_bytes=vmem_mib * MIB, dimension_semantics=semantics)


def _sigmoid(x):
    return 1.0 / (1.0 + jnp.exp(-x))


def _gelu(x):
    return 0.5 * x * (1.0 + lax.erf(x * (1.0 / math.sqrt(2.0))))


def _gelu_and_grad(x):
    cdf = 0.5 * (1.0 + lax.erf(x * (1.0 / math.sqrt(2.0))))
    return x * cdf, cdf + x * jnp.exp(-0.5 * x * x) * (1.0 / math.sqrt(2.0 * math.pi))


def _log_sigmoid(x):
    return jnp.minimum(x, 0.0) - jnp.log(1.0 + jnp.exp(-jnp.abs(x)))


def _rms_bwd(dxh, xh, r):
    return r * (dxh - xh * jnp.mean(dxh * xh, axis=-1, keepdims=True))


def _chunk_cumsum(v, row_in_chunk, reverse):
    rows = v.shape[0]
    for sh in (1, 2, 4, 8, 16, 32):
        if reverse:
            v = v + jnp.where(row_in_chunk + sh < GLA_CHUNK, pltpu.roll(v, rows - sh, axis=0), 0.0)
        else:
            v = v + jnp.where(row_in_chunk >= sh, pltpu.roll(v, sh, axis=0), 0.0)
    return v


def _inproj(x, g1, w_in_t, token):
    seq = x.shape[0]
    tm = min(seq, 512)

    def body(x_ref, g_ref, w_ref, token_ref, p_ref):
        xv = x_ref[...]
        r = lax.rsqrt(jnp.mean(xv * xv, axis=-1, keepdims=True) + EPS)
        h = (xv * r * g_ref[...]).astype(BF16)
        p_ref[:, 0:COL_U] = _nt(h, w_ref[0:ROW_LR, :])
        p_ref[:, COL_U:COL_LR] = _nt(h, w_ref[ROW_UV:PROJ_W, :])
        p_ref[:, COL_LR:PROJ_WP] = _nt(h, w_ref[ROW_LR : ROW_LR + LANES, :])

    return pl.pallas_call(
        body,
        name="inproj",
        grid=(seq // tm,),
        in_specs=[pl.BlockSpec((tm, D_MODEL), lambda i: (i, 0)), _resident((1, D_MODEL)), _resident((PROJ_W, D_MODEL)), _resident(TOKEN_SHAPE)],
        out_specs=pl.BlockSpec((tm, PROJ_WP), lambda i: (i, 0)),
        out_shape=jax.ShapeDtypeStruct((seq, PROJ_WP), F32),
        compiler_params=_params(48, ("parallel",)),
    )(x, g1, w_in_t, token)


def _gla_tile(seq):
    return min(seq, 1024)


def _gla_decay_terms(lr_bf, wd_ref, bd_ref, pair, row_in_chunk, reverse, n):
    cols = pl.ds(pair * LANES, LANES)
    pre = _nn(lr_bf, wd_ref[:, cols]) + bd_ref[:, cols]
    la = _log_sigmoid(pre) * (1.0 / GLA_TAU)
    b = _chunk_cumsum(la, row_in_chunk, reverse)
    b3 = b.reshape(n, GLA_CHUNK, LANES)
    blast = b3[:, 0:1, :] if reverse else b3[:, GLA_CHUNK - 1 : GLA_CHUNK, :]
    return pre, b3, blast


def _gla_fwd(p, wd_pad, bd, token, reverse):
    seq = p.shape[0]
    tg = _gla_tile(seq)
    nt = seq // tg
    n = tg // GLA_CHUNK
    scale = GLA_DK**-0.5

    def tile(i):
        return nt - 1 - i if reverse else i

    def body(q_ref, k_ref, v_ref, lr_ref, wd_ref, bd_ref, token_ref, o_ref, st_ref, carry):
        @pl.when(pl.program_id(0) == 0)
        def _():
            carry[...] = jnp.zeros_like(carry)

        lr_bf = lr_ref[...].astype(BF16)
        states = [carry[h] for h in range(GLA_HEADS)]
        row_in_chunk = lax.broadcasted_iota(jnp.int32, (tg, LANES), 0) % GLA_CHUNK
        lane_head = lax.broadcasted_iota(jnp.int32, (1, LANES), 1) // GLA_DK
        tt = lax.broadcasted_iota(jnp.int32, (GLA_CHUNK, GLA_CHUNK), 0)
        ss = lax.broadcasted_iota(jnp.int32, (GLA_CHUNK, GLA_CHUNK), 1)
        causal = (tt <= ss) if reverse else (tt >= ss)
        order = range(n - 1, -1, -1) if reverse else range(n)
        heads = range(GLA_HEADS)
        qds, vhs, decs, sc_raw, dst = {}, {}, {}, {}, {}
        for pair in range(2):
            cols = pl.ds(pair * LANES, LANES)
            _, b3, blast = _gla_decay_terms(lr_bf, wd_ref, bd_ref, pair, row_in_chunk, reverse, n)
            q3 = q_ref[:, cols].reshape(n, GLA_CHUNK, LANES) * scale
            k3 = k_ref[:, cols].reshape(n, GLA_CHUNK, LANES)
            qd = q3 * jnp.exp(b3)
            kd = (k3 * jnp.exp(-b3)).astype(BF16)
            kte = k3 * jnp.exp(blast - b3)
            decs[pair] = jnp.exp(blast)
            qds[pair] = qd.astype(BF16)
            m0 = (lane_head == 0).astype(F32)
            m1 = (lane_head == 1).astype(F32)
            q_both = jnp.concatenate([(qd * m0).astype(BF16), (qd * m1).astype(BF16)], axis=1)
            sc_both = _bnt(q_both, kd)
            for hh, m in ((0, m0), (1, m1)):
                h = 2 * pair + hh
                vhs[h] = v_ref[:, pl.ds(h * GLA_DV, GLA_DV)].reshape(n, GLA_CHUNK, GLA_DV).astype(BF16)
                sc_raw[h] = sc_both[:, hh * GLA_CHUNK : (hh + 1) * GLA_CHUNK, :]
                dst[h] = _btn(vhs[h], (kte * m).astype(BF16))
        o_intra, befores = {}, {}
        for h in heads:
            o_intra[h] = _bnn(jnp.where(causal, sc_raw[h], 0.0).astype(BF16), vhs[h])
            st, before = states[h], [None] * n
            for j in order:
                before[j] = st
                st = st * decs[h // 2][j] + dst[h][j]
            states[h] = st
            befores[h] = jnp.stack(before).astype(BF16)
        outs = {}
        for pair in range(2):
            both = jnp.concatenate([befores[2 * pair], befores[2 * pair + 1]], axis=1)
            o_inter = _bnt(qds[pair], both)
            for hh in range(2):
                h = 2 * pair + hh
                outs[h] = (o_intra[h] + o_inter[:, :, hh * GLA_DV : (hh + 1) * GLA_DV]).reshape(tg, GLA_DV)
        for h in range(GLA_HEADS):
            o_ref[:, pl.ds(h * GLA_DV, GLA_DV)] = outs[h]
            st_ref[:, h] = befores[h]
            carry[h] = states[h]

    nchunks = seq // GLA_CHUNK
    return pl.pallas_call(
        body,
        name="gla_fwd_rev" if reverse else "gla_fwd",
        grid=(nt,),
        in_specs=[
            pl.BlockSpec((tg, KEY_W), lambda i: (tile(i), COL_Q // KEY_W)),
            pl.BlockSpec((tg, KEY_W), lambda i: (tile(i), COL_K // KEY_W)),
            pl.BlockSpec((tg, GLA_W), lambda i: (tile(i), COL_V // GLA_W)),
            pl.BlockSpec((tg, LANES), lambda i: (tile(i), COL_LR // LANES)),
            _resident((LANES, KEY_W)),
            _resident((1, KEY_W)),
            _resident(TOKEN_SHAPE),
        ],
        out_specs=[
            pl.BlockSpec((tg, GLA_W), lambda i: (tile(i), 0)),
            pl.BlockSpec((n, GLA_HEADS, GLA_DV, LANES), lambda i: (tile(i), 0, 0, 0)),
        ],
        out_shape=[
            jax.ShapeDtypeStruct((seq, GLA_W), F32),
            jax.ShapeDtypeStruct((nchunks, GLA_HEADS, GLA_DV, LANES), BF16),
        ],
        scratch_shapes=[pltpu.VMEM((GLA_HEADS, GLA_DV, LANES), F32)],
        compiler_params=_params(48),
    )(p, p, p, p, wd_pad, bd, token)


def _mixer_out(x, o_f, o_b, p, gn, lng, lnb, ws_bf, bs_col, w_out, token):
    seq = x.shape[0]
    tm = min(seq, 512)

    def body(x_ref, of_ref, ob_ref, g_ref, u_ref, vv_ref, gn_ref, lng_ref, lnb_ref, ws_ref, bs_ref, wo_ref, token_ref, x1_ref, yc_ref, vn_sc):
        for h in range(GLA_HEADS):
            cols = pl.ds(h * GLA_DV, GLA_DV)
            oh = of_ref[:, cols] + ob_ref[:, cols]
            on = oh * lax.rsqrt(jnp.mean(oh * oh, axis=-1, keepdims=True) + EPS)
            gh = g_ref[:, cols]
            yc_ref[:, cols] = (on * gn_ref[:, cols] * (gh * _sigmoid(gh))).astype(BF16)
        zv = _gelu(vv_ref[...])
        xc = zv - jnp.mean(zv, axis=-1, keepdims=True)
        vhat = xc * lax.rsqrt(jnp.mean(xc * xc, axis=-1, keepdims=True) + EPS)
        vn_sc[...] = (vhat * lng_ref[...] + lnb_ref[...]).astype(BF16)
        for c in range(tm // GMLP_CHUNK):
            rows = pl.ds(c * GMLP_CHUNK, GMLP_CHUNK)
            for g in range(GMLP_GROUPS):
                cols = pl.ds(g * LANES, LANES)
                s = _nn(ws_ref[g], vn_sc[rows, cols]) + bs_ref[g]
                yc_ref[rows, pl.ds(GLA_W + g * LANES, LANES)] = (_gelu(u_ref[rows, cols]) * s).astype(BF16)
        x1_ref[...] = x_ref[...] + _nn(yc_ref[...], wo_ref[...])

    row = lambda w: pl.BlockSpec((tm, w), lambda i: (i, 0))
    pcol = lambda col: pl.BlockSpec((tm, GLA_W), lambda i: (i, col // GLA_W))
    return pl.pallas_call(
        body,
        name="mixer_out",
        grid=(seq // tm,),
        in_specs=[
            row(D_MODEL), row(GLA_W), row(GLA_W), pcol(COL_G), pcol(COL_U), pcol(COL_VV),
            _resident((1, GLA_W)), _resident((1, GMLP_W)), _resident((1, GMLP_W)),
            _resident((GMLP_GROUPS, GMLP_CHUNK, GMLP_CHUNK)), _resident((GMLP_GROUPS, GMLP_CHUNK, 1)),
            _resident((D_MODEL, D_MODEL)), _resident(TOKEN_SHAPE),
        ],
        out_specs=[row(D_MODEL), row(D_MODEL)],
        out_shape=[jax.ShapeDtypeStruct((seq, D_MODEL), F32), jax.ShapeDtypeStruct((seq, D_MODEL), BF16)],
        scratch_shapes=[pltpu.VMEM((tm, GMLP_W), BF16)],
        compiler_params=_params(48, ("parallel",)),
    )(x, o_f, o_b, p, p, p, gn, lng, lnb, ws_bf, bs_col, w_out, token)


def _ffn_fwd(x1, target, g2, gf, wg_t, wu_t, wd):
    seq = x1.shape[0]
    tm = min(seq, 256)

    def body(x1_ref, t_ref, g2_ref, gf_ref, wg_ref, wu_ref, wd_ref, h2_ref, gate_ref, up_ref, act_ref, dx2_ref, loss_ref, dgf_ref):
        @pl.when(pl.program_id(0) == 0)
        def _():
            loss_ref[...] = jnp.zeros_like(loss_ref)
            dgf_ref[...] = jnp.zeros_like(dgf_ref)

        x1v = x1_ref[...]
        h2 = (x1v * lax.rsqrt(jnp.mean(x1v * x1v, axis=-1, keepdims=True) + EPS) * g2_ref[...]).astype(BF16)
        h2_ref[...] = h2
        gate = _nt(h2, wg_ref[...])
        up = _nt(h2, wu_ref[...])
        act = (gate * _sigmoid(gate) * up).astype(BF16)
        gate_ref[...] = gate
        up_ref[...] = up
        act_ref[...] = act
        x2 = x1v + _nn(act, wd_ref[...])
        rf = lax.rsqrt(jnp.mean(x2 * x2, axis=-1, keepdims=True) + EPS)
        xh = x2 * rf
        err = xh * gf_ref[...] - t_ref[...]
        loss_ref[...] += 0.5 * jnp.sum(jnp.mean(err * err, axis=-1, keepdims=True))
        dy = err * (1.0 / D_MODEL)
        dgf_ref[...] += jnp.sum(dy * xh, axis=0, keepdims=True)
        dx2_ref[...] = _rms_bwd(dy * gf_ref[...], xh, rf)

    row = lambda w: pl.BlockSpec((tm, w), lambda i: (i, 0))
    weight = _resident((D_FF, D_MODEL))
    return pl.pallas_call(
        body,
        name="ffn_fwd",
        grid=(seq // tm,),
        in_specs=[row(D_MODEL), row(D_MODEL), _resident((1, D_MODEL)), _resident((1, D_MODEL)), weight, weight, weight],
        out_specs=[row(D_MODEL), row(D_FF), row(D_FF), row(D_FF), row(D_MODEL),
                   pl.BlockSpec((1, LANES), lambda i: (0, 0)), pl.BlockSpec((1, D_MODEL), lambda i: (0, 0))],
        out_shape=[
            jax.ShapeDtypeStruct((seq, D_MODEL), BF16),
            jax.ShapeDtypeStruct((seq, D_FF), F32),
            jax.ShapeDtypeStruct((seq, D_FF), F32),
            jax.ShapeDtypeStruct((seq, D_FF), BF16),
            jax.ShapeDtypeStruct((seq, D_MODEL), F32),
            jax.ShapeDtypeStruct((1, LANES), F32),
            jax.ShapeDtypeStruct((1, D_MODEL), F32),
        ],
        compiler_params=_params(56),
    )(x1, target, g2, gf, wg_t, wu_t, wd)


def _ffn_bwd(dx2, gate, up, x1, g2, wg_t, wu_t, wd):
    seq = x1.shape[0]
    tm = min(seq, 256)

    def body(dx2_ref, gate_ref, up_ref, x1_ref, g2_ref, wg_ref, wu_ref, wd_ref, dgate_ref, dup_ref, dx1_ref, dg2_ref):
        @pl.when(pl.program_id(0) == 0)
        def _():
            dg2_ref[...] = jnp.zeros_like(dg2_ref)

        dx2v = dx2_ref[...]
        dact = _nt(dx2v.astype(BF16), wd_ref[...])
        gate = gate_ref[...]
        sg = _sigmoid(gate)
        dgate = (dact * up_ref[...] * (sg * (1.0 + gate * (1.0 - sg)))).astype(BF16)
        dup = (dact * (gate * sg)).astype(BF16)
        dgate_ref[...] = dgate
        dup_ref[...] = dup
        dh2 = _nn(dgate, wg_ref[...]) + _nn(dup, wu_ref[...])
        x1v = x1_ref[...]
        r2 = lax.rsqrt(jnp.mean(x1v * x1v, axis=-1, keepdims=True) + EPS)
        xh = x1v * r2
        dg2_ref[...] += jnp.sum(dh2 * xh, axis=0, keepdims=True)
        dx1_ref[...] = dx2v + _rms_bwd(dh2 * g2_ref[...], xh, r2)

    row = lambda w: pl.BlockSpec((tm, w), lambda i: (i, 0))
    weight = _resident((D_FF, D_MODEL))
    return pl.pallas_call(
        body,
        name="ffn_bwd",
        grid=(seq // tm,),
        in_specs=[row(D_MODEL), row(D_FF), row(D_FF), row(D_MODEL), _resident((1, D_MODEL)), weight, weight, weight],
        out_specs=[row(D_FF), row(D_FF), row(D_MODEL), pl.BlockSpec((1, D_MODEL), lambda i: (0, 0))],
        out_shape=[
            jax.ShapeDtypeStruct((seq, D_FF), BF16),
            jax.ShapeDtypeStruct((seq, D_FF), BF16),
            jax.ShapeDtypeStruct((seq, D_MODEL), F32),
            jax.ShapeDtypeStruct((1, D_MODEL), F32),
        ],
        compiler_params=_params(56),
    )(dx2, gate, up, x1, g2, wg_t, wu_t, wd)


WGRAD_ROWS = D_FF // 2


def _ffn_wgrad(h2, dgate, dup, act, dx2):
    seq = h2.shape[0]
    tm = min(seq, 512)

    def body(h2_ref, dgate_ref, dup_ref, act_ref, dx2_ref, dwg_ref, dwu_ref, dwd_ref):
        @pl.when(pl.program_id(1) == 0)
        def _():
            dwg_ref[...] = jnp.zeros_like(dwg_ref)
            dwu_ref[...] = jnp.zeros_like(dwu_ref)
            dwd_ref[...] = jnp.zeros_like(dwd_ref)

        h2v = h2_ref[...]
        dwg_ref[...] += _tn(dgate_ref[...], h2v)
        dwu_ref[...] += _tn(dup_ref[...], h2v)
        dwd_ref[...] += _tn(act_ref[...], dx2_ref[...].astype(BF16))

    ff = pl.BlockSpec((tm, WGRAD_ROWS), lambda j, i: (i, j))
    row = pl.BlockSpec((tm, D_MODEL), lambda j, i: (i, 0))
    out = pl.BlockSpec((WGRAD_ROWS, D_MODEL), lambda j, i: (j, 0))
    return pl.pallas_call(
        body,
        name="ffn_wgrad",
        grid=(D_FF // WGRAD_ROWS, seq // tm),
        in_specs=[row, ff, ff, ff, row],
        out_specs=[out, out, out],
        out_shape=[jax.ShapeDtypeStruct((D_FF, D_MODEL), F32)] * 3,
        compiler_params=_params(56, ("parallel", "arbitrary")),
    )(h2, dgate, dup, act, dx2)


def _mixer_bwd(dx1, ycat, o_f, o_b, p, gn, lng, lnb, ws_bf, wst_bf, bs_col, w_out, token):
    seq = dx1.shape[0]
    tm = min(seq, 512)
    nsteps = seq // tm

    def body(dx1_ref, yc_ref, of_ref, ob_ref, g_ref, u_ref, vv_ref, gn_ref, lng_ref, lnb_ref, ws_ref, wst_ref, bs_ref, wo_ref, token_ref,
             do_ref, dg_ref, du_ref, dvv_ref, dwo_ref, dgn_ref, dlng_ref, dlnb_ref, dws_ref, dbs_ref, vn_sc, dvn_sc, dbs_acc):
        step = pl.program_id(0)

        @pl.when(step == 0)
        def _():
            for r in (dwo_ref, dgn_ref, dlng_ref, dlnb_ref, dws_ref, dbs_acc):
                r[...] = jnp.zeros_like(r)

        dx1b = dx1_ref[...].astype(BF16)
        dyc = _nt(dx1b, wo_ref[...])
        dwo_ref[...] += _tn(yc_ref[...], dx1b)
        for h in range(GLA_HEADS):
            cols = pl.ds(h * GLA_DV, GLA_DV)
            dya = dyc[:, h * GLA_DV : (h + 1) * GLA_DV]
            oh = of_ref[:, cols] + ob_ref[:, cols]
            rn = lax.rsqrt(jnp.mean(oh * oh, axis=-1, keepdims=True) + EPS)
            on = oh * rn
            gh = g_ref[:, cols]
            sg = _sigmoid(gh)
            sil = gh * sg
            gnh = gn_ref[:, cols]
            dgn_ref[:, cols] += jnp.sum(dya * on * sil, axis=0, keepdims=True)
            dg_ref[:, cols] = (dya * on * gnh * (sg * (1.0 + gh * (1.0 - sg)))).astype(BF16)
            do_ref[:, cols] = _rms_bwd(dya * gnh * sil, on, rn)
        vv = vv_ref[...]
        zv, zv_grad = _gelu_and_grad(vv)
        xc = zv - jnp.mean(zv, axis=-1, keepdims=True)
        rstd = lax.rsqrt(jnp.mean(xc * xc, axis=-1, keepdims=True) + EPS)
        vhat = xc * rstd
        vn_sc[...] = (vhat * lng_ref[...] + lnb_ref[...]).astype(BF16)
        for c in range(tm // GMLP_CHUNK):
            rows = pl.ds(c * GMLP_CHUNK, GMLP_CHUNK)
            for g in range(GMLP_GROUPS):
                cols = pl.ds(g * LANES, LANES)
                vn = vn_sc[rows, cols]
                s = _nn(ws_ref[g], vn) + bs_ref[g]
                dyb = dyc[c * GMLP_CHUNK : (c + 1) * GMLP_CHUNK, GLA_W + g * LANES : GLA_W + (g + 1) * LANES]
                zu, zu_grad = _gelu_and_grad(u_ref[rows, cols])
                du_ref[rows, cols] = (dyb * s * zu_grad).astype(BF16)
                ds = dyb * zu
                dbs_acc[g] += ds
                dsb = ds.astype(BF16)
                dws_ref[g] += _nt(dsb, vn)
                dvn_sc[rows, cols] = _nn(wst_ref[g], dsb)
        dvn = dvn_sc[...]
        dlng_ref[...] += jnp.sum(dvn * vhat, axis=0, keepdims=True)
        dlnb_ref[...] += jnp.sum(dvn, axis=0, keepdims=True)
        dvh = dvn * lng_ref[...]
        dzv = rstd * (dvh - jnp.mean(dvh, axis=-1, keepdims=True) - vhat * jnp.mean(dvh * vhat, axis=-1, keepdims=True))
        dvv_ref[...] = (dzv * zv_grad).astype(BF16)

        @pl.when(step == nsteps - 1)
        def _():
            dbs_ref[...] = jnp.sum(dbs_acc[...], axis=-1, keepdims=True)

    row = lambda w: pl.BlockSpec((tm, w), lambda i: (i, 0))
    pcol = lambda col: pl.BlockSpec((tm, GLA_W), lambda i: (i, col // GLA_W))
    const = lambda shape: pl.BlockSpec(shape, lambda i: (0,) * len(shape))
    return pl.pallas_call(
        body,
        name="mixer_bwd",
        grid=(nsteps,),
        in_specs=[
            row(D_MODEL), row(D_MODEL), row(GLA_W), row(GLA_W), pcol(COL_G), pcol(COL_U), pcol(COL_VV),
            _resident((1, GLA_W)), _resident((1, GMLP_W)), _resident((1, GMLP_W)),
            _resident((GMLP_GROUPS, GMLP_CHUNK, GMLP_CHUNK)), _resident((GMLP_GROUPS, GMLP_CHUNK, GMLP_CHUNK)),
            _resident((GMLP_GROUPS, GMLP_CHUNK, 1)), _resident((D_MODEL, D_MODEL)), _resident(TOKEN_SHAPE),
        ],
        out_specs=[
            row(GLA_W), row(GLA_W), row(GMLP_W), row(GMLP_W), const((D_MODEL, D_MODEL)),
            const((1, GLA_W)), const((1, GMLP_W)), const((1, GMLP_W)),
            const((GMLP_GROUPS, GMLP_CHUNK, GMLP_CHUNK)), const((GMLP_GROUPS, GMLP_CHUNK, 1)),
        ],
        out_shape=[
            jax.ShapeDtypeStruct((seq, GLA_W), F32), jax.ShapeDtypeStruct((seq, GLA_W), BF16),
            jax.ShapeDtypeStruct((seq, GMLP_W), BF16), jax.ShapeDtypeStruct((seq, GMLP_W), BF16),
            jax.ShapeDtypeStruct((D_MODEL, D_MODEL), F32),
            jax.ShapeDtypeStruct((1, GLA_W), F32), jax.ShapeDtypeStruct((1, GMLP_W), F32), jax.ShapeDtypeStruct((1, GMLP_W), F32),
            jax.ShapeDtypeStruct((GMLP_GROUPS, GMLP_CHUNK, GMLP_CHUNK), F32), jax.ShapeDtypeStruct((GMLP_GROUPS, GMLP_CHUNK, 1), F32),
        ],
        scratch_shapes=[pltpu.VMEM((tm, GMLP_W), BF16), pltpu.VMEM((tm, GMLP_W), F32), pltpu.VMEM((GMLP_GROUPS, GMLP_CHUNK, GMLP_CHUNK), F32)],
        compiler_params=_params(56),
    )(dx1, ycat, o_f, o_b, p, p, p, gn, lng, lnb, ws_bf, wst_bf, bs_col, w_out, token)


def _gla_bwd(p, do, st, wd_pad, bd, token, reverse, other=None):
    seq = p.shape[0]
    tg = _gla_tile(seq)
    nt = seq // tg
    n = tg // GLA_CHUNK
    scale = GLA_DK**-0.5

    def tile(i):
        return i if reverse else nt - 1 - i

    def body(q_ref, k_ref, v_ref, lr_ref, do_ref, st_ref, wd_ref, bd_ref, token_ref, *rest):
        others, (dq_ref, dk_ref, dv_ref, dlr_ref, dwd_ref, dbd_ref, carry) = rest[:-7], rest[-7:]
        if others:
            odq_ref, odk_ref, odv_ref, odlr_ref = others

            def put(ref, idx, val, oref):
                ref[idx] = (val + oref[idx]).astype(BF16)
        else:
            odq_ref = odk_ref = odv_ref = odlr_ref = None

            def put(ref, idx, val, oref):
                ref[idx] = val

        @pl.when(pl.program_id(0) == 0)
        def _():
            carry[...] = jnp.zeros_like(carry)
            dwd_ref[...] = jnp.zeros_like(dwd_ref)
            dbd_ref[...] = jnp.zeros_like(dbd_ref)

        lr_bf = lr_ref[...].astype(BF16)
        carries = [carry[h] for h in range(GLA_HEADS)]
        row_in_chunk = lax.broadcasted_iota(jnp.int32, (tg, LANES), 0) % GLA_CHUNK
        lane_head = lax.broadcasted_iota(jnp.int32, (1, LANES), 1) // GLA_DK
        tt = lax.broadcasted_iota(jnp.int32, (GLA_CHUNK, GLA_CHUNK), 0)
        ss = lax.broadcasted_iota(jnp.int32, (GLA_CHUNK, GLA_CHUNK), 1)
        causal = (tt <= ss) if reverse else (tt >= ss)
        order = range(n) if reverse else range(n - 1, -1, -1)
        dlr = jnp.zeros((tg, LANES), F32)
        heads = range(GLA_HEADS)
        pv, masks, qdh, vhs, dohs, stbs = {}, {}, {}, {}, {}, {}
        sc_raw, dp, acc = {}, {}, {}
        for pair in range(2):
            cols = pl.ds(pair * LANES, LANES)
            pre, b3, blast = _gla_decay_terms(lr_bf, wd_ref, bd_ref, pair, row_in_chunk, reverse, n)
            q3 = q_ref[:, cols].reshape(n, GLA_CHUNK, LANES) * scale
            k3 = k_ref[:, cols].reshape(n, GLA_CHUNK, LANES)
            eb = jnp.exp(b3)
            emb = jnp.exp(-b3)
            ekte = jnp.exp(blast - b3)
            kdf = k3 * emb
            kte = k3 * ekte
            both = pl.ds(2 * pair * GLA_DV, 2 * GLA_DV)
            pv[pair] = dict(pre=pre, eb=eb, emb=emb, ekte=ekte, qd=q3 * eb, kdf=kdf, kd=kdf.astype(BF16), kte=kte, kte_bf=kte.astype(BF16),
                            dec=jnp.exp(blast), v=v_ref[:, both].reshape(n, GLA_CHUNK, 2 * GLA_DV).astype(BF16),
                            do=do_ref[:, both].reshape(n, GLA_CHUNK, 2 * GLA_DV).astype(BF16))
            for hh in range(2):
                h = 2 * pair + hh
                masks[h] = (lane_head == hh).astype(F32)
                qdh[h] = (pv[pair]["qd"] * masks[h]).astype(BF16)
                vhs[h] = pv[pair]["v"][:, :, hh * GLA_DV : (hh + 1) * GLA_DV]
                dohs[h] = pv[pair]["do"][:, :, hh * GLA_DV : (hh + 1) * GLA_DV]
                stbs[h] = st_ref[:, h]
                dp[h] = _bnt(dohs[h], vhs[h])
                acc[h] = _btn(dohs[h], qdh[h])
            sc_both = _bnt(jnp.concatenate([qdh[2 * pair], qdh[2 * pair + 1]], axis=1), pv[pair]["kd"])
            for hh in range(2):
                sc_raw[2 * pair + hh] = sc_both[:, hh * GLA_CHUNK : (hh + 1) * GLA_CHUNK, :]
        dsa, sc = {}, {}
        for h in heads:
            sc[h] = jnp.where(causal, sc_raw[h], 0.0).astype(BF16)
            dp[h] = jnp.where(causal, dp[h], 0.0).astype(BF16)
            dec = pv[h // 2]["dec"]
            c, after = carries[h], [None] * n
            for j in order:
                after[j] = c
                c = acc[h][j] + dec[j] * c
            carries[h] = c
            dsa[h] = jnp.stack(after)
        dvs, dqs, dks, dwds, dbds = [], [], [], [], []
        for pair in range(2):
            cols = pl.ds(pair * LANES, LANES)
            v = pv[pair]
            h0, h1 = 2 * pair, 2 * pair + 1
            dsa_both = jnp.concatenate([dsa[h0], dsa[h1]], axis=1)
            dsa_bf = dsa_both.astype(BF16)
            stb_bf = jnp.concatenate([stbs[h0], stbs[h1]], axis=1)
            dq_intra = _bnn(jnp.concatenate([dp[h0], dp[h1]], axis=1), v["kd"])
            dqd = (dq_intra[:, :GLA_CHUNK, :] * masks[h0] + dq_intra[:, GLA_CHUNK:, :] * masks[h1]) + _bnn(v["do"], stb_bf)
            dkd = _btn(dp[h0], qdh[h0]) + _btn(dp[h1], qdh[h1])
            dkte = _bnn(v["v"], dsa_bf)
            ddec = jnp.sum(dsa[h0] * stbs[h0].astype(F32) + dsa[h1] * stbs[h1].astype(F32), axis=1, keepdims=True)
            dv_inter = _bnt(v["kte_bf"], dsa_bf)
            for hh, h in ((0, h0), (1, h1)):
                dvs.append((_btn(sc[h], dohs[h]) + dv_inter[:, :, hh * GLA_DV : (hh + 1) * GLA_DV]).reshape(tg, GLA_DV))
            dqs.append((dqd * (scale * v["eb"])).reshape(tg, LANES))
            dks.append((dkd * v["emb"] + dkte * v["ekte"]).reshape(tg, LANES))
            db = dqd * v["qd"] - dkd * v["kdf"] - dkte * v["kte"]
            dblast = jnp.sum(dkte * v["kte"], axis=1, keepdims=True) + ddec * v["dec"]
            dla = _chunk_cumsum(db.reshape(tg, LANES), row_in_chunk, not reverse) + jnp.broadcast_to(dblast, (n, GLA_CHUNK, LANES)).reshape(tg, LANES)
            dpre = (dla * (1.0 / GLA_TAU) * _sigmoid(-v["pre"]))
            dpre_bf = dpre.astype(BF16)
            dlr = dlr + _nt(dpre_bf, wd_ref[:, cols])
            dwds.append(_tn(lr_bf, dpre_bf))
            dbds.append(jnp.sum(dpre, axis=0, keepdims=True))
        put(dlr_ref, (slice(None), slice(None)), dlr, odlr_ref)
        for pair in range(2):
            cols = pl.ds(pair * LANES, LANES)
            put(dq_ref, (slice(None), cols), dqs[pair], odq_ref)
            put(dk_ref, (slice(None), cols), dks[pair], odk_ref)
            dwd_ref[:, cols] += dwds[pair]
            dbd_ref[:, cols] += dbds[pair]
        for h in range(GLA_HEADS):
            put(dv_ref, (slice(None), pl.ds(h * GLA_DV, GLA_DV)), dvs[h], odv_ref)
            carry[h] = carries[h]

    pieces = [
        pl.BlockSpec((tg, KEY_W), lambda i: (tile(i), 0)),
        pl.BlockSpec((tg, KEY_W), lambda i: (tile(i), 0)),
        pl.BlockSpec((tg, GLA_W), lambda i: (tile(i), 0)),
        pl.BlockSpec((tg, LANES), lambda i: (tile(i), 0)),
    ]
    piece_dtype = BF16 if other else F32
    return pl.pallas_call(
        body,
        name="gla_bwd_rev" if reverse else "gla_bwd",
        grid=(nt,),
        in_specs=[
            pl.BlockSpec((tg, KEY_W), lambda i: (tile(i), COL_Q // KEY_W)),
            pl.BlockSpec((tg, KEY_W), lambda i: (tile(i), COL_K // KEY_W)),
            pl.BlockSpec((tg, GLA_W), lambda i: (tile(i), COL_V // GLA_W)),
            pl.BlockSpec((tg, LANES), lambda i: (tile(i), COL_LR // LANES)),
            pl.BlockSpec((tg, GLA_W), lambda i: (tile(i), 0)),
            pl.BlockSpec((n, GLA_HEADS, GLA_DV, LANES), lambda i: (tile(i), 0, 0, 0)),
            _resident((LANES, KEY_W)),
            _resident((1, KEY_W)),
            _resident(TOKEN_SHAPE),
        ] + (pieces if other else []),
        out_specs=pieces + [pl.BlockSpec((LANES, KEY_W), lambda i: (0, 0)), pl.BlockSpec((1, KEY_W), lambda i: (0, 0))],
        out_shape=[
            jax.ShapeDtypeStruct((seq, KEY_W), piece_dtype), jax.ShapeDtypeStruct((seq, KEY_W), piece_dtype),
            jax.ShapeDtypeStruct((seq, GLA_W), piece_dtype), jax.ShapeDtypeStruct((seq, LANES), piece_dtype),
            jax.ShapeDtypeStruct((LANES, KEY_W), F32), jax.ShapeDtypeStruct((1, KEY_W), F32),
        ],
        scratch_shapes=[pltpu.VMEM((GLA_HEADS, GLA_DV, LANES), F32)],
        compiler_params=_params(56),
    )(p, p, p, p, do, st, wd_pad, bd, token, *(other or ()))


def _inproj_wgrad(x, g1, dq, dk, dv, dg, du, dvv, dlr):
    seq = x.shape[0]
    tm = min(seq, 512)

    def body(x_ref, g1_ref, dq_ref, dk_ref, dv_ref, dg_ref, du_ref, dvv_ref, dlr_ref, dw_ref, dp_ref):
        @pl.when(pl.program_id(0) == 0)
        def _():
            dw_ref[...] = jnp.zeros_like(dw_ref)

        for col, ref in ((COL_Q, dq_ref), (COL_K, dk_ref), (COL_V, dv_ref), (COL_G, dg_ref), (COL_U, du_ref), (COL_VV, dvv_ref), (COL_LR, dlr_ref)):
            dp_ref[:, col : col + ref.shape[1]] = ref[...]
        xv = x_ref[...]
        h = (xv * lax.rsqrt(jnp.mean(xv * xv, axis=-1, keepdims=True) + EPS) * g1_ref[...]).astype(BF16)
        dw_ref[0:ROW_LR, :] += _tn(dp_ref[:, 0:COL_U], h)
        dw_ref[ROW_UV:PROJ_W, :] += _tn(dp_ref[:, COL_U:COL_LR], h)
        dw_ref[ROW_LR:ROW_UV, :] += _tn(dp_ref[:, COL_LR:PROJ_WP], h)[0 : ROW_UV - ROW_LR]

    row = lambda w: pl.BlockSpec((tm, w), lambda i: (i, 0))
    return pl.pallas_call(
        body,
        name="inproj_wgrad",
        grid=(seq // tm,),
        in_specs=[row(D_MODEL), _resident((1, D_MODEL)), row(KEY_W), row(KEY_W), row(GLA_W), row(GLA_W), row(GMLP_W), row(GMLP_W), row(LANES)],
        out_specs=[pl.BlockSpec((PROJ_W, D_MODEL), lambda i: (0, 0)), row(PROJ_WP)],
        out_shape=[jax.ShapeDtypeStruct((PROJ_W, D_MODEL), F32), jax.ShapeDtypeStruct((seq, PROJ_WP), BF16)],
        compiler_params=_params(56),
    )(x, g1, dq, dk, dv, dg, du, dvv, dlr)


def _inproj_dx(x, dx1, g1, w_in_t, dp, token):
    seq = x.shape[0]
    tm = min(seq, 512)

    def body(x_ref, dx1_ref, g1_ref, w_ref, dp_ref, token_ref, dx_ref, dg1_ref):
        @pl.when(pl.program_id(0) == 0)
        def _():
            dg1_ref[...] = jnp.zeros_like(dg1_ref)

        xv = x_ref[...]
        r1 = lax.rsqrt(jnp.mean(xv * xv, axis=-1, keepdims=True) + EPS)
        xh = xv * r1
        dh = (_nn(dp_ref[:, 0:COL_U], w_ref[0:ROW_LR, :]) + _nn(dp_ref[:, COL_U:COL_LR], w_ref[ROW_UV:PROJ_W, :])
              + _nn(dp_ref[:, COL_LR:PROJ_WP], w_ref[ROW_LR : ROW_LR + LANES, :]))
        dg1_ref[...] += jnp.sum(dh * xh, axis=0, keepdims=True)
        dx_ref[...] = dx1_ref[...] + _rms_bwd(dh * g1_ref[...], xh, r1)

    row = lambda w: pl.BlockSpec((tm, w), lambda i: (i, 0))
    return pl.pallas_call(
        body,
        name="inproj_dx",
        grid=(seq // tm,),
        in_specs=[row(D_MODEL), row(D_MODEL), _resident((1, D_MODEL)), _resident((PROJ_W, D_MODEL)), row(PROJ_WP), _resident(TOKEN_SHAPE)],
        out_specs=[row(D_MODEL), pl.BlockSpec((1, D_MODEL), lambda i: (0, 0))],
        out_shape=[jax.ShapeDtypeStruct((seq, D_MODEL), F32), jax.ShapeDtypeStruct((1, D_MODEL), F32)],
        compiler_params=_params(48),
    )(x, dx1, g1, w_in_t, dp, token)


def _in_hbm(a):
    return pltpu.with_memory_space_constraint(a, pltpu.HBM)


def _row_tile(rows, multiple=8):
    for t in range(min(rows, 512), 0, -1):
        if rows % t == 0 and t % multiple == 0:
            return t
    return rows


def _cast_into_slot(w, shard, token):
    rows, cols = w.shape
    tr = _row_tile(rows, 16)

    def body(s_ref, w_ref, token_ref, o_ref):
        o_ref[...] = w_ref[...].astype(BF16)

    return pl.pallas_call(
        body,
        name="cast_into_slot",
        grid_spec=pltpu.PrefetchScalarGridSpec(
            num_scalar_prefetch=1,
            grid=(rows // tr,),
            in_specs=[pl.BlockSpec((tr, cols), lambda i, s_ref: (i, 0)), pl.BlockSpec(TOKEN_SHAPE, lambda i, s_ref: (0, 0))],
            out_specs=pl.BlockSpec((None, tr, cols), lambda i, s_ref: (s_ref[0], i, 0)),
        ),
        out_shape=pltpu.HBM((N_SHARDS, rows, cols), BF16),
        compiler_params=_params(32, ("parallel",)),
    )(shard, _in_hbm(w), token)


def _add_halves(grads4, recvs, c):
    n = len(grads4)
    _, rows, _ = grads4[0].shape
    tr = _row_tile(rows, 16)

    def body(c_ref, *refs):
        for k in range(n):
            total = refs[k][...] + refs[n + k][...]
            refs[2 * n + k][...] = total
            refs[3 * n + k][...] = total.astype(BF16)

    out = pl.BlockSpec((None, tr, HALF), lambda s, i, c_ref: (s, i, 0))
    mine = pl.BlockSpec((None, tr, HALF), lambda s, i, c_ref: (s, i, c_ref[0]))
    outs = pl.pallas_call(
        body,
        name="add_halves",
        grid_spec=pltpu.PrefetchScalarGridSpec(
            num_scalar_prefetch=1,
            grid=(N_SHARDS, rows // tr),
            in_specs=[mine] * n + [out] * n,
            out_specs=[out] * (2 * n),
        ),
        out_shape=[pltpu.HBM((N_SHARDS, rows, HALF), F32)] * n + [pltpu.HBM((N_SHARDS, rows, HALF), BF16)] * n,
        compiler_params=_params(48, ("parallel", "parallel")),
    )(c, *[_in_hbm(a) for a in list(grads4) + list(recvs)])
    return list(zip(outs[:n], outs[n:]))


def _add_partials(part4, recv3, shard_core):
    _, rows, _ = part4.shape
    tr = _row_tile(rows, 16)

    def body(sc_ref, p_ref, r_ref, o_ref):
        o_ref[...] = ((p_ref[...] + r_ref[0].astype(F32)) + r_ref[1].astype(F32)) + r_ref[2].astype(F32)

    return pl.pallas_call(
        body,
        name="add_partials",
        grid_spec=pltpu.PrefetchScalarGridSpec(
            num_scalar_prefetch=1,
            grid=(rows // tr,),
            in_specs=[
                pl.BlockSpec((None, tr, HALF), lambda i, sc_ref: (sc_ref[0], i, 0)),
                pl.BlockSpec((3, tr, HALF), lambda i, sc_ref: (0, i, 0)),
            ],
            out_specs=pl.BlockSpec((tr, HALF), lambda i, sc_ref: (i, sc_ref[1])),
        ),
        out_shape=pltpu.HBM((rows, 2 * HALF), F32),
        compiler_params=_params(32, ("parallel",)),
    )(shard_core, _in_hbm(part4), _in_hbm(recv3))


def _adam_math(w, g, m, v):
    m = ADAM_B1 * m + (1.0 - ADAM_B1) * g
    v = ADAM_B2 * v + (1.0 - ADAM_B2) * (g * g)
    m_hat = m / (1.0 - ADAM_B1**ADAM_STEP)
    v_hat = v / (1.0 - ADAM_B2**ADAM_STEP)
    delta = -ADAM_LR * (m_hat / (jnp.sqrt(v_hat) + ADAM_EPS) + ADAM_WD * w)
    return delta, m, v


def _adamw(w, g, m, v):
    rows, cols = w.shape
    tr = _row_tile(rows)

    def body(w_ref, g_ref, m_ref, v_ref, go_ref, d_ref, mo_ref, vo_ref):
        gv = g_ref[...]
        go_ref[...] = gv
        d_ref[...], mo_ref[...], vo_ref[...] = _adam_math(w_ref[...], gv, m_ref[...], v_ref[...])

    spec = pl.BlockSpec((tr, cols), lambda i: (i, 0))
    return pl.pallas_call(
        body, name="adamw", grid=(rows // tr,), in_specs=[spec] * 4, out_specs=[spec] * 4, out_shape=[pltpu.HBM(w.shape, F32)] * 4,
        compiler_params=_params(32, ("parallel",)),
    )(_in_hbm(w), _in_hbm(g), _in_hbm(m), _in_hbm(v))


SMALL_ROWS = 560
DECAY_ROWS = 8
SMALL_TOTAL = SMALL_ROWS + 2 * N_SHARDS * DECAY_ROWS


def _adamw_small(gathered, own, wp, mp, vp):
    out_rows = SMALL_ROWS + 2 * DECAY_ROWS

    def body(ga_ref, own_ref, w_ref, m_ref, v_ref, g_ref, d_ref, mo_ref, vo_ref):
        x, y, c = _position()
        shard, me = 2 * x + y, 4 * x + 2 * y + c
        total = lambda rows: functools.reduce(lambda a, b: a + b, [jnp.where(me == d, own_ref[rows, :], ga_ref[d, rows, :]) for d in range(8)])
        g_ref[pl.ds(0, SMALL_ROWS), :] = total(pl.ds(0, SMALL_ROWS))
        for k in range(2):
            start = pl.multiple_of(SMALL_ROWS + k * N_SHARDS * DECAY_ROWS + shard * DECAY_ROWS, DECAY_ROWS)
            g_ref[pl.ds(SMALL_ROWS + k * DECAY_ROWS, DECAY_ROWS), :] = total(pl.ds(start, DECAY_ROWS))
        d_ref[...], mo_ref[...], vo_ref[...] = _adam_math(w_ref[...], g_ref[...], m_ref[...], v_ref[...])

    shape = jax.ShapeDtypeStruct((out_rows, LANES), F32)
    return pl.pallas_call(body, name="adamw_small", out_shape=[shape] * 4, compiler_params=_params(32, None))(gathered, own, wp, mp, vp)


ANY = pl.BlockSpec(memory_space=pl.ANY)


def _position():
    return lax.axis_index("x"), lax.axis_index("y"), lax.axis_index("c")


def _other_chips(x, y):
    return [(1 - x, y), (x, 1 - y), (1 - x, 1 - y)]


HBM = pl.BlockSpec(memory_space=pltpu.HBM)
SEM = pl.BlockSpec(memory_space=pltpu.SEMAPHORE)
TOKEN = jax.ShapeDtypeStruct(TOKEN_SHAPE, F32)
DATAFLOW = pltpu.SideEffectType.DATAFLOW_SIDE_EFFECTING


def _half_block(ref4, slot, core):
    return ref4.at[slot, :, pl.ds(pl.multiple_of(core * HALF, HALF), HALF)]


def _gather_ici_copies(bufs, lands, send_sems, recv_sems):
    x, y, c = _position()
    pairs = []
    for k, ref4 in enumerate(bufs):
        mine = _half_block(ref4, 2 * x + y, c)
        for j, (px, py) in enumerate(_other_chips(x, y)):
            sems = dict(send_sem=send_sems.at[3 * k + j], recv_sem=recv_sems.at[3 * k + j], device_id=(px, py, c), device_id_type=MESH)
            pairs.append((functools.partial(pltpu.make_async_remote_copy, src_ref=mine, dst_ref=mine, **sems),
                          functools.partial(pltpu.make_async_remote_copy, src_ref=mine, dst_ref=_half_block(ref4, 2 * px + py, c), **sems)))
    return pairs


def _gather_d2d_copies(bufs, lands, send_sems, recv_sems):
    x, y, c = _position()
    pairs = []
    for k, ref4 in enumerate(bufs):
        for j, (px, py) in enumerate(_other_chips(x, y)):
            have = _half_block(ref4, 2 * px + py, c)
            sems = dict(send_sem=send_sems.at[3 * k + j], recv_sem=recv_sems.at[3 * k + j], device_id=(x, y, 1 - c), device_id_type=MESH)
            pairs.append((functools.partial(pltpu.make_async_remote_copy, src_ref=have, dst_ref=have, **sems),
                          functools.partial(pltpu.make_async_remote_copy, src_ref=have, dst_ref=_half_block(ref4, 2 * px + py, 1 - c), **sems)))
    return pairs


def _gather_forward(bufs):
    n = len(bufs)

    def body(*refs):
        outs = refs[n : 2 * n]
        send_sems, recv_sems = refs[2 * n :]
        d2d = _gather_d2d_copies(outs, (), send_sems, recv_sems)
        for forward, _ in d2d:
            forward().start()
        for forward, arrival in d2d:
            arrival().wait_recv()
            forward().wait_send()

    return pl.pallas_call(
        body,
        name="gather_forward",
        in_specs=[ANY] * n,
        out_specs=[ANY] * n,
        out_shape=[jax.ShapeDtypeStruct(b.shape, b.dtype) for b in bufs],
        input_output_aliases={k: k for k in range(n)},
        scratch_shapes=[pltpu.SemaphoreType.DMA((3 * n,)), pltpu.SemaphoreType.DMA((3 * n,))],
        compiler_params=pltpu.CompilerParams(has_side_effects=True),
    )(*bufs)


def _exchange_halves(grads4):
    n = len(grads4)

    def body(*refs):
        ins, outs = refs[:n], refs[n : 2 * n]
        send_sems, recv_sems = refs[2 * n :]
        x, y, c = _position()
        copies = []
        for k in range(n):
            cp = pltpu.make_async_remote_copy(
                src_ref=ins[k].at[:, :, pl.ds(pl.multiple_of((1 - c) * HALF, HALF), HALF)], dst_ref=outs[k],
                send_sem=send_sems.at[k], recv_sem=recv_sems.at[k], device_id=(x, y, 1 - c), device_id_type=MESH)
            cp.start()
            copies.append(cp)
        for cp in copies:
            cp.wait()

    return pl.pallas_call(
        body,
        name="exchange_halves",
        in_specs=[ANY] * n,
        out_specs=[ANY] * n,
        out_shape=[jax.ShapeDtypeStruct((N_SHARDS, g.shape[1], HALF), g.dtype) for g in grads4],
        scratch_shapes=[pltpu.SemaphoreType.DMA((n,)), pltpu.SemaphoreType.DMA((n,))],
        compiler_params=pltpu.CompilerParams(has_side_effects=True),
    )(*grads4)


def _both_ends(**copy):
    maker = functools.partial(pltpu.make_async_remote_copy, **copy)
    return maker, maker


def _scatter_copies(parts, lands, send_sems, recv_sems):
    x, y, c = _position()
    return [_both_ends(src_ref=parts[k].at[2 * px + py], dst_ref=lands[k].at[j], send_sem=send_sems.at[3 * k + j],
                       recv_sem=recv_sems.at[3 * k + j], device_id=(px, py, c), device_id_type=MESH)
            for k in range(len(parts)) for j, (px, py) in enumerate(_other_chips(x, y))]


def _exchange_copies(grads, lands, send_sems, recv_sems):
    x, y, c = _position()
    return [_both_ends(src_ref=grads[k].at[:, :, pl.ds(pl.multiple_of((1 - c) * HALF, HALF), HALF)], dst_ref=lands[k],
                       send_sem=send_sems.at[k], recv_sem=recv_sems.at[k], device_id=(x, y, 1 - c), device_id_type=MESH)
            for k in range(len(grads))]


def _exchange_lands(grads4):
    return [jax.ShapeDtypeStruct((N_SHARDS, g.shape[1], HALF), g.dtype) for g in grads4]


def _scatter_lands(parts4):
    return [jax.ShapeDtypeStruct((3,) + g.shape[1:], g.dtype) for g in parts4]


def _small_gather_copies(blocks, lands, send_sems, recv_sems):
    x, y, c = _position()
    flip = lambda v, bit: 1 - v if bit else v
    return [_both_ends(src_ref=blocks[0], dst_ref=lands[0].at[4 * x + 2 * y + c], send_sem=send_sems.at[r - 1],
                       recv_sem=recv_sems.at[r - 1], device_id=(flip(x, r & 4), flip(y, r & 2), flip(c, r & 1)), device_id_type=MESH)
            for r in range(1, 8)]


def _split_start(name, srcs, land_shapes, make_copies, nsem, after=()):
    n, nl, na = len(srcs), len(land_shapes), len(after)
    lands = [lax.empty(a.shape, a.dtype) for a in land_shapes]

    def body(*refs):
        send_sems, recv_sems = refs[n + nl + na], refs[n + nl + na + 1]
        token = refs[2 * (n + nl) + na + 2]
        for send, _ in make_copies(refs[:n], refs[n : n + nl], send_sems, recv_sems):
            send().start()
        token[...] = jnp.zeros_like(token)

    hbm = lambda a: pltpu.HBM(a.shape, a.dtype)
    out = pl.pallas_call(
        body,
        name=name,
        in_specs=[HBM] * (n + nl) + [ANY] * na,
        out_specs=(SEM, SEM, *[HBM] * (n + nl), pl.BlockSpec(memory_space=pltpu.VMEM)),
        out_shape=(pltpu.SemaphoreType.DMA((nsem,)), pltpu.SemaphoreType.DMA((nsem,)), *[hbm(a) for a in list(srcs) + lands], TOKEN),
        input_output_aliases={k: 2 + k for k in range(n + nl)},
        compiler_params=pltpu.CompilerParams(has_side_effects=DATAFLOW),
    )(*[pltpu.with_memory_space_constraint(a, pltpu.HBM) for a in list(srcs) + lands], *after)
    return out[0], out[1], list(out[2 : 2 + n]), list(out[2 + n : 2 + n + nl]), out[2 + n + nl]


def _split_wait(name, send_sems, recv_sems, srcs, lands, make_copies, after):
    n, nl = len(srcs), len(lands)

    def body(*refs):
        for send, arrival in make_copies(refs[:n], refs[n : n + nl], refs[n + nl], refs[n + nl + 1]):
            send().wait_send()
            arrival().wait_recv()

    hbm = lambda a: pltpu.HBM(a.shape, a.dtype)
    out = pl.pallas_call(
        body,
        name=name,
        in_specs=[HBM] * (n + nl) + [SEM, SEM] + [ANY] * len(after),
        out_specs=tuple([HBM] * (n + nl)),
        out_shape=tuple(hbm(a) for a in list(srcs) + list(lands)),
        input_output_aliases={k: k for k in range(n + nl)},
        compiler_params=pltpu.CompilerParams(has_side_effects=DATAFLOW),
    )(*srcs, *lands, send_sems, recv_sems, *after)
    return list(out[:n]), list(out[n:])


def _join_halves(bufs, after=()):
    n, na = len(bufs), len(after)

    def body(*refs):
        outs = refs[n + na : 2 * n + na]
        send_sems, recv_sems = refs[2 * n + na :]
        x, y, c = _position()
        half = lambda ref, core: ref.at[:, pl.ds(pl.multiple_of(core * HALF, HALF), HALF)]
        for k in range(n):
            mine = half(outs[k], c)
            pltpu.make_async_remote_copy(
                src_ref=mine, dst_ref=mine, send_sem=send_sems.at[k], recv_sem=recv_sems.at[k],
                device_id=(x, y, 1 - c), device_id_type=MESH).start()
        for k in range(n):
            wait = pltpu.make_async_remote_copy(
                src_ref=half(outs[k], c), dst_ref=half(outs[k], 1 - c), send_sem=send_sems.at[k], recv_sem=recv_sems.at[k],
                device_id=(x, y, 1 - c), device_id_type=MESH)
            wait.wait_send()
            wait.wait_recv()

    return pl.pallas_call(
        body,
        name="join_halves",
        in_specs=[ANY] * (n + na),
        out_specs=[ANY] * n,
        out_shape=[jax.ShapeDtypeStruct(b.shape, b.dtype) for b in bufs],
        input_output_aliases={k: k for k in range(n)},
        scratch_shapes=[pltpu.SemaphoreType.DMA((n,)), pltpu.SemaphoreType.DMA((n,))],
        compiler_params=pltpu.CompilerParams(has_side_effects=True),
    )(*bufs, *after)


def _allgather_small(block):
    m_per, ncol = block.shape

    def body(x_ref, out_ref, send_sems, recv_sems, local_sem):
        x, y, c = _position()
        me, sibling = (x, y, c), (x, y, 1 - c)
        chips = _other_chips(x, y)

        def rows(px, py, pc):
            return out_ref.at[4 * px + 2 * py + pc]

        def copy(k, blk, to, src=None):
            return pltpu.make_async_remote_copy(
                src_ref=rows(*blk) if src is None else src, dst_ref=rows(*blk),
                send_sem=send_sems.at[k], recv_sem=recv_sems.at[k], device_id=to, device_id_type=MESH)

        mine = pltpu.make_async_copy(x_ref, rows(*me), local_sem)
        mine.start()
        first = [copy(0, me, sibling, src=x_ref)] + [copy(1 + j, me, (*chip, c), src=x_ref) for j, chip in enumerate(chips)]
        for cp in first:
            cp.start()
        passed = [copy(4 + j, (*chip, c), sibling) for j, chip in enumerate(chips)]
        for j, chip in enumerate(chips):
            copy(1 + j, (*chip, c), me).wait_recv()
            passed[j].start()
        copy(0, sibling, me).wait_recv()
        for j, chip in enumerate(chips):
            copy(4 + j, (*chip, 1 - c), me).wait_recv()
        for cp in first + passed:
            cp.wait_send()
        mine.wait()

    return pl.pallas_call(
        body,
        name="allgather_small",
        in_specs=[pl.BlockSpec(memory_space=pltpu.VMEM)],
        out_specs=pl.BlockSpec(memory_space=pltpu.VMEM),
        out_shape=jax.ShapeDtypeStruct((8, m_per, ncol), block.dtype),
        scratch_shapes=[pltpu.SemaphoreType.DMA((7,)), pltpu.SemaphoreType.DMA((7,)), pltpu.SemaphoreType.DMA],
        compiler_params=pltpu.CompilerParams(has_side_effects=True, vmem_limit_bytes=32 * MIB),
    )(block)


SMALL_NAMES = ["norm1_g", "b_decay_f", "b_decay_b", "gla_norm_g", "gmlp_ln_g", "gmlp_ln_b", "w_spatial", "b_spatial", "norm2_g", "final_norm_g"]


def _pack_small(parts, decay_parts):
    flat = jnp.concatenate([a.reshape(-1) for a in parts])
    flat = jnp.pad(flat, (0, SMALL_ROWS * LANES - flat.shape[0])).reshape(SMALL_ROWS, LANES)
    return jnp.concatenate([flat] + [d.reshape(-1, LANES) for d in decay_parts], axis=0)


def _unpack_small(packed, like):
    out, off = [], 0
    flat = packed[:SMALL_ROWS].reshape(-1)
    for a in like:
        out.append(flat[off : off + a.size].reshape(a.shape))
        off += a.size
    return out


def kernel(x, norm1_g, w_in, w_decay_f, b_decay_f, w_decay_b, b_decay_b, gla_norm_g, gmlp_ln_g, gmlp_ln_b, w_spatial, b_spatial, w_out, norm2_g, w_gate, w_up, w_down, final_norm_g, loss_target, m_norm1_g, m_w_in, m_w_decay_f, m_b_decay_f, m_w_decay_b, m_b_decay_b, m_gla_norm_g, m_gmlp_ln_g, m_gmlp_ln_b, m_w_spatial, m_b_spatial, m_w_out, m_norm2_g, m_w_gate, m_w_up, m_w_down, m_final_norm_g, v_norm1_g, v_w_in, v_w_decay_f, v_b_decay_f, v_w_decay_b, v_b_decay_b, v_gla_norm_g, v_gmlp_ln_g, v_gmlp_ln_b, v_w_spatial, v_b_spatial, v_w_out, v_norm2_g, v_w_gate, v_w_up, v_w_down, v_final_norm_g):
    args = dict(locals())
    cx, cy, cc = lax.axis_index("x"), lax.axis_index("y"), lax.axis_index("c")
    shard = 2 * cx + cy
    xs = x[0]
    target = loss_target[0]

    big_names = ["w_in", "w_out", "w_gate", "w_up", "w_down"]
    transposed = ("w_in", "w_gate", "w_up")
    rows_of = lambda pre, k: jnp.transpose(args[pre + k][0]) if k in transposed else args[pre + k][0]
    big_shards = {k: rows_of("", k) for k in big_names}
    c_arr = cc.reshape(1).astype(jnp.int32)
    s_arr = shard.reshape(1).astype(jnp.int32)
    sc_arr = jnp.stack([shard, cc]).astype(jnp.int32)
    zero_token = jnp.zeros(TOKEN_SHAPE, F32)
    w_send, w_recv, (w_in4,), _, token_w_in = _split_start(
        "w_in_gather_start", [_cast_into_slot(big_shards["w_in"], s_arr, zero_token)], [], _gather_ici_copies, 3)
    late = ["w_out", "w_gate", "w_up", "w_down"]
    late_slots = [_cast_into_slot(big_shards[k], s_arr, token_w_in) for k in late]
    dec_block = jnp.concatenate([w_decay_f[0].reshape(-1, LANES), w_decay_b[0].reshape(-1, LANES)], axis=0)
    dec_all = _allgather_small(dec_block)
    (w_in4,), _ = _split_wait("w_in_gather_wait", w_send, w_recv, [w_in4], [], _gather_ici_copies, (dec_all, *late_slots))
    (w_in4,) = _gather_forward([w_in4])
    w_in_t = w_in4.reshape(PROJ_W, D_MODEL)
    g_send, g_recv, late_bufs, _, token_gather = _split_start(
        "gather_start", late_slots, [], _gather_ici_copies, 3 * len(late), after=(w_in4,))
    dec_all = dec_all[::2].reshape(N_SHARDS, 2, LOWRANK, KEY_W // N_SHARDS)
    wdf_full = jnp.transpose(dec_all[:, 0], (1, 0, 2)).reshape(LOWRANK, KEY_W)
    wdb_full = jnp.transpose(dec_all[:, 1], (1, 0, 2)).reshape(LOWRANK, KEY_W)
    wd_pad_f = jnp.zeros((LANES, KEY_W), F32).at[0:LOWRANK].set(wdf_full).astype(BF16)
    wd_pad_b = jnp.zeros((LANES, KEY_W), F32).at[LOWRANK : 2 * LOWRANK].set(wdb_full).astype(BF16)

    ws_bf = w_spatial[0].astype(BF16)
    wst_bf = jnp.transpose(w_spatial[0], (0, 2, 1)).astype(BF16)
    bs_col = b_spatial[0].reshape(GMLP_GROUPS, GMLP_CHUNK, 1)

    p = _inproj(xs, norm1_g, w_in_t, token_gather)
    o_f, st_f = _gla_fwd(p, wd_pad_f, b_decay_f, token_gather, reverse=False)
    o_b, st_b = _gla_fwd(p, wd_pad_b, b_decay_b, token_gather, reverse=True)
    late_bufs, _ = _split_wait("gather_wait", g_send, g_recv, late_bufs, [], _gather_ici_copies, (o_f, o_b))
    (w_out4,) = _gather_forward(late_bufs[:1])
    f_send, f_recv, ffn_bufs, _, token_forward = _split_start(
        "forward_start", late_bufs[1:], [], _gather_d2d_copies, 3 * (len(late) - 1), after=(w_out4,))
    w_out_full = w_out4.reshape(-1, D_MODEL)
    x1, ycat = _mixer_out(xs, o_f, o_b, p, gla_norm_g, gmlp_ln_g, gmlp_ln_b, ws_bf, bs_col, w_out_full, token_forward)
    ffn_bufs, _ = _split_wait("forward_wait", f_send, f_recv, ffn_bufs, [], _gather_d2d_copies, (x1,))
    wg_t, wu_t, wd = [b.reshape(-1, D_MODEL) for b in ffn_bufs]
    gf = final_norm_g.reshape(1, D_MODEL)
    h2, gate, up, act, dx2, loss_acc, dgf = _ffn_fwd(x1, target, norm2_g, gf, wg_t, wu_t, wd)

    dgate, dup, dx1, dg2 = _ffn_bwd(dx2, gate, up, x1, norm2_g, wg_t, wu_t, wd)
    ffn_grads4 = [g.reshape(N_SHARDS, FF_SHARD, D_MODEL) for g in _ffn_wgrad(h2, dgate, dup, act, dx2)]
    e_send, e_recv, e_srcs, e_lands, token_exchange = _split_start(
        "exchange_start", ffn_grads4, _exchange_lands(ffn_grads4), _exchange_copies, len(ffn_grads4))
    do, dg, du, dvv, dwo, dgn, dlng, dlnb, dws, dbs = _mixer_bwd(
        dx1, ycat, o_f, o_b, p, gla_norm_g, gmlp_ln_g, gmlp_ln_b, ws_bf, wst_bf, bs_col, w_out_full, token_exchange)
    ffn_mine, ffn_other = _split_wait("exchange_wait", e_send, e_recv, e_srcs, e_lands, _exchange_copies, (do,))
    ffn_parts = _add_halves(ffn_mine, ffn_other, c_arr)
    ffn_payload = [pb for _, pb in ffn_parts]
    s_send, s_recv, s_parts, s_lands, token_scatter = _split_start(
        "scatter_start", ffn_payload, _scatter_lands(ffn_payload), _scatter_copies, 3 * len(ffn_payload))
    dq_f, dk_f, dv_f, dlr_f, dwdec_f, dbdec_f = _gla_bwd(p, do, st_f, wd_pad_f, b_decay_f, token_scatter, reverse=False)
    dq, dk, dv, dlr, dwdec_b, dbdec_b = _gla_bwd(
        p, do, st_b, wd_pad_b, b_decay_b, token_scatter, reverse=True, other=(dq_f, dk_f, dv_f, dlr_f))
    dwin_t, dp = _inproj_wgrad(xs, norm1_g, dq, dk, dv, dg, du, dvv, dlr)
    _, ffn_recv = _split_wait("scatter_wait", s_send, s_recv, s_parts, s_lands, _scatter_copies, (dwin_t,))

    dwin4 = dwin_t.reshape(N_SHARDS, PROJ_W // N_SHARDS, D_MODEL)
    dwo4 = dwo.reshape(N_SHARDS, D_MODEL // N_SHARDS, D_MODEL)
    proj_grads4 = [dwin4, dwo4]
    proj_parts = [_add_halves([g], [r], c_arr)[0] for g, r in zip(proj_grads4, _exchange_halves(proj_grads4))]
    proj_payload = [pb for _, pb in proj_parts]
    p_send, p_recv, p_parts, p_lands, token_proj = _split_start(
        "proj_scatter_start", proj_payload, _scatter_lands(proj_payload), _scatter_copies, 3 * len(proj_payload))
    dx, dg1 = _inproj_dx(xs, dx1, norm1_g, w_in_t, dp, token_proj)
    _, proj_recv = _split_wait("proj_scatter_wait", p_send, p_recv, p_parts, p_lands, _scatter_copies, (dx,))

    dwdec_f16 = dwdec_f[0:LOWRANK]
    dwdec_b16 = dwdec_b[LOWRANK : 2 * LOWRANK]
    shard_major = lambda a: jnp.transpose(a.reshape(LOWRANK, N_SHARDS, KEY_W // N_SHARDS), (1, 0, 2))
    small_grads = {
        "norm1_g": dg1, "b_decay_f": dbdec_f, "b_decay_b": dbdec_b, "gla_norm_g": dgn, "gmlp_ln_g": dlng, "gmlp_ln_b": dlnb,
        "w_spatial": dws, "b_spatial": dbs, "norm2_g": dg2, "final_norm_g": dgf,
    }
    g_pack = _pack_small([small_grads[k] for k in SMALL_NAMES] + [loss_acc], [shard_major(dwdec_f16), shard_major(dwdec_b16)])
    sg_send, sg_recv, (g_pack,), g_lands, token_small = _split_start(
        "small_gather_start", [g_pack], [jax.ShapeDtypeStruct((8, SMALL_TOTAL, LANES), F32)], _small_gather_copies, 7)

    parts_f32 = [pf for pf, _ in proj_parts + ffn_parts]
    bufs = [_add_partials(pf, r, sc_arr) for pf, r in zip(parts_f32, proj_recv + ffn_recv)]
    big_grads = dict(zip(big_names, _join_halves(bufs, after=(token_small,))))
    big_updates = {k: _adamw(big_shards[k], big_grads[k], rows_of("m_", k), rows_of("v_", k)) for k in big_names}

    (g_pack,), (g_all,) = _split_wait(
        "small_gather_wait", sg_send, sg_recv, [g_pack], g_lands, _small_gather_copies, tuple(u[1] for u in big_updates.values()))
    pack_own = lambda pre: _pack_small([args[pre + k] for k in SMALL_NAMES], [args[pre + "w_decay_f"], args[pre + "w_decay_b"]])
    sg, sd, sm, sv = _adamw_small(g_all, g_pack, pack_own(""), pack_own("m_"), pack_own("v_"))

    names = ["norm1_g", "w_in", "w_decay_f", "b_decay_f", "w_decay_b", "b_decay_b", "gla_norm_g", "gmlp_ln_g", "gmlp_ln_b",
             "w_spatial", "b_spatial", "w_out", "norm2_g", "w_gate", "w_up", "w_down", "final_norm_g"]
    like = [args[k] for k in SMALL_NAMES]
    results = {"g": {}, "d": {}, "m": {}, "v": {}}
    for tag, packed in (("g", sg), ("d", sd), ("m", sm), ("v", sv)):
        for k, a in zip(SMALL_NAMES, _unpack_small(packed, like)):
            results[tag][k] = a
        results[tag]["w_decay_f"] = packed[SMALL_ROWS : SMALL_ROWS + DECAY_ROWS].reshape(w_decay_f.shape)
        results[tag]["w_decay_b"] = packed[SMALL_ROWS + DECAY_ROWS :].reshape(w_decay_b.shape)
    for k in big_names:
        for tag, a in zip("gdmv", big_updates[k]):
            results[tag][k] = (jnp.transpose(a) if k in transposed else a).reshape(args[k].shape)

    loss = sg[:SMALL_ROWS].reshape(-1)[sum(a.size for a in like)]
    grad_x = dx.reshape(x.shape)
    return (loss, grad_x, *[results["g"][k] for k in names], *[results["d"][k] for k in names],
            *[results["m"][k] for k in names], *[results["v"][k] for k in names])
```

```python
import functools
import math

import jax
import jax.numpy as jnp
from jax import lax
from jax.experimental import pallas as pl
from jax.experimental.pallas import tpu as pltpu

F32, BF16 = jnp.float32, jnp.bfloat16

D_MODEL = 1024
GLA_HEADS = 4
GLA_DK = 64
GLA_DV = 128
KEY_W = GLA_HEADS * GLA_DK
GLA_W = GLA_HEADS * GLA_DV
GMLP_W = 512
GMLP_GROUPS = 4
GMLP_CHUNK = 128
LOWRANK = 16
GLA_CHUNK = 64
GLA_TAU = 16.0
PROJ_W = 2592
PROJ_WP = 2688
D_FF = 2816
N_SHARDS = 4
FF_SHARD = D_FF // N_SHARDS
EPS = 1e-6
LANES = 128
TOKEN_SHAPE = (8, LANES)
MIB = 1024 * 1024

ADAM_LR = 0.001
ADAM_B1 = 0.9
ADAM_B2 = 0.999
ADAM_EPS = 1e-08
ADAM_WD = 0.01
ADAM_STEP = 10

COL_Q, COL_K = 0, 256
COL_V, COL_G, COL_U, COL_VV = 512, 1024, 1536, 2048
COL_LR = 2560
ROW_LR, ROW_UV = 1536, 1568
HALF = D_MODEL // 2

MESH = pl.DeviceIdType.MESH


def _nn(a, b):
    return jnp.dot(a, b, preferred_element_type=F32)


def _nt(a, b):
    return lax.dot_general(a, b, (((1,), (1,)), ((), ())), preferred_element_type=F32)


def _tn(a, b):
    return lax.dot_general(a, b, (((0,), (0,)), ((), ())), preferred_element_type=F32)


def _bnn(a, b):
    return jnp.einsum("nik,nkj->nij", a, b, preferred_element_type=F32)


def _bnt(a, b):
    return jnp.einsum("nik,njk->nij", a, b, preferred_element_type=F32)


def _btn(a, b):
    return jnp.einsum("nki,nkj->nij", a, b, preferred_element_type=F32)


def _resident(shape):
    zeros = (0,) * len(shape)
    return pl.BlockSpec(shape, lambda *_: zeros, pipeline_mode=pl.Buffered(1))


def _params(vmem_mib, semantics=("arbitrary",)):
    return pltpu.CompilerParams(vmem_limit_bytes=vmem_mib * MIB, dimension_semantics=semantics)


def _sigmoid(x):
    return 1.0 / (1.0 + jnp.exp(-x))


def _gelu(x):
    return 0.5 * x * (1.0 + lax.erf(x * (1.0 / math.sqrt(2.0))))


def _gelu_and_grad(x):
    cdf = 0.5 * (1.0 + lax.erf(x * (1.0 / math.sqrt(2.0))))
    return x * cdf, cdf + x * jnp.exp(-0.5 * x * x) * (1.0 / math.sqrt(2.0 * math.pi))


def _log_sigmoid(x):
    return jnp.minimum(x, 0.0) - jnp.log(1.0 + jnp.exp(-jnp.abs(x)))


def _rms_bwd(dxh, xh, r):
    return r * (dxh - xh * jnp.mean(dxh * xh, axis=-1, keepdims=True))


def _chunk_cumsum(v, row_in_chunk, reverse):
    rows = v.shape[0]
    for sh in (1, 2, 4, 8, 16, 32):
        if reverse:
            v = v + jnp.where(row_in_chunk + sh < GLA_CHUNK, pltpu.roll(v, rows - sh, axis=0), 0.0)
        else:
            v = v + jnp.where(row_in_chunk >= sh, pltpu.roll(v, sh, axis=0), 0.0)
    return v


def _inproj(x, g1, w_in_t, token):
    seq = x.shape[0]
    tm = min(seq, 512)

    def body(x_ref, g_ref, w_ref, token_ref, p_ref):
        xv = x_ref[...]
        r = lax.rsqrt(jnp.mean(xv * xv, axis=-1, keepdims=True) + EPS)
        h = (xv * r * g_ref[...]).astype(BF16)
        p_ref[:, 0:COL_U] = _nt(h, w_ref[0:ROW_LR, :])
        p_ref[:, COL_U:COL_LR] = _nt(h, w_ref[ROW_UV:PROJ_W, :])
        p_ref[:, COL_LR:PROJ_WP] = _nt(h, w_ref[ROW_LR : ROW_LR + LANES, :])

    return pl.pallas_call(
        body,
        name="inproj",
        grid=(seq // tm,),
        in_specs=[pl.BlockSpec((tm, D_MODEL), lambda i: (i, 0)), _resident((1, D_MODEL)), _resident((PROJ_W, D_MODEL)), _resident(TOKEN_SHAPE)],
        out_specs=pl.BlockSpec((tm, PROJ_WP), lambda i: (i, 0)),
        out_shape=jax.ShapeDtypeStruct((seq, PROJ_WP), F32),
        compiler_params=_params(48, ("parallel",)),
    )(x, g1, w_in_t, token)


def _gla_tile(seq):
    return min(seq, 1024)


def _gla_decay_terms(lr_bf, wd_ref, bd_ref, pair, row_in_chunk, reverse, n):
    cols = pl.ds(pair * LANES, LANES)
    pre = _nn(lr_bf, wd_ref[:, cols]) + bd_ref[:, cols]
    la = _log_sigmoid(pre) * (1.0 / GLA_TAU)
    b = _chunk_cumsum(la, row_in_chunk, reverse)
    b3 = b.reshape(n, GLA_CHUNK, LANES)
    blast = b3[:, 0:1, :] if reverse else b3[:, GLA_CHUNK - 1 : GLA_CHUNK, :]
    return pre, b3, blast


def _gla_fwd(p, wd_pad, bd, token, reverse):
    seq = p.shape[0]
    tg = _gla_tile(seq)
    nt = seq // tg
    n = tg // GLA_CHUNK
    scale = GLA_DK**-0.5

    def tile(i):
        return nt - 1 - i if reverse else i

    def body(q_ref, k_ref, v_ref, lr_ref, wd_ref, bd_ref, token_ref, o_ref, st_ref, carry):
        @pl.when(pl.program_id(0) == 0)
        def _():
            carry[...] = jnp.zeros_like(carry)

        lr_bf = lr_ref[...].astype(BF16)
        states = [carry[h] for h in range(GLA_HEADS)]
        row_in_chunk = lax.broadcasted_iota(jnp.int32, (tg, LANES), 0) % GLA_CHUNK
        lane_head = lax.broadcasted_iota(jnp.int32, (1, LANES), 1) // GLA_DK
        tt = lax.broadcasted_iota(jnp.int32, (GLA_CHUNK, GLA_CHUNK), 0)
        ss = lax.broadcasted_iota(jnp.int32, (GLA_CHUNK, GLA_CHUNK), 1)
        causal = (tt <= ss) if reverse else (tt >= ss)
        order = range(n - 1, -1, -1) if reverse else range(n)
        heads = range(GLA_HEADS)
        qds, vhs, decs, sc_raw, dst = {}, {}, {}, {}, {}
        for pair in range(2):
            cols = pl.ds(pair * LANES, LANES)
            _, b3, blast = _gla_decay_terms(lr_bf, wd_ref, bd_ref, pair, row_in_chunk, reverse, n)
            q3 = q_ref[:, cols].reshape(n, GLA_CHUNK, LANES) * scale
            k3 = k_ref[:, cols].reshape(n, GLA_CHUNK, LANES)
            qd = q3 * jnp.exp(b3)
            kd = (k3 * jnp.exp(-b3)).astype(BF16)
            kte = k3 * jnp.exp(blast - b3)
            decs[pair] = jnp.exp(blast)
            qds[pair] = qd.astype(BF16)
            m0 = (lane_head == 0).astype(F32)
            m1 = (lane_head == 1).astype(F32)
            q_both = jnp.concatenate([(qd * m0).astype(BF16), (qd * m1).astype(BF16)], axis=1)
            sc_both = _bnt(q_both, kd)
            for hh, m in ((0, m0), (1, m1)):
                h = 2 * pair + hh
                vhs[h] = v_ref[:, pl.ds(h * GLA_DV, GLA_DV)].reshape(n, GLA_CHUNK, GLA_DV).astype(BF16)
                sc_raw[h] = sc_both[:, hh * GLA_CHUNK : (hh + 1) * GLA_CHUNK, :]
                dst[h] = _btn(vhs[h], (kte * m).astype(BF16))
        o_intra, befores = {}, {}
        for h in heads:
            o_intra[h] = _bnn(jnp.where(causal, sc_raw[h], 0.0).astype(BF16), vhs[h])
            st, before = states[h], [None] * n
            for j in order:
                before[j] = st
                st = st * decs[h // 2][j] + dst[h][j]
            states[h] = st
            befores[h] = jnp.stack(before).astype(BF16)
        outs = {}
        for pair in range(2):
            both = jnp.concatenate([befores[2 * pair], befores[2 * pair + 1]], axis=1)
            o_inter = _bnt(qds[pair], both)
            for hh in range(2):
                h = 2 * pair + hh
                outs[h] = (o_intra[h] + o_inter[:, :, hh * GLA_DV : (hh + 1) * GLA_DV]).reshape(tg, GLA_DV)
        for h in range(GLA_HEADS):
            o_ref[:, pl.ds(h * GLA_DV, GLA_DV)] = outs[h]
            st_ref[:, h] = befores[h]
            carry[h] = states[h]

    nchunks = seq // GLA_CHUNK
    return pl.pallas_call(
        body,
        name="gla_fwd_rev" if reverse else "gla_fwd",
        grid=(nt,),
        in_specs=[
            pl.BlockSpec((tg, KEY_W), lambda i: (tile(i), COL_Q // KEY_W)),
            pl.BlockSpec((tg, KEY_W), lambda i: (tile(i), COL_K // KEY_W)),
            pl.BlockSpec((tg, GLA_W), lambda i: (tile(i), COL_V // GLA_W)),
            pl.BlockSpec((tg, LANES), lambda i: (tile(i), COL_LR // LANES)),
            _resident((LANES, KEY_W)),
            _resident((1, KEY_W)),
            _resident(TOKEN_SHAPE),
        ],
        out_specs=[
            pl.BlockSpec((tg, GLA_W), lambda i: (tile(i), 0)),
            pl.BlockSpec((n, GLA_HEADS, GLA_DV, LANES), lambda i: (tile(i), 0, 0, 0)),
        ],
        out_shape=[
            jax.ShapeDtypeStruct((seq, GLA_W), F32),
            jax.ShapeDtypeStruct((nchunks, GLA_HEADS, GLA_DV, LANES), BF16),
        ],
        scratch_shapes=[pltpu.VMEM((GLA_HEADS, GLA_DV, LANES), F32)],
        compiler_params=_params(48),
    )(p, p, p, p, wd_pad, bd, token)


def _mixer_out(x, o_f, o_b, p, gn, lng, lnb, ws_bf, bs_col, w_out, token):
    seq = x.shape[0]
    tm = min(seq, 512)

    def body(x_ref, of_ref, ob_ref, g_ref, u_ref, vv_ref, gn_ref, lng_ref, lnb_ref, ws_ref, bs_ref, wo_ref, token_ref, x1_ref, yc_ref, vn_sc):
        for h in range(GLA_HEADS):
            cols = pl.ds(h * GLA_DV, GLA_DV)
            oh = of_ref[:, cols] + ob_ref[:, cols]
            on = oh * lax.rsqrt(jnp.mean(oh * oh, axis=-1, keepdims=True) + EPS)
            gh = g_ref[:, cols]
            yc_ref[:, cols] = (on * gn_ref[:, cols] * (gh * _sigmoid(gh))).astype(BF16)
        zv = _gelu(vv_ref[...])
        xc = zv - jnp.mean(zv, axis=-1, keepdims=True)
        vhat = xc * lax.rsqrt(jnp.mean(xc * xc, axis=-1, keepdims=True) + EPS)
        vn_sc[...] = (vhat * lng_ref[...] + lnb_ref[...]).astype(BF16)
        for c in range(tm // GMLP_CHUNK):
            rows = pl.ds(c * GMLP_CHUNK, GMLP_CHUNK)
            for g in range(GMLP_GROUPS):
                cols = pl.ds(g * LANES, LANES)
                s = _nn(ws_ref[g], vn_sc[rows, cols]) + bs_ref[g]
                yc_ref[rows, pl.ds(GLA_W + g * LANES, LANES)] = (_gelu(u_ref[rows, cols]) * s).astype(BF16)
        x1_ref[...] = x_ref[...] + _nn(yc_ref[...], wo_ref[...])

    row = lambda w: pl.BlockSpec((tm, w), lambda i: (i, 0))
    pcol = lambda col: pl.BlockSpec((tm, GLA_W), lambda i: (i, col // GLA_W))
    return pl.pallas_call(
        body,
        name="mixer_out",
        grid=(seq // tm,),
        in_specs=[
            row(D_MODEL), row(GLA_W), row(GLA_W), pcol(COL_G), pcol(COL_U), pcol(COL_VV),
            _resident((1, GLA_W)), _resident((1, GMLP_W)), _resident((1, GMLP_W)),
            _resident((GMLP_GROUPS, GMLP_CHUNK, GMLP_CHUNK)), _resident((GMLP_GROUPS, GMLP_CHUNK, 1)),
            _resident((D_MODEL, D_MODEL)), _resident(TOKEN_SHAPE),
        ],
        out_specs=[row(D_MODEL), row(D_MODEL)],
        out_shape=[jax.ShapeDtypeStruct((seq, D_MODEL), F32), jax.ShapeDtypeStruct((seq, D_MODEL), BF16)],
        scratch_shapes=[pltpu.VMEM((tm, GMLP_W), BF16)],
        compiler_params=_params(48, ("parallel",)),
    )(x, o_f, o_b, p, p, p, gn, lng, lnb, ws_bf, bs_col, w_out, token)


def _ffn_fwd(x1, target, g2, gf, wg_t, wu_t, wd):
    seq = x1.shape[0]
    tm = min(seq, 256)

    def body(x1_ref, t_ref, g2_ref, gf_ref, wg_ref, wu_ref, wd_ref, h2_ref, gate_ref, up_ref, act_ref, dx2_ref, loss_ref, dgf_ref):
        @pl.when(pl.program_id(0) == 0)
        def _():
            loss_ref[...] = jnp.zeros_like(loss_ref)
            dgf_ref[...] = jnp.zeros_like(dgf_ref)

        x1v = x1_ref[...]
        h2 = (x1v * lax.rsqrt(jnp.mean(x1v * x1v, axis=-1, keepdims=True) + EPS) * g2_ref[...]).astype(BF16)
        h2_ref[...] = h2
        gate = _nt(h2, wg_ref[...])
        up = _nt(h2, wu_ref[...])
        act = (gate * _sigmoid(gate) * up).astype(BF16)
        gate_ref[...] = gate
        up_ref[...] = up
        act_ref[...] = act
        x2 = x1v + _nn(act, wd_ref[...])
        rf = lax.rsqrt(jnp.mean(x2 * x2, axis=-1, keepdims=True) + EPS)
        xh = x2 * rf
        err = xh * gf_ref[...] - t_ref[...]
        loss_ref[...] += 0.5 * jnp.sum(jnp.mean(err * err, axis=-1, keepdims=True))
        dy = err * (1.0 / D_MODEL)
        dgf_ref[...] += jnp.sum(dy * xh, axis=0, keepdims=True)
        dx2_ref[...] = _rms_bwd(dy * gf_ref[...], xh, rf)

    row = lambda w: pl.BlockSpec((tm, w), lambda i: (i, 0))
    weight = _resident((D_FF, D_MODEL))
    return pl.pallas_call(
        body,
        name="ffn_fwd",
        grid=(seq // tm,),
        in_specs=[row(D_MODEL), row(D_MODEL), _resident((1, D_MODEL)), _resident((1, D_MODEL)), weight, weight, weight],
        out_specs=[row(D_MODEL), row(D_FF), row(D_FF), row(D_FF), row(D_MODEL),
                   pl.BlockSpec((1, LANES), lambda i: (0, 0)), pl.BlockSpec((1, D_MODEL), lambda i: (0, 0))],
        out_shape=[
            jax.ShapeDtypeStruct((seq, D_MODEL), BF16),
            jax.ShapeDtypeStruct((seq, D_FF), F32),
            jax.ShapeDtypeStruct((seq, D_FF), F32),
            jax.ShapeDtypeStruct((seq, D_FF), BF16),
            jax.ShapeDtypeStruct((seq, D_MODEL), F32),
            jax.ShapeDtypeStruct((1, LANES), F32),
            jax.ShapeDtypeStruct((1, D_MODEL), F32),
        ],
        compiler_params=_params(56),
    )(x1, target, g2, gf, wg_t, wu_t, wd)


def _ffn_bwd(dx2, gate, up, x1, g2, wg_t, wu_t, wd):
    seq = x1.shape[0]
    tm = min(seq, 256)

    def body(dx2_ref, gate_ref, up_ref, x1_ref, g2_ref, wg_ref, wu_ref, wd_ref, dgate_ref, dup_ref, dx1_ref, dg2_ref):
        @pl.when(pl.program_id(0) == 0)
        def _():
            dg2_ref[...] = jnp.zeros_like(dg2_ref)

        dx2v = dx2_ref[...]
        dact = _nt(dx2v.astype(BF16), wd_ref[...])
        gate = gate_ref[...]
        sg = _sigmoid(gate)
        dgate = (dact * up_ref[...] * (sg * (1.0 + gate * (1.0 - sg)))).astype(BF16)
        dup = (dact * (gate * sg)).astype(BF16)
        dgate_ref[...] = dgate
        dup_ref[...] = dup
        dh2 = _nn(dgate, wg_ref[...]) + _nn(dup, wu_ref[...])
        x1v = x1_ref[...]
        r2 = lax.rsqrt(jnp.mean(x1v * x1v, axis=-1, keepdims=True) + EPS)
        xh = x1v * r2
        dg2_ref[...] += jnp.sum(dh2 * xh, axis=0, keepdims=True)
        dx1_ref[...] = dx2v + _rms_bwd(dh2 * g2_ref[...], xh, r2)

    row = lambda w: pl.BlockSpec((tm, w), lambda i: (i, 0))
    weight = _resident((D_FF, D_MODEL))
    return pl.pallas_call(
        body,
        name="ffn_bwd",
        grid=(seq // tm,),
        in_specs=[row(D_MODEL), row(D_FF), row(D_FF), row(D_MODEL), _resident((1, D_MODEL)), weight, weight, weight],
        out_specs=[row(D_FF), row(D_FF), row(D_MODEL), pl.BlockSpec((1, D_MODEL), lambda i: (0, 0))],
        out_shape=[
            jax.ShapeDtypeStruct((seq, D_FF), BF16),
            jax.ShapeDtypeStruct((seq, D_FF), BF16),
            jax.ShapeDtypeStruct((seq, D_MODEL), F32),
            jax.ShapeDtypeStruct((1, D_MODEL), F32),
        ],
        compiler_params=_params(56),
    )(dx2, gate, up, x1, g2, wg_t, wu_t, wd)


WGRAD_ROWS = D_FF // 2


def _ffn_wgrad(h2, dgate, dup, act, dx2):
    seq = h2.shape[0]
    tm = min(seq, 512)

    def body(h2_ref, dgate_ref, dup_ref, act_ref, dx2_ref, dwg_ref, dwu_ref, dwd_ref):
        @pl.when(pl.program_id(1) == 0)
        def _():
            dwg_ref[...] = jnp.zeros_like(dwg_ref)
            dwu_ref[...] = jnp.zeros_like(dwu_ref)
            dwd_ref[...] = jnp.zeros_like(dwd_ref)

        h2v = h2_ref[...]
        dwg_ref[...] += _tn(dgate_ref[...], h2v)
        dwu_ref[...] += _tn(dup_ref[...], h2v)
        dwd_ref[...] += _tn(act_ref[...], dx2_ref[...].astype(BF16))

    ff = pl.BlockSpec((tm, WGRAD_ROWS), lambda j, i: (i, j))
    row = pl.BlockSpec((tm, D_MODEL), lambda j, i: (i, 0))
    out = pl.BlockSpec((WGRAD_ROWS, D_MODEL), lambda j, i: (j, 0))
    return pl.pallas_call(
        body,
        name="ffn_wgrad",
        grid=(D_FF // WGRAD_ROWS, seq // tm),
        in_specs=[row, ff, ff, ff, row],
        out_specs=[out, out, out],
        out_shape=[jax.ShapeDtypeStruct((D_FF, D_MODEL), F32)] * 3,
        compiler_params=_params(56, ("parallel", "arbitrary")),
    )(h2, dgate, dup, act, dx2)


def _mixer_bwd(dx1, ycat, o_f, o_b, p, gn, lng, lnb, ws_bf, wst_bf, bs_col, w_out, token):
    seq = dx1.shape[0]
    tm = min(seq, 512)
    nsteps = seq // tm

    def body(dx1_ref, yc_ref, of_ref, ob_ref, g_ref, u_ref, vv_ref, gn_ref, lng_ref, lnb_ref, ws_ref, wst_ref, bs_ref, wo_ref, token_ref,
             do_ref, dg_ref, du_ref, dvv_ref, dwo_ref, dgn_ref, dlng_ref, dlnb_ref, dws_ref, dbs_ref, vn_sc, dvn_sc, dbs_acc):
        step = pl.program_id(0)

        @pl.when(step == 0)
        def _():
            for r in (dwo_ref, dgn_ref, dlng_ref, dlnb_ref, dws_ref, dbs_acc):
                r[...] = jnp.zeros_like(r)

        dx1b = dx1_ref[...].astype(BF16)
        dyc = _nt(dx1b, wo_ref[...])
        dwo_ref[...] += _tn(yc_ref[...], dx1b)
        for h in range(GLA_HEADS):
            cols = pl.ds(h * GLA_DV, GLA_DV)
            dya = dyc[:, h * GLA_DV : (h + 1) * GLA_DV]
            oh = of_ref[:, cols] + ob_ref[:, cols]
            rn = lax.rsqrt(jnp.mean(oh * oh, axis=-1, keepdims=True) + EPS)
            on = oh * rn
            gh = g_ref[:, cols]
            sg = _sigmoid(gh)
            sil = gh * sg
            gnh = gn_ref[:, cols]
            dgn_ref[:, cols] += jnp.sum(dya * on * sil, axis=0, keepdims=True)
            dg_ref[:, cols] = (dya * on * gnh * (sg * (1.0 + gh * (1.0 - sg)))).astype(BF16)
            do_ref[:, cols] = _rms_bwd(dya * gnh * sil, on, rn)
        vv = vv_ref[...]
        zv, zv_grad = _gelu_and_grad(vv)
        xc = zv - jnp.mean(zv, axis=-1, keepdims=True)
        rstd = lax.rsqrt(jnp.mean(xc * xc, axis=-1, keepdims=True) + EPS)
        vhat = xc * rstd
        vn_sc[...] = (vhat * lng_ref[...] + lnb_ref[...]).astype(BF16)
        for c in range(tm // GMLP_CHUNK):
            rows = pl.ds(c * GMLP_CHUNK, GMLP_CHUNK)
            for g in range(GMLP_GROUPS):
                cols = pl.ds(g * LANES, LANES)
                vn = vn_sc[rows, cols]
                s = _nn(ws_ref[g], vn) + bs_ref[g]
                dyb = dyc[c * GMLP_CHUNK : (c + 1) * GMLP_CHUNK, GLA_W + g * LANES : GLA_W + (g + 1) * LANES]
                zu, zu_grad = _gelu_and_grad(u_ref[rows, cols])
                du_ref[rows, cols] = (dyb * s * zu_grad).astype(BF16)
                ds = dyb * zu
                dbs_acc[g] += ds
                dsb = ds.astype(BF16)
                dws_ref[g] += _nt(dsb, vn)
                dvn_sc[rows, cols] = _nn(wst_ref[g], dsb)
        dvn = dvn_sc[...]
        dlng_ref[...] += jnp.sum(dvn * vhat, axis=0, keepdims=True)
        dlnb_ref[...] += jnp.sum(dvn, axis=0, keepdims=True)
        dvh = dvn * lng_ref[...]
        dzv = rstd * (dvh - jnp.mean(dvh, axis=-1, keepdims=True) - vhat * jnp.mean(dvh * vhat, axis=-1, keepdims=True))
        dvv_ref[...] = (dzv * zv_grad).astype(BF16)

        @pl.when(step == nsteps - 1)
        def _():
            dbs_ref[...] = jnp.sum(dbs_acc[...], axis=-1, keepdims=True)

    row = lambda w: pl.BlockSpec((tm, w), lambda i: (i, 0))
    pcol = lambda col: pl.BlockSpec((tm, GLA_W), lambda i: (i, col // GLA_W))
    const = lambda shape: pl.BlockSpec(shape, lambda i: (0,) * len(shape))
    return pl.pallas_call(
        body,
        name="mixer_bwd",
        grid=(nsteps,),
        in_specs=[
            row(D_MODEL), row(D_MODEL), row(GLA_W), row(GLA_W), pcol(COL_G), pcol(COL_U), pcol(COL_VV),
            _resident((1, GLA_W)), _resident((1, GMLP_W)), _resident((1, GMLP_W)),
            _resident((GMLP_GROUPS, GMLP_CHUNK, GMLP_CHUNK)), _resident((GMLP_GROUPS, GMLP_CHUNK, GMLP_CHUNK)),
            _resident((GMLP_GROUPS, GMLP_CHUNK, 1)), _resident((D_MODEL, D_MODEL)), _resident(TOKEN_SHAPE),
        ],
        out_specs=[
            row(GLA_W), row(GLA_W), row(GMLP_W), row(GMLP_W), const((D_MODEL, D_MODEL)),
            const((1, GLA_W)), const((1, GMLP_W)), const((1, GMLP_W)),
            const((GMLP_GROUPS, GMLP_CHUNK, GMLP_CHUNK)), const((GMLP_GROUPS, GMLP_CHUNK, 1)),
        ],
        out_shape=[
            jax.ShapeDtypeStruct((seq, GLA_W), F32), jax.ShapeDtypeStruct((seq, GLA_W), BF16),
            jax.ShapeDtypeStruct((seq, GMLP_W), BF16), jax.ShapeDtypeStruct((seq, GMLP_W), BF16),
            jax.ShapeDtypeStruct((D_MODEL, D_MODEL), F32),
            jax.ShapeDtypeStruct((1, GLA_W), F32), jax.ShapeDtypeStruct((1, GMLP_W), F32), jax.ShapeDtypeStruct((1, GMLP_W), F32),
            jax.ShapeDtypeStruct((GMLP_GROUPS, GMLP_CHUNK, GMLP_CHUNK), F32), jax.ShapeDtypeStruct((GMLP_GROUPS, GMLP_CHUNK, 1), F32),
        ],
        scratch_shapes=[pltpu.VMEM((tm, GMLP_W), BF16), pltpu.VMEM((tm, GMLP_W), F32), pltpu.VMEM((GMLP_GROUPS, GMLP_CHUNK, GMLP_CHUNK), F32)],
        compiler_params=_params(56),
    )(dx1, ycat, o_f, o_b, p, p, p, gn, lng, lnb, ws_bf, wst_bf, bs_col, w_out, token)


def _gla_bwd(p, do, st, wd_pad, bd, token, reverse, other=None):
    seq = p.shape[0]
    tg = _gla_tile(seq)
    nt = seq // tg
    n = tg // GLA_CHUNK
    scale = GLA_DK**-0.5

    def tile(i):
        return i if reverse else nt - 1 - i

    def body(q_ref, k_ref, v_ref, lr_ref, do_ref, st_ref, wd_ref, bd_ref, token_ref, *rest):
        others, (dq_ref, dk_ref, dv_ref, dlr_ref, dwd_ref, dbd_ref, carry) = rest[:-7], rest[-7:]
        if others:
            odq_ref, odk_ref, odv_ref, odlr_ref = others

            def put(ref, idx, val, oref):
                ref[idx] = (val + oref[idx]).astype(BF16)
        else:
            odq_ref = odk_ref = odv_ref = odlr_ref = None

            def put(ref, idx, val, oref):
                ref[idx] = val

        @pl.when(pl.program_id(0) == 0)
        def _():
            carry[...] = jnp.zeros_like(carry)
            dwd_ref[...] = jnp.zeros_like(dwd_ref)
            dbd_ref[...] = jnp.zeros_like(dbd_ref)

        lr_bf = lr_ref[...].astype(BF16)
        carries = [carry[h] for h in range(GLA_HEADS)]
        row_in_chunk = lax.broadcasted_iota(jnp.int32, (tg, LANES), 0) % GLA_CHUNK
        lane_head = lax.broadcasted_iota(jnp.int32, (1, LANES), 1) // GLA_DK
        tt = lax.broadcasted_iota(jnp.int32, (GLA_CHUNK, GLA_CHUNK), 0)
        ss = lax.broadcasted_iota(jnp.int32, (GLA_CHUNK, GLA_CHUNK), 1)
        causal = (tt <= ss) if reverse else (tt >= ss)
        order = range(n) if reverse else range(n - 1, -1, -1)
        dlr = jnp.zeros((tg, LANES), F32)
        heads = range(GLA_HEADS)
        pv, masks, qdh, vhs, dohs, stbs = {}, {}, {}, {}, {}, {}
        sc_raw, dp, acc = {}, {}, {}
        for pair in range(2):
            cols = pl.ds(pair * LANES, LANES)
            pre, b3, blast = _gla_decay_terms(lr_bf, wd_ref, bd_ref, pair, row_in_chunk, reverse, n)
            q3 = q_ref[:, cols].reshape(n, GLA_CHUNK, LANES) * scale
            k3 = k_ref[:, cols].reshape(n, GLA_CHUNK, LANES)
            eb = jnp.exp(b3)
            emb = jnp.exp(-b3)
            ekte = jnp.exp(blast - b3)
            kdf = k3 * emb
            kte = k3 * ekte
            both = pl.ds(2 * pair * GLA_DV, 2 * GLA_DV)
            pv[pair] = dict(pre=pre, eb=eb, emb=emb, ekte=ekte, qd=q3 * eb, kdf=kdf, kd=kdf.astype(BF16), kte=kte, kte_bf=kte.astype(BF16),
                            dec=jnp.exp(blast), v=v_ref[:, both].reshape(n, GLA_CHUNK, 2 * GLA_DV).astype(BF16),
                            do=do_ref[:, both].reshape(n, GLA_CHUNK, 2 * GLA_DV).astype(BF16))
            for hh in range(2):
                h = 2 * pair + hh
                masks[h] = (lane_head == hh).astype(F32)
                qdh[h] = (pv[pair]["qd"] * masks[h]).astype(BF16)
                vhs[h] = pv[pair]["v"][:, :, hh * GLA_DV : (hh + 1) * GLA_DV]
                dohs[h] = pv[pair]["do"][:, :, hh * GLA_DV : (hh + 1) * GLA_DV]
                stbs[h] = st_ref[:, h]
                dp[h] = _bnt(dohs[h], vhs[h])
                acc[h] = _btn(dohs[h], qdh[h])
            sc_both = _bnt(jnp.concatenate([qdh[2 * pair], qdh[2 * pair + 1]], axis=1), pv[pair]["kd"])
            for hh in range(2):
                sc_raw[2 * pair + hh] = sc_both[:, hh * GLA_CHUNK : (hh + 1) * GLA_CHUNK, :]
        dsa, sc = {}, {}
        for h in heads:
            sc[h] = jnp.where(causal, sc_raw[h], 0.0).astype(BF16)
            dp[h] = jnp.where(causal, dp[h], 0.0).astype(BF16)
            dec = pv[h // 2]["dec"]
            c, after = carries[h], [None] * n
            for j in order:
                after[j] = c
                c = acc[h][j] + dec[j] * c
            carries[h] = c
            dsa[h] = jnp.stack(after)
        dvs, dqs, dks, dwds, dbds = [], [], [], [], []
        for pair in range(2):
            cols = pl.ds(pair * LANES, LANES)
            v = pv[pair]
            h0, h1 = 2 * pair, 2 * pair + 1
            dsa_both = jnp.concatenate([dsa[h0], dsa[h1]], axis=1)
            dsa_bf = dsa_both.astype(BF16)
            stb_bf = jnp.concatenate([stbs[h0], stbs[h1]], axis=1)
            dq_intra = _bnn(jnp.concatenate([dp[h0], dp[h1]], axis=1), v["kd"])
            dqd = (dq_intra[:, :GLA_CHUNK, :] * masks[h0] + dq_intra[:, GLA_CHUNK:, :] * masks[h1]) + _bnn(v["do"], stb_bf)
            dkd = _btn(dp[h0], qdh[h0]) + _btn(dp[h1], qdh[h1])
            dkte = _bnn(v["v"], dsa_bf)
            ddec = jnp.sum(dsa[h0] * stbs[h0].astype(F32) + dsa[h1] * stbs[h1].astype(F32), axis=1, keepdims=True)
            dv_inter = _bnt(v["kte_bf"], dsa_bf)
            for hh, h in ((0, h0), (1, h1)):
                dvs.append((_btn(sc[h], dohs[h]) + dv_inter[:, :, hh * GLA_DV : (hh + 1) * GLA_DV]).reshape(tg, GLA_DV))
            dqs.append((dqd * (scale * v["eb"])).reshape(tg, LANES))
            dks.append((dkd * v["emb"] + dkte * v["ekte"]).reshape(tg, LANES))
            db = dqd * v["qd"] - dkd * v["kdf"] - dkte * v["kte"]
            dblast = jnp.sum(dkte * v["kte"], axis=1, keepdims=True) + ddec * v["dec"]
            dla = _chunk_cumsum(db.reshape(tg, LANES), row_in_chunk, not reverse) + jnp.broadcast_to(dblast, (n, GLA_CHUNK, LANES)).reshape(tg, LANES)
            dpre = (dla * (1.0 / GLA_TAU) * _sigmoid(-v["pre"]))
            dpre_bf = dpre.astype(BF16)
            dlr = dlr + _nt(dpre_bf, wd_ref[:, cols])
            dwds.append(_tn(lr_bf, dpre_bf))
            dbds.append(jnp.sum(dpre, axis=0, keepdims=True))
        put(dlr_ref, (slice(None), slice(None)), dlr, odlr_ref)
        for pair in range(2):
            cols = pl.ds(pair * LANES, LANES)
            put(dq_ref, (slice(None), cols), dqs[pair], odq_ref)
            put(dk_ref, (slice(None), cols), dks[pair], odk_ref)
            dwd_ref[:, cols] += dwds[pair]
            dbd_ref[:, cols] += dbds[pair]
        for h in range(GLA_HEADS):
            put(dv_ref, (slice(None), pl.ds(h * GLA_DV, GLA_DV)), dvs[h], odv_ref)
            carry[h] = carries[h]

    pieces = [
        pl.BlockSpec((tg, KEY_W), lambda i: (tile(i), 0)),
        pl.BlockSpec((tg, KEY_W), lambda i: (tile(i), 0)),
        pl.BlockSpec((tg, GLA_W), lambda i: (tile(i), 0)),
        pl.BlockSpec((tg, LANES), lambda i: (tile(i), 0)),
    ]
    piece_dtype = BF16 if other else F32
    return pl.pallas_call(
        body,
        name="gla_bwd_rev" if reverse else "gla_bwd",
        grid=(nt,),
        in_specs=[
            pl.BlockSpec((tg, KEY_W), lambda i: (tile(i), COL_Q // KEY_W)),
            pl.BlockSpec((tg, KEY_W), lambda i: (tile(i), COL_K // KEY_W)),
            pl.BlockSpec((tg, GLA_W), lambda i: (tile(i), COL_V // GLA_W)),
            pl.BlockSpec((tg, LANES), lambda i: (tile(i), COL_LR // LANES)),
            pl.BlockSpec((tg, GLA_W), lambda i: (tile(i), 0)),
            pl.BlockSpec((n, GLA_HEADS, GLA_DV, LANES), lambda i: (tile(i), 0, 0, 0)),
            _resident((LANES, KEY_W)),
            _resident((1, KEY_W)),
            _resident(TOKEN_SHAPE),
        ] + (pieces if other else []),
        out_specs=pieces + [pl.BlockSpec((LANES, KEY_W), lambda i: (0, 0)), pl.BlockSpec((1, KEY_W), lambda i: (0, 0))],
        out_shape=[
            jax.ShapeDtypeStruct((seq, KEY_W), piece_dtype), jax.ShapeDtypeStruct((seq, KEY_W), piece_dtype),
            jax.ShapeDtypeStruct((seq, GLA_W), piece_dtype), jax.ShapeDtypeStruct((seq, LANES), piece_dtype),
            jax.ShapeDtypeStruct((LANES, KEY_W), F32), jax.ShapeDtypeStruct((1, KEY_W), F32),
        ],
        scratch_shapes=[pltpu.VMEM((GLA_HEADS, GLA_DV, LANES), F32)],
        compiler_params=_params(56),
    )(p, p, p, p, do, st, wd_pad, bd, token, *(other or ()))


def _inproj_wgrad(x, g1, dq, dk, dv, dg, du, dvv, dlr):
    seq = x.shape[0]
    tm = min(seq, 512)

    def body(x_ref, g1_ref, dq_ref, dk_ref, dv_ref, dg_ref, du_ref, dvv_ref, dlr_ref, dw_ref, dp_ref):
        @pl.when(pl.program_id(0) == 0)
        def _():
            dw_ref[...] = jnp.zeros_like(dw_ref)

        for col, ref in ((COL_Q, dq_ref), (COL_K, dk_ref), (COL_V, dv_ref), (COL_G, dg_ref), (COL_U, du_ref), (COL_VV, dvv_ref), (COL_LR, dlr_ref)):
            dp_ref[:, col : col + ref.shape[1]] = ref[...]
        xv = x_ref[...]
        h = (xv * lax.rsqrt(jnp.mean(xv * xv, axis=-1, keepdims=True) + EPS) * g1_ref[...]).astype(BF16)
        dw_ref[0:ROW_LR, :] += _tn(dp_ref[:, 0:COL_U], h)
        dw_ref[ROW_UV:PROJ_W, :] += _tn(dp_ref[:, COL_U:COL_LR], h)
        dw_ref[ROW_LR:ROW_UV, :] += _tn(dp_ref[:, COL_LR:PROJ_WP], h)[0 : ROW_UV - ROW_LR]

    row = lambda w: pl.BlockSpec((tm, w), lambda i: (i, 0))
    return pl.pallas_call(
        body,
        name="inproj_wgrad",
        grid=(seq // tm,),
        in_specs=[row(D_MODEL), _resident((1, D_MODEL)), row(KEY_W), row(KEY_W), row(GLA_W), row(GLA_W), row(GMLP_W), row(GMLP_W), row(LANES)],
        out_specs=[pl.BlockSpec((PROJ_W, D_MODEL), lambda i: (0, 0)), row(PROJ_WP)],
        out_shape=[jax.ShapeDtypeStruct((PROJ_W, D_MODEL), F32), jax.ShapeDtypeStruct((seq, PROJ_WP), BF16)],
        compiler_params=_params(56),
    )(x, g1, dq, dk, dv, dg, du, dvv, dlr)


def _inproj_dx(x, dx1, g1, w_in_t, dp, token):
    seq = x.shape[0]
    tm = min(seq, 512)

    def body(x_ref, dx1_ref, g1_ref, w_ref, dp_ref, token_ref, dx_ref, dg1_ref):
        @pl.when(pl.program_id(0) == 0)
        def _():
            dg1_ref[...] = jnp.zeros_like(dg1_ref)

        xv = x_ref[...]
        r1 = lax.rsqrt(jnp.mean(xv * xv, axis=-1, keepdims=True) + EPS)
        xh = xv * r1
        dh = (_nn(dp_ref[:, 0:COL_U], w_ref[0:ROW_LR, :]) + _nn(dp_ref[:, COL_U:COL_LR], w_ref[ROW_UV:PROJ_W, :])
              + _nn(dp_ref[:, COL_LR:PROJ_WP], w_ref[ROW_LR : ROW_LR + LANES, :]))
        dg1_ref[...] += jnp.sum(dh * xh, axis=0, keepdims=True)
        dx_ref[...] = dx1_ref[...] + _rms_bwd(dh * g1_ref[...], xh, r1)

    row = lambda w: pl.BlockSpec((tm, w), lambda i: (i, 0))
    return pl.pallas_call(
        body,
        name="inproj_dx",
        grid=(seq // tm,),
        in_specs=[row(D_MODEL), row(D_MODEL), _resident((1, D_MODEL)), _resident((PROJ_W, D_MODEL)), row(PROJ_WP), _resident(TOKEN_SHAPE)],
        out_specs=[row(D_MODEL), pl.BlockSpec((1, D_MODEL), lambda i: (0, 0))],
        out_shape=[jax.ShapeDtypeStruct((seq, D_MODEL), F32), jax.ShapeDtypeStruct((1, D_MODEL), F32)],
        compiler_params=_params(48),
    )(x, dx1, g1, w_in_t, dp, token)


def _in_hbm(a):
    return pltpu.with_memory_space_constraint(a, pltpu.HBM)


def _row_tile(rows, multiple=8):
    for t in range(min(rows, 512), 0, -1):
        if rows % t == 0 and t % multiple == 0:
            return t
    return rows


def _cast_into_slot(w, shard, token):
    rows, cols = w.shape
    tr = _row_tile(rows, 16)

    def body(s_ref, w_ref, token_ref, o_ref):
        o_ref[...] = w_ref[...].astype(BF16)

    return pl.pallas_call(
        body,
        name="cast_into_slot",
        grid_spec=pltpu.PrefetchScalarGridSpec(
            num_scalar_prefetch=1,
            grid=(rows // tr,),
            in_specs=[pl.BlockSpec((tr, cols), lambda i, s_ref: (i, 0)), pl.BlockSpec(TOKEN_SHAPE, lambda i, s_ref: (0, 0))],
            out_specs=pl.BlockSpec((None, tr, cols), lambda i, s_ref: (s_ref[0], i, 0)),
        ),
        out_shape=pltpu.HBM((N_SHARDS, rows, cols), BF16),
        compiler_params=_params(32, ("parallel",)),
    )(shard, _in_hbm(w), token)


def _add_halves(grads4, recvs, shard_core):
    n = len(grads4)
    _, rows, _ = grads4[0].shape
    tr = _row_tile(rows, 16)

    def body(sc_ref, *refs):
        for k in range(n):
            total = refs[k][...] + refs[n + k][...]
            refs[3 * n + k][...] = total.astype(BF16)

            @pl.when(pl.program_id(1) == sc_ref[0])
            def _(k=k, total=total):
                refs[2 * n + k][...] = total

    theirs = pl.BlockSpec((None, tr, HALF), lambda i, s, sc_ref: (s, i, 0))
    mine = pl.BlockSpec((None, tr, HALF), lambda i, s, sc_ref: (s, i, sc_ref[1]))
    kept = pl.BlockSpec((tr, HALF), lambda i, s, sc_ref: (i, 0))
    outs = pl.pallas_call(
        body,
        name="add_halves",
        grid_spec=pltpu.PrefetchScalarGridSpec(
            num_scalar_prefetch=1,
            grid=(rows // tr, N_SHARDS),
            in_specs=[mine] * n + [theirs] * n,
            out_specs=[kept] * n + [theirs] * n,
        ),
        out_shape=[pltpu.HBM((rows, HALF), F32)] * n + [pltpu.HBM((N_SHARDS, rows, HALF), BF16)] * n,
        compiler_params=_params(48, ("parallel", "arbitrary")),
    )(shard_core, *[_in_hbm(a) for a in list(grads4) + list(recvs)])
    return list(zip(outs[:n], outs[n:]))


def _add_partials(part, recv3, shard_core):
    rows, _ = part.shape
    tr = _row_tile(rows, 16)

    def body(sc_ref, p_ref, r_ref, o_ref):
        o_ref[...] = ((p_ref[...] + r_ref[0].astype(F32)) + r_ref[1].astype(F32)) + r_ref[2].astype(F32)

    return pl.pallas_call(
        body,
        name="add_partials",
        grid_spec=pltpu.PrefetchScalarGridSpec(
            num_scalar_prefetch=1,
            grid=(rows // tr,),
            in_specs=[
                pl.BlockSpec((tr, HALF), lambda i, sc_ref: (i, 0)),
                pl.BlockSpec((3, tr, HALF), lambda i, sc_ref: (0, i, 0)),
            ],
            out_specs=pl.BlockSpec((tr, HALF), lambda i, sc_ref: (i, sc_ref[1])),
        ),
        out_shape=pltpu.HBM((rows, 2 * HALF), F32),
        compiler_params=_params(32, ("parallel",)),
    )(shard_core, _in_hbm(part), _in_hbm(recv3))


def _adam_math(w, g, m, v):
    m = ADAM_B1 * m + (1.0 - ADAM_B1) * g
    v = ADAM_B2 * v + (1.0 - ADAM_B2) * (g * g)
    m_hat = m / (1.0 - ADAM_B1**ADAM_STEP)
    v_hat = v / (1.0 - ADAM_B2**ADAM_STEP)
    delta = -ADAM_LR * (m_hat / (jnp.sqrt(v_hat) + ADAM_EPS) + ADAM_WD * w)
    return delta, m, v


def _adamw(w, g, m, v):
    rows, cols = w.shape
    tr = _row_tile(rows)

    def body(w_ref, g_ref, m_ref, v_ref, go_ref, d_ref, mo_ref, vo_ref):
        gv = g_ref[...]
        go_ref[...] = gv
        d_ref[...], mo_ref[...], vo_ref[...] = _adam_math(w_ref[...], gv, m_ref[...], v_ref[...])

    spec = pl.BlockSpec((tr, cols), lambda i: (i, 0))
    return pl.pallas_call(
        body, name="adamw", grid=(rows // tr,), in_specs=[spec] * 4, out_specs=[spec] * 4, out_shape=[pltpu.HBM(w.shape, F32)] * 4,
        compiler_params=_params(32, ("parallel",)),
    )(_in_hbm(w), _in_hbm(g), _in_hbm(m), _in_hbm(v))


SMALL_ROWS = 560
DECAY_ROWS = 8
SMALL_TOTAL = SMALL_ROWS + 2 * N_SHARDS * DECAY_ROWS


def _adamw_small(gathered, own, wp, mp, vp):
    out_rows = SMALL_ROWS + 2 * DECAY_ROWS

    def body(ga_ref, own_ref, w_ref, m_ref, v_ref, g_ref, d_ref, mo_ref, vo_ref):
        x, y, c = _position()
        shard, me = 2 * x + y, 4 * x + 2 * y + c
        total = lambda rows: functools.reduce(lambda a, b: a + b, [jnp.where(me == d, own_ref[rows, :], ga_ref[d, rows, :]) for d in range(8)])
        g_ref[pl.ds(0, SMALL_ROWS), :] = total(pl.ds(0, SMALL_ROWS))
        for k in range(2):
            start = pl.multiple_of(SMALL_ROWS + k * N_SHARDS * DECAY_ROWS + shard * DECAY_ROWS, DECAY_ROWS)
            g_ref[pl.ds(SMALL_ROWS + k * DECAY_ROWS, DECAY_ROWS), :] = total(pl.ds(start, DECAY_ROWS))
        d_ref[...], mo_ref[...], vo_ref[...] = _adam_math(w_ref[...], g_ref[...], m_ref[...], v_ref[...])

    shape = jax.ShapeDtypeStruct((out_rows, LANES), F32)
    return pl.pallas_call(body, name="adamw_small", out_shape=[shape] * 4, compiler_params=_params(32, None))(gathered, own, wp, mp, vp)


ANY = pl.BlockSpec(memory_space=pl.ANY)


def _position():
    return lax.axis_index("x"), lax.axis_index("y"), lax.axis_index("c")


def _other_chips(x, y):
    return [(1 - x, y), (x, 1 - y), (1 - x, 1 - y)]


HBM = pl.BlockSpec(memory_space=pltpu.HBM)
SEM = pl.BlockSpec(memory_space=pltpu.SEMAPHORE)
TOKEN = jax.ShapeDtypeStruct(TOKEN_SHAPE, F32)
DATAFLOW = pltpu.SideEffectType.DATAFLOW_SIDE_EFFECTING


def _half_block(ref4, slot, core):
    return ref4.at[slot, :, pl.ds(pl.multiple_of(core * HALF, HALF), HALF)]


def _gather_ici_copies(bufs, lands, send_sems, recv_sems):
    x, y, c = _position()
    pairs = []
    for k, ref4 in enumerate(bufs):
        mine = _half_block(ref4, 2 * x + y, c)
        for j, (px, py) in enumerate(_other_chips(x, y)):
            sems = dict(send_sem=send_sems.at[3 * k + j], recv_sem=recv_sems.at[3 * k + j], device_id=(px, py, c), device_id_type=MESH)
            pairs.append((functools.partial(pltpu.make_async_remote_copy, src_ref=mine, dst_ref=mine, **sems),
                          functools.partial(pltpu.make_async_remote_copy, src_ref=mine, dst_ref=_half_block(ref4, 2 * px + py, c), **sems)))
    return pairs


def _gather_d2d_copies(bufs, lands, send_sems, recv_sems):
    x, y, c = _position()
    pairs = []
    for k, ref4 in enumerate(bufs):
        for j, (px, py) in enumerate(_other_chips(x, y)):
            have = _half_block(ref4, 2 * px + py, c)
            sems = dict(send_sem=send_sems.at[3 * k + j], recv_sem=recv_sems.at[3 * k + j], device_id=(x, y, 1 - c), device_id_type=MESH)
            pairs.append((functools.partial(pltpu.make_async_remote_copy, src_ref=have, dst_ref=have, **sems),
                          functools.partial(pltpu.make_async_remote_copy, src_ref=have, dst_ref=_half_block(ref4, 2 * px + py, 1 - c), **sems)))
    return pairs


def _gather_forward(bufs):
    n = len(bufs)

    def body(*refs):
        outs = refs[n : 2 * n]
        send_sems, recv_sems = refs[2 * n :]
        d2d = _gather_d2d_copies(outs, (), send_sems, recv_sems)
        for forward, _ in d2d:
            forward().start()
        for forward, arrival in d2d:
            arrival().wait_recv()
            forward().wait_send()

    return pl.pallas_call(
        body,
        name="gather_forward",
        in_specs=[ANY] * n,
        out_specs=[ANY] * n,
        out_shape=[jax.ShapeDtypeStruct(b.shape, b.dtype) for b in bufs],
        input_output_aliases={k: k for k in range(n)},
        scratch_shapes=[pltpu.SemaphoreType.DMA((3 * n,)), pltpu.SemaphoreType.DMA((3 * n,))],
        compiler_params=pltpu.CompilerParams(has_side_effects=True),
    )(*bufs)


def _exchange_halves(grads4):
    n = len(grads4)

    def body(*refs):
        ins, outs = refs[:n], refs[n : 2 * n]
        send_sems, recv_sems = refs[2 * n :]
        x, y, c = _position()
        copies = []
        for k in range(n):
            cp = pltpu.make_async_remote_copy(
                src_ref=ins[k].at[:, :, pl.ds(pl.multiple_of((1 - c) * HALF, HALF), HALF)], dst_ref=outs[k],
                send_sem=send_sems.at[k], recv_sem=recv_sems.at[k], device_id=(x, y, 1 - c), device_id_type=MESH)
            cp.start()
            copies.append(cp)
        for cp in copies:
            cp.wait()

    return pl.pallas_call(
        body,
        name="exchange_halves",
        in_specs=[ANY] * n,
        out_specs=[ANY] * n,
        out_shape=[jax.ShapeDtypeStruct((N_SHARDS, g.shape[1], HALF), g.dtype) for g in grads4],
        scratch_shapes=[pltpu.SemaphoreType.DMA((n,)), pltpu.SemaphoreType.DMA((n,))],
        compiler_params=pltpu.CompilerParams(has_side_effects=True),
    )(*grads4)


def _both_ends(**copy):
    maker = functools.partial(pltpu.make_async_remote_copy, **copy)
    return maker, maker


def _scatter_copies(parts, lands, send_sems, recv_sems):
    x, y, c = _position()
    return [_both_ends(src_ref=parts[k].at[2 * px + py], dst_ref=lands[k].at[j], send_sem=send_sems.at[3 * k + j],
                       recv_sem=recv_sems.at[3 * k + j], device_id=(px, py, c), device_id_type=MESH)
            for k in range(len(parts)) for j, (px, py) in enumerate(_other_chips(x, y))]


def _exchange_copies(grads, lands, send_sems, recv_sems):
    x, y, c = _position()
    return [_both_ends(src_ref=grads[k].at[:, :, pl.ds(pl.multiple_of((1 - c) * HALF, HALF), HALF)], dst_ref=lands[k],
                       send_sem=send_sems.at[k], recv_sem=recv_sems.at[k], device_id=(x, y, 1 - c), device_id_type=MESH)
            for k in range(len(grads))]


def _exchange_lands(grads4):
    return [jax.ShapeDtypeStruct((N_SHARDS, g.shape[1], HALF), g.dtype) for g in grads4]


def _scatter_lands(parts4):
    return [jax.ShapeDtypeStruct((3,) + g.shape[1:], g.dtype) for g in parts4]


def _small_gather_copies(blocks, lands, send_sems, recv_sems):
    x, y, c = _position()
    flip = lambda v, bit: 1 - v if bit else v
    return [_both_ends(src_ref=blocks[0], dst_ref=lands[0].at[4 * x + 2 * y + c], send_sem=send_sems.at[r - 1],
                       recv_sem=recv_sems.at[r - 1], device_id=(flip(x, r & 4), flip(y, r & 2), flip(c, r & 1)), device_id_type=MESH)
            for r in range(1, 8)]


def _split_start(name, srcs, land_shapes, make_copies, nsem, after=()):
    n, nl, na = len(srcs), len(land_shapes), len(after)
    lands = [lax.empty(a.shape, a.dtype) for a in land_shapes]

    def body(*refs):
        send_sems, recv_sems = refs[n + nl + na], refs[n + nl + na + 1]
        token = refs[2 * (n + nl) + na + 2]
        for send, _ in make_copies(refs[:n], refs[n : n + nl], send_sems, recv_sems):
            send().start()
        token[...] = jnp.zeros_like(token)

    hbm = lambda a: pltpu.HBM(a.shape, a.dtype)
    out = pl.pallas_call(
        body,
        name=name,
        in_specs=[HBM] * (n + nl) + [ANY] * na,
        out_specs=(SEM, SEM, *[HBM] * (n + nl), pl.BlockSpec(memory_space=pltpu.VMEM)),
        out_shape=(pltpu.SemaphoreType.DMA((nsem,)), pltpu.SemaphoreType.DMA((nsem,)), *[hbm(a) for a in list(srcs) + lands], TOKEN),
        input_output_aliases={k: 2 + k for k in range(n + nl)},
        compiler_params=pltpu.CompilerParams(has_side_effects=DATAFLOW),
    )(*[pltpu.with_memory_space_constraint(a, pltpu.HBM) for a in list(srcs) + lands], *after)
    return out[0], out[1], list(out[2 : 2 + n]), list(out[2 + n : 2 + n + nl]), out[2 + n + nl]


def _split_wait(name, send_sems, recv_sems, srcs, lands, make_copies, after):
    n, nl = len(srcs), len(lands)

    def body(*refs):
        for send, arrival in make_copies(refs[:n], refs[n : n + nl], refs[n + nl], refs[n + nl + 1]):
            send().wait_send()
            arrival().wait_recv()

    hbm = lambda a: pltpu.HBM(a.shape, a.dtype)
    out = pl.pallas_call(
        body,
        name=name,
        in_specs=[HBM] * (n + nl) + [SEM, SEM] + [ANY] * len(after),
        out_specs=tuple([HBM] * (n + nl)),
        out_shape=tuple(hbm(a) for a in list(srcs) + list(lands)),
        input_output_aliases={k: k for k in range(n + nl)},
        compiler_params=pltpu.CompilerParams(has_side_effects=DATAFLOW),
    )(*srcs, *lands, send_sems, recv_sems, *after)
    return list(out[:n]), list(out[n:])


def _join_copies(bufs, lands, send_sems, recv_sems):
    x, y, c = _position()
    half = lambda ref, core: ref.at[:, pl.ds(pl.multiple_of(core * HALF, HALF), HALF)]
    pairs = []
    for k, ref in enumerate(bufs):
        sems = dict(send_sem=send_sems.at[k], recv_sem=recv_sems.at[k], device_id=(x, y, 1 - c), device_id_type=MESH)
        pairs.append((functools.partial(pltpu.make_async_remote_copy, src_ref=half(ref, c), dst_ref=half(ref, c), **sems),
                      functools.partial(pltpu.make_async_remote_copy, src_ref=half(ref, c), dst_ref=half(ref, 1 - c), **sems)))
    return pairs


def _join_halves(bufs, after=()):
    n, na = len(bufs), len(after)

    def body(*refs):
        joins = _join_copies(refs[n + na : 2 * n + na], (), *refs[2 * n + na :])
        for send, _ in joins:
            send().start()
        for send, arrival in joins:
            send().wait_send()
            arrival().wait_recv()

    return pl.pallas_call(
        body,
        name="join_halves",
        in_specs=[ANY] * (n + na),
        out_specs=[ANY] * n,
        out_shape=[jax.ShapeDtypeStruct(b.shape, b.dtype) for b in bufs],
        input_output_aliases={k: k for k in range(n)},
        scratch_shapes=[pltpu.SemaphoreType.DMA((n,)), pltpu.SemaphoreType.DMA((n,))],
        compiler_params=pltpu.CompilerParams(has_side_effects=True),
    )(*bufs, *after)


def _allgather_small(block):
    m_per, ncol = block.shape

    def body(x_ref, out_ref, send_sems, recv_sems, local_sem):
        x, y, c = _position()
        me, sibling = (x, y, c), (x, y, 1 - c)
        chips = _other_chips(x, y)

        def rows(px, py, pc):
            return out_ref.at[4 * px + 2 * py + pc]

        def copy(k, blk, to, src=None):
            return pltpu.make_async_remote_copy(
                src_ref=rows(*blk) if src is None else src, dst_ref=rows(*blk),
                send_sem=send_sems.at[k], recv_sem=recv_sems.at[k], device_id=to, device_id_type=MESH)

        mine = pltpu.make_async_copy(x_ref, rows(*me), local_sem)
        mine.start()
        first = [copy(0, me, sibling, src=x_ref)] + [copy(1 + j, me, (*chip, c), src=x_ref) for j, chip in enumerate(chips)]
        for cp in first:
            cp.start()
        passed = [copy(4 + j, (*chip, c), sibling) for j, chip in enumerate(chips)]
        for j, chip in enumerate(chips):
            copy(1 + j, (*chip, c), me).wait_recv()
            passed[j].start()
        copy(0, sibling, me).wait_recv()
        for j, chip in enumerate(chips):
            copy(4 + j, (*chip, 1 - c), me).wait_recv()
        for cp in first + passed:
            cp.wait_send()
        mine.wait()

    return pl.pallas_call(
        body,
        name="allgather_small",
        in_specs=[pl.BlockSpec(memory_space=pltpu.VMEM)],
        out_specs=pl.BlockSpec(memory_space=pltpu.VMEM),
        out_shape=jax.ShapeDtypeStruct((8, m_per, ncol), block.dtype),
        scratch_shapes=[pltpu.SemaphoreType.DMA((7,)), pltpu.SemaphoreType.DMA((7,)), pltpu.SemaphoreType.DMA],
        compiler_params=pltpu.CompilerParams(has_side_effects=True, vmem_limit_bytes=32 * MIB),
    )(block)


SMALL_NAMES = ["norm1_g", "b_decay_f", "b_decay_b", "gla_norm_g", "gmlp_ln_g", "gmlp_ln_b", "w_spatial", "b_spatial", "norm2_g", "final_norm_g"]


def _pack_small(parts, decay_parts):
    flat = jnp.concatenate([a.reshape(-1) for a in parts])
    flat = jnp.pad(flat, (0, SMALL_ROWS * LANES - flat.shape[0])).reshape(SMALL_ROWS, LANES)
    return jnp.concatenate([flat] + [d.reshape(-1, LANES) for d in decay_parts], axis=0)


def _unpack_small(packed, like):
    out, off = [], 0
    flat = packed[:SMALL_ROWS].reshape(-1)
    for a in like:
        out.append(flat[off : off + a.size].reshape(a.shape))
        off += a.size
    return out


def kernel(x, norm1_g, w_in, w_decay_f, b_decay_f, w_decay_b, b_decay_b, gla_norm_g, gmlp_ln_g, gmlp_ln_b, w_spatial, b_spatial, w_out, norm2_g, w_gate, w_up, w_down, final_norm_g, loss_target, m_norm1_g, m_w_in, m_w_decay_f, m_b_decay_f, m_w_decay_b, m_b_decay_b, m_gla_norm_g, m_gmlp_ln_g, m_gmlp_ln_b, m_w_spatial, m_b_spatial, m_w_out, m_norm2_g, m_w_gate, m_w_up, m_w_down, m_final_norm_g, v_norm1_g, v_w_in, v_w_decay_f, v_b_decay_f, v_w_decay_b, v_b_decay_b, v_gla_norm_g, v_gmlp_ln_g, v_gmlp_ln_b, v_w_spatial, v_b_spatial, v_w_out, v_norm2_g, v_w_gate, v_w_up, v_w_down, v_final_norm_g):
    args = dict(locals())
    cx, cy, cc = lax.axis_index("x"), lax.axis_index("y"), lax.axis_index("c")
    shard = 2 * cx + cy
    xs = x[0]
    target = loss_target[0]

    big_names = ["w_in", "w_out", "w_gate", "w_up", "w_down"]
    transposed = ("w_in", "w_gate", "w_up")
    rows_of = lambda pre, k: jnp.transpose(args[pre + k][0]) if k in transposed else args[pre + k][0]
    big_shards = {k: rows_of("", k) for k in big_names}
    s_arr = shard.reshape(1).astype(jnp.int32)
    sc_arr = jnp.stack([shard, cc]).astype(jnp.int32)
    zero_token = jnp.zeros(TOKEN_SHAPE, F32)
    w_send, w_recv, (w_in4,), _, token_w_in = _split_start(
        "w_in_gather_start", [_cast_into_slot(big_shards["w_in"], s_arr, zero_token)], [], _gather_ici_copies, 3)
    late = ["w_out", "w_gate", "w_up", "w_down"]
    late_slots = [_cast_into_slot(big_shards[k], s_arr, token_w_in) for k in late]
    dec_block = jnp.concatenate([w_decay_f[0].reshape(-1, LANES), w_decay_b[0].reshape(-1, LANES)], axis=0)
    dec_all = _allgather_small(dec_block)
    (w_in4,), _ = _split_wait("w_in_gather_wait", w_send, w_recv, [w_in4], [], _gather_ici_copies, (dec_all, *late_slots))
    (w_in4,) = _gather_forward([w_in4])
    w_in_t = w_in4.reshape(PROJ_W, D_MODEL)
    g_send, g_recv, late_bufs, _, token_gather = _split_start(
        "gather_start", late_slots, [], _gather_ici_copies, 3 * len(late), after=(w_in4,))
    dec_all = dec_all[::2].reshape(N_SHARDS, 2, LOWRANK, KEY_W // N_SHARDS)
    wdf_full = jnp.transpose(dec_all[:, 0], (1, 0, 2)).reshape(LOWRANK, KEY_W)
    wdb_full = jnp.transpose(dec_all[:, 1], (1, 0, 2)).reshape(LOWRANK, KEY_W)
    wd_pad_f = jnp.zeros((LANES, KEY_W), F32).at[0:LOWRANK].set(wdf_full).astype(BF16)
    wd_pad_b = jnp.zeros((LANES, KEY_W), F32).at[LOWRANK : 2 * LOWRANK].set(wdb_full).astype(BF16)

    ws_bf = w_spatial[0].astype(BF16)
    wst_bf = jnp.transpose(w_spatial[0], (0, 2, 1)).astype(BF16)
    bs_col = b_spatial[0].reshape(GMLP_GROUPS, GMLP_CHUNK, 1)

    p = _inproj(xs, norm1_g, w_in_t, token_gather)
    o_f, st_f = _gla_fwd(p, wd_pad_f, b_decay_f, token_gather, reverse=False)
    o_b, st_b = _gla_fwd(p, wd_pad_b, b_decay_b, token_gather, reverse=True)
    late_bufs, _ = _split_wait("gather_wait", g_send, g_recv, late_bufs, [], _gather_ici_copies, (o_f, o_b))
    (w_out4,) = _gather_forward(late_bufs[:1])
    f_send, f_recv, ffn_bufs, _, token_forward = _split_start(
        "forward_start", late_bufs[1:], [], _gather_d2d_copies, 3 * (len(late) - 1), after=(w_out4,))
    w_out_full = w_out4.reshape(-1, D_MODEL)
    x1, ycat = _mixer_out(xs, o_f, o_b, p, gla_norm_g, gmlp_ln_g, gmlp_ln_b, ws_bf, bs_col, w_out_full, token_forward)
    ffn_bufs, _ = _split_wait("forward_wait", f_send, f_recv, ffn_bufs, [], _gather_d2d_copies, (x1,))
    wg_t, wu_t, wd = [b.reshape(-1, D_MODEL) for b in ffn_bufs]
    gf = final_norm_g.reshape(1, D_MODEL)
    h2, gate, up, act, dx2, loss_acc, dgf = _ffn_fwd(x1, target, norm2_g, gf, wg_t, wu_t, wd)

    dgate, dup, dx1, dg2 = _ffn_bwd(dx2, gate, up, x1, norm2_g, wg_t, wu_t, wd)
    ffn_grads4 = [g.reshape(N_SHARDS, FF_SHARD, D_MODEL) for g in _ffn_wgrad(h2, dgate, dup, act, dx2)]
    e_send, e_recv, e_srcs, e_lands, token_exchange = _split_start(
        "exchange_start", ffn_grads4, _exchange_lands(ffn_grads4), _exchange_copies, len(ffn_grads4))
    do, dg, du, dvv, dwo, dgn, dlng, dlnb, dws, dbs = _mixer_bwd(
        dx1, ycat, o_f, o_b, p, gla_norm_g, gmlp_ln_g, gmlp_ln_b, ws_bf, wst_bf, bs_col, w_out_full, token_exchange)
    ffn_mine, ffn_other = _split_wait("exchange_wait", e_send, e_recv, e_srcs, e_lands, _exchange_copies, (do,))
    ffn_parts = _add_halves(ffn_mine, ffn_other, sc_arr)
    ffn_payload = [pb for _, pb in ffn_parts]
    s_send, s_recv, s_parts, s_lands, token_scatter = _split_start(
        "scatter_start", ffn_payload, _scatter_lands(ffn_payload), _scatter_copies, 3 * len(ffn_payload))
    dq_f, dk_f, dv_f, dlr_f, dwdec_f, dbdec_f = _gla_bwd(p, do, st_f, wd_pad_f, b_decay_f, token_scatter, reverse=False)
    dq, dk, dv, dlr, dwdec_b, dbdec_b = _gla_bwd(
        p, do, st_b, wd_pad_b, b_decay_b, token_scatter, reverse=True, other=(dq_f, dk_f, dv_f, dlr_f))
    dwin_t, dp = _inproj_wgrad(xs, norm1_g, dq, dk, dv, dg, du, dvv, dlr)
    _, ffn_recv = _split_wait("scatter_wait", s_send, s_recv, s_parts, s_lands, _scatter_copies, (dwin_t,))
    ffn_bufs = [_add_partials(pf, r, sc_arr) for (pf, _), r in zip(ffn_parts, ffn_recv)]

    dwin4 = dwin_t.reshape(N_SHARDS, PROJ_W // N_SHARDS, D_MODEL)
    dwo4 = dwo.reshape(N_SHARDS, D_MODEL // N_SHARDS, D_MODEL)
    proj_grads4 = [dwin4, dwo4]
    proj_parts = [_add_halves([g], [r], sc_arr)[0] for g, r in zip(proj_grads4, _exchange_halves(proj_grads4))]
    proj_payload = [pb for _, pb in proj_parts]
    p_send, p_recv, p_parts, p_lands, token_proj = _split_start(
        "proj_scatter_start", proj_payload, _scatter_lands(proj_payload), _scatter_copies, 3 * len(proj_payload))
    j_send, j_recv, ffn_bufs, _, token_join = _split_start("join_start", ffn_bufs, [], _join_copies, len(ffn_bufs), after=(token_proj,))
    dx, dg1 = _inproj_dx(xs, dx1, norm1_g, w_in_t, dp, token_join)
    _, proj_recv = _split_wait("proj_scatter_wait", p_send, p_recv, p_parts, p_lands, _scatter_copies, (dx,))
    ffn_bufs, _ = _split_wait("join_wait", j_send, j_recv, ffn_bufs, [], _join_copies, (dx,))

    dwdec_f16 = dwdec_f[0:LOWRANK]
    dwdec_b16 = dwdec_b[LOWRANK : 2 * LOWRANK]
    shard_major = lambda a: jnp.transpose(a.reshape(LOWRANK, N_SHARDS, KEY_W // N_SHARDS), (1, 0, 2))
    small_grads = {
        "norm1_g": dg1, "b_decay_f": dbdec_f, "b_decay_b": dbdec_b, "gla_norm_g": dgn, "gmlp_ln_g": dlng, "gmlp_ln_b": dlnb,
        "w_spatial": dws, "b_spatial": dbs, "norm2_g": dg2, "final_norm_g": dgf,
    }
    g_pack = _pack_small([small_grads[k] for k in SMALL_NAMES] + [loss_acc], [shard_major(dwdec_f16), shard_major(dwdec_b16)])
    sg_send, sg_recv, (g_pack,), g_lands, token_small = _split_start(
        "small_gather_start", [g_pack], [jax.ShapeDtypeStruct((8, SMALL_TOTAL, LANES), F32)], _small_gather_copies, 7)

    proj_bufs = [_add_partials(pf, r, sc_arr) for (pf, _), r in zip(proj_parts, proj_recv)]
    big_grads = dict(zip(big_names, list(_join_halves(proj_bufs, after=(token_small,))) + ffn_bufs))
    big_updates = {k: _adamw(big_shards[k], big_grads[k], rows_of("m_", k), rows_of("v_", k)) for k in big_names}

    (g_pack,), (g_all,) = _split_wait(
        "small_gather_wait", sg_send, sg_recv, [g_pack], g_lands, _small_gather_copies, tuple(u[1] for u in big_updates.values()))
    pack_own = lambda pre: _pack_small([args[pre + k] for k in SMALL_NAMES], [args[pre + "w_decay_f"], args[pre + "w_decay_b"]])
    sg, sd, sm, sv = _adamw_small(g_all, g_pack, pack_own(""), pack_own("m_"), pack_own("v_"))

    names = ["norm1_g", "w_in", "w_decay_f", "b_decay_f", "w_decay_b", "b_decay_b", "gla_norm_g", "gmlp_ln_g", "gmlp_ln_b",
             "w_spatial", "b_spatial", "w_out", "norm2_g", "w_gate", "w_up", "w_down", "final_norm_g"]
    like = [args[k] for k in SMALL_NAMES]
    results = {"g": {}, "d": {}, "m": {}, "v": {}}
    for tag, packed in (("g", sg), ("d", sd), ("m", sm), ("v", sv)):
        for k, a in zip(SMALL_NAMES, _unpack_small(packed, like)):
            results[tag][k] = a
        results[tag]["w_decay_f"] = packed[SMALL_ROWS : SMALL_ROWS + DECAY_ROWS].reshape(w_decay_f.shape)
        results[tag]["w_decay_b"] = packed[SMALL_ROWS + DECAY_ROWS :].reshape(w_decay_b.shape)
    for k in big_names:
        for tag, a in zip("gdmv", big_updates[k]):
            results[tag][k] = (jnp.transpose(a) if k in transposed else a).reshape(args[k].shape)

    loss = sg[:SMALL_ROWS].reshape(-1)[sum(a.size for a in like)]
    grad_x = dx.reshape(x.shape)
    return (loss, grad_x, *[results["g"][k] for k in names], *[results["d"][k] for k in names],
            *[results["m"][k] for k in names], *[results["v"][k] for k in names])
```

```python
import functools
import math

import jax
import jax.numpy as jnp
from jax import lax
from jax.experimental import pallas as pl
from jax.experimental.pallas import tpu as pltpu

F32, BF16 = jnp.float32, jnp.bfloat16

D_MODEL = 1024
GLA_HEADS = 4
GLA_DK = 64
GLA_DV = 128
KEY_W = GLA_HEADS * GLA_DK
GLA_W = GLA_HEADS * GLA_DV
GMLP_W = 512
GMLP_GROUPS = 4
GMLP_CHUNK = 128
LOWRANK = 16
GLA_CHUNK = 64
GLA_TAU = 16.0
PROJ_W = 2592
PROJ_WP = 2688
D_FF = 2816
N_SHARDS = 4
FF_SHARD = D_FF // N_SHARDS
EPS = 1e-6
LANES = 128
TOKEN_SHAPE = (8, LANES)
MIB = 1024 * 1024

ADAM_LR = 0.001
ADAM_B1 = 0.9
ADAM_B2 = 0.999
ADAM_EPS = 1e-08
ADAM_WD = 0.01
ADAM_STEP = 10

COL_Q, COL_K = 0, 256
COL_V, COL_G, COL_U, COL_VV = 512, 1024, 1536, 2048
COL_LR = 2560
ROW_LR, ROW_UV = 1536, 1568
HALF = D_MODEL // 2

MESH = pl.DeviceIdType.MESH


def _nn(a, b):
    return jnp.dot(a, b, preferred_element_type=F32)


def _nt(a, b):
    return lax.dot_general(a, b, (((1,), (1,)), ((), ())), preferred_element_type=F32)


def _tn(a, b):
    return lax.dot_general(a, b, (((0,), (0,)), ((), ())), preferred_element_type=F32)


def _bnn(a, b):
    return jnp.einsum("nik,nkj->nij", a, b, preferred_element_type=F32)


def _bnt(a, b):
    return jnp.einsum("nik,njk->nij", a, b, preferred_element_type=F32)


def _btn(a, b):
    return jnp.einsum("nki,nkj->nij", a, b, preferred_element_type=F32)


def _resident(shape):
    zeros = (0,) * len(shape)
    return pl.BlockSpec(shape, lambda *_: zeros, pipeline_mode=pl.Buffered(1))


def _params(vmem_mib, semantics=("arbitrary",)):
    return pltpu.CompilerParams(vmem_limit_bytes=vmem_mib * MIB, dimension_semantics=semantics)


def _sigmoid(x):
    return 1.0 / (1.0 + jnp.exp(-x))


def _gelu(x):
    return 0.5 * x * (1.0 + lax.erf(x * (1.0 / math.sqrt(2.0))))


def _gelu_and_grad(x):
    cdf = 0.5 * (1.0 + lax.erf(x * (1.0 / math.sqrt(2.0))))
    return x * cdf, cdf + x * jnp.exp(-0.5 * x * x) * (1.0 / math.sqrt(2.0 * math.pi))


def _log_sigmoid(x):
    return jnp.minimum(x, 0.0) - jnp.log(1.0 + jnp.exp(-jnp.abs(x)))


def _rms_bwd(dxh, xh, r):
    return r * (dxh - xh * jnp.mean(dxh * xh, axis=-1, keepdims=True))


def _chunk_cumsum(v, row_in_chunk, reverse):
    rows = v.shape[0]
    for sh in (1, 2, 4, 8, 16, 32):
        if reverse:
            v = v + jnp.where(row_in_chunk + sh < GLA_CHUNK, pltpu.roll(v, rows - sh, axis=0), 0.0)
        else:
            v = v + jnp.where(row_in_chunk >= sh, pltpu.roll(v, sh, axis=0), 0.0)
    return v


def _inproj(x, g1, w_in_t, token):
    seq = x.shape[0]
    tm = min(seq, 512)

    def body(x_ref, g_ref, w_ref, token_ref, p_ref):
        xv = x_ref[...]
        r = lax.rsqrt(jnp.mean(xv * xv, axis=-1, keepdims=True) + EPS)
        h = (xv * r * g_ref[...]).astype(BF16)
        p_ref[:, 0:COL_U] = _nt(h, w_ref[0:ROW_LR, :])
        p_ref[:, COL_U:COL_LR] = _nt(h, w_ref[ROW_UV:PROJ_W, :])
        p_ref[:, COL_LR:PROJ_WP] = _nt(h, w_ref[ROW_LR : ROW_LR + LANES, :])

    return pl.pallas_call(
        body,
        name="inproj",
        grid=(seq // tm,),
        in_specs=[pl.BlockSpec((tm, D_MODEL), lambda i: (i, 0)), _resident((1, D_MODEL)), _resident((PROJ_W, D_MODEL)), _resident(TOKEN_SHAPE)],
        out_specs=pl.BlockSpec((tm, PROJ_WP), lambda i: (i, 0)),
        out_shape=jax.ShapeDtypeStruct((seq, PROJ_WP), F32),
        compiler_params=_params(48, ("parallel",)),
    )(x, g1, w_in_t, token)


def _gla_tile(seq):
    return min(seq, 1024)


def _gla_decay_terms(lr_bf, wd_ref, bd_ref, pair, row_in_chunk, reverse, n):
    cols = pl.ds(pair * LANES, LANES)
    pre = _nn(lr_bf, wd_ref[:, cols]) + bd_ref[:, cols]
    la = _log_sigmoid(pre) * (1.0 / GLA_TAU)
    b = _chunk_cumsum(la, row_in_chunk, reverse)
    b3 = b.reshape(n, GLA_CHUNK, LANES)
    blast = b3[:, 0:1, :] if reverse else b3[:, GLA_CHUNK - 1 : GLA_CHUNK, :]
    return pre, b3, blast


def _gla_fwd(p, wd_pad, bd, token, reverse):
    seq = p.shape[0]
    tg = _gla_tile(seq)
    nt = seq // tg
    n = tg // GLA_CHUNK
    scale = GLA_DK**-0.5

    def tile(i):
        return nt - 1 - i if reverse else i

    def body(q_ref, k_ref, v_ref, lr_ref, wd_ref, bd_ref, token_ref, o_ref, st_ref, carry):
        @pl.when(pl.program_id(0) == 0)
        def _():
            carry[...] = jnp.zeros_like(carry)

        lr_bf = lr_ref[...].astype(BF16)
        states = [carry[h] for h in range(GLA_HEADS)]
        row_in_chunk = lax.broadcasted_iota(jnp.int32, (tg, LANES), 0) % GLA_CHUNK
        lane_head = lax.broadcasted_iota(jnp.int32, (1, LANES), 1) // GLA_DK
        tt = lax.broadcasted_iota(jnp.int32, (GLA_CHUNK, GLA_CHUNK), 0)
        ss = lax.broadcasted_iota(jnp.int32, (GLA_CHUNK, GLA_CHUNK), 1)
        causal = (tt <= ss) if reverse else (tt >= ss)
        order = range(n - 1, -1, -1) if reverse else range(n)
        heads = range(GLA_HEADS)
        qds, vhs, decs, sc_raw, dst = {}, {}, {}, {}, {}
        for pair in range(2):
            cols = pl.ds(pair * LANES, LANES)
            _, b3, blast = _gla_decay_terms(lr_bf, wd_ref, bd_ref, pair, row_in_chunk, reverse, n)
            q3 = q_ref[:, cols].reshape(n, GLA_CHUNK, LANES) * scale
            k3 = k_ref[:, cols].reshape(n, GLA_CHUNK, LANES)
            qd = q3 * jnp.exp(b3)
            kd = (k3 * jnp.exp(-b3)).astype(BF16)
            kte = k3 * jnp.exp(blast - b3)
            decs[pair] = jnp.exp(blast)
            qds[pair] = qd.astype(BF16)
            m0 = (lane_head == 0).astype(F32)
            m1 = (lane_head == 1).astype(F32)
            q_both = jnp.concatenate([(qd * m0).astype(BF16), (qd * m1).astype(BF16)], axis=1)
            sc_both = _bnt(q_both, kd)
            for hh, m in ((0, m0), (1, m1)):
                h = 2 * pair + hh
                vhs[h] = v_ref[:, pl.ds(h * GLA_DV, GLA_DV)].reshape(n, GLA_CHUNK, GLA_DV).astype(BF16)
                sc_raw[h] = sc_both[:, hh * GLA_CHUNK : (hh + 1) * GLA_CHUNK, :]
                dst[h] = _btn(vhs[h], (kte * m).astype(BF16))
        o_intra, befores = {}, {}
        for h in heads:
            o_intra[h] = _bnn(jnp.where(causal, sc_raw[h], 0.0).astype(BF16), vhs[h])
            st, before = states[h], [None] * n
            for j in order:
                before[j] = st
                st = st * decs[h // 2][j] + dst[h][j]
            states[h] = st
            befores[h] = jnp.stack(before).astype(BF16)
        outs = {}
        for pair in range(2):
            both = jnp.concatenate([befores[2 * pair], befores[2 * pair + 1]], axis=1)
            o_inter = _bnt(qds[pair], both)
            for hh in range(2):
                h = 2 * pair + hh
                outs[h] = (o_intra[h] + o_inter[:, :, hh * GLA_DV : (hh + 1) * GLA_DV]).reshape(tg, GLA_DV)
        for h in range(GLA_HEADS):
            o_ref[:, pl.ds(h * GLA_DV, GLA_DV)] = outs[h]
            st_ref[:, h] = befores[h]
            carry[h] = states[h]

    nchunks = seq // GLA_CHUNK
    return pl.pallas_call(
        body,
        name="gla_fwd_rev" if reverse else "gla_fwd",
        grid=(nt,),
        in_specs=[
            pl.BlockSpec((tg, KEY_W), lambda i: (tile(i), COL_Q // KEY_W)),
            pl.BlockSpec((tg, KEY_W), lambda i: (tile(i), COL_K // KEY_W)),
            pl.BlockSpec((tg, GLA_W), lambda i: (tile(i), COL_V // GLA_W)),
            pl.BlockSpec((tg, LANES), lambda i: (tile(i), COL_LR // LANES)),
            _resident((LANES, KEY_W)),
            _resident((1, KEY_W)),
            _resident(TOKEN_SHAPE),
        ],
        out_specs=[
            pl.BlockSpec((tg, GLA_W), lambda i: (tile(i), 0)),
            pl.BlockSpec((n, GLA_HEADS, GLA_DV, LANES), lambda i: (tile(i), 0, 0, 0)),
        ],
        out_shape=[
            jax.ShapeDtypeStruct((seq, GLA_W), F32),
            jax.ShapeDtypeStruct((nchunks, GLA_HEADS, GLA_DV, LANES), BF16),
        ],
        scratch_shapes=[pltpu.VMEM((GLA_HEADS, GLA_DV, LANES), F32)],
        compiler_params=_params(48),
    )(p, p, p, p, wd_pad, bd, token)


def _mixer_out(x, o_f, o_b, p, gn, lng, lnb, ws_bf, bs_col, w_out, token):
    seq = x.shape[0]
    tm = min(seq, 512)

    def body(x_ref, of_ref, ob_ref, g_ref, u_ref, vv_ref, gn_ref, lng_ref, lnb_ref, ws_ref, bs_ref, wo_ref, token_ref, x1_ref, yc_ref, vn_sc):
        for h in range(GLA_HEADS):
            cols = pl.ds(h * GLA_DV, GLA_DV)
            oh = of_ref[:, cols] + ob_ref[:, cols]
            on = oh * lax.rsqrt(jnp.mean(oh * oh, axis=-1, keepdims=True) + EPS)
            gh = g_ref[:, cols]
            yc_ref[:, cols] = (on * gn_ref[:, cols] * (gh * _sigmoid(gh))).astype(BF16)
        zv = _gelu(vv_ref[...])
        xc = zv - jnp.mean(zv, axis=-1, keepdims=True)
        vhat = xc * lax.rsqrt(jnp.mean(xc * xc, axis=-1, keepdims=True) + EPS)
        vn_sc[...] = (vhat * lng_ref[...] + lnb_ref[...]).astype(BF16)
        for c in range(tm // GMLP_CHUNK):
            rows = pl.ds(c * GMLP_CHUNK, GMLP_CHUNK)
            for g in range(GMLP_GROUPS):
                cols = pl.ds(g * LANES, LANES)
                s = _nn(ws_ref[g], vn_sc[rows, cols]) + bs_ref[g]
                yc_ref[rows, pl.ds(GLA_W + g * LANES, LANES)] = (_gelu(u_ref[rows, cols]) * s).astype(BF16)
        x1_ref[...] = x_ref[...] + _nn(yc_ref[...], wo_ref[...])

    row = lambda w: pl.BlockSpec((tm, w), lambda i: (i, 0))
    pcol = lambda col: pl.BlockSpec((tm, GLA_W), lambda i: (i, col // GLA_W))
    return pl.pallas_call(
        body,
        name="mixer_out",
        grid=(seq // tm,),
        in_specs=[
            row(D_MODEL), row(GLA_W), row(GLA_W), pcol(COL_G), pcol(COL_U), pcol(COL_VV),
            _resident((1, GLA_W)), _resident((1, GMLP_W)), _resident((1, GMLP_W)),
            _resident((GMLP_GROUPS, GMLP_CHUNK, GMLP_CHUNK)), _resident((GMLP_GROUPS, GMLP_CHUNK, 1)),
            _resident((D_MODEL, D_MODEL)), _resident(TOKEN_SHAPE),
        ],
        out_specs=[row(D_MODEL), row(D_MODEL)],
        out_shape=[jax.ShapeDtypeStruct((seq, D_MODEL), F32), jax.ShapeDtypeStruct((seq, D_MODEL), BF16)],
        scratch_shapes=[pltpu.VMEM((tm, GMLP_W), BF16)],
        compiler_params=_params(48, ("parallel",)),
    )(x, o_f, o_b, p, p, p, gn, lng, lnb, ws_bf, bs_col, w_out, token)


def _ffn_fwd(x1, target, g2, gf, wg_t, wu_t, wd):
    seq = x1.shape[0]
    tm = min(seq, 256)

    def body(x1_ref, t_ref, g2_ref, gf_ref, wg_ref, wu_ref, wd_ref, h2_ref, gate_ref, up_ref, act_ref, dx2_ref, loss_ref, dgf_ref):
        @pl.when(pl.program_id(0) == 0)
        def _():
            loss_ref[...] = jnp.zeros_like(loss_ref)
            dgf_ref[...] = jnp.zeros_like(dgf_ref)

        x1v = x1_ref[...]
        h2 = (x1v * lax.rsqrt(jnp.mean(x1v * x1v, axis=-1, keepdims=True) + EPS) * g2_ref[...]).astype(BF16)
        h2_ref[...] = h2
        gate = _nt(h2, wg_ref[...])
        up = _nt(h2, wu_ref[...])
        act = (gate * _sigmoid(gate) * up).astype(BF16)
        gate_ref[...] = gate
        up_ref[...] = up
        act_ref[...] = act
        x2 = x1v + _nn(act, wd_ref[...])
        rf = lax.rsqrt(jnp.mean(x2 * x2, axis=-1, keepdims=True) + EPS)
        xh = x2 * rf
        err = xh * gf_ref[...] - t_ref[...]
        loss_ref[...] += 0.5 * jnp.sum(jnp.mean(err * err, axis=-1, keepdims=True))
        dy = err * (1.0 / D_MODEL)
        dgf_ref[...] += jnp.sum(dy * xh, axis=0, keepdims=True)
        dx2_ref[...] = _rms_bwd(dy * gf_ref[...], xh, rf)

    row = lambda w: pl.BlockSpec((tm, w), lambda i: (i, 0))
    weight = _resident((D_FF, D_MODEL))
    return pl.pallas_call(
        body,
        name="ffn_fwd",
        grid=(seq // tm,),
        in_specs=[row(D_MODEL), row(D_MODEL), _resident((1, D_MODEL)), _resident((1, D_MODEL)), weight, weight, weight],
        out_specs=[row(D_MODEL), row(D_FF), row(D_FF), row(D_FF), row(D_MODEL),
                   pl.BlockSpec((1, LANES), lambda i: (0, 0)), pl.BlockSpec((1, D_MODEL), lambda i: (0, 0))],
        out_shape=[
            jax.ShapeDtypeStruct((seq, D_MODEL), BF16),
            jax.ShapeDtypeStruct((seq, D_FF), F32),
            jax.ShapeDtypeStruct((seq, D_FF), F32),
            jax.ShapeDtypeStruct((seq, D_FF), BF16),
            jax.ShapeDtypeStruct((seq, D_MODEL), F32),
            jax.ShapeDtypeStruct((1, LANES), F32),
            jax.ShapeDtypeStruct((1, D_MODEL), F32),
        ],
        compiler_params=_params(56),
    )(x1, target, g2, gf, wg_t, wu_t, wd)


def _ffn_bwd(dx2, gate, up, x1, g2, wg_t, wu_t, wd):
    seq = x1.shape[0]
    tm = min(seq, 256)

    def body(dx2_ref, gate_ref, up_ref, x1_ref, g2_ref, wg_ref, wu_ref, wd_ref, dgate_ref, dup_ref, dx1_ref, dg2_ref):
        @pl.when(pl.program_id(0) == 0)
        def _():
            dg2_ref[...] = jnp.zeros_like(dg2_ref)

        dx2v = dx2_ref[...]
        dact = _nt(dx2v.astype(BF16), wd_ref[...])
        gate = gate_ref[...]
        sg = _sigmoid(gate)
        dgate = (dact * up_ref[...] * (sg * (1.0 + gate * (1.0 - sg)))).astype(BF16)
        dup = (dact * (gate * sg)).astype(BF16)
        dgate_ref[...] = dgate
        dup_ref[...] = dup
        dh2 = _nn(dgate, wg_ref[...]) + _nn(dup, wu_ref[...])
        x1v = x1_ref[...]
        r2 = lax.rsqrt(jnp.mean(x1v * x1v, axis=-1, keepdims=True) + EPS)
        xh = x1v * r2
        dg2_ref[...] += jnp.sum(dh2 * xh, axis=0, keepdims=True)
        dx1_ref[...] = dx2v + _rms_bwd(dh2 * g2_ref[...], xh, r2)

    row = lambda w: pl.BlockSpec((tm, w), lambda i: (i, 0))
    weight = _resident((D_FF, D_MODEL))
    return pl.pallas_call(
        body,
        name="ffn_bwd",
        grid=(seq // tm,),
        in_specs=[row(D_MODEL), row(D_FF), row(D_FF), row(D_MODEL), _resident((1, D_MODEL)), weight, weight, weight],
        out_specs=[row(D_FF), row(D_FF), row(D_MODEL), pl.BlockSpec((1, D_MODEL), lambda i: (0, 0))],
        out_shape=[
            jax.ShapeDtypeStruct((seq, D_FF), BF16),
            jax.ShapeDtypeStruct((seq, D_FF), BF16),
            jax.ShapeDtypeStruct((seq, D_MODEL), F32),
            jax.ShapeDtypeStruct((1, D_MODEL), F32),
        ],
        compiler_params=_params(56),
    )(dx2, gate, up, x1, g2, wg_t, wu_t, wd)


WGRAD_ROWS = D_FF // 2


def _ffn_wgrad(h2, dgate, dup, act, dx2):
    seq = h2.shape[0]
    tm = min(seq, 512)

    def body(h2_ref, dgate_ref, dup_ref, act_ref, dx2_ref, dwg_ref, dwu_ref, dwd_ref):
        @pl.when(pl.program_id(1) == 0)
        def _():
            dwg_ref[...] = jnp.zeros_like(dwg_ref)
            dwu_ref[...] = jnp.zeros_like(dwu_ref)
            dwd_ref[...] = jnp.zeros_like(dwd_ref)

        h2v = h2_ref[...]
        dwg_ref[...] += _tn(dgate_ref[...], h2v)
        dwu_ref[...] += _tn(dup_ref[...], h2v)
        dwd_ref[...] += _tn(act_ref[...], dx2_ref[...].astype(BF16))

    ff = pl.BlockSpec((tm, WGRAD_ROWS), lambda j, i: (i, j))
    row = pl.BlockSpec((tm, D_MODEL), lambda j, i: (i, 0))
    out = pl.BlockSpec((WGRAD_ROWS, D_MODEL), lambda j, i: (j, 0))
    return pl.pallas_call(
        body,
        name="ffn_wgrad",
        grid=(D_FF // WGRAD_ROWS, seq // tm),
        in_specs=[row, ff, ff, ff, row],
        out_specs=[out, out, out],
        out_shape=[jax.ShapeDtypeStruct((D_FF, D_MODEL), F32)] * 3,
        compiler_params=_params(56, ("parallel", "arbitrary")),
    )(h2, dgate, dup, act, dx2)


def _mixer_bwd(dx1, ycat, o_f, o_b, p, gn, lng, lnb, ws_bf, wst_bf, bs_col, w_out, token):
    seq = dx1.shape[0]
    tm = min(seq, 512)
    nsteps = seq // tm

    def body(dx1_ref, yc_ref, of_ref, ob_ref, g_ref, u_ref, vv_ref, gn_ref, lng_ref, lnb_ref, ws_ref, wst_ref, bs_ref, wo_ref, token_ref,
             do_ref, dg_ref, du_ref, dvv_ref, dwo_ref, dgn_ref, dlng_ref, dlnb_ref, dws_ref, dbs_ref, vn_sc, dvn_sc, dbs_acc):
        step = pl.program_id(0)

        @pl.when(step == 0)
        def _():
            for r in (dwo_ref, dgn_ref, dlng_ref, dlnb_ref, dws_ref, dbs_acc):
                r[...] = jnp.zeros_like(r)

        dx1b = dx1_ref[...].astype(BF16)
        dyc = _nt(dx1b, wo_ref[...])
        dwo_ref[...] += _tn(yc_ref[...], dx1b)
        for h in range(GLA_HEADS):
            cols = pl.ds(h * GLA_DV, GLA_DV)
            dya = dyc[:, h * GLA_DV : (h + 1) * GLA_DV]
            oh = of_ref[:, cols] + ob_ref[:, cols]
            rn = lax.rsqrt(jnp.mean(oh * oh, axis=-1, keepdims=True) + EPS)
            on = oh * rn
            gh = g_ref[:, cols]
            sg = _sigmoid(gh)
            sil = gh * sg
            gnh = gn_ref[:, cols]
            dgn_ref[:, cols] += jnp.sum(dya * on * sil, axis=0, keepdims=True)
            dg_ref[:, cols] = (dya * on * gnh * (sg * (1.0 + gh * (1.0 - sg)))).astype(BF16)
            do_ref[:, cols] = _rms_bwd(dya * gnh * sil, on, rn)
        vv = vv_ref[...]
        zv, zv_grad = _gelu_and_grad(vv)
        xc = zv - jnp.mean(zv, axis=-1, keepdims=True)
        rstd = lax.rsqrt(jnp.mean(xc * xc, axis=-1, keepdims=True) + EPS)
        vhat = xc * rstd
        vn_sc[...] = (vhat * lng_ref[...] + lnb_ref[...]).astype(BF16)
        for c in range(tm // GMLP_CHUNK):
            rows = pl.ds(c * GMLP_CHUNK, GMLP_CHUNK)
            for g in range(GMLP_GROUPS):
                cols = pl.ds(g * LANES, LANES)
                vn = vn_sc[rows, cols]
                s = _nn(ws_ref[g], vn) + bs_ref[g]
                dyb = dyc[c * GMLP_CHUNK : (c + 1) * GMLP_CHUNK, GLA_W + g * LANES : GLA_W + (g + 1) * LANES]
                zu, zu_grad = _gelu_and_grad(u_ref[rows, cols])
                du_ref[rows, cols] = (dyb * s * zu_grad).astype(BF16)
                ds = dyb * zu
                dbs_acc[g] += ds
                dsb = ds.astype(BF16)
                dws_ref[g] += _nt(dsb, vn)
                dvn_sc[rows, cols] = _nn(wst_ref[g], dsb)
        dvn = dvn_sc[...]
        dlng_ref[...] += jnp.sum(dvn * vhat, axis=0, keepdims=True)
        dlnb_ref[...] += jnp.sum(dvn, axis=0, keepdims=True)
        dvh = dvn * lng_ref[...]
        dzv = rstd * (dvh - jnp.mean(dvh, axis=-1, keepdims=True) - vhat * jnp.mean(dvh * vhat, axis=-1, keepdims=True))
        dvv_ref[...] = (dzv * zv_grad).astype(BF16)

        @pl.when(step == nsteps - 1)
        def _():
            dbs_ref[...] = jnp.sum(dbs_acc[...], axis=-1, keepdims=True)

    row = lambda w: pl.BlockSpec((tm, w), lambda i: (i, 0))
    pcol = lambda col: pl.BlockSpec((tm, GLA_W), lambda i: (i, col // GLA_W))
    const = lambda shape: pl.BlockSpec(shape, lambda i: (0,) * len(shape))
    return pl.pallas_call(
        body,
        name="mixer_bwd",
        grid=(nsteps,),
        in_specs=[
            row(D_MODEL), row(D_MODEL), row(GLA_W), row(GLA_W), pcol(COL_G), pcol(COL_U), pcol(COL_VV),
            _resident((1, GLA_W)), _resident((1, GMLP_W)), _resident((1, GMLP_W)),
            _resident((GMLP_GROUPS, GMLP_CHUNK, GMLP_CHUNK)), _resident((GMLP_GROUPS, GMLP_CHUNK, GMLP_CHUNK)),
            _resident((GMLP_GROUPS, GMLP_CHUNK, 1)), _resident((D_MODEL, D_MODEL)), _resident(TOKEN_SHAPE),
        ],
        out_specs=[
            row(GLA_W), row(GLA_W), row(GMLP_W), row(GMLP_W), const((D_MODEL, D_MODEL)),
            const((1, GLA_W)), const((1, GMLP_W)), const((1, GMLP_W)),
            const((GMLP_GROUPS, GMLP_CHUNK, GMLP_CHUNK)), const((GMLP_GROUPS, GMLP_CHUNK, 1)),
        ],
        out_shape=[
            jax.ShapeDtypeStruct((seq, GLA_W), F32), jax.ShapeDtypeStruct((seq, GLA_W), BF16),
            jax.ShapeDtypeStruct((seq, GMLP_W), BF16), jax.ShapeDtypeStruct((seq, GMLP_W), BF16),
            jax.ShapeDtypeStruct((D_MODEL, D_MODEL), F32),
            jax.ShapeDtypeStruct((1, GLA_W), F32), jax.ShapeDtypeStruct((1, GMLP_W), F32), jax.ShapeDtypeStruct((1, GMLP_W), F32),
            jax.ShapeDtypeStruct((GMLP_GROUPS, GMLP_CHUNK, GMLP_CHUNK), F32), jax.ShapeDtypeStruct((GMLP_GROUPS, GMLP_CHUNK, 1), F32),
        ],
        scratch_shapes=[pltpu.VMEM((tm, GMLP_W), BF16), pltpu.VMEM((tm, GMLP_W), F32), pltpu.VMEM((GMLP_GROUPS, GMLP_CHUNK, GMLP_CHUNK), F32)],
        compiler_params=_params(56),
    )(dx1, ycat, o_f, o_b, p, p, p, gn, lng, lnb, ws_bf, wst_bf, bs_col, w_out, token)


def _gla_bwd(p, do, st, wd_pad, bd, token, reverse, other=None):
    seq = p.shape[0]
    tg = _gla_tile(seq)
    nt = seq // tg
    n = tg // GLA_CHUNK
    scale = GLA_DK**-0.5

    def tile(i):
        return i if reverse else nt - 1 - i

    def body(q_ref, k_ref, v_ref, lr_ref, do_ref, st_ref, wd_ref, bd_ref, token_ref, *rest):
        others, (dq_ref, dk_ref, dv_ref, dlr_ref, dwd_ref, dbd_ref, carry) = rest[:-7], rest[-7:]
        if others:
            odq_ref, odk_ref, odv_ref, odlr_ref = others

            def put(ref, idx, val, oref):
                ref[idx] = (val + oref[idx]).astype(BF16)
        else:
            odq_ref = odk_ref = odv_ref = odlr_ref = None

            def put(ref, idx, val, oref):
                ref[idx] = val

        @pl.when(pl.program_id(0) == 0)
        def _():
            carry[...] = jnp.zeros_like(carry)
            dwd_ref[...] = jnp.zeros_like(dwd_ref)
            dbd_ref[...] = jnp.zeros_like(dbd_ref)

        lr_bf = lr_ref[...].astype(BF16)
        carries = [carry[h] for h in range(GLA_HEADS)]
        row_in_chunk = lax.broadcasted_iota(jnp.int32, (tg, LANES), 0) % GLA_CHUNK
        lane_head = lax.broadcasted_iota(jnp.int32, (1, LANES), 1) // GLA_DK
        tt = lax.broadcasted_iota(jnp.int32, (GLA_CHUNK, GLA_CHUNK), 0)
        ss = lax.broadcasted_iota(jnp.int32, (GLA_CHUNK, GLA_CHUNK), 1)
        causal = (tt <= ss) if reverse else (tt >= ss)
        order = range(n) if reverse else range(n - 1, -1, -1)
        dlr = jnp.zeros((tg, LANES), F32)
        heads = range(GLA_HEADS)
        pv, masks, qdh, vhs, dohs, stbs = {}, {}, {}, {}, {}, {}
        sc_raw, dp, acc = {}, {}, {}
        for pair in range(2):
            cols = pl.ds(pair * LANES, LANES)
            pre, b3, blast = _gla_decay_terms(lr_bf, wd_ref, bd_ref, pair, row_in_chunk, reverse, n)
            q3 = q_ref[:, cols].reshape(n, GLA_CHUNK, LANES) * scale
            k3 = k_ref[:, cols].reshape(n, GLA_CHUNK, LANES)
            eb = jnp.exp(b3)
            emb = jnp.exp(-b3)
            ekte = jnp.exp(blast - b3)
            kdf = k3 * emb
            kte = k3 * ekte
            both = pl.ds(2 * pair * GLA_DV, 2 * GLA_DV)
            pv[pair] = dict(pre=pre, eb=eb, emb=emb, ekte=ekte, qd=q3 * eb, kdf=kdf, kd=kdf.astype(BF16), kte=kte, kte_bf=kte.astype(BF16),
                            dec=jnp.exp(blast), v=v_ref[:, both].reshape(n, GLA_CHUNK, 2 * GLA_DV).astype(BF16),
                            do=do_ref[:, both].reshape(n, GLA_CHUNK, 2 * GLA_DV).astype(BF16))
            for hh in range(2):
                h = 2 * pair + hh
                masks[h] = (lane_head == hh).astype(F32)
                qdh[h] = (pv[pair]["qd"] * masks[h]).astype(BF16)
                vhs[h] = pv[pair]["v"][:, :, hh * GLA_DV : (hh + 1) * GLA_DV]
                dohs[h] = pv[pair]["do"][:, :, hh * GLA_DV : (hh + 1) * GLA_DV]
                stbs[h] = st_ref[:, h]
                dp[h] = _bnt(dohs[h], vhs[h])
                acc[h] = _btn(dohs[h], qdh[h])
            sc_both = _bnt(jnp.concatenate([qdh[2 * pair], qdh[2 * pair + 1]], axis=1), pv[pair]["kd"])
            for hh in range(2):
                sc_raw[2 * pair + hh] = sc_both[:, hh * GLA_CHUNK : (hh + 1) * GLA_CHUNK, :]
        dsa, sc = {}, {}
        for h in heads:
            sc[h] = jnp.where(causal, sc_raw[h], 0.0).astype(BF16)
            dp[h] = jnp.where(causal, dp[h], 0.0).astype(BF16)
            dec = pv[h // 2]["dec"]
            c, after = carries[h], [None] * n
            for j in order:
                after[j] = c
                c = acc[h][j] + dec[j] * c
            carries[h] = c
            dsa[h] = jnp.stack(after)
        dvs, dqs, dks, dwds, dbds = [], [], [], [], []
        for pair in range(2):
            cols = pl.ds(pair * LANES, LANES)
            v = pv[pair]
            h0, h1 = 2 * pair, 2 * pair + 1
            dsa_both = jnp.concatenate([dsa[h0], dsa[h1]], axis=1)
            dsa_bf = dsa_both.astype(BF16)
            stb_bf = jnp.concatenate([stbs[h0], stbs[h1]], axis=1)
            dq_intra = _bnn(jnp.concatenate([dp[h0], dp[h1]], axis=1), v["kd"])
            dqd = (dq_intra[:, :GLA_CHUNK, :] * masks[h0] + dq_intra[:, GLA_CHUNK:, :] * masks[h1]) + _bnn(v["do"], stb_bf)
            dkd = _btn(dp[h0], qdh[h0]) + _btn(dp[h1], qdh[h1])
            dkte = _bnn(v["v"], dsa_bf)
            ddec = jnp.sum(dsa[h0] * stbs[h0].astype(F32) + dsa[h1] * stbs[h1].astype(F32), axis=1, keepdims=True)
            dv_inter = _bnt(v["kte_bf"], dsa_bf)
            for hh, h in ((0, h0), (1, h1)):
                dvs.append((_btn(sc[h], dohs[h]) + dv_inter[:, :, hh * GLA_DV : (hh + 1) * GLA_DV]).reshape(tg, GLA_DV))
            dqs.append((dqd * (scale * v["eb"])).reshape(tg, LANES))
            dks.append((dkd * v["emb"] + dkte * v["ekte"]).reshape(tg, LANES))
            db = dqd * v["qd"] - dkd * v["kdf"] - dkte * v["kte"]
            dblast = jnp.sum(dkte * v["kte"], axis=1, keepdims=True) + ddec * v["dec"]
            dla = _chunk_cumsum(db.reshape(tg, LANES), row_in_chunk, not reverse) + jnp.broadcast_to(dblast, (n, GLA_CHUNK, LANES)).reshape(tg, LANES)
            dpre = (dla * (1.0 / GLA_TAU) * _sigmoid(-v["pre"]))
            dpre_bf = dpre.astype(BF16)
            dlr = dlr + _nt(dpre_bf, wd_ref[:, cols])
            dwds.append(_tn(lr_bf, dpre_bf))
            dbds.append(jnp.sum(dpre, axis=0, keepdims=True))
        put(dlr_ref, (slice(None), slice(None)), dlr, odlr_ref)
        for pair in range(2):
            cols = pl.ds(pair * LANES, LANES)
            put(dq_ref, (slice(None), cols), dqs[pair], odq_ref)
            put(dk_ref, (slice(None), cols), dks[pair], odk_ref)
            dwd_ref[:, cols] += dwds[pair]
            dbd_ref[:, cols] += dbds[pair]
        for h in range(GLA_HEADS):
            put(dv_ref, (slice(None), pl.ds(h * GLA_DV, GLA_DV)), dvs[h], odv_ref)
            carry[h] = carries[h]

    pieces = [
        pl.BlockSpec((tg, KEY_W), lambda i: (tile(i), 0)),
        pl.BlockSpec((tg, KEY_W), lambda i: (tile(i), 0)),
        pl.BlockSpec((tg, GLA_W), lambda i: (tile(i), 0)),
        pl.BlockSpec((tg, LANES), lambda i: (tile(i), 0)),
    ]
    piece_dtype = BF16 if other else F32
    return pl.pallas_call(
        body,
        name="gla_bwd_rev" if reverse else "gla_bwd",
        grid=(nt,),
        in_specs=[
            pl.BlockSpec((tg, KEY_W), lambda i: (tile(i), COL_Q // KEY_W)),
            pl.BlockSpec((tg, KEY_W), lambda i: (tile(i), COL_K // KEY_W)),
            pl.BlockSpec((tg, GLA_W), lambda i: (tile(i), COL_V // GLA_W)),
            pl.BlockSpec((tg, LANES), lambda i: (tile(i), COL_LR // LANES)),
            pl.BlockSpec((tg, GLA_W), lambda i: (tile(i), 0)),
            pl.BlockSpec((n, GLA_HEADS, GLA_DV, LANES), lambda i: (tile(i), 0, 0, 0)),
            _resident((LANES, KEY_W)),
            _resident((1, KEY_W)),
            _resident(TOKEN_SHAPE),
        ] + (pieces if other else []),
        out_specs=pieces + [pl.BlockSpec((LANES, KEY_W), lambda i: (0, 0)), pl.BlockSpec((1, KEY_W), lambda i: (0, 0))],
        out_shape=[
            jax.ShapeDtypeStruct((seq, KEY_W), piece_dtype), jax.ShapeDtypeStruct((seq, KEY_W), piece_dtype),
            jax.ShapeDtypeStruct((seq, GLA_W), piece_dtype), jax.ShapeDtypeStruct((seq, LANES), piece_dtype),
            jax.ShapeDtypeStruct((LANES, KEY_W), F32), jax.ShapeDtypeStruct((1, KEY_W), F32),
        ],
        scratch_shapes=[pltpu.VMEM((GLA_HEADS, GLA_DV, LANES), F32)],
        compiler_params=_params(56),
    )(p, p, p, p, do, st, wd_pad, bd, token, *(other or ()))


def _inproj_wgrad(x, g1, dq, dk, dv, dg, du, dvv, dlr):
    seq = x.shape[0]
    tm = min(seq, 512)

    def body(x_ref, g1_ref, dq_ref, dk_ref, dv_ref, dg_ref, du_ref, dvv_ref, dlr_ref, dw_ref, dp_ref):
        @pl.when(pl.program_id(0) == 0)
        def _():
            dw_ref[...] = jnp.zeros_like(dw_ref)

        for col, ref in ((COL_Q, dq_ref), (COL_K, dk_ref), (COL_V, dv_ref), (COL_G, dg_ref), (COL_U, du_ref), (COL_VV, dvv_ref), (COL_LR, dlr_ref)):
            dp_ref[:, col : col + ref.shape[1]] = ref[...]
        xv = x_ref[...]
        h = (xv * lax.rsqrt(jnp.mean(xv * xv, axis=-1, keepdims=True) + EPS) * g1_ref[...]).astype(BF16)
        dw_ref[0:ROW_LR, :] += _tn(dp_ref[:, 0:COL_U], h)
        dw_ref[ROW_UV:PROJ_W, :] += _tn(dp_ref[:, COL_U:COL_LR], h)
        dw_ref[ROW_LR:ROW_UV, :] += _tn(dp_ref[:, COL_LR:PROJ_WP], h)[0 : ROW_UV - ROW_LR]

    row = lambda w: pl.BlockSpec((tm, w), lambda i: (i, 0))
    return pl.pallas_call(
        body,
        name="inproj_wgrad",
        grid=(seq // tm,),
        in_specs=[row(D_MODEL), _resident((1, D_MODEL)), row(KEY_W), row(KEY_W), row(GLA_W), row(GLA_W), row(GMLP_W), row(GMLP_W), row(LANES)],
        out_specs=[pl.BlockSpec((PROJ_W, D_MODEL), lambda i: (0, 0)), row(PROJ_WP)],
        out_shape=[jax.ShapeDtypeStruct((PROJ_W, D_MODEL), F32), jax.ShapeDtypeStruct((seq, PROJ_WP), BF16)],
        compiler_params=_params(56),
    )(x, g1, dq, dk, dv, dg, du, dvv, dlr)


def _inproj_dx(x, dx1, g1, w_in_t, dp, token):
    seq = x.shape[0]
    tm = min(seq, 512)

    def body(x_ref, dx1_ref, g1_ref, w_ref, dp_ref, token_ref, dx_ref, dg1_ref):
        @pl.when(pl.program_id(0) == 0)
        def _():
            dg1_ref[...] = jnp.zeros_like(dg1_ref)

        xv = x_ref[...]
        r1 = lax.rsqrt(jnp.mean(xv * xv, axis=-1, keepdims=True) + EPS)
        xh = xv * r1
        dh = (_nn(dp_ref[:, 0:COL_U], w_ref[0:ROW_LR, :]) + _nn(dp_ref[:, COL_U:COL_LR], w_ref[ROW_UV:PROJ_W, :])
              + _nn(dp_ref[:, COL_LR:PROJ_WP], w_ref[ROW_LR : ROW_LR + LANES, :]))
        dg1_ref[...] += jnp.sum(dh * xh, axis=0, keepdims=True)
        dx_ref[...] = dx1_ref[...] + _rms_bwd(dh * g1_ref[...], xh, r1)

    row = lambda w: pl.BlockSpec((tm, w), lambda i: (i, 0))
    return pl.pallas_call(
        body,
        name="inproj_dx",
        grid=(seq // tm,),
        in_specs=[row(D_MODEL), row(D_MODEL), _resident((1, D_MODEL)), _resident((PROJ_W, D_MODEL)), row(PROJ_WP), _resident(TOKEN_SHAPE)],
        out_specs=[row(D_MODEL), pl.BlockSpec((1, D_MODEL), lambda i: (0, 0))],
        out_shape=[jax.ShapeDtypeStruct((seq, D_MODEL), F32), jax.ShapeDtypeStruct((1, D_MODEL), F32)],
        compiler_params=_params(48),
    )(x, dx1, g1, w_in_t, dp, token)


def _in_hbm(a):
    return pltpu.with_memory_space_constraint(a, pltpu.HBM)


def _row_tile(rows, multiple=8):
    for t in range(min(rows, 512), 0, -1):
        if rows % t == 0 and t % multiple == 0:
            return t
    return rows


def _cast_into_slot(w, shard, token):
    rows, cols = w.shape
    tr = _row_tile(rows, 16)

    def body(s_ref, w_ref, token_ref, o_ref):
        o_ref[...] = w_ref[...].astype(BF16)

    return pl.pallas_call(
        body,
        name="cast_into_slot",
        grid_spec=pltpu.PrefetchScalarGridSpec(
            num_scalar_prefetch=1,
            grid=(rows // tr,),
            in_specs=[pl.BlockSpec((tr, cols), lambda i, s_ref: (i, 0)), pl.BlockSpec(TOKEN_SHAPE, lambda i, s_ref: (0, 0))],
            out_specs=pl.BlockSpec((None, tr, cols), lambda i, s_ref: (s_ref[0], i, 0)),
        ),
        out_shape=pltpu.HBM((N_SHARDS, rows, cols), BF16),
        compiler_params=_params(32, ("parallel",)),
    )(shard, _in_hbm(w), token)


def _add_halves(grads4, recvs, shard_core):
    n = len(grads4)
    _, rows, _ = grads4[0].shape
    tr = _row_tile(rows, 16)

    def body(sc_ref, *refs):
        for k in range(n):
            total = refs[k][...] + refs[n + k][...]
            refs[3 * n + k][...] = total.astype(BF16)

            @pl.when(pl.program_id(1) == sc_ref[0])
            def _(k=k, total=total):
                refs[2 * n + k][...] = total

    theirs = pl.BlockSpec((None, tr, HALF), lambda i, s, sc_ref: (s, i, 0))
    mine = pl.BlockSpec((None, tr, HALF), lambda i, s, sc_ref: (s, i, sc_ref[1]))
    kept = pl.BlockSpec((tr, HALF), lambda i, s, sc_ref: (i, 0))
    outs = pl.pallas_call(
        body,
        name="add_halves",
        grid_spec=pltpu.PrefetchScalarGridSpec(
            num_scalar_prefetch=1,
            grid=(rows // tr, N_SHARDS),
            in_specs=[mine] * n + [theirs] * n,
            out_specs=[kept] * n + [theirs] * n,
        ),
        out_shape=[pltpu.HBM((rows, HALF), F32)] * n + [pltpu.HBM((N_SHARDS, rows, HALF), BF16)] * n,
        compiler_params=_params(48, ("parallel", "arbitrary")),
    )(shard_core, *[_in_hbm(a) for a in list(grads4) + list(recvs)])
    return list(zip(outs[:n], outs[n:]))


def _add_partials(part, recv3, shard_core):
    rows, _ = part.shape
    tr = _row_tile(rows, 16)

    def body(sc_ref, p_ref, r_ref, o_ref):
        o_ref[...] = ((p_ref[...] + r_ref[0].astype(F32)) + r_ref[1].astype(F32)) + r_ref[2].astype(F32)

    return pl.pallas_call(
        body,
        name="add_partials",
        grid_spec=pltpu.PrefetchScalarGridSpec(
            num_scalar_prefetch=1,
            grid=(rows // tr,),
            in_specs=[
                pl.BlockSpec((tr, HALF), lambda i, sc_ref: (i, 0)),
                pl.BlockSpec((3, tr, HALF), lambda i, sc_ref: (0, i, 0)),
            ],
            out_specs=pl.BlockSpec((tr, HALF), lambda i, sc_ref: (i, sc_ref[1])),
        ),
        out_shape=pltpu.HBM((rows, 2 * HALF), F32),
        compiler_params=_params(32, ("parallel",)),
    )(shard_core, _in_hbm(part), _in_hbm(recv3))


def _adam_math(w, g, m, v):
    m = ADAM_B1 * m + (1.0 - ADAM_B1) * g
    v = ADAM_B2 * v + (1.0 - ADAM_B2) * (g * g)
    m_hat = m / (1.0 - ADAM_B1**ADAM_STEP)
    v_hat = v / (1.0 - ADAM_B2**ADAM_STEP)
    delta = -ADAM_LR * (m_hat / (jnp.sqrt(v_hat) + ADAM_EPS) + ADAM_WD * w)
    return delta, m, v


def _adamw(w, g, m, v):
    rows, cols = w.shape
    tr = _row_tile(rows)

    def body(w_ref, g_ref, m_ref, v_ref, go_ref, d_ref, mo_ref, vo_ref):
        gv = g_ref[...]
        go_ref[...] = gv
        d_ref[...], mo_ref[...], vo_ref[...] = _adam_math(w_ref[...], gv, m_ref[...], v_ref[...])

    spec = pl.BlockSpec((tr, cols), lambda i: (i, 0))
    return pl.pallas_call(
        body, name="adamw", grid=(rows // tr,), in_specs=[spec] * 4, out_specs=[spec] * 4, out_shape=[pltpu.HBM(w.shape, F32)] * 4,
        compiler_params=_params(32, ("parallel",)),
    )(_in_hbm(w), _in_hbm(g), _in_hbm(m), _in_hbm(v))


SMALL_ROWS = 560
DECAY_ROWS = 8
SMALL_TOTAL = SMALL_ROWS + 2 * N_SHARDS * DECAY_ROWS


def _adamw_small(gathered, own, wp, mp, vp):
    out_rows = SMALL_ROWS + 2 * DECAY_ROWS

    def body(ga_ref, own_ref, w_ref, m_ref, v_ref, g_ref, d_ref, mo_ref, vo_ref):
        x, y, c = _position()
        shard, me = 2 * x + y, 4 * x + 2 * y + c
        total = lambda rows: functools.reduce(lambda a, b: a + b, [jnp.where(me == d, own_ref[rows, :], ga_ref[d, rows, :]) for d in range(8)])
        g_ref[pl.ds(0, SMALL_ROWS), :] = total(pl.ds(0, SMALL_ROWS))
        for k in range(2):
            start = pl.multiple_of(SMALL_ROWS + k * N_SHARDS * DECAY_ROWS + shard * DECAY_ROWS, DECAY_ROWS)
            g_ref[pl.ds(SMALL_ROWS + k * DECAY_ROWS, DECAY_ROWS), :] = total(pl.ds(start, DECAY_ROWS))
        d_ref[...], mo_ref[...], vo_ref[...] = _adam_math(w_ref[...], g_ref[...], m_ref[...], v_ref[...])

    shape = jax.ShapeDtypeStruct((out_rows, LANES), F32)
    return pl.pallas_call(body, name="adamw_small", out_shape=[shape] * 4, compiler_params=_params(32, None))(gathered, own, wp, mp, vp)


ANY = pl.BlockSpec(memory_space=pl.ANY)


def _position():
    return lax.axis_index("x"), lax.axis_index("y"), lax.axis_index("c")


def _other_chips(x, y):
    return [(1 - x, y), (x, 1 - y), (1 - x, 1 - y)]


HBM = pl.BlockSpec(memory_space=pltpu.HBM)
SEM = pl.BlockSpec(memory_space=pltpu.SEMAPHORE)
TOKEN = jax.ShapeDtypeStruct(TOKEN_SHAPE, F32)
DATAFLOW = pltpu.SideEffectType.DATAFLOW_SIDE_EFFECTING


def _half_block(ref4, slot, core):
    return ref4.at[slot, :, pl.ds(pl.multiple_of(core * HALF, HALF), HALF)]


def _gather_ici_copies(bufs, lands, send_sems, recv_sems):
    x, y, c = _position()
    pairs = []
    for k, ref4 in enumerate(bufs):
        mine = _half_block(ref4, 2 * x + y, c)
        for j, (px, py) in enumerate(_other_chips(x, y)):
            sems = dict(send_sem=send_sems.at[3 * k + j], recv_sem=recv_sems.at[3 * k + j], device_id=(px, py, c), device_id_type=MESH)
            pairs.append((functools.partial(pltpu.make_async_remote_copy, src_ref=mine, dst_ref=mine, **sems),
                          functools.partial(pltpu.make_async_remote_copy, src_ref=mine, dst_ref=_half_block(ref4, 2 * px + py, c), **sems)))
    return pairs


def _gather_d2d_copies(bufs, lands, send_sems, recv_sems):
    x, y, c = _position()
    pairs = []
    for k, ref4 in enumerate(bufs):
        for j, (px, py) in enumerate(_other_chips(x, y)):
            have = _half_block(ref4, 2 * px + py, c)
            sems = dict(send_sem=send_sems.at[3 * k + j], recv_sem=recv_sems.at[3 * k + j], device_id=(x, y, 1 - c), device_id_type=MESH)
            pairs.append((functools.partial(pltpu.make_async_remote_copy, src_ref=have, dst_ref=have, **sems),
                          functools.partial(pltpu.make_async_remote_copy, src_ref=have, dst_ref=_half_block(ref4, 2 * px + py, 1 - c), **sems)))
    return pairs


def _gather_forward(bufs):
    n = len(bufs)

    def body(*refs):
        outs = refs[n : 2 * n]
        send_sems, recv_sems = refs[2 * n :]
        d2d = _gather_d2d_copies(outs, (), send_sems, recv_sems)
        for forward, _ in d2d:
            forward().start()
        for forward, arrival in d2d:
            arrival().wait_recv()
            forward().wait_send()

    return pl.pallas_call(
        body,
        name="gather_forward",
        in_specs=[ANY] * n,
        out_specs=[ANY] * n,
        out_shape=[jax.ShapeDtypeStruct(b.shape, b.dtype) for b in bufs],
        input_output_aliases={k: k for k in range(n)},
        scratch_shapes=[pltpu.SemaphoreType.DMA((3 * n,)), pltpu.SemaphoreType.DMA((3 * n,))],
        compiler_params=pltpu.CompilerParams(has_side_effects=True),
    )(*bufs)


def _exchange_halves(grads4):
    n = len(grads4)

    def body(*refs):
        ins, outs = refs[:n], refs[n : 2 * n]
        send_sems, recv_sems = refs[2 * n :]
        x, y, c = _position()
        copies = []
        for k in range(n):
            cp = pltpu.make_async_remote_copy(
                src_ref=ins[k].at[:, :, pl.ds(pl.multiple_of((1 - c) * HALF, HALF), HALF)], dst_ref=outs[k],
                send_sem=send_sems.at[k], recv_sem=recv_sems.at[k], device_id=(x, y, 1 - c), device_id_type=MESH)
            cp.start()
            copies.append(cp)
        for cp in copies:
            cp.wait()

    return pl.pallas_call(
        body,
        name="exchange_halves",
        in_specs=[ANY] * n,
        out_specs=[ANY] * n,
        out_shape=[jax.ShapeDtypeStruct((N_SHARDS, g.shape[1], HALF), g.dtype) for g in grads4],
        scratch_shapes=[pltpu.SemaphoreType.DMA((n,)), pltpu.SemaphoreType.DMA((n,))],
        compiler_params=pltpu.CompilerParams(has_side_effects=True),
    )(*grads4)


def _both_ends(**copy):
    maker = functools.partial(pltpu.make_async_remote_copy, **copy)
    return maker, maker


def _scatter_copies(parts, lands, send_sems, recv_sems):
    x, y, c = _position()
    return [_both_ends(src_ref=parts[k].at[2 * px + py], dst_ref=lands[k].at[j], send_sem=send_sems.at[3 * k + j],
                       recv_sem=recv_sems.at[3 * k + j], device_id=(px, py, c), device_id_type=MESH)
            for k in range(len(parts)) for j, (px, py) in enumerate(_other_chips(x, y))]


def _exchange_copies(grads, lands, send_sems, recv_sems):
    x, y, c = _position()
    return [_both_ends(src_ref=grads[k].at[:, :, pl.ds(pl.multiple_of((1 - c) * HALF, HALF), HALF)], dst_ref=lands[k],
                       send_sem=send_sems.at[k], recv_sem=recv_sems.at[k], device_id=(x, y, 1 - c), device_id_type=MESH)
            for k in range(len(grads))]


def _exchange_lands(grads4):
    return [jax.ShapeDtypeStruct((N_SHARDS, g.shape[1], HALF), g.dtype) for g in grads4]


def _scatter_lands(parts4):
    return [jax.ShapeDtypeStruct((3,) + g.shape[1:], g.dtype) for g in parts4]


def _small_gather_copies(blocks, lands, send_sems, recv_sems):
    x, y, c = _position()
    flip = lambda v, bit: 1 - v if bit else v
    return [_both_ends(src_ref=blocks[0], dst_ref=lands[0].at[4 * x + 2 * y + c], send_sem=send_sems.at[r - 1],
                       recv_sem=recv_sems.at[r - 1], device_id=(flip(x, r & 4), flip(y, r & 2), flip(c, r & 1)), device_id_type=MESH)
            for r in range(1, 8)]


def _split_start(name, srcs, land_shapes, make_copies, nsem, after=()):
    n, nl, na = len(srcs), len(land_shapes), len(after)
    lands = [lax.empty(a.shape, a.dtype) for a in land_shapes]

    def body(*refs):
        send_sems, recv_sems = refs[n + nl + na], refs[n + nl + na + 1]
        token = refs[2 * (n + nl) + na + 2]
        for send, _ in make_copies(refs[:n], refs[n : n + nl], send_sems, recv_sems):
            send().start()
        token[...] = jnp.zeros_like(token)

    hbm = lambda a: pltpu.HBM(a.shape, a.dtype)
    out = pl.pallas_call(
        body,
        name=name,
        in_specs=[HBM] * (n + nl) + [ANY] * na,
        out_specs=(SEM, SEM, *[HBM] * (n + nl), pl.BlockSpec(memory_space=pltpu.VMEM)),
        out_shape=(pltpu.SemaphoreType.DMA((nsem,)), pltpu.SemaphoreType.DMA((nsem,)), *[hbm(a) for a in list(srcs) + lands], TOKEN),
        input_output_aliases={k: 2 + k for k in range(n + nl)},
        compiler_params=pltpu.CompilerParams(has_side_effects=DATAFLOW),
    )(*[pltpu.with_memory_space_constraint(a, pltpu.HBM) for a in list(srcs) + lands], *after)
    return out[0], out[1], list(out[2 : 2 + n]), list(out[2 + n : 2 + n + nl]), out[2 + n + nl]


def _split_wait(name, send_sems, recv_sems, srcs, lands, make_copies, after):
    n, nl = len(srcs), len(lands)

    def body(*refs):
        for send, arrival in make_copies(refs[:n], refs[n : n + nl], refs[n + nl], refs[n + nl + 1]):
            send().wait_send()
            arrival().wait_recv()

    hbm = lambda a: pltpu.HBM(a.shape, a.dtype)
    out = pl.pallas_call(
        body,
        name=name,
        in_specs=[HBM] * (n + nl) + [SEM, SEM] + [ANY] * len(after),
        out_specs=tuple([HBM] * (n + nl)),
        out_shape=tuple(hbm(a) for a in list(srcs) + list(lands)),
        input_output_aliases={k: k for k in range(n + nl)},
        compiler_params=pltpu.CompilerParams(has_side_effects=DATAFLOW),
    )(*srcs, *lands, send_sems, recv_sems, *after)
    return list(out[:n]), list(out[n:])


def _join_copies(bufs, lands, send_sems, recv_sems):
    x, y, c = _position()
    half = lambda ref, core: ref.at[:, pl.ds(pl.multiple_of(core * HALF, HALF), HALF)]
    pairs = []
    for k, ref in enumerate(bufs):
        sems = dict(send_sem=send_sems.at[k], recv_sem=recv_sems.at[k], device_id=(x, y, 1 - c), device_id_type=MESH)
        pairs.append((functools.partial(pltpu.make_async_remote_copy, src_ref=half(ref, c), dst_ref=half(ref, c), **sems),
                      functools.partial(pltpu.make_async_remote_copy, src_ref=half(ref, c), dst_ref=half(ref, 1 - c), **sems)))
    return pairs


def _join_halves(bufs, after=()):
    n, na = len(bufs), len(after)

    def body(*refs):
        joins = _join_copies(refs[n + na : 2 * n + na], (), *refs[2 * n + na :])
        for send, _ in joins:
            send().start()
        for send, arrival in joins:
            send().wait_send()
            arrival().wait_recv()

    return pl.pallas_call(
        body,
        name="join_halves",
        in_specs=[ANY] * (n + na),
        out_specs=[ANY] * n,
        out_shape=[jax.ShapeDtypeStruct(b.shape, b.dtype) for b in bufs],
        input_output_aliases={k: k for k in range(n)},
        scratch_shapes=[pltpu.SemaphoreType.DMA((n,)), pltpu.SemaphoreType.DMA((n,))],
        compiler_params=pltpu.CompilerParams(has_side_effects=True),
    )(*bufs, *after)


def _allgather_small(block):
    m_per, ncol = block.shape

    def body(x_ref, out_ref, send_sems, recv_sems, local_sem):
        x, y, c = _position()
        me, sibling = (x, y, c), (x, y, 1 - c)
        chips = _other_chips(x, y)

        def rows(px, py, pc):
            return out_ref.at[4 * px + 2 * py + pc]

        def copy(k, blk, to, src=None):
            return pltpu.make_async_remote_copy(
                src_ref=rows(*blk) if src is None else src, dst_ref=rows(*blk),
                send_sem=send_sems.at[k], recv_sem=recv_sems.at[k], device_id=to, device_id_type=MESH)

        mine = pltpu.make_async_copy(x_ref, rows(*me), local_sem)
        mine.start()
        first = [copy(0, me, sibling, src=x_ref)] + [copy(1 + j, me, (*chip, c), src=x_ref) for j, chip in enumerate(chips)]
        for cp in first:
            cp.start()
        passed = [copy(4 + j, (*chip, c), sibling) for j, chip in enumerate(chips)]
        for j, chip in enumerate(chips):
            copy(1 + j, (*chip, c), me).wait_recv()
            passed[j].start()
        copy(0, sibling, me).wait_recv()
        for j, chip in enumerate(chips):
            copy(4 + j, (*chip, 1 - c), me).wait_recv()
        for cp in first + passed:
            cp.wait_send()
        mine.wait()

    return pl.pallas_call(
        body,
        name="allgather_small",
        in_specs=[pl.BlockSpec(memory_space=pltpu.VMEM)],
        out_specs=pl.BlockSpec(memory_space=pltpu.VMEM),
        out_shape=jax.ShapeDtypeStruct((8, m_per, ncol), block.dtype),
        scratch_shapes=[pltpu.SemaphoreType.DMA((7,)), pltpu.SemaphoreType.DMA((7,)), pltpu.SemaphoreType.DMA],
        compiler_params=pltpu.CompilerParams(has_side_effects=True, vmem_limit_bytes=32 * MIB),
    )(block)


SMALL_NAMES = ["norm1_g", "b_decay_f", "b_decay_b", "gla_norm_g", "gmlp_ln_g", "gmlp_ln_b", "w_spatial", "b_spatial", "norm2_g", "final_norm_g"]


def _pack_small(parts, decay_parts):
    flat = jnp.concatenate([a.reshape(-1) for a in parts])
    flat = jnp.pad(flat, (0, SMALL_ROWS * LANES - flat.shape[0])).reshape(SMALL_ROWS, LANES)
    return jnp.concatenate([flat] + [d.reshape(-1, LANES) for d in decay_parts], axis=0)


def _unpack_small(packed, like):
    out, off = [], 0
    flat = packed[:SMALL_ROWS].reshape(-1)
    for a in like:
        out.append(flat[off : off + a.size].reshape(a.shape))
        off += a.size
    return out


def kernel(x, norm1_g, w_in, w_decay_f, b_decay_f, w_decay_b, b_decay_b, gla_norm_g, gmlp_ln_g, gmlp_ln_b, w_spatial, b_spatial, w_out, norm2_g, w_gate, w_up, w_down, final_norm_g, loss_target, m_norm1_g, m_w_in, m_w_decay_f, m_b_decay_f, m_w_decay_b, m_b_decay_b, m_gla_norm_g, m_gmlp_ln_g, m_gmlp_ln_b, m_w_spatial, m_b_spatial, m_w_out, m_norm2_g, m_w_gate, m_w_up, m_w_down, m_final_norm_g, v_norm1_g, v_w_in, v_w_decay_f, v_b_decay_f, v_w_decay_b, v_b_decay_b, v_gla_norm_g, v_gmlp_ln_g, v_gmlp_ln_b, v_w_spatial, v_b_spatial, v_w_out, v_norm2_g, v_w_gate, v_w_up, v_w_down, v_final_norm_g):
    args = dict(locals())
    cx, cy, cc = lax.axis_index("x"), lax.axis_index("y"), lax.axis_index("c")
    shard = 2 * cx + cy
    xs = x[0]
    target = loss_target[0]

    big_names = ["w_in", "w_out", "w_gate", "w_up", "w_down"]
    transposed = ("w_in", "w_gate", "w_up")
    rows_of = lambda pre, k: jnp.transpose(args[pre + k][0]) if k in transposed else args[pre + k][0]
    big_shards = {k: rows_of("", k) for k in big_names}
    s_arr = shard.reshape(1).astype(jnp.int32)
    sc_arr = jnp.stack([shard, cc]).astype(jnp.int32)
    zero_token = jnp.zeros(TOKEN_SHAPE, F32)
    w_send, w_recv, (w_in4,), _, token_w_in = _split_start(
        "w_in_gather_start", [_cast_into_slot(big_shards["w_in"], s_arr, zero_token)], [], _gather_ici_copies, 3)
    late = ["w_out", "w_gate", "w_up", "w_down"]
    late_slots = [_cast_into_slot(big_shards[k], s_arr, token_w_in) for k in late]
    dec_block = jnp.concatenate([w_decay_f[0].reshape(-1, LANES), w_decay_b[0].reshape(-1, LANES)], axis=0)
    dec_all = _allgather_small(dec_block)
    (w_in4,), _ = _split_wait("w_in_gather_wait", w_send, w_recv, [w_in4], [], _gather_ici_copies, (dec_all, *late_slots))
    (w_in4,) = _gather_forward([w_in4])
    w_in_t = w_in4.reshape(PROJ_W, D_MODEL)
    g_send, g_recv, late_bufs, _, token_gather = _split_start(
        "gather_start", late_slots, [], _gather_ici_copies, 3 * len(late), after=(w_in4,))
    dec_all = dec_all[::2].reshape(N_SHARDS, 2, LOWRANK, KEY_W // N_SHARDS)
    wdf_full = jnp.transpose(dec_all[:, 0], (1, 0, 2)).reshape(LOWRANK, KEY_W)
    wdb_full = jnp.transpose(dec_all[:, 1], (1, 0, 2)).reshape(LOWRANK, KEY_W)
    wd_pad_f = jnp.zeros((LANES, KEY_W), F32).at[0:LOWRANK].set(wdf_full).astype(BF16)
    wd_pad_b = jnp.zeros((LANES, KEY_W), F32).at[LOWRANK : 2 * LOWRANK].set(wdb_full).astype(BF16)

    ws_bf = w_spatial[0].astype(BF16)
    wst_bf = jnp.transpose(w_spatial[0], (0, 2, 1)).astype(BF16)
    bs_col = b_spatial[0].reshape(GMLP_GROUPS, GMLP_CHUNK, 1)

    p = _inproj(xs, norm1_g, w_in_t, token_gather)
    o_f, st_f = _gla_fwd(p, wd_pad_f, b_decay_f, token_gather, reverse=False)
    o_b, st_b = _gla_fwd(p, wd_pad_b, b_decay_b, token_gather, reverse=True)
    late_bufs, _ = _split_wait("gather_wait", g_send, g_recv, late_bufs, [], _gather_ici_copies, (o_f, o_b))
    (w_out4,) = _gather_forward(late_bufs[:1])
    f_send, f_recv, ffn_bufs, _, token_forward = _split_start(
        "forward_start", late_bufs[1:], [], _gather_d2d_copies, 3 * (len(late) - 1), after=(w_out4,))
    w_out_full = w_out4.reshape(-1, D_MODEL)
    x1, ycat = _mixer_out(xs, o_f, o_b, p, gla_norm_g, gmlp_ln_g, gmlp_ln_b, ws_bf, bs_col, w_out_full, token_forward)
    ffn_bufs, _ = _split_wait("forward_wait", f_send, f_recv, ffn_bufs, [], _gather_d2d_copies, (x1,))
    wg_t, wu_t, wd = [b.reshape(-1, D_MODEL) for b in ffn_bufs]
    gf = final_norm_g.reshape(1, D_MODEL)
    h2, gate, up, act, dx2, loss_acc, dgf = _ffn_fwd(x1, target, norm2_g, gf, wg_t, wu_t, wd)

    dgate, dup, dx1, dg2 = _ffn_bwd(dx2, gate, up, x1, norm2_g, wg_t, wu_t, wd)
    ffn_grads4 = [g.reshape(N_SHARDS, FF_SHARD, D_MODEL) for g in _ffn_wgrad(h2, dgate, dup, act, dx2)]
    e_send, e_recv, e_srcs, e_lands, token_exchange = _split_start(
        "exchange_start", ffn_grads4, _exchange_lands(ffn_grads4), _exchange_copies, len(ffn_grads4))
    do, dg, du, dvv, dwo, dgn, dlng, dlnb, dws, dbs = _mixer_bwd(
        dx1, ycat, o_f, o_b, p, gla_norm_g, gmlp_ln_g, gmlp_ln_b, ws_bf, wst_bf, bs_col, w_out_full, token_exchange)
    ffn_mine, ffn_other = _split_wait("exchange_wait", e_send, e_recv, e_srcs, e_lands, _exchange_copies, (do,))
    ffn_parts = _add_halves(ffn_mine, ffn_other, sc_arr)
    ffn_payload = [pb for _, pb in ffn_parts]
    s_send, s_recv, s_parts, s_lands, token_scatter = _split_start(
        "scatter_start", ffn_payload, _scatter_lands(ffn_payload), _scatter_copies, 3 * len(ffn_payload))
    dq_f, dk_f, dv_f, dlr_f, dwdec_f, dbdec_f = _gla_bwd(p, do, st_f, wd_pad_f, b_decay_f, token_scatter, reverse=False)
    dq, dk, dv, dlr, dwdec_b, dbdec_b = _gla_bwd(
        p, do, st_b, wd_pad_b, b_decay_b, token_scatter, reverse=True, other=(dq_f, dk_f, dv_f, dlr_f))
    dwin_t, dp = _inproj_wgrad(xs, norm1_g, dq, dk, dv, dg, du, dvv, dlr)
    _, ffn_recv = _split_wait("scatter_wait", s_send, s_recv, s_parts, s_lands, _scatter_copies, (dwin_t,))
    ffn_bufs = [_add_partials(pf, r, sc_arr) for (pf, _), r in zip(ffn_parts, ffn_recv)]

    dwin4 = dwin_t.reshape(N_SHARDS, PROJ_W // N_SHARDS, D_MODEL)
    dwo4 = dwo.reshape(N_SHARDS, D_MODEL // N_SHARDS, D_MODEL)
    proj_grads4 = [dwin4, dwo4]
    proj_parts = [_add_halves([g], [r], sc_arr)[0] for g, r in zip(proj_grads4, _exchange_halves(proj_grads4))]
    proj_payload = [pb for _, pb in proj_parts]
    p_send, p_recv, p_parts, p_lands, token_proj = _split_start(
        "proj_scatter_start", proj_payload, _scatter_lands(proj_payload), _scatter_copies, 3 * len(proj_payload))
    j_send, j_recv, ffn_bufs, _, token_join = _split_start("join_start", ffn_bufs, [], _join_copies, len(ffn_bufs), after=(token_proj,))
    dx, dg1 = _inproj_dx(xs, dx1, norm1_g, w_in_t, dp, token_join)
    _, proj_recv = _split_wait("proj_scatter_wait", p_send, p_recv, p_parts, p_lands, _scatter_copies, (dx,))

    dwdec_f16 = dwdec_f[0:LOWRANK]
    dwdec_b16 = dwdec_b[LOWRANK : 2 * LOWRANK]
    shard_major = lambda a: jnp.transpose(a.reshape(LOWRANK, N_SHARDS, KEY_W // N_SHARDS), (1, 0, 2))
    small_grads = {
        "norm1_g": dg1, "b_decay_f": dbdec_f, "b_decay_b": dbdec_b, "gla_norm_g": dgn, "gmlp_ln_g": dlng, "gmlp_ln_b": dlnb,
        "w_spatial": dws, "b_spatial": dbs, "norm2_g": dg2, "final_norm_g": dgf,
    }
    g_pack = _pack_small([small_grads[k] for k in SMALL_NAMES] + [loss_acc], [shard_major(dwdec_f16), shard_major(dwdec_b16)])
    sg_send, sg_recv, (g_pack,), g_lands, token_small = _split_start(
        "small_gather_start", [g_pack], [jax.ShapeDtypeStruct((8, SMALL_TOTAL, LANES), F32)], _small_gather_copies, 7)

    ffn_bufs, _ = _split_wait("join_wait", j_send, j_recv, ffn_bufs, [], _join_copies, (dx, token_small))
    proj_bufs = [_add_partials(pf, r, sc_arr) for (pf, _), r in zip(proj_parts, proj_recv)]
    big_grads = dict(zip(big_names, list(_join_halves(proj_bufs, after=(token_small,))) + ffn_bufs))
    big_updates = {k: _adamw(big_shards[k], big_grads[k], rows_of("m_", k), rows_of("v_", k)) for k in big_names}

    (g_pack,), (g_all,) = _split_wait(
        "small_gather_wait", sg_send, sg_recv, [g_pack], g_lands, _small_gather_copies, tuple(u[1] for u in big_updates.values()))
    pack_own = lambda pre: _pack_small([args[pre + k] for k in SMALL_NAMES], [args[pre + "w_decay_f"], args[pre + "w_decay_b"]])
    sg, sd, sm, sv = _adamw_small(g_all, g_pack, pack_own(""), pack_own("m_"), pack_own("v_"))

    names = ["norm1_g", "w_in", "w_decay_f", "b_decay_f", "w_decay_b", "b_decay_b", "gla_norm_g", "gmlp_ln_g", "gmlp_ln_b",
             "w_spatial", "b_spatial", "w_out", "norm2_g", "w_gate", "w_up", "w_down", "final_norm_g"]
    like = [args[k] for k in SMALL_NAMES]
    results = {"g": {}, "d": {}, "m": {}, "v": {}}
    for tag, packed in (("g", sg), ("d", sd), ("m", sm), ("v", sv)):
        for k, a in zip(SMALL_NAMES, _unpack_small(packed, like)):
            results[tag][k] = a
        results[tag]["w_decay_f"] = packed[SMALL_ROWS : SMALL_ROWS + DECAY_ROWS].reshape(w_decay_f.shape)
        results[tag]["w_decay_b"] = packed[SMALL_ROWS + DECAY_ROWS :].reshape(w_decay_b.shape)
    for k in big_names:
        for tag, a in zip("gdmv", big_updates[k]):
            results[tag][k] = (jnp.transpose(a) if k in transposed else a).reshape(args[k].shape)

    loss = sg[:SMALL_ROWS].reshape(-1)[sum(a.size for a in like)]
    grad_x = dx.reshape(x.shape)
    return (loss, grad_x, *[results["g"][k] for k in names], *[results["d"][k] for k in names],
            *[results["m"][k] for k in names], *[results["v"][k] for k in names])
```

```python
import functools
import math

import jax
import jax.numpy as jnp
from jax import lax
from jax.experimental import pallas as pl
from jax.experimental.pallas import tpu as pltpu

F32, BF16 = jnp.float32, jnp.bfloat16

D_MODEL = 1024
GLA_HEADS = 4
GLA_DK = 64
GLA_DV = 128
KEY_W = GLA_HEADS * GLA_DK
GLA_W = GLA_HEADS * GLA_DV
GMLP_W = 512
GMLP_GROUPS = 4
GMLP_CHUNK = 128
LOWRANK = 16
GLA_CHUNK = 64
GLA_TAU = 16.0
PROJ_W = 2592
PROJ_WP = 2688
D_FF = 2816
N_SHARDS = 4
FF_SHARD = D_FF // N_SHARDS
EPS = 1e-6
LANES = 128
TOKEN_SHAPE = (8, LANES)
MIB = 1024 * 1024

ADAM_LR = 0.001
ADAM_B1 = 0.9
ADAM_B2 = 0.999
ADAM_EPS = 1e-08
ADAM_WD = 0.01
ADAM_STEP = 10

COL_Q, COL_K = 0, 256
COL_V, COL_G, COL_U, COL_VV = 512, 1024, 1536, 2048
COL_LR = 2560
ROW_LR, ROW_UV = 1536, 1568
HALF = D_MODEL // 2

MESH = pl.DeviceIdType.MESH


def _nn(a, b):
    return jnp.dot(a, b, preferred_element_type=F32)


def _nt(a, b):
    return lax.dot_general(a, b, (((1,), (1,)), ((), ())), preferred_element_type=F32)


def _tn(a, b):
    return lax.dot_general(a, b, (((0,), (0,)), ((), ())), preferred_element_type=F32)


def _bnn(a, b):
    return jnp.einsum("nik,nkj->nij", a, b, preferred_element_type=F32)


def _bnt(a, b):
    return jnp.einsum("nik,njk->nij", a, b, preferred_element_type=F32)


def _btn(a, b):
    return jnp.einsum("nki,nkj->nij", a, b, preferred_element_type=F32)


def _resident(shape):
    zeros = (0,) * len(shape)
    return pl.BlockSpec(shape, lambda *_: zeros, pipeline_mode=pl.Buffered(1))


def _params(vmem_mib, semantics=("arbitrary",)):
    return pltpu.CompilerParams(vmem_limit_bytes=vmem_mib * MIB, dimension_semantics=semantics)


def _sigmoid(x):
    return 1.0 / (1.0 + jnp.exp(-x))


def _gelu(x):
    return 0.5 * x * (1.0 + lax.erf(x * (1.0 / math.sqrt(2.0))))


def _gelu_and_grad(x):
    cdf = 0.5 * (1.0 + lax.erf(x * (1.0 / math.sqrt(2.0))))
    return x * cdf, cdf + x * jnp.exp(-0.5 * x * x) * (1.0 / math.sqrt(2.0 * math.pi))


def _log_sigmoid(x):
    return jnp.minimum(x, 0.0) - jnp.log(1.0 + jnp.exp(-jnp.abs(x)))


def _rms_bwd(dxh, xh, r):
    return r * (dxh - xh * jnp.mean(dxh * xh, axis=-1, keepdims=True))


def _chunk_cumsum(v, row_in_chunk, reverse):
    rows = v.shape[0]
    for sh in (1, 2, 4, 8, 16, 32):
        if reverse:
            v = v + jnp.where(row_in_chunk + sh < GLA_CHUNK, pltpu.roll(v, rows - sh, axis=0), 0.0)
        else:
            v = v + jnp.where(row_in_chunk >= sh, pltpu.roll(v, sh, axis=0), 0.0)
    return v


def _inproj(x, g1, w_in_t, token):
    seq = x.shape[0]
    tm = min(seq, 512)

    def body(x_ref, g_ref, w_ref, token_ref, p_ref):
        xv = x_ref[...]
        r = lax.rsqrt(jnp.mean(xv * xv, axis=-1, keepdims=True) + EPS)
        h = (xv * r * g_ref[...]).astype(BF16)
        p_ref[:, 0:COL_U] = _nt(h, w_ref[0:ROW_LR, :])
        p_ref[:, COL_U:COL_LR] = _nt(h, w_ref[ROW_UV:PROJ_W, :])
        p_ref[:, COL_LR:PROJ_WP] = _nt(h, w_ref[ROW_LR : ROW_LR + LANES, :])

    return pl.pallas_call(
        body,
        name="inproj",
        grid=(seq // tm,),
        in_specs=[pl.BlockSpec((tm, D_MODEL), lambda i: (i, 0)), _resident((1, D_MODEL)), _resident((PROJ_W, D_MODEL)), _resident(TOKEN_SHAPE)],
        out_specs=pl.BlockSpec((tm, PROJ_WP), lambda i: (i, 0)),
        out_shape=jax.ShapeDtypeStruct((seq, PROJ_WP), F32),
        compiler_params=_params(48, ("parallel",)),
    )(x, g1, w_in_t, token)


def _gla_tile(seq):
    return min(seq, 1024)


def _gla_decay_terms(lr_bf, wd_ref, bd_ref, pair, row_in_chunk, reverse, n):
    cols = pl.ds(pair * LANES, LANES)
    pre = _nn(lr_bf, wd_ref[:, cols]) + bd_ref[:, cols]
    la = _log_sigmoid(pre) * (1.0 / GLA_TAU)
    b = _chunk_cumsum(la, row_in_chunk, reverse)
    b3 = b.reshape(n, GLA_CHUNK, LANES)
    blast = b3[:, 0:1, :] if reverse else b3[:, GLA_CHUNK - 1 : GLA_CHUNK, :]
    return pre, b3, blast


def _gla_fwd(p, wd_pad, bd, token, reverse):
    seq = p.shape[0]
    tg = _gla_tile(seq)
    nt = seq // tg
    n = tg // GLA_CHUNK
    scale = GLA_DK**-0.5

    def tile(i):
        return nt - 1 - i if reverse else i

    def body(q_ref, k_ref, v_ref, lr_ref, wd_ref, bd_ref, token_ref, o_ref, st_ref, carry):
        @pl.when(pl.program_id(0) == 0)
        def _():
            carry[...] = jnp.zeros_like(carry)

        lr_bf = lr_ref[...].astype(BF16)
        states = [carry[h] for h in range(GLA_HEADS)]
        row_in_chunk = lax.broadcasted_iota(jnp.int32, (tg, LANES), 0) % GLA_CHUNK
        lane_head = lax.broadcasted_iota(jnp.int32, (1, LANES), 1) // GLA_DK
        tt = lax.broadcasted_iota(jnp.int32, (GLA_CHUNK, GLA_CHUNK), 0)
        ss = lax.broadcasted_iota(jnp.int32, (GLA_CHUNK, GLA_CHUNK), 1)
        causal = (tt <= ss) if reverse else (tt >= ss)
        order = range(n - 1, -1, -1) if reverse else range(n)
        heads = range(GLA_HEADS)
        qds, vhs, decs, sc_raw, dst = {}, {}, {}, {}, {}
        for pair in range(2):
            cols = pl.ds(pair * LANES, LANES)
            _, b3, blast = _gla_decay_terms(lr_bf, wd_ref, bd_ref, pair, row_in_chunk, reverse, n)
            q3 = q_ref[:, cols].reshape(n, GLA_CHUNK, LANES) * scale
            k3 = k_ref[:, cols].reshape(n, GLA_CHUNK, LANES)
            qd = q3 * jnp.exp(b3)
            kd = (k3 * jnp.exp(-b3)).astype(BF16)
            kte = k3 * jnp.exp(blast - b3)
            decs[pair] = jnp.exp(blast)
            qds[pair] = qd.astype(BF16)
            m0 = (lane_head == 0).astype(F32)
            m1 = (lane_head == 1).astype(F32)
            q_both = jnp.concatenate([(qd * m0).astype(BF16), (qd * m1).astype(BF16)], axis=1)
            sc_both = _bnt(q_both, kd)
            for hh, m in ((0, m0), (1, m1)):
                h = 2 * pair + hh
                vhs[h] = v_ref[:, pl.ds(h * GLA_DV, GLA_DV)].reshape(n, GLA_CHUNK, GLA_DV).astype(BF16)
                sc_raw[h] = sc_both[:, hh * GLA_CHUNK : (hh + 1) * GLA_CHUNK, :]
                dst[h] = _btn(vhs[h], (kte * m).astype(BF16))
        o_intra, befores = {}, {}
        for h in heads:
            o_intra[h] = _bnn(jnp.where(causal, sc_raw[h], 0.0).astype(BF16), vhs[h])
            st, before = states[h], [None] * n
            for j in order:
                before[j] = st
                st = st * decs[h // 2][j] + dst[h][j]
            states[h] = st
            befores[h] = jnp.stack(before).astype(BF16)
        outs = {}
        for pair in range(2):
            both = jnp.concatenate([befores[2 * pair], befores[2 * pair + 1]], axis=1)
            o_inter = _bnt(qds[pair], both)
            for hh in range(2):
                h = 2 * pair + hh
                outs[h] = (o_intra[h] + o_inter[:, :, hh * GLA_DV : (hh + 1) * GLA_DV]).reshape(tg, GLA_DV)
        for h in range(GLA_HEADS):
            o_ref[:, pl.ds(h * GLA_DV, GLA_DV)] = outs[h]
            st_ref[:, h] = befores[h]
            carry[h] = states[h]

    nchunks = seq // GLA_CHUNK
    return pl.pallas_call(
        body,
        name="gla_fwd_rev" if reverse else "gla_fwd",
        grid=(nt,),
        in_specs=[
            pl.BlockSpec((tg, KEY_W), lambda i: (tile(i), COL_Q // KEY_W)),
            pl.BlockSpec((tg, KEY_W), lambda i: (tile(i), COL_K // KEY_W)),
            pl.BlockSpec((tg, GLA_W), lambda i: (tile(i), COL_V // GLA_W)),
            pl.BlockSpec((tg, LANES), lambda i: (tile(i), COL_LR // LANES)),
            _resident((LANES, KEY_W)),
            _resident((1, KEY_W)),
            _resident(TOKEN_SHAPE),
        ],
        out_specs=[
            pl.BlockSpec((tg, GLA_W), lambda i: (tile(i), 0)),
            pl.BlockSpec((n, GLA_HEADS, GLA_DV, LANES), lambda i: (tile(i), 0, 0, 0)),
        ],
        out_shape=[
            jax.ShapeDtypeStruct((seq, GLA_W), F32),
            jax.ShapeDtypeStruct((nchunks, GLA_HEADS, GLA_DV, LANES), BF16),
        ],
        scratch_shapes=[pltpu.VMEM((GLA_HEADS, GLA_DV, LANES), F32)],
        compiler_params=_params(48),
    )(p, p, p, p, wd_pad, bd, token)


def _mixer_out(x, o_f, o_b, p, gn, lng, lnb, ws_bf, bs_col, w_out, token):
    seq = x.shape[0]
    tm = min(seq, 512)

    def body(x_ref, of_ref, ob_ref, g_ref, u_ref, vv_ref, gn_ref, lng_ref, lnb_ref, ws_ref, bs_ref, wo_ref, token_ref, x1_ref, yc_ref, vn_sc):
        for h in range(GLA_HEADS):
            cols = pl.ds(h * GLA_DV, GLA_DV)
            oh = of_ref[:, cols] + ob_ref[:, cols]
            on = oh * lax.rsqrt(jnp.mean(oh * oh, axis=-1, keepdims=True) + EPS)
            gh = g_ref[:, cols]
            yc_ref[:, cols] = (on * gn_ref[:, cols] * (gh * _sigmoid(gh))).astype(BF16)
        zv = _gelu(vv_ref[...])
        xc = zv - jnp.mean(zv, axis=-1, keepdims=True)
        vhat = xc * lax.rsqrt(jnp.mean(xc * xc, axis=-1, keepdims=True) + EPS)
        vn_sc[...] = (vhat * lng_ref[...] + lnb_ref[...]).astype(BF16)
        for c in range(tm // GMLP_CHUNK):
            rows = pl.ds(c * GMLP_CHUNK, GMLP_CHUNK)
            for g in range(GMLP_GROUPS):
                cols = pl.ds(g * LANES, LANES)
                s = _nn(ws_ref[g], vn_sc[rows, cols]) + bs_ref[g]
                yc_ref[rows, pl.ds(GLA_W + g * LANES, LANES)] = (_gelu(u_ref[rows, cols]) * s).astype(BF16)
        x1_ref[...] = x_ref[...] + _nn(yc_ref[...], wo_ref[...])

    row = lambda w: pl.BlockSpec((tm, w), lambda i: (i, 0))
    pcol = lambda col: pl.BlockSpec((tm, GLA_W), lambda i: (i, col // GLA_W))
    return pl.pallas_call(
        body,
        name="mixer_out",
        grid=(seq // tm,),
        in_specs=[
            row(D_MODEL), row(GLA_W), row(GLA_W), pcol(COL_G), pcol(COL_U), pcol(COL_VV),
            _resident((1, GLA_W)), _resident((1, GMLP_W)), _resident((1, GMLP_W)),
            _resident((GMLP_GROUPS, GMLP_CHUNK, GMLP_CHUNK)), _resident((GMLP_GROUPS, GMLP_CHUNK, 1)),
            _resident((D_MODEL, D_MODEL)), _resident(TOKEN_SHAPE),
        ],
        out_specs=[row(D_MODEL), row(D_MODEL)],
        out_shape=[jax.ShapeDtypeStruct((seq, D_MODEL), F32), jax.ShapeDtypeStruct((seq, D_MODEL), BF16)],
        scratch_shapes=[pltpu.VMEM((tm, GMLP_W), BF16)],
        compiler_params=_params(48, ("parallel",)),
    )(x, o_f, o_b, p, p, p, gn, lng, lnb, ws_bf, bs_col, w_out, token)


def _ffn_fwd(x1, target, g2, gf, wg_t, wu_t, wd):
    seq = x1.shape[0]
    tm = min(seq, 256)

    def body(x1_ref, t_ref, g2_ref, gf_ref, wg_ref, wu_ref, wd_ref, h2_ref, gate_ref, up_ref, act_ref, dx2_ref, loss_ref, dgf_ref):
        @pl.when(pl.program_id(0) == 0)
        def _():
            loss_ref[...] = jnp.zeros_like(loss_ref)
            dgf_ref[...] = jnp.zeros_like(dgf_ref)

        x1v = x1_ref[...]
        h2 = (x1v * lax.rsqrt(jnp.mean(x1v * x1v, axis=-1, keepdims=True) + EPS) * g2_ref[...]).astype(BF16)
        h2_ref[...] = h2
        gate = _nt(h2, wg_ref[...])
        up = _nt(h2, wu_ref[...])
        act = (gate * _sigmoid(gate) * up).astype(BF16)
        gate_ref[...] = gate
        up_ref[...] = up
        act_ref[...] = act
        x2 = x1v + _nn(act, wd_ref[...])
        rf = lax.rsqrt(jnp.mean(x2 * x2, axis=-1, keepdims=True) + EPS)
        xh = x2 * rf
        err = xh * gf_ref[...] - t_ref[...]
        loss_ref[...] += 0.5 * jnp.sum(jnp.mean(err * err, axis=-1, keepdims=True))
        dy = err * (1.0 / D_MODEL)
        dgf_ref[...] += jnp.sum(dy * xh, axis=0, keepdims=True)
        dx2_ref[...] = _rms_bwd(dy * gf_ref[...], xh, rf)

    row = lambda w: pl.BlockSpec((tm, w), lambda i: (i, 0))
    weight = _resident((D_FF, D_MODEL))
    return pl.pallas_call(
        body,
        name="ffn_fwd",
        grid=(seq // tm,),
        in_specs=[row(D_MODEL), row(D_MODEL), _resident((1, D_MODEL)), _resident((1, D_MODEL)), weight, weight, weight],
        out_specs=[row(D_MODEL), row(D_FF), row(D_FF), row(D_FF), row(D_MODEL),
                   pl.BlockSpec((1, LANES), lambda i: (0, 0)), pl.BlockSpec((1, D_MODEL), lambda i: (0, 0))],
        out_shape=[
            jax.ShapeDtypeStruct((seq, D_MODEL), BF16),
            jax.ShapeDtypeStruct((seq, D_FF), F32),
            jax.ShapeDtypeStruct((seq, D_FF), F32),
            jax.ShapeDtypeStruct((seq, D_FF), BF16),
            jax.ShapeDtypeStruct((seq, D_MODEL), F32),
            jax.ShapeDtypeStruct((1, LANES), F32),
            jax.ShapeDtypeStruct((1, D_MODEL), F32),
        ],
        compiler_params=_params(56),
    )(x1, target, g2, gf, wg_t, wu_t, wd)


def _ffn_bwd(dx2, gate, up, x1, g2, wg_t, wu_t, wd):
    seq = x1.shape[0]
    tm = min(seq, 256)

    def body(dx2_ref, gate_ref, up_ref, x1_ref, g2_ref, wg_ref, wu_ref, wd_ref, dgate_ref, dup_ref, dx1_ref, dg2_ref):
        @pl.when(pl.program_id(0) == 0)
        def _():
            dg2_ref[...] = jnp.zeros_like(dg2_ref)

        dx2v = dx2_ref[...]
        dact = _nt(dx2v.astype(BF16), wd_ref[...])
        gate = gate_ref[...]
        sg = _sigmoid(gate)
        dgate = (dact * up_ref[...] * (sg * (1.0 + gate * (1.0 - sg)))).astype(BF16)
        dup = (dact * (gate * sg)).astype(BF16)
        dgate_ref[...] = dgate
        dup_ref[...] = dup
        dh2 = _nn(dgate, wg_ref[...]) + _nn(dup, wu_ref[...])
        x1v = x1_ref[...]
        r2 = lax.rsqrt(jnp.mean(x1v * x1v, axis=-1, keepdims=True) + EPS)
        xh = x1v * r2
        dg2_ref[...] += jnp.sum(dh2 * xh, axis=0, keepdims=True)
        dx1_ref[...] = dx2v + _rms_bwd(dh2 * g2_ref[...], xh, r2)

    row = lambda w: pl.BlockSpec((tm, w), lambda i: (i, 0))
    weight = _resident((D_FF, D_MODEL))
    return pl.pallas_call(
        body,
        name="ffn_bwd",
        grid=(seq // tm,),
        in_specs=[row(D_MODEL), row(D_FF), row(D_FF), row(D_MODEL), _resident((1, D_MODEL)), weight, weight, weight],
        out_specs=[row(D_FF), row(D_FF), row(D_MODEL), pl.BlockSpec((1, D_MODEL), lambda i: (0, 0))],
        out_shape=[
            jax.ShapeDtypeStruct((seq, D_FF), BF16),
            jax.ShapeDtypeStruct((seq, D_FF), BF16),
            jax.ShapeDtypeStruct((seq, D_MODEL), F32),
            jax.ShapeDtypeStruct((1, D_MODEL), F32),
        ],
        compiler_params=_params(56),
    )(dx2, gate, up, x1, g2, wg_t, wu_t, wd)


WGRAD_ROWS = D_FF // 2


def _ffn_wgrad(h2, dgate, dup, act, dx2):
    seq = h2.shape[0]
    tm = min(seq, 512)

    def body(h2_ref, dgate_ref, dup_ref, act_ref, dx2_ref, dwg_ref, dwu_ref, dwd_ref):
        @pl.when(pl.program_id(1) == 0)
        def _():
            dwg_ref[...] = jnp.zeros_like(dwg_ref)
            dwu_ref[...] = jnp.zeros_like(dwu_ref)
            dwd_ref[...] = jnp.zeros_like(dwd_ref)

        h2v = h2_ref[...]
        dwg_ref[...] += _tn(dgate_ref[...], h2v)
        dwu_ref[...] += _tn(dup_ref[...], h2v)
        dwd_ref[...] += _tn(act_ref[...], dx2_ref[...].astype(BF16))

    ff = pl.BlockSpec((tm, WGRAD_ROWS), lambda j, i: (i, j))
    row = pl.BlockSpec((tm, D_MODEL), lambda j, i: (i, 0))
    out = pl.BlockSpec((WGRAD_ROWS, D_MODEL), lambda j, i: (j, 0))
    return pl.pallas_call(
        body,
        name="ffn_wgrad",
        grid=(D_FF // WGRAD_ROWS, seq // tm),
        in_specs=[row, ff, ff, ff, row],
        out_specs=[out, out, out],
        out_shape=[jax.ShapeDtypeStruct((D_FF, D_MODEL), F32)] * 3,
        compiler_params=_params(56, ("parallel", "arbitrary")),
    )(h2, dgate, dup, act, dx2)


def _mixer_bwd(dx1, ycat, o_f, o_b, p, gn, lng, lnb, ws_bf, wst_bf, bs_col, w_out, token):
    seq = dx1.shape[0]
    tm = min(seq, 512)
    nsteps = seq // tm

    def body(dx1_ref, yc_ref, of_ref, ob_ref, g_ref, u_ref, vv_ref, gn_ref, lng_ref, lnb_ref, ws_ref, wst_ref, bs_ref, wo_ref, token_ref,
             do_ref, dg_ref, du_ref, dvv_ref, dwo_ref, dgn_ref, dlng_ref, dlnb_ref, dws_ref, dbs_ref, vn_sc, dvn_sc, dbs_acc):
        step = pl.program_id(0)

        @pl.when(step == 0)
        def _():
            for r in (dwo_ref, dgn_ref, dlng_ref, dlnb_ref, dws_ref, dbs_acc):
                r[...] = jnp.zeros_like(r)

        dx1b = dx1_ref[...].astype(BF16)
        dyc = _nt(dx1b, wo_ref[...])
        dwo_ref[...] += _tn(yc_ref[...], dx1b)
        for h in range(GLA_HEADS):
            cols = pl.ds(h * GLA_DV, GLA_DV)
            dya = dyc[:, h * GLA_DV : (h + 1) * GLA_DV]
            oh = of_ref[:, cols] + ob_ref[:, cols]
            rn = lax.rsqrt(jnp.mean(oh * oh, axis=-1, keepdims=True) + EPS)
            on = oh * rn
            gh = g_ref[:, cols]
            sg = _sigmoid(gh)
            sil = gh * sg
            gnh = gn_ref[:, cols]
            dgn_ref[:, cols] += jnp.sum(dya * on * sil, axis=0, keepdims=True)
            dg_ref[:, cols] = (dya * on * gnh * (sg * (1.0 + gh * (1.0 - sg)))).astype(BF16)
            do_ref[:, cols] = _rms_bwd(dya * gnh * sil, on, rn)
        vv = vv_ref[...]
        zv, zv_grad = _gelu_and_grad(vv)
        xc = zv - jnp.mean(zv, axis=-1, keepdims=True)
        rstd = lax.rsqrt(jnp.mean(xc * xc, axis=-1, keepdims=True) + EPS)
        vhat = xc * rstd
        vn_sc[...] = (vhat * lng_ref[...] + lnb_ref[...]).astype(BF16)
        for c in range(tm // GMLP_CHUNK):
            rows = pl.ds(c * GMLP_CHUNK, GMLP_CHUNK)
            for g in range(GMLP_GROUPS):
                cols = pl.ds(g * LANES, LANES)
                vn = vn_sc[rows, cols]
                s = _nn(ws_ref[g], vn) + bs_ref[g]
                dyb = dyc[c * GMLP_CHUNK : (c + 1) * GMLP_CHUNK, GLA_W + g * LANES : GLA_W + (g + 1) * LANES]
                zu, zu_grad = _gelu_and_grad(u_ref[rows, cols])
                du_ref[rows, cols] = (dyb * s * zu_grad).astype(BF16)
                ds = dyb * zu
                dbs_acc[g] += ds
                dsb = ds.astype(BF16)
                dws_ref[g] += _nt(dsb, vn)
                dvn_sc[rows, cols] = _nn(wst_ref[g], dsb)
        dvn = dvn_sc[...]
        dlng_ref[...] += jnp.sum(dvn * vhat, axis=0, keepdims=True)
        dlnb_ref[...] += jnp.sum(dvn, axis=0, keepdims=True)
        dvh = dvn * lng_ref[...]
        dzv = rstd * (dvh - jnp.mean(dvh, axis=-1, keepdims=True) - vhat * jnp.mean(dvh * vhat, axis=-1, keepdims=True))
        dvv_ref[...] = (dzv * zv_grad).astype(BF16)

        @pl.when(step == nsteps - 1)
        def _():
            dbs_ref[...] = jnp.sum(dbs_acc[...], axis=-1, keepdims=True)

    row = lambda w: pl.BlockSpec((tm, w), lambda i: (i, 0))
    pcol = lambda col: pl.BlockSpec((tm, GLA_W), lambda i: (i, col // GLA_W))
    const = lambda shape: pl.BlockSpec(shape, lambda i: (0,) * len(shape))
    return pl.pallas_call(
        body,
        name="mixer_bwd",
        grid=(nsteps,),
        in_specs=[
            row(D_MODEL), row(D_MODEL), row(GLA_W), row(GLA_W), pcol(COL_G), pcol(COL_U), pcol(COL_VV),
            _resident((1, GLA_W)), _resident((1, GMLP_W)), _resident((1, GMLP_W)),
            _resident((GMLP_GROUPS, GMLP_CHUNK, GMLP_CHUNK)), _resident((GMLP_GROUPS, GMLP_CHUNK, GMLP_CHUNK)),
            _resident((GMLP_GROUPS, GMLP_CHUNK, 1)), _resident((D_MODEL, D_MODEL)), _resident(TOKEN_SHAPE),
        ],
        out_specs=[
            row(GLA_W), row(GLA_W), row(GMLP_W), row(GMLP_W), const((D_MODEL, D_MODEL)),
            const((1, GLA_W)), const((1, GMLP_W)), const((1, GMLP_W)),
            const((GMLP_GROUPS, GMLP_CHUNK, GMLP_CHUNK)), const((GMLP_GROUPS, GMLP_CHUNK, 1)),
        ],
        out_shape=[
            jax.ShapeDtypeStruct((seq, GLA_W), F32), jax.ShapeDtypeStruct((seq, GLA_W), BF16),
            jax.ShapeDtypeStruct((seq, GMLP_W), BF16), jax.ShapeDtypeStruct((seq, GMLP_W), BF16),
            jax.ShapeDtypeStruct((D_MODEL, D_MODEL), F32),
            jax.ShapeDtypeStruct((1, GLA_W), F32), jax.ShapeDtypeStruct((1, GMLP_W), F32), jax.ShapeDtypeStruct((1, GMLP_W), F32),
            jax.ShapeDtypeStruct((GMLP_GROUPS, GMLP_CHUNK, GMLP_CHUNK), F32), jax.ShapeDtypeStruct((GMLP_GROUPS, GMLP_CHUNK, 1), F32),
        ],
        scratch_shapes=[pltpu.VMEM((tm, GMLP_W), BF16), pltpu.VMEM((tm, GMLP_W), F32), pltpu.VMEM((GMLP_GROUPS, GMLP_CHUNK, GMLP_CHUNK), F32)],
        compiler_params=_params(56),
    )(dx1, ycat, o_f, o_b, p, p, p, gn, lng, lnb, ws_bf, wst_bf, bs_col, w_out, token)


def _gla_bwd(p, do, st, wd_pad, bd, token, reverse, other=None):
    seq = p.shape[0]
    tg = _gla_tile(seq)
    nt = seq // tg
    n = tg // GLA_CHUNK
    scale = GLA_DK**-0.5

    def tile(i):
        return i if reverse else nt - 1 - i

    def body(q_ref, k_ref, v_ref, lr_ref, do_ref, st_ref, wd_ref, bd_ref, token_ref, *rest):
        others, (dq_ref, dk_ref, dv_ref, dlr_ref, dwd_ref, dbd_ref, carry) = rest[:-7], rest[-7:]
        if others:
            odq_ref, odk_ref, odv_ref, odlr_ref = others

            def put(ref, idx, val, oref):
                ref[idx] = (val + oref[idx]).astype(BF16)
        else:
            odq_ref = odk_ref = odv_ref = odlr_ref = None

            def put(ref, idx, val, oref):
                ref[idx] = val

        @pl.when(pl.program_id(0) == 0)
        def _():
            carry[...] = jnp.zeros_like(carry)
            dwd_ref[...] = jnp.zeros_like(dwd_ref)
            dbd_ref[...] = jnp.zeros_like(dbd_ref)

        lr_bf = lr_ref[...].astype(BF16)
        carries = [carry[h] for h in range(GLA_HEADS)]
        row_in_chunk = lax.broadcasted_iota(jnp.int32, (tg, LANES), 0) % GLA_CHUNK
        lane_head = lax.broadcasted_iota(jnp.int32, (1, LANES), 1) // GLA_DK
        tt = lax.broadcasted_iota(jnp.int32, (GLA_CHUNK, GLA_CHUNK), 0)
        ss = lax.broadcasted_iota(jnp.int32, (GLA_CHUNK, GLA_CHUNK), 1)
        causal = (tt <= ss) if reverse else (tt >= ss)
        order = range(n) if reverse else range(n - 1, -1, -1)
        dlr = jnp.zeros((tg, LANES), F32)
        heads = range(GLA_HEADS)
        pv, masks, qdh, vhs, dohs, stbs = {}, {}, {}, {}, {}, {}
        sc_raw, dp, acc = {}, {}, {}
        for pair in range(2):
            cols = pl.ds(pair * LANES, LANES)
            pre, b3, blast = _gla_decay_terms(lr_bf, wd_ref, bd_ref, pair, row_in_chunk, reverse, n)
            q3 = q_ref[:, cols].reshape(n, GLA_CHUNK, LANES) * scale
            k3 = k_ref[:, cols].reshape(n, GLA_CHUNK, LANES)
            eb = jnp.exp(b3)
            emb = jnp.exp(-b3)
            ekte = jnp.exp(blast - b3)
            kdf = k3 * emb
            kte = k3 * ekte
            both = pl.ds(2 * pair * GLA_DV, 2 * GLA_DV)
            pv[pair] = dict(pre=pre, eb=eb, emb=emb, ekte=ekte, qd=q3 * eb, kdf=kdf, kd=kdf.astype(BF16), kte=kte, kte_bf=kte.astype(BF16),
                            dec=jnp.exp(blast), v=v_ref[:, both].reshape(n, GLA_CHUNK, 2 * GLA_DV).astype(BF16),
                            do=do_ref[:, both].reshape(n, GLA_CHUNK, 2 * GLA_DV).astype(BF16))
            for hh in range(2):
                h = 2 * pair + hh
                masks[h] = (lane_head == hh).astype(F32)
                qdh[h] = (pv[pair]["qd"] * masks[h]).astype(BF16)
                vhs[h] = pv[pair]["v"][:, :, hh * GLA_DV : (hh + 1) * GLA_DV]
                dohs[h] = pv[pair]["do"][:, :, hh * GLA_DV : (hh + 1) * GLA_DV]
                stbs[h] = st_ref[:, h]
                dp[h] = _bnt(dohs[h], vhs[h])
                acc[h] = _btn(dohs[h], qdh[h])
            sc_both = _bnt(jnp.concatenate([qdh[2 * pair], qdh[2 * pair + 1]], axis=1), pv[pair]["kd"])
            for hh in range(2):
                sc_raw[2 * pair + hh] = sc_both[:, hh * GLA_CHUNK : (hh + 1) * GLA_CHUNK, :]
        dsa, sc = {}, {}
        for h in heads:
            sc[h] = jnp.where(causal, sc_raw[h], 0.0).astype(BF16)
            dp[h] = jnp.where(causal, dp[h], 0.0).astype(BF16)
            dec = pv[h // 2]["dec"]
            c, after = carries[h], [None] * n
            for j in order:
                after[j] = c
                c = acc[h][j] + dec[j] * c
            carries[h] = c
            dsa[h] = jnp.stack(after)
        dvs, dqs, dks, dwds, dbds = [], [], [], [], []
        for pair in range(2):
            cols = pl.ds(pair * LANES, LANES)
            v = pv[pair]
            h0, h1 = 2 * pair, 2 * pair + 1
            dsa_both = jnp.concatenate([dsa[h0], dsa[h1]], axis=1)
            dsa_bf = dsa_both.astype(BF16)
            stb_bf = jnp.concatenate([stbs[h0], stbs[h1]], axis=1)
            dq_intra = _bnn(jnp.concatenate([dp[h0], dp[h1]], axis=1), v["kd"])
            dqd = (dq_intra[:, :GLA_CHUNK, :] * masks[h0] + dq_intra[:, GLA_CHUNK:, :] * masks[h1]) + _bnn(v["do"], stb_bf)
            dkd = _btn(dp[h0], qdh[h0]) + _btn(dp[h1], qdh[h1])
            dkte = _bnn(v["v"], dsa_bf)
            ddec = jnp.sum(dsa[h0] * stbs[h0].astype(F32) + dsa[h1] * stbs[h1].astype(F32), axis=1, keepdims=True)
            dv_inter = _bnt(v["kte_bf"], dsa_bf)
            for hh, h in ((0, h0), (1, h1)):
                dvs.append((_btn(sc[h], dohs[h]) + dv_inter[:, :, hh * GLA_DV : (hh + 1) * GLA_DV]).reshape(tg, GLA_DV))
            dqs.append((dqd * (scale * v["eb"])).reshape(tg, LANES))
            dks.append((dkd * v["emb"] + dkte * v["ekte"]).reshape(tg, LANES))
            db = dqd * v["qd"] - dkd * v["kdf"] - dkte * v["kte"]
            dblast = jnp.sum(dkte * v["kte"], axis=1, keepdims=True) + ddec * v["dec"]
            dla = _chunk_cumsum(db.reshape(tg, LANES), row_in_chunk, not reverse) + jnp.broadcast_to(dblast, (n, GLA_CHUNK, LANES)).reshape(tg, LANES)
            dpre = (dla * (1.0 / GLA_TAU) * _sigmoid(-v["pre"]))
            dpre_bf = dpre.astype(BF16)
            dlr = dlr + _nt(dpre_bf, wd_ref[:, cols])
            dwds.append(_tn(lr_bf, dpre_bf))
            dbds.append(jnp.sum(dpre, axis=0, keepdims=True))
        put(dlr_ref, (slice(None), slice(None)), dlr, odlr_ref)
        for pair in range(2):
            cols = pl.ds(pair * LANES, LANES)
            put(dq_ref, (slice(None), cols), dqs[pair], odq_ref)
            put(dk_ref, (slice(None), cols), dks[pair], odk_ref)
            dwd_ref[:, cols] += dwds[pair]
            dbd_ref[:, cols] += dbds[pair]
        for h in range(GLA_HEADS):
            put(dv_ref, (slice(None), pl.ds(h * GLA_DV, GLA_DV)), dvs[h], odv_ref)
            carry[h] = carries[h]

    pieces = [
        pl.BlockSpec((tg, KEY_W), lambda i: (tile(i), 0)),
        pl.BlockSpec((tg, KEY_W), lambda i: (tile(i), 0)),
        pl.BlockSpec((tg, GLA_W), lambda i: (tile(i), 0)),
        pl.BlockSpec((tg, LANES), lambda i: (tile(i), 0)),
    ]
    piece_dtype = BF16 if other else F32
    return pl.pallas_call(
        body,
        name="gla_bwd_rev" if reverse else "gla_bwd",
        grid=(nt,),
        in_specs=[
            pl.BlockSpec((tg, KEY_W), lambda i: (tile(i), COL_Q // KEY_W)),
            pl.BlockSpec((tg, KEY_W), lambda i: (tile(i), COL_K // KEY_W)),
            pl.BlockSpec((tg, GLA_W), lambda i: (tile(i), COL_V // GLA_W)),
            pl.BlockSpec((tg, LANES), lambda i: (tile(i), COL_LR // LANES)),
            pl.BlockSpec((tg, GLA_W), lambda i: (tile(i), 0)),
            pl.BlockSpec((n, GLA_HEADS, GLA_DV, LANES), lambda i: (tile(i), 0, 0, 0)),
            _resident((LANES, KEY_W)),
            _resident((1, KEY_W)),
            _resident(TOKEN_SHAPE),
        ] + (pieces if other else []),
        out_specs=pieces + [pl.BlockSpec((LANES, KEY_W), lambda i: (0, 0)), pl.BlockSpec((1, KEY_W), lambda i: (0, 0))],
        out_shape=[
            jax.ShapeDtypeStruct((seq, KEY_W), piece_dtype), jax.ShapeDtypeStruct((seq, KEY_W), piece_dtype),
            jax.ShapeDtypeStruct((seq, GLA_W), piece_dtype), jax.ShapeDtypeStruct((seq, LANES), piece_dtype),
            jax.ShapeDtypeStruct((LANES, KEY_W), F32), jax.ShapeDtypeStruct((1, KEY_W), F32),
        ],
        scratch_shapes=[pltpu.VMEM((GLA_HEADS, GLA_DV, LANES), F32)],
        compiler_params=_params(56),
    )(p, p, p, p, do, st, wd_pad, bd, token, *(other or ()))


def _inproj_wgrad(x, g1, dq, dk, dv, dg, du, dvv, dlr):
    seq = x.shape[0]
    tm = min(seq, 512)

    def body(x_ref, g1_ref, dq_ref, dk_ref, dv_ref, dg_ref, du_ref, dvv_ref, dlr_ref, dw_ref, dp_ref):
        @pl.when(pl.program_id(0) == 0)
        def _():
            dw_ref[...] = jnp.zeros_like(dw_ref)

        for col, ref in ((COL_Q, dq_ref), (COL_K, dk_ref), (COL_V, dv_ref), (COL_G, dg_ref), (COL_U, du_ref), (COL_VV, dvv_ref), (COL_LR, dlr_ref)):
            dp_ref[:, col : col + ref.shape[1]] = ref[...]
        xv = x_ref[...]
        h = (xv * lax.rsqrt(jnp.mean(xv * xv, axis=-1, keepdims=True) + EPS) * g1_ref[...]).astype(BF16)
        dw_ref[0:ROW_LR, :] += _tn(dp_ref[:, 0:COL_U], h)
        dw_ref[ROW_UV:PROJ_W, :] += _tn(dp_ref[:, COL_U:COL_LR], h)
        dw_ref[ROW_LR:ROW_UV, :] += _tn(dp_ref[:, COL_LR:PROJ_WP], h)[0 : ROW_UV - ROW_LR]

    row = lambda w: pl.BlockSpec((tm, w), lambda i: (i, 0))
    return pl.pallas_call(
        body,
        name="inproj_wgrad",
        grid=(seq // tm,),
        in_specs=[row(D_MODEL), _resident((1, D_MODEL)), row(KEY_W), row(KEY_W), row(GLA_W), row(GLA_W), row(GMLP_W), row(GMLP_W), row(LANES)],
        out_specs=[pl.BlockSpec((PROJ_W, D_MODEL), lambda i: (0, 0)), row(PROJ_WP)],
        out_shape=[jax.ShapeDtypeStruct((PROJ_W, D_MODEL), F32), jax.ShapeDtypeStruct((seq, PROJ_WP), BF16)],
        compiler_params=_params(56),
    )(x, g1, dq, dk, dv, dg, du, dvv, dlr)


def _inproj_dx(x, dx1, g1, w_in_t, dp, token):
    seq = x.shape[0]
    tm = min(seq, 512)

    def body(x_ref, dx1_ref, g1_ref, w_ref, dp_ref, token_ref, dx_ref, dg1_ref):
        @pl.when(pl.program_id(0) == 0)
        def _():
            dg1_ref[...] = jnp.zeros_like(dg1_ref)

        xv = x_ref[...]
        r1 = lax.rsqrt(jnp.mean(xv * xv, axis=-1, keepdims=True) + EPS)
        xh = xv * r1
        dh = (_nn(dp_ref[:, 0:COL_U], w_ref[0:ROW_LR, :]) + _nn(dp_ref[:, COL_U:COL_LR], w_ref[ROW_UV:PROJ_W, :])
              + _nn(dp_ref[:, COL_LR:PROJ_WP], w_ref[ROW_LR : ROW_LR + LANES, :]))
        dg1_ref[...] += jnp.sum(dh * xh, axis=0, keepdims=True)
        dx_ref[...] = dx1_ref[...] + _rms_bwd(dh * g1_ref[...], xh, r1)

    row = lambda w: pl.BlockSpec((tm, w), lambda i: (i, 0))
    return pl.pallas_call(
        body,
        name="inproj_dx",
        grid=(seq // tm,),
        in_specs=[row(D_MODEL), row(D_MODEL), _resident((1, D_MODEL)), _resident((PROJ_W, D_MODEL)), row(PROJ_WP), _resident(TOKEN_SHAPE)],
        out_specs=[row(D_MODEL), pl.BlockSpec((1, D_MODEL), lambda i: (0, 0))],
        out_shape=[jax.ShapeDtypeStruct((seq, D_MODEL), F32), jax.ShapeDtypeStruct((1, D_MODEL), F32)],
        compiler_params=_params(48),
    )(x, dx1, g1, w_in_t, dp, token)


def _in_hbm(a):
    return pltpu.with_memory_space_constraint(a, pltpu.HBM)


def _row_tile(rows, multiple=8):
    for t in range(min(rows, 512), 0, -1):
        if rows % t == 0 and t % multiple == 0:
            return t
    return rows


def _cast_into_slot(w, shard, token):
    rows, cols = w.shape
    tr = _row_tile(rows, 16)

    def body(s_ref, w_ref, token_ref, o_ref):
        o_ref[...] = w_ref[...].astype(BF16)

    return pl.pallas_call(
        body,
        name="cast_into_slot",
        grid_spec=pltpu.PrefetchScalarGridSpec(
            num_scalar_prefetch=1,
            grid=(rows // tr,),
            in_specs=[pl.BlockSpec((tr, cols), lambda i, s_ref: (i, 0)), pl.BlockSpec(TOKEN_SHAPE, lambda i, s_ref: (0, 0))],
            out_specs=pl.BlockSpec((None, tr, cols), lambda i, s_ref: (s_ref[0], i, 0)),
        ),
        out_shape=pltpu.HBM((N_SHARDS, rows, cols), BF16),
        compiler_params=_params(32, ("parallel",)),
    )(shard, _in_hbm(w), token)


def _add_halves(grads4, recvs, shard_core):
    n = len(grads4)
    _, rows, _ = grads4[0].shape
    tr = _row_tile(rows, 16)

    def body(sc_ref, *refs):
        for k in range(n):
            total = refs[k][...] + refs[n + k][...]
            refs[3 * n + k][...] = total.astype(BF16)

            @pl.when(pl.program_id(1) == sc_ref[0])
            def _(k=k, total=total):
                refs[2 * n + k][...] = total

    theirs = pl.BlockSpec((None, tr, HALF), lambda i, s, sc_ref: (s, i, 0))
    mine = pl.BlockSpec((None, tr, HALF), lambda i, s, sc_ref: (s, i, sc_ref[1]))
    kept = pl.BlockSpec((tr, HALF), lambda i, s, sc_ref: (i, 0))
    outs = pl.pallas_call(
        body,
        name="add_halves",
        grid_spec=pltpu.PrefetchScalarGridSpec(
            num_scalar_prefetch=1,
            grid=(rows // tr, N_SHARDS),
            in_specs=[mine] * n + [theirs] * n,
            out_specs=[kept] * n + [theirs] * n,
        ),
        out_shape=[pltpu.HBM((rows, HALF), F32)] * n + [pltpu.HBM((N_SHARDS, rows, HALF), BF16)] * n,
        compiler_params=_params(48, ("parallel", "arbitrary")),
    )(shard_core, *[_in_hbm(a) for a in list(grads4) + list(recvs)])
    return list(zip(outs[:n], outs[n:]))


def _add_partials(part, recv3, shard_core):
    rows, _ = part.shape
    tr = _row_tile(rows, 16)

    def body(sc_ref, p_ref, r_ref, o_ref):
        o_ref[...] = ((p_ref[...] + r_ref[0].astype(F32)) + r_ref[1].astype(F32)) + r_ref[2].astype(F32)

    return pl.pallas_call(
        body,
        name="add_partials",
        grid_spec=pltpu.PrefetchScalarGridSpec(
            num_scalar_prefetch=1,
            grid=(rows // tr,),
            in_specs=[
                pl.BlockSpec((tr, HALF), lambda i, sc_ref: (i, 0)),
                pl.BlockSpec((3, tr, HALF), lambda i, sc_ref: (0, i, 0)),
            ],
            out_specs=pl.BlockSpec((tr, HALF), lambda i, sc_ref: (i, sc_ref[1])),
        ),
        out_shape=pltpu.HBM((rows, 2 * HALF), F32),
        compiler_params=_params(32, ("parallel",)),
    )(shard_core, _in_hbm(part), _in_hbm(recv3))


def _adam_math(w, g, m, v):
    m = ADAM_B1 * m + (1.0 - ADAM_B1) * g
    v = ADAM_B2 * v + (1.0 - ADAM_B2) * (g * g)
    m_hat = m / (1.0 - ADAM_B1**ADAM_STEP)
    v_hat = v / (1.0 - ADAM_B2**ADAM_STEP)
    delta = -ADAM_LR * (m_hat / (jnp.sqrt(v_hat) + ADAM_EPS) + ADAM_WD * w)
    return delta, m, v


def _adamw(w, g, m, v):
    rows, cols = w.shape
    tr = _row_tile(rows)

    def body(w_ref, g_ref, m_ref, v_ref, go_ref, d_ref, mo_ref, vo_ref):
        gv = g_ref[...]
        go_ref[...] = gv
        d_ref[...], mo_ref[...], vo_ref[...] = _adam_math(w_ref[...], gv, m_ref[...], v_ref[...])

    spec = pl.BlockSpec((tr, cols), lambda i: (i, 0))
    return pl.pallas_call(
        body, name="adamw", grid=(rows // tr,), in_specs=[spec] * 4, out_specs=[spec] * 4, out_shape=[pltpu.HBM(w.shape, F32)] * 4,
        compiler_params=_params(32, ("parallel",)),
    )(_in_hbm(w), _in_hbm(g), _in_hbm(m), _in_hbm(v))


SMALL_ROWS = 560
DECAY_ROWS = 8
SMALL_TOTAL = SMALL_ROWS + 2 * N_SHARDS * DECAY_ROWS


def _adamw_small(gathered, own, wp, mp, vp, like):
    out_rows = SMALL_ROWS + 2 * DECAY_ROWS
    places, off = [], 0
    for a in like:
        rows = a.size // LANES
        kept = a.shape[-1] == LANES
        places.append((off, rows, kept, (rows, LANES) if kept else (1, a.size)))
        off += rows
    loss_row = off
    decay_shape = (LOWRANK, KEY_W // N_SHARDS)
    n = len(places) + 2

    def body(ga_ref, own_ref, w_ref, m_ref, v_ref, *refs):
        outs, loss_ref, packed = refs[: 4 * n], refs[4 * n], refs[4 * n + 1 :]
        g_sc = packed[0]
        x, y, c = _position()
        shard, me = 2 * x + y, 4 * x + 2 * y + c
        total = lambda rows: functools.reduce(lambda a, b: a + b, [jnp.where(me == d, own_ref[rows, :], ga_ref[d, rows, :]) for d in range(8)])
        g_sc[pl.ds(0, SMALL_ROWS), :] = total(pl.ds(0, SMALL_ROWS))
        for k in range(2):
            start = pl.multiple_of(SMALL_ROWS + k * N_SHARDS * DECAY_ROWS + shard * DECAY_ROWS, DECAY_ROWS)
            g_sc[pl.ds(SMALL_ROWS + k * DECAY_ROWS, DECAY_ROWS), :] = total(pl.ds(start, DECAY_ROWS))
        packed[1][...], packed[2][...], packed[3][...] = _adam_math(w_ref[...], g_sc[...], m_ref[...], v_ref[...])
        loss_ref[...] = g_sc[loss_row : loss_row + 1, :]
        for t, res in enumerate(packed):
            for (at, rows, kept, _), out in zip(places, outs[t * n :]):
                if kept:
                    out[...] = res[at : at + rows, :]
                else:
                    for r in range(rows):
                        out[:, r * LANES : (r + 1) * LANES] = res[at + r : at + r + 1, :]
            for k in range(2):
                out = outs[t * n + len(places) + k]
                both = res[SMALL_ROWS + k * DECAY_ROWS : SMALL_ROWS + (k + 1) * DECAY_ROWS, :]
                halves = (both, pltpu.roll(both, LANES // 2, axis=1))
                for r in range(DECAY_ROWS):
                    for h in range(2):
                        out[2 * r + h : 2 * r + h + 1, :] = halves[h][r : r + 1, 0 : LANES // 2]

    shapes = [jax.ShapeDtypeStruct(s, F32) for *_, s in places] + [jax.ShapeDtypeStruct(decay_shape, F32)] * 2
    out = pl.pallas_call(
        body,
        name="adamw_small",
        out_shape=shapes * 4 + [jax.ShapeDtypeStruct((1, LANES), F32)],
        scratch_shapes=[pltpu.VMEM((out_rows, LANES), F32)] * 4,
        compiler_params=_params(32, None),
    )(gathered, own, wp, mp, vp)
    return [list(out[t * n : (t + 1) * n]) for t in range(4)], out[4 * n]


ANY = pl.BlockSpec(memory_space=pl.ANY)


def _position():
    return lax.axis_index("x"), lax.axis_index("y"), lax.axis_index("c")


def _other_chips(x, y):
    return [(1 - x, y), (x, 1 - y), (1 - x, 1 - y)]


HBM = pl.BlockSpec(memory_space=pltpu.HBM)
SEM = pl.BlockSpec(memory_space=pltpu.SEMAPHORE)
TOKEN = jax.ShapeDtypeStruct(TOKEN_SHAPE, F32)
DATAFLOW = pltpu.SideEffectType.DATAFLOW_SIDE_EFFECTING


def _half_block(ref4, slot, core):
    return ref4.at[slot, :, pl.ds(pl.multiple_of(core * HALF, HALF), HALF)]


def _gather_ici_copies(bufs, lands, send_sems, recv_sems):
    x, y, c = _position()
    pairs = []
    for k, ref4 in enumerate(bufs):
        mine = _half_block(ref4, 2 * x + y, c)
        for j, (px, py) in enumerate(_other_chips(x, y)):
            sems = dict(send_sem=send_sems.at[3 * k + j], recv_sem=recv_sems.at[3 * k + j], device_id=(px, py, c), device_id_type=MESH)
            pairs.append((functools.partial(pltpu.make_async_remote_copy, src_ref=mine, dst_ref=mine, **sems),
                          functools.partial(pltpu.make_async_remote_copy, src_ref=mine, dst_ref=_half_block(ref4, 2 * px + py, c), **sems)))
    return pairs


def _gather_d2d_copies(bufs, lands, send_sems, recv_sems):
    x, y, c = _position()
    pairs = []
    for k, ref4 in enumerate(bufs):
        for j, (px, py) in enumerate(_other_chips(x, y)):
            have = _half_block(ref4, 2 * px + py, c)
            sems = dict(send_sem=send_sems.at[3 * k + j], recv_sem=recv_sems.at[3 * k + j], device_id=(x, y, 1 - c), device_id_type=MESH)
            pairs.append((functools.partial(pltpu.make_async_remote_copy, src_ref=have, dst_ref=have, **sems),
                          functools.partial(pltpu.make_async_remote_copy, src_ref=have, dst_ref=_half_block(ref4, 2 * px + py, 1 - c), **sems)))
    return pairs


def _gather_forward(bufs):
    n = len(bufs)

    def body(*refs):
        outs = refs[n : 2 * n]
        send_sems, recv_sems = refs[2 * n :]
        d2d = _gather_d2d_copies(outs, (), send_sems, recv_sems)
        for forward, _ in d2d:
            forward().start()
        for forward, arrival in d2d:
            arrival().wait_recv()
            forward().wait_send()

    return pl.pallas_call(
        body,
        name="gather_forward",
        in_specs=[ANY] * n,
        out_specs=[ANY] * n,
        out_shape=[jax.ShapeDtypeStruct(b.shape, b.dtype) for b in bufs],
        input_output_aliases={k: k for k in range(n)},
        scratch_shapes=[pltpu.SemaphoreType.DMA((3 * n,)), pltpu.SemaphoreType.DMA((3 * n,))],
        compiler_params=pltpu.CompilerParams(has_side_effects=True),
    )(*bufs)


def _exchange_halves(grads4):
    n = len(grads4)

    def body(*refs):
        ins, outs = refs[:n], refs[n : 2 * n]
        send_sems, recv_sems = refs[2 * n :]
        x, y, c = _position()
        copies = []
        for k in range(n):
            cp = pltpu.make_async_remote_copy(
                src_ref=ins[k].at[:, :, pl.ds(pl.multiple_of((1 - c) * HALF, HALF), HALF)], dst_ref=outs[k],
                send_sem=send_sems.at[k], recv_sem=recv_sems.at[k], device_id=(x, y, 1 - c), device_id_type=MESH)
            cp.start()
            copies.append(cp)
        for cp in copies:
            cp.wait()

    return pl.pallas_call(
        body,
        name="exchange_halves",
        in_specs=[ANY] * n,
        out_specs=[ANY] * n,
        out_shape=[jax.ShapeDtypeStruct((N_SHARDS, g.shape[1], HALF), g.dtype) for g in grads4],
        scratch_shapes=[pltpu.SemaphoreType.DMA((n,)), pltpu.SemaphoreType.DMA((n,))],
        compiler_params=pltpu.CompilerParams(has_side_effects=True),
    )(*grads4)


def _both_ends(**copy):
    maker = functools.partial(pltpu.make_async_remote_copy, **copy)
    return maker, maker


def _scatter_copies(parts, lands, send_sems, recv_sems):
    x, y, c = _position()
    return [_both_ends(src_ref=parts[k].at[2 * px + py], dst_ref=lands[k].at[j], send_sem=send_sems.at[3 * k + j],
                       recv_sem=recv_sems.at[3 * k + j], device_id=(px, py, c), device_id_type=MESH)
            for k in range(len(parts)) for j, (px, py) in enumerate(_other_chips(x, y))]


def _exchange_copies(grads, lands, send_sems, recv_sems):
    x, y, c = _position()
    return [_both_ends(src_ref=grads[k].at[:, :, pl.ds(pl.multiple_of((1 - c) * HALF, HALF), HALF)], dst_ref=lands[k],
                       send_sem=send_sems.at[k], recv_sem=recv_sems.at[k], device_id=(x, y, 1 - c), device_id_type=MESH)
            for k in range(len(grads))]


def _exchange_lands(grads4):
    return [jax.ShapeDtypeStruct((N_SHARDS, g.shape[1], HALF), g.dtype) for g in grads4]


def _scatter_lands(parts4):
    return [jax.ShapeDtypeStruct((3,) + g.shape[1:], g.dtype) for g in parts4]


def _small_gather_copies(blocks, lands, send_sems, recv_sems):
    x, y, c = _position()
    flip = lambda v, bit: 1 - v if bit else v
    return [_both_ends(src_ref=blocks[0], dst_ref=lands[0].at[4 * x + 2 * y + c], send_sem=send_sems.at[r - 1],
                       recv_sem=recv_sems.at[r - 1], device_id=(flip(x, r & 4), flip(y, r & 2), flip(c, r & 1)), device_id_type=MESH)
            for r in range(1, 8)]


def _split_start(name, srcs, land_shapes, make_copies, nsem, after=()):
    n, nl, na = len(srcs), len(land_shapes), len(after)
    lands = [lax.empty(a.shape, a.dtype) for a in land_shapes]

    def body(*refs):
        send_sems, recv_sems = refs[n + nl + na], refs[n + nl + na + 1]
        token = refs[2 * (n + nl) + na + 2]
        for send, _ in make_copies(refs[:n], refs[n : n + nl], send_sems, recv_sems):
            send().start()
        token[...] = jnp.zeros_like(token)

    hbm = lambda a: pltpu.HBM(a.shape, a.dtype)
    out = pl.pallas_call(
        body,
        name=name,
        in_specs=[HBM] * (n + nl) + [ANY] * na,
        out_specs=(SEM, SEM, *[HBM] * (n + nl), pl.BlockSpec(memory_space=pltpu.VMEM)),
        out_shape=(pltpu.SemaphoreType.DMA((nsem,)), pltpu.SemaphoreType.DMA((nsem,)), *[hbm(a) for a in list(srcs) + lands], TOKEN),
        input_output_aliases={k: 2 + k for k in range(n + nl)},
        compiler_params=pltpu.CompilerParams(has_side_effects=DATAFLOW),
    )(*[pltpu.with_memory_space_constraint(a, pltpu.HBM) for a in list(srcs) + lands], *after)
    return out[0], out[1], list(out[2 : 2 + n]), list(out[2 + n : 2 + n + nl]), out[2 + n + nl]


def _split_wait(name, send_sems, recv_sems, srcs, lands, make_copies, after):
    n, nl = len(srcs), len(lands)

    def body(*refs):
        for send, arrival in make_copies(refs[:n], refs[n : n + nl], refs[n + nl], refs[n + nl + 1]):
            send().wait_send()
            arrival().wait_recv()

    hbm = lambda a: pltpu.HBM(a.shape, a.dtype)
    out = pl.pallas_call(
        body,
        name=name,
        in_specs=[HBM] * (n + nl) + [SEM, SEM] + [ANY] * len(after),
        out_specs=tuple([HBM] * (n + nl)),
        out_shape=tuple(hbm(a) for a in list(srcs) + list(lands)),
        input_output_aliases={k: k for k in range(n + nl)},
        compiler_params=pltpu.CompilerParams(has_side_effects=DATAFLOW),
    )(*srcs, *lands, send_sems, recv_sems, *after)
    return list(out[:n]), list(out[n:])


def _join_copies(bufs, lands, send_sems, recv_sems):
    x, y, c = _position()
    half = lambda ref, core: ref.at[:, pl.ds(pl.multiple_of(core * HALF, HALF), HALF)]
    pairs = []
    for k, ref in enumerate(bufs):
        sems = dict(send_sem=send_sems.at[k], recv_sem=recv_sems.at[k], device_id=(x, y, 1 - c), device_id_type=MESH)
        pairs.append((functools.partial(pltpu.make_async_remote_copy, src_ref=half(ref, c), dst_ref=half(ref, c), **sems),
                      functools.partial(pltpu.make_async_remote_copy, src_ref=half(ref, c), dst_ref=half(ref, 1 - c), **sems)))
    return pairs


def _join_halves(bufs, after=()):
    n, na = len(bufs), len(after)

    def body(*refs):
        joins = _join_copies(refs[n + na : 2 * n + na], (), *refs[2 * n + na :])
        for send, _ in joins:
            send().start()
        for send, arrival in joins:
            send().wait_send()
            arrival().wait_recv()

    return pl.pallas_call(
        body,
        name="join_halves",
        in_specs=[ANY] * (n + na),
        out_specs=[ANY] * n,
        out_shape=[jax.ShapeDtypeStruct(b.shape, b.dtype) for b in bufs],
        input_output_aliases={k: k for k in range(n)},
        scratch_shapes=[pltpu.SemaphoreType.DMA((n,)), pltpu.SemaphoreType.DMA((n,))],
        compiler_params=pltpu.CompilerParams(has_side_effects=True),
    )(*bufs, *after)


def _allgather_small(block):
    m_per, ncol = block.shape

    def body(x_ref, out_ref, send_sems, recv_sems, local_sem):
        x, y, c = _position()
        me, sibling = (x, y, c), (x, y, 1 - c)
        chips = _other_chips(x, y)

        def rows(px, py, pc):
            return out_ref.at[4 * px + 2 * py + pc]

        def copy(k, blk, to, src=None):
            return pltpu.make_async_remote_copy(
                src_ref=rows(*blk) if src is None else src, dst_ref=rows(*blk),
                send_sem=send_sems.at[k], recv_sem=recv_sems.at[k], device_id=to, device_id_type=MESH)

        mine = pltpu.make_async_copy(x_ref, rows(*me), local_sem)
        mine.start()
        first = [copy(0, me, sibling, src=x_ref)] + [copy(1 + j, me, (*chip, c), src=x_ref) for j, chip in enumerate(chips)]
        for cp in first:
            cp.start()
        passed = [copy(4 + j, (*chip, c), sibling) for j, chip in enumerate(chips)]
        for j, chip in enumerate(chips):
            copy(1 + j, (*chip, c), me).wait_recv()
            passed[j].start()
        copy(0, sibling, me).wait_recv()
        for j, chip in enumerate(chips):
            copy(4 + j, (*chip, 1 - c), me).wait_recv()
        for cp in first + passed:
            cp.wait_send()
        mine.wait()

    return pl.pallas_call(
        body,
        name="allgather_small",
        in_specs=[pl.BlockSpec(memory_space=pltpu.VMEM)],
        out_specs=pl.BlockSpec(memory_space=pltpu.VMEM),
        out_shape=jax.ShapeDtypeStruct((8, m_per, ncol), block.dtype),
        scratch_shapes=[pltpu.SemaphoreType.DMA((7,)), pltpu.SemaphoreType.DMA((7,)), pltpu.SemaphoreType.DMA],
        compiler_params=pltpu.CompilerParams(has_side_effects=True, vmem_limit_bytes=32 * MIB),
    )(block)


SMALL_NAMES = ["norm1_g", "b_decay_f", "b_decay_b", "gla_norm_g", "gmlp_ln_g", "gmlp_ln_b", "w_spatial", "b_spatial", "norm2_g", "final_norm_g"]


def _pack_small(parts, decay_parts):
    flat = jnp.concatenate([a.reshape(-1) for a in parts])
    flat = jnp.pad(flat, (0, SMALL_ROWS * LANES - flat.shape[0])).reshape(SMALL_ROWS, LANES)
    return jnp.concatenate([flat] + [d.reshape(-1, LANES) for d in decay_parts], axis=0)


def kernel(x, norm1_g, w_in, w_decay_f, b_decay_f, w_decay_b, b_decay_b, gla_norm_g, gmlp_ln_g, gmlp_ln_b, w_spatial, b_spatial, w_out, norm2_g, w_gate, w_up, w_down, final_norm_g, loss_target, m_norm1_g, m_w_in, m_w_decay_f, m_b_decay_f, m_w_decay_b, m_b_decay_b, m_gla_norm_g, m_gmlp_ln_g, m_gmlp_ln_b, m_w_spatial, m_b_spatial, m_w_out, m_norm2_g, m_w_gate, m_w_up, m_w_down, m_final_norm_g, v_norm1_g, v_w_in, v_w_decay_f, v_b_decay_f, v_w_decay_b, v_b_decay_b, v_gla_norm_g, v_gmlp_ln_g, v_gmlp_ln_b, v_w_spatial, v_b_spatial, v_w_out, v_norm2_g, v_w_gate, v_w_up, v_w_down, v_final_norm_g):
    args = dict(locals())
    cx, cy, cc = lax.axis_index("x"), lax.axis_index("y"), lax.axis_index("c")
    shard = 2 * cx + cy
    xs = x[0]
    target = loss_target[0]

    big_names = ["w_in", "w_out", "w_gate", "w_up", "w_down"]
    transposed = ("w_in", "w_gate", "w_up")
    rows_of = lambda pre, k: jnp.transpose(args[pre + k][0]) if k in transposed else args[pre + k][0]
    big_shards = {k: rows_of("", k) for k in big_names}
    s_arr = shard.reshape(1).astype(jnp.int32)
    sc_arr = jnp.stack([shard, cc]).astype(jnp.int32)
    zero_token = jnp.zeros(TOKEN_SHAPE, F32)
    w_send, w_recv, (w_in4,), _, token_w_in = _split_start(
        "w_in_gather_start", [_cast_into_slot(big_shards["w_in"], s_arr, zero_token)], [], _gather_ici_copies, 3)
    late = ["w_out", "w_gate", "w_up", "w_down"]
    late_slots = [_cast_into_slot(big_shards[k], s_arr, token_w_in) for k in late]
    dec_block = jnp.concatenate([w_decay_f[0].reshape(-1, LANES), w_decay_b[0].reshape(-1, LANES)], axis=0)
    dec_all = _allgather_small(dec_block)
    (w_in4,), _ = _split_wait("w_in_gather_wait", w_send, w_recv, [w_in4], [], _gather_ici_copies, (dec_all, *late_slots))
    (w_in4,) = _gather_forward([w_in4])
    w_in_t = w_in4.reshape(PROJ_W, D_MODEL)
    g_send, g_recv, late_bufs, _, token_gather = _split_start(
        "gather_start", late_slots, [], _gather_ici_copies, 3 * len(late), after=(w_in4,))
    dec_all = dec_all[::2].reshape(N_SHARDS, 2, LOWRANK, KEY_W // N_SHARDS)
    wdf_full = jnp.transpose(dec_all[:, 0], (1, 0, 2)).reshape(LOWRANK, KEY_W)
    wdb_full = jnp.transpose(dec_all[:, 1], (1, 0, 2)).reshape(LOWRANK, KEY_W)
    wd_pad_f = jnp.zeros((LANES, KEY_W), F32).at[0:LOWRANK].set(wdf_full).astype(BF16)
    wd_pad_b = jnp.zeros((LANES, KEY_W), F32).at[LOWRANK : 2 * LOWRANK].set(wdb_full).astype(BF16)

    ws_bf = w_spatial[0].astype(BF16)
    wst_bf = jnp.transpose(w_spatial[0], (0, 2, 1)).astype(BF16)
    bs_col = b_spatial[0].reshape(GMLP_GROUPS, GMLP_CHUNK, 1)

    p = _inproj(xs, norm1_g, w_in_t, token_gather)
    o_f, st_f = _gla_fwd(p, wd_pad_f, b_decay_f, token_gather, reverse=False)
    o_b, st_b = _gla_fwd(p, wd_pad_b, b_decay_b, token_gather, reverse=True)
    late_bufs, _ = _split_wait("gather_wait", g_send, g_recv, late_bufs, [], _gather_ici_copies, (o_f, o_b))
    (w_out4,) = _gather_forward(late_bufs[:1])
    f_send, f_recv, ffn_bufs, _, token_forward = _split_start(
        "forward_start", late_bufs[1:], [], _gather_d2d_copies, 3 * (len(late) - 1), after=(w_out4,))
    w_out_full = w_out4.reshape(-1, D_MODEL)
    x1, ycat = _mixer_out(xs, o_f, o_b, p, gla_norm_g, gmlp_ln_g, gmlp_ln_b, ws_bf, bs_col, w_out_full, token_forward)
    ffn_bufs, _ = _split_wait("forward_wait", f_send, f_recv, ffn_bufs, [], _gather_d2d_copies, (x1,))
    wg_t, wu_t, wd = [b.reshape(-1, D_MODEL) for b in ffn_bufs]
    gf = final_norm_g.reshape(1, D_MODEL)
    h2, gate, up, act, dx2, loss_acc, dgf = _ffn_fwd(x1, target, norm2_g, gf, wg_t, wu_t, wd)

    dgate, dup, dx1, dg2 = _ffn_bwd(dx2, gate, up, x1, norm2_g, wg_t, wu_t, wd)
    ffn_grads4 = [g.reshape(N_SHARDS, FF_SHARD, D_MODEL) for g in _ffn_wgrad(h2, dgate, dup, act, dx2)]
    e_send, e_recv, e_srcs, e_lands, token_exchange = _split_start(
        "exchange_start", ffn_grads4, _exchange_lands(ffn_grads4), _exchange_copies, len(ffn_grads4))
    do, dg, du, dvv, dwo, dgn, dlng, dlnb, dws, dbs = _mixer_bwd(
        dx1, ycat, o_f, o_b, p, gla_norm_g, gmlp_ln_g, gmlp_ln_b, ws_bf, wst_bf, bs_col, w_out_full, token_exchange)
    ffn_mine, ffn_other = _split_wait("exchange_wait", e_send, e_recv, e_srcs, e_lands, _exchange_copies, (do,))
    ffn_parts = _add_halves(ffn_mine, ffn_other, sc_arr)
    ffn_payload = [pb for _, pb in ffn_parts]
    s_send, s_recv, s_parts, s_lands, token_scatter = _split_start(
        "scatter_start", ffn_payload, _scatter_lands(ffn_payload), _scatter_copies, 3 * len(ffn_payload))
    dq_f, dk_f, dv_f, dlr_f, dwdec_f, dbdec_f = _gla_bwd(p, do, st_f, wd_pad_f, b_decay_f, token_scatter, reverse=False)
    dq, dk, dv, dlr, dwdec_b, dbdec_b = _gla_bwd(
        p, do, st_b, wd_pad_b, b_decay_b, token_scatter, reverse=True, other=(dq_f, dk_f, dv_f, dlr_f))
    dwin_t, dp = _inproj_wgrad(xs, norm1_g, dq, dk, dv, dg, du, dvv, dlr)
    _, ffn_recv = _split_wait("scatter_wait", s_send, s_recv, s_parts, s_lands, _scatter_copies, (dwin_t,))
    ffn_bufs = [_add_partials(pf, r, sc_arr) for (pf, _), r in zip(ffn_parts, ffn_recv)]

    dwin4 = dwin_t.reshape(N_SHARDS, PROJ_W // N_SHARDS, D_MODEL)
    dwo4 = dwo.reshape(N_SHARDS, D_MODEL // N_SHARDS, D_MODEL)
    proj_grads4 = [dwin4, dwo4]
    proj_parts = [_add_halves([g], [r], sc_arr)[0] for g, r in zip(proj_grads4, _exchange_halves(proj_grads4))]
    proj_payload = [pb for _, pb in proj_parts]
    p_send, p_recv, p_parts, p_lands, token_proj = _split_start(
        "proj_scatter_start", proj_payload, _scatter_lands(proj_payload), _scatter_copies, 3 * len(proj_payload))
    j_send, j_recv, ffn_bufs, _, token_join = _split_start("join_start", ffn_bufs, [], _join_copies, len(ffn_bufs), after=(token_proj,))
    dx, dg1 = _inproj_dx(xs, dx1, norm1_g, w_in_t, dp, token_join)
    _, proj_recv = _split_wait("proj_scatter_wait", p_send, p_recv, p_parts, p_lands, _scatter_copies, (dx,))

    dwdec_f16 = dwdec_f[0:LOWRANK]
    dwdec_b16 = dwdec_b[LOWRANK : 2 * LOWRANK]
    shard_major = lambda a: jnp.transpose(a.reshape(LOWRANK, N_SHARDS, KEY_W // N_SHARDS), (1, 0, 2))
    small_grads = {
        "norm1_g": dg1, "b_decay_f": dbdec_f, "b_decay_b": dbdec_b, "gla_norm_g": dgn, "gmlp_ln_g": dlng, "gmlp_ln_b": dlnb,
        "w_spatial": dws, "b_spatial": dbs, "norm2_g": dg2, "final_norm_g": dgf,
    }
    g_pack = _pack_small([small_grads[k] for k in SMALL_NAMES] + [loss_acc], [shard_major(dwdec_f16), shard_major(dwdec_b16)])
    sg_send, sg_recv, (g_pack,), g_lands, token_small = _split_start(
        "small_gather_start", [g_pack], [jax.ShapeDtypeStruct((8, SMALL_TOTAL, LANES), F32)], _small_gather_copies, 7)

    ffn_bufs, _ = _split_wait("join_wait", j_send, j_recv, ffn_bufs, [], _join_copies, (dx, token_small))
    proj_bufs = [_add_partials(pf, r, sc_arr) for (pf, _), r in zip(proj_parts, proj_recv)]
    big_grads = dict(zip(big_names, list(_join_halves(proj_bufs, after=(token_small,))) + ffn_bufs))
    big_updates = {k: _adamw(big_shards[k], big_grads[k], rows_of("m_", k), rows_of("v_", k)) for k in big_names}

    (g_pack,), (g_all,) = _split_wait(
        "small_gather_wait", sg_send, sg_recv, [g_pack], g_lands, _small_gather_copies, tuple(u[1] for u in big_updates.values()))
    pack_own = lambda pre: _pack_small([args[pre + k] for k in SMALL_NAMES], [args[pre + "w_decay_f"], args[pre + "w_decay_b"]])
    small_updates, loss_row = _adamw_small(g_all, g_pack, pack_own(""), pack_own("m_"), pack_own("v_"), [args[k] for k in SMALL_NAMES])

    names = ["norm1_g", "w_in", "w_decay_f", "b_decay_f", "w_decay_b", "b_decay_b", "gla_norm_g", "gmlp_ln_g", "gmlp_ln_b",
             "w_spatial", "b_spatial", "w_out", "norm2_g", "w_gate", "w_up", "w_down", "final_norm_g"]
    results = {"g": {}, "d": {}, "m": {}, "v": {}}
    for tag, arrays in zip("gdmv", small_updates):
        for k, a in zip(SMALL_NAMES + ["w_decay_f", "w_decay_b"], arrays):
            results[tag][k] = a.reshape(args[k].shape)
    for k in big_names:
        for tag, a in zip("gdmv", big_updates[k]):
            results[tag][k] = (jnp.transpose(a) if k in transposed else a).reshape(args[k].shape)

    loss = loss_row[0, 0]
    grad_x = dx.reshape(x.shape)
    return (loss, grad_x, *[results["g"][k] for k in names], *[results["d"][k] for k in names],
            *[results["m"][k] for k in names], *[results["v"][k] for k in names])
```

```python
import functools
import math

import jax
import jax.numpy as jnp
from jax import lax
from jax.experimental import pallas as pl
from jax.experimental.pallas import tpu as pltpu

F32, BF16 = jnp.float32, jnp.bfloat16

D_MODEL = 1024
GLA_HEADS = 4
GLA_DK = 64
GLA_DV = 128
KEY_W = GLA_HEADS * GLA_DK
GLA_W = GLA_HEADS * GLA_DV
GMLP_W = 512
GMLP_GROUPS = 4
GMLP_CHUNK = 128
LOWRANK = 16
GLA_CHUNK = 64
GLA_TAU = 16.0
PROJ_W = 2592
PROJ_WP = 2688
D_FF = 2816
N_SHARDS = 4
FF_SHARD = D_FF // N_SHARDS
EPS = 1e-6
LANES = 128
TOKEN_SHAPE = (8, LANES)
MIB = 1024 * 1024

ADAM_LR = 0.001
ADAM_B1 = 0.9
ADAM_B2 = 0.999
ADAM_EPS = 1e-08
ADAM_WD = 0.01
ADAM_STEP = 10

COL_Q, COL_K = 0, 256
COL_V, COL_G, COL_U, COL_VV = 512, 1024, 1536, 2048
COL_LR = 2560
ROW_LR, ROW_UV = 1536, 1568
HALF = D_MODEL // 2

MESH = pl.DeviceIdType.MESH


def _nn(a, b):
    return jnp.dot(a, b, preferred_element_type=F32)


def _nt(a, b):
    return lax.dot_general(a, b, (((1,), (1,)), ((), ())), preferred_element_type=F32)


def _tn(a, b):
    return lax.dot_general(a, b, (((0,), (0,)), ((), ())), preferred_element_type=F32)


def _bnn(a, b):
    return jnp.einsum("nik,nkj->nij", a, b, preferred_element_type=F32)


def _bnt(a, b):
    return jnp.einsum("nik,njk->nij", a, b, preferred_element_type=F32)


def _btn(a, b):
    return jnp.einsum("nki,nkj->nij", a, b, preferred_element_type=F32)


def _resident(shape):
    zeros = (0,) * len(shape)
    return pl.BlockSpec(shape, lambda *_: zeros, pipeline_mode=pl.Buffered(1))


def _params(vmem_mib, semantics=("arbitrary",)):
    return pltpu.CompilerParams(vmem_limit_bytes=vmem_mib * MIB, dimension_semantics=semantics)


def _sigmoid(x):
    return 1.0 / (1.0 + jnp.exp(-x))


def _gelu(x):
    return 0.5 * x * (1.0 + lax.erf(x * (1.0 / math.sqrt(2.0))))


def _gelu_and_grad(x):
    cdf = 0.5 * (1.0 + lax.erf(x * (1.0 / math.sqrt(2.0))))
    return x * cdf, cdf + x * jnp.exp(-0.5 * x * x) * (1.0 / math.sqrt(2.0 * math.pi))


def _log_sigmoid(x):
    return jnp.minimum(x, 0.0) - jnp.log(1.0 + jnp.exp(-jnp.abs(x)))


def _rms_bwd(dxh, xh, r):
    return r * (dxh - xh * jnp.mean(dxh * xh, axis=-1, keepdims=True))


def _chunk_cumsum(v, row_in_chunk, reverse):
    rows = v.shape[0]
    for sh in (1, 2, 4, 8, 16, 32):
        if reverse:
            v = v + jnp.where(row_in_chunk + sh < GLA_CHUNK, pltpu.roll(v, rows - sh, axis=0), 0.0)
        else:
            v = v + jnp.where(row_in_chunk >= sh, pltpu.roll(v, sh, axis=0), 0.0)
    return v


def _inproj(x, g1, w_in_t, token):
    seq = x.shape[0]
    tm = min(seq, 512)

    def body(x_ref, g_ref, w_ref, token_ref, p_ref):
        xv = x_ref[...]
        r = lax.rsqrt(jnp.mean(xv * xv, axis=-1, keepdims=True) + EPS)
        h = (xv * r * g_ref[...]).astype(BF16)
        p_ref[:, 0:COL_U] = _nt(h, w_ref[0:ROW_LR, :])
        p_ref[:, COL_U:COL_LR] = _nt(h, w_ref[ROW_UV:PROJ_W, :])
        p_ref[:, COL_LR:PROJ_WP] = _nt(h, w_ref[ROW_LR : ROW_LR + LANES, :])

    return pl.pallas_call(
        body,
        name="inproj",
        grid=(seq // tm,),
        in_specs=[pl.BlockSpec((tm, D_MODEL), lambda i: (i, 0)), _resident((1, D_MODEL)), _resident((PROJ_W, D_MODEL)), _resident(TOKEN_SHAPE)],
        out_specs=pl.BlockSpec((tm, PROJ_WP), lambda i: (i, 0)),
        out_shape=jax.ShapeDtypeStruct((seq, PROJ_WP), F32),
        compiler_params=_params(48, ("parallel",)),
    )(x, g1, w_in_t, token)


def _gla_tile(seq):
    return min(seq, 1024)


def _gla_decay_terms(lr_bf, wd_ref, bd_ref, pair, row_in_chunk, reverse, n):
    cols = pl.ds(pair * LANES, LANES)
    pre = _nn(lr_bf, wd_ref[:, cols]) + bd_ref[:, cols]
    la = _log_sigmoid(pre) * (1.0 / GLA_TAU)
    b = _chunk_cumsum(la, row_in_chunk, reverse)
    b3 = b.reshape(n, GLA_CHUNK, LANES)
    blast = b3[:, 0:1, :] if reverse else b3[:, GLA_CHUNK - 1 : GLA_CHUNK, :]
    return pre, b3, blast


def _gla_fwd(p, wd_pad, bd, token, reverse):
    seq = p.shape[0]
    tg = _gla_tile(seq)
    nt = seq // tg
    n = tg // GLA_CHUNK
    scale = GLA_DK**-0.5

    def tile(i):
        return nt - 1 - i if reverse else i

    def body(q_ref, k_ref, v_ref, lr_ref, wd_ref, bd_ref, token_ref, o_ref, st_ref, carry):
        @pl.when(pl.program_id(0) == 0)
        def _():
            carry[...] = jnp.zeros_like(carry)

        lr_bf = lr_ref[...].astype(BF16)
        states = [carry[h] for h in range(GLA_HEADS)]
        row_in_chunk = lax.broadcasted_iota(jnp.int32, (tg, LANES), 0) % GLA_CHUNK
        lane_head = lax.broadcasted_iota(jnp.int32, (1, LANES), 1) // GLA_DK
        tt = lax.broadcasted_iota(jnp.int32, (GLA_CHUNK, GLA_CHUNK), 0)
        ss = lax.broadcasted_iota(jnp.int32, (GLA_CHUNK, GLA_CHUNK), 1)
        causal = (tt <= ss) if reverse else (tt >= ss)
        order = range(n - 1, -1, -1) if reverse else range(n)
        heads = range(GLA_HEADS)
        qds, vhs, decs, sc_raw, dst = {}, {}, {}, {}, {}
        for pair in range(2):
            cols = pl.ds(pair * LANES, LANES)
            _, b3, blast = _gla_decay_terms(lr_bf, wd_ref, bd_ref, pair, row_in_chunk, reverse, n)
            q3 = q_ref[:, cols].reshape(n, GLA_CHUNK, LANES) * scale
            k3 = k_ref[:, cols].reshape(n, GLA_CHUNK, LANES)
            qd = q3 * jnp.exp(b3)
            kd = (k3 * jnp.exp(-b3)).astype(BF16)
            kte = k3 * jnp.exp(blast - b3)
            decs[pair] = jnp.exp(blast)
            qds[pair] = qd.astype(BF16)
            m0 = (lane_head == 0).astype(F32)
            m1 = (lane_head == 1).astype(F32)
            q_both = jnp.concatenate([(qd * m0).astype(BF16), (qd * m1).astype(BF16)], axis=1)
            sc_both = _bnt(q_both, kd)
            for hh, m in ((0, m0), (1, m1)):
                h = 2 * pair + hh
                vhs[h] = v_ref[:, pl.ds(h * GLA_DV, GLA_DV)].reshape(n, GLA_CHUNK, GLA_DV).astype(BF16)
                sc_raw[h] = sc_both[:, hh * GLA_CHUNK : (hh + 1) * GLA_CHUNK, :]
                dst[h] = _btn(vhs[h], (kte * m).astype(BF16))
        o_intra, befores = {}, {}
        for h in heads:
            o_intra[h] = _bnn(jnp.where(causal, sc_raw[h], 0.0).astype(BF16), vhs[h])
            st, before = states[h], [None] * n
            for j in order:
                before[j] = st
                st = st * decs[h // 2][j] + dst[h][j]
            states[h] = st
            befores[h] = jnp.stack(before).astype(BF16)
        outs = {}
        for pair in range(2):
            both = jnp.concatenate([befores[2 * pair], befores[2 * pair + 1]], axis=1)
            o_inter = _bnt(qds[pair], both)
            for hh in range(2):
                h = 2 * pair + hh
                outs[h] = (o_intra[h] + o_inter[:, :, hh * GLA_DV : (hh + 1) * GLA_DV]).reshape(tg, GLA_DV)
        for h in range(GLA_HEADS):
            o_ref[:, pl.ds(h * GLA_DV, GLA_DV)] = outs[h]
            st_ref[:, h] = befores[h]
            carry[h] = states[h]

    nchunks = seq // GLA_CHUNK
    return pl.pallas_call(
        body,
        name="gla_fwd_rev" if reverse else "gla_fwd",
        grid=(nt,),
        in_specs=[
            pl.BlockSpec((tg, KEY_W), lambda i: (tile(i), COL_Q // KEY_W)),
            pl.BlockSpec((tg, KEY_W), lambda i: (tile(i), COL_K // KEY_W)),
            pl.BlockSpec((tg, GLA_W), lambda i: (tile(i), COL_V // GLA_W)),
            pl.BlockSpec((tg, LANES), lambda i: (tile(i), COL_LR // LANES)),
            _resident((LANES, KEY_W)),
            _resident((1, KEY_W)),
            _resident(TOKEN_SHAPE),
        ],
        out_specs=[
            pl.BlockSpec((tg, GLA_W), lambda i: (tile(i), 0)),
            pl.BlockSpec((n, GLA_HEADS, GLA_DV, LANES), lambda i: (tile(i), 0, 0, 0)),
        ],
        out_shape=[
            jax.ShapeDtypeStruct((seq, GLA_W), F32),
            jax.ShapeDtypeStruct((nchunks, GLA_HEADS, GLA_DV, LANES), BF16),
        ],
        scratch_shapes=[pltpu.VMEM((GLA_HEADS, GLA_DV, LANES), F32)],
        compiler_params=_params(48),
    )(p, p, p, p, wd_pad, bd, token)


def _mixer_out(x, o_f, o_b, p, gn, lng, lnb, ws_bf, bs_col, w_out, token):
    seq = x.shape[0]
    tm = min(seq, 512)

    def body(x_ref, of_ref, ob_ref, g_ref, u_ref, vv_ref, gn_ref, lng_ref, lnb_ref, ws_ref, bs_ref, wo_ref, token_ref, x1_ref, yc_ref, vn_sc):
        for h in range(GLA_HEADS):
            cols = pl.ds(h * GLA_DV, GLA_DV)
            oh = of_ref[:, cols] + ob_ref[:, cols]
            on = oh * lax.rsqrt(jnp.mean(oh * oh, axis=-1, keepdims=True) + EPS)
            gh = g_ref[:, cols]
            yc_ref[:, cols] = (on * gn_ref[:, cols] * (gh * _sigmoid(gh))).astype(BF16)
        zv = _gelu(vv_ref[...])
        xc = zv - jnp.mean(zv, axis=-1, keepdims=True)
        vhat = xc * lax.rsqrt(jnp.mean(xc * xc, axis=-1, keepdims=True) + EPS)
        vn_sc[...] = (vhat * lng_ref[...] + lnb_ref[...]).astype(BF16)
        for c in range(tm // GMLP_CHUNK):
            rows = pl.ds(c * GMLP_CHUNK, GMLP_CHUNK)
            for g in range(GMLP_GROUPS):
                cols = pl.ds(g * LANES, LANES)
                s = _nn(ws_ref[g], vn_sc[rows, cols]) + bs_ref[g]
                yc_ref[rows, pl.ds(GLA_W + g * LANES, LANES)] = (_gelu(u_ref[rows, cols]) * s).astype(BF16)
        x1_ref[...] = x_ref[...] + _nn(yc_ref[...], wo_ref[...])

    row = lambda w: pl.BlockSpec((tm, w), lambda i: (i, 0))
    pcol = lambda col: pl.BlockSpec((tm, GLA_W), lambda i: (i, col // GLA_W))
    return pl.pallas_call(
        body,
        name="mixer_out",
        grid=(seq // tm,),
        in_specs=[
            row(D_MODEL), row(GLA_W), row(GLA_W), pcol(COL_G), pcol(COL_U), pcol(COL_VV),
            _resident((1, GLA_W)), _resident((1, GMLP_W)), _resident((1, GMLP_W)),
            _resident((GMLP_GROUPS, GMLP_CHUNK, GMLP_CHUNK)), _resident((GMLP_GROUPS, GMLP_CHUNK, 1)),
            _resident((D_MODEL, D_MODEL)), _resident(TOKEN_SHAPE),
        ],
        out_specs=[row(D_MODEL), row(D_MODEL)],
        out_shape=[jax.ShapeDtypeStruct((seq, D_MODEL), F32), jax.ShapeDtypeStruct((seq, D_MODEL), BF16)],
        scratch_shapes=[pltpu.VMEM((tm, GMLP_W), BF16)],
        compiler_params=_params(48, ("parallel",)),
    )(x, o_f, o_b, p, p, p, gn, lng, lnb, ws_bf, bs_col, w_out, token)


def _ffn_fwd(x1, target, g2, gf, wg_t, wu_t, wd):
    seq = x1.shape[0]
    tm = min(seq, 256)

    def body(x1_ref, t_ref, g2_ref, gf_ref, wg_ref, wu_ref, wd_ref, h2_ref, gate_ref, up_ref, act_ref, dx2_ref, loss_ref, dgf_ref):
        @pl.when(pl.program_id(0) == 0)
        def _():
            loss_ref[...] = jnp.zeros_like(loss_ref)
            dgf_ref[...] = jnp.zeros_like(dgf_ref)

        x1v = x1_ref[...]
        h2 = (x1v * lax.rsqrt(jnp.mean(x1v * x1v, axis=-1, keepdims=True) + EPS) * g2_ref[...]).astype(BF16)
        h2_ref[...] = h2
        gate = _nt(h2, wg_ref[...])
        up = _nt(h2, wu_ref[...])
        act = (gate * _sigmoid(gate) * up).astype(BF16)
        gate_ref[...] = gate
        up_ref[...] = up
        act_ref[...] = act
        x2 = x1v + _nn(act, wd_ref[...])
        rf = lax.rsqrt(jnp.mean(x2 * x2, axis=-1, keepdims=True) + EPS)
        xh = x2 * rf
        err = xh * gf_ref[...] - t_ref[...]
        loss_ref[...] += 0.5 * jnp.sum(jnp.mean(err * err, axis=-1, keepdims=True))
        dy = err * (1.0 / D_MODEL)
        dgf_ref[...] += jnp.sum(dy * xh, axis=0, keepdims=True)
        dx2_ref[...] = _rms_bwd(dy * gf_ref[...], xh, rf)

    row = lambda w: pl.BlockSpec((tm, w), lambda i: (i, 0))
    weight = _resident((D_FF, D_MODEL))
    return pl.pallas_call(
        body,
        name="ffn_fwd",
        grid=(seq // tm,),
        in_specs=[row(D_MODEL), row(D_MODEL), _resident((1, D_MODEL)), _resident((1, D_MODEL)), weight, weight, weight],
        out_specs=[row(D_MODEL), row(D_FF), row(D_FF), row(D_FF), row(D_MODEL),
                   pl.BlockSpec((1, LANES), lambda i: (0, 0)), pl.BlockSpec((1, D_MODEL), lambda i: (0, 0))],
        out_shape=[
            jax.ShapeDtypeStruct((seq, D_MODEL), BF16),
            jax.ShapeDtypeStruct((seq, D_FF), F32),
            jax.ShapeDtypeStruct((seq, D_FF), F32),
            jax.ShapeDtypeStruct((seq, D_FF), BF16),
            jax.ShapeDtypeStruct((seq, D_MODEL), F32),
            jax.ShapeDtypeStruct((1, LANES), F32),
            jax.ShapeDtypeStruct((1, D_MODEL), F32),
        ],
        compiler_params=_params(56),
    )(x1, target, g2, gf, wg_t, wu_t, wd)


def _ffn_bwd(dx2, gate, up, x1, g2, wg_t, wu_t, wd):
    seq = x1.shape[0]
    tm = min(seq, 256)

    def body(dx2_ref, gate_ref, up_ref, x1_ref, g2_ref, wg_ref, wu_ref, wd_ref, dgate_ref, dup_ref, dx1_ref, dg2_ref):
        @pl.when(pl.program_id(0) == 0)
        def _():
            dg2_ref[...] = jnp.zeros_like(dg2_ref)

        dx2v = dx2_ref[...]
        dact = _nt(dx2v.astype(BF16), wd_ref[...])
        gate = gate_ref[...]
        sg = _sigmoid(gate)
        dgate = (dact * up_ref[...] * (sg * (1.0 + gate * (1.0 - sg)))).astype(BF16)
        dup = (dact * (gate * sg)).astype(BF16)
        dgate_ref[...] = dgate
        dup_ref[...] = dup
        dh2 = _nn(dgate, wg_ref[...]) + _nn(dup, wu_ref[...])
        x1v = x1_ref[...]
        r2 = lax.rsqrt(jnp.mean(x1v * x1v, axis=-1, keepdims=True) + EPS)
        xh = x1v * r2
        dg2_ref[...] += jnp.sum(dh2 * xh, axis=0, keepdims=True)
        dx1_ref[...] = dx2v + _rms_bwd(dh2 * g2_ref[...], xh, r2)

    row = lambda w: pl.BlockSpec((tm, w), lambda i: (i, 0))
    weight = _resident((D_FF, D_MODEL))
    return pl.pallas_call(
        body,
        name="ffn_bwd",
        grid=(seq // tm,),
        in_specs=[row(D_MODEL), row(D_FF), row(D_FF), row(D_MODEL), _resident((1, D_MODEL)), weight, weight, weight],
        out_specs=[row(D_FF), row(D_FF), row(D_MODEL), pl.BlockSpec((1, D_MODEL), lambda i: (0, 0))],
        out_shape=[
            jax.ShapeDtypeStruct((seq, D_FF), BF16),
            jax.ShapeDtypeStruct((seq, D_FF), BF16),
            jax.ShapeDtypeStruct((seq, D_MODEL), F32),
            jax.ShapeDtypeStruct((1, D_MODEL), F32),
        ],
        compiler_params=_params(56),
    )(dx2, gate, up, x1, g2, wg_t, wu_t, wd)


WGRAD_ROWS = D_FF // 2


def _ffn_wgrad(h2, dgate, dup, act, dx2):
    seq = h2.shape[0]
    tm = min(seq, 512)

    def body(h2_ref, dgate_ref, dup_ref, act_ref, dx2_ref, dwg_ref, dwu_ref, dwd_ref):
        @pl.when(pl.program_id(1) == 0)
        def _():
            dwg_ref[...] = jnp.zeros_like(dwg_ref)
            dwu_ref[...] = jnp.zeros_like(dwu_ref)
            dwd_ref[...] = jnp.zeros_like(dwd_ref)

        h2v = h2_ref[...]
        dwg_ref[...] += _tn(dgate_ref[...], h2v)
        dwu_ref[...] += _tn(dup_ref[...], h2v)
        dwd_ref[...] += _tn(act_ref[...], dx2_ref[...].astype(BF16))

    ff = pl.BlockSpec((tm, WGRAD_ROWS), lambda j, i: (i, j))
    row = pl.BlockSpec((tm, D_MODEL), lambda j, i: (i, 0))
    out = pl.BlockSpec((WGRAD_ROWS, D_MODEL), lambda j, i: (j, 0))
    return pl.pallas_call(
        body,
        name="ffn_wgrad",
        grid=(D_FF // WGRAD_ROWS, seq // tm),
        in_specs=[row, ff, ff, ff, row],
        out_specs=[out, out, out],
        out_shape=[jax.ShapeDtypeStruct((D_FF, D_MODEL), F32)] * 3,
        compiler_params=_params(56, ("parallel", "arbitrary")),
    )(h2, dgate, dup, act, dx2)


def _mixer_bwd(dx1, ycat, o_f, o_b, p, gn, lng, lnb, ws_bf, wst_bf, bs_col, w_out, token):
    seq = dx1.shape[0]
    tm = min(seq, 512)
    nsteps = seq // tm

    def body(dx1_ref, yc_ref, of_ref, ob_ref, g_ref, u_ref, vv_ref, gn_ref, lng_ref, lnb_ref, ws_ref, wst_ref, bs_ref, wo_ref, token_ref,
             do_ref, dg_ref, du_ref, dvv_ref, dwo_ref, dgn_ref, dlng_ref, dlnb_ref, dws_ref, dbs_ref, vn_sc, dvn_sc, dbs_acc):
        step = pl.program_id(0)

        @pl.when(step == 0)
        def _():
            for r in (dwo_ref, dgn_ref, dlng_ref, dlnb_ref, dws_ref, dbs_acc):
                r[...] = jnp.zeros_like(r)

        dx1b = dx1_ref[...].astype(BF16)
        dyc = _nt(dx1b, wo_ref[...])
        dwo_ref[...] += _tn(yc_ref[...], dx1b)
        for h in range(GLA_HEADS):
            cols = pl.ds(h * GLA_DV, GLA_DV)
            dya = dyc[:, h * GLA_DV : (h + 1) * GLA_DV]
            oh = of_ref[:, cols] + ob_ref[:, cols]
            rn = lax.rsqrt(jnp.mean(oh * oh, axis=-1, keepdims=True) + EPS)
            on = oh * rn
            gh = g_ref[:, cols]
            sg = _sigmoid(gh)
            sil = gh * sg
            gnh = gn_ref[:, cols]
            dgn_ref[:, cols] += jnp.sum(dya * on * sil, axis=0, keepdims=True)
            dg_ref[:, cols] = (dya * on * gnh * (sg * (1.0 + gh * (1.0 - sg)))).astype(BF16)
            do_ref[:, cols] = _rms_bwd(dya * gnh * sil, on, rn)
        vv = vv_ref[...]
        zv, zv_grad = _gelu_and_grad(vv)
        xc = zv - jnp.mean(zv, axis=-1, keepdims=True)
        rstd = lax.rsqrt(jnp.mean(xc * xc, axis=-1, keepdims=True) + EPS)
        vhat = xc * rstd
        vn_sc[...] = (vhat * lng_ref[...] + lnb_ref[...]).astype(BF16)
        for c in range(tm // GMLP_CHUNK):
            rows = pl.ds(c * GMLP_CHUNK, GMLP_CHUNK)
            for g in range(GMLP_GROUPS):
                cols = pl.ds(g * LANES, LANES)
                vn = vn_sc[rows, cols]
                s = _nn(ws_ref[g], vn) + bs_ref[g]
                dyb = dyc[c * GMLP_CHUNK : (c + 1) * GMLP_CHUNK, GLA_W + g * LANES : GLA_W + (g + 1) * LANES]
                zu, zu_grad = _gelu_and_grad(u_ref[rows, cols])
                du_ref[rows, cols] = (dyb * s * zu_grad).astype(BF16)
                ds = dyb * zu
                dbs_acc[g] += ds
                dsb = ds.astype(BF16)
                dws_ref[g] += _nt(dsb, vn)
                dvn_sc[rows, cols] = _nn(wst_ref[g], dsb)
        dvn = dvn_sc[...]
        dlng_ref[...] += jnp.sum(dvn * vhat, axis=0, keepdims=True)
        dlnb_ref[...] += jnp.sum(dvn, axis=0, keepdims=True)
        dvh = dvn * lng_ref[...]
        dzv = rstd * (dvh - jnp.mean(dvh, axis=-1, keepdims=True) - vhat * jnp.mean(dvh * vhat, axis=-1, keepdims=True))
        dvv_ref[...] = (dzv * zv_grad).astype(BF16)

        @pl.when(step == nsteps - 1)
        def _():
            dbs_ref[...] = jnp.sum(dbs_acc[...], axis=-1, keepdims=True)

    row = lambda w: pl.BlockSpec((tm, w), lambda i: (i, 0))
    pcol = lambda col: pl.BlockSpec((tm, GLA_W), lambda i: (i, col // GLA_W))
    const = lambda shape: pl.BlockSpec(shape, lambda i: (0,) * len(shape))
    return pl.pallas_call(
        body,
        name="mixer_bwd",
        grid=(nsteps,),
        in_specs=[
            row(D_MODEL), row(D_MODEL), row(GLA_W), row(GLA_W), pcol(COL_G), pcol(COL_U), pcol(COL_VV),
            _resident((1, GLA_W)), _resident((1, GMLP_W)), _resident((1, GMLP_W)),
            _resident((GMLP_GROUPS, GMLP_CHUNK, GMLP_CHUNK)), _resident((GMLP_GROUPS, GMLP_CHUNK, GMLP_CHUNK)),
            _resident((GMLP_GROUPS, GMLP_CHUNK, 1)), _resident((D_MODEL, D_MODEL)), _resident(TOKEN_SHAPE),
        ],
        out_specs=[
            row(GLA_W), row(GLA_W), row(GMLP_W), row(GMLP_W), const((D_MODEL, D_MODEL)),
            const((1, GLA_W)), const((1, GMLP_W)), const((1, GMLP_W)),
            const((GMLP_GROUPS, GMLP_CHUNK, GMLP_CHUNK)), const((GMLP_GROUPS, GMLP_CHUNK, 1)),
        ],
        out_shape=[
            jax.ShapeDtypeStruct((seq, GLA_W), F32), jax.ShapeDtypeStruct((seq, GLA_W), BF16),
            jax.ShapeDtypeStruct((seq, GMLP_W), BF16), jax.ShapeDtypeStruct((seq, GMLP_W), BF16),
            jax.ShapeDtypeStruct((D_MODEL, D_MODEL), F32),
            jax.ShapeDtypeStruct((1, GLA_W), F32), jax.ShapeDtypeStruct((1, GMLP_W), F32), jax.ShapeDtypeStruct((1, GMLP_W), F32),
            jax.ShapeDtypeStruct((GMLP_GROUPS, GMLP_CHUNK, GMLP_CHUNK), F32), jax.ShapeDtypeStruct((GMLP_GROUPS, GMLP_CHUNK, 1), F32),
        ],
        scratch_shapes=[pltpu.VMEM((tm, GMLP_W), BF16), pltpu.VMEM((tm, GMLP_W), F32), pltpu.VMEM((GMLP_GROUPS, GMLP_CHUNK, GMLP_CHUNK), F32)],
        compiler_params=_params(56),
    )(dx1, ycat, o_f, o_b, p, p, p, gn, lng, lnb, ws_bf, wst_bf, bs_col, w_out, token)


def _gla_bwd(p, do, st, wd_pad, bd, token, reverse, other=None):
    seq = p.shape[0]
    tg = _gla_tile(seq)
    nt = seq // tg
    n = tg // GLA_CHUNK
    scale = GLA_DK**-0.5

    def tile(i):
        return i if reverse else nt - 1 - i

    def body(q_ref, k_ref, v_ref, lr_ref, do_ref, st_ref, wd_ref, bd_ref, token_ref, *rest):
        others, (dq_ref, dk_ref, dv_ref, dlr_ref, dwd_ref, dbd_ref, carry) = rest[:-7], rest[-7:]
        if others:
            odq_ref, odk_ref, odv_ref, odlr_ref = others

            def put(ref, idx, val, oref):
                ref[idx] = (val + oref[idx]).astype(BF16)
        else:
            odq_ref = odk_ref = odv_ref = odlr_ref = None

            def put(ref, idx, val, oref):
                ref[idx] = val

        @pl.when(pl.program_id(0) == 0)
        def _():
            carry[...] = jnp.zeros_like(carry)
            dwd_ref[...] = jnp.zeros_like(dwd_ref)
            dbd_ref[...] = jnp.zeros_like(dbd_ref)

        lr_bf = lr_ref[...].astype(BF16)
        carries = [carry[h] for h in range(GLA_HEADS)]
        row_in_chunk = lax.broadcasted_iota(jnp.int32, (tg, LANES), 0) % GLA_CHUNK
        lane_head = lax.broadcasted_iota(jnp.int32, (1, LANES), 1) // GLA_DK
        tt = lax.broadcasted_iota(jnp.int32, (GLA_CHUNK, GLA_CHUNK), 0)
        ss = lax.broadcasted_iota(jnp.int32, (GLA_CHUNK, GLA_CHUNK), 1)
        causal = (tt <= ss) if reverse else (tt >= ss)
        order = range(n) if reverse else range(n - 1, -1, -1)
        dlr = jnp.zeros((tg, LANES), F32)
        heads = range(GLA_HEADS)
        pv, masks, qdh, vhs, dohs, stbs = {}, {}, {}, {}, {}, {}
        sc_raw, dp, acc = {}, {}, {}
        for pair in range(2):
            cols = pl.ds(pair * LANES, LANES)
            pre, b3, blast = _gla_decay_terms(lr_bf, wd_ref, bd_ref, pair, row_in_chunk, reverse, n)
            q3 = q_ref[:, cols].reshape(n, GLA_CHUNK, LANES) * scale
            k3 = k_ref[:, cols].reshape(n, GLA_CHUNK, LANES)
            eb = jnp.exp(b3)
            emb = jnp.exp(-b3)
            ekte = jnp.exp(blast - b3)
            kdf = k3 * emb
            kte = k3 * ekte
            both = pl.ds(2 * pair * GLA_DV, 2 * GLA_DV)
            pv[pair] = dict(pre=pre, eb=eb, emb=emb, ekte=ekte, qd=q3 * eb, kdf=kdf, kd=kdf.astype(BF16), kte=kte, kte_bf=kte.astype(BF16),
                            dec=jnp.exp(blast), v=v_ref[:, both].reshape(n, GLA_CHUNK, 2 * GLA_DV).astype(BF16),
                            do=do_ref[:, both].reshape(n, GLA_CHUNK, 2 * GLA_DV).astype(BF16))
            for hh in range(2):
                h = 2 * pair + hh
                masks[h] = (lane_head == hh).astype(F32)
                qdh[h] = (pv[pair]["qd"] * masks[h]).astype(BF16)
                vhs[h] = pv[pair]["v"][:, :, hh * GLA_DV : (hh + 1) * GLA_DV]
                dohs[h] = pv[pair]["do"][:, :, hh * GLA_DV : (hh + 1) * GLA_DV]
                stbs[h] = st_ref[:, h]
                dp[h] = _bnt(dohs[h], vhs[h])
                acc[h] = _btn(dohs[h], qdh[h])
            sc_both = _bnt(jnp.concatenate([qdh[2 * pair], qdh[2 * pair + 1]], axis=1), pv[pair]["kd"])
            for hh in range(2):
                sc_raw[2 * pair + hh] = sc_both[:, hh * GLA_CHUNK : (hh + 1) * GLA_CHUNK, :]
        dsa, sc = {}, {}
        for h in heads:
            sc[h] = jnp.where(causal, sc_raw[h], 0.0).astype(BF16)
            dp[h] = jnp.where(causal, dp[h], 0.0).astype(BF16)
            dec = pv[h // 2]["dec"]
            c, after = carries[h], [None] * n
            for j in order:
                after[j] = c
                c = acc[h][j] + dec[j] * c
            carries[h] = c
            dsa[h] = jnp.stack(after)
        dvs, dqs, dks, dwds, dbds = [], [], [], [], []
        for pair in range(2):
            cols = pl.ds(pair * LANES, LANES)
            v = pv[pair]
            h0, h1 = 2 * pair, 2 * pair + 1
            dsa_both = jnp.concatenate([dsa[h0], dsa[h1]], axis=1)
            dsa_bf = dsa_both.astype(BF16)
            stb_bf = jnp.concatenate([stbs[h0], stbs[h1]], axis=1)
            dq_intra = _bnn(jnp.concatenate([dp[h0], dp[h1]], axis=1), v["kd"])
            dqd = (dq_intra[:, :GLA_CHUNK, :] * masks[h0] + dq_intra[:, GLA_CHUNK:, :] * masks[h1]) + _bnn(v["do"], stb_bf)
            dkd = _btn(dp[h0], qdh[h0]) + _btn(dp[h1], qdh[h1])
            dkte = _bnn(v["v"], dsa_bf)
            ddec = jnp.sum(dsa[h0] * stbs[h0].astype(F32) + dsa[h1] * stbs[h1].astype(F32), axis=1, keepdims=True)
            dv_inter = _bnt(v["kte_bf"], dsa_bf)
            for hh, h in ((0, h0), (1, h1)):
                dvs.append((_btn(sc[h], dohs[h]) + dv_inter[:, :, hh * GLA_DV : (hh + 1) * GLA_DV]).reshape(tg, GLA_DV))
            dqs.append((dqd * (scale * v["eb"])).reshape(tg, LANES))
            dks.append((dkd * v["emb"] + dkte * v["ekte"]).reshape(tg, LANES))
            db = dqd * v["qd"] - dkd * v["kdf"] - dkte * v["kte"]
            dblast = jnp.sum(dkte * v["kte"], axis=1, keepdims=True) + ddec * v["dec"]
            dla = _chunk_cumsum(db.reshape(tg, LANES), row_in_chunk, not reverse) + jnp.broadcast_to(dblast, (n, GLA_CHUNK, LANES)).reshape(tg, LANES)
            dpre = (dla * (1.0 / GLA_TAU) * _sigmoid(-v["pre"]))
            dpre_bf = dpre.astype(BF16)
            dlr = dlr + _nt(dpre_bf, wd_ref[:, cols])
            dwds.append(_tn(lr_bf, dpre_bf))
            dbds.append(jnp.sum(dpre, axis=0, keepdims=True))
        put(dlr_ref, (slice(None), slice(None)), dlr, odlr_ref)
        for pair in range(2):
            cols = pl.ds(pair * LANES, LANES)
            put(dq_ref, (slice(None), cols), dqs[pair], odq_ref)
            put(dk_ref, (slice(None), cols), dks[pair], odk_ref)
            dwd_ref[:, cols] += dwds[pair]
            dbd_ref[:, cols] += dbds[pair]
        for h in range(GLA_HEADS):
            put(dv_ref, (slice(None), pl.ds(h * GLA_DV, GLA_DV)), dvs[h], odv_ref)
            carry[h] = carries[h]

    pieces = [
        pl.BlockSpec((tg, KEY_W), lambda i: (tile(i), 0)),
        pl.BlockSpec((tg, KEY_W), lambda i: (tile(i), 0)),
        pl.BlockSpec((tg, GLA_W), lambda i: (tile(i), 0)),
        pl.BlockSpec((tg, LANES), lambda i: (tile(i), 0)),
    ]
    piece_dtype = BF16 if other else F32
    return pl.pallas_call(
        body,
        name="gla_bwd_rev" if reverse else "gla_bwd",
        grid=(nt,),
        in_specs=[
            pl.BlockSpec((tg, KEY_W), lambda i: (tile(i), COL_Q // KEY_W)),
            pl.BlockSpec((tg, KEY_W), lambda i: (tile(i), COL_K // KEY_W)),
            pl.BlockSpec((tg, GLA_W), lambda i: (tile(i), COL_V // GLA_W)),
            pl.BlockSpec((tg, LANES), lambda i: (tile(i), COL_LR // LANES)),
            pl.BlockSpec((tg, GLA_W), lambda i: (tile(i), 0)),
            pl.BlockSpec((n, GLA_HEADS, GLA_DV, LANES), lambda i: (tile(i), 0, 0, 0)),
            _resident((LANES, KEY_W)),
            _resident((1, KEY_W)),
            _resident(TOKEN_SHAPE),
        ] + (pieces if other else []),
        out_specs=pieces + [pl.BlockSpec((LANES, KEY_W), lambda i: (0, 0)), pl.BlockSpec((1, KEY_W), lambda i: (0, 0))],
        out_shape=[
            jax.ShapeDtypeStruct((seq, KEY_W), piece_dtype), jax.ShapeDtypeStruct((seq, KEY_W), piece_dtype),
            jax.ShapeDtypeStruct((seq, GLA_W), piece_dtype), jax.ShapeDtypeStruct((seq, LANES), piece_dtype),
            jax.ShapeDtypeStruct((LANES, KEY_W), F32), jax.ShapeDtypeStruct((1, KEY_W), F32),
        ],
        scratch_shapes=[pltpu.VMEM((GLA_HEADS, GLA_DV, LANES), F32)],
        compiler_params=_params(56),
    )(p, p, p, p, do, st, wd_pad, bd, token, *(other or ()))


def _inproj_wgrad(x, g1, dq, dk, dv, dg, du, dvv, dlr):
    seq = x.shape[0]
    tm = min(seq, 512)

    def body(x_ref, g1_ref, dq_ref, dk_ref, dv_ref, dg_ref, du_ref, dvv_ref, dlr_ref, dw_ref, dp_ref):
        @pl.when(pl.program_id(0) == 0)
        def _():
            dw_ref[...] = jnp.zeros_like(dw_ref)

        for col, ref in ((COL_Q, dq_ref), (COL_K, dk_ref), (COL_V, dv_ref), (COL_G, dg_ref), (COL_U, du_ref), (COL_VV, dvv_ref), (COL_LR, dlr_ref)):
            dp_ref[:, col : col + ref.shape[1]] = ref[...]
        xv = x_ref[...]
        h = (xv * lax.rsqrt(jnp.mean(xv * xv, axis=-1, keepdims=True) + EPS) * g1_ref[...]).astype(BF16)
        dw_ref[0:ROW_LR, :] += _tn(dp_ref[:, 0:COL_U], h)
        dw_ref[ROW_UV:PROJ_W, :] += _tn(dp_ref[:, COL_U:COL_LR], h)
        dw_ref[ROW_LR:ROW_UV, :] += _tn(dp_ref[:, COL_LR:PROJ_WP], h)[0 : ROW_UV - ROW_LR]

    row = lambda w: pl.BlockSpec((tm, w), lambda i: (i, 0))
    return pl.pallas_call(
        body,
        name="inproj_wgrad",
        grid=(seq // tm,),
        in_specs=[row(D_MODEL), _resident((1, D_MODEL)), row(KEY_W), row(KEY_W), row(GLA_W), row(GLA_W), row(GMLP_W), row(GMLP_W), row(LANES)],
        out_specs=[pl.BlockSpec((PROJ_W, D_MODEL), lambda i: (0, 0)), row(PROJ_WP)],
        out_shape=[jax.ShapeDtypeStruct((PROJ_W, D_MODEL), F32), jax.ShapeDtypeStruct((seq, PROJ_WP), BF16)],
        compiler_params=_params(56),
    )(x, g1, dq, dk, dv, dg, du, dvv, dlr)


def _inproj_dx(x, dx1, g1, w_in_t, dp, token):
    seq = x.shape[0]
    tm = min(seq, 512)

    def body(x_ref, dx1_ref, g1_ref, w_ref, dp_ref, token_ref, dx_ref, dg1_ref):
        @pl.when(pl.program_id(0) == 0)
        def _():
            dg1_ref[...] = jnp.zeros_like(dg1_ref)

        xv = x_ref[...]
        r1 = lax.rsqrt(jnp.mean(xv * xv, axis=-1, keepdims=True) + EPS)
        xh = xv * r1
        dh = (_nn(dp_ref[:, 0:COL_U], w_ref[0:ROW_LR, :]) + _nn(dp_ref[:, COL_U:COL_LR], w_ref[ROW_UV:PROJ_W, :])
              + _nn(dp_ref[:, COL_LR:PROJ_WP], w_ref[ROW_LR : ROW_LR + LANES, :]))
        dg1_ref[...] += jnp.sum(dh * xh, axis=0, keepdims=True)
        dx_ref[...] = dx1_ref[...] + _rms_bwd(dh * g1_ref[...], xh, r1)

    row = lambda w: pl.BlockSpec((tm, w), lambda i: (i, 0))
    return pl.pallas_call(
        body,
        name="inproj_dx",
        grid=(seq // tm,),
        in_specs=[row(D_MODEL), row(D_MODEL), _resident((1, D_MODEL)), _resident((PROJ_W, D_MODEL)), row(PROJ_WP), _resident(TOKEN_SHAPE)],
        out_specs=[row(D_MODEL), pl.BlockSpec((1, D_MODEL), lambda i: (0, 0))],
        out_shape=[jax.ShapeDtypeStruct((seq, D_MODEL), F32), jax.ShapeDtypeStruct((1, D_MODEL), F32)],
        compiler_params=_params(48),
    )(x, dx1, g1, w_in_t, dp, token)


def _in_hbm(a):
    return pltpu.with_memory_space_constraint(a, pltpu.HBM)


def _row_tile(rows, multiple=8):
    for t in range(min(rows, 512), 0, -1):
        if rows % t == 0 and t % multiple == 0:
            return t
    return rows


def _cast_into_slot(w, shard, token):
    rows, cols = w.shape
    tr = _row_tile(rows, 16)

    def body(s_ref, w_ref, token_ref, o_ref):
        o_ref[...] = w_ref[...].astype(BF16)

    return pl.pallas_call(
        body,
        name="cast_into_slot",
        grid_spec=pltpu.PrefetchScalarGridSpec(
            num_scalar_prefetch=1,
            grid=(rows // tr,),
            in_specs=[pl.BlockSpec((tr, cols), lambda i, s_ref: (i, 0)), pl.BlockSpec(TOKEN_SHAPE, lambda i, s_ref: (0, 0))],
            out_specs=pl.BlockSpec((None, tr, cols), lambda i, s_ref: (s_ref[0], i, 0)),
        ),
        out_shape=pltpu.HBM((N_SHARDS, rows, cols), BF16),
        compiler_params=_params(32, ("parallel",)),
    )(shard, _in_hbm(w), token)


def _add_halves(grads4, recvs, shard_core):
    n = len(grads4)
    _, rows, _ = grads4[0].shape
    tr = _row_tile(rows, 16)

    def body(sc_ref, *refs):
        for k in range(n):
            total = refs[k][...] + refs[n + k][...]
            refs[3 * n + k][...] = total.astype(BF16)

            @pl.when(pl.program_id(1) == sc_ref[0])
            def _(k=k, total=total):
                refs[2 * n + k][...] = total

    theirs = pl.BlockSpec((None, tr, HALF), lambda i, s, sc_ref: (s, i, 0))
    mine = pl.BlockSpec((None, tr, HALF), lambda i, s, sc_ref: (s, i, sc_ref[1]))
    kept = pl.BlockSpec((tr, HALF), lambda i, s, sc_ref: (i, 0))
    outs = pl.pallas_call(
        body,
        name="add_halves",
        grid_spec=pltpu.PrefetchScalarGridSpec(
            num_scalar_prefetch=1,
            grid=(rows // tr, N_SHARDS),
            in_specs=[mine] * n + [theirs] * n,
            out_specs=[kept] * n + [theirs] * n,
        ),
        out_shape=[pltpu.HBM((rows, HALF), F32)] * n + [pltpu.HBM((N_SHARDS, rows, HALF), BF16)] * n,
        compiler_params=_params(48, ("parallel", "arbitrary")),
    )(shard_core, *[_in_hbm(a) for a in list(grads4) + list(recvs)])
    return list(zip(outs[:n], outs[n:]))


def _add_partials(part, recv3, shard_core, token):
    rows, _ = part.shape
    tr = _row_tile(rows, 16)

    def body(sc_ref, p_ref, r_ref, token_ref, o_ref):
        o_ref[...] = ((p_ref[...] + r_ref[0].astype(F32)) + r_ref[1].astype(F32)) + r_ref[2].astype(F32)

    return pl.pallas_call(
        body,
        name="add_partials",
        grid_spec=pltpu.PrefetchScalarGridSpec(
            num_scalar_prefetch=1,
            grid=(rows // tr,),
            in_specs=[
                pl.BlockSpec((tr, HALF), lambda i, sc_ref: (i, 0)),
                pl.BlockSpec((3, tr, HALF), lambda i, sc_ref: (0, i, 0)),
                pl.BlockSpec(TOKEN_SHAPE, lambda i, sc_ref: (0, 0)),
            ],
            out_specs=pl.BlockSpec((tr, HALF), lambda i, sc_ref: (i, sc_ref[1])),
        ),
        out_shape=pltpu.HBM((rows, 2 * HALF), F32),
        compiler_params=_params(32, ("parallel",)),
    )(shard_core, _in_hbm(part), _in_hbm(recv3), token)


def _adam_math(w, g, m, v):
    m = ADAM_B1 * m + (1.0 - ADAM_B1) * g
    v = ADAM_B2 * v + (1.0 - ADAM_B2) * (g * g)
    m_hat = m / (1.0 - ADAM_B1**ADAM_STEP)
    v_hat = v / (1.0 - ADAM_B2**ADAM_STEP)
    delta = -ADAM_LR * (m_hat / (jnp.sqrt(v_hat) + ADAM_EPS) + ADAM_WD * w)
    return delta, m, v


def _adamw(w, g, m, v):
    rows, cols = w.shape
    tr = _row_tile(rows)

    def body(w_ref, g_ref, m_ref, v_ref, go_ref, d_ref, mo_ref, vo_ref):
        gv = g_ref[...]
        go_ref[...] = gv
        d_ref[...], mo_ref[...], vo_ref[...] = _adam_math(w_ref[...], gv, m_ref[...], v_ref[...])

    spec = pl.BlockSpec((tr, cols), lambda i: (i, 0))
    return pl.pallas_call(
        body, name="adamw", grid=(rows // tr,), in_specs=[spec] * 4, out_specs=[spec] * 4, out_shape=[pltpu.HBM(w.shape, F32)] * 4,
        compiler_params=_params(32, ("parallel",)),
    )(_in_hbm(w), _in_hbm(g), _in_hbm(m), _in_hbm(v))


SMALL_ROWS = 560
DECAY_ROWS = 8
SMALL_TOTAL = SMALL_ROWS + 2 * N_SHARDS * DECAY_ROWS


def _adamw_small(gathered, own, wp, mp, vp, like):
    out_rows = SMALL_ROWS + 2 * DECAY_ROWS
    places, off = [], 0
    for a in like:
        rows = a.size // LANES
        kept = a.shape[-1] == LANES
        places.append((off, rows, kept, (rows, LANES) if kept else (1, a.size)))
        off += rows
    loss_row = off
    decay_shape = (LOWRANK, KEY_W // N_SHARDS)
    n = len(places) + 2

    def body(ga_ref, own_ref, w_ref, m_ref, v_ref, *refs):
        outs, loss_ref, packed = refs[: 4 * n], refs[4 * n], refs[4 * n + 1 :]
        g_sc = packed[0]
        x, y, c = _position()
        shard, me = 2 * x + y, 4 * x + 2 * y + c
        total = lambda rows: functools.reduce(lambda a, b: a + b, [jnp.where(me == d, own_ref[rows, :], ga_ref[d, rows, :]) for d in range(8)])
        g_sc[pl.ds(0, SMALL_ROWS), :] = total(pl.ds(0, SMALL_ROWS))
        for k in range(2):
            start = pl.multiple_of(SMALL_ROWS + k * N_SHARDS * DECAY_ROWS + shard * DECAY_ROWS, DECAY_ROWS)
            g_sc[pl.ds(SMALL_ROWS + k * DECAY_ROWS, DECAY_ROWS), :] = total(pl.ds(start, DECAY_ROWS))
        packed[1][...], packed[2][...], packed[3][...] = _adam_math(w_ref[...], g_sc[...], m_ref[...], v_ref[...])
        loss_ref[...] = g_sc[loss_row : loss_row + 1, :]
        for t, res in enumerate(packed):
            for (at, rows, kept, _), out in zip(places, outs[t * n :]):
                if kept:
                    out[...] = res[at : at + rows, :]
                else:
                    for r in range(rows):
                        out[:, r * LANES : (r + 1) * LANES] = res[at + r : at + r + 1, :]
            for k in range(2):
                out = outs[t * n + len(places) + k]
                both = res[SMALL_ROWS + k * DECAY_ROWS : SMALL_ROWS + (k + 1) * DECAY_ROWS, :]
                halves = (both, pltpu.roll(both, LANES // 2, axis=1))
                for r in range(DECAY_ROWS):
                    for h in range(2):
                        out[2 * r + h : 2 * r + h + 1, :] = halves[h][r : r + 1, 0 : LANES // 2]

    shapes = [jax.ShapeDtypeStruct(s, F32) for *_, s in places] + [jax.ShapeDtypeStruct(decay_shape, F32)] * 2
    out = pl.pallas_call(
        body,
        name="adamw_small",
        out_shape=shapes * 4 + [jax.ShapeDtypeStruct((1, LANES), F32)],
        scratch_shapes=[pltpu.VMEM((out_rows, LANES), F32)] * 4,
        compiler_params=_params(32, None),
    )(gathered, own, wp, mp, vp)
    return [list(out[t * n : (t + 1) * n]) for t in range(4)], out[4 * n]


ANY = pl.BlockSpec(memory_space=pl.ANY)


def _position():
    return lax.axis_index("x"), lax.axis_index("y"), lax.axis_index("c")


def _other_chips(x, y):
    return [(1 - x, y), (x, 1 - y), (1 - x, 1 - y)]


HBM = pl.BlockSpec(memory_space=pltpu.HBM)
SEM = pl.BlockSpec(memory_space=pltpu.SEMAPHORE)
TOKEN = jax.ShapeDtypeStruct(TOKEN_SHAPE, F32)
DATAFLOW = pltpu.SideEffectType.DATAFLOW_SIDE_EFFECTING


def _half_block(ref4, slot, core):
    return ref4.at[slot, :, pl.ds(pl.multiple_of(core * HALF, HALF), HALF)]


def _gather_ici_copies(bufs, lands, send_sems, recv_sems):
    x, y, c = _position()
    pairs = []
    for k, ref4 in enumerate(bufs):
        mine = _half_block(ref4, 2 * x + y, c)
        for j, (px, py) in enumerate(_other_chips(x, y)):
            sems = dict(send_sem=send_sems.at[3 * k + j], recv_sem=recv_sems.at[3 * k + j], device_id=(px, py, c), device_id_type=MESH)
            pairs.append((functools.partial(pltpu.make_async_remote_copy, src_ref=mine, dst_ref=mine, **sems),
                          functools.partial(pltpu.make_async_remote_copy, src_ref=mine, dst_ref=_half_block(ref4, 2 * px + py, c), **sems)))
    return pairs


def _gather_d2d_copies(bufs, lands, send_sems, recv_sems):
    x, y, c = _position()
    pairs = []
    for k, ref4 in enumerate(bufs):
        for j, (px, py) in enumerate(_other_chips(x, y)):
            have = _half_block(ref4, 2 * px + py, c)
            sems = dict(send_sem=send_sems.at[3 * k + j], recv_sem=recv_sems.at[3 * k + j], device_id=(x, y, 1 - c), device_id_type=MESH)
            pairs.append((functools.partial(pltpu.make_async_remote_copy, src_ref=have, dst_ref=have, **sems),
                          functools.partial(pltpu.make_async_remote_copy, src_ref=have, dst_ref=_half_block(ref4, 2 * px + py, 1 - c), **sems)))
    return pairs


def _gather_forward(bufs):
    n = len(bufs)

    def body(*refs):
        outs = refs[n : 2 * n]
        send_sems, recv_sems = refs[2 * n :]
        d2d = _gather_d2d_copies(outs, (), send_sems, recv_sems)
        for forward, _ in d2d:
            forward().start()
        for forward, arrival in d2d:
            arrival().wait_recv()
            forward().wait_send()

    return pl.pallas_call(
        body,
        name="gather_forward",
        in_specs=[ANY] * n,
        out_specs=[ANY] * n,
        out_shape=[jax.ShapeDtypeStruct(b.shape, b.dtype) for b in bufs],
        input_output_aliases={k: k for k in range(n)},
        scratch_shapes=[pltpu.SemaphoreType.DMA((3 * n,)), pltpu.SemaphoreType.DMA((3 * n,))],
        compiler_params=pltpu.CompilerParams(has_side_effects=True),
    )(*bufs)


def _both_ends(**copy):
    maker = functools.partial(pltpu.make_async_remote_copy, **copy)
    return maker, maker


def _scatter_copies(parts, lands, send_sems, recv_sems):
    x, y, c = _position()
    return [_both_ends(src_ref=parts[k].at[2 * px + py], dst_ref=lands[k].at[j], send_sem=send_sems.at[3 * k + j],
                       recv_sem=recv_sems.at[3 * k + j], device_id=(px, py, c), device_id_type=MESH)
            for k in range(len(parts)) for j, (px, py) in enumerate(_other_chips(x, y))]


def _exchange_copies(grads, lands, send_sems, recv_sems):
    x, y, c = _position()
    return [_both_ends(src_ref=grads[k].at[:, :, pl.ds(pl.multiple_of((1 - c) * HALF, HALF), HALF)], dst_ref=lands[k],
                       send_sem=send_sems.at[k], recv_sem=recv_sems.at[k], device_id=(x, y, 1 - c), device_id_type=MESH)
            for k in range(len(grads))]


def _exchange_lands(grads4):
    return [jax.ShapeDtypeStruct((N_SHARDS, g.shape[1], HALF), g.dtype) for g in grads4]


def _scatter_lands(parts4):
    return [jax.ShapeDtypeStruct((3,) + g.shape[1:], g.dtype) for g in parts4]


def _small_gather_copies(blocks, lands, send_sems, recv_sems):
    x, y, c = _position()
    flip = lambda v, bit: 1 - v if bit else v
    return [_both_ends(src_ref=blocks[0], dst_ref=lands[0].at[4 * x + 2 * y + c], send_sem=send_sems.at[r - 1],
                       recv_sem=recv_sems.at[r - 1], device_id=(flip(x, r & 4), flip(y, r & 2), flip(c, r & 1)), device_id_type=MESH)
            for r in range(1, 8)]


def _split_start(name, srcs, land_shapes, make_copies, nsem, after=()):
    n, nl, na = len(srcs), len(land_shapes), len(after)
    lands = [lax.empty(a.shape, a.dtype) for a in land_shapes]

    def body(*refs):
        send_sems, recv_sems = refs[n + nl + na], refs[n + nl + na + 1]
        token = refs[2 * (n + nl) + na + 2]
        for send, _ in make_copies(refs[:n], refs[n : n + nl], send_sems, recv_sems):
            send().start()
        token[...] = jnp.zeros_like(token)

    hbm = lambda a: pltpu.HBM(a.shape, a.dtype)
    out = pl.pallas_call(
        body,
        name=name,
        in_specs=[HBM] * (n + nl) + [ANY] * na,
        out_specs=(SEM, SEM, *[HBM] * (n + nl), pl.BlockSpec(memory_space=pltpu.VMEM)),
        out_shape=(pltpu.SemaphoreType.DMA((nsem,)), pltpu.SemaphoreType.DMA((nsem,)), *[hbm(a) for a in list(srcs) + lands], TOKEN),
        input_output_aliases={k: 2 + k for k in range(n + nl)},
        compiler_params=pltpu.CompilerParams(has_side_effects=DATAFLOW),
    )(*[pltpu.with_memory_space_constraint(a, pltpu.HBM) for a in list(srcs) + lands], *after)
    return out[0], out[1], list(out[2 : 2 + n]), list(out[2 + n : 2 + n + nl]), out[2 + n + nl]


def _split_wait(name, send_sems, recv_sems, srcs, lands, make_copies, after):
    n, nl = len(srcs), len(lands)

    def body(*refs):
        for send, arrival in make_copies(refs[:n], refs[n : n + nl], refs[n + nl], refs[n + nl + 1]):
            send().wait_send()
            arrival().wait_recv()

    hbm = lambda a: pltpu.HBM(a.shape, a.dtype)
    out = pl.pallas_call(
        body,
        name=name,
        in_specs=[HBM] * (n + nl) + [SEM, SEM] + [ANY] * len(after),
        out_specs=tuple([HBM] * (n + nl)),
        out_shape=tuple(hbm(a) for a in list(srcs) + list(lands)),
        input_output_aliases={k: k for k in range(n + nl)},
        compiler_params=pltpu.CompilerParams(has_side_effects=DATAFLOW),
    )(*srcs, *lands, send_sems, recv_sems, *after)
    return list(out[:n]), list(out[n:])


def _join_copies(bufs, lands, send_sems, recv_sems):
    x, y, c = _position()
    half = lambda ref, core: ref.at[:, pl.ds(pl.multiple_of(core * HALF, HALF), HALF)]
    pairs = []
    for k, ref in enumerate(bufs):
        sems = dict(send_sem=send_sems.at[k], recv_sem=recv_sems.at[k], device_id=(x, y, 1 - c), device_id_type=MESH)
        pairs.append((functools.partial(pltpu.make_async_remote_copy, src_ref=half(ref, c), dst_ref=half(ref, c), **sems),
                      functools.partial(pltpu.make_async_remote_copy, src_ref=half(ref, c), dst_ref=half(ref, 1 - c), **sems)))
    return pairs


def _join_halves(bufs, after=()):
    n, na = len(bufs), len(after)

    def body(*refs):
        joins = _join_copies(refs[n + na : 2 * n + na], (), *refs[2 * n + na :])
        for send, _ in joins:
            send().start()
        for send, arrival in joins:
            send().wait_send()
            arrival().wait_recv()

    return pl.pallas_call(
        body,
        name="join_halves",
        in_specs=[ANY] * (n + na),
        out_specs=[ANY] * n,
        out_shape=[jax.ShapeDtypeStruct(b.shape, b.dtype) for b in bufs],
        input_output_aliases={k: k for k in range(n)},
        scratch_shapes=[pltpu.SemaphoreType.DMA((n,)), pltpu.SemaphoreType.DMA((n,))],
        compiler_params=pltpu.CompilerParams(has_side_effects=True),
    )(*bufs, *after)


def _allgather_small(block):
    m_per, ncol = block.shape

    def body(x_ref, out_ref, send_sems, recv_sems, local_sem):
        x, y, c = _position()
        me, sibling = (x, y, c), (x, y, 1 - c)
        chips = _other_chips(x, y)

        def rows(px, py, pc):
            return out_ref.at[4 * px + 2 * py + pc]

        def copy(k, blk, to, src=None):
            return pltpu.make_async_remote_copy(
                src_ref=rows(*blk) if src is None else src, dst_ref=rows(*blk),
                send_sem=send_sems.at[k], recv_sem=recv_sems.at[k], device_id=to, device_id_type=MESH)

        mine = pltpu.make_async_copy(x_ref, rows(*me), local_sem)
        mine.start()
        first = [copy(0, me, sibling, src=x_ref)] + [copy(1 + j, me, (*chip, c), src=x_ref) for j, chip in enumerate(chips)]
        for cp in first:
            cp.start()
        passed = [copy(4 + j, (*chip, c), sibling) for j, chip in enumerate(chips)]
        for j, chip in enumerate(chips):
            copy(1 + j, (*chip, c), me).wait_recv()
            passed[j].start()
        copy(0, sibling, me).wait_recv()
        for j, chip in enumerate(chips):
            copy(4 + j, (*chip, 1 - c), me).wait_recv()
        for cp in first + passed:
            cp.wait_send()
        mine.wait()

    return pl.pallas_call(
        body,
        name="allgather_small",
        in_specs=[pl.BlockSpec(memory_space=pltpu.VMEM)],
        out_specs=pl.BlockSpec(memory_space=pltpu.VMEM),
        out_shape=jax.ShapeDtypeStruct((8, m_per, ncol), block.dtype),
        scratch_shapes=[pltpu.SemaphoreType.DMA((7,)), pltpu.SemaphoreType.DMA((7,)), pltpu.SemaphoreType.DMA],
        compiler_params=pltpu.CompilerParams(has_side_effects=True, vmem_limit_bytes=32 * MIB),
    )(block)


SMALL_NAMES = ["norm1_g", "b_decay_f", "b_decay_b", "gla_norm_g", "gmlp_ln_g", "gmlp_ln_b", "w_spatial", "b_spatial", "norm2_g", "final_norm_g"]


def _pack_small(parts, decay_parts):
    flat = jnp.concatenate([a.reshape(-1) for a in parts])
    flat = jnp.pad(flat, (0, SMALL_ROWS * LANES - flat.shape[0])).reshape(SMALL_ROWS, LANES)
    return jnp.concatenate([flat] + [d.reshape(-1, LANES) for d in decay_parts], axis=0)


def kernel(x, norm1_g, w_in, w_decay_f, b_decay_f, w_decay_b, b_decay_b, gla_norm_g, gmlp_ln_g, gmlp_ln_b, w_spatial, b_spatial, w_out, norm2_g, w_gate, w_up, w_down, final_norm_g, loss_target, m_norm1_g, m_w_in, m_w_decay_f, m_b_decay_f, m_w_decay_b, m_b_decay_b, m_gla_norm_g, m_gmlp_ln_g, m_gmlp_ln_b, m_w_spatial, m_b_spatial, m_w_out, m_norm2_g, m_w_gate, m_w_up, m_w_down, m_final_norm_g, v_norm1_g, v_w_in, v_w_decay_f, v_b_decay_f, v_w_decay_b, v_b_decay_b, v_gla_norm_g, v_gmlp_ln_g, v_gmlp_ln_b, v_w_spatial, v_b_spatial, v_w_out, v_norm2_g, v_w_gate, v_w_up, v_w_down, v_final_norm_g):
    args = dict(locals())
    cx, cy, cc = lax.axis_index("x"), lax.axis_index("y"), lax.axis_index("c")
    shard = 2 * cx + cy
    xs = x[0]
    target = loss_target[0]

    big_names = ["w_in", "w_out", "w_gate", "w_up", "w_down"]
    transposed = ("w_in", "w_gate", "w_up")
    rows_of = lambda pre, k: jnp.transpose(args[pre + k][0]) if k in transposed else args[pre + k][0]
    big_shards = {k: rows_of("", k) for k in big_names}
    s_arr = shard.reshape(1).astype(jnp.int32)
    sc_arr = jnp.stack([shard, cc]).astype(jnp.int32)
    zero_token = jnp.zeros(TOKEN_SHAPE, F32)
    w_send, w_recv, (w_in4,), _, token_w_in = _split_start(
        "w_in_gather_start", [_cast_into_slot(big_shards["w_in"], s_arr, zero_token)], [], _gather_ici_copies, 3)
    late = ["w_out", "w_gate", "w_up", "w_down"]
    late_slots = [_cast_into_slot(big_shards[k], s_arr, token_w_in) for k in late]
    dec_block = jnp.concatenate([w_decay_f[0].reshape(-1, LANES), w_decay_b[0].reshape(-1, LANES)], axis=0)
    dec_all = _allgather_small(dec_block)
    (w_in4,), _ = _split_wait("w_in_gather_wait", w_send, w_recv, [w_in4], [], _gather_ici_copies, (dec_all, *late_slots))
    (w_in4,) = _gather_forward([w_in4])
    w_in_t = w_in4.reshape(PROJ_W, D_MODEL)
    g_send, g_recv, late_bufs, _, token_gather = _split_start(
        "gather_start", late_slots, [], _gather_ici_copies, 3 * len(late), after=(w_in4,))
    dec_all = dec_all[::2].reshape(N_SHARDS, 2, LOWRANK, KEY_W // N_SHARDS)
    wdf_full = jnp.transpose(dec_all[:, 0], (1, 0, 2)).reshape(LOWRANK, KEY_W)
    wdb_full = jnp.transpose(dec_all[:, 1], (1, 0, 2)).reshape(LOWRANK, KEY_W)
    wd_pad_f = jnp.zeros((LANES, KEY_W), F32).at[0:LOWRANK].set(wdf_full).astype(BF16)
    wd_pad_b = jnp.zeros((LANES, KEY_W), F32).at[LOWRANK : 2 * LOWRANK].set(wdb_full).astype(BF16)

    ws_bf = w_spatial[0].astype(BF16)
    wst_bf = jnp.transpose(w_spatial[0], (0, 2, 1)).astype(BF16)
    bs_col = b_spatial[0].reshape(GMLP_GROUPS, GMLP_CHUNK, 1)

    p = _inproj(xs, norm1_g, w_in_t, token_gather)
    o_f, st_f = _gla_fwd(p, wd_pad_f, b_decay_f, token_gather, reverse=False)
    o_b, st_b = _gla_fwd(p, wd_pad_b, b_decay_b, token_gather, reverse=True)
    late_bufs, _ = _split_wait("gather_wait", g_send, g_recv, late_bufs, [], _gather_ici_copies, (o_f, o_b))
    (w_out4,) = _gather_forward(late_bufs[:1])
    f_send, f_recv, ffn_bufs, _, token_forward = _split_start(
        "forward_start", late_bufs[1:], [], _gather_d2d_copies, 3 * (len(late) - 1), after=(w_out4,))
    w_out_full = w_out4.reshape(-1, D_MODEL)
    x1, ycat = _mixer_out(xs, o_f, o_b, p, gla_norm_g, gmlp_ln_g, gmlp_ln_b, ws_bf, bs_col, w_out_full, token_forward)
    ffn_bufs, _ = _split_wait("forward_wait", f_send, f_recv, ffn_bufs, [], _gather_d2d_copies, (x1,))
    wg_t, wu_t, wd = [b.reshape(-1, D_MODEL) for b in ffn_bufs]
    gf = final_norm_g.reshape(1, D_MODEL)
    h2, gate, up, act, dx2, loss_acc, dgf = _ffn_fwd(x1, target, norm2_g, gf, wg_t, wu_t, wd)

    dgate, dup, dx1, dg2 = _ffn_bwd(dx2, gate, up, x1, norm2_g, wg_t, wu_t, wd)
    ffn_grads4 = [g.reshape(N_SHARDS, FF_SHARD, D_MODEL) for g in _ffn_wgrad(h2, dgate, dup, act, dx2)]
    e_send, e_recv, e_srcs, e_lands, token_exchange = _split_start(
        "exchange_start", ffn_grads4, _exchange_lands(ffn_grads4), _exchange_copies, len(ffn_grads4))
    do, dg, du, dvv, dwo, dgn, dlng, dlnb, dws, dbs = _mixer_bwd(
        dx1, ycat, o_f, o_b, p, gla_norm_g, gmlp_ln_g, gmlp_ln_b, ws_bf, wst_bf, bs_col, w_out_full, token_exchange)
    ffn_mine, ffn_other = _split_wait("exchange_wait", e_send, e_recv, e_srcs, e_lands, _exchange_copies, (do,))
    ffn_parts = _add_halves(ffn_mine, ffn_other, sc_arr)
    ffn_payload = [pb for _, pb in ffn_parts]
    s_send, s_recv, s_parts, s_lands, token_scatter = _split_start(
        "scatter_start", ffn_payload, _scatter_lands(ffn_payload), _scatter_copies, 3 * len(ffn_payload))
    dq_f, dk_f, dv_f, dlr_f, dwdec_f, dbdec_f = _gla_bwd(p, do, st_f, wd_pad_f, b_decay_f, token_scatter, reverse=False)
    dq, dk, dv, dlr, dwdec_b, dbdec_b = _gla_bwd(
        p, do, st_b, wd_pad_b, b_decay_b, token_scatter, reverse=True, other=(dq_f, dk_f, dv_f, dlr_f))
    dwin_t, dp = _inproj_wgrad(xs, norm1_g, dq, dk, dv, dg, du, dvv, dlr)
    _, ffn_recv = _split_wait("scatter_wait", s_send, s_recv, s_parts, s_lands, _scatter_copies, (dwin_t,))

    dwin4 = dwin_t.reshape(N_SHARDS, PROJ_W // N_SHARDS, D_MODEL)
    dwo4 = dwo.reshape(N_SHARDS, D_MODEL // N_SHARDS, D_MODEL)
    proj_grads4 = [dwin4, dwo4]
    x_send, x_recv, x_srcs, x_lands, token_swap = _split_start(
        "proj_exchange_start", proj_grads4, _exchange_lands(proj_grads4), _exchange_copies, len(proj_grads4))
    ffn_bufs = [_add_partials(pf, r, sc_arr, token_swap) for (pf, _), r in zip(ffn_parts, ffn_recv)]
    proj_mine, proj_other = _split_wait("proj_exchange_wait", x_send, x_recv, x_srcs, x_lands, _exchange_copies, tuple(ffn_bufs))
    proj_parts = [_add_halves([g], [r], sc_arr)[0] for g, r in zip(proj_mine, proj_other)]
    proj_payload = [pb for _, pb in proj_parts]
    p_send, p_recv, p_parts, p_lands, token_proj = _split_start(
        "proj_scatter_start", proj_payload, _scatter_lands(proj_payload), _scatter_copies, 3 * len(proj_payload))
    j_send, j_recv, ffn_bufs, _, token_join = _split_start("join_start", ffn_bufs, [], _join_copies, len(ffn_bufs), after=(token_proj,))
    dx, dg1 = _inproj_dx(xs, dx1, norm1_g, w_in_t, dp, token_join)
    _, proj_recv = _split_wait("proj_scatter_wait", p_send, p_recv, p_parts, p_lands, _scatter_copies, (dx,))

    dwdec_f16 = dwdec_f[0:LOWRANK]
    dwdec_b16 = dwdec_b[LOWRANK : 2 * LOWRANK]
    shard_major = lambda a: jnp.transpose(a.reshape(LOWRANK, N_SHARDS, KEY_W // N_SHARDS), (1, 0, 2))
    small_grads = {
        "norm1_g": dg1, "b_decay_f": dbdec_f, "b_decay_b": dbdec_b, "gla_norm_g": dgn, "gmlp_ln_g": dlng, "gmlp_ln_b": dlnb,
        "w_spatial": dws, "b_spatial": dbs, "norm2_g": dg2, "final_norm_g": dgf,
    }
    g_pack = _pack_small([small_grads[k] for k in SMALL_NAMES] + [loss_acc], [shard_major(dwdec_f16), shard_major(dwdec_b16)])
    sg_send, sg_recv, (g_pack,), g_lands, token_small = _split_start(
        "small_gather_start", [g_pack], [jax.ShapeDtypeStruct((8, SMALL_TOTAL, LANES), F32)], _small_gather_copies, 7)

    ffn_bufs, _ = _split_wait("join_wait", j_send, j_recv, ffn_bufs, [], _join_copies, (dx, token_small))
    proj_bufs = [_add_partials(pf, r, sc_arr, token_small) for (pf, _), r in zip(proj_parts, proj_recv)]
    big_grads = dict(zip(big_names, list(_join_halves(proj_bufs, after=(token_small,))) + ffn_bufs))
    big_updates = {k: _adamw(big_shards[k], big_grads[k], rows_of("m_", k), rows_of("v_", k)) for k in big_names}

    (g_pack,), (g_all,) = _split_wait(
        "small_gather_wait", sg_send, sg_recv, [g_pack], g_lands, _small_gather_copies, tuple(u[1] for u in big_updates.values()))
    pack_own = lambda pre: _pack_small([args[pre + k] for k in SMALL_NAMES], [args[pre + "w_decay_f"], args[pre + "w_decay_b"]])
    small_updates, loss_row = _adamw_small(g_all, g_pack, pack_own(""), pack_own("m_"), pack_own("v_"), [args[k] for k in SMALL_NAMES])

    names = ["norm1_g", "w_in", "w_decay_f", "b_decay_f", "w_decay_b", "b_decay_b", "gla_norm_g", "gmlp_ln_g", "gmlp_ln_b",
             "w_spatial", "b_spatial", "w_out", "norm2_g", "w_gate", "w_up", "w_down", "final_norm_g"]
    results = {"g": {}, "d": {}, "m": {}, "v": {}}
    for tag, arrays in zip("gdmv", small_updates):
        for k, a in zip(SMALL_NAMES + ["w_decay_f", "w_decay_b"], arrays):
            results[tag][k] = a.reshape(args[k].shape)
    for k in big_names:
        for tag, a in zip("gdmv", big_updates[k]):
            results[tag][k] = (jnp.transpose(a) if k in transposed else a).reshape(args[k].shape)

    loss = loss_row[0, 0]
    grad_x = dx.reshape(x.shape)
    return (loss, grad_x, *[results["g"][k] for k in names], *[results["d"][k] for k in names],
            *[results["m"][k] for k in names], *[results["v"][k] for k in names])
```

```python
import functools
import math

import jax
import jax.numpy as jnp
from jax import lax
from jax.experimental import pallas as pl
from jax.experimental.pallas import tpu as pltpu

F32, BF16 = jnp.float32, jnp.bfloat16

D_MODEL = 1024
GLA_HEADS = 4
GLA_DK = 64
GLA_DV = 128
KEY_W = GLA_HEADS * GLA_DK
GLA_W = GLA_HEADS * GLA_DV
GMLP_W = 512
GMLP_GROUPS = 4
GMLP_CHUNK = 128
LOWRANK = 16
GLA_CHUNK = 64
GLA_TAU = 16.0
PROJ_W = 2592
PROJ_WP = 2688
D_FF = 2816
N_SHARDS = 4
FF_SHARD = D_FF // N_SHARDS
EPS = 1e-6
LANES = 128
TOKEN_SHAPE = (8, LANES)
MIB = 1024 * 1024

ADAM_LR = 0.001
ADAM_B1 = 0.9
ADAM_B2 = 0.999
ADAM_EPS = 1e-08
ADAM_WD = 0.01
ADAM_STEP = 10

COL_Q, COL_K = 0, 256
COL_V, COL_G, COL_U, COL_VV = 512, 1024, 1536, 2048
COL_LR = 2560
ROW_LR, ROW_UV = 1536, 1568
HALF = D_MODEL // 2

MESH = pl.DeviceIdType.MESH


def _nn(a, b):
    return jnp.dot(a, b, preferred_element_type=F32)


def _nt(a, b):
    return lax.dot_general(a, b, (((1,), (1,)), ((), ())), preferred_element_type=F32)


def _tn(a, b):
    return lax.dot_general(a, b, (((0,), (0,)), ((), ())), preferred_element_type=F32)


def _bnn(a, b):
    return jnp.einsum("nik,nkj->nij", a, b, preferred_element_type=F32)


def _bnt(a, b):
    return jnp.einsum("nik,njk->nij", a, b, preferred_element_type=F32)


def _btn(a, b):
    return jnp.einsum("nki,nkj->nij", a, b, preferred_element_type=F32)


def _resident(shape):
    zeros = (0,) * len(shape)
    return pl.BlockSpec(shape, lambda *_: zeros, pipeline_mode=pl.Buffered(1))


def _params(vmem_mib, semantics=("arbitrary",)):
    return pltpu.CompilerParams(vmem_limit_bytes=vmem_mib * MIB, dimension_semantics=semantics)


def _sigmoid(x):
    return 1.0 / (1.0 + jnp.exp(-x))


def _gelu(x):
    return 0.5 * x * (1.0 + lax.erf(x * (1.0 / math.sqrt(2.0))))


def _gelu_and_grad(x):
    cdf = 0.5 * (1.0 + lax.erf(x * (1.0 / math.sqrt(2.0))))
    return x * cdf, cdf + x * jnp.exp(-0.5 * x * x) * (1.0 / math.sqrt(2.0 * math.pi))


def _log_sigmoid(x):
    return jnp.minimum(x, 0.0) - jnp.log(1.0 + jnp.exp(-jnp.abs(x)))


def _rms_bwd(dxh, xh, r):
    return r * (dxh - xh * jnp.mean(dxh * xh, axis=-1, keepdims=True))


def _chunk_cumsum(v, row_in_chunk, reverse):
    rows = v.shape[0]
    for sh in (1, 2, 4, 8, 16, 32):
        if reverse:
            v = v + jnp.where(row_in_chunk + sh < GLA_CHUNK, pltpu.roll(v, rows - sh, axis=0), 0.0)
        else:
            v = v + jnp.where(row_in_chunk >= sh, pltpu.roll(v, sh, axis=0), 0.0)
    return v


def _inproj(x, g1, w_in_t, token):
    seq = x.shape[0]
    tm = min(seq, 512)

    def body(x_ref, g_ref, w_ref, token_ref, p_ref):
        xv = x_ref[...]
        r = lax.rsqrt(jnp.mean(xv * xv, axis=-1, keepdims=True) + EPS)
        h = (xv * r * g_ref[...]).astype(BF16)
        p_ref[:, 0:COL_U] = _nt(h, w_ref[0:ROW_LR, :])
        p_ref[:, COL_U:COL_LR] = _nt(h, w_ref[ROW_UV:PROJ_W, :])
        p_ref[:, COL_LR:PROJ_WP] = _nt(h, w_ref[ROW_LR : ROW_LR + LANES, :])

    return pl.pallas_call(
        body,
        name="inproj",
        grid=(seq // tm,),
        in_specs=[pl.BlockSpec((tm, D_MODEL), lambda i: (i, 0)), _resident((1, D_MODEL)), _resident((PROJ_W, D_MODEL)), _resident(TOKEN_SHAPE)],
        out_specs=pl.BlockSpec((tm, PROJ_WP), lambda i: (i, 0)),
        out_shape=jax.ShapeDtypeStruct((seq, PROJ_WP), F32),
        compiler_params=_params(48, ("parallel",)),
    )(x, g1, w_in_t, token)


def _gla_tile(seq):
    return min(seq, 1024)


def _gla_decay_terms(lr_bf, wd_ref, bd_ref, pair, row_in_chunk, reverse, n):
    cols = pl.ds(pair * LANES, LANES)
    pre = _nn(lr_bf, wd_ref[:, cols]) + bd_ref[:, cols]
    la = _log_sigmoid(pre) * (1.0 / GLA_TAU)
    b = _chunk_cumsum(la, row_in_chunk, reverse)
    b3 = b.reshape(n, GLA_CHUNK, LANES)
    blast = b3[:, 0:1, :] if reverse else b3[:, GLA_CHUNK - 1 : GLA_CHUNK, :]
    return pre, b3, blast


def _gla_fwd(p, wd_pad, bd, token, reverse):
    seq = p.shape[0]
    tg = _gla_tile(seq)
    nt = seq // tg
    n = tg // GLA_CHUNK
    scale = GLA_DK**-0.5

    def tile(i):
        return nt - 1 - i if reverse else i

    def body(q_ref, k_ref, v_ref, lr_ref, wd_ref, bd_ref, token_ref, o_ref, st_ref, carry):
        @pl.when(pl.program_id(0) == 0)
        def _():
            carry[...] = jnp.zeros_like(carry)

        lr_bf = lr_ref[...].astype(BF16)
        states = [carry[h] for h in range(GLA_HEADS)]
        row_in_chunk = lax.broadcasted_iota(jnp.int32, (tg, LANES), 0) % GLA_CHUNK
        lane_head = lax.broadcasted_iota(jnp.int32, (1, LANES), 1) // GLA_DK
        tt = lax.broadcasted_iota(jnp.int32, (GLA_CHUNK, GLA_CHUNK), 0)
        ss = lax.broadcasted_iota(jnp.int32, (GLA_CHUNK, GLA_CHUNK), 1)
        causal = (tt <= ss) if reverse else (tt >= ss)
        order = range(n - 1, -1, -1) if reverse else range(n)
        heads = range(GLA_HEADS)
        qds, vhs, decs, sc_raw, dst = {}, {}, {}, {}, {}
        for pair in range(2):
            cols = pl.ds(pair * LANES, LANES)
            _, b3, blast = _gla_decay_terms(lr_bf, wd_ref, bd_ref, pair, row_in_chunk, reverse, n)
            q3 = q_ref[:, cols].reshape(n, GLA_CHUNK, LANES) * scale
            k3 = k_ref[:, cols].reshape(n, GLA_CHUNK, LANES)
            qd = q3 * jnp.exp(b3)
            kd = (k3 * jnp.exp(-b3)).astype(BF16)
            kte = k3 * jnp.exp(blast - b3)
            decs[pair] = jnp.exp(blast)
            qds[pair] = qd.astype(BF16)
            m0 = (lane_head == 0).astype(F32)
            m1 = (lane_head == 1).astype(F32)
            q_both = jnp.concatenate([(qd * m0).astype(BF16), (qd * m1).astype(BF16)], axis=1)
            sc_both = _bnt(q_both, kd)
            for hh, m in ((0, m0), (1, m1)):
                h = 2 * pair + hh
                vhs[h] = v_ref[:, pl.ds(h * GLA_DV, GLA_DV)].reshape(n, GLA_CHUNK, GLA_DV).astype(BF16)
                sc_raw[h] = sc_both[:, hh * GLA_CHUNK : (hh + 1) * GLA_CHUNK, :]
                dst[h] = _btn(vhs[h], (kte * m).astype(BF16))
        o_intra, befores = {}, {}
        for h in heads:
            o_intra[h] = _bnn(jnp.where(causal, sc_raw[h], 0.0).astype(BF16), vhs[h])
            st, before = states[h], [None] * n
            for j in order:
                before[j] = st
                st = st * decs[h // 2][j] + dst[h][j]
            states[h] = st
            befores[h] = jnp.stack(before).astype(BF16)
        outs = {}
        for pair in range(2):
            both = jnp.concatenate([befores[2 * pair], befores[2 * pair + 1]], axis=1)
            o_inter = _bnt(qds[pair], both)
            for hh in range(2):
                h = 2 * pair + hh
                outs[h] = (o_intra[h] + o_inter[:, :, hh * GLA_DV : (hh + 1) * GLA_DV]).reshape(tg, GLA_DV)
        for h in range(GLA_HEADS):
            o_ref[:, pl.ds(h * GLA_DV, GLA_DV)] = outs[h]
            st_ref[:, h] = befores[h]
            carry[h] = states[h]

    nchunks = seq // GLA_CHUNK
    return pl.pallas_call(
        body,
        name="gla_fwd_rev" if reverse else "gla_fwd",
        grid=(nt,),
        in_specs=[
            pl.BlockSpec((tg, KEY_W), lambda i: (tile(i), COL_Q // KEY_W)),
            pl.BlockSpec((tg, KEY_W), lambda i: (tile(i), COL_K // KEY_W)),
            pl.BlockSpec((tg, GLA_W), lambda i: (tile(i), COL_V // GLA_W)),
            pl.BlockSpec((tg, LANES), lambda i: (tile(i), COL_LR // LANES)),
            _resident((LANES, KEY_W)),
            _resident((1, KEY_W)),
            _resident(TOKEN_SHAPE),
        ],
        out_specs=[
            pl.BlockSpec((tg, GLA_W), lambda i: (tile(i), 0)),
            pl.BlockSpec((n, GLA_HEADS, GLA_DV, LANES), lambda i: (tile(i), 0, 0, 0)),
        ],
        out_shape=[
            jax.ShapeDtypeStruct((seq, GLA_W), F32),
            jax.ShapeDtypeStruct((nchunks, GLA_HEADS, GLA_DV, LANES), BF16),
        ],
        scratch_shapes=[pltpu.VMEM((GLA_HEADS, GLA_DV, LANES), F32)],
        compiler_params=_params(48),
    )(p, p, p, p, wd_pad, bd, token)


def _mixer_out(x, o_f, o_b, p, gn, lng, lnb, ws_bf, bs_col, w_out, token):
    seq = x.shape[0]
    tm = min(seq, 512)

    def body(x_ref, of_ref, ob_ref, g_ref, u_ref, vv_ref, gn_ref, lng_ref, lnb_ref, ws_ref, bs_ref, wo_ref, token_ref, x1_ref, yc_ref, vn_sc):
        for h in range(GLA_HEADS):
            cols = pl.ds(h * GLA_DV, GLA_DV)
            oh = of_ref[:, cols] + ob_ref[:, cols]
            on = oh * lax.rsqrt(jnp.mean(oh * oh, axis=-1, keepdims=True) + EPS)
            gh = g_ref[:, cols]
            yc_ref[:, cols] = (on * gn_ref[:, cols] * (gh * _sigmoid(gh))).astype(BF16)
        zv = _gelu(vv_ref[...])
        xc = zv - jnp.mean(zv, axis=-1, keepdims=True)
        vhat = xc * lax.rsqrt(jnp.mean(xc * xc, axis=-1, keepdims=True) + EPS)
        vn_sc[...] = (vhat * lng_ref[...] + lnb_ref[...]).astype(BF16)
        for c in range(tm // GMLP_CHUNK):
            rows = pl.ds(c * GMLP_CHUNK, GMLP_CHUNK)
            for g in range(GMLP_GROUPS):
                cols = pl.ds(g * LANES, LANES)
                s = _nn(ws_ref[g], vn_sc[rows, cols]) + bs_ref[g]
                yc_ref[rows, pl.ds(GLA_W + g * LANES, LANES)] = (_gelu(u_ref[rows, cols]) * s).astype(BF16)
        x1_ref[...] = x_ref[...] + _nn(yc_ref[...], wo_ref[...])

    row = lambda w: pl.BlockSpec((tm, w), lambda i: (i, 0))
    pcol = lambda col: pl.BlockSpec((tm, GLA_W), lambda i: (i, col // GLA_W))
    return pl.pallas_call(
        body,
        name="mixer_out",
        grid=(seq // tm,),
        in_specs=[
            row(D_MODEL), row(GLA_W), row(GLA_W), pcol(COL_G), pcol(COL_U), pcol(COL_VV),
            _resident((1, GLA_W)), _resident((1, GMLP_W)), _resident((1, GMLP_W)),
            _resident((GMLP_GROUPS, GMLP_CHUNK, GMLP_CHUNK)), _resident((GMLP_GROUPS, GMLP_CHUNK, 1)),
            _resident((D_MODEL, D_MODEL)), _resident(TOKEN_SHAPE),
        ],
        out_specs=[row(D_MODEL), row(D_MODEL)],
        out_shape=[jax.ShapeDtypeStruct((seq, D_MODEL), F32), jax.ShapeDtypeStruct((seq, D_MODEL), BF16)],
        scratch_shapes=[pltpu.VMEM((tm, GMLP_W), BF16)],
        compiler_params=_params(48, ("parallel",)),
    )(x, o_f, o_b, p, p, p, gn, lng, lnb, ws_bf, bs_col, w_out, token)


def _ffn_fwd(x1, target, g2, gf, wg_t, wu_t, wd):
    seq = x1.shape[0]
    tm = min(seq, 256)

    def body(x1_ref, t_ref, g2_ref, gf_ref, wg_ref, wu_ref, wd_ref, h2_ref, gate_ref, up_ref, act_ref, dx2_ref, loss_ref, dgf_ref):
        @pl.when(pl.program_id(0) == 0)
        def _():
            loss_ref[...] = jnp.zeros_like(loss_ref)
            dgf_ref[...] = jnp.zeros_like(dgf_ref)

        x1v = x1_ref[...]
        h2 = (x1v * lax.rsqrt(jnp.mean(x1v * x1v, axis=-1, keepdims=True) + EPS) * g2_ref[...]).astype(BF16)
        h2_ref[...] = h2
        gate = _nt(h2, wg_ref[...])
        up = _nt(h2, wu_ref[...])
        act = (gate * _sigmoid(gate) * up).astype(BF16)
        gate_ref[...] = gate
        up_ref[...] = up
        act_ref[...] = act
        x2 = x1v + _nn(act, wd_ref[...])
        rf = lax.rsqrt(jnp.mean(x2 * x2, axis=-1, keepdims=True) + EPS)
        xh = x2 * rf
        err = xh * gf_ref[...] - t_ref[...]
        loss_ref[...] += 0.5 * jnp.sum(jnp.mean(err * err, axis=-1, keepdims=True))
        dy = err * (1.0 / D_MODEL)
        dgf_ref[...] += jnp.sum(dy * xh, axis=0, keepdims=True)
        dx2_ref[...] = _rms_bwd(dy * gf_ref[...], xh, rf)

    row = lambda w: pl.BlockSpec((tm, w), lambda i: (i, 0))
    weight = _resident((D_FF, D_MODEL))
    return pl.pallas_call(
        body,
        name="ffn_fwd",
        grid=(seq // tm,),
        in_specs=[row(D_MODEL), row(D_MODEL), _resident((1, D_MODEL)), _resident((1, D_MODEL)), weight, weight, weight],
        out_specs=[row(D_MODEL), row(D_FF), row(D_FF), row(D_FF), row(D_MODEL),
                   pl.BlockSpec((1, LANES), lambda i: (0, 0)), pl.BlockSpec((1, D_MODEL), lambda i: (0, 0))],
        out_shape=[
            jax.ShapeDtypeStruct((seq, D_MODEL), BF16),
            jax.ShapeDtypeStruct((seq, D_FF), F32),
            jax.ShapeDtypeStruct((seq, D_FF), F32),
            jax.ShapeDtypeStruct((seq, D_FF), BF16),
            jax.ShapeDtypeStruct((seq, D_MODEL), F32),
            jax.ShapeDtypeStruct((1, LANES), F32),
            jax.ShapeDtypeStruct((1, D_MODEL), F32),
        ],
        compiler_params=_params(56),
    )(x1, target, g2, gf, wg_t, wu_t, wd)


def _ffn_bwd(dx2, gate, up, x1, g2, wg_t, wu_t, wd):
    seq = x1.shape[0]
    tm = min(seq, 256)

    def body(dx2_ref, gate_ref, up_ref, x1_ref, g2_ref, wg_ref, wu_ref, wd_ref, dgate_ref, dup_ref, dx1_ref, dg2_ref):
        @pl.when(pl.program_id(0) == 0)
        def _():
            dg2_ref[...] = jnp.zeros_like(dg2_ref)

        dx2v = dx2_ref[...]
        dact = _nt(dx2v.astype(BF16), wd_ref[...])
        gate = gate_ref[...]
        sg = _sigmoid(gate)
        dgate = (dact * up_ref[...] * (sg * (1.0 + gate * (1.0 - sg)))).astype(BF16)
        dup = (dact * (gate * sg)).astype(BF16)
        dgate_ref[...] = dgate
        dup_ref[...] = dup
        dh2 = _nn(dgate, wg_ref[...]) + _nn(dup, wu_ref[...])
        x1v = x1_ref[...]
        r2 = lax.rsqrt(jnp.mean(x1v * x1v, axis=-1, keepdims=True) + EPS)
        xh = x1v * r2
        dg2_ref[...] += jnp.sum(dh2 * xh, axis=0, keepdims=True)
        dx1_ref[...] = dx2v + _rms_bwd(dh2 * g2_ref[...], xh, r2)

    row = lambda w: pl.BlockSpec((tm, w), lambda i: (i, 0))
    weight = _resident((D_FF, D_MODEL))
    return pl.pallas_call(
        body,
        name="ffn_bwd",
        grid=(seq // tm,),
        in_specs=[row(D_MODEL), row(D_FF), row(D_FF), row(D_MODEL), _resident((1, D_MODEL)), weight, weight, weight],
        out_specs=[row(D_FF), row(D_FF), row(D_MODEL), pl.BlockSpec((1, D_MODEL), lambda i: (0, 0))],
        out_shape=[
            jax.ShapeDtypeStruct((seq, D_FF), BF16),
            jax.ShapeDtypeStruct((seq, D_FF), BF16),
            jax.ShapeDtypeStruct((seq, D_MODEL), F32),
            jax.ShapeDtypeStruct((1, D_MODEL), F32),
        ],
        compiler_params=_params(56),
    )(dx2, gate, up, x1, g2, wg_t, wu_t, wd)


WGRAD_ROWS = D_FF // 2


def _ffn_wgrad(h2, dgate, dup, act, dx2):
    seq = h2.shape[0]
    tm = min(seq, 512)

    def body(h2_ref, dgate_ref, dup_ref, act_ref, dx2_ref, dwg_ref, dwu_ref, dwd_ref):
        @pl.when(pl.program_id(1) == 0)
        def _():
            dwg_ref[...] = jnp.zeros_like(dwg_ref)
            dwu_ref[...] = jnp.zeros_like(dwu_ref)
            dwd_ref[...] = jnp.zeros_like(dwd_ref)

        h2v = h2_ref[...]
        dwg_ref[...] += _tn(dgate_ref[...], h2v)
        dwu_ref[...] += _tn(dup_ref[...], h2v)
        dwd_ref[...] += _tn(act_ref[...], dx2_ref[...].astype(BF16))

    ff = pl.BlockSpec((tm, WGRAD_ROWS), lambda j, i: (i, j))
    row = pl.BlockSpec((tm, D_MODEL), lambda j, i: (i, 0))
    out = pl.BlockSpec((WGRAD_ROWS, D_MODEL), lambda j, i: (j, 0))
    return pl.pallas_call(
        body,
        name="ffn_wgrad",
        grid=(D_FF // WGRAD_ROWS, seq // tm),
        in_specs=[row, ff, ff, ff, row],
        out_specs=[out, out, out],
        out_shape=[jax.ShapeDtypeStruct((D_FF, D_MODEL), F32)] * 3,
        compiler_params=_params(56, ("parallel", "arbitrary")),
    )(h2, dgate, dup, act, dx2)


def _mixer_bwd(dx1, ycat, o_f, o_b, p, gn, lng, lnb, ws_bf, wst_bf, bs_col, w_out, token):
    seq = dx1.shape[0]
    tm = min(seq, 512)
    nsteps = seq // tm

    def body(dx1_ref, yc_ref, of_ref, ob_ref, g_ref, u_ref, vv_ref, gn_ref, lng_ref, lnb_ref, ws_ref, wst_ref, bs_ref, wo_ref, token_ref,
             do_ref, dg_ref, du_ref, dvv_ref, dwo_ref, dgn_ref, dlng_ref, dlnb_ref, dws_ref, dbs_ref, vn_sc, dvn_sc, dbs_acc):
        step = pl.program_id(0)

        @pl.when(step == 0)
        def _():
            for r in (dwo_ref, dgn_ref, dlng_ref, dlnb_ref, dws_ref, dbs_acc):
                r[...] = jnp.zeros_like(r)

        dx1b = dx1_ref[...].astype(BF16)
        dyc = _nt(dx1b, wo_ref[...])
        dwo_ref[...] += _tn(yc_ref[...], dx1b)
        for h in range(GLA_HEADS):
            cols = pl.ds(h * GLA_DV, GLA_DV)
            dya = dyc[:, h * GLA_DV : (h + 1) * GLA_DV]
            oh = of_ref[:, cols] + ob_ref[:, cols]
            rn = lax.rsqrt(jnp.mean(oh * oh, axis=-1, keepdims=True) + EPS)
            on = oh * rn
            gh = g_ref[:, cols]
            sg = _sigmoid(gh)
            sil = gh * sg
            gnh = gn_ref[:, cols]
            dgn_ref[:, cols] += jnp.sum(dya * on * sil, axis=0, keepdims=True)
            dg_ref[:, cols] = (dya * on * gnh * (sg * (1.0 + gh * (1.0 - sg)))).astype(BF16)
            do_ref[:, cols] = _rms_bwd(dya * gnh * sil, on, rn)
        vv = vv_ref[...]
        zv, zv_grad = _gelu_and_grad(vv)
        xc = zv - jnp.mean(zv, axis=-1, keepdims=True)
        rstd = lax.rsqrt(jnp.mean(xc * xc, axis=-1, keepdims=True) + EPS)
        vhat = xc * rstd
        vn_sc[...] = (vhat * lng_ref[...] + lnb_ref[...]).astype(BF16)
        for c in range(tm // GMLP_CHUNK):
            rows = pl.ds(c * GMLP_CHUNK, GMLP_CHUNK)
            for g in range(GMLP_GROUPS):
                cols = pl.ds(g * LANES, LANES)
                vn = vn_sc[rows, cols]
                s = _nn(ws_ref[g], vn) + bs_ref[g]
                dyb = dyc[c * GMLP_CHUNK : (c + 1) * GMLP_CHUNK, GLA_W + g * LANES : GLA_W + (g + 1) * LANES]
                zu, zu_grad = _gelu_and_grad(u_ref[rows, cols])
                du_ref[rows, cols] = (dyb * s * zu_grad).astype(BF16)
                ds = dyb * zu
                dbs_acc[g] += ds
                dsb = ds.astype(BF16)
                dws_ref[g] += _nt(dsb, vn)
                dvn_sc[rows, cols] = _nn(wst_ref[g], dsb)
        dvn = dvn_sc[...]
        dlng_ref[...] += jnp.sum(dvn * vhat, axis=0, keepdims=True)
        dlnb_ref[...] += jnp.sum(dvn, axis=0, keepdims=True)
        dvh = dvn * lng_ref[...]
        dzv = rstd * (dvh - jnp.mean(dvh, axis=-1, keepdims=True) - vhat * jnp.mean(dvh * vhat, axis=-1, keepdims=True))
        dvv_ref[...] = (dzv * zv_grad).astype(BF16)

        @pl.when(step == nsteps - 1)
        def _():
            dbs_ref[...] = jnp.sum(dbs_acc[...], axis=-1, keepdims=True)

    row = lambda w: pl.BlockSpec((tm, w), lambda i: (i, 0))
    pcol = lambda col: pl.BlockSpec((tm, GLA_W), lambda i: (i, col // GLA_W))
    const = lambda shape: pl.BlockSpec(shape, lambda i: (0,) * len(shape))
    return pl.pallas_call(
        body,
        name="mixer_bwd",
        grid=(nsteps,),
        in_specs=[
            row(D_MODEL), row(D_MODEL), row(GLA_W), row(GLA_W), pcol(COL_G), pcol(COL_U), pcol(COL_VV),
            _resident((1, GLA_W)), _resident((1, GMLP_W)), _resident((1, GMLP_W)),
            _resident((GMLP_GROUPS, GMLP_CHUNK, GMLP_CHUNK)), _resident((GMLP_GROUPS, GMLP_CHUNK, GMLP_CHUNK)),
            _resident((GMLP_GROUPS, GMLP_CHUNK, 1)), _resident((D_MODEL, D_MODEL)), _resident(TOKEN_SHAPE),
        ],
        out_specs=[
            row(GLA_W), row(GLA_W), row(GMLP_W), row(GMLP_W), const((D_MODEL, D_MODEL)),
            const((1, GLA_W)), const((1, GMLP_W)), const((1, GMLP_W)),
            const((GMLP_GROUPS, GMLP_CHUNK, GMLP_CHUNK)), const((GMLP_GROUPS, GMLP_CHUNK, 1)),
        ],
        out_shape=[
            jax.ShapeDtypeStruct((seq, GLA_W), F32), jax.ShapeDtypeStruct((seq, GLA_W), BF16),
            jax.ShapeDtypeStruct((seq, GMLP_W), BF16), jax.ShapeDtypeStruct((seq, GMLP_W), BF16),
            jax.ShapeDtypeStruct((D_MODEL, D_MODEL), F32),
            jax.ShapeDtypeStruct((1, GLA_W), F32), jax.ShapeDtypeStruct((1, GMLP_W), F32), jax.ShapeDtypeStruct((1, GMLP_W), F32),
            jax.ShapeDtypeStruct((GMLP_GROUPS, GMLP_CHUNK, GMLP_CHUNK), F32), jax.ShapeDtypeStruct((GMLP_GROUPS, GMLP_CHUNK, 1), F32),
        ],
        scratch_shapes=[pltpu.VMEM((tm, GMLP_W), BF16), pltpu.VMEM((tm, GMLP_W), F32), pltpu.VMEM((GMLP_GROUPS, GMLP_CHUNK, GMLP_CHUNK), F32)],
        compiler_params=_params(56),
    )(dx1, ycat, o_f, o_b, p, p, p, gn, lng, lnb, ws_bf, wst_bf, bs_col, w_out, token)


def _gla_bwd(p, do, st, wd_pad, bd, token, reverse, other=None):
    seq = p.shape[0]
    tg = _gla_tile(seq)
    nt = seq // tg
    n = tg // GLA_CHUNK
    scale = GLA_DK**-0.5

    def tile(i):
        return i if reverse else nt - 1 - i

    def body(q_ref, k_ref, v_ref, lr_ref, do_ref, st_ref, wd_ref, bd_ref, token_ref, *rest):
        others, (dq_ref, dk_ref, dv_ref, dlr_ref, dwd_ref, dbd_ref, carry) = rest[:-7], rest[-7:]
        if others:
            odq_ref, odk_ref, odv_ref, odlr_ref = others

            def put(ref, idx, val, oref):
                ref[idx] = (val + oref[idx]).astype(BF16)
        else:
            odq_ref = odk_ref = odv_ref = odlr_ref = None

            def put(ref, idx, val, oref):
                ref[idx] = val

        @pl.when(pl.program_id(0) == 0)
        def _():
            carry[...] = jnp.zeros_like(carry)
            dwd_ref[...] = jnp.zeros_like(dwd_ref)
            dbd_ref[...] = jnp.zeros_like(dbd_ref)

        lr_bf = lr_ref[...].astype(BF16)
        carries = [carry[h] for h in range(GLA_HEADS)]
        row_in_chunk = lax.broadcasted_iota(jnp.int32, (tg, LANES), 0) % GLA_CHUNK
        lane_head = lax.broadcasted_iota(jnp.int32, (1, LANES), 1) // GLA_DK
        tt = lax.broadcasted_iota(jnp.int32, (GLA_CHUNK, GLA_CHUNK), 0)
        ss = lax.broadcasted_iota(jnp.int32, (GLA_CHUNK, GLA_CHUNK), 1)
        causal = (tt <= ss) if reverse else (tt >= ss)
        order = range(n) if reverse else range(n - 1, -1, -1)
        dlr = jnp.zeros((tg, LANES), F32)
        heads = range(GLA_HEADS)
        pv, masks, qdh, vhs, dohs, stbs = {}, {}, {}, {}, {}, {}
        sc_raw, dp, acc = {}, {}, {}
        for pair in range(2):
            cols = pl.ds(pair * LANES, LANES)
            pre, b3, blast = _gla_decay_terms(lr_bf, wd_ref, bd_ref, pair, row_in_chunk, reverse, n)
            q3 = q_ref[:, cols].reshape(n, GLA_CHUNK, LANES) * scale
            k3 = k_ref[:, cols].reshape(n, GLA_CHUNK, LANES)
            eb = jnp.exp(b3)
            emb = jnp.exp(-b3)
            ekte = jnp.exp(blast - b3)
            kdf = k3 * emb
            kte = k3 * ekte
            both = pl.ds(2 * pair * GLA_DV, 2 * GLA_DV)
            pv[pair] = dict(pre=pre, eb=eb, emb=emb, ekte=ekte, qd=q3 * eb, kdf=kdf, kd=kdf.astype(BF16), kte=kte, kte_bf=kte.astype(BF16),
                            dec=jnp.exp(blast), v=v_ref[:, both].reshape(n, GLA_CHUNK, 2 * GLA_DV).astype(BF16),
                            do=do_ref[:, both].reshape(n, GLA_CHUNK, 2 * GLA_DV).astype(BF16))
            for hh in range(2):
                h = 2 * pair + hh
                masks[h] = (lane_head == hh).astype(F32)
                qdh[h] = (pv[pair]["qd"] * masks[h]).astype(BF16)
                vhs[h] = pv[pair]["v"][:, :, hh * GLA_DV : (hh + 1) * GLA_DV]
                dohs[h] = pv[pair]["do"][:, :, hh * GLA_DV : (hh + 1) * GLA_DV]
                stbs[h] = st_ref[:, h]
                dp[h] = _bnt(dohs[h], vhs[h])
                acc[h] = _btn(dohs[h], qdh[h])
            sc_both = _bnt(jnp.concatenate([qdh[2 * pair], qdh[2 * pair + 1]], axis=1), pv[pair]["kd"])
            for hh in range(2):
                sc_raw[2 * pair + hh] = sc_both[:, hh * GLA_CHUNK : (hh + 1) * GLA_CHUNK, :]
        dsa, sc = {}, {}
        for h in heads:
            sc[h] = jnp.where(causal, sc_raw[h], 0.0).astype(BF16)
            dp[h] = jnp.where(causal, dp[h], 0.0).astype(BF16)
            dec = pv[h // 2]["dec"]
            c, after = carries[h], [None] * n
            for j in order:
                after[j] = c
                c = acc[h][j] + dec[j] * c
            carries[h] = c
            dsa[h] = jnp.stack(after)
        dvs, dqs, dks, dwds, dbds = [], [], [], [], []
        for pair in range(2):
            cols = pl.ds(pair * LANES, LANES)
            v = pv[pair]
            h0, h1 = 2 * pair, 2 * pair + 1
            dsa_both = jnp.concatenate([dsa[h0], dsa[h1]], axis=1)
            dsa_bf = dsa_both.astype(BF16)
            stb_bf = jnp.concatenate([stbs[h0], stbs[h1]], axis=1)
            dq_intra = _bnn(jnp.concatenate([dp[h0], dp[h1]], axis=1), v["kd"])
            dqd = (dq_intra[:, :GLA_CHUNK, :] * masks[h0] + dq_intra[:, GLA_CHUNK:, :] * masks[h1]) + _bnn(v["do"], stb_bf)
            dkd = _btn(dp[h0], qdh[h0]) + _btn(dp[h1], qdh[h1])
            dkte = _bnn(v["v"], dsa_bf)
            ddec = jnp.sum(dsa[h0] * stbs[h0].astype(F32) + dsa[h1] * stbs[h1].astype(F32), axis=1, keepdims=True)
            dv_inter = _bnt(v["kte_bf"], dsa_bf)
            for hh, h in ((0, h0), (1, h1)):
                dvs.append((_btn(sc[h], dohs[h]) + dv_inter[:, :, hh * GLA_DV : (hh + 1) * GLA_DV]).reshape(tg, GLA_DV))
            dqs.append((dqd * (scale * v["eb"])).reshape(tg, LANES))
            dks.append((dkd * v["emb"] + dkte * v["ekte"]).reshape(tg, LANES))
            db = dqd * v["qd"] - dkd * v["kdf"] - dkte * v["kte"]
            dblast = jnp.sum(dkte * v["kte"], axis=1, keepdims=True) + ddec * v["dec"]
            dla = _chunk_cumsum(db.reshape(tg, LANES), row_in_chunk, not reverse) + jnp.broadcast_to(dblast, (n, GLA_CHUNK, LANES)).reshape(tg, LANES)
            dpre = (dla * (1.0 / GLA_TAU) * _sigmoid(-v["pre"]))
            dpre_bf = dpre.astype(BF16)
            dlr = dlr + _nt(dpre_bf, wd_ref[:, cols])
            dwds.append(_tn(lr_bf, dpre_bf))
            dbds.append(jnp.sum(dpre, axis=0, keepdims=True))
        put(dlr_ref, (slice(None), slice(None)), dlr, odlr_ref)
        for pair in range(2):
            cols = pl.ds(pair * LANES, LANES)
            put(dq_ref, (slice(None), cols), dqs[pair], odq_ref)
            put(dk_ref, (slice(None), cols), dks[pair], odk_ref)
            dwd_ref[:, cols] += dwds[pair]
            dbd_ref[:, cols] += dbds[pair]
        for h in range(GLA_HEADS):
            put(dv_ref, (slice(None), pl.ds(h * GLA_DV, GLA_DV)), dvs[h], odv_ref)
            carry[h] = carries[h]

    pieces = [
        pl.BlockSpec((tg, KEY_W), lambda i: (tile(i), 0)),
        pl.BlockSpec((tg, KEY_W), lambda i: (tile(i), 0)),
        pl.BlockSpec((tg, GLA_W), lambda i: (tile(i), 0)),
        pl.BlockSpec((tg, LANES), lambda i: (tile(i), 0)),
    ]
    piece_dtype = BF16 if other else F32
    return pl.pallas_call(
        body,
        name="gla_bwd_rev" if reverse else "gla_bwd",
        grid=(nt,),
        in_specs=[
            pl.BlockSpec((tg, KEY_W), lambda i: (tile(i), COL_Q // KEY_W)),
            pl.BlockSpec((tg, KEY_W), lambda i: (tile(i), COL_K // KEY_W)),
            pl.BlockSpec((tg, GLA_W), lambda i: (tile(i), COL_V // GLA_W)),
            pl.BlockSpec((tg, LANES), lambda i: (tile(i), COL_LR // LANES)),
            pl.BlockSpec((tg, GLA_W), lambda i: (tile(i), 0)),
            pl.BlockSpec((n, GLA_HEADS, GLA_DV, LANES), lambda i: (tile(i), 0, 0, 0)),
            _resident((LANES, KEY_W)),
            _resident((1, KEY_W)),
            _resident(TOKEN_SHAPE),
        ] + (pieces if other else []),
        out_specs=pieces + [pl.BlockSpec((LANES, KEY_W), lambda i: (0, 0)), pl.BlockSpec((1, KEY_W), lambda i: (0, 0))],
        out_shape=[
            jax.ShapeDtypeStruct((seq, KEY_W), piece_dtype), jax.ShapeDtypeStruct((seq, KEY_W), piece_dtype),
            jax.ShapeDtypeStruct((seq, GLA_W), piece_dtype), jax.ShapeDtypeStruct((seq, LANES), piece_dtype),
            jax.ShapeDtypeStruct((LANES, KEY_W), F32), jax.ShapeDtypeStruct((1, KEY_W), F32),
        ],
        scratch_shapes=[pltpu.VMEM((GLA_HEADS, GLA_DV, LANES), F32)],
        compiler_params=_params(56),
    )(p, p, p, p, do, st, wd_pad, bd, token, *(other or ()))


def _inproj_wgrad(x, g1, dq, dk, dv, dg, du, dvv, dlr):
    seq = x.shape[0]
    tm = min(seq, 512)

    def body(x_ref, g1_ref, dq_ref, dk_ref, dv_ref, dg_ref, du_ref, dvv_ref, dlr_ref, dw_ref, dp_ref):
        @pl.when(pl.program_id(0) == 0)
        def _():
            dw_ref[...] = jnp.zeros_like(dw_ref)

        for col, ref in ((COL_Q, dq_ref), (COL_K, dk_ref), (COL_V, dv_ref), (COL_G, dg_ref), (COL_U, du_ref), (COL_VV, dvv_ref), (COL_LR, dlr_ref)):
            dp_ref[:, col : col + ref.shape[1]] = ref[...]
        xv = x_ref[...]
        h = (xv * lax.rsqrt(jnp.mean(xv * xv, axis=-1, keepdims=True) + EPS) * g1_ref[...]).astype(BF16)
        dw_ref[0:ROW_LR, :] += _tn(dp_ref[:, 0:COL_U], h)
        dw_ref[ROW_UV:PROJ_W, :] += _tn(dp_ref[:, COL_U:COL_LR], h)
        dw_ref[ROW_LR:ROW_UV, :] += _tn(dp_ref[:, COL_LR:PROJ_WP], h)[0 : ROW_UV - ROW_LR]

    row = lambda w: pl.BlockSpec((tm, w), lambda i: (i, 0))
    return pl.pallas_call(
        body,
        name="inproj_wgrad",
        grid=(seq // tm,),
        in_specs=[row(D_MODEL), _resident((1, D_MODEL)), row(KEY_W), row(KEY_W), row(GLA_W), row(GLA_W), row(GMLP_W), row(GMLP_W), row(LANES)],
        out_specs=[pl.BlockSpec((PROJ_W, D_MODEL), lambda i: (0, 0)), row(PROJ_WP)],
        out_shape=[jax.ShapeDtypeStruct((PROJ_W, D_MODEL), F32), jax.ShapeDtypeStruct((seq, PROJ_WP), BF16)],
        compiler_params=_params(56),
    )(x, g1, dq, dk, dv, dg, du, dvv, dlr)


def _inproj_dx(x, dx1, g1, w_in_t, dp, token):
    seq = x.shape[0]
    tm = min(seq, 512)

    def body(x_ref, dx1_ref, g1_ref, w_ref, dp_ref, token_ref, dx_ref, dg1_ref):
        @pl.when(pl.program_id(0) == 0)
        def _():
            dg1_ref[...] = jnp.zeros_like(dg1_ref)

        xv = x_ref[...]
        r1 = lax.rsqrt(jnp.mean(xv * xv, axis=-1, keepdims=True) + EPS)
        xh = xv * r1
        dh = (_nn(dp_ref[:, 0:COL_U], w_ref[0:ROW_LR, :]) + _nn(dp_ref[:, COL_U:COL_LR], w_ref[ROW_UV:PROJ_W, :])
              + _nn(dp_ref[:, COL_LR:PROJ_WP], w_ref[ROW_LR : ROW_LR + LANES, :]))
        dg1_ref[...] += jnp.sum(dh * xh, axis=0, keepdims=True)
        dx_ref[...] = dx1_ref[...] + _rms_bwd(dh * g1_ref[...], xh, r1)

    row = lambda w: pl.BlockSpec((tm, w), lambda i: (i, 0))
    return pl.pallas_call(
        body,
        name="inproj_dx",
        grid=(seq // tm,),
        in_specs=[row(D_MODEL), row(D_MODEL), _resident((1, D_MODEL)), _resident((PROJ_W, D_MODEL)), row(PROJ_WP), _resident(TOKEN_SHAPE)],
        out_specs=[row(D_MODEL), pl.BlockSpec((1, D_MODEL), lambda i: (0, 0))],
        out_shape=[jax.ShapeDtypeStruct((seq, D_MODEL), F32), jax.ShapeDtypeStruct((1, D_MODEL), F32)],
        compiler_params=_params(48),
    )(x, dx1, g1, w_in_t, dp, token)


def _in_hbm(a):
    return pltpu.with_memory_space_constraint(a, pltpu.HBM)


def _row_tile(rows, multiple=8):
    for t in range(min(rows, 512), 0, -1):
        if rows % t == 0 and t % multiple == 0:
            return t
    return rows


def _cast_into_slot(w, shard, token):
    rows, cols = w.shape
    tr = _row_tile(rows, 16)

    def body(s_ref, w_ref, token_ref, o_ref):
        o_ref[...] = w_ref[...].astype(BF16)

    return pl.pallas_call(
        body,
        name="cast_into_slot",
        grid_spec=pltpu.PrefetchScalarGridSpec(
            num_scalar_prefetch=1,
            grid=(rows // tr,),
            in_specs=[pl.BlockSpec((tr, cols), lambda i, s_ref: (i, 0)), pl.BlockSpec(TOKEN_SHAPE, lambda i, s_ref: (0, 0))],
            out_specs=pl.BlockSpec((None, tr, cols), lambda i, s_ref: (s_ref[0], i, 0)),
        ),
        out_shape=pltpu.HBM((N_SHARDS, rows, cols), BF16),
        compiler_params=_params(32, ("parallel",)),
    )(shard, _in_hbm(w), token)


def _add_halves(grads4, recvs, shard_core):
    n = len(grads4)
    _, rows, _ = grads4[0].shape
    tr = _row_tile(rows, 16)

    def body(sc_ref, *refs):
        for k in range(n):
            total = refs[k][...] + refs[n + k][...]
            refs[3 * n + k][...] = total.astype(BF16)

            @pl.when(pl.program_id(1) == sc_ref[0])
            def _(k=k, total=total):
                refs[2 * n + k][...] = total

    theirs = pl.BlockSpec((None, tr, HALF), lambda i, s, sc_ref: (s, i, 0))
    mine = pl.BlockSpec((None, tr, HALF), lambda i, s, sc_ref: (s, i, sc_ref[1]))
    kept = pl.BlockSpec((tr, HALF), lambda i, s, sc_ref: (i, 0))
    outs = pl.pallas_call(
        body,
        name="add_halves",
        grid_spec=pltpu.PrefetchScalarGridSpec(
            num_scalar_prefetch=1,
            grid=(rows // tr, N_SHARDS),
            in_specs=[mine] * n + [theirs] * n,
            out_specs=[kept] * n + [theirs] * n,
        ),
        out_shape=[pltpu.HBM((rows, HALF), F32)] * n + [pltpu.HBM((N_SHARDS, rows, HALF), BF16)] * n,
        compiler_params=_params(48, ("parallel", "arbitrary")),
    )(shard_core, *[_in_hbm(a) for a in list(grads4) + list(recvs)])
    return list(zip(outs[:n], outs[n:]))


def _add_partials(part, recv3, shard_core, token):
    rows, _ = part.shape
    tr = _row_tile(rows, 16)

    def body(sc_ref, p_ref, r_ref, token_ref, o_ref):
        o_ref[...] = ((p_ref[...] + r_ref[0].astype(F32)) + r_ref[1].astype(F32)) + r_ref[2].astype(F32)

    return pl.pallas_call(
        body,
        name="add_partials",
        grid_spec=pltpu.PrefetchScalarGridSpec(
            num_scalar_prefetch=1,
            grid=(rows // tr,),
            in_specs=[
                pl.BlockSpec((tr, HALF), lambda i, sc_ref: (i, 0)),
                pl.BlockSpec((3, tr, HALF), lambda i, sc_ref: (0, i, 0)),
                pl.BlockSpec(TOKEN_SHAPE, lambda i, sc_ref: (0, 0)),
            ],
            out_specs=pl.BlockSpec((tr, HALF), lambda i, sc_ref: (i, sc_ref[1])),
        ),
        out_shape=pltpu.HBM((rows, 2 * HALF), F32),
        compiler_params=_params(32, ("parallel",)),
    )(shard_core, _in_hbm(part), _in_hbm(recv3), token)


def _adam_math(w, g, m, v):
    m = ADAM_B1 * m + (1.0 - ADAM_B1) * g
    v = ADAM_B2 * v + (1.0 - ADAM_B2) * (g * g)
    m_hat = m / (1.0 - ADAM_B1**ADAM_STEP)
    v_hat = v / (1.0 - ADAM_B2**ADAM_STEP)
    delta = -ADAM_LR * (m_hat / (jnp.sqrt(v_hat) + ADAM_EPS) + ADAM_WD * w)
    return delta, m, v


def _adamw(w, g, m, v):
    rows, cols = w.shape
    tr = _row_tile(rows)

    def body(w_ref, g_ref, m_ref, v_ref, go_ref, d_ref, mo_ref, vo_ref):
        gv = g_ref[...]
        go_ref[...] = gv
        d_ref[...], mo_ref[...], vo_ref[...] = _adam_math(w_ref[...], gv, m_ref[...], v_ref[...])

    spec = pl.BlockSpec((tr, cols), lambda i: (i, 0))
    return pl.pallas_call(
        body, name="adamw", grid=(rows // tr,), in_specs=[spec] * 4, out_specs=[spec] * 4, out_shape=[pltpu.HBM(w.shape, F32)] * 4,
        compiler_params=_params(32, ("parallel",)),
    )(_in_hbm(w), _in_hbm(g), _in_hbm(m), _in_hbm(v))


SMALL_ROWS = 560
DECAY_ROWS = 8
SMALL_TOTAL = SMALL_ROWS + 2 * N_SHARDS * DECAY_ROWS


def _adamw_small(gathered, own, wp, mp, vp, like):
    out_rows = SMALL_ROWS + 2 * DECAY_ROWS
    places, off = [], 0
    for a in like:
        rows = a.size // LANES
        kept = a.shape[-1] == LANES
        places.append((off, rows, kept, (rows, LANES) if kept else (1, a.size)))
        off += rows
    loss_row = off
    decay_shape = (LOWRANK, KEY_W // N_SHARDS)
    n = len(places) + 2

    def body(ga_ref, own_ref, w_ref, m_ref, v_ref, *refs):
        outs, loss_ref, packed = refs[: 4 * n], refs[4 * n], refs[4 * n + 1 :]
        g_sc = packed[0]
        x, y, c = _position()
        shard, me = 2 * x + y, 4 * x + 2 * y + c
        total = lambda rows: functools.reduce(lambda a, b: a + b, [jnp.where(me == d, own_ref[rows, :], ga_ref[d, rows, :]) for d in range(8)])
        g_sc[pl.ds(0, SMALL_ROWS), :] = total(pl.ds(0, SMALL_ROWS))
        for k in range(2):
            start = pl.multiple_of(SMALL_ROWS + k * N_SHARDS * DECAY_ROWS + shard * DECAY_ROWS, DECAY_ROWS)
            g_sc[pl.ds(SMALL_ROWS + k * DECAY_ROWS, DECAY_ROWS), :] = total(pl.ds(start, DECAY_ROWS))
        packed[1][...], packed[2][...], packed[3][...] = _adam_math(w_ref[...], g_sc[...], m_ref[...], v_ref[...])
        loss_ref[...] = g_sc[loss_row : loss_row + 1, :]
        for t, res in enumerate(packed):
            for (at, rows, kept, _), out in zip(places, outs[t * n :]):
                if kept:
                    out[...] = res[at : at + rows, :]
                else:
                    for r in range(rows):
                        out[:, r * LANES : (r + 1) * LANES] = res[at + r : at + r + 1, :]
            for k in range(2):
                out = outs[t * n + len(places) + k]
                both = res[SMALL_ROWS + k * DECAY_ROWS : SMALL_ROWS + (k + 1) * DECAY_ROWS, :]
                halves = (both, pltpu.roll(both, LANES // 2, axis=1))
                for r in range(DECAY_ROWS):
                    for h in range(2):
                        out[2 * r + h : 2 * r + h + 1, :] = halves[h][r : r + 1, 0 : LANES // 2]

    shapes = [jax.ShapeDtypeStruct(s, F32) for *_, s in places] + [jax.ShapeDtypeStruct(decay_shape, F32)] * 2
    out = pl.pallas_call(
        body,
        name="adamw_small",
        out_shape=shapes * 4 + [jax.ShapeDtypeStruct((1, LANES), F32)],
        scratch_shapes=[pltpu.VMEM((out_rows, LANES), F32)] * 4,
        compiler_params=_params(32, None),
    )(gathered, own, wp, mp, vp)
    return [list(out[t * n : (t + 1) * n]) for t in range(4)], out[4 * n]


ANY = pl.BlockSpec(memory_space=pl.ANY)


def _position():
    return lax.axis_index("x"), lax.axis_index("y"), lax.axis_index("c")


def _other_chips(x, y):
    return [(1 - x, y), (x, 1 - y), (1 - x, 1 - y)]


HBM = pl.BlockSpec(memory_space=pltpu.HBM)
SEM = pl.BlockSpec(memory_space=pltpu.SEMAPHORE)
TOKEN = jax.ShapeDtypeStruct(TOKEN_SHAPE, F32)
DATAFLOW = pltpu.SideEffectType.DATAFLOW_SIDE_EFFECTING


def _half_block(ref4, slot, core):
    return ref4.at[slot, :, pl.ds(pl.multiple_of(core * HALF, HALF), HALF)]


def _gather_ici_copies(bufs, lands, send_sems, recv_sems):
    x, y, c = _position()
    pairs = []
    for k, ref4 in enumerate(bufs):
        mine = _half_block(ref4, 2 * x + y, c)
        for j, (px, py) in enumerate(_other_chips(x, y)):
            sems = dict(send_sem=send_sems.at[3 * k + j], recv_sem=recv_sems.at[3 * k + j], device_id=(px, py, c), device_id_type=MESH)
            pairs.append((functools.partial(pltpu.make_async_remote_copy, src_ref=mine, dst_ref=mine, **sems),
                          functools.partial(pltpu.make_async_remote_copy, src_ref=mine, dst_ref=_half_block(ref4, 2 * px + py, c), **sems)))
    return pairs


def _gather_d2d_copies(bufs, lands, send_sems, recv_sems):
    x, y, c = _position()
    pairs = []
    for k, ref4 in enumerate(bufs):
        for j, (px, py) in enumerate(_other_chips(x, y)):
            have = _half_block(ref4, 2 * px + py, c)
            sems = dict(send_sem=send_sems.at[3 * k + j], recv_sem=recv_sems.at[3 * k + j], device_id=(x, y, 1 - c), device_id_type=MESH)
            pairs.append((functools.partial(pltpu.make_async_remote_copy, src_ref=have, dst_ref=have, **sems),
                          functools.partial(pltpu.make_async_remote_copy, src_ref=have, dst_ref=_half_block(ref4, 2 * px + py, 1 - c), **sems)))
    return pairs


def _gather_forward(bufs):
    n = len(bufs)

    def body(*refs):
        outs = refs[n : 2 * n]
        send_sems, recv_sems = refs[2 * n :]
        d2d = _gather_d2d_copies(outs, (), send_sems, recv_sems)
        for forward, _ in d2d:
            forward().start()
        for forward, arrival in d2d:
            arrival().wait_recv()
            forward().wait_send()

    return pl.pallas_call(
        body,
        name="gather_forward",
        in_specs=[ANY] * n,
        out_specs=[ANY] * n,
        out_shape=[jax.ShapeDtypeStruct(b.shape, b.dtype) for b in bufs],
        input_output_aliases={k: k for k in range(n)},
        scratch_shapes=[pltpu.SemaphoreType.DMA((3 * n,)), pltpu.SemaphoreType.DMA((3 * n,))],
        compiler_params=pltpu.CompilerParams(has_side_effects=True),
    )(*bufs)


def _both_ends(**copy):
    maker = functools.partial(pltpu.make_async_remote_copy, **copy)
    return maker, maker


def _scatter_copies(parts, lands, send_sems, recv_sems):
    x, y, c = _position()
    return [_both_ends(src_ref=parts[k].at[2 * px + py], dst_ref=lands[k].at[j], send_sem=send_sems.at[3 * k + j],
                       recv_sem=recv_sems.at[3 * k + j], device_id=(px, py, c), device_id_type=MESH)
            for k in range(len(parts)) for j, (px, py) in enumerate(_other_chips(x, y))]


def _exchange_copies(grads, lands, send_sems, recv_sems):
    x, y, c = _position()
    return [_both_ends(src_ref=grads[k].at[:, :, pl.ds(pl.multiple_of((1 - c) * HALF, HALF), HALF)], dst_ref=lands[k],
                       send_sem=send_sems.at[k], recv_sem=recv_sems.at[k], device_id=(x, y, 1 - c), device_id_type=MESH)
            for k in range(len(grads))]


def _exchange_lands(grads4):
    return [jax.ShapeDtypeStruct((N_SHARDS, g.shape[1], HALF), g.dtype) for g in grads4]


def _scatter_lands(parts4):
    return [jax.ShapeDtypeStruct((3,) + g.shape[1:], g.dtype) for g in parts4]


def _small_gather_copies(blocks, lands, send_sems, recv_sems):
    x, y, c = _position()
    flip = lambda v, bit: 1 - v if bit else v
    return [_both_ends(src_ref=blocks[0], dst_ref=lands[0].at[4 * x + 2 * y + c], send_sem=send_sems.at[r - 1],
                       recv_sem=recv_sems.at[r - 1], device_id=(flip(x, r & 4), flip(y, r & 2), flip(c, r & 1)), device_id_type=MESH)
            for r in range(1, 8)]


def _split_start(name, srcs, land_shapes, make_copies, nsem, after=()):
    n, nl, na = len(srcs), len(land_shapes), len(after)
    lands = [lax.empty(a.shape, a.dtype) for a in land_shapes]

    def body(*refs):
        send_sems, recv_sems = refs[n + nl + na], refs[n + nl + na + 1]
        token = refs[2 * (n + nl) + na + 2]
        for send, _ in make_copies(refs[:n], refs[n : n + nl], send_sems, recv_sems):
            send().start()
        token[...] = jnp.zeros_like(token)

    hbm = lambda a: pltpu.HBM(a.shape, a.dtype)
    out = pl.pallas_call(
        body,
        name=name,
        in_specs=[HBM] * (n + nl) + [ANY] * na,
        out_specs=(SEM, SEM, *[HBM] * (n + nl), pl.BlockSpec(memory_space=pltpu.VMEM)),
        out_shape=(pltpu.SemaphoreType.DMA((nsem,)), pltpu.SemaphoreType.DMA((nsem,)), *[hbm(a) for a in list(srcs) + lands], TOKEN),
        input_output_aliases={k: 2 + k for k in range(n + nl)},
        compiler_params=pltpu.CompilerParams(has_side_effects=DATAFLOW),
    )(*[pltpu.with_memory_space_constraint(a, pltpu.HBM) for a in list(srcs) + lands], *after)
    return out[0], out[1], list(out[2 : 2 + n]), list(out[2 + n : 2 + n + nl]), out[2 + n + nl]


def _split_wait(name, send_sems, recv_sems, srcs, lands, make_copies, after):
    n, nl = len(srcs), len(lands)

    def body(*refs):
        for send, arrival in make_copies(refs[:n], refs[n : n + nl], refs[n + nl], refs[n + nl + 1]):
            send().wait_send()
            arrival().wait_recv()

    hbm = lambda a: pltpu.HBM(a.shape, a.dtype)
    out = pl.pallas_call(
        body,
        name=name,
        in_specs=[HBM] * (n + nl) + [SEM, SEM] + [ANY] * len(after),
        out_specs=tuple([HBM] * (n + nl)),
        out_shape=tuple(hbm(a) for a in list(srcs) + list(lands)),
        input_output_aliases={k: k for k in range(n + nl)},
        compiler_params=pltpu.CompilerParams(has_side_effects=DATAFLOW),
    )(*srcs, *lands, send_sems, recv_sems, *after)
    return list(out[:n]), list(out[n:])


def _join_copies(bufs, lands, send_sems, recv_sems, first=0):
    x, y, c = _position()
    half = lambda ref, core: ref.at[:, pl.ds(pl.multiple_of(core * HALF, HALF), HALF)]
    pairs = []
    for k, ref in enumerate(bufs):
        sems = dict(send_sem=send_sems.at[first + k], recv_sem=recv_sems.at[first + k], device_id=(x, y, 1 - c), device_id_type=MESH)
        pairs.append((functools.partial(pltpu.make_async_remote_copy, src_ref=half(ref, c), dst_ref=half(ref, c), **sems),
                      functools.partial(pltpu.make_async_remote_copy, src_ref=half(ref, c), dst_ref=half(ref, 1 - c), **sems)))
    return pairs


def _allgather_small(block):
    m_per, ncol = block.shape

    def body(x_ref, out_ref, send_sems, recv_sems, local_sem):
        x, y, c = _position()
        me, sibling = (x, y, c), (x, y, 1 - c)
        chips = _other_chips(x, y)

        def rows(px, py, pc):
            return out_ref.at[4 * px + 2 * py + pc]

        def copy(k, blk, to, src=None):
            return pltpu.make_async_remote_copy(
                src_ref=rows(*blk) if src is None else src, dst_ref=rows(*blk),
                send_sem=send_sems.at[k], recv_sem=recv_sems.at[k], device_id=to, device_id_type=MESH)

        mine = pltpu.make_async_copy(x_ref, rows(*me), local_sem)
        mine.start()
        first = [copy(0, me, sibling, src=x_ref)] + [copy(1 + j, me, (*chip, c), src=x_ref) for j, chip in enumerate(chips)]
        for cp in first:
            cp.start()
        passed = [copy(4 + j, (*chip, c), sibling) for j, chip in enumerate(chips)]
        for j, chip in enumerate(chips):
            copy(1 + j, (*chip, c), me).wait_recv()
            passed[j].start()
        copy(0, sibling, me).wait_recv()
        for j, chip in enumerate(chips):
            copy(4 + j, (*chip, 1 - c), me).wait_recv()
        for cp in first + passed:
            cp.wait_send()
        mine.wait()

    return pl.pallas_call(
        body,
        name="allgather_small",
        in_specs=[pl.BlockSpec(memory_space=pltpu.VMEM)],
        out_specs=pl.BlockSpec(memory_space=pltpu.VMEM),
        out_shape=jax.ShapeDtypeStruct((8, m_per, ncol), block.dtype),
        scratch_shapes=[pltpu.SemaphoreType.DMA((7,)), pltpu.SemaphoreType.DMA((7,)), pltpu.SemaphoreType.DMA],
        compiler_params=pltpu.CompilerParams(has_side_effects=True, vmem_limit_bytes=32 * MIB),
    )(block)


SMALL_NAMES = ["norm1_g", "b_decay_f", "b_decay_b", "gla_norm_g", "gmlp_ln_g", "gmlp_ln_b", "w_spatial", "b_spatial", "norm2_g", "final_norm_g"]


def _pack_small(parts, decay_parts):
    flat = jnp.concatenate([a.reshape(-1) for a in parts])
    flat = jnp.pad(flat, (0, SMALL_ROWS * LANES - flat.shape[0])).reshape(SMALL_ROWS, LANES)
    return jnp.concatenate([flat] + [d.reshape(-1, LANES) for d in decay_parts], axis=0)


def kernel(x, norm1_g, w_in, w_decay_f, b_decay_f, w_decay_b, b_decay_b, gla_norm_g, gmlp_ln_g, gmlp_ln_b, w_spatial, b_spatial, w_out, norm2_g, w_gate, w_up, w_down, final_norm_g, loss_target, m_norm1_g, m_w_in, m_w_decay_f, m_b_decay_f, m_w_decay_b, m_b_decay_b, m_gla_norm_g, m_gmlp_ln_g, m_gmlp_ln_b, m_w_spatial, m_b_spatial, m_w_out, m_norm2_g, m_w_gate, m_w_up, m_w_down, m_final_norm_g, v_norm1_g, v_w_in, v_w_decay_f, v_b_decay_f, v_w_decay_b, v_b_decay_b, v_gla_norm_g, v_gmlp_ln_g, v_gmlp_ln_b, v_w_spatial, v_b_spatial, v_w_out, v_norm2_g, v_w_gate, v_w_up, v_w_down, v_final_norm_g):
    args = dict(locals())
    cx, cy, cc = lax.axis_index("x"), lax.axis_index("y"), lax.axis_index("c")
    shard = 2 * cx + cy
    xs = x[0]
    target = loss_target[0]

    big_names = ["w_in", "w_out", "w_gate", "w_up", "w_down"]
    transposed = ("w_in", "w_gate", "w_up")
    rows_of = lambda pre, k: jnp.transpose(args[pre + k][0]) if k in transposed else args[pre + k][0]
    big_shards = {k: rows_of("", k) for k in big_names}
    s_arr = shard.reshape(1).astype(jnp.int32)
    sc_arr = jnp.stack([shard, cc]).astype(jnp.int32)
    zero_token = jnp.zeros(TOKEN_SHAPE, F32)
    w_send, w_recv, (w_in4,), _, token_w_in = _split_start(
        "w_in_gather_start", [_cast_into_slot(big_shards["w_in"], s_arr, zero_token)], [], _gather_ici_copies, 3)
    late = ["w_out", "w_gate", "w_up", "w_down"]
    late_slots = [_cast_into_slot(big_shards[k], s_arr, token_w_in) for k in late]
    dec_block = jnp.concatenate([w_decay_f[0].reshape(-1, LANES), w_decay_b[0].reshape(-1, LANES)], axis=0)
    dec_all = _allgather_small(dec_block)
    (w_in4,), _ = _split_wait("w_in_gather_wait", w_send, w_recv, [w_in4], [], _gather_ici_copies, (dec_all, *late_slots))
    (w_in4,) = _gather_forward([w_in4])
    w_in_t = w_in4.reshape(PROJ_W, D_MODEL)
    g_send, g_recv, late_bufs, _, token_gather = _split_start(
        "gather_start", late_slots, [], _gather_ici_copies, 3 * len(late), after=(w_in4,))
    dec_all = dec_all[::2].reshape(N_SHARDS, 2, LOWRANK, KEY_W // N_SHARDS)
    wdf_full = jnp.transpose(dec_all[:, 0], (1, 0, 2)).reshape(LOWRANK, KEY_W)
    wdb_full = jnp.transpose(dec_all[:, 1], (1, 0, 2)).reshape(LOWRANK, KEY_W)
    wd_pad_f = jnp.zeros((LANES, KEY_W), F32).at[0:LOWRANK].set(wdf_full).astype(BF16)
    wd_pad_b = jnp.zeros((LANES, KEY_W), F32).at[LOWRANK : 2 * LOWRANK].set(wdb_full).astype(BF16)

    ws_bf = w_spatial[0].astype(BF16)
    wst_bf = jnp.transpose(w_spatial[0], (0, 2, 1)).astype(BF16)
    bs_col = b_spatial[0].reshape(GMLP_GROUPS, GMLP_CHUNK, 1)

    p = _inproj(xs, norm1_g, w_in_t, token_gather)
    o_f, st_f = _gla_fwd(p, wd_pad_f, b_decay_f, token_gather, reverse=False)
    o_b, st_b = _gla_fwd(p, wd_pad_b, b_decay_b, token_gather, reverse=True)
    late_bufs, _ = _split_wait("gather_wait", g_send, g_recv, late_bufs, [], _gather_ici_copies, (o_f, o_b))
    (w_out4,) = _gather_forward(late_bufs[:1])
    f_send, f_recv, ffn_bufs, _, token_forward = _split_start(
        "forward_start", late_bufs[1:], [], _gather_d2d_copies, 3 * (len(late) - 1), after=(w_out4,))
    w_out_full = w_out4.reshape(-1, D_MODEL)
    x1, ycat = _mixer_out(xs, o_f, o_b, p, gla_norm_g, gmlp_ln_g, gmlp_ln_b, ws_bf, bs_col, w_out_full, token_forward)
    ffn_bufs, _ = _split_wait("forward_wait", f_send, f_recv, ffn_bufs, [], _gather_d2d_copies, (x1,))
    wg_t, wu_t, wd = [b.reshape(-1, D_MODEL) for b in ffn_bufs]
    gf = final_norm_g.reshape(1, D_MODEL)
    h2, gate, up, act, dx2, loss_acc, dgf = _ffn_fwd(x1, target, norm2_g, gf, wg_t, wu_t, wd)

    dgate, dup, dx1, dg2 = _ffn_bwd(dx2, gate, up, x1, norm2_g, wg_t, wu_t, wd)
    ffn_grads4 = [g.reshape(N_SHARDS, FF_SHARD, D_MODEL) for g in _ffn_wgrad(h2, dgate, dup, act, dx2)]
    e_send, e_recv, e_srcs, e_lands, token_exchange = _split_start(
        "exchange_start", ffn_grads4, _exchange_lands(ffn_grads4), _exchange_copies, len(ffn_grads4))
    do, dg, du, dvv, dwo, dgn, dlng, dlnb, dws, dbs = _mixer_bwd(
        dx1, ycat, o_f, o_b, p, gla_norm_g, gmlp_ln_g, gmlp_ln_b, ws_bf, wst_bf, bs_col, w_out_full, token_exchange)
    ffn_mine, ffn_other = _split_wait("exchange_wait", e_send, e_recv, e_srcs, e_lands, _exchange_copies, (do,))
    ffn_parts = _add_halves(ffn_mine, ffn_other, sc_arr)
    ffn_payload = [pb for _, pb in ffn_parts]
    s_send, s_recv, s_parts, s_lands, token_scatter = _split_start(
        "scatter_start", ffn_payload, _scatter_lands(ffn_payload), _scatter_copies, 3 * len(ffn_payload))
    dq_f, dk_f, dv_f, dlr_f, dwdec_f, dbdec_f = _gla_bwd(p, do, st_f, wd_pad_f, b_decay_f, token_scatter, reverse=False)
    dq, dk, dv, dlr, dwdec_b, dbdec_b = _gla_bwd(
        p, do, st_b, wd_pad_b, b_decay_b, token_scatter, reverse=True, other=(dq_f, dk_f, dv_f, dlr_f))
    dwin_t, dp = _inproj_wgrad(xs, norm1_g, dq, dk, dv, dg, du, dvv, dlr)
    _, ffn_recv = _split_wait("scatter_wait", s_send, s_recv, s_parts, s_lands, _scatter_copies, (dwin_t,))

    dwin4 = dwin_t.reshape(N_SHARDS, PROJ_W // N_SHARDS, D_MODEL)
    dwo4 = dwo.reshape(N_SHARDS, D_MODEL // N_SHARDS, D_MODEL)
    proj_grads4 = [dwin4, dwo4]
    x_send, x_recv, x_srcs, x_lands, token_swap = _split_start(
        "proj_exchange_start", proj_grads4, _exchange_lands(proj_grads4), _exchange_copies, len(proj_grads4))
    ffn_bufs = [_add_partials(pf, r, sc_arr, token_swap) for (pf, _), r in zip(ffn_parts, ffn_recv)]
    proj_mine, proj_other = _split_wait("proj_exchange_wait", x_send, x_recv, x_srcs, x_lands, _exchange_copies, tuple(ffn_bufs))
    proj_parts = [_add_halves([g], [r], sc_arr)[0] for g, r in zip(proj_mine, proj_other)]
    proj_payload = [pb for _, pb in proj_parts]
    p_send, p_recv, p_parts, p_lands, token_proj = _split_start(
        "proj_scatter_start", proj_payload, _scatter_lands(proj_payload), _scatter_copies, 3 * len(proj_payload))
    j_send, j_recv, ffn_bufs, _, token_join = _split_start("join_start", ffn_bufs, [], _join_copies, len(ffn_bufs), after=(token_proj,))
    dx, dg1 = _inproj_dx(xs, dx1, norm1_g, w_in_t, dp, token_join)
    _, proj_recv = _split_wait("proj_scatter_wait", p_send, p_recv, p_parts, p_lands, _scatter_copies, (dx,))

    dwdec_f16 = dwdec_f[0:LOWRANK]
    dwdec_b16 = dwdec_b[LOWRANK : 2 * LOWRANK]
    shard_major = lambda a: jnp.transpose(a.reshape(LOWRANK, N_SHARDS, KEY_W // N_SHARDS), (1, 0, 2))
    small_grads = {
        "norm1_g": dg1, "b_decay_f": dbdec_f, "b_decay_b": dbdec_b, "gla_norm_g": dgn, "gmlp_ln_g": dlng, "gmlp_ln_b": dlnb,
        "w_spatial": dws, "b_spatial": dbs, "norm2_g": dg2, "final_norm_g": dgf,
    }
    g_pack = _pack_small([small_grads[k] for k in SMALL_NAMES] + [loss_acc], [shard_major(dwdec_f16), shard_major(dwdec_b16)])
    proj_bufs = [_add_partials(pf, r, sc_arr, token_join) for (pf, _), r in zip(proj_parts, proj_recv)]
    proj_join_copies = functools.partial(_join_copies, first=7)
    tail_copies = lambda srcs, lands, send_sems, recv_sems: (
        _small_gather_copies(srcs[:1], lands, send_sems, recv_sems) + proj_join_copies(srcs[1:], (), send_sems, recv_sems))
    t_send, t_recv, (g_pack, *proj_bufs), g_lands, token_tail = _split_start(
        "tail_start", [g_pack] + proj_bufs, [jax.ShapeDtypeStruct((8, SMALL_TOTAL, LANES), F32)], tail_copies, 7 + len(proj_bufs))

    ffn_bufs, _ = _split_wait("join_wait", j_send, j_recv, ffn_bufs, [], _join_copies, (dx, token_tail))
    adamw = lambda k, g: _adamw(big_shards[k], g, rows_of("m_", k), rows_of("v_", k))
    big_updates = {k: adamw(k, g) for k, g in zip(big_names[2:], ffn_bufs)}
    proj_bufs, _ = _split_wait(
        "proj_join_wait", t_send, t_recv, proj_bufs, [], proj_join_copies, tuple(u[1] for u in big_updates.values()))
    big_updates.update({k: adamw(k, g) for k, g in zip(big_names[:2], proj_bufs)})

    (g_pack,), (g_all,) = _split_wait(
        "small_gather_wait", t_send, t_recv, [g_pack], g_lands, _small_gather_copies, tuple(u[1] for u in big_updates.values()))
    pack_own = lambda pre: _pack_small([args[pre + k] for k in SMALL_NAMES], [args[pre + "w_decay_f"], args[pre + "w_decay_b"]])
    small_updates, loss_row = _adamw_small(g_all, g_pack, pack_own(""), pack_own("m_"), pack_own("v_"), [args[k] for k in SMALL_NAMES])

    names = ["norm1_g", "w_in", "w_decay_f", "b_decay_f", "w_decay_b", "b_decay_b", "gla_norm_g", "gmlp_ln_g", "gmlp_ln_b",
             "w_spatial", "b_spatial", "w_out", "norm2_g", "w_gate", "w_up", "w_down", "final_norm_g"]
    results = {"g": {}, "d": {}, "m": {}, "v": {}}
    for tag, arrays in zip("gdmv", small_updates):
        for k, a in zip(SMALL_NAMES + ["w_decay_f", "w_decay_b"], arrays):
            results[tag][k] = a.reshape(args[k].shape)
    for k in big_names:
        for tag, a in zip("gdmv", big_updates[k]):
            results[tag][k] = (jnp.transpose(a) if k in transposed else a).reshape(args[k].shape)

    loss = loss_row[0, 0]
    grad_x = dx.reshape(x.shape)
    return (loss, grad_x, *[results["g"][k] for k in names], *[results["d"][k] for k in names],
            *[results["m"][k] for k in names], *[results["v"][k] for k in names])
```

```python
import functools
import math

import jax
import jax.numpy as jnp
from jax import lax
from jax.experimental import pallas as pl
from jax.experimental.pallas import tpu as pltpu

F32, BF16 = jnp.float32, jnp.bfloat16

D_MODEL = 1024
GLA_HEADS = 4
GLA_DK = 64
GLA_DV = 128
KEY_W = GLA_HEADS * GLA_DK
GLA_W = GLA_HEADS * GLA_DV
GMLP_W = 512
GMLP_GROUPS = 4
GMLP_CHUNK = 128
LOWRANK = 16
GLA_CHUNK = 64
GLA_TAU = 16.0
PROJ_W = 2592
PROJ_WP = 2688
D_FF = 2816
N_SHARDS = 4
FF_SHARD = D_FF // N_SHARDS
EPS = 1e-6
LANES = 128
TOKEN_SHAPE = (8, LANES)
MIB = 1024 * 1024

ADAM_LR = 0.001
ADAM_B1 = 0.9
ADAM_B2 = 0.999
ADAM_EPS = 1e-08
ADAM_WD = 0.01
ADAM_STEP = 10

COL_Q, COL_K = 0, 256
COL_V, COL_G, COL_U, COL_VV = 512, 1024, 1536, 2048
COL_LR = 2560
ROW_LR, ROW_UV = 1536, 1568
HALF = D_MODEL // 2

MESH = pl.DeviceIdType.MESH


def _nn(a, b):
    return jnp.dot(a, b, preferred_element_type=F32)


def _nt(a, b):
    return lax.dot_general(a, b, (((1,), (1,)), ((), ())), preferred_element_type=F32)


def _tn(a, b):
    return lax.dot_general(a, b, (((0,), (0,)), ((), ())), preferred_element_type=F32)


def _bnn(a, b):
    return jnp.einsum("nik,nkj->nij", a, b, preferred_element_type=F32)


def _bnt(a, b):
    return jnp.einsum("nik,njk->nij", a, b, preferred_element_type=F32)


def _btn(a, b):
    return jnp.einsum("nki,nkj->nij", a, b, preferred_element_type=F32)


def _resident(shape):
    zeros = (0,) * len(shape)
    return pl.BlockSpec(shape, lambda *_: zeros, pipeline_mode=pl.Buffered(1))


def _params(vmem_mib, semantics=("arbitrary",)):
    return pltpu.CompilerParams(vmem_limit_bytes=vmem_mib * MIB, dimension_semantics=semantics)


def _sigmoid(x):
    return 1.0 / (1.0 + jnp.exp(-x))


def _gelu(x):
    return 0.5 * x * (1.0 + lax.erf(x * (1.0 / math.sqrt(2.0))))


def _gelu_and_grad(x):
    cdf = 0.5 * (1.0 + lax.erf(x * (1.0 / math.sqrt(2.0))))
    return x * cdf, cdf + x * jnp.exp(-0.5 * x * x) * (1.0 / math.sqrt(2.0 * math.pi))


def _log_sigmoid(x):
    return jnp.minimum(x, 0.0) - jnp.log(1.0 + jnp.exp(-jnp.abs(x)))


def _rms_bwd(dxh, xh, r):
    return r * (dxh - xh * jnp.mean(dxh * xh, axis=-1, keepdims=True))


def _chunk_cumsum(v, row_in_chunk, reverse):
    rows = v.shape[0]
    for sh in (1, 2, 4, 8, 16, 32):
        if reverse:
            v = v + jnp.where(row_in_chunk + sh < GLA_CHUNK, pltpu.roll(v, rows - sh, axis=0), 0.0)
        else:
            v = v + jnp.where(row_in_chunk >= sh, pltpu.roll(v, sh, axis=0), 0.0)
    return v


def _inproj(x, g1, w_in_t, token):
    seq = x.shape[0]
    tm = min(seq, 512)

    def body(x_ref, g_ref, w_ref, token_ref, p_ref):
        xv = x_ref[...]
        r = lax.rsqrt(jnp.mean(xv * xv, axis=-1, keepdims=True) + EPS)
        h = (xv * r * g_ref[...]).astype(BF16)
        p_ref[:, 0:COL_U] = _nt(h, w_ref[0:ROW_LR, :])
        p_ref[:, COL_U:COL_LR] = _nt(h, w_ref[ROW_UV:PROJ_W, :])
        p_ref[:, COL_LR:PROJ_WP] = _nt(h, w_ref[ROW_LR : ROW_LR + LANES, :])

    return pl.pallas_call(
        body,
        name="inproj",
        grid=(seq // tm,),
        in_specs=[pl.BlockSpec((tm, D_MODEL), lambda i: (i, 0)), _resident((1, D_MODEL)), _resident((PROJ_W, D_MODEL)), _resident(TOKEN_SHAPE)],
        out_specs=pl.BlockSpec((tm, PROJ_WP), lambda i: (i, 0)),
        out_shape=jax.ShapeDtypeStruct((seq, PROJ_WP), F32),
        compiler_params=_params(48, ("parallel",)),
    )(x, g1, w_in_t, token)


def _gla_tile(seq):
    return min(seq, 1024)


def _gla_decay_terms(lr_bf, wd_ref, bd_ref, pair, row_in_chunk, reverse, n):
    cols = pl.ds(pair * LANES, LANES)
    pre = _nn(lr_bf, wd_ref[:, cols]) + bd_ref[:, cols]
    la = _log_sigmoid(pre) * (1.0 / GLA_TAU)
    b = _chunk_cumsum(la, row_in_chunk, reverse)
    b3 = b.reshape(n, GLA_CHUNK, LANES)
    blast = b3[:, 0:1, :] if reverse else b3[:, GLA_CHUNK - 1 : GLA_CHUNK, :]
    return pre, b3, blast


def _gla_fwd(p, wd_pad, bd, token, reverse):
    seq = p.shape[0]
    tg = _gla_tile(seq)
    nt = seq // tg
    n = tg // GLA_CHUNK
    scale = GLA_DK**-0.5

    def tile(i):
        return nt - 1 - i if reverse else i

    def body(q_ref, k_ref, v_ref, lr_ref, wd_ref, bd_ref, token_ref, o_ref, st_ref, carry):
        @pl.when(pl.program_id(0) == 0)
        def _():
            carry[...] = jnp.zeros_like(carry)

        lr_bf = lr_ref[...].astype(BF16)
        states = [carry[h] for h in range(GLA_HEADS)]
        row_in_chunk = lax.broadcasted_iota(jnp.int32, (tg, LANES), 0) % GLA_CHUNK
        lane_head = lax.broadcasted_iota(jnp.int32, (1, LANES), 1) // GLA_DK
        tt = lax.broadcasted_iota(jnp.int32, (GLA_CHUNK, GLA_CHUNK), 0)
        ss = lax.broadcasted_iota(jnp.int32, (GLA_CHUNK, GLA_CHUNK), 1)
        causal = (tt <= ss) if reverse else (tt >= ss)
        order = range(n - 1, -1, -1) if reverse else range(n)
        heads = range(GLA_HEADS)
        qds, vhs, decs, sc_raw, dst = {}, {}, {}, {}, {}
        for pair in range(2):
            cols = pl.ds(pair * LANES, LANES)
            _, b3, blast = _gla_decay_terms(lr_bf, wd_ref, bd_ref, pair, row_in_chunk, reverse, n)
            q3 = q_ref[:, cols].reshape(n, GLA_CHUNK, LANES) * scale
            k3 = k_ref[:, cols].reshape(n, GLA_CHUNK, LANES)
            qd = q3 * jnp.exp(b3)
            kd = (k3 * jnp.exp(-b3)).astype(BF16)
            kte = k3 * jnp.exp(blast - b3)
            decs[pair] = jnp.exp(blast)
            qds[pair] = qd.astype(BF16)
            m0 = (lane_head == 0).astype(F32)
            m1 = (lane_head == 1).astype(F32)
            q_both = jnp.concatenate([(qd * m0).astype(BF16), (qd * m1).astype(BF16)], axis=1)
            sc_both = _bnt(q_both, kd)
            for hh, m in ((0, m0), (1, m1)):
                h = 2 * pair + hh
                vhs[h] = v_ref[:, pl.ds(h * GLA_DV, GLA_DV)].reshape(n, GLA_CHUNK, GLA_DV).astype(BF16)
                sc_raw[h] = sc_both[:, hh * GLA_CHUNK : (hh + 1) * GLA_CHUNK, :]
                dst[h] = _btn(vhs[h], (kte * m).astype(BF16))
        o_intra, befores = {}, {}
        for h in heads:
            o_intra[h] = _bnn(jnp.where(causal, sc_raw[h], 0.0).astype(BF16), vhs[h])
            st, before = states[h], [None] * n
            for j in order:
                before[j] = st
                st = st * decs[h // 2][j] + dst[h][j]
            states[h] = st
            befores[h] = jnp.stack(before).astype(BF16)
        outs = {}
        for pair in range(2):
            both = jnp.concatenate([befores[2 * pair], befores[2 * pair + 1]], axis=1)
            o_inter = _bnt(qds[pair], both)
            for hh in range(2):
                h = 2 * pair + hh
                outs[h] = (o_intra[h] + o_inter[:, :, hh * GLA_DV : (hh + 1) * GLA_DV]).reshape(tg, GLA_DV)
        for h in range(GLA_HEADS):
            o_ref[:, pl.ds(h * GLA_DV, GLA_DV)] = outs[h]
            st_ref[:, h] = befores[h]
            carry[h] = states[h]

    nchunks = seq // GLA_CHUNK
    return pl.pallas_call(
        body,
        name="gla_fwd_rev" if reverse else "gla_fwd",
        grid=(nt,),
        in_specs=[
            pl.BlockSpec((tg, KEY_W), lambda i: (tile(i), COL_Q // KEY_W)),
            pl.BlockSpec((tg, KEY_W), lambda i: (tile(i), COL_K // KEY_W)),
            pl.BlockSpec((tg, GLA_W), lambda i: (tile(i), COL_V // GLA_W)),
            pl.BlockSpec((tg, LANES), lambda i: (tile(i), COL_LR // LANES)),
            _resident((LANES, KEY_W)),
            _resident((1, KEY_W)),
            _resident(TOKEN_SHAPE),
        ],
        out_specs=[
            pl.BlockSpec((tg, GLA_W), lambda i: (tile(i), 0)),
            pl.BlockSpec((n, GLA_HEADS, GLA_DV, LANES), lambda i: (tile(i), 0, 0, 0)),
        ],
        out_shape=[
            jax.ShapeDtypeStruct((seq, GLA_W), F32),
            jax.ShapeDtypeStruct((nchunks, GLA_HEADS, GLA_DV, LANES), BF16),
        ],
        scratch_shapes=[pltpu.VMEM((GLA_HEADS, GLA_DV, LANES), F32)],
        compiler_params=_params(48),
    )(p, p, p, p, wd_pad, bd, token)


def _mixer_out(x, o_f, o_b, p, gn, lng, lnb, ws_bf, bs_col, w_out, token):
    seq = x.shape[0]
    tm = min(seq, 512)

    def body(x_ref, of_ref, ob_ref, g_ref, u_ref, vv_ref, gn_ref, lng_ref, lnb_ref, ws_ref, bs_ref, wo_ref, token_ref, x1_ref, yc_ref, vn_sc):
        for h in range(GLA_HEADS):
            cols = pl.ds(h * GLA_DV, GLA_DV)
            oh = of_ref[:, cols] + ob_ref[:, cols]
            on = oh * lax.rsqrt(jnp.mean(oh * oh, axis=-1, keepdims=True) + EPS)
            gh = g_ref[:, cols]
            yc_ref[:, cols] = (on * gn_ref[:, cols] * (gh * _sigmoid(gh))).astype(BF16)
        zv = _gelu(vv_ref[...])
        xc = zv - jnp.mean(zv, axis=-1, keepdims=True)
        vhat = xc * lax.rsqrt(jnp.mean(xc * xc, axis=-1, keepdims=True) + EPS)
        vn_sc[...] = (vhat * lng_ref[...] + lnb_ref[...]).astype(BF16)
        for c in range(tm // GMLP_CHUNK):
            rows = pl.ds(c * GMLP_CHUNK, GMLP_CHUNK)
            for g in range(GMLP_GROUPS):
                cols = pl.ds(g * LANES, LANES)
                s = _nn(ws_ref[g], vn_sc[rows, cols]) + bs_ref[g]
                yc_ref[rows, pl.ds(GLA_W + g * LANES, LANES)] = (_gelu(u_ref[rows, cols]) * s).astype(BF16)
        x1_ref[...] = x_ref[...] + _nn(yc_ref[...], wo_ref[...])

    row = lambda w: pl.BlockSpec((tm, w), lambda i: (i, 0))
    pcol = lambda col: pl.BlockSpec((tm, GLA_W), lambda i: (i, col // GLA_W))
    return pl.pallas_call(
        body,
        name="mixer_out",
        grid=(seq // tm,),
        in_specs=[
            row(D_MODEL), row(GLA_W), row(GLA_W), pcol(COL_G), pcol(COL_U), pcol(COL_VV),
            _resident((1, GLA_W)), _resident((1, GMLP_W)), _resident((1, GMLP_W)),
            _resident((GMLP_GROUPS, GMLP_CHUNK, GMLP_CHUNK)), _resident((GMLP_GROUPS, GMLP_CHUNK, 1)),
            _resident((D_MODEL, D_MODEL)), _resident(TOKEN_SHAPE),
        ],
        out_specs=[row(D_MODEL), row(D_MODEL)],
        out_shape=[jax.ShapeDtypeStruct((seq, D_MODEL), F32), jax.ShapeDtypeStruct((seq, D_MODEL), BF16)],
        scratch_shapes=[pltpu.VMEM((tm, GMLP_W), BF16)],
        compiler_params=_params(48, ("parallel",)),
    )(x, o_f, o_b, p, p, p, gn, lng, lnb, ws_bf, bs_col, w_out, token)


def _ffn_fwd(x1, target, g2, gf, wg_t, wu_t, wd):
    seq = x1.shape[0]
    tm = min(seq, 256)

    def body(x1_ref, t_ref, g2_ref, gf_ref, wg_ref, wu_ref, wd_ref, h2_ref, gate_ref, up_ref, act_ref, dx2_ref, loss_ref, dgf_ref):
        @pl.when(pl.program_id(0) == 0)
        def _():
            loss_ref[...] = jnp.zeros_like(loss_ref)
            dgf_ref[...] = jnp.zeros_like(dgf_ref)

        x1v = x1_ref[...]
        h2 = (x1v * lax.rsqrt(jnp.mean(x1v * x1v, axis=-1, keepdims=True) + EPS) * g2_ref[...]).astype(BF16)
        h2_ref[...] = h2
        gate = _nt(h2, wg_ref[...])
        up = _nt(h2, wu_ref[...])
        act = (gate * _sigmoid(gate) * up).astype(BF16)
        gate_ref[...] = gate
        up_ref[...] = up
        act_ref[...] = act
        x2 = x1v + _nn(act, wd_ref[...])
        rf = lax.rsqrt(jnp.mean(x2 * x2, axis=-1, keepdims=True) + EPS)
        xh = x2 * rf
        err = xh * gf_ref[...] - t_ref[...]
        loss_ref[...] += 0.5 * jnp.sum(jnp.mean(err * err, axis=-1, keepdims=True))
        dy = err * (1.0 / D_MODEL)
        dgf_ref[...] += jnp.sum(dy * xh, axis=0, keepdims=True)
        dx2_ref[...] = _rms_bwd(dy * gf_ref[...], xh, rf)

    row = lambda w: pl.BlockSpec((tm, w), lambda i: (i, 0))
    weight = _resident((D_FF, D_MODEL))
    return pl.pallas_call(
        body,
        name="ffn_fwd",
        grid=(seq // tm,),
        in_specs=[row(D_MODEL), row(D_MODEL), _resident((1, D_MODEL)), _resident((1, D_MODEL)), weight, weight, weight],
        out_specs=[row(D_MODEL), row(D_FF), row(D_FF), row(D_FF), row(D_MODEL),
                   pl.BlockSpec((1, LANES), lambda i: (0, 0)), pl.BlockSpec((1, D_MODEL), lambda i: (0, 0))],
        out_shape=[
            jax.ShapeDtypeStruct((seq, D_MODEL), BF16),
            jax.ShapeDtypeStruct((seq, D_FF), F32),
            jax.ShapeDtypeStruct((seq, D_FF), F32),
            jax.ShapeDtypeStruct((seq, D_FF), BF16),
            jax.ShapeDtypeStruct((seq, D_MODEL), F32),
            jax.ShapeDtypeStruct((1, LANES), F32),
            jax.ShapeDtypeStruct((1, D_MODEL), F32),
        ],
        compiler_params=_params(56),
    )(x1, target, g2, gf, wg_t, wu_t, wd)


def _ffn_bwd(dx2, gate, up, x1, g2, wg_t, wu_t, wd):
    seq = x1.shape[0]
    tm = min(seq, 256)

    def body(dx2_ref, gate_ref, up_ref, x1_ref, g2_ref, wg_ref, wu_ref, wd_ref, dgate_ref, dup_ref, dx1_ref, dg2_ref):
        @pl.when(pl.program_id(0) == 0)
        def _():
            dg2_ref[...] = jnp.zeros_like(dg2_ref)

        dx2v = dx2_ref[...]
        dact = _nt(dx2v.astype(BF16), wd_ref[...])
        gate = gate_ref[...]
        sg = _sigmoid(gate)
        dgate = (dact * up_ref[...] * (sg * (1.0 + gate * (1.0 - sg)))).astype(BF16)
        dup = (dact * (gate * sg)).astype(BF16)
        dgate_ref[...] = dgate
        dup_ref[...] = dup
        dh2 = _nn(dgate, wg_ref[...]) + _nn(dup, wu_ref[...])
        x1v = x1_ref[...]
        r2 = lax.rsqrt(jnp.mean(x1v * x1v, axis=-1, keepdims=True) + EPS)
        xh = x1v * r2
        dg2_ref[...] += jnp.sum(dh2 * xh, axis=0, keepdims=True)
        dx1_ref[...] = dx2v + _rms_bwd(dh2 * g2_ref[...], xh, r2)

    row = lambda w: pl.BlockSpec((tm, w), lambda i: (i, 0))
    weight = _resident((D_FF, D_MODEL))
    return pl.pallas_call(
        body,
        name="ffn_bwd",
        grid=(seq // tm,),
        in_specs=[row(D_MODEL), row(D_FF), row(D_FF), row(D_MODEL), _resident((1, D_MODEL)), weight, weight, weight],
        out_specs=[row(D_FF), row(D_FF), row(D_MODEL), pl.BlockSpec((1, D_MODEL), lambda i: (0, 0))],
        out_shape=[
            jax.ShapeDtypeStruct((seq, D_FF), BF16),
            jax.ShapeDtypeStruct((seq, D_FF), BF16),
            jax.ShapeDtypeStruct((seq, D_MODEL), F32),
            jax.ShapeDtypeStruct((1, D_MODEL), F32),
        ],
        compiler_params=_params(56),
    )(dx2, gate, up, x1, g2, wg_t, wu_t, wd)


WGRAD_ROWS = D_FF // 2


def _ffn_wgrad(h2, dgate, dup, act, dx2):
    seq = h2.shape[0]
    tm = min(seq, 512)

    def body(h2_ref, dgate_ref, dup_ref, act_ref, dx2_ref, dwg_ref, dwu_ref, dwd_ref):
        @pl.when(pl.program_id(1) == 0)
        def _():
            dwg_ref[...] = jnp.zeros_like(dwg_ref)
            dwu_ref[...] = jnp.zeros_like(dwu_ref)
            dwd_ref[...] = jnp.zeros_like(dwd_ref)

        h2v = h2_ref[...]
        dwg_ref[...] += _tn(dgate_ref[...], h2v)
        dwu_ref[...] += _tn(dup_ref[...], h2v)
        dwd_ref[...] += _tn(act_ref[...], dx2_ref[...].astype(BF16))

    ff = pl.BlockSpec((tm, WGRAD_ROWS), lambda j, i: (i, j))
    row = pl.BlockSpec((tm, D_MODEL), lambda j, i: (i, 0))
    out = pl.BlockSpec((WGRAD_ROWS, D_MODEL), lambda j, i: (j, 0))
    return pl.pallas_call(
        body,
        name="ffn_wgrad",
        grid=(D_FF // WGRAD_ROWS, seq // tm),
        in_specs=[row, ff, ff, ff, row],
        out_specs=[out, out, out],
        out_shape=[jax.ShapeDtypeStruct((D_FF, D_MODEL), F32)] * 3,
        compiler_params=_params(56, ("parallel", "arbitrary")),
    )(h2, dgate, dup, act, dx2)


def _mixer_bwd(dx1, ycat, o_f, o_b, p, gn, lng, lnb, ws_bf, wst_bf, bs_col, w_out, token):
    seq = dx1.shape[0]
    tm = min(seq, 512)
    nsteps = seq // tm

    def body(dx1_ref, yc_ref, of_ref, ob_ref, g_ref, u_ref, vv_ref, gn_ref, lng_ref, lnb_ref, ws_ref, wst_ref, bs_ref, wo_ref, token_ref,
             do_ref, dg_ref, du_ref, dvv_ref, dwo_ref, dgn_ref, dlng_ref, dlnb_ref, dws_ref, dbs_ref, vn_sc, dvn_sc, dbs_acc):
        step = pl.program_id(0)

        @pl.when(step == 0)
        def _():
            for r in (dwo_ref, dgn_ref, dlng_ref, dlnb_ref, dws_ref, dbs_acc):
                r[...] = jnp.zeros_like(r)

        dx1b = dx1_ref[...].astype(BF16)
        dyc = _nt(dx1b, wo_ref[...])
        dwo_ref[...] += _tn(yc_ref[...], dx1b)
        for h in range(GLA_HEADS):
            cols = pl.ds(h * GLA_DV, GLA_DV)
            dya = dyc[:, h * GLA_DV : (h + 1) * GLA_DV]
            oh = of_ref[:, cols] + ob_ref[:, cols]
            rn = lax.rsqrt(jnp.mean(oh * oh, axis=-1, keepdims=True) + EPS)
            on = oh * rn
            gh = g_ref[:, cols]
            sg = _sigmoid(gh)
            sil = gh * sg
            gnh = gn_ref[:, cols]
            dgn_ref[:, cols] += jnp.sum(dya * on * sil, axis=0, keepdims=True)
            dg_ref[:, cols] = (dya * on * gnh * (sg * (1.0 + gh * (1.0 - sg)))).astype(BF16)
            do_ref[:, cols] = _rms_bwd(dya * gnh * sil, on, rn)
        vv = vv_ref[...]
        zv, zv_grad = _gelu_and_grad(vv)
        xc = zv - jnp.mean(zv, axis=-1, keepdims=True)
        rstd = lax.rsqrt(jnp.mean(xc * xc, axis=-1, keepdims=True) + EPS)
        vhat = xc * rstd
        vn_sc[...] = (vhat * lng_ref[...] + lnb_ref[...]).astype(BF16)
        for c in range(tm // GMLP_CHUNK):
            rows = pl.ds(c * GMLP_CHUNK, GMLP_CHUNK)
            for g in range(GMLP_GROUPS):
                cols = pl.ds(g * LANES, LANES)
                vn = vn_sc[rows, cols]
                s = _nn(ws_ref[g], vn) + bs_ref[g]
                dyb = dyc[c * GMLP_CHUNK : (c + 1) * GMLP_CHUNK, GLA_W + g * LANES : GLA_W + (g + 1) * LANES]
                zu, zu_grad = _gelu_and_grad(u_ref[rows, cols])
                du_ref[rows, cols] = (dyb * s * zu_grad).astype(BF16)
                ds = dyb * zu
                dbs_acc[g] += ds
                dsb = ds.astype(BF16)
                dws_ref[g] += _nt(dsb, vn)
                dvn_sc[rows, cols] = _nn(wst_ref[g], dsb)
        dvn = dvn_sc[...]
        dlng_ref[...] += jnp.sum(dvn * vhat, axis=0, keepdims=True)
        dlnb_ref[...] += jnp.sum(dvn, axis=0, keepdims=True)
        dvh = dvn * lng_ref[...]
        dzv = rstd * (dvh - jnp.mean(dvh, axis=-1, keepdims=True) - vhat * jnp.mean(dvh * vhat, axis=-1, keepdims=True))
        dvv_ref[...] = (dzv * zv_grad).astype(BF16)

        @pl.when(step == nsteps - 1)
        def _():
            dbs_ref[...] = jnp.sum(dbs_acc[...], axis=-1, keepdims=True)

    row = lambda w: pl.BlockSpec((tm, w), lambda i: (i, 0))
    pcol = lambda col: pl.BlockSpec((tm, GLA_W), lambda i: (i, col // GLA_W))
    const = lambda shape: pl.BlockSpec(shape, lambda i: (0,) * len(shape))
    return pl.pallas_call(
        body,
        name="mixer_bwd",
        grid=(nsteps,),
        in_specs=[
            row(D_MODEL), row(D_MODEL), row(GLA_W), row(GLA_W), pcol(COL_G), pcol(COL_U), pcol(COL_VV),
            _resident((1, GLA_W)), _resident((1, GMLP_W)), _resident((1, GMLP_W)),
            _resident((GMLP_GROUPS, GMLP_CHUNK, GMLP_CHUNK)), _resident((GMLP_GROUPS, GMLP_CHUNK, GMLP_CHUNK)),
            _resident((GMLP_GROUPS, GMLP_CHUNK, 1)), _resident((D_MODEL, D_MODEL)), _resident(TOKEN_SHAPE),
        ],
        out_specs=[
            row(GLA_W), row(GLA_W), row(GMLP_W), row(GMLP_W), const((D_MODEL, D_MODEL)),
            const((1, GLA_W)), const((1, GMLP_W)), const((1, GMLP_W)),
            const((GMLP_GROUPS, GMLP_CHUNK, GMLP_CHUNK)), const((GMLP_GROUPS, GMLP_CHUNK, 1)),
        ],
        out_shape=[
            jax.ShapeDtypeStruct((seq, GLA_W), F32), jax.ShapeDtypeStruct((seq, GLA_W), BF16),
            jax.ShapeDtypeStruct((seq, GMLP_W), BF16), jax.ShapeDtypeStruct((seq, GMLP_W), BF16),
            jax.ShapeDtypeStruct((D_MODEL, D_MODEL), F32),
            jax.ShapeDtypeStruct((1, GLA_W), F32), jax.ShapeDtypeStruct((1, GMLP_W), F32), jax.ShapeDtypeStruct((1, GMLP_W), F32),
            jax.ShapeDtypeStruct((GMLP_GROUPS, GMLP_CHUNK, GMLP_CHUNK), F32), jax.ShapeDtypeStruct((GMLP_GROUPS, GMLP_CHUNK, 1), F32),
        ],
        scratch_shapes=[pltpu.VMEM((tm, GMLP_W), BF16), pltpu.VMEM((tm, GMLP_W), F32), pltpu.VMEM((GMLP_GROUPS, GMLP_CHUNK, GMLP_CHUNK), F32)],
        compiler_params=_params(56),
    )(dx1, ycat, o_f, o_b, p, p, p, gn, lng, lnb, ws_bf, wst_bf, bs_col, w_out, token)


def _gla_bwd(p, do, st, wd_pad, bd, token, reverse, other=None):
    seq = p.shape[0]
    tg = _gla_tile(seq)
    nt = seq // tg
    n = tg // GLA_CHUNK
    scale = GLA_DK**-0.5

    def tile(i):
        return i if reverse else nt - 1 - i

    def body(q_ref, k_ref, v_ref, lr_ref, do_ref, st_ref, wd_ref, bd_ref, token_ref, *rest):
        others, (dq_ref, dk_ref, dv_ref, dlr_ref, dwd_ref, dbd_ref, carry) = rest[:-7], rest[-7:]
        if others:
            odq_ref, odk_ref, odv_ref, odlr_ref = others

            def put(ref, idx, val, oref):
                ref[idx] = (val + oref[idx]).astype(BF16)
        else:
            odq_ref = odk_ref = odv_ref = odlr_ref = None

            def put(ref, idx, val, oref):
                ref[idx] = val

        @pl.when(pl.program_id(0) == 0)
        def _():
            carry[...] = jnp.zeros_like(carry)
            dwd_ref[...] = jnp.zeros_like(dwd_ref)
            dbd_ref[...] = jnp.zeros_like(dbd_ref)

        lr_bf = lr_ref[...].astype(BF16)
        carries = [carry[h] for h in range(GLA_HEADS)]
        row_in_chunk = lax.broadcasted_iota(jnp.int32, (tg, LANES), 0) % GLA_CHUNK
        lane_head = lax.broadcasted_iota(jnp.int32, (1, LANES), 1) // GLA_DK
        tt = lax.broadcasted_iota(jnp.int32, (GLA_CHUNK, GLA_CHUNK), 0)
        ss = lax.broadcasted_iota(jnp.int32, (GLA_CHUNK, GLA_CHUNK), 1)
        causal = (tt <= ss) if reverse else (tt >= ss)
        order = range(n) if reverse else range(n - 1, -1, -1)
        dlr = jnp.zeros((tg, LANES), F32)
        heads = range(GLA_HEADS)
        pv, masks, qdh, vhs, dohs, stbs = {}, {}, {}, {}, {}, {}
        sc_raw, dp, acc = {}, {}, {}
        for pair in range(2):
            cols = pl.ds(pair * LANES, LANES)
            pre, b3, blast = _gla_decay_terms(lr_bf, wd_ref, bd_ref, pair, row_in_chunk, reverse, n)
            q3 = q_ref[:, cols].reshape(n, GLA_CHUNK, LANES) * scale
            k3 = k_ref[:, cols].reshape(n, GLA_CHUNK, LANES)
            eb = jnp.exp(b3)
            emb = jnp.exp(-b3)
            ekte = jnp.exp(blast - b3)
            kdf = k3 * emb
            kte = k3 * ekte
            both = pl.ds(2 * pair * GLA_DV, 2 * GLA_DV)
            pv[pair] = dict(pre=pre, eb=eb, emb=emb, ekte=ekte, qd=q3 * eb, kdf=kdf, kd=kdf.astype(BF16), kte=kte, kte_bf=kte.astype(BF16),
                            dec=jnp.exp(blast), v=v_ref[:, both].reshape(n, GLA_CHUNK, 2 * GLA_DV).astype(BF16),
                            do=do_ref[:, both].reshape(n, GLA_CHUNK, 2 * GLA_DV).astype(BF16))
            for hh in range(2):
                h = 2 * pair + hh
                masks[h] = (lane_head == hh).astype(F32)
                qdh[h] = (pv[pair]["qd"] * masks[h]).astype(BF16)
                vhs[h] = pv[pair]["v"][:, :, hh * GLA_DV : (hh + 1) * GLA_DV]
                dohs[h] = pv[pair]["do"][:, :, hh * GLA_DV : (hh + 1) * GLA_DV]
                stbs[h] = st_ref[:, h]
                dp[h] = _bnt(dohs[h], vhs[h])
                acc[h] = _btn(dohs[h], qdh[h])
            sc_both = _bnt(jnp.concatenate([qdh[2 * pair], qdh[2 * pair + 1]], axis=1), pv[pair]["kd"])
            for hh in range(2):
                sc_raw[2 * pair + hh] = sc_both[:, hh * GLA_CHUNK : (hh + 1) * GLA_CHUNK, :]
        dsa, sc = {}, {}
        for h in heads:
            sc[h] = jnp.where(causal, sc_raw[h], 0.0).astype(BF16)
            dp[h] = jnp.where(causal, dp[h], 0.0).astype(BF16)
            dec = pv[h // 2]["dec"]
            c, after = carries[h], [None] * n
            for j in order:
                after[j] = c
                c = acc[h][j] + dec[j] * c
            carries[h] = c
            dsa[h] = jnp.stack(after)
        dvs, dqs, dks, dwds, dbds = [], [], [], [], []
        for pair in range(2):
            cols = pl.ds(pair * LANES, LANES)
            v = pv[pair]
            h0, h1 = 2 * pair, 2 * pair + 1
            dsa_both = jnp.concatenate([dsa[h0], dsa[h1]], axis=1)
            dsa_bf = dsa_both.astype(BF16)
            stb_bf = jnp.concatenate([stbs[h0], stbs[h1]], axis=1)
            dq_intra = _bnn(jnp.concatenate([dp[h0], dp[h1]], axis=1), v["kd"])
            dqd = (dq_intra[:, :GLA_CHUNK, :] * masks[h0] + dq_intra[:, GLA_CHUNK:, :] * masks[h1]) + _bnn(v["do"], stb_bf)
            dkd = _btn(dp[h0], qdh[h0]) + _btn(dp[h1], qdh[h1])
            dkte = _bnn(v["v"], dsa_bf)
            ddec = jnp.sum(dsa[h0] * stbs[h0].astype(F32) + dsa[h1] * stbs[h1].astype(F32), axis=1, keepdims=True)
            dv_inter = _bnt(v["kte_bf"], dsa_bf)
            for hh, h in ((0, h0), (1, h1)):
                dvs.append((_btn(sc[h], dohs[h]) + dv_inter[:, :, hh * GLA_DV : (hh + 1) * GLA_DV]).reshape(tg, GLA_DV))
            dqs.append((dqd * (scale * v["eb"])).reshape(tg, LANES))
            dks.append((dkd * v["emb"] + dkte * v["ekte"]).reshape(tg, LANES))
            db = dqd * v["qd"] - dkd * v["kdf"] - dkte * v["kte"]
            dblast = jnp.sum(dkte * v["kte"], axis=1, keepdims=True) + ddec * v["dec"]
            dla = _chunk_cumsum(db.reshape(tg, LANES), row_in_chunk, not reverse) + jnp.broadcast_to(dblast, (n, GLA_CHUNK, LANES)).reshape(tg, LANES)
            dpre = (dla * (1.0 / GLA_TAU) * _sigmoid(-v["pre"]))
            dpre_bf = dpre.astype(BF16)
            dlr = dlr + _nt(dpre_bf, wd_ref[:, cols])
            dwds.append(_tn(lr_bf, dpre_bf))
            dbds.append(jnp.sum(dpre, axis=0, keepdims=True))
        put(dlr_ref, (slice(None), slice(None)), dlr, odlr_ref)
        for pair in range(2):
            cols = pl.ds(pair * LANES, LANES)
            put(dq_ref, (slice(None), cols), dqs[pair], odq_ref)
            put(dk_ref, (slice(None), cols), dks[pair], odk_ref)
            dwd_ref[:, cols] += dwds[pair]
            dbd_ref[:, cols] += dbds[pair]
        for h in range(GLA_HEADS):
            put(dv_ref, (slice(None), pl.ds(h * GLA_DV, GLA_DV)), dvs[h], odv_ref)
            carry[h] = carries[h]

    pieces = [
        pl.BlockSpec((tg, KEY_W), lambda i: (tile(i), 0)),
        pl.BlockSpec((tg, KEY_W), lambda i: (tile(i), 0)),
        pl.BlockSpec((tg, GLA_W), lambda i: (tile(i), 0)),
        pl.BlockSpec((tg, LANES), lambda i: (tile(i), 0)),
    ]
    piece_dtype = BF16 if other else F32
    return pl.pallas_call(
        body,
        name="gla_bwd_rev" if reverse else "gla_bwd",
        grid=(nt,),
        in_specs=[
            pl.BlockSpec((tg, KEY_W), lambda i: (tile(i), COL_Q // KEY_W)),
            pl.BlockSpec((tg, KEY_W), lambda i: (tile(i), COL_K // KEY_W)),
            pl.BlockSpec((tg, GLA_W), lambda i: (tile(i), COL_V // GLA_W)),
            pl.BlockSpec((tg, LANES), lambda i: (tile(i), COL_LR // LANES)),
            pl.BlockSpec((tg, GLA_W), lambda i: (tile(i), 0)),
            pl.BlockSpec((n, GLA_HEADS, GLA_DV, LANES), lambda i: (tile(i), 0, 0, 0)),
            _resident((LANES, KEY_W)),
            _resident((1, KEY_W)),
            _resident(TOKEN_SHAPE),
        ] + (pieces if other else []),
        out_specs=pieces + [pl.BlockSpec((LANES, KEY_W), lambda i: (0, 0)), pl.BlockSpec((1, KEY_W), lambda i: (0, 0))],
        out_shape=[
            jax.ShapeDtypeStruct((seq, KEY_W), piece_dtype), jax.ShapeDtypeStruct((seq, KEY_W), piece_dtype),
            jax.ShapeDtypeStruct((seq, GLA_W), piece_dtype), jax.ShapeDtypeStruct((seq, LANES), piece_dtype),
            jax.ShapeDtypeStruct((LANES, KEY_W), F32), jax.ShapeDtypeStruct((1, KEY_W), F32),
        ],
        scratch_shapes=[pltpu.VMEM((GLA_HEADS, GLA_DV, LANES), F32)],
        compiler_params=_params(56),
    )(p, p, p, p, do, st, wd_pad, bd, token, *(other or ()))


def _inproj_wgrad(x, g1, dq, dk, dv, dg, du, dvv, dlr):
    seq = x.shape[0]
    tm = min(seq, 512)

    def body(x_ref, g1_ref, dq_ref, dk_ref, dv_ref, dg_ref, du_ref, dvv_ref, dlr_ref, dw_ref, dp_ref):
        @pl.when(pl.program_id(0) == 0)
        def _():
            dw_ref[...] = jnp.zeros_like(dw_ref)

        for col, ref in ((COL_Q, dq_ref), (COL_K, dk_ref), (COL_V, dv_ref), (COL_G, dg_ref), (COL_U, du_ref), (COL_VV, dvv_ref), (COL_LR, dlr_ref)):
            dp_ref[:, col : col + ref.shape[1]] = ref[...]
        xv = x_ref[...]
        h = (xv * lax.rsqrt(jnp.mean(xv * xv, axis=-1, keepdims=True) + EPS) * g1_ref[...]).astype(BF16)
        dw_ref[0:ROW_LR, :] += _tn(dp_ref[:, 0:COL_U], h)
        dw_ref[ROW_UV:PROJ_W, :] += _tn(dp_ref[:, COL_U:COL_LR], h)
        dw_ref[ROW_LR:ROW_UV, :] += _tn(dp_ref[:, COL_LR:PROJ_WP], h)[0 : ROW_UV - ROW_LR]

    row = lambda w: pl.BlockSpec((tm, w), lambda i: (i, 0))
    return pl.pallas_call(
        body,
        name="inproj_wgrad",
        grid=(seq // tm,),
        in_specs=[row(D_MODEL), _resident((1, D_MODEL)), row(KEY_W), row(KEY_W), row(GLA_W), row(GLA_W), row(GMLP_W), row(GMLP_W), row(LANES)],
        out_specs=[pl.BlockSpec((PROJ_W, D_MODEL), lambda i: (0, 0)), row(PROJ_WP)],
        out_shape=[jax.ShapeDtypeStruct((PROJ_W, D_MODEL), F32), jax.ShapeDtypeStruct((seq, PROJ_WP), BF16)],
        compiler_params=_params(56),
    )(x, g1, dq, dk, dv, dg, du, dvv, dlr)


def _inproj_dx(x, dx1, g1, w_in_t, dp, token):
    seq = x.shape[0]
    tm = min(seq, 512)

    def body(x_ref, dx1_ref, g1_ref, w_ref, dp_ref, token_ref, dx_ref, dg1_ref):
        @pl.when(pl.program_id(0) == 0)
        def _():
            dg1_ref[...] = jnp.zeros_like(dg1_ref)

        xv = x_ref[...]
        r1 = lax.rsqrt(jnp.mean(xv * xv, axis=-1, keepdims=True) + EPS)
        xh = xv * r1
        dh = (_nn(dp_ref[:, 0:COL_U], w_ref[0:ROW_LR, :]) + _nn(dp_ref[:, COL_U:COL_LR], w_ref[ROW_UV:PROJ_W, :])
              + _nn(dp_ref[:, COL_LR:PROJ_WP], w_ref[ROW_LR : ROW_LR + LANES, :]))
        dg1_ref[...] += jnp.sum(dh * xh, axis=0, keepdims=True)
        dx_ref[...] = dx1_ref[...] + _rms_bwd(dh * g1_ref[...], xh, r1)

    row = lambda w: pl.BlockSpec((tm, w), lambda i: (i, 0))
    return pl.pallas_call(
        body,
        name="inproj_dx",
        grid=(seq // tm,),
        in_specs=[row(D_MODEL), row(D_MODEL), _resident((1, D_MODEL)), _resident((PROJ_W, D_MODEL)), row(PROJ_WP), _resident(TOKEN_SHAPE)],
        out_specs=[row(D_MODEL), pl.BlockSpec((1, D_MODEL), lambda i: (0, 0))],
        out_shape=[jax.ShapeDtypeStruct((seq, D_MODEL), F32), jax.ShapeDtypeStruct((1, D_MODEL), F32)],
        compiler_params=_params(48),
    )(x, dx1, g1, w_in_t, dp, token)


def _in_hbm(a):
    return pltpu.with_memory_space_constraint(a, pltpu.HBM)


def _row_tile(rows, multiple=8):
    for t in range(min(rows, 512), 0, -1):
        if rows % t == 0 and t % multiple == 0:
            return t
    return rows


def _cast_into_slot(w, shard, token):
    rows, cols = w.shape
    tr = _row_tile(rows, 16)

    def body(s_ref, w_ref, token_ref, o_ref):
        o_ref[...] = w_ref[...].astype(BF16)

    return pl.pallas_call(
        body,
        name="cast_into_slot",
        grid_spec=pltpu.PrefetchScalarGridSpec(
            num_scalar_prefetch=1,
            grid=(rows // tr,),
            in_specs=[pl.BlockSpec((tr, cols), lambda i, s_ref: (i, 0)), pl.BlockSpec(TOKEN_SHAPE, lambda i, s_ref: (0, 0))],
            out_specs=pl.BlockSpec((None, tr, cols), lambda i, s_ref: (s_ref[0], i, 0)),
        ),
        out_shape=pltpu.HBM((N_SHARDS, rows, cols), BF16),
        compiler_params=_params(32, ("parallel",)),
    )(shard, _in_hbm(w), token)


def _add_halves(grads4, recvs, shard_core):
    n = len(grads4)
    _, rows, _ = grads4[0].shape
    tr = _row_tile(rows, 16)

    def body(sc_ref, *refs):
        for k in range(n):
            total = refs[k][...] + refs[n + k][...]
            refs[3 * n + k][...] = total.astype(BF16)

            @pl.when(pl.program_id(1) == sc_ref[0])
            def _(k=k, total=total):
                refs[2 * n + k][...] = total

    theirs = pl.BlockSpec((None, tr, HALF), lambda i, s, sc_ref: (s, i, 0))
    mine = pl.BlockSpec((None, tr, HALF), lambda i, s, sc_ref: (s, i, sc_ref[1]))
    kept = pl.BlockSpec((tr, HALF), lambda i, s, sc_ref: (i, 0))
    outs = pl.pallas_call(
        body,
        name="add_halves",
        grid_spec=pltpu.PrefetchScalarGridSpec(
            num_scalar_prefetch=1,
            grid=(rows // tr, N_SHARDS),
            in_specs=[mine] * n + [theirs] * n,
            out_specs=[kept] * n + [theirs] * n,
        ),
        out_shape=[pltpu.HBM((rows, HALF), F32)] * n + [pltpu.HBM((N_SHARDS, rows, HALF), BF16)] * n,
        compiler_params=_params(48, ("parallel", "arbitrary")),
    )(shard_core, *[_in_hbm(a) for a in list(grads4) + list(recvs)])
    return list(zip(outs[:n], outs[n:]))


def _add_partials(part, recv3, shard_core, token):
    rows, _ = part.shape
    tr = _row_tile(rows, 16)

    def body(sc_ref, p_ref, r_ref, token_ref, o_ref):
        o_ref[...] = ((p_ref[...] + r_ref[0].astype(F32)) + r_ref[1].astype(F32)) + r_ref[2].astype(F32)

    return pl.pallas_call(
        body,
        name="add_partials",
        grid_spec=pltpu.PrefetchScalarGridSpec(
            num_scalar_prefetch=1,
            grid=(rows // tr,),
            in_specs=[
                pl.BlockSpec((tr, HALF), lambda i, sc_ref: (i, 0)),
                pl.BlockSpec((3, tr, HALF), lambda i, sc_ref: (0, i, 0)),
                pl.BlockSpec(TOKEN_SHAPE, lambda i, sc_ref: (0, 0)),
            ],
            out_specs=pl.BlockSpec((tr, HALF), lambda i, sc_ref: (i, sc_ref[1])),
        ),
        out_shape=pltpu.HBM((rows, 2 * HALF), F32),
        compiler_params=_params(32, ("parallel",)),
    )(shard_core, _in_hbm(part), _in_hbm(recv3), token)


def _adam_math(w, g, m, v):
    m = ADAM_B1 * m + (1.0 - ADAM_B1) * g
    v = ADAM_B2 * v + (1.0 - ADAM_B2) * (g * g)
    m_hat = m / (1.0 - ADAM_B1**ADAM_STEP)
    v_hat = v / (1.0 - ADAM_B2**ADAM_STEP)
    delta = -ADAM_LR * (m_hat / (jnp.sqrt(v_hat) + ADAM_EPS) + ADAM_WD * w)
    return delta, m, v


def _adamw(w, g, m, v):
    rows, cols = w.shape
    tr = _row_tile(rows)

    def body(w_ref, g_ref, m_ref, v_ref, go_ref, d_ref, mo_ref, vo_ref):
        gv = g_ref[...]
        go_ref[...] = gv
        d_ref[...], mo_ref[...], vo_ref[...] = _adam_math(w_ref[...], gv, m_ref[...], v_ref[...])

    spec = pl.BlockSpec((tr, cols), lambda i: (i, 0))
    return pl.pallas_call(
        body, name="adamw", grid=(rows // tr,), in_specs=[spec] * 4, out_specs=[spec] * 4, out_shape=[pltpu.HBM(w.shape, F32)] * 4,
        compiler_params=_params(32, ("parallel",)),
    )(_in_hbm(w), _in_hbm(g), _in_hbm(m), _in_hbm(v))


SMALL_ROWS = 560
DECAY_ROWS = 8
SMALL_TOTAL = SMALL_ROWS + 2 * N_SHARDS * DECAY_ROWS


def _adamw_small(gathered, own, wp, mp, vp, like):
    out_rows = SMALL_ROWS + 2 * DECAY_ROWS
    places, off = [], 0
    for a in like:
        rows = a.size // LANES
        kept = a.shape[-1] == LANES
        places.append((off, rows, kept, (rows, LANES) if kept else (1, a.size)))
        off += rows
    loss_row = off
    decay_shape = (LOWRANK, KEY_W // N_SHARDS)
    n = len(places) + 2

    def body(ga_ref, own_ref, w_ref, m_ref, v_ref, *refs):
        outs, loss_ref, packed = refs[: 4 * n], refs[4 * n], refs[4 * n + 1 :]
        g_sc = packed[0]
        x, y, c = _position()
        shard, me = 2 * x + y, 4 * x + 2 * y + c
        total = lambda rows: functools.reduce(lambda a, b: a + b, [jnp.where(me == d, own_ref[rows, :], ga_ref[d, rows, :]) for d in range(8)])
        g_sc[pl.ds(0, SMALL_ROWS), :] = total(pl.ds(0, SMALL_ROWS))
        for k in range(2):
            start = pl.multiple_of(SMALL_ROWS + k * N_SHARDS * DECAY_ROWS + shard * DECAY_ROWS, DECAY_ROWS)
            g_sc[pl.ds(SMALL_ROWS + k * DECAY_ROWS, DECAY_ROWS), :] = total(pl.ds(start, DECAY_ROWS))
        packed[1][...], packed[2][...], packed[3][...] = _adam_math(w_ref[...], g_sc[...], m_ref[...], v_ref[...])
        loss_ref[...] = g_sc[loss_row : loss_row + 1, :]
        for t, res in enumerate(packed):
            for (at, rows, kept, _), out in zip(places, outs[t * n :]):
                if kept:
                    out[...] = res[at : at + rows, :]
                else:
                    for r in range(rows):
                        out[:, r * LANES : (r + 1) * LANES] = res[at + r : at + r + 1, :]
            for k in range(2):
                out = outs[t * n + len(places) + k]
                both = res[SMALL_ROWS + k * DECAY_ROWS : SMALL_ROWS + (k + 1) * DECAY_ROWS, :]
                halves = (both, pltpu.roll(both, LANES // 2, axis=1))
                for r in range(DECAY_ROWS):
                    for h in range(2):
                        out[2 * r + h : 2 * r + h + 1, :] = halves[h][r : r + 1, 0 : LANES // 2]

    shapes = [jax.ShapeDtypeStruct(s, F32) for *_, s in places] + [jax.ShapeDtypeStruct(decay_shape, F32)] * 2
    out = pl.pallas_call(
        body,
        name="adamw_small",
        out_shape=shapes * 4 + [jax.ShapeDtypeStruct((1, LANES), F32)],
        scratch_shapes=[pltpu.VMEM((out_rows, LANES), F32)] * 4,
        compiler_params=_params(32, None),
    )(gathered, own, wp, mp, vp)
    return [list(out[t * n : (t + 1) * n]) for t in range(4)], out[4 * n]


ANY = pl.BlockSpec(memory_space=pl.ANY)


def _position():
    return lax.axis_index("x"), lax.axis_index("y"), lax.axis_index("c")


def _other_chips(x, y):
    return [(1 - x, y), (x, 1 - y), (1 - x, 1 - y)]


HBM = pl.BlockSpec(memory_space=pltpu.HBM)
SEM = pl.BlockSpec(memory_space=pltpu.SEMAPHORE)
TOKEN = jax.ShapeDtypeStruct(TOKEN_SHAPE, F32)
DATAFLOW = pltpu.SideEffectType.DATAFLOW_SIDE_EFFECTING


def _half_block(ref4, slot, core):
    return ref4.at[slot, :, pl.ds(pl.multiple_of(core * HALF, HALF), HALF)]


def _gather_ici_copies(bufs, lands, send_sems, recv_sems):
    x, y, c = _position()
    pairs = []
    for k, ref4 in enumerate(bufs):
        mine = _half_block(ref4, 2 * x + y, c)
        for j, (px, py) in enumerate(_other_chips(x, y)):
            sems = dict(send_sem=send_sems.at[3 * k + j], recv_sem=recv_sems.at[3 * k + j], device_id=(px, py, c), device_id_type=MESH)
            pairs.append((functools.partial(pltpu.make_async_remote_copy, src_ref=mine, dst_ref=mine, **sems),
                          functools.partial(pltpu.make_async_remote_copy, src_ref=mine, dst_ref=_half_block(ref4, 2 * px + py, c), **sems)))
    return pairs


def _gather_d2d_copies(bufs, lands, send_sems, recv_sems):
    x, y, c = _position()
    pairs = []
    for k, ref4 in enumerate(bufs):
        for j, (px, py) in enumerate(_other_chips(x, y)):
            have = _half_block(ref4, 2 * px + py, c)
            sems = dict(send_sem=send_sems.at[3 * k + j], recv_sem=recv_sems.at[3 * k + j], device_id=(x, y, 1 - c), device_id_type=MESH)
            pairs.append((functools.partial(pltpu.make_async_remote_copy, src_ref=have, dst_ref=have, **sems),
                          functools.partial(pltpu.make_async_remote_copy, src_ref=have, dst_ref=_half_block(ref4, 2 * px + py, 1 - c), **sems)))
    return pairs


def _gather_forward(bufs):
    n = len(bufs)

    def body(*refs):
        outs = refs[n : 2 * n]
        send_sems, recv_sems = refs[2 * n :]
        d2d = _gather_d2d_copies(outs, (), send_sems, recv_sems)
        for forward, _ in d2d:
            forward().start()
        for forward, arrival in d2d:
            arrival().wait_recv()
            forward().wait_send()

    return pl.pallas_call(
        body,
        name="gather_forward",
        in_specs=[ANY] * n,
        out_specs=[ANY] * n,
        out_shape=[jax.ShapeDtypeStruct(b.shape, b.dtype) for b in bufs],
        input_output_aliases={k: k for k in range(n)},
        scratch_shapes=[pltpu.SemaphoreType.DMA((3 * n,)), pltpu.SemaphoreType.DMA((3 * n,))],
        compiler_params=pltpu.CompilerParams(has_side_effects=True),
    )(*bufs)


def _both_ends(**copy):
    maker = functools.partial(pltpu.make_async_remote_copy, **copy)
    return maker, maker


def _scatter_copies(parts, lands, send_sems, recv_sems):
    x, y, c = _position()
    return [_both_ends(src_ref=parts[k].at[2 * px + py], dst_ref=lands[k].at[j], send_sem=send_sems.at[3 * k + j],
                       recv_sem=recv_sems.at[3 * k + j], device_id=(px, py, c), device_id_type=MESH)
            for k in range(len(parts)) for j, (px, py) in enumerate(_other_chips(x, y))]


def _exchange_copies(grads, lands, send_sems, recv_sems):
    x, y, c = _position()
    return [_both_ends(src_ref=grads[k].at[:, :, pl.ds(pl.multiple_of((1 - c) * HALF, HALF), HALF)], dst_ref=lands[k],
                       send_sem=send_sems.at[k], recv_sem=recv_sems.at[k], device_id=(x, y, 1 - c), device_id_type=MESH)
            for k in range(len(grads))]


def _exchange_lands(grads4):
    return [jax.ShapeDtypeStruct((N_SHARDS, g.shape[1], HALF), g.dtype) for g in grads4]


def _scatter_lands(parts4):
    return [jax.ShapeDtypeStruct((3,) + g.shape[1:], g.dtype) for g in parts4]


def _small_gather_copies(blocks, lands, send_sems, recv_sems):
    x, y, c = _position()
    flip = lambda v, bit: 1 - v if bit else v
    return [_both_ends(src_ref=blocks[0], dst_ref=lands[0].at[4 * x + 2 * y + c], send_sem=send_sems.at[r - 1],
                       recv_sem=recv_sems.at[r - 1], device_id=(flip(x, r & 4), flip(y, r & 2), flip(c, r & 1)), device_id_type=MESH)
            for r in range(1, 8)]


def _split_start(name, srcs, land_shapes, make_copies, nsem, after=()):
    n, nl, na = len(srcs), len(land_shapes), len(after)
    lands = [lax.empty(a.shape, a.dtype) for a in land_shapes]

    def body(*refs):
        send_sems, recv_sems = refs[n + nl + na], refs[n + nl + na + 1]
        token = refs[2 * (n + nl) + na + 2]
        for send, _ in make_copies(refs[:n], refs[n : n + nl], send_sems, recv_sems):
            send().start()
        token[...] = jnp.zeros_like(token)

    hbm = lambda a: pltpu.HBM(a.shape, a.dtype)
    out = pl.pallas_call(
        body,
        name=name,
        in_specs=[HBM] * (n + nl) + [ANY] * na,
        out_specs=(SEM, SEM, *[HBM] * (n + nl), pl.BlockSpec(memory_space=pltpu.VMEM)),
        out_shape=(pltpu.SemaphoreType.DMA((nsem,)), pltpu.SemaphoreType.DMA((nsem,)), *[hbm(a) for a in list(srcs) + lands], TOKEN),
        input_output_aliases={k: 2 + k for k in range(n + nl)},
        compiler_params=pltpu.CompilerParams(has_side_effects=DATAFLOW),
    )(*[pltpu.with_memory_space_constraint(a, pltpu.HBM) for a in list(srcs) + lands], *after)
    return out[0], out[1], list(out[2 : 2 + n]), list(out[2 + n : 2 + n + nl]), out[2 + n + nl]


def _split_wait(name, send_sems, recv_sems, srcs, lands, make_copies, after):
    n, nl = len(srcs), len(lands)

    def body(*refs):
        for send, arrival in make_copies(refs[:n], refs[n : n + nl], refs[n + nl], refs[n + nl + 1]):
            send().wait_send()
            arrival().wait_recv()

    hbm = lambda a: pltpu.HBM(a.shape, a.dtype)
    out = pl.pallas_call(
        body,
        name=name,
        in_specs=[HBM] * (n + nl) + [SEM, SEM] + [ANY] * len(after),
        out_specs=tuple([HBM] * (n + nl)),
        out_shape=tuple(hbm(a) for a in list(srcs) + list(lands)),
        input_output_aliases={k: k for k in range(n + nl)},
        compiler_params=pltpu.CompilerParams(has_side_effects=DATAFLOW),
    )(*srcs, *lands, send_sems, recv_sems, *after)
    return list(out[:n]), list(out[n:])


def _join_copies(bufs, lands, send_sems, recv_sems, first=0):
    x, y, c = _position()
    half = lambda ref, core: ref.at[:, pl.ds(pl.multiple_of(core * HALF, HALF), HALF)]
    pairs = []
    for k, ref in enumerate(bufs):
        sems = dict(send_sem=send_sems.at[first + k], recv_sem=recv_sems.at[first + k], device_id=(x, y, 1 - c), device_id_type=MESH)
        pairs.append((functools.partial(pltpu.make_async_remote_copy, src_ref=half(ref, c), dst_ref=half(ref, c), **sems),
                      functools.partial(pltpu.make_async_remote_copy, src_ref=half(ref, c), dst_ref=half(ref, 1 - c), **sems)))
    return pairs


def _allgather_small(block):
    m_per, ncol = block.shape

    def body(x_ref, out_ref, send_sems, recv_sems, local_sem):
        x, y, c = _position()
        me, sibling = (x, y, c), (x, y, 1 - c)
        chips = _other_chips(x, y)

        def rows(px, py, pc):
            return out_ref.at[4 * px + 2 * py + pc]

        def copy(k, blk, to, src=None):
            return pltpu.make_async_remote_copy(
                src_ref=rows(*blk) if src is None else src, dst_ref=rows(*blk),
                send_sem=send_sems.at[k], recv_sem=recv_sems.at[k], device_id=to, device_id_type=MESH)

        mine = pltpu.make_async_copy(x_ref, rows(*me), local_sem)
        mine.start()
        first = [copy(0, me, sibling, src=x_ref)] + [copy(1 + j, me, (*chip, c), src=x_ref) for j, chip in enumerate(chips)]
        for cp in first:
            cp.start()
        passed = [copy(4 + j, (*chip, c), sibling) for j, chip in enumerate(chips)]
        for j, chip in enumerate(chips):
            copy(1 + j, (*chip, c), me).wait_recv()
            passed[j].start()
        copy(0, sibling, me).wait_recv()
        for j, chip in enumerate(chips):
            copy(4 + j, (*chip, 1 - c), me).wait_recv()
        for cp in first + passed:
            cp.wait_send()
        mine.wait()

    return pl.pallas_call(
        body,
        name="allgather_small",
        in_specs=[pl.BlockSpec(memory_space=pltpu.VMEM)],
        out_specs=pl.BlockSpec(memory_space=pltpu.VMEM),
        out_shape=jax.ShapeDtypeStruct((8, m_per, ncol), block.dtype),
        scratch_shapes=[pltpu.SemaphoreType.DMA((7,)), pltpu.SemaphoreType.DMA((7,)), pltpu.SemaphoreType.DMA],
        compiler_params=pltpu.CompilerParams(has_side_effects=True, vmem_limit_bytes=32 * MIB),
    )(block)


SMALL_NAMES = ["norm1_g", "b_decay_f", "b_decay_b", "gla_norm_g", "gmlp_ln_g", "gmlp_ln_b", "w_spatial", "b_spatial", "norm2_g", "final_norm_g"]


def _pack_small(parts, decay_parts):
    flat = jnp.concatenate([a.reshape(-1) for a in parts])
    flat = jnp.pad(flat, (0, SMALL_ROWS * LANES - flat.shape[0])).reshape(SMALL_ROWS, LANES)
    return jnp.concatenate([flat] + [d.reshape(-1, LANES) for d in decay_parts], axis=0)


def kernel(x, norm1_g, w_in, w_decay_f, b_decay_f, w_decay_b, b_decay_b, gla_norm_g, gmlp_ln_g, gmlp_ln_b, w_spatial, b_spatial, w_out, norm2_g, w_gate, w_up, w_down, final_norm_g, loss_target, m_norm1_g, m_w_in, m_w_decay_f, m_b_decay_f, m_w_decay_b, m_b_decay_b, m_gla_norm_g, m_gmlp_ln_g, m_gmlp_ln_b, m_w_spatial, m_b_spatial, m_w_out, m_norm2_g, m_w_gate, m_w_up, m_w_down, m_final_norm_g, v_norm1_g, v_w_in, v_w_decay_f, v_b_decay_f, v_w_decay_b, v_b_decay_b, v_gla_norm_g, v_gmlp_ln_g, v_gmlp_ln_b, v_w_spatial, v_b_spatial, v_w_out, v_norm2_g, v_w_gate, v_w_up, v_w_down, v_final_norm_g):
    args = dict(locals())
    cx, cy, cc = lax.axis_index("x"), lax.axis_index("y"), lax.axis_index("c")
    shard = 2 * cx + cy
    xs = x[0]
    target = loss_target[0]

    big_names = ["w_in", "w_out", "w_gate", "w_up", "w_down"]
    transposed = ("w_in", "w_gate", "w_up")
    rows_of = lambda pre, k: jnp.transpose(args[pre + k][0]) if k in transposed else args[pre + k][0]
    big_shards = {k: rows_of("", k) for k in big_names}
    s_arr = shard.reshape(1).astype(jnp.int32)
    sc_arr = jnp.stack([shard, cc]).astype(jnp.int32)
    zero_token = jnp.zeros(TOKEN_SHAPE, F32)
    w_send, w_recv, (w_in4,), _, token_w_in = _split_start(
        "w_in_gather_start", [_cast_into_slot(big_shards["w_in"], s_arr, zero_token)], [], _gather_ici_copies, 3)
    late = ["w_out", "w_gate", "w_up", "w_down"]
    late_slots = [_cast_into_slot(big_shards[k], s_arr, token_w_in) for k in late]
    dec_block = jnp.concatenate([w_decay_f[0].reshape(-1, LANES), w_decay_b[0].reshape(-1, LANES)], axis=0)
    dec_all = _allgather_small(dec_block)
    (w_in4,), _ = _split_wait("w_in_gather_wait", w_send, w_recv, [w_in4], [], _gather_ici_copies, (dec_all, *late_slots))
    (w_in4,) = _gather_forward([w_in4])
    w_in_t = w_in4.reshape(PROJ_W, D_MODEL)
    g_send, g_recv, late_bufs, _, token_gather = _split_start(
        "gather_start", late_slots, [], _gather_ici_copies, 3 * len(late), after=(w_in4,))
    dec_all = dec_all[::2].reshape(N_SHARDS, 2, LOWRANK, KEY_W // N_SHARDS)
    wdf_full = jnp.transpose(dec_all[:, 0], (1, 0, 2)).reshape(LOWRANK, KEY_W)
    wdb_full = jnp.transpose(dec_all[:, 1], (1, 0, 2)).reshape(LOWRANK, KEY_W)
    wd_pad_f = jnp.zeros((LANES, KEY_W), F32).at[0:LOWRANK].set(wdf_full).astype(BF16)
    wd_pad_b = jnp.zeros((LANES, KEY_W), F32).at[LOWRANK : 2 * LOWRANK].set(wdb_full).astype(BF16)

    ws_bf = w_spatial[0].astype(BF16)
    wst_bf = jnp.transpose(w_spatial[0], (0, 2, 1)).astype(BF16)
    bs_col = b_spatial[0].reshape(GMLP_GROUPS, GMLP_CHUNK, 1)

    p = _inproj(xs, norm1_g, w_in_t, token_gather)
    o_f, st_f = _gla_fwd(p, wd_pad_f, b_decay_f, token_gather, reverse=False)
    o_b, st_b = _gla_fwd(p, wd_pad_b, b_decay_b, token_gather, reverse=True)
    late_bufs, _ = _split_wait("gather_wait", g_send, g_recv, late_bufs, [], _gather_ici_copies, (o_f, o_b))
    (w_out4,) = _gather_forward(late_bufs[:1])
    f_send, f_recv, ffn_bufs, _, token_forward = _split_start(
        "forward_start", late_bufs[1:], [], _gather_d2d_copies, 3 * (len(late) - 1), after=(w_out4,))
    w_out_full = w_out4.reshape(-1, D_MODEL)
    x1, ycat = _mixer_out(xs, o_f, o_b, p, gla_norm_g, gmlp_ln_g, gmlp_ln_b, ws_bf, bs_col, w_out_full, token_forward)
    ffn_bufs, _ = _split_wait("forward_wait", f_send, f_recv, ffn_bufs, [], _gather_d2d_copies, (x1,))
    wg_t, wu_t, wd = [b.reshape(-1, D_MODEL) for b in ffn_bufs]
    gf = final_norm_g.reshape(1, D_MODEL)
    h2, gate, up, act, dx2, loss_acc, dgf = _ffn_fwd(x1, target, norm2_g, gf, wg_t, wu_t, wd)

    dgate, dup, dx1, dg2 = _ffn_bwd(dx2, gate, up, x1, norm2_g, wg_t, wu_t, wd)
    ffn_grads4 = [g.reshape(N_SHARDS, FF_SHARD, D_MODEL) for g in _ffn_wgrad(h2, dgate, dup, act, dx2)]
    e_send, e_recv, e_srcs, e_lands, token_exchange = _split_start(
        "exchange_start", ffn_grads4, _exchange_lands(ffn_grads4), _exchange_copies, len(ffn_grads4))
    do, dg, du, dvv, dwo, dgn, dlng, dlnb, dws, dbs = _mixer_bwd(
        dx1, ycat, o_f, o_b, p, gla_norm_g, gmlp_ln_g, gmlp_ln_b, ws_bf, wst_bf, bs_col, w_out_full, token_exchange)
    ffn_mine, ffn_other = _split_wait("exchange_wait", e_send, e_recv, e_srcs, e_lands, _exchange_copies, (do,))
    ffn_parts = _add_halves(ffn_mine, ffn_other, sc_arr)
    ffn_payload = [pb for _, pb in ffn_parts]
    s_send, s_recv, s_parts, s_lands, token_scatter = _split_start(
        "scatter_start", ffn_payload, _scatter_lands(ffn_payload), _scatter_copies, 3 * len(ffn_payload))
    dq_f, dk_f, dv_f, dlr_f, dwdec_f, dbdec_f = _gla_bwd(p, do, st_f, wd_pad_f, b_decay_f, token_scatter, reverse=False)
    dq, dk, dv, dlr, dwdec_b, dbdec_b = _gla_bwd(
        p, do, st_b, wd_pad_b, b_decay_b, token_scatter, reverse=True, other=(dq_f, dk_f, dv_f, dlr_f))
    dwin_t, dp = _inproj_wgrad(xs, norm1_g, dq, dk, dv, dg, du, dvv, dlr)
    _, ffn_recv = _split_wait("scatter_wait", s_send, s_recv, s_parts, s_lands, _scatter_copies, (dwin_t,))

    dwin4 = dwin_t.reshape(N_SHARDS, PROJ_W // N_SHARDS, D_MODEL)
    dwo4 = dwo.reshape(N_SHARDS, D_MODEL // N_SHARDS, D_MODEL)
    proj_grads4 = [dwin4, dwo4]
    x_send, x_recv, x_srcs, x_lands, token_swap = _split_start(
        "proj_exchange_start", proj_grads4, _exchange_lands(proj_grads4), _exchange_copies, len(proj_grads4))
    ffn_bufs = [_add_partials(pf, r, sc_arr, token_swap) for (pf, _), r in zip(ffn_parts, ffn_recv)]
    proj_mine, proj_other = _split_wait("proj_exchange_wait", x_send, x_recv, x_srcs, x_lands, _exchange_copies, tuple(ffn_bufs))
    proj_parts = [_add_halves([g], [r], sc_arr)[0] for g, r in zip(proj_mine, proj_other)]
    proj_payload = [pb for _, pb in proj_parts]
    n_proj = len(proj_payload)
    ffn_join_copies = functools.partial(_join_copies, first=3 * n_proj)
    scatter_and_join = lambda srcs, lands, send_sems, recv_sems: (
        _scatter_copies(srcs[:n_proj], lands, send_sems, recv_sems) + ffn_join_copies(srcs[n_proj:], (), send_sems, recv_sems))
    p_send, p_recv, started, p_lands, token_join = _split_start(
        "proj_scatter_start", proj_payload + ffn_bufs, _scatter_lands(proj_payload), scatter_and_join, 3 * n_proj + len(ffn_bufs))
    p_parts, ffn_bufs = started[:n_proj], started[n_proj:]
    dx, dg1 = _inproj_dx(xs, dx1, norm1_g, w_in_t, dp, token_join)
    _, proj_recv = _split_wait("proj_scatter_wait", p_send, p_recv, p_parts, p_lands, _scatter_copies, (dx,))

    dwdec_f16 = dwdec_f[0:LOWRANK]
    dwdec_b16 = dwdec_b[LOWRANK : 2 * LOWRANK]
    shard_major = lambda a: jnp.transpose(a.reshape(LOWRANK, N_SHARDS, KEY_W // N_SHARDS), (1, 0, 2))
    small_grads = {
        "norm1_g": dg1, "b_decay_f": dbdec_f, "b_decay_b": dbdec_b, "gla_norm_g": dgn, "gmlp_ln_g": dlng, "gmlp_ln_b": dlnb,
        "w_spatial": dws, "b_spatial": dbs, "norm2_g": dg2, "final_norm_g": dgf,
    }
    g_pack = _pack_small([small_grads[k] for k in SMALL_NAMES] + [loss_acc], [shard_major(dwdec_f16), shard_major(dwdec_b16)])
    proj_bufs = [_add_partials(pf, r, sc_arr, token_join) for (pf, _), r in zip(proj_parts, proj_recv)]
    proj_join_copies = functools.partial(_join_copies, first=7)
    tail_copies = lambda srcs, lands, send_sems, recv_sems: (
        _small_gather_copies(srcs[:1], lands, send_sems, recv_sems) + proj_join_copies(srcs[1:], (), send_sems, recv_sems))
    t_send, t_recv, (g_pack, *proj_bufs), g_lands, token_tail = _split_start(
        "tail_start", [g_pack] + proj_bufs, [jax.ShapeDtypeStruct((8, SMALL_TOTAL, LANES), F32)], tail_copies, 7 + len(proj_bufs))

    ffn_bufs, _ = _split_wait("join_wait", p_send, p_recv, ffn_bufs, [], ffn_join_copies, (dx, token_tail))
    adamw = lambda k, g: _adamw(big_shards[k], g, rows_of("m_", k), rows_of("v_", k))
    big_updates = {k: adamw(k, g) for k, g in zip(big_names[2:], ffn_bufs)}
    proj_bufs, _ = _split_wait(
        "proj_join_wait", t_send, t_recv, proj_bufs, [], proj_join_copies, tuple(u[1] for u in big_updates.values()))
    big_updates.update({k: adamw(k, g) for k, g in zip(big_names[:2], proj_bufs)})

    (g_pack,), (g_all,) = _split_wait(
        "small_gather_wait", t_send, t_recv, [g_pack], g_lands, _small_gather_copies, tuple(u[1] for u in big_updates.values()))
    pack_own = lambda pre: _pack_small([args[pre + k] for k in SMALL_NAMES], [args[pre + "w_decay_f"], args[pre + "w_decay_b"]])
    small_updates, loss_row = _adamw_small(g_all, g_pack, pack_own(""), pack_own("m_"), pack_own("v_"), [args[k] for k in SMALL_NAMES])

    names = ["norm1_g", "w_in", "w_decay_f", "b_decay_f", "w_decay_b", "b_decay_b", "gla_norm_g", "gmlp_ln_g", "gmlp_ln_b",
             "w_spatial", "b_spatial", "w_out", "norm2_g", "w_gate", "w_up", "w_down", "final_norm_g"]
    results = {"g": {}, "d": {}, "m": {}, "v": {}}
    for tag, arrays in zip("gdmv", small_updates):
        for k, a in zip(SMALL_NAMES + ["w_decay_f", "w_decay_b"], arrays):
            results[tag][k] = a.reshape(args[k].shape)
    for k in big_names:
        for tag, a in zip("gdmv", big_updates[k]):
            results[tag][k] = (jnp.transpose(a) if k in transposed else a).reshape(args[k].shape)

    loss = loss_row[0, 0]
    grad_x = dx.reshape(x.shape)
    return (loss, grad_x, *[results["g"][k] for k in names], *[results["d"][k] for k in names],
            *[results["m"][k] for k in names], *[results["v"][k] for k in names])
```

```python
import functools
import math

import jax
import jax.numpy as jnp
from jax import lax
from jax.experimental import pallas as pl
from jax.experimental.pallas import tpu as pltpu

F32, BF16 = jnp.float32, jnp.bfloat16

D_MODEL = 1024
GLA_HEADS = 4
GLA_DK = 64
GLA_DV = 128
KEY_W = GLA_HEADS * GLA_DK
GLA_W = GLA_HEADS * GLA_DV
GMLP_W = 512
GMLP_GROUPS = 4
GMLP_CHUNK = 128
LOWRANK = 16
GLA_CHUNK = 64
GLA_TAU = 16.0
PROJ_W = 2592
PROJ_WP = 2688
D_FF = 2816
N_SHARDS = 4
FF_SHARD = D_FF // N_SHARDS
EPS = 1e-6
LANES = 128
TOKEN_SHAPE = (8, LANES)
MIB = 1024 * 1024

ADAM_LR = 0.001
ADAM_B1 = 0.9
ADAM_B2 = 0.999
ADAM_EPS = 1e-08
ADAM_WD = 0.01
ADAM_STEP = 10

COL_Q, COL_K = 0, 256
COL_V, COL_G, COL_U, COL_VV = 512, 1024, 1536, 2048
COL_LR = 2560
ROW_LR, ROW_UV = 1536, 1568
HALF = D_MODEL // 2

MESH = pl.DeviceIdType.MESH


def _nn(a, b):
    return jnp.dot(a, b, preferred_element_type=F32)


def _nt(a, b):
    return lax.dot_general(a, b, (((1,), (1,)), ((), ())), preferred_element_type=F32)


def _tn(a, b):
    return lax.dot_general(a, b, (((0,), (0,)), ((), ())), preferred_element_type=F32)


def _bnn(a, b):
    return jnp.einsum("nik,nkj->nij", a, b, preferred_element_type=F32)


def _bnt(a, b):
    return jnp.einsum("nik,njk->nij", a, b, preferred_element_type=F32)


def _btn(a, b):
    return jnp.einsum("nki,nkj->nij", a, b, preferred_element_type=F32)


def _resident(shape):
    zeros = (0,) * len(shape)
    return pl.BlockSpec(shape, lambda *_: zeros, pipeline_mode=pl.Buffered(1))


def _params(vmem_mib, semantics=("arbitrary",)):
    return pltpu.CompilerParams(vmem_limit_bytes=vmem_mib * MIB, dimension_semantics=semantics)


def _sigmoid(x):
    return 1.0 / (1.0 + jnp.exp(-x))


def _gelu(x):
    return 0.5 * x * (1.0 + lax.erf(x * (1.0 / math.sqrt(2.0))))


def _gelu_and_grad(x):
    cdf = 0.5 * (1.0 + lax.erf(x * (1.0 / math.sqrt(2.0))))
    return x * cdf, cdf + x * jnp.exp(-0.5 * x * x) * (1.0 / math.sqrt(2.0 * math.pi))


def _log_sigmoid(x):
    return jnp.minimum(x, 0.0) - jnp.log(1.0 + jnp.exp(-jnp.abs(x)))


def _rms_bwd(dxh, xh, r):
    return r * (dxh - xh * jnp.mean(dxh * xh, axis=-1, keepdims=True))


def _chunk_cumsum(v, row_in_chunk, reverse):
    rows = v.shape[0]
    for sh in (1, 2, 4, 8, 16, 32):
        if reverse:
            v = v + jnp.where(row_in_chunk + sh < GLA_CHUNK, pltpu.roll(v, rows - sh, axis=0), 0.0)
        else:
            v = v + jnp.where(row_in_chunk >= sh, pltpu.roll(v, sh, axis=0), 0.0)
    return v


def _inproj(x, g1, w_in_t, token):
    seq = x.shape[0]
    tm = min(seq, 512)

    def body(x_ref, g_ref, w_ref, token_ref, p_ref):
        xv = x_ref[...]
        r = lax.rsqrt(jnp.mean(xv * xv, axis=-1, keepdims=True) + EPS)
        h = (xv * r * g_ref[...]).astype(BF16)
        p_ref[:, 0:COL_U] = _nt(h, w_ref[0:ROW_LR, :])
        p_ref[:, COL_U:COL_LR] = _nt(h, w_ref[ROW_UV:PROJ_W, :])
        p_ref[:, COL_LR:PROJ_WP] = _nt(h, w_ref[ROW_LR : ROW_LR + LANES, :])

    return pl.pallas_call(
        body,
        name="inproj",
        grid=(seq // tm,),
        in_specs=[pl.BlockSpec((tm, D_MODEL), lambda i: (i, 0)), _resident((1, D_MODEL)), _resident((PROJ_W, D_MODEL)), _resident(TOKEN_SHAPE)],
        out_specs=pl.BlockSpec((tm, PROJ_WP), lambda i: (i, 0)),
        out_shape=jax.ShapeDtypeStruct((seq, PROJ_WP), F32),
        compiler_params=_params(48, ("parallel",)),
    )(x, g1, w_in_t, token)


def _gla_tile(seq):
    return min(seq, 1024)


def _gla_decay_terms(lr_bf, wd_ref, bd_ref, pair, row_in_chunk, reverse, n):
    cols = pl.ds(pair * LANES, LANES)
    pre = _nn(lr_bf, wd_ref[:, cols]) + bd_ref[:, cols]
    la = _log_sigmoid(pre) * (1.0 / GLA_TAU)
    b = _chunk_cumsum(la, row_in_chunk, reverse)
    b3 = b.reshape(n, GLA_CHUNK, LANES)
    blast = b3[:, 0:1, :] if reverse else b3[:, GLA_CHUNK - 1 : GLA_CHUNK, :]
    return pre, b3, blast


def _gla_fwd(p, wd_pad, bd, token, reverse):
    seq = p.shape[0]
    tg = _gla_tile(seq)
    nt = seq // tg
    n = tg // GLA_CHUNK
    scale = GLA_DK**-0.5

    def tile(i):
        return nt - 1 - i if reverse else i

    def body(q_ref, k_ref, v_ref, lr_ref, wd_ref, bd_ref, token_ref, o_ref, st_ref, carry):
        @pl.when(pl.program_id(0) == 0)
        def _():
            carry[...] = jnp.zeros_like(carry)

        lr_bf = lr_ref[...].astype(BF16)
        states = [carry[h] for h in range(GLA_HEADS)]
        row_in_chunk = lax.broadcasted_iota(jnp.int32, (tg, LANES), 0) % GLA_CHUNK
        lane_head = lax.broadcasted_iota(jnp.int32, (1, LANES), 1) // GLA_DK
        tt = lax.broadcasted_iota(jnp.int32, (GLA_CHUNK, GLA_CHUNK), 0)
        ss = lax.broadcasted_iota(jnp.int32, (GLA_CHUNK, GLA_CHUNK), 1)
        causal = (tt <= ss) if reverse else (tt >= ss)
        order = range(n - 1, -1, -1) if reverse else range(n)
        heads = range(GLA_HEADS)
        qds, vhs, decs, sc_raw, dst = {}, {}, {}, {}, {}
        for pair in range(2):
            cols = pl.ds(pair * LANES, LANES)
            _, b3, blast = _gla_decay_terms(lr_bf, wd_ref, bd_ref, pair, row_in_chunk, reverse, n)
            q3 = q_ref[:, cols].reshape(n, GLA_CHUNK, LANES) * scale
            k3 = k_ref[:, cols].reshape(n, GLA_CHUNK, LANES)
            qd = q3 * jnp.exp(b3)
            kd = (k3 * jnp.exp(-b3)).astype(BF16)
            kte = k3 * jnp.exp(blast - b3)
            decs[pair] = jnp.exp(blast)
            qds[pair] = qd.astype(BF16)
            m0 = (lane_head == 0).astype(F32)
            m1 = (lane_head == 1).astype(F32)
            q_both = jnp.concatenate([(qd * m0).astype(BF16), (qd * m1).astype(BF16)], axis=1)
            sc_both = _bnt(q_both, kd)
            for hh, m in ((0, m0), (1, m1)):
                h = 2 * pair + hh
                vhs[h] = v_ref[:, pl.ds(h * GLA_DV, GLA_DV)].reshape(n, GLA_CHUNK, GLA_DV).astype(BF16)
                sc_raw[h] = sc_both[:, hh * GLA_CHUNK : (hh + 1) * GLA_CHUNK, :]
                dst[h] = _btn(vhs[h], (kte * m).astype(BF16))
        o_intra, befores = {}, {}
        for h in heads:
            o_intra[h] = _bnn(jnp.where(causal, sc_raw[h], 0.0).astype(BF16), vhs[h])
            st, before = states[h], [None] * n
            for j in order:
                before[j] = st
                st = st * decs[h // 2][j] + dst[h][j]
            states[h] = st
            befores[h] = jnp.stack(before).astype(BF16)
        outs = {}
        for pair in range(2):
            both = jnp.concatenate([befores[2 * pair], befores[2 * pair + 1]], axis=1)
            o_inter = _bnt(qds[pair], both)
            for hh in range(2):
                h = 2 * pair + hh
                outs[h] = (o_intra[h] + o_inter[:, :, hh * GLA_DV : (hh + 1) * GLA_DV]).reshape(tg, GLA_DV)
        for h in range(GLA_HEADS):
            o_ref[:, pl.ds(h * GLA_DV, GLA_DV)] = outs[h]
            st_ref[:, h] = befores[h]
            carry[h] = states[h]

    nchunks = seq // GLA_CHUNK
    return pl.pallas_call(
        body,
        name="gla_fwd_rev" if reverse else "gla_fwd",
        grid=(nt,),
        in_specs=[
            pl.BlockSpec((tg, KEY_W), lambda i: (tile(i), COL_Q // KEY_W)),
            pl.BlockSpec((tg, KEY_W), lambda i: (tile(i), COL_K // KEY_W)),
            pl.BlockSpec((tg, GLA_W), lambda i: (tile(i), COL_V // GLA_W)),
            pl.BlockSpec((tg, LANES), lambda i: (tile(i), COL_LR // LANES)),
            _resident((LANES, KEY_W)),
            _resident((1, KEY_W)),
            _resident(TOKEN_SHAPE),
        ],
        out_specs=[
            pl.BlockSpec((tg, GLA_W), lambda i: (tile(i), 0)),
            pl.BlockSpec((n, GLA_HEADS, GLA_DV, LANES), lambda i: (tile(i), 0, 0, 0)),
        ],
        out_shape=[
            jax.ShapeDtypeStruct((seq, GLA_W), F32),
            jax.ShapeDtypeStruct((nchunks, GLA_HEADS, GLA_DV, LANES), BF16),
        ],
        scratch_shapes=[pltpu.VMEM((GLA_HEADS, GLA_DV, LANES), F32)],
        compiler_params=_params(48),
    )(p, p, p, p, wd_pad, bd, token)


def _mixer_out(x, o_f, o_b, p, gn, lng, lnb, ws_bf, bs_col, w_out, token):
    seq = x.shape[0]
    tm = min(seq, 512)

    def body(x_ref, of_ref, ob_ref, g_ref, u_ref, vv_ref, gn_ref, lng_ref, lnb_ref, ws_ref, bs_ref, wo_ref, token_ref, x1_ref, yc_ref, vn_sc):
        for h in range(GLA_HEADS):
            cols = pl.ds(h * GLA_DV, GLA_DV)
            oh = of_ref[:, cols] + ob_ref[:, cols]
            on = oh * lax.rsqrt(jnp.mean(oh * oh, axis=-1, keepdims=True) + EPS)
            gh = g_ref[:, cols]
            yc_ref[:, cols] = (on * gn_ref[:, cols] * (gh * _sigmoid(gh))).astype(BF16)
        zv = _gelu(vv_ref[...])
        xc = zv - jnp.mean(zv, axis=-1, keepdims=True)
        vhat = xc * lax.rsqrt(jnp.mean(xc * xc, axis=-1, keepdims=True) + EPS)
        vn_sc[...] = (vhat * lng_ref[...] + lnb_ref[...]).astype(BF16)
        for c in range(tm // GMLP_CHUNK):
            rows = pl.ds(c * GMLP_CHUNK, GMLP_CHUNK)
            for g in range(GMLP_GROUPS):
                cols = pl.ds(g * LANES, LANES)
                s = _nn(ws_ref[g], vn_sc[rows, cols]) + bs_ref[g]
                yc_ref[rows, pl.ds(GLA_W + g * LANES, LANES)] = (_gelu(u_ref[rows, cols]) * s).astype(BF16)
        x1_ref[...] = x_ref[...] + _nn(yc_ref[...], wo_ref[...])

    row = lambda w: pl.BlockSpec((tm, w), lambda i: (i, 0))
    pcol = lambda col: pl.BlockSpec((tm, GLA_W), lambda i: (i, col // GLA_W))
    return pl.pallas_call(
        body,
        name="mixer_out",
        grid=(seq // tm,),
        in_specs=[
            row(D_MODEL), row(GLA_W), row(GLA_W), pcol(COL_G), pcol(COL_U), pcol(COL_VV),
            _resident((1, GLA_W)), _resident((1, GMLP_W)), _resident((1, GMLP_W)),
            _resident((GMLP_GROUPS, GMLP_CHUNK, GMLP_CHUNK)), _resident((GMLP_GROUPS, GMLP_CHUNK, 1)),
            _resident((D_MODEL, D_MODEL)), _resident(TOKEN_SHAPE),
        ],
        out_specs=[row(D_MODEL), row(D_MODEL)],
        out_shape=[jax.ShapeDtypeStruct((seq, D_MODEL), F32), jax.ShapeDtypeStruct((seq, D_MODEL), BF16)],
        scratch_shapes=[pltpu.VMEM((tm, GMLP_W), BF16)],
        compiler_params=_params(48, ("parallel",)),
    )(x, o_f, o_b, p, p, p, gn, lng, lnb, ws_bf, bs_col, w_out, token)


def _ffn_fwd(x1, target, g2, gf, wg_t, wu_t, wd):
    seq = x1.shape[0]
    tm = min(seq, 256)

    def body(x1_ref, t_ref, g2_ref, gf_ref, wg_ref, wu_ref, wd_ref, h2_ref, gate_ref, up_ref, act_ref, dx2_ref, loss_ref, dgf_ref):
        @pl.when(pl.program_id(0) == 0)
        def _():
            loss_ref[...] = jnp.zeros_like(loss_ref)
            dgf_ref[...] = jnp.zeros_like(dgf_ref)

        x1v = x1_ref[...]
        h2 = (x1v * lax.rsqrt(jnp.mean(x1v * x1v, axis=-1, keepdims=True) + EPS) * g2_ref[...]).astype(BF16)
        h2_ref[...] = h2
        gate = _nt(h2, wg_ref[...])
        up = _nt(h2, wu_ref[...])
        act = (gate * _sigmoid(gate) * up).astype(BF16)
        gate_ref[...] = gate
        up_ref[...] = up
        act_ref[...] = act
        x2 = x1v + _nn(act, wd_ref[...])
        rf = lax.rsqrt(jnp.mean(x2 * x2, axis=-1, keepdims=True) + EPS)
        xh = x2 * rf
        err = xh * gf_ref[...] - t_ref[...]
        loss_ref[...] += 0.5 * jnp.sum(jnp.mean(err * err, axis=-1, keepdims=True))
        dy = err * (1.0 / D_MODEL)
        dgf_ref[...] += jnp.sum(dy * xh, axis=0, keepdims=True)
        dx2_ref[...] = _rms_bwd(dy * gf_ref[...], xh, rf)

    row = lambda w: pl.BlockSpec((tm, w), lambda i: (i, 0))
    weight = _resident((D_FF, D_MODEL))
    return pl.pallas_call(
        body,
        name="ffn_fwd",
        grid=(seq // tm,),
        in_specs=[row(D_MODEL), row(D_MODEL), _resident((1, D_MODEL)), _resident((1, D_MODEL)), weight, weight, weight],
        out_specs=[row(D_MODEL), row(D_FF), row(D_FF), row(D_FF), row(D_MODEL),
                   pl.BlockSpec((1, LANES), lambda i: (0, 0)), pl.BlockSpec((1, D_MODEL), lambda i: (0, 0))],
        out_shape=[
            jax.ShapeDtypeStruct((seq, D_MODEL), BF16),
            jax.ShapeDtypeStruct((seq, D_FF), F32),
            jax.ShapeDtypeStruct((seq, D_FF), F32),
            jax.ShapeDtypeStruct((seq, D_FF), BF16),
            jax.ShapeDtypeStruct((seq, D_MODEL), F32),
            jax.ShapeDtypeStruct((1, LANES), F32),
            jax.ShapeDtypeStruct((1, D_MODEL), F32),
        ],
        compiler_params=_params(56),
    )(x1, target, g2, gf, wg_t, wu_t, wd)


def _ffn_bwd(dx2, gate, up, x1, g2, wg_t, wu_t, wd):
    seq = x1.shape[0]
    tm = min(seq, 256)

    def body(dx2_ref, gate_ref, up_ref, x1_ref, g2_ref, wg_ref, wu_ref, wd_ref, dgate_ref, dup_ref, dx1_ref, dg2_ref):
        @pl.when(pl.program_id(0) == 0)
        def _():
            dg2_ref[...] = jnp.zeros_like(dg2_ref)

        dx2v = dx2_ref[...]
        dact = _nt(dx2v.astype(BF16), wd_ref[...])
        gate = gate_ref[...]
        sg = _sigmoid(gate)
        dgate = (dact * up_ref[...] * (sg * (1.0 + gate * (1.0 - sg)))).astype(BF16)
        dup = (dact * (gate * sg)).astype(BF16)
        dgate_ref[...] = dgate
        dup_ref[...] = dup
        dh2 = _nn(dgate, wg_ref[...]) + _nn(dup, wu_ref[...])
        x1v = x1_ref[...]
        r2 = lax.rsqrt(jnp.mean(x1v * x1v, axis=-1, keepdims=True) + EPS)
        xh = x1v * r2
        dg2_ref[...] += jnp.sum(dh2 * xh, axis=0, keepdims=True)
        dx1_ref[...] = dx2v + _rms_bwd(dh2 * g2_ref[...], xh, r2)

    row = lambda w: pl.BlockSpec((tm, w), lambda i: (i, 0))
    weight = _resident((D_FF, D_MODEL))
    return pl.pallas_call(
        body,
        name="ffn_bwd",
        grid=(seq // tm,),
        in_specs=[row(D_MODEL), row(D_FF), row(D_FF), row(D_MODEL), _resident((1, D_MODEL)), weight, weight, weight],
        out_specs=[row(D_FF), row(D_FF), row(D_MODEL), pl.BlockSpec((1, D_MODEL), lambda i: (0, 0))],
        out_shape=[
            jax.ShapeDtypeStruct((seq, D_FF), BF16),
            jax.ShapeDtypeStruct((seq, D_FF), BF16),
            jax.ShapeDtypeStruct((seq, D_MODEL), F32),
            jax.ShapeDtypeStruct((1, D_MODEL), F32),
        ],
        compiler_params=_params(56),
    )(dx2, gate, up, x1, g2, wg_t, wu_t, wd)


WGRAD_ROWS = D_FF // 2


def _ffn_wgrad(h2, dgate, dup, act, dx2):
    seq = h2.shape[0]
    tm = min(seq, 512)

    def body(h2_ref, dgate_ref, dup_ref, act_ref, dx2_ref, dwg_ref, dwu_ref, dwd_ref):
        @pl.when(pl.program_id(1) == 0)
        def _():
            dwg_ref[...] = jnp.zeros_like(dwg_ref)
            dwu_ref[...] = jnp.zeros_like(dwu_ref)
            dwd_ref[...] = jnp.zeros_like(dwd_ref)

        h2v = h2_ref[...]
        dwg_ref[...] += _tn(dgate_ref[...], h2v)
        dwu_ref[...] += _tn(dup_ref[...], h2v)
        dwd_ref[...] += _tn(act_ref[...], dx2_ref[...].astype(BF16))

    ff = pl.BlockSpec((tm, WGRAD_ROWS), lambda j, i: (i, j))
    row = pl.BlockSpec((tm, D_MODEL), lambda j, i: (i, 0))
    out = pl.BlockSpec((WGRAD_ROWS, D_MODEL), lambda j, i: (j, 0))
    return pl.pallas_call(
        body,
        name="ffn_wgrad",
        grid=(D_FF // WGRAD_ROWS, seq // tm),
        in_specs=[row, ff, ff, ff, row],
        out_specs=[out, out, out],
        out_shape=[jax.ShapeDtypeStruct((D_FF, D_MODEL), F32)] * 3,
        compiler_params=_params(56, ("parallel", "arbitrary")),
    )(h2, dgate, dup, act, dx2)


def _mixer_bwd(dx1, ycat, o_f, o_b, p, gn, lng, lnb, ws_bf, wst_bf, bs_col, w_out, token):
    seq = dx1.shape[0]
    tm = min(seq, 512)
    nsteps = seq // tm

    def body(dx1_ref, yc_ref, of_ref, ob_ref, g_ref, u_ref, vv_ref, gn_ref, lng_ref, lnb_ref, ws_ref, wst_ref, bs_ref, wo_ref, token_ref,
             do_ref, dg_ref, du_ref, dvv_ref, dwo_ref, dgn_ref, dlng_ref, dlnb_ref, dws_ref, dbs_ref, vn_sc, dvn_sc, dbs_acc):
        step = pl.program_id(0)

        @pl.when(step == 0)
        def _():
            for r in (dwo_ref, dgn_ref, dlng_ref, dlnb_ref, dws_ref, dbs_acc):
                r[...] = jnp.zeros_like(r)

        dx1b = dx1_ref[...].astype(BF16)
        dyc = _nt(dx1b, wo_ref[...])
        dwo_ref[...] += _tn(yc_ref[...], dx1b)
        for h in range(GLA_HEADS):
            cols = pl.ds(h * GLA_DV, GLA_DV)
            dya = dyc[:, h * GLA_DV : (h + 1) * GLA_DV]
            oh = of_ref[:, cols] + ob_ref[:, cols]
            rn = lax.rsqrt(jnp.mean(oh * oh, axis=-1, keepdims=True) + EPS)
            on = oh * rn
            gh = g_ref[:, cols]
            sg = _sigmoid(gh)
            sil = gh * sg
            gnh = gn_ref[:, cols]
            dgn_ref[:, cols] += jnp.sum(dya * on * sil, axis=0, keepdims=True)
            dg_ref[:, cols] = (dya * on * gnh * (sg * (1.0 + gh * (1.0 - sg)))).astype(BF16)
            do_ref[:, cols] = _rms_bwd(dya * gnh * sil, on, rn)
        vv = vv_ref[...]
        zv, zv_grad = _gelu_and_grad(vv)
        xc = zv - jnp.mean(zv, axis=-1, keepdims=True)
        rstd = lax.rsqrt(jnp.mean(xc * xc, axis=-1, keepdims=True) + EPS)
        vhat = xc * rstd
        vn_sc[...] = (vhat * lng_ref[...] + lnb_ref[...]).astype(BF16)
        for c in range(tm // GMLP_CHUNK):
            rows = pl.ds(c * GMLP_CHUNK, GMLP_CHUNK)
            for g in range(GMLP_GROUPS):
                cols = pl.ds(g * LANES, LANES)
                vn = vn_sc[rows, cols]
                s = _nn(ws_ref[g], vn) + bs_ref[g]
                dyb = dyc[c * GMLP_CHUNK : (c + 1) * GMLP_CHUNK, GLA_W + g * LANES : GLA_W + (g + 1) * LANES]
                zu, zu_grad = _gelu_and_grad(u_ref[rows, cols])
                du_ref[rows, cols] = (dyb * s * zu_grad).astype(BF16)
                ds = dyb * zu
                dbs_acc[g] += ds
                dsb = ds.astype(BF16)
                dws_ref[g] += _nt(dsb, vn)
                dvn_sc[rows, cols] = _nn(wst_ref[g], dsb)
        dvn = dvn_sc[...]
        dlng_ref[...] += jnp.sum(dvn * vhat, axis=0, keepdims=True)
        dlnb_ref[...] += jnp.sum(dvn, axis=0, keepdims=True)
        dvh = dvn * lng_ref[...]
        dzv = rstd * (dvh - jnp.mean(dvh, axis=-1, keepdims=True) - vhat * jnp.mean(dvh * vhat, axis=-1, keepdims=True))
        dvv_ref[...] = (dzv * zv_grad).astype(BF16)

        @pl.when(step == nsteps - 1)
        def _():
            dbs_ref[...] = jnp.sum(dbs_acc[...], axis=-1, keepdims=True)

    row = lambda w: pl.BlockSpec((tm, w), lambda i: (i, 0))
    pcol = lambda col: pl.BlockSpec((tm, GLA_W), lambda i: (i, col // GLA_W))
    const = lambda shape: pl.BlockSpec(shape, lambda i: (0,) * len(shape))
    return pl.pallas_call(
        body,
        name="mixer_bwd",
        grid=(nsteps,),
        in_specs=[
            row(D_MODEL), row(D_MODEL), row(GLA_W), row(GLA_W), pcol(COL_G), pcol(COL_U), pcol(COL_VV),
            _resident((1, GLA_W)), _resident((1, GMLP_W)), _resident((1, GMLP_W)),
            _resident((GMLP_GROUPS, GMLP_CHUNK, GMLP_CHUNK)), _resident((GMLP_GROUPS, GMLP_CHUNK, GMLP_CHUNK)),
            _resident((GMLP_GROUPS, GMLP_CHUNK, 1)), _resident((D_MODEL, D_MODEL)), _resident(TOKEN_SHAPE),
        ],
        out_specs=[
            row(GLA_W), row(GLA_W), row(GMLP_W), row(GMLP_W), const((D_MODEL, D_MODEL)),
            const((1, GLA_W)), const((1, GMLP_W)), const((1, GMLP_W)),
            const((GMLP_GROUPS, GMLP_CHUNK, GMLP_CHUNK)), const((GMLP_GROUPS, GMLP_CHUNK, 1)),
        ],
        out_shape=[
            jax.ShapeDtypeStruct((seq, GLA_W), F32), jax.ShapeDtypeStruct((seq, GLA_W), BF16),
            jax.ShapeDtypeStruct((seq, GMLP_W), BF16), jax.ShapeDtypeStruct((seq, GMLP_W), BF16),
            jax.ShapeDtypeStruct((D_MODEL, D_MODEL), F32),
            jax.ShapeDtypeStruct((1, GLA_W), F32), jax.ShapeDtypeStruct((1, GMLP_W), F32), jax.ShapeDtypeStruct((1, GMLP_W), F32),
            jax.ShapeDtypeStruct((GMLP_GROUPS, GMLP_CHUNK, GMLP_CHUNK), F32), jax.ShapeDtypeStruct((GMLP_GROUPS, GMLP_CHUNK, 1), F32),
        ],
        scratch_shapes=[pltpu.VMEM((tm, GMLP_W), BF16), pltpu.VMEM((tm, GMLP_W), F32), pltpu.VMEM((GMLP_GROUPS, GMLP_CHUNK, GMLP_CHUNK), F32)],
        compiler_params=_params(56),
    )(dx1, ycat, o_f, o_b, p, p, p, gn, lng, lnb, ws_bf, wst_bf, bs_col, w_out, token)


def _gla_bwd(p, do, st, wd_pad, bd, token, reverse, other=None):
    seq = p.shape[0]
    tg = _gla_tile(seq)
    nt = seq // tg
    n = tg // GLA_CHUNK
    scale = GLA_DK**-0.5

    def tile(i):
        return i if reverse else nt - 1 - i

    def body(q_ref, k_ref, v_ref, lr_ref, do_ref, st_ref, wd_ref, bd_ref, token_ref, *rest):
        others, (dq_ref, dk_ref, dv_ref, dlr_ref, dwd_ref, dbd_ref, carry) = rest[:-7], rest[-7:]
        if others:
            odq_ref, odk_ref, odv_ref, odlr_ref = others

            def put(ref, idx, val, oref):
                ref[idx] = (val + oref[idx]).astype(BF16)
        else:
            odq_ref = odk_ref = odv_ref = odlr_ref = None

            def put(ref, idx, val, oref):
                ref[idx] = val

        @pl.when(pl.program_id(0) == 0)
        def _():
            carry[...] = jnp.zeros_like(carry)
            dwd_ref[...] = jnp.zeros_like(dwd_ref)
            dbd_ref[...] = jnp.zeros_like(dbd_ref)

        lr_bf = lr_ref[...].astype(BF16)
        carries = [carry[h] for h in range(GLA_HEADS)]
        row_in_chunk = lax.broadcasted_iota(jnp.int32, (tg, LANES), 0) % GLA_CHUNK
        lane_head = lax.broadcasted_iota(jnp.int32, (1, LANES), 1) // GLA_DK
        tt = lax.broadcasted_iota(jnp.int32, (GLA_CHUNK, GLA_CHUNK), 0)
        ss = lax.broadcasted_iota(jnp.int32, (GLA_CHUNK, GLA_CHUNK), 1)
        causal = (tt <= ss) if reverse else (tt >= ss)
        order = range(n) if reverse else range(n - 1, -1, -1)
        dlr = jnp.zeros((tg, LANES), F32)
        heads = range(GLA_HEADS)
        pv, masks, qdh, vhs, dohs, stbs = {}, {}, {}, {}, {}, {}
        sc_raw, dp, acc = {}, {}, {}
        for pair in range(2):
            cols = pl.ds(pair * LANES, LANES)
            pre, b3, blast = _gla_decay_terms(lr_bf, wd_ref, bd_ref, pair, row_in_chunk, reverse, n)
            q3 = q_ref[:, cols].reshape(n, GLA_CHUNK, LANES) * scale
            k3 = k_ref[:, cols].reshape(n, GLA_CHUNK, LANES)
            eb = jnp.exp(b3)
            emb = jnp.exp(-b3)
            ekte = jnp.exp(blast - b3)
            kdf = k3 * emb
            kte = k3 * ekte
            both = pl.ds(2 * pair * GLA_DV, 2 * GLA_DV)
            pv[pair] = dict(pre=pre, eb=eb, emb=emb, ekte=ekte, qd=q3 * eb, kdf=kdf, kd=kdf.astype(BF16), kte=kte, kte_bf=kte.astype(BF16),
                            dec=jnp.exp(blast), v=v_ref[:, both].reshape(n, GLA_CHUNK, 2 * GLA_DV).astype(BF16),
                            do=do_ref[:, both].reshape(n, GLA_CHUNK, 2 * GLA_DV).astype(BF16))
            for hh in range(2):
                h = 2 * pair + hh
                masks[h] = (lane_head == hh).astype(F32)
                qdh[h] = (pv[pair]["qd"] * masks[h]).astype(BF16)
                vhs[h] = pv[pair]["v"][:, :, hh * GLA_DV : (hh + 1) * GLA_DV]
                dohs[h] = pv[pair]["do"][:, :, hh * GLA_DV : (hh + 1) * GLA_DV]
                stbs[h] = st_ref[:, h]
                dp[h] = _bnt(dohs[h], vhs[h])
                acc[h] = _btn(dohs[h], qdh[h])
            sc_both = _bnt(jnp.concatenate([qdh[2 * pair], qdh[2 * pair + 1]], axis=1), pv[pair]["kd"])
            for hh in range(2):
                sc_raw[2 * pair + hh] = sc_both[:, hh * GLA_CHUNK : (hh + 1) * GLA_CHUNK, :]
        dsa, sc = {}, {}
        for h in heads:
            sc[h] = jnp.where(causal, sc_raw[h], 0.0).astype(BF16)
            dp[h] = jnp.where(causal, dp[h], 0.0).astype(BF16)
            dec = pv[h // 2]["dec"]
            c, after = carries[h], [None] * n
            for j in order:
                after[j] = c
                c = acc[h][j] + dec[j] * c
            carries[h] = c
            dsa[h] = jnp.stack(after)
        dvs, dqs, dks, dwds, dbds = [], [], [], [], []
        for pair in range(2):
            cols = pl.ds(pair * LANES, LANES)
            v = pv[pair]
            h0, h1 = 2 * pair, 2 * pair + 1
            dsa_both = jnp.concatenate([dsa[h0], dsa[h1]], axis=1)
            dsa_bf = dsa_both.astype(BF16)
            stb_bf = jnp.concatenate([stbs[h0], stbs[h1]], axis=1)
            dq_intra = _bnn(jnp.concatenate([dp[h0], dp[h1]], axis=1), v["kd"])
            dqd = (dq_intra[:, :GLA_CHUNK, :] * masks[h0] + dq_intra[:, GLA_CHUNK:, :] * masks[h1]) + _bnn(v["do"], stb_bf)
            dkd = _btn(dp[h0], qdh[h0]) + _btn(dp[h1], qdh[h1])
            dkte = _bnn(v["v"], dsa_bf)
            ddec = jnp.sum(dsa[h0] * stbs[h0].astype(F32) + dsa[h1] * stbs[h1].astype(F32), axis=1, keepdims=True)
            dv_inter = _bnt(v["kte_bf"], dsa_bf)
            for hh, h in ((0, h0), (1, h1)):
                dvs.append((_btn(sc[h], dohs[h]) + dv_inter[:, :, hh * GLA_DV : (hh + 1) * GLA_DV]).reshape(tg, GLA_DV))
            dqs.append((dqd * (scale * v["eb"])).reshape(tg, LANES))
            dks.append((dkd * v["emb"] + dkte * v["ekte"]).reshape(tg, LANES))
            db = dqd * v["qd"] - dkd * v["kdf"] - dkte * v["kte"]
            dblast = jnp.sum(dkte * v["kte"], axis=1, keepdims=True) + ddec * v["dec"]
            dla = _chunk_cumsum(db.reshape(tg, LANES), row_in_chunk, not reverse) + jnp.broadcast_to(dblast, (n, GLA_CHUNK, LANES)).reshape(tg, LANES)
            dpre = (dla * (1.0 / GLA_TAU) * _sigmoid(-v["pre"]))
            dpre_bf = dpre.astype(BF16)
            dlr = dlr + _nt(dpre_bf, wd_ref[:, cols])
            dwds.append(_tn(lr_bf, dpre_bf))
            dbds.append(jnp.sum(dpre, axis=0, keepdims=True))
        put(dlr_ref, (slice(None), slice(None)), dlr, odlr_ref)
        for pair in range(2):
            cols = pl.ds(pair * LANES, LANES)
            put(dq_ref, (slice(None), cols), dqs[pair], odq_ref)
            put(dk_ref, (slice(None), cols), dks[pair], odk_ref)
            dwd_ref[:, cols] += dwds[pair]
            dbd_ref[:, cols] += dbds[pair]
        for h in range(GLA_HEADS):
            put(dv_ref, (slice(None), pl.ds(h * GLA_DV, GLA_DV)), dvs[h], odv_ref)
            carry[h] = carries[h]

    pieces = [
        pl.BlockSpec((tg, KEY_W), lambda i: (tile(i), 0)),
        pl.BlockSpec((tg, KEY_W), lambda i: (tile(i), 0)),
        pl.BlockSpec((tg, GLA_W), lambda i: (tile(i), 0)),
        pl.BlockSpec((tg, LANES), lambda i: (tile(i), 0)),
    ]
    piece_dtype = BF16 if other else F32
    return pl.pallas_call(
        body,
        name="gla_bwd_rev" if reverse else "gla_bwd",
        grid=(nt,),
        in_specs=[
            pl.BlockSpec((tg, KEY_W), lambda i: (tile(i), COL_Q // KEY_W)),
            pl.BlockSpec((tg, KEY_W), lambda i: (tile(i), COL_K // KEY_W)),
            pl.BlockSpec((tg, GLA_W), lambda i: (tile(i), COL_V // GLA_W)),
            pl.BlockSpec((tg, LANES), lambda i: (tile(i), COL_LR // LANES)),
            pl.BlockSpec((tg, GLA_W), lambda i: (tile(i), 0)),
            pl.BlockSpec((n, GLA_HEADS, GLA_DV, LANES), lambda i: (tile(i), 0, 0, 0)),
            _resident((LANES, KEY_W)),
            _resident((1, KEY_W)),
            _resident(TOKEN_SHAPE),
        ] + (pieces if other else []),
        out_specs=pieces + [pl.BlockSpec((LANES, KEY_W), lambda i: (0, 0)), pl.BlockSpec((1, KEY_W), lambda i: (0, 0))],
        out_shape=[
            jax.ShapeDtypeStruct((seq, KEY_W), piece_dtype), jax.ShapeDtypeStruct((seq, KEY_W), piece_dtype),
            jax.ShapeDtypeStruct((seq, GLA_W), piece_dtype), jax.ShapeDtypeStruct((seq, LANES), piece_dtype),
            jax.ShapeDtypeStruct((LANES, KEY_W), F32), jax.ShapeDtypeStruct((1, KEY_W), F32),
        ],
        scratch_shapes=[pltpu.VMEM((GLA_HEADS, GLA_DV, LANES), F32)],
        compiler_params=_params(56),
    )(p, p, p, p, do, st, wd_pad, bd, token, *(other or ()))


def _inproj_wgrad(x, g1, dq, dk, dv, dg, du, dvv, dlr):
    seq = x.shape[0]
    tm = min(seq, 512)

    def body(x_ref, g1_ref, dq_ref, dk_ref, dv_ref, dg_ref, du_ref, dvv_ref, dlr_ref, dw_ref, dp_ref):
        @pl.when(pl.program_id(0) == 0)
        def _():
            dw_ref[...] = jnp.zeros_like(dw_ref)

        for col, ref in ((COL_Q, dq_ref), (COL_K, dk_ref), (COL_V, dv_ref), (COL_G, dg_ref), (COL_U, du_ref), (COL_VV, dvv_ref), (COL_LR, dlr_ref)):
            dp_ref[:, col : col + ref.shape[1]] = ref[...]
        xv = x_ref[...]
        h = (xv * lax.rsqrt(jnp.mean(xv * xv, axis=-1, keepdims=True) + EPS) * g1_ref[...]).astype(BF16)
        dw_ref[0:ROW_LR, :] += _tn(dp_ref[:, 0:COL_U], h)
        dw_ref[ROW_UV:PROJ_W, :] += _tn(dp_ref[:, COL_U:COL_LR], h)
        dw_ref[ROW_LR:ROW_UV, :] += _tn(dp_ref[:, COL_LR:PROJ_WP], h)[0 : ROW_UV - ROW_LR]

    row = lambda w: pl.BlockSpec((tm, w), lambda i: (i, 0))
    return pl.pallas_call(
        body,
        name="inproj_wgrad",
        grid=(seq // tm,),
        in_specs=[row(D_MODEL), _resident((1, D_MODEL)), row(KEY_W), row(KEY_W), row(GLA_W), row(GLA_W), row(GMLP_W), row(GMLP_W), row(LANES)],
        out_specs=[pl.BlockSpec((PROJ_W, D_MODEL), lambda i: (0, 0)), row(PROJ_WP)],
        out_shape=[jax.ShapeDtypeStruct((PROJ_W, D_MODEL), F32), jax.ShapeDtypeStruct((seq, PROJ_WP), BF16)],
        compiler_params=_params(56),
    )(x, g1, dq, dk, dv, dg, du, dvv, dlr)


def _inproj_dx(x, dx1, g1, w_in_t, dp, token):
    seq = x.shape[0]
    tm = min(seq, 512)

    def body(x_ref, dx1_ref, g1_ref, w_ref, dp_ref, token_ref, dx_ref, dg1_ref):
        @pl.when(pl.program_id(0) == 0)
        def _():
            dg1_ref[...] = jnp.zeros_like(dg1_ref)

        xv = x_ref[...]
        r1 = lax.rsqrt(jnp.mean(xv * xv, axis=-1, keepdims=True) + EPS)
        xh = xv * r1
        dh = (_nn(dp_ref[:, 0:COL_U], w_ref[0:ROW_LR, :]) + _nn(dp_ref[:, COL_U:COL_LR], w_ref[ROW_UV:PROJ_W, :])
              + _nn(dp_ref[:, COL_LR:PROJ_WP], w_ref[ROW_LR : ROW_LR + LANES, :]))
        dg1_ref[...] += jnp.sum(dh * xh, axis=0, keepdims=True)
        dx_ref[...] = dx1_ref[...] + _rms_bwd(dh * g1_ref[...], xh, r1)

    row = lambda w: pl.BlockSpec((tm, w), lambda i: (i, 0))
    return pl.pallas_call(
        body,
        name="inproj_dx",
        grid=(seq // tm,),
        in_specs=[row(D_MODEL), row(D_MODEL), _resident((1, D_MODEL)), _resident((PROJ_W, D_MODEL)), row(PROJ_WP), _resident(TOKEN_SHAPE)],
        out_specs=[row(D_MODEL), pl.BlockSpec((1, D_MODEL), lambda i: (0, 0))],
        out_shape=[jax.ShapeDtypeStruct((seq, D_MODEL), F32), jax.ShapeDtypeStruct((1, D_MODEL), F32)],
        compiler_params=_params(48),
    )(x, dx1, g1, w_in_t, dp, token)


def _in_hbm(a):
    return pltpu.with_memory_space_constraint(a, pltpu.HBM)


def _row_tile(rows, multiple=8):
    for t in range(min(rows, 512), 0, -1):
        if rows % t == 0 and t % multiple == 0:
            return t
    return rows


def _cast_into_slot(w, shard, token):
    rows, cols = w.shape
    tr = _row_tile(rows, 16)

    def body(s_ref, w_ref, token_ref, o_ref):
        o_ref[...] = w_ref[...].astype(BF16)

    return pl.pallas_call(
        body,
        name="cast_into_slot",
        grid_spec=pltpu.PrefetchScalarGridSpec(
            num_scalar_prefetch=1,
            grid=(rows // tr,),
            in_specs=[pl.BlockSpec((tr, cols), lambda i, s_ref: (i, 0)), pl.BlockSpec(TOKEN_SHAPE, lambda i, s_ref: (0, 0))],
            out_specs=pl.BlockSpec((None, tr, cols), lambda i, s_ref: (s_ref[0], i, 0)),
        ),
        out_shape=pltpu.HBM((N_SHARDS, rows, cols), BF16),
        compiler_params=_params(32, ("parallel",)),
    )(shard, _in_hbm(w), token)


def _add_halves(grads4, recvs, shard_core):
    n = len(grads4)
    _, rows, _ = grads4[0].shape
    tr = _row_tile(rows, 16)

    def body(sc_ref, *refs):
        for k in range(n):
            total = refs[k][...] + refs[n + k][...]
            refs[3 * n + k][...] = total.astype(BF16)

            @pl.when(pl.program_id(1) == sc_ref[0])
            def _(k=k, total=total):
                refs[2 * n + k][...] = total

    theirs = pl.BlockSpec((None, tr, HALF), lambda i, s, sc_ref: (s, i, 0))
    mine = pl.BlockSpec((None, tr, HALF), lambda i, s, sc_ref: (s, i, sc_ref[1]))
    kept = pl.BlockSpec((tr, HALF), lambda i, s, sc_ref: (i, 0))
    outs = pl.pallas_call(
        body,
        name="add_halves",
        grid_spec=pltpu.PrefetchScalarGridSpec(
            num_scalar_prefetch=1,
            grid=(rows // tr, N_SHARDS),
            in_specs=[mine] * n + [theirs] * n,
            out_specs=[kept] * n + [theirs] * n,
        ),
        out_shape=[pltpu.HBM((rows, HALF), F32)] * n + [pltpu.HBM((N_SHARDS, rows, HALF), BF16)] * n,
        compiler_params=_params(48, ("parallel", "arbitrary")),
    )(shard_core, *[_in_hbm(a) for a in list(grads4) + list(recvs)])
    return list(zip(outs[:n], outs[n:]))


def _add_partials(part, recv3, shard_core, token):
    rows, _ = part.shape
    tr = _row_tile(rows, 16)

    def body(sc_ref, p_ref, r_ref, token_ref, o_ref):
        o_ref[...] = ((p_ref[...] + r_ref[0].astype(F32)) + r_ref[1].astype(F32)) + r_ref[2].astype(F32)

    return pl.pallas_call(
        body,
        name="add_partials",
        grid_spec=pltpu.PrefetchScalarGridSpec(
            num_scalar_prefetch=1,
            grid=(rows // tr,),
            in_specs=[
                pl.BlockSpec((tr, HALF), lambda i, sc_ref: (i, 0)),
                pl.BlockSpec((3, tr, HALF), lambda i, sc_ref: (0, i, 0)),
                pl.BlockSpec(TOKEN_SHAPE, lambda i, sc_ref: (0, 0)),
            ],
            out_specs=pl.BlockSpec((tr, HALF), lambda i, sc_ref: (i, sc_ref[1])),
        ),
        out_shape=pltpu.HBM((rows, 2 * HALF), F32),
        compiler_params=_params(32, ("parallel",)),
    )(shard_core, _in_hbm(part), _in_hbm(recv3), token)


def _adam_math(w, g, m, v):
    m = ADAM_B1 * m + (1.0 - ADAM_B1) * g
    v = ADAM_B2 * v + (1.0 - ADAM_B2) * (g * g)
    m_hat = m / (1.0 - ADAM_B1**ADAM_STEP)
    v_hat = v / (1.0 - ADAM_B2**ADAM_STEP)
    delta = -ADAM_LR * (m_hat / (jnp.sqrt(v_hat) + ADAM_EPS) + ADAM_WD * w)
    return delta, m, v


def _adamw(w, g, m, v):
    rows, cols = w.shape
    tr = _row_tile(rows)

    def body(w_ref, g_ref, m_ref, v_ref, go_ref, d_ref, mo_ref, vo_ref):
        gv = g_ref[...]
        go_ref[...] = gv
        d_ref[...], mo_ref[...], vo_ref[...] = _adam_math(w_ref[...], gv, m_ref[...], v_ref[...])

    spec = pl.BlockSpec((tr, cols), lambda i: (i, 0))
    return pl.pallas_call(
        body, name="adamw", grid=(rows // tr,), in_specs=[spec] * 4, out_specs=[spec] * 4, out_shape=[pltpu.HBM(w.shape, F32)] * 4,
        compiler_params=_params(32, ("parallel",)),
    )(_in_hbm(w), _in_hbm(g), _in_hbm(m), _in_hbm(v))


SMALL_ROWS = 560
DECAY_ROWS = 8
SMALL_TOTAL = SMALL_ROWS + 2 * N_SHARDS * DECAY_ROWS


def _adamw_small(gathered, own, wp, mp, vp, like):
    out_rows = SMALL_ROWS + 2 * DECAY_ROWS
    places, off = [], 0
    for a in like:
        rows = a.size // LANES
        kept = a.shape[-1] == LANES
        places.append((off, rows, kept, (rows, LANES) if kept else (1, a.size)))
        off += rows
    loss_row = off
    decay_shape = (LOWRANK, KEY_W // N_SHARDS)
    n = len(places) + 2

    def body(ga_ref, own_ref, w_ref, m_ref, v_ref, *refs):
        outs, loss_ref, packed = refs[: 4 * n], refs[4 * n], refs[4 * n + 1 :]
        g_sc = packed[0]
        x, y, c = _position()
        shard, me = 2 * x + y, 4 * x + 2 * y + c
        total = lambda rows: functools.reduce(lambda a, b: a + b, [jnp.where(me == d, own_ref[rows, :], ga_ref[d, rows, :]) for d in range(8)])
        g_sc[pl.ds(0, SMALL_ROWS), :] = total(pl.ds(0, SMALL_ROWS))
        for k in range(2):
            start = pl.multiple_of(SMALL_ROWS + k * N_SHARDS * DECAY_ROWS + shard * DECAY_ROWS, DECAY_ROWS)
            g_sc[pl.ds(SMALL_ROWS + k * DECAY_ROWS, DECAY_ROWS), :] = total(pl.ds(start, DECAY_ROWS))
        packed[1][...], packed[2][...], packed[3][...] = _adam_math(w_ref[...], g_sc[...], m_ref[...], v_ref[...])
        loss_ref[...] = g_sc[loss_row : loss_row + 1, :]
        for t, res in enumerate(packed):
            for (at, rows, kept, _), out in zip(places, outs[t * n :]):
                if kept:
                    out[...] = res[at : at + rows, :]
                else:
                    for r in range(rows):
                        out[:, r * LANES : (r + 1) * LANES] = res[at + r : at + r + 1, :]
            for k in range(2):
                out = outs[t * n + len(places) + k]
                both = res[SMALL_ROWS + k * DECAY_ROWS : SMALL_ROWS + (k + 1) * DECAY_ROWS, :]
                halves = (both, pltpu.roll(both, LANES // 2, axis=1))
                for r in range(DECAY_ROWS):
                    for h in range(2):
                        out[2 * r + h : 2 * r + h + 1, :] = halves[h][r : r + 1, 0 : LANES // 2]

    shapes = [jax.ShapeDtypeStruct(s, F32) for *_, s in places] + [jax.ShapeDtypeStruct(decay_shape, F32)] * 2
    out = pl.pallas_call(
        body,
        name="adamw_small",
        out_shape=shapes * 4 + [jax.ShapeDtypeStruct((1, LANES), F32)],
        scratch_shapes=[pltpu.VMEM((out_rows, LANES), F32)] * 4,
        compiler_params=_params(32, None),
    )(gathered, own, wp, mp, vp)
    return [list(out[t * n : (t + 1) * n]) for t in range(4)], out[4 * n]


ANY = pl.BlockSpec(memory_space=pl.ANY)


def _position():
    return lax.axis_index("x"), lax.axis_index("y"), lax.axis_index("c")


def _other_chips(x, y):
    return [(1 - x, y), (x, 1 - y), (1 - x, 1 - y)]


HBM = pl.BlockSpec(memory_space=pltpu.HBM)
SEM = pl.BlockSpec(memory_space=pltpu.SEMAPHORE)
TOKEN = jax.ShapeDtypeStruct(TOKEN_SHAPE, F32)
DATAFLOW = pltpu.SideEffectType.DATAFLOW_SIDE_EFFECTING


def _half_block(ref4, slot, core):
    return ref4.at[slot, :, pl.ds(pl.multiple_of(core * HALF, HALF), HALF)]


def _gather_ici_copies(bufs, lands, send_sems, recv_sems):
    x, y, c = _position()
    pairs = []
    for k, ref4 in enumerate(bufs):
        mine = _half_block(ref4, 2 * x + y, c)
        for j, (px, py) in enumerate(_other_chips(x, y)):
            sems = dict(send_sem=send_sems.at[3 * k + j], recv_sem=recv_sems.at[3 * k + j], device_id=(px, py, c), device_id_type=MESH)
            pairs.append((functools.partial(pltpu.make_async_remote_copy, src_ref=mine, dst_ref=mine, **sems),
                          functools.partial(pltpu.make_async_remote_copy, src_ref=mine, dst_ref=_half_block(ref4, 2 * px + py, c), **sems)))
    return pairs


def _gather_d2d_copies(bufs, lands, send_sems, recv_sems, first=0):
    x, y, c = _position()
    pairs = []
    for k, ref4 in enumerate(bufs):
        for j, (px, py) in enumerate(_other_chips(x, y)):
            have = _half_block(ref4, 2 * px + py, c)
            at = first + 3 * k + j
            sems = dict(send_sem=send_sems.at[at], recv_sem=recv_sems.at[at], device_id=(x, y, 1 - c), device_id_type=MESH)
            pairs.append((functools.partial(pltpu.make_async_remote_copy, src_ref=have, dst_ref=have, **sems),
                          functools.partial(pltpu.make_async_remote_copy, src_ref=have, dst_ref=_half_block(ref4, 2 * px + py, 1 - c), **sems)))
    return pairs


def _gather_forward(bufs):
    n = len(bufs)

    def body(*refs):
        outs = refs[n : 2 * n]
        send_sems, recv_sems = refs[2 * n :]
        d2d = _gather_d2d_copies(outs, (), send_sems, recv_sems)
        for forward, _ in d2d:
            forward().start()
        for forward, arrival in d2d:
            arrival().wait_recv()
            forward().wait_send()

    return pl.pallas_call(
        body,
        name="gather_forward",
        in_specs=[ANY] * n,
        out_specs=[ANY] * n,
        out_shape=[jax.ShapeDtypeStruct(b.shape, b.dtype) for b in bufs],
        input_output_aliases={k: k for k in range(n)},
        scratch_shapes=[pltpu.SemaphoreType.DMA((3 * n,)), pltpu.SemaphoreType.DMA((3 * n,))],
        compiler_params=pltpu.CompilerParams(has_side_effects=True),
    )(*bufs)


def _both_ends(**copy):
    maker = functools.partial(pltpu.make_async_remote_copy, **copy)
    return maker, maker


def _scatter_copies(parts, lands, send_sems, recv_sems):
    x, y, c = _position()
    return [_both_ends(src_ref=parts[k].at[2 * px + py], dst_ref=lands[k].at[j], send_sem=send_sems.at[3 * k + j],
                       recv_sem=recv_sems.at[3 * k + j], device_id=(px, py, c), device_id_type=MESH)
            for k in range(len(parts)) for j, (px, py) in enumerate(_other_chips(x, y))]


def _exchange_copies(grads, lands, send_sems, recv_sems):
    x, y, c = _position()
    return [_both_ends(src_ref=grads[k].at[:, :, pl.ds(pl.multiple_of((1 - c) * HALF, HALF), HALF)], dst_ref=lands[k],
                       send_sem=send_sems.at[k], recv_sem=recv_sems.at[k], device_id=(x, y, 1 - c), device_id_type=MESH)
            for k in range(len(grads))]


def _exchange_lands(grads4):
    return [jax.ShapeDtypeStruct((N_SHARDS, g.shape[1], HALF), g.dtype) for g in grads4]


def _scatter_lands(parts4):
    return [jax.ShapeDtypeStruct((3,) + g.shape[1:], g.dtype) for g in parts4]


def _small_gather_copies(blocks, lands, send_sems, recv_sems):
    x, y, c = _position()
    flip = lambda v, bit: 1 - v if bit else v
    return [_both_ends(src_ref=blocks[0], dst_ref=lands[0].at[4 * x + 2 * y + c], send_sem=send_sems.at[r - 1],
                       recv_sem=recv_sems.at[r - 1], device_id=(flip(x, r & 4), flip(y, r & 2), flip(c, r & 1)), device_id_type=MESH)
            for r in range(1, 8)]


def _split_start(name, srcs, land_shapes, make_copies, nsem, after=()):
    n, nl, na = len(srcs), len(land_shapes), len(after)
    lands = [lax.empty(a.shape, a.dtype) for a in land_shapes]

    def body(*refs):
        send_sems, recv_sems = refs[n + nl + na], refs[n + nl + na + 1]
        token = refs[2 * (n + nl) + na + 2]
        for send, _ in make_copies(refs[:n], refs[n : n + nl], send_sems, recv_sems):
            send().start()
        token[...] = jnp.zeros_like(token)

    hbm = lambda a: pltpu.HBM(a.shape, a.dtype)
    out = pl.pallas_call(
        body,
        name=name,
        in_specs=[HBM] * (n + nl) + [ANY] * na,
        out_specs=(SEM, SEM, *[HBM] * (n + nl), pl.BlockSpec(memory_space=pltpu.VMEM)),
        out_shape=(pltpu.SemaphoreType.DMA((nsem,)), pltpu.SemaphoreType.DMA((nsem,)), *[hbm(a) for a in list(srcs) + lands], TOKEN),
        input_output_aliases={k: 2 + k for k in range(n + nl)},
        compiler_params=pltpu.CompilerParams(has_side_effects=DATAFLOW),
    )(*[pltpu.with_memory_space_constraint(a, pltpu.HBM) for a in list(srcs) + lands], *after)
    return out[0], out[1], list(out[2 : 2 + n]), list(out[2 + n : 2 + n + nl]), out[2 + n + nl]


def _split_wait(name, send_sems, recv_sems, srcs, lands, make_copies, after):
    n, nl = len(srcs), len(lands)

    def body(*refs):
        for send, arrival in make_copies(refs[:n], refs[n : n + nl], refs[n + nl], refs[n + nl + 1]):
            send().wait_send()
            arrival().wait_recv()

    hbm = lambda a: pltpu.HBM(a.shape, a.dtype)
    out = pl.pallas_call(
        body,
        name=name,
        in_specs=[HBM] * (n + nl) + [SEM, SEM] + [ANY] * len(after),
        out_specs=tuple([HBM] * (n + nl)),
        out_shape=tuple(hbm(a) for a in list(srcs) + list(lands)),
        input_output_aliases={k: k for k in range(n + nl)},
        compiler_params=pltpu.CompilerParams(has_side_effects=DATAFLOW),
    )(*srcs, *lands, send_sems, recv_sems, *after)
    return list(out[:n]), list(out[n:])


def _join_copies(bufs, lands, send_sems, recv_sems, first=0):
    x, y, c = _position()
    half = lambda ref, core: ref.at[:, pl.ds(pl.multiple_of(core * HALF, HALF), HALF)]
    pairs = []
    for k, ref in enumerate(bufs):
        sems = dict(send_sem=send_sems.at[first + k], recv_sem=recv_sems.at[first + k], device_id=(x, y, 1 - c), device_id_type=MESH)
        pairs.append((functools.partial(pltpu.make_async_remote_copy, src_ref=half(ref, c), dst_ref=half(ref, c), **sems),
                      functools.partial(pltpu.make_async_remote_copy, src_ref=half(ref, c), dst_ref=half(ref, 1 - c), **sems)))
    return pairs


def _allgather_small(block):
    m_per, ncol = block.shape

    def body(x_ref, out_ref, send_sems, recv_sems, local_sem):
        x, y, c = _position()
        me, sibling = (x, y, c), (x, y, 1 - c)
        chips = _other_chips(x, y)

        def rows(px, py, pc):
            return out_ref.at[4 * px + 2 * py + pc]

        def copy(k, blk, to, src=None):
            return pltpu.make_async_remote_copy(
                src_ref=rows(*blk) if src is None else src, dst_ref=rows(*blk),
                send_sem=send_sems.at[k], recv_sem=recv_sems.at[k], device_id=to, device_id_type=MESH)

        mine = pltpu.make_async_copy(x_ref, rows(*me), local_sem)
        mine.start()
        first = [copy(0, me, sibling, src=x_ref)] + [copy(1 + j, me, (*chip, c), src=x_ref) for j, chip in enumerate(chips)]
        for cp in first:
            cp.start()
        passed = [copy(4 + j, (*chip, c), sibling) for j, chip in enumerate(chips)]
        for j, chip in enumerate(chips):
            copy(1 + j, (*chip, c), me).wait_recv()
            passed[j].start()
        copy(0, sibling, me).wait_recv()
        for j, chip in enumerate(chips):
            copy(4 + j, (*chip, 1 - c), me).wait_recv()
        for cp in first + passed:
            cp.wait_send()
        mine.wait()

    return pl.pallas_call(
        body,
        name="allgather_small",
        in_specs=[pl.BlockSpec(memory_space=pltpu.VMEM)],
        out_specs=pl.BlockSpec(memory_space=pltpu.VMEM),
        out_shape=jax.ShapeDtypeStruct((8, m_per, ncol), block.dtype),
        scratch_shapes=[pltpu.SemaphoreType.DMA((7,)), pltpu.SemaphoreType.DMA((7,)), pltpu.SemaphoreType.DMA],
        compiler_params=pltpu.CompilerParams(has_side_effects=True, vmem_limit_bytes=32 * MIB),
    )(block)


SMALL_NAMES = ["norm1_g", "b_decay_f", "b_decay_b", "gla_norm_g", "gmlp_ln_g", "gmlp_ln_b", "w_spatial", "b_spatial", "norm2_g", "final_norm_g"]


def _pack_small(parts, decay_parts):
    flat = jnp.concatenate([a.reshape(-1) for a in parts])
    flat = jnp.pad(flat, (0, SMALL_ROWS * LANES - flat.shape[0])).reshape(SMALL_ROWS, LANES)
    return jnp.concatenate([flat] + [d.reshape(-1, LANES) for d in decay_parts], axis=0)


def kernel(x, norm1_g, w_in, w_decay_f, b_decay_f, w_decay_b, b_decay_b, gla_norm_g, gmlp_ln_g, gmlp_ln_b, w_spatial, b_spatial, w_out, norm2_g, w_gate, w_up, w_down, final_norm_g, loss_target, m_norm1_g, m_w_in, m_w_decay_f, m_b_decay_f, m_w_decay_b, m_b_decay_b, m_gla_norm_g, m_gmlp_ln_g, m_gmlp_ln_b, m_w_spatial, m_b_spatial, m_w_out, m_norm2_g, m_w_gate, m_w_up, m_w_down, m_final_norm_g, v_norm1_g, v_w_in, v_w_decay_f, v_b_decay_f, v_w_decay_b, v_b_decay_b, v_gla_norm_g, v_gmlp_ln_g, v_gmlp_ln_b, v_w_spatial, v_b_spatial, v_w_out, v_norm2_g, v_w_gate, v_w_up, v_w_down, v_final_norm_g):
    args = dict(locals())
    cx, cy, cc = lax.axis_index("x"), lax.axis_index("y"), lax.axis_index("c")
    shard = 2 * cx + cy
    xs = x[0]
    target = loss_target[0]

    big_names = ["w_in", "w_out", "w_gate", "w_up", "w_down"]
    transposed = ("w_in", "w_gate", "w_up")
    rows_of = lambda pre, k: jnp.transpose(args[pre + k][0]) if k in transposed else args[pre + k][0]
    big_shards = {k: rows_of("", k) for k in big_names}
    s_arr = shard.reshape(1).astype(jnp.int32)
    sc_arr = jnp.stack([shard, cc]).astype(jnp.int32)
    zero_token = jnp.zeros(TOKEN_SHAPE, F32)
    w_send, w_recv, (w_in4,), _, token_w_in = _split_start(
        "w_in_gather_start", [_cast_into_slot(big_shards["w_in"], s_arr, zero_token)], [], _gather_ici_copies, 3)
    late = ["w_out", "w_gate", "w_up", "w_down"]
    late_slots = [_cast_into_slot(big_shards[k], s_arr, token_w_in) for k in late]
    dec_block = jnp.concatenate([w_decay_f[0].reshape(-1, LANES), w_decay_b[0].reshape(-1, LANES)], axis=0)
    dec_all = _allgather_small(dec_block)
    (w_in4,), _ = _split_wait("w_in_gather_wait", w_send, w_recv, [w_in4], [], _gather_ici_copies, (dec_all, *late_slots))
    (w_in4,) = _gather_forward([w_in4])
    w_in_t = w_in4.reshape(PROJ_W, D_MODEL)
    g_send, g_recv, late_bufs, _, token_gather = _split_start(
        "gather_start", late_slots, [], _gather_ici_copies, 3 * len(late), after=(w_in4,))
    dec_all = dec_all[::2].reshape(N_SHARDS, 2, LOWRANK, KEY_W // N_SHARDS)
    wdf_full = jnp.transpose(dec_all[:, 0], (1, 0, 2)).reshape(LOWRANK, KEY_W)
    wdb_full = jnp.transpose(dec_all[:, 1], (1, 0, 2)).reshape(LOWRANK, KEY_W)
    wd_pad_f = jnp.zeros((LANES, KEY_W), F32).at[0:LOWRANK].set(wdf_full).astype(BF16)
    wd_pad_b = jnp.zeros((LANES, KEY_W), F32).at[LOWRANK : 2 * LOWRANK].set(wdb_full).astype(BF16)

    ws_bf = w_spatial[0].astype(BF16)
    wst_bf = jnp.transpose(w_spatial[0], (0, 2, 1)).astype(BF16)
    bs_col = b_spatial[0].reshape(GMLP_GROUPS, GMLP_CHUNK, 1)

    p = _inproj(xs, norm1_g, w_in_t, token_gather)
    o_f, st_f = _gla_fwd(p, wd_pad_f, b_decay_f, token_gather, reverse=False)
    o_b, st_b = _gla_fwd(p, wd_pad_b, b_decay_b, token_gather, reverse=True)
    late_bufs, _ = _split_wait("gather_wait", g_send, g_recv, late_bufs, [], _gather_ici_copies, (o_f, o_b))
    f_send, f_recv, late_bufs, _, token_forward = _split_start("forward_start", late_bufs, [], _gather_d2d_copies, 3 * len(late))
    (w_out4,), _ = _split_wait("w_out_forward_wait", f_send, f_recv, late_bufs[:1], [], _gather_d2d_copies, (token_forward,))
    w_out_full = w_out4.reshape(-1, D_MODEL)
    x1, ycat = _mixer_out(xs, o_f, o_b, p, gla_norm_g, gmlp_ln_g, gmlp_ln_b, ws_bf, bs_col, w_out_full, token_forward)
    ffn_bufs, _ = _split_wait(
        "forward_wait", f_send, f_recv, late_bufs[1:], [], functools.partial(_gather_d2d_copies, first=3), (x1,))
    wg_t, wu_t, wd = [b.reshape(-1, D_MODEL) for b in ffn_bufs]
    gf = final_norm_g.reshape(1, D_MODEL)
    h2, gate, up, act, dx2, loss_acc, dgf = _ffn_fwd(x1, target, norm2_g, gf, wg_t, wu_t, wd)

    dgate, dup, dx1, dg2 = _ffn_bwd(dx2, gate, up, x1, norm2_g, wg_t, wu_t, wd)
    ffn_grads4 = [g.reshape(N_SHARDS, FF_SHARD, D_MODEL) for g in _ffn_wgrad(h2, dgate, dup, act, dx2)]
    e_send, e_recv, e_srcs, e_lands, token_exchange = _split_start(
        "exchange_start", ffn_grads4, _exchange_lands(ffn_grads4), _exchange_copies, len(ffn_grads4))
    do, dg, du, dvv, dwo, dgn, dlng, dlnb, dws, dbs = _mixer_bwd(
        dx1, ycat, o_f, o_b, p, gla_norm_g, gmlp_ln_g, gmlp_ln_b, ws_bf, wst_bf, bs_col, w_out_full, token_exchange)
    ffn_mine, ffn_other = _split_wait("exchange_wait", e_send, e_recv, e_srcs, e_lands, _exchange_copies, (do,))
    ffn_parts = _add_halves(ffn_mine, ffn_other, sc_arr)
    ffn_payload = [pb for _, pb in ffn_parts]
    s_send, s_recv, s_parts, s_lands, token_scatter = _split_start(
        "scatter_start", ffn_payload, _scatter_lands(ffn_payload), _scatter_copies, 3 * len(ffn_payload))
    dq_f, dk_f, dv_f, dlr_f, dwdec_f, dbdec_f = _gla_bwd(p, do, st_f, wd_pad_f, b_decay_f, token_scatter, reverse=False)
    dq, dk, dv, dlr, dwdec_b, dbdec_b = _gla_bwd(
        p, do, st_b, wd_pad_b, b_decay_b, token_scatter, reverse=True, other=(dq_f, dk_f, dv_f, dlr_f))
    dwin_t, dp = _inproj_wgrad(xs, norm1_g, dq, dk, dv, dg, du, dvv, dlr)
    _, ffn_recv = _split_wait("scatter_wait", s_send, s_recv, s_parts, s_lands, _scatter_copies, (dwin_t,))

    dwin4 = dwin_t.reshape(N_SHARDS, PROJ_W // N_SHARDS, D_MODEL)
    dwo4 = dwo.reshape(N_SHARDS, D_MODEL // N_SHARDS, D_MODEL)
    proj_grads4 = [dwin4, dwo4]
    x_send, x_recv, x_srcs, x_lands, token_swap = _split_start(
        "proj_exchange_start", proj_grads4, _exchange_lands(proj_grads4), _exchange_copies, len(proj_grads4))
    ffn_bufs = [_add_partials(pf, r, sc_arr, token_swap) for (pf, _), r in zip(ffn_parts, ffn_recv)]
    proj_mine, proj_other = _split_wait("proj_exchange_wait", x_send, x_recv, x_srcs, x_lands, _exchange_copies, tuple(ffn_bufs))
    proj_parts = [_add_halves([g], [r], sc_arr)[0] for g, r in zip(proj_mine, proj_other)]
    proj_payload = [pb for _, pb in proj_parts]
    n_proj = len(proj_payload)
    ffn_join_copies = functools.partial(_join_copies, first=3 * n_proj)
    scatter_and_join = lambda srcs, lands, send_sems, recv_sems: (
        _scatter_copies(srcs[:n_proj], lands, send_sems, recv_sems) + ffn_join_copies(srcs[n_proj:], (), send_sems, recv_sems))
    p_send, p_recv, started, p_lands, token_join = _split_start(
        "proj_scatter_start", proj_payload + ffn_bufs, _scatter_lands(proj_payload), scatter_and_join, 3 * n_proj + len(ffn_bufs))
    p_parts, ffn_bufs = started[:n_proj], started[n_proj:]
    dx, dg1 = _inproj_dx(xs, dx1, norm1_g, w_in_t, dp, token_join)
    _, proj_recv = _split_wait("proj_scatter_wait", p_send, p_recv, p_parts, p_lands, _scatter_copies, (dx,))

    dwdec_f16 = dwdec_f[0:LOWRANK]
    dwdec_b16 = dwdec_b[LOWRANK : 2 * LOWRANK]
    shard_major = lambda a: jnp.transpose(a.reshape(LOWRANK, N_SHARDS, KEY_W // N_SHARDS), (1, 0, 2))
    small_grads = {
        "norm1_g": dg1, "b_decay_f": dbdec_f, "b_decay_b": dbdec_b, "gla_norm_g": dgn, "gmlp_ln_g": dlng, "gmlp_ln_b": dlnb,
        "w_spatial": dws, "b_spatial": dbs, "norm2_g": dg2, "final_norm_g": dgf,
    }
    g_pack = _pack_small([small_grads[k] for k in SMALL_NAMES] + [loss_acc], [shard_major(dwdec_f16), shard_major(dwdec_b16)])
    proj_bufs = [_add_partials(pf, r, sc_arr, token_join) for (pf, _), r in zip(proj_parts, proj_recv)]
    proj_join_copies = functools.partial(_join_copies, first=7)
    tail_copies = lambda srcs, lands, send_sems, recv_sems: (
        _small_gather_copies(srcs[:1], lands, send_sems, recv_sems) + proj_join_copies(srcs[1:], (), send_sems, recv_sems))
    t_send, t_recv, (g_pack, *proj_bufs), g_lands, token_tail = _split_start(
        "tail_start", [g_pack] + proj_bufs, [jax.ShapeDtypeStruct((8, SMALL_TOTAL, LANES), F32)], tail_copies, 7 + len(proj_bufs))

    ffn_bufs, _ = _split_wait("join_wait", p_send, p_recv, ffn_bufs, [], ffn_join_copies, (dx, token_tail))
    adamw = lambda k, g: _adamw(big_shards[k], g, rows_of("m_", k), rows_of("v_", k))
    big_updates = {k: adamw(k, g) for k, g in zip(big_names[2:], ffn_bufs)}
    proj_bufs, _ = _split_wait(
        "proj_join_wait", t_send, t_recv, proj_bufs, [], proj_join_copies, tuple(u[1] for u in big_updates.values()))
    big_updates.update({k: adamw(k, g) for k, g in zip(big_names[:2], proj_bufs)})

    (g_pack,), (g_all,) = _split_wait(
        "small_gather_wait", t_send, t_recv, [g_pack], g_lands, _small_gather_copies, tuple(u[1] for u in big_updates.values()))
    pack_own = lambda pre: _pack_small([args[pre + k] for k in SMALL_NAMES], [args[pre + "w_decay_f"], args[pre + "w_decay_b"]])
    small_updates, loss_row = _adamw_small(g_all, g_pack, pack_own(""), pack_own("m_"), pack_own("v_"), [args[k] for k in SMALL_NAMES])

    names = ["norm1_g", "w_in", "w_decay_f", "b_decay_f", "w_decay_b", "b_decay_b", "gla_norm_g", "gmlp_ln_g", "gmlp_ln_b",
             "w_spatial", "b_spatial", "w_out", "norm2_g", "w_gate", "w_up", "w_down", "final_norm_g"]
    results = {"g": {}, "d": {}, "m": {}, "v": {}}
    for tag, arrays in zip("gdmv", small_updates):
        for k, a in zip(SMALL_NAMES + ["w_decay_f", "w_decay_b"], arrays):
            results[tag][k] = a.reshape(args[k].shape)
    for k in big_names:
        for tag, a in zip("gdmv", big_updates[k]):
            results[tag][k] = (jnp.transpose(a) if k in transposed else a).reshape(args[k].shape)

    loss = loss_row[0, 0]
    grad_x = dx.reshape(x.shape)
    return (loss, grad_x, *[results["g"][k] for k in names], *[results["d"][k] for k in names],
            *[results["m"][k] for k in names], *[results["v"][k] for k in names])
```

```python
import functools
import math

import jax
import jax.numpy as jnp
from jax import lax
from jax.experimental import pallas as pl
from jax.experimental.pallas import tpu as pltpu

F32, BF16 = jnp.float32, jnp.bfloat16

D_MODEL = 1024
GLA_HEADS = 4
GLA_DK = 64
GLA_DV = 128
KEY_W = GLA_HEADS * GLA_DK
GLA_W = GLA_HEADS * GLA_DV
GMLP_W = 512
GMLP_GROUPS = 4
GMLP_CHUNK = 128
LOWRANK = 16
GLA_CHUNK = 64
GLA_TAU = 16.0
PROJ_W = 2592
PROJ_WP = 2688
D_FF = 2816
N_SHARDS = 4
FF_SHARD = D_FF // N_SHARDS
EPS = 1e-6
LANES = 128
TOKEN_SHAPE = (8, LANES)
MIB = 1024 * 1024

ADAM_LR = 0.001
ADAM_B1 = 0.9
ADAM_B2 = 0.999
ADAM_EPS = 1e-08
ADAM_WD = 0.01
ADAM_STEP = 10

COL_Q, COL_K = 0, 256
COL_V, COL_G, COL_U, COL_VV = 512, 1024, 1536, 2048
COL_LR = 2560
ROW_LR, ROW_UV = 1536, 1568
HALF = D_MODEL // 2

MESH = pl.DeviceIdType.MESH


def _nn(a, b):
    return jnp.dot(a, b, preferred_element_type=F32)


def _nt(a, b):
    return lax.dot_general(a, b, (((1,), (1,)), ((), ())), preferred_element_type=F32)


def _tn(a, b):
    return lax.dot_general(a, b, (((0,), (0,)), ((), ())), preferred_element_type=F32)


def _bnn(a, b):
    return jnp.einsum("nik,nkj->nij", a, b, preferred_element_type=F32)


def _bnt(a, b):
    return jnp.einsum("nik,njk->nij", a, b, preferred_element_type=F32)


def _btn(a, b):
    return jnp.einsum("nki,nkj->nij", a, b, preferred_element_type=F32)


def _resident(shape):
    zeros = (0,) * len(shape)
    return pl.BlockSpec(shape, lambda *_: zeros, pipeline_mode=pl.Buffered(1))


def _params(vmem_mib, semantics=("arbitrary",)):
    return pltpu.CompilerParams(vmem_limit_bytes=vmem_mib * MIB, dimension_semantics=semantics)


def _sigmoid(x):
    return 1.0 / (1.0 + jnp.exp(-x))


def _gelu(x):
    return 0.5 * x * (1.0 + lax.erf(x * (1.0 / math.sqrt(2.0))))


def _gelu_and_grad(x):
    cdf = 0.5 * (1.0 + lax.erf(x * (1.0 / math.sqrt(2.0))))
    return x * cdf, cdf + x * jnp.exp(-0.5 * x * x) * (1.0 / math.sqrt(2.0 * math.pi))


def _log_sigmoid(x):
    return jnp.minimum(x, 0.0) - jnp.log(1.0 + jnp.exp(-jnp.abs(x)))


def _rms_bwd(dxh, xh, r):
    return r * (dxh - xh * jnp.mean(dxh * xh, axis=-1, keepdims=True))


def _chunk_cumsum(v, row_in_chunk, reverse):
    rows = v.shape[0]
    for sh in (1, 2, 4, 8, 16, 32):
        if reverse:
            v = v + jnp.where(row_in_chunk + sh < GLA_CHUNK, pltpu.roll(v, rows - sh, axis=0), 0.0)
        else:
            v = v + jnp.where(row_in_chunk >= sh, pltpu.roll(v, sh, axis=0), 0.0)
    return v


def _inproj(x, g1, w_in_t, token):
    seq = x.shape[0]
    tm = min(seq, 512)

    def body(x_ref, g_ref, w_ref, token_ref, p_ref):
        xv = x_ref[...]
        r = lax.rsqrt(jnp.mean(xv * xv, axis=-1, keepdims=True) + EPS)
        h = (xv * r * g_ref[...]).astype(BF16)
        p_ref[:, 0:COL_U] = _nt(h, w_ref[0:ROW_LR, :])
        p_ref[:, COL_U:COL_LR] = _nt(h, w_ref[ROW_UV:PROJ_W, :])
        p_ref[:, COL_LR:PROJ_WP] = _nt(h, w_ref[ROW_LR : ROW_LR + LANES, :])

    return pl.pallas_call(
        body,
        name="inproj",
        grid=(seq // tm,),
        in_specs=[pl.BlockSpec((tm, D_MODEL), lambda i: (i, 0)), _resident((1, D_MODEL)), _resident((PROJ_W, D_MODEL)), _resident(TOKEN_SHAPE)],
        out_specs=pl.BlockSpec((tm, PROJ_WP), lambda i: (i, 0)),
        out_shape=jax.ShapeDtypeStruct((seq, PROJ_WP), F32),
        compiler_params=_params(48, ("parallel",)),
    )(x, g1, w_in_t, token)


def _gla_tile(seq):
    return min(seq, 1024)


def _gla_decay_terms(lr_bf, wd_ref, bd_ref, pair, row_in_chunk, reverse, n):
    cols = pl.ds(pair * LANES, LANES)
    pre = _nn(lr_bf, wd_ref[:, cols]) + bd_ref[:, cols]
    la = _log_sigmoid(pre) * (1.0 / GLA_TAU)
    b = _chunk_cumsum(la, row_in_chunk, reverse)
    b3 = b.reshape(n, GLA_CHUNK, LANES)
    blast = b3[:, 0:1, :] if reverse else b3[:, GLA_CHUNK - 1 : GLA_CHUNK, :]
    return pre, b3, blast


def _gla_fwd(p, wd_pad, bd, token, reverse):
    seq = p.shape[0]
    tg = _gla_tile(seq)
    nt = seq // tg
    n = tg // GLA_CHUNK
    scale = GLA_DK**-0.5

    def tile(i):
        return nt - 1 - i if reverse else i

    def body(q_ref, k_ref, v_ref, lr_ref, wd_ref, bd_ref, token_ref, o_ref, st_ref, carry):
        @pl.when(pl.program_id(0) == 0)
        def _():
            carry[...] = jnp.zeros_like(carry)

        lr_bf = lr_ref[...].astype(BF16)
        states = [carry[h] for h in range(GLA_HEADS)]
        row_in_chunk = lax.broadcasted_iota(jnp.int32, (tg, LANES), 0) % GLA_CHUNK
        lane_head = lax.broadcasted_iota(jnp.int32, (1, LANES), 1) // GLA_DK
        tt = lax.broadcasted_iota(jnp.int32, (GLA_CHUNK, GLA_CHUNK), 0)
        ss = lax.broadcasted_iota(jnp.int32, (GLA_CHUNK, GLA_CHUNK), 1)
        causal = (tt <= ss) if reverse else (tt >= ss)
        order = range(n - 1, -1, -1) if reverse else range(n)
        heads = range(GLA_HEADS)
        qds, vhs, decs, sc_raw, dst = {}, {}, {}, {}, {}
        for pair in range(2):
            cols = pl.ds(pair * LANES, LANES)
            _, b3, blast = _gla_decay_terms(lr_bf, wd_ref, bd_ref, pair, row_in_chunk, reverse, n)
            q3 = q_ref[:, cols].reshape(n, GLA_CHUNK, LANES) * scale
            k3 = k_ref[:, cols].reshape(n, GLA_CHUNK, LANES)
            qd = q3 * jnp.exp(b3)
            kd = (k3 * jnp.exp(-b3)).astype(BF16)
            kte = k3 * jnp.exp(blast - b3)
            decs[pair] = jnp.exp(blast)
            qds[pair] = qd.astype(BF16)
            m0 = (lane_head == 0).astype(F32)
            m1 = (lane_head == 1).astype(F32)
            q_both = jnp.concatenate([(qd * m0).astype(BF16), (qd * m1).astype(BF16)], axis=1)
            sc_both = _bnt(q_both, kd)
            for hh, m in ((0, m0), (1, m1)):
                h = 2 * pair + hh
                vhs[h] = v_ref[:, pl.ds(h * GLA_DV, GLA_DV)].reshape(n, GLA_CHUNK, GLA_DV).astype(BF16)
                sc_raw[h] = sc_both[:, hh * GLA_CHUNK : (hh + 1) * GLA_CHUNK, :]
                dst[h] = _btn(vhs[h], (kte * m).astype(BF16))
        o_intra, befores = {}, {}
        for h in heads:
            o_intra[h] = _bnn(jnp.where(causal, sc_raw[h], 0.0).astype(BF16), vhs[h])
            st, before = states[h], [None] * n
            for j in order:
                before[j] = st
                st = st * decs[h // 2][j] + dst[h][j]
            states[h] = st
            befores[h] = jnp.stack(before).astype(BF16)
        outs = {}
        for pair in range(2):
            both = jnp.concatenate([befores[2 * pair], befores[2 * pair + 1]], axis=1)
            o_inter = _bnt(qds[pair], both)
            for hh in range(2):
                h = 2 * pair + hh
                outs[h] = (o_intra[h] + o_inter[:, :, hh * GLA_DV : (hh + 1) * GLA_DV]).reshape(tg, GLA_DV)
        for h in range(GLA_HEADS):
            o_ref[:, pl.ds(h * GLA_DV, GLA_DV)] = outs[h]
            st_ref[:, h] = befores[h]
            carry[h] = states[h]

    nchunks = seq // GLA_CHUNK
    return pl.pallas_call(
        body,
        name="gla_fwd_rev" if reverse else "gla_fwd",
        grid=(nt,),
        in_specs=[
            pl.BlockSpec((tg, KEY_W), lambda i: (tile(i), COL_Q // KEY_W)),
            pl.BlockSpec((tg, KEY_W), lambda i: (tile(i), COL_K // KEY_W)),
            pl.BlockSpec((tg, GLA_W), lambda i: (tile(i), COL_V // GLA_W)),
            pl.BlockSpec((tg, LANES), lambda i: (tile(i), COL_LR // LANES)),
            _resident((LANES, KEY_W)),
            _resident((1, KEY_W)),
            _resident(TOKEN_SHAPE),
        ],
        out_specs=[
            pl.BlockSpec((tg, GLA_W), lambda i: (tile(i), 0)),
            pl.BlockSpec((n, GLA_HEADS, GLA_DV, LANES), lambda i: (tile(i), 0, 0, 0)),
        ],
        out_shape=[
            jax.ShapeDtypeStruct((seq, GLA_W), F32),
            jax.ShapeDtypeStruct((nchunks, GLA_HEADS, GLA_DV, LANES), BF16),
        ],
        scratch_shapes=[pltpu.VMEM((GLA_HEADS, GLA_DV, LANES), F32)],
        compiler_params=_params(48),
    )(p, p, p, p, wd_pad, bd, token)


def _mixer_out(x, o_f, o_b, p, gn, lng, lnb, ws_bf, bs_col, w_out, token):
    seq = x.shape[0]
    tm = min(seq, 512)

    def body(x_ref, of_ref, ob_ref, g_ref, u_ref, vv_ref, gn_ref, lng_ref, lnb_ref, ws_ref, bs_ref, wo_ref, token_ref, x1_ref, yc_ref, vn_sc):
        for h in range(GLA_HEADS):
            cols = pl.ds(h * GLA_DV, GLA_DV)
            oh = of_ref[:, cols] + ob_ref[:, cols]
            on = oh * lax.rsqrt(jnp.mean(oh * oh, axis=-1, keepdims=True) + EPS)
            gh = g_ref[:, cols]
            yc_ref[:, cols] = (on * gn_ref[:, cols] * (gh * _sigmoid(gh))).astype(BF16)
        zv = _gelu(vv_ref[...])
        xc = zv - jnp.mean(zv, axis=-1, keepdims=True)
        vhat = xc * lax.rsqrt(jnp.mean(xc * xc, axis=-1, keepdims=True) + EPS)
        vn_sc[...] = (vhat * lng_ref[...] + lnb_ref[...]).astype(BF16)
        for c in range(tm // GMLP_CHUNK):
            rows = pl.ds(c * GMLP_CHUNK, GMLP_CHUNK)
            for g in range(GMLP_GROUPS):
                cols = pl.ds(g * LANES, LANES)
                s = _nn(ws_ref[g], vn_sc[rows, cols]) + bs_ref[g]
                yc_ref[rows, pl.ds(GLA_W + g * LANES, LANES)] = (_gelu(u_ref[rows, cols]) * s).astype(BF16)
        x1_ref[...] = x_ref[...] + _nn(yc_ref[...], wo_ref[...])

    row = lambda w: pl.BlockSpec((tm, w), lambda i: (i, 0))
    pcol = lambda col: pl.BlockSpec((tm, GLA_W), lambda i: (i, col // GLA_W))
    return pl.pallas_call(
        body,
        name="mixer_out",
        grid=(seq // tm,),
        in_specs=[
            row(D_MODEL), row(GLA_W), row(GLA_W), pcol(COL_G), pcol(COL_U), pcol(COL_VV),
            _resident((1, GLA_W)), _resident((1, GMLP_W)), _resident((1, GMLP_W)),
            _resident((GMLP_GROUPS, GMLP_CHUNK, GMLP_CHUNK)), _resident((GMLP_GROUPS, GMLP_CHUNK, 1)),
            _resident((D_MODEL, D_MODEL)), _resident(TOKEN_SHAPE),
        ],
        out_specs=[row(D_MODEL), row(D_MODEL)],
        out_shape=[jax.ShapeDtypeStruct((seq, D_MODEL), F32), jax.ShapeDtypeStruct((seq, D_MODEL), BF16)],
        scratch_shapes=[pltpu.VMEM((tm, GMLP_W), BF16)],
        compiler_params=_params(48, ("parallel",)),
    )(x, o_f, o_b, p, p, p, gn, lng, lnb, ws_bf, bs_col, w_out, token)


def _ffn_fwd(x1, target, g2, gf, wg_t, wu_t, wd):
    seq = x1.shape[0]
    tm = min(seq, 256)

    def body(x1_ref, t_ref, g2_ref, gf_ref, wg_ref, wu_ref, wd_ref, h2_ref, gate_ref, up_ref, act_ref, dx2_ref, loss_ref, dgf_ref):
        @pl.when(pl.program_id(0) == 0)
        def _():
            loss_ref[...] = jnp.zeros_like(loss_ref)
            dgf_ref[...] = jnp.zeros_like(dgf_ref)

        x1v = x1_ref[...]
        h2 = (x1v * lax.rsqrt(jnp.mean(x1v * x1v, axis=-1, keepdims=True) + EPS) * g2_ref[...]).astype(BF16)
        h2_ref[...] = h2
        gate = _nt(h2, wg_ref[...])
        up = _nt(h2, wu_ref[...])
        act = (gate * _sigmoid(gate) * up).astype(BF16)
        gate_ref[...] = gate
        up_ref[...] = up
        act_ref[...] = act
        x2 = x1v + _nn(act, wd_ref[...])
        rf = lax.rsqrt(jnp.mean(x2 * x2, axis=-1, keepdims=True) + EPS)
        xh = x2 * rf
        err = xh * gf_ref[...] - t_ref[...]
        loss_ref[...] += 0.5 * jnp.sum(jnp.mean(err * err, axis=-1, keepdims=True))
        dy = err * (1.0 / D_MODEL)
        dgf_ref[...] += jnp.sum(dy * xh, axis=0, keepdims=True)
        dx2_ref[...] = _rms_bwd(dy * gf_ref[...], xh, rf)

    row = lambda w: pl.BlockSpec((tm, w), lambda i: (i, 0))
    weight = _resident((D_FF, D_MODEL))
    return pl.pallas_call(
        body,
        name="ffn_fwd",
        grid=(seq // tm,),
        in_specs=[row(D_MODEL), row(D_MODEL), _resident((1, D_MODEL)), _resident((1, D_MODEL)), weight, weight, weight],
        out_specs=[row(D_MODEL), row(D_FF), row(D_FF), row(D_FF), row(D_MODEL),
                   pl.BlockSpec((1, LANES), lambda i: (0, 0)), pl.BlockSpec((1, D_MODEL), lambda i: (0, 0))],
        out_shape=[
            jax.ShapeDtypeStruct((seq, D_MODEL), BF16),
            jax.ShapeDtypeStruct((seq, D_FF), F32),
            jax.ShapeDtypeStruct((seq, D_FF), F32),
            jax.ShapeDtypeStruct((seq, D_FF), BF16),
            jax.ShapeDtypeStruct((seq, D_MODEL), F32),
            jax.ShapeDtypeStruct((1, LANES), F32),
            jax.ShapeDtypeStruct((1, D_MODEL), F32),
        ],
        compiler_params=_params(56),
    )(x1, target, g2, gf, wg_t, wu_t, wd)


def _ffn_bwd(dx2, gate, up, x1, g2, wg_t, wu_t, wd):
    seq = x1.shape[0]
    tm = min(seq, 256)

    def body(dx2_ref, gate_ref, up_ref, x1_ref, g2_ref, wg_ref, wu_ref, wd_ref, dgate_ref, dup_ref, dx1_ref, dg2_ref):
        @pl.when(pl.program_id(0) == 0)
        def _():
            dg2_ref[...] = jnp.zeros_like(dg2_ref)

        dx2v = dx2_ref[...]
        dact = _nt(dx2v.astype(BF16), wd_ref[...])
        gate = gate_ref[...]
        sg = _sigmoid(gate)
        dgate = (dact * up_ref[...] * (sg * (1.0 + gate * (1.0 - sg)))).astype(BF16)
        dup = (dact * (gate * sg)).astype(BF16)
        dgate_ref[...] = dgate
        dup_ref[...] = dup
        dh2 = _nn(dgate, wg_ref[...]) + _nn(dup, wu_ref[...])
        x1v = x1_ref[...]
        r2 = lax.rsqrt(jnp.mean(x1v * x1v, axis=-1, keepdims=True) + EPS)
        xh = x1v * r2
        dg2_ref[...] += jnp.sum(dh2 * xh, axis=0, keepdims=True)
        dx1_ref[...] = dx2v + _rms_bwd(dh2 * g2_ref[...], xh, r2)

    row = lambda w: pl.BlockSpec((tm, w), lambda i: (i, 0))
    weight = _resident((D_FF, D_MODEL))
    return pl.pallas_call(
        body,
        name="ffn_bwd",
        grid=(seq // tm,),
        in_specs=[row(D_MODEL), row(D_FF), row(D_FF), row(D_MODEL), _resident((1, D_MODEL)), weight, weight, weight],
        out_specs=[row(D_FF), row(D_FF), row(D_MODEL), pl.BlockSpec((1, D_MODEL), lambda i: (0, 0))],
        out_shape=[
            jax.ShapeDtypeStruct((seq, D_FF), BF16),
            jax.ShapeDtypeStruct((seq, D_FF), BF16),
            jax.ShapeDtypeStruct((seq, D_MODEL), F32),
            jax.ShapeDtypeStruct((1, D_MODEL), F32),
        ],
        compiler_params=_params(56),
    )(dx2, gate, up, x1, g2, wg_t, wu_t, wd)


WGRAD_ROWS = D_FF // 2


def _ffn_wgrad(h2, dgate, dup, act, dx2):
    seq = h2.shape[0]
    tm = min(seq, 512)

    def body(h2_ref, dgate_ref, dup_ref, act_ref, dx2_ref, dwg_ref, dwu_ref, dwd_ref):
        @pl.when(pl.program_id(1) == 0)
        def _():
            dwg_ref[...] = jnp.zeros_like(dwg_ref)
            dwu_ref[...] = jnp.zeros_like(dwu_ref)
            dwd_ref[...] = jnp.zeros_like(dwd_ref)

        h2v = h2_ref[...]
        dwg_ref[...] += _tn(dgate_ref[...], h2v)
        dwu_ref[...] += _tn(dup_ref[...], h2v)
        dwd_ref[...] += _tn(act_ref[...], dx2_ref[...].astype(BF16))

    ff = pl.BlockSpec((tm, WGRAD_ROWS), lambda j, i: (i, j))
    row = pl.BlockSpec((tm, D_MODEL), lambda j, i: (i, 0))
    out = pl.BlockSpec((WGRAD_ROWS, D_MODEL), lambda j, i: (j, 0))
    return pl.pallas_call(
        body,
        name="ffn_wgrad",
        grid=(D_FF // WGRAD_ROWS, seq // tm),
        in_specs=[row, ff, ff, ff, row],
        out_specs=[out, out, out],
        out_shape=[jax.ShapeDtypeStruct((D_FF, D_MODEL), F32)] * 3,
        compiler_params=_params(56, ("parallel", "arbitrary")),
    )(h2, dgate, dup, act, dx2)


def _mixer_bwd(dx1, ycat, o_f, o_b, p, gn, lng, lnb, ws_bf, wst_bf, bs_col, w_out, token):
    seq = dx1.shape[0]
    tm = min(seq, 512)
    nsteps = seq // tm

    def body(dx1_ref, yc_ref, of_ref, ob_ref, g_ref, u_ref, vv_ref, gn_ref, lng_ref, lnb_ref, ws_ref, wst_ref, bs_ref, wo_ref, token_ref,
             do_ref, dg_ref, du_ref, dvv_ref, dwo_ref, dgn_ref, dlng_ref, dlnb_ref, dws_ref, dbs_ref, vn_sc, dvn_sc, dbs_acc):
        step = pl.program_id(0)

        @pl.when(step == 0)
        def _():
            for r in (dwo_ref, dgn_ref, dlng_ref, dlnb_ref, dws_ref, dbs_acc):
                r[...] = jnp.zeros_like(r)

        dx1b = dx1_ref[...].astype(BF16)
        dyc = _nt(dx1b, wo_ref[...])
        dwo_ref[...] += _tn(yc_ref[...], dx1b)
        for h in range(GLA_HEADS):
            cols = pl.ds(h * GLA_DV, GLA_DV)
            dya = dyc[:, h * GLA_DV : (h + 1) * GLA_DV]
            oh = of_ref[:, cols] + ob_ref[:, cols]
            rn = lax.rsqrt(jnp.mean(oh * oh, axis=-1, keepdims=True) + EPS)
            on = oh * rn
            gh = g_ref[:, cols]
            sg = _sigmoid(gh)
            sil = gh * sg
            gnh = gn_ref[:, cols]
            dgn_ref[:, cols] += jnp.sum(dya * on * sil, axis=0, keepdims=True)
            dg_ref[:, cols] = (dya * on * gnh * (sg * (1.0 + gh * (1.0 - sg)))).astype(BF16)
            do_ref[:, cols] = _rms_bwd(dya * gnh * sil, on, rn)
        vv = vv_ref[...]
        zv, zv_grad = _gelu_and_grad(vv)
        xc = zv - jnp.mean(zv, axis=-1, keepdims=True)
        rstd = lax.rsqrt(jnp.mean(xc * xc, axis=-1, keepdims=True) + EPS)
        vhat = xc * rstd
        vn_sc[...] = (vhat * lng_ref[...] + lnb_ref[...]).astype(BF16)
        for c in range(tm // GMLP_CHUNK):
            rows = pl.ds(c * GMLP_CHUNK, GMLP_CHUNK)
            for g in range(GMLP_GROUPS):
                cols = pl.ds(g * LANES, LANES)
                vn = vn_sc[rows, cols]
                s = _nn(ws_ref[g], vn) + bs_ref[g]
                dyb = dyc[c * GMLP_CHUNK : (c + 1) * GMLP_CHUNK, GLA_W + g * LANES : GLA_W + (g + 1) * LANES]
                zu, zu_grad = _gelu_and_grad(u_ref[rows, cols])
                du_ref[rows, cols] = (dyb * s * zu_grad).astype(BF16)
                ds = dyb * zu
                dbs_acc[g] += ds
                dsb = ds.astype(BF16)
                dws_ref[g] += _nt(dsb, vn)
                dvn_sc[rows, cols] = _nn(wst_ref[g], dsb)
        dvn = dvn_sc[...]
        dlng_ref[...] += jnp.sum(dvn * vhat, axis=0, keepdims=True)
        dlnb_ref[...] += jnp.sum(dvn, axis=0, keepdims=True)
        dvh = dvn * lng_ref[...]
        dzv = rstd * (dvh - jnp.mean(dvh, axis=-1, keepdims=True) - vhat * jnp.mean(dvh * vhat, axis=-1, keepdims=True))
        dvv_ref[...] = (dzv * zv_grad).astype(BF16)

        @pl.when(step == nsteps - 1)
        def _():
            dbs_ref[...] = jnp.sum(dbs_acc[...], axis=-1, keepdims=True)

    row = lambda w: pl.BlockSpec((tm, w), lambda i: (i, 0))
    pcol = lambda col: pl.BlockSpec((tm, GLA_W), lambda i: (i, col // GLA_W))
    const = lambda shape: pl.BlockSpec(shape, lambda i: (0,) * len(shape))
    return pl.pallas_call(
        body,
        name="mixer_bwd",
        grid=(nsteps,),
        in_specs=[
            row(D_MODEL), row(D_MODEL), row(GLA_W), row(GLA_W), pcol(COL_G), pcol(COL_U), pcol(COL_VV),
            _resident((1, GLA_W)), _resident((1, GMLP_W)), _resident((1, GMLP_W)),
            _resident((GMLP_GROUPS, GMLP_CHUNK, GMLP_CHUNK)), _resident((GMLP_GROUPS, GMLP_CHUNK, GMLP_CHUNK)),
            _resident((GMLP_GROUPS, GMLP_CHUNK, 1)), _resident((D_MODEL, D_MODEL)), _resident(TOKEN_SHAPE),
        ],
        out_specs=[
            row(GLA_W), row(GLA_W), row(GMLP_W), row(GMLP_W), const((D_MODEL, D_MODEL)),
            const((1, GLA_W)), const((1, GMLP_W)), const((1, GMLP_W)),
            const((GMLP_GROUPS, GMLP_CHUNK, GMLP_CHUNK)), const((GMLP_GROUPS, GMLP_CHUNK, 1)),
        ],
        out_shape=[
            jax.ShapeDtypeStruct((seq, GLA_W), F32), jax.ShapeDtypeStruct((seq, GLA_W), BF16),
            jax.ShapeDtypeStruct((seq, GMLP_W), BF16), jax.ShapeDtypeStruct((seq, GMLP_W), BF16),
            jax.ShapeDtypeStruct((D_MODEL, D_MODEL), F32),
            jax.ShapeDtypeStruct((1, GLA_W), F32), jax.ShapeDtypeStruct((1, GMLP_W), F32), jax.ShapeDtypeStruct((1, GMLP_W), F32),
            jax.ShapeDtypeStruct((GMLP_GROUPS, GMLP_CHUNK, GMLP_CHUNK), F32), jax.ShapeDtypeStruct((GMLP_GROUPS, GMLP_CHUNK, 1), F32),
        ],
        scratch_shapes=[pltpu.VMEM((tm, GMLP_W), BF16), pltpu.VMEM((tm, GMLP_W), F32), pltpu.VMEM((GMLP_GROUPS, GMLP_CHUNK, GMLP_CHUNK), F32)],
        compiler_params=_params(56),
    )(dx1, ycat, o_f, o_b, p, p, p, gn, lng, lnb, ws_bf, wst_bf, bs_col, w_out, token)


def _gla_bwd(p, do, st, wd_pad, bd, token, reverse, other=None):
    seq = p.shape[0]
    tg = _gla_tile(seq)
    nt = seq // tg
    n = tg // GLA_CHUNK
    scale = GLA_DK**-0.5

    def tile(i):
        return i if reverse else nt - 1 - i

    def body(q_ref, k_ref, v_ref, lr_ref, do_ref, st_ref, wd_ref, bd_ref, token_ref, *rest):
        others, (dq_ref, dk_ref, dv_ref, dlr_ref, dwd_ref, dbd_ref, carry) = rest[:-7], rest[-7:]
        if others:
            odq_ref, odk_ref, odv_ref, odlr_ref = others

            def put(ref, idx, val, oref):
                ref[idx] = (val + oref[idx]).astype(BF16)
        else:
            odq_ref = odk_ref = odv_ref = odlr_ref = None

            def put(ref, idx, val, oref):
                ref[idx] = val

        @pl.when(pl.program_id(0) == 0)
        def _():
            carry[...] = jnp.zeros_like(carry)
            dwd_ref[...] = jnp.zeros_like(dwd_ref)
            dbd_ref[...] = jnp.zeros_like(dbd_ref)

        lr_bf = lr_ref[...].astype(BF16)
        carries = [carry[h] for h in range(GLA_HEADS)]
        row_in_chunk = lax.broadcasted_iota(jnp.int32, (tg, LANES), 0) % GLA_CHUNK
        lane_head = lax.broadcasted_iota(jnp.int32, (1, LANES), 1) // GLA_DK
        tt = lax.broadcasted_iota(jnp.int32, (GLA_CHUNK, GLA_CHUNK), 0)
        ss = lax.broadcasted_iota(jnp.int32, (GLA_CHUNK, GLA_CHUNK), 1)
        causal = (tt <= ss) if reverse else (tt >= ss)
        order = range(n) if reverse else range(n - 1, -1, -1)
        dlr = jnp.zeros((tg, LANES), F32)
        heads = range(GLA_HEADS)
        pv, masks, qdh, vhs, dohs, stbs = {}, {}, {}, {}, {}, {}
        sc_raw, dp, acc = {}, {}, {}
        for pair in range(2):
            cols = pl.ds(pair * LANES, LANES)
            pre, b3, blast = _gla_decay_terms(lr_bf, wd_ref, bd_ref, pair, row_in_chunk, reverse, n)
            q3 = q_ref[:, cols].reshape(n, GLA_CHUNK, LANES) * scale
            k3 = k_ref[:, cols].reshape(n, GLA_CHUNK, LANES)
            eb = jnp.exp(b3)
            emb = jnp.exp(-b3)
            ekte = jnp.exp(blast - b3)
            kdf = k3 * emb
            kte = k3 * ekte
            both = pl.ds(2 * pair * GLA_DV, 2 * GLA_DV)
            pv[pair] = dict(pre=pre, eb=eb, emb=emb, ekte=ekte, qd=q3 * eb, kdf=kdf, kd=kdf.astype(BF16), kte=kte, kte_bf=kte.astype(BF16),
                            dec=jnp.exp(blast), v=v_ref[:, both].reshape(n, GLA_CHUNK, 2 * GLA_DV).astype(BF16),
                            do=do_ref[:, both].reshape(n, GLA_CHUNK, 2 * GLA_DV).astype(BF16))
            for hh in range(2):
                h = 2 * pair + hh
                masks[h] = (lane_head == hh).astype(F32)
                qdh[h] = (pv[pair]["qd"] * masks[h]).astype(BF16)
                vhs[h] = pv[pair]["v"][:, :, hh * GLA_DV : (hh + 1) * GLA_DV]
                dohs[h] = pv[pair]["do"][:, :, hh * GLA_DV : (hh + 1) * GLA_DV]
                stbs[h] = st_ref[:, h]
                dp[h] = _bnt(dohs[h], vhs[h])
                acc[h] = _btn(dohs[h], qdh[h])
            sc_both = _bnt(jnp.concatenate([qdh[2 * pair], qdh[2 * pair + 1]], axis=1), pv[pair]["kd"])
            for hh in range(2):
                sc_raw[2 * pair + hh] = sc_both[:, hh * GLA_CHUNK : (hh + 1) * GLA_CHUNK, :]
        dsa, sc = {}, {}
        for h in heads:
            sc[h] = jnp.where(causal, sc_raw[h], 0.0).astype(BF16)
            dp[h] = jnp.where(causal, dp[h], 0.0).astype(BF16)
            dec = pv[h // 2]["dec"]
            c, after = carries[h], [None] * n
            for j in order:
                after[j] = c
                c = acc[h][j] + dec[j] * c
            carries[h] = c
            dsa[h] = jnp.stack(after)
        dvs, dqs, dks, dwds, dbds = [], [], [], [], []
        for pair in range(2):
            cols = pl.ds(pair * LANES, LANES)
            v = pv[pair]
            h0, h1 = 2 * pair, 2 * pair + 1
            dsa_both = jnp.concatenate([dsa[h0], dsa[h1]], axis=1)
            dsa_bf = dsa_both.astype(BF16)
            stb_bf = jnp.concatenate([stbs[h0], stbs[h1]], axis=1)
            dq_intra = _bnn(jnp.concatenate([dp[h0], dp[h1]], axis=1), v["kd"])
            dqd = (dq_intra[:, :GLA_CHUNK, :] * masks[h0] + dq_intra[:, GLA_CHUNK:, :] * masks[h1]) + _bnn(v["do"], stb_bf)
            dkd = _btn(dp[h0], qdh[h0]) + _btn(dp[h1], qdh[h1])
            dkte = _bnn(v["v"], dsa_bf)
            ddec = jnp.sum(dsa[h0] * stbs[h0].astype(F32) + dsa[h1] * stbs[h1].astype(F32), axis=1, keepdims=True)
            dv_inter = _bnt(v["kte_bf"], dsa_bf)
            for hh, h in ((0, h0), (1, h1)):
                dvs.append((_btn(sc[h], dohs[h]) + dv_inter[:, :, hh * GLA_DV : (hh + 1) * GLA_DV]).reshape(tg, GLA_DV))
            dqs.append((dqd * (scale * v["eb"])).reshape(tg, LANES))
            dks.append((dkd * v["emb"] + dkte * v["ekte"]).reshape(tg, LANES))
            db = dqd * v["qd"] - dkd * v["kdf"] - dkte * v["kte"]
            dblast = jnp.sum(dkte * v["kte"], axis=1, keepdims=True) + ddec * v["dec"]
            dla = _chunk_cumsum(db.reshape(tg, LANES), row_in_chunk, not reverse) + jnp.broadcast_to(dblast, (n, GLA_CHUNK, LANES)).reshape(tg, LANES)
            dpre = (dla * (1.0 / GLA_TAU) * _sigmoid(-v["pre"]))
            dpre_bf = dpre.astype(BF16)
            dlr = dlr + _nt(dpre_bf, wd_ref[:, cols])
            dwds.append(_tn(lr_bf, dpre_bf))
            dbds.append(jnp.sum(dpre, axis=0, keepdims=True))
        put(dlr_ref, (slice(None), slice(None)), dlr, odlr_ref)
        for pair in range(2):
            cols = pl.ds(pair * LANES, LANES)
            put(dq_ref, (slice(None), cols), dqs[pair], odq_ref)
            put(dk_ref, (slice(None), cols), dks[pair], odk_ref)
            dwd_ref[:, cols] += dwds[pair]
            dbd_ref[:, cols] += dbds[pair]
        for h in range(GLA_HEADS):
            put(dv_ref, (slice(None), pl.ds(h * GLA_DV, GLA_DV)), dvs[h], odv_ref)
            carry[h] = carries[h]

    pieces = [
        pl.BlockSpec((tg, KEY_W), lambda i: (tile(i), 0)),
        pl.BlockSpec((tg, KEY_W), lambda i: (tile(i), 0)),
        pl.BlockSpec((tg, GLA_W), lambda i: (tile(i), 0)),
        pl.BlockSpec((tg, LANES), lambda i: (tile(i), 0)),
    ]
    piece_dtype = BF16 if other else F32
    return pl.pallas_call(
        body,
        name="gla_bwd_rev" if reverse else "gla_bwd",
        grid=(nt,),
        in_specs=[
            pl.BlockSpec((tg, KEY_W), lambda i: (tile(i), COL_Q // KEY_W)),
            pl.BlockSpec((tg, KEY_W), lambda i: (tile(i), COL_K // KEY_W)),
            pl.BlockSpec((tg, GLA_W), lambda i: (tile(i), COL_V // GLA_W)),
            pl.BlockSpec((tg, LANES), lambda i: (tile(i), COL_LR // LANES)),
            pl.BlockSpec((tg, GLA_W), lambda i: (tile(i), 0)),
            pl.BlockSpec((n, GLA_HEADS, GLA_DV, LANES), lambda i: (tile(i), 0, 0, 0)),
            _resident((LANES, KEY_W)),
            _resident((1, KEY_W)),
            _resident(TOKEN_SHAPE),
        ] + (pieces if other else []),
        out_specs=pieces + [pl.BlockSpec((LANES, KEY_W), lambda i: (0, 0)), pl.BlockSpec((1, KEY_W), lambda i: (0, 0))],
        out_shape=[
            jax.ShapeDtypeStruct((seq, KEY_W), piece_dtype), jax.ShapeDtypeStruct((seq, KEY_W), piece_dtype),
            jax.ShapeDtypeStruct((seq, GLA_W), piece_dtype), jax.ShapeDtypeStruct((seq, LANES), piece_dtype),
            jax.ShapeDtypeStruct((LANES, KEY_W), F32), jax.ShapeDtypeStruct((1, KEY_W), F32),
        ],
        scratch_shapes=[pltpu.VMEM((GLA_HEADS, GLA_DV, LANES), F32)],
        compiler_params=_params(56),
    )(p, p, p, p, do, st, wd_pad, bd, token, *(other or ()))


def _inproj_wgrad(x, g1, dq, dk, dv, dg, du, dvv, dlr):
    seq = x.shape[0]
    tm = min(seq, 512)

    def body(x_ref, g1_ref, dq_ref, dk_ref, dv_ref, dg_ref, du_ref, dvv_ref, dlr_ref, dw_ref, dp_ref):
        @pl.when(pl.program_id(0) == 0)
        def _():
            dw_ref[...] = jnp.zeros_like(dw_ref)

        for col, ref in ((COL_Q, dq_ref), (COL_K, dk_ref), (COL_V, dv_ref), (COL_G, dg_ref), (COL_U, du_ref), (COL_VV, dvv_ref), (COL_LR, dlr_ref)):
            dp_ref[:, col : col + ref.shape[1]] = ref[...]
        xv = x_ref[...]
        h = (xv * lax.rsqrt(jnp.mean(xv * xv, axis=-1, keepdims=True) + EPS) * g1_ref[...]).astype(BF16)
        dw_ref[0:ROW_LR, :] += _tn(dp_ref[:, 0:COL_U], h)
        dw_ref[ROW_UV:PROJ_W, :] += _tn(dp_ref[:, COL_U:COL_LR], h)
        dw_ref[ROW_LR:ROW_UV, :] += _tn(dp_ref[:, COL_LR:PROJ_WP], h)[0 : ROW_UV - ROW_LR]

    row = lambda w: pl.BlockSpec((tm, w), lambda i: (i, 0))
    return pl.pallas_call(
        body,
        name="inproj_wgrad",
        grid=(seq // tm,),
        in_specs=[row(D_MODEL), _resident((1, D_MODEL)), row(KEY_W), row(KEY_W), row(GLA_W), row(GLA_W), row(GMLP_W), row(GMLP_W), row(LANES)],
        out_specs=[pl.BlockSpec((PROJ_W, D_MODEL), lambda i: (0, 0)), row(PROJ_WP)],
        out_shape=[jax.ShapeDtypeStruct((PROJ_W, D_MODEL), F32), jax.ShapeDtypeStruct((seq, PROJ_WP), BF16)],
        compiler_params=_params(56),
    )(x, g1, dq, dk, dv, dg, du, dvv, dlr)


def _inproj_dx(x, dx1, g1, w_in_t, dp, token):
    seq = x.shape[0]
    tm = min(seq, 512)

    def body(x_ref, dx1_ref, g1_ref, w_ref, dp_ref, token_ref, dx_ref, dg1_ref):
        @pl.when(pl.program_id(0) == 0)
        def _():
            dg1_ref[...] = jnp.zeros_like(dg1_ref)

        xv = x_ref[...]
        r1 = lax.rsqrt(jnp.mean(xv * xv, axis=-1, keepdims=True) + EPS)
        xh = xv * r1
        dh = (_nn(dp_ref[:, 0:COL_U], w_ref[0:ROW_LR, :]) + _nn(dp_ref[:, COL_U:COL_LR], w_ref[ROW_UV:PROJ_W, :])
              + _nn(dp_ref[:, COL_LR:PROJ_WP], w_ref[ROW_LR : ROW_LR + LANES, :]))
        dg1_ref[...] += jnp.sum(dh * xh, axis=0, keepdims=True)
        dx_ref[...] = dx1_ref[...] + _rms_bwd(dh * g1_ref[...], xh, r1)

    row = lambda w: pl.BlockSpec((tm, w), lambda i: (i, 0))
    return pl.pallas_call(
        body,
        name="inproj_dx",
        grid=(seq // tm,),
        in_specs=[row(D_MODEL), row(D_MODEL), _resident((1, D_MODEL)), _resident((PROJ_W, D_MODEL)), row(PROJ_WP), _resident(TOKEN_SHAPE)],
        out_specs=[row(D_MODEL), pl.BlockSpec((1, D_MODEL), lambda i: (0, 0))],
        out_shape=[jax.ShapeDtypeStruct((seq, D_MODEL), F32), jax.ShapeDtypeStruct((1, D_MODEL), F32)],
        compiler_params=_params(48),
    )(x, dx1, g1, w_in_t, dp, token)


def _in_hbm(a):
    return pltpu.with_memory_space_constraint(a, pltpu.HBM)


def _row_tile(rows, multiple=8):
    for t in range(min(rows, 512), 0, -1):
        if rows % t == 0 and t % multiple == 0:
            return t
    return rows


def _cast_into_slot(w, shard, token):
    rows, cols = w.shape
    tr = _row_tile(rows, 16)

    def body(s_ref, w_ref, token_ref, o_ref):
        o_ref[...] = w_ref[...].astype(BF16)

    return pl.pallas_call(
        body,
        name="cast_into_slot",
        grid_spec=pltpu.PrefetchScalarGridSpec(
            num_scalar_prefetch=1,
            grid=(rows // tr,),
            in_specs=[pl.BlockSpec((tr, cols), lambda i, s_ref: (i, 0)), pl.BlockSpec(TOKEN_SHAPE, lambda i, s_ref: (0, 0))],
            out_specs=pl.BlockSpec((None, tr, cols), lambda i, s_ref: (s_ref[0], i, 0)),
        ),
        out_shape=pltpu.HBM((N_SHARDS, rows, cols), BF16),
        compiler_params=_params(32, ("parallel",)),
    )(shard, _in_hbm(w), token)


def _add_halves(grads4, recvs, shard_core):
    n = len(grads4)
    _, rows, _ = grads4[0].shape
    tr = _row_tile(rows, 16)

    def body(sc_ref, *refs):
        for k in range(n):
            total = refs[k][...] + refs[n + k][...]
            refs[3 * n + k][...] = total.astype(BF16)

            @pl.when(pl.program_id(1) == sc_ref[0])
            def _(k=k, total=total):
                refs[2 * n + k][...] = total

    theirs = pl.BlockSpec((None, tr, HALF), lambda i, s, sc_ref: (s, i, 0))
    mine = pl.BlockSpec((None, tr, HALF), lambda i, s, sc_ref: (s, i, sc_ref[1]))
    kept = pl.BlockSpec((tr, HALF), lambda i, s, sc_ref: (i, 0))
    outs = pl.pallas_call(
        body,
        name="add_halves",
        grid_spec=pltpu.PrefetchScalarGridSpec(
            num_scalar_prefetch=1,
            grid=(rows // tr, N_SHARDS),
            in_specs=[mine] * n + [theirs] * n,
            out_specs=[kept] * n + [theirs] * n,
        ),
        out_shape=[pltpu.HBM((rows, HALF), F32)] * n + [pltpu.HBM((N_SHARDS, rows, HALF), BF16)] * n,
        compiler_params=_params(48, ("parallel", "arbitrary")),
    )(shard_core, *[_in_hbm(a) for a in list(grads4) + list(recvs)])
    return list(zip(outs[:n], outs[n:]))


def _add_partials(part, recv3, shard_core, token):
    rows, _ = part.shape
    tr = _row_tile(rows, 16)

    def body(sc_ref, p_ref, r_ref, token_ref, o_ref):
        o_ref[...] = ((p_ref[...] + r_ref[0].astype(F32)) + r_ref[1].astype(F32)) + r_ref[2].astype(F32)

    return pl.pallas_call(
        body,
        name="add_partials",
        grid_spec=pltpu.PrefetchScalarGridSpec(
            num_scalar_prefetch=1,
            grid=(rows // tr,),
            in_specs=[
                pl.BlockSpec((tr, HALF), lambda i, sc_ref: (i, 0)),
                pl.BlockSpec((3, tr, HALF), lambda i, sc_ref: (0, i, 0)),
                pl.BlockSpec(TOKEN_SHAPE, lambda i, sc_ref: (0, 0)),
            ],
            out_specs=pl.BlockSpec((tr, HALF), lambda i, sc_ref: (i, sc_ref[1])),
        ),
        out_shape=pltpu.HBM((rows, 2 * HALF), F32),
        compiler_params=_params(32, ("parallel",)),
    )(shard_core, _in_hbm(part), _in_hbm(recv3), token)


def _adam_math(w, g, m, v):
    m = ADAM_B1 * m + (1.0 - ADAM_B1) * g
    v = ADAM_B2 * v + (1.0 - ADAM_B2) * (g * g)
    m_hat = m / (1.0 - ADAM_B1**ADAM_STEP)
    v_hat = v / (1.0 - ADAM_B2**ADAM_STEP)
    delta = -ADAM_LR * (m_hat / (jnp.sqrt(v_hat) + ADAM_EPS) + ADAM_WD * w)
    return delta, m, v


def _adamw(w, g, m, v):
    rows, cols = w.shape
    tr = _row_tile(rows)

    def body(w_ref, g_ref, m_ref, v_ref, go_ref, d_ref, mo_ref, vo_ref):
        gv = g_ref[...]
        go_ref[...] = gv
        d_ref[...], mo_ref[...], vo_ref[...] = _adam_math(w_ref[...], gv, m_ref[...], v_ref[...])

    spec = pl.BlockSpec((tr, cols), lambda i: (i, 0))
    return pl.pallas_call(
        body, name="adamw", grid=(rows // tr,), in_specs=[spec] * 4, out_specs=[spec] * 4, out_shape=[pltpu.HBM(w.shape, F32)] * 4,
        compiler_params=_params(32, ("parallel",)),
    )(_in_hbm(w), _in_hbm(g), _in_hbm(m), _in_hbm(v))


SMALL_ROWS = 560
DECAY_ROWS = 8
SMALL_TOTAL = SMALL_ROWS + 2 * N_SHARDS * DECAY_ROWS


def _adamw_small(gathered, own, wp, mp, vp, like):
    out_rows = SMALL_ROWS + 2 * DECAY_ROWS
    places, off = [], 0
    for a in like:
        rows = a.size // LANES
        kept = a.shape[-1] == LANES
        places.append((off, rows, kept, (rows, LANES) if kept else (1, a.size)))
        off += rows
    loss_row = off
    decay_shape = (LOWRANK, KEY_W // N_SHARDS)
    n = len(places) + 2

    def body(ga_ref, own_ref, w_ref, m_ref, v_ref, *refs):
        outs, loss_ref, packed = refs[: 4 * n], refs[4 * n], refs[4 * n + 1 :]
        g_sc = packed[0]
        x, y, c = _position()
        shard, me = 2 * x + y, 4 * x + 2 * y + c
        total = lambda rows: functools.reduce(lambda a, b: a + b, [jnp.where(me == d, own_ref[rows, :], ga_ref[d, rows, :]) for d in range(8)])
        g_sc[pl.ds(0, SMALL_ROWS), :] = total(pl.ds(0, SMALL_ROWS))
        for k in range(2):
            start = pl.multiple_of(SMALL_ROWS + k * N_SHARDS * DECAY_ROWS + shard * DECAY_ROWS, DECAY_ROWS)
            g_sc[pl.ds(SMALL_ROWS + k * DECAY_ROWS, DECAY_ROWS), :] = total(pl.ds(start, DECAY_ROWS))
        packed[1][...], packed[2][...], packed[3][...] = _adam_math(w_ref[...], g_sc[...], m_ref[...], v_ref[...])
        loss_ref[...] = g_sc[loss_row : loss_row + 1, :]
        for t, res in enumerate(packed):
            for (at, rows, kept, _), out in zip(places, outs[t * n :]):
                if kept:
                    out[...] = res[at : at + rows, :]
                else:
                    for r in range(rows):
                        out[:, r * LANES : (r + 1) * LANES] = res[at + r : at + r + 1, :]
            for k in range(2):
                out = outs[t * n + len(places) + k]
                both = res[SMALL_ROWS + k * DECAY_ROWS : SMALL_ROWS + (k + 1) * DECAY_ROWS, :]
                halves = (both, pltpu.roll(both, LANES // 2, axis=1))
                for r in range(DECAY_ROWS):
                    for h in range(2):
                        out[2 * r + h : 2 * r + h + 1, :] = halves[h][r : r + 1, 0 : LANES // 2]

    shapes = [jax.ShapeDtypeStruct(s, F32) for *_, s in places] + [jax.ShapeDtypeStruct(decay_shape, F32)] * 2
    out = pl.pallas_call(
        body,
        name="adamw_small",
        out_shape=shapes * 4 + [jax.ShapeDtypeStruct((1, LANES), F32)],
        scratch_shapes=[pltpu.VMEM((out_rows, LANES), F32)] * 4,
        compiler_params=_params(32, None),
    )(gathered, own, wp, mp, vp)
    return [list(out[t * n : (t + 1) * n]) for t in range(4)], out[4 * n]


ANY = pl.BlockSpec(memory_space=pl.ANY)


def _position():
    return lax.axis_index("x"), lax.axis_index("y"), lax.axis_index("c")


def _other_chips(x, y):
    return [(1 - x, y), (x, 1 - y), (1 - x, 1 - y)]


HBM = pl.BlockSpec(memory_space=pltpu.HBM)
SEM = pl.BlockSpec(memory_space=pltpu.SEMAPHORE)
TOKEN = jax.ShapeDtypeStruct(TOKEN_SHAPE, F32)
DATAFLOW = pltpu.SideEffectType.DATAFLOW_SIDE_EFFECTING


def _half_block(ref4, slot, core):
    return ref4.at[slot, :, pl.ds(pl.multiple_of(core * HALF, HALF), HALF)]


def _gather_ici_copies(bufs, lands, send_sems, recv_sems):
    x, y, c = _position()
    pairs = []
    for k, ref4 in enumerate(bufs):
        mine = _half_block(ref4, 2 * x + y, c)
        for j, (px, py) in enumerate(_other_chips(x, y)):
            sems = dict(send_sem=send_sems.at[3 * k + j], recv_sem=recv_sems.at[3 * k + j], device_id=(px, py, c), device_id_type=MESH)
            pairs.append((functools.partial(pltpu.make_async_remote_copy, src_ref=mine, dst_ref=mine, **sems),
                          functools.partial(pltpu.make_async_remote_copy, src_ref=mine, dst_ref=_half_block(ref4, 2 * px + py, c), **sems)))
    return pairs


def _gather_d2d_copies(bufs, lands, send_sems, recv_sems, first=0):
    x, y, c = _position()
    pairs = []
    for k, ref4 in enumerate(bufs):
        for j, (px, py) in enumerate(_other_chips(x, y)):
            have = _half_block(ref4, 2 * px + py, c)
            at = first + 3 * k + j
            sems = dict(send_sem=send_sems.at[at], recv_sem=recv_sems.at[at], device_id=(x, y, 1 - c), device_id_type=MESH)
            pairs.append((functools.partial(pltpu.make_async_remote_copy, src_ref=have, dst_ref=have, **sems),
                          functools.partial(pltpu.make_async_remote_copy, src_ref=have, dst_ref=_half_block(ref4, 2 * px + py, 1 - c), **sems)))
    return pairs


def _gather_forward(bufs):
    n = len(bufs)

    def body(*refs):
        outs = refs[n : 2 * n]
        send_sems, recv_sems = refs[2 * n :]
        d2d = _gather_d2d_copies(outs, (), send_sems, recv_sems)
        for forward, _ in d2d:
            forward().start()
        for forward, arrival in d2d:
            arrival().wait_recv()
            forward().wait_send()

    return pl.pallas_call(
        body,
        name="gather_forward",
        in_specs=[ANY] * n,
        out_specs=[ANY] * n,
        out_shape=[jax.ShapeDtypeStruct(b.shape, b.dtype) for b in bufs],
        input_output_aliases={k: k for k in range(n)},
        scratch_shapes=[pltpu.SemaphoreType.DMA((3 * n,)), pltpu.SemaphoreType.DMA((3 * n,))],
        compiler_params=pltpu.CompilerParams(has_side_effects=True),
    )(*bufs)


def _both_ends(**copy):
    maker = functools.partial(pltpu.make_async_remote_copy, **copy)
    return maker, maker


def _scatter_copies(parts, lands, send_sems, recv_sems):
    x, y, c = _position()
    return [_both_ends(src_ref=parts[k].at[2 * px + py], dst_ref=lands[k].at[j], send_sem=send_sems.at[3 * k + j],
                       recv_sem=recv_sems.at[3 * k + j], device_id=(px, py, c), device_id_type=MESH)
            for k in range(len(parts)) for j, (px, py) in enumerate(_other_chips(x, y))]


def _exchange_copies(grads, lands, send_sems, recv_sems):
    x, y, c = _position()
    return [_both_ends(src_ref=grads[k].at[:, :, pl.ds(pl.multiple_of((1 - c) * HALF, HALF), HALF)], dst_ref=lands[k],
                       send_sem=send_sems.at[k], recv_sem=recv_sems.at[k], device_id=(x, y, 1 - c), device_id_type=MESH)
            for k in range(len(grads))]


def _exchange_lands(grads4):
    return [jax.ShapeDtypeStruct((N_SHARDS, g.shape[1], HALF), g.dtype) for g in grads4]


def _scatter_lands(parts4):
    return [jax.ShapeDtypeStruct((3,) + g.shape[1:], g.dtype) for g in parts4]


def _small_gather_copies(blocks, lands, send_sems, recv_sems):
    x, y, c = _position()
    flip = lambda v, bit: 1 - v if bit else v
    return [_both_ends(src_ref=blocks[0], dst_ref=lands[0].at[4 * x + 2 * y + c], send_sem=send_sems.at[r - 1],
                       recv_sem=recv_sems.at[r - 1], device_id=(flip(x, r & 4), flip(y, r & 2), flip(c, r & 1)), device_id_type=MESH)
            for r in range(1, 8)]


def _split_start(name, srcs, land_shapes, make_copies, nsem, after=()):
    n, nl, na = len(srcs), len(land_shapes), len(after)
    lands = [lax.empty(a.shape, a.dtype) for a in land_shapes]

    def body(*refs):
        send_sems, recv_sems = refs[n + nl + na], refs[n + nl + na + 1]
        token = refs[2 * (n + nl) + na + 2]
        for send, _ in make_copies(refs[:n], refs[n : n + nl], send_sems, recv_sems):
            send().start()
        token[...] = jnp.zeros_like(token)

    hbm = lambda a: pltpu.HBM(a.shape, a.dtype)
    out = pl.pallas_call(
        body,
        name=name,
        in_specs=[HBM] * (n + nl) + [ANY] * na,
        out_specs=(SEM, SEM, *[HBM] * (n + nl), pl.BlockSpec(memory_space=pltpu.VMEM)),
        out_shape=(pltpu.SemaphoreType.DMA((nsem,)), pltpu.SemaphoreType.DMA((nsem,)), *[hbm(a) for a in list(srcs) + lands], TOKEN),
        input_output_aliases={k: 2 + k for k in range(n + nl)},
        compiler_params=pltpu.CompilerParams(has_side_effects=DATAFLOW),
    )(*[pltpu.with_memory_space_constraint(a, pltpu.HBM) for a in list(srcs) + lands], *after)
    return out[0], out[1], list(out[2 : 2 + n]), list(out[2 + n : 2 + n + nl]), out[2 + n + nl]


def _split_wait(name, send_sems, recv_sems, srcs, lands, make_copies, after):
    n, nl = len(srcs), len(lands)

    def body(*refs):
        for send, arrival in make_copies(refs[:n], refs[n : n + nl], refs[n + nl], refs[n + nl + 1]):
            send().wait_send()
            arrival().wait_recv()

    hbm = lambda a: pltpu.HBM(a.shape, a.dtype)
    out = pl.pallas_call(
        body,
        name=name,
        in_specs=[HBM] * (n + nl) + [SEM, SEM] + [ANY] * len(after),
        out_specs=tuple([HBM] * (n + nl)),
        out_shape=tuple(hbm(a) for a in list(srcs) + list(lands)),
        input_output_aliases={k: k for k in range(n + nl)},
        compiler_params=pltpu.CompilerParams(has_side_effects=DATAFLOW),
    )(*srcs, *lands, send_sems, recv_sems, *after)
    return list(out[:n]), list(out[n:])


def _join_copies(bufs, lands, send_sems, recv_sems, first=0):
    x, y, c = _position()
    half = lambda ref, core: ref.at[:, pl.ds(pl.multiple_of(core * HALF, HALF), HALF)]
    pairs = []
    for k, ref in enumerate(bufs):
        sems = dict(send_sem=send_sems.at[first + k], recv_sem=recv_sems.at[first + k], device_id=(x, y, 1 - c), device_id_type=MESH)
        pairs.append((functools.partial(pltpu.make_async_remote_copy, src_ref=half(ref, c), dst_ref=half(ref, c), **sems),
                      functools.partial(pltpu.make_async_remote_copy, src_ref=half(ref, c), dst_ref=half(ref, 1 - c), **sems)))
    return pairs


def _allgather_small(block):
    m_per, ncol = block.shape

    def body(x_ref, out_ref, send_sems, recv_sems, local_sem):
        x, y, c = _position()
        me, sibling = (x, y, c), (x, y, 1 - c)
        chips = _other_chips(x, y)

        def rows(px, py, pc):
            return out_ref.at[4 * px + 2 * py + pc]

        def copy(k, blk, to, src=None):
            return pltpu.make_async_remote_copy(
                src_ref=rows(*blk) if src is None else src, dst_ref=rows(*blk),
                send_sem=send_sems.at[k], recv_sem=recv_sems.at[k], device_id=to, device_id_type=MESH)

        mine = pltpu.make_async_copy(x_ref, rows(*me), local_sem)
        mine.start()
        first = [copy(0, me, sibling, src=x_ref)] + [copy(1 + j, me, (*chip, c), src=x_ref) for j, chip in enumerate(chips)]
        for cp in first:
            cp.start()
        passed = [copy(4 + j, (*chip, c), sibling) for j, chip in enumerate(chips)]
        for j, chip in enumerate(chips):
            copy(1 + j, (*chip, c), me).wait_recv()
            passed[j].start()
        copy(0, sibling, me).wait_recv()
        for j, chip in enumerate(chips):
            copy(4 + j, (*chip, 1 - c), me).wait_recv()
        for cp in first + passed:
            cp.wait_send()
        mine.wait()

    return pl.pallas_call(
        body,
        name="allgather_small",
        in_specs=[pl.BlockSpec(memory_space=pltpu.VMEM)],
        out_specs=pl.BlockSpec(memory_space=pltpu.VMEM),
        out_shape=jax.ShapeDtypeStruct((8, m_per, ncol), block.dtype),
        scratch_shapes=[pltpu.SemaphoreType.DMA((7,)), pltpu.SemaphoreType.DMA((7,)), pltpu.SemaphoreType.DMA],
        compiler_params=pltpu.CompilerParams(has_side_effects=True, vmem_limit_bytes=32 * MIB),
    )(block)


SMALL_NAMES = ["norm1_g", "b_decay_f", "b_decay_b", "gla_norm_g", "gmlp_ln_g", "gmlp_ln_b", "w_spatial", "b_spatial", "norm2_g", "final_norm_g"]


def _pack_small(parts, decay_parts):
    flat = jnp.concatenate([a.reshape(-1) for a in parts])
    flat = jnp.pad(flat, (0, SMALL_ROWS * LANES - flat.shape[0])).reshape(SMALL_ROWS, LANES)
    return jnp.concatenate([flat] + [d.reshape(-1, LANES) for d in decay_parts], axis=0)


def kernel(x, norm1_g, w_in, w_decay_f, b_decay_f, w_decay_b, b_decay_b, gla_norm_g, gmlp_ln_g, gmlp_ln_b, w_spatial, b_spatial, w_out, norm2_g, w_gate, w_up, w_down, final_norm_g, loss_target, m_norm1_g, m_w_in, m_w_decay_f, m_b_decay_f, m_w_decay_b, m_b_decay_b, m_gla_norm_g, m_gmlp_ln_g, m_gmlp_ln_b, m_w_spatial, m_b_spatial, m_w_out, m_norm2_g, m_w_gate, m_w_up, m_w_down, m_final_norm_g, v_norm1_g, v_w_in, v_w_decay_f, v_b_decay_f, v_w_decay_b, v_b_decay_b, v_gla_norm_g, v_gmlp_ln_g, v_gmlp_ln_b, v_w_spatial, v_b_spatial, v_w_out, v_norm2_g, v_w_gate, v_w_up, v_w_down, v_final_norm_g):
    args = dict(locals())
    cx, cy, cc = lax.axis_index("x"), lax.axis_index("y"), lax.axis_index("c")
    shard = 2 * cx + cy
    xs = x[0]
    target = loss_target[0]

    big_names = ["w_in", "w_out", "w_gate", "w_up", "w_down"]
    transposed = ("w_in", "w_gate", "w_up")
    rows_of = lambda pre, k: jnp.transpose(args[pre + k][0]) if k in transposed else args[pre + k][0]
    big_shards = {k: rows_of("", k) for k in big_names}
    s_arr = shard.reshape(1).astype(jnp.int32)
    sc_arr = jnp.stack([shard, cc]).astype(jnp.int32)
    zero_token = jnp.zeros(TOKEN_SHAPE, F32)
    w_send, w_recv, (w_in4,), _, token_w_in = _split_start(
        "w_in_gather_start", [_cast_into_slot(big_shards["w_in"], s_arr, zero_token)], [], _gather_ici_copies, 3)
    late = ["w_out", "w_gate", "w_up", "w_down"]
    late_slots = [_cast_into_slot(big_shards[k], s_arr, token_w_in) for k in late]
    dec_block = jnp.concatenate([w_decay_f[0].reshape(-1, LANES), w_decay_b[0].reshape(-1, LANES)], axis=0)
    dec_all = _allgather_small(dec_block)
    (w_in4,), _ = _split_wait("w_in_gather_wait", w_send, w_recv, [w_in4], [], _gather_ici_copies, (dec_all, *late_slots))
    (w_in4,) = _gather_forward([w_in4])
    w_in_t = w_in4.reshape(PROJ_W, D_MODEL)
    g_send, g_recv, late_bufs, _, token_gather = _split_start(
        "gather_start", late_slots, [], _gather_ici_copies, 3 * len(late), after=(w_in4,))
    dec_all = dec_all[::2].reshape(N_SHARDS, 2, LOWRANK, KEY_W // N_SHARDS)
    wdf_full = jnp.transpose(dec_all[:, 0], (1, 0, 2)).reshape(LOWRANK, KEY_W)
    wdb_full = jnp.transpose(dec_all[:, 1], (1, 0, 2)).reshape(LOWRANK, KEY_W)
    wd_pad_f = jnp.zeros((LANES, KEY_W), F32).at[0:LOWRANK].set(wdf_full).astype(BF16)
    wd_pad_b = jnp.zeros((LANES, KEY_W), F32).at[LOWRANK : 2 * LOWRANK].set(wdb_full).astype(BF16)

    ws_bf = w_spatial[0].astype(BF16)
    wst_bf = jnp.transpose(w_spatial[0], (0, 2, 1)).astype(BF16)
    bs_col = b_spatial[0].reshape(GMLP_GROUPS, GMLP_CHUNK, 1)

    p = _inproj(xs, norm1_g, w_in_t, token_gather)
    o_f, st_f = _gla_fwd(p, wd_pad_f, b_decay_f, token_gather, reverse=False)
    o_b, st_b = _gla_fwd(p, wd_pad_b, b_decay_b, token_gather, reverse=True)
    late_bufs, _ = _split_wait("gather_wait", g_send, g_recv, late_bufs, [], _gather_ici_copies, (o_f, o_b))
    f_send, f_recv, late_bufs, _, token_forward = _split_start("forward_start", late_bufs, [], _gather_d2d_copies, 3 * len(late))
    (w_out4,), _ = _split_wait("w_out_forward_wait", f_send, f_recv, late_bufs[:1], [], _gather_d2d_copies, (token_forward,))
    w_out_full = _in_hbm(w_out4.reshape(-1, D_MODEL))
    x1, ycat = _mixer_out(xs, o_f, o_b, p, gla_norm_g, gmlp_ln_g, gmlp_ln_b, ws_bf, bs_col, w_out_full, token_forward)
    ffn_bufs, _ = _split_wait(
        "forward_wait", f_send, f_recv, late_bufs[1:], [], functools.partial(_gather_d2d_copies, first=3), (x1,))
    wg_t, wu_t, wd = [b.reshape(-1, D_MODEL) for b in ffn_bufs]
    gf = final_norm_g.reshape(1, D_MODEL)
    h2, gate, up, act, dx2, loss_acc, dgf = _ffn_fwd(x1, target, norm2_g, gf, wg_t, wu_t, wd)

    dgate, dup, dx1, dg2 = _ffn_bwd(dx2, gate, up, x1, norm2_g, wg_t, wu_t, wd)
    ffn_grads4 = [g.reshape(N_SHARDS, FF_SHARD, D_MODEL) for g in _ffn_wgrad(h2, dgate, dup, act, dx2)]
    e_send, e_recv, e_srcs, e_lands, token_exchange = _split_start(
        "exchange_start", ffn_grads4, _exchange_lands(ffn_grads4), _exchange_copies, len(ffn_grads4))
    do, dg, du, dvv, dwo, dgn, dlng, dlnb, dws, dbs = _mixer_bwd(
        dx1, ycat, o_f, o_b, p, gla_norm_g, gmlp_ln_g, gmlp_ln_b, ws_bf, wst_bf, bs_col, w_out_full, token_exchange)
    ffn_mine, ffn_other = _split_wait("exchange_wait", e_send, e_recv, e_srcs, e_lands, _exchange_copies, (do,))
    ffn_parts = _add_halves(ffn_mine, ffn_other, sc_arr)
    ffn_payload = [pb for _, pb in ffn_parts]
    s_send, s_recv, s_parts, s_lands, token_scatter = _split_start(
        "scatter_start", ffn_payload, _scatter_lands(ffn_payload), _scatter_copies, 3 * len(ffn_payload))
    dq_f, dk_f, dv_f, dlr_f, dwdec_f, dbdec_f = _gla_bwd(p, do, st_f, wd_pad_f, b_decay_f, token_scatter, reverse=False)
    dq, dk, dv, dlr, dwdec_b, dbdec_b = _gla_bwd(
        p, do, st_b, wd_pad_b, b_decay_b, token_scatter, reverse=True, other=(dq_f, dk_f, dv_f, dlr_f))
    dwin_t, dp = _inproj_wgrad(xs, norm1_g, dq, dk, dv, dg, du, dvv, dlr)
    _, ffn_recv = _split_wait("scatter_wait", s_send, s_recv, s_parts, s_lands, _scatter_copies, (dwin_t,))

    dwin4 = dwin_t.reshape(N_SHARDS, PROJ_W // N_SHARDS, D_MODEL)
    dwo4 = dwo.reshape(N_SHARDS, D_MODEL // N_SHARDS, D_MODEL)
    proj_grads4 = [dwin4, dwo4]
    x_send, x_recv, x_srcs, x_lands, token_swap = _split_start(
        "proj_exchange_start", proj_grads4, _exchange_lands(proj_grads4), _exchange_copies, len(proj_grads4))
    ffn_bufs = [_add_partials(pf, r, sc_arr, token_swap) for (pf, _), r in zip(ffn_parts, ffn_recv)]
    proj_mine, proj_other = _split_wait("proj_exchange_wait", x_send, x_recv, x_srcs, x_lands, _exchange_copies, tuple(ffn_bufs))
    proj_parts = [_add_halves([g], [r], sc_arr)[0] for g, r in zip(proj_mine, proj_other)]
    proj_payload = [pb for _, pb in proj_parts]
    n_proj = len(proj_payload)
    ffn_join_copies = functools.partial(_join_copies, first=3 * n_proj)
    scatter_and_join = lambda srcs, lands, send_sems, recv_sems: (
        _scatter_copies(srcs[:n_proj], lands, send_sems, recv_sems) + ffn_join_copies(srcs[n_proj:], (), send_sems, recv_sems))
    p_send, p_recv, started, p_lands, token_join = _split_start(
        "proj_scatter_start", proj_payload + ffn_bufs, _scatter_lands(proj_payload), scatter_and_join, 3 * n_proj + len(ffn_bufs))
    p_parts, ffn_bufs = started[:n_proj], started[n_proj:]
    dx, dg1 = _inproj_dx(xs, dx1, norm1_g, w_in_t, dp, token_join)
    _, proj_recv = _split_wait("proj_scatter_wait", p_send, p_recv, p_parts, p_lands, _scatter_copies, (dx,))

    dwdec_f16 = dwdec_f[0:LOWRANK]
    dwdec_b16 = dwdec_b[LOWRANK : 2 * LOWRANK]
    shard_major = lambda a: jnp.transpose(a.reshape(LOWRANK, N_SHARDS, KEY_W // N_SHARDS), (1, 0, 2))
    small_grads = {
        "norm1_g": dg1, "b_decay_f": dbdec_f, "b_decay_b": dbdec_b, "gla_norm_g": dgn, "gmlp_ln_g": dlng, "gmlp_ln_b": dlnb,
        "w_spatial": dws, "b_spatial": dbs, "norm2_g": dg2, "final_norm_g": dgf,
    }
    g_pack = _pack_small([small_grads[k] for k in SMALL_NAMES] + [loss_acc], [shard_major(dwdec_f16), shard_major(dwdec_b16)])
    proj_bufs = [_add_partials(pf, r, sc_arr, token_join) for (pf, _), r in zip(proj_parts, proj_recv)]
    proj_join_copies = functools.partial(_join_copies, first=7)
    tail_copies = lambda srcs, lands, send_sems, recv_sems: (
        _small_gather_copies(srcs[:1], lands, send_sems, recv_sems) + proj_join_copies(srcs[1:], (), send_sems, recv_sems))
    t_send, t_recv, (g_pack, *proj_bufs), g_lands, token_tail = _split_start(
        "tail_start", [g_pack] + proj_bufs, [jax.ShapeDtypeStruct((8, SMALL_TOTAL, LANES), F32)], tail_copies, 7 + len(proj_bufs))

    ffn_bufs, _ = _split_wait("join_wait", p_send, p_recv, ffn_bufs, [], ffn_join_copies, (dx, token_tail))
    adamw = lambda k, g: _adamw(big_shards[k], g, rows_of("m_", k), rows_of("v_", k))
    big_updates = {k: adamw(k, g) for k, g in zip(big_names[2:], ffn_bufs)}
    proj_bufs, _ = _split_wait(
        "proj_join_wait", t_send, t_recv, proj_bufs, [], proj_join_copies, tuple(u[1] for u in big_updates.values()))
    big_updates.update({k: adamw(k, g) for k, g in zip(big_names[:2], proj_bufs)})

    (g_pack,), (g_all,) = _split_wait(
        "small_gather_wait", t_send, t_recv, [g_pack], g_lands, _small_gather_copies, tuple(u[1] for u in big_updates.values()))
    pack_own = lambda pre: _pack_small([args[pre + k] for k in SMALL_NAMES], [args[pre + "w_decay_f"], args[pre + "w_decay_b"]])
    small_updates, loss_row = _adamw_small(g_all, g_pack, pack_own(""), pack_own("m_"), pack_own("v_"), [args[k] for k in SMALL_NAMES])

    names = ["norm1_g", "w_in", "w_decay_f", "b_decay_f", "w_decay_b", "b_decay_b", "gla_norm_g", "gmlp_ln_g", "gmlp_ln_b",
             "w_spatial", "b_spatial", "w_out", "norm2_g", "w_gate", "w_up", "w_down", "final_norm_g"]
    results = {"g": {}, "d": {}, "m": {}, "v": {}}
    for tag, arrays in zip("gdmv", small_updates):
        for k, a in zip(SMALL_NAMES + ["w_decay_f", "w_decay_b"], arrays):
            results[tag][k] = a.reshape(args[k].shape)
    for k in big_names:
        for tag, a in zip("gdmv", big_updates[k]):
            results[tag][k] = (jnp.transpose(a) if k in transposed else a).reshape(args[k].shape)

    loss = loss_row[0, 0]
    grad_x = dx.reshape(x.shape)
    return (loss, grad_x, *[results["g"][k] for k in names], *[results["d"][k] for k in names],
            *[results["m"][k] for k in names], *[results["v"][k] for k in names])
```

```python
import functools
import math

import jax
import jax.numpy as jnp
from jax import lax
from jax.experimental import pallas as pl
from jax.experimental.pallas import tpu as pltpu

F32, BF16 = jnp.float32, jnp.bfloat16

D_MODEL = 1024
GLA_HEADS = 4
GLA_DK = 64
GLA_DV = 128
KEY_W = GLA_HEADS * GLA_DK
GLA_W = GLA_HEADS * GLA_DV
GMLP_W = 512
GMLP_GROUPS = 4
GMLP_CHUNK = 128
LOWRANK = 16
GLA_CHUNK = 64
GLA_TAU = 16.0
PROJ_W = 2592
PROJ_WP = 2688
D_FF = 2816
N_SHARDS = 4
FF_SHARD = D_FF // N_SHARDS
EPS = 1e-6
LANES = 128
TOKEN_SHAPE = (8, LANES)
MIB = 1024 * 1024

ADAM_LR = 0.001
ADAM_B1 = 0.9
ADAM_B2 = 0.999
ADAM_EPS = 1e-08
ADAM_WD = 0.01
ADAM_STEP = 10

COL_Q, COL_K = 0, 256
COL_V, COL_G, COL_U, COL_VV = 512, 1024, 1536, 2048
COL_LR = 2560
ROW_LR, ROW_UV = 1536, 1568
HALF = D_MODEL // 2

MESH = pl.DeviceIdType.MESH


def _nn(a, b):
    return jnp.dot(a, b, preferred_element_type=F32)


def _nt(a, b):
    return lax.dot_general(a, b, (((1,), (1,)), ((), ())), preferred_element_type=F32)


def _tn(a, b):
    return lax.dot_general(a, b, (((0,), (0,)), ((), ())), preferred_element_type=F32)


def _bnn(a, b):
    return jnp.einsum("nik,nkj->nij", a, b, preferred_element_type=F32)


def _bnt(a, b):
    return jnp.einsum("nik,njk->nij", a, b, preferred_element_type=F32)


def _btn(a, b):
    return jnp.einsum("nki,nkj->nij", a, b, preferred_element_type=F32)


def _resident(shape):
    zeros = (0,) * len(shape)
    return pl.BlockSpec(shape, lambda *_: zeros, pipeline_mode=pl.Buffered(1))


def _params(vmem_mib, semantics=("arbitrary",)):
    return pltpu.CompilerParams(vmem_limit_bytes=vmem_mib * MIB, dimension_semantics=semantics)


def _sigmoid(x):
    return 1.0 / (1.0 + jnp.exp(-x))


def _gelu(x):
    return 0.5 * x * (1.0 + lax.erf(x * (1.0 / math.sqrt(2.0))))


def _gelu_and_grad(x):
    cdf = 0.5 * (1.0 + lax.erf(x * (1.0 / math.sqrt(2.0))))
    return x * cdf, cdf + x * jnp.exp(-0.5 * x * x) * (1.0 / math.sqrt(2.0 * math.pi))


def _log_sigmoid(x):
    return jnp.minimum(x, 0.0) - jnp.log(1.0 + jnp.exp(-jnp.abs(x)))


def _rms_bwd(dxh, xh, r):
    return r * (dxh - xh * jnp.mean(dxh * xh, axis=-1, keepdims=True))


def _chunk_cumsum(v, row_in_chunk, reverse):
    rows = v.shape[0]
    for sh in (1, 2, 4, 8, 16, 32):
        if reverse:
            v = v + jnp.where(row_in_chunk + sh < GLA_CHUNK, pltpu.roll(v, rows - sh, axis=0), 0.0)
        else:
            v = v + jnp.where(row_in_chunk >= sh, pltpu.roll(v, sh, axis=0), 0.0)
    return v


def _inproj(x, g1, w_in_t, token):
    seq = x.shape[0]
    tm = min(seq, 512)

    def body(x_ref, g_ref, w_ref, token_ref, p_ref):
        xv = x_ref[...]
        r = lax.rsqrt(jnp.mean(xv * xv, axis=-1, keepdims=True) + EPS)
        h = (xv * r * g_ref[...]).astype(BF16)
        p_ref[:, 0:COL_U] = _nt(h, w_ref[0:ROW_LR, :])
        p_ref[:, COL_U:COL_LR] = _nt(h, w_ref[ROW_UV:PROJ_W, :])
        p_ref[:, COL_LR:PROJ_WP] = _nt(h, w_ref[ROW_LR : ROW_LR + LANES, :])

    return pl.pallas_call(
        body,
        name="inproj",
        grid=(seq // tm,),
        in_specs=[pl.BlockSpec((tm, D_MODEL), lambda i: (i, 0)), _resident((1, D_MODEL)), _resident((PROJ_W, D_MODEL)), _resident(TOKEN_SHAPE)],
        out_specs=pl.BlockSpec((tm, PROJ_WP), lambda i: (i, 0)),
        out_shape=jax.ShapeDtypeStruct((seq, PROJ_WP), F32),
        compiler_params=_params(48, ("parallel",)),
    )(x, g1, w_in_t, token)


def _gla_tile(seq):
    return min(seq, 1024)


def _gla_decay_terms(lr_bf, wd_ref, bd_ref, pair, row_in_chunk, reverse, n):
    cols = pl.ds(pair * LANES, LANES)
    pre = _nn(lr_bf, wd_ref[:, cols]) + bd_ref[:, cols]
    la = _log_sigmoid(pre) * (1.0 / GLA_TAU)
    b = _chunk_cumsum(la, row_in_chunk, reverse)
    b3 = b.reshape(n, GLA_CHUNK, LANES)
    blast = b3[:, 0:1, :] if reverse else b3[:, GLA_CHUNK - 1 : GLA_CHUNK, :]
    return pre, b3, blast


def _gla_fwd(p, wd_pad, bd, token, reverse):
    seq = p.shape[0]
    tg = _gla_tile(seq)
    nt = seq // tg
    n = tg // GLA_CHUNK
    scale = GLA_DK**-0.5

    def tile(i):
        return nt - 1 - i if reverse else i

    def body(q_ref, k_ref, v_ref, lr_ref, wd_ref, bd_ref, token_ref, o_ref, st_ref, carry):
        @pl.when(pl.program_id(0) == 0)
        def _():
            carry[...] = jnp.zeros_like(carry)

        lr_bf = lr_ref[...].astype(BF16)
        states = [carry[h] for h in range(GLA_HEADS)]
        row_in_chunk = lax.broadcasted_iota(jnp.int32, (tg, LANES), 0) % GLA_CHUNK
        lane_head = lax.broadcasted_iota(jnp.int32, (1, LANES), 1) // GLA_DK
        tt = lax.broadcasted_iota(jnp.int32, (GLA_CHUNK, GLA_CHUNK), 0)
        ss = lax.broadcasted_iota(jnp.int32, (GLA_CHUNK, GLA_CHUNK), 1)
        causal = (tt <= ss) if reverse else (tt >= ss)
        order = range(n - 1, -1, -1) if reverse else range(n)
        heads = range(GLA_HEADS)
        qds, vhs, decs, sc_raw, dst = {}, {}, {}, {}, {}
        for pair in range(2):
            cols = pl.ds(pair * LANES, LANES)
            _, b3, blast = _gla_decay_terms(lr_bf, wd_ref, bd_ref, pair, row_in_chunk, reverse, n)
            q3 = q_ref[:, cols].reshape(n, GLA_CHUNK, LANES) * scale
            k3 = k_ref[:, cols].reshape(n, GLA_CHUNK, LANES)
            qd = q3 * jnp.exp(b3)
            kd = (k3 * jnp.exp(-b3)).astype(BF16)
            kte = k3 * jnp.exp(blast - b3)
            decs[pair] = jnp.exp(blast)
            qds[pair] = qd.astype(BF16)
            m0 = (lane_head == 0).astype(F32)
            m1 = (lane_head == 1).astype(F32)
            q_both = jnp.concatenate([(qd * m0).astype(BF16), (qd * m1).astype(BF16)], axis=1)
            sc_both = _bnt(q_both, kd)
            for hh, m in ((0, m0), (1, m1)):
                h = 2 * pair + hh
                vhs[h] = v_ref[:, pl.ds(h * GLA_DV, GLA_DV)].reshape(n, GLA_CHUNK, GLA_DV).astype(BF16)
                sc_raw[h] = sc_both[:, hh * GLA_CHUNK : (hh + 1) * GLA_CHUNK, :]
                dst[h] = _btn(vhs[h], (kte * m).astype(BF16))
        o_intra, befores = {}, {}
        for h in heads:
            o_intra[h] = _bnn(jnp.where(causal, sc_raw[h], 0.0).astype(BF16), vhs[h])
            st, before = states[h], [None] * n
            for j in order:
                before[j] = st
                st = st * decs[h // 2][j] + dst[h][j]
            states[h] = st
            befores[h] = jnp.stack(before).astype(BF16)
        outs = {}
        for pair in range(2):
            both = jnp.concatenate([befores[2 * pair], befores[2 * pair + 1]], axis=1)
            o_inter = _bnt(qds[pair], both)
            for hh in range(2):
                h = 2 * pair + hh
                outs[h] = (o_intra[h] + o_inter[:, :, hh * GLA_DV : (hh + 1) * GLA_DV]).reshape(tg, GLA_DV)
        for h in range(GLA_HEADS):
            o_ref[:, pl.ds(h * GLA_DV, GLA_DV)] = outs[h]
            st_ref[:, h] = befores[h]
            carry[h] = states[h]

    nchunks = seq // GLA_CHUNK
    return pl.pallas_call(
        body,
        name="gla_fwd_rev" if reverse else "gla_fwd",
        grid=(nt,),
        in_specs=[
            pl.BlockSpec((tg, KEY_W), lambda i: (tile(i), COL_Q // KEY_W)),
            pl.BlockSpec((tg, KEY_W), lambda i: (tile(i), COL_K // KEY_W)),
            pl.BlockSpec((tg, GLA_W), lambda i: (tile(i), COL_V // GLA_W)),
            pl.BlockSpec((tg, LANES), lambda i: (tile(i), COL_LR // LANES)),
            _resident((LANES, KEY_W)),
            _resident((1, KEY_W)),
            _resident(TOKEN_SHAPE),
        ],
        out_specs=[
            pl.BlockSpec((tg, GLA_W), lambda i: (tile(i), 0)),
            pl.BlockSpec((n, GLA_HEADS, GLA_DV, LANES), lambda i: (tile(i), 0, 0, 0)),
        ],
        out_shape=[
            jax.ShapeDtypeStruct((seq, GLA_W), F32),
            jax.ShapeDtypeStruct((nchunks, GLA_HEADS, GLA_DV, LANES), BF16),
        ],
        scratch_shapes=[pltpu.VMEM((GLA_HEADS, GLA_DV, LANES), F32)],
        compiler_params=_params(48),
    )(p, p, p, p, wd_pad, bd, token)


def _mixer_out(x, o_f, o_b, p, gn, lng, lnb, ws_bf, bs_col, w_out, token):
    seq = x.shape[0]
    tm = min(seq, 512)

    def body(x_ref, of_ref, ob_ref, g_ref, u_ref, vv_ref, gn_ref, lng_ref, lnb_ref, ws_ref, bs_ref, wo_ref, token_ref, x1_ref, yc_ref, vn_sc):
        for h in range(GLA_HEADS):
            cols = pl.ds(h * GLA_DV, GLA_DV)
            oh = of_ref[:, cols] + ob_ref[:, cols]
            on = oh * lax.rsqrt(jnp.mean(oh * oh, axis=-1, keepdims=True) + EPS)
            gh = g_ref[:, cols]
            yc_ref[:, cols] = (on * gn_ref[:, cols] * (gh * _sigmoid(gh))).astype(BF16)
        zv = _gelu(vv_ref[...])
        xc = zv - jnp.mean(zv, axis=-1, keepdims=True)
        vhat = xc * lax.rsqrt(jnp.mean(xc * xc, axis=-1, keepdims=True) + EPS)
        vn_sc[...] = (vhat * lng_ref[...] + lnb_ref[...]).astype(BF16)
        for c in range(tm // GMLP_CHUNK):
            rows = pl.ds(c * GMLP_CHUNK, GMLP_CHUNK)
            for g in range(GMLP_GROUPS):
                cols = pl.ds(g * LANES, LANES)
                s = _nn(ws_ref[g], vn_sc[rows, cols]) + bs_ref[g]
                yc_ref[rows, pl.ds(GLA_W + g * LANES, LANES)] = (_gelu(u_ref[rows, cols]) * s).astype(BF16)
        x1_ref[...] = x_ref[...] + _nn(yc_ref[...], wo_ref[...])

    row = lambda w: pl.BlockSpec((tm, w), lambda i: (i, 0))
    pcol = lambda col: pl.BlockSpec((tm, GLA_W), lambda i: (i, col // GLA_W))
    return pl.pallas_call(
        body,
        name="mixer_out",
        grid=(seq // tm,),
        in_specs=[
            row(D_MODEL), row(GLA_W), row(GLA_W), pcol(COL_G), pcol(COL_U), pcol(COL_VV),
            _resident((1, GLA_W)), _resident((1, GMLP_W)), _resident((1, GMLP_W)),
            _resident((GMLP_GROUPS, GMLP_CHUNK, GMLP_CHUNK)), _resident((GMLP_GROUPS, GMLP_CHUNK, 1)),
            _resident((D_MODEL, D_MODEL)), _resident(TOKEN_SHAPE),
        ],
        out_specs=[row(D_MODEL), row(D_MODEL)],
        out_shape=[jax.ShapeDtypeStruct((seq, D_MODEL), F32), jax.ShapeDtypeStruct((seq, D_MODEL), BF16)],
        scratch_shapes=[pltpu.VMEM((tm, GMLP_W), BF16)],
        compiler_params=_params(48, ("parallel",)),
    )(x, o_f, o_b, p, p, p, gn, lng, lnb, ws_bf, bs_col, w_out, token)


def _ffn_fwd(x1, target, g2, gf, wg_t, wu_t, wd):
    seq = x1.shape[0]
    tm = min(seq, 256)

    def body(x1_ref, t_ref, g2_ref, gf_ref, wg_ref, wu_ref, wd_ref, h2_ref, gate_ref, up_ref, act_ref, dx2_ref, loss_ref, dgf_ref):
        @pl.when(pl.program_id(0) == 0)
        def _():
            loss_ref[...] = jnp.zeros_like(loss_ref)
            dgf_ref[...] = jnp.zeros_like(dgf_ref)

        x1v = x1_ref[...]
        h2 = (x1v * lax.rsqrt(jnp.mean(x1v * x1v, axis=-1, keepdims=True) + EPS) * g2_ref[...]).astype(BF16)
        h2_ref[...] = h2
        gate = _nt(h2, wg_ref[...])
        up = _nt(h2, wu_ref[...])
        act = (gate * _sigmoid(gate) * up).astype(BF16)
        gate_ref[...] = gate
        up_ref[...] = up
        act_ref[...] = act
        x2 = x1v + _nn(act, wd_ref[...])
        rf = lax.rsqrt(jnp.mean(x2 * x2, axis=-1, keepdims=True) + EPS)
        xh = x2 * rf
        err = xh * gf_ref[...] - t_ref[...]
        loss_ref[...] += 0.5 * jnp.sum(jnp.mean(err * err, axis=-1, keepdims=True))
        dy = err * (1.0 / D_MODEL)
        dgf_ref[...] += jnp.sum(dy * xh, axis=0, keepdims=True)
        dx2_ref[...] = _rms_bwd(dy * gf_ref[...], xh, rf)

    row = lambda w: pl.BlockSpec((tm, w), lambda i: (i, 0))
    weight = _resident((D_FF, D_MODEL))
    return pl.pallas_call(
        body,
        name="ffn_fwd",
        grid=(seq // tm,),
        in_specs=[row(D_MODEL), row(D_MODEL), _resident((1, D_MODEL)), _resident((1, D_MODEL)), weight, weight, weight],
        out_specs=[row(D_MODEL), row(D_FF), row(D_FF), row(D_FF), row(D_MODEL),
                   pl.BlockSpec((1, LANES), lambda i: (0, 0)), pl.BlockSpec((1, D_MODEL), lambda i: (0, 0))],
        out_shape=[
            jax.ShapeDtypeStruct((seq, D_MODEL), BF16),
            jax.ShapeDtypeStruct((seq, D_FF), F32),
            jax.ShapeDtypeStruct((seq, D_FF), F32),
            jax.ShapeDtypeStruct((seq, D_FF), BF16),
            jax.ShapeDtypeStruct((seq, D_MODEL), F32),
            jax.ShapeDtypeStruct((1, LANES), F32),
            jax.ShapeDtypeStruct((1, D_MODEL), F32),
        ],
        compiler_params=_params(56),
    )(x1, target, g2, gf, wg_t, wu_t, wd)


def _ffn_bwd(dx2, gate, up, x1, g2, wg_t, wu_t, wd):
    seq = x1.shape[0]
    tm = min(seq, 256)

    def body(dx2_ref, gate_ref, up_ref, x1_ref, g2_ref, wg_ref, wu_ref, wd_ref, dgate_ref, dup_ref, dx1_ref, dg2_ref):
        @pl.when(pl.program_id(0) == 0)
        def _():
            dg2_ref[...] = jnp.zeros_like(dg2_ref)

        dx2v = dx2_ref[...]
        dact = _nt(dx2v.astype(BF16), wd_ref[...])
        gate = gate_ref[...]
        sg = _sigmoid(gate)
        dgate = (dact * up_ref[...] * (sg * (1.0 + gate * (1.0 - sg)))).astype(BF16)
        dup = (dact * (gate * sg)).astype(BF16)
        dgate_ref[...] = dgate
        dup_ref[...] = dup
        dh2 = _nn(dgate, wg_ref[...]) + _nn(dup, wu_ref[...])
        x1v = x1_ref[...]
        r2 = lax.rsqrt(jnp.mean(x1v * x1v, axis=-1, keepdims=True) + EPS)
        xh = x1v * r2
        dg2_ref[...] += jnp.sum(dh2 * xh, axis=0, keepdims=True)
        dx1_ref[...] = dx2v + _rms_bwd(dh2 * g2_ref[...], xh, r2)

    row = lambda w: pl.BlockSpec((tm, w), lambda i: (i, 0))
    weight = _resident((D_FF, D_MODEL))
    return pl.pallas_call(
        body,
        name="ffn_bwd",
        grid=(seq // tm,),
        in_specs=[row(D_MODEL), row(D_FF), row(D_FF), row(D_MODEL), _resident((1, D_MODEL)), weight, weight, weight],
        out_specs=[row(D_FF), row(D_FF), row(D_MODEL), pl.BlockSpec((1, D_MODEL), lambda i: (0, 0))],
        out_shape=[
            jax.ShapeDtypeStruct((seq, D_FF), BF16),
            jax.ShapeDtypeStruct((seq, D_FF), BF16),
            jax.ShapeDtypeStruct((seq, D_MODEL), F32),
            jax.ShapeDtypeStruct((1, D_MODEL), F32),
        ],
        compiler_params=_params(56),
    )(dx2, gate, up, x1, g2, wg_t, wu_t, wd)


WGRAD_ROWS = D_FF // 2


def _ffn_wgrad(h2, dgate, dup, act, dx2):
    seq = h2.shape[0]
    tm = min(seq, 512)

    def body(h2_ref, dgate_ref, dup_ref, act_ref, dx2_ref, dwg_ref, dwu_ref, dwd_ref):
        @pl.when(pl.program_id(1) == 0)
        def _():
            dwg_ref[...] = jnp.zeros_like(dwg_ref)
            dwu_ref[...] = jnp.zeros_like(dwu_ref)
            dwd_ref[...] = jnp.zeros_like(dwd_ref)

        h2v = h2_ref[...]
        dwg_ref[...] += _tn(dgate_ref[...], h2v)
        dwu_ref[...] += _tn(dup_ref[...], h2v)
        dwd_ref[...] += _tn(act_ref[...], dx2_ref[...].astype(BF16))

    ff = pl.BlockSpec((tm, WGRAD_ROWS), lambda j, i: (i, j))
    row = pl.BlockSpec((tm, D_MODEL), lambda j, i: (i, 0))
    out = pl.BlockSpec((WGRAD_ROWS, D_MODEL), lambda j, i: (j, 0))
    return pl.pallas_call(
        body,
        name="ffn_wgrad",
        grid=(D_FF // WGRAD_ROWS, seq // tm),
        in_specs=[row, ff, ff, ff, row],
        out_specs=[out, out, out],
        out_shape=[jax.ShapeDtypeStruct((D_FF, D_MODEL), F32)] * 3,
        compiler_params=_params(56, ("parallel", "arbitrary")),
    )(h2, dgate, dup, act, dx2)


def _mixer_bwd(dx1, ycat, o_f, o_b, p, gn, lng, lnb, ws_bf, wst_bf, bs_col, w_out, token):
    seq = dx1.shape[0]
    tm = min(seq, 512)
    nsteps = seq // tm

    def body(dx1_ref, yc_ref, of_ref, ob_ref, g_ref, u_ref, vv_ref, gn_ref, lng_ref, lnb_ref, ws_ref, wst_ref, bs_ref, wo_ref, token_ref,
             do_ref, dg_ref, du_ref, dvv_ref, dwo_ref, dgn_ref, dlng_ref, dlnb_ref, dws_ref, dbs_ref, vn_sc, dvn_sc, dbs_acc):
        step = pl.program_id(0)

        @pl.when(step == 0)
        def _():
            for r in (dwo_ref, dgn_ref, dlng_ref, dlnb_ref, dws_ref, dbs_acc):
                r[...] = jnp.zeros_like(r)

        dx1b = dx1_ref[...].astype(BF16)
        dyc = _nt(dx1b, wo_ref[...])
        dwo_ref[...] += _tn(yc_ref[...], dx1b)
        for h in range(GLA_HEADS):
            cols = pl.ds(h * GLA_DV, GLA_DV)
            dya = dyc[:, h * GLA_DV : (h + 1) * GLA_DV]
            oh = of_ref[:, cols] + ob_ref[:, cols]
            rn = lax.rsqrt(jnp.mean(oh * oh, axis=-1, keepdims=True) + EPS)
            on = oh * rn
            gh = g_ref[:, cols]
            sg = _sigmoid(gh)
            sil = gh * sg
            gnh = gn_ref[:, cols]
            dgn_ref[:, cols] += jnp.sum(dya * on * sil, axis=0, keepdims=True)
            dg_ref[:, cols] = (dya * on * gnh * (sg * (1.0 + gh * (1.0 - sg)))).astype(BF16)
            do_ref[:, cols] = _rms_bwd(dya * gnh * sil, on, rn)
        vv = vv_ref[...]
        zv, zv_grad = _gelu_and_grad(vv)
        xc = zv - jnp.mean(zv, axis=-1, keepdims=True)
        rstd = lax.rsqrt(jnp.mean(xc * xc, axis=-1, keepdims=True) + EPS)
        vhat = xc * rstd
        vn_sc[...] = (vhat * lng_ref[...] + lnb_ref[...]).astype(BF16)
        for c in range(tm // GMLP_CHUNK):
            rows = pl.ds(c * GMLP_CHUNK, GMLP_CHUNK)
            for g in range(GMLP_GROUPS):
                cols = pl.ds(g * LANES, LANES)
                vn = vn_sc[rows, cols]
                s = _nn(ws_ref[g], vn) + bs_ref[g]
                dyb = dyc[c * GMLP_CHUNK : (c + 1) * GMLP_CHUNK, GLA_W + g * LANES : GLA_W + (g + 1) * LANES]
                zu, zu_grad = _gelu_and_grad(u_ref[rows, cols])
                du_ref[rows, cols] = (dyb * s * zu_grad).astype(BF16)
                ds = dyb * zu
                dbs_acc[g] += ds
                dsb = ds.astype(BF16)
                dws_ref[g] += _nt(dsb, vn)
                dvn_sc[rows, cols] = _nn(wst_ref[g], dsb)
        dvn = dvn_sc[...]
        dlng_ref[...] += jnp.sum(dvn * vhat, axis=0, keepdims=True)
        dlnb_ref[...] += jnp.sum(dvn, axis=0, keepdims=True)
        dvh = dvn * lng_ref[...]
        dzv = rstd * (dvh - jnp.mean(dvh, axis=-1, keepdims=True) - vhat * jnp.mean(dvh * vhat, axis=-1, keepdims=True))
        dvv_ref[...] = (dzv * zv_grad).astype(BF16)

        @pl.when(step == nsteps - 1)
        def _():
            dbs_ref[...] = jnp.sum(dbs_acc[...], axis=-1, keepdims=True)

    row = lambda w: pl.BlockSpec((tm, w), lambda i: (i, 0))
    pcol = lambda col: pl.BlockSpec((tm, GLA_W), lambda i: (i, col // GLA_W))
    const = lambda shape: pl.BlockSpec(shape, lambda i: (0,) * len(shape))
    return pl.pallas_call(
        body,
        name="mixer_bwd",
        grid=(nsteps,),
        in_specs=[
            row(D_MODEL), row(D_MODEL), row(GLA_W), row(GLA_W), pcol(COL_G), pcol(COL_U), pcol(COL_VV),
            _resident((1, GLA_W)), _resident((1, GMLP_W)), _resident((1, GMLP_W)),
            _resident((GMLP_GROUPS, GMLP_CHUNK, GMLP_CHUNK)), _resident((GMLP_GROUPS, GMLP_CHUNK, GMLP_CHUNK)),
            _resident((GMLP_GROUPS, GMLP_CHUNK, 1)), _resident((D_MODEL, D_MODEL)), _resident(TOKEN_SHAPE),
        ],
        out_specs=[
            row(GLA_W), row(GLA_W), row(GMLP_W), row(GMLP_W), const((D_MODEL, D_MODEL)),
            const((1, GLA_W)), const((1, GMLP_W)), const((1, GMLP_W)),
            const((GMLP_GROUPS, GMLP_CHUNK, GMLP_CHUNK)), const((GMLP_GROUPS, GMLP_CHUNK, 1)),
        ],
        out_shape=[
            jax.ShapeDtypeStruct((seq, GLA_W), F32), jax.ShapeDtypeStruct((seq, GLA_W), BF16),
            jax.ShapeDtypeStruct((seq, GMLP_W), BF16), jax.ShapeDtypeStruct((seq, GMLP_W), BF16),
            jax.ShapeDtypeStruct((D_MODEL, D_MODEL), F32),
            jax.ShapeDtypeStruct((1, GLA_W), F32), jax.ShapeDtypeStruct((1, GMLP_W), F32), jax.ShapeDtypeStruct((1, GMLP_W), F32),
            jax.ShapeDtypeStruct((GMLP_GROUPS, GMLP_CHUNK, GMLP_CHUNK), F32), jax.ShapeDtypeStruct((GMLP_GROUPS, GMLP_CHUNK, 1), F32),
        ],
        scratch_shapes=[pltpu.VMEM((tm, GMLP_W), BF16), pltpu.VMEM((tm, GMLP_W), F32), pltpu.VMEM((GMLP_GROUPS, GMLP_CHUNK, GMLP_CHUNK), F32)],
        compiler_params=_params(56),
    )(dx1, ycat, o_f, o_b, p, p, p, gn, lng, lnb, ws_bf, wst_bf, bs_col, w_out, token)


def _gla_bwd(p, do, st, wd_pad, bd, token, reverse, other=None):
    seq = p.shape[0]
    tg = _gla_tile(seq)
    nt = seq // tg
    n = tg // GLA_CHUNK
    scale = GLA_DK**-0.5

    def tile(i):
        return i if reverse else nt - 1 - i

    def body(q_ref, k_ref, v_ref, lr_ref, do_ref, st_ref, wd_ref, bd_ref, token_ref, *rest):
        others, (dq_ref, dk_ref, dv_ref, dlr_ref, dwd_ref, dbd_ref, carry) = rest[:-7], rest[-7:]
        if others:
            odq_ref, odk_ref, odv_ref, odlr_ref = others

            def put(ref, idx, val, oref):
                ref[idx] = (val + oref[idx]).astype(BF16)
        else:
            odq_ref = odk_ref = odv_ref = odlr_ref = None

            def put(ref, idx, val, oref):
                ref[idx] = val

        @pl.when(pl.program_id(0) == 0)
        def _():
            carry[...] = jnp.zeros_like(carry)
            dwd_ref[...] = jnp.zeros_like(dwd_ref)
            dbd_ref[...] = jnp.zeros_like(dbd_ref)

        lr_bf = lr_ref[...].astype(BF16)
        carries = [carry[h] for h in range(GLA_HEADS)]
        row_in_chunk = lax.broadcasted_iota(jnp.int32, (tg, LANES), 0) % GLA_CHUNK
        lane_head = lax.broadcasted_iota(jnp.int32, (1, LANES), 1) // GLA_DK
        tt = lax.broadcasted_iota(jnp.int32, (GLA_CHUNK, GLA_CHUNK), 0)
        ss = lax.broadcasted_iota(jnp.int32, (GLA_CHUNK, GLA_CHUNK), 1)
        causal = (tt <= ss) if reverse else (tt >= ss)
        order = range(n) if reverse else range(n - 1, -1, -1)
        dlr = jnp.zeros((tg, LANES), F32)
        heads = range(GLA_HEADS)
        pv, masks, qdh, vhs, dohs, stbs = {}, {}, {}, {}, {}, {}
        sc_raw, dp, acc = {}, {}, {}
        for pair in range(2):
            cols = pl.ds(pair * LANES, LANES)
            pre, b3, blast = _gla_decay_terms(lr_bf, wd_ref, bd_ref, pair, row_in_chunk, reverse, n)
            q3 = q_ref[:, cols].reshape(n, GLA_CHUNK, LANES) * scale
            k3 = k_ref[:, cols].reshape(n, GLA_CHUNK, LANES)
            eb = jnp.exp(b3)
            emb = jnp.exp(-b3)
            ekte = jnp.exp(blast - b3)
            kdf = k3 * emb
            kte = k3 * ekte
            both = pl.ds(2 * pair * GLA_DV, 2 * GLA_DV)
            pv[pair] = dict(pre=pre, eb=eb, emb=emb, ekte=ekte, qd=q3 * eb, kdf=kdf, kd=kdf.astype(BF16), kte=kte, kte_bf=kte.astype(BF16),
                            dec=jnp.exp(blast), v=v_ref[:, both].reshape(n, GLA_CHUNK, 2 * GLA_DV).astype(BF16),
                            do=do_ref[:, both].reshape(n, GLA_CHUNK, 2 * GLA_DV).astype(BF16))
            for hh in range(2):
                h = 2 * pair + hh
                masks[h] = (lane_head == hh).astype(F32)
                qdh[h] = (pv[pair]["qd"] * masks[h]).astype(BF16)
                vhs[h] = pv[pair]["v"][:, :, hh * GLA_DV : (hh + 1) * GLA_DV]
                dohs[h] = pv[pair]["do"][:, :, hh * GLA_DV : (hh + 1) * GLA_DV]
                stbs[h] = st_ref[:, h]
                dp[h] = _bnt(dohs[h], vhs[h])
                acc[h] = _btn(dohs[h], qdh[h])
            sc_both = _bnt(jnp.concatenate([qdh[2 * pair], qdh[2 * pair + 1]], axis=1), pv[pair]["kd"])
            for hh in range(2):
                sc_raw[2 * pair + hh] = sc_both[:, hh * GLA_CHUNK : (hh + 1) * GLA_CHUNK, :]
        dsa, sc = {}, {}
        for h in heads:
            sc[h] = jnp.where(causal, sc_raw[h], 0.0).astype(BF16)
            dp[h] = jnp.where(causal, dp[h], 0.0).astype(BF16)
            dec = pv[h // 2]["dec"]
            c, after = carries[h], [None] * n
            for j in order:
                after[j] = c
                c = acc[h][j] + dec[j] * c
            carries[h] = c
            dsa[h] = jnp.stack(after)
        dvs, dqs, dks, dwds, dbds = [], [], [], [], []
        for pair in range(2):
            cols = pl.ds(pair * LANES, LANES)
            v = pv[pair]
            h0, h1 = 2 * pair, 2 * pair + 1
            dsa_both = jnp.concatenate([dsa[h0], dsa[h1]], axis=1)
            dsa_bf = dsa_both.astype(BF16)
            stb_bf = jnp.concatenate([stbs[h0], stbs[h1]], axis=1)
            dq_intra = _bnn(jnp.concatenate([dp[h0], dp[h1]], axis=1), v["kd"])
            dqd = (dq_intra[:, :GLA_CHUNK, :] * masks[h0] + dq_intra[:, GLA_CHUNK:, :] * masks[h1]) + _bnn(v["do"], stb_bf)
            dkd = _btn(dp[h0], qdh[h0]) + _btn(dp[h1], qdh[h1])
            dkte = _bnn(v["v"], dsa_bf)
            ddec = jnp.sum(dsa[h0] * stbs[h0].astype(F32) + dsa[h1] * stbs[h1].astype(F32), axis=1, keepdims=True)
            dv_inter = _bnt(v["kte_bf"], dsa_bf)
            for hh, h in ((0, h0), (1, h1)):
                dvs.append((_btn(sc[h], dohs[h]) + dv_inter[:, :, hh * GLA_DV : (hh + 1) * GLA_DV]).reshape(tg, GLA_DV))
            dqs.append((dqd * (scale * v["eb"])).reshape(tg, LANES))
            dks.append((dkd * v["emb"] + dkte * v["ekte"]).reshape(tg, LANES))
            db = dqd * v["qd"] - dkd * v["kdf"] - dkte * v["kte"]
            dblast = jnp.sum(dkte * v["kte"], axis=1, keepdims=True) + ddec * v["dec"]
            dla = _chunk_cumsum(db.reshape(tg, LANES), row_in_chunk, not reverse) + jnp.broadcast_to(dblast, (n, GLA_CHUNK, LANES)).reshape(tg, LANES)
            dpre = (dla * (1.0 / GLA_TAU) * _sigmoid(-v["pre"]))
            dpre_bf = dpre.astype(BF16)
            dlr = dlr + _nt(dpre_bf, wd_ref[:, cols])
            dwds.append(_tn(lr_bf, dpre_bf))
            dbds.append(jnp.sum(dpre, axis=0, keepdims=True))
        put(dlr_ref, (slice(None), slice(None)), dlr, odlr_ref)
        for pair in range(2):
            cols = pl.ds(pair * LANES, LANES)
            put(dq_ref, (slice(None), cols), dqs[pair], odq_ref)
            put(dk_ref, (slice(None), cols), dks[pair], odk_ref)
            dwd_ref[:, cols] += dwds[pair]
            dbd_ref[:, cols] += dbds[pair]
        for h in range(GLA_HEADS):
            put(dv_ref, (slice(None), pl.ds(h * GLA_DV, GLA_DV)), dvs[h], odv_ref)
            carry[h] = carries[h]

    pieces = [
        pl.BlockSpec((tg, KEY_W), lambda i: (tile(i), 0)),
        pl.BlockSpec((tg, KEY_W), lambda i: (tile(i), 0)),
        pl.BlockSpec((tg, GLA_W), lambda i: (tile(i), 0)),
        pl.BlockSpec((tg, LANES), lambda i: (tile(i), 0)),
    ]
    piece_dtype = BF16 if other else F32
    return pl.pallas_call(
        body,
        name="gla_bwd_rev" if reverse else "gla_bwd",
        grid=(nt,),
        in_specs=[
            pl.BlockSpec((tg, KEY_W), lambda i: (tile(i), COL_Q // KEY_W)),
            pl.BlockSpec((tg, KEY_W), lambda i: (tile(i), COL_K // KEY_W)),
            pl.BlockSpec((tg, GLA_W), lambda i: (tile(i), COL_V // GLA_W)),
            pl.BlockSpec((tg, LANES), lambda i: (tile(i), COL_LR // LANES)),
            pl.BlockSpec((tg, GLA_W), lambda i: (tile(i), 0)),
            pl.BlockSpec((n, GLA_HEADS, GLA_DV, LANES), lambda i: (tile(i), 0, 0, 0)),
            _resident((LANES, KEY_W)),
            _resident((1, KEY_W)),
            _resident(TOKEN_SHAPE),
        ] + (pieces if other else []),
        out_specs=pieces + [pl.BlockSpec((LANES, KEY_W), lambda i: (0, 0)), pl.BlockSpec((1, KEY_W), lambda i: (0, 0))],
        out_shape=[
            jax.ShapeDtypeStruct((seq, KEY_W), piece_dtype), jax.ShapeDtypeStruct((seq, KEY_W), piece_dtype),
            jax.ShapeDtypeStruct((seq, GLA_W), piece_dtype), jax.ShapeDtypeStruct((seq, LANES), piece_dtype),
            jax.ShapeDtypeStruct((LANES, KEY_W), F32), jax.ShapeDtypeStruct((1, KEY_W), F32),
        ],
        scratch_shapes=[pltpu.VMEM((GLA_HEADS, GLA_DV, LANES), F32)],
        compiler_params=_params(56),
    )(p, p, p, p, do, st, wd_pad, bd, token, *(other or ()))


def _inproj_wgrad(x, g1, dq, dk, dv, dg, du, dvv, dlr):
    seq = x.shape[0]
    tm = min(seq, 512)

    def body(x_ref, g1_ref, dq_ref, dk_ref, dv_ref, dg_ref, du_ref, dvv_ref, dlr_ref, dw_ref, dp_ref):
        @pl.when(pl.program_id(0) == 0)
        def _():
            dw_ref[...] = jnp.zeros_like(dw_ref)

        for col, ref in ((COL_Q, dq_ref), (COL_K, dk_ref), (COL_V, dv_ref), (COL_G, dg_ref), (COL_U, du_ref), (COL_VV, dvv_ref), (COL_LR, dlr_ref)):
            dp_ref[:, col : col + ref.shape[1]] = ref[...]
        xv = x_ref[...]
        h = (xv * lax.rsqrt(jnp.mean(xv * xv, axis=-1, keepdims=True) + EPS) * g1_ref[...]).astype(BF16)
        dw_ref[0:ROW_LR, :] += _tn(dp_ref[:, 0:COL_U], h)
        dw_ref[ROW_UV:PROJ_W, :] += _tn(dp_ref[:, COL_U:COL_LR], h)
        dw_ref[ROW_LR:ROW_UV, :] += _tn(dp_ref[:, COL_LR:PROJ_WP], h)[0 : ROW_UV - ROW_LR]

    row = lambda w: pl.BlockSpec((tm, w), lambda i: (i, 0))
    return pl.pallas_call(
        body,
        name="inproj_wgrad",
        grid=(seq // tm,),
        in_specs=[row(D_MODEL), _resident((1, D_MODEL)), row(KEY_W), row(KEY_W), row(GLA_W), row(GLA_W), row(GMLP_W), row(GMLP_W), row(LANES)],
        out_specs=[pl.BlockSpec((PROJ_W, D_MODEL), lambda i: (0, 0)), row(PROJ_WP)],
        out_shape=[jax.ShapeDtypeStruct((PROJ_W, D_MODEL), F32), jax.ShapeDtypeStruct((seq, PROJ_WP), BF16)],
        compiler_params=_params(56),
    )(x, g1, dq, dk, dv, dg, du, dvv, dlr)


def _inproj_dx(x, dx1, g1, w_in_t, dp, token):
    seq = x.shape[0]
    tm = min(seq, 512)

    def body(x_ref, dx1_ref, g1_ref, w_ref, dp_ref, token_ref, dx_ref, dg1_ref):
        @pl.when(pl.program_id(0) == 0)
        def _():
            dg1_ref[...] = jnp.zeros_like(dg1_ref)

        xv = x_ref[...]
        r1 = lax.rsqrt(jnp.mean(xv * xv, axis=-1, keepdims=True) + EPS)
        xh = xv * r1
        dh = (_nn(dp_ref[:, 0:COL_U], w_ref[0:ROW_LR, :]) + _nn(dp_ref[:, COL_U:COL_LR], w_ref[ROW_UV:PROJ_W, :])
              + _nn(dp_ref[:, COL_LR:PROJ_WP], w_ref[ROW_LR : ROW_LR + LANES, :]))
        dg1_ref[...] += jnp.sum(dh * xh, axis=0, keepdims=True)
        dx_ref[...] = dx1_ref[...] + _rms_bwd(dh * g1_ref[...], xh, r1)

    row = lambda w: pl.BlockSpec((tm, w), lambda i: (i, 0))
    return pl.pallas_call(
        body,
        name="inproj_dx",
        grid=(seq // tm,),
        in_specs=[row(D_MODEL), row(D_MODEL), _resident((1, D_MODEL)), _resident((PROJ_W, D_MODEL)), row(PROJ_WP), _resident(TOKEN_SHAPE)],
        out_specs=[row(D_MODEL), pl.BlockSpec((1, D_MODEL), lambda i: (0, 0))],
        out_shape=[jax.ShapeDtypeStruct((seq, D_MODEL), F32), jax.ShapeDtypeStruct((1, D_MODEL), F32)],
        compiler_params=_params(48),
    )(x, dx1, g1, w_in_t, dp, token)


def _in_hbm(a):
    return pltpu.with_memory_space_constraint(a, pltpu.HBM)


def _row_tile(rows, multiple=8, limit=512):
    for t in range(min(rows, limit), 0, -1):
        if rows % t == 0 and t % multiple == 0:
            return t
    return rows


def _cast_into_slot(w, shard, token):
    rows, cols = w.shape
    tr = _row_tile(rows, 16)

    def body(s_ref, w_ref, token_ref, o_ref):
        o_ref[...] = w_ref[...].astype(BF16)

    return pl.pallas_call(
        body,
        name="cast_into_slot",
        grid_spec=pltpu.PrefetchScalarGridSpec(
            num_scalar_prefetch=1,
            grid=(rows // tr,),
            in_specs=[pl.BlockSpec((tr, cols), lambda i, s_ref: (i, 0)), pl.BlockSpec(TOKEN_SHAPE, lambda i, s_ref: (0, 0))],
            out_specs=pl.BlockSpec((None, tr, cols), lambda i, s_ref: (s_ref[0], i, 0)),
        ),
        out_shape=pltpu.HBM((N_SHARDS, rows, cols), BF16),
        compiler_params=_params(32, ("parallel",)),
    )(shard, _in_hbm(w), token)


def _add_halves(grads4, recvs, shard_core):
    n = len(grads4)
    _, rows, _ = grads4[0].shape
    tr = _row_tile(rows, 16)

    def body(sc_ref, *refs):
        for k in range(n):
            total = refs[k][...] + refs[n + k][...]
            refs[3 * n + k][...] = total.astype(BF16)

            @pl.when(pl.program_id(1) == sc_ref[0])
            def _(k=k, total=total):
                refs[2 * n + k][...] = total

    theirs = pl.BlockSpec((None, tr, HALF), lambda i, s, sc_ref: (s, i, 0))
    mine = pl.BlockSpec((None, tr, HALF), lambda i, s, sc_ref: (s, i, sc_ref[1]))
    kept = pl.BlockSpec((tr, HALF), lambda i, s, sc_ref: (i, 0))
    outs = pl.pallas_call(
        body,
        name="add_halves",
        grid_spec=pltpu.PrefetchScalarGridSpec(
            num_scalar_prefetch=1,
            grid=(rows // tr, N_SHARDS),
            in_specs=[mine] * n + [theirs] * n,
            out_specs=[kept] * n + [theirs] * n,
        ),
        out_shape=[pltpu.HBM((rows, HALF), F32)] * n + [pltpu.HBM((N_SHARDS, rows, HALF), BF16)] * n,
        compiler_params=_params(48, ("parallel", "arbitrary")),
    )(shard_core, *[_in_hbm(a) for a in list(grads4) + list(recvs)])
    return list(zip(outs[:n], outs[n:]))


def _add_partials(part, recv3, shard_core, token):
    rows, _ = part.shape
    tr = _row_tile(rows, 16)

    def body(sc_ref, p_ref, r_ref, token_ref, o_ref):
        o_ref[...] = ((p_ref[...] + r_ref[0].astype(F32)) + r_ref[1].astype(F32)) + r_ref[2].astype(F32)

    return pl.pallas_call(
        body,
        name="add_partials",
        grid_spec=pltpu.PrefetchScalarGridSpec(
            num_scalar_prefetch=1,
            grid=(rows // tr,),
            in_specs=[
                pl.BlockSpec((tr, HALF), lambda i, sc_ref: (i, 0)),
                pl.BlockSpec((3, tr, HALF), lambda i, sc_ref: (0, i, 0)),
                pl.BlockSpec(TOKEN_SHAPE, lambda i, sc_ref: (0, 0)),
            ],
            out_specs=pl.BlockSpec((tr, HALF), lambda i, sc_ref: (i, sc_ref[1])),
        ),
        out_shape=pltpu.HBM((rows, 2 * HALF), F32),
        compiler_params=_params(32, ("parallel",)),
    )(shard_core, _in_hbm(part), _in_hbm(recv3), token)


def _adam_math(w, g, m, v):
    m = ADAM_B1 * m + (1.0 - ADAM_B1) * g
    v = ADAM_B2 * v + (1.0 - ADAM_B2) * (g * g)
    m_hat = m / (1.0 - ADAM_B1**ADAM_STEP)
    v_hat = v / (1.0 - ADAM_B2**ADAM_STEP)
    delta = -ADAM_LR * (m_hat / (jnp.sqrt(v_hat) + ADAM_EPS) + ADAM_WD * w)
    return delta, m, v


def _adamw(w, g, m, v):
    rows, cols = w.shape
    tr = _row_tile(rows, limit=rows // 3)
    steps, n_in, n_out = rows // tr, 3, 2

    def body(w_hbm, g_hbm, m_hbm, v_hbm, go_hbm, d_hbm, mo_hbm, vo_hbm, in_buf, out_buf, in_sem, out_sem):
        ins, outs = (w_hbm, g_hbm, m_hbm, v_hbm), (go_hbm, d_hbm, mo_hbm, vo_hbm)
        block = lambda i: pl.ds(i * tr, tr)
        load = lambda i, k: pltpu.make_async_copy(ins[k].at[block(i)], in_buf.at[k, i % n_in], in_sem.at[k, i % n_in])
        store = lambda i, k: pltpu.make_async_copy(out_buf.at[k, i % n_out], outs[k].at[block(i)], out_sem.at[k, i % n_out])
        for i in range(min(n_in, steps)):
            for k in range(4):
                load(i, k).start()
        for i in range(steps):
            for k in range(4):
                load(i, k).wait()
                if i >= n_out:
                    store(i - n_out, k).wait()
            s, o = i % n_in, i % n_out
            gv = in_buf[1, s]
            out_buf[0, o] = gv
            out_buf[1, o], out_buf[2, o], out_buf[3, o] = _adam_math(in_buf[0, s], gv, in_buf[2, s], in_buf[3, s])
            for k in range(4):
                store(i, k).start()
                if i + n_in < steps:
                    load(i + n_in, k).start()
        for i in range(max(steps - n_out, 0), steps):
            for k in range(4):
                store(i, k).wait()

    return pl.pallas_call(
        body,
        name="adamw",
        in_specs=[ANY] * 4,
        out_specs=[ANY] * 4,
        out_shape=[pltpu.HBM(w.shape, F32)] * 4,
        scratch_shapes=[pltpu.VMEM((4, n_in, tr, cols), F32), pltpu.VMEM((4, n_out, tr, cols), F32),
                        pltpu.SemaphoreType.DMA((4, n_in)), pltpu.SemaphoreType.DMA((4, n_out))],
        compiler_params=_params(32, None),
    )(_in_hbm(w), _in_hbm(g), _in_hbm(m), _in_hbm(v))


SMALL_ROWS = 560
DECAY_ROWS = 8
SMALL_TOTAL = SMALL_ROWS + 2 * N_SHARDS * DECAY_ROWS


def _adamw_small(gathered, own, wp, mp, vp, like):
    out_rows = SMALL_ROWS + 2 * DECAY_ROWS
    places, off = [], 0
    for a in like:
        rows = a.size // LANES
        kept = a.shape[-1] == LANES
        places.append((off, rows, kept, (rows, LANES) if kept else (1, a.size)))
        off += rows
    loss_row = off
    decay_shape = (LOWRANK, KEY_W // N_SHARDS)
    n = len(places) + 2

    def body(ga_ref, own_ref, w_ref, m_ref, v_ref, *refs):
        outs, loss_ref, packed = refs[: 4 * n], refs[4 * n], refs[4 * n + 1 :]
        g_sc = packed[0]
        x, y, c = _position()
        shard, me = 2 * x + y, 4 * x + 2 * y + c
        total = lambda rows: functools.reduce(lambda a, b: a + b, [jnp.where(me == d, own_ref[rows, :], ga_ref[d, rows, :]) for d in range(8)])
        g_sc[pl.ds(0, SMALL_ROWS), :] = total(pl.ds(0, SMALL_ROWS))
        for k in range(2):
            start = pl.multiple_of(SMALL_ROWS + k * N_SHARDS * DECAY_ROWS + shard * DECAY_ROWS, DECAY_ROWS)
            g_sc[pl.ds(SMALL_ROWS + k * DECAY_ROWS, DECAY_ROWS), :] = total(pl.ds(start, DECAY_ROWS))
        packed[1][...], packed[2][...], packed[3][...] = _adam_math(w_ref[...], g_sc[...], m_ref[...], v_ref[...])
        loss_ref[...] = g_sc[loss_row : loss_row + 1, :]
        for t, res in enumerate(packed):
            for (at, rows, kept, _), out in zip(places, outs[t * n :]):
                if kept:
                    out[...] = res[at : at + rows, :]
                else:
                    for r in range(rows):
                        out[:, r * LANES : (r + 1) * LANES] = res[at + r : at + r + 1, :]
            for k in range(2):
                out = outs[t * n + len(places) + k]
                both = res[SMALL_ROWS + k * DECAY_ROWS : SMALL_ROWS + (k + 1) * DECAY_ROWS, :]
                halves = (both, pltpu.roll(both, LANES // 2, axis=1))
                for r in range(DECAY_ROWS):
                    for h in range(2):
                        out[2 * r + h : 2 * r + h + 1, :] = halves[h][r : r + 1, 0 : LANES // 2]

    shapes = [jax.ShapeDtypeStruct(s, F32) for *_, s in places] + [jax.ShapeDtypeStruct(decay_shape, F32)] * 2
    out = pl.pallas_call(
        body,
        name="adamw_small",
        out_shape=shapes * 4 + [jax.ShapeDtypeStruct((1, LANES), F32)],
        scratch_shapes=[pltpu.VMEM((out_rows, LANES), F32)] * 4,
        compiler_params=_params(32, None),
    )(gathered, own, wp, mp, vp)
    return [list(out[t * n : (t + 1) * n]) for t in range(4)], out[4 * n]


ANY = pl.BlockSpec(memory_space=pl.ANY)


def _position():
    return lax.axis_index("x"), lax.axis_index("y"), lax.axis_index("c")


def _other_chips(x, y):
    return [(1 - x, y), (x, 1 - y), (1 - x, 1 - y)]


HBM = pl.BlockSpec(memory_space=pltpu.HBM)
SEM = pl.BlockSpec(memory_space=pltpu.SEMAPHORE)
TOKEN = jax.ShapeDtypeStruct(TOKEN_SHAPE, F32)
DATAFLOW = pltpu.SideEffectType.DATAFLOW_SIDE_EFFECTING


def _half_block(ref4, slot, core):
    return ref4.at[slot, :, pl.ds(pl.multiple_of(core * HALF, HALF), HALF)]


def _gather_ici_copies(bufs, lands, send_sems, recv_sems):
    x, y, c = _position()
    pairs = []
    for k, ref4 in enumerate(bufs):
        mine = _half_block(ref4, 2 * x + y, c)
        for j, (px, py) in enumerate(_other_chips(x, y)):
            sems = dict(send_sem=send_sems.at[3 * k + j], recv_sem=recv_sems.at[3 * k + j], device_id=(px, py, c), device_id_type=MESH)
            pairs.append((functools.partial(pltpu.make_async_remote_copy, src_ref=mine, dst_ref=mine, **sems),
                          functools.partial(pltpu.make_async_remote_copy, src_ref=mine, dst_ref=_half_block(ref4, 2 * px + py, c), **sems)))
    return pairs


def _gather_d2d_copies(bufs, lands, send_sems, recv_sems, first=0):
    x, y, c = _position()
    pairs = []
    for k, ref4 in enumerate(bufs):
        for j, (px, py) in enumerate(_other_chips(x, y)):
            have = _half_block(ref4, 2 * px + py, c)
            at = first + 3 * k + j
            sems = dict(send_sem=send_sems.at[at], recv_sem=recv_sems.at[at], device_id=(x, y, 1 - c), device_id_type=MESH)
            pairs.append((functools.partial(pltpu.make_async_remote_copy, src_ref=have, dst_ref=have, **sems),
                          functools.partial(pltpu.make_async_remote_copy, src_ref=have, dst_ref=_half_block(ref4, 2 * px + py, 1 - c), **sems)))
    return pairs


def _gather_forward(bufs):
    n = len(bufs)

    def body(*refs):
        outs = refs[n : 2 * n]
        send_sems, recv_sems = refs[2 * n :]
        d2d = _gather_d2d_copies(outs, (), send_sems, recv_sems)
        for forward, _ in d2d:
            forward().start()
        for forward, arrival in d2d:
            arrival().wait_recv()
            forward().wait_send()

    return pl.pallas_call(
        body,
        name="gather_forward",
        in_specs=[ANY] * n,
        out_specs=[ANY] * n,
        out_shape=[jax.ShapeDtypeStruct(b.shape, b.dtype) for b in bufs],
        input_output_aliases={k: k for k in range(n)},
        scratch_shapes=[pltpu.SemaphoreType.DMA((3 * n,)), pltpu.SemaphoreType.DMA((3 * n,))],
        compiler_params=pltpu.CompilerParams(has_side_effects=True),
    )(*bufs)


def _both_ends(**copy):
    maker = functools.partial(pltpu.make_async_remote_copy, **copy)
    return maker, maker


def _scatter_copies(parts, lands, send_sems, recv_sems):
    x, y, c = _position()
    return [_both_ends(src_ref=parts[k].at[2 * px + py], dst_ref=lands[k].at[j], send_sem=send_sems.at[3 * k + j],
                       recv_sem=recv_sems.at[3 * k + j], device_id=(px, py, c), device_id_type=MESH)
            for k in range(len(parts)) for j, (px, py) in enumerate(_other_chips(x, y))]


def _exchange_copies(grads, lands, send_sems, recv_sems):
    x, y, c = _position()
    return [_both_ends(src_ref=grads[k].at[:, :, pl.ds(pl.multiple_of((1 - c) * HALF, HALF), HALF)], dst_ref=lands[k],
                       send_sem=send_sems.at[k], recv_sem=recv_sems.at[k], device_id=(x, y, 1 - c), device_id_type=MESH)
            for k in range(len(grads))]


def _exchange_lands(grads4):
    return [jax.ShapeDtypeStruct((N_SHARDS, g.shape[1], HALF), g.dtype) for g in grads4]


def _scatter_lands(parts4):
    return [jax.ShapeDtypeStruct((3,) + g.shape[1:], g.dtype) for g in parts4]


def _small_gather_copies(blocks, lands, send_sems, recv_sems):
    x, y, c = _position()
    flip = lambda v, bit: 1 - v if bit else v
    return [_both_ends(src_ref=blocks[0], dst_ref=lands[0].at[4 * x + 2 * y + c], send_sem=send_sems.at[r - 1],
                       recv_sem=recv_sems.at[r - 1], device_id=(flip(x, r & 4), flip(y, r & 2), flip(c, r & 1)), device_id_type=MESH)
            for r in range(1, 8)]


def _split_start(name, srcs, land_shapes, make_copies, nsem, after=()):
    n, nl, na = len(srcs), len(land_shapes), len(after)
    lands = [lax.empty(a.shape, a.dtype) for a in land_shapes]

    def body(*refs):
        send_sems, recv_sems = refs[n + nl + na], refs[n + nl + na + 1]
        token = refs[2 * (n + nl) + na + 2]
        for send, _ in make_copies(refs[:n], refs[n : n + nl], send_sems, recv_sems):
            send().start()
        token[...] = jnp.zeros_like(token)

    hbm = lambda a: pltpu.HBM(a.shape, a.dtype)
    out = pl.pallas_call(
        body,
        name=name,
        in_specs=[HBM] * (n + nl) + [ANY] * na,
        out_specs=(SEM, SEM, *[HBM] * (n + nl), pl.BlockSpec(memory_space=pltpu.VMEM)),
        out_shape=(pltpu.SemaphoreType.DMA((nsem,)), pltpu.SemaphoreType.DMA((nsem,)), *[hbm(a) for a in list(srcs) + lands], TOKEN),
        input_output_aliases={k: 2 + k for k in range(n + nl)},
        compiler_params=pltpu.CompilerParams(has_side_effects=DATAFLOW),
    )(*[pltpu.with_memory_space_constraint(a, pltpu.HBM) for a in list(srcs) + lands], *after)
    return out[0], out[1], list(out[2 : 2 + n]), list(out[2 + n : 2 + n + nl]), out[2 + n + nl]


def _split_wait(name, send_sems, recv_sems, srcs, lands, make_copies, after):
    n, nl = len(srcs), len(lands)

    def body(*refs):
        for send, arrival in make_copies(refs[:n], refs[n : n + nl], refs[n + nl], refs[n + nl + 1]):
            send().wait_send()
            arrival().wait_recv()

    hbm = lambda a: pltpu.HBM(a.shape, a.dtype)
    out = pl.pallas_call(
        body,
        name=name,
        in_specs=[HBM] * (n + nl) + [SEM, SEM] + [ANY] * len(after),
        out_specs=tuple([HBM] * (n + nl)),
        out_shape=tuple(hbm(a) for a in list(srcs) + list(lands)),
        input_output_aliases={k: k for k in range(n + nl)},
        compiler_params=pltpu.CompilerParams(has_side_effects=DATAFLOW),
    )(*srcs, *lands, send_sems, recv_sems, *after)
    return list(out[:n]), list(out[n:])


def _join_copies(bufs, lands, send_sems, recv_sems, first=0):
    x, y, c = _position()
    half = lambda ref, core: ref.at[:, pl.ds(pl.multiple_of(core * HALF, HALF), HALF)]
    pairs = []
    for k, ref in enumerate(bufs):
        sems = dict(send_sem=send_sems.at[first + k], recv_sem=recv_sems.at[first + k], device_id=(x, y, 1 - c), device_id_type=MESH)
        pairs.append((functools.partial(pltpu.make_async_remote_copy, src_ref=half(ref, c), dst_ref=half(ref, c), **sems),
                      functools.partial(pltpu.make_async_remote_copy, src_ref=half(ref, c), dst_ref=half(ref, 1 - c), **sems)))
    return pairs


def _allgather_small(block):
    m_per, ncol = block.shape

    def body(x_ref, out_ref, send_sems, recv_sems, local_sem):
        x, y, c = _position()
        me, sibling = (x, y, c), (x, y, 1 - c)
        chips = _other_chips(x, y)

        def rows(px, py, pc):
            return out_ref.at[4 * px + 2 * py + pc]

        def copy(k, blk, to, src=None):
            return pltpu.make_async_remote_copy(
                src_ref=rows(*blk) if src is None else src, dst_ref=rows(*blk),
                send_sem=send_sems.at[k], recv_sem=recv_sems.at[k], device_id=to, device_id_type=MESH)

        mine = pltpu.make_async_copy(x_ref, rows(*me), local_sem)
        mine.start()
        first = [copy(0, me, sibling, src=x_ref)] + [copy(1 + j, me, (*chip, c), src=x_ref) for j, chip in enumerate(chips)]
        for cp in first:
            cp.start()
        passed = [copy(4 + j, (*chip, c), sibling) for j, chip in enumerate(chips)]
        for j, chip in enumerate(chips):
            copy(1 + j, (*chip, c), me).wait_recv()
            passed[j].start()
        copy(0, sibling, me).wait_recv()
        for j, chip in enumerate(chips):
            copy(4 + j, (*chip, 1 - c), me).wait_recv()
        for cp in first + passed:
            cp.wait_send()
        mine.wait()

    return pl.pallas_call(
        body,
        name="allgather_small",
        in_specs=[pl.BlockSpec(memory_space=pltpu.VMEM)],
        out_specs=pl.BlockSpec(memory_space=pltpu.VMEM),
        out_shape=jax.ShapeDtypeStruct((8, m_per, ncol), block.dtype),
        scratch_shapes=[pltpu.SemaphoreType.DMA((7,)), pltpu.SemaphoreType.DMA((7,)), pltpu.SemaphoreType.DMA],
        compiler_params=pltpu.CompilerParams(has_side_effects=True, vmem_limit_bytes=32 * MIB),
    )(block)


SMALL_NAMES = ["norm1_g", "b_decay_f", "b_decay_b", "gla_norm_g", "gmlp_ln_g", "gmlp_ln_b", "w_spatial", "b_spatial", "norm2_g", "final_norm_g"]


def _pack_small(parts, decay_parts):
    flat = jnp.concatenate([a.reshape(-1) for a in parts])
    flat = jnp.pad(flat, (0, SMALL_ROWS * LANES - flat.shape[0])).reshape(SMALL_ROWS, LANES)
    return jnp.concatenate([flat] + [d.reshape(-1, LANES) for d in decay_parts], axis=0)


def kernel(x, norm1_g, w_in, w_decay_f, b_decay_f, w_decay_b, b_decay_b, gla_norm_g, gmlp_ln_g, gmlp_ln_b, w_spatial, b_spatial, w_out, norm2_g, w_gate, w_up, w_down, final_norm_g, loss_target, m_norm1_g, m_w_in, m_w_decay_f, m_b_decay_f, m_w_decay_b, m_b_decay_b, m_gla_norm_g, m_gmlp_ln_g, m_gmlp_ln_b, m_w_spatial, m_b_spatial, m_w_out, m_norm2_g, m_w_gate, m_w_up, m_w_down, m_final_norm_g, v_norm1_g, v_w_in, v_w_decay_f, v_b_decay_f, v_w_decay_b, v_b_decay_b, v_gla_norm_g, v_gmlp_ln_g, v_gmlp_ln_b, v_w_spatial, v_b_spatial, v_w_out, v_norm2_g, v_w_gate, v_w_up, v_w_down, v_final_norm_g):
    args = dict(locals())
    cx, cy, cc = lax.axis_index("x"), lax.axis_index("y"), lax.axis_index("c")
    shard = 2 * cx + cy
    xs = x[0]
    target = loss_target[0]

    big_names = ["w_in", "w_out", "w_gate", "w_up", "w_down"]
    transposed = ("w_in", "w_gate", "w_up")
    rows_of = lambda pre, k: jnp.transpose(args[pre + k][0]) if k in transposed else args[pre + k][0]
    big_shards = {k: rows_of("", k) for k in big_names}
    s_arr = shard.reshape(1).astype(jnp.int32)
    sc_arr = jnp.stack([shard, cc]).astype(jnp.int32)
    zero_token = jnp.zeros(TOKEN_SHAPE, F32)
    w_send, w_recv, (w_in4,), _, token_w_in = _split_start(
        "w_in_gather_start", [_cast_into_slot(big_shards["w_in"], s_arr, zero_token)], [], _gather_ici_copies, 3)
    late = ["w_out", "w_gate", "w_up", "w_down"]
    late_slots = [_cast_into_slot(big_shards[k], s_arr, token_w_in) for k in late]
    dec_block = jnp.concatenate([w_decay_f[0].reshape(-1, LANES), w_decay_b[0].reshape(-1, LANES)], axis=0)
    dec_all = _allgather_small(dec_block)
    (w_in4,), _ = _split_wait("w_in_gather_wait", w_send, w_recv, [w_in4], [], _gather_ici_copies, (dec_all, *late_slots))
    (w_in4,) = _gather_forward([w_in4])
    w_in_t = w_in4.reshape(PROJ_W, D_MODEL)
    g_send, g_recv, late_bufs, _, token_gather = _split_start(
        "gather_start", late_slots, [], _gather_ici_copies, 3 * len(late), after=(w_in4,))
    dec_all = dec_all[::2].reshape(N_SHARDS, 2, LOWRANK, KEY_W // N_SHARDS)
    wdf_full = jnp.transpose(dec_all[:, 0], (1, 0, 2)).reshape(LOWRANK, KEY_W)
    wdb_full = jnp.transpose(dec_all[:, 1], (1, 0, 2)).reshape(LOWRANK, KEY_W)
    wd_pad_f = jnp.zeros((LANES, KEY_W), F32).at[0:LOWRANK].set(wdf_full).astype(BF16)
    wd_pad_b = jnp.zeros((LANES, KEY_W), F32).at[LOWRANK : 2 * LOWRANK].set(wdb_full).astype(BF16)

    ws_bf = w_spatial[0].astype(BF16)
    wst_bf = jnp.transpose(w_spatial[0], (0, 2, 1)).astype(BF16)
    bs_col = b_spatial[0].reshape(GMLP_GROUPS, GMLP_CHUNK, 1)

    p = _inproj(xs, norm1_g, w_in_t, token_gather)
    o_f, st_f = _gla_fwd(p, wd_pad_f, b_decay_f, token_gather, reverse=False)
    o_b, st_b = _gla_fwd(p, wd_pad_b, b_decay_b, token_gather, reverse=True)
    late_bufs, _ = _split_wait("gather_wait", g_send, g_recv, late_bufs, [], _gather_ici_copies, (o_f, o_b))
    f_send, f_recv, late_bufs, _, token_forward = _split_start("forward_start", late_bufs, [], _gather_d2d_copies, 3 * len(late))
    (w_out4,), _ = _split_wait("w_out_forward_wait", f_send, f_recv, late_bufs[:1], [], _gather_d2d_copies, (token_forward,))
    w_out_full = w_out4.reshape(-1, D_MODEL)
    x1, ycat = _mixer_out(xs, o_f, o_b, p, gla_norm_g, gmlp_ln_g, gmlp_ln_b, ws_bf, bs_col, w_out_full, token_forward)
    ffn_bufs, _ = _split_wait(
        "forward_wait", f_send, f_recv, late_bufs[1:], [], functools.partial(_gather_d2d_copies, first=3), (x1,))
    wg_t, wu_t, wd = [b.reshape(-1, D_MODEL) for b in ffn_bufs]
    gf = final_norm_g.reshape(1, D_MODEL)
    h2, gate, up, act, dx2, loss_acc, dgf = _ffn_fwd(x1, target, norm2_g, gf, wg_t, wu_t, wd)

    dgate, dup, dx1, dg2 = _ffn_bwd(dx2, gate, up, x1, norm2_g, wg_t, wu_t, wd)
    ffn_grads4 = [g.reshape(N_SHARDS, FF_SHARD, D_MODEL) for g in _ffn_wgrad(h2, dgate, dup, act, dx2)]
    e_send, e_recv, e_srcs, e_lands, token_exchange = _split_start(
        "exchange_start", ffn_grads4, _exchange_lands(ffn_grads4), _exchange_copies, len(ffn_grads4))
    do, dg, du, dvv, dwo, dgn, dlng, dlnb, dws, dbs = _mixer_bwd(
        dx1, ycat, o_f, o_b, p, gla_norm_g, gmlp_ln_g, gmlp_ln_b, ws_bf, wst_bf, bs_col, w_out_full, token_exchange)
    ffn_mine, ffn_other = _split_wait("exchange_wait", e_send, e_recv, e_srcs, e_lands, _exchange_copies, (do,))
    ffn_parts = _add_halves(ffn_mine, ffn_other, sc_arr)
    ffn_payload = [pb for _, pb in ffn_parts]
    s_send, s_recv, s_parts, s_lands, token_scatter = _split_start(
        "scatter_start", ffn_payload, _scatter_lands(ffn_payload), _scatter_copies, 3 * len(ffn_payload))
    dq_f, dk_f, dv_f, dlr_f, dwdec_f, dbdec_f = _gla_bwd(p, do, st_f, wd_pad_f, b_decay_f, token_scatter, reverse=False)
    dq, dk, dv, dlr, dwdec_b, dbdec_b = _gla_bwd(
        p, do, st_b, wd_pad_b, b_decay_b, token_scatter, reverse=True, other=(dq_f, dk_f, dv_f, dlr_f))
    dwin_t, dp = _inproj_wgrad(xs, norm1_g, dq, dk, dv, dg, du, dvv, dlr)
    _, ffn_recv = _split_wait("scatter_wait", s_send, s_recv, s_parts, s_lands, _scatter_copies, (dwin_t,))

    dwin4 = dwin_t.reshape(N_SHARDS, PROJ_W // N_SHARDS, D_MODEL)
    dwo4 = dwo.reshape(N_SHARDS, D_MODEL // N_SHARDS, D_MODEL)
    proj_grads4 = [dwin4, dwo4]
    x_send, x_recv, x_srcs, x_lands, token_swap = _split_start(
        "proj_exchange_start", proj_grads4, _exchange_lands(proj_grads4), _exchange_copies, len(proj_grads4))
    ffn_bufs = [_add_partials(pf, r, sc_arr, token_swap) for (pf, _), r in zip(ffn_parts, ffn_recv)]
    proj_mine, proj_other = _split_wait("proj_exchange_wait", x_send, x_recv, x_srcs, x_lands, _exchange_copies, tuple(ffn_bufs))
    proj_parts = [_add_halves([g], [r], sc_arr)[0] for g, r in zip(proj_mine, proj_other)]
    proj_payload = [pb for _, pb in proj_parts]
    n_proj = len(proj_payload)
    ffn_join_copies = functools.partial(_join_copies, first=3 * n_proj)
    scatter_and_join = lambda srcs, lands, send_sems, recv_sems: (
        _scatter_copies(srcs[:n_proj], lands, send_sems, recv_sems) + ffn_join_copies(srcs[n_proj:], (), send_sems, recv_sems))
    p_send, p_recv, started, p_lands, token_join = _split_start(
        "proj_scatter_start", proj_payload + ffn_bufs, _scatter_lands(proj_payload), scatter_and_join, 3 * n_proj + len(ffn_bufs))
    p_parts, ffn_bufs = started[:n_proj], started[n_proj:]
    dx, dg1 = _inproj_dx(xs, dx1, norm1_g, w_in_t, dp, token_join)
    _, proj_recv = _split_wait("proj_scatter_wait", p_send, p_recv, p_parts, p_lands, _scatter_copies, (dx,))

    dwdec_f16 = dwdec_f[0:LOWRANK]
    dwdec_b16 = dwdec_b[LOWRANK : 2 * LOWRANK]
    shard_major = lambda a: jnp.transpose(a.reshape(LOWRANK, N_SHARDS, KEY_W // N_SHARDS), (1, 0, 2))
    small_grads = {
        "norm1_g": dg1, "b_decay_f": dbdec_f, "b_decay_b": dbdec_b, "gla_norm_g": dgn, "gmlp_ln_g": dlng, "gmlp_ln_b": dlnb,
        "w_spatial": dws, "b_spatial": dbs, "norm2_g": dg2, "final_norm_g": dgf,
    }
    g_pack = _pack_small([small_grads[k] for k in SMALL_NAMES] + [loss_acc], [shard_major(dwdec_f16), shard_major(dwdec_b16)])
    proj_bufs = [_add_partials(pf, r, sc_arr, token_join) for (pf, _), r in zip(proj_parts, proj_recv)]
    proj_join_copies = functools.partial(_join_copies, first=7)
    tail_copies = lambda srcs, lands, send_sems, recv_sems: (
        _small_gather_copies(srcs[:1], lands, send_sems, recv_sems) + proj_join_copies(srcs[1:], (), send_sems, recv_sems))
    t_send, t_recv, (g_pack, *proj_bufs), g_lands, token_tail = _split_start(
        "tail_start", [g_pack] + proj_bufs, [jax.ShapeDtypeStruct((8, SMALL_TOTAL, LANES), F32)], tail_copies, 7 + len(proj_bufs))

    ffn_bufs, _ = _split_wait("join_wait", p_send, p_recv, ffn_bufs, [], ffn_join_copies, (dx, token_tail))
    adamw = lambda k, g: _adamw(big_shards[k], g, rows_of("m_", k), rows_of("v_", k))
    big_updates = {k: adamw(k, g) for k, g in zip(big_names[2:], ffn_bufs)}
    proj_bufs, _ = _split_wait(
        "proj_join_wait", t_send, t_recv, proj_bufs, [], proj_join_copies, tuple(u[1] for u in big_updates.values()))
    big_updates.update({k: adamw(k, g) for k, g in zip(big_names[:2], proj_bufs)})

    (g_pack,), (g_all,) = _split_wait(
        "small_gather_wait", t_send, t_recv, [g_pack], g_lands, _small_gather_copies, tuple(u[1] for u in big_updates.values()))
    pack_own = lambda pre: _pack_small([args[pre + k] for k in SMALL_NAMES], [args[pre + "w_decay_f"], args[pre + "w_decay_b"]])
    small_updates, loss_row = _adamw_small(g_all, g_pack, pack_own(""), pack_own("m_"), pack_own("v_"), [args[k] for k in SMALL_NAMES])

    names = ["norm1_g", "w_in", "w_decay_f", "b_decay_f", "w_decay_b", "b_decay_b", "gla_norm_g", "gmlp_ln_g", "gmlp_ln_b",
             "w_spatial", "b_spatial", "w_out", "norm2_g", "w_gate", "w_up", "w_down", "final_norm_g"]
    results = {"g": {}, "d": {}, "m": {}, "v": {}}
    for tag, arrays in zip("gdmv", small_updates):
        for k, a in zip(SMALL_NAMES + ["w_decay_f", "w_decay_b"], arrays):
            results[tag][k] = a.reshape(args[k].shape)
    for k in big_names:
        for tag, a in zip("gdmv", big_updates[k]):
            results[tag][k] = (jnp.transpose(a) if k in transposed else a).reshape(args[k].shape)

    loss = loss_row[0, 0]
    grad_x = dx.reshape(x.shape)
    return (loss, grad_x, *[results["g"][k] for k in names], *[results["d"][k] for k in names],
            *[results["m"][k] for k in names], *[results["v"][k] for k in names])
```

```python
import functools
import math

import jax
import jax.numpy as jnp
from jax import lax
from jax.experimental import pallas as pl
from jax.experimental.pallas import tpu as pltpu

F32, BF16 = jnp.float32, jnp.bfloat16

D_MODEL = 1024
GLA_HEADS = 4
GLA_DK = 64
GLA_DV = 128
KEY_W = GLA_HEADS * GLA_DK
GLA_W = GLA_HEADS * GLA_DV
GMLP_W = 512
GMLP_GROUPS = 4
GMLP_CHUNK = 128
LOWRANK = 16
GLA_CHUNK = 64
GLA_TAU = 16.0
PROJ_W = 2592
PROJ_WP = 2688
D_FF = 2816
N_SHARDS = 4
FF_SHARD = D_FF // N_SHARDS
EPS = 1e-6
LANES = 128
TOKEN_SHAPE = (8, LANES)
MIB = 1024 * 1024

ADAM_LR = 0.001
ADAM_B1 = 0.9
ADAM_B2 = 0.999
ADAM_EPS = 1e-08
ADAM_WD = 0.01
ADAM_STEP = 10

COL_Q, COL_K = 0, 256
COL_V, COL_G, COL_U, COL_VV = 512, 1024, 1536, 2048
COL_LR = 2560
ROW_LR, ROW_UV = 1536, 1568
HALF = D_MODEL // 2

MESH = pl.DeviceIdType.MESH


def _nn(a, b):
    return jnp.dot(a, b, preferred_element_type=F32)


def _nt(a, b):
    return lax.dot_general(a, b, (((1,), (1,)), ((), ())), preferred_element_type=F32)


def _tn(a, b):
    return lax.dot_general(a, b, (((0,), (0,)), ((), ())), preferred_element_type=F32)


def _bnn(a, b):
    return jnp.einsum("nik,nkj->nij", a, b, preferred_element_type=F32)


def _bnt(a, b):
    return jnp.einsum("nik,njk->nij", a, b, preferred_element_type=F32)


def _btn(a, b):
    return jnp.einsum("nki,nkj->nij", a, b, preferred_element_type=F32)


def _resident(shape):
    zeros = (0,) * len(shape)
    return pl.BlockSpec(shape, lambda *_: zeros, pipeline_mode=pl.Buffered(1))


def _params(vmem_mib, semantics=("arbitrary",)):
    return pltpu.CompilerParams(vmem_limit_bytes=vmem_mib * MIB, dimension_semantics=semantics)


def _sigmoid(x):
    return 1.0 / (1.0 + jnp.exp(-x))


def _gelu(x):
    return 0.5 * x * (1.0 + lax.erf(x * (1.0 / math.sqrt(2.0))))


def _gelu_and_grad(x):
    cdf = 0.5 * (1.0 + lax.erf(x * (1.0 / math.sqrt(2.0))))
    return x * cdf, cdf + x * jnp.exp(-0.5 * x * x) * (1.0 / math.sqrt(2.0 * math.pi))


def _log_sigmoid(x):
    return jnp.minimum(x, 0.0) - jnp.log(1.0 + jnp.exp(-jnp.abs(x)))


def _rms_bwd(dxh, xh, r):
    return r * (dxh - xh * jnp.mean(dxh * xh, axis=-1, keepdims=True))


def _chunk_cumsum(v, row_in_chunk, reverse):
    rows = v.shape[0]
    for sh in (1, 2, 4, 8, 16, 32):
        if reverse:
            v = v + jnp.where(row_in_chunk + sh < GLA_CHUNK, pltpu.roll(v, rows - sh, axis=0), 0.0)
        else:
            v = v + jnp.where(row_in_chunk >= sh, pltpu.roll(v, sh, axis=0), 0.0)
    return v


def _inproj(x, g1, w_in_t, token):
    seq = x.shape[0]
    tm = min(seq, 512)

    def body(x_ref, g_ref, w_ref, token_ref, p_ref):
        xv = x_ref[...]
        r = lax.rsqrt(jnp.mean(xv * xv, axis=-1, keepdims=True) + EPS)
        h = (xv * r * g_ref[...]).astype(BF16)
        p_ref[:, 0:COL_U] = _nt(h, w_ref[0:ROW_LR, :])
        p_ref[:, COL_U:COL_LR] = _nt(h, w_ref[ROW_UV:PROJ_W, :])
        p_ref[:, COL_LR:PROJ_WP] = _nt(h, w_ref[ROW_LR : ROW_LR + LANES, :])

    return pl.pallas_call(
        body,
        name="inproj",
        grid=(seq // tm,),
        in_specs=[pl.BlockSpec((tm, D_MODEL), lambda i: (i, 0)), _resident((1, D_MODEL)), _resident((PROJ_W, D_MODEL)), _resident(TOKEN_SHAPE)],
        out_specs=pl.BlockSpec((tm, PROJ_WP), lambda i: (i, 0)),
        out_shape=jax.ShapeDtypeStruct((seq, PROJ_WP), F32),
        compiler_params=_params(48, ("parallel",)),
    )(x, g1, w_in_t, token)


def _gla_tile(seq):
    return min(seq, 1024)


def _gla_decay_terms(lr_bf, wd_ref, bd_ref, pair, row_in_chunk, reverse, n):
    cols = pl.ds(pair * LANES, LANES)
    pre = _nn(lr_bf, wd_ref[:, cols]) + bd_ref[:, cols]
    la = _log_sigmoid(pre) * (1.0 / GLA_TAU)
    b = _chunk_cumsum(la, row_in_chunk, reverse)
    b3 = b.reshape(n, GLA_CHUNK, LANES)
    blast = b3[:, 0:1, :] if reverse else b3[:, GLA_CHUNK - 1 : GLA_CHUNK, :]
    return pre, b3, blast


def _gla_fwd(p, wd_pad, bd, token, reverse):
    seq = p.shape[0]
    tg = _gla_tile(seq)
    nt = seq // tg
    n = tg // GLA_CHUNK
    scale = GLA_DK**-0.5

    def tile(i):
        return nt - 1 - i if reverse else i

    def body(q_ref, k_ref, v_ref, lr_ref, wd_ref, bd_ref, token_ref, o_ref, st_ref, carry):
        @pl.when(pl.program_id(0) == 0)
        def _():
            carry[...] = jnp.zeros_like(carry)

        lr_bf = lr_ref[...].astype(BF16)
        states = [carry[h] for h in range(GLA_HEADS)]
        row_in_chunk = lax.broadcasted_iota(jnp.int32, (tg, LANES), 0) % GLA_CHUNK
        lane_head = lax.broadcasted_iota(jnp.int32, (1, LANES), 1) // GLA_DK
        tt = lax.broadcasted_iota(jnp.int32, (GLA_CHUNK, GLA_CHUNK), 0)
        ss = lax.broadcasted_iota(jnp.int32, (GLA_CHUNK, GLA_CHUNK), 1)
        causal = (tt <= ss) if reverse else (tt >= ss)
        order = range(n - 1, -1, -1) if reverse else range(n)
        heads = range(GLA_HEADS)
        qds, vhs, decs, sc_raw, dst = {}, {}, {}, {}, {}
        for pair in range(2):
            cols = pl.ds(pair * LANES, LANES)
            _, b3, blast = _gla_decay_terms(lr_bf, wd_ref, bd_ref, pair, row_in_chunk, reverse, n)
            q3 = q_ref[:, cols].reshape(n, GLA_CHUNK, LANES) * scale
            k3 = k_ref[:, cols].reshape(n, GLA_CHUNK, LANES)
            qd = q3 * jnp.exp(b3)
            kd = (k3 * jnp.exp(-b3)).astype(BF16)
            kte = k3 * jnp.exp(blast - b3)
            decs[pair] = jnp.exp(blast)
            qds[pair] = qd.astype(BF16)
            m0 = (lane_head == 0).astype(F32)
            m1 = (lane_head == 1).astype(F32)
            q_both = jnp.concatenate([(qd * m0).astype(BF16), (qd * m1).astype(BF16)], axis=1)
            sc_both = _bnt(q_both, kd)
            for hh, m in ((0, m0), (1, m1)):
                h = 2 * pair + hh
                vhs[h] = v_ref[:, pl.ds(h * GLA_DV, GLA_DV)].reshape(n, GLA_CHUNK, GLA_DV).astype(BF16)
                sc_raw[h] = sc_both[:, hh * GLA_CHUNK : (hh + 1) * GLA_CHUNK, :]
                dst[h] = _btn(vhs[h], (kte * m).astype(BF16))
        o_intra, befores = {}, {}
        for h in heads:
            o_intra[h] = _bnn(jnp.where(causal, sc_raw[h], 0.0).astype(BF16), vhs[h])
            st, before = states[h], [None] * n
            for j in order:
                before[j] = st
                st = st * decs[h // 2][j] + dst[h][j]
            states[h] = st
            befores[h] = jnp.stack(before).astype(BF16)
        outs = {}
        for pair in range(2):
            both = jnp.concatenate([befores[2 * pair], befores[2 * pair + 1]], axis=1)
            o_inter = _bnt(qds[pair], both)
            for hh in range(2):
                h = 2 * pair + hh
                outs[h] = (o_intra[h] + o_inter[:, :, hh * GLA_DV : (hh + 1) * GLA_DV]).reshape(tg, GLA_DV)
        for h in range(GLA_HEADS):
            o_ref[:, pl.ds(h * GLA_DV, GLA_DV)] = outs[h]
            st_ref[:, h] = befores[h]
            carry[h] = states[h]

    nchunks = seq // GLA_CHUNK
    return pl.pallas_call(
        body,
        name="gla_fwd_rev" if reverse else "gla_fwd",
        grid=(nt,),
        in_specs=[
            pl.BlockSpec((tg, KEY_W), lambda i: (tile(i), COL_Q // KEY_W)),
            pl.BlockSpec((tg, KEY_W), lambda i: (tile(i), COL_K // KEY_W)),
            pl.BlockSpec((tg, GLA_W), lambda i: (tile(i), COL_V // GLA_W)),
            pl.BlockSpec((tg, LANES), lambda i: (tile(i), COL_LR // LANES)),
            _resident((LANES, KEY_W)),
            _resident((1, KEY_W)),
            _resident(TOKEN_SHAPE),
        ],
        out_specs=[
            pl.BlockSpec((tg, GLA_W), lambda i: (tile(i), 0)),
            pl.BlockSpec((n, GLA_HEADS, GLA_DV, LANES), lambda i: (tile(i), 0, 0, 0)),
        ],
        out_shape=[
            jax.ShapeDtypeStruct((seq, GLA_W), F32),
            jax.ShapeDtypeStruct((nchunks, GLA_HEADS, GLA_DV, LANES), BF16),
        ],
        scratch_shapes=[pltpu.VMEM((GLA_HEADS, GLA_DV, LANES), F32)],
        compiler_params=_params(48),
    )(p, p, p, p, wd_pad, bd, token)


def _mixer_out(x, o_f, o_b, p, gn, lng, lnb, ws_bf, bs_col, w_out, token):
    seq = x.shape[0]
    tm = min(seq, 512)

    def body(x_ref, of_ref, ob_ref, g_ref, u_ref, vv_ref, gn_ref, lng_ref, lnb_ref, ws_ref, bs_ref, wo_ref, token_ref, x1_ref, yc_ref, vn_sc):
        for h in range(GLA_HEADS):
            cols = pl.ds(h * GLA_DV, GLA_DV)
            oh = of_ref[:, cols] + ob_ref[:, cols]
            on = oh * lax.rsqrt(jnp.mean(oh * oh, axis=-1, keepdims=True) + EPS)
            gh = g_ref[:, cols]
            yc_ref[:, cols] = (on * gn_ref[:, cols] * (gh * _sigmoid(gh))).astype(BF16)
        zv = _gelu(vv_ref[...])
        xc = zv - jnp.mean(zv, axis=-1, keepdims=True)
        vhat = xc * lax.rsqrt(jnp.mean(xc * xc, axis=-1, keepdims=True) + EPS)
        vn_sc[...] = (vhat * lng_ref[...] + lnb_ref[...]).astype(BF16)
        for c in range(tm // GMLP_CHUNK):
            rows = pl.ds(c * GMLP_CHUNK, GMLP_CHUNK)
            for g in range(GMLP_GROUPS):
                cols = pl.ds(g * LANES, LANES)
                s = _nn(ws_ref[g], vn_sc[rows, cols]) + bs_ref[g]
                yc_ref[rows, pl.ds(GLA_W + g * LANES, LANES)] = (_gelu(u_ref[rows, cols]) * s).astype(BF16)
        x1_ref[...] = x_ref[...] + _nn(yc_ref[...], wo_ref[...])

    row = lambda w: pl.BlockSpec((tm, w), lambda i: (i, 0))
    pcol = lambda col: pl.BlockSpec((tm, GLA_W), lambda i: (i, col // GLA_W))
    return pl.pallas_call(
        body,
        name="mixer_out",
        grid=(seq // tm,),
        in_specs=[
            row(D_MODEL), row(GLA_W), row(GLA_W), pcol(COL_G), pcol(COL_U), pcol(COL_VV),
            _resident((1, GLA_W)), _resident((1, GMLP_W)), _resident((1, GMLP_W)),
            _resident((GMLP_GROUPS, GMLP_CHUNK, GMLP_CHUNK)), _resident((GMLP_GROUPS, GMLP_CHUNK, 1)),
            _resident((D_MODEL, D_MODEL)), _resident(TOKEN_SHAPE),
        ],
        out_specs=[row(D_MODEL), row(D_MODEL)],
        out_shape=[jax.ShapeDtypeStruct((seq, D_MODEL), F32), jax.ShapeDtypeStruct((seq, D_MODEL), BF16)],
        scratch_shapes=[pltpu.VMEM((tm, GMLP_W), BF16)],
        compiler_params=_params(48, ("parallel",)),
    )(x, o_f, o_b, p, p, p, gn, lng, lnb, ws_bf, bs_col, w_out, token)


def _ffn_fwd(x1, target, g2, gf, wg_t, wu_t, wd):
    seq = x1.shape[0]
    tm = min(seq, 256)

    def body(x1_ref, t_ref, g2_ref, gf_ref, wg_ref, wu_ref, wd_ref, h2_ref, gate_ref, up_ref, act_ref, dx2_ref, loss_ref, dgf_ref):
        @pl.when(pl.program_id(0) == 0)
        def _():
            loss_ref[...] = jnp.zeros_like(loss_ref)
            dgf_ref[...] = jnp.zeros_like(dgf_ref)

        x1v = x1_ref[...]
        h2 = (x1v * lax.rsqrt(jnp.mean(x1v * x1v, axis=-1, keepdims=True) + EPS) * g2_ref[...]).astype(BF16)
        h2_ref[...] = h2
        gate = _nt(h2, wg_ref[...])
        up = _nt(h2, wu_ref[...])
        act = (gate * _sigmoid(gate) * up).astype(BF16)
        gate_ref[...] = gate
        up_ref[...] = up
        act_ref[...] = act
        x2 = x1v + _nn(act, wd_ref[...])
        rf = lax.rsqrt(jnp.mean(x2 * x2, axis=-1, keepdims=True) + EPS)
        xh = x2 * rf
        err = xh * gf_ref[...] - t_ref[...]
        loss_ref[...] += 0.5 * jnp.sum(jnp.mean(err * err, axis=-1, keepdims=True))
        dy = err * (1.0 / D_MODEL)
        dgf_ref[...] += jnp.sum(dy * xh, axis=0, keepdims=True)
        dx2_ref[...] = _rms_bwd(dy * gf_ref[...], xh, rf)

    row = lambda w: pl.BlockSpec((tm, w), lambda i: (i, 0))
    weight = _resident((D_FF, D_MODEL))
    return pl.pallas_call(
        body,
        name="ffn_fwd",
        grid=(seq // tm,),
        in_specs=[row(D_MODEL), row(D_MODEL), _resident((1, D_MODEL)), _resident((1, D_MODEL)), weight, weight, weight],
        out_specs=[row(D_MODEL), row(D_FF), row(D_FF), row(D_FF), row(D_MODEL),
                   pl.BlockSpec((1, LANES), lambda i: (0, 0)), pl.BlockSpec((1, D_MODEL), lambda i: (0, 0))],
        out_shape=[
            jax.ShapeDtypeStruct((seq, D_MODEL), BF16),
            jax.ShapeDtypeStruct((seq, D_FF), F32),
            jax.ShapeDtypeStruct((seq, D_FF), F32),
            jax.ShapeDtypeStruct((seq, D_FF), BF16),
            jax.ShapeDtypeStruct((seq, D_MODEL), F32),
            jax.ShapeDtypeStruct((1, LANES), F32),
            jax.ShapeDtypeStruct((1, D_MODEL), F32),
        ],
        compiler_params=_params(56),
    )(x1, target, g2, gf, wg_t, wu_t, wd)


def _ffn_bwd(dx2, gate, up, x1, g2, wg_t, wu_t, wd):
    seq = x1.shape[0]
    tm = min(seq, 256)

    def body(dx2_ref, gate_ref, up_ref, x1_ref, g2_ref, wg_ref, wu_ref, wd_ref, dgate_ref, dup_ref, dx1_ref, dg2_ref):
        @pl.when(pl.program_id(0) == 0)
        def _():
            dg2_ref[...] = jnp.zeros_like(dg2_ref)

        dx2v = dx2_ref[...]
        dact = _nt(dx2v.astype(BF16), wd_ref[...])
        gate = gate_ref[...]
        sg = _sigmoid(gate)
        dgate = (dact * up_ref[...] * (sg * (1.0 + gate * (1.0 - sg)))).astype(BF16)
        dup = (dact * (gate * sg)).astype(BF16)
        dgate_ref[...] = dgate
        dup_ref[...] = dup
        dh2 = _nn(dgate, wg_ref[...]) + _nn(dup, wu_ref[...])
        x1v = x1_ref[...]
        r2 = lax.rsqrt(jnp.mean(x1v * x1v, axis=-1, keepdims=True) + EPS)
        xh = x1v * r2
        dg2_ref[...] += jnp.sum(dh2 * xh, axis=0, keepdims=True)
        dx1_ref[...] = dx2v + _rms_bwd(dh2 * g2_ref[...], xh, r2)

    row = lambda w: pl.BlockSpec((tm, w), lambda i: (i, 0))
    weight = _resident((D_FF, D_MODEL))
    return pl.pallas_call(
        body,
        name="ffn_bwd",
        grid=(seq // tm,),
        in_specs=[row(D_MODEL), row(D_FF), row(D_FF), row(D_MODEL), _resident((1, D_MODEL)), weight, weight, weight],
        out_specs=[row(D_FF), row(D_FF), row(D_MODEL), pl.BlockSpec((1, D_MODEL), lambda i: (0, 0))],
        out_shape=[
            jax.ShapeDtypeStruct((seq, D_FF), BF16),
            jax.ShapeDtypeStruct((seq, D_FF), BF16),
            jax.ShapeDtypeStruct((seq, D_MODEL), F32),
            jax.ShapeDtypeStruct((1, D_MODEL), F32),
        ],
        compiler_params=_params(56),
    )(dx2, gate, up, x1, g2, wg_t, wu_t, wd)


WGRAD_ROWS = D_FF // 2


def _ffn_wgrad(h2, dgate, dup, act, dx2):
    seq = h2.shape[0]
    tm = min(seq, 512)

    def body(h2_ref, dgate_ref, dup_ref, act_ref, dx2_ref, dwg_ref, dwu_ref, dwd_ref):
        @pl.when(pl.program_id(1) == 0)
        def _():
            dwg_ref[...] = jnp.zeros_like(dwg_ref)
            dwu_ref[...] = jnp.zeros_like(dwu_ref)
            dwd_ref[...] = jnp.zeros_like(dwd_ref)

        h2v = h2_ref[...]
        dwg_ref[...] += _tn(dgate_ref[...], h2v)
        dwu_ref[...] += _tn(dup_ref[...], h2v)
        dwd_ref[...] += _tn(act_ref[...], dx2_ref[...].astype(BF16))

    ff = pl.BlockSpec((tm, WGRAD_ROWS), lambda j, i: (i, j))
    row = pl.BlockSpec((tm, D_MODEL), lambda j, i: (i, 0))
    out = pl.BlockSpec((WGRAD_ROWS, D_MODEL), lambda j, i: (j, 0))
    return pl.pallas_call(
        body,
        name="ffn_wgrad",
        grid=(D_FF // WGRAD_ROWS, seq // tm),
        in_specs=[row, ff, ff, ff, row],
        out_specs=[out, out, out],
        out_shape=[jax.ShapeDtypeStruct((D_FF, D_MODEL), F32)] * 3,
        compiler_params=_params(56, ("parallel", "arbitrary")),
    )(h2, dgate, dup, act, dx2)


def _mixer_bwd(dx1, ycat, o_f, o_b, p, gn, lng, lnb, ws_bf, wst_bf, bs_col, w_out, token):
    seq = dx1.shape[0]
    tm = min(seq, 512)
    nsteps = seq // tm

    def body(dx1_ref, yc_ref, of_ref, ob_ref, g_ref, u_ref, vv_ref, gn_ref, lng_ref, lnb_ref, ws_ref, wst_ref, bs_ref, wo_ref, token_ref,
             do_ref, dg_ref, du_ref, dvv_ref, dwo_ref, dgn_ref, dlng_ref, dlnb_ref, dws_ref, dbs_ref, vn_sc, dvn_sc, dbs_acc):
        step = pl.program_id(0)

        @pl.when(step == 0)
        def _():
            for r in (dwo_ref, dgn_ref, dlng_ref, dlnb_ref, dws_ref, dbs_acc):
                r[...] = jnp.zeros_like(r)

        dx1b = dx1_ref[...].astype(BF16)
        dyc = _nt(dx1b, wo_ref[...])
        dwo_ref[...] += _tn(yc_ref[...], dx1b)
        for h in range(GLA_HEADS):
            cols = pl.ds(h * GLA_DV, GLA_DV)
            dya = dyc[:, h * GLA_DV : (h + 1) * GLA_DV]
            oh = of_ref[:, cols] + ob_ref[:, cols]
            rn = lax.rsqrt(jnp.mean(oh * oh, axis=-1, keepdims=True) + EPS)
            on = oh * rn
            gh = g_ref[:, cols]
            sg = _sigmoid(gh)
            sil = gh * sg
            gnh = gn_ref[:, cols]
            dgn_ref[:, cols] += jnp.sum(dya * on * sil, axis=0, keepdims=True)
            dg_ref[:, cols] = (dya * on * gnh * (sg * (1.0 + gh * (1.0 - sg)))).astype(BF16)
            do_ref[:, cols] = _rms_bwd(dya * gnh * sil, on, rn)
        vv = vv_ref[...]
        zv, zv_grad = _gelu_and_grad(vv)
        xc = zv - jnp.mean(zv, axis=-1, keepdims=True)
        rstd = lax.rsqrt(jnp.mean(xc * xc, axis=-1, keepdims=True) + EPS)
        vhat = xc * rstd
        vn_sc[...] = (vhat * lng_ref[...] + lnb_ref[...]).astype(BF16)
        for c in range(tm // GMLP_CHUNK):
            rows = pl.ds(c * GMLP_CHUNK, GMLP_CHUNK)
            for g in range(GMLP_GROUPS):
                cols = pl.ds(g * LANES, LANES)
                vn = vn_sc[rows, cols]
                s = _nn(ws_ref[g], vn) + bs_ref[g]
                dyb = dyc[c * GMLP_CHUNK : (c + 1) * GMLP_CHUNK, GLA_W + g * LANES : GLA_W + (g + 1) * LANES]
                zu, zu_grad = _gelu_and_grad(u_ref[rows, cols])
                du_ref[rows, cols] = (dyb * s * zu_grad).astype(BF16)
                ds = dyb * zu
                dbs_acc[g] += ds
                dsb = ds.astype(BF16)
                dws_ref[g] += _nt(dsb, vn)
                dvn_sc[rows, cols] = _nn(wst_ref[g], dsb)
        dvn = dvn_sc[...]
        dlng_ref[...] += jnp.sum(dvn * vhat, axis=0, keepdims=True)
        dlnb_ref[...] += jnp.sum(dvn, axis=0, keepdims=True)
        dvh = dvn * lng_ref[...]
        dzv = rstd * (dvh - jnp.mean(dvh, axis=-1, keepdims=True) - vhat * jnp.mean(dvh * vhat, axis=-1, keepdims=True))
        dvv_ref[...] = (dzv * zv_grad).astype(BF16)

        @pl.when(step == nsteps - 1)
        def _():
            dbs_ref[...] = jnp.sum(dbs_acc[...], axis=-1, keepdims=True)

    row = lambda w: pl.BlockSpec((tm, w), lambda i: (i, 0))
    pcol = lambda col: pl.BlockSpec((tm, GLA_W), lambda i: (i, col // GLA_W))
    const = lambda shape: pl.BlockSpec(shape, lambda i: (0,) * len(shape))
    return pl.pallas_call(
        body,
        name="mixer_bwd",
        grid=(nsteps,),
        in_specs=[
            row(D_MODEL), row(D_MODEL), row(GLA_W), row(GLA_W), pcol(COL_G), pcol(COL_U), pcol(COL_VV),
            _resident((1, GLA_W)), _resident((1, GMLP_W)), _resident((1, GMLP_W)),
            _resident((GMLP_GROUPS, GMLP_CHUNK, GMLP_CHUNK)), _resident((GMLP_GROUPS, GMLP_CHUNK, GMLP_CHUNK)),
            _resident((GMLP_GROUPS, GMLP_CHUNK, 1)), _resident((D_MODEL, D_MODEL)), _resident(TOKEN_SHAPE),
        ],
        out_specs=[
            row(GLA_W), row(GLA_W), row(GMLP_W), row(GMLP_W), const((D_MODEL, D_MODEL)),
            const((1, GLA_W)), const((1, GMLP_W)), const((1, GMLP_W)),
            const((GMLP_GROUPS, GMLP_CHUNK, GMLP_CHUNK)), const((GMLP_GROUPS, GMLP_CHUNK, 1)),
        ],
        out_shape=[
            jax.ShapeDtypeStruct((seq, GLA_W), F32), jax.ShapeDtypeStruct((seq, GLA_W), BF16),
            jax.ShapeDtypeStruct((seq, GMLP_W), BF16), jax.ShapeDtypeStruct((seq, GMLP_W), BF16),
            jax.ShapeDtypeStruct((D_MODEL, D_MODEL), F32),
            jax.ShapeDtypeStruct((1, GLA_W), F32), jax.ShapeDtypeStruct((1, GMLP_W), F32), jax.ShapeDtypeStruct((1, GMLP_W), F32),
            jax.ShapeDtypeStruct((GMLP_GROUPS, GMLP_CHUNK, GMLP_CHUNK), F32), jax.ShapeDtypeStruct((GMLP_GROUPS, GMLP_CHUNK, 1), F32),
        ],
        scratch_shapes=[pltpu.VMEM((tm, GMLP_W), BF16), pltpu.VMEM((tm, GMLP_W), F32), pltpu.VMEM((GMLP_GROUPS, GMLP_CHUNK, GMLP_CHUNK), F32)],
        compiler_params=_params(56),
    )(dx1, ycat, o_f, o_b, p, p, p, gn, lng, lnb, ws_bf, wst_bf, bs_col, w_out, token)


def _gla_bwd(p, do, st, wd_pad, bd, token, reverse, other=None):
    seq = p.shape[0]
    tg = _gla_tile(seq)
    nt = seq // tg
    n = tg // GLA_CHUNK
    scale = GLA_DK**-0.5

    def tile(i):
        return i if reverse else nt - 1 - i

    def body(q_ref, k_ref, v_ref, lr_ref, do_ref, st_ref, wd_ref, bd_ref, token_ref, *rest):
        others, (dq_ref, dk_ref, dv_ref, dlr_ref, dwd_ref, dbd_ref, carry) = rest[:-7], rest[-7:]
        if others:
            odq_ref, odk_ref, odv_ref, odlr_ref = others

            def put(ref, idx, val, oref):
                ref[idx] = (val + oref[idx]).astype(BF16)
        else:
            odq_ref = odk_ref = odv_ref = odlr_ref = None

            def put(ref, idx, val, oref):
                ref[idx] = val

        @pl.when(pl.program_id(0) == 0)
        def _():
            carry[...] = jnp.zeros_like(carry)
            dwd_ref[...] = jnp.zeros_like(dwd_ref)
            dbd_ref[...] = jnp.zeros_like(dbd_ref)

        lr_bf = lr_ref[...].astype(BF16)
        carries = [carry[h] for h in range(GLA_HEADS)]
        row_in_chunk = lax.broadcasted_iota(jnp.int32, (tg, LANES), 0) % GLA_CHUNK
        lane_head = lax.broadcasted_iota(jnp.int32, (1, LANES), 1) // GLA_DK
        tt = lax.broadcasted_iota(jnp.int32, (GLA_CHUNK, GLA_CHUNK), 0)
        ss = lax.broadcasted_iota(jnp.int32, (GLA_CHUNK, GLA_CHUNK), 1)
        causal = (tt <= ss) if reverse else (tt >= ss)
        order = range(n) if reverse else range(n - 1, -1, -1)
        dlr = jnp.zeros((tg, LANES), F32)
        heads = range(GLA_HEADS)
        pv, masks, qdh, vhs, dohs, stbs = {}, {}, {}, {}, {}, {}
        sc_raw, dp, acc = {}, {}, {}
        for pair in range(2):
            cols = pl.ds(pair * LANES, LANES)
            pre, b3, blast = _gla_decay_terms(lr_bf, wd_ref, bd_ref, pair, row_in_chunk, reverse, n)
            q3 = q_ref[:, cols].reshape(n, GLA_CHUNK, LANES) * scale
            k3 = k_ref[:, cols].reshape(n, GLA_CHUNK, LANES)
            eb = jnp.exp(b3)
            emb = jnp.exp(-b3)
            ekte = jnp.exp(blast - b3)
            kdf = k3 * emb
            kte = k3 * ekte
            both = pl.ds(2 * pair * GLA_DV, 2 * GLA_DV)
            pv[pair] = dict(pre=pre, eb=eb, emb=emb, ekte=ekte, qd=q3 * eb, kdf=kdf, kd=kdf.astype(BF16), kte=kte, kte_bf=kte.astype(BF16),
                            dec=jnp.exp(blast), v=v_ref[:, both].reshape(n, GLA_CHUNK, 2 * GLA_DV).astype(BF16),
                            do=do_ref[:, both].reshape(n, GLA_CHUNK, 2 * GLA_DV).astype(BF16))
            for hh in range(2):
                h = 2 * pair + hh
                masks[h] = (lane_head == hh).astype(F32)
                qdh[h] = (pv[pair]["qd"] * masks[h]).astype(BF16)
                vhs[h] = pv[pair]["v"][:, :, hh * GLA_DV : (hh + 1) * GLA_DV]
                dohs[h] = pv[pair]["do"][:, :, hh * GLA_DV : (hh + 1) * GLA_DV]
                stbs[h] = st_ref[:, h]
                dp[h] = _bnt(dohs[h], vhs[h])
                acc[h] = _btn(dohs[h], qdh[h])
            sc_both = _bnt(jnp.concatenate([qdh[2 * pair], qdh[2 * pair + 1]], axis=1), pv[pair]["kd"])
            for hh in range(2):
                sc_raw[2 * pair + hh] = sc_both[:, hh * GLA_CHUNK : (hh + 1) * GLA_CHUNK, :]
        dsa, sc = {}, {}
        for h in heads:
            sc[h] = jnp.where(causal, sc_raw[h], 0.0).astype(BF16)
            dp[h] = jnp.where(causal, dp[h], 0.0).astype(BF16)
            dec = pv[h // 2]["dec"]
            c, after = carries[h], [None] * n
            for j in order:
                after[j] = c
                c = acc[h][j] + dec[j] * c
            carries[h] = c
            dsa[h] = jnp.stack(after)
        dvs, dqs, dks, dwds, dbds = [], [], [], [], []
        for pair in range(2):
            cols = pl.ds(pair * LANES, LANES)
            v = pv[pair]
            h0, h1 = 2 * pair, 2 * pair + 1
            dsa_both = jnp.concatenate([dsa[h0], dsa[h1]], axis=1)
            dsa_bf = dsa_both.astype(BF16)
            stb_bf = jnp.concatenate([stbs[h0], stbs[h1]], axis=1)
            dq_intra = _bnn(jnp.concatenate([dp[h0], dp[h1]], axis=1), v["kd"])
            dqd = (dq_intra[:, :GLA_CHUNK, :] * masks[h0] + dq_intra[:, GLA_CHUNK:, :] * masks[h1]) + _bnn(v["do"], stb_bf)
            dkd = _btn(dp[h0], qdh[h0]) + _btn(dp[h1], qdh[h1])
            dkte = _bnn(v["v"], dsa_bf)
            ddec = jnp.sum(dsa[h0] * stbs[h0].astype(F32) + dsa[h1] * stbs[h1].astype(F32), axis=1, keepdims=True)
            dv_inter = _bnt(v["kte_bf"], dsa_bf)
            for hh, h in ((0, h0), (1, h1)):
                dvs.append((_btn(sc[h], dohs[h]) + dv_inter[:, :, hh * GLA_DV : (hh + 1) * GLA_DV]).reshape(tg, GLA_DV))
            dqs.append((dqd * (scale * v["eb"])).reshape(tg, LANES))
            dks.append((dkd * v["emb"] + dkte * v["ekte"]).reshape(tg, LANES))
            db = dqd * v["qd"] - dkd * v["kdf"] - dkte * v["kte"]
            dblast = jnp.sum(dkte * v["kte"], axis=1, keepdims=True) + ddec * v["dec"]
            dla = _chunk_cumsum(db.reshape(tg, LANES), row_in_chunk, not reverse) + jnp.broadcast_to(dblast, (n, GLA_CHUNK, LANES)).reshape(tg, LANES)
            dpre = (dla * (1.0 / GLA_TAU) * _sigmoid(-v["pre"]))
            dpre_bf = dpre.astype(BF16)
            dlr = dlr + _nt(dpre_bf, wd_ref[:, cols])
            dwds.append(_tn(lr_bf, dpre_bf))
            dbds.append(jnp.sum(dpre, axis=0, keepdims=True))
        put(dlr_ref, (slice(None), slice(None)), dlr, odlr_ref)
        for pair in range(2):
            cols = pl.ds(pair * LANES, LANES)
            put(dq_ref, (slice(None), cols), dqs[pair], odq_ref)
            put(dk_ref, (slice(None), cols), dks[pair], odk_ref)
            dwd_ref[:, cols] += dwds[pair]
            dbd_ref[:, cols] += dbds[pair]
        for h in range(GLA_HEADS):
            put(dv_ref, (slice(None), pl.ds(h * GLA_DV, GLA_DV)), dvs[h], odv_ref)
            carry[h] = carries[h]

    pieces = [
        pl.BlockSpec((tg, KEY_W), lambda i: (tile(i), 0)),
        pl.BlockSpec((tg, KEY_W), lambda i: (tile(i), 0)),
        pl.BlockSpec((tg, GLA_W), lambda i: (tile(i), 0)),
        pl.BlockSpec((tg, LANES), lambda i: (tile(i), 0)),
    ]
    piece_dtype = BF16 if other else F32
    return pl.pallas_call(
        body,
        name="gla_bwd_rev" if reverse else "gla_bwd",
        grid=(nt,),
        in_specs=[
            pl.BlockSpec((tg, KEY_W), lambda i: (tile(i), COL_Q // KEY_W)),
            pl.BlockSpec((tg, KEY_W), lambda i: (tile(i), COL_K // KEY_W)),
            pl.BlockSpec((tg, GLA_W), lambda i: (tile(i), COL_V // GLA_W)),
            pl.BlockSpec((tg, LANES), lambda i: (tile(i), COL_LR // LANES)),
            pl.BlockSpec((tg, GLA_W), lambda i: (tile(i), 0)),
            pl.BlockSpec((n, GLA_HEADS, GLA_DV, LANES), lambda i: (tile(i), 0, 0, 0)),
            _resident((LANES, KEY_W)),
            _resident((1, KEY_W)),
            _resident(TOKEN_SHAPE),
        ] + (pieces if other else []),
        out_specs=pieces + [pl.BlockSpec((LANES, KEY_W), lambda i: (0, 0)), pl.BlockSpec((1, KEY_W), lambda i: (0, 0))],
        out_shape=[
            jax.ShapeDtypeStruct((seq, KEY_W), piece_dtype), jax.ShapeDtypeStruct((seq, KEY_W), piece_dtype),
            jax.ShapeDtypeStruct((seq, GLA_W), piece_dtype), jax.ShapeDtypeStruct((seq, LANES), piece_dtype),
            jax.ShapeDtypeStruct((LANES, KEY_W), F32), jax.ShapeDtypeStruct((1, KEY_W), F32),
        ],
        scratch_shapes=[pltpu.VMEM((GLA_HEADS, GLA_DV, LANES), F32)],
        compiler_params=_params(56),
    )(p, p, p, p, do, st, wd_pad, bd, token, *(other or ()))


def _inproj_wgrad(x, g1, dq, dk, dv, dg, du, dvv, dlr):
    seq = x.shape[0]
    tm = min(seq, 512)

    def body(x_ref, g1_ref, dq_ref, dk_ref, dv_ref, dg_ref, du_ref, dvv_ref, dlr_ref, dw_ref, dp_ref):
        @pl.when(pl.program_id(0) == 0)
        def _():
            dw_ref[...] = jnp.zeros_like(dw_ref)

        for col, ref in ((COL_Q, dq_ref), (COL_K, dk_ref), (COL_V, dv_ref), (COL_G, dg_ref), (COL_U, du_ref), (COL_VV, dvv_ref), (COL_LR, dlr_ref)):
            dp_ref[:, col : col + ref.shape[1]] = ref[...]
        xv = x_ref[...]
        h = (xv * lax.rsqrt(jnp.mean(xv * xv, axis=-1, keepdims=True) + EPS) * g1_ref[...]).astype(BF16)
        dw_ref[0:ROW_LR, :] += _tn(dp_ref[:, 0:COL_U], h)
        dw_ref[ROW_UV:PROJ_W, :] += _tn(dp_ref[:, COL_U:COL_LR], h)
        dw_ref[ROW_LR:ROW_UV, :] += _tn(dp_ref[:, COL_LR:PROJ_WP], h)[0 : ROW_UV - ROW_LR]

    row = lambda w: pl.BlockSpec((tm, w), lambda i: (i, 0))
    return pl.pallas_call(
        body,
        name="inproj_wgrad",
        grid=(seq // tm,),
        in_specs=[row(D_MODEL), _resident((1, D_MODEL)), row(KEY_W), row(KEY_W), row(GLA_W), row(GLA_W), row(GMLP_W), row(GMLP_W), row(LANES)],
        out_specs=[pl.BlockSpec((PROJ_W, D_MODEL), lambda i: (0, 0)), row(PROJ_WP)],
        out_shape=[jax.ShapeDtypeStruct((PROJ_W, D_MODEL), F32), jax.ShapeDtypeStruct((seq, PROJ_WP), BF16)],
        compiler_params=_params(56),
    )(x, g1, dq, dk, dv, dg, du, dvv, dlr)


def _inproj_dx(x, dx1, g1, w_in_t, dp, token):
    seq = x.shape[0]
    tm = min(seq, 512)

    def body(x_ref, dx1_ref, g1_ref, w_ref, dp_ref, token_ref, dx_ref, dg1_ref):
        @pl.when(pl.program_id(0) == 0)
        def _():
            dg1_ref[...] = jnp.zeros_like(dg1_ref)

        xv = x_ref[...]
        r1 = lax.rsqrt(jnp.mean(xv * xv, axis=-1, keepdims=True) + EPS)
        xh = xv * r1
        dh = (_nn(dp_ref[:, 0:COL_U], w_ref[0:ROW_LR, :]) + _nn(dp_ref[:, COL_U:COL_LR], w_ref[ROW_UV:PROJ_W, :])
              + _nn(dp_ref[:, COL_LR:PROJ_WP], w_ref[ROW_LR : ROW_LR + LANES, :]))
        dg1_ref[...] += jnp.sum(dh * xh, axis=0, keepdims=True)
        dx_ref[...] = dx1_ref[...] + _rms_bwd(dh * g1_ref[...], xh, r1)

    row = lambda w: pl.BlockSpec((tm, w), lambda i: (i, 0))
    return pl.pallas_call(
        body,
        name="inproj_dx",
        grid=(seq // tm,),
        in_specs=[row(D_MODEL), row(D_MODEL), _resident((1, D_MODEL)), _resident((PROJ_W, D_MODEL)), row(PROJ_WP), _resident(TOKEN_SHAPE)],
        out_specs=[row(D_MODEL), pl.BlockSpec((1, D_MODEL), lambda i: (0, 0))],
        out_shape=[jax.ShapeDtypeStruct((seq, D_MODEL), F32), jax.ShapeDtypeStruct((1, D_MODEL), F32)],
        compiler_params=_params(48),
    )(x, dx1, g1, w_in_t, dp, token)


def _in_hbm(a):
    return pltpu.with_memory_space_constraint(a, pltpu.HBM)


def _row_tile(rows, multiple=8):
    for t in range(min(rows, 512), 0, -1):
        if rows % t == 0 and t % multiple == 0:
            return t
    return rows


def _cast_into_slot(w, shard, token):
    rows, cols = w.shape
    tr = _row_tile(rows, 16)

    def body(s_ref, w_ref, token_ref, o_ref):
        o_ref[...] = w_ref[...].astype(BF16)

    return pl.pallas_call(
        body,
        name="cast_into_slot",
        grid_spec=pltpu.PrefetchScalarGridSpec(
            num_scalar_prefetch=1,
            grid=(rows // tr,),
            in_specs=[pl.BlockSpec((tr, cols), lambda i, s_ref: (i, 0)), pl.BlockSpec(TOKEN_SHAPE, lambda i, s_ref: (0, 0))],
            out_specs=pl.BlockSpec((None, tr, cols), lambda i, s_ref: (s_ref[0], i, 0)),
        ),
        out_shape=pltpu.HBM((N_SHARDS, rows, cols), BF16),
        compiler_params=_params(32, ("parallel",)),
    )(shard, _in_hbm(w), token)


def _add_halves(grads4, recvs, shard_core):
    n = len(grads4)
    _, rows, _ = grads4[0].shape
    tr = _row_tile(rows, 16)

    def body(sc_ref, *refs):
        for k in range(n):
            total = refs[k][...] + refs[n + k][...]
            refs[3 * n + k][...] = total.astype(BF16)

            @pl.when(pl.program_id(1) == sc_ref[0])
            def _(k=k, total=total):
                refs[2 * n + k][...] = total

    theirs = pl.BlockSpec((None, tr, HALF), lambda i, s, sc_ref: (s, i, 0))
    mine = pl.BlockSpec((None, tr, HALF), lambda i, s, sc_ref: (s, i, sc_ref[1]))
    kept = pl.BlockSpec((tr, HALF), lambda i, s, sc_ref: (i, 0))
    outs = pl.pallas_call(
        body,
        name="add_halves",
        grid_spec=pltpu.PrefetchScalarGridSpec(
            num_scalar_prefetch=1,
            grid=(rows // tr, N_SHARDS),
            in_specs=[mine] * n + [theirs] * n,
            out_specs=[kept] * n + [theirs] * n,
        ),
        out_shape=[pltpu.HBM((rows, HALF), F32)] * n + [pltpu.HBM((N_SHARDS, rows, HALF), BF16)] * n,
        compiler_params=_params(48, ("parallel", "arbitrary")),
    )(shard_core, *[_in_hbm(a) for a in list(grads4) + list(recvs)])
    return list(zip(outs[:n], outs[n:]))


def _add_partials(part, recv3, shard_core, token):
    rows, _ = part.shape
    tr = _row_tile(rows, 16)

    def body(sc_ref, p_ref, r_ref, token_ref, o_ref):
        o_ref[...] = ((p_ref[...] + r_ref[0].astype(F32)) + r_ref[1].astype(F32)) + r_ref[2].astype(F32)

    return pl.pallas_call(
        body,
        name="add_partials",
        grid_spec=pltpu.PrefetchScalarGridSpec(
            num_scalar_prefetch=1,
            grid=(rows // tr,),
            in_specs=[
                pl.BlockSpec((tr, HALF), lambda i, sc_ref: (i, 0)),
                pl.BlockSpec((3, tr, HALF), lambda i, sc_ref: (0, i, 0)),
                pl.BlockSpec(TOKEN_SHAPE, lambda i, sc_ref: (0, 0)),
            ],
            out_specs=pl.BlockSpec((tr, HALF), lambda i, sc_ref: (i, sc_ref[1])),
        ),
        out_shape=pltpu.HBM((rows, 2 * HALF), F32),
        compiler_params=_params(32, ("parallel",)),
    )(shard_core, _in_hbm(part), _in_hbm(recv3), token)


def _adam_math(w, g, m, v):
    m = ADAM_B1 * m + (1.0 - ADAM_B1) * g
    v = ADAM_B2 * v + (1.0 - ADAM_B2) * (g * g)
    m_hat = m / (1.0 - ADAM_B1**ADAM_STEP)
    v_hat = v / (1.0 - ADAM_B2**ADAM_STEP)
    delta = -ADAM_LR * (m_hat / (jnp.sqrt(v_hat) + ADAM_EPS) + ADAM_WD * w)
    return delta, m, v


def _adamw(w, g, m, v):
    rows, cols = w.shape
    tr = _row_tile(rows)

    def body(w_ref, g_ref, m_ref, v_ref, go_ref, d_ref, mo_ref, vo_ref):
        gv = g_ref[...]
        go_ref[...] = gv
        d_ref[...], mo_ref[...], vo_ref[...] = _adam_math(w_ref[...], gv, m_ref[...], v_ref[...])

    spec = pl.BlockSpec((tr, cols), lambda i: (i, 0))
    return pl.pallas_call(
        body, name="adamw", grid=(rows // tr,), in_specs=[spec] * 4, out_specs=[spec] * 4, out_shape=[pltpu.HBM(w.shape, F32)] * 4,
        compiler_params=_params(32, ("parallel",)),
    )(_in_hbm(w), _in_hbm(g), _in_hbm(m), _in_hbm(v))


SMALL_ROWS = 560
DECAY_ROWS = 8
SMALL_TOTAL = SMALL_ROWS + 2 * N_SHARDS * DECAY_ROWS


def _adamw_small(gathered, own, wp, mp, vp, like):
    out_rows = SMALL_ROWS + 2 * DECAY_ROWS
    places, off = [], 0
    for a in like:
        rows = a.size // LANES
        kept = a.shape[-1] == LANES
        places.append((off, rows, kept, (rows, LANES) if kept else (1, a.size)))
        off += rows
    loss_row = off
    decay_shape = (LOWRANK, KEY_W // N_SHARDS)
    n = len(places) + 2

    def body(ga_ref, own_ref, w_ref, m_ref, v_ref, *refs):
        outs, loss_ref, packed = refs[: 4 * n], refs[4 * n], refs[4 * n + 1 :]
        g_sc = packed[0]
        x, y, c = _position()
        shard, me = 2 * x + y, 4 * x + 2 * y + c
        total = lambda rows: functools.reduce(lambda a, b: a + b, [jnp.where(me == d, own_ref[rows, :], ga_ref[d, rows, :]) for d in range(8)])
        g_sc[pl.ds(0, SMALL_ROWS), :] = total(pl.ds(0, SMALL_ROWS))
        for k in range(2):
            start = pl.multiple_of(SMALL_ROWS + k * N_SHARDS * DECAY_ROWS + shard * DECAY_ROWS, DECAY_ROWS)
            g_sc[pl.ds(SMALL_ROWS + k * DECAY_ROWS, DECAY_ROWS), :] = total(pl.ds(start, DECAY_ROWS))
        packed[1][...], packed[2][...], packed[3][...] = _adam_math(w_ref[...], g_sc[...], m_ref[...], v_ref[...])
        loss_ref[...] = g_sc[loss_row : loss_row + 1, :]
        for t, res in enumerate(packed):
            for (at, rows, kept, _), out in zip(places, outs[t * n :]):
                if kept:
                    out[...] = res[at : at + rows, :]
                else:
                    for r in range(rows):
                        out[:, r * LANES : (r + 1) * LANES] = res[at + r : at + r + 1, :]
            for k in range(2):
                out = outs[t * n + len(places) + k]
                both = res[SMALL_ROWS + k * DECAY_ROWS : SMALL_ROWS + (k + 1) * DECAY_ROWS, :]
                halves = (both, pltpu.roll(both, LANES // 2, axis=1))
                for r in range(DECAY_ROWS):
                    for h in range(2):
                        out[2 * r + h : 2 * r + h + 1, :] = halves[h][r : r + 1, 0 : LANES // 2]

    shapes = [jax.ShapeDtypeStruct(s, F32) for *_, s in places] + [jax.ShapeDtypeStruct(decay_shape, F32)] * 2
    out = pl.pallas_call(
        body,
        name="adamw_small",
        out_shape=shapes * 4 + [jax.ShapeDtypeStruct((1, LANES), F32)],
        scratch_shapes=[pltpu.VMEM((out_rows, LANES), F32)] * 4,
        compiler_params=_params(32, None),
    )(gathered, own, wp, mp, vp)
    return [list(out[t * n : (t + 1) * n]) for t in range(4)], out[4 * n]


ANY = pl.BlockSpec(memory_space=pl.ANY)


def _position():
    return lax.axis_index("x"), lax.axis_index("y"), lax.axis_index("c")


def _other_chips(x, y):
    return [(1 - x, y), (x, 1 - y), (1 - x, 1 - y)]


HBM = pl.BlockSpec(memory_space=pltpu.HBM)
SEM = pl.BlockSpec(memory_space=pltpu.SEMAPHORE)
TOKEN = jax.ShapeDtypeStruct(TOKEN_SHAPE, F32)
DATAFLOW = pltpu.SideEffectType.DATAFLOW_SIDE_EFFECTING


def _half_block(ref4, slot, core):
    return ref4.at[slot, :, pl.ds(pl.multiple_of(core * HALF, HALF), HALF)]


def _gather_ici_copies(bufs, lands, send_sems, recv_sems, first=0):
    x, y, c = _position()
    pairs = []
    for k, ref4 in enumerate(bufs):
        mine = _half_block(ref4, 2 * x + y, c)
        for j, (px, py) in enumerate(_other_chips(x, y)):
            at = first + 3 * k + j
            sems = dict(send_sem=send_sems.at[at], recv_sem=recv_sems.at[at], device_id=(px, py, c), device_id_type=MESH)
            pairs.append((functools.partial(pltpu.make_async_remote_copy, src_ref=mine, dst_ref=mine, **sems),
                          functools.partial(pltpu.make_async_remote_copy, src_ref=mine, dst_ref=_half_block(ref4, 2 * px + py, c), **sems)))
    return pairs


def _gather_d2d_copies(bufs, lands, send_sems, recv_sems, first=0):
    x, y, c = _position()
    pairs = []
    for k, ref4 in enumerate(bufs):
        for j, (px, py) in enumerate(_other_chips(x, y)):
            have = _half_block(ref4, 2 * px + py, c)
            at = first + 3 * k + j
            sems = dict(send_sem=send_sems.at[at], recv_sem=recv_sems.at[at], device_id=(x, y, 1 - c), device_id_type=MESH)
            pairs.append((functools.partial(pltpu.make_async_remote_copy, src_ref=have, dst_ref=have, **sems),
                          functools.partial(pltpu.make_async_remote_copy, src_ref=have, dst_ref=_half_block(ref4, 2 * px + py, 1 - c), **sems)))
    return pairs


def _gather_forward(bufs):
    n = len(bufs)

    def body(*refs):
        outs = refs[n : 2 * n]
        send_sems, recv_sems = refs[2 * n :]
        d2d = _gather_d2d_copies(outs, (), send_sems, recv_sems)
        for forward, _ in d2d:
            forward().start()
        for forward, arrival in d2d:
            arrival().wait_recv()
            forward().wait_send()

    return pl.pallas_call(
        body,
        name="gather_forward",
        in_specs=[ANY] * n,
        out_specs=[ANY] * n,
        out_shape=[jax.ShapeDtypeStruct(b.shape, b.dtype) for b in bufs],
        input_output_aliases={k: k for k in range(n)},
        scratch_shapes=[pltpu.SemaphoreType.DMA((3 * n,)), pltpu.SemaphoreType.DMA((3 * n,))],
        compiler_params=pltpu.CompilerParams(has_side_effects=True),
    )(*bufs)


def _both_ends(**copy):
    maker = functools.partial(pltpu.make_async_remote_copy, **copy)
    return maker, maker


def _scatter_copies(parts, lands, send_sems, recv_sems):
    x, y, c = _position()
    return [_both_ends(src_ref=parts[k].at[2 * px + py], dst_ref=lands[k].at[j], send_sem=send_sems.at[3 * k + j],
                       recv_sem=recv_sems.at[3 * k + j], device_id=(px, py, c), device_id_type=MESH)
            for k in range(len(parts)) for j, (px, py) in enumerate(_other_chips(x, y))]


def _exchange_copies(grads, lands, send_sems, recv_sems):
    x, y, c = _position()
    return [_both_ends(src_ref=grads[k].at[:, :, pl.ds(pl.multiple_of((1 - c) * HALF, HALF), HALF)], dst_ref=lands[k],
                       send_sem=send_sems.at[k], recv_sem=recv_sems.at[k], device_id=(x, y, 1 - c), device_id_type=MESH)
            for k in range(len(grads))]


def _exchange_lands(grads4):
    return [jax.ShapeDtypeStruct((N_SHARDS, g.shape[1], HALF), g.dtype) for g in grads4]


def _scatter_lands(parts4):
    return [jax.ShapeDtypeStruct((3,) + g.shape[1:], g.dtype) for g in parts4]


def _small_gather_copies(blocks, lands, send_sems, recv_sems):
    x, y, c = _position()
    flip = lambda v, bit: 1 - v if bit else v
    return [_both_ends(src_ref=blocks[0], dst_ref=lands[0].at[4 * x + 2 * y + c], send_sem=send_sems.at[r - 1],
                       recv_sem=recv_sems.at[r - 1], device_id=(flip(x, r & 4), flip(y, r & 2), flip(c, r & 1)), device_id_type=MESH)
            for r in range(1, 8)]


def _split_start(name, srcs, land_shapes, make_copies, nsem, after=()):
    n, nl, na = len(srcs), len(land_shapes), len(after)
    lands = [lax.empty(a.shape, a.dtype) for a in land_shapes]

    def body(*refs):
        send_sems, recv_sems = refs[n + nl + na], refs[n + nl + na + 1]
        token = refs[2 * (n + nl) + na + 2]
        for send, _ in make_copies(refs[:n], refs[n : n + nl], send_sems, recv_sems):
            send().start()
        token[...] = jnp.zeros_like(token)

    hbm = lambda a: pltpu.HBM(a.shape, a.dtype)
    out = pl.pallas_call(
        body,
        name=name,
        in_specs=[HBM] * (n + nl) + [ANY] * na,
        out_specs=(SEM, SEM, *[HBM] * (n + nl), pl.BlockSpec(memory_space=pltpu.VMEM)),
        out_shape=(pltpu.SemaphoreType.DMA((nsem,)), pltpu.SemaphoreType.DMA((nsem,)), *[hbm(a) for a in list(srcs) + lands], TOKEN),
        input_output_aliases={k: 2 + k for k in range(n + nl)},
        compiler_params=pltpu.CompilerParams(has_side_effects=DATAFLOW),
    )(*[pltpu.with_memory_space_constraint(a, pltpu.HBM) for a in list(srcs) + lands], *after)
    return out[0], out[1], list(out[2 : 2 + n]), list(out[2 + n : 2 + n + nl]), out[2 + n + nl]


def _split_wait(name, send_sems, recv_sems, srcs, lands, make_copies, after):
    n, nl = len(srcs), len(lands)

    def body(*refs):
        for send, arrival in make_copies(refs[:n], refs[n : n + nl], refs[n + nl], refs[n + nl + 1]):
            send().wait_send()
            arrival().wait_recv()

    hbm = lambda a: pltpu.HBM(a.shape, a.dtype)
    out = pl.pallas_call(
        body,
        name=name,
        in_specs=[HBM] * (n + nl) + [SEM, SEM] + [ANY] * len(after),
        out_specs=tuple([HBM] * (n + nl)),
        out_shape=tuple(hbm(a) for a in list(srcs) + list(lands)),
        input_output_aliases={k: k for k in range(n + nl)},
        compiler_params=pltpu.CompilerParams(has_side_effects=DATAFLOW),
    )(*srcs, *lands, send_sems, recv_sems, *after)
    return list(out[:n]), list(out[n:])


def _join_copies(bufs, lands, send_sems, recv_sems, first=0):
    x, y, c = _position()
    half = lambda ref, core: ref.at[:, pl.ds(pl.multiple_of(core * HALF, HALF), HALF)]
    pairs = []
    for k, ref in enumerate(bufs):
        sems = dict(send_sem=send_sems.at[first + k], recv_sem=recv_sems.at[first + k], device_id=(x, y, 1 - c), device_id_type=MESH)
        pairs.append((functools.partial(pltpu.make_async_remote_copy, src_ref=half(ref, c), dst_ref=half(ref, c), **sems),
                      functools.partial(pltpu.make_async_remote_copy, src_ref=half(ref, c), dst_ref=half(ref, 1 - c), **sems)))
    return pairs


def _allgather_small(block):
    m_per, ncol = block.shape

    def body(x_ref, out_ref, send_sems, recv_sems, local_sem):
        x, y, c = _position()
        me, sibling = (x, y, c), (x, y, 1 - c)
        chips = _other_chips(x, y)

        def rows(px, py, pc):
            return out_ref.at[4 * px + 2 * py + pc]

        def copy(k, blk, to, src=None):
            return pltpu.make_async_remote_copy(
                src_ref=rows(*blk) if src is None else src, dst_ref=rows(*blk),
                send_sem=send_sems.at[k], recv_sem=recv_sems.at[k], device_id=to, device_id_type=MESH)

        mine = pltpu.make_async_copy(x_ref, rows(*me), local_sem)
        mine.start()
        first = [copy(0, me, sibling, src=x_ref)] + [copy(1 + j, me, (*chip, c), src=x_ref) for j, chip in enumerate(chips)]
        for cp in first:
            cp.start()
        passed = [copy(4 + j, (*chip, c), sibling) for j, chip in enumerate(chips)]
        for j, chip in enumerate(chips):
            copy(1 + j, (*chip, c), me).wait_recv()
            passed[j].start()
        copy(0, sibling, me).wait_recv()
        for j, chip in enumerate(chips):
            copy(4 + j, (*chip, 1 - c), me).wait_recv()
        for cp in first + passed:
            cp.wait_send()
        mine.wait()

    return pl.pallas_call(
        body,
        name="allgather_small",
        in_specs=[pl.BlockSpec(memory_space=pltpu.VMEM)],
        out_specs=pl.BlockSpec(memory_space=pltpu.VMEM),
        out_shape=jax.ShapeDtypeStruct((8, m_per, ncol), block.dtype),
        scratch_shapes=[pltpu.SemaphoreType.DMA((7,)), pltpu.SemaphoreType.DMA((7,)), pltpu.SemaphoreType.DMA],
        compiler_params=pltpu.CompilerParams(has_side_effects=True, vmem_limit_bytes=32 * MIB),
    )(block)


SMALL_NAMES = ["norm1_g", "b_decay_f", "b_decay_b", "gla_norm_g", "gmlp_ln_g", "gmlp_ln_b", "w_spatial", "b_spatial", "norm2_g", "final_norm_g"]


def _pack_small(parts, decay_parts):
    flat = jnp.concatenate([a.reshape(-1) for a in parts])
    flat = jnp.pad(flat, (0, SMALL_ROWS * LANES - flat.shape[0])).reshape(SMALL_ROWS, LANES)
    return jnp.concatenate([flat] + [d.reshape(-1, LANES) for d in decay_parts], axis=0)


def kernel(x, norm1_g, w_in, w_decay_f, b_decay_f, w_decay_b, b_decay_b, gla_norm_g, gmlp_ln_g, gmlp_ln_b, w_spatial, b_spatial, w_out, norm2_g, w_gate, w_up, w_down, final_norm_g, loss_target, m_norm1_g, m_w_in, m_w_decay_f, m_b_decay_f, m_w_decay_b, m_b_decay_b, m_gla_norm_g, m_gmlp_ln_g, m_gmlp_ln_b, m_w_spatial, m_b_spatial, m_w_out, m_norm2_g, m_w_gate, m_w_up, m_w_down, m_final_norm_g, v_norm1_g, v_w_in, v_w_decay_f, v_b_decay_f, v_w_decay_b, v_b_decay_b, v_gla_norm_g, v_gmlp_ln_g, v_gmlp_ln_b, v_w_spatial, v_b_spatial, v_w_out, v_norm2_g, v_w_gate, v_w_up, v_w_down, v_final_norm_g):
    args = dict(locals())
    cx, cy, cc = lax.axis_index("x"), lax.axis_index("y"), lax.axis_index("c")
    shard = 2 * cx + cy
    xs = x[0]
    target = loss_target[0]

    big_names = ["w_in", "w_out", "w_gate", "w_up", "w_down"]
    transposed = ("w_in", "w_gate", "w_up")
    rows_of = lambda pre, k: jnp.transpose(args[pre + k][0]) if k in transposed else args[pre + k][0]
    big_shards = {k: rows_of("", k) for k in big_names}
    s_arr = shard.reshape(1).astype(jnp.int32)
    sc_arr = jnp.stack([shard, cc]).astype(jnp.int32)
    zero_token = jnp.zeros(TOKEN_SHAPE, F32)
    dec_block = jnp.concatenate([w_decay_f[0].reshape(-1, LANES), w_decay_b[0].reshape(-1, LANES)], axis=0)
    first_copies = lambda srcs, lands, send_sems, recv_sems: (
        _small_gather_copies(srcs[:1], lands, send_sems, recv_sems) + _gather_ici_copies(srcs[1:], (), send_sems, recv_sems, first=7))
    w_send, w_recv, first_srcs, dec_lands, token_w_in = _split_start(
        "w_in_gather_start", [dec_block, _cast_into_slot(big_shards["w_in"], s_arr, zero_token)],
        [jax.ShapeDtypeStruct((8,) + dec_block.shape, F32)], first_copies, 7 + 3)
    late = ["w_out", "w_gate", "w_up", "w_down"]
    late_slots = [_cast_into_slot(big_shards[k], s_arr, token_w_in) for k in late]
    (dec_block, w_in4), (dec_all,) = _split_wait("w_in_gather_wait", w_send, w_recv, first_srcs, dec_lands, first_copies, tuple(late_slots))
    own = (jnp.arange(8) == 4 * cx + 2 * cy + cc)[:, None, None]
    dec_all = jnp.where(own, dec_block[None], dec_all)
    (w_in4,) = _gather_forward([w_in4])
    w_in_t = w_in4.reshape(PROJ_W, D_MODEL)
    g_send, g_recv, late_bufs, _, token_gather = _split_start(
        "gather_start", late_slots, [], _gather_ici_copies, 3 * len(late), after=(w_in4,))
    dec_all = dec_all[::2].reshape(N_SHARDS, 2, LOWRANK, KEY_W // N_SHARDS)
    wdf_full = jnp.transpose(dec_all[:, 0], (1, 0, 2)).reshape(LOWRANK, KEY_W)
    wdb_full = jnp.transpose(dec_all[:, 1], (1, 0, 2)).reshape(LOWRANK, KEY_W)
    wd_pad_f = jnp.zeros((LANES, KEY_W), F32).at[0:LOWRANK].set(wdf_full).astype(BF16)
    wd_pad_b = jnp.zeros((LANES, KEY_W), F32).at[LOWRANK : 2 * LOWRANK].set(wdb_full).astype(BF16)

    ws_bf = w_spatial[0].astype(BF16)
    wst_bf = jnp.transpose(w_spatial[0], (0, 2, 1)).astype(BF16)
    bs_col = b_spatial[0].reshape(GMLP_GROUPS, GMLP_CHUNK, 1)

    p = _inproj(xs, norm1_g, w_in_t, token_gather)
    o_f, st_f = _gla_fwd(p, wd_pad_f, b_decay_f, token_gather, reverse=False)
    o_b, st_b = _gla_fwd(p, wd_pad_b, b_decay_b, token_gather, reverse=True)
    late_bufs, _ = _split_wait("gather_wait", g_send, g_recv, late_bufs, [], _gather_ici_copies, (o_f, o_b))
    f_send, f_recv, late_bufs, _, token_forward = _split_start("forward_start", late_bufs, [], _gather_d2d_copies, 3 * len(late))
    (w_out4,), _ = _split_wait("w_out_forward_wait", f_send, f_recv, late_bufs[:1], [], _gather_d2d_copies, (token_forward,))
    w_out_full = w_out4.reshape(-1, D_MODEL)
    x1, ycat = _mixer_out(xs, o_f, o_b, p, gla_norm_g, gmlp_ln_g, gmlp_ln_b, ws_bf, bs_col, w_out_full, token_forward)
    ffn_bufs, _ = _split_wait(
        "forward_wait", f_send, f_recv, late_bufs[1:], [], functools.partial(_gather_d2d_copies, first=3), (x1,))
    wg_t, wu_t, wd = [b.reshape(-1, D_MODEL) for b in ffn_bufs]
    gf = final_norm_g.reshape(1, D_MODEL)
    h2, gate, up, act, dx2, loss_acc, dgf = _ffn_fwd(x1, target, norm2_g, gf, wg_t, wu_t, wd)

    dgate, dup, dx1, dg2 = _ffn_bwd(dx2, gate, up, x1, norm2_g, wg_t, wu_t, wd)
    ffn_grads4 = [g.reshape(N_SHARDS, FF_SHARD, D_MODEL) for g in _ffn_wgrad(h2, dgate, dup, act, dx2)]
    e_send, e_recv, e_srcs, e_lands, token_exchange = _split_start(
        "exchange_start", ffn_grads4, _exchange_lands(ffn_grads4), _exchange_copies, len(ffn_grads4))
    do, dg, du, dvv, dwo, dgn, dlng, dlnb, dws, dbs = _mixer_bwd(
        dx1, ycat, o_f, o_b, p, gla_norm_g, gmlp_ln_g, gmlp_ln_b, ws_bf, wst_bf, bs_col, w_out_full, token_exchange)
    ffn_mine, ffn_other = _split_wait("exchange_wait", e_send, e_recv, e_srcs, e_lands, _exchange_copies, (do,))
    ffn_parts = _add_halves(ffn_mine, ffn_other, sc_arr)
    ffn_payload = [pb for _, pb in ffn_parts]
    s_send, s_recv, s_parts, s_lands, token_scatter = _split_start(
        "scatter_start", ffn_payload, _scatter_lands(ffn_payload), _scatter_copies, 3 * len(ffn_payload))
    dq_f, dk_f, dv_f, dlr_f, dwdec_f, dbdec_f = _gla_bwd(p, do, st_f, wd_pad_f, b_decay_f, token_scatter, reverse=False)
    dq, dk, dv, dlr, dwdec_b, dbdec_b = _gla_bwd(
        p, do, st_b, wd_pad_b, b_decay_b, token_scatter, reverse=True, other=(dq_f, dk_f, dv_f, dlr_f))
    dwin_t, dp = _inproj_wgrad(xs, norm1_g, dq, dk, dv, dg, du, dvv, dlr)
    _, ffn_recv = _split_wait("scatter_wait", s_send, s_recv, s_parts, s_lands, _scatter_copies, (dwin_t,))

    dwin4 = dwin_t.reshape(N_SHARDS, PROJ_W // N_SHARDS, D_MODEL)
    dwo4 = dwo.reshape(N_SHARDS, D_MODEL // N_SHARDS, D_MODEL)
    proj_grads4 = [dwin4, dwo4]
    x_send, x_recv, x_srcs, x_lands, token_swap = _split_start(
        "proj_exchange_start", proj_grads4, _exchange_lands(proj_grads4), _exchange_copies, len(proj_grads4))
    ffn_bufs = [_add_partials(pf, r, sc_arr, token_swap) for (pf, _), r in zip(ffn_parts, ffn_recv)]
    proj_mine, proj_other = _split_wait("proj_exchange_wait", x_send, x_recv, x_srcs, x_lands, _exchange_copies, tuple(ffn_bufs))
    proj_parts = [_add_halves([g], [r], sc_arr)[0] for g, r in zip(proj_mine, proj_other)]
    proj_payload = [pb for _, pb in proj_parts]
    n_proj = len(proj_payload)
    ffn_join_copies = functools.partial(_join_copies, first=3 * n_proj)
    scatter_and_join = lambda srcs, lands, send_sems, recv_sems: (
        _scatter_copies(srcs[:n_proj], lands, send_sems, recv_sems) + ffn_join_copies(srcs[n_proj:], (), send_sems, recv_sems))
    p_send, p_recv, started, p_lands, token_join = _split_start(
        "proj_scatter_start", proj_payload + ffn_bufs, _scatter_lands(proj_payload), scatter_and_join, 3 * n_proj + len(ffn_bufs))
    p_parts, ffn_bufs = started[:n_proj], started[n_proj:]
    dx, dg1 = _inproj_dx(xs, dx1, norm1_g, w_in_t, dp, token_join)
    _, proj_recv = _split_wait("proj_scatter_wait", p_send, p_recv, p_parts, p_lands, _scatter_copies, (dx,))

    dwdec_f16 = dwdec_f[0:LOWRANK]
    dwdec_b16 = dwdec_b[LOWRANK : 2 * LOWRANK]
    shard_major = lambda a: jnp.transpose(a.reshape(LOWRANK, N_SHARDS, KEY_W // N_SHARDS), (1, 0, 2))
    small_grads = {
        "norm1_g": dg1, "b_decay_f": dbdec_f, "b_decay_b": dbdec_b, "gla_norm_g": dgn, "gmlp_ln_g": dlng, "gmlp_ln_b": dlnb,
        "w_spatial": dws, "b_spatial": dbs, "norm2_g": dg2, "final_norm_g": dgf,
    }
    g_pack = _pack_small([small_grads[k] for k in SMALL_NAMES] + [loss_acc], [shard_major(dwdec_f16), shard_major(dwdec_b16)])
    proj_bufs = [_add_partials(pf, r, sc_arr, token_join) for (pf, _), r in zip(proj_parts, proj_recv)]
    proj_join_copies = functools.partial(_join_copies, first=7)
    tail_copies = lambda srcs, lands, send_sems, recv_sems: (
        _small_gather_copies(srcs[:1], lands, send_sems, recv_sems) + proj_join_copies(srcs[1:], (), send_sems, recv_sems))
    t_send, t_recv, (g_pack, *proj_bufs), g_lands, token_tail = _split_start(
        "tail_start", [g_pack] + proj_bufs, [jax.ShapeDtypeStruct((8, SMALL_TOTAL, LANES), F32)], tail_copies, 7 + len(proj_bufs))

    ffn_bufs, _ = _split_wait("join_wait", p_send, p_recv, ffn_bufs, [], ffn_join_copies, (dx, token_tail))
    adamw = lambda k, g: _adamw(big_shards[k], g, rows_of("m_", k), rows_of("v_", k))
    big_updates = {k: adamw(k, g) for k, g in zip(big_names[2:], ffn_bufs)}
    proj_bufs, _ = _split_wait(
        "proj_join_wait", t_send, t_recv, proj_bufs, [], proj_join_copies, tuple(u[1] for u in big_updates.values()))
    big_updates.update({k: adamw(k, g) for k, g in zip(big_names[:2], proj_bufs)})

    (g_pack,), (g_all,) = _split_wait(
        "small_gather_wait", t_send, t_recv, [g_pack], g_lands, _small_gather_copies, tuple(u[1] for u in big_updates.values()))
    pack_own = lambda pre: _pack_small([args[pre + k] for k in SMALL_NAMES], [args[pre + "w_decay_f"], args[pre + "w_decay_b"]])
    small_updates, loss_row = _adamw_small(g_all, g_pack, pack_own(""), pack_own("m_"), pack_own("v_"), [args[k] for k in SMALL_NAMES])

    names = ["norm1_g", "w_in", "w_decay_f", "b_decay_f", "w_decay_b", "b_decay_b", "gla_norm_g", "gmlp_ln_g", "gmlp_ln_b",
             "w_spatial", "b_spatial", "w_out", "norm2_g", "w_gate", "w_up", "w_down", "final_norm_g"]
    results = {"g": {}, "d": {}, "m": {}, "v": {}}
    for tag, arrays in zip("gdmv", small_updates):
        for k, a in zip(SMALL_NAMES + ["w_decay_f", "w_decay_b"], arrays):
            results[tag][k] = a.reshape(args[k].shape)
    for k in big_names:
        for tag, a in zip("gdmv", big_updates[k]):
            results[tag][k] = (jnp.transpose(a) if k in transposed else a).reshape(args[k].shape)

    loss = loss_row[0, 0]
    grad_x = dx.reshape(x.shape)
    return (loss, grad_x, *[results["g"][k] for k in names], *[results["d"][k] for k in names],
            *[results["m"][k] for k in names], *[results["v"][k] for k in names])
```
